```python
import math
import jax, jax.numpy as jnp
from jax import lax
import numpy as np

D_MODEL = 1024
BATCH = 8
SEQ = 4096
DEPTH = 4

N_MIXERS = 3
N_A_LAYERS = (DEPTH + 2) // 3
N_B_LAYERS = (DEPTH + 1) // 3
N_C_LAYERS = DEPTH // 3
EPS = 1e-6

SB_HEAD_DIM = 64
SB_HEADS = D_MODEL // SB_HEAD_DIM
Q_BLOCK = 128

GM_CHUNK = 128
GM_HALF = 2 * D_MODEL
GM_GROUPS = 16
GM_GROUP_DIM = GM_HALF // GM_GROUPS

SSM_INNER = 2 * D_MODEL
SSM_HEAD_DIM = 64
SSM_HEADS = SSM_INNER // SSM_HEAD_DIM
SSM_GROUPS = 8
SSM_HPG = SSM_HEADS // SSM_GROUPS
SSM_STATE = 128
SSM_CONV = 4
SSM_CHUNK = 128
SSM_CONV_DIM = SSM_INNER + 2 * SSM_GROUPS * SSM_STATE
SSM_PROJ = SSM_INNER + SSM_CONV_DIM + SSM_HEADS

FFN_HIDDEN = -(-8 * D_MODEL // (3 * 256)) * 256

kernel_name = "hybrid_sb_gmlp_ssd_trunk"


def rmsnorm(x, g):
    xf = x.astype(jnp.float32)
    y = xf * lax.rsqrt(jnp.mean(xf * xf, axis=-1, keepdims=True) + EPS)
    return (y * g.astype(jnp.float32)).astype(x.dtype)


def stick_breaking_attention(h, w_qkv, q_gain, k_gain, w_o):
    B_, S, _ = h.shape
    qkv = (h @ w_qkv).reshape(B_, S, 3, SB_HEADS, SB_HEAD_DIM)
    q = rmsnorm(qkv[:, :, 0], q_gain).transpose(0, 2, 1, 3)
    k = rmsnorm(qkv[:, :, 1], k_gain).transpose(0, 2, 1, 3)
    v = qkv[:, :, 2].transpose(0, 2, 1, 3)
    scale = 1.0 / math.sqrt(SB_HEAD_DIM)
    nb = S // Q_BLOCK
    q_blocks = q.reshape(B_, SB_HEADS, nb, Q_BLOCK, SB_HEAD_DIM).transpose(2, 0, 1, 3, 4)
    key_pos = jnp.arange(S)

    def one_block(args):
        qb, start = args
        t = start + jnp.arange(Q_BLOCK)
        z = jnp.einsum('bhtd,bhsd->bhts', qb, k).astype(jnp.float32) * scale
        mask = key_pos[None, :] < t[:, None]
        log_beta = jax.nn.log_sigmoid(z)
        log_1m = jnp.where(mask, jax.nn.log_sigmoid(-z), 0.0)
        suffix = lax.cumsum(log_1m, axis=3, reverse=True) - log_1m
        w = jnp.where(mask, jnp.exp(log_beta + suffix), 0.0)
        return jnp.einsum('bhts,bhsd->bhtd', w.astype(v.dtype), v)

    out = lax.map(one_block, (q_blocks, jnp.arange(nb) * Q_BLOCK))
    out = out.transpose(1, 0, 3, 2, 4).reshape(B_, S, SB_HEADS * SB_HEAD_DIM)
    return out @ w_o


def chunked_gmlp(h, w_in, b_in, v_gain, w_s, b_s, w_out):
    B_, S, _ = h.shape
    zz = jax.nn.gelu(h @ w_in + b_in, approximate=False)
    u, v = zz[..., :GM_HALF], zz[..., GM_HALF:]
    v = rmsnorm(v, v_gain)
    nc = S // GM_CHUNK
    v = v.reshape(B_, nc, GM_CHUNK, GM_GROUPS, GM_GROUP_DIM)
    causal = jnp.tril(jnp.ones((GM_CHUNK, GM_CHUNK), dtype=bool))
    w_s_c = jnp.where(causal, w_s, 0.0).astype(v.dtype)
    mixed = jnp.einsum('gts,bnsgc->bntgc', w_s_c, v) + b_s.T[None, None, :, :, None]
    return (u * mixed.reshape(B_, S, GM_HALF)) @ w_out


def ssd_chunked(x, dt, A, Bm, Cm):
    B_, S, _, _ = x.shape
    nc = S // SSM_CHUNK
    L = SSM_CHUNK
    f32 = jnp.float32
    xdt = (x.astype(f32) * dt[..., None]).reshape(B_, nc, L, SSM_GROUPS, SSM_HPG, SSM_HEAD_DIM)
    a = (dt * A).reshape(B_, nc, L, SSM_GROUPS, SSM_HPG)
    Bc = Bm.astype(f32).reshape(B_, nc, L, SSM_GROUPS, SSM_STATE)
    Cc = Cm.astype(f32).reshape(B_, nc, L, SSM_GROUPS, SSM_STATE)
    a_cum = jnp.cumsum(a, axis=2)

    a_t = a_cum.transpose(0, 1, 3, 4, 2)
    seg = a_t[..., :, None] - a_t[..., None, :]
    causal = jnp.tril(jnp.ones((L, L), dtype=bool))
    decay = jnp.exp(jnp.where(causal, seg, -jnp.inf))
    cb = jnp.einsum('bclgn,bcsgn->bcgls', Cc, Bc)
    y_diag = jnp.einsum('bcgrls,bcsgrp->bclgrp', cb[:, :, :, None] * decay, xdt)

    decay_to_end = jnp.exp(a_cum[:, :, -1:] - a_cum)
    states = jnp.einsum('bclgn,bclgrp->bcgrpn', Bc, xdt * decay_to_end[..., None])
    chunk_decay = jnp.exp(a_cum[:, :, -1])

    def step(hstate, inp):
        st, dec = inp
        return hstate * dec[..., None, None] + st, hstate

    init = jnp.zeros((B_, SSM_GROUPS, SSM_HPG, SSM_HEAD_DIM, SSM_STATE), f32)
    _, prev = lax.scan(step, init, (states.transpose(1, 0, 2, 3, 4, 5),
                                    chunk_decay.transpose(1, 0, 2, 3)))
    prev = prev.transpose(1, 0, 2, 3, 4, 5)

    y_off = jnp.einsum('bclgn,bcgrpn->bclgrp', Cc, prev) * jnp.exp(a_cum)[..., None]
    y = (y_diag + y_off).reshape(B_, S, SSM_HEADS, SSM_HEAD_DIM)
    return y.astype(x.dtype)


def mamba2_mixer(h, w_in, conv_w, conv_b, dt_bias, a_log, d_skip, norm_gain, w_out):
    B_, S, _ = h.shape
    zxbcdt = h @ w_in
    z = zxbcdt[..., :SSM_INNER]
    xbc = zxbcdt[..., SSM_INNER:SSM_INNER + SSM_CONV_DIM]
    dt = zxbcdt[..., SSM_INNER + SSM_CONV_DIM:]
    xbc = lax.conv_general_dilated(
        xbc, conv_w[:, None, :].astype(xbc.dtype), window_strides=(1,),
        padding=[(SSM_CONV - 1, 0)], dimension_numbers=('NWC', 'WIO', 'NWC'),
        feature_group_count=SSM_CONV_DIM) + conv_b
    xbc = jax.nn.silu(xbc)
    xs = xbc[..., :SSM_INNER].reshape(B_, S, SSM_HEADS, SSM_HEAD_DIM)
    Bm = xbc[..., SSM_INNER:SSM_INNER + SSM_GROUPS * SSM_STATE].reshape(B_, S, SSM_GROUPS, SSM_STATE)
    Cm = xbc[..., SSM_INNER + SSM_GROUPS * SSM_STATE:].reshape(B_, S, SSM_GROUPS, SSM_STATE)
    dt = jax.nn.softplus(dt.astype(jnp.float32) + dt_bias.astype(jnp.float32))
    A = -jnp.exp(a_log.astype(jnp.float32))
    y = ssd_chunked(xs, dt, A, Bm, Cm)
    y = (y + xs * d_skip[:, None]).reshape(B_, S, SSM_INNER)
    yg = (y * jax.nn.silu(z)).astype(jnp.float32).reshape(B_, S, SSM_GROUPS, SSM_INNER // SSM_GROUPS)
    yg = yg * lax.rsqrt(jnp.mean(yg * yg, axis=-1, keepdims=True) + EPS)
    y = (yg.reshape(B_, S, SSM_INNER) * norm_gain.astype(jnp.float32)).astype(h.dtype)
    return y @ w_out


def swiglu(h, w_gu, w_down):
    gu = h @ w_gu
    return (jax.nn.silu(gu[..., :FFN_HIDDEN]) * gu[..., FFN_HIDDEN:]) @ w_down


def _fwd_setup_inputs(seed: int = 0) -> dict:
    key = jax.random.key(seed)
    ks = iter(jax.random.split(key, 40))

    def nrm(shape, scale):
        return jax.random.normal(next(ks), shape, jnp.float32) * scale

    def gain(shape):
        return 1.0 + nrm(shape, 0.02)

    D = D_MODEL
    x = nrm((BATCH, SEQ, D), 1.0)
    mix_norm = gain((DEPTH, D))
    ffn_norm = gain((DEPTH, D))
    sb_w_qkv = nrm((N_A_LAYERS, D, 3 * SB_HEADS * SB_HEAD_DIM), D ** -0.5)
    sb_q_gain = gain((N_A_LAYERS, SB_HEAD_DIM))
    sb_k_gain = gain((N_A_LAYERS, SB_HEAD_DIM))
    sb_w_o = nrm((N_A_LAYERS, SB_HEADS * SB_HEAD_DIM, D), (SB_HEADS * SB_HEAD_DIM) ** -0.5)
    gm_w_in = nrm((N_B_LAYERS, D, 2 * GM_HALF), D ** -0.5)
    gm_b_in = nrm((N_B_LAYERS, 2 * GM_HALF), 0.01)
    gm_v_gain = gain((N_B_LAYERS, GM_HALF))
    gm_w_s = nrm((N_B_LAYERS, GM_GROUPS, GM_CHUNK, GM_CHUNK), GM_CHUNK ** -0.5)
    gm_b_s = gain((N_B_LAYERS, GM_GROUPS, GM_CHUNK))
    gm_w_out = nrm((N_B_LAYERS, GM_HALF, D), GM_HALF ** -0.5)
    ssm_w_in = nrm((N_C_LAYERS, D, SSM_PROJ), D ** -0.5)
    ssm_conv_w = nrm((N_C_LAYERS, SSM_CONV, SSM_CONV_DIM), SSM_CONV ** -0.5)
    ssm_conv_b = nrm((N_C_LAYERS, SSM_CONV_DIM), 0.01)
    dt0 = jnp.exp(jax.random.uniform(next(ks), (N_C_LAYERS, SSM_HEADS), jnp.float32,
                                     math.log(1e-3), math.log(1e-1)))
    ssm_dt_bias = dt0 + jnp.log(-jnp.expm1(-dt0))
    ssm_a_log = jnp.log(jax.random.uniform(next(ks), (N_C_LAYERS, SSM_HEADS), jnp.float32, 1.0, 16.0))
    ssm_d = gain((N_C_LAYERS, SSM_HEADS))
    ssm_norm_gain = gain((N_C_LAYERS, SSM_INNER))
    ssm_w_out = nrm((N_C_LAYERS, SSM_INNER, D), SSM_INNER ** -0.5)
    ffn_w_gu = nrm((DEPTH, D, 2 * FFN_HIDDEN), D ** -0.5)
    ffn_w_down = nrm((DEPTH, FFN_HIDDEN, D), FFN_HIDDEN ** -0.5)
    return {
        "x": x, "mix_norm": mix_norm, "ffn_norm": ffn_norm,
        "sb_w_qkv": sb_w_qkv, "sb_q_gain": sb_q_gain, "sb_k_gain": sb_k_gain, "sb_w_o": sb_w_o,
        "gm_w_in": gm_w_in, "gm_b_in": gm_b_in, "gm_v_gain": gm_v_gain, "gm_w_s": gm_w_s,
        "gm_b_s": gm_b_s, "gm_w_out": gm_w_out,
        "ssm_w_in": ssm_w_in, "ssm_conv_w": ssm_conv_w, "ssm_conv_b": ssm_conv_b,
        "ssm_dt_bias": ssm_dt_bias, "ssm_a_log": ssm_a_log, "ssm_d": ssm_d,
        "ssm_norm_gain": ssm_norm_gain, "ssm_w_out": ssm_w_out,
        "ffn_w_gu": ffn_w_gu, "ffn_w_down": ffn_w_down,
    }


def _fwd_reference(x, mix_norm, ffn_norm,
              sb_w_qkv, sb_q_gain, sb_k_gain, sb_w_o,
              gm_w_in, gm_b_in, gm_v_gain, gm_w_s, gm_b_s, gm_w_out,
              ssm_w_in, ssm_conv_w, ssm_conv_b, ssm_dt_bias, ssm_a_log, ssm_d,
              ssm_norm_gain, ssm_w_out,
              ffn_w_gu, ffn_w_down):
    for i in range(DEPTH):
        h = rmsnorm(x, mix_norm[i])
        kind = i % N_MIXERS
        j = i // N_MIXERS
        if kind == 0:
            m = stick_breaking_attention(h, sb_w_qkv[j], sb_q_gain[j], sb_k_gain[j], sb_w_o[j])
        elif kind == 1:
            m = chunked_gmlp(h, gm_w_in[j], gm_b_in[j], gm_v_gain[j], gm_w_s[j], gm_b_s[j], gm_w_out[j])
        else:
            m = mamba2_mixer(h, ssm_w_in[j], ssm_conv_w[j], ssm_conv_b[j], ssm_dt_bias[j],
                             ssm_a_log[j], ssm_d[j], ssm_norm_gain[j], ssm_w_out[j])
        x = x + m
        x = x + swiglu(rmsnorm(x, ffn_norm[i]), ffn_w_gu[i], ffn_w_down[i])
    return x


import jax as _jax
import jax.numpy as _jnp

TWIN_FORMAT = 'train_step'
FWD_PARAMS = ['x', 'mix_norm', 'ffn_norm', 'sb_w_qkv', 'sb_q_gain', 'sb_k_gain', 'sb_w_o', 'gm_w_in', 'gm_b_in', 'gm_v_gain', 'gm_w_s', 'gm_b_s', 'gm_w_out', 'ssm_w_in', 'ssm_conv_w', 'ssm_conv_b', 'ssm_dt_bias', 'ssm_a_log', 'ssm_d', 'ssm_norm_gain', 'ssm_w_out', 'ffn_w_gu', 'ffn_w_down']
TWIN_WEIGHTS = ['mix_norm', 'ffn_norm', 'sb_w_qkv', 'sb_q_gain', 'sb_k_gain', 'sb_w_o', 'gm_w_in', 'gm_b_in', 'gm_v_gain', 'gm_w_s', 'gm_b_s', 'gm_w_out', 'ssm_w_in', 'ssm_conv_w', 'ssm_conv_b', 'ssm_dt_bias', 'ssm_a_log', 'ssm_d', 'ssm_norm_gain', 'ssm_w_out', 'ffn_w_gu', 'ffn_w_down']
TWIN_DIFF_INPUT = 'x'
TWIN_INPUTS = ['x', 'mix_norm', 'ffn_norm', 'sb_w_qkv', 'sb_q_gain', 'sb_k_gain', 'sb_w_o', 'gm_w_in', 'gm_b_in', 'gm_v_gain', 'gm_w_s', 'gm_b_s', 'gm_w_out', 'ssm_w_in', 'ssm_conv_w', 'ssm_conv_b', 'ssm_dt_bias', 'ssm_a_log', 'ssm_d', 'ssm_norm_gain', 'ssm_w_out', 'ffn_w_gu', 'ffn_w_down', 'loss_target', 'm_mix_norm', 'm_ffn_norm', 'm_sb_w_qkv', 'm_sb_q_gain', 'm_sb_k_gain', 'm_sb_w_o', 'm_gm_w_in', 'm_gm_b_in', 'm_gm_v_gain', 'm_gm_w_s', 'm_gm_b_s', 'm_gm_w_out', 'm_ssm_w_in', 'm_ssm_conv_w', 'm_ssm_conv_b', 'm_ssm_dt_bias', 'm_ssm_a_log', 'm_ssm_d', 'm_ssm_norm_gain', 'm_ssm_w_out', 'm_ffn_w_gu', 'm_ffn_w_down', 'v_mix_norm', 'v_ffn_norm', 'v_sb_w_qkv', 'v_sb_q_gain', 'v_sb_k_gain', 'v_sb_w_o', 'v_gm_w_in', 'v_gm_b_in', 'v_gm_v_gain', 'v_gm_w_s', 'v_gm_b_s', 'v_gm_w_out', 'v_ssm_w_in', 'v_ssm_conv_w', 'v_ssm_conv_b', 'v_ssm_dt_bias', 'v_ssm_a_log', 'v_ssm_d', 'v_ssm_norm_gain', 'v_ssm_w_out', 'v_ffn_w_gu', 'v_ffn_w_down']
TWIN_OUTPUTS = ['loss', 'grad_x', 'grad_mix_norm', 'grad_ffn_norm', 'grad_sb_w_qkv', 'grad_sb_q_gain', 'grad_sb_k_gain', 'grad_sb_w_o', 'grad_gm_w_in', 'grad_gm_b_in', 'grad_gm_v_gain', 'grad_gm_w_s', 'grad_gm_b_s', 'grad_gm_w_out', 'grad_ssm_w_in', 'grad_ssm_conv_w', 'grad_ssm_conv_b', 'grad_ssm_dt_bias', 'grad_ssm_a_log', 'grad_ssm_d', 'grad_ssm_norm_gain', 'grad_ssm_w_out', 'grad_ffn_w_gu', 'grad_ffn_w_down', 'delta_mix_norm', 'delta_ffn_norm', 'delta_sb_w_qkv', 'delta_sb_q_gain', 'delta_sb_k_gain', 'delta_sb_w_o', 'delta_gm_w_in', 'delta_gm_b_in', 'delta_gm_v_gain', 'delta_gm_w_s', 'delta_gm_b_s', 'delta_gm_w_out', 'delta_ssm_w_in', 'delta_ssm_conv_w', 'delta_ssm_conv_b', 'delta_ssm_dt_bias', 'delta_ssm_a_log', 'delta_ssm_d', 'delta_ssm_norm_gain', 'delta_ssm_w_out', 'delta_ffn_w_gu', 'delta_ffn_w_down', 'new_m_mix_norm', 'new_m_ffn_norm', 'new_m_sb_w_qkv', 'new_m_sb_q_gain', 'new_m_sb_k_gain', 'new_m_sb_w_o', 'new_m_gm_w_in', 'new_m_gm_b_in', 'new_m_gm_v_gain', 'new_m_gm_w_s', 'new_m_gm_b_s', 'new_m_gm_w_out', 'new_m_ssm_w_in', 'new_m_ssm_conv_w', 'new_m_ssm_conv_b', 'new_m_ssm_dt_bias', 'new_m_ssm_a_log', 'new_m_ssm_d', 'new_m_ssm_norm_gain', 'new_m_ssm_w_out', 'new_m_ffn_w_gu', 'new_m_ffn_w_down', 'new_v_mix_norm', 'new_v_ffn_norm', 'new_v_sb_w_qkv', 'new_v_sb_q_gain', 'new_v_sb_k_gain', 'new_v_sb_w_o', 'new_v_gm_w_in', 'new_v_gm_b_in', 'new_v_gm_v_gain', 'new_v_gm_w_s', 'new_v_gm_b_s', 'new_v_gm_w_out', 'new_v_ssm_w_in', 'new_v_ssm_conv_w', 'new_v_ssm_conv_b', 'new_v_ssm_dt_bias', 'new_v_ssm_a_log', 'new_v_ssm_d', 'new_v_ssm_norm_gain', 'new_v_ssm_w_out', 'new_v_ffn_w_gu', 'new_v_ffn_w_down']
TWIN_LEAF_KINDS = {'loss': 'loss', 'grad_x': 'grad_x', 'grad_mix_norm': 'grad_w', 'grad_ffn_norm': 'grad_w', 'grad_sb_w_qkv': 'grad_w', 'grad_sb_q_gain': 'grad_w', 'grad_sb_k_gain': 'grad_w', 'grad_sb_w_o': 'grad_w', 'grad_gm_w_in': 'grad_w', 'grad_gm_b_in': 'grad_w', 'grad_gm_v_gain': 'grad_w', 'grad_gm_w_s': 'grad_w', 'grad_gm_b_s': 'grad_w', 'grad_gm_w_out': 'grad_w', 'grad_ssm_w_in': 'grad_w', 'grad_ssm_conv_w': 'grad_w', 'grad_ssm_conv_b': 'grad_w', 'grad_ssm_dt_bias': 'grad_w', 'grad_ssm_a_log': 'grad_w', 'grad_ssm_d': 'grad_w', 'grad_ssm_norm_gain': 'grad_w', 'grad_ssm_w_out': 'grad_w', 'grad_ffn_w_gu': 'grad_w', 'grad_ffn_w_down': 'grad_w', 'delta_mix_norm': 'delta_w', 'delta_ffn_norm': 'delta_w', 'delta_sb_w_qkv': 'delta_w', 'delta_sb_q_gain': 'delta_w', 'delta_sb_k_gain': 'delta_w', 'delta_sb_w_o': 'delta_w', 'delta_gm_w_in': 'delta_w', 'delta_gm_b_in': 'delta_w', 'delta_gm_v_gain': 'delta_w', 'delta_gm_w_s': 'delta_w', 'delta_gm_b_s': 'delta_w', 'delta_gm_w_out': 'delta_w', 'delta_ssm_w_in': 'delta_w', 'delta_ssm_conv_w': 'delta_w', 'delta_ssm_conv_b': 'delta_w', 'delta_ssm_dt_bias': 'delta_w', 'delta_ssm_a_log': 'delta_w', 'delta_ssm_d': 'delta_w', 'delta_ssm_norm_gain': 'delta_w', 'delta_ssm_w_out': 'delta_w', 'delta_ffn_w_gu': 'delta_w', 'delta_ffn_w_down': 'delta_w', 'new_m_mix_norm': 'new_m', 'new_m_ffn_norm': 'new_m', 'new_m_sb_w_qkv': 'new_m', 'new_m_sb_q_gain': 'new_m', 'new_m_sb_k_gain': 'new_m', 'new_m_sb_w_o': 'new_m', 'new_m_gm_w_in': 'new_m', 'new_m_gm_b_in': 'new_m', 'new_m_gm_v_gain': 'new_m', 'new_m_gm_w_s': 'new_m', 'new_m_gm_b_s': 'new_m', 'new_m_gm_w_out': 'new_m', 'new_m_ssm_w_in': 'new_m', 'new_m_ssm_conv_w': 'new_m', 'new_m_ssm_conv_b': 'new_m', 'new_m_ssm_dt_bias': 'new_m', 'new_m_ssm_a_log': 'new_m', 'new_m_ssm_d': 'new_m', 'new_m_ssm_norm_gain': 'new_m', 'new_m_ssm_w_out': 'new_m', 'new_m_ffn_w_gu': 'new_m', 'new_m_ffn_w_down': 'new_m', 'new_v_mix_norm': 'new_v', 'new_v_ffn_norm': 'new_v', 'new_v_sb_w_qkv': 'new_v', 'new_v_sb_q_gain': 'new_v', 'new_v_sb_k_gain': 'new_v', 'new_v_sb_w_o': 'new_v', 'new_v_gm_w_in': 'new_v', 'new_v_gm_b_in': 'new_v', 'new_v_gm_v_gain': 'new_v', 'new_v_gm_w_s': 'new_v', 'new_v_gm_b_s': 'new_v', 'new_v_gm_w_out': 'new_v', 'new_v_ssm_w_in': 'new_v', 'new_v_ssm_conv_w': 'new_v', 'new_v_ssm_conv_b': 'new_v', 'new_v_ssm_dt_bias': 'new_v', 'new_v_ssm_a_log': 'new_v', 'new_v_ssm_d': 'new_v', 'new_v_ssm_norm_gain': 'new_v', 'new_v_ssm_w_out': 'new_v', 'new_v_ffn_w_gu': 'new_v', 'new_v_ffn_w_down': 'new_v'}


def _forward(args):
    return _fwd_reference(*[args[k] for k in FWD_PARAMS])


def _output_shape():
    out = _jax.eval_shape(lambda: _forward(_fwd_setup_inputs(0)))
    return out.shape, out.dtype

N_MICROBATCH = 1
ADAM_LR = 0.001
ADAM_B1 = 0.9
ADAM_B2 = 0.999
ADAM_EPS = 1e-08
ADAM_WD = 0.01
ADAM_STEP = 10
PER_EXAMPLE_BATCH_AXIS = {'x': 0, 'loss_target': 0}
SHARED_INPUTS = []
_WEIGHT_DTYPES = {'mix_norm': _jnp.float32, 'ffn_norm': _jnp.float32, 'sb_w_qkv': _jnp.float32, 'sb_q_gain': _jnp.float32, 'sb_k_gain': _jnp.float32, 'sb_w_o': _jnp.float32, 'gm_w_in': _jnp.float32, 'gm_b_in': _jnp.float32, 'gm_v_gain': _jnp.float32, 'gm_w_s': _jnp.float32, 'gm_b_s': _jnp.float32, 'gm_w_out': _jnp.float32, 'ssm_w_in': _jnp.float32, 'ssm_conv_w': _jnp.float32, 'ssm_conv_b': _jnp.float32, 'ssm_dt_bias': _jnp.float32, 'ssm_a_log': _jnp.float32, 'ssm_d': _jnp.float32, 'ssm_norm_gain': _jnp.float32, 'ssm_w_out': _jnp.float32, 'ffn_w_gu': _jnp.float32, 'ffn_w_down': _jnp.float32}
MOMENT_SCALE = {'mix_norm': 1.514705e+01, 'ffn_norm': 2.440780e+01, 'sb_w_qkv': 8.391113e-01, 'sb_q_gain': 2.914360e+01, 'sb_k_gain': 2.909717e+01, 'sb_w_o': 1.294676e+00, 'gm_w_in': 5.337572e-01, 'gm_b_in': 7.311729e+00, 'gm_v_gain': 3.183771e+00, 'gm_w_s': 1.954393e+00, 'gm_b_s': 6.945001e+00, 'gm_w_out': 4.273290e+00, 'ssm_w_in': 6.355316e-01, 'ssm_conv_w': 1.039154e+00, 'ssm_conv_b': 3.454516e+00, 'ssm_dt_bias': 7.020059e-01, 'ssm_a_log': 6.766963e+00, 'ssm_d': 7.121516e+00, 'ssm_norm_gain': 2.275592e+01, 'ssm_w_out': 2.998101e+00, 'ffn_w_gu': 4.344571e-01, 'ffn_w_down': 6.566771e-01}


def _to_microbatches(a, axis):
    t = _jnp.moveaxis(a, axis, 0)
    t = t.reshape((N_MICROBATCH, t.shape[0] // N_MICROBATCH) + t.shape[1:])
    return _jnp.moveaxis(t, 1, axis + 1)


def setup_inputs(seed: int = 0) -> dict:
    inp = _fwd_setup_inputs(seed)
    key = _jax.random.fold_in(_jax.random.key(seed), 7919)
    shape, _ = _output_shape()
    out = dict(inp)
    out["loss_target"] = _jax.random.normal(_jax.random.fold_in(key, 0), shape, _jnp.float32)
    for i, name in enumerate(TWIN_WEIGHTS):
        w = inp[name].astype(_jnp.float32)
        if MOMENT_SCALE is None:
            s = _jnp.sqrt(_jnp.mean(_jnp.square(w)) + 1e-30)
        else:
            s = MOMENT_SCALE[name]
        km, kv = _jax.random.split(_jax.random.fold_in(key, i + 1))
        out[name] = w
        out["m_" + name] = s * _jax.random.normal(km, w.shape, _jnp.float32)
        out["v_" + name] = (s * s) * _jax.random.uniform(kv, w.shape, _jnp.float32, 0.5, 1.5)
    if N_MICROBATCH > 1:
        for name, axis in PER_EXAMPLE_BATCH_AXIS.items():
            out[name] = _to_microbatches(out[name], axis)
    return {'x': out['x'], 'mix_norm': out['mix_norm'], 'ffn_norm': out['ffn_norm'], 'sb_w_qkv': out['sb_w_qkv'], 'sb_q_gain': out['sb_q_gain'], 'sb_k_gain': out['sb_k_gain'], 'sb_w_o': out['sb_w_o'], 'gm_w_in': out['gm_w_in'], 'gm_b_in': out['gm_b_in'], 'gm_v_gain': out['gm_v_gain'], 'gm_w_s': out['gm_w_s'], 'gm_b_s': out['gm_b_s'], 'gm_w_out': out['gm_w_out'], 'ssm_w_in': out['ssm_w_in'], 'ssm_conv_w': out['ssm_conv_w'], 'ssm_conv_b': out['ssm_conv_b'], 'ssm_dt_bias': out['ssm_dt_bias'], 'ssm_a_log': out['ssm_a_log'], 'ssm_d': out['ssm_d'], 'ssm_norm_gain': out['ssm_norm_gain'], 'ssm_w_out': out['ssm_w_out'], 'ffn_w_gu': out['ffn_w_gu'], 'ffn_w_down': out['ffn_w_down'], 'loss_target': out['loss_target'], 'm_mix_norm': out['m_mix_norm'], 'm_ffn_norm': out['m_ffn_norm'], 'm_sb_w_qkv': out['m_sb_w_qkv'], 'm_sb_q_gain': out['m_sb_q_gain'], 'm_sb_k_gain': out['m_sb_k_gain'], 'm_sb_w_o': out['m_sb_w_o'], 'm_gm_w_in': out['m_gm_w_in'], 'm_gm_b_in': out['m_gm_b_in'], 'm_gm_v_gain': out['m_gm_v_gain'], 'm_gm_w_s': out['m_gm_w_s'], 'm_gm_b_s': out['m_gm_b_s'], 'm_gm_w_out': out['m_gm_w_out'], 'm_ssm_w_in': out['m_ssm_w_in'], 'm_ssm_conv_w': out['m_ssm_conv_w'], 'm_ssm_conv_b': out['m_ssm_conv_b'], 'm_ssm_dt_bias': out['m_ssm_dt_bias'], 'm_ssm_a_log': out['m_ssm_a_log'], 'm_ssm_d': out['m_ssm_d'], 'm_ssm_norm_gain': out['m_ssm_norm_gain'], 'm_ssm_w_out': out['m_ssm_w_out'], 'm_ffn_w_gu': out['m_ffn_w_gu'], 'm_ffn_w_down': out['m_ffn_w_down'], 'v_mix_norm': out['v_mix_norm'], 'v_ffn_norm': out['v_ffn_norm'], 'v_sb_w_qkv': out['v_sb_w_qkv'], 'v_sb_q_gain': out['v_sb_q_gain'], 'v_sb_k_gain': out['v_sb_k_gain'], 'v_sb_w_o': out['v_sb_w_o'], 'v_gm_w_in': out['v_gm_w_in'], 'v_gm_b_in': out['v_gm_b_in'], 'v_gm_v_gain': out['v_gm_v_gain'], 'v_gm_w_s': out['v_gm_w_s'], 'v_gm_b_s': out['v_gm_b_s'], 'v_gm_w_out': out['v_gm_w_out'], 'v_ssm_w_in': out['v_ssm_w_in'], 'v_ssm_conv_w': out['v_ssm_conv_w'], 'v_ssm_conv_b': out['v_ssm_conv_b'], 'v_ssm_dt_bias': out['v_ssm_dt_bias'], 'v_ssm_a_log': out['v_ssm_a_log'], 'v_ssm_d': out['v_ssm_d'], 'v_ssm_norm_gain': out['v_ssm_norm_gain'], 'v_ssm_w_out': out['v_ssm_w_out'], 'v_ffn_w_gu': out['v_ffn_w_gu'], 'v_ffn_w_down': out['v_ffn_w_down']}


def _loss(weights, diff, rest, loss_target):
    with _jax.named_scope("forward"):
        args = {**rest, TWIN_DIFF_INPUT: diff, **{k: w.astype(_WEIGHT_DTYPES[k]) for k, w in weights.items()}}
        y = _forward(args)
    with _jax.named_scope("loss_head"):
        err = _jnp.square(y.astype(_jnp.float32) - loss_target)
        return 0.5 * _jnp.sum(_jnp.mean(err, axis=-1)) if err.ndim else 0.5 * err


def _adamw(w, g, m, v):
    m = ADAM_B1 * m + (1.0 - ADAM_B1) * g
    v = ADAM_B2 * v + (1.0 - ADAM_B2) * _jnp.square(g)
    m_hat = m / (1.0 - ADAM_B1 ** ADAM_STEP)
    v_hat = v / (1.0 - ADAM_B2 ** ADAM_STEP)
    delta = -ADAM_LR * (m_hat / (_jnp.sqrt(v_hat) + ADAM_EPS) + ADAM_WD * w)
    return delta, m, v


def reference(x, mix_norm, ffn_norm, sb_w_qkv, sb_q_gain, sb_k_gain, sb_w_o, gm_w_in, gm_b_in, gm_v_gain, gm_w_s, gm_b_s, gm_w_out, ssm_w_in, ssm_conv_w, ssm_conv_b, ssm_dt_bias, ssm_a_log, ssm_d, ssm_norm_gain, ssm_w_out, ffn_w_gu, ffn_w_down, loss_target, m_mix_norm, m_ffn_norm, m_sb_w_qkv, m_sb_q_gain, m_sb_k_gain, m_sb_w_o, m_gm_w_in, m_gm_b_in, m_gm_v_gain, m_gm_w_s, m_gm_b_s, m_gm_w_out, m_ssm_w_in, m_ssm_conv_w, m_ssm_conv_b, m_ssm_dt_bias, m_ssm_a_log, m_ssm_d, m_ssm_norm_gain, m_ssm_w_out, m_ffn_w_gu, m_ffn_w_down, v_mix_norm, v_ffn_norm, v_sb_w_qkv, v_sb_q_gain, v_sb_k_gain, v_sb_w_o, v_gm_w_in, v_gm_b_in, v_gm_v_gain, v_gm_w_s, v_gm_b_s, v_gm_w_out, v_ssm_w_in, v_ssm_conv_w, v_ssm_conv_b, v_ssm_dt_bias, v_ssm_a_log, v_ssm_d, v_ssm_norm_gain, v_ssm_w_out, v_ffn_w_gu, v_ffn_w_down):
    given = dict(x=x, mix_norm=mix_norm, ffn_norm=ffn_norm, sb_w_qkv=sb_w_qkv, sb_q_gain=sb_q_gain, sb_k_gain=sb_k_gain, sb_w_o=sb_w_o, gm_w_in=gm_w_in, gm_b_in=gm_b_in, gm_v_gain=gm_v_gain, gm_w_s=gm_w_s, gm_b_s=gm_b_s, gm_w_out=gm_w_out, ssm_w_in=ssm_w_in, ssm_conv_w=ssm_conv_w, ssm_conv_b=ssm_conv_b, ssm_dt_bias=ssm_dt_bias, ssm_a_log=ssm_a_log, ssm_d=ssm_d, ssm_norm_gain=ssm_norm_gain, ssm_w_out=ssm_w_out, ffn_w_gu=ffn_w_gu, ffn_w_down=ffn_w_down, loss_target=loss_target, m_mix_norm=m_mix_norm, m_ffn_norm=m_ffn_norm, m_sb_w_qkv=m_sb_w_qkv, m_sb_q_gain=m_sb_q_gain, m_sb_k_gain=m_sb_k_gain, m_sb_w_o=m_sb_w_o, m_gm_w_in=m_gm_w_in, m_gm_b_in=m_gm_b_in, m_gm_v_gain=m_gm_v_gain, m_gm_w_s=m_gm_w_s, m_gm_b_s=m_gm_b_s, m_gm_w_out=m_gm_w_out, m_ssm_w_in=m_ssm_w_in, m_ssm_conv_w=m_ssm_conv_w, m_ssm_conv_b=m_ssm_conv_b, m_ssm_dt_bias=m_ssm_dt_bias, m_ssm_a_log=m_ssm_a_log, m_ssm_d=m_ssm_d, m_ssm_norm_gain=m_ssm_norm_gain, m_ssm_w_out=m_ssm_w_out, m_ffn_w_gu=m_ffn_w_gu, m_ffn_w_down=m_ffn_w_down, v_mix_norm=v_mix_norm, v_ffn_norm=v_ffn_norm, v_sb_w_qkv=v_sb_w_qkv, v_sb_q_gain=v_sb_q_gain, v_sb_k_gain=v_sb_k_gain, v_sb_w_o=v_sb_w_o, v_gm_w_in=v_gm_w_in, v_gm_b_in=v_gm_b_in, v_gm_v_gain=v_gm_v_gain, v_gm_w_s=v_gm_w_s, v_gm_b_s=v_gm_b_s, v_gm_w_out=v_gm_w_out, v_ssm_w_in=v_ssm_w_in, v_ssm_conv_w=v_ssm_conv_w, v_ssm_conv_b=v_ssm_conv_b, v_ssm_dt_bias=v_ssm_dt_bias, v_ssm_a_log=v_ssm_a_log, v_ssm_d=v_ssm_d, v_ssm_norm_gain=v_ssm_norm_gain, v_ssm_w_out=v_ssm_w_out, v_ffn_w_gu=v_ffn_w_gu, v_ffn_w_down=v_ffn_w_down)
    weights = {n: given[n] for n in TWIN_WEIGHTS}
    shared = {n: given[n] for n in SHARED_INPUTS}
    per_example = {n: given[n] for n in ['x']}
    grad_fn = _jax.value_and_grad(_loss, argnums=(0, 1))

    def one_microbatch(ex, loss_target):
        ex = dict(ex)
        diff = ex.pop(TWIN_DIFF_INPUT)
        return grad_fn(weights, diff, {**shared, **ex}, loss_target)

    if N_MICROBATCH == 1:
        loss, (grad_w, grad_x) = one_microbatch(per_example, given["loss_target"])
    else:
        def body(carry, xs):
            loss_sum, grad_sum = carry
            l_k, (gw_k, gx_k) = one_microbatch(xs[0], xs[1])
            with _jax.named_scope("update"):
                return (loss_sum + l_k, _jax.tree.map(_jnp.add, grad_sum, gw_k)), gx_k

        init = (_jnp.zeros((), _jnp.float32), _jax.tree.map(_jnp.zeros_like, weights))
        (loss, grad_w), grad_x = _jax.lax.scan(body, init, (per_example, given["loss_target"]))
    with _jax.named_scope("update"):
        delta_w, new_m, new_v = {}, {}, {}
        for n in TWIN_WEIGHTS:
            delta_w[n], new_m[n], new_v[n] = _adamw(weights[n], grad_w[n], given["m_" + n], given["v_" + n])
    return (loss, grad_x, *[grad_w[n] for n in TWIN_WEIGHTS], *[delta_w[n] for n in TWIN_WEIGHTS],
            *[new_m[n] for n in TWIN_WEIGHTS], *[new_v[n] for n in TWIN_WEIGHTS])
```

```python
import math

import jax
import jax.numpy as jnp
from jax import lax
from jax.experimental import pallas as pl
from jax.experimental.pallas import tpu as pltpu

F32 = jnp.float32
BF16 = jnp.bfloat16
EPS = 1e-6
N_DEV = 8
SB_HEAD_DIM = 64
SSM_STATE = 128
SSM_CONV = 4
ADAM_LR = 0.001
ADAM_B1 = 0.9
ADAM_B2 = 0.999
ADAM_EPS = 1e-08
ADAM_WD = 0.01
ADAM_STEP = 10
VMEM_LIMIT_BYTES = 56 * 1024 * 1024
LANES = 128
PACK_COLS = 1024

BIG = ("sb_w_qkv", "sb_w_o", "gm_w_in", "gm_w_out", "ssm_w_in", "ssm_w_out", "ffn_w_gu", "ffn_w_down")
COL_SHARDED = ("sb_w_qkv", "gm_w_in", "ssm_w_in", "ffn_w_gu")
SMALL = ("mix_norm", "ffn_norm", "sb_q_gain", "sb_k_gain", "gm_b_in", "gm_v_gain", "gm_w_s", "gm_b_s",
         "ssm_conv_w", "ssm_conv_b", "ssm_dt_bias", "ssm_a_log", "ssm_d", "ssm_norm_gain")
SMALL_SHARDED = ("ssm_conv_w", "ssm_conv_b", "ssm_norm_gain")
WEIGHTS = ("mix_norm", "ffn_norm", "sb_w_qkv", "sb_q_gain", "sb_k_gain", "sb_w_o", "gm_w_in", "gm_b_in",
           "gm_v_gain", "gm_w_s", "gm_b_s", "gm_w_out", "ssm_w_in", "ssm_conv_w", "ssm_conv_b", "ssm_dt_bias",
           "ssm_a_log", "ssm_d", "ssm_norm_gain", "ssm_w_out", "ffn_w_gu", "ffn_w_down")


def _params(semantics=None):
    return pltpu.CompilerParams(dimension_semantics=semantics, vmem_limit_bytes=VMEM_LIMIT_BYTES)


def _pick(n, prefs):
    for t in prefs:
        if t <= n and n % t == 0:
            return t
    return n


def _dot(a, b, ca=1, cb=0):
    return lax.dot_general(a, b, (((ca,), (cb,)), ((), ())), preferred_element_type=F32)


def _split3(v):
    h1 = v.astype(BF16)
    r1 = v - h1.astype(F32)
    h2 = r1.astype(BF16)
    h3 = (r1 - h2.astype(F32)).astype(BF16)
    return h1, h2, h3


def _dot_exact_left(mat01, v):
    h1, h2, h3 = _split3(v)
    return _dot(mat01, h1) + _dot(mat01, h2) + _dot(mat01, h3)


def _dot_split2_right(v, mat01):
    hi = v.astype(BF16)
    lo = (v - hi.astype(F32)).astype(BF16)
    return _dot(hi, mat01) + _dot(lo, mat01)


def _sigmoid(v):
    return 1.0 / (1.0 + jnp.exp(-v))


def _softplus(v):
    return jnp.maximum(v, 0.0) + jnp.log(1.0 + jnp.exp(-jnp.abs(v)))


def _erf(v):
    a = jnp.abs(v)
    t = 1.0 / (1.0 + 0.3275911 * a)
    poly = t * (0.254829592 + t * (-0.284496736 + t * (1.421413741 + t * (-1.453152027 + t * 1.061405429))))
    e = 1.0 - poly * jnp.exp(-a * a)
    return jnp.where(v < 0, -e, e)


def _gelu_and_grad(v):
    cdf = 0.5 * (1.0 + _erf(v * (1.0 / math.sqrt(2.0))))
    pdf = jnp.exp(-0.5 * v * v) * (1.0 / math.sqrt(2.0 * math.pi))
    return v * cdf, cdf + v * pdf


def matmul(a, b, *, ta=False, tb=False, out_dtype=F32, residual=None, name):
    if ta:
        K, M = a.shape
    else:
        M, K = a.shape
    if tb:
        N, Kb = b.shape
    else:
        Kb, N = b.shape
    assert K == Kb, (a.shape, b.shape, ta, tb)
    tm = _pick(M, (1024, 1408, 768, 512, 256, 128))
    tn = _pick(N, (1024, 1408, 1536, 768, 512, 256, 128))
    tk = _pick(K, (512, 1408, 256, 128))
    nk = K // tk
    a_spec = pl.BlockSpec((tk, tm), lambda i, j, k: (k, i)) if ta else pl.BlockSpec((tm, tk), lambda i, j, k: (i, k))
    b_spec = pl.BlockSpec((tn, tk), lambda i, j, k: (j, k)) if tb else pl.BlockSpec((tk, tn), lambda i, j, k: (k, j))
    o_spec = pl.BlockSpec((tm, tn), lambda i, j, k: (i, j))
    ca, cb = (0 if ta else 1), (1 if tb else 0)
    has_res = residual is not None

    def body(*refs):
        if has_res:
            a_ref, b_ref, r_ref, o_ref, acc = refs
        else:
            a_ref, b_ref, o_ref, acc = refs
        k = pl.program_id(2)

        @pl.when(k == 0)
        def _():
            acc[...] = jnp.zeros_like(acc)

        acc[...] += _dot(a_ref[...].astype(BF16), b_ref[...].astype(BF16), ca, cb)

        @pl.when(k == nk - 1)
        def _():
            r = acc[...]
            if has_res:
                r = r + r_ref[...]
            o_ref[...] = r.astype(out_dtype)

    in_specs = [a_spec, b_spec] + ([o_spec] if has_res else [])
    args = (a, b) + ((residual,) if has_res else ())
    return pl.pallas_call(
        body, name=name, grid=(M // tm, N // tn, nk), in_specs=in_specs, out_specs=o_spec,
        out_shape=jax.ShapeDtypeStruct((M, N), out_dtype), scratch_shapes=[pltpu.VMEM((tm, tn), F32)],
        compiler_params=_params(("parallel", "parallel", "arbitrary")),
    )(*args)


def rms_fwd(x, gain, *, name):
    S, D = x.shape
    tr = _pick(S, (512, 256, 128))

    def body(x_ref, g_ref, o_ref):
        xv = x_ref[...]
        r = lax.rsqrt(jnp.mean(xv * xv, axis=1, keepdims=True) + EPS)
        o_ref[...] = (xv * r * g_ref[...]).astype(BF16)

    return pl.pallas_call(
        body, name=name, grid=(S // tr,),
        in_specs=[pl.BlockSpec((tr, D), lambda i: (i, 0)), pl.BlockSpec((1, D), lambda i: (0, 0))],
        out_specs=pl.BlockSpec((tr, D), lambda i: (i, 0)), out_shape=jax.ShapeDtypeStruct((S, D), BF16),
        compiler_params=_params(("parallel",)),
    )(x, gain)


def rms_bwd(x, gain, dh, dres, *, name):
    S, D = x.shape
    tr = _pick(S, (512, 256, 128))

    def body(x_ref, g_ref, dh_ref, dr_ref, dx_ref, dg_ref):
        @pl.when(pl.program_id(0) == 0)
        def _():
            dg_ref[...] = jnp.zeros_like(dg_ref)

        xv = x_ref[...]
        dhv = dh_ref[...]
        r = lax.rsqrt(jnp.mean(xv * xv, axis=1, keepdims=True) + EPS)
        xhat = xv * r
        t = dhv * g_ref[...]
        dx_ref[...] = dr_ref[...] + r * (t - xhat * jnp.mean(xhat * t, axis=1, keepdims=True))
        dg_ref[...] += jnp.sum(dhv * xhat, axis=0, keepdims=True)

    row = pl.BlockSpec((tr, D), lambda i: (i, 0))
    vec = pl.BlockSpec((1, D), lambda i: (0, 0))
    return pl.pallas_call(
        body, name=name, grid=(S // tr,), in_specs=[row, vec, row, row], out_specs=[row, vec],
        out_shape=[jax.ShapeDtypeStruct((S, D), F32), jax.ShapeDtypeStruct((1, D), F32)],
        compiler_params=_params(("arbitrary",)),
    )(x, gain, dh, dres)


def swiglu_fwd(gu, *, name):
    S, F2 = gu.shape
    F = F2 // 2
    tr = _pick(S, (256, 128))

    def body(gu_ref, o_ref):
        g = gu_ref[:, :F]
        u = gu_ref[:, F:]
        o_ref[...] = (g * _sigmoid(g) * u).astype(BF16)

    return pl.pallas_call(
        body, name=name, grid=(S // tr,), in_specs=[pl.BlockSpec((tr, F2), lambda i: (i, 0))],
        out_specs=pl.BlockSpec((tr, F), lambda i: (i, 0)), out_shape=jax.ShapeDtypeStruct((S, F), BF16),
        compiler_params=_params(("parallel",)),
    )(gu)


def swiglu_bwd(gu, da, *, name):
    S, F2 = gu.shape
    F = F2 // 2
    tr = _pick(S, (256, 128))

    def body(gu_ref, da_ref, o_ref):
        g = gu_ref[:, :F]
        u = gu_ref[:, F:]
        dav = da_ref[...]
        s = _sigmoid(g)
        o_ref[:, :F] = (dav * u * (s * (1.0 + g * (1.0 - s)))).astype(BF16)
        o_ref[:, F:] = (dav * g * s).astype(BF16)

    return pl.pallas_call(
        body, name=name, grid=(S // tr,),
        in_specs=[pl.BlockSpec((tr, F2), lambda i: (i, 0)), pl.BlockSpec((tr, F), lambda i: (i, 0))],
        out_specs=pl.BlockSpec((tr, F2), lambda i: (i, 0)), out_shape=jax.ShapeDtypeStruct((S, F2), BF16),
        compiler_params=_params(("parallel",)),
    )(gu, da)


def loss_head(y, target, *, name):
    S, D = y.shape
    tr = _pick(S, (512, 256, 128))

    def body(y_ref, t_ref, dy_ref, l_ref):
        @pl.when(pl.program_id(0) == 0)
        def _():
            l_ref[...] = jnp.zeros_like(l_ref)

        err = y_ref[...] - t_ref[...]
        dy_ref[...] = err * (1.0 / D)
        l_ref[...] += jnp.sum(0.5 * jnp.mean(err * err, axis=1, keepdims=True), axis=0, keepdims=True)

    row = pl.BlockSpec((tr, D), lambda i: (i, 0))
    one = pl.BlockSpec((1, 1), lambda i: (0, 0))
    dy, l = pl.pallas_call(
        body, name=name, grid=(S // tr,), in_specs=[row, row], out_specs=[row, one],
        out_shape=[jax.ShapeDtypeStruct((S, D), F32), jax.ShapeDtypeStruct((1, 1), F32)],
        compiler_params=_params(("arbitrary",)),
    )(y, target)
    return dy, l


def adamw(w, g, m, v, *, name):
    R, C = w.shape
    tr = _pick(R, (512, 256, 128, 64, 32, 16, 8))

    def body(w_ref, g_ref, m_ref, v_ref, d_ref, mo_ref, vo_ref):
        gv = g_ref[...]
        mn = ADAM_B1 * m_ref[...] + (1.0 - ADAM_B1) * gv
        vn = ADAM_B2 * v_ref[...] + (1.0 - ADAM_B2) * jnp.square(gv)
        m_hat = mn / (1.0 - ADAM_B1 ** ADAM_STEP)
        v_hat = vn / (1.0 - ADAM_B2 ** ADAM_STEP)
        d_ref[...] = -ADAM_LR * (m_hat / (jnp.sqrt(v_hat) + ADAM_EPS) + ADAM_WD * w_ref[...])
        mo_ref[...] = mn
        vo_ref[...] = vn

    blk = pl.BlockSpec((tr, C), lambda i: (i, 0))
    sds = jax.ShapeDtypeStruct((R, C), F32)
    return pl.pallas_call(
        body, name=name, grid=(R // tr,), in_specs=[blk] * 4, out_specs=[blk] * 3, out_shape=[sds] * 3,
        compiler_params=_params(("parallel",)),
    )(w, g, m, v)


def _tri(n, kind):
    r = lax.broadcasted_iota(jnp.int32, (n, n), 0)
    c = lax.broadcasted_iota(jnp.int32, (n, n), 1)
    if kind == "row_gt_col":
        return (r > c).astype(BF16)
    if kind == "row_ge_col":
        return (r >= c).astype(BF16)
    if kind == "row_le_col":
        return (r <= c).astype(BF16)
    raise ValueError(kind)


def _sb_tile(qi, kj, scale, r_carry, u_strict, masked):
    z = _dot(qi, kj, 1, 1) * scale
    e = jnp.exp(-jnp.abs(z))
    lb = jnp.minimum(z, 0.0) - jnp.log(1.0 + e)
    l1m = lb - z
    if masked:
        tq, tk = z.shape
        keep = lax.broadcasted_iota(jnp.int32, (tq, tk), 1) < lax.broadcasted_iota(jnp.int32, (tq, tk), 0)
        l1m = jnp.where(keep, l1m, 0.0)
    suffix = _dot_split2_right(l1m, u_strict)
    w = jnp.exp(lb + suffix + r_carry)
    if masked:
        w = jnp.where(keep, w, 0.0)
    else:
        keep = None
    return z, e, l1m, w, keep


def sb_attn_fwd(qkv, q_gain, k_gain, *, name):
    S, D3 = qkv.shape
    D = D3 // 3
    npairs = D // LANES
    hd = SB_HEAD_DIM
    T = min(256, S)
    nb = S // T
    scale = 1.0 / math.sqrt(hd)

    def body(q_ref, k_ref, v_ref, qg_ref, kg_ref, us_ref, o_ref, ox_ref, qn_s, kn_s, vb_s):
        us = us_ref[...]
        for hh in range(2):
            sl = slice(hd * hh, hd * hh + hd)

            def prep(i, _):
                rows = pl.ds(pl.multiple_of(i * T, T), T)
                qv = q_ref[rows, sl]
                kv = k_ref[rows, sl]
                qn_s[rows, :] = (qv * lax.rsqrt(jnp.mean(qv * qv, axis=1, keepdims=True) + EPS) * qg_ref[...]).astype(BF16)
                kn_s[rows, :] = (kv * lax.rsqrt(jnp.mean(kv * kv, axis=1, keepdims=True) + EPS) * kg_ref[...]).astype(BF16)
                vb_s[rows, :] = v_ref[rows, sl].astype(BF16)
                return 0

            lax.fori_loop(0, nb, prep, 0)

            def qblock(i, _):
                rows_i = pl.ds(pl.multiple_of(i * T, T), T)
                qi = qn_s[rows_i, :]

                def tile(j, carry, masked):
                    acc, acc_lo, rc = carry
                    rows_j = pl.ds(pl.multiple_of(j * T, T), T)
                    _, _, l1m, w, _ = _sb_tile(qi, kn_s[rows_j, :], scale, rc, us, masked)
                    vj = vb_s[rows_j, :]
                    wb = w.astype(BF16)
                    acc = acc + _dot(wb, vj)
                    acc_lo = acc_lo + _dot((w - wb.astype(F32)).astype(BF16), vj)
                    rc = rc + jnp.sum(l1m, axis=1, keepdims=True)
                    return acc, acc_lo, rc

                zero = jnp.zeros((T, hd), F32)
                carry = tile(i, (zero, zero, jnp.zeros((T, 1), F32)), True)
                acc, acc_lo, _ = lax.fori_loop(0, i, lambda jj, c: tile(i - 1 - jj, c, False), carry)
                o_ref[rows_i, sl] = acc
                ox_ref[rows_i, sl] = acc + acc_lo
                return 0

            lax.fori_loop(0, nb, qblock, 0)

    col = lambda off: pl.BlockSpec((S, LANES), lambda p, off=off: (0, off + p))
    gain = pl.BlockSpec((1, hd), lambda p: (0, 0))
    return pl.pallas_call(
        body, name=name, grid=(npairs,),
        in_specs=[col(0), col(npairs), col(2 * npairs), gain, gain, pl.BlockSpec((T, T), lambda p: (0, 0))],
        out_specs=[pl.BlockSpec((S, LANES), lambda p: (0, p))] * 2, out_shape=[jax.ShapeDtypeStruct((S, D), F32)] * 2,
        scratch_shapes=[pltpu.VMEM((S, hd), BF16)] * 3,
        compiler_params=_params(("parallel",)),
    )(qkv, qkv, qkv, q_gain, k_gain, _tri(T, "row_gt_col"))


def sb_attn_bwd(qkv, ox, do, q_gain, k_gain, *, name):
    S, D3 = qkv.shape
    D = D3 // 3
    npairs = D // LANES
    hd = SB_HEAD_DIM
    T = min(256, S)
    nb = S // T
    scale = 1.0 / math.sqrt(hd)

    def body(q_ref, k_ref, v_ref, ox_ref, do_ref, qg_ref, kg_ref, us_ref, ui_ref,
             dq_ref, dk_ref, dv_ref, dg_ref, qn_s, kn_s, vb_s, dob_s, dqn_s, dkn_s, dv_s):
        @pl.when(pl.program_id(0) == 0)
        def _():
            dg_ref[...] = jnp.zeros_like(dg_ref)

        us = us_ref[...]
        ui = ui_ref[...]
        for hh in range(2):
            sl = slice(hd * hh, hd * hh + hd)

            def prep(i, _):
                rows = pl.ds(pl.multiple_of(i * T, T), T)
                qv = q_ref[rows, sl]
                kv = k_ref[rows, sl]
                qn_s[rows, :] = (qv * lax.rsqrt(jnp.mean(qv * qv, axis=1, keepdims=True) + EPS) * qg_ref[...]).astype(BF16)
                kn_s[rows, :] = (kv * lax.rsqrt(jnp.mean(kv * kv, axis=1, keepdims=True) + EPS) * kg_ref[...]).astype(BF16)
                vb_s[rows, :] = v_ref[rows, sl].astype(BF16)
                dob_s[rows, :] = do_ref[rows, sl].astype(BF16)
                dkn_s[rows, :] = jnp.zeros((T, hd), F32)
                dv_s[rows, :] = jnp.zeros((T, hd), F32)
                return 0

            lax.fori_loop(0, nb, prep, 0)

            def qblock(i, _):
                rows_i = pl.ds(pl.multiple_of(i * T, T), T)
                qi = qn_s[rows_i, :]
                doi = dob_s[rows_i, :]
                dt_total = jnp.sum(doi.astype(F32) * ox_ref[rows_i, sl], axis=1, keepdims=True)

                def tile(j, carry, masked):
                    dq_acc, rc, gc = carry
                    rows_j = pl.ds(pl.multiple_of(j * T, T), T)
                    kj = kn_s[rows_j, :]
                    z, e, l1m, w, keep = _sb_tile(qi, kj, scale, rc, us, masked)
                    g = _dot(doi, vb_s[rows_j, :], 1, 1) * w
                    dv_s[rows_j, :] += _dot(w.astype(BF16), doi, 0, 0)
                    g_prefix = dt_total - (gc + _dot_split2_right(g, ui))
                    inv = 1.0 / (1.0 + e)
                    pos = z >= 0
                    sig = jnp.where(pos, inv, e * inv)
                    sig_neg = jnp.where(pos, e * inv, inv)
                    dz = g * sig_neg - g_prefix * sig
                    if masked:
                        dz = jnp.where(keep, dz, 0.0)
                    dzb = (dz * scale).astype(BF16)
                    dq_acc = dq_acc + _dot(dzb, kj)
                    dkn_s[rows_j, :] += _dot(dzb, qi, 0, 0)
                    rc = rc + jnp.sum(l1m, axis=1, keepdims=True)
                    gc = gc + jnp.sum(g, axis=1, keepdims=True)
                    return dq_acc, rc, gc

                zero1 = jnp.zeros((T, 1), F32)
                carry = tile(i, (jnp.zeros((T, hd), F32), zero1, zero1), True)
                dq_acc, _, _ = lax.fori_loop(0, i, lambda jj, c: tile(i - 1 - jj, c, False), carry)
                dqn_s[rows_i, :] = dq_acc
                return 0

            lax.fori_loop(0, nb, qblock, 0)

            def finish(i, carry):
                gq, gk = carry
                rows = pl.ds(pl.multiple_of(i * T, T), T)
                outs = []
                for raw_ref, gain_ref, dn_s in ((q_ref, qg_ref, dqn_s), (k_ref, kg_ref, dkn_s)):
                    raw = raw_ref[rows, sl]
                    r = lax.rsqrt(jnp.mean(raw * raw, axis=1, keepdims=True) + EPS)
                    hat = raw * r
                    dn = dn_s[rows, :]
                    t = dn * gain_ref[...]
                    outs.append((r * (t - hat * jnp.mean(hat * t, axis=1, keepdims=True)),
                                 jnp.sum(dn * hat, axis=0, keepdims=True)))
                dq_ref[rows, sl] = outs[0][0]
                dk_ref[rows, sl] = outs[1][0]
                dv_ref[rows, sl] = dv_s[rows, :]
                return gq + outs[0][1], gk + outs[1][1]

            zg = jnp.zeros((1, hd), F32)
            gq, gk = lax.fori_loop(0, nb, finish, (zg, zg))
            dg_ref[0:1, 0:hd] += gq
            dg_ref[1:2, 0:hd] += gk

    col = lambda off: pl.BlockSpec((S, LANES), lambda p, off=off: (0, off + p))
    gain = pl.BlockSpec((1, hd), lambda p: (0, 0))
    tri = pl.BlockSpec((T, T), lambda p: (0, 0))
    pair = pl.BlockSpec((S, LANES), lambda p: (0, p))
    dq, dk, dv, dg = pl.pallas_call(
        body, name=name, grid=(npairs,),
        in_specs=[col(0), col(npairs), col(2 * npairs), pair, pair, gain, gain, tri, tri],
        out_specs=[pair, pair, pair, pl.BlockSpec((8, LANES), lambda p: (0, 0))],
        out_shape=[jax.ShapeDtypeStruct((S, D), F32)] * 3 + [jax.ShapeDtypeStruct((8, LANES), F32)],
        scratch_shapes=[pltpu.VMEM((S, hd), BF16)] * 4 + [pltpu.VMEM((S, hd), F32)] * 3,
        compiler_params=_params(("arbitrary",)),
    )(qkv, qkv, qkv, ox, do, q_gain, k_gain, _tri(T, "row_gt_col"), _tri(T, "row_ge_col"))
    return jnp.concatenate([dq, dk, dv], axis=1), dg[0:1, :hd], dg[1:2, :hd]


def gmlp_fwd(zzpre, b_in, v_gain, wc, bsf, *, name):
    S, H2 = zzpre.shape
    H = H2 // 2
    G, T, _ = wc.shape
    gd = H // G

    def body(z_ref, b_ref, vg_ref, wc_ref, bs_ref, p_ref):
        zz, _ = _gelu_and_grad(z_ref[...] + b_ref[...])
        u = zz[:, :H]
        v = zz[:, H:]
        vn = v * lax.rsqrt(jnp.mean(v * v, axis=1, keepdims=True) + EPS) * vg_ref[...]
        for g in range(G):
            gs = slice(g * gd, (g + 1) * gd)
            mixed = _dot(wc_ref[g], vn[:, gs].astype(BF16)) + bs_ref[g]
            p_ref[:, gs] = (u[:, gs] * mixed).astype(BF16)

    full3 = lambda shp: pl.BlockSpec(shp, lambda c: (0, 0, 0))
    return pl.pallas_call(
        body, name=name, grid=(S // T,),
        in_specs=[pl.BlockSpec((T, H2), lambda c: (c, 0)), pl.BlockSpec((1, H2), lambda c: (0, 0)),
                  pl.BlockSpec((1, H), lambda c: (0, 0)), full3((G, T, T)), full3((G, T, gd))],
        out_specs=pl.BlockSpec((T, H), lambda c: (c, 0)), out_shape=jax.ShapeDtypeStruct((S, H), BF16),
        compiler_params=_params(("parallel",)),
    )(zzpre, b_in, v_gain, wc, bsf)


def gmlp_bwd(zzpre, b_in, v_gain, wc, bsf, dp, *, name):
    S, H2 = zzpre.shape
    H = H2 // 2
    G, T, _ = wc.shape
    gd = H // G
    assert G <= LANES

    def body(z_ref, b_ref, vg_ref, wc_ref, bs_ref, dp_ref, dzz_ref, db_ref, dvg_ref, dws_ref, dbs_ref):
        @pl.when(pl.program_id(0) == 0)
        def _():
            db_ref[...] = jnp.zeros_like(db_ref)
            dvg_ref[...] = jnp.zeros_like(dvg_ref)
            dws_ref[...] = jnp.zeros_like(dws_ref)
            dbs_ref[...] = jnp.zeros_like(dbs_ref)

        zz, gp = _gelu_and_grad(z_ref[...] + b_ref[...])
        u = zz[:, :H]
        v = zz[:, H:]
        r = lax.rsqrt(jnp.mean(v * v, axis=1, keepdims=True) + EPS)
        vhat = v * r
        vg = vg_ref[...]
        vn = vhat * vg
        dpv = dp_ref[...]
        tril = lax.broadcasted_iota(jnp.int32, (T, T), 1) <= lax.broadcasted_iota(jnp.int32, (T, T), 0)
        lane = lax.broadcasted_iota(jnp.int32, (T, LANES), 1)
        dbs = jnp.zeros((T, LANES), F32)
        du_parts, dvn_parts = [], []
        for g in range(G):
            gs = slice(g * gd, (g + 1) * gd)
            vng = vn[:, gs].astype(BF16)
            wcg = wc_ref[g]
            mixed = _dot(wcg, vng) + bs_ref[g]
            dpg = dpv[:, gs]
            du_parts.append(dpg * mixed)
            dmx = dpg * u[:, gs]
            dmxb = dmx.astype(BF16)
            dvn_parts.append(_dot(wcg, dmxb, 0, 0))
            dws_ref[g] += jnp.where(tril, _dot(dmxb, vng, 1, 1), 0.0)
            dbs = dbs + jnp.where(lane == g, jnp.sum(dmx, axis=1, keepdims=True), 0.0)
        dbs_ref[...] += dbs
        du = jnp.concatenate(du_parts, axis=1)
        dvn = jnp.concatenate(dvn_parts, axis=1)
        dvg_ref[...] += jnp.sum(dvn * vhat, axis=0, keepdims=True)
        t = dvn * vg
        dv = r * (t - vhat * jnp.mean(vhat * t, axis=1, keepdims=True))
        dzu = du * gp[:, :H]
        dzv = dv * gp[:, H:]
        dzz_ref[:, :H] = dzu.astype(BF16)
        dzz_ref[:, H:] = dzv.astype(BF16)
        db_ref[:, :H] += jnp.sum(dzu, axis=0, keepdims=True)
        db_ref[:, H:] += jnp.sum(dzv, axis=0, keepdims=True)

    full3 = lambda shp: pl.BlockSpec(shp, lambda c: (0, 0, 0))
    vec = lambda n: pl.BlockSpec((1, n), lambda c: (0, 0))
    return pl.pallas_call(
        body, name=name, grid=(S // T,),
        in_specs=[pl.BlockSpec((T, H2), lambda c: (c, 0)), vec(H2), vec(H), full3((G, T, T)), full3((G, T, gd)),
                  pl.BlockSpec((T, H), lambda c: (c, 0))],
        out_specs=[pl.BlockSpec((T, H2), lambda c: (c, 0)), vec(H2), vec(H), full3((G, T, T)),
                   pl.BlockSpec((T, LANES), lambda c: (0, 0))],
        out_shape=[jax.ShapeDtypeStruct((S, H2), BF16), jax.ShapeDtypeStruct((1, H2), F32),
                   jax.ShapeDtypeStruct((1, H), F32), jax.ShapeDtypeStruct((G, T, T), F32),
                   jax.ShapeDtypeStruct((T, LANES), F32)],
        compiler_params=_params(("arbitrary",)),
    )(zzpre, b_in, v_gain, wc, bsf, dp)


def _shift_rows(v, k, n_rows):
    if k == 0:
        return v
    rolled = pltpu.roll(v, k % n_rows, 0)
    row = lax.broadcasted_iota(jnp.int32, v.shape, 0)
    keep = (row >= k) if k > 0 else (row < n_rows + k)
    return jnp.where(keep, rolled, 0.0)


def conv_fwd(zx, conv_w, conv_b, col0, *, name):
    S = zx.shape[0]
    C = conv_w.shape[1]
    tc = _pick(C, (256, 128))
    off = col0 // tc
    assert col0 % tc == 0

    def body(x_ref, w_ref, b_ref, o_ref):
        xv = x_ref[...]
        acc = b_ref[...] + w_ref[SSM_CONV - 1:SSM_CONV, :] * xv
        for k in range(SSM_CONV - 1):
            acc = acc + w_ref[k:k + 1, :] * _shift_rows(xv, SSM_CONV - 1 - k, S)
        o_ref[...] = acc * _sigmoid(acc)

    return pl.pallas_call(
        body, name=name, grid=(C // tc,),
        in_specs=[pl.BlockSpec((S, tc), lambda j: (0, off + j)), pl.BlockSpec((SSM_CONV, tc), lambda j: (0, j)),
                  pl.BlockSpec((1, tc), lambda j: (0, j))],
        out_specs=pl.BlockSpec((S, tc), lambda j: (0, j)), out_shape=jax.ShapeDtypeStruct((S, C), F32),
        compiler_params=_params(("parallel",)),
    )(zx, conv_w, conv_b)


def conv_bwd(zx, conv_w, conv_b, col0, dout, *, name):
    S = zx.shape[0]
    C = conv_w.shape[1]
    tc = _pick(C, (256, 128))
    off = col0 // tc

    def body(x_ref, w_ref, b_ref, do_ref, dx_ref, dw_ref, db_ref):
        xv = x_ref[...]
        shifted = [_shift_rows(xv, SSM_CONV - 1 - k, S) for k in range(SSM_CONV)]
        acc = b_ref[...]
        for k in range(SSM_CONV):
            acc = acc + w_ref[k:k + 1, :] * shifted[k]
        s = _sigmoid(acc)
        dacc = do_ref[...] * (s * (1.0 + acc * (1.0 - s)))
        db_ref[...] = jnp.sum(dacc, axis=0, keepdims=True)
        dx = jnp.zeros_like(xv)
        for k in range(SSM_CONV):
            dw_ref[k:k + 1, :] = jnp.sum(dacc * shifted[k], axis=0, keepdims=True)
            dx = dx + w_ref[k:k + 1, :] * _shift_rows(dacc, -(SSM_CONV - 1 - k), S)
        dx_ref[...] = dx

    slab = pl.BlockSpec((S, tc), lambda j: (0, j))
    return pl.pallas_call(
        body, name=name, grid=(C // tc,),
        in_specs=[pl.BlockSpec((S, tc), lambda j: (0, off + j)), pl.BlockSpec((SSM_CONV, tc), lambda j: (0, j)),
                  pl.BlockSpec((1, tc), lambda j: (0, j)), slab],
        out_specs=[slab, pl.BlockSpec((SSM_CONV, tc), lambda j: (0, j)), pl.BlockSpec((1, tc), lambda j: (0, j))],
        out_shape=[jax.ShapeDtypeStruct((S, C), F32), jax.ShapeDtypeStruct((SSM_CONV, C), F32),
                   jax.ShapeDtypeStruct((1, C), F32)],
        compiler_params=_params(("parallel",)),
    )(zx, conv_w, conv_b, dout)


def _ssd_chunk_terms(dtraw, bias, a_log, tl):
    dt = _softplus(dtraw + bias)
    a_neg = -jnp.exp(a_log)
    ac = _dot_exact_left(tl, dt * a_neg)
    ac_last = ac[ac.shape[0] - 1:, :]
    return dt, a_neg, ac, ac.T, jnp.exp(ac), jnp.exp(ac_last - ac), jnp.exp(ac_last)


def _ssd_specs(S, L, G, hpg, pd, inner):
    gw = hpg * pd
    n = SSM_STATE
    xb = inner // n

    def mk(cidx):
        return dict(
            x=pl.BlockSpec((L, gw), lambda g, c: (cidx(c), g)),
            b=pl.BlockSpec((L, n), lambda g, c: (cidx(c), xb + g)),
            c=pl.BlockSpec((L, n), lambda g, c: (cidx(c), xb + G + g)),
            z=pl.BlockSpec((L, gw), lambda g, c: (cidx(c), g)),
            dt=pl.BlockSpec((L, LANES), lambda g, c: (cidx(c), g)),
            gvec=pl.BlockSpec((1, 1, LANES), lambda g, c: (g, 0, 0)),
            chan=pl.BlockSpec((1, gw), lambda g, c: (0, g)),
            tri=pl.BlockSpec((L, L), lambda g, c: (0, 0)),
            hp=pl.BlockSpec((1, 1, gw, n), lambda g, c: (g, cidx(c), 0, 0)),
            bc=pl.BlockSpec((L, n), lambda g, c: (cidx(c), g)),
        )
    return mk


def ssd_fwd(xbc, zx, dtg, bias_g, alog_g, d_chan, ngain, L, G, *, name):
    S = xbc.shape[0]
    n = SSM_STATE
    inner = xbc.shape[1] - 2 * G * n
    gw = inner // G
    pd = SB_HEAD_DIM
    hpg = gw // pd
    nc = S // L
    sp = _ssd_specs(S, L, G, hpg, pd, inner)(lambda c: c)

    def body(x_ref, b_ref, c_ref, z_ref, dt_ref, bias_ref, alog_ref, d_ref, ng_ref, tl_ref,
             yn_ref, y_ref, hp_ref, state):
        @pl.when(pl.program_id(1) == 0)
        def _():
            state[...] = jnp.zeros_like(state)

        dt, _, ac, act, ea, dte, cd = _ssd_chunk_terms(dt_ref[...], bias_ref[0], alog_ref[0], tl_ref[...])
        xv = x_ref[...]
        bm = b_ref[...].astype(BF16)
        cm = c_ref[...].astype(BF16)
        cb = _dot(cm, bm, 1, 1)
        tril = lax.broadcasted_iota(jnp.int32, (L, L), 1) <= lax.broadcasted_iota(jnp.int32, (L, L), 0)
        hp_ref[0, 0] = state[...]
        for r in range(hpg):
            ps = slice(r * pd, (r + 1) * pd)
            xr = xv[:, ps]
            xdt = xr * dt[:, r:r + 1]
            lm = jnp.exp(jnp.where(tril, ac[:, r:r + 1] - act[r:r + 1, :], -jnp.inf))
            hprev = state[ps, :]
            y = _dot((cb * lm).astype(BF16), xdt.astype(BF16))
            y = y + _dot(cm, hprev.astype(BF16), 1, 1) * ea[:, r:r + 1]
            y_ref[:, ps] = y + xr * d_ref[:, ps]
            st = _dot((xdt * dte[:, r:r + 1]).astype(BF16), bm, 0, 0)
            state[ps, :] = hprev * cd[:, r:r + 1] + st
        yfull = y_ref[...]
        zg = z_ref[...]
        yg = yfull * (zg * _sigmoid(zg))
        yn_ref[...] = (yg * lax.rsqrt(jnp.mean(yg * yg, axis=1, keepdims=True) + EPS) * ng_ref[...]).astype(BF16)

    return pl.pallas_call(
        body, name=name, grid=(G, nc),
        in_specs=[sp["x"], sp["b"], sp["c"], sp["z"], sp["dt"], sp["gvec"], sp["gvec"], sp["chan"], sp["chan"], sp["tri"]],
        out_specs=[sp["x"], sp["x"], sp["hp"]],
        out_shape=[jax.ShapeDtypeStruct((S, inner), BF16), jax.ShapeDtypeStruct((S, inner), F32),
                   jax.ShapeDtypeStruct((G, nc, gw, n), F32)],
        scratch_shapes=[pltpu.VMEM((gw, n), F32)],
        compiler_params=_params(("arbitrary", "arbitrary")),
    )(xbc, xbc, xbc, zx, dtg, bias_g, alog_g, d_chan, ngain, _tri(L, "row_ge_col"))


def ssd_bwd(xbc, zx, dtg, bias_g, alog_g, d_chan, ngain, yfull, hp, dyn, L, G, *, name):
    S = xbc.shape[0]
    n = SSM_STATE
    inner = xbc.shape[1] - 2 * G * n
    gw = inner // G
    pd = SB_HEAD_DIM
    hpg = gw // pd
    nc = S // L
    sp = _ssd_specs(S, L, G, hpg, pd, inner)(lambda c: nc - 1 - c)

    def body(x_ref, b_ref, c_ref, z_ref, dt_ref, bias_ref, alog_ref, d_ref, ng_ref, tl_ref, tu_ref,
             yf_ref, hp_ref, dyn_ref,
             dz_ref, dx_ref, db_ref, dc_ref, ddt_ref, dbias_ref, dalog_ref, dd_ref, dng_ref, dstate):
        first = pl.program_id(1) == 0

        @pl.when(first)
        def _():
            dstate[...] = jnp.zeros_like(dstate)
            dbias_ref[...] = jnp.zeros_like(dbias_ref)
            dalog_ref[...] = jnp.zeros_like(dalog_ref)
            dd_ref[...] = jnp.zeros_like(dd_ref)
            dng_ref[...] = jnp.zeros_like(dng_ref)

        dtraw = dt_ref[...]
        dt, a_neg, ac, act, ea, dte, cd = _ssd_chunk_terms(dtraw, bias_ref[0], alog_ref[0], tl_ref[...])
        xv = x_ref[...]
        bm = b_ref[...].astype(BF16)
        cm = c_ref[...].astype(BF16)
        cb = _dot(cm, bm, 1, 1)
        tril = lax.broadcasted_iota(jnp.int32, (L, L), 1) <= lax.broadcasted_iota(jnp.int32, (L, L), 0)
        lane = lax.broadcasted_iota(jnp.int32, (L, LANES), 1)
        lane1 = lax.broadcasted_iota(jnp.int32, (1, LANES), 1)

        yfull = yf_ref[...]
        zg = z_ref[...]
        sg = _sigmoid(zg)
        gate = zg * sg
        yg = yfull * gate
        rr = lax.rsqrt(jnp.mean(yg * yg, axis=1, keepdims=True) + EPS)
        yhat = yg * rr
        dynv = dyn_ref[...]
        dng_ref[...] += jnp.sum(dynv * yhat, axis=0, keepdims=True)
        t = dynv * ng_ref[...]
        dyg = rr * (t - yhat * jnp.mean(yhat * t, axis=1, keepdims=True))
        dy = dyg * gate
        dz_ref[...] = dyg * yfull * (sg * (1.0 + zg * (1.0 - sg)))

        dcb = jnp.zeros((L, L), F32)
        dc_acc = jnp.zeros((L, n), F32)
        db_acc = jnp.zeros((L, n), F32)
        dac = jnp.zeros((L, LANES), F32)
        xdx = jnp.zeros((L, LANES), F32)
        tail = jnp.zeros((1, LANES), F32)
        dskip = jnp.zeros((1, LANES), F32)
        ones_l = jnp.ones((L, LANES), BF16)
        for r in range(hpg):
            ps = slice(r * pd, (r + 1) * pd)
            xr = xv[:, ps]
            dyr = dy[:, ps]
            dtr = dt[:, r:r + 1]
            dter = dte[:, r:r + 1]
            cdr = cd[:, r:r + 1]
            xdt = xr * dtr
            xdtb = xdt.astype(BF16)
            dyrb = dyr.astype(BF16)
            lm = jnp.exp(jnp.where(tril, ac[:, r:r + 1] - act[r:r + 1, :], -jnp.inf))
            m32 = cb * lm
            mb = m32.astype(BF16)
            hprev = hp_ref[0, 0, ps, :]
            hpb = hprev.astype(BF16)
            dhn = dstate[ps, :]
            dhnb = dhn.astype(BF16)
            ear = ea[:, r:r + 1]
            gy = (dyr * ear).astype(BF16)
            dc_acc = dc_acc + _dot(gy, hpb)
            dstate[ps, :] = _dot(gy, cm, 0, 0) + dhn * cdr
            bdh = _dot(bm, dhnb, 1, 1)
            db_acc = db_acc + _dot((xdt * dter).astype(BF16), dhnb)
            dm = _dot(dyrb, xdtb, 1, 1)
            dxdt = bdh * dter + _dot(mb, dyrb, 0, 0)
            dcb = dcb + dm * lm
            wmat = dm * m32
            whi = wmat.astype(BF16)
            wlo = (wmat - whi.astype(F32)).astype(BF16)
            col_w = _dot(whi, ones_l, 0, 0) + _dot(wlo, ones_l, 0, 0)
            sdte = jnp.sum(xdt * bdh * dter, axis=1, keepdims=True)
            e_r = jnp.sum(wmat, axis=1, keepdims=True) + jnp.sum(dyr * _dot(cm, hpb, 1, 1) * ear, axis=1, keepdims=True) - sdte
            c_r = cdr * jnp.sum(jnp.sum(dhn * hprev, axis=1, keepdims=True), axis=0, keepdims=True) \
                + jnp.sum(sdte, axis=0, keepdims=True)
            dac = dac + jnp.where(lane == r, e_r - col_w, 0.0)
            xdx = xdx + jnp.where(lane == r, jnp.sum(dxdt * xr, axis=1, keepdims=True), 0.0)
            tail = tail + jnp.where(lane1 == r, c_r, 0.0)
            dskip = dskip + jnp.where(lane1 == r, jnp.sum(jnp.sum(dyr * xr, axis=1, keepdims=True), axis=0, keepdims=True), 0.0)
            dx_ref[:, ps] = dxdt * dtr + dyr * d_ref[:, ps]
        dcbb = dcb.astype(BF16)
        dc_ref[...] = dc_acc + _dot(dcbb, bm)
        db_ref[...] = db_acc + _dot(dcbb, cm, 0, 0)
        da = _dot_exact_left(tu_ref[...], dac) + tail
        real = lane < hpg
        ddt = jnp.where(real, (da * a_neg + xdx) * _sigmoid(dtraw + bias_ref[0]), 0.0)
        ddt_ref[...] = ddt
        dd_ref[0] += dskip
        dbias_ref[0] += jnp.sum(ddt, axis=0, keepdims=True)
        dalog_ref[0] += jnp.where(lane1 < hpg, jnp.sum(da * dt, axis=0, keepdims=True) * a_neg, 0.0)

    return pl.pallas_call(
        body, name=name, grid=(G, nc),
        in_specs=[sp["x"], sp["b"], sp["c"], sp["z"], sp["dt"], sp["gvec"], sp["gvec"], sp["chan"], sp["chan"],
                  sp["tri"], sp["tri"], sp["x"], sp["hp"], sp["x"]],
        out_specs=[sp["x"], sp["x"], sp["bc"], sp["bc"], sp["dt"], sp["gvec"], sp["gvec"], sp["gvec"], sp["chan"]],
        out_shape=[jax.ShapeDtypeStruct((S, inner), F32), jax.ShapeDtypeStruct((S, inner), F32),
                   jax.ShapeDtypeStruct((S, G * n), F32), jax.ShapeDtypeStruct((S, G * n), F32),
                   jax.ShapeDtypeStruct((S, G * LANES), F32), jax.ShapeDtypeStruct((G, 1, LANES), F32),
                   jax.ShapeDtypeStruct((G, 1, LANES), F32), jax.ShapeDtypeStruct((G, 1, LANES), F32),
                   jax.ShapeDtypeStruct((1, inner), F32)],
        scratch_shapes=[pltpu.VMEM((gw, n), F32)],
        compiler_params=_params(("arbitrary", "arbitrary")),
    )(xbc, xbc, xbc, zx, dtg, bias_g, alog_g, d_chan, ngain, _tri(L, "row_ge_col"), _tri(L, "row_le_col"),
      yfull, hp, dyn)


def _spread_dt(w_dt, G, hpg):
    K = w_dt.shape[0]
    w = w_dt.reshape(K, G, hpg)
    return jnp.pad(w, ((0, 0), (0, 0), (0, LANES - hpg))).reshape(K, G * LANES)


def _group_vec(v, G, hpg):
    return jnp.pad(v.reshape(G, 1, hpg), ((0, 0), (0, 0), (0, LANES - hpg)))


def local_step(x, target, W):
    S, D = x.shape
    depth = W["mix_norm"].shape[0]
    gm_groups, gm_chunk = W["gm_w_s"].shape[1], W["gm_w_s"].shape[2]
    inner = W["ssm_w_out"].shape[1]
    heads = W["ssm_dt_bias"].shape[1]
    conv_dim = W["ssm_conv_w"].shape[2]
    G = (conv_dim - inner) // (2 * SSM_STATE)
    hpg = heads // G
    L = gm_chunk

    saved = []
    for i in range(depth):
        kind, j = i % 3, i // 3
        s = dict(x=x)
        h = rms_fwd(x, W["mix_norm"][i:i + 1], name=f"rms_mix_fwd")
        s["h"] = h
        if kind == 0:
            qkv = matmul(h, W["sb_w_qkv"][j], name="mm_qkv")
            o, ox = sb_attn_fwd(qkv, W["sb_q_gain"][j:j + 1], W["sb_k_gain"][j:j + 1], name="sb_fwd")
            x1 = matmul(o, W["sb_w_o"][j], residual=x, name="mm_sb_out")
            s.update(qkv=qkv, o=o, ox=ox)
        elif kind == 1:
            wc = jnp.where(jnp.tril(jnp.ones((gm_chunk, gm_chunk), bool)), W["gm_w_s"][j], 0.0).astype(BF16)
            bsf = jnp.broadcast_to(W["gm_b_s"][j][:, :, None], (gm_groups, gm_chunk, W["gm_w_out"].shape[1] // gm_groups)).astype(F32)
            zzpre = matmul(h, W["gm_w_in"][j], name="mm_gm_in")
            p = gmlp_fwd(zzpre, W["gm_b_in"][j:j + 1], W["gm_v_gain"][j:j + 1], wc, bsf, name="gm_fwd")
            x1 = matmul(p, W["gm_w_out"][j], residual=x, name="mm_gm_out")
            s.update(zzpre=zzpre, p=p, wc=wc, bsf=bsf)
        else:
            w_in = W["ssm_w_in"][j]
            w_zx = w_in[:, :inner + conv_dim]
            w_dtg = _spread_dt(w_in[:, inner + conv_dim:], G, hpg)
            bias_g = _group_vec(W["ssm_dt_bias"][j], G, hpg)
            alog_g = _group_vec(W["ssm_a_log"][j], G, hpg)
            d_chan = jnp.repeat(W["ssm_d"][j], SB_HEAD_DIM)[None, :]
            ngain = W["ssm_norm_gain"][j:j + 1]
            zx = matmul(h, w_zx, name="mm_ssm_zx")
            dtg = matmul(h, w_dtg, name="mm_ssm_dt")
            xbc = conv_fwd(zx, W["ssm_conv_w"][j], W["ssm_conv_b"][j:j + 1], inner, name="conv_fwd")
            yn, yfull, hp = ssd_fwd(xbc, zx, dtg, bias_g, alog_g, d_chan, ngain, L, G, name="ssd_fwd")
            x1 = matmul(yn, W["ssm_w_out"][j], residual=x, name="mm_ssm_out")
            s.update(w_zx=w_zx, w_dtg=w_dtg, bias_g=bias_g, alog_g=alog_g, d_chan=d_chan, ngain=ngain,
                     zx=zx, dtg=dtg, xbc=xbc, yn=yn, yfull=yfull, hp=hp)
        h2 = rms_fwd(x1, W["ffn_norm"][i:i + 1], name="rms_ffn_fwd")
        gu = matmul(h2, W["ffn_w_gu"][i], name="mm_ffn_gu")
        a = swiglu_fwd(gu, name="swiglu_fwd")
        x2 = matmul(a, W["ffn_w_down"][i], residual=x1, name="mm_ffn_down")
        s.update(x1=x1, h2=h2, gu=gu, a=a)
        saved.append(s)
        x = x2

    dx, loss = loss_head(x, target, name="loss_head")

    gw = {k: [None] * W[k].shape[0] for k in WEIGHTS}
    for i in reversed(range(depth)):
        kind, j = i % 3, i // 3
        s = saved[i]
        da = matmul(dx, W["ffn_w_down"][i], tb=True, name="mm_ffn_da")
        gw["ffn_w_down"][i] = matmul(s["a"], dx, ta=True, out_dtype=BF16, name="mm_ffn_dwdown")
        dgu = swiglu_bwd(s["gu"], da, name="swiglu_bwd")
        dh2 = matmul(dgu, W["ffn_w_gu"][i], tb=True, name="mm_ffn_dh")
        gw["ffn_w_gu"][i] = matmul(s["h2"], dgu, ta=True, out_dtype=BF16, name="mm_ffn_dwgu")
        dx1, dgn = rms_bwd(s["x1"], W["ffn_norm"][i:i + 1], dh2, dx, name="rms_ffn_bwd")
        gw["ffn_norm"][i] = dgn[0]
        if kind == 0:
            do = matmul(dx1, W["sb_w_o"][j], tb=True, name="mm_sb_do")
            gw["sb_w_o"][j] = matmul(s["o"], dx1, ta=True, out_dtype=BF16, name="mm_sb_dwo")
            dqkv, dqg, dkg = sb_attn_bwd(s["qkv"], s["ox"], do, W["sb_q_gain"][j:j + 1], W["sb_k_gain"][j:j + 1],
                                         name="sb_bwd")
            gw["sb_q_gain"][j] = dqg[0]
            gw["sb_k_gain"][j] = dkg[0]
            dh = matmul(dqkv, W["sb_w_qkv"][j], tb=True, name="mm_sb_dh")
            gw["sb_w_qkv"][j] = matmul(s["h"], dqkv, ta=True, out_dtype=BF16, name="mm_sb_dwqkv")
        elif kind == 1:
            dp = matmul(dx1, W["gm_w_out"][j], tb=True, name="mm_gm_dp")
            gw["gm_w_out"][j] = matmul(s["p"], dx1, ta=True, out_dtype=BF16, name="mm_gm_dwout")
            dzz, db_in, dvg, dws, dbs = gmlp_bwd(s["zzpre"], W["gm_b_in"][j:j + 1], W["gm_v_gain"][j:j + 1],
                                                s["wc"], s["bsf"], dp, name="gm_bwd")
            gw["gm_b_in"][j] = db_in[0]
            gw["gm_v_gain"][j] = dvg[0]
            gw["gm_w_s"][j] = dws
            gw["gm_b_s"][j] = dbs[:, :gm_groups].T
            dh = matmul(dzz, W["gm_w_in"][j], tb=True, name="mm_gm_dh")
            gw["gm_w_in"][j] = matmul(s["h"], dzz, ta=True, out_dtype=BF16, name="mm_gm_dwin")
        else:
            dyn = matmul(dx1, W["ssm_w_out"][j], tb=True, name="mm_ssm_dyn")
            gw["ssm_w_out"][j] = matmul(s["yn"], dx1, ta=True, out_dtype=BF16, name="mm_ssm_dwout")
            dz, dxs, dbm, dcm, ddt, dbias, dalog, dd, dng = ssd_bwd(
                s["xbc"], s["zx"], s["dtg"], s["bias_g"], s["alog_g"], s["d_chan"], s["ngain"], s["yfull"], s["hp"],
                dyn, L, G, name="ssd_bwd")
            dxbc = jnp.concatenate([dxs, dbm, dcm], axis=1)
            dpre, dcw, dcb = conv_bwd(s["zx"], W["ssm_conv_w"][j], W["ssm_conv_b"][j:j + 1], inner, dxbc,
                                      name="conv_bwd")
            dzx = jnp.concatenate([dz, dpre], axis=1)
            dh = matmul(ddt, s["w_dtg"], tb=True, name="mm_ssm_dh_dt")
            dh = matmul(dzx, s["w_zx"], tb=True, residual=dh, name="mm_ssm_dh")
            dw_zx = matmul(s["h"], dzx, ta=True, out_dtype=BF16, name="mm_ssm_dwzx")
            dw_dtg = matmul(s["h"], ddt, ta=True, out_dtype=BF16, name="mm_ssm_dwdt")
            dw_dt = dw_dtg.reshape(D, G, LANES)[:, :, :hpg].reshape(D, heads)
            gw["ssm_w_in"][j] = jnp.concatenate([dw_zx, dw_dt], axis=1)
            gw["ssm_conv_w"][j] = dcw
            gw["ssm_conv_b"][j] = dcb[0]
            gw["ssm_dt_bias"][j] = dbias[:, 0, :hpg].reshape(heads)
            gw["ssm_a_log"][j] = dalog[:, 0, :hpg].reshape(heads)
            gw["ssm_d"][j] = dd[:, 0, :hpg].reshape(heads)
            gw["ssm_norm_gain"][j] = dng[0]
        dx, dgn = rms_bwd(s["x"], W["mix_norm"][i:i + 1], dh, dx1, name="rms_mix_bwd")
        gw["mix_norm"][i] = dgn[0]

    grads = {k: jnp.stack(v, axis=0) for k, v in gw.items()}
    return loss, dx, grads


MESH = pl.DeviceIdType.MESH
HBM_SPEC = pl.BlockSpec(memory_space=pltpu.HBM)
VMEM_SPEC = pl.BlockSpec(memory_space=pltpu.VMEM)


def _my_position():
    return lax.axis_index("x"), lax.axis_index("y"), lax.axis_index("c")


def _flip(v, bit):
    return 1 - v if bit else v


def all_gather_packed(shard, *, name):
    R, C = shard.shape

    def body(x_ref, out_ref, send_sems, recv_sems, local_sem):
        x, y, c = _my_position()
        me, sibling = (x, y, c), (x, y, 1 - c)
        chips = [(1 - x, y), (x, 1 - y), (1 - x, 1 - y)]

        def slot(px, py, pc):
            return out_ref.at[4 * px + 2 * py + pc]

        def copy(k, block, to, src=None):
            return pltpu.make_async_remote_copy(
                src_ref=slot(*block) if src is None else src, dst_ref=slot(*block),
                send_sem=send_sems.at[k], recv_sem=recv_sems.at[k], device_id=to, device_id_type=MESH)

        mine = pltpu.make_async_copy(x_ref, slot(*me), local_sem)
        mine.start()
        first = [copy(0, me, sibling, src=x_ref)]
        first += [copy(1 + j, me, (*chip, c), src=x_ref) for j, chip in enumerate(chips)]
        for cp in first:
            cp.start()
        passed = [copy(4 + j, (*chip, c), sibling) for j, chip in enumerate(chips)]
        for j, chip in enumerate(chips):
            copy(1 + j, (*chip, c), me).wait_recv()
            passed[j].start()
        copy(0, sibling, me).wait_recv()
        for j, chip in enumerate(chips):
            copy(4 + j, (*chip, 1 - c), me).wait_recv()
        for cp in first + passed:
            cp.wait_send()
        mine.wait()

    return pl.pallas_call(
        body, name=name, out_shape=jax.ShapeDtypeStruct((N_DEV, R, C), shard.dtype),
        in_specs=[HBM_SPEC], out_specs=HBM_SPEC,
        scratch_shapes=[pltpu.SemaphoreType.DMA((7,)), pltpu.SemaphoreType.DMA((7,)), pltpu.SemaphoreType.DMA],
    )(shard)


def exchange_for_reduce_scatter(g, *, name):
    _, R, C = g.shape

    def body(g_ref, out_ref, send_sems, recv_sems, local_sem):
        x, y, c = _my_position()
        me = 4 * x + 2 * y + c
        mine = pltpu.make_async_copy(g_ref.at[me], out_ref.at[me], local_sem)
        mine.start()
        copies = []
        for k in range(1, N_DEV):
            px, py, pc = _flip(x, k & 4), _flip(y, k & 2), _flip(c, k & 1)
            copies.append(pltpu.make_async_remote_copy(
                src_ref=g_ref.at[4 * px + 2 * py + pc], dst_ref=out_ref.at[me],
                send_sem=send_sems.at[k - 1], recv_sem=recv_sems.at[k - 1], device_id=(px, py, pc), device_id_type=MESH))
        for cp in copies:
            cp.start()
        for cp in copies:
            cp.wait()
        mine.wait()

    return pl.pallas_call(
        body, name=name, out_shape=jax.ShapeDtypeStruct(g.shape, g.dtype), in_specs=[HBM_SPEC], out_specs=HBM_SPEC,
        scratch_shapes=[pltpu.SemaphoreType.DMA((7,)), pltpu.SemaphoreType.DMA((7,)), pltpu.SemaphoreType.DMA],
    )(g)


def sum_slots(recv, *, name):
    n, R, C = recv.shape
    tr = _pick(R, (512, 256, 128))

    def body(r_ref, o_ref):
        acc = r_ref[0].astype(F32)
        for s in range(1, n):
            acc = acc + r_ref[s].astype(F32)
        o_ref[...] = acc

    return pl.pallas_call(
        body, name=name, grid=(R // tr,), in_specs=[pl.BlockSpec((n, tr, C), lambda i: (0, i, 0))],
        out_specs=pl.BlockSpec((tr, C), lambda i: (i, 0)), out_shape=jax.ShapeDtypeStruct((R, C), F32),
        compiler_params=_params(("parallel",)),
    )(recv)


def all_reduce_small(v, *, name):
    R, C = v.shape

    def body(v_ref, o_ref, buf, send_sems, recv_sems):
        x, y, c = _my_position()
        me = 4 * x + 2 * y + c
        buf[me] = v_ref[...]
        copies = []
        for k in range(1, N_DEV):
            px, py, pc = _flip(x, k & 4), _flip(y, k & 2), _flip(c, k & 1)
            copies.append(pltpu.make_async_remote_copy(
                src_ref=v_ref, dst_ref=buf.at[me], send_sem=send_sems.at[k - 1], recv_sem=recv_sems.at[k - 1],
                device_id=(px, py, pc), device_id_type=MESH))
        for cp in copies:
            cp.start()
        for cp in copies:
            cp.wait()
        acc = buf[0]
        for s in range(1, N_DEV):
            acc = acc + buf[s]
        o_ref[...] = acc

    return pl.pallas_call(
        body, name=name, out_shape=jax.ShapeDtypeStruct((R, C), F32), in_specs=[VMEM_SPEC], out_specs=VMEM_SPEC,
        scratch_shapes=[pltpu.VMEM((N_DEV, R, C), F32), pltpu.SemaphoreType.DMA((7,)), pltpu.SemaphoreType.DMA((7,))],
        compiler_params=pltpu.CompilerParams(vmem_limit_bytes=VMEM_LIMIT_BYTES),
    )(v)


def _pad_rows(a, mult):
    pad = (-a.shape[0]) % mult
    return jnp.pad(a, ((0, pad), (0, 0))) if pad else a


def _pack_rows(arrays, cols, row_mult):
    return _pad_rows(jnp.concatenate([a.reshape(-1, cols) for a in arrays], axis=0), row_mult)


def _unpack_rows(packed, shapes):
    out, r = [], 0
    cols = packed.shape[-1]
    for shp in shapes:
        n = math.prod(shp) // cols
        out.append(packed[..., r:r + n, :].reshape(packed.shape[:-2] + tuple(shp)))
        r += n
    return out


def _gathered_to_full(g, name):
    if name in COL_SHARDED or name == "ssm_conv_w":
        return jnp.moveaxis(g, 0, -2).reshape(g.shape[1:-1] + (N_DEV * g.shape[-1],))
    if name in ("ssm_conv_b", "ssm_norm_gain"):
        return jnp.moveaxis(g, 0, -2).reshape(g.shape[1:-1] + (N_DEV * g.shape[-1],))
    return jnp.moveaxis(g, 0, 1).reshape((g.shape[1], N_DEV * g.shape[2], g.shape[3]))


def _full_to_shards(full, name):
    if name in COL_SHARDED:
        n = full.shape[-1] // N_DEV
        return jnp.moveaxis(full.reshape(full.shape[:-1] + (N_DEV, n)), -2, 0)
    k = full.shape[1] // N_DEV
    return jnp.moveaxis(full.reshape((full.shape[0], N_DEV, k, full.shape[2])), 1, 0)


def _pack_small(arrays):
    flat = []
    for a in arrays:
        f = a.reshape(-1).astype(F32)
        flat.append(jnp.pad(f, (0, (-f.shape[0]) % LANES)))
    return _pad_rows(jnp.concatenate(flat).reshape(-1, LANES), 8)


def _unpack_small(packed, shapes):
    flat = packed.reshape(-1)
    out, r = [], 0
    for shp in shapes:
        n = math.prod(shp)
        out.append(flat[r:r + n].reshape(shp))
        r += n + (-n) % LANES
    return out


ARG_NAMES = ("x",) + WEIGHTS + ("loss_target",) + tuple("m_" + w for w in WEIGHTS) + tuple("v_" + w for w in WEIGHTS)


def kernel(x, mix_norm, ffn_norm, sb_w_qkv, sb_q_gain, sb_k_gain, sb_w_o, gm_w_in, gm_b_in, gm_v_gain, gm_w_s, gm_b_s, gm_w_out, ssm_w_in, ssm_conv_w, ssm_conv_b, ssm_dt_bias, ssm_a_log, ssm_d, ssm_norm_gain, ssm_w_out, ffn_w_gu, ffn_w_down, loss_target, m_mix_norm, m_ffn_norm, m_sb_w_qkv, m_sb_q_gain, m_sb_k_gain, m_sb_w_o, m_gm_w_in, m_gm_b_in, m_gm_v_gain, m_gm_w_s, m_gm_b_s, m_gm_w_out, m_ssm_w_in, m_ssm_conv_w, m_ssm_conv_b, m_ssm_dt_bias, m_ssm_a_log, m_ssm_d, m_ssm_norm_gain, m_ssm_w_out, m_ffn_w_gu, m_ffn_w_down, v_mix_norm, v_ffn_norm, v_sb_w_qkv, v_sb_q_gain, v_sb_k_gain, v_sb_w_o, v_gm_w_in, v_gm_b_in, v_gm_v_gain, v_gm_w_s, v_gm_b_s, v_gm_w_out, v_ssm_w_in, v_ssm_conv_w, v_ssm_conv_b, v_ssm_dt_bias, v_ssm_a_log, v_ssm_d, v_ssm_norm_gain, v_ssm_w_out, v_ffn_w_gu, v_ffn_w_down):
    given = dict(zip(ARG_NAMES, (x, mix_norm, ffn_norm, sb_w_qkv, sb_q_gain, sb_k_gain, sb_w_o, gm_w_in, gm_b_in, gm_v_gain, gm_w_s, gm_b_s, gm_w_out, ssm_w_in, ssm_conv_w, ssm_conv_b, ssm_dt_bias, ssm_a_log, ssm_d, ssm_norm_gain, ssm_w_out, ffn_w_gu, ffn_w_down, loss_target, m_mix_norm, m_ffn_norm, m_sb_w_qkv, m_sb_q_gain, m_sb_k_gain, m_sb_w_o, m_gm_w_in, m_gm_b_in, m_gm_v_gain, m_gm_w_s, m_gm_b_s, m_gm_w_out, m_ssm_w_in, m_ssm_conv_w, m_ssm_conv_b, m_ssm_dt_bias, m_ssm_a_log, m_ssm_d, m_ssm_norm_gain, m_ssm_w_out, m_ffn_w_gu, m_ffn_w_down, v_mix_norm, v_ffn_norm, v_sb_w_qkv, v_sb_q_gain, v_sb_k_gain, v_sb_w_o, v_gm_w_in, v_gm_b_in, v_gm_v_gain, v_gm_w_s, v_gm_b_s, v_gm_w_out, v_ssm_w_in, v_ssm_conv_w, v_ssm_conv_b, v_ssm_dt_bias, v_ssm_a_log, v_ssm_d, v_ssm_norm_gain, v_ssm_w_out, v_ffn_w_gu, v_ffn_w_down)))
    mx, my, mc = _my_position()
    me = 4 * mx + 2 * my + mc

    big_shapes = [given[k].shape for k in BIG]
    sharded_small = [lax.bitcast_convert_type(given[k], BF16) for k in SMALL_SHARDED]
    sharded_small_shapes = [a.shape for a in sharded_small]
    tail = _pad_rows(jnp.concatenate([a.reshape(-1) for a in sharded_small]).reshape(-1, LANES), 1)
    tail = jnp.pad(tail.reshape(-1), (0, (-tail.size) % PACK_COLS)).reshape(-1, PACK_COLS)
    tail_rows = tail.shape[0]
    pack = _pad_rows(jnp.concatenate([given[k].astype(BF16).reshape(-1, PACK_COLS) for k in BIG] + [tail], axis=0), 512)
    gathered = all_gather_packed(pack, name="all_gather_weights")
    W = {k: given[k] for k in SMALL if k not in SMALL_SHARDED}
    parts = _unpack_rows(gathered, big_shapes)
    for k, g in zip(BIG, parts):
        W[k] = _gathered_to_full(g, k)
    r0 = sum(math.prod(s) for s in big_shapes) // PACK_COLS
    tail_g = gathered[:, r0:r0 + tail_rows, :].reshape(N_DEV, -1)
    off = 0
    for k, shp in zip(SMALL_SHARDED, sharded_small_shapes):
        n = math.prod(shp)
        g = lax.bitcast_convert_type(tail_g[:, off:off + n].reshape((N_DEV,) + shp), F32)
        W[k] = _gathered_to_full(g, k)
        off += n

    loss, gx, grads = local_step(given["x"][0], given["loss_target"][0], W)

    contrib = jnp.concatenate([_full_to_shards(grads[k], k).reshape(N_DEV, -1, PACK_COLS) for k in BIG], axis=1)
    contrib = jnp.pad(contrib, ((0, 0), (0, pack.shape[0] - contrib.shape[1]), (0, 0)))
    received = exchange_for_reduce_scatter(contrib, name="reduce_scatter_exchange")
    gbig = sum_slots(received, name="reduce_scatter_sum")

    small_shapes = [grads[k].shape for k in SMALL] + [(1, 1)]
    reduced = all_reduce_small(_pack_small([grads[k] for k in SMALL] + [loss]), name="all_reduce_small")
    small_full = dict(zip(SMALL + ("loss",), _unpack_small(reduced, small_shapes)))

    wbig = _pack_rows([given[k] for k in BIG], PACK_COLS, 512)
    mbig = _pack_rows([given["m_" + k] for k in BIG], PACK_COLS, 512)
    vbig = _pack_rows([given["v_" + k] for k in BIG], PACK_COLS, 512)
    gbig = gbig[:wbig.shape[0]]
    dbig, nmbig, nvbig = adamw(wbig, gbig, mbig, vbig, name="adamw_big")
    out_g, out_d, out_m, out_v = {}, {}, {}, {}
    for dst, src in ((out_g, gbig), (out_d, dbig), (out_m, nmbig), (out_v, nvbig)):
        dst.update(zip(BIG, _unpack_rows(src, big_shapes)))

    gsmall = {}
    for k in SMALL:
        g = small_full[k]
        if k in SMALL_SHARDED:
            n = given[k].shape[-1]
            g = lax.dynamic_slice_in_dim(g, me * n, n, axis=g.ndim - 1)
        gsmall[k] = g
    local_shapes = [given[k].shape for k in SMALL]
    dsm, nmsm, nvsm = adamw(_pack_small([given[k] for k in SMALL]), _pack_small([gsmall[k] for k in SMALL]),
                            _pack_small([given["m_" + k] for k in SMALL]), _pack_small([given["v_" + k] for k in SMALL]),
                            name="adamw_small")
    out_g.update(gsmall)
    for dst, src in ((out_d, dsm), (out_m, nmsm), (out_v, nvsm)):
        dst.update(zip(SMALL, _unpack_small(src, local_shapes)))

    return (small_full["loss"].reshape(()), gx[None],
            *[out_g[k] for k in WEIGHTS], *[out_d[k] for k in WEIGHTS],
            *[out_m[k] for k in WEIGHTS], *[out_v[k] for k in WEIGHTS])
```

```python
import math

import jax
import jax.numpy as jnp
from jax import lax
from jax.experimental import pallas as pl
from jax.experimental.pallas import tpu as pltpu

F32 = jnp.float32
BF16 = jnp.bfloat16
EPS = 1e-6
N_DEV = 8
SB_HEAD_DIM = 64
SSM_STATE = 128
SSM_CONV = 4
ADAM_LR = 0.001
ADAM_B1 = 0.9
ADAM_B2 = 0.999
ADAM_EPS = 1e-08
ADAM_WD = 0.01
ADAM_STEP = 10
VMEM_LIMIT_BYTES = 56 * 1024 * 1024
LANES = 128
PACK_COLS = 1024

BIG = ("sb_w_qkv", "sb_w_o", "gm_w_in", "gm_w_out", "ssm_w_in", "ssm_w_out", "ffn_w_gu", "ffn_w_down")
COL_SHARDED = ("sb_w_qkv", "gm_w_in", "ssm_w_in", "ffn_w_gu")
SMALL = ("mix_norm", "ffn_norm", "sb_q_gain", "sb_k_gain", "gm_b_in", "gm_v_gain", "gm_w_s", "gm_b_s",
         "ssm_conv_w", "ssm_conv_b", "ssm_dt_bias", "ssm_a_log", "ssm_d", "ssm_norm_gain")
SMALL_SHARDED = ("ssm_conv_w", "ssm_conv_b", "ssm_norm_gain")
WEIGHTS = ("mix_norm", "ffn_norm", "sb_w_qkv", "sb_q_gain", "sb_k_gain", "sb_w_o", "gm_w_in", "gm_b_in",
           "gm_v_gain", "gm_w_s", "gm_b_s", "gm_w_out", "ssm_w_in", "ssm_conv_w", "ssm_conv_b", "ssm_dt_bias",
           "ssm_a_log", "ssm_d", "ssm_norm_gain", "ssm_w_out", "ffn_w_gu", "ffn_w_down")


def _params(semantics=None):
    return pltpu.CompilerParams(dimension_semantics=semantics, vmem_limit_bytes=VMEM_LIMIT_BYTES)


def _pick(n, prefs):
    for t in prefs:
        if t <= n and n % t == 0:
            return t
    return n


def _dot(a, b, ca=1, cb=0):
    return lax.dot_general(a, b, (((ca,), (cb,)), ((), ())), preferred_element_type=F32)


def _split3(v):
    h1 = v.astype(BF16)
    r1 = v - h1.astype(F32)
    h2 = r1.astype(BF16)
    h3 = (r1 - h2.astype(F32)).astype(BF16)
    return h1, h2, h3


def _dot_exact_left(mat01, v):
    h1, h2, h3 = _split3(v)
    return _dot(mat01, h1) + _dot(mat01, h2) + _dot(mat01, h3)


def _dot_split2_right(v, mat01):
    hi = v.astype(BF16)
    lo = (v - hi.astype(F32)).astype(BF16)
    return _dot(hi, mat01) + _dot(lo, mat01)


def _sigmoid(v):
    return 1.0 / (1.0 + jnp.exp(-v))


def _softplus(v):
    return jnp.maximum(v, 0.0) + jnp.log(1.0 + jnp.exp(-jnp.abs(v)))


def _erf(v):
    a = jnp.abs(v)
    t = 1.0 / (1.0 + 0.3275911 * a)
    poly = t * (0.254829592 + t * (-0.284496736 + t * (1.421413741 + t * (-1.453152027 + t * 1.061405429))))
    e = 1.0 - poly * jnp.exp(-a * a)
    return jnp.where(v < 0, -e, e)


def _gelu_and_grad(v):
    cdf = 0.5 * (1.0 + _erf(v * (1.0 / math.sqrt(2.0))))
    pdf = jnp.exp(-0.5 * v * v) * (1.0 / math.sqrt(2.0 * math.pi))
    return v * cdf, cdf + v * pdf


def matmul(a, b, *, ta=False, tb=False, out_dtype=F32, residual=None, name):
    if ta:
        K, M = a.shape
    else:
        M, K = a.shape
    if tb:
        N, Kb = b.shape
    else:
        Kb, N = b.shape
    assert K == Kb, (a.shape, b.shape, ta, tb)
    tm = _pick(M, (1024, 1408, 768, 512, 256, 128))
    tn = _pick(N, (1024, 1408, 1536, 768, 512, 256, 128))
    tk = _pick(K, (512, 1408, 256, 128))
    nk = K // tk
    a_spec = pl.BlockSpec((tk, tm), lambda i, j, k: (k, i)) if ta else pl.BlockSpec((tm, tk), lambda i, j, k: (i, k))
    b_spec = pl.BlockSpec((tn, tk), lambda i, j, k: (j, k)) if tb else pl.BlockSpec((tk, tn), lambda i, j, k: (k, j))
    o_spec = pl.BlockSpec((tm, tn), lambda i, j, k: (i, j))
    ca, cb = (0 if ta else 1), (1 if tb else 0)
    has_res = residual is not None

    def body(*refs):
        if has_res:
            a_ref, b_ref, r_ref, o_ref, acc = refs
        else:
            a_ref, b_ref, o_ref, acc = refs
        k = pl.program_id(2)

        @pl.when(k == 0)
        def _():
            acc[...] = jnp.zeros_like(acc)

        acc[...] += _dot(a_ref[...].astype(BF16), b_ref[...].astype(BF16), ca, cb)

        @pl.when(k == nk - 1)
        def _():
            r = acc[...]
            if has_res:
                r = r + r_ref[...]
            o_ref[...] = r.astype(out_dtype)

    in_specs = [a_spec, b_spec] + ([o_spec] if has_res else [])
    args = (a, b) + ((residual,) if has_res else ())
    return pl.pallas_call(
        body, name=name, grid=(M // tm, N // tn, nk), in_specs=in_specs, out_specs=o_spec,
        out_shape=jax.ShapeDtypeStruct((M, N), out_dtype), scratch_shapes=[pltpu.VMEM((tm, tn), F32)],
        compiler_params=_params(("parallel", "parallel", "arbitrary")),
    )(*args)


def rms_fwd(x, gain, *, name):
    S, D = x.shape
    tr = _pick(S, (512, 256, 128))

    def body(x_ref, g_ref, o_ref):
        xv = x_ref[...]
        r = lax.rsqrt(jnp.mean(xv * xv, axis=1, keepdims=True) + EPS)
        o_ref[...] = (xv * r * g_ref[...]).astype(BF16)

    return pl.pallas_call(
        body, name=name, grid=(S // tr,),
        in_specs=[pl.BlockSpec((tr, D), lambda i: (i, 0)), pl.BlockSpec((1, D), lambda i: (0, 0))],
        out_specs=pl.BlockSpec((tr, D), lambda i: (i, 0)), out_shape=jax.ShapeDtypeStruct((S, D), BF16),
        compiler_params=_params(("parallel",)),
    )(x, gain)


def rms_bwd(x, gain, dh, dres, *, name):
    S, D = x.shape
    tr = _pick(S, (512, 256, 128))

    def body(x_ref, g_ref, dh_ref, dr_ref, dx_ref, dg_ref):
        @pl.when(pl.program_id(0) == 0)
        def _():
            dg_ref[...] = jnp.zeros_like(dg_ref)

        xv = x_ref[...]
        dhv = dh_ref[...]
        r = lax.rsqrt(jnp.mean(xv * xv, axis=1, keepdims=True) + EPS)
        xhat = xv * r
        t = dhv * g_ref[...]
        dx_ref[...] = dr_ref[...] + r * (t - xhat * jnp.mean(xhat * t, axis=1, keepdims=True))
        dg_ref[...] += jnp.sum(dhv * xhat, axis=0, keepdims=True)

    row = pl.BlockSpec((tr, D), lambda i: (i, 0))
    vec = pl.BlockSpec((1, D), lambda i: (0, 0))
    return pl.pallas_call(
        body, name=name, grid=(S // tr,), in_specs=[row, vec, row, row], out_specs=[row, vec],
        out_shape=[jax.ShapeDtypeStruct((S, D), F32), jax.ShapeDtypeStruct((1, D), F32)],
        compiler_params=_params(("arbitrary",)),
    )(x, gain, dh, dres)


def swiglu_fwd(gu, *, name):
    S, F2 = gu.shape
    F = F2 // 2
    tr = _pick(S, (256, 128))

    def body(gu_ref, o_ref):
        g = gu_ref[:, :F]
        u = gu_ref[:, F:]
        o_ref[...] = (g * _sigmoid(g) * u).astype(BF16)

    return pl.pallas_call(
        body, name=name, grid=(S // tr,), in_specs=[pl.BlockSpec((tr, F2), lambda i: (i, 0))],
        out_specs=pl.BlockSpec((tr, F), lambda i: (i, 0)), out_shape=jax.ShapeDtypeStruct((S, F), BF16),
        compiler_params=_params(("parallel",)),
    )(gu)


def swiglu_bwd(gu, da, *, name):
    S, F2 = gu.shape
    F = F2 // 2
    tr = _pick(S, (256, 128))

    def body(gu_ref, da_ref, o_ref):
        g = gu_ref[:, :F]
        u = gu_ref[:, F:]
        dav = da_ref[...]
        s = _sigmoid(g)
        o_ref[:, :F] = (dav * u * (s * (1.0 + g * (1.0 - s)))).astype(BF16)
        o_ref[:, F:] = (dav * g * s).astype(BF16)

    return pl.pallas_call(
        body, name=name, grid=(S // tr,),
        in_specs=[pl.BlockSpec((tr, F2), lambda i: (i, 0)), pl.BlockSpec((tr, F), lambda i: (i, 0))],
        out_specs=pl.BlockSpec((tr, F2), lambda i: (i, 0)), out_shape=jax.ShapeDtypeStruct((S, F2), BF16),
        compiler_params=_params(("parallel",)),
    )(gu, da)


def loss_head(y, target, *, name):
    S, D = y.shape
    tr = _pick(S, (512, 256, 128))

    def body(y_ref, t_ref, dy_ref, l_ref):
        @pl.when(pl.program_id(0) == 0)
        def _():
            l_ref[...] = jnp.zeros_like(l_ref)

        err = y_ref[...] - t_ref[...]
        dy_ref[...] = err * (1.0 / D)
        l_ref[...] += jnp.sum(0.5 * jnp.mean(err * err, axis=1, keepdims=True), axis=0, keepdims=True)

    row = pl.BlockSpec((tr, D), lambda i: (i, 0))
    one = pl.BlockSpec((1, 1), lambda i: (0, 0))
    dy, l = pl.pallas_call(
        body, name=name, grid=(S // tr,), in_specs=[row, row], out_specs=[row, one],
        out_shape=[jax.ShapeDtypeStruct((S, D), F32), jax.ShapeDtypeStruct((1, 1), F32)],
        compiler_params=_params(("arbitrary",)),
    )(y, target)
    return dy, l


def adamw(w, g, m, v, *, name):
    R, C = w.shape
    tr = _pick(R, (512, 256, 128, 64, 32, 16, 8))

    def body(w_ref, g_ref, m_ref, v_ref, d_ref, mo_ref, vo_ref):
        gv = g_ref[...]
        mn = ADAM_B1 * m_ref[...] + (1.0 - ADAM_B1) * gv
        vn = ADAM_B2 * v_ref[...] + (1.0 - ADAM_B2) * jnp.square(gv)
        m_hat = mn / (1.0 - ADAM_B1 ** ADAM_STEP)
        v_hat = vn / (1.0 - ADAM_B2 ** ADAM_STEP)
        d_ref[...] = -ADAM_LR * (m_hat / (jnp.sqrt(v_hat) + ADAM_EPS) + ADAM_WD * w_ref[...])
        mo_ref[...] = mn
        vo_ref[...] = vn

    blk = pl.BlockSpec((tr, C), lambda i: (i, 0))
    sds = jax.ShapeDtypeStruct((R, C), F32)
    return pl.pallas_call(
        body, name=name, grid=(R // tr,), in_specs=[blk] * 4, out_specs=[blk] * 3, out_shape=[sds] * 3,
        compiler_params=_params(("parallel",)),
    )(w, g, m, v)


def _tri(n, kind):
    r = lax.broadcasted_iota(jnp.int32, (n, n), 0)
    c = lax.broadcasted_iota(jnp.int32, (n, n), 1)
    if kind == "row_gt_col":
        return (r > c).astype(BF16)
    if kind == "row_ge_col":
        return (r >= c).astype(BF16)
    if kind == "row_le_col":
        return (r <= c).astype(BF16)
    raise ValueError(kind)


def _sb_tile(qi, kj, scale, r_carry, u_strict, masked):
    z = _dot(qi, kj, 1, 1) * scale
    e = jnp.exp(-jnp.abs(z))
    lb = jnp.minimum(z, 0.0) - jnp.log(1.0 + e)
    l1m = lb - z
    keep = None
    if masked:
        tq, tk = z.shape
        keep = lax.broadcasted_iota(jnp.int32, (tq, tk), 1) < lax.broadcasted_iota(jnp.int32, (tq, tk), 0)
        l1m = jnp.where(keep, l1m, 0.0)
    w = jnp.exp(lb + _dot(l1m.astype(BF16), u_strict) + r_carry)
    if masked:
        w = jnp.where(keep, w, 0.0)
    return z, e, l1m, w, keep


def _sb_prep(T, nb, hd, refs_in, gains, scratch):
    def prep(i, _):
        rows = pl.ds(pl.multiple_of(i * T, T), T)
        for hh in range(2):
            sl = slice(hd * hh, hd * hh + hd)
            for n, (src, dst) in enumerate(zip(refs_in, scratch)):
                v = src[rows, sl]
                if n < 2:
                    v = v * lax.rsqrt(jnp.mean(v * v, axis=1, keepdims=True) + EPS) * gains[n][...]
                dst[hh, rows, :] = v.astype(BF16)
        return 0

    lax.fori_loop(0, nb, prep, 0)


def sb_attn_fwd(qkv, q_gain, k_gain, *, name):
    S, D3 = qkv.shape
    D = D3 // 3
    npairs = D // LANES
    hd = SB_HEAD_DIM
    T = min(256, S)
    nb = S // T
    scale = 1.0 / math.sqrt(hd)

    def body(q_ref, k_ref, v_ref, qg_ref, kg_ref, us_ref, o_ref, qn_s, kn_s, vb_s):
        us = us_ref[...]
        _sb_prep(T, nb, hd, (q_ref, k_ref, v_ref), (qg_ref, kg_ref), (qn_s, kn_s, vb_s))

        def qblock(i, _):
            rows_i = pl.ds(pl.multiple_of(i * T, T), T)
            qi = [qn_s[hh, rows_i, :] for hh in range(2)]

            def tile(j, carry, masked):
                rows_j = pl.ds(pl.multiple_of(j * T, T), T)
                out = []
                for hh in range(2):
                    acc, rc = carry[hh]
                    _, _, l1m, w, _ = _sb_tile(qi[hh], kn_s[hh, rows_j, :], scale, rc, us, masked)
                    acc = acc + _dot(w.astype(BF16), vb_s[hh, rows_j, :])
                    rc = rc + jnp.sum(l1m, axis=1, keepdims=True)
                    out.append((acc, rc))
                return tuple(out)

            init = (jnp.zeros((T, hd), F32), jnp.zeros((T, 1), F32))
            carry = tile(i, (init, init), True)
            carry = lax.fori_loop(0, i, lambda jj, c: tile(i - 1 - jj, c, False), carry)
            o_ref[rows_i, :] = jnp.concatenate([carry[0][0], carry[1][0]], axis=1)
            return 0

        lax.fori_loop(0, nb, qblock, 0)

    col = lambda off: pl.BlockSpec((S, LANES), lambda p, off=off: (0, off + p))
    gain = pl.BlockSpec((1, hd), lambda p: (0, 0))
    return pl.pallas_call(
        body, name=name, grid=(npairs,),
        in_specs=[col(0), col(npairs), col(2 * npairs), gain, gain, pl.BlockSpec((T, T), lambda p: (0, 0))],
        out_specs=pl.BlockSpec((S, LANES), lambda p: (0, p)), out_shape=jax.ShapeDtypeStruct((S, D), F32),
        scratch_shapes=[pltpu.VMEM((2, S, hd), BF16)] * 3,
        compiler_params=_params(("parallel",)),
    )(qkv, qkv, qkv, q_gain, k_gain, _tri(T, "row_gt_col"))


def sb_attn_bwd(qkv, o, do, q_gain, k_gain, *, name):
    S, D3 = qkv.shape
    D = D3 // 3
    npairs = D // LANES
    hd = SB_HEAD_DIM
    T = min(256, S)
    nb = S // T
    scale = 1.0 / math.sqrt(hd)

    def body(q_ref, k_ref, v_ref, o_ref, do_ref, qg_ref, kg_ref, us_ref, ui_ref,
             dq_ref, dk_ref, dv_ref, dg_ref, qn_s, kn_s, vb_s, dob_s, dkn_s, dv_s):
        @pl.when(pl.program_id(0) == 0)
        def _():
            dg_ref[...] = jnp.zeros_like(dg_ref)

        us = us_ref[...]
        ui = ui_ref[...]
        _sb_prep(T, nb, hd, (q_ref, k_ref, v_ref, do_ref), (qg_ref, kg_ref), (qn_s, kn_s, vb_s, dob_s))
        dkn_s[...] = jnp.zeros_like(dkn_s)
        dv_s[...] = jnp.zeros_like(dv_s)

        def qblock(i, _):
            rows_i = pl.ds(pl.multiple_of(i * T, T), T)
            qi = [qn_s[hh, rows_i, :] for hh in range(2)]
            doi = [dob_s[hh, rows_i, :] for hh in range(2)]
            dt_total = [jnp.sum(doi[hh].astype(F32) * o_ref[rows_i, hd * hh:hd * hh + hd], axis=1, keepdims=True)
                        for hh in range(2)]

            def tile(j, carry, masked):
                rows_j = pl.ds(pl.multiple_of(j * T, T), T)
                out = []
                for hh in range(2):
                    dq_acc, rc, gc = carry[hh]
                    kj = kn_s[hh, rows_j, :]
                    z, e, l1m, w, keep = _sb_tile(qi[hh], kj, scale, rc, us, masked)
                    wb = w.astype(BF16)
                    g = _dot(doi[hh], vb_s[hh, rows_j, :], 1, 1) * wb.astype(F32)
                    dv_s[hh, rows_j, :] += _dot(wb, doi[hh], 0, 0)
                    g_prefix = dt_total[hh] - (gc + _dot_split2_right(g, ui))
                    inv = 1.0 / (1.0 + e)
                    pos = z >= 0
                    sig = jnp.where(pos, inv, e * inv)
                    sig_neg = jnp.where(pos, e * inv, inv)
                    dz = g * sig_neg - g_prefix * sig
                    if masked:
                        dz = jnp.where(keep, dz, 0.0)
                    dzb = (dz * scale).astype(BF16)
                    dq_acc = dq_acc + _dot(dzb, kj)
                    dkn_s[hh, rows_j, :] += _dot(dzb, qi[hh], 0, 0)
                    rc = rc + jnp.sum(l1m, axis=1, keepdims=True)
                    gc = gc + jnp.sum(g, axis=1, keepdims=True)
                    out.append((dq_acc, rc, gc))
                return tuple(out)

            zero1 = jnp.zeros((T, 1), F32)
            init = (jnp.zeros((T, hd), F32), zero1, zero1)
            carry = tile(i, (init, init), True)
            carry = lax.fori_loop(0, i, lambda jj, c: tile(i - 1 - jj, c, False), carry)
            dq_ref[rows_i, :] = jnp.concatenate([carry[0][0], carry[1][0]], axis=1)
            return 0

        lax.fori_loop(0, nb, qblock, 0)

        def finish(i, carry):
            rows = pl.ds(pl.multiple_of(i * T, T), T)
            new = []
            for hh in range(2):
                sl = slice(hd * hh, hd * hh + hd)
                outs = []
                for raw_ref, gain_ref, dn in ((q_ref, qg_ref, dq_ref[rows, sl]), (k_ref, kg_ref, dkn_s[hh, rows, :])):
                    raw = raw_ref[rows, sl]
                    r = lax.rsqrt(jnp.mean(raw * raw, axis=1, keepdims=True) + EPS)
                    hat = raw * r
                    t = dn * gain_ref[...]
                    outs.append((r * (t - hat * jnp.mean(hat * t, axis=1, keepdims=True)),
                                 jnp.sum(dn * hat, axis=0, keepdims=True)))
                dq_ref[rows, sl] = outs[0][0]
                dk_ref[rows, sl] = outs[1][0]
                dv_ref[rows, sl] = dv_s[hh, rows, :]
                new.append((carry[hh][0] + outs[0][1], carry[hh][1] + outs[1][1]))
            return tuple(new)

        zg = (jnp.zeros((1, hd), F32), jnp.zeros((1, hd), F32))
        tot = lax.fori_loop(0, nb, finish, (zg, zg))
        dg_ref[0:1, 0:hd] += tot[0][0] + tot[1][0]
        dg_ref[1:2, 0:hd] += tot[0][1] + tot[1][1]

    col = lambda off: pl.BlockSpec((S, LANES), lambda p, off=off: (0, off + p))
    gain = pl.BlockSpec((1, hd), lambda p: (0, 0))
    tri = pl.BlockSpec((T, T), lambda p: (0, 0))
    pair = pl.BlockSpec((S, LANES), lambda p: (0, p))
    dq, dk, dv, dg = pl.pallas_call(
        body, name=name, grid=(npairs,),
        in_specs=[col(0), col(npairs), col(2 * npairs), pair, pair, gain, gain, tri, tri],
        out_specs=[pair, pair, pair, pl.BlockSpec((8, LANES), lambda p: (0, 0))],
        out_shape=[jax.ShapeDtypeStruct((S, D), F32)] * 3 + [jax.ShapeDtypeStruct((8, LANES), F32)],
        scratch_shapes=[pltpu.VMEM((2, S, hd), BF16)] * 4 + [pltpu.VMEM((2, S, hd), F32)] * 2,
        compiler_params=_params(("arbitrary",)),
    )(qkv, qkv, qkv, o, do, q_gain, k_gain, _tri(T, "row_gt_col"), _tri(T, "row_ge_col"))
    return jnp.concatenate([dq, dk, dv], axis=1), dg[0:1, :hd], dg[1:2, :hd]


def gmlp_fwd(zzpre, b_in, v_gain, wc, bsf, *, name):
    S, H2 = zzpre.shape
    H = H2 // 2
    G, T, _ = wc.shape
    gd = H // G

    def body(z_ref, b_ref, vg_ref, wc_ref, bs_ref, p_ref):
        zz, _ = _gelu_and_grad(z_ref[...] + b_ref[...])
        u = zz[:, :H]
        v = zz[:, H:]
        vn = v * lax.rsqrt(jnp.mean(v * v, axis=1, keepdims=True) + EPS) * vg_ref[...]
        for g in range(G):
            gs = slice(g * gd, (g + 1) * gd)
            mixed = _dot(wc_ref[g], vn[:, gs].astype(BF16)) + bs_ref[g]
            p_ref[:, gs] = (u[:, gs] * mixed).astype(BF16)

    full3 = lambda shp: pl.BlockSpec(shp, lambda c: (0, 0, 0))
    return pl.pallas_call(
        body, name=name, grid=(S // T,),
        in_specs=[pl.BlockSpec((T, H2), lambda c: (c, 0)), pl.BlockSpec((1, H2), lambda c: (0, 0)),
                  pl.BlockSpec((1, H), lambda c: (0, 0)), full3((G, T, T)), full3((G, T, gd))],
        out_specs=pl.BlockSpec((T, H), lambda c: (c, 0)), out_shape=jax.ShapeDtypeStruct((S, H), BF16),
        compiler_params=_params(("parallel",)),
    )(zzpre, b_in, v_gain, wc, bsf)


def gmlp_bwd(zzpre, b_in, v_gain, wc, bsf, dp, *, name):
    S, H2 = zzpre.shape
    H = H2 // 2
    G, T, _ = wc.shape
    gd = H // G
    assert G <= LANES

    def body(z_ref, b_ref, vg_ref, wc_ref, bs_ref, dp_ref, dzz_ref, db_ref, dvg_ref, dws_ref, dbs_ref):
        @pl.when(pl.program_id(0) == 0)
        def _():
            db_ref[...] = jnp.zeros_like(db_ref)
            dvg_ref[...] = jnp.zeros_like(dvg_ref)
            dws_ref[...] = jnp.zeros_like(dws_ref)
            dbs_ref[...] = jnp.zeros_like(dbs_ref)

        zz, gp = _gelu_and_grad(z_ref[...] + b_ref[...])
        u = zz[:, :H]
        v = zz[:, H:]
        r = lax.rsqrt(jnp.mean(v * v, axis=1, keepdims=True) + EPS)
        vhat = v * r
        vg = vg_ref[...]
        vn = vhat * vg
        dpv = dp_ref[...]
        tril = lax.broadcasted_iota(jnp.int32, (T, T), 1) <= lax.broadcasted_iota(jnp.int32, (T, T), 0)
        lane = lax.broadcasted_iota(jnp.int32, (T, LANES), 1)
        dbs = jnp.zeros((T, LANES), F32)
        du_parts, dvn_parts = [], []
        for g in range(G):
            gs = slice(g * gd, (g + 1) * gd)
            vng = vn[:, gs].astype(BF16)
            wcg = wc_ref[g]
            mixed = _dot(wcg, vng) + bs_ref[g]
            dpg = dpv[:, gs]
            du_parts.append(dpg * mixed)
            dmx = dpg * u[:, gs]
            dmxb = dmx.astype(BF16)
            dvn_parts.append(_dot(wcg, dmxb, 0, 0))
            dws_ref[g] += jnp.where(tril, _dot(dmxb, vng, 1, 1), 0.0)
            dbs = dbs + jnp.where(lane == g, jnp.sum(dmx, axis=1, keepdims=True), 0.0)
        dbs_ref[...] += dbs
        du = jnp.concatenate(du_parts, axis=1)
        dvn = jnp.concatenate(dvn_parts, axis=1)
        dvg_ref[...] += jnp.sum(dvn * vhat, axis=0, keepdims=True)
        t = dvn * vg
        dv = r * (t - vhat * jnp.mean(vhat * t, axis=1, keepdims=True))
        dzu = du * gp[:, :H]
        dzv = dv * gp[:, H:]
        dzz_ref[:, :H] = dzu.astype(BF16)
        dzz_ref[:, H:] = dzv.astype(BF16)
        db_ref[:, :H] += jnp.sum(dzu, axis=0, keepdims=True)
        db_ref[:, H:] += jnp.sum(dzv, axis=0, keepdims=True)

    full3 = lambda shp: pl.BlockSpec(shp, lambda c: (0, 0, 0))
    vec = lambda n: pl.BlockSpec((1, n), lambda c: (0, 0))
    return pl.pallas_call(
        body, name=name, grid=(S // T,),
        in_specs=[pl.BlockSpec((T, H2), lambda c: (c, 0)), vec(H2), vec(H), full3((G, T, T)), full3((G, T, gd)),
                  pl.BlockSpec((T, H), lambda c: (c, 0))],
        out_specs=[pl.BlockSpec((T, H2), lambda c: (c, 0)), vec(H2), vec(H), full3((G, T, T)),
                   pl.BlockSpec((T, LANES), lambda c: (0, 0))],
        out_shape=[jax.ShapeDtypeStruct((S, H2), BF16), jax.ShapeDtypeStruct((1, H2), F32),
                   jax.ShapeDtypeStruct((1, H), F32), jax.ShapeDtypeStruct((G, T, T), F32),
                   jax.ShapeDtypeStruct((T, LANES), F32)],
        compiler_params=_params(("arbitrary",)),
    )(zzpre, b_in, v_gain, wc, bsf, dp)


def _shift_rows(v, k, n_rows):
    if k == 0:
        return v
    rolled = pltpu.roll(v, k % n_rows, 0)
    row = lax.broadcasted_iota(jnp.int32, v.shape, 0)
    keep = (row >= k) if k > 0 else (row < n_rows + k)
    return jnp.where(keep, rolled, 0.0)


def conv_fwd(zx, conv_w, conv_b, col0, *, name):
    S = zx.shape[0]
    C = conv_w.shape[1]
    tc = _pick(C, (256, 128))
    off = col0 // tc
    assert col0 % tc == 0

    def body(x_ref, w_ref, b_ref, o_ref):
        xv = x_ref[...]
        acc = b_ref[...] + w_ref[SSM_CONV - 1:SSM_CONV, :] * xv
        for k in range(SSM_CONV - 1):
            acc = acc + w_ref[k:k + 1, :] * _shift_rows(xv, SSM_CONV - 1 - k, S)
        o_ref[...] = acc * _sigmoid(acc)

    return pl.pallas_call(
        body, name=name, grid=(C // tc,),
        in_specs=[pl.BlockSpec((S, tc), lambda j: (0, off + j)), pl.BlockSpec((SSM_CONV, tc), lambda j: (0, j)),
                  pl.BlockSpec((1, tc), lambda j: (0, j))],
        out_specs=pl.BlockSpec((S, tc), lambda j: (0, j)), out_shape=jax.ShapeDtypeStruct((S, C), F32),
        compiler_params=_params(("parallel",)),
    )(zx, conv_w, conv_b)


def conv_bwd(zx, conv_w, conv_b, col0, dout, *, name):
    S = zx.shape[0]
    C = conv_w.shape[1]
    tc = _pick(C, (256, 128))
    off = col0 // tc

    def body(x_ref, w_ref, b_ref, do_ref, dx_ref, dw_ref, db_ref):
        xv = x_ref[...]
        shifted = [_shift_rows(xv, SSM_CONV - 1 - k, S) for k in range(SSM_CONV)]
        acc = b_ref[...]
        for k in range(SSM_CONV):
            acc = acc + w_ref[k:k + 1, :] * shifted[k]
        s = _sigmoid(acc)
        dacc = do_ref[...] * (s * (1.0 + acc * (1.0 - s)))
        db_ref[...] = jnp.sum(dacc, axis=0, keepdims=True)
        dx = jnp.zeros_like(xv)
        for k in range(SSM_CONV):
            dw_ref[k:k + 1, :] = jnp.sum(dacc * shifted[k], axis=0, keepdims=True)
            dx = dx + w_ref[k:k + 1, :] * _shift_rows(dacc, -(SSM_CONV - 1 - k), S)
        dx_ref[...] = dx

    slab = pl.BlockSpec((S, tc), lambda j: (0, j))
    return pl.pallas_call(
        body, name=name, grid=(C // tc,),
        in_specs=[pl.BlockSpec((S, tc), lambda j: (0, off + j)), pl.BlockSpec((SSM_CONV, tc), lambda j: (0, j)),
                  pl.BlockSpec((1, tc), lambda j: (0, j)), slab],
        out_specs=[slab, pl.BlockSpec((SSM_CONV, tc), lambda j: (0, j)), pl.BlockSpec((1, tc), lambda j: (0, j))],
        out_shape=[jax.ShapeDtypeStruct((S, C), F32), jax.ShapeDtypeStruct((SSM_CONV, C), F32),
                   jax.ShapeDtypeStruct((1, C), F32)],
        compiler_params=_params(("parallel",)),
    )(zx, conv_w, conv_b, dout)


def _ssd_chunk_terms(dtraw, bias, a_log, tl):
    dt = _softplus(dtraw + bias)
    a_neg = -jnp.exp(a_log)
    ac = _dot_exact_left(tl, dt * a_neg)
    ac_last = ac[ac.shape[0] - 1:, :]
    return dt, a_neg, ac, ac.T, jnp.exp(ac), jnp.exp(ac_last - ac), jnp.exp(ac_last)


def _ssd_specs(S, L, G, hpg, pd, inner):
    gw = hpg * pd
    n = SSM_STATE
    xb = inner // n

    def mk(cidx):
        return dict(
            x=pl.BlockSpec((L, gw), lambda g, c: (cidx(c), g)),
            b=pl.BlockSpec((L, n), lambda g, c: (cidx(c), xb + g)),
            c=pl.BlockSpec((L, n), lambda g, c: (cidx(c), xb + G + g)),
            z=pl.BlockSpec((L, gw), lambda g, c: (cidx(c), g)),
            dt=pl.BlockSpec((L, LANES), lambda g, c: (cidx(c), g)),
            gvec=pl.BlockSpec((1, 1, LANES), lambda g, c: (g, 0, 0)),
            chan=pl.BlockSpec((1, gw), lambda g, c: (0, g)),
            tri=pl.BlockSpec((L, L), lambda g, c: (0, 0)),
            hp=pl.BlockSpec((1, 1, gw, n), lambda g, c: (g, cidx(c), 0, 0)),
            bc=pl.BlockSpec((L, n), lambda g, c: (cidx(c), g)),
        )
    return mk


def ssd_fwd(xbc, zx, dtg, bias_g, alog_g, d_chan, ngain, L, G, *, name):
    S = xbc.shape[0]
    n = SSM_STATE
    inner = xbc.shape[1] - 2 * G * n
    gw = inner // G
    pd = SB_HEAD_DIM
    hpg = gw // pd
    nc = S // L
    sp = _ssd_specs(S, L, G, hpg, pd, inner)(lambda c: c)

    def body(x_ref, b_ref, c_ref, z_ref, dt_ref, bias_ref, alog_ref, d_ref, ng_ref, tl_ref,
             yn_ref, y_ref, hp_ref, state):
        @pl.when(pl.program_id(1) == 0)
        def _():
            state[...] = jnp.zeros_like(state)

        dt, _, ac, act, ea, dte, cd = _ssd_chunk_terms(dt_ref[...], bias_ref[0], alog_ref[0], tl_ref[...])
        xv = x_ref[...]
        bm = b_ref[...].astype(BF16)
        cm = c_ref[...].astype(BF16)
        cb = _dot(cm, bm, 1, 1)
        tril = lax.broadcasted_iota(jnp.int32, (L, L), 1) <= lax.broadcasted_iota(jnp.int32, (L, L), 0)
        hp_ref[0, 0] = state[...]
        for r in range(hpg):
            ps = slice(r * pd, (r + 1) * pd)
            xr = xv[:, ps]
            xdt = xr * dt[:, r:r + 1]
            lm = jnp.exp(jnp.where(tril, ac[:, r:r + 1] - act[r:r + 1, :], -jnp.inf))
            hprev = state[ps, :]
            y = _dot((cb * lm).astype(BF16), xdt.astype(BF16))
            y = y + _dot(cm, hprev.astype(BF16), 1, 1) * ea[:, r:r + 1]
            y_ref[:, ps] = y + xr * d_ref[:, ps]
            st = _dot((xdt * dte[:, r:r + 1]).astype(BF16), bm, 0, 0)
            state[ps, :] = hprev * cd[:, r:r + 1] + st
        yfull = y_ref[...]
        zg = z_ref[...]
        yg = yfull * (zg * _sigmoid(zg))
        yn_ref[...] = (yg * lax.rsqrt(jnp.mean(yg * yg, axis=1, keepdims=True) + EPS) * ng_ref[...]).astype(BF16)

    return pl.pallas_call(
        body, name=name, grid=(G, nc),
        in_specs=[sp["x"], sp["b"], sp["c"], sp["z"], sp["dt"], sp["gvec"], sp["gvec"], sp["chan"], sp["chan"], sp["tri"]],
        out_specs=[sp["x"], sp["x"], sp["hp"]],
        out_shape=[jax.ShapeDtypeStruct((S, inner), BF16), jax.ShapeDtypeStruct((S, inner), F32),
                   jax.ShapeDtypeStruct((G, nc, gw, n), F32)],
        scratch_shapes=[pltpu.VMEM((gw, n), F32)],
        compiler_params=_params(("arbitrary", "arbitrary")),
    )(xbc, xbc, xbc, zx, dtg, bias_g, alog_g, d_chan, ngain, _tri(L, "row_ge_col"))


def ssd_bwd(xbc, zx, dtg, bias_g, alog_g, d_chan, ngain, yfull, hp, dyn, L, G, *, name):
    S = xbc.shape[0]
    n = SSM_STATE
    inner = xbc.shape[1] - 2 * G * n
    gw = inner // G
    pd = SB_HEAD_DIM
    hpg = gw // pd
    nc = S // L
    sp = _ssd_specs(S, L, G, hpg, pd, inner)(lambda c: nc - 1 - c)

    def body(x_ref, b_ref, c_ref, z_ref, dt_ref, bias_ref, alog_ref, d_ref, ng_ref, tl_ref, tu_ref,
             yf_ref, hp_ref, dyn_ref,
             dz_ref, dx_ref, db_ref, dc_ref, ddt_ref, dbias_ref, dalog_ref, dd_ref, dng_ref, dstate):
        first = pl.program_id(1) == 0

        @pl.when(first)
        def _():
            dstate[...] = jnp.zeros_like(dstate)
            dbias_ref[...] = jnp.zeros_like(dbias_ref)
            dalog_ref[...] = jnp.zeros_like(dalog_ref)
            dd_ref[...] = jnp.zeros_like(dd_ref)
            dng_ref[...] = jnp.zeros_like(dng_ref)

        dtraw = dt_ref[...]
        dt, a_neg, ac, act, ea, dte, cd = _ssd_chunk_terms(dtraw, bias_ref[0], alog_ref[0], tl_ref[...])
        xv = x_ref[...]
        bm = b_ref[...].astype(BF16)
        cm = c_ref[...].astype(BF16)
        cb = _dot(cm, bm, 1, 1)
        tril = lax.broadcasted_iota(jnp.int32, (L, L), 1) <= lax.broadcasted_iota(jnp.int32, (L, L), 0)
        lane = lax.broadcasted_iota(jnp.int32, (L, LANES), 1)
        lane1 = lax.broadcasted_iota(jnp.int32, (1, LANES), 1)

        yfull = yf_ref[...]
        zg = z_ref[...]
        sg = _sigmoid(zg)
        gate = zg * sg
        yg = yfull * gate
        rr = lax.rsqrt(jnp.mean(yg * yg, axis=1, keepdims=True) + EPS)
        yhat = yg * rr
        dynv = dyn_ref[...]
        dng_ref[...] += jnp.sum(dynv * yhat, axis=0, keepdims=True)
        t = dynv * ng_ref[...]
        dyg = rr * (t - yhat * jnp.mean(yhat * t, axis=1, keepdims=True))
        dy = dyg * gate
        dz_ref[...] = dyg * yfull * (sg * (1.0 + zg * (1.0 - sg)))

        dcb = jnp.zeros((L, L), F32)
        dc_acc = jnp.zeros((L, n), F32)
        db_acc = jnp.zeros((L, n), F32)
        dac = jnp.zeros((L, LANES), F32)
        xdx = jnp.zeros((L, LANES), F32)
        tail = jnp.zeros((1, LANES), F32)
        dskip = jnp.zeros((1, LANES), F32)
        ones_l = jnp.ones((L, LANES), BF16)
        for r in range(hpg):
            ps = slice(r * pd, (r + 1) * pd)
            xr = xv[:, ps]
            dyr = dy[:, ps]
            dtr = dt[:, r:r + 1]
            dter = dte[:, r:r + 1]
            cdr = cd[:, r:r + 1]
            xdt = xr * dtr
            xdtb = xdt.astype(BF16)
            dyrb = dyr.astype(BF16)
            lm = jnp.exp(jnp.where(tril, ac[:, r:r + 1] - act[r:r + 1, :], -jnp.inf))
            m32 = cb * lm
            mb = m32.astype(BF16)
            hprev = hp_ref[0, 0, ps, :]
            hpb = hprev.astype(BF16)
            dhn = dstate[ps, :]
            dhnb = dhn.astype(BF16)
            ear = ea[:, r:r + 1]
            gy = (dyr * ear).astype(BF16)
            dc_acc = dc_acc + _dot(gy, hpb)
            dstate[ps, :] = _dot(gy, cm, 0, 0) + dhn * cdr
            bdh = _dot(bm, dhnb, 1, 1)
            db_acc = db_acc + _dot((xdt * dter).astype(BF16), dhnb)
            dm = _dot(dyrb, xdtb, 1, 1)
            dxdt = bdh * dter + _dot(mb, dyrb, 0, 0)
            dcb = dcb + dm * lm
            wmat = dm * m32
            whi = wmat.astype(BF16)
            wlo = (wmat - whi.astype(F32)).astype(BF16)
            col_w = _dot(whi, ones_l, 0, 0) + _dot(wlo, ones_l, 0, 0)
            sdte = jnp.sum(xdt * bdh * dter, axis=1, keepdims=True)
            e_r = jnp.sum(wmat, axis=1, keepdims=True) + jnp.sum(dyr * _dot(cm, hpb, 1, 1) * ear, axis=1, keepdims=True) - sdte
            c_r = cdr * jnp.sum(jnp.sum(dhn * hprev, axis=1, keepdims=True), axis=0, keepdims=True) \
                + jnp.sum(sdte, axis=0, keepdims=True)
            dac = dac + jnp.where(lane == r, e_r - col_w, 0.0)
            xdx = xdx + jnp.where(lane == r, jnp.sum(dxdt * xr, axis=1, keepdims=True), 0.0)
            tail = tail + jnp.where(lane1 == r, c_r, 0.0)
            dskip = dskip + jnp.where(lane1 == r, jnp.sum(jnp.sum(dyr * xr, axis=1, keepdims=True), axis=0, keepdims=True), 0.0)
            dx_ref[:, ps] = dxdt * dtr + dyr * d_ref[:, ps]
        dcbb = dcb.astype(BF16)
        dc_ref[...] = dc_acc + _dot(dcbb, bm)
        db_ref[...] = db_acc + _dot(dcbb, cm, 0, 0)
        da = _dot_exact_left(tu_ref[...], dac) + tail
        real = lane < hpg
        ddt = jnp.where(real, (da * a_neg + xdx) * _sigmoid(dtraw + bias_ref[0]), 0.0)
        ddt_ref[...] = ddt
        dd_ref[0] += dskip
        dbias_ref[0] += jnp.sum(ddt, axis=0, keepdims=True)
        dalog_ref[0] += jnp.where(lane1 < hpg, jnp.sum(da * dt, axis=0, keepdims=True) * a_neg, 0.0)

    return pl.pallas_call(
        body, name=name, grid=(G, nc),
        in_specs=[sp["x"], sp["b"], sp["c"], sp["z"], sp["dt"], sp["gvec"], sp["gvec"], sp["chan"], sp["chan"],
                  sp["tri"], sp["tri"], sp["x"], sp["hp"], sp["x"]],
        out_specs=[sp["x"], sp["x"], sp["bc"], sp["bc"], sp["dt"], sp["gvec"], sp["gvec"], sp["gvec"], sp["chan"]],
        out_shape=[jax.ShapeDtypeStruct((S, inner), F32), jax.ShapeDtypeStruct((S, inner), F32),
                   jax.ShapeDtypeStruct((S, G * n), F32), jax.ShapeDtypeStruct((S, G * n), F32),
                   jax.ShapeDtypeStruct((S, G * LANES), F32), jax.ShapeDtypeStruct((G, 1, LANES), F32),
                   jax.ShapeDtypeStruct((G, 1, LANES), F32), jax.ShapeDtypeStruct((G, 1, LANES), F32),
                   jax.ShapeDtypeStruct((1, inner), F32)],
        scratch_shapes=[pltpu.VMEM((gw, n), F32)],
        compiler_params=_params(("arbitrary", "arbitrary")),
    )(xbc, xbc, xbc, zx, dtg, bias_g, alog_g, d_chan, ngain, _tri(L, "row_ge_col"), _tri(L, "row_le_col"),
      yfull, hp, dyn)


def _spread_dt(w_dt, G, hpg):
    K = w_dt.shape[0]
    w = w_dt.reshape(K, G, hpg)
    return jnp.pad(w, ((0, 0), (0, 0), (0, LANES - hpg))).reshape(K, G * LANES)


def _group_vec(v, G, hpg):
    return jnp.pad(v.reshape(G, 1, hpg), ((0, 0), (0, 0), (0, LANES - hpg)))


def local_step(x, target, W):
    S, D = x.shape
    depth = W["mix_norm"].shape[0]
    gm_groups, gm_chunk = W["gm_w_s"].shape[1], W["gm_w_s"].shape[2]
    inner = W["ssm_w_out"].shape[1]
    heads = W["ssm_dt_bias"].shape[1]
    conv_dim = W["ssm_conv_w"].shape[2]
    G = (conv_dim - inner) // (2 * SSM_STATE)
    hpg = heads // G
    L = gm_chunk

    saved = []
    for i in range(depth):
        kind, j = i % 3, i // 3
        s = dict(x=x)
        h = rms_fwd(x, W["mix_norm"][i:i + 1], name=f"rms_mix_fwd")
        s["h"] = h
        if kind == 0:
            qkv = matmul(h, W["sb_w_qkv"][j], name="mm_qkv")
            o = sb_attn_fwd(qkv, W["sb_q_gain"][j:j + 1], W["sb_k_gain"][j:j + 1], name="sb_fwd")
            x1 = matmul(o, W["sb_w_o"][j], residual=x, name="mm_sb_out")
            s.update(qkv=qkv, o=o)
        elif kind == 1:
            wc = jnp.where(jnp.tril(jnp.ones((gm_chunk, gm_chunk), bool)), W["gm_w_s"][j], 0.0).astype(BF16)
            bsf = jnp.broadcast_to(W["gm_b_s"][j][:, :, None], (gm_groups, gm_chunk, W["gm_w_out"].shape[1] // gm_groups)).astype(F32)
            zzpre = matmul(h, W["gm_w_in"][j], name="mm_gm_in")
            p = gmlp_fwd(zzpre, W["gm_b_in"][j:j + 1], W["gm_v_gain"][j:j + 1], wc, bsf, name="gm_fwd")
            x1 = matmul(p, W["gm_w_out"][j], residual=x, name="mm_gm_out")
            s.update(zzpre=zzpre, p=p, wc=wc, bsf=bsf)
        else:
            w_in = W["ssm_w_in"][j]
            w_zx = w_in[:, :inner + conv_dim]
            w_dtg = _spread_dt(w_in[:, inner + conv_dim:], G, hpg)
            bias_g = _group_vec(W["ssm_dt_bias"][j], G, hpg)
            alog_g = _group_vec(W["ssm_a_log"][j], G, hpg)
            d_chan = jnp.repeat(W["ssm_d"][j], SB_HEAD_DIM)[None, :]
            ngain = W["ssm_norm_gain"][j:j + 1]
            zx = matmul(h, w_zx, name="mm_ssm_zx")
            dtg = matmul(h, w_dtg, name="mm_ssm_dt")
            xbc = conv_fwd(zx, W["ssm_conv_w"][j], W["ssm_conv_b"][j:j + 1], inner, name="conv_fwd")
            yn, yfull, hp = ssd_fwd(xbc, zx, dtg, bias_g, alog_g, d_chan, ngain, L, G, name="ssd_fwd")
            x1 = matmul(yn, W["ssm_w_out"][j], residual=x, name="mm_ssm_out")
            s.update(w_zx=w_zx, w_dtg=w_dtg, bias_g=bias_g, alog_g=alog_g, d_chan=d_chan, ngain=ngain,
                     zx=zx, dtg=dtg, xbc=xbc, yn=yn, yfull=yfull, hp=hp)
        h2 = rms_fwd(x1, W["ffn_norm"][i:i + 1], name="rms_ffn_fwd")
        gu = matmul(h2, W["ffn_w_gu"][i], name="mm_ffn_gu")
        a = swiglu_fwd(gu, name="swiglu_fwd")
        x2 = matmul(a, W["ffn_w_down"][i], residual=x1, name="mm_ffn_down")
        s.update(x1=x1, h2=h2, gu=gu, a=a)
        saved.append(s)
        x = x2

    dx, loss = loss_head(x, target, name="loss_head")

    gw = {k: [None] * W[k].shape[0] for k in WEIGHTS}
    for i in reversed(range(depth)):
        kind, j = i % 3, i // 3
        s = saved[i]
        da = matmul(dx, W["ffn_w_down"][i], tb=True, name="mm_ffn_da")
        gw["ffn_w_down"][i] = matmul(s["a"], dx, ta=True, out_dtype=BF16, name="mm_ffn_dwdown")
        dgu = swiglu_bwd(s["gu"], da, name="swiglu_bwd")
        dh2 = matmul(dgu, W["ffn_w_gu"][i], tb=True, name="mm_ffn_dh")
        gw["ffn_w_gu"][i] = matmul(s["h2"], dgu, ta=True, out_dtype=BF16, name="mm_ffn_dwgu")
        dx1, dgn = rms_bwd(s["x1"], W["ffn_norm"][i:i + 1], dh2, dx, name="rms_ffn_bwd")
        gw["ffn_norm"][i] = dgn[0]
        if kind == 0:
            do = matmul(dx1, W["sb_w_o"][j], tb=True, name="mm_sb_do")
            gw["sb_w_o"][j] = matmul(s["o"], dx1, ta=True, out_dtype=BF16, name="mm_sb_dwo")
            dqkv, dqg, dkg = sb_attn_bwd(s["qkv"], s["o"], do, W["sb_q_gain"][j:j + 1], W["sb_k_gain"][j:j + 1],
                                         name="sb_bwd")
            gw["sb_q_gain"][j] = dqg[0]
            gw["sb_k_gain"][j] = dkg[0]
            dh = matmul(dqkv, W["sb_w_qkv"][j], tb=True, name="mm_sb_dh")
            gw["sb_w_qkv"][j] = matmul(s["h"], dqkv, ta=True, out_dtype=BF16, name="mm_sb_dwqkv")
        elif kind == 1:
            dp = matmul(dx1, W["gm_w_out"][j], tb=True, name="mm_gm_dp")
            gw["gm_w_out"][j] = matmul(s["p"], dx1, ta=True, out_dtype=BF16, name="mm_gm_dwout")
            dzz, db_in, dvg, dws, dbs = gmlp_bwd(s["zzpre"], W["gm_b_in"][j:j + 1], W["gm_v_gain"][j:j + 1],
                                                s["wc"], s["bsf"], dp, name="gm_bwd")
            gw["gm_b_in"][j] = db_in[0]
            gw["gm_v_gain"][j] = dvg[0]
            gw["gm_w_s"][j] = dws
            gw["gm_b_s"][j] = dbs[:, :gm_groups].T
            dh = matmul(dzz, W["gm_w_in"][j], tb=True, name="mm_gm_dh")
            gw["gm_w_in"][j] = matmul(s["h"], dzz, ta=True, out_dtype=BF16, name="mm_gm_dwin")
        else:
            dyn = matmul(dx1, W["ssm_w_out"][j], tb=True, name="mm_ssm_dyn")
            gw["ssm_w_out"][j] = matmul(s["yn"], dx1, ta=True, out_dtype=BF16, name="mm_ssm_dwout")
            dz, dxs, dbm, dcm, ddt, dbias, dalog, dd, dng = ssd_bwd(
                s["xbc"], s["zx"], s["dtg"], s["bias_g"], s["alog_g"], s["d_chan"], s["ngain"], s["yfull"], s["hp"],
                dyn, L, G, name="ssd_bwd")
            dxbc = jnp.concatenate([dxs, dbm, dcm], axis=1)
            dpre, dcw, dcb = conv_bwd(s["zx"], W["ssm_conv_w"][j], W["ssm_conv_b"][j:j + 1], inner, dxbc,
                                      name="conv_bwd")
            dzx = jnp.concatenate([dz, dpre], axis=1)
            dh = matmul(ddt, s["w_dtg"], tb=True, name="mm_ssm_dh_dt")
            dh = matmul(dzx, s["w_zx"], tb=True, residual=dh, name="mm_ssm_dh")
            dw_zx = matmul(s["h"], dzx, ta=True, out_dtype=BF16, name="mm_ssm_dwzx")
            dw_dtg = matmul(s["h"], ddt, ta=True, out_dtype=BF16, name="mm_ssm_dwdt")
            dw_dt = dw_dtg.reshape(D, G, LANES)[:, :, :hpg].reshape(D, heads)
            gw["ssm_w_in"][j] = jnp.concatenate([dw_zx, dw_dt], axis=1)
            gw["ssm_conv_w"][j] = dcw
            gw["ssm_conv_b"][j] = dcb[0]
            gw["ssm_dt_bias"][j] = dbias[:, 0, :hpg].reshape(heads)
            gw["ssm_a_log"][j] = dalog[:, 0, :hpg].reshape(heads)
            gw["ssm_d"][j] = dd[:, 0, :hpg].reshape(heads)
            gw["ssm_norm_gain"][j] = dng[0]
        dx, dgn = rms_bwd(s["x"], W["mix_norm"][i:i + 1], dh, dx1, name="rms_mix_bwd")
        gw["mix_norm"][i] = dgn[0]

    grads = {k: jnp.stack(v, axis=0) for k, v in gw.items()}
    return loss, dx, grads


MESH = pl.DeviceIdType.MESH
HBM_SPEC = pl.BlockSpec(memory_space=pltpu.HBM)
VMEM_SPEC = pl.BlockSpec(memory_space=pltpu.VMEM)


def _my_position():
    return lax.axis_index("x"), lax.axis_index("y"), lax.axis_index("c")


def _flip(v, bit):
    return 1 - v if bit else v


def all_gather_packed(shard, *, name):
    R, C = shard.shape

    def body(x_ref, out_ref, send_sems, recv_sems, local_sem):
        x, y, c = _my_position()
        me, sibling = (x, y, c), (x, y, 1 - c)
        chips = [(1 - x, y), (x, 1 - y), (1 - x, 1 - y)]

        def slot(px, py, pc):
            return out_ref.at[4 * px + 2 * py + pc]

        def copy(k, block, to, src=None):
            return pltpu.make_async_remote_copy(
                src_ref=slot(*block) if src is None else src, dst_ref=slot(*block),
                send_sem=send_sems.at[k], recv_sem=recv_sems.at[k], device_id=to, device_id_type=MESH)

        mine = pltpu.make_async_copy(x_ref, slot(*me), local_sem)
        mine.start()
        first = [copy(0, me, sibling, src=x_ref)]
        first += [copy(1 + j, me, (*chip, c), src=x_ref) for j, chip in enumerate(chips)]
        for cp in first:
            cp.start()
        passed = [copy(4 + j, (*chip, c), sibling) for j, chip in enumerate(chips)]
        for j, chip in enumerate(chips):
            copy(1 + j, (*chip, c), me).wait_recv()
            passed[j].start()
        copy(0, sibling, me).wait_recv()
        for j, chip in enumerate(chips):
            copy(4 + j, (*chip, 1 - c), me).wait_recv()
        for cp in first + passed:
            cp.wait_send()
        mine.wait()

    return pl.pallas_call(
        body, name=name, out_shape=jax.ShapeDtypeStruct((N_DEV, R, C), shard.dtype),
        in_specs=[HBM_SPEC], out_specs=HBM_SPEC,
        scratch_shapes=[pltpu.SemaphoreType.DMA((7,)), pltpu.SemaphoreType.DMA((7,)), pltpu.SemaphoreType.DMA],
    )(shard)


def exchange_for_reduce_scatter(g, *, name):
    _, R, C = g.shape

    def body(g_ref, out_ref, send_sems, recv_sems, local_sem):
        x, y, c = _my_position()
        me = 4 * x + 2 * y + c
        mine = pltpu.make_async_copy(g_ref.at[me], out_ref.at[me], local_sem)
        mine.start()
        copies = []
        for k in range(1, N_DEV):
            px, py, pc = _flip(x, k & 4), _flip(y, k & 2), _flip(c, k & 1)
            copies.append(pltpu.make_async_remote_copy(
                src_ref=g_ref.at[4 * px + 2 * py + pc], dst_ref=out_ref.at[me],
                send_sem=send_sems.at[k - 1], recv_sem=recv_sems.at[k - 1], device_id=(px, py, pc), device_id_type=MESH))
        for cp in copies:
            cp.start()
        for cp in copies:
            cp.wait()
        mine.wait()

    return pl.pallas_call(
        body, name=name, out_shape=jax.ShapeDtypeStruct(g.shape, g.dtype), in_specs=[HBM_SPEC], out_specs=HBM_SPEC,
        scratch_shapes=[pltpu.SemaphoreType.DMA((7,)), pltpu.SemaphoreType.DMA((7,)), pltpu.SemaphoreType.DMA],
    )(g)


def sum_slots(recv, *, name):
    n, R, C = recv.shape
    tr = _pick(R, (512, 256, 128))

    def body(r_ref, o_ref):
        acc = r_ref[0].astype(F32)
        for s in range(1, n):
            acc = acc + r_ref[s].astype(F32)
        o_ref[...] = acc

    return pl.pallas_call(
        body, name=name, grid=(R // tr,), in_specs=[pl.BlockSpec((n, tr, C), lambda i: (0, i, 0))],
        out_specs=pl.BlockSpec((tr, C), lambda i: (i, 0)), out_shape=jax.ShapeDtypeStruct((R, C), F32),
        compiler_params=_params(("parallel",)),
    )(recv)


def all_reduce_small(v, *, name):
    R, C = v.shape

    def body(v_ref, o_ref, buf, send_sems, recv_sems):
        x, y, c = _my_position()
        me = 4 * x + 2 * y + c
        buf[me] = v_ref[...]
        copies = []
        for k in range(1, N_DEV):
            px, py, pc = _flip(x, k & 4), _flip(y, k & 2), _flip(c, k & 1)
            copies.append(pltpu.make_async_remote_copy(
                src_ref=v_ref, dst_ref=buf.at[me], send_sem=send_sems.at[k - 1], recv_sem=recv_sems.at[k - 1],
                device_id=(px, py, pc), device_id_type=MESH))
        for cp in copies:
            cp.start()
        for cp in copies:
            cp.wait()
        acc = buf[0]
        for s in range(1, N_DEV):
            acc = acc + buf[s]
        o_ref[...] = acc

    return pl.pallas_call(
        body, name=name, out_shape=jax.ShapeDtypeStruct((R, C), F32), in_specs=[VMEM_SPEC], out_specs=VMEM_SPEC,
        scratch_shapes=[pltpu.VMEM((N_DEV, R, C), F32), pltpu.SemaphoreType.DMA((7,)), pltpu.SemaphoreType.DMA((7,))],
        compiler_params=pltpu.CompilerParams(vmem_limit_bytes=VMEM_LIMIT_BYTES),
    )(v)


def _pad_rows(a, mult):
    pad = (-a.shape[0]) % mult
    return jnp.pad(a, ((0, pad), (0, 0))) if pad else a


def _pack_rows(arrays, cols, row_mult):
    return _pad_rows(jnp.concatenate([a.reshape(-1, cols) for a in arrays], axis=0), row_mult)


def _unpack_rows(packed, shapes):
    out, r = [], 0
    cols = packed.shape[-1]
    for shp in shapes:
        n = math.prod(shp) // cols
        out.append(packed[..., r:r + n, :].reshape(packed.shape[:-2] + tuple(shp)))
        r += n
    return out


def _gathered_to_full(g, name):
    if name in COL_SHARDED or name == "ssm_conv_w":
        return jnp.moveaxis(g, 0, -2).reshape(g.shape[1:-1] + (N_DEV * g.shape[-1],))
    if name in ("ssm_conv_b", "ssm_norm_gain"):
        return jnp.moveaxis(g, 0, -2).reshape(g.shape[1:-1] + (N_DEV * g.shape[-1],))
    return jnp.moveaxis(g, 0, 1).reshape((g.shape[1], N_DEV * g.shape[2], g.shape[3]))


def _full_to_shards(full, name):
    if name in COL_SHARDED:
        n = full.shape[-1] // N_DEV
        return jnp.moveaxis(full.reshape(full.shape[:-1] + (N_DEV, n)), -2, 0)
    k = full.shape[1] // N_DEV
    return jnp.moveaxis(full.reshape((full.shape[0], N_DEV, k, full.shape[2])), 1, 0)


def _pack_small(arrays):
    flat = []
    for a in arrays:
        f = a.reshape(-1).astype(F32)
        flat.append(jnp.pad(f, (0, (-f.shape[0]) % LANES)))
    return _pad_rows(jnp.concatenate(flat).reshape(-1, LANES), 8)


def _unpack_small(packed, shapes):
    flat = packed.reshape(-1)
    out, r = [], 0
    for shp in shapes:
        n = math.prod(shp)
        out.append(flat[r:r + n].reshape(shp))
        r += n + (-n) % LANES
    return out


ARG_NAMES = ("x",) + WEIGHTS + ("loss_target",) + tuple("m_" + w for w in WEIGHTS) + tuple("v_" + w for w in WEIGHTS)


def kernel(x, mix_norm, ffn_norm, sb_w_qkv, sb_q_gain, sb_k_gain, sb_w_o, gm_w_in, gm_b_in, gm_v_gain, gm_w_s, gm_b_s, gm_w_out, ssm_w_in, ssm_conv_w, ssm_conv_b, ssm_dt_bias, ssm_a_log, ssm_d, ssm_norm_gain, ssm_w_out, ffn_w_gu, ffn_w_down, loss_target, m_mix_norm, m_ffn_norm, m_sb_w_qkv, m_sb_q_gain, m_sb_k_gain, m_sb_w_o, m_gm_w_in, m_gm_b_in, m_gm_v_gain, m_gm_w_s, m_gm_b_s, m_gm_w_out, m_ssm_w_in, m_ssm_conv_w, m_ssm_conv_b, m_ssm_dt_bias, m_ssm_a_log, m_ssm_d, m_ssm_norm_gain, m_ssm_w_out, m_ffn_w_gu, m_ffn_w_down, v_mix_norm, v_ffn_norm, v_sb_w_qkv, v_sb_q_gain, v_sb_k_gain, v_sb_w_o, v_gm_w_in, v_gm_b_in, v_gm_v_gain, v_gm_w_s, v_gm_b_s, v_gm_w_out, v_ssm_w_in, v_ssm_conv_w, v_ssm_conv_b, v_ssm_dt_bias, v_ssm_a_log, v_ssm_d, v_ssm_norm_gain, v_ssm_w_out, v_ffn_w_gu, v_ffn_w_down):
    given = dict(zip(ARG_NAMES, (x, mix_norm, ffn_norm, sb_w_qkv, sb_q_gain, sb_k_gain, sb_w_o, gm_w_in, gm_b_in, gm_v_gain, gm_w_s, gm_b_s, gm_w_out, ssm_w_in, ssm_conv_w, ssm_conv_b, ssm_dt_bias, ssm_a_log, ssm_d, ssm_norm_gain, ssm_w_out, ffn_w_gu, ffn_w_down, loss_target, m_mix_norm, m_ffn_norm, m_sb_w_qkv, m_sb_q_gain, m_sb_k_gain, m_sb_w_o, m_gm_w_in, m_gm_b_in, m_gm_v_gain, m_gm_w_s, m_gm_b_s, m_gm_w_out, m_ssm_w_in, m_ssm_conv_w, m_ssm_conv_b, m_ssm_dt_bias, m_ssm_a_log, m_ssm_d, m_ssm_norm_gain, m_ssm_w_out, m_ffn_w_gu, m_ffn_w_down, v_mix_norm, v_ffn_norm, v_sb_w_qkv, v_sb_q_gain, v_sb_k_gain, v_sb_w_o, v_gm_w_in, v_gm_b_in, v_gm_v_gain, v_gm_w_s, v_gm_b_s, v_gm_w_out, v_ssm_w_in, v_ssm_conv_w, v_ssm_conv_b, v_ssm_dt_bias, v_ssm_a_log, v_ssm_d, v_ssm_norm_gain, v_ssm_w_out, v_ffn_w_gu, v_ffn_w_down)))
    mx, my, mc = _my_position()
    me = 4 * mx + 2 * my + mc

    big_shapes = [given[k].shape for k in BIG]
    sharded_small = [lax.bitcast_convert_type(given[k], BF16) for k in SMALL_SHARDED]
    sharded_small_shapes = [a.shape for a in sharded_small]
    tail = _pad_rows(jnp.concatenate([a.reshape(-1) for a in sharded_small]).reshape(-1, LANES), 1)
    tail = jnp.pad(tail.reshape(-1), (0, (-tail.size) % PACK_COLS)).reshape(-1, PACK_COLS)
    tail_rows = tail.shape[0]
    pack = _pad_rows(jnp.concatenate([given[k].astype(BF16).reshape(-1, PACK_COLS) for k in BIG] + [tail], axis=0), 512)
    gathered = all_gather_packed(pack, name="all_gather_weights")
    W = {k: given[k] for k in SMALL if k not in SMALL_SHARDED}
    parts = _unpack_rows(gathered, big_shapes)
    for k, g in zip(BIG, parts):
        W[k] = _gathered_to_full(g, k)
    r0 = sum(math.prod(s) for s in big_shapes) // PACK_COLS
    tail_g = gathered[:, r0:r0 + tail_rows, :].reshape(N_DEV, -1)
    off = 0
    for k, shp in zip(SMALL_SHARDED, sharded_small_shapes):
        n = math.prod(shp)
        g = lax.bitcast_convert_type(tail_g[:, off:off + n].reshape((N_DEV,) + shp), F32)
        W[k] = _gathered_to_full(g, k)
        off += n

    loss, gx, grads = local_step(given["x"][0], given["loss_target"][0], W)

    contrib = jnp.concatenate([_full_to_shards(grads[k], k).reshape(N_DEV, -1, PACK_COLS) for k in BIG], axis=1)
    contrib = jnp.pad(contrib, ((0, 0), (0, pack.shape[0] - contrib.shape[1]), (0, 0)))
    received = exchange_for_reduce_scatter(contrib, name="reduce_scatter_exchange")
    gbig = sum_slots(received, name="reduce_scatter_sum")

    small_shapes = [grads[k].shape for k in SMALL] + [(1, 1)]
    reduced = all_reduce_small(_pack_small([grads[k] for k in SMALL] + [loss]), name="all_reduce_small")
    small_full = dict(zip(SMALL + ("loss",), _unpack_small(reduced, small_shapes)))

    wbig = _pack_rows([given[k] for k in BIG], PACK_COLS, 512)
    mbig = _pack_rows([given["m_" + k] for k in BIG], PACK_COLS, 512)
    vbig = _pack_rows([given["v_" + k] for k in BIG], PACK_COLS, 512)
    gbig = gbig[:wbig.shape[0]]
    dbig, nmbig, nvbig = adamw(wbig, gbig, mbig, vbig, name="adamw_big")
    out_g, out_d, out_m, out_v = {}, {}, {}, {}
    for dst, src in ((out_g, gbig), (out_d, dbig), (out_m, nmbig), (out_v, nvbig)):
        dst.update(zip(BIG, _unpack_rows(src, big_shapes)))

    gsmall = {}
    for k in SMALL:
        g = small_full[k]
        if k in SMALL_SHARDED:
            n = given[k].shape[-1]
            g = lax.dynamic_slice_in_dim(g, me * n, n, axis=g.ndim - 1)
        gsmall[k] = g
    local_shapes = [given[k].shape for k in SMALL]
    dsm, nmsm, nvsm = adamw(_pack_small([given[k] for k in SMALL]), _pack_small([gsmall[k] for k in SMALL]),
                            _pack_small([given["m_" + k] for k in SMALL]), _pack_small([given["v_" + k] for k in SMALL]),
                            name="adamw_small")
    out_g.update(gsmall)
    for dst, src in ((out_d, dsm), (out_m, nmsm), (out_v, nvsm)):
        dst.update(zip(SMALL, _unpack_small(src, local_shapes)))

    return (small_full["loss"].reshape(()), gx[None],
            *[out_g[k] for k in WEIGHTS], *[out_d[k] for k in WEIGHTS],
            *[out_m[k] for k in WEIGHTS], *[out_v[k] for k in WEIGHTS])
```

```python
import math

import jax
import jax.numpy as jnp
from jax import lax
from jax.experimental import pallas as pl
from jax.experimental.pallas import tpu as pltpu

F32 = jnp.float32
BF16 = jnp.bfloat16
EPS = 1e-6
N_DEV = 8
SB_HEAD_DIM = 64
SSM_STATE = 128
SSM_CONV = 4
ADAM_LR = 0.001
ADAM_B1 = 0.9
ADAM_B2 = 0.999
ADAM_EPS = 1e-08
ADAM_WD = 0.01
ADAM_STEP = 10
VMEM_LIMIT_BYTES = 56 * 1024 * 1024
LANES = 128
PACK_COLS = 1024

BIG = ("sb_w_qkv", "sb_w_o", "gm_w_in", "gm_w_out", "ssm_w_in", "ssm_w_out", "ffn_w_gu", "ffn_w_down")
COL_SHARDED = ("sb_w_qkv", "gm_w_in", "ssm_w_in", "ffn_w_gu")
SMALL = ("mix_norm", "ffn_norm", "sb_q_gain", "sb_k_gain", "gm_b_in", "gm_v_gain", "gm_w_s", "gm_b_s",
         "ssm_conv_w", "ssm_conv_b", "ssm_dt_bias", "ssm_a_log", "ssm_d", "ssm_norm_gain")
SMALL_SHARDED = ("ssm_conv_w", "ssm_conv_b", "ssm_norm_gain")
WEIGHTS = ("mix_norm", "ffn_norm", "sb_w_qkv", "sb_q_gain", "sb_k_gain", "sb_w_o", "gm_w_in", "gm_b_in",
           "gm_v_gain", "gm_w_s", "gm_b_s", "gm_w_out", "ssm_w_in", "ssm_conv_w", "ssm_conv_b", "ssm_dt_bias",
           "ssm_a_log", "ssm_d", "ssm_norm_gain", "ssm_w_out", "ffn_w_gu", "ffn_w_down")


def _params(semantics=None):
    return pltpu.CompilerParams(dimension_semantics=semantics, vmem_limit_bytes=VMEM_LIMIT_BYTES)


def _pick(n, prefs):
    for t in prefs:
        if t <= n and n % t == 0:
            return t
    return n


def _dot(a, b, ca=1, cb=0):
    return lax.dot_general(a, b, (((ca,), (cb,)), ((), ())), preferred_element_type=F32)


def _split3(v):
    h1 = v.astype(BF16)
    r1 = v - h1.astype(F32)
    h2 = r1.astype(BF16)
    h3 = (r1 - h2.astype(F32)).astype(BF16)
    return h1, h2, h3


def _dot_exact_left(mat01, v):
    h1, h2, h3 = _split3(v)
    return _dot(mat01, h1) + _dot(mat01, h2) + _dot(mat01, h3)


def _dot_split2_right(v, mat01):
    hi = v.astype(BF16)
    lo = (v - hi.astype(F32)).astype(BF16)
    return _dot(hi, mat01) + _dot(lo, mat01)


def _sigmoid(v):
    return 1.0 / (1.0 + jnp.exp(-v))


def _softplus(v):
    return jnp.maximum(v, 0.0) + jnp.log(1.0 + jnp.exp(-jnp.abs(v)))


def _erf(v):
    a = jnp.abs(v)
    t = 1.0 / (1.0 + 0.3275911 * a)
    poly = t * (0.254829592 + t * (-0.284496736 + t * (1.421413741 + t * (-1.453152027 + t * 1.061405429))))
    e = 1.0 - poly * jnp.exp(-a * a)
    return jnp.where(v < 0, -e, e)


def _gelu_and_grad(v):
    cdf = 0.5 * (1.0 + _erf(v * (1.0 / math.sqrt(2.0))))
    pdf = jnp.exp(-0.5 * v * v) * (1.0 / math.sqrt(2.0 * math.pi))
    return v * cdf, cdf + v * pdf


def matmul(a, b, *, ta=False, tb=False, out_dtype=F32, residual=None, name):
    if ta:
        K, M = a.shape
    else:
        M, K = a.shape
    if tb:
        N, Kb = b.shape
    else:
        Kb, N = b.shape
    assert K == Kb, (a.shape, b.shape, ta, tb)
    tm = _pick(M, (1024, 1408, 768, 512, 256, 128))
    tn = _pick(N, (1024, 1408, 1536, 768, 512, 256, 128))
    tk = _pick(K, (512, 1408, 256, 128))
    nk = K // tk
    a_spec = pl.BlockSpec((tk, tm), lambda i, j, k: (k, i)) if ta else pl.BlockSpec((tm, tk), lambda i, j, k: (i, k))
    b_spec = pl.BlockSpec((tn, tk), lambda i, j, k: (j, k)) if tb else pl.BlockSpec((tk, tn), lambda i, j, k: (k, j))
    o_spec = pl.BlockSpec((tm, tn), lambda i, j, k: (i, j))
    ca, cb = (0 if ta else 1), (1 if tb else 0)
    has_res = residual is not None

    def body(*refs):
        if has_res:
            a_ref, b_ref, r_ref, o_ref, acc = refs
        else:
            a_ref, b_ref, o_ref, acc = refs
        k = pl.program_id(2)

        @pl.when(k == 0)
        def _():
            acc[...] = jnp.zeros_like(acc)

        acc[...] += _dot(a_ref[...].astype(BF16), b_ref[...].astype(BF16), ca, cb)

        @pl.when(k == nk - 1)
        def _():
            r = acc[...]
            if has_res:
                r = r + r_ref[...]
            o_ref[...] = r.astype(out_dtype)

    in_specs = [a_spec, b_spec] + ([o_spec] if has_res else [])
    args = (a, b) + ((residual,) if has_res else ())
    return pl.pallas_call(
        body, name=name, grid=(M // tm, N // tn, nk), in_specs=in_specs, out_specs=o_spec,
        out_shape=jax.ShapeDtypeStruct((M, N), out_dtype), scratch_shapes=[pltpu.VMEM((tm, tn), F32)],
        compiler_params=_params(("parallel", "parallel", "arbitrary")),
    )(*args)


def rms_fwd(x, gain, *, name):
    S, D = x.shape
    tr = _pick(S, (512, 256, 128))

    def body(x_ref, g_ref, o_ref):
        xv = x_ref[...]
        r = lax.rsqrt(jnp.mean(xv * xv, axis=1, keepdims=True) + EPS)
        o_ref[...] = (xv * r * g_ref[...]).astype(BF16)

    return pl.pallas_call(
        body, name=name, grid=(S // tr,),
        in_specs=[pl.BlockSpec((tr, D), lambda i: (i, 0)), pl.BlockSpec((1, D), lambda i: (0, 0))],
        out_specs=pl.BlockSpec((tr, D), lambda i: (i, 0)), out_shape=jax.ShapeDtypeStruct((S, D), BF16),
        compiler_params=_params(("parallel",)),
    )(x, gain)


def rms_bwd(x, gain, dh, dres, *, name):
    S, D = x.shape
    tr = _pick(S, (512, 256, 128))

    def body(x_ref, g_ref, dh_ref, dr_ref, dx_ref, dg_ref):
        @pl.when(pl.program_id(0) == 0)
        def _():
            dg_ref[...] = jnp.zeros_like(dg_ref)

        xv = x_ref[...]
        dhv = dh_ref[...]
        r = lax.rsqrt(jnp.mean(xv * xv, axis=1, keepdims=True) + EPS)
        xhat = xv * r
        t = dhv * g_ref[...]
        dx_ref[...] = dr_ref[...] + r * (t - xhat * jnp.mean(xhat * t, axis=1, keepdims=True))
        dg_ref[...] += jnp.sum(dhv * xhat, axis=0, keepdims=True)

    row = pl.BlockSpec((tr, D), lambda i: (i, 0))
    vec = pl.BlockSpec((1, D), lambda i: (0, 0))
    return pl.pallas_call(
        body, name=name, grid=(S // tr,), in_specs=[row, vec, row, row], out_specs=[row, vec],
        out_shape=[jax.ShapeDtypeStruct((S, D), F32), jax.ShapeDtypeStruct((1, D), F32)],
        compiler_params=_params(("arbitrary",)),
    )(x, gain, dh, dres)


def swiglu_fwd(gu, *, name):
    S, F2 = gu.shape
    F = F2 // 2
    tr = _pick(S, (256, 128))

    def body(gu_ref, o_ref):
        g = gu_ref[:, :F]
        u = gu_ref[:, F:]
        o_ref[...] = (g * _sigmoid(g) * u).astype(BF16)

    return pl.pallas_call(
        body, name=name, grid=(S // tr,), in_specs=[pl.BlockSpec((tr, F2), lambda i: (i, 0))],
        out_specs=pl.BlockSpec((tr, F), lambda i: (i, 0)), out_shape=jax.ShapeDtypeStruct((S, F), BF16),
        compiler_params=_params(("parallel",)),
    )(gu)


def swiglu_bwd(gu, da, *, name):
    S, F2 = gu.shape
    F = F2 // 2
    tr = _pick(S, (256, 128))

    def body(gu_ref, da_ref, o_ref):
        g = gu_ref[:, :F]
        u = gu_ref[:, F:]
        dav = da_ref[...]
        s = _sigmoid(g)
        o_ref[:, :F] = (dav * u * (s * (1.0 + g * (1.0 - s)))).astype(BF16)
        o_ref[:, F:] = (dav * g * s).astype(BF16)

    return pl.pallas_call(
        body, name=name, grid=(S // tr,),
        in_specs=[pl.BlockSpec((tr, F2), lambda i: (i, 0)), pl.BlockSpec((tr, F), lambda i: (i, 0))],
        out_specs=pl.BlockSpec((tr, F2), lambda i: (i, 0)), out_shape=jax.ShapeDtypeStruct((S, F2), BF16),
        compiler_params=_params(("parallel",)),
    )(gu, da)


def loss_head(y, target, *, name):
    S, D = y.shape
    tr = _pick(S, (512, 256, 128))

    def body(y_ref, t_ref, dy_ref, l_ref):
        @pl.when(pl.program_id(0) == 0)
        def _():
            l_ref[...] = jnp.zeros_like(l_ref)

        err = y_ref[...] - t_ref[...]
        dy_ref[...] = err * (1.0 / D)
        l_ref[...] += jnp.sum(0.5 * jnp.mean(err * err, axis=1, keepdims=True), axis=0, keepdims=True)

    row = pl.BlockSpec((tr, D), lambda i: (i, 0))
    one = pl.BlockSpec((1, 1), lambda i: (0, 0))
    dy, l = pl.pallas_call(
        body, name=name, grid=(S // tr,), in_specs=[row, row], out_specs=[row, one],
        out_shape=[jax.ShapeDtypeStruct((S, D), F32), jax.ShapeDtypeStruct((1, 1), F32)],
        compiler_params=_params(("arbitrary",)),
    )(y, target)
    return dy, l


def adamw(w, g, m, v, *, name):
    R, C = w.shape
    tr = _pick(R, (512, 256, 128, 64, 32, 16, 8))

    def body(w_ref, g_ref, m_ref, v_ref, d_ref, mo_ref, vo_ref):
        gv = g_ref[...]
        mn = ADAM_B1 * m_ref[...] + (1.0 - ADAM_B1) * gv
        vn = ADAM_B2 * v_ref[...] + (1.0 - ADAM_B2) * jnp.square(gv)
        m_hat = mn / (1.0 - ADAM_B1 ** ADAM_STEP)
        v_hat = vn / (1.0 - ADAM_B2 ** ADAM_STEP)
        d_ref[...] = -ADAM_LR * (m_hat / (jnp.sqrt(v_hat) + ADAM_EPS) + ADAM_WD * w_ref[...])
        mo_ref[...] = mn
        vo_ref[...] = vn

    blk = pl.BlockSpec((tr, C), lambda i: (i, 0))
    sds = jax.ShapeDtypeStruct((R, C), F32)
    return pl.pallas_call(
        body, name=name, grid=(R // tr,), in_specs=[blk] * 4, out_specs=[blk] * 3, out_shape=[sds] * 3,
        compiler_params=_params(("parallel",)),
    )(w, g, m, v)


def _tri(n, kind):
    r = lax.broadcasted_iota(jnp.int32, (n, n), 0)
    c = lax.broadcasted_iota(jnp.int32, (n, n), 1)
    if kind == "row_gt_col":
        return (r > c).astype(BF16)
    if kind == "row_ge_col":
        return (r >= c).astype(BF16)
    if kind == "row_le_col":
        return (r <= c).astype(BF16)
    raise ValueError(kind)


def _sb_tile(qi, kj, scale, r_carry, u_strict, masked):
    z = _dot(qi, kj, 1, 1) * scale
    e = jnp.exp(-jnp.abs(z))
    lb = jnp.minimum(z, 0.0) - jnp.log(1.0 + e)
    l1m = lb - z
    keep = None
    if masked:
        tq, tk = z.shape
        keep = lax.broadcasted_iota(jnp.int32, (tq, tk), 1) < lax.broadcasted_iota(jnp.int32, (tq, tk), 0)
        l1m = jnp.where(keep, l1m, 0.0)
    w = jnp.exp(lb + _dot(l1m.astype(BF16), u_strict) + r_carry)
    if masked:
        w = jnp.where(keep, w, 0.0)
    return z, e, l1m, w, keep


def _sb_prep(T, nb, hd, refs_in, gains, scratch):
    def prep(i, _):
        rows = pl.ds(pl.multiple_of(i * T, T), T)
        for hh in range(2):
            sl = slice(hd * hh, hd * hh + hd)
            for n, (src, dst) in enumerate(zip(refs_in, scratch)):
                v = src[rows, sl]
                if n < 2:
                    v = v * lax.rsqrt(jnp.mean(v * v, axis=1, keepdims=True) + EPS) * gains[n][...]
                dst[hh, rows, :] = v.astype(BF16)
        return 0

    lax.fori_loop(0, nb, prep, 0)


def sb_attn_fwd(qkv, q_gain, k_gain, *, name, gather=None):
    S, D3 = qkv.shape
    D = D3 // 3
    npairs = D // LANES
    hd = SB_HEAD_DIM
    T = min(256, S)
    nb = S // T
    scale = 1.0 / math.sqrt(hd)

    def body(*refs):
        if gather is None:
            q_ref, k_ref, v_ref, qg_ref, kg_ref, us_ref, o_ref, qn_s, kn_s, vb_s = refs
        else:
            q_ref, k_ref, v_ref, qg_ref, kg_ref, us_ref, ag_in, o_ref, ag_out, qn_s, kn_s, vb_s = refs[:12]
            comm = (ag_in, ag_out) + refs[12:]
            step = pl.program_id(0)
            pl.when(step == 0)(lambda: _ag_phase(comm, "start"))
            pl.when(step == (5 * npairs) // 8)(lambda: _ag_phase(comm, "forward"))
        us = us_ref[...]
        _sb_prep(T, nb, hd, (q_ref, k_ref, v_ref), (qg_ref, kg_ref), (qn_s, kn_s, vb_s))

        def qblock(i, _):
            rows_i = pl.ds(pl.multiple_of(i * T, T), T)
            qi = [qn_s[hh, rows_i, :] for hh in range(2)]

            def tile(j, carry, masked):
                rows_j = pl.ds(pl.multiple_of(j * T, T), T)
                out = []
                for hh in range(2):
                    acc, rc = carry[hh]
                    _, _, l1m, w, _ = _sb_tile(qi[hh], kn_s[hh, rows_j, :], scale, rc, us, masked)
                    acc = acc + _dot(w.astype(BF16), vb_s[hh, rows_j, :])
                    rc = rc + jnp.sum(l1m, axis=1, keepdims=True)
                    out.append((acc, rc))
                return tuple(out)

            init = (jnp.zeros((T, hd), F32), jnp.zeros((T, 1), F32))
            carry = tile(i, (init, init), True)
            carry = lax.fori_loop(0, i, lambda jj, c: tile(i - 1 - jj, c, False), carry)
            o_ref[rows_i, :] = jnp.concatenate([carry[0][0], carry[1][0]], axis=1)
            return 0

        lax.fori_loop(0, nb, qblock, 0)
        if gather is not None:
            pl.when(step == npairs - 1)(lambda: _ag_phase(comm, "finish"))

    col = lambda off: pl.BlockSpec((S, LANES), lambda p, off=off: (0, off + p))
    gain = pl.BlockSpec((1, hd), lambda p: (0, 0))
    in_specs = [col(0), col(npairs), col(2 * npairs), gain, gain, pl.BlockSpec((T, T), lambda p: (0, 0))]
    out_specs = [pl.BlockSpec((S, LANES), lambda p: (0, p))]
    out_shape = [jax.ShapeDtypeStruct((S, D), F32)]
    scratch = [pltpu.VMEM((2, S, hd), BF16)] * 3
    args = [qkv, qkv, qkv, q_gain, k_gain, _tri(T, "row_gt_col")]
    if gather is not None:
        in_specs.append(HBM_SPEC)
        out_specs.append(HBM_SPEC)
        out_shape.append(jax.ShapeDtypeStruct((N_DEV,) + gather.shape, gather.dtype))
        scratch += COMM_SEMAPHORES
        args.append(gather)
    out = pl.pallas_call(
        body, name=name, grid=(npairs,), in_specs=in_specs, out_specs=out_specs, out_shape=out_shape,
        scratch_shapes=scratch, compiler_params=_params(("arbitrary",)),
    )(*args)
    return out[0] if gather is None else tuple(out)


def sb_attn_bwd(qkv, o, do, q_gain, k_gain, *, name, scatter=None):
    S, D3 = qkv.shape
    D = D3 // 3
    npairs = D // LANES
    hd = SB_HEAD_DIM
    T = min(256, S)
    nb = S // T
    scale = 1.0 / math.sqrt(hd)

    def body(*refs):
        if scatter is None:
            (q_ref, k_ref, v_ref, o_ref, do_ref, qg_ref, kg_ref, us_ref, ui_ref,
             dq_ref, dk_ref, dv_ref, dg_ref, qn_s, kn_s, vb_s, dob_s, dkn_s, dv_s) = refs
        else:
            (q_ref, k_ref, v_ref, o_ref, do_ref, qg_ref, kg_ref, us_ref, ui_ref, rs_in,
             dq_ref, dk_ref, dv_ref, dg_ref, rs_out, qn_s, kn_s, vb_s, dob_s, dkn_s, dv_s) = refs[:21]
            comm = (rs_in, rs_out) + refs[21:]
            pl.when(pl.program_id(0) == 0)(lambda: _rs_phase(comm, "start"))

        @pl.when(pl.program_id(0) == 0)
        def _():
            dg_ref[...] = jnp.zeros_like(dg_ref)

        us = us_ref[...]
        ui = ui_ref[...]
        _sb_prep(T, nb, hd, (q_ref, k_ref, v_ref, do_ref), (qg_ref, kg_ref), (qn_s, kn_s, vb_s, dob_s))
        dkn_s[...] = jnp.zeros_like(dkn_s)
        dv_s[...] = jnp.zeros_like(dv_s)

        def qblock(i, _):
            rows_i = pl.ds(pl.multiple_of(i * T, T), T)
            qi = [qn_s[hh, rows_i, :] for hh in range(2)]
            doi = [dob_s[hh, rows_i, :] for hh in range(2)]
            dt_total = [jnp.sum(doi[hh].astype(F32) * o_ref[rows_i, hd * hh:hd * hh + hd], axis=1, keepdims=True)
                        for hh in range(2)]

            def tile(j, carry, masked):
                rows_j = pl.ds(pl.multiple_of(j * T, T), T)
                out = []
                for hh in range(2):
                    dq_acc, rc, gc = carry[hh]
                    kj = kn_s[hh, rows_j, :]
                    z, e, l1m, w, keep = _sb_tile(qi[hh], kj, scale, rc, us, masked)
                    wb = w.astype(BF16)
                    g = _dot(doi[hh], vb_s[hh, rows_j, :], 1, 1) * wb.astype(F32)
                    dv_s[hh, rows_j, :] += _dot(wb, doi[hh], 0, 0)
                    g_prefix = dt_total[hh] - (gc + _dot_split2_right(g, ui))
                    inv = 1.0 / (1.0 + e)
                    pos = z >= 0
                    sig = jnp.where(pos, inv, e * inv)
                    sig_neg = jnp.where(pos, e * inv, inv)
                    dz = g * sig_neg - g_prefix * sig
                    if masked:
                        dz = jnp.where(keep, dz, 0.0)
                    dzb = (dz * scale).astype(BF16)
                    dq_acc = dq_acc + _dot(dzb, kj)
                    dkn_s[hh, rows_j, :] += _dot(dzb, qi[hh], 0, 0)
                    rc = rc + jnp.sum(l1m, axis=1, keepdims=True)
                    gc = gc + jnp.sum(g, axis=1, keepdims=True)
                    out.append((dq_acc, rc, gc))
                return tuple(out)

            zero1 = jnp.zeros((T, 1), F32)
            init = (jnp.zeros((T, hd), F32), zero1, zero1)
            carry = tile(i, (init, init), True)
            carry = lax.fori_loop(0, i, lambda jj, c: tile(i - 1 - jj, c, False), carry)
            dq_ref[rows_i, :] = jnp.concatenate([carry[0][0], carry[1][0]], axis=1)
            return 0

        lax.fori_loop(0, nb, qblock, 0)

        def finish(i, carry):
            rows = pl.ds(pl.multiple_of(i * T, T), T)
            new = []
            for hh in range(2):
                sl = slice(hd * hh, hd * hh + hd)
                outs = []
                for raw_ref, gain_ref, dn in ((q_ref, qg_ref, dq_ref[rows, sl]), (k_ref, kg_ref, dkn_s[hh, rows, :])):
                    raw = raw_ref[rows, sl]
                    r = lax.rsqrt(jnp.mean(raw * raw, axis=1, keepdims=True) + EPS)
                    hat = raw * r
                    t = dn * gain_ref[...]
                    outs.append((r * (t - hat * jnp.mean(hat * t, axis=1, keepdims=True)),
                                 jnp.sum(dn * hat, axis=0, keepdims=True)))
                dq_ref[rows, sl] = outs[0][0]
                dk_ref[rows, sl] = outs[1][0]
                dv_ref[rows, sl] = dv_s[hh, rows, :]
                new.append((carry[hh][0] + outs[0][1], carry[hh][1] + outs[1][1]))
            return tuple(new)

        zg = (jnp.zeros((1, hd), F32), jnp.zeros((1, hd), F32))
        tot = lax.fori_loop(0, nb, finish, (zg, zg))
        dg_ref[0:1, 0:hd] += tot[0][0] + tot[1][0]
        dg_ref[1:2, 0:hd] += tot[0][1] + tot[1][1]
        if scatter is not None:
            pl.when(pl.program_id(0) == npairs - 1)(lambda: _rs_phase(comm, "finish"))

    col = lambda off: pl.BlockSpec((S, LANES), lambda p, off=off: (0, off + p))
    gain = pl.BlockSpec((1, hd), lambda p: (0, 0))
    tri = pl.BlockSpec((T, T), lambda p: (0, 0))
    pair = pl.BlockSpec((S, LANES), lambda p: (0, p))
    in_specs = [col(0), col(npairs), col(2 * npairs), pair, pair, gain, gain, tri, tri]
    out_specs = [pair, pair, pair, pl.BlockSpec((8, LANES), lambda p: (0, 0))]
    out_shape = [jax.ShapeDtypeStruct((S, D), F32)] * 3 + [jax.ShapeDtypeStruct((8, LANES), F32)]
    scratch = [pltpu.VMEM((2, S, hd), BF16)] * 4 + [pltpu.VMEM((2, S, hd), F32)] * 2
    args = [qkv, qkv, qkv, o, do, q_gain, k_gain, _tri(T, "row_gt_col"), _tri(T, "row_ge_col")]
    if scatter is not None:
        in_specs.append(HBM_SPEC)
        out_specs.append(HBM_SPEC)
        out_shape.append(jax.ShapeDtypeStruct(scatter.shape, scatter.dtype))
        scratch += COMM_SEMAPHORES
        args.append(scatter)
    out = pl.pallas_call(
        body, name=name, grid=(npairs,), in_specs=in_specs, out_specs=out_specs, out_shape=out_shape,
        scratch_shapes=scratch, compiler_params=_params(("arbitrary",)),
    )(*args)
    dq, dk, dv, dg = out[:4]
    res = (jnp.concatenate([dq, dk, dv], axis=1), dg[0:1, :hd], dg[1:2, :hd])
    return res if scatter is None else res + (out[4],)


def gmlp_fwd(zzpre, b_in, v_gain, wc, bsf, *, name):
    S, H2 = zzpre.shape
    H = H2 // 2
    G, T, _ = wc.shape
    gd = H // G

    def body(z_ref, b_ref, vg_ref, wc_ref, bs_ref, p_ref):
        zz, _ = _gelu_and_grad(z_ref[...] + b_ref[...])
        u = zz[:, :H]
        v = zz[:, H:]
        vn = v * lax.rsqrt(jnp.mean(v * v, axis=1, keepdims=True) + EPS) * vg_ref[...]
        for g in range(G):
            gs = slice(g * gd, (g + 1) * gd)
            mixed = _dot(wc_ref[g], vn[:, gs].astype(BF16)) + bs_ref[g]
            p_ref[:, gs] = (u[:, gs] * mixed).astype(BF16)

    full3 = lambda shp: pl.BlockSpec(shp, lambda c: (0, 0, 0))
    return pl.pallas_call(
        body, name=name, grid=(S // T,),
        in_specs=[pl.BlockSpec((T, H2), lambda c: (c, 0)), pl.BlockSpec((1, H2), lambda c: (0, 0)),
                  pl.BlockSpec((1, H), lambda c: (0, 0)), full3((G, T, T)), full3((G, T, gd))],
        out_specs=pl.BlockSpec((T, H), lambda c: (c, 0)), out_shape=jax.ShapeDtypeStruct((S, H), BF16),
        compiler_params=_params(("parallel",)),
    )(zzpre, b_in, v_gain, wc, bsf)


def gmlp_bwd(zzpre, b_in, v_gain, wc, bsf, dp, *, name):
    S, H2 = zzpre.shape
    H = H2 // 2
    G, T, _ = wc.shape
    gd = H // G
    assert G <= LANES

    def body(z_ref, b_ref, vg_ref, wc_ref, bs_ref, dp_ref, dzz_ref, db_ref, dvg_ref, dws_ref, dbs_ref):
        @pl.when(pl.program_id(0) == 0)
        def _():
            db_ref[...] = jnp.zeros_like(db_ref)
            dvg_ref[...] = jnp.zeros_like(dvg_ref)
            dws_ref[...] = jnp.zeros_like(dws_ref)
            dbs_ref[...] = jnp.zeros_like(dbs_ref)

        zz, gp = _gelu_and_grad(z_ref[...] + b_ref[...])
        u = zz[:, :H]
        v = zz[:, H:]
        r = lax.rsqrt(jnp.mean(v * v, axis=1, keepdims=True) + EPS)
        vhat = v * r
        vg = vg_ref[...]
        vn = vhat * vg
        dpv = dp_ref[...]
        tril = lax.broadcasted_iota(jnp.int32, (T, T), 1) <= lax.broadcasted_iota(jnp.int32, (T, T), 0)
        lane = lax.broadcasted_iota(jnp.int32, (T, LANES), 1)
        dbs = jnp.zeros((T, LANES), F32)
        du_parts, dvn_parts = [], []
        for g in range(G):
            gs = slice(g * gd, (g + 1) * gd)
            vng = vn[:, gs].astype(BF16)
            wcg = wc_ref[g]
            mixed = _dot(wcg, vng) + bs_ref[g]
            dpg = dpv[:, gs]
            du_parts.append(dpg * mixed)
            dmx = dpg * u[:, gs]
            dmxb = dmx.astype(BF16)
            dvn_parts.append(_dot(wcg, dmxb, 0, 0))
            dws_ref[g] += jnp.where(tril, _dot(dmxb, vng, 1, 1), 0.0)
            dbs = dbs + jnp.where(lane == g, jnp.sum(dmx, axis=1, keepdims=True), 0.0)
        dbs_ref[...] += dbs
        du = jnp.concatenate(du_parts, axis=1)
        dvn = jnp.concatenate(dvn_parts, axis=1)
        dvg_ref[...] += jnp.sum(dvn * vhat, axis=0, keepdims=True)
        t = dvn * vg
        dv = r * (t - vhat * jnp.mean(vhat * t, axis=1, keepdims=True))
        dzu = du * gp[:, :H]
        dzv = dv * gp[:, H:]
        dzz_ref[:, :H] = dzu.astype(BF16)
        dzz_ref[:, H:] = dzv.astype(BF16)
        db_ref[:, :H] += jnp.sum(dzu, axis=0, keepdims=True)
        db_ref[:, H:] += jnp.sum(dzv, axis=0, keepdims=True)

    full3 = lambda shp: pl.BlockSpec(shp, lambda c: (0, 0, 0))
    vec = lambda n: pl.BlockSpec((1, n), lambda c: (0, 0))
    return pl.pallas_call(
        body, name=name, grid=(S // T,),
        in_specs=[pl.BlockSpec((T, H2), lambda c: (c, 0)), vec(H2), vec(H), full3((G, T, T)), full3((G, T, gd)),
                  pl.BlockSpec((T, H), lambda c: (c, 0))],
        out_specs=[pl.BlockSpec((T, H2), lambda c: (c, 0)), vec(H2), vec(H), full3((G, T, T)),
                   pl.BlockSpec((T, LANES), lambda c: (0, 0))],
        out_shape=[jax.ShapeDtypeStruct((S, H2), BF16), jax.ShapeDtypeStruct((1, H2), F32),
                   jax.ShapeDtypeStruct((1, H), F32), jax.ShapeDtypeStruct((G, T, T), F32),
                   jax.ShapeDtypeStruct((T, LANES), F32)],
        compiler_params=_params(("arbitrary",)),
    )(zzpre, b_in, v_gain, wc, bsf, dp)


def _shift_rows(v, k, n_rows):
    if k == 0:
        return v
    rolled = pltpu.roll(v, k % n_rows, 0)
    row = lax.broadcasted_iota(jnp.int32, v.shape, 0)
    keep = (row >= k) if k > 0 else (row < n_rows + k)
    return jnp.where(keep, rolled, 0.0)


def conv_fwd(zx, conv_w, conv_b, col0, *, name):
    S = zx.shape[0]
    C = conv_w.shape[1]
    tc = _pick(C, (256, 128))
    off = col0 // tc
    assert col0 % tc == 0

    def body(x_ref, w_ref, b_ref, o_ref):
        xv = x_ref[...]
        acc = b_ref[...] + w_ref[SSM_CONV - 1:SSM_CONV, :] * xv
        for k in range(SSM_CONV - 1):
            acc = acc + w_ref[k:k + 1, :] * _shift_rows(xv, SSM_CONV - 1 - k, S)
        o_ref[...] = acc * _sigmoid(acc)

    return pl.pallas_call(
        body, name=name, grid=(C // tc,),
        in_specs=[pl.BlockSpec((S, tc), lambda j: (0, off + j)), pl.BlockSpec((SSM_CONV, tc), lambda j: (0, j)),
                  pl.BlockSpec((1, tc), lambda j: (0, j))],
        out_specs=pl.BlockSpec((S, tc), lambda j: (0, j)), out_shape=jax.ShapeDtypeStruct((S, C), F32),
        compiler_params=_params(("parallel",)),
    )(zx, conv_w, conv_b)


def conv_bwd(zx, conv_w, conv_b, col0, dout, *, name):
    S = zx.shape[0]
    C = conv_w.shape[1]
    tc = _pick(C, (256, 128))
    off = col0 // tc

    def body(x_ref, w_ref, b_ref, do_ref, dx_ref, dw_ref, db_ref):
        xv = x_ref[...]
        shifted = [_shift_rows(xv, SSM_CONV - 1 - k, S) for k in range(SSM_CONV)]
        acc = b_ref[...]
        for k in range(SSM_CONV):
            acc = acc + w_ref[k:k + 1, :] * shifted[k]
        s = _sigmoid(acc)
        dacc = do_ref[...] * (s * (1.0 + acc * (1.0 - s)))
        db_ref[...] = jnp.sum(dacc, axis=0, keepdims=True)
        dx = jnp.zeros_like(xv)
        for k in range(SSM_CONV):
            dw_ref[k:k + 1, :] = jnp.sum(dacc * shifted[k], axis=0, keepdims=True)
            dx = dx + w_ref[k:k + 1, :] * _shift_rows(dacc, -(SSM_CONV - 1 - k), S)
        dx_ref[...] = dx

    slab = pl.BlockSpec((S, tc), lambda j: (0, j))
    return pl.pallas_call(
        body, name=name, grid=(C // tc,),
        in_specs=[pl.BlockSpec((S, tc), lambda j: (0, off + j)), pl.BlockSpec((SSM_CONV, tc), lambda j: (0, j)),
                  pl.BlockSpec((1, tc), lambda j: (0, j)), slab],
        out_specs=[slab, pl.BlockSpec((SSM_CONV, tc), lambda j: (0, j)), pl.BlockSpec((1, tc), lambda j: (0, j))],
        out_shape=[jax.ShapeDtypeStruct((S, C), F32), jax.ShapeDtypeStruct((SSM_CONV, C), F32),
                   jax.ShapeDtypeStruct((1, C), F32)],
        compiler_params=_params(("parallel",)),
    )(zx, conv_w, conv_b, dout)


def _ssd_chunk_terms(dtraw, bias, a_log, tl):
    dt = _softplus(dtraw + bias)
    a_neg = -jnp.exp(a_log)
    ac = _dot_exact_left(tl, dt * a_neg)
    ac_last = ac[ac.shape[0] - 1:, :]
    return dt, a_neg, ac, ac.T, jnp.exp(ac), jnp.exp(ac_last - ac), jnp.exp(ac_last)


def _ssd_specs(S, L, G, hpg, pd, inner):
    gw = hpg * pd
    n = SSM_STATE
    xb = inner // n

    def mk(cidx):
        return dict(
            x=pl.BlockSpec((L, gw), lambda g, c: (cidx(c), g)),
            b=pl.BlockSpec((L, n), lambda g, c: (cidx(c), xb + g)),
            c=pl.BlockSpec((L, n), lambda g, c: (cidx(c), xb + G + g)),
            z=pl.BlockSpec((L, gw), lambda g, c: (cidx(c), g)),
            dt=pl.BlockSpec((L, LANES), lambda g, c: (cidx(c), g)),
            gvec=pl.BlockSpec((1, 1, LANES), lambda g, c: (g, 0, 0)),
            chan=pl.BlockSpec((1, gw), lambda g, c: (0, g)),
            tri=pl.BlockSpec((L, L), lambda g, c: (0, 0)),
            hp=pl.BlockSpec((1, 1, gw, n), lambda g, c: (g, cidx(c), 0, 0)),
            bc=pl.BlockSpec((L, n), lambda g, c: (cidx(c), g)),
        )
    return mk


def ssd_fwd(xbc, zx, dtg, bias_g, alog_g, d_chan, ngain, L, G, *, name):
    S = xbc.shape[0]
    n = SSM_STATE
    inner = xbc.shape[1] - 2 * G * n
    gw = inner // G
    pd = SB_HEAD_DIM
    hpg = gw // pd
    nc = S // L
    sp = _ssd_specs(S, L, G, hpg, pd, inner)(lambda c: c)

    def body(x_ref, b_ref, c_ref, z_ref, dt_ref, bias_ref, alog_ref, d_ref, ng_ref, tl_ref,
             yn_ref, y_ref, hp_ref, state):
        @pl.when(pl.program_id(1) == 0)
        def _():
            state[...] = jnp.zeros_like(state)

        dt, _, ac, act, ea, dte, cd = _ssd_chunk_terms(dt_ref[...], bias_ref[0], alog_ref[0], tl_ref[...])
        xv = x_ref[...]
        bm = b_ref[...].astype(BF16)
        cm = c_ref[...].astype(BF16)
        cb = _dot(cm, bm, 1, 1)
        tril = lax.broadcasted_iota(jnp.int32, (L, L), 1) <= lax.broadcasted_iota(jnp.int32, (L, L), 0)
        hp_ref[0, 0] = state[...]
        for r in range(hpg):
            ps = slice(r * pd, (r + 1) * pd)
            xr = xv[:, ps]
            xdt = xr * dt[:, r:r + 1]
            lm = jnp.exp(jnp.where(tril, ac[:, r:r + 1] - act[r:r + 1, :], -jnp.inf))
            hprev = state[ps, :]
            y = _dot((cb * lm).astype(BF16), xdt.astype(BF16))
            y = y + _dot(cm, hprev.astype(BF16), 1, 1) * ea[:, r:r + 1]
            y_ref[:, ps] = y + xr * d_ref[:, ps]
            st = _dot((xdt * dte[:, r:r + 1]).astype(BF16), bm, 0, 0)
            state[ps, :] = hprev * cd[:, r:r + 1] + st
        yfull = y_ref[...]
        zg = z_ref[...]
        yg = yfull * (zg * _sigmoid(zg))
        yn_ref[...] = (yg * lax.rsqrt(jnp.mean(yg * yg, axis=1, keepdims=True) + EPS) * ng_ref[...]).astype(BF16)

    return pl.pallas_call(
        body, name=name, grid=(G, nc),
        in_specs=[sp["x"], sp["b"], sp["c"], sp["z"], sp["dt"], sp["gvec"], sp["gvec"], sp["chan"], sp["chan"], sp["tri"]],
        out_specs=[sp["x"], sp["x"], sp["hp"]],
        out_shape=[jax.ShapeDtypeStruct((S, inner), BF16), jax.ShapeDtypeStruct((S, inner), F32),
                   jax.ShapeDtypeStruct((G, nc, gw, n), F32)],
        scratch_shapes=[pltpu.VMEM((gw, n), F32)],
        compiler_params=_params(("arbitrary", "arbitrary")),
    )(xbc, xbc, xbc, zx, dtg, bias_g, alog_g, d_chan, ngain, _tri(L, "row_ge_col"))


def ssd_bwd(xbc, zx, dtg, bias_g, alog_g, d_chan, ngain, yfull, hp, dyn, L, G, *, name):
    S = xbc.shape[0]
    n = SSM_STATE
    inner = xbc.shape[1] - 2 * G * n
    gw = inner // G
    pd = SB_HEAD_DIM
    hpg = gw // pd
    nc = S // L
    sp = _ssd_specs(S, L, G, hpg, pd, inner)(lambda c: nc - 1 - c)

    def body(x_ref, b_ref, c_ref, z_ref, dt_ref, bias_ref, alog_ref, d_ref, ng_ref, tl_ref, tu_ref,
             yf_ref, hp_ref, dyn_ref,
             dz_ref, dx_ref, db_ref, dc_ref, ddt_ref, dbias_ref, dalog_ref, dd_ref, dng_ref, dstate):
        first = pl.program_id(1) == 0

        @pl.when(first)
        def _():
            dstate[...] = jnp.zeros_like(dstate)
            dbias_ref[...] = jnp.zeros_like(dbias_ref)
            dalog_ref[...] = jnp.zeros_like(dalog_ref)
            dd_ref[...] = jnp.zeros_like(dd_ref)
            dng_ref[...] = jnp.zeros_like(dng_ref)

        dtraw = dt_ref[...]
        dt, a_neg, ac, act, ea, dte, cd = _ssd_chunk_terms(dtraw, bias_ref[0], alog_ref[0], tl_ref[...])
        xv = x_ref[...]
        bm = b_ref[...].astype(BF16)
        cm = c_ref[...].astype(BF16)
        cb = _dot(cm, bm, 1, 1)
        tril = lax.broadcasted_iota(jnp.int32, (L, L), 1) <= lax.broadcasted_iota(jnp.int32, (L, L), 0)
        lane = lax.broadcasted_iota(jnp.int32, (L, LANES), 1)
        lane1 = lax.broadcasted_iota(jnp.int32, (1, LANES), 1)

        yfull = yf_ref[...]
        zg = z_ref[...]
        sg = _sigmoid(zg)
        gate = zg * sg
        yg = yfull * gate
        rr = lax.rsqrt(jnp.mean(yg * yg, axis=1, keepdims=True) + EPS)
        yhat = yg * rr
        dynv = dyn_ref[...]
        dng_ref[...] += jnp.sum(dynv * yhat, axis=0, keepdims=True)
        t = dynv * ng_ref[...]
        dyg = rr * (t - yhat * jnp.mean(yhat * t, axis=1, keepdims=True))
        dy = dyg * gate
        dz_ref[...] = dyg * yfull * (sg * (1.0 + zg * (1.0 - sg)))

        dcb = jnp.zeros((L, L), F32)
        dc_acc = jnp.zeros((L, n), F32)
        db_acc = jnp.zeros((L, n), F32)
        dac = jnp.zeros((L, LANES), F32)
        xdx = jnp.zeros((L, LANES), F32)
        tail = jnp.zeros((1, LANES), F32)
        dskip = jnp.zeros((1, LANES), F32)
        ones_l = jnp.ones((L, LANES), BF16)
        for r in range(hpg):
            ps = slice(r * pd, (r + 1) * pd)
            xr = xv[:, ps]
            dyr = dy[:, ps]
            dtr = dt[:, r:r + 1]
            dter = dte[:, r:r + 1]
            cdr = cd[:, r:r + 1]
            xdt = xr * dtr
            xdtb = xdt.astype(BF16)
            dyrb = dyr.astype(BF16)
            lm = jnp.exp(jnp.where(tril, ac[:, r:r + 1] - act[r:r + 1, :], -jnp.inf))
            m32 = cb * lm
            mb = m32.astype(BF16)
            hprev = hp_ref[0, 0, ps, :]
            hpb = hprev.astype(BF16)
            dhn = dstate[ps, :]
            dhnb = dhn.astype(BF16)
            ear = ea[:, r:r + 1]
            gy = (dyr * ear).astype(BF16)
            dc_acc = dc_acc + _dot(gy, hpb)
            dstate[ps, :] = _dot(gy, cm, 0, 0) + dhn * cdr
            bdh = _dot(bm, dhnb, 1, 1)
            db_acc = db_acc + _dot((xdt * dter).astype(BF16), dhnb)
            dm = _dot(dyrb, xdtb, 1, 1)
            dxdt = bdh * dter + _dot(mb, dyrb, 0, 0)
            dcb = dcb + dm * lm
            wmat = dm * m32
            whi = wmat.astype(BF16)
            wlo = (wmat - whi.astype(F32)).astype(BF16)
            col_w = _dot(whi, ones_l, 0, 0) + _dot(wlo, ones_l, 0, 0)
            sdte = jnp.sum(xdt * bdh * dter, axis=1, keepdims=True)
            e_r = jnp.sum(wmat, axis=1, keepdims=True) + jnp.sum(dyr * _dot(cm, hpb, 1, 1) * ear, axis=1, keepdims=True) - sdte
            c_r = cdr * jnp.sum(jnp.sum(dhn * hprev, axis=1, keepdims=True), axis=0, keepdims=True) \
                + jnp.sum(sdte, axis=0, keepdims=True)
            dac = dac + jnp.where(lane == r, e_r - col_w, 0.0)
            xdx = xdx + jnp.where(lane == r, jnp.sum(dxdt * xr, axis=1, keepdims=True), 0.0)
            tail = tail + jnp.where(lane1 == r, c_r, 0.0)
            dskip = dskip + jnp.where(lane1 == r, jnp.sum(jnp.sum(dyr * xr, axis=1, keepdims=True), axis=0, keepdims=True), 0.0)
            dx_ref[:, ps] = dxdt * dtr + dyr * d_ref[:, ps]
        dcbb = dcb.astype(BF16)
        dc_ref[...] = dc_acc + _dot(dcbb, bm)
        db_ref[...] = db_acc + _dot(dcbb, cm, 0, 0)
        da = _dot_exact_left(tu_ref[...], dac) + tail
        real = lane < hpg
        ddt = jnp.where(real, (da * a_neg + xdx) * _sigmoid(dtraw + bias_ref[0]), 0.0)
        ddt_ref[...] = ddt
        dd_ref[0] += dskip
        dbias_ref[0] += jnp.sum(ddt, axis=0, keepdims=True)
        dalog_ref[0] += jnp.where(lane1 < hpg, jnp.sum(da * dt, axis=0, keepdims=True) * a_neg, 0.0)

    return pl.pallas_call(
        body, name=name, grid=(G, nc),
        in_specs=[sp["x"], sp["b"], sp["c"], sp["z"], sp["dt"], sp["gvec"], sp["gvec"], sp["chan"], sp["chan"],
                  sp["tri"], sp["tri"], sp["x"], sp["hp"], sp["x"]],
        out_specs=[sp["x"], sp["x"], sp["bc"], sp["bc"], sp["dt"], sp["gvec"], sp["gvec"], sp["gvec"], sp["chan"]],
        out_shape=[jax.ShapeDtypeStruct((S, inner), F32), jax.ShapeDtypeStruct((S, inner), F32),
                   jax.ShapeDtypeStruct((S, G * n), F32), jax.ShapeDtypeStruct((S, G * n), F32),
                   jax.ShapeDtypeStruct((S, G * LANES), F32), jax.ShapeDtypeStruct((G, 1, LANES), F32),
                   jax.ShapeDtypeStruct((G, 1, LANES), F32), jax.ShapeDtypeStruct((G, 1, LANES), F32),
                   jax.ShapeDtypeStruct((1, inner), F32)],
        scratch_shapes=[pltpu.VMEM((gw, n), F32)],
        compiler_params=_params(("arbitrary", "arbitrary")),
    )(xbc, xbc, xbc, zx, dtg, bias_g, alog_g, d_chan, ngain, _tri(L, "row_ge_col"), _tri(L, "row_le_col"),
      yfull, hp, dyn)


def _spread_dt(w_dt, G, hpg):
    K = w_dt.shape[0]
    w = w_dt.reshape(K, G, hpg)
    return jnp.pad(w, ((0, 0), (0, 0), (0, LANES - hpg))).reshape(K, G * LANES)


def _group_vec(v, G, hpg):
    return jnp.pad(v.reshape(G, 1, hpg), ((0, 0), (0, 0), (0, LANES - hpg)))


def local_step(x, target, W, late=None):
    S, D = x.shape
    depth = W["mix_norm"].shape[0]
    gm_groups, gm_chunk = W["gm_w_s"].shape[1], W["gm_w_s"].shape[2]
    heads = W["ssm_dt_bias"].shape[1]
    inner = heads * SB_HEAD_DIM
    L = gm_chunk
    received = None

    saved = []
    for i in range(depth):
        kind, j = i % 3, i // 3
        s = dict(x=x)
        h = rms_fwd(x, W["mix_norm"][i:i + 1], name=f"rms_mix_fwd")
        s["h"] = h
        if kind == 0:
            qkv = matmul(h, W["sb_w_qkv"][j], name="mm_qkv")
            if late is not None and i == 0:
                o, gathered = sb_attn_fwd(qkv, W["sb_q_gain"][j:j + 1], W["sb_k_gain"][j:j + 1], name="sb_fwd_gather",
                                          gather=late.shard)
                late.fill(W, gathered)
            else:
                o = sb_attn_fwd(qkv, W["sb_q_gain"][j:j + 1], W["sb_k_gain"][j:j + 1], name="sb_fwd")
            x1 = matmul(o, W["sb_w_o"][j], residual=x, name="mm_sb_out")
            s.update(qkv=qkv, o=o)
        elif kind == 1:
            wc = jnp.where(jnp.tril(jnp.ones((gm_chunk, gm_chunk), bool)), W["gm_w_s"][j], 0.0).astype(BF16)
            bsf = jnp.broadcast_to(W["gm_b_s"][j][:, :, None], (gm_groups, gm_chunk, W["gm_v_gain"].shape[1] // gm_groups)).astype(F32)
            zzpre = matmul(h, W["gm_w_in"][j], name="mm_gm_in")
            p = gmlp_fwd(zzpre, W["gm_b_in"][j:j + 1], W["gm_v_gain"][j:j + 1], wc, bsf, name="gm_fwd")
            x1 = matmul(p, W["gm_w_out"][j], residual=x, name="mm_gm_out")
            s.update(zzpre=zzpre, p=p, wc=wc, bsf=bsf)
        else:
            conv_dim = W["ssm_conv_w"].shape[2]
            G = (conv_dim - inner) // (2 * SSM_STATE)
            hpg = heads // G
            w_in = W["ssm_w_in"][j]
            w_zx = w_in[:, :inner + conv_dim]
            w_dtg = _spread_dt(w_in[:, inner + conv_dim:], G, hpg)
            bias_g = _group_vec(W["ssm_dt_bias"][j], G, hpg)
            alog_g = _group_vec(W["ssm_a_log"][j], G, hpg)
            d_chan = jnp.repeat(W["ssm_d"][j], SB_HEAD_DIM)[None, :]
            ngain = W["ssm_norm_gain"][j:j + 1]
            zx = matmul(h, w_zx, name="mm_ssm_zx")
            dtg = matmul(h, w_dtg, name="mm_ssm_dt")
            xbc = conv_fwd(zx, W["ssm_conv_w"][j], W["ssm_conv_b"][j:j + 1], inner, name="conv_fwd")
            yn, yfull, hp = ssd_fwd(xbc, zx, dtg, bias_g, alog_g, d_chan, ngain, L, G, name="ssd_fwd")
            x1 = matmul(yn, W["ssm_w_out"][j], residual=x, name="mm_ssm_out")
            s.update(w_zx=w_zx, w_dtg=w_dtg, bias_g=bias_g, alog_g=alog_g, d_chan=d_chan, ngain=ngain,
                     zx=zx, dtg=dtg, xbc=xbc, yn=yn, yfull=yfull, hp=hp)
        h2 = rms_fwd(x1, W["ffn_norm"][i:i + 1], name="rms_ffn_fwd")
        gu = matmul(h2, W["ffn_w_gu"][i], name="mm_ffn_gu")
        a = swiglu_fwd(gu, name="swiglu_fwd")
        x2 = matmul(a, W["ffn_w_down"][i], residual=x1, name="mm_ffn_down")
        s.update(x1=x1, h2=h2, gu=gu, a=a)
        saved.append(s)
        x = x2

    dx, loss = loss_head(x, target, name="loss_head")

    gw = {k: {} for k in WEIGHTS}
    for i in reversed(range(depth)):
        kind, j = i % 3, i // 3
        s = saved[i]
        da = matmul(dx, W["ffn_w_down"][i], tb=True, name="mm_ffn_da")
        gw["ffn_w_down"][i] = matmul(s["a"], dx, ta=True, out_dtype=BF16, name="mm_ffn_dwdown")
        dgu = swiglu_bwd(s["gu"], da, name="swiglu_bwd")
        dh2 = matmul(dgu, W["ffn_w_gu"][i], tb=True, name="mm_ffn_dh")
        gw["ffn_w_gu"][i] = matmul(s["h2"], dgu, ta=True, out_dtype=BF16, name="mm_ffn_dwgu")
        dx1, dgn = rms_bwd(s["x1"], W["ffn_norm"][i:i + 1], dh2, dx, name="rms_ffn_bwd")
        gw["ffn_norm"][i] = dgn[0]
        if kind == 0:
            do = matmul(dx1, W["sb_w_o"][j], tb=True, name="mm_sb_do")
            gw["sb_w_o"][j] = matmul(s["o"], dx1, ta=True, out_dtype=BF16, name="mm_sb_dwo")
            if late is not None and i == 0:
                dqkv, dqg, dkg, received = sb_attn_bwd(
                    s["qkv"], s["o"], do, W["sb_q_gain"][j:j + 1], W["sb_k_gain"][j:j + 1], name="sb_bwd_scatter",
                    scatter=late.contributions(gw))
            else:
                dqkv, dqg, dkg = sb_attn_bwd(s["qkv"], s["o"], do, W["sb_q_gain"][j:j + 1], W["sb_k_gain"][j:j + 1],
                                             name="sb_bwd")
            gw["sb_q_gain"][j] = dqg[0]
            gw["sb_k_gain"][j] = dkg[0]
            dh = matmul(dqkv, W["sb_w_qkv"][j], tb=True, name="mm_sb_dh")
            gw["sb_w_qkv"][j] = matmul(s["h"], dqkv, ta=True, out_dtype=BF16, name="mm_sb_dwqkv")
        elif kind == 1:
            dp = matmul(dx1, W["gm_w_out"][j], tb=True, name="mm_gm_dp")
            gw["gm_w_out"][j] = matmul(s["p"], dx1, ta=True, out_dtype=BF16, name="mm_gm_dwout")
            dzz, db_in, dvg, dws, dbs = gmlp_bwd(s["zzpre"], W["gm_b_in"][j:j + 1], W["gm_v_gain"][j:j + 1],
                                                s["wc"], s["bsf"], dp, name="gm_bwd")
            gw["gm_b_in"][j] = db_in[0]
            gw["gm_v_gain"][j] = dvg[0]
            gw["gm_w_s"][j] = dws
            gw["gm_b_s"][j] = dbs[:, :gm_groups].T
            dh = matmul(dzz, W["gm_w_in"][j], tb=True, name="mm_gm_dh")
            gw["gm_w_in"][j] = matmul(s["h"], dzz, ta=True, out_dtype=BF16, name="mm_gm_dwin")
        else:
            conv_dim = W["ssm_conv_w"].shape[2]
            G = (conv_dim - inner) // (2 * SSM_STATE)
            hpg = heads // G
            dyn = matmul(dx1, W["ssm_w_out"][j], tb=True, name="mm_ssm_dyn")
            gw["ssm_w_out"][j] = matmul(s["yn"], dx1, ta=True, out_dtype=BF16, name="mm_ssm_dwout")
            dz, dxs, dbm, dcm, ddt, dbias, dalog, dd, dng = ssd_bwd(
                s["xbc"], s["zx"], s["dtg"], s["bias_g"], s["alog_g"], s["d_chan"], s["ngain"], s["yfull"], s["hp"],
                dyn, L, G, name="ssd_bwd")
            dxbc = jnp.concatenate([dxs, dbm, dcm], axis=1)
            dpre, dcw, dcb = conv_bwd(s["zx"], W["ssm_conv_w"][j], W["ssm_conv_b"][j:j + 1], inner, dxbc,
                                      name="conv_bwd")
            dzx = jnp.concatenate([dz, dpre], axis=1)
            dh = matmul(ddt, s["w_dtg"], tb=True, name="mm_ssm_dh_dt")
            dh = matmul(dzx, s["w_zx"], tb=True, residual=dh, name="mm_ssm_dh")
            dw_zx = matmul(s["h"], dzx, ta=True, out_dtype=BF16, name="mm_ssm_dwzx")
            dw_dtg = matmul(s["h"], ddt, ta=True, out_dtype=BF16, name="mm_ssm_dwdt")
            dw_dt = dw_dtg.reshape(D, G, LANES)[:, :, :hpg].reshape(D, heads)
            gw["ssm_w_in"][j] = jnp.concatenate([dw_zx, dw_dt], axis=1)
            gw["ssm_conv_w"][j] = dcw
            gw["ssm_conv_b"][j] = dcb[0]
            gw["ssm_dt_bias"][j] = dbias[:, 0, :hpg].reshape(heads)
            gw["ssm_a_log"][j] = dalog[:, 0, :hpg].reshape(heads)
            gw["ssm_d"][j] = dd[:, 0, :hpg].reshape(heads)
            gw["ssm_norm_gain"][j] = dng[0]
        dx, dgn = rms_bwd(s["x"], W["mix_norm"][i:i + 1], dh, dx1, name="rms_mix_bwd")
        gw["mix_norm"][i] = dgn[0]

    return loss, dx, gw, received


MESH = pl.DeviceIdType.MESH
HBM_SPEC = pl.BlockSpec(memory_space=pltpu.HBM)
VMEM_SPEC = pl.BlockSpec(memory_space=pltpu.VMEM)


def _my_position():
    return lax.axis_index("x"), lax.axis_index("y"), lax.axis_index("c")


def _flip(v, bit):
    return 1 - v if bit else v


def all_gather_packed(shard, *, name):
    R, C = shard.shape

    def body(x_ref, out_ref, send_sems, recv_sems, local_sem):
        refs = (x_ref, out_ref, send_sems, recv_sems, local_sem)
        _ag_phase(refs, "start")
        _ag_phase(refs, "forward")
        _ag_phase(refs, "finish")

    return pl.pallas_call(
        body, name=name, out_shape=jax.ShapeDtypeStruct((N_DEV, R, C), shard.dtype),
        in_specs=[HBM_SPEC], out_specs=HBM_SPEC, scratch_shapes=COMM_SEMAPHORES,
    )(shard)


COMM_SEMAPHORES = [pltpu.SemaphoreType.DMA((7,)), pltpu.SemaphoreType.DMA((7,)), pltpu.SemaphoreType.DMA]


def _ag_phase(refs, phase):
    x_ref, out_ref, send_sems, recv_sems, local_sem = refs
    x, y, c = _my_position()
    me, sibling = (x, y, c), (x, y, 1 - c)
    chips = [(1 - x, y), (x, 1 - y), (1 - x, 1 - y)]

    def slot(px, py, pc):
        return out_ref.at[4 * px + 2 * py + pc]

    def copy(k, block, to, src=None):
        return pltpu.make_async_remote_copy(
            src_ref=slot(*block) if src is None else src, dst_ref=slot(*block),
            send_sem=send_sems.at[k], recv_sem=recv_sems.at[k], device_id=to, device_id_type=MESH)

    mine = pltpu.make_async_copy(x_ref, slot(*me), local_sem)
    first = [copy(0, me, sibling, src=x_ref)]
    first += [copy(1 + j, me, (*chip, c), src=x_ref) for j, chip in enumerate(chips)]
    passed = [copy(4 + j, (*chip, c), sibling) for j, chip in enumerate(chips)]
    if phase == "start":
        mine.start()
        for cp in first:
            cp.start()
    elif phase == "forward":
        for j, chip in enumerate(chips):
            copy(1 + j, (*chip, c), me).wait_recv()
            passed[j].start()
    else:
        copy(0, sibling, me).wait_recv()
        for j, chip in enumerate(chips):
            copy(4 + j, (*chip, 1 - c), me).wait_recv()
        for cp in first + passed:
            cp.wait_send()
        mine.wait()


def _rs_phase(refs, phase):
    g_ref, out_ref, send_sems, recv_sems, local_sem = refs
    x, y, c = _my_position()
    me = 4 * x + 2 * y + c
    mine = pltpu.make_async_copy(g_ref.at[me], out_ref.at[me], local_sem)
    copies = []
    for k in range(1, N_DEV):
        px, py, pc = _flip(x, k & 4), _flip(y, k & 2), _flip(c, k & 1)
        copies.append(pltpu.make_async_remote_copy(
            src_ref=g_ref.at[4 * px + 2 * py + pc], dst_ref=out_ref.at[me],
            send_sem=send_sems.at[k - 1], recv_sem=recv_sems.at[k - 1], device_id=(px, py, pc), device_id_type=MESH))
    if phase == "start":
        mine.start()
        for cp in copies:
            cp.start()
    else:
        for cp in copies:
            cp.wait()
        mine.wait()


def exchange_for_reduce_scatter(g, *, name):
    def body(g_ref, out_ref, send_sems, recv_sems, local_sem):
        refs = (g_ref, out_ref, send_sems, recv_sems, local_sem)
        _rs_phase(refs, "start")
        _rs_phase(refs, "finish")

    return pl.pallas_call(
        body, name=name, out_shape=jax.ShapeDtypeStruct(g.shape, g.dtype), in_specs=[HBM_SPEC], out_specs=HBM_SPEC,
        scratch_shapes=COMM_SEMAPHORES,
    )(g)


def sum_slots(recv, *, name):
    n, R, C = recv.shape
    tr = _pick(R, (512, 256, 128))

    def body(r_ref, o_ref):
        acc = r_ref[0].astype(F32)
        for s in range(1, n):
            acc = acc + r_ref[s].astype(F32)
        o_ref[...] = acc

    return pl.pallas_call(
        body, name=name, grid=(R // tr,), in_specs=[pl.BlockSpec((n, tr, C), lambda i: (0, i, 0))],
        out_specs=pl.BlockSpec((tr, C), lambda i: (i, 0)), out_shape=jax.ShapeDtypeStruct((R, C), F32),
        compiler_params=_params(("parallel",)),
    )(recv)


def all_reduce_small(v, *, name):
    R, C = v.shape

    def body(v_ref, o_ref, buf, send_sems, recv_sems):
        x, y, c = _my_position()
        me = 4 * x + 2 * y + c
        buf[me] = v_ref[...]
        copies = []
        for k in range(1, N_DEV):
            px, py, pc = _flip(x, k & 4), _flip(y, k & 2), _flip(c, k & 1)
            copies.append(pltpu.make_async_remote_copy(
                src_ref=v_ref, dst_ref=buf.at[me], send_sem=send_sems.at[k - 1], recv_sem=recv_sems.at[k - 1],
                device_id=(px, py, pc), device_id_type=MESH))
        for cp in copies:
            cp.start()
        for cp in copies:
            cp.wait()
        acc = buf[0]
        for s in range(1, N_DEV):
            acc = acc + buf[s]
        o_ref[...] = acc

    return pl.pallas_call(
        body, name=name, out_shape=jax.ShapeDtypeStruct((R, C), F32), in_specs=[VMEM_SPEC], out_specs=VMEM_SPEC,
        scratch_shapes=[pltpu.VMEM((N_DEV, R, C), F32), pltpu.SemaphoreType.DMA((7,)), pltpu.SemaphoreType.DMA((7,))],
        compiler_params=pltpu.CompilerParams(vmem_limit_bytes=VMEM_LIMIT_BYTES),
    )(v)


def _pad_rows(a, mult):
    pad = (-a.shape[0]) % mult
    return jnp.pad(a, ((0, pad), (0, 0))) if pad else a


def _pack_rows(arrays, cols, row_mult):
    return _pad_rows(jnp.concatenate([a.reshape(-1, cols) for a in arrays], axis=0), row_mult)


def _unpack_rows(packed, shapes):
    out, r = [], 0
    cols = packed.shape[-1]
    for shp in shapes:
        n = math.prod(shp) // cols
        out.append(packed[..., r:r + n, :].reshape(packed.shape[:-2] + tuple(shp)))
        r += n
    return out


def _gathered_to_full(g, name):
    if name in COL_SHARDED or name == "ssm_conv_w":
        return jnp.moveaxis(g, 0, -2).reshape(g.shape[1:-1] + (N_DEV * g.shape[-1],))
    if name in ("ssm_conv_b", "ssm_norm_gain"):
        return jnp.moveaxis(g, 0, -2).reshape(g.shape[1:-1] + (N_DEV * g.shape[-1],))
    return jnp.moveaxis(g, 0, 1).reshape((g.shape[1], N_DEV * g.shape[2], g.shape[3]))


def _full_to_shards(full, name):
    if name in COL_SHARDED:
        n = full.shape[-1] // N_DEV
        return jnp.moveaxis(full.reshape(full.shape[:-1] + (N_DEV, n)), -2, 0)
    k = full.shape[1] // N_DEV
    return jnp.moveaxis(full.reshape((full.shape[0], N_DEV, k, full.shape[2])), 1, 0)


def _pack_small(arrays):
    flat = []
    for a in arrays:
        f = a.reshape(-1).astype(F32)
        flat.append(jnp.pad(f, (0, (-f.shape[0]) % LANES)))
    return _pad_rows(jnp.concatenate(flat).reshape(-1, LANES), 8)


def _unpack_small(packed, shapes):
    flat = packed.reshape(-1)
    out, r = [], 0
    for shp in shapes:
        n = math.prod(shp)
        out.append(flat[r:r + n].reshape(shp))
        r += n + (-n) % LANES
    return out


ARG_NAMES = ("x",) + WEIGHTS + ("loss_target",) + tuple("m_" + w for w in WEIGHTS) + tuple("v_" + w for w in WEIGHTS)


def kernel(x, mix_norm, ffn_norm, sb_w_qkv, sb_q_gain, sb_k_gain, sb_w_o, gm_w_in, gm_b_in, gm_v_gain, gm_w_s, gm_b_s, gm_w_out, ssm_w_in, ssm_conv_w, ssm_conv_b, ssm_dt_bias, ssm_a_log, ssm_d, ssm_norm_gain, ssm_w_out, ffn_w_gu, ffn_w_down, loss_target, m_mix_norm, m_ffn_norm, m_sb_w_qkv, m_sb_q_gain, m_sb_k_gain, m_sb_w_o, m_gm_w_in, m_gm_b_in, m_gm_v_gain, m_gm_w_s, m_gm_b_s, m_gm_w_out, m_ssm_w_in, m_ssm_conv_w, m_ssm_conv_b, m_ssm_dt_bias, m_ssm_a_log, m_ssm_d, m_ssm_norm_gain, m_ssm_w_out, m_ffn_w_gu, m_ffn_w_down, v_mix_norm, v_ffn_norm, v_sb_w_qkv, v_sb_q_gain, v_sb_k_gain, v_sb_w_o, v_gm_w_in, v_gm_b_in, v_gm_v_gain, v_gm_w_s, v_gm_b_s, v_gm_w_out, v_ssm_w_in, v_ssm_conv_w, v_ssm_conv_b, v_ssm_dt_bias, v_ssm_a_log, v_ssm_d, v_ssm_norm_gain, v_ssm_w_out, v_ffn_w_gu, v_ffn_w_down):
    given = dict(zip(ARG_NAMES, (x, mix_norm, ffn_norm, sb_w_qkv, sb_q_gain, sb_k_gain, sb_w_o, gm_w_in, gm_b_in, gm_v_gain, gm_w_s, gm_b_s, gm_w_out, ssm_w_in, ssm_conv_w, ssm_conv_b, ssm_dt_bias, ssm_a_log, ssm_d, ssm_norm_gain, ssm_w_out, ffn_w_gu, ffn_w_down, loss_target, m_mix_norm, m_ffn_norm, m_sb_w_qkv, m_sb_q_gain, m_sb_k_gain, m_sb_w_o, m_gm_w_in, m_gm_b_in, m_gm_v_gain, m_gm_w_s, m_gm_b_s, m_gm_w_out, m_ssm_w_in, m_ssm_conv_w, m_ssm_conv_b, m_ssm_dt_bias, m_ssm_a_log, m_ssm_d, m_ssm_norm_gain, m_ssm_w_out, m_ffn_w_gu, m_ffn_w_down, v_mix_norm, v_ffn_norm, v_sb_w_qkv, v_sb_q_gain, v_sb_k_gain, v_sb_w_o, v_gm_w_in, v_gm_b_in, v_gm_v_gain, v_gm_w_s, v_gm_b_s, v_gm_w_out, v_ssm_w_in, v_ssm_conv_w, v_ssm_conv_b, v_ssm_dt_bias, v_ssm_a_log, v_ssm_d, v_ssm_norm_gain, v_ssm_w_out, v_ffn_w_gu, v_ffn_w_down)))
    mx, my, mc = _my_position()
    me = 4 * mx + 2 * my + mc

    pieces = [(k, l) for k in BIG for l in range(given[k].shape[0])]
    early = [("sb_w_qkv", 0), ("sb_w_o", 0), ("ffn_w_gu", 0), ("ffn_w_down", 0)]
    late_pieces = [p for p in pieces if p not in early]
    last = [("sb_w_qkv", 0)]
    main = [p for p in pieces if p not in last]
    row_mult = 256

    def rows_of(p):
        return math.prod(given[p[0]].shape[1:]) // PACK_COLS

    def pack_shards(ps, prefix="", dtype=F32, extra=()):
        parts = [given[prefix + k][l].astype(dtype).reshape(-1, PACK_COLS) for k, l in ps] + list(extra)
        return _pad_rows(jnp.concatenate(parts, axis=0), row_mult)

    def split_rows(packed, ps):
        out, r = [], 0
        for p in ps:
            out.append(packed[..., r:r + rows_of(p), :])
            r += rows_of(p)
        return out

    def piece_to_full(g, k):
        shp = given[k].shape[1:]
        g = g.reshape((N_DEV,) + shp)
        if k in COL_SHARDED:
            return jnp.moveaxis(g, 0, 1).reshape(shp[0], N_DEV * shp[1])
        return g.reshape(N_DEV * shp[0], shp[1])

    def full_to_piece(full, k):
        shp = given[k].shape[1:]
        g = jnp.moveaxis(full.reshape(shp[0], N_DEV, shp[1]), 1, 0) if k in COL_SHARDED else full
        return g.reshape(N_DEV, -1, PACK_COLS)

    def contributions(gw, ps):
        c = jnp.concatenate([full_to_piece(gw[k][l], k) for k, l in ps], axis=1)
        return jnp.pad(c, ((0, 0), (0, (-c.shape[1]) % row_mult), (0, 0)))

    sharded_small = [lax.bitcast_convert_type(given[k], BF16) for k in SMALL_SHARDED]
    tail = jnp.concatenate([a.reshape(-1) for a in sharded_small])
    tail = jnp.pad(tail, (0, (-tail.size) % PACK_COLS)).reshape(-1, PACK_COLS)

    W = {k: given[k] for k in SMALL if k not in SMALL_SHARDED}
    W.update({k: [None] * given[k].shape[0] for k in BIG})
    gathered_early = all_gather_packed(pack_shards(early, dtype=BF16), name="all_gather_early")
    for (k, l), g in zip(early, split_rows(gathered_early, early)):
        W[k][l] = piece_to_full(g, k)

    class Late:
        shard = pack_shards(late_pieces, dtype=BF16, extra=[tail])

        @staticmethod
        def fill(weights, gathered):
            for (k, l), g in zip(late_pieces, split_rows(gathered, late_pieces)):
                weights[k][l] = piece_to_full(g, k)
            r0 = sum(rows_of(p) for p in late_pieces)
            tail_g = gathered[:, r0:r0 + tail.shape[0], :].reshape(N_DEV, -1)
            off = 0
            for k, a in zip(SMALL_SHARDED, sharded_small):
                g = lax.bitcast_convert_type(tail_g[:, off:off + a.size].reshape((N_DEV,) + a.shape), F32)
                weights[k] = jnp.moveaxis(g, 0, -2).reshape(g.shape[1:-1] + (N_DEV * g.shape[-1],))
                off += a.size

        @staticmethod
        def contributions(gw):
            return contributions(gw, main)

    loss, gx, gw, received_main = local_step(given["x"][0], given["loss_target"][0], W, late=Late)
    received_last = exchange_for_reduce_scatter(contributions(gw, last), name="reduce_scatter_last")

    grads_small = {k: jnp.stack([gw[k][l] for l in sorted(gw[k])], axis=0) for k in SMALL}
    small_shapes = [grads_small[k].shape for k in SMALL] + [(1, 1)]
    reduced = all_reduce_small(_pack_small([grads_small[k] for k in SMALL] + [loss]), name="all_reduce_small")
    small_full = dict(zip(SMALL + ("loss",), _unpack_small(reduced, small_shapes)))

    by_piece = {}
    for grp, received, tag in ((main, received_main, "main"), (last, received_last, "last")):
        w = pack_shards(grp)
        g = sum_slots(received, name="reduce_scatter_sum_" + tag)
        res = (g,) + tuple(adamw(w, g, pack_shards(grp, "m_"), pack_shards(grp, "v_"), name="adamw_" + tag))
        for n, packed in enumerate(res):
            for p, rows in zip(grp, split_rows(packed, grp)):
                by_piece[(n,) + p] = rows.reshape(given[p[0]].shape[1:])
    out_g, out_d, out_m, out_v = ({k: jnp.stack([by_piece[(n, k, l)] for l in range(given[k].shape[0])], axis=0)
                                   for k in BIG} for n in range(4))

    gsmall = {}
    for k in SMALL:
        g = small_full[k]
        if k in SMALL_SHARDED:
            n = given[k].shape[-1]
            g = lax.dynamic_slice_in_dim(g, me * n, n, axis=g.ndim - 1)
        gsmall[k] = g
    local_shapes = [given[k].shape for k in SMALL]
    dsm, nmsm, nvsm = adamw(_pack_small([given[k] for k in SMALL]), _pack_small([gsmall[k] for k in SMALL]),
                            _pack_small([given["m_" + k] for k in SMALL]), _pack_small([given["v_" + k] for k in SMALL]),
                            name="adamw_small")
    out_g.update(gsmall)
    for dst, src in ((out_d, dsm), (out_m, nmsm), (out_v, nvsm)):
        dst.update(zip(SMALL, _unpack_small(src, local_shapes)))

    return (small_full["loss"].reshape(()), gx[None],
            *[out_g[k] for k in WEIGHTS], *[out_d[k] for k in WEIGHTS],
            *[out_m[k] for k in WEIGHTS], *[out_v[k] for k in WEIGHTS])
```

```python
import math

import jax
import jax.numpy as jnp
from jax import lax
from jax.experimental import pallas as pl
from jax.experimental.pallas import tpu as pltpu

F32 = jnp.float32
BF16 = jnp.bfloat16
EPS = 1e-6
N_DEV = 8
SB_HEAD_DIM = 64
SSM_STATE = 128
SSM_CONV = 4
ADAM_LR = 0.001
ADAM_B1 = 0.9
ADAM_B2 = 0.999
ADAM_EPS = 1e-08
ADAM_WD = 0.01
ADAM_STEP = 10
VMEM_LIMIT_BYTES = 56 * 1024 * 1024
LANES = 128
PACK_COLS = 1024

BIG = ("sb_w_qkv", "sb_w_o", "gm_w_in", "gm_w_out", "ssm_w_in", "ssm_w_out", "ffn_w_gu", "ffn_w_down")
COL_SHARDED = ("sb_w_qkv", "gm_w_in", "ssm_w_in", "ffn_w_gu")
SMALL = ("mix_norm", "ffn_norm", "sb_q_gain", "sb_k_gain", "gm_b_in", "gm_v_gain", "gm_w_s", "gm_b_s",
         "ssm_conv_w", "ssm_conv_b", "ssm_dt_bias", "ssm_a_log", "ssm_d", "ssm_norm_gain")
SMALL_SHARDED = ("ssm_conv_w", "ssm_conv_b", "ssm_norm_gain")
WEIGHTS = ("mix_norm", "ffn_norm", "sb_w_qkv", "sb_q_gain", "sb_k_gain", "sb_w_o", "gm_w_in", "gm_b_in",
           "gm_v_gain", "gm_w_s", "gm_b_s", "gm_w_out", "ssm_w_in", "ssm_conv_w", "ssm_conv_b", "ssm_dt_bias",
           "ssm_a_log", "ssm_d", "ssm_norm_gain", "ssm_w_out", "ffn_w_gu", "ffn_w_down")


def _params(semantics=None):
    return pltpu.CompilerParams(dimension_semantics=semantics, vmem_limit_bytes=VMEM_LIMIT_BYTES)


def _pick(n, prefs):
    for t in prefs:
        if t <= n and n % t == 0:
            return t
    return n


def _dot(a, b, ca=1, cb=0):
    return lax.dot_general(a, b, (((ca,), (cb,)), ((), ())), preferred_element_type=F32)


def _split3(v):
    h1 = v.astype(BF16)
    r1 = v - h1.astype(F32)
    h2 = r1.astype(BF16)
    h3 = (r1 - h2.astype(F32)).astype(BF16)
    return h1, h2, h3


def _dot_exact_left(mat01, v):
    h1, h2, h3 = _split3(v)
    return _dot(mat01, h1) + _dot(mat01, h2) + _dot(mat01, h3)


def _dot_split2_right(v, mat01):
    hi = v.astype(BF16)
    lo = (v - hi.astype(F32)).astype(BF16)
    return _dot(hi, mat01) + _dot(lo, mat01)


def _sigmoid(v):
    return 1.0 / (1.0 + jnp.exp(-v))


def _softplus(v):
    return jnp.maximum(v, 0.0) + jnp.log(1.0 + jnp.exp(-jnp.abs(v)))


def _erf(v):
    a = jnp.abs(v)
    t = 1.0 / (1.0 + 0.3275911 * a)
    poly = t * (0.254829592 + t * (-0.284496736 + t * (1.421413741 + t * (-1.453152027 + t * 1.061405429))))
    e = 1.0 - poly * jnp.exp(-a * a)
    return jnp.where(v < 0, -e, e)


def _gelu_and_grad(v):
    cdf = 0.5 * (1.0 + _erf(v * (1.0 / math.sqrt(2.0))))
    pdf = jnp.exp(-0.5 * v * v) * (1.0 / math.sqrt(2.0 * math.pi))
    return v * cdf, cdf + v * pdf


def matmul(a, b, *, ta=False, tb=False, out_dtype=F32, residual=None, name):
    if ta:
        K, M = a.shape
    else:
        M, K = a.shape
    if tb:
        N, Kb = b.shape
    else:
        Kb, N = b.shape
    assert K == Kb, (a.shape, b.shape, ta, tb)
    tm = _pick(M, (1024, 1408, 768, 512, 256, 128))
    tn = _pick(N, (1024, 1408, 1536, 768, 512, 256, 128))
    tk = _pick(K, (512, 1408, 256, 128))
    nk = K // tk
    a_spec = pl.BlockSpec((tk, tm), lambda i, j, k: (k, i)) if ta else pl.BlockSpec((tm, tk), lambda i, j, k: (i, k))
    b_spec = pl.BlockSpec((tn, tk), lambda i, j, k: (j, k)) if tb else pl.BlockSpec((tk, tn), lambda i, j, k: (k, j))
    o_spec = pl.BlockSpec((tm, tn), lambda i, j, k: (i, j))
    ca, cb = (0 if ta else 1), (1 if tb else 0)
    has_res = residual is not None

    def body(*refs):
        if has_res:
            a_ref, b_ref, r_ref, o_ref, acc = refs
        else:
            a_ref, b_ref, o_ref, acc = refs
        k = pl.program_id(2)

        @pl.when(k == 0)
        def _():
            acc[...] = jnp.zeros_like(acc)

        acc[...] += _dot(a_ref[...].astype(BF16), b_ref[...].astype(BF16), ca, cb)

        @pl.when(k == nk - 1)
        def _():
            r = acc[...]
            if has_res:
                r = r + r_ref[...]
            o_ref[...] = r.astype(out_dtype)

    in_specs = [a_spec, b_spec] + ([o_spec] if has_res else [])
    args = (a, b) + ((residual,) if has_res else ())
    return pl.pallas_call(
        body, name=name, grid=(M // tm, N // tn, nk), in_specs=in_specs, out_specs=o_spec,
        out_shape=jax.ShapeDtypeStruct((M, N), out_dtype), scratch_shapes=[pltpu.VMEM((tm, tn), F32)],
        compiler_params=_params(("parallel", "parallel", "arbitrary")),
    )(*args)


def rms_fwd(x, gain, *, name):
    S, D = x.shape
    tr = _pick(S, (512, 256, 128))

    def body(x_ref, g_ref, o_ref):
        xv = x_ref[...]
        r = lax.rsqrt(jnp.mean(xv * xv, axis=1, keepdims=True) + EPS)
        o_ref[...] = (xv * r * g_ref[...]).astype(BF16)

    return pl.pallas_call(
        body, name=name, grid=(S // tr,),
        in_specs=[pl.BlockSpec((tr, D), lambda i: (i, 0)), pl.BlockSpec((1, D), lambda i: (0, 0))],
        out_specs=pl.BlockSpec((tr, D), lambda i: (i, 0)), out_shape=jax.ShapeDtypeStruct((S, D), BF16),
        compiler_params=_params(("parallel",)),
    )(x, gain)


def rms_bwd(x, gain, dh, dres, *, name):
    S, D = x.shape
    tr = _pick(S, (512, 256, 128))

    def body(x_ref, g_ref, dh_ref, dr_ref, dx_ref, dg_ref):
        @pl.when(pl.program_id(0) == 0)
        def _():
            dg_ref[...] = jnp.zeros_like(dg_ref)

        xv = x_ref[...]
        dhv = dh_ref[...]
        r = lax.rsqrt(jnp.mean(xv * xv, axis=1, keepdims=True) + EPS)
        xhat = xv * r
        t = dhv * g_ref[...]
        dx_ref[...] = dr_ref[...] + r * (t - xhat * jnp.mean(xhat * t, axis=1, keepdims=True))
        dg_ref[...] += jnp.sum(dhv * xhat, axis=0, keepdims=True)

    row = pl.BlockSpec((tr, D), lambda i: (i, 0))
    vec = pl.BlockSpec((1, D), lambda i: (0, 0))
    return pl.pallas_call(
        body, name=name, grid=(S // tr,), in_specs=[row, vec, row, row], out_specs=[row, vec],
        out_shape=[jax.ShapeDtypeStruct((S, D), F32), jax.ShapeDtypeStruct((1, D), F32)],
        compiler_params=_params(("arbitrary",)),
    )(x, gain, dh, dres)


def swiglu_fwd(gu, *, name):
    S, F2 = gu.shape
    F = F2 // 2
    tr = _pick(S, (256, 128))

    def body(gu_ref, o_ref):
        g = gu_ref[:, :F]
        u = gu_ref[:, F:]
        o_ref[...] = (g * _sigmoid(g) * u).astype(BF16)

    return pl.pallas_call(
        body, name=name, grid=(S // tr,), in_specs=[pl.BlockSpec((tr, F2), lambda i: (i, 0))],
        out_specs=pl.BlockSpec((tr, F), lambda i: (i, 0)), out_shape=jax.ShapeDtypeStruct((S, F), BF16),
        compiler_params=_params(("parallel",)),
    )(gu)


def swiglu_bwd(gu, da, *, name):
    S, F2 = gu.shape
    F = F2 // 2
    tr = _pick(S, (256, 128))

    def body(gu_ref, da_ref, o_ref):
        g = gu_ref[:, :F]
        u = gu_ref[:, F:]
        dav = da_ref[...]
        s = _sigmoid(g)
        o_ref[:, :F] = (dav * u * (s * (1.0 + g * (1.0 - s)))).astype(BF16)
        o_ref[:, F:] = (dav * g * s).astype(BF16)

    return pl.pallas_call(
        body, name=name, grid=(S // tr,),
        in_specs=[pl.BlockSpec((tr, F2), lambda i: (i, 0)), pl.BlockSpec((tr, F), lambda i: (i, 0))],
        out_specs=pl.BlockSpec((tr, F2), lambda i: (i, 0)), out_shape=jax.ShapeDtypeStruct((S, F2), BF16),
        compiler_params=_params(("parallel",)),
    )(gu, da)


def loss_head(y, target, *, name):
    S, D = y.shape
    tr = _pick(S, (512, 256, 128))

    def body(y_ref, t_ref, dy_ref, l_ref):
        @pl.when(pl.program_id(0) == 0)
        def _():
            l_ref[...] = jnp.zeros_like(l_ref)

        err = y_ref[...] - t_ref[...]
        dy_ref[...] = err * (1.0 / D)
        l_ref[...] += jnp.sum(0.5 * jnp.mean(err * err, axis=1, keepdims=True), axis=0, keepdims=True)

    row = pl.BlockSpec((tr, D), lambda i: (i, 0))
    one = pl.BlockSpec((1, 1), lambda i: (0, 0))
    dy, l = pl.pallas_call(
        body, name=name, grid=(S // tr,), in_specs=[row, row], out_specs=[row, one],
        out_shape=[jax.ShapeDtypeStruct((S, D), F32), jax.ShapeDtypeStruct((1, 1), F32)],
        compiler_params=_params(("arbitrary",)),
    )(y, target)
    return dy, l


def adamw(w, g, m, v, *, name):
    R, C = w.shape
    tr = _pick(R, (512, 256, 128, 64, 32, 16, 8))

    def body(w_ref, g_ref, m_ref, v_ref, d_ref, mo_ref, vo_ref):
        gv = g_ref[...]
        mn = ADAM_B1 * m_ref[...] + (1.0 - ADAM_B1) * gv
        vn = ADAM_B2 * v_ref[...] + (1.0 - ADAM_B2) * jnp.square(gv)
        m_hat = mn / (1.0 - ADAM_B1 ** ADAM_STEP)
        v_hat = vn / (1.0 - ADAM_B2 ** ADAM_STEP)
        d_ref[...] = -ADAM_LR * (m_hat / (jnp.sqrt(v_hat) + ADAM_EPS) + ADAM_WD * w_ref[...])
        mo_ref[...] = mn
        vo_ref[...] = vn

    blk = pl.BlockSpec((tr, C), lambda i: (i, 0))
    sds = jax.ShapeDtypeStruct((R, C), F32)
    return pl.pallas_call(
        body, name=name, grid=(R // tr,), in_specs=[blk] * 4, out_specs=[blk] * 3, out_shape=[sds] * 3,
        compiler_params=_params(("parallel",)),
    )(w, g, m, v)


def _tri(n, kind):
    r = lax.broadcasted_iota(jnp.int32, (n, n), 0)
    c = lax.broadcasted_iota(jnp.int32, (n, n), 1)
    if kind == "row_gt_col":
        return (r > c).astype(BF16)
    if kind == "row_ge_col":
        return (r >= c).astype(BF16)
    if kind == "row_le_col":
        return (r <= c).astype(BF16)
    raise ValueError(kind)


def _sb_tile(qs, kj, r_carry, u_strict, masked):
    z = _dot(qs, kj, 1, 1)
    lb = jnp.minimum(z, 0.0) - jnp.log(1.0 + jnp.exp(-jnp.abs(z)))
    l1m = lb - z
    keep = None
    if masked:
        tq, tk = z.shape
        keep = lax.broadcasted_iota(jnp.int32, (tq, tk), 1) < lax.broadcasted_iota(jnp.int32, (tq, tk), 0)
        l1m = jnp.where(keep, l1m, 0.0)
    w = jnp.exp(lb + _dot(l1m.astype(BF16), u_strict) + r_carry)
    if masked:
        w = jnp.where(keep, w, 0.0)
    return lb, l1m, w, keep


def _sb_prep(T, nb, hd, refs_in, gains, scratch):
    q_scale = 1.0 / math.sqrt(hd)
    assert math.log2(q_scale) == round(math.log2(q_scale))

    def prep(i, _):
        rows = pl.ds(pl.multiple_of(i * T, T), T)
        for hh in range(2):
            sl = slice(hd * hh, hd * hh + hd)
            for n, (src, dst) in enumerate(zip(refs_in, scratch)):
                v = src[rows, sl]
                if n < 2:
                    v = v * lax.rsqrt(jnp.mean(v * v, axis=1, keepdims=True) + EPS) * gains[n][...]
                if n == 0:
                    v = v * q_scale
                dst[hh, rows, :] = v.astype(BF16)
        return 0

    lax.fori_loop(0, nb, prep, 0)


def _sb_chains(m, T):
    rows = [pl.ds(pl.multiple_of((2 * m + qb) * T, T), T) for qb in range(2)]
    return rows, [(hh, qb) for qb in range(2) for hh in range(2)]


def sb_attn_fwd(qkv, q_gain, k_gain, *, name, gather=None):
    S, D3 = qkv.shape
    D = D3 // 3
    npairs = D // LANES
    hd = SB_HEAD_DIM
    T = min(256, S)
    nb = S // T
    assert nb % 2 == 0

    def body(*refs):
        if gather is None:
            q_ref, k_ref, v_ref, qg_ref, kg_ref, us_ref, o_ref, qn_s, kn_s, vb_s = refs
        else:
            q_ref, k_ref, v_ref, qg_ref, kg_ref, us_ref, ag_in, o_ref, ag_out, qn_s, kn_s, vb_s = refs[:12]
            comm = (ag_in, ag_out) + refs[12:]
            step = pl.program_id(0)
            pl.when(step == 0)(lambda: _ag_phase(comm, "start"))
            pl.when(step == (5 * npairs) // 8)(lambda: _ag_phase(comm, "forward"))
        us = us_ref[...]
        _sb_prep(T, nb, hd, (q_ref, k_ref, v_ref), (qg_ref, kg_ref), (qn_s, kn_s, vb_s))

        def superblock(m, _):
            rows_q, chains = _sb_chains(m, T)
            qs = {ch: qn_s[ch[0], rows_q[ch[1]], :] for ch in chains}

            def tile(j, carry, which):
                rows_j = pl.ds(pl.multiple_of(j * T, T), T)
                new = dict(carry)
                for ch, masked in which:
                    acc, rc = carry[ch]
                    _, l1m, w, _ = _sb_tile(qs[ch], kn_s[ch[0], rows_j, :], rc, us, masked)
                    new[ch] = (acc + _dot(w.astype(BF16), vb_s[ch[0], rows_j, :]),
                               rc + jnp.sum(l1m, axis=1, keepdims=True))
                return new

            carry = {ch: (jnp.zeros((T, hd), F32), jnp.zeros((T, 1), F32)) for ch in chains}
            carry = tile(2 * m + 1, carry, [(ch, True) for ch in chains if ch[1] == 1])
            carry = tile(2 * m, carry, [(ch, ch[1] == 0) for ch in chains])
            carry = lax.fori_loop(0, 2 * m, lambda jj, c: tile(2 * m - 1 - jj, c, [(ch, False) for ch in chains]),
                                  carry)
            for qb in range(2):
                o_ref[rows_q[qb], :] = jnp.concatenate([carry[(0, qb)][0], carry[(1, qb)][0]], axis=1)
            return 0

        lax.fori_loop(0, nb // 2, superblock, 0)
        if gather is not None:
            pl.when(step == npairs - 1)(lambda: _ag_phase(comm, "finish"))

    col = lambda off: pl.BlockSpec((S, LANES), lambda p, off=off: (0, off + p))
    gain = pl.BlockSpec((1, hd), lambda p: (0, 0))
    in_specs = [col(0), col(npairs), col(2 * npairs), gain, gain, pl.BlockSpec((T, T), lambda p: (0, 0))]
    out_specs = [pl.BlockSpec((S, LANES), lambda p: (0, p))]
    out_shape = [jax.ShapeDtypeStruct((S, D), F32)]
    scratch = [pltpu.VMEM((2, S, hd), BF16)] * 3
    args = [qkv, qkv, qkv, q_gain, k_gain, _tri(T, "row_gt_col")]
    if gather is not None:
        in_specs.append(HBM_SPEC)
        out_specs.append(HBM_SPEC)
        out_shape.append(jax.ShapeDtypeStruct((N_DEV,) + gather.shape, gather.dtype))
        scratch += COMM_SEMAPHORES
        args.append(gather)
    out = pl.pallas_call(
        body, name=name, grid=(npairs,), in_specs=in_specs, out_specs=out_specs, out_shape=out_shape,
        scratch_shapes=scratch, compiler_params=_params(("arbitrary",)),
    )(*args)
    return out[0] if gather is None else tuple(out)


def sb_attn_bwd(qkv, o, do, q_gain, k_gain, *, name, scatter=None):
    S, D3 = qkv.shape
    D = D3 // 3
    npairs = D // LANES
    hd = SB_HEAD_DIM
    T = min(256, S)
    nb = S // T
    scale = 1.0 / math.sqrt(hd)

    def body(*refs):
        if scatter is None:
            (q_ref, k_ref, v_ref, o_ref, do_ref, qg_ref, kg_ref, us_ref,
             dq_ref, dk_ref, dv_ref, dg_ref, qn_s, kn_s, vb_s, dob_s) = refs
        else:
            (q_ref, k_ref, v_ref, o_ref, do_ref, qg_ref, kg_ref, us_ref, rs_in,
             dq_ref, dk_ref, dv_ref, dg_ref, rs_out, qn_s, kn_s, vb_s, dob_s) = refs[:18]
            comm = (rs_in, rs_out) + refs[18:]
            pl.when(pl.program_id(0) == 0)(lambda: _rs_phase(comm, "start"))

        @pl.when(pl.program_id(0) == 0)
        def _():
            dg_ref[...] = jnp.zeros_like(dg_ref)

        us = us_ref[...]
        _sb_prep(T, nb, hd, (q_ref, k_ref, v_ref, do_ref), (qg_ref, kg_ref), (qn_s, kn_s, vb_s, dob_s))
        dk_ref[...] = jnp.zeros_like(dk_ref)
        dv_ref[...] = jnp.zeros_like(dv_ref)

        def superblock(m, _):
            rows_q, chains = _sb_chains(m, T)
            qs = {ch: qn_s[ch[0], rows_q[ch[1]], :] for ch in chains}
            doi = {ch: dob_s[ch[0], rows_q[ch[1]], :] for ch in chains}
            dt_total = {ch: jnp.sum(doi[ch].astype(F32) * o_ref[rows_q[ch[1]], hd * ch[0]:hd * ch[0] + hd],
                                    axis=1, keepdims=True) for ch in chains}

            def tile(j, carry, which):
                rows_j = pl.ds(pl.multiple_of(j * T, T), T)
                new = dict(carry)
                dk_part, dv_part = {}, {}
                for ch, masked in which:
                    hh = ch[0]
                    dq_acc, rc, gc = carry[ch]
                    kj = kn_s[hh, rows_j, :]
                    lb, l1m, w, keep = _sb_tile(qs[ch], kj, rc, us, masked)
                    wb = w.astype(BF16)
                    g = _dot(doi[ch], vb_s[hh, rows_j, :], 1, 1) * wb.astype(F32)
                    g_upto = dt_total[ch] - (gc + _dot_split2_right(g, us))
                    dz = g - g_upto * jnp.exp(lb)
                    if masked:
                        dz = jnp.where(keep, dz, 0.0)
                    dzb = dz.astype(BF16)
                    dv_part[hh] = dv_part.get(hh, 0.0) + _dot(wb, doi[ch], 0, 0)
                    dk_part[hh] = dk_part.get(hh, 0.0) + _dot(dzb, qs[ch], 0, 0)
                    new[ch] = (dq_acc + _dot(dzb, kj), rc + jnp.sum(l1m, axis=1, keepdims=True),
                               gc + jnp.sum(g, axis=1, keepdims=True))
                dv_ref[rows_j, :] += jnp.concatenate([dv_part[0], dv_part[1]], axis=1)
                dk_ref[rows_j, :] += jnp.concatenate([dk_part[0], dk_part[1]], axis=1)
                return new

            zero1 = jnp.zeros((T, 1), F32)
            carry = {ch: (jnp.zeros((T, hd), F32), zero1, zero1) for ch in chains}
            carry = tile(2 * m + 1, carry, [(ch, True) for ch in chains if ch[1] == 1])
            carry = tile(2 * m, carry, [(ch, ch[1] == 0) for ch in chains])
            carry = lax.fori_loop(0, 2 * m, lambda jj, c: tile(2 * m - 1 - jj, c, [(ch, False) for ch in chains]),
                                  carry)
            for qb in range(2):
                dq_ref[rows_q[qb], :] = jnp.concatenate([carry[(0, qb)][0], carry[(1, qb)][0]], axis=1) * scale
            return 0

        lax.fori_loop(0, nb // 2, superblock, 0)

        def finish(i, carry):
            rows = pl.ds(pl.multiple_of(i * T, T), T)
            new = []
            for hh in range(2):
                sl = slice(hd * hh, hd * hh + hd)
                outs = []
                for raw_ref, gain_ref, dn in ((q_ref, qg_ref, dq_ref[rows, sl]), (k_ref, kg_ref, dk_ref[rows, sl])):
                    raw = raw_ref[rows, sl]
                    r = lax.rsqrt(jnp.mean(raw * raw, axis=1, keepdims=True) + EPS)
                    hat = raw * r
                    t = dn * gain_ref[...]
                    outs.append((r * (t - hat * jnp.mean(hat * t, axis=1, keepdims=True)),
                                 jnp.sum(dn * hat, axis=0, keepdims=True)))
                dq_ref[rows, sl] = outs[0][0]
                dk_ref[rows, sl] = outs[1][0]
                new.append((carry[hh][0] + outs[0][1], carry[hh][1] + outs[1][1]))
            return tuple(new)

        zg = (jnp.zeros((1, hd), F32), jnp.zeros((1, hd), F32))
        tot = lax.fori_loop(0, nb, finish, (zg, zg))
        dg_ref[0:1, 0:hd] += tot[0][0] + tot[1][0]
        dg_ref[1:2, 0:hd] += tot[0][1] + tot[1][1]
        if scatter is not None:
            pl.when(pl.program_id(0) == npairs - 1)(lambda: _rs_phase(comm, "finish"))

    col = lambda off: pl.BlockSpec((S, LANES), lambda p, off=off: (0, off + p))
    gain = pl.BlockSpec((1, hd), lambda p: (0, 0))
    tri = pl.BlockSpec((T, T), lambda p: (0, 0))
    pair = pl.BlockSpec((S, LANES), lambda p: (0, p))
    in_specs = [col(0), col(npairs), col(2 * npairs), pair, pair, gain, gain, tri]
    out_specs = [pair, pair, pair, pl.BlockSpec((8, LANES), lambda p: (0, 0))]
    out_shape = [jax.ShapeDtypeStruct((S, D), F32)] * 3 + [jax.ShapeDtypeStruct((8, LANES), F32)]
    scratch = [pltpu.VMEM((2, S, hd), BF16)] * 4
    args = [qkv, qkv, qkv, o, do, q_gain, k_gain, _tri(T, "row_gt_col")]
    if scatter is not None:
        in_specs.append(HBM_SPEC)
        out_specs.append(HBM_SPEC)
        out_shape.append(jax.ShapeDtypeStruct(scatter.shape, scatter.dtype))
        scratch += COMM_SEMAPHORES
        args.append(scatter)
    out = pl.pallas_call(
        body, name=name, grid=(npairs,), in_specs=in_specs, out_specs=out_specs, out_shape=out_shape,
        scratch_shapes=scratch, compiler_params=_params(("arbitrary",)),
    )(*args)
    dq, dk, dv, dg = out[:4]
    res = (jnp.concatenate([dq, dk, dv], axis=1), dg[0:1, :hd], dg[1:2, :hd])
    return res if scatter is None else res + (out[4],)


def gmlp_fwd(zzpre, b_in, v_gain, wc, bsf, *, name):
    S, H2 = zzpre.shape
    H = H2 // 2
    G, T, _ = wc.shape
    gd = H // G

    def body(z_ref, b_ref, vg_ref, wc_ref, bs_ref, p_ref):
        zz, _ = _gelu_and_grad(z_ref[...] + b_ref[...])
        u = zz[:, :H]
        v = zz[:, H:]
        vn = v * lax.rsqrt(jnp.mean(v * v, axis=1, keepdims=True) + EPS) * vg_ref[...]
        for g in range(G):
            gs = slice(g * gd, (g + 1) * gd)
            mixed = _dot(wc_ref[g], vn[:, gs].astype(BF16)) + bs_ref[g]
            p_ref[:, gs] = (u[:, gs] * mixed).astype(BF16)

    full3 = lambda shp: pl.BlockSpec(shp, lambda c: (0, 0, 0))
    return pl.pallas_call(
        body, name=name, grid=(S // T,),
        in_specs=[pl.BlockSpec((T, H2), lambda c: (c, 0)), pl.BlockSpec((1, H2), lambda c: (0, 0)),
                  pl.BlockSpec((1, H), lambda c: (0, 0)), full3((G, T, T)), full3((G, T, gd))],
        out_specs=pl.BlockSpec((T, H), lambda c: (c, 0)), out_shape=jax.ShapeDtypeStruct((S, H), BF16),
        compiler_params=_params(("parallel",)),
    )(zzpre, b_in, v_gain, wc, bsf)


def gmlp_bwd(zzpre, b_in, v_gain, wc, bsf, dp, *, name):
    S, H2 = zzpre.shape
    H = H2 // 2
    G, T, _ = wc.shape
    gd = H // G
    assert G <= LANES

    def body(z_ref, b_ref, vg_ref, wc_ref, bs_ref, dp_ref, dzz_ref, db_ref, dvg_ref, dws_ref, dbs_ref):
        @pl.when(pl.program_id(0) == 0)
        def _():
            db_ref[...] = jnp.zeros_like(db_ref)
            dvg_ref[...] = jnp.zeros_like(dvg_ref)
            dws_ref[...] = jnp.zeros_like(dws_ref)
            dbs_ref[...] = jnp.zeros_like(dbs_ref)

        zz, gp = _gelu_and_grad(z_ref[...] + b_ref[...])
        u = zz[:, :H]
        v = zz[:, H:]
        r = lax.rsqrt(jnp.mean(v * v, axis=1, keepdims=True) + EPS)
        vhat = v * r
        vg = vg_ref[...]
        vn = vhat * vg
        dpv = dp_ref[...]
        tril = lax.broadcasted_iota(jnp.int32, (T, T), 1) <= lax.broadcasted_iota(jnp.int32, (T, T), 0)
        lane = lax.broadcasted_iota(jnp.int32, (T, LANES), 1)
        dbs = jnp.zeros((T, LANES), F32)
        du_parts, dvn_parts = [], []
        for g in range(G):
            gs = slice(g * gd, (g + 1) * gd)
            vng = vn[:, gs].astype(BF16)
            wcg = wc_ref[g]
            mixed = _dot(wcg, vng) + bs_ref[g]
            dpg = dpv[:, gs]
            du_parts.append(dpg * mixed)
            dmx = dpg * u[:, gs]
            dmxb = dmx.astype(BF16)
            dvn_parts.append(_dot(wcg, dmxb, 0, 0))
            dws_ref[g] += jnp.where(tril, _dot(dmxb, vng, 1, 1), 0.0)
            dbs = dbs + jnp.where(lane == g, jnp.sum(dmx, axis=1, keepdims=True), 0.0)
        dbs_ref[...] += dbs
        du = jnp.concatenate(du_parts, axis=1)
        dvn = jnp.concatenate(dvn_parts, axis=1)
        dvg_ref[...] += jnp.sum(dvn * vhat, axis=0, keepdims=True)
        t = dvn * vg
        dv = r * (t - vhat * jnp.mean(vhat * t, axis=1, keepdims=True))
        dzu = du * gp[:, :H]
        dzv = dv * gp[:, H:]
        dzz_ref[:, :H] = dzu.astype(BF16)
        dzz_ref[:, H:] = dzv.astype(BF16)
        db_ref[:, :H] += jnp.sum(dzu, axis=0, keepdims=True)
        db_ref[:, H:] += jnp.sum(dzv, axis=0, keepdims=True)

    full3 = lambda shp: pl.BlockSpec(shp, lambda c: (0, 0, 0))
    vec = lambda n: pl.BlockSpec((1, n), lambda c: (0, 0))
    return pl.pallas_call(
        body, name=name, grid=(S // T,),
        in_specs=[pl.BlockSpec((T, H2), lambda c: (c, 0)), vec(H2), vec(H), full3((G, T, T)), full3((G, T, gd)),
                  pl.BlockSpec((T, H), lambda c: (c, 0))],
        out_specs=[pl.BlockSpec((T, H2), lambda c: (c, 0)), vec(H2), vec(H), full3((G, T, T)),
                   pl.BlockSpec((T, LANES), lambda c: (0, 0))],
        out_shape=[jax.ShapeDtypeStruct((S, H2), BF16), jax.ShapeDtypeStruct((1, H2), F32),
                   jax.ShapeDtypeStruct((1, H), F32), jax.ShapeDtypeStruct((G, T, T), F32),
                   jax.ShapeDtypeStruct((T, LANES), F32)],
        compiler_params=_params(("arbitrary",)),
    )(zzpre, b_in, v_gain, wc, bsf, dp)


def _shift_rows(v, k, n_rows):
    if k == 0:
        return v
    rolled = pltpu.roll(v, k % n_rows, 0)
    row = lax.broadcasted_iota(jnp.int32, v.shape, 0)
    keep = (row >= k) if k > 0 else (row < n_rows + k)
    return jnp.where(keep, rolled, 0.0)


def conv_fwd(zx, conv_w, conv_b, col0, *, name):
    S = zx.shape[0]
    C = conv_w.shape[1]
    tc = _pick(C, (256, 128))
    off = col0 // tc
    assert col0 % tc == 0

    def body(x_ref, w_ref, b_ref, o_ref):
        xv = x_ref[...]
        acc = b_ref[...] + w_ref[SSM_CONV - 1:SSM_CONV, :] * xv
        for k in range(SSM_CONV - 1):
            acc = acc + w_ref[k:k + 1, :] * _shift_rows(xv, SSM_CONV - 1 - k, S)
        o_ref[...] = acc * _sigmoid(acc)

    return pl.pallas_call(
        body, name=name, grid=(C // tc,),
        in_specs=[pl.BlockSpec((S, tc), lambda j: (0, off + j)), pl.BlockSpec((SSM_CONV, tc), lambda j: (0, j)),
                  pl.BlockSpec((1, tc), lambda j: (0, j))],
        out_specs=pl.BlockSpec((S, tc), lambda j: (0, j)), out_shape=jax.ShapeDtypeStruct((S, C), F32),
        compiler_params=_params(("parallel",)),
    )(zx, conv_w, conv_b)


def conv_bwd(zx, conv_w, conv_b, col0, dout, *, name):
    S = zx.shape[0]
    C = conv_w.shape[1]
    tc = _pick(C, (256, 128))
    off = col0 // tc

    def body(x_ref, w_ref, b_ref, do_ref, dx_ref, dw_ref, db_ref):
        xv = x_ref[...]
        shifted = [_shift_rows(xv, SSM_CONV - 1 - k, S) for k in range(SSM_CONV)]
        acc = b_ref[...]
        for k in range(SSM_CONV):
            acc = acc + w_ref[k:k + 1, :] * shifted[k]
        s = _sigmoid(acc)
        dacc = do_ref[...] * (s * (1.0 + acc * (1.0 - s)))
        db_ref[...] = jnp.sum(dacc, axis=0, keepdims=True)
        dx = jnp.zeros_like(xv)
        for k in range(SSM_CONV):
            dw_ref[k:k + 1, :] = jnp.sum(dacc * shifted[k], axis=0, keepdims=True)
            dx = dx + w_ref[k:k + 1, :] * _shift_rows(dacc, -(SSM_CONV - 1 - k), S)
        dx_ref[...] = dx

    slab = pl.BlockSpec((S, tc), lambda j: (0, j))
    return pl.pallas_call(
        body, name=name, grid=(C // tc,),
        in_specs=[pl.BlockSpec((S, tc), lambda j: (0, off + j)), pl.BlockSpec((SSM_CONV, tc), lambda j: (0, j)),
                  pl.BlockSpec((1, tc), lambda j: (0, j)), slab],
        out_specs=[slab, pl.BlockSpec((SSM_CONV, tc), lambda j: (0, j)), pl.BlockSpec((1, tc), lambda j: (0, j))],
        out_shape=[jax.ShapeDtypeStruct((S, C), F32), jax.ShapeDtypeStruct((SSM_CONV, C), F32),
                   jax.ShapeDtypeStruct((1, C), F32)],
        compiler_params=_params(("parallel",)),
    )(zx, conv_w, conv_b, dout)


def _ssd_chunk_terms(dtraw, bias, a_log, tl):
    dt = _softplus(dtraw + bias)
    a_neg = -jnp.exp(a_log)
    ac = _dot_exact_left(tl, dt * a_neg)
    ac_last = ac[ac.shape[0] - 1:, :]
    return dt, a_neg, ac, ac.T, jnp.exp(ac), jnp.exp(ac_last - ac), jnp.exp(ac_last)


def _ssd_specs(S, L, G, hpg, pd, inner):
    gw = hpg * pd
    n = SSM_STATE
    xb = inner // n

    def mk(cidx):
        return dict(
            x=pl.BlockSpec((L, gw), lambda g, c: (cidx(c), g)),
            b=pl.BlockSpec((L, n), lambda g, c: (cidx(c), xb + g)),
            c=pl.BlockSpec((L, n), lambda g, c: (cidx(c), xb + G + g)),
            z=pl.BlockSpec((L, gw), lambda g, c: (cidx(c), g)),
            dt=pl.BlockSpec((L, LANES), lambda g, c: (cidx(c), g)),
            gvec=pl.BlockSpec((1, 1, LANES), lambda g, c: (g, 0, 0)),
            chan=pl.BlockSpec((1, gw), lambda g, c: (0, g)),
            tri=pl.BlockSpec((L, L), lambda g, c: (0, 0)),
            hp=pl.BlockSpec((1, 1, gw, n), lambda g, c: (g, cidx(c), 0, 0)),
            bc=pl.BlockSpec((L, n), lambda g, c: (cidx(c), g)),
        )
    return mk


def ssd_fwd(xbc, zx, dtg, bias_g, alog_g, d_chan, ngain, L, G, *, name):
    S = xbc.shape[0]
    n = SSM_STATE
    inner = xbc.shape[1] - 2 * G * n
    gw = inner // G
    pd = SB_HEAD_DIM
    hpg = gw // pd
    nc = S // L
    sp = _ssd_specs(S, L, G, hpg, pd, inner)(lambda c: c)

    def body(x_ref, b_ref, c_ref, z_ref, dt_ref, bias_ref, alog_ref, d_ref, ng_ref, tl_ref,
             yn_ref, y_ref, hp_ref, state):
        @pl.when(pl.program_id(1) == 0)
        def _():
            state[...] = jnp.zeros_like(state)

        dt, _, ac, act, ea, dte, cd = _ssd_chunk_terms(dt_ref[...], bias_ref[0], alog_ref[0], tl_ref[...])
        xv = x_ref[...]
        bm = b_ref[...].astype(BF16)
        cm = c_ref[...].astype(BF16)
        cb = _dot(cm, bm, 1, 1)
        tril = lax.broadcasted_iota(jnp.int32, (L, L), 1) <= lax.broadcasted_iota(jnp.int32, (L, L), 0)
        hp_ref[0, 0] = state[...]
        for r in range(hpg):
            ps = slice(r * pd, (r + 1) * pd)
            xr = xv[:, ps]
            xdt = xr * dt[:, r:r + 1]
            lm = jnp.exp(jnp.where(tril, ac[:, r:r + 1] - act[r:r + 1, :], -jnp.inf))
            hprev = state[ps, :]
            y = _dot((cb * lm).astype(BF16), xdt.astype(BF16))
            y = y + _dot(cm, hprev.astype(BF16), 1, 1) * ea[:, r:r + 1]
            y_ref[:, ps] = y + xr * d_ref[:, ps]
            st = _dot((xdt * dte[:, r:r + 1]).astype(BF16), bm, 0, 0)
            state[ps, :] = hprev * cd[:, r:r + 1] + st
        yfull = y_ref[...]
        zg = z_ref[...]
        yg = yfull * (zg * _sigmoid(zg))
        yn_ref[...] = (yg * lax.rsqrt(jnp.mean(yg * yg, axis=1, keepdims=True) + EPS) * ng_ref[...]).astype(BF16)

    return pl.pallas_call(
        body, name=name, grid=(G, nc),
        in_specs=[sp["x"], sp["b"], sp["c"], sp["z"], sp["dt"], sp["gvec"], sp["gvec"], sp["chan"], sp["chan"], sp["tri"]],
        out_specs=[sp["x"], sp["x"], sp["hp"]],
        out_shape=[jax.ShapeDtypeStruct((S, inner), BF16), jax.ShapeDtypeStruct((S, inner), F32),
                   jax.ShapeDtypeStruct((G, nc, gw, n), F32)],
        scratch_shapes=[pltpu.VMEM((gw, n), F32)],
        compiler_params=_params(("arbitrary", "arbitrary")),
    )(xbc, xbc, xbc, zx, dtg, bias_g, alog_g, d_chan, ngain, _tri(L, "row_ge_col"))


def ssd_bwd(xbc, zx, dtg, bias_g, alog_g, d_chan, ngain, yfull, hp, dyn, L, G, *, name):
    S = xbc.shape[0]
    n = SSM_STATE
    inner = xbc.shape[1] - 2 * G * n
    gw = inner // G
    pd = SB_HEAD_DIM
    hpg = gw // pd
    nc = S // L
    sp = _ssd_specs(S, L, G, hpg, pd, inner)(lambda c: nc - 1 - c)

    def body(x_ref, b_ref, c_ref, z_ref, dt_ref, bias_ref, alog_ref, d_ref, ng_ref, tl_ref, tu_ref,
             yf_ref, hp_ref, dyn_ref,
             dz_ref, dx_ref, db_ref, dc_ref, ddt_ref, dbias_ref, dalog_ref, dd_ref, dng_ref, dstate):
        first = pl.program_id(1) == 0

        @pl.when(first)
        def _():
            dstate[...] = jnp.zeros_like(dstate)
            dbias_ref[...] = jnp.zeros_like(dbias_ref)
            dalog_ref[...] = jnp.zeros_like(dalog_ref)
            dd_ref[...] = jnp.zeros_like(dd_ref)
            dng_ref[...] = jnp.zeros_like(dng_ref)

        dtraw = dt_ref[...]
        dt, a_neg, ac, act, ea, dte, cd = _ssd_chunk_terms(dtraw, bias_ref[0], alog_ref[0], tl_ref[...])
        xv = x_ref[...]
        bm = b_ref[...].astype(BF16)
        cm = c_ref[...].astype(BF16)
        cb = _dot(cm, bm, 1, 1)
        tril = lax.broadcasted_iota(jnp.int32, (L, L), 1) <= lax.broadcasted_iota(jnp.int32, (L, L), 0)
        lane = lax.broadcasted_iota(jnp.int32, (L, LANES), 1)
        lane1 = lax.broadcasted_iota(jnp.int32, (1, LANES), 1)

        yfull = yf_ref[...]
        zg = z_ref[...]
        sg = _sigmoid(zg)
        gate = zg * sg
        yg = yfull * gate
        rr = lax.rsqrt(jnp.mean(yg * yg, axis=1, keepdims=True) + EPS)
        yhat = yg * rr
        dynv = dyn_ref[...]
        dng_ref[...] += jnp.sum(dynv * yhat, axis=0, keepdims=True)
        t = dynv * ng_ref[...]
        dyg = rr * (t - yhat * jnp.mean(yhat * t, axis=1, keepdims=True))
        dy = dyg * gate
        dz_ref[...] = dyg * yfull * (sg * (1.0 + zg * (1.0 - sg)))

        dcb = jnp.zeros((L, L), F32)
        dc_acc = jnp.zeros((L, n), F32)
        db_acc = jnp.zeros((L, n), F32)
        dac = jnp.zeros((L, LANES), F32)
        xdx = jnp.zeros((L, LANES), F32)
        tail = jnp.zeros((1, LANES), F32)
        dskip = jnp.zeros((1, LANES), F32)
        ones_l = jnp.ones((L, LANES), BF16)
        for r in range(hpg):
            ps = slice(r * pd, (r + 1) * pd)
            xr = xv[:, ps]
            dyr = dy[:, ps]
            dtr = dt[:, r:r + 1]
            dter = dte[:, r:r + 1]
            cdr = cd[:, r:r + 1]
            xdt = xr * dtr
            xdtb = xdt.astype(BF16)
            dyrb = dyr.astype(BF16)
            lm = jnp.exp(jnp.where(tril, ac[:, r:r + 1] - act[r:r + 1, :], -jnp.inf))
            m32 = cb * lm
            mb = m32.astype(BF16)
            hprev = hp_ref[0, 0, ps, :]
            hpb = hprev.astype(BF16)
            dhn = dstate[ps, :]
            dhnb = dhn.astype(BF16)
            ear = ea[:, r:r + 1]
            gy = (dyr * ear).astype(BF16)
            dc_acc = dc_acc + _dot(gy, hpb)
            dstate[ps, :] = _dot(gy, cm, 0, 0) + dhn * cdr
            bdh = _dot(bm, dhnb, 1, 1)
            db_acc = db_acc + _dot((xdt * dter).astype(BF16), dhnb)
            dm = _dot(dyrb, xdtb, 1, 1)
            dxdt = bdh * dter + _dot(mb, dyrb, 0, 0)
            dcb = dcb + dm * lm
            wmat = dm * m32
            whi = wmat.astype(BF16)
            wlo = (wmat - whi.astype(F32)).astype(BF16)
            col_w = _dot(whi, ones_l, 0, 0) + _dot(wlo, ones_l, 0, 0)
            sdte = jnp.sum(xdt * bdh * dter, axis=1, keepdims=True)
            e_r = jnp.sum(wmat, axis=1, keepdims=True) + jnp.sum(dyr * _dot(cm, hpb, 1, 1) * ear, axis=1, keepdims=True) - sdte
            c_r = cdr * jnp.sum(jnp.sum(dhn * hprev, axis=1, keepdims=True), axis=0, keepdims=True) \
                + jnp.sum(sdte, axis=0, keepdims=True)
            dac = dac + jnp.where(lane == r, e_r - col_w, 0.0)
            xdx = xdx + jnp.where(lane == r, jnp.sum(dxdt * xr, axis=1, keepdims=True), 0.0)
            tail = tail + jnp.where(lane1 == r, c_r, 0.0)
            dskip = dskip + jnp.where(lane1 == r, jnp.sum(jnp.sum(dyr * xr, axis=1, keepdims=True), axis=0, keepdims=True), 0.0)
            dx_ref[:, ps] = dxdt * dtr + dyr * d_ref[:, ps]
        dcbb = dcb.astype(BF16)
        dc_ref[...] = dc_acc + _dot(dcbb, bm)
        db_ref[...] = db_acc + _dot(dcbb, cm, 0, 0)
        da = _dot_exact_left(tu_ref[...], dac) + tail
        real = lane < hpg
        ddt = jnp.where(real, (da * a_neg + xdx) * _sigmoid(dtraw + bias_ref[0]), 0.0)
        ddt_ref[...] = ddt
        dd_ref[0] += dskip
        dbias_ref[0] += jnp.sum(ddt, axis=0, keepdims=True)
        dalog_ref[0] += jnp.where(lane1 < hpg, jnp.sum(da * dt, axis=0, keepdims=True) * a_neg, 0.0)

    return pl.pallas_call(
        body, name=name, grid=(G, nc),
        in_specs=[sp["x"], sp["b"], sp["c"], sp["z"], sp["dt"], sp["gvec"], sp["gvec"], sp["chan"], sp["chan"],
                  sp["tri"], sp["tri"], sp["x"], sp["hp"], sp["x"]],
        out_specs=[sp["x"], sp["x"], sp["bc"], sp["bc"], sp["dt"], sp["gvec"], sp["gvec"], sp["gvec"], sp["chan"]],
        out_shape=[jax.ShapeDtypeStruct((S, inner), F32), jax.ShapeDtypeStruct((S, inner), F32),
                   jax.ShapeDtypeStruct((S, G * n), F32), jax.ShapeDtypeStruct((S, G * n), F32),
                   jax.ShapeDtypeStruct((S, G * LANES), F32), jax.ShapeDtypeStruct((G, 1, LANES), F32),
                   jax.ShapeDtypeStruct((G, 1, LANES), F32), jax.ShapeDtypeStruct((G, 1, LANES), F32),
                   jax.ShapeDtypeStruct((1, inner), F32)],
        scratch_shapes=[pltpu.VMEM((gw, n), F32)],
        compiler_params=_params(("arbitrary", "arbitrary")),
    )(xbc, xbc, xbc, zx, dtg, bias_g, alog_g, d_chan, ngain, _tri(L, "row_ge_col"), _tri(L, "row_le_col"),
      yfull, hp, dyn)


def _spread_dt(w_dt, G, hpg):
    K = w_dt.shape[0]
    w = w_dt.reshape(K, G, hpg)
    return jnp.pad(w, ((0, 0), (0, 0), (0, LANES - hpg))).reshape(K, G * LANES)


def _group_vec(v, G, hpg):
    return jnp.pad(v.reshape(G, 1, hpg), ((0, 0), (0, 0), (0, LANES - hpg)))


def local_step(x, target, W, late=None):
    S, D = x.shape
    depth = W["mix_norm"].shape[0]
    gm_groups, gm_chunk = W["gm_w_s"].shape[1], W["gm_w_s"].shape[2]
    heads = W["ssm_dt_bias"].shape[1]
    inner = heads * SB_HEAD_DIM
    L = gm_chunk
    received = None

    saved = []
    for i in range(depth):
        kind, j = i % 3, i // 3
        s = dict(x=x)
        h = rms_fwd(x, W["mix_norm"][i:i + 1], name=f"rms_mix_fwd")
        s["h"] = h
        if kind == 0:
            qkv = matmul(h, W["sb_w_qkv"][j], name="mm_qkv")
            if late is not None and i == 0:
                o, gathered = sb_attn_fwd(qkv, W["sb_q_gain"][j:j + 1], W["sb_k_gain"][j:j + 1], name="sb_fwd_gather",
                                          gather=late.shard)
                late.fill(W, gathered)
            else:
                o = sb_attn_fwd(qkv, W["sb_q_gain"][j:j + 1], W["sb_k_gain"][j:j + 1], name="sb_fwd")
            x1 = matmul(o, W["sb_w_o"][j], residual=x, name="mm_sb_out")
            s.update(qkv=qkv, o=o)
        elif kind == 1:
            wc = jnp.where(jnp.tril(jnp.ones((gm_chunk, gm_chunk), bool)), W["gm_w_s"][j], 0.0).astype(BF16)
            bsf = jnp.broadcast_to(W["gm_b_s"][j][:, :, None], (gm_groups, gm_chunk, W["gm_v_gain"].shape[1] // gm_groups)).astype(F32)
            zzpre = matmul(h, W["gm_w_in"][j], name="mm_gm_in")
            p = gmlp_fwd(zzpre, W["gm_b_in"][j:j + 1], W["gm_v_gain"][j:j + 1], wc, bsf, name="gm_fwd")
            x1 = matmul(p, W["gm_w_out"][j], residual=x, name="mm_gm_out")
            s.update(zzpre=zzpre, p=p, wc=wc, bsf=bsf)
        else:
            conv_dim = W["ssm_conv_w"].shape[2]
            G = (conv_dim - inner) // (2 * SSM_STATE)
            hpg = heads // G
            w_in = W["ssm_w_in"][j]
            w_zx = w_in[:, :inner + conv_dim]
            w_dtg = _spread_dt(w_in[:, inner + conv_dim:], G, hpg)
            bias_g = _group_vec(W["ssm_dt_bias"][j], G, hpg)
            alog_g = _group_vec(W["ssm_a_log"][j], G, hpg)
            d_chan = jnp.repeat(W["ssm_d"][j], SB_HEAD_DIM)[None, :]
            ngain = W["ssm_norm_gain"][j:j + 1]
            zx = matmul(h, w_zx, name="mm_ssm_zx")
            dtg = matmul(h, w_dtg, name="mm_ssm_dt")
            xbc = conv_fwd(zx, W["ssm_conv_w"][j], W["ssm_conv_b"][j:j + 1], inner, name="conv_fwd")
            yn, yfull, hp = ssd_fwd(xbc, zx, dtg, bias_g, alog_g, d_chan, ngain, L, G, name="ssd_fwd")
            x1 = matmul(yn, W["ssm_w_out"][j], residual=x, name="mm_ssm_out")
            s.update(w_zx=w_zx, w_dtg=w_dtg, bias_g=bias_g, alog_g=alog_g, d_chan=d_chan, ngain=ngain,
                     zx=zx, dtg=dtg, xbc=xbc, yn=yn, yfull=yfull, hp=hp)
        h2 = rms_fwd(x1, W["ffn_norm"][i:i + 1], name="rms_ffn_fwd")
        gu = matmul(h2, W["ffn_w_gu"][i], name="mm_ffn_gu")
        a = swiglu_fwd(gu, name="swiglu_fwd")
        x2 = matmul(a, W["ffn_w_down"][i], residual=x1, name="mm_ffn_down")
        s.update(x1=x1, h2=h2, gu=gu, a=a)
        saved.append(s)
        x = x2

    dx, loss = loss_head(x, target, name="loss_head")

    gw = {k: {} for k in WEIGHTS}
    for i in reversed(range(depth)):
        kind, j = i % 3, i // 3
        s = saved[i]
        da = matmul(dx, W["ffn_w_down"][i], tb=True, name="mm_ffn_da")
        gw["ffn_w_down"][i] = matmul(s["a"], dx, ta=True, out_dtype=BF16, name="mm_ffn_dwdown")
        dgu = swiglu_bwd(s["gu"], da, name="swiglu_bwd")
        dh2 = matmul(dgu, W["ffn_w_gu"][i], tb=True, name="mm_ffn_dh")
        gw["ffn_w_gu"][i] = matmul(s["h2"], dgu, ta=True, out_dtype=BF16, name="mm_ffn_dwgu")
        dx1, dgn = rms_bwd(s["x1"], W["ffn_norm"][i:i + 1], dh2, dx, name="rms_ffn_bwd")
        gw["ffn_norm"][i] = dgn[0]
        if kind == 0:
            do = matmul(dx1, W["sb_w_o"][j], tb=True, name="mm_sb_do")
            gw["sb_w_o"][j] = matmul(s["o"], dx1, ta=True, out_dtype=BF16, name="mm_sb_dwo")
            if late is not None and i == 0:
                dqkv, dqg, dkg, received = sb_attn_bwd(
                    s["qkv"], s["o"], do, W["sb_q_gain"][j:j + 1], W["sb_k_gain"][j:j + 1], name="sb_bwd_scatter",
                    scatter=late.contributions(gw))
            else:
                dqkv, dqg, dkg = sb_attn_bwd(s["qkv"], s["o"], do, W["sb_q_gain"][j:j + 1], W["sb_k_gain"][j:j + 1],
                                             name="sb_bwd")
            gw["sb_q_gain"][j] = dqg[0]
            gw["sb_k_gain"][j] = dkg[0]
            dh = matmul(dqkv, W["sb_w_qkv"][j], tb=True, name="mm_sb_dh")
            gw["sb_w_qkv"][j] = matmul(s["h"], dqkv, ta=True, out_dtype=BF16, name="mm_sb_dwqkv")
        elif kind == 1:
            dp = matmul(dx1, W["gm_w_out"][j], tb=True, name="mm_gm_dp")
            gw["gm_w_out"][j] = matmul(s["p"], dx1, ta=True, out_dtype=BF16, name="mm_gm_dwout")
            dzz, db_in, dvg, dws, dbs = gmlp_bwd(s["zzpre"], W["gm_b_in"][j:j + 1], W["gm_v_gain"][j:j + 1],
                                                s["wc"], s["bsf"], dp, name="gm_bwd")
            gw["gm_b_in"][j] = db_in[0]
            gw["gm_v_gain"][j] = dvg[0]
            gw["gm_w_s"][j] = dws
            gw["gm_b_s"][j] = dbs[:, :gm_groups].T
            dh = matmul(dzz, W["gm_w_in"][j], tb=True, name="mm_gm_dh")
            gw["gm_w_in"][j] = matmul(s["h"], dzz, ta=True, out_dtype=BF16, name="mm_gm_dwin")
        else:
            conv_dim = W["ssm_conv_w"].shape[2]
            G = (conv_dim - inner) // (2 * SSM_STATE)
            hpg = heads // G
            dyn = matmul(dx1, W["ssm_w_out"][j], tb=True, name="mm_ssm_dyn")
            gw["ssm_w_out"][j] = matmul(s["yn"], dx1, ta=True, out_dtype=BF16, name="mm_ssm_dwout")
            dz, dxs, dbm, dcm, ddt, dbias, dalog, dd, dng = ssd_bwd(
                s["xbc"], s["zx"], s["dtg"], s["bias_g"], s["alog_g"], s["d_chan"], s["ngain"], s["yfull"], s["hp"],
                dyn, L, G, name="ssd_bwd")
            dxbc = jnp.concatenate([dxs, dbm, dcm], axis=1)
            dpre, dcw, dcb = conv_bwd(s["zx"], W["ssm_conv_w"][j], W["ssm_conv_b"][j:j + 1], inner, dxbc,
                                      name="conv_bwd")
            dzx = jnp.concatenate([dz, dpre], axis=1)
            dh = matmul(ddt, s["w_dtg"], tb=True, name="mm_ssm_dh_dt")
            dh = matmul(dzx, s["w_zx"], tb=True, residual=dh, name="mm_ssm_dh")
            dw_zx = matmul(s["h"], dzx, ta=True, out_dtype=BF16, name="mm_ssm_dwzx")
            dw_dtg = matmul(s["h"], ddt, ta=True, out_dtype=BF16, name="mm_ssm_dwdt")
            dw_dt = dw_dtg.reshape(D, G, LANES)[:, :, :hpg].reshape(D, heads)
            gw["ssm_w_in"][j] = jnp.concatenate([dw_zx, dw_dt], axis=1)
            gw["ssm_conv_w"][j] = dcw
            gw["ssm_conv_b"][j] = dcb[0]
            gw["ssm_dt_bias"][j] = dbias[:, 0, :hpg].reshape(heads)
            gw["ssm_a_log"][j] = dalog[:, 0, :hpg].reshape(heads)
            gw["ssm_d"][j] = dd[:, 0, :hpg].reshape(heads)
            gw["ssm_norm_gain"][j] = dng[0]
        dx, dgn = rms_bwd(s["x"], W["mix_norm"][i:i + 1], dh, dx1, name="rms_mix_bwd")
        gw["mix_norm"][i] = dgn[0]

    return loss, dx, gw, received


MESH = pl.DeviceIdType.MESH
HBM_SPEC = pl.BlockSpec(memory_space=pltpu.HBM)
VMEM_SPEC = pl.BlockSpec(memory_space=pltpu.VMEM)


def _my_position():
    return lax.axis_index("x"), lax.axis_index("y"), lax.axis_index("c")


def _flip(v, bit):
    return 1 - v if bit else v


def all_gather_packed(shard, *, name):
    R, C = shard.shape

    def body(x_ref, out_ref, send_sems, recv_sems, local_sem):
        refs = (x_ref, out_ref, send_sems, recv_sems, local_sem)
        _ag_phase(refs, "start")
        _ag_phase(refs, "forward")
        _ag_phase(refs, "finish")

    return pl.pallas_call(
        body, name=name, out_shape=jax.ShapeDtypeStruct((N_DEV, R, C), shard.dtype),
        in_specs=[HBM_SPEC], out_specs=HBM_SPEC, scratch_shapes=COMM_SEMAPHORES,
    )(shard)


COMM_SEMAPHORES = [pltpu.SemaphoreType.DMA((7,)), pltpu.SemaphoreType.DMA((7,)), pltpu.SemaphoreType.DMA]


def _ag_phase(refs, phase):
    x_ref, out_ref, send_sems, recv_sems, local_sem = refs
    x, y, c = _my_position()
    me, sibling = (x, y, c), (x, y, 1 - c)
    chips = [(1 - x, y), (x, 1 - y), (1 - x, 1 - y)]

    def slot(px, py, pc):
        return out_ref.at[4 * px + 2 * py + pc]

    def copy(k, block, to, src=None):
        return pltpu.make_async_remote_copy(
            src_ref=slot(*block) if src is None else src, dst_ref=slot(*block),
            send_sem=send_sems.at[k], recv_sem=recv_sems.at[k], device_id=to, device_id_type=MESH)

    mine = pltpu.make_async_copy(x_ref, slot(*me), local_sem)
    first = [copy(0, me, sibling, src=x_ref)]
    first += [copy(1 + j, me, (*chip, c), src=x_ref) for j, chip in enumerate(chips)]
    passed = [copy(4 + j, (*chip, c), sibling) for j, chip in enumerate(chips)]
    if phase == "start":
        mine.start()
        for cp in first:
            cp.start()
    elif phase == "forward":
        for j, chip in enumerate(chips):
            copy(1 + j, (*chip, c), me).wait_recv()
            passed[j].start()
    else:
        copy(0, sibling, me).wait_recv()
        for j, chip in enumerate(chips):
            copy(4 + j, (*chip, 1 - c), me).wait_recv()
        for cp in first + passed:
            cp.wait_send()
        mine.wait()


def _rs_phase(refs, phase):
    g_ref, out_ref, send_sems, recv_sems, local_sem = refs
    x, y, c = _my_position()
    me = 4 * x + 2 * y + c
    mine = pltpu.make_async_copy(g_ref.at[me], out_ref.at[me], local_sem)
    copies = []
    for k in range(1, N_DEV):
        px, py, pc = _flip(x, k & 4), _flip(y, k & 2), _flip(c, k & 1)
        copies.append(pltpu.make_async_remote_copy(
            src_ref=g_ref.at[4 * px + 2 * py + pc], dst_ref=out_ref.at[me],
            send_sem=send_sems.at[k - 1], recv_sem=recv_sems.at[k - 1], device_id=(px, py, pc), device_id_type=MESH))
    if phase == "start":
        mine.start()
        for cp in copies:
            cp.start()
    else:
        for cp in copies:
            cp.wait()
        mine.wait()


def exchange_for_reduce_scatter(g, *, name):
    def body(g_ref, out_ref, send_sems, recv_sems, local_sem):
        refs = (g_ref, out_ref, send_sems, recv_sems, local_sem)
        _rs_phase(refs, "start")
        _rs_phase(refs, "finish")

    return pl.pallas_call(
        body, name=name, out_shape=jax.ShapeDtypeStruct(g.shape, g.dtype), in_specs=[HBM_SPEC], out_specs=HBM_SPEC,
        scratch_shapes=COMM_SEMAPHORES,
    )(g)


def sum_slots(recv, *, name):
    n, R, C = recv.shape
    tr = _pick(R, (512, 256, 128))

    def body(r_ref, o_ref):
        acc = r_ref[0].astype(F32)
        for s in range(1, n):
            acc = acc + r_ref[s].astype(F32)
        o_ref[...] = acc

    return pl.pallas_call(
        body, name=name, grid=(R // tr,), in_specs=[pl.BlockSpec((n, tr, C), lambda i: (0, i, 0))],
        out_specs=pl.BlockSpec((tr, C), lambda i: (i, 0)), out_shape=jax.ShapeDtypeStruct((R, C), F32),
        compiler_params=_params(("parallel",)),
    )(recv)


def all_reduce_small(v, *, name):
    R, C = v.shape

    def body(v_ref, o_ref, buf, send_sems, recv_sems):
        x, y, c = _my_position()
        me = 4 * x + 2 * y + c
        buf[me] = v_ref[...]
        copies = []
        for k in range(1, N_DEV):
            px, py, pc = _flip(x, k & 4), _flip(y, k & 2), _flip(c, k & 1)
            copies.append(pltpu.make_async_remote_copy(
                src_ref=v_ref, dst_ref=buf.at[me], send_sem=send_sems.at[k - 1], recv_sem=recv_sems.at[k - 1],
                device_id=(px, py, pc), device_id_type=MESH))
        for cp in copies:
            cp.start()
        for cp in copies:
            cp.wait()
        acc = buf[0]
        for s in range(1, N_DEV):
            acc = acc + buf[s]
        o_ref[...] = acc

    return pl.pallas_call(
        body, name=name, out_shape=jax.ShapeDtypeStruct((R, C), F32), in_specs=[VMEM_SPEC], out_specs=VMEM_SPEC,
        scratch_shapes=[pltpu.VMEM((N_DEV, R, C), F32), pltpu.SemaphoreType.DMA((7,)), pltpu.SemaphoreType.DMA((7,))],
        compiler_params=pltpu.CompilerParams(vmem_limit_bytes=VMEM_LIMIT_BYTES),
    )(v)


def _pad_rows(a, mult):
    pad = (-a.shape[0]) % mult
    return jnp.pad(a, ((0, pad), (0, 0))) if pad else a


def _pack_rows(arrays, cols, row_mult):
    return _pad_rows(jnp.concatenate([a.reshape(-1, cols) for a in arrays], axis=0), row_mult)


def _unpack_rows(packed, shapes):
    out, r = [], 0
    cols = packed.shape[-1]
    for shp in shapes:
        n = math.prod(shp) // cols
        out.append(packed[..., r:r + n, :].reshape(packed.shape[:-2] + tuple(shp)))
        r += n
    return out


def _gathered_to_full(g, name):
    if name in COL_SHARDED or name == "ssm_conv_w":
        return jnp.moveaxis(g, 0, -2).reshape(g.shape[1:-1] + (N_DEV * g.shape[-1],))
    if name in ("ssm_conv_b", "ssm_norm_gain"):
        return jnp.moveaxis(g, 0, -2).reshape(g.shape[1:-1] + (N_DEV * g.shape[-1],))
    return jnp.moveaxis(g, 0, 1).reshape((g.shape[1], N_DEV * g.shape[2], g.shape[3]))


def _full_to_shards(full, name):
    if name in COL_SHARDED:
        n = full.shape[-1] // N_DEV
        return jnp.moveaxis(full.reshape(full.shape[:-1] + (N_DEV, n)), -2, 0)
    k = full.shape[1] // N_DEV
    return jnp.moveaxis(full.reshape((full.shape[0], N_DEV, k, full.shape[2])), 1, 0)


def _pack_small(arrays):
    flat = []
    for a in arrays:
        f = a.reshape(-1).astype(F32)
        flat.append(jnp.pad(f, (0, (-f.shape[0]) % LANES)))
    return _pad_rows(jnp.concatenate(flat).reshape(-1, LANES), 8)


def _unpack_small(packed, shapes):
    flat = packed.reshape(-1)
    out, r = [], 0
    for shp in shapes:
        n = math.prod(shp)
        out.append(flat[r:r + n].reshape(shp))
        r += n + (-n) % LANES
    return out


ARG_NAMES = ("x",) + WEIGHTS + ("loss_target",) + tuple("m_" + w for w in WEIGHTS) + tuple("v_" + w for w in WEIGHTS)


def kernel(x, mix_norm, ffn_norm, sb_w_qkv, sb_q_gain, sb_k_gain, sb_w_o, gm_w_in, gm_b_in, gm_v_gain, gm_w_s, gm_b_s, gm_w_out, ssm_w_in, ssm_conv_w, ssm_conv_b, ssm_dt_bias, ssm_a_log, ssm_d, ssm_norm_gain, ssm_w_out, ffn_w_gu, ffn_w_down, loss_target, m_mix_norm, m_ffn_norm, m_sb_w_qkv, m_sb_q_gain, m_sb_k_gain, m_sb_w_o, m_gm_w_in, m_gm_b_in, m_gm_v_gain, m_gm_w_s, m_gm_b_s, m_gm_w_out, m_ssm_w_in, m_ssm_conv_w, m_ssm_conv_b, m_ssm_dt_bias, m_ssm_a_log, m_ssm_d, m_ssm_norm_gain, m_ssm_w_out, m_ffn_w_gu, m_ffn_w_down, v_mix_norm, v_ffn_norm, v_sb_w_qkv, v_sb_q_gain, v_sb_k_gain, v_sb_w_o, v_gm_w_in, v_gm_b_in, v_gm_v_gain, v_gm_w_s, v_gm_b_s, v_gm_w_out, v_ssm_w_in, v_ssm_conv_w, v_ssm_conv_b, v_ssm_dt_bias, v_ssm_a_log, v_ssm_d, v_ssm_norm_gain, v_ssm_w_out, v_ffn_w_gu, v_ffn_w_down):
    given = dict(zip(ARG_NAMES, (x, mix_norm, ffn_norm, sb_w_qkv, sb_q_gain, sb_k_gain, sb_w_o, gm_w_in, gm_b_in, gm_v_gain, gm_w_s, gm_b_s, gm_w_out, ssm_w_in, ssm_conv_w, ssm_conv_b, ssm_dt_bias, ssm_a_log, ssm_d, ssm_norm_gain, ssm_w_out, ffn_w_gu, ffn_w_down, loss_target, m_mix_norm, m_ffn_norm, m_sb_w_qkv, m_sb_q_gain, m_sb_k_gain, m_sb_w_o, m_gm_w_in, m_gm_b_in, m_gm_v_gain, m_gm_w_s, m_gm_b_s, m_gm_w_out, m_ssm_w_in, m_ssm_conv_w, m_ssm_conv_b, m_ssm_dt_bias, m_ssm_a_log, m_ssm_d, m_ssm_norm_gain, m_ssm_w_out, m_ffn_w_gu, m_ffn_w_down, v_mix_norm, v_ffn_norm, v_sb_w_qkv, v_sb_q_gain, v_sb_k_gain, v_sb_w_o, v_gm_w_in, v_gm_b_in, v_gm_v_gain, v_gm_w_s, v_gm_b_s, v_gm_w_out, v_ssm_w_in, v_ssm_conv_w, v_ssm_conv_b, v_ssm_dt_bias, v_ssm_a_log, v_ssm_d, v_ssm_norm_gain, v_ssm_w_out, v_ffn_w_gu, v_ffn_w_down)))
    mx, my, mc = _my_position()
    me = 4 * mx + 2 * my + mc

    pieces = [(k, l) for k in BIG for l in range(given[k].shape[0])]
    early = [("sb_w_qkv", 0), ("sb_w_o", 0), ("ffn_w_gu", 0), ("ffn_w_down", 0)]
    late_pieces = [p for p in pieces if p not in early]
    last = [("sb_w_qkv", 0)]
    main = [p for p in pieces if p not in last]
    row_mult = 256

    def rows_of(p):
        return math.prod(given[p[0]].shape[1:]) // PACK_COLS

    def pack_shards(ps, prefix="", dtype=F32, extra=()):
        parts = [given[prefix + k][l].astype(dtype).reshape(-1, PACK_COLS) for k, l in ps] + list(extra)
        return _pad_rows(jnp.concatenate(parts, axis=0), row_mult)

    def split_rows(packed, ps):
        out, r = [], 0
        for p in ps:
            out.append(packed[..., r:r + rows_of(p), :])
            r += rows_of(p)
        return out

    def piece_to_full(g, k):
        shp = given[k].shape[1:]
        g = g.reshape((N_DEV,) + shp)
        if k in COL_SHARDED:
            return jnp.moveaxis(g, 0, 1).reshape(shp[0], N_DEV * shp[1])
        return g.reshape(N_DEV * shp[0], shp[1])

    def full_to_piece(full, k):
        shp = given[k].shape[1:]
        g = jnp.moveaxis(full.reshape(shp[0], N_DEV, shp[1]), 1, 0) if k in COL_SHARDED else full
        return g.reshape(N_DEV, -1, PACK_COLS)

    def contributions(gw, ps):
        c = jnp.concatenate([full_to_piece(gw[k][l], k) for k, l in ps], axis=1)
        return jnp.pad(c, ((0, 0), (0, (-c.shape[1]) % row_mult), (0, 0)))

    sharded_small = [lax.bitcast_convert_type(given[k], BF16) for k in SMALL_SHARDED]
    tail = jnp.concatenate([a.reshape(-1) for a in sharded_small])
    tail = jnp.pad(tail, (0, (-tail.size) % PACK_COLS)).reshape(-1, PACK_COLS)

    W = {k: given[k] for k in SMALL if k not in SMALL_SHARDED}
    W.update({k: [None] * given[k].shape[0] for k in BIG})
    gathered_early = all_gather_packed(pack_shards(early, dtype=BF16), name="all_gather_early")
    for (k, l), g in zip(early, split_rows(gathered_early, early)):
        W[k][l] = piece_to_full(g, k)

    class Late:
        shard = pack_shards(late_pieces, dtype=BF16, extra=[tail])

        @staticmethod
        def fill(weights, gathered):
            for (k, l), g in zip(late_pieces, split_rows(gathered, late_pieces)):
                weights[k][l] = piece_to_full(g, k)
            r0 = sum(rows_of(p) for p in late_pieces)
            tail_g = gathered[:, r0:r0 + tail.shape[0], :].reshape(N_DEV, -1)
            off = 0
            for k, a in zip(SMALL_SHARDED, sharded_small):
                g = lax.bitcast_convert_type(tail_g[:, off:off + a.size].reshape((N_DEV,) + a.shape), F32)
                weights[k] = jnp.moveaxis(g, 0, -2).reshape(g.shape[1:-1] + (N_DEV * g.shape[-1],))
                off += a.size

        @staticmethod
        def contributions(gw):
            return contributions(gw, main)

    loss, gx, gw, received_main = local_step(given["x"][0], given["loss_target"][0], W, late=Late)
    received_last = exchange_for_reduce_scatter(contributions(gw, last), name="reduce_scatter_last")

    grads_small = {k: jnp.stack([gw[k][l] for l in sorted(gw[k])], axis=0) for k in SMALL}
    small_shapes = [grads_small[k].shape for k in SMALL] + [(1, 1)]
    reduced = all_reduce_small(_pack_small([grads_small[k] for k in SMALL] + [loss]), name="all_reduce_small")
    small_full = dict(zip(SMALL + ("loss",), _unpack_small(reduced, small_shapes)))

    by_piece = {}
    for grp, received, tag in ((main, received_main, "main"), (last, received_last, "last")):
        w = pack_shards(grp)
        g = sum_slots(received, name="reduce_scatter_sum_" + tag)
        res = (g,) + tuple(adamw(w, g, pack_shards(grp, "m_"), pack_shards(grp, "v_"), name="adamw_" + tag))
        for n, packed in enumerate(res):
            for p, rows in zip(grp, split_rows(packed, grp)):
                by_piece[(n,) + p] = rows.reshape(given[p[0]].shape[1:])
    out_g, out_d, out_m, out_v = ({k: jnp.stack([by_piece[(n, k, l)] for l in range(given[k].shape[0])], axis=0)
                                   for k in BIG} for n in range(4))

    gsmall = {}
    for k in SMALL:
        g = small_full[k]
        if k in SMALL_SHARDED:
            n = given[k].shape[-1]
            g = lax.dynamic_slice_in_dim(g, me * n, n, axis=g.ndim - 1)
        gsmall[k] = g
    local_shapes = [given[k].shape for k in SMALL]
    dsm, nmsm, nvsm = adamw(_pack_small([given[k] for k in SMALL]), _pack_small([gsmall[k] for k in SMALL]),
                            _pack_small([given["m_" + k] for k in SMALL]), _pack_small([given["v_" + k] for k in SMALL]),
                            name="adamw_small")
    out_g.update(gsmall)
    for dst, src in ((out_d, dsm), (out_m, nmsm), (out_v, nvsm)):
        dst.update(zip(SMALL, _unpack_small(src, local_shapes)))

    return (small_full["loss"].reshape(()), gx[None],
            *[out_g[k] for k in WEIGHTS], *[out_d[k] for k in WEIGHTS],
            *[out_m[k] for k in WEIGHTS], *[out_v[k] for k in WEIGHTS])
```

```python
import math

import jax
import jax.numpy as jnp
from jax import lax
from jax.experimental import pallas as pl
from jax.experimental.pallas import tpu as pltpu

F32 = jnp.float32
BF16 = jnp.bfloat16
EPS = 1e-6
N_DEV = 8
SB_HEAD_DIM = 64
SSM_STATE = 128
SSM_CONV = 4
ADAM_LR = 0.001
ADAM_B1 = 0.9
ADAM_B2 = 0.999
ADAM_EPS = 1e-08
ADAM_WD = 0.01
ADAM_STEP = 10
VMEM_LIMIT_BYTES = 56 * 1024 * 1024
LANES = 128
PACK_COLS = 1024

BIG = ("sb_w_qkv", "sb_w_o", "gm_w_in", "gm_w_out", "ssm_w_in", "ssm_w_out", "ffn_w_gu", "ffn_w_down")
COL_SHARDED = ("sb_w_qkv", "gm_w_in", "ssm_w_in", "ffn_w_gu")
SMALL = ("mix_norm", "ffn_norm", "sb_q_gain", "sb_k_gain", "gm_b_in", "gm_v_gain", "gm_w_s", "gm_b_s",
         "ssm_conv_w", "ssm_conv_b", "ssm_dt_bias", "ssm_a_log", "ssm_d", "ssm_norm_gain")
SMALL_SHARDED = ("ssm_conv_w", "ssm_conv_b", "ssm_norm_gain")
WEIGHTS = ("mix_norm", "ffn_norm", "sb_w_qkv", "sb_q_gain", "sb_k_gain", "sb_w_o", "gm_w_in", "gm_b_in",
           "gm_v_gain", "gm_w_s", "gm_b_s", "gm_w_out", "ssm_w_in", "ssm_conv_w", "ssm_conv_b", "ssm_dt_bias",
           "ssm_a_log", "ssm_d", "ssm_norm_gain", "ssm_w_out", "ffn_w_gu", "ffn_w_down")


def _params(semantics=None):
    return pltpu.CompilerParams(dimension_semantics=semantics, vmem_limit_bytes=VMEM_LIMIT_BYTES)


def _pick(n, prefs):
    for t in prefs:
        if t <= n and n % t == 0:
            return t
    return n


def _dot(a, b, ca=1, cb=0):
    return lax.dot_general(a, b, (((ca,), (cb,)), ((), ())), preferred_element_type=F32)


def _split3(v):
    h1 = v.astype(BF16)
    r1 = v - h1.astype(F32)
    h2 = r1.astype(BF16)
    h3 = (r1 - h2.astype(F32)).astype(BF16)
    return h1, h2, h3


def _dot_exact_left(mat01, v):
    h1, h2, h3 = _split3(v)
    return _dot(mat01, h1) + _dot(mat01, h2) + _dot(mat01, h3)


def _dot_split2_right(v, mat01):
    hi = v.astype(BF16)
    lo = (v - hi.astype(F32)).astype(BF16)
    return _dot(hi, mat01) + _dot(lo, mat01)


def _sigmoid(v):
    return 1.0 / (1.0 + jnp.exp(-v))


def _softplus(v):
    return jnp.maximum(v, 0.0) + jnp.log(1.0 + jnp.exp(-jnp.abs(v)))


def _erf(v):
    a = jnp.abs(v)
    t = 1.0 / (1.0 + 0.3275911 * a)
    poly = t * (0.254829592 + t * (-0.284496736 + t * (1.421413741 + t * (-1.453152027 + t * 1.061405429))))
    e = 1.0 - poly * jnp.exp(-a * a)
    return jnp.where(v < 0, -e, e)


def _gelu_and_grad(v):
    cdf = 0.5 * (1.0 + _erf(v * (1.0 / math.sqrt(2.0))))
    pdf = jnp.exp(-0.5 * v * v) * (1.0 / math.sqrt(2.0 * math.pi))
    return v * cdf, cdf + v * pdf


def matmul(a, b, *, ta=False, tb=False, out_dtype=F32, residual=None, name):
    if ta:
        K, M = a.shape
    else:
        M, K = a.shape
    if tb:
        N, Kb = b.shape
    else:
        Kb, N = b.shape
    assert K == Kb, (a.shape, b.shape, ta, tb)
    tm = _pick(M, (1024, 1408, 768, 512, 256, 128))
    tn = _pick(N, (1024, 1408, 1536, 768, 512, 256, 128))
    tk = _pick(K, (512, 1408, 256, 128))
    nk = K // tk
    a_spec = pl.BlockSpec((tk, tm), lambda i, j, k: (k, i)) if ta else pl.BlockSpec((tm, tk), lambda i, j, k: (i, k))
    b_spec = pl.BlockSpec((tn, tk), lambda i, j, k: (j, k)) if tb else pl.BlockSpec((tk, tn), lambda i, j, k: (k, j))
    o_spec = pl.BlockSpec((tm, tn), lambda i, j, k: (i, j))
    ca, cb = (0 if ta else 1), (1 if tb else 0)
    has_res = residual is not None

    def body(*refs):
        if has_res:
            a_ref, b_ref, r_ref, o_ref, acc = refs
        else:
            a_ref, b_ref, o_ref, acc = refs
        k = pl.program_id(2)

        @pl.when(k == 0)
        def _():
            acc[...] = jnp.zeros_like(acc)

        acc[...] += _dot(a_ref[...].astype(BF16), b_ref[...].astype(BF16), ca, cb)

        @pl.when(k == nk - 1)
        def _():
            r = acc[...]
            if has_res:
                r = r + r_ref[...]
            o_ref[...] = r.astype(out_dtype)

    in_specs = [a_spec, b_spec] + ([o_spec] if has_res else [])
    args = (a, b) + ((residual,) if has_res else ())
    return pl.pallas_call(
        body, name=name, grid=(M // tm, N // tn, nk), in_specs=in_specs, out_specs=o_spec,
        out_shape=jax.ShapeDtypeStruct((M, N), out_dtype), scratch_shapes=[pltpu.VMEM((tm, tn), F32)],
        compiler_params=_params(("parallel", "parallel", "arbitrary")),
    )(*args)


def rms_fwd(x, gain, *, name):
    S, D = x.shape
    tr = _pick(S, (512, 256, 128))

    def body(x_ref, g_ref, o_ref):
        xv = x_ref[...]
        r = lax.rsqrt(jnp.mean(xv * xv, axis=1, keepdims=True) + EPS)
        o_ref[...] = (xv * r * g_ref[...]).astype(BF16)

    return pl.pallas_call(
        body, name=name, grid=(S // tr,),
        in_specs=[pl.BlockSpec((tr, D), lambda i: (i, 0)), pl.BlockSpec((1, D), lambda i: (0, 0))],
        out_specs=pl.BlockSpec((tr, D), lambda i: (i, 0)), out_shape=jax.ShapeDtypeStruct((S, D), BF16),
        compiler_params=_params(("parallel",)),
    )(x, gain)


def rms_bwd(x, gain, dh, dres, *, name):
    S, D = x.shape
    tr = _pick(S, (512, 256, 128))

    def body(x_ref, g_ref, dh_ref, dr_ref, dx_ref, dg_ref):
        @pl.when(pl.program_id(0) == 0)
        def _():
            dg_ref[...] = jnp.zeros_like(dg_ref)

        xv = x_ref[...]
        dhv = dh_ref[...]
        r = lax.rsqrt(jnp.mean(xv * xv, axis=1, keepdims=True) + EPS)
        xhat = xv * r
        t = dhv * g_ref[...]
        dx_ref[...] = dr_ref[...] + r * (t - xhat * jnp.mean(xhat * t, axis=1, keepdims=True))
        dg_ref[...] += jnp.sum(dhv * xhat, axis=0, keepdims=True)

    row = pl.BlockSpec((tr, D), lambda i: (i, 0))
    vec = pl.BlockSpec((1, D), lambda i: (0, 0))
    return pl.pallas_call(
        body, name=name, grid=(S // tr,), in_specs=[row, vec, row, row], out_specs=[row, vec],
        out_shape=[jax.ShapeDtypeStruct((S, D), F32), jax.ShapeDtypeStruct((1, D), F32)],
        compiler_params=_params(("arbitrary",)),
    )(x, gain, dh, dres)


def swiglu_fwd(gu, *, name):
    S, F2 = gu.shape
    F = F2 // 2
    tr = _pick(S, (256, 128))

    def body(gu_ref, o_ref):
        g = gu_ref[:, :F]
        u = gu_ref[:, F:]
        o_ref[...] = (g * _sigmoid(g) * u).astype(BF16)

    return pl.pallas_call(
        body, name=name, grid=(S // tr,), in_specs=[pl.BlockSpec((tr, F2), lambda i: (i, 0))],
        out_specs=pl.BlockSpec((tr, F), lambda i: (i, 0)), out_shape=jax.ShapeDtypeStruct((S, F), BF16),
        compiler_params=_params(("parallel",)),
    )(gu)


def swiglu_bwd(gu, da, *, name):
    S, F2 = gu.shape
    F = F2 // 2
    tr = _pick(S, (256, 128))

    def body(gu_ref, da_ref, o_ref):
        g = gu_ref[:, :F]
        u = gu_ref[:, F:]
        dav = da_ref[...]
        s = _sigmoid(g)
        o_ref[:, :F] = (dav * u * (s * (1.0 + g * (1.0 - s)))).astype(BF16)
        o_ref[:, F:] = (dav * g * s).astype(BF16)

    return pl.pallas_call(
        body, name=name, grid=(S // tr,),
        in_specs=[pl.BlockSpec((tr, F2), lambda i: (i, 0)), pl.BlockSpec((tr, F), lambda i: (i, 0))],
        out_specs=pl.BlockSpec((tr, F2), lambda i: (i, 0)), out_shape=jax.ShapeDtypeStruct((S, F2), BF16),
        compiler_params=_params(("parallel",)),
    )(gu, da)


def loss_head(y, target, *, name):
    S, D = y.shape
    tr = _pick(S, (512, 256, 128))

    def body(y_ref, t_ref, dy_ref, l_ref):
        @pl.when(pl.program_id(0) == 0)
        def _():
            l_ref[...] = jnp.zeros_like(l_ref)

        err = y_ref[...] - t_ref[...]
        dy_ref[...] = err * (1.0 / D)
        l_ref[...] += jnp.sum(0.5 * jnp.mean(err * err, axis=1, keepdims=True), axis=0, keepdims=True)

    row = pl.BlockSpec((tr, D), lambda i: (i, 0))
    one = pl.BlockSpec((1, 1), lambda i: (0, 0))
    dy, l = pl.pallas_call(
        body, name=name, grid=(S // tr,), in_specs=[row, row], out_specs=[row, one],
        out_shape=[jax.ShapeDtypeStruct((S, D), F32), jax.ShapeDtypeStruct((1, 1), F32)],
        compiler_params=_params(("arbitrary",)),
    )(y, target)
    return dy, l


def adamw(w, g, m, v, *, name):
    R, C = w.shape
    tr = _pick(R, (512, 256, 128, 64, 32, 16, 8))

    def body(w_ref, g_ref, m_ref, v_ref, d_ref, mo_ref, vo_ref):
        gv = g_ref[...]
        mn = ADAM_B1 * m_ref[...] + (1.0 - ADAM_B1) * gv
        vn = ADAM_B2 * v_ref[...] + (1.0 - ADAM_B2) * jnp.square(gv)
        m_hat = mn / (1.0 - ADAM_B1 ** ADAM_STEP)
        v_hat = vn / (1.0 - ADAM_B2 ** ADAM_STEP)
        d_ref[...] = -ADAM_LR * (m_hat / (jnp.sqrt(v_hat) + ADAM_EPS) + ADAM_WD * w_ref[...])
        mo_ref[...] = mn
        vo_ref[...] = vn

    blk = pl.BlockSpec((tr, C), lambda i: (i, 0))
    sds = jax.ShapeDtypeStruct((R, C), F32)
    return pl.pallas_call(
        body, name=name, grid=(R // tr,), in_specs=[blk] * 4, out_specs=[blk] * 3, out_shape=[sds] * 3,
        compiler_params=_params(("parallel",)),
    )(w, g, m, v)


def _tri(n, kind):
    r = lax.broadcasted_iota(jnp.int32, (n, n), 0)
    c = lax.broadcasted_iota(jnp.int32, (n, n), 1)
    if kind == "row_gt_col":
        return (r > c).astype(BF16)
    if kind == "row_ge_col":
        return (r >= c).astype(BF16)
    if kind == "row_le_col":
        return (r <= c).astype(BF16)
    raise ValueError(kind)


def _sb_tile(qs, kj, r_carry, u_strict, masked):
    z = _dot(qs, kj, 1, 1)
    lb = jnp.minimum(z, 0.0) - jnp.log(1.0 + jnp.exp(-jnp.abs(z)))
    l1m = lb - z
    keep = None
    if masked:
        tq, tk = z.shape
        keep = lax.broadcasted_iota(jnp.int32, (tq, tk), 1) < lax.broadcasted_iota(jnp.int32, (tq, tk), 0)
        l1m = jnp.where(keep, l1m, 0.0)
    w = jnp.exp(lb + _dot(l1m.astype(BF16), u_strict) + r_carry)
    if masked:
        w = jnp.where(keep, w, 0.0)
    return lb, l1m, w, keep


def _sb_prep(T, nb, hd, refs_in, gains, scratch):
    q_scale = 1.0 / math.sqrt(hd)
    assert math.log2(q_scale) == round(math.log2(q_scale))

    def prep(i, _):
        rows = pl.ds(pl.multiple_of(i * T, T), T)
        for hh in range(2):
            sl = slice(hd * hh, hd * hh + hd)
            for n, (src, dst) in enumerate(zip(refs_in, scratch)):
                v = src[rows, sl]
                if n < 2:
                    v = v * lax.rsqrt(jnp.mean(v * v, axis=1, keepdims=True) + EPS) * gains[n][...]
                if n == 0:
                    v = v * q_scale
                dst[hh, rows, :] = v.astype(BF16)
        return 0

    lax.fori_loop(0, nb, prep, 0)


def _sb_chains(m, T):
    rows = [pl.ds(pl.multiple_of((2 * m + qb) * T, T), T) for qb in range(2)]
    return rows, [(hh, qb) for qb in range(2) for hh in range(2)]


def sb_attn_fwd(qkv, q_gain, k_gain, *, name, gather=None):
    S, D3 = qkv.shape
    D = D3 // 3
    npairs = D // LANES
    hd = SB_HEAD_DIM
    T = min(256, S)
    nb = S // T
    assert nb % 2 == 0

    def body(*refs):
        if gather is None:
            q_ref, k_ref, v_ref, qg_ref, kg_ref, us_ref, o_ref, qn_s, kn_s, vb_s = refs
        else:
            q_ref, k_ref, v_ref, qg_ref, kg_ref, us_ref, ag_in, o_ref, ag_out, qn_s, kn_s, vb_s = refs[:12]
            comm = (ag_in, ag_out) + refs[12:]
            step = pl.program_id(0)
            pl.when(step == 0)(lambda: _ag_phase(comm, "start"))
            pl.when(step == (5 * npairs) // 8)(lambda: _ag_phase(comm, "forward"))
        us = us_ref[...]
        _sb_prep(T, nb, hd, (q_ref, k_ref, v_ref), (qg_ref, kg_ref), (qn_s, kn_s, vb_s))

        def superblock(m, _):
            rows_q, chains = _sb_chains(m, T)
            qs = {ch: qn_s[ch[0], rows_q[ch[1]], :] for ch in chains}

            def tile(j, carry, which):
                rows_j = pl.ds(pl.multiple_of(j * T, T), T)
                new = dict(carry)
                for ch, masked in which:
                    acc, rc = carry[ch]
                    _, l1m, w, _ = _sb_tile(qs[ch], kn_s[ch[0], rows_j, :], rc, us, masked)
                    new[ch] = (acc + _dot(w.astype(BF16), vb_s[ch[0], rows_j, :]),
                               rc + jnp.sum(l1m, axis=1, keepdims=True))
                return new

            carry = {ch: (jnp.zeros((T, hd), F32), jnp.zeros((T, 1), F32)) for ch in chains}
            carry = tile(2 * m + 1, carry, [(ch, True) for ch in chains if ch[1] == 1])
            carry = tile(2 * m, carry, [(ch, ch[1] == 0) for ch in chains])
            carry = lax.fori_loop(0, 2 * m, lambda jj, c: tile(2 * m - 1 - jj, c, [(ch, False) for ch in chains]),
                                  carry)
            for qb in range(2):
                o_ref[rows_q[qb], :] = jnp.concatenate([carry[(0, qb)][0], carry[(1, qb)][0]], axis=1)
            return 0

        lax.fori_loop(0, nb // 2, superblock, 0)
        if gather is not None:
            pl.when(step == npairs - 1)(lambda: _ag_phase(comm, "finish"))

    col = lambda off: pl.BlockSpec((S, LANES), lambda p, off=off: (0, off + p))
    gain = pl.BlockSpec((1, hd), lambda p: (0, 0))
    in_specs = [col(0), col(npairs), col(2 * npairs), gain, gain, pl.BlockSpec((T, T), lambda p: (0, 0))]
    out_specs = [pl.BlockSpec((S, LANES), lambda p: (0, p))]
    out_shape = [jax.ShapeDtypeStruct((S, D), F32)]
    scratch = [pltpu.VMEM((2, S, hd), BF16)] * 3
    args = [qkv, qkv, qkv, q_gain, k_gain, _tri(T, "row_gt_col")]
    if gather is not None:
        in_specs.append(HBM_SPEC)
        out_specs.append(HBM_SPEC)
        out_shape.append(jax.ShapeDtypeStruct((N_DEV,) + gather.shape, gather.dtype))
        scratch += COMM_SEMAPHORES
        args.append(gather)
    out = pl.pallas_call(
        body, name=name, grid=(npairs,), in_specs=in_specs, out_specs=out_specs, out_shape=out_shape,
        scratch_shapes=scratch, compiler_params=_params(("arbitrary",)),
    )(*args)
    return out[0] if gather is None else tuple(out)


def sb_attn_bwd(qkv, o, do, q_gain, k_gain, *, name, scatter=None):
    S, D3 = qkv.shape
    D = D3 // 3
    npairs = D // LANES
    hd = SB_HEAD_DIM
    T = min(256, S)
    nb = S // T
    scale = 1.0 / math.sqrt(hd)

    def body(*refs):
        if scatter is None:
            (q_ref, k_ref, v_ref, o_ref, do_ref, qg_ref, kg_ref, us_ref,
             dq_ref, dk_ref, dv_ref, dg_ref, qn_s, kn_s, vb_s, dob_s) = refs
        else:
            ns = len(scatter)
            q_ref, k_ref, v_ref, o_ref, do_ref, qg_ref, kg_ref, us_ref = refs[:8]
            rs_in = refs[8:8 + ns]
            dq_ref, dk_ref, dv_ref, dg_ref = refs[8 + ns:12 + ns]
            rs_out = refs[12 + ns:12 + 2 * ns]
            qn_s, kn_s, vb_s, dob_s = refs[12 + 2 * ns:16 + 2 * ns]
            rs_sems = refs[16 + 2 * ns:]
            pl.when(pl.program_id(0) == 0)(lambda: _rs_phase(rs_in, rs_out, rs_sems, "start"))

        @pl.when(pl.program_id(0) == 0)
        def _():
            dg_ref[...] = jnp.zeros_like(dg_ref)

        us = us_ref[...]
        _sb_prep(T, nb, hd, (q_ref, k_ref, v_ref, do_ref), (qg_ref, kg_ref), (qn_s, kn_s, vb_s, dob_s))
        dk_ref[...] = jnp.zeros_like(dk_ref)
        dv_ref[...] = jnp.zeros_like(dv_ref)

        def superblock(m, _):
            rows_q, chains = _sb_chains(m, T)
            qs = {ch: qn_s[ch[0], rows_q[ch[1]], :] for ch in chains}
            doi = {ch: dob_s[ch[0], rows_q[ch[1]], :] for ch in chains}
            dt_total = {ch: jnp.sum(doi[ch].astype(F32) * o_ref[rows_q[ch[1]], hd * ch[0]:hd * ch[0] + hd],
                                    axis=1, keepdims=True) for ch in chains}

            def tile(j, carry, which):
                rows_j = pl.ds(pl.multiple_of(j * T, T), T)
                new = dict(carry)
                dk_part, dv_part = {}, {}
                for ch, masked in which:
                    hh = ch[0]
                    dq_acc, rc, gc = carry[ch]
                    kj = kn_s[hh, rows_j, :]
                    lb, l1m, w, keep = _sb_tile(qs[ch], kj, rc, us, masked)
                    wb = w.astype(BF16)
                    g = _dot(doi[ch], vb_s[hh, rows_j, :], 1, 1) * wb.astype(F32)
                    g_upto = dt_total[ch] - (gc + _dot_split2_right(g, us))
                    dz = g - g_upto * jnp.exp(lb)
                    if masked:
                        dz = jnp.where(keep, dz, 0.0)
                    dzb = dz.astype(BF16)
                    dv_part[hh] = dv_part.get(hh, 0.0) + _dot(wb, doi[ch], 0, 0)
                    dk_part[hh] = dk_part.get(hh, 0.0) + _dot(dzb, qs[ch], 0, 0)
                    new[ch] = (dq_acc + _dot(dzb, kj), rc + jnp.sum(l1m, axis=1, keepdims=True),
                               gc + jnp.sum(g, axis=1, keepdims=True))
                dv_ref[rows_j, :] += jnp.concatenate([dv_part[0], dv_part[1]], axis=1)
                dk_ref[rows_j, :] += jnp.concatenate([dk_part[0], dk_part[1]], axis=1)
                return new

            zero1 = jnp.zeros((T, 1), F32)
            carry = {ch: (jnp.zeros((T, hd), F32), zero1, zero1) for ch in chains}
            carry = tile(2 * m + 1, carry, [(ch, True) for ch in chains if ch[1] == 1])
            carry = tile(2 * m, carry, [(ch, ch[1] == 0) for ch in chains])
            carry = lax.fori_loop(0, 2 * m, lambda jj, c: tile(2 * m - 1 - jj, c, [(ch, False) for ch in chains]),
                                  carry)
            for qb in range(2):
                dq_ref[rows_q[qb], :] = jnp.concatenate([carry[(0, qb)][0], carry[(1, qb)][0]], axis=1) * scale
            return 0

        lax.fori_loop(0, nb // 2, superblock, 0)

        def finish(i, carry):
            rows = pl.ds(pl.multiple_of(i * T, T), T)
            new = []
            for hh in range(2):
                sl = slice(hd * hh, hd * hh + hd)
                outs = []
                for raw_ref, gain_ref, dn in ((q_ref, qg_ref, dq_ref[rows, sl]), (k_ref, kg_ref, dk_ref[rows, sl])):
                    raw = raw_ref[rows, sl]
                    r = lax.rsqrt(jnp.mean(raw * raw, axis=1, keepdims=True) + EPS)
                    hat = raw * r
                    t = dn * gain_ref[...]
                    outs.append((r * (t - hat * jnp.mean(hat * t, axis=1, keepdims=True)),
                                 jnp.sum(dn * hat, axis=0, keepdims=True)))
                dq_ref[rows, sl] = outs[0][0]
                dk_ref[rows, sl] = outs[1][0]
                new.append((carry[hh][0] + outs[0][1], carry[hh][1] + outs[1][1]))
            return tuple(new)

        zg = (jnp.zeros((1, hd), F32), jnp.zeros((1, hd), F32))
        tot = lax.fori_loop(0, nb, finish, (zg, zg))
        dg_ref[0:1, 0:hd] += tot[0][0] + tot[1][0]
        dg_ref[1:2, 0:hd] += tot[0][1] + tot[1][1]
        if scatter is not None:
            pl.when(pl.program_id(0) == npairs - 1)(lambda: _rs_phase(rs_in, rs_out, rs_sems, "finish"))

    col = lambda off: pl.BlockSpec((S, LANES), lambda p, off=off: (0, off + p))
    gain = pl.BlockSpec((1, hd), lambda p: (0, 0))
    tri = pl.BlockSpec((T, T), lambda p: (0, 0))
    pair = pl.BlockSpec((S, LANES), lambda p: (0, p))
    in_specs = [col(0), col(npairs), col(2 * npairs), pair, pair, gain, gain, tri]
    out_specs = [pair, pair, pair, pl.BlockSpec((8, LANES), lambda p: (0, 0))]
    out_shape = [jax.ShapeDtypeStruct((S, D), F32)] * 3 + [jax.ShapeDtypeStruct((8, LANES), F32)]
    scratch = [pltpu.VMEM((2, S, hd), BF16)] * 4
    args = [qkv, qkv, qkv, o, do, q_gain, k_gain, _tri(T, "row_gt_col")]
    if scatter is not None:
        in_specs += [HBM_SPEC] * len(scatter)
        out_specs += [HBM_SPEC] * len(scatter)
        out_shape += [jax.ShapeDtypeStruct(g.shape, g.dtype) for g in scatter]
        scratch += _rs_semaphores(len(scatter))
        args += list(scatter)
    out = pl.pallas_call(
        body, name=name, grid=(npairs,), in_specs=in_specs, out_specs=out_specs, out_shape=out_shape,
        scratch_shapes=scratch, compiler_params=_params(("arbitrary",)),
    )(*args)
    dq, dk, dv, dg = out[:4]
    res = (jnp.concatenate([dq, dk, dv], axis=1), dg[0:1, :hd], dg[1:2, :hd])
    return res if scatter is None else res + (list(out[4:]),)


def gmlp_fwd(zzpre, b_in, v_gain, wc, bsf, *, name):
    S, H2 = zzpre.shape
    H = H2 // 2
    G, T, _ = wc.shape
    gd = H // G

    def body(z_ref, b_ref, vg_ref, wc_ref, bs_ref, p_ref):
        zz, _ = _gelu_and_grad(z_ref[...] + b_ref[...])
        u = zz[:, :H]
        v = zz[:, H:]
        vn = v * lax.rsqrt(jnp.mean(v * v, axis=1, keepdims=True) + EPS) * vg_ref[...]
        for g in range(G):
            gs = slice(g * gd, (g + 1) * gd)
            mixed = _dot(wc_ref[g], vn[:, gs].astype(BF16)) + bs_ref[g]
            p_ref[:, gs] = (u[:, gs] * mixed).astype(BF16)

    full3 = lambda shp: pl.BlockSpec(shp, lambda c: (0, 0, 0))
    return pl.pallas_call(
        body, name=name, grid=(S // T,),
        in_specs=[pl.BlockSpec((T, H2), lambda c: (c, 0)), pl.BlockSpec((1, H2), lambda c: (0, 0)),
                  pl.BlockSpec((1, H), lambda c: (0, 0)), full3((G, T, T)), full3((G, T, gd))],
        out_specs=pl.BlockSpec((T, H), lambda c: (c, 0)), out_shape=jax.ShapeDtypeStruct((S, H), BF16),
        compiler_params=_params(("parallel",)),
    )(zzpre, b_in, v_gain, wc, bsf)


def gmlp_bwd(zzpre, b_in, v_gain, wc, bsf, dp, *, name):
    S, H2 = zzpre.shape
    H = H2 // 2
    G, T, _ = wc.shape
    gd = H // G
    assert G <= LANES

    def body(z_ref, b_ref, vg_ref, wc_ref, bs_ref, dp_ref, dzz_ref, db_ref, dvg_ref, dws_ref, dbs_ref):
        @pl.when(pl.program_id(0) == 0)
        def _():
            db_ref[...] = jnp.zeros_like(db_ref)
            dvg_ref[...] = jnp.zeros_like(dvg_ref)
            dws_ref[...] = jnp.zeros_like(dws_ref)
            dbs_ref[...] = jnp.zeros_like(dbs_ref)

        zz, gp = _gelu_and_grad(z_ref[...] + b_ref[...])
        u = zz[:, :H]
        v = zz[:, H:]
        r = lax.rsqrt(jnp.mean(v * v, axis=1, keepdims=True) + EPS)
        vhat = v * r
        vg = vg_ref[...]
        vn = vhat * vg
        dpv = dp_ref[...]
        tril = lax.broadcasted_iota(jnp.int32, (T, T), 1) <= lax.broadcasted_iota(jnp.int32, (T, T), 0)
        lane = lax.broadcasted_iota(jnp.int32, (T, LANES), 1)
        dbs = jnp.zeros((T, LANES), F32)
        du_parts, dvn_parts = [], []
        for g in range(G):
            gs = slice(g * gd, (g + 1) * gd)
            vng = vn[:, gs].astype(BF16)
            wcg = wc_ref[g]
            mixed = _dot(wcg, vng) + bs_ref[g]
            dpg = dpv[:, gs]
            du_parts.append(dpg * mixed)
            dmx = dpg * u[:, gs]
            dmxb = dmx.astype(BF16)
            dvn_parts.append(_dot(wcg, dmxb, 0, 0))
            dws_ref[g] += jnp.where(tril, _dot(dmxb, vng, 1, 1), 0.0)
            dbs = dbs + jnp.where(lane == g, jnp.sum(dmx, axis=1, keepdims=True), 0.0)
        dbs_ref[...] += dbs
        du = jnp.concatenate(du_parts, axis=1)
        dvn = jnp.concatenate(dvn_parts, axis=1)
        dvg_ref[...] += jnp.sum(dvn * vhat, axis=0, keepdims=True)
        t = dvn * vg
        dv = r * (t - vhat * jnp.mean(vhat * t, axis=1, keepdims=True))
        dzu = du * gp[:, :H]
        dzv = dv * gp[:, H:]
        dzz_ref[:, :H] = dzu.astype(BF16)
        dzz_ref[:, H:] = dzv.astype(BF16)
        db_ref[:, :H] += jnp.sum(dzu, axis=0, keepdims=True)
        db_ref[:, H:] += jnp.sum(dzv, axis=0, keepdims=True)

    full3 = lambda shp: pl.BlockSpec(shp, lambda c: (0, 0, 0))
    vec = lambda n: pl.BlockSpec((1, n), lambda c: (0, 0))
    return pl.pallas_call(
        body, name=name, grid=(S // T,),
        in_specs=[pl.BlockSpec((T, H2), lambda c: (c, 0)), vec(H2), vec(H), full3((G, T, T)), full3((G, T, gd)),
                  pl.BlockSpec((T, H), lambda c: (c, 0))],
        out_specs=[pl.BlockSpec((T, H2), lambda c: (c, 0)), vec(H2), vec(H), full3((G, T, T)),
                   pl.BlockSpec((T, LANES), lambda c: (0, 0))],
        out_shape=[jax.ShapeDtypeStruct((S, H2), BF16), jax.ShapeDtypeStruct((1, H2), F32),
                   jax.ShapeDtypeStruct((1, H), F32), jax.ShapeDtypeStruct((G, T, T), F32),
                   jax.ShapeDtypeStruct((T, LANES), F32)],
        compiler_params=_params(("arbitrary",)),
    )(zzpre, b_in, v_gain, wc, bsf, dp)


def _shift_rows(v, k, n_rows):
    if k == 0:
        return v
    rolled = pltpu.roll(v, k % n_rows, 0)
    row = lax.broadcasted_iota(jnp.int32, v.shape, 0)
    keep = (row >= k) if k > 0 else (row < n_rows + k)
    return jnp.where(keep, rolled, 0.0)


def conv_fwd(zx, conv_w, conv_b, col0, *, name):
    S = zx.shape[0]
    C = conv_w.shape[1]
    tc = _pick(C, (256, 128))
    off = col0 // tc
    assert col0 % tc == 0

    def body(x_ref, w_ref, b_ref, o_ref):
        xv = x_ref[...]
        acc = b_ref[...] + w_ref[SSM_CONV - 1:SSM_CONV, :] * xv
        for k in range(SSM_CONV - 1):
            acc = acc + w_ref[k:k + 1, :] * _shift_rows(xv, SSM_CONV - 1 - k, S)
        o_ref[...] = acc * _sigmoid(acc)

    return pl.pallas_call(
        body, name=name, grid=(C // tc,),
        in_specs=[pl.BlockSpec((S, tc), lambda j: (0, off + j)), pl.BlockSpec((SSM_CONV, tc), lambda j: (0, j)),
                  pl.BlockSpec((1, tc), lambda j: (0, j))],
        out_specs=pl.BlockSpec((S, tc), lambda j: (0, j)), out_shape=jax.ShapeDtypeStruct((S, C), F32),
        compiler_params=_params(("parallel",)),
    )(zx, conv_w, conv_b)


def conv_bwd(zx, conv_w, conv_b, col0, dout, *, name):
    S = zx.shape[0]
    C = conv_w.shape[1]
    tc = _pick(C, (256, 128))
    off = col0 // tc

    def body(x_ref, w_ref, b_ref, do_ref, dx_ref, dw_ref, db_ref):
        xv = x_ref[...]
        shifted = [_shift_rows(xv, SSM_CONV - 1 - k, S) for k in range(SSM_CONV)]
        acc = b_ref[...]
        for k in range(SSM_CONV):
            acc = acc + w_ref[k:k + 1, :] * shifted[k]
        s = _sigmoid(acc)
        dacc = do_ref[...] * (s * (1.0 + acc * (1.0 - s)))
        db_ref[...] = jnp.sum(dacc, axis=0, keepdims=True)
        dx = jnp.zeros_like(xv)
        for k in range(SSM_CONV):
            dw_ref[k:k + 1, :] = jnp.sum(dacc * shifted[k], axis=0, keepdims=True)
            dx = dx + w_ref[k:k + 1, :] * _shift_rows(dacc, -(SSM_CONV - 1 - k), S)
        dx_ref[...] = dx

    slab = pl.BlockSpec((S, tc), lambda j: (0, j))
    return pl.pallas_call(
        body, name=name, grid=(C // tc,),
        in_specs=[pl.BlockSpec((S, tc), lambda j: (0, off + j)), pl.BlockSpec((SSM_CONV, tc), lambda j: (0, j)),
                  pl.BlockSpec((1, tc), lambda j: (0, j)), slab],
        out_specs=[slab, pl.BlockSpec((SSM_CONV, tc), lambda j: (0, j)), pl.BlockSpec((1, tc), lambda j: (0, j))],
        out_shape=[jax.ShapeDtypeStruct((S, C), F32), jax.ShapeDtypeStruct((SSM_CONV, C), F32),
                   jax.ShapeDtypeStruct((1, C), F32)],
        compiler_params=_params(("parallel",)),
    )(zx, conv_w, conv_b, dout)


def _ssd_chunk_terms(dtraw, bias, a_log, tl):
    dt = _softplus(dtraw + bias)
    a_neg = -jnp.exp(a_log)
    ac = _dot_exact_left(tl, dt * a_neg)
    ac_last = ac[ac.shape[0] - 1:, :]
    return dt, a_neg, ac, ac.T, jnp.exp(ac), jnp.exp(ac_last - ac), jnp.exp(ac_last)


def _ssd_specs(S, L, G, hpg, pd, inner):
    gw = hpg * pd
    n = SSM_STATE
    xb = inner // n

    def mk(cidx):
        return dict(
            x=pl.BlockSpec((L, gw), lambda g, c: (cidx(c), g)),
            b=pl.BlockSpec((L, n), lambda g, c: (cidx(c), xb + g)),
            c=pl.BlockSpec((L, n), lambda g, c: (cidx(c), xb + G + g)),
            z=pl.BlockSpec((L, gw), lambda g, c: (cidx(c), g)),
            dt=pl.BlockSpec((L, LANES), lambda g, c: (cidx(c), g)),
            gvec=pl.BlockSpec((1, 1, LANES), lambda g, c: (g, 0, 0)),
            chan=pl.BlockSpec((1, gw), lambda g, c: (0, g)),
            tri=pl.BlockSpec((L, L), lambda g, c: (0, 0)),
            hp=pl.BlockSpec((1, 1, gw, n), lambda g, c: (g, cidx(c), 0, 0)),
            bc=pl.BlockSpec((L, n), lambda g, c: (cidx(c), g)),
        )
    return mk


def ssd_fwd(xbc, zx, dtg, bias_g, alog_g, d_chan, ngain, L, G, *, name):
    S = xbc.shape[0]
    n = SSM_STATE
    inner = xbc.shape[1] - 2 * G * n
    gw = inner // G
    pd = SB_HEAD_DIM
    hpg = gw // pd
    nc = S // L
    sp = _ssd_specs(S, L, G, hpg, pd, inner)(lambda c: c)

    def body(x_ref, b_ref, c_ref, z_ref, dt_ref, bias_ref, alog_ref, d_ref, ng_ref, tl_ref,
             yn_ref, y_ref, hp_ref, state):
        @pl.when(pl.program_id(1) == 0)
        def _():
            state[...] = jnp.zeros_like(state)

        dt, _, ac, act, ea, dte, cd = _ssd_chunk_terms(dt_ref[...], bias_ref[0], alog_ref[0], tl_ref[...])
        xv = x_ref[...]
        bm = b_ref[...].astype(BF16)
        cm = c_ref[...].astype(BF16)
        cb = _dot(cm, bm, 1, 1)
        tril = lax.broadcasted_iota(jnp.int32, (L, L), 1) <= lax.broadcasted_iota(jnp.int32, (L, L), 0)
        hp_ref[0, 0] = state[...]
        for r in range(hpg):
            ps = slice(r * pd, (r + 1) * pd)
            xr = xv[:, ps]
            xdt = xr * dt[:, r:r + 1]
            lm = jnp.exp(jnp.where(tril, ac[:, r:r + 1] - act[r:r + 1, :], -jnp.inf))
            hprev = state[ps, :]
            y = _dot((cb * lm).astype(BF16), xdt.astype(BF16))
            y = y + _dot(cm, hprev.astype(BF16), 1, 1) * ea[:, r:r + 1]
            y_ref[:, ps] = y + xr * d_ref[:, ps]
            st = _dot((xdt * dte[:, r:r + 1]).astype(BF16), bm, 0, 0)
            state[ps, :] = hprev * cd[:, r:r + 1] + st
        yfull = y_ref[...]
        zg = z_ref[...]
        yg = yfull * (zg * _sigmoid(zg))
        yn_ref[...] = (yg * lax.rsqrt(jnp.mean(yg * yg, axis=1, keepdims=True) + EPS) * ng_ref[...]).astype(BF16)

    return pl.pallas_call(
        body, name=name, grid=(G, nc),
        in_specs=[sp["x"], sp["b"], sp["c"], sp["z"], sp["dt"], sp["gvec"], sp["gvec"], sp["chan"], sp["chan"], sp["tri"]],
        out_specs=[sp["x"], sp["x"], sp["hp"]],
        out_shape=[jax.ShapeDtypeStruct((S, inner), BF16), jax.ShapeDtypeStruct((S, inner), F32),
                   jax.ShapeDtypeStruct((G, nc, gw, n), F32)],
        scratch_shapes=[pltpu.VMEM((gw, n), F32)],
        compiler_params=_params(("arbitrary", "arbitrary")),
    )(xbc, xbc, xbc, zx, dtg, bias_g, alog_g, d_chan, ngain, _tri(L, "row_ge_col"))


def ssd_bwd(xbc, zx, dtg, bias_g, alog_g, d_chan, ngain, yfull, hp, dyn, L, G, *, name):
    S = xbc.shape[0]
    n = SSM_STATE
    inner = xbc.shape[1] - 2 * G * n
    gw = inner // G
    pd = SB_HEAD_DIM
    hpg = gw // pd
    nc = S // L
    sp = _ssd_specs(S, L, G, hpg, pd, inner)(lambda c: nc - 1 - c)

    def body(x_ref, b_ref, c_ref, z_ref, dt_ref, bias_ref, alog_ref, d_ref, ng_ref, tl_ref, tu_ref,
             yf_ref, hp_ref, dyn_ref,
             dz_ref, dx_ref, db_ref, dc_ref, ddt_ref, dbias_ref, dalog_ref, dd_ref, dng_ref, dstate):
        first = pl.program_id(1) == 0

        @pl.when(first)
        def _():
            dstate[...] = jnp.zeros_like(dstate)
            dbias_ref[...] = jnp.zeros_like(dbias_ref)
            dalog_ref[...] = jnp.zeros_like(dalog_ref)
            dd_ref[...] = jnp.zeros_like(dd_ref)
            dng_ref[...] = jnp.zeros_like(dng_ref)

        dtraw = dt_ref[...]
        dt, a_neg, ac, act, ea, dte, cd = _ssd_chunk_terms(dtraw, bias_ref[0], alog_ref[0], tl_ref[...])
        xv = x_ref[...]
        bm = b_ref[...].astype(BF16)
        cm = c_ref[...].astype(BF16)
        cb = _dot(cm, bm, 1, 1)
        tril = lax.broadcasted_iota(jnp.int32, (L, L), 1) <= lax.broadcasted_iota(jnp.int32, (L, L), 0)
        lane = lax.broadcasted_iota(jnp.int32, (L, LANES), 1)
        lane1 = lax.broadcasted_iota(jnp.int32, (1, LANES), 1)

        yfull = yf_ref[...]
        zg = z_ref[...]
        sg = _sigmoid(zg)
        gate = zg * sg
        yg = yfull * gate
        rr = lax.rsqrt(jnp.mean(yg * yg, axis=1, keepdims=True) + EPS)
        yhat = yg * rr
        dynv = dyn_ref[...]
        dng_ref[...] += jnp.sum(dynv * yhat, axis=0, keepdims=True)
        t = dynv * ng_ref[...]
        dyg = rr * (t - yhat * jnp.mean(yhat * t, axis=1, keepdims=True))
        dy = dyg * gate
        dz_ref[...] = dyg * yfull * (sg * (1.0 + zg * (1.0 - sg)))

        dcb = jnp.zeros((L, L), F32)
        dc_acc = jnp.zeros((L, n), F32)
        db_acc = jnp.zeros((L, n), F32)
        dac = jnp.zeros((L, LANES), F32)
        xdx = jnp.zeros((L, LANES), F32)
        tail = jnp.zeros((1, LANES), F32)
        dskip = jnp.zeros((1, LANES), F32)
        ones_l = jnp.ones((L, LANES), BF16)
        for r in range(hpg):
            ps = slice(r * pd, (r + 1) * pd)
            xr = xv[:, ps]
            dyr = dy[:, ps]
            dtr = dt[:, r:r + 1]
            dter = dte[:, r:r + 1]
            cdr = cd[:, r:r + 1]
            xdt = xr * dtr
            xdtb = xdt.astype(BF16)
            dyrb = dyr.astype(BF16)
            lm = jnp.exp(jnp.where(tril, ac[:, r:r + 1] - act[r:r + 1, :], -jnp.inf))
            m32 = cb * lm
            mb = m32.astype(BF16)
            hprev = hp_ref[0, 0, ps, :]
            hpb = hprev.astype(BF16)
            dhn = dstate[ps, :]
            dhnb = dhn.astype(BF16)
            ear = ea[:, r:r + 1]
            gy = (dyr * ear).astype(BF16)
            dc_acc = dc_acc + _dot(gy, hpb)
            dstate[ps, :] = _dot(gy, cm, 0, 0) + dhn * cdr
            bdh = _dot(bm, dhnb, 1, 1)
            db_acc = db_acc + _dot((xdt * dter).astype(BF16), dhnb)
            dm = _dot(dyrb, xdtb, 1, 1)
            dxdt = bdh * dter + _dot(mb, dyrb, 0, 0)
            dcb = dcb + dm * lm
            wmat = dm * m32
            whi = wmat.astype(BF16)
            wlo = (wmat - whi.astype(F32)).astype(BF16)
            col_w = _dot(whi, ones_l, 0, 0) + _dot(wlo, ones_l, 0, 0)
            sdte = jnp.sum(xdt * bdh * dter, axis=1, keepdims=True)
            e_r = jnp.sum(wmat, axis=1, keepdims=True) + jnp.sum(dyr * _dot(cm, hpb, 1, 1) * ear, axis=1, keepdims=True) - sdte
            c_r = cdr * jnp.sum(jnp.sum(dhn * hprev, axis=1, keepdims=True), axis=0, keepdims=True) \
                + jnp.sum(sdte, axis=0, keepdims=True)
            dac = dac + jnp.where(lane == r, e_r - col_w, 0.0)
            xdx = xdx + jnp.where(lane == r, jnp.sum(dxdt * xr, axis=1, keepdims=True), 0.0)
            tail = tail + jnp.where(lane1 == r, c_r, 0.0)
            dskip = dskip + jnp.where(lane1 == r, jnp.sum(jnp.sum(dyr * xr, axis=1, keepdims=True), axis=0, keepdims=True), 0.0)
            dx_ref[:, ps] = dxdt * dtr + dyr * d_ref[:, ps]
        dcbb = dcb.astype(BF16)
        dc_ref[...] = dc_acc + _dot(dcbb, bm)
        db_ref[...] = db_acc + _dot(dcbb, cm, 0, 0)
        da = _dot_exact_left(tu_ref[...], dac) + tail
        real = lane < hpg
        ddt = jnp.where(real, (da * a_neg + xdx) * _sigmoid(dtraw + bias_ref[0]), 0.0)
        ddt_ref[...] = ddt
        dd_ref[0] += dskip
        dbias_ref[0] += jnp.sum(ddt, axis=0, keepdims=True)
        dalog_ref[0] += jnp.where(lane1 < hpg, jnp.sum(da * dt, axis=0, keepdims=True) * a_neg, 0.0)

    return pl.pallas_call(
        body, name=name, grid=(G, nc),
        in_specs=[sp["x"], sp["b"], sp["c"], sp["z"], sp["dt"], sp["gvec"], sp["gvec"], sp["chan"], sp["chan"],
                  sp["tri"], sp["tri"], sp["x"], sp["hp"], sp["x"]],
        out_specs=[sp["x"], sp["x"], sp["bc"], sp["bc"], sp["dt"], sp["gvec"], sp["gvec"], sp["gvec"], sp["chan"]],
        out_shape=[jax.ShapeDtypeStruct((S, inner), F32), jax.ShapeDtypeStruct((S, inner), F32),
                   jax.ShapeDtypeStruct((S, G * n), F32), jax.ShapeDtypeStruct((S, G * n), F32),
                   jax.ShapeDtypeStruct((S, G * LANES), F32), jax.ShapeDtypeStruct((G, 1, LANES), F32),
                   jax.ShapeDtypeStruct((G, 1, LANES), F32), jax.ShapeDtypeStruct((G, 1, LANES), F32),
                   jax.ShapeDtypeStruct((1, inner), F32)],
        scratch_shapes=[pltpu.VMEM((gw, n), F32)],
        compiler_params=_params(("arbitrary", "arbitrary")),
    )(xbc, xbc, xbc, zx, dtg, bias_g, alog_g, d_chan, ngain, _tri(L, "row_ge_col"), _tri(L, "row_le_col"),
      yfull, hp, dyn)


def _spread_dt(w_dt, G, hpg):
    K = w_dt.shape[0]
    w = w_dt.reshape(K, G, hpg)
    return jnp.pad(w, ((0, 0), (0, 0), (0, LANES - hpg))).reshape(K, G * LANES)


def _group_vec(v, G, hpg):
    return jnp.pad(v.reshape(G, 1, hpg), ((0, 0), (0, 0), (0, LANES - hpg)))


def local_step(x, target, W, late=None):
    S, D = x.shape
    depth = W["mix_norm"].shape[0]
    gm_groups, gm_chunk = W["gm_w_s"].shape[1], W["gm_w_s"].shape[2]
    heads = W["ssm_dt_bias"].shape[1]
    inner = heads * SB_HEAD_DIM
    L = gm_chunk
    received = None

    saved = []
    for i in range(depth):
        kind, j = i % 3, i // 3
        s = dict(x=x)
        h = rms_fwd(x, W["mix_norm"][i:i + 1], name=f"rms_mix_fwd")
        s["h"] = h
        if kind == 0:
            qkv = matmul(h, W["sb_w_qkv"][j], name="mm_qkv")
            if late is not None and i == 0:
                o, gathered = sb_attn_fwd(qkv, W["sb_q_gain"][j:j + 1], W["sb_k_gain"][j:j + 1], name="sb_fwd_gather",
                                          gather=late.shard)
                late.fill(W, gathered)
            else:
                o = sb_attn_fwd(qkv, W["sb_q_gain"][j:j + 1], W["sb_k_gain"][j:j + 1], name="sb_fwd")
            x1 = matmul(o, W["sb_w_o"][j], residual=x, name="mm_sb_out")
            s.update(qkv=qkv, o=o)
        elif kind == 1:
            wc = jnp.where(jnp.tril(jnp.ones((gm_chunk, gm_chunk), bool)), W["gm_w_s"][j], 0.0).astype(BF16)
            bsf = jnp.broadcast_to(W["gm_b_s"][j][:, :, None], (gm_groups, gm_chunk, W["gm_v_gain"].shape[1] // gm_groups)).astype(F32)
            zzpre = matmul(h, W["gm_w_in"][j], name="mm_gm_in")
            p = gmlp_fwd(zzpre, W["gm_b_in"][j:j + 1], W["gm_v_gain"][j:j + 1], wc, bsf, name="gm_fwd")
            x1 = matmul(p, W["gm_w_out"][j], residual=x, name="mm_gm_out")
            s.update(zzpre=zzpre, p=p, wc=wc, bsf=bsf)
        else:
            conv_dim = W["ssm_conv_w"].shape[2]
            G = (conv_dim - inner) // (2 * SSM_STATE)
            hpg = heads // G
            w_in = W["ssm_w_in"][j]
            w_zx = w_in[:, :inner + conv_dim]
            w_dtg = _spread_dt(w_in[:, inner + conv_dim:], G, hpg)
            bias_g = _group_vec(W["ssm_dt_bias"][j], G, hpg)
            alog_g = _group_vec(W["ssm_a_log"][j], G, hpg)
            d_chan = jnp.repeat(W["ssm_d"][j], SB_HEAD_DIM)[None, :]
            ngain = W["ssm_norm_gain"][j:j + 1]
            zx = matmul(h, w_zx, name="mm_ssm_zx")
            dtg = matmul(h, w_dtg, name="mm_ssm_dt")
            xbc = conv_fwd(zx, W["ssm_conv_w"][j], W["ssm_conv_b"][j:j + 1], inner, name="conv_fwd")
            yn, yfull, hp = ssd_fwd(xbc, zx, dtg, bias_g, alog_g, d_chan, ngain, L, G, name="ssd_fwd")
            x1 = matmul(yn, W["ssm_w_out"][j], residual=x, name="mm_ssm_out")
            s.update(w_zx=w_zx, w_dtg=w_dtg, bias_g=bias_g, alog_g=alog_g, d_chan=d_chan, ngain=ngain,
                     zx=zx, dtg=dtg, xbc=xbc, yn=yn, yfull=yfull, hp=hp)
        h2 = rms_fwd(x1, W["ffn_norm"][i:i + 1], name="rms_ffn_fwd")
        gu = matmul(h2, W["ffn_w_gu"][i], name="mm_ffn_gu")
        a = swiglu_fwd(gu, name="swiglu_fwd")
        x2 = matmul(a, W["ffn_w_down"][i], residual=x1, name="mm_ffn_down")
        s.update(x1=x1, h2=h2, gu=gu, a=a)
        saved.append(s)
        x = x2

    dx, loss = loss_head(x, target, name="loss_head")

    gw = {k: {} for k in WEIGHTS}
    for i in reversed(range(depth)):
        kind, j = i % 3, i // 3
        s = saved[i]
        da = matmul(dx, W["ffn_w_down"][i], tb=True, name="mm_ffn_da")
        gw["ffn_w_down"][i] = matmul(s["a"], dx, ta=True, out_dtype=BF16, name="mm_ffn_dwdown")
        dgu = swiglu_bwd(s["gu"], da, name="swiglu_bwd")
        dh2 = matmul(dgu, W["ffn_w_gu"][i], tb=True, name="mm_ffn_dh")
        gw["ffn_w_gu"][i] = matmul(s["h2"], dgu, ta=True, out_dtype=BF16, name="mm_ffn_dwgu")
        dx1, dgn = rms_bwd(s["x1"], W["ffn_norm"][i:i + 1], dh2, dx, name="rms_ffn_bwd")
        gw["ffn_norm"][i] = dgn[0]
        if kind == 0:
            do = matmul(dx1, W["sb_w_o"][j], tb=True, name="mm_sb_do")
            gw["sb_w_o"][j] = matmul(s["o"], dx1, ta=True, out_dtype=BF16, name="mm_sb_dwo")
            if late is not None and i == 0:
                dqkv, dqg, dkg, received = sb_attn_bwd(
                    s["qkv"], s["o"], do, W["sb_q_gain"][j:j + 1], W["sb_k_gain"][j:j + 1], name="sb_bwd_scatter",
                    scatter=late.contributions(gw))
            else:
                dqkv, dqg, dkg = sb_attn_bwd(s["qkv"], s["o"], do, W["sb_q_gain"][j:j + 1], W["sb_k_gain"][j:j + 1],
                                             name="sb_bwd")
            gw["sb_q_gain"][j] = dqg[0]
            gw["sb_k_gain"][j] = dkg[0]
            dh = matmul(dqkv, W["sb_w_qkv"][j], tb=True, name="mm_sb_dh")
            gw["sb_w_qkv"][j] = matmul(s["h"], dqkv, ta=True, out_dtype=BF16, name="mm_sb_dwqkv")
        elif kind == 1:
            dp = matmul(dx1, W["gm_w_out"][j], tb=True, name="mm_gm_dp")
            gw["gm_w_out"][j] = matmul(s["p"], dx1, ta=True, out_dtype=BF16, name="mm_gm_dwout")
            dzz, db_in, dvg, dws, dbs = gmlp_bwd(s["zzpre"], W["gm_b_in"][j:j + 1], W["gm_v_gain"][j:j + 1],
                                                s["wc"], s["bsf"], dp, name="gm_bwd")
            gw["gm_b_in"][j] = db_in[0]
            gw["gm_v_gain"][j] = dvg[0]
            gw["gm_w_s"][j] = dws
            gw["gm_b_s"][j] = dbs[:, :gm_groups].T
            dh = matmul(dzz, W["gm_w_in"][j], tb=True, name="mm_gm_dh")
            gw["gm_w_in"][j] = matmul(s["h"], dzz, ta=True, out_dtype=BF16, name="mm_gm_dwin")
        else:
            conv_dim = W["ssm_conv_w"].shape[2]
            G = (conv_dim - inner) // (2 * SSM_STATE)
            hpg = heads // G
            dyn = matmul(dx1, W["ssm_w_out"][j], tb=True, name="mm_ssm_dyn")
            gw["ssm_w_out"][j] = matmul(s["yn"], dx1, ta=True, out_dtype=BF16, name="mm_ssm_dwout")
            dz, dxs, dbm, dcm, ddt, dbias, dalog, dd, dng = ssd_bwd(
                s["xbc"], s["zx"], s["dtg"], s["bias_g"], s["alog_g"], s["d_chan"], s["ngain"], s["yfull"], s["hp"],
                dyn, L, G, name="ssd_bwd")
            dxbc = jnp.concatenate([dxs, dbm, dcm], axis=1)
            dpre, dcw, dcb = conv_bwd(s["zx"], W["ssm_conv_w"][j], W["ssm_conv_b"][j:j + 1], inner, dxbc,
                                      name="conv_bwd")
            dzx = jnp.concatenate([dz, dpre], axis=1)
            dh = matmul(ddt, s["w_dtg"], tb=True, name="mm_ssm_dh_dt")
            dh = matmul(dzx, s["w_zx"], tb=True, residual=dh, name="mm_ssm_dh")
            dw_zx = matmul(s["h"], dzx, ta=True, out_dtype=BF16, name="mm_ssm_dwzx")
            dw_dtg = matmul(s["h"], ddt, ta=True, out_dtype=BF16, name="mm_ssm_dwdt")
            dw_dt = dw_dtg.reshape(D, G, LANES)[:, :, :hpg].reshape(D, heads)
            gw["ssm_w_in"][j] = jnp.concatenate([dw_zx, dw_dt], axis=1)
            gw["ssm_conv_w"][j] = dcw
            gw["ssm_conv_b"][j] = dcb[0]
            gw["ssm_dt_bias"][j] = dbias[:, 0, :hpg].reshape(heads)
            gw["ssm_a_log"][j] = dalog[:, 0, :hpg].reshape(heads)
            gw["ssm_d"][j] = dd[:, 0, :hpg].reshape(heads)
            gw["ssm_norm_gain"][j] = dng[0]
        dx, dgn = rms_bwd(s["x"], W["mix_norm"][i:i + 1], dh, dx1, name="rms_mix_bwd")
        gw["mix_norm"][i] = dgn[0]

    return loss, dx, gw, received


MESH = pl.DeviceIdType.MESH
HBM_SPEC = pl.BlockSpec(memory_space=pltpu.HBM)
VMEM_SPEC = pl.BlockSpec(memory_space=pltpu.VMEM)


def _my_position():
    return lax.axis_index("x"), lax.axis_index("y"), lax.axis_index("c")


def _flip(v, bit):
    return 1 - v if bit else v


def all_gather_packed(shard, *, name):
    R, C = shard.shape

    def body(x_ref, out_ref, send_sems, recv_sems, local_sem):
        refs = (x_ref, out_ref, send_sems, recv_sems, local_sem)
        _ag_phase(refs, "start")
        _ag_phase(refs, "forward")
        _ag_phase(refs, "finish")

    return pl.pallas_call(
        body, name=name, out_shape=jax.ShapeDtypeStruct((N_DEV, R, C), shard.dtype),
        in_specs=[HBM_SPEC], out_specs=HBM_SPEC, scratch_shapes=COMM_SEMAPHORES,
    )(shard)


COMM_SEMAPHORES = [pltpu.SemaphoreType.DMA((7,)), pltpu.SemaphoreType.DMA((7,)), pltpu.SemaphoreType.DMA]


def _ag_phase(refs, phase):
    x_ref, out_ref, send_sems, recv_sems, local_sem = refs
    x, y, c = _my_position()
    me, sibling = (x, y, c), (x, y, 1 - c)
    chips = [(1 - x, y), (x, 1 - y), (1 - x, 1 - y)]

    def slot(px, py, pc):
        return out_ref.at[4 * px + 2 * py + pc]

    def copy(k, block, to, src=None):
        return pltpu.make_async_remote_copy(
            src_ref=slot(*block) if src is None else src, dst_ref=slot(*block),
            send_sem=send_sems.at[k], recv_sem=recv_sems.at[k], device_id=to, device_id_type=MESH)

    mine = pltpu.make_async_copy(x_ref, slot(*me), local_sem)
    first = [copy(0, me, sibling, src=x_ref)]
    first += [copy(1 + j, me, (*chip, c), src=x_ref) for j, chip in enumerate(chips)]
    passed = [copy(4 + j, (*chip, c), sibling) for j, chip in enumerate(chips)]
    if phase == "start":
        mine.start()
        for cp in first:
            cp.start()
    elif phase == "forward":
        for j, chip in enumerate(chips):
            copy(1 + j, (*chip, c), me).wait_recv()
            passed[j].start()
    else:
        copy(0, sibling, me).wait_recv()
        for j, chip in enumerate(chips):
            copy(4 + j, (*chip, 1 - c), me).wait_recv()
        for cp in first + passed:
            cp.wait_send()
        mine.wait()


def _rs_semaphores(n):
    return [pltpu.SemaphoreType.DMA((7 * n,)), pltpu.SemaphoreType.DMA((7 * n,)), pltpu.SemaphoreType.DMA((n,))]


def _rs_phase(g_refs, out_refs, sems, phase):
    send_sems, recv_sems, local_sems = sems
    x, y, c = _my_position()
    me = 4 * x + 2 * y + c
    copies = []
    for p, (g_ref, out_ref) in enumerate(zip(g_refs, out_refs)):
        copies.append(pltpu.make_async_copy(g_ref.at[me], out_ref.at[me], local_sems.at[p]))
        for k in range(1, N_DEV):
            px, py, pc = _flip(x, k & 4), _flip(y, k & 2), _flip(c, k & 1)
            copies.append(pltpu.make_async_remote_copy(
                src_ref=g_ref.at[4 * px + 2 * py + pc], dst_ref=out_ref.at[me],
                send_sem=send_sems.at[7 * p + k - 1], recv_sem=recv_sems.at[7 * p + k - 1],
                device_id=(px, py, pc), device_id_type=MESH))
    for cp in copies:
        if phase == "start":
            cp.start()
        else:
            cp.wait()


def exchange_for_reduce_scatter(gs, *, name):
    n = len(gs)

    def body(*refs):
        for phase in ("start", "finish"):
            _rs_phase(refs[:n], refs[n:2 * n], refs[2 * n:], phase)

    return pl.pallas_call(
        body, name=name, out_shape=[jax.ShapeDtypeStruct(g.shape, g.dtype) for g in gs],
        in_specs=[HBM_SPEC] * n, out_specs=[HBM_SPEC] * n, scratch_shapes=_rs_semaphores(n),
    )(*gs)


def sum_slots(recv, *, name):
    n, R, C = recv.shape
    tr = _pick(R, (512, 256, 128))

    def body(r_ref, o_ref):
        acc = r_ref[0].astype(F32)
        for s in range(1, n):
            acc = acc + r_ref[s].astype(F32)
        o_ref[...] = acc

    return pl.pallas_call(
        body, name=name, grid=(R // tr,), in_specs=[pl.BlockSpec((n, tr, C), lambda i: (0, i, 0))],
        out_specs=pl.BlockSpec((tr, C), lambda i: (i, 0)), out_shape=jax.ShapeDtypeStruct((R, C), F32),
        compiler_params=_params(("parallel",)),
    )(recv)


def all_reduce_small(v, *, name):
    R, C = v.shape

    def body(v_ref, o_ref, buf, send_sems, recv_sems):
        x, y, c = _my_position()
        me = 4 * x + 2 * y + c
        buf[me] = v_ref[...]
        copies = []
        for k in range(1, N_DEV):
            px, py, pc = _flip(x, k & 4), _flip(y, k & 2), _flip(c, k & 1)
            copies.append(pltpu.make_async_remote_copy(
                src_ref=v_ref, dst_ref=buf.at[me], send_sem=send_sems.at[k - 1], recv_sem=recv_sems.at[k - 1],
                device_id=(px, py, pc), device_id_type=MESH))
        for cp in copies:
            cp.start()
        for cp in copies:
            cp.wait()
        acc = buf[0]
        for s in range(1, N_DEV):
            acc = acc + buf[s]
        o_ref[...] = acc

    return pl.pallas_call(
        body, name=name, out_shape=jax.ShapeDtypeStruct((R, C), F32), in_specs=[VMEM_SPEC], out_specs=VMEM_SPEC,
        scratch_shapes=[pltpu.VMEM((N_DEV, R, C), F32), pltpu.SemaphoreType.DMA((7,)), pltpu.SemaphoreType.DMA((7,))],
        compiler_params=pltpu.CompilerParams(vmem_limit_bytes=VMEM_LIMIT_BYTES),
    )(v)


def _pad_rows(a, mult):
    pad = (-a.shape[0]) % mult
    return jnp.pad(a, ((0, pad), (0, 0))) if pad else a


def _pack_rows(arrays, cols, row_mult):
    return _pad_rows(jnp.concatenate([a.reshape(-1, cols) for a in arrays], axis=0), row_mult)


def _unpack_rows(packed, shapes):
    out, r = [], 0
    cols = packed.shape[-1]
    for shp in shapes:
        n = math.prod(shp) // cols
        out.append(packed[..., r:r + n, :].reshape(packed.shape[:-2] + tuple(shp)))
        r += n
    return out


def _gathered_to_full(g, name):
    if name in COL_SHARDED or name == "ssm_conv_w":
        return jnp.moveaxis(g, 0, -2).reshape(g.shape[1:-1] + (N_DEV * g.shape[-1],))
    if name in ("ssm_conv_b", "ssm_norm_gain"):
        return jnp.moveaxis(g, 0, -2).reshape(g.shape[1:-1] + (N_DEV * g.shape[-1],))
    return jnp.moveaxis(g, 0, 1).reshape((g.shape[1], N_DEV * g.shape[2], g.shape[3]))


def _full_to_shards(full, name):
    if name in COL_SHARDED:
        n = full.shape[-1] // N_DEV
        return jnp.moveaxis(full.reshape(full.shape[:-1] + (N_DEV, n)), -2, 0)
    k = full.shape[1] // N_DEV
    return jnp.moveaxis(full.reshape((full.shape[0], N_DEV, k, full.shape[2])), 1, 0)


def _pack_small(arrays):
    flat = []
    for a in arrays:
        f = a.reshape(-1).astype(F32)
        flat.append(jnp.pad(f, (0, (-f.shape[0]) % LANES)))
    return _pad_rows(jnp.concatenate(flat).reshape(-1, LANES), 8)


def _unpack_small(packed, shapes):
    flat = packed.reshape(-1)
    out, r = [], 0
    for shp in shapes:
        n = math.prod(shp)
        out.append(flat[r:r + n].reshape(shp))
        r += n + (-n) % LANES
    return out


ARG_NAMES = ("x",) + WEIGHTS + ("loss_target",) + tuple("m_" + w for w in WEIGHTS) + tuple("v_" + w for w in WEIGHTS)


def kernel(x, mix_norm, ffn_norm, sb_w_qkv, sb_q_gain, sb_k_gain, sb_w_o, gm_w_in, gm_b_in, gm_v_gain, gm_w_s, gm_b_s, gm_w_out, ssm_w_in, ssm_conv_w, ssm_conv_b, ssm_dt_bias, ssm_a_log, ssm_d, ssm_norm_gain, ssm_w_out, ffn_w_gu, ffn_w_down, loss_target, m_mix_norm, m_ffn_norm, m_sb_w_qkv, m_sb_q_gain, m_sb_k_gain, m_sb_w_o, m_gm_w_in, m_gm_b_in, m_gm_v_gain, m_gm_w_s, m_gm_b_s, m_gm_w_out, m_ssm_w_in, m_ssm_conv_w, m_ssm_conv_b, m_ssm_dt_bias, m_ssm_a_log, m_ssm_d, m_ssm_norm_gain, m_ssm_w_out, m_ffn_w_gu, m_ffn_w_down, v_mix_norm, v_ffn_norm, v_sb_w_qkv, v_sb_q_gain, v_sb_k_gain, v_sb_w_o, v_gm_w_in, v_gm_b_in, v_gm_v_gain, v_gm_w_s, v_gm_b_s, v_gm_w_out, v_ssm_w_in, v_ssm_conv_w, v_ssm_conv_b, v_ssm_dt_bias, v_ssm_a_log, v_ssm_d, v_ssm_norm_gain, v_ssm_w_out, v_ffn_w_gu, v_ffn_w_down):
    given = dict(zip(ARG_NAMES, (x, mix_norm, ffn_norm, sb_w_qkv, sb_q_gain, sb_k_gain, sb_w_o, gm_w_in, gm_b_in, gm_v_gain, gm_w_s, gm_b_s, gm_w_out, ssm_w_in, ssm_conv_w, ssm_conv_b, ssm_dt_bias, ssm_a_log, ssm_d, ssm_norm_gain, ssm_w_out, ffn_w_gu, ffn_w_down, loss_target, m_mix_norm, m_ffn_norm, m_sb_w_qkv, m_sb_q_gain, m_sb_k_gain, m_sb_w_o, m_gm_w_in, m_gm_b_in, m_gm_v_gain, m_gm_w_s, m_gm_b_s, m_gm_w_out, m_ssm_w_in, m_ssm_conv_w, m_ssm_conv_b, m_ssm_dt_bias, m_ssm_a_log, m_ssm_d, m_ssm_norm_gain, m_ssm_w_out, m_ffn_w_gu, m_ffn_w_down, v_mix_norm, v_ffn_norm, v_sb_w_qkv, v_sb_q_gain, v_sb_k_gain, v_sb_w_o, v_gm_w_in, v_gm_b_in, v_gm_v_gain, v_gm_w_s, v_gm_b_s, v_gm_w_out, v_ssm_w_in, v_ssm_conv_w, v_ssm_conv_b, v_ssm_dt_bias, v_ssm_a_log, v_ssm_d, v_ssm_norm_gain, v_ssm_w_out, v_ffn_w_gu, v_ffn_w_down)))
    mx, my, mc = _my_position()
    me = 4 * mx + 2 * my + mc

    pieces = [(k, l) for k in BIG for l in range(given[k].shape[0])]
    early = [("sb_w_qkv", 0), ("sb_w_o", 0), ("ffn_w_gu", 0), ("ffn_w_down", 0)]
    late_pieces = [p for p in pieces if p not in early]
    last = [("sb_w_qkv", 0)]
    main = [p for p in pieces if p not in last]
    row_mult = 256

    def rows_of(p):
        return math.prod(given[p[0]].shape[1:]) // PACK_COLS

    def pack_shards(ps, prefix="", dtype=F32, extra=()):
        parts = [given[prefix + k][l].astype(dtype).reshape(-1, PACK_COLS) for k, l in ps] + list(extra)
        return _pad_rows(jnp.concatenate(parts, axis=0), row_mult)

    def split_rows(packed, ps):
        out, r = [], 0
        for p in ps:
            out.append(packed[..., r:r + rows_of(p), :])
            r += rows_of(p)
        return out

    def piece_to_full(g, k):
        shp = given[k].shape[1:]
        g = g.reshape((N_DEV,) + shp)
        if k in COL_SHARDED:
            return jnp.moveaxis(g, 0, 1).reshape(shp[0], N_DEV * shp[1])
        return g.reshape(N_DEV * shp[0], shp[1])

    def full_to_piece(full, k):
        shp = given[k].shape[1:]
        g = jnp.moveaxis(full.reshape(shp[0], N_DEV, shp[1]), 1, 0) if k in COL_SHARDED else full
        return g.reshape(N_DEV, -1, PACK_COLS)

    def contributions(gw, ps):
        return [full_to_piece(gw[k][l], k) for k, l in ps]

    sharded_small = [lax.bitcast_convert_type(given[k], BF16) for k in SMALL_SHARDED]
    tail = jnp.concatenate([a.reshape(-1) for a in sharded_small])
    tail = jnp.pad(tail, (0, (-tail.size) % PACK_COLS)).reshape(-1, PACK_COLS)

    W = {k: given[k] for k in SMALL if k not in SMALL_SHARDED}
    W.update({k: [None] * given[k].shape[0] for k in BIG})
    gathered_early = all_gather_packed(pack_shards(early, dtype=BF16), name="all_gather_early")
    for (k, l), g in zip(early, split_rows(gathered_early, early)):
        W[k][l] = piece_to_full(g, k)

    class Late:
        shard = pack_shards(late_pieces, dtype=BF16, extra=[tail])

        @staticmethod
        def fill(weights, gathered):
            for (k, l), g in zip(late_pieces, split_rows(gathered, late_pieces)):
                weights[k][l] = piece_to_full(g, k)
            r0 = sum(rows_of(p) for p in late_pieces)
            tail_g = gathered[:, r0:r0 + tail.shape[0], :].reshape(N_DEV, -1)
            off = 0
            for k, a in zip(SMALL_SHARDED, sharded_small):
                g = lax.bitcast_convert_type(tail_g[:, off:off + a.size].reshape((N_DEV,) + a.shape), F32)
                weights[k] = jnp.moveaxis(g, 0, -2).reshape(g.shape[1:-1] + (N_DEV * g.shape[-1],))
                off += a.size

        @staticmethod
        def contributions(gw):
            return contributions(gw, main)

    loss, gx, gw, received_main = local_step(given["x"][0], given["loss_target"][0], W, late=Late)
    received_last = exchange_for_reduce_scatter(contributions(gw, last), name="reduce_scatter_last")

    grads_small = {k: jnp.stack([gw[k][l] for l in sorted(gw[k])], axis=0) for k in SMALL}
    small_shapes = [grads_small[k].shape for k in SMALL] + [(1, 1)]
    reduced = all_reduce_small(_pack_small([grads_small[k] for k in SMALL] + [loss]), name="all_reduce_small")
    small_full = dict(zip(SMALL + ("loss",), _unpack_small(reduced, small_shapes)))

    g_piece = {}
    for grp, received in ((main, received_main), (last, received_last)):
        for p, r in zip(grp, received):
            g_piece[p] = sum_slots(r, name="reduce_scatter_sum").reshape(given[p[0]].shape[1:])
    out_g, out_d, out_m, out_v = {}, {}, {}, {}
    for k in BIG:
        g = jnp.concatenate([g_piece[(k, l)] for l in range(given[k].shape[0])], axis=0)
        d, nm, nv = adamw(given[k].reshape(g.shape), g, given["m_" + k].reshape(g.shape),
                          given["v_" + k].reshape(g.shape), name="adamw_" + k)
        out_g[k], out_d[k], out_m[k], out_v[k] = (a.reshape(given[k].shape) for a in (g, d, nm, nv))

    gsmall = {}
    for k in SMALL:
        g = small_full[k]
        if k in SMALL_SHARDED:
            n = given[k].shape[-1]
            g = lax.dynamic_slice_in_dim(g, me * n, n, axis=g.ndim - 1)
        gsmall[k] = g
    local_shapes = [given[k].shape for k in SMALL]
    dsm, nmsm, nvsm = adamw(_pack_small([given[k] for k in SMALL]), _pack_small([gsmall[k] for k in SMALL]),
                            _pack_small([given["m_" + k] for k in SMALL]), _pack_small([given["v_" + k] for k in SMALL]),
                            name="adamw_small")
    out_g.update(gsmall)
    for dst, src in ((out_d, dsm), (out_m, nmsm), (out_v, nvsm)):
        dst.update(zip(SMALL, _unpack_small(src, local_shapes)))

    return (small_full["loss"].reshape(()), gx[None],
            *[out_g[k] for k in WEIGHTS], *[out_d[k] for k in WEIGHTS],
            *[out_m[k] for k in WEIGHTS], *[out_v[k] for k in WEIGHTS])
```

```python
import math

import jax
import jax.numpy as jnp
from jax import lax
from jax.experimental import pallas as pl
from jax.experimental.pallas import tpu as pltpu

F32 = jnp.float32
BF16 = jnp.bfloat16
EPS = 1e-6
N_DEV = 8
SB_HEAD_DIM = 64
SSM_STATE = 128
SSM_CONV = 4
ADAM_LR = 0.001
ADAM_B1 = 0.9
ADAM_B2 = 0.999
ADAM_EPS = 1e-08
ADAM_WD = 0.01
ADAM_STEP = 10
VMEM_LIMIT_BYTES = 56 * 1024 * 1024
MATMUL_VMEM_BUDGET = 40 * 1024 * 1024
LANES = 128
PACK_COLS = 1024

BIG = ("sb_w_qkv", "sb_w_o", "gm_w_in", "gm_w_out", "ssm_w_in", "ssm_w_out", "ffn_w_gu", "ffn_w_down")
COL_SHARDED = ("sb_w_qkv", "gm_w_in", "ssm_w_in", "ffn_w_gu")
SMALL = ("mix_norm", "ffn_norm", "sb_q_gain", "sb_k_gain", "gm_b_in", "gm_v_gain", "gm_w_s", "gm_b_s",
         "ssm_conv_w", "ssm_conv_b", "ssm_dt_bias", "ssm_a_log", "ssm_d", "ssm_norm_gain")
SMALL_SHARDED = ("ssm_conv_w", "ssm_conv_b", "ssm_norm_gain")
WEIGHTS = ("mix_norm", "ffn_norm", "sb_w_qkv", "sb_q_gain", "sb_k_gain", "sb_w_o", "gm_w_in", "gm_b_in",
           "gm_v_gain", "gm_w_s", "gm_b_s", "gm_w_out", "ssm_w_in", "ssm_conv_w", "ssm_conv_b", "ssm_dt_bias",
           "ssm_a_log", "ssm_d", "ssm_norm_gain", "ssm_w_out", "ffn_w_gu", "ffn_w_down")


def _params(semantics=None):
    return pltpu.CompilerParams(dimension_semantics=semantics, vmem_limit_bytes=VMEM_LIMIT_BYTES)


def _pick(n, prefs):
    for t in prefs:
        if t <= n and n % t == 0:
            return t
    return n


def _dot(a, b, ca=1, cb=0):
    return lax.dot_general(a, b, (((ca,), (cb,)), ((), ())), preferred_element_type=F32)


def _split3(v):
    h1 = v.astype(BF16)
    r1 = v - h1.astype(F32)
    h2 = r1.astype(BF16)
    h3 = (r1 - h2.astype(F32)).astype(BF16)
    return h1, h2, h3


def _dot_exact_left(mat01, v):
    h1, h2, h3 = _split3(v)
    return _dot(mat01, h1) + _dot(mat01, h2) + _dot(mat01, h3)


def _dot_split2_right(v, mat01):
    hi = v.astype(BF16)
    lo = (v - hi.astype(F32)).astype(BF16)
    return _dot(hi, mat01) + _dot(lo, mat01)


def _sigmoid(v):
    return 1.0 / (1.0 + jnp.exp(-v))


def _softplus(v):
    return jnp.maximum(v, 0.0) + jnp.log(1.0 + jnp.exp(-jnp.abs(v)))


def _erf(v):
    a = jnp.abs(v)
    t = 1.0 / (1.0 + 0.3275911 * a)
    poly = t * (0.254829592 + t * (-0.284496736 + t * (1.421413741 + t * (-1.453152027 + t * 1.061405429))))
    e = 1.0 - poly * jnp.exp(-a * a)
    return jnp.where(v < 0, -e, e)


def _gelu_and_grad(v):
    cdf = 0.5 * (1.0 + _erf(v * (1.0 / math.sqrt(2.0))))
    pdf = jnp.exp(-0.5 * v * v) * (1.0 / math.sqrt(2.0 * math.pi))
    return v * cdf, cdf + v * pdf


def matmul(a, b, *, ta=False, tb=False, out_dtype=F32, residual=None, a_split=1, b_split=1, name):
    if a_split > 1:
        assert not ta and a.shape[0] == a_split
        M, K = a.shape[1], a_split * a.shape[2]
    elif ta:
        K, M = a.shape
    else:
        M, K = a.shape
    if b_split > 1:
        assert not tb and b.shape[0] == b_split
        Kb, N = b.shape[1], b_split * b.shape[2]
    elif tb:
        N, Kb = b.shape
    else:
        Kb, N = b.shape
    assert K == Kb, (a.shape, b.shape, ta, tb)
    has_res = residual is not None
    tm = _pick(M, (1024, 1408, 768, 512, 256, 128))
    tn = _pick(N // b_split, (1024, 1408, 1536, 768, 512, 256, 128))

    def vmem_bytes(tk):
        tiles = tm * tk * a.dtype.itemsize + tk * tn * b.dtype.itemsize
        outs = tm * tn * jnp.dtype(out_dtype).itemsize + (tm * tn * 4 if has_res else 0)
        return 2 * tiles + 2 * outs + (tm * tn * 4 if tk < K else 0)

    kp = K // a_split
    tk = next((t for t in (K, 2048, 1408, 1024, 512, 256) if t <= kp and kp % t == 0 and vmem_bytes(t) <= MATMUL_VMEM_BUDGET),
              _pick(kp, (128,)))
    nk = K // tk
    if a_split > 1:
        nkb = kp // tk
        a_spec = pl.BlockSpec((None, tm, tk), lambda i, j, k: (k // nkb, i, k % nkb))
    else:
        a_spec = pl.BlockSpec((tk, tm), lambda i, j, k: (k, i)) if ta else pl.BlockSpec((tm, tk), lambda i, j, k: (i, k))
    if b_split > 1:
        njb = N // b_split // tn
        b_spec = pl.BlockSpec((None, tk, tn), lambda i, j, k: (j // njb, k, j % njb))
    else:
        b_spec = pl.BlockSpec((tn, tk), lambda i, j, k: (j, k)) if tb else pl.BlockSpec((tk, tn), lambda i, j, k: (k, j))
    o_spec = pl.BlockSpec((tm, tn), lambda i, j, k: (i, j))
    ca, cb = (0 if ta else 1), (1 if tb else 0)

    def body(*refs):
        a_ref, b_ref = refs[:2]
        r_ref = refs[2] if has_res else None
        o_ref = refs[3] if has_res else refs[2]

        def finish(r):
            if has_res:
                r = r + r_ref[...]
            o_ref[...] = r.astype(out_dtype)

        def part():
            return _dot(a_ref[...].astype(BF16), b_ref[...].astype(BF16), ca, cb)

        if nk == 1:
            finish(part())
            return
        acc = refs[-1]
        k = pl.program_id(2)

        @pl.when(k == 0)
        def _():
            acc[...] = part()

        @pl.when(jnp.logical_and(k > 0, k < nk - 1))
        def _():
            acc[...] += part()

        @pl.when(k == nk - 1)
        def _():
            finish(acc[...] + part())

    in_specs = [a_spec, b_spec] + ([o_spec] if has_res else [])
    args = (a, b) + ((residual,) if has_res else ())
    return pl.pallas_call(
        body, name=name, grid=(M // tm, N // tn, nk), in_specs=in_specs, out_specs=o_spec,
        out_shape=jax.ShapeDtypeStruct((M, N), out_dtype),
        scratch_shapes=[pltpu.VMEM((tm, tn), F32)] if nk > 1 else [],
        compiler_params=_params(("parallel", "parallel", "arbitrary")),
    )(*args)


def rms_fwd(x, gain, *, name):
    S, D = x.shape
    tr = _pick(S, (512, 256, 128))

    def body(x_ref, g_ref, o_ref):
        xv = x_ref[...]
        r = lax.rsqrt(jnp.mean(xv * xv, axis=1, keepdims=True) + EPS)
        o_ref[...] = (xv * r * g_ref[...]).astype(BF16)

    return pl.pallas_call(
        body, name=name, grid=(S // tr,),
        in_specs=[pl.BlockSpec((tr, D), lambda i: (i, 0)), pl.BlockSpec((1, D), lambda i: (0, 0))],
        out_specs=pl.BlockSpec((tr, D), lambda i: (i, 0)), out_shape=jax.ShapeDtypeStruct((S, D), BF16),
        compiler_params=_params(("parallel",)),
    )(x, gain)


def rms_bwd(x, gain, dh, dres, *, name):
    S, D = x.shape
    tr = _pick(S, (512, 256, 128))

    def body(x_ref, g_ref, dh_ref, dr_ref, dx_ref, dg_ref):
        @pl.when(pl.program_id(0) == 0)
        def _():
            dg_ref[...] = jnp.zeros_like(dg_ref)

        xv = x_ref[...]
        dhv = dh_ref[...]
        r = lax.rsqrt(jnp.mean(xv * xv, axis=1, keepdims=True) + EPS)
        xhat = xv * r
        t = dhv * g_ref[...]
        dx_ref[...] = dr_ref[...] + r * (t - xhat * jnp.mean(xhat * t, axis=1, keepdims=True))
        dg_ref[...] += jnp.sum(dhv * xhat, axis=0, keepdims=True)

    row = pl.BlockSpec((tr, D), lambda i: (i, 0))
    vec = pl.BlockSpec((1, D), lambda i: (0, 0))
    return pl.pallas_call(
        body, name=name, grid=(S // tr,), in_specs=[row, vec, row, row], out_specs=[row, vec],
        out_shape=[jax.ShapeDtypeStruct((S, D), F32), jax.ShapeDtypeStruct((1, D), F32)],
        compiler_params=_params(("arbitrary",)),
    )(x, gain, dh, dres)


def swiglu_fwd(gu, *, name):
    S, F2 = gu.shape
    F = F2 // 2
    tr = _pick(S, (256, 128))

    def body(gu_ref, o_ref):
        g = gu_ref[:, :F]
        u = gu_ref[:, F:]
        o_ref[...] = (g * _sigmoid(g) * u).astype(BF16)

    return pl.pallas_call(
        body, name=name, grid=(S // tr,), in_specs=[pl.BlockSpec((tr, F2), lambda i: (i, 0))],
        out_specs=pl.BlockSpec((tr, F), lambda i: (i, 0)), out_shape=jax.ShapeDtypeStruct((S, F), BF16),
        compiler_params=_params(("parallel",)),
    )(gu)


def swiglu_bwd(gu, da, *, name):
    S, F2 = gu.shape
    F = F2 // 2
    tr = _pick(S, (256, 128))

    def body(gu_ref, da_ref, o_ref):
        g = gu_ref[:, :F]
        u = gu_ref[:, F:]
        dav = da_ref[...]
        s = _sigmoid(g)
        o_ref[:, :F] = (dav * u * (s * (1.0 + g * (1.0 - s)))).astype(BF16)
        o_ref[:, F:] = (dav * g * s).astype(BF16)

    return pl.pallas_call(
        body, name=name, grid=(S // tr,),
        in_specs=[pl.BlockSpec((tr, F2), lambda i: (i, 0)), pl.BlockSpec((tr, F), lambda i: (i, 0))],
        out_specs=pl.BlockSpec((tr, F2), lambda i: (i, 0)), out_shape=jax.ShapeDtypeStruct((S, F2), BF16),
        compiler_params=_params(("parallel",)),
    )(gu, da)


def loss_head(y, target, *, name):
    S, D = y.shape
    tr = _pick(S, (512, 256, 128))

    def body(y_ref, t_ref, dy_ref, l_ref):
        @pl.when(pl.program_id(0) == 0)
        def _():
            l_ref[...] = jnp.zeros_like(l_ref)

        err = y_ref[...] - t_ref[...]
        dy_ref[...] = err * (1.0 / D)
        l_ref[...] += jnp.sum(0.5 * jnp.mean(err * err, axis=1, keepdims=True), axis=0, keepdims=True)

    row = pl.BlockSpec((tr, D), lambda i: (i, 0))
    one = pl.BlockSpec((1, 1), lambda i: (0, 0))
    dy, l = pl.pallas_call(
        body, name=name, grid=(S // tr,), in_specs=[row, row], out_specs=[row, one],
        out_shape=[jax.ShapeDtypeStruct((S, D), F32), jax.ShapeDtypeStruct((1, 1), F32)],
        compiler_params=_params(("arbitrary",)),
    )(y, target)
    return dy, l


def adamw(w, g, m, v, *, name):
    R, C = w.shape[-2:]
    tr = _pick(R, (512, 256, 128, 64, 32, 16, 8))

    def body(w_ref, g_ref, m_ref, v_ref, d_ref, mo_ref, vo_ref):
        gv = g_ref[...]
        mn = ADAM_B1 * m_ref[...] + (1.0 - ADAM_B1) * gv
        vn = ADAM_B2 * v_ref[...] + (1.0 - ADAM_B2) * jnp.square(gv)
        m_hat = mn / (1.0 - ADAM_B1 ** ADAM_STEP)
        v_hat = vn / (1.0 - ADAM_B2 ** ADAM_STEP)
        d_ref[...] = -ADAM_LR * (m_hat / (jnp.sqrt(v_hat) + ADAM_EPS) + ADAM_WD * w_ref[...])
        mo_ref[...] = mn
        vo_ref[...] = vn

    if w.ndim == 3:
        grid = (w.shape[0], R // tr)
        blk = pl.BlockSpec((None, tr, C), lambda l, i: (l, i, 0))
    else:
        grid = (R // tr,)
        blk = pl.BlockSpec((tr, C), lambda i: (i, 0))
    sds = jax.ShapeDtypeStruct(w.shape, F32)
    return pl.pallas_call(
        body, name=name, grid=grid, in_specs=[blk] * 4, out_specs=[blk] * 3, out_shape=[sds] * 3,
        compiler_params=_params(("parallel",) * len(grid)),
    )(w, g, m, v)


def _tri(n, kind):
    r = lax.broadcasted_iota(jnp.int32, (n, n), 0)
    c = lax.broadcasted_iota(jnp.int32, (n, n), 1)
    if kind == "row_gt_col":
        return (r > c).astype(BF16)
    if kind == "row_ge_col":
        return (r >= c).astype(BF16)
    if kind == "row_le_col":
        return (r <= c).astype(BF16)
    raise ValueError(kind)


def _sb_tile(qs, kj, r_carry, u_strict, masked):
    z = _dot(qs, kj, 1, 1)
    lb = jnp.minimum(z, 0.0) - jnp.log(1.0 + jnp.exp(-jnp.abs(z)))
    l1m = lb - z
    keep = None
    if masked:
        tq, tk = z.shape
        keep = lax.broadcasted_iota(jnp.int32, (tq, tk), 1) < lax.broadcasted_iota(jnp.int32, (tq, tk), 0)
        l1m = jnp.where(keep, l1m, 0.0)
    w = jnp.exp(lb + _dot(l1m.astype(BF16), u_strict) + r_carry)
    if masked:
        w = jnp.where(keep, w, 0.0)
    return lb, l1m, w, keep


def _sb_prep(T, nb, hd, refs_in, gains, scratch):
    q_scale = 1.0 / math.sqrt(hd)
    assert math.log2(q_scale) == round(math.log2(q_scale))

    def prep(i, _):
        rows = pl.ds(pl.multiple_of(i * T, T), T)
        for hh in range(2):
            sl = slice(hd * hh, hd * hh + hd)
            for n, (src, dst) in enumerate(zip(refs_in, scratch)):
                v = src[rows, sl]
                if n < 2:
                    v = v * lax.rsqrt(jnp.mean(v * v, axis=1, keepdims=True) + EPS) * gains[n][...]
                if n == 0:
                    v = v * q_scale
                dst[hh, rows, :] = v.astype(BF16)
        return 0

    lax.fori_loop(0, nb, prep, 0)


def _sb_chains(m, T):
    rows = [pl.ds(pl.multiple_of((2 * m + qb) * T, T), T) for qb in range(2)]
    return rows, [(hh, qb) for qb in range(2) for hh in range(2)]


def sb_attn_fwd(qkv, q_gain, k_gain, *, name, gather=None):
    S, D3 = qkv.shape
    D = D3 // 3
    npairs = D // LANES
    hd = SB_HEAD_DIM
    T = min(256, S)
    nb = S // T
    assert nb % 2 == 0

    def body(*refs):
        if gather is None:
            q_ref, k_ref, v_ref, qg_ref, kg_ref, us_ref, o_ref, qn_s, kn_s, vb_s = refs
        else:
            q_ref, k_ref, v_ref, qg_ref, kg_ref, us_ref, ag_in, o_ref, ag_out, qn_s, kn_s, vb_s = refs[:12]
            comm = (ag_in, ag_out) + refs[12:]
            step = pl.program_id(0)
            pl.when(step == 0)(lambda: _ag_phase(comm, "start"))
            pl.when(step == (5 * npairs) // 8)(lambda: _ag_phase(comm, "forward"))
        us = us_ref[...]
        _sb_prep(T, nb, hd, (q_ref, k_ref, v_ref), (qg_ref, kg_ref), (qn_s, kn_s, vb_s))

        def superblock(m, _):
            rows_q, chains = _sb_chains(m, T)
            qs = {ch: qn_s[ch[0], rows_q[ch[1]], :] for ch in chains}

            def tile(j, carry, which):
                rows_j = pl.ds(pl.multiple_of(j * T, T), T)
                new = dict(carry)
                for ch, masked in which:
                    acc, rc = carry[ch]
                    _, l1m, w, _ = _sb_tile(qs[ch], kn_s[ch[0], rows_j, :], rc, us, masked)
                    new[ch] = (acc + _dot(w.astype(BF16), vb_s[ch[0], rows_j, :]),
                               rc + jnp.sum(l1m, axis=1, keepdims=True))
                return new

            carry = {ch: (jnp.zeros((T, hd), F32), jnp.zeros((T, 1), F32)) for ch in chains}
            carry = tile(2 * m + 1, carry, [(ch, True) for ch in chains if ch[1] == 1])
            carry = tile(2 * m, carry, [(ch, ch[1] == 0) for ch in chains])
            carry = lax.fori_loop(0, 2 * m, lambda jj, c: tile(2 * m - 1 - jj, c, [(ch, False) for ch in chains]),
                                  carry)
            for qb in range(2):
                o_ref[rows_q[qb], :] = jnp.concatenate([carry[(0, qb)][0], carry[(1, qb)][0]], axis=1)
            return 0

        lax.fori_loop(0, nb // 2, superblock, 0)
        if gather is not None:
            pl.when(step == npairs - 1)(lambda: _ag_phase(comm, "finish"))

    col = lambda off: pl.BlockSpec((S, LANES), lambda p, off=off: (0, off + p))
    gain = pl.BlockSpec((1, hd), lambda p: (0, 0))
    in_specs = [col(0), col(npairs), col(2 * npairs), gain, gain, pl.BlockSpec((T, T), lambda p: (0, 0))]
    out_specs = [pl.BlockSpec((S, LANES), lambda p: (0, p))]
    out_shape = [jax.ShapeDtypeStruct((S, D), F32)]
    scratch = [pltpu.VMEM((2, S, hd), BF16)] * 3
    args = [qkv, qkv, qkv, q_gain, k_gain, _tri(T, "row_gt_col")]
    if gather is not None:
        in_specs.append(HBM_SPEC)
        out_specs.append(HBM_SPEC)
        out_shape.append(jax.ShapeDtypeStruct((N_DEV,) + gather.shape, gather.dtype))
        scratch += COMM_SEMAPHORES
        args.append(gather)
    out = pl.pallas_call(
        body, name=name, grid=(npairs,), in_specs=in_specs, out_specs=out_specs, out_shape=out_shape,
        scratch_shapes=scratch, compiler_params=_params(("arbitrary",)),
    )(*args)
    return out[0] if gather is None else tuple(out)


def sb_attn_bwd(qkv, o, do, q_gain, k_gain, *, name, scatter=None):
    S, D3 = qkv.shape
    D = D3 // 3
    npairs = D // LANES
    hd = SB_HEAD_DIM
    T = min(256, S)
    nb = S // T
    scale = 1.0 / math.sqrt(hd)

    def body(*refs):
        if scatter is None:
            (q_ref, k_ref, v_ref, o_ref, do_ref, qg_ref, kg_ref, us_ref,
             dqkv_ref, dg_ref, qn_s, kn_s, vb_s, dob_s) = refs
        else:
            ns = len(scatter)
            q_ref, k_ref, v_ref, o_ref, do_ref, qg_ref, kg_ref, us_ref = refs[:8]
            rs_in = refs[8:8 + ns]
            dqkv_ref, dg_ref = refs[8 + ns:10 + ns]
            rs_out = refs[10 + ns:10 + 2 * ns]
            qn_s, kn_s, vb_s, dob_s = refs[10 + 2 * ns:14 + 2 * ns]
            rs_sems = refs[14 + 2 * ns:]
            pl.when(pl.program_id(0) == 0)(lambda: _rs_phase(rs_in, rs_out, rs_sems, "start"))
        dq_ref, dk_ref, dv_ref = dqkv_ref.at[0], dqkv_ref.at[1], dqkv_ref.at[2]

        @pl.when(pl.program_id(0) == 0)
        def _():
            dg_ref[...] = jnp.zeros_like(dg_ref)

        us = us_ref[...]
        u_prefix = (1.0 - us.astype(F32)).astype(BF16)
        _sb_prep(T, nb, hd, (q_ref, k_ref, v_ref, do_ref), (qg_ref, kg_ref), (qn_s, kn_s, vb_s, dob_s))
        dk_ref[...] = jnp.zeros_like(dk_ref)
        dv_ref[...] = jnp.zeros_like(dv_ref)

        def superblock(m, _):
            rows_q, chains = _sb_chains(m, T)
            qs = {ch: qn_s[ch[0], rows_q[ch[1]], :] for ch in chains}
            doi = {ch: dob_s[ch[0], rows_q[ch[1]], :] for ch in chains}
            dt_total = {ch: jnp.sum(doi[ch].astype(F32) * o_ref[rows_q[ch[1]], hd * ch[0]:hd * ch[0] + hd],
                                    axis=1, keepdims=True) for ch in chains}

            def tile(j, carry, which):
                rows_j = pl.ds(pl.multiple_of(j * T, T), T)
                new = dict(carry)
                dk_part, dv_part = {}, {}
                for ch, masked in which:
                    hh = ch[0]
                    dq_acc, rc, gc = carry[ch]
                    kj = kn_s[hh, rows_j, :]
                    lb, l1m, w, keep = _sb_tile(qs[ch], kj, rc, us, masked)
                    wb = w.astype(BF16)
                    g = _dot(doi[ch], vb_s[hh, rows_j, :], 1, 1) * wb.astype(F32)
                    g_row = jnp.sum(g, axis=1, keepdims=True)
                    g_upto = (dt_total[ch] - gc - g_row) + _dot(g.astype(BF16), u_prefix)
                    dz = g - g_upto * jnp.exp(lb)
                    if masked:
                        dz = jnp.where(keep, dz, 0.0)
                    dzb = dz.astype(BF16)
                    dv_part[hh] = dv_part.get(hh, 0.0) + _dot(wb, doi[ch], 0, 0)
                    dk_part[hh] = dk_part.get(hh, 0.0) + _dot(dzb, qs[ch], 0, 0)
                    new[ch] = (dq_acc + _dot(dzb, kj), rc + jnp.sum(l1m, axis=1, keepdims=True),
                               gc + g_row)
                dv_ref[rows_j, :] += jnp.concatenate([dv_part[0], dv_part[1]], axis=1)
                dk_ref[rows_j, :] += jnp.concatenate([dk_part[0], dk_part[1]], axis=1)
                return new

            zero1 = jnp.zeros((T, 1), F32)
            carry = {ch: (jnp.zeros((T, hd), F32), zero1, zero1) for ch in chains}
            carry = tile(2 * m + 1, carry, [(ch, True) for ch in chains if ch[1] == 1])
            carry = tile(2 * m, carry, [(ch, ch[1] == 0) for ch in chains])
            carry = lax.fori_loop(0, 2 * m, lambda jj, c: tile(2 * m - 1 - jj, c, [(ch, False) for ch in chains]),
                                  carry)
            for qb in range(2):
                dq_ref[rows_q[qb], :] = jnp.concatenate([carry[(0, qb)][0], carry[(1, qb)][0]], axis=1) * scale
            return 0

        lax.fori_loop(0, nb // 2, superblock, 0)

        def finish(i, carry):
            rows = pl.ds(pl.multiple_of(i * T, T), T)
            new = []
            for hh in range(2):
                sl = slice(hd * hh, hd * hh + hd)
                outs = []
                for raw_ref, gain_ref, dn in ((q_ref, qg_ref, dq_ref[rows, sl]), (k_ref, kg_ref, dk_ref[rows, sl])):
                    raw = raw_ref[rows, sl]
                    r = lax.rsqrt(jnp.mean(raw * raw, axis=1, keepdims=True) + EPS)
                    hat = raw * r
                    t = dn * gain_ref[...]
                    outs.append((r * (t - hat * jnp.mean(hat * t, axis=1, keepdims=True)),
                                 jnp.sum(dn * hat, axis=0, keepdims=True)))
                dq_ref[rows, sl] = outs[0][0]
                dk_ref[rows, sl] = outs[1][0]
                new.append((carry[hh][0] + outs[0][1], carry[hh][1] + outs[1][1]))
            return tuple(new)

        zg = (jnp.zeros((1, hd), F32), jnp.zeros((1, hd), F32))
        tot = lax.fori_loop(0, nb, finish, (zg, zg))
        dg_ref[0:1, 0:hd] += tot[0][0] + tot[1][0]
        dg_ref[1:2, 0:hd] += tot[0][1] + tot[1][1]
        if scatter is not None:
            pl.when(pl.program_id(0) == npairs - 1)(lambda: _rs_phase(rs_in, rs_out, rs_sems, "finish"))

    col = lambda off: pl.BlockSpec((S, LANES), lambda p, off=off: (0, off + p))
    gain = pl.BlockSpec((1, hd), lambda p: (0, 0))
    tri = pl.BlockSpec((T, T), lambda p: (0, 0))
    pair = pl.BlockSpec((S, LANES), lambda p: (0, p))
    in_specs = [col(0), col(npairs), col(2 * npairs), pair, pair, gain, gain, tri]
    out_specs = [pl.BlockSpec((3, S, LANES), lambda p: (0, 0, p)), pl.BlockSpec((8, LANES), lambda p: (0, 0))]
    out_shape = [jax.ShapeDtypeStruct((3, S, D), F32), jax.ShapeDtypeStruct((8, LANES), F32)]
    scratch = [pltpu.VMEM((2, S, hd), BF16)] * 4
    args = [qkv, qkv, qkv, o, do, q_gain, k_gain, _tri(T, "row_gt_col")]
    if scatter is not None:
        in_specs += [HBM_SPEC] * len(scatter)
        out_specs += [HBM_SPEC] * len(scatter)
        out_shape += [jax.ShapeDtypeStruct(g.shape, g.dtype) for g in scatter]
        scratch += _rs_semaphores(len(scatter))
        args += list(scatter)
    out = pl.pallas_call(
        body, name=name, grid=(npairs,), in_specs=in_specs, out_specs=out_specs, out_shape=out_shape,
        scratch_shapes=scratch, compiler_params=_params(("arbitrary",)),
    )(*args)
    res = (out[0], out[1][0:1, :hd], out[1][1:2, :hd])
    return res if scatter is None else res + (list(out[2:]),)


def gmlp_fwd(zzpre, b_in, v_gain, wc, bsf, *, name):
    S, H2 = zzpre.shape
    H = H2 // 2
    G, T, _ = wc.shape
    gd = H // G

    def body(z_ref, b_ref, vg_ref, wc_ref, bs_ref, p_ref):
        zz, _ = _gelu_and_grad(z_ref[...] + b_ref[...])
        u = zz[:, :H]
        v = zz[:, H:]
        vn = v * lax.rsqrt(jnp.mean(v * v, axis=1, keepdims=True) + EPS) * vg_ref[...]
        for g in range(G):
            gs = slice(g * gd, (g + 1) * gd)
            mixed = _dot(wc_ref[g], vn[:, gs].astype(BF16)) + bs_ref[g]
            p_ref[:, gs] = (u[:, gs] * mixed).astype(BF16)

    full3 = lambda shp: pl.BlockSpec(shp, lambda c: (0, 0, 0))
    return pl.pallas_call(
        body, name=name, grid=(S // T,),
        in_specs=[pl.BlockSpec((T, H2), lambda c: (c, 0)), pl.BlockSpec((1, H2), lambda c: (0, 0)),
                  pl.BlockSpec((1, H), lambda c: (0, 0)), full3((G, T, T)), full3((G, T, gd))],
        out_specs=pl.BlockSpec((T, H), lambda c: (c, 0)), out_shape=jax.ShapeDtypeStruct((S, H), BF16),
        compiler_params=_params(("parallel",)),
    )(zzpre, b_in, v_gain, wc, bsf)


def gmlp_bwd(zzpre, b_in, v_gain, wc, bsf, dp, *, name):
    S, H2 = zzpre.shape
    H = H2 // 2
    G, T, _ = wc.shape
    gd = H // G
    assert G <= LANES

    def body(z_ref, b_ref, vg_ref, wc_ref, bs_ref, dp_ref, dzz_ref, db_ref, dvg_ref, dws_ref, dbs_ref):
        @pl.when(pl.program_id(0) == 0)
        def _():
            db_ref[...] = jnp.zeros_like(db_ref)
            dvg_ref[...] = jnp.zeros_like(dvg_ref)
            dws_ref[...] = jnp.zeros_like(dws_ref)
            dbs_ref[...] = jnp.zeros_like(dbs_ref)

        zz, gp = _gelu_and_grad(z_ref[...] + b_ref[...])
        u = zz[:, :H]
        v = zz[:, H:]
        r = lax.rsqrt(jnp.mean(v * v, axis=1, keepdims=True) + EPS)
        vhat = v * r
        vg = vg_ref[...]
        vn = vhat * vg
        dpv = dp_ref[...]
        tril = lax.broadcasted_iota(jnp.int32, (T, T), 1) <= lax.broadcasted_iota(jnp.int32, (T, T), 0)
        lane = lax.broadcasted_iota(jnp.int32, (T, LANES), 1)
        dbs = jnp.zeros((T, LANES), F32)
        du_parts, dvn_parts = [], []
        for g in range(G):
            gs = slice(g * gd, (g + 1) * gd)
            vng = vn[:, gs].astype(BF16)
            wcg = wc_ref[g]
            mixed = _dot(wcg, vng) + bs_ref[g]
            dpg = dpv[:, gs]
            du_parts.append(dpg * mixed)
            dmx = dpg * u[:, gs]
            dmxb = dmx.astype(BF16)
            dvn_parts.append(_dot(wcg, dmxb, 0, 0))
            dws_ref[g] += jnp.where(tril, _dot(dmxb, vng, 1, 1), 0.0)
            dbs = dbs + jnp.where(lane == g, jnp.sum(dmx, axis=1, keepdims=True), 0.0)
        dbs_ref[...] += dbs
        du = jnp.concatenate(du_parts, axis=1)
        dvn = jnp.concatenate(dvn_parts, axis=1)
        dvg_ref[...] += jnp.sum(dvn * vhat, axis=0, keepdims=True)
        t = dvn * vg
        dv = r * (t - vhat * jnp.mean(vhat * t, axis=1, keepdims=True))
        dzu = du * gp[:, :H]
        dzv = dv * gp[:, H:]
        dzz_ref[:, :H] = dzu.astype(BF16)
        dzz_ref[:, H:] = dzv.astype(BF16)
        db_ref[:, :H] += jnp.sum(dzu, axis=0, keepdims=True)
        db_ref[:, H:] += jnp.sum(dzv, axis=0, keepdims=True)

    full3 = lambda shp: pl.BlockSpec(shp, lambda c: (0, 0, 0))
    vec = lambda n: pl.BlockSpec((1, n), lambda c: (0, 0))
    return pl.pallas_call(
        body, name=name, grid=(S // T,),
        in_specs=[pl.BlockSpec((T, H2), lambda c: (c, 0)), vec(H2), vec(H), full3((G, T, T)), full3((G, T, gd)),
                  pl.BlockSpec((T, H), lambda c: (c, 0))],
        out_specs=[pl.BlockSpec((T, H2), lambda c: (c, 0)), vec(H2), vec(H), full3((G, T, T)),
                   pl.BlockSpec((T, LANES), lambda c: (0, 0))],
        out_shape=[jax.ShapeDtypeStruct((S, H2), BF16), jax.ShapeDtypeStruct((1, H2), F32),
                   jax.ShapeDtypeStruct((1, H), F32), jax.ShapeDtypeStruct((G, T, T), F32),
                   jax.ShapeDtypeStruct((T, LANES), F32)],
        compiler_params=_params(("arbitrary",)),
    )(zzpre, b_in, v_gain, wc, bsf, dp)


def _shift_rows(v, k, n_rows):
    if k == 0:
        return v
    rolled = pltpu.roll(v, k % n_rows, 0)
    row = lax.broadcasted_iota(jnp.int32, v.shape, 0)
    keep = (row >= k) if k > 0 else (row < n_rows + k)
    return jnp.where(keep, rolled, 0.0)


def conv_fwd(zx, conv_w, conv_b, col0, *, name):
    S = zx.shape[0]
    C = conv_w.shape[1]
    tc = _pick(C, (256, 128))
    off = col0 // tc
    assert col0 % tc == 0

    def body(x_ref, w_ref, b_ref, o_ref):
        xv = x_ref[...]
        acc = b_ref[...] + w_ref[SSM_CONV - 1:SSM_CONV, :] * xv
        for k in range(SSM_CONV - 1):
            acc = acc + w_ref[k:k + 1, :] * _shift_rows(xv, SSM_CONV - 1 - k, S)
        o_ref[...] = acc * _sigmoid(acc)

    return pl.pallas_call(
        body, name=name, grid=(C // tc,),
        in_specs=[pl.BlockSpec((S, tc), lambda j: (0, off + j)), pl.BlockSpec((SSM_CONV, tc), lambda j: (0, j)),
                  pl.BlockSpec((1, tc), lambda j: (0, j))],
        out_specs=pl.BlockSpec((S, tc), lambda j: (0, j)), out_shape=jax.ShapeDtypeStruct((S, C), F32),
        compiler_params=_params(("parallel",)),
    )(zx, conv_w, conv_b)


def conv_bwd(zx, conv_w, conv_b, col0, dout, *, name):
    S = zx.shape[0]
    C = conv_w.shape[1]
    tc = _pick(C, (256, 128))
    off = col0 // tc

    def body(x_ref, w_ref, b_ref, do_ref, dx_ref, dw_ref, db_ref):
        xv = x_ref[...]
        shifted = [_shift_rows(xv, SSM_CONV - 1 - k, S) for k in range(SSM_CONV)]
        acc = b_ref[...]
        for k in range(SSM_CONV):
            acc = acc + w_ref[k:k + 1, :] * shifted[k]
        s = _sigmoid(acc)
        dacc = do_ref[...] * (s * (1.0 + acc * (1.0 - s)))
        db_ref[...] = jnp.sum(dacc, axis=0, keepdims=True)
        dx = jnp.zeros_like(xv)
        for k in range(SSM_CONV):
            dw_ref[k:k + 1, :] = jnp.sum(dacc * shifted[k], axis=0, keepdims=True)
            dx = dx + w_ref[k:k + 1, :] * _shift_rows(dacc, -(SSM_CONV - 1 - k), S)
        dx_ref[...] = dx

    slab = pl.BlockSpec((S, tc), lambda j: (0, j))
    return pl.pallas_call(
        body, name=name, grid=(C // tc,),
        in_specs=[pl.BlockSpec((S, tc), lambda j: (0, off + j)), pl.BlockSpec((SSM_CONV, tc), lambda j: (0, j)),
                  pl.BlockSpec((1, tc), lambda j: (0, j)), slab],
        out_specs=[slab, pl.BlockSpec((SSM_CONV, tc), lambda j: (0, j)), pl.BlockSpec((1, tc), lambda j: (0, j))],
        out_shape=[jax.ShapeDtypeStruct((S, C), F32), jax.ShapeDtypeStruct((SSM_CONV, C), F32),
                   jax.ShapeDtypeStruct((1, C), F32)],
        compiler_params=_params(("parallel",)),
    )(zx, conv_w, conv_b, dout)


def _ssd_chunk_terms(dtraw, bias, a_log, tl):
    dt = _softplus(dtraw + bias)
    a_neg = -jnp.exp(a_log)
    ac = _dot_exact_left(tl, dt * a_neg)
    ac_last = ac[ac.shape[0] - 1:, :]
    return dt, a_neg, ac, ac.T, jnp.exp(ac), jnp.exp(ac_last - ac), jnp.exp(ac_last)


def _ssd_specs(S, L, G, hpg, pd, inner):
    gw = hpg * pd
    n = SSM_STATE
    xb = inner // n

    def mk(cidx):
        return dict(
            x=pl.BlockSpec((L, gw), lambda g, c: (cidx(c), g)),
            b=pl.BlockSpec((L, n), lambda g, c: (cidx(c), xb + g)),
            c=pl.BlockSpec((L, n), lambda g, c: (cidx(c), xb + G + g)),
            z=pl.BlockSpec((L, gw), lambda g, c: (cidx(c), g)),
            dt=pl.BlockSpec((L, LANES), lambda g, c: (cidx(c), g)),
            gvec=pl.BlockSpec((1, 1, LANES), lambda g, c: (g, 0, 0)),
            chan=pl.BlockSpec((1, gw), lambda g, c: (0, g)),
            tri=pl.BlockSpec((L, L), lambda g, c: (0, 0)),
            hp=pl.BlockSpec((1, 1, gw, n), lambda g, c: (g, cidx(c), 0, 0)),
            bc=pl.BlockSpec((L, n), lambda g, c: (cidx(c), g)),
        )
    return mk


def ssd_fwd(xbc, zx, dtg, bias_g, alog_g, d_chan, ngain, L, G, *, name):
    S = xbc.shape[0]
    n = SSM_STATE
    inner = xbc.shape[1] - 2 * G * n
    gw = inner // G
    pd = SB_HEAD_DIM
    hpg = gw // pd
    nc = S // L
    sp = _ssd_specs(S, L, G, hpg, pd, inner)(lambda c: c)

    def body(x_ref, b_ref, c_ref, z_ref, dt_ref, bias_ref, alog_ref, d_ref, ng_ref, tl_ref,
             yn_ref, y_ref, hp_ref, state):
        @pl.when(pl.program_id(1) == 0)
        def _():
            state[...] = jnp.zeros_like(state)

        dt, _, ac, act, ea, dte, cd = _ssd_chunk_terms(dt_ref[...], bias_ref[0], alog_ref[0], tl_ref[...])
        xv = x_ref[...]
        bm = b_ref[...].astype(BF16)
        cm = c_ref[...].astype(BF16)
        cb = _dot(cm, bm, 1, 1)
        tril = lax.broadcasted_iota(jnp.int32, (L, L), 1) <= lax.broadcasted_iota(jnp.int32, (L, L), 0)
        hp_ref[0, 0] = state[...]
        for r in range(hpg):
            ps = slice(r * pd, (r + 1) * pd)
            xr = xv[:, ps]
            xdt = xr * dt[:, r:r + 1]
            lm = jnp.exp(jnp.where(tril, ac[:, r:r + 1] - act[r:r + 1, :], -jnp.inf))
            hprev = state[ps, :]
            y = _dot((cb * lm).astype(BF16), xdt.astype(BF16))
            y = y + _dot(cm, hprev.astype(BF16), 1, 1) * ea[:, r:r + 1]
            y_ref[:, ps] = y + xr * d_ref[:, ps]
            st = _dot((xdt * dte[:, r:r + 1]).astype(BF16), bm, 0, 0)
            state[ps, :] = hprev * cd[:, r:r + 1] + st
        yfull = y_ref[...]
        zg = z_ref[...]
        yg = yfull * (zg * _sigmoid(zg))
        yn_ref[...] = (yg * lax.rsqrt(jnp.mean(yg * yg, axis=1, keepdims=True) + EPS) * ng_ref[...]).astype(BF16)

    return pl.pallas_call(
        body, name=name, grid=(G, nc),
        in_specs=[sp["x"], sp["b"], sp["c"], sp["z"], sp["dt"], sp["gvec"], sp["gvec"], sp["chan"], sp["chan"], sp["tri"]],
        out_specs=[sp["x"], sp["x"], sp["hp"]],
        out_shape=[jax.ShapeDtypeStruct((S, inner), BF16), jax.ShapeDtypeStruct((S, inner), F32),
                   jax.ShapeDtypeStruct((G, nc, gw, n), F32)],
        scratch_shapes=[pltpu.VMEM((gw, n), F32)],
        compiler_params=_params(("arbitrary", "arbitrary")),
    )(xbc, xbc, xbc, zx, dtg, bias_g, alog_g, d_chan, ngain, _tri(L, "row_ge_col"))


def ssd_bwd(xbc, zx, dtg, bias_g, alog_g, d_chan, ngain, yfull, hp, dyn, L, G, *, name):
    S = xbc.shape[0]
    n = SSM_STATE
    inner = xbc.shape[1] - 2 * G * n
    gw = inner // G
    pd = SB_HEAD_DIM
    hpg = gw // pd
    nc = S // L
    sp = _ssd_specs(S, L, G, hpg, pd, inner)(lambda c: nc - 1 - c)

    def body(x_ref, b_ref, c_ref, z_ref, dt_ref, bias_ref, alog_ref, d_ref, ng_ref, tl_ref, tu_ref,
             yf_ref, hp_ref, dyn_ref,
             dz_ref, dx_ref, db_ref, dc_ref, ddt_ref, dbias_ref, dalog_ref, dd_ref, dng_ref, dstate):
        first = pl.program_id(1) == 0

        @pl.when(first)
        def _():
            dstate[...] = jnp.zeros_like(dstate)
            dbias_ref[...] = jnp.zeros_like(dbias_ref)
            dalog_ref[...] = jnp.zeros_like(dalog_ref)
            dd_ref[...] = jnp.zeros_like(dd_ref)
            dng_ref[...] = jnp.zeros_like(dng_ref)

        dtraw = dt_ref[...]
        dt, a_neg, ac, act, ea, dte, cd = _ssd_chunk_terms(dtraw, bias_ref[0], alog_ref[0], tl_ref[...])
        xv = x_ref[...]
        bm = b_ref[...].astype(BF16)
        cm = c_ref[...].astype(BF16)
        cb = _dot(cm, bm, 1, 1)
        tril = lax.broadcasted_iota(jnp.int32, (L, L), 1) <= lax.broadcasted_iota(jnp.int32, (L, L), 0)
        lane = lax.broadcasted_iota(jnp.int32, (L, LANES), 1)
        lane1 = lax.broadcasted_iota(jnp.int32, (1, LANES), 1)

        yfull = yf_ref[...]
        zg = z_ref[...]
        sg = _sigmoid(zg)
        gate = zg * sg
        yg = yfull * gate
        rr = lax.rsqrt(jnp.mean(yg * yg, axis=1, keepdims=True) + EPS)
        yhat = yg * rr
        dynv = dyn_ref[...]
        dng_ref[...] += jnp.sum(dynv * yhat, axis=0, keepdims=True)
        t = dynv * ng_ref[...]
        dyg = rr * (t - yhat * jnp.mean(yhat * t, axis=1, keepdims=True))
        dy = dyg * gate
        dz_ref[...] = dyg * yfull * (sg * (1.0 + zg * (1.0 - sg)))

        dcb = jnp.zeros((L, L), F32)
        dc_acc = jnp.zeros((L, n), F32)
        db_acc = jnp.zeros((L, n), F32)
        dac = jnp.zeros((L, LANES), F32)
        xdx = jnp.zeros((L, LANES), F32)
        tail = jnp.zeros((1, LANES), F32)
        dskip = jnp.zeros((1, LANES), F32)
        ones_l = jnp.ones((L, LANES), BF16)
        for r in range(hpg):
            ps = slice(r * pd, (r + 1) * pd)
            xr = xv[:, ps]
            dyr = dy[:, ps]
            dtr = dt[:, r:r + 1]
            dter = dte[:, r:r + 1]
            cdr = cd[:, r:r + 1]
            xdt = xr * dtr
            xdtb = xdt.astype(BF16)
            dyrb = dyr.astype(BF16)
            lm = jnp.exp(jnp.where(tril, ac[:, r:r + 1] - act[r:r + 1, :], -jnp.inf))
            m32 = cb * lm
            mb = m32.astype(BF16)
            hprev = hp_ref[0, 0, ps, :]
            hpb = hprev.astype(BF16)
            dhn = dstate[ps, :]
            dhnb = dhn.astype(BF16)
            ear = ea[:, r:r + 1]
            gy = (dyr * ear).astype(BF16)
            dc_acc = dc_acc + _dot(gy, hpb)
            dstate[ps, :] = _dot(gy, cm, 0, 0) + dhn * cdr
            bdh = _dot(bm, dhnb, 1, 1)
            db_acc = db_acc + _dot((xdt * dter).astype(BF16), dhnb)
            dm = _dot(dyrb, xdtb, 1, 1)
            dxdt = bdh * dter + _dot(mb, dyrb, 0, 0)
            dcb = dcb + dm * lm
            wmat = dm * m32
            whi = wmat.astype(BF16)
            wlo = (wmat - whi.astype(F32)).astype(BF16)
            col_w = _dot(whi, ones_l, 0, 0) + _dot(wlo, ones_l, 0, 0)
            sdte = jnp.sum(xdt * bdh * dter, axis=1, keepdims=True)
            e_r = jnp.sum(wmat, axis=1, keepdims=True) + jnp.sum(dyr * _dot(cm, hpb, 1, 1) * ear, axis=1, keepdims=True) - sdte
            c_r = cdr * jnp.sum(jnp.sum(dhn * hprev, axis=1, keepdims=True), axis=0, keepdims=True) \
                + jnp.sum(sdte, axis=0, keepdims=True)
            dac = dac + jnp.where(lane == r, e_r - col_w, 0.0)
            xdx = xdx + jnp.where(lane == r, jnp.sum(dxdt * xr, axis=1, keepdims=True), 0.0)
            tail = tail + jnp.where(lane1 == r, c_r, 0.0)
            dskip = dskip + jnp.where(lane1 == r, jnp.sum(jnp.sum(dyr * xr, axis=1, keepdims=True), axis=0, keepdims=True), 0.0)
            dx_ref[:, ps] = dxdt * dtr + dyr * d_ref[:, ps]
        dcbb = dcb.astype(BF16)
        dc_ref[...] = dc_acc + _dot(dcbb, bm)
        db_ref[...] = db_acc + _dot(dcbb, cm, 0, 0)
        da = _dot_exact_left(tu_ref[...], dac) + tail
        real = lane < hpg
        ddt = jnp.where(real, (da * a_neg + xdx) * _sigmoid(dtraw + bias_ref[0]), 0.0)
        ddt_ref[...] = ddt
        dd_ref[0] += dskip
        dbias_ref[0] += jnp.sum(ddt, axis=0, keepdims=True)
        dalog_ref[0] += jnp.where(lane1 < hpg, jnp.sum(da * dt, axis=0, keepdims=True) * a_neg, 0.0)

    return pl.pallas_call(
        body, name=name, grid=(G, nc),
        in_specs=[sp["x"], sp["b"], sp["c"], sp["z"], sp["dt"], sp["gvec"], sp["gvec"], sp["chan"], sp["chan"],
                  sp["tri"], sp["tri"], sp["x"], sp["hp"], sp["x"]],
        out_specs=[sp["x"], sp["x"], sp["bc"], sp["bc"], sp["dt"], sp["gvec"], sp["gvec"], sp["gvec"], sp["chan"]],
        out_shape=[jax.ShapeDtypeStruct((S, inner), F32), jax.ShapeDtypeStruct((S, inner), F32),
                   jax.ShapeDtypeStruct((S, G * n), F32), jax.ShapeDtypeStruct((S, G * n), F32),
                   jax.ShapeDtypeStruct((S, G * LANES), F32), jax.ShapeDtypeStruct((G, 1, LANES), F32),
                   jax.ShapeDtypeStruct((G, 1, LANES), F32), jax.ShapeDtypeStruct((G, 1, LANES), F32),
                   jax.ShapeDtypeStruct((1, inner), F32)],
        scratch_shapes=[pltpu.VMEM((gw, n), F32)],
        compiler_params=_params(("arbitrary", "arbitrary")),
    )(xbc, xbc, xbc, zx, dtg, bias_g, alog_g, d_chan, ngain, _tri(L, "row_ge_col"), _tri(L, "row_le_col"),
      yfull, hp, dyn)


def _spread_dt(w_dt, G, hpg):
    K = w_dt.shape[0]
    w = w_dt.reshape(K, G, hpg)
    return jnp.pad(w, ((0, 0), (0, 0), (0, LANES - hpg))).reshape(K, G * LANES)


def _group_vec(v, G, hpg):
    return jnp.pad(v.reshape(G, 1, hpg), ((0, 0), (0, 0), (0, LANES - hpg)))


def local_step(x, target, W, late=None):
    S, D = x.shape
    depth = W["mix_norm"].shape[0]
    gm_groups, gm_chunk = W["gm_w_s"].shape[1], W["gm_w_s"].shape[2]
    heads = W["ssm_dt_bias"].shape[1]
    inner = heads * SB_HEAD_DIM
    L = gm_chunk
    received = None

    saved = []
    for i in range(depth):
        kind, j = i % 3, i // 3
        s = dict(x=x)
        h = rms_fwd(x, W["mix_norm"][i:i + 1], name=f"rms_mix_fwd")
        s["h"] = h
        if kind == 0:
            qkv = matmul(h, W["sb_w_qkv"][j], name="mm_qkv")
            if late is not None and i == 0:
                o, gathered = sb_attn_fwd(qkv, W["sb_q_gain"][j:j + 1], W["sb_k_gain"][j:j + 1], name="sb_fwd_gather",
                                          gather=late.shard)
                late.fill(W, gathered)
            else:
                o = sb_attn_fwd(qkv, W["sb_q_gain"][j:j + 1], W["sb_k_gain"][j:j + 1], name="sb_fwd")
            x1 = matmul(o, W["sb_w_o"][j], residual=x, name="mm_sb_out")
            s.update(qkv=qkv, o=o)
        elif kind == 1:
            wc = jnp.where(jnp.tril(jnp.ones((gm_chunk, gm_chunk), bool)), W["gm_w_s"][j], 0.0).astype(BF16)
            bsf = jnp.broadcast_to(W["gm_b_s"][j][:, :, None], (gm_groups, gm_chunk, W["gm_v_gain"].shape[1] // gm_groups)).astype(F32)
            zzpre = matmul(h, W["gm_w_in"][j], name="mm_gm_in")
            p = gmlp_fwd(zzpre, W["gm_b_in"][j:j + 1], W["gm_v_gain"][j:j + 1], wc, bsf, name="gm_fwd")
            x1 = matmul(p, W["gm_w_out"][j], residual=x, name="mm_gm_out")
            s.update(zzpre=zzpre, p=p, wc=wc, bsf=bsf)
        else:
            conv_dim = W["ssm_conv_w"].shape[2]
            G = (conv_dim - inner) // (2 * SSM_STATE)
            hpg = heads // G
            w_in = W["ssm_w_in"][j]
            w_zx = w_in[:, :inner + conv_dim]
            w_dtg = _spread_dt(w_in[:, inner + conv_dim:], G, hpg)
            bias_g = _group_vec(W["ssm_dt_bias"][j], G, hpg)
            alog_g = _group_vec(W["ssm_a_log"][j], G, hpg)
            d_chan = jnp.repeat(W["ssm_d"][j], SB_HEAD_DIM)[None, :]
            ngain = W["ssm_norm_gain"][j:j + 1]
            zx = matmul(h, w_zx, name="mm_ssm_zx")
            dtg = matmul(h, w_dtg, name="mm_ssm_dt")
            xbc = conv_fwd(zx, W["ssm_conv_w"][j], W["ssm_conv_b"][j:j + 1], inner, name="conv_fwd")
            yn, yfull, hp = ssd_fwd(xbc, zx, dtg, bias_g, alog_g, d_chan, ngain, L, G, name="ssd_fwd")
            x1 = matmul(yn, W["ssm_w_out"][j], residual=x, name="mm_ssm_out")
            s.update(w_zx=w_zx, w_dtg=w_dtg, bias_g=bias_g, alog_g=alog_g, d_chan=d_chan, ngain=ngain,
                     zx=zx, dtg=dtg, xbc=xbc, yn=yn, yfull=yfull, hp=hp)
        h2 = rms_fwd(x1, W["ffn_norm"][i:i + 1], name="rms_ffn_fwd")
        gu = matmul(h2, W["ffn_w_gu"][i], name="mm_ffn_gu")
        a = swiglu_fwd(gu, name="swiglu_fwd")
        x2 = matmul(a, W["ffn_w_down"][i], residual=x1, name="mm_ffn_down")
        s.update(x1=x1, h2=h2, gu=gu, a=a)
        saved.append(s)
        x = x2

    dx, loss = loss_head(x, target, name="loss_head")

    gw = {k: {} for k in WEIGHTS}
    for i in reversed(range(depth)):
        kind, j = i % 3, i // 3
        s = saved[i]
        da = matmul(dx, W["ffn_w_down"][i], tb=True, name="mm_ffn_da")
        gw["ffn_w_down"][i] = matmul(s["a"], dx, ta=True, out_dtype=BF16, name="mm_ffn_dwdown")
        dgu = swiglu_bwd(s["gu"], da, name="swiglu_bwd")
        dh2 = matmul(dgu, W["ffn_w_gu"][i], tb=True, name="mm_ffn_dh")
        gw["ffn_w_gu"][i] = matmul(s["h2"], dgu, ta=True, out_dtype=BF16, name="mm_ffn_dwgu")
        dx1, dgn = rms_bwd(s["x1"], W["ffn_norm"][i:i + 1], dh2, dx, name="rms_ffn_bwd")
        gw["ffn_norm"][i] = dgn[0]
        if kind == 0:
            do = matmul(dx1, W["sb_w_o"][j], tb=True, name="mm_sb_do")
            gw["sb_w_o"][j] = matmul(s["o"], dx1, ta=True, out_dtype=BF16, name="mm_sb_dwo")
            if late is not None and i == 0:
                dqkv, dqg, dkg, received = sb_attn_bwd(
                    s["qkv"], s["o"], do, W["sb_q_gain"][j:j + 1], W["sb_k_gain"][j:j + 1], name="sb_bwd_scatter",
                    scatter=late.contributions(gw))
            else:
                dqkv, dqg, dkg = sb_attn_bwd(s["qkv"], s["o"], do, W["sb_q_gain"][j:j + 1], W["sb_k_gain"][j:j + 1],
                                             name="sb_bwd")
            gw["sb_q_gain"][j] = dqg[0]
            gw["sb_k_gain"][j] = dkg[0]
            dh = matmul(dqkv, W["sb_w_qkv"][j], tb=True, a_split=3, name="mm_sb_dh")
            gw["sb_w_qkv"][j] = matmul(s["h"], dqkv, ta=True, b_split=3, out_dtype=BF16, name="mm_sb_dwqkv")
        elif kind == 1:
            dp = matmul(dx1, W["gm_w_out"][j], tb=True, name="mm_gm_dp")
            gw["gm_w_out"][j] = matmul(s["p"], dx1, ta=True, out_dtype=BF16, name="mm_gm_dwout")
            dzz, db_in, dvg, dws, dbs = gmlp_bwd(s["zzpre"], W["gm_b_in"][j:j + 1], W["gm_v_gain"][j:j + 1],
                                                s["wc"], s["bsf"], dp, name="gm_bwd")
            gw["gm_b_in"][j] = db_in[0]
            gw["gm_v_gain"][j] = dvg[0]
            gw["gm_w_s"][j] = dws
            gw["gm_b_s"][j] = dbs[:, :gm_groups].T
            dh = matmul(dzz, W["gm_w_in"][j], tb=True, name="mm_gm_dh")
            gw["gm_w_in"][j] = matmul(s["h"], dzz, ta=True, out_dtype=BF16, name="mm_gm_dwin")
        else:
            conv_dim = W["ssm_conv_w"].shape[2]
            G = (conv_dim - inner) // (2 * SSM_STATE)
            hpg = heads // G
            dyn = matmul(dx1, W["ssm_w_out"][j], tb=True, name="mm_ssm_dyn")
            gw["ssm_w_out"][j] = matmul(s["yn"], dx1, ta=True, out_dtype=BF16, name="mm_ssm_dwout")
            dz, dxs, dbm, dcm, ddt, dbias, dalog, dd, dng = ssd_bwd(
                s["xbc"], s["zx"], s["dtg"], s["bias_g"], s["alog_g"], s["d_chan"], s["ngain"], s["yfull"], s["hp"],
                dyn, L, G, name="ssd_bwd")
            dxbc = jnp.concatenate([dxs, dbm, dcm], axis=1)
            dpre, dcw, dcb = conv_bwd(s["zx"], W["ssm_conv_w"][j], W["ssm_conv_b"][j:j + 1], inner, dxbc,
                                      name="conv_bwd")
            dzx = jnp.concatenate([dz, dpre], axis=1)
            dh = matmul(ddt, s["w_dtg"], tb=True, name="mm_ssm_dh_dt")
            dh = matmul(dzx, s["w_zx"], tb=True, residual=dh, name="mm_ssm_dh")
            dw_zx = matmul(s["h"], dzx, ta=True, out_dtype=BF16, name="mm_ssm_dwzx")
            dw_dtg = matmul(s["h"], ddt, ta=True, out_dtype=BF16, name="mm_ssm_dwdt")
            dw_dt = dw_dtg.reshape(D, G, LANES)[:, :, :hpg].reshape(D, heads)
            gw["ssm_w_in"][j] = jnp.concatenate([dw_zx, dw_dt], axis=1)
            gw["ssm_conv_w"][j] = dcw
            gw["ssm_conv_b"][j] = dcb[0]
            gw["ssm_dt_bias"][j] = dbias[:, 0, :hpg].reshape(heads)
            gw["ssm_a_log"][j] = dalog[:, 0, :hpg].reshape(heads)
            gw["ssm_d"][j] = dd[:, 0, :hpg].reshape(heads)
            gw["ssm_norm_gain"][j] = dng[0]
        dx, dgn = rms_bwd(s["x"], W["mix_norm"][i:i + 1], dh, dx1, name="rms_mix_bwd")
        gw["mix_norm"][i] = dgn[0]

    return loss, dx, gw, received


MESH = pl.DeviceIdType.MESH
HBM_SPEC = pl.BlockSpec(memory_space=pltpu.HBM)
VMEM_SPEC = pl.BlockSpec(memory_space=pltpu.VMEM)


def _my_position():
    return lax.axis_index("x"), lax.axis_index("y"), lax.axis_index("c")


def _flip(v, bit):
    return 1 - v if bit else v


def all_gather_packed(shard, *, name):
    R, C = shard.shape

    def body(x_ref, out_ref, send_sems, recv_sems, local_sem):
        refs = (x_ref, out_ref, send_sems, recv_sems, local_sem)
        _ag_phase(refs, "start")
        _ag_phase(refs, "forward")
        _ag_phase(refs, "finish")

    return pl.pallas_call(
        body, name=name, out_shape=jax.ShapeDtypeStruct((N_DEV, R, C), shard.dtype),
        in_specs=[HBM_SPEC], out_specs=HBM_SPEC, scratch_shapes=COMM_SEMAPHORES,
    )(shard)


COMM_SEMAPHORES = [pltpu.SemaphoreType.DMA((7,)), pltpu.SemaphoreType.DMA((7,)), pltpu.SemaphoreType.DMA]


def _ag_phase(refs, phase):
    x_ref, out_ref, send_sems, recv_sems, local_sem = refs
    x, y, c = _my_position()
    me, sibling = (x, y, c), (x, y, 1 - c)
    chips = [(1 - x, y), (x, 1 - y), (1 - x, 1 - y)]

    def slot(px, py, pc):
        return out_ref.at[4 * px + 2 * py + pc]

    def copy(k, block, to, src=None):
        return pltpu.make_async_remote_copy(
            src_ref=slot(*block) if src is None else src, dst_ref=slot(*block),
            send_sem=send_sems.at[k], recv_sem=recv_sems.at[k], device_id=to, device_id_type=MESH)

    mine = pltpu.make_async_copy(x_ref, slot(*me), local_sem)
    first = [copy(0, me, sibling, src=x_ref)]
    first += [copy(1 + j, me, (*chip, c), src=x_ref) for j, chip in enumerate(chips)]
    passed = [copy(4 + j, (*chip, c), sibling) for j, chip in enumerate(chips)]
    if phase == "start":
        mine.start()
        for cp in first:
            cp.start()
    elif phase == "forward":
        for j, chip in enumerate(chips):
            copy(1 + j, (*chip, c), me).wait_recv()
            passed[j].start()
    else:
        copy(0, sibling, me).wait_recv()
        for j, chip in enumerate(chips):
            copy(4 + j, (*chip, 1 - c), me).wait_recv()
        for cp in first + passed:
            cp.wait_send()
        mine.wait()


def _rs_semaphores(n):
    return [pltpu.SemaphoreType.DMA((7 * n,)), pltpu.SemaphoreType.DMA((7 * n,)), pltpu.SemaphoreType.DMA((n,))]


def _rs_phase(g_refs, out_refs, sems, phase):
    send_sems, recv_sems, local_sems = sems
    x, y, c = _my_position()
    me = 4 * x + 2 * y + c
    copies = []
    for p, (g_ref, out_ref) in enumerate(zip(g_refs, out_refs)):
        copies.append(pltpu.make_async_copy(g_ref.at[me], out_ref.at[me], local_sems.at[p]))
        for k in range(1, N_DEV):
            px, py, pc = _flip(x, k & 4), _flip(y, k & 2), _flip(c, k & 1)
            copies.append(pltpu.make_async_remote_copy(
                src_ref=g_ref.at[4 * px + 2 * py + pc], dst_ref=out_ref.at[me],
                send_sem=send_sems.at[7 * p + k - 1], recv_sem=recv_sems.at[7 * p + k - 1],
                device_id=(px, py, pc), device_id_type=MESH))
    for cp in copies:
        if phase == "start":
            cp.start()
        else:
            cp.wait()


def exchange_for_reduce_scatter(gs, *, name):
    n = len(gs)

    def body(*refs):
        for phase in ("start", "finish"):
            _rs_phase(refs[:n], refs[n:2 * n], refs[2 * n:], phase)

    return pl.pallas_call(
        body, name=name, out_shape=[jax.ShapeDtypeStruct(g.shape, g.dtype) for g in gs],
        in_specs=[HBM_SPEC] * n, out_specs=[HBM_SPEC] * n, scratch_shapes=_rs_semaphores(n),
    )(*gs)


def sum_slots(recv, *, name):
    n, R, C = recv.shape
    tr = _pick(R, (512, 256, 128))

    def body(r_ref, o_ref):
        acc = r_ref[0].astype(F32)
        for s in range(1, n):
            acc = acc + r_ref[s].astype(F32)
        o_ref[...] = acc

    return pl.pallas_call(
        body, name=name, grid=(R // tr,), in_specs=[pl.BlockSpec((n, tr, C), lambda i: (0, i, 0))],
        out_specs=pl.BlockSpec((tr, C), lambda i: (i, 0)), out_shape=jax.ShapeDtypeStruct((R, C), F32),
        compiler_params=_params(("parallel",)),
    )(recv)


def all_reduce_small(v, *, name):
    R, C = v.shape

    def body(v_ref, o_ref, buf, send_sems, recv_sems):
        x, y, c = _my_position()
        me = 4 * x + 2 * y + c
        buf[me] = v_ref[...]
        copies = []
        for k in range(1, N_DEV):
            px, py, pc = _flip(x, k & 4), _flip(y, k & 2), _flip(c, k & 1)
            copies.append(pltpu.make_async_remote_copy(
                src_ref=v_ref, dst_ref=buf.at[me], send_sem=send_sems.at[k - 1], recv_sem=recv_sems.at[k - 1],
                device_id=(px, py, pc), device_id_type=MESH))
        for cp in copies:
            cp.start()
        for cp in copies:
            cp.wait()
        acc = buf[0]
        for s in range(1, N_DEV):
            acc = acc + buf[s]
        o_ref[...] = acc

    return pl.pallas_call(
        body, name=name, out_shape=jax.ShapeDtypeStruct((R, C), F32), in_specs=[VMEM_SPEC], out_specs=VMEM_SPEC,
        scratch_shapes=[pltpu.VMEM((N_DEV, R, C), F32), pltpu.SemaphoreType.DMA((7,)), pltpu.SemaphoreType.DMA((7,))],
        compiler_params=pltpu.CompilerParams(vmem_limit_bytes=VMEM_LIMIT_BYTES),
    )(v)


def _pad_rows(a, mult):
    pad = (-a.shape[0]) % mult
    return jnp.pad(a, ((0, pad), (0, 0))) if pad else a


def _pack_small(arrays):
    flat = []
    for a in arrays:
        f = a.reshape(-1).astype(F32)
        flat.append(jnp.pad(f, (0, (-f.shape[0]) % LANES)))
    return _pad_rows(jnp.concatenate(flat).reshape(-1, LANES), 8)


def _unpack_small(packed, shapes):
    flat = packed.reshape(-1)
    out, r = [], 0
    for shp in shapes:
        n = math.prod(shp)
        out.append(flat[r:r + n].reshape(shp))
        r += n + (-n) % LANES
    return out


ARG_NAMES = ("x",) + WEIGHTS + ("loss_target",) + tuple("m_" + w for w in WEIGHTS) + tuple("v_" + w for w in WEIGHTS)


def kernel(x, mix_norm, ffn_norm, sb_w_qkv, sb_q_gain, sb_k_gain, sb_w_o, gm_w_in, gm_b_in, gm_v_gain, gm_w_s, gm_b_s, gm_w_out, ssm_w_in, ssm_conv_w, ssm_conv_b, ssm_dt_bias, ssm_a_log, ssm_d, ssm_norm_gain, ssm_w_out, ffn_w_gu, ffn_w_down, loss_target, m_mix_norm, m_ffn_norm, m_sb_w_qkv, m_sb_q_gain, m_sb_k_gain, m_sb_w_o, m_gm_w_in, m_gm_b_in, m_gm_v_gain, m_gm_w_s, m_gm_b_s, m_gm_w_out, m_ssm_w_in, m_ssm_conv_w, m_ssm_conv_b, m_ssm_dt_bias, m_ssm_a_log, m_ssm_d, m_ssm_norm_gain, m_ssm_w_out, m_ffn_w_gu, m_ffn_w_down, v_mix_norm, v_ffn_norm, v_sb_w_qkv, v_sb_q_gain, v_sb_k_gain, v_sb_w_o, v_gm_w_in, v_gm_b_in, v_gm_v_gain, v_gm_w_s, v_gm_b_s, v_gm_w_out, v_ssm_w_in, v_ssm_conv_w, v_ssm_conv_b, v_ssm_dt_bias, v_ssm_a_log, v_ssm_d, v_ssm_norm_gain, v_ssm_w_out, v_ffn_w_gu, v_ffn_w_down):
    given = dict(zip(ARG_NAMES, (x, mix_norm, ffn_norm, sb_w_qkv, sb_q_gain, sb_k_gain, sb_w_o, gm_w_in, gm_b_in, gm_v_gain, gm_w_s, gm_b_s, gm_w_out, ssm_w_in, ssm_conv_w, ssm_conv_b, ssm_dt_bias, ssm_a_log, ssm_d, ssm_norm_gain, ssm_w_out, ffn_w_gu, ffn_w_down, loss_target, m_mix_norm, m_ffn_norm, m_sb_w_qkv, m_sb_q_gain, m_sb_k_gain, m_sb_w_o, m_gm_w_in, m_gm_b_in, m_gm_v_gain, m_gm_w_s, m_gm_b_s, m_gm_w_out, m_ssm_w_in, m_ssm_conv_w, m_ssm_conv_b, m_ssm_dt_bias, m_ssm_a_log, m_ssm_d, m_ssm_norm_gain, m_ssm_w_out, m_ffn_w_gu, m_ffn_w_down, v_mix_norm, v_ffn_norm, v_sb_w_qkv, v_sb_q_gain, v_sb_k_gain, v_sb_w_o, v_gm_w_in, v_gm_b_in, v_gm_v_gain, v_gm_w_s, v_gm_b_s, v_gm_w_out, v_ssm_w_in, v_ssm_conv_w, v_ssm_conv_b, v_ssm_dt_bias, v_ssm_a_log, v_ssm_d, v_ssm_norm_gain, v_ssm_w_out, v_ffn_w_gu, v_ffn_w_down)))
    mx, my, mc = _my_position()
    me = 4 * mx + 2 * my + mc

    pieces = [(k, l) for k in BIG for l in range(given[k].shape[0])]
    early = [("sb_w_qkv", 0), ("sb_w_o", 0), ("ffn_w_gu", 0), ("ffn_w_down", 0)]
    late_pieces = [p for p in pieces if p not in early]
    last = [("sb_w_qkv", 0)]
    main = [p for p in pieces if p not in last]
    row_mult = 256

    def rows_of(p):
        return math.prod(given[p[0]].shape[1:]) // PACK_COLS

    def pack_shards(ps, prefix="", dtype=F32, extra=()):
        parts = [given[prefix + k][l].astype(dtype).reshape(-1, PACK_COLS) for k, l in ps] + list(extra)
        return _pad_rows(jnp.concatenate(parts, axis=0), row_mult)

    def split_rows(packed, ps):
        out, r = [], 0
        for p in ps:
            out.append(packed[..., r:r + rows_of(p), :])
            r += rows_of(p)
        return out

    def piece_to_full(g, k):
        shp = given[k].shape[1:]
        g = g.reshape((N_DEV,) + shp)
        if k in COL_SHARDED:
            return jnp.moveaxis(g, 0, 1).reshape(shp[0], N_DEV * shp[1])
        return g.reshape(N_DEV * shp[0], shp[1])

    def full_to_piece(full, k):
        shp = given[k].shape[1:]
        g = jnp.moveaxis(full.reshape(shp[0], N_DEV, shp[1]), 1, 0) if k in COL_SHARDED else full
        return g.reshape(N_DEV, -1, PACK_COLS)

    def contributions(gw, ps):
        return [full_to_piece(gw[k][l], k) for k, l in ps]

    sharded_small = [lax.bitcast_convert_type(given[k], BF16) for k in SMALL_SHARDED]
    tail = jnp.concatenate([a.reshape(-1) for a in sharded_small])
    tail = jnp.pad(tail, (0, (-tail.size) % PACK_COLS)).reshape(-1, PACK_COLS)

    W = {k: given[k] for k in SMALL if k not in SMALL_SHARDED}
    W.update({k: [None] * given[k].shape[0] for k in BIG})
    gathered_early = all_gather_packed(pack_shards(early, dtype=BF16), name="all_gather_early")
    for (k, l), g in zip(early, split_rows(gathered_early, early)):
        W[k][l] = piece_to_full(g, k)

    class Late:
        shard = pack_shards(late_pieces, dtype=BF16, extra=[tail])

        @staticmethod
        def fill(weights, gathered):
            for (k, l), g in zip(late_pieces, split_rows(gathered, late_pieces)):
                weights[k][l] = piece_to_full(g, k)
            r0 = sum(rows_of(p) for p in late_pieces)
            tail_g = gathered[:, r0:r0 + tail.shape[0], :].reshape(N_DEV, -1)
            off = 0
            for k, a in zip(SMALL_SHARDED, sharded_small):
                g = lax.bitcast_convert_type(tail_g[:, off:off + a.size].reshape((N_DEV,) + a.shape), F32)
                weights[k] = jnp.moveaxis(g, 0, -2).reshape(g.shape[1:-1] + (N_DEV * g.shape[-1],))
                off += a.size

        @staticmethod
        def contributions(gw):
            return contributions(gw, main)

    loss, gx, gw, received_main = local_step(given["x"][0], given["loss_target"][0], W, late=Late)
    received_last = exchange_for_reduce_scatter(contributions(gw, last), name="reduce_scatter_last")

    grads_small = {k: jnp.stack([gw[k][l] for l in sorted(gw[k])], axis=0) for k in SMALL}
    small_shapes = [grads_small[k].shape for k in SMALL] + [(1, 1)]
    reduced = all_reduce_small(_pack_small([grads_small[k] for k in SMALL] + [loss]), name="all_reduce_small")
    small_full = dict(zip(SMALL + ("loss",), _unpack_small(reduced, small_shapes)))

    g_piece = {}
    for grp, received in ((main, received_main), (last, received_last)):
        for p, r in zip(grp, received):
            g_piece[p] = sum_slots(r, name="reduce_scatter_sum").reshape(given[p[0]].shape[1:])
    out_g, out_d, out_m, out_v = {}, {}, {}, {}
    for k in BIG:
        g = jnp.stack([g_piece[(k, l)] for l in range(given[k].shape[0])], axis=0)
        out_g[k] = g
        out_d[k], out_m[k], out_v[k] = adamw(given[k], g, given["m_" + k], given["v_" + k], name="adamw_" + k)

    gsmall = {}
    for k in SMALL:
        g = small_full[k]
        if k in SMALL_SHARDED:
            n = given[k].shape[-1]
            g = lax.dynamic_slice_in_dim(g, me * n, n, axis=g.ndim - 1)
        gsmall[k] = g
    local_shapes = [given[k].shape for k in SMALL]
    dsm, nmsm, nvsm = adamw(_pack_small([given[k] for k in SMALL]), _pack_small([gsmall[k] for k in SMALL]),
                            _pack_small([given["m_" + k] for k in SMALL]), _pack_small([given["v_" + k] for k in SMALL]),
                            name="adamw_small")
    out_g.update(gsmall)
    for dst, src in ((out_d, dsm), (out_m, nmsm), (out_v, nvsm)):
        dst.update(zip(SMALL, _unpack_small(src, local_shapes)))

    return (small_full["loss"].reshape(()), gx[None],
            *[out_g[k] for k in WEIGHTS], *[out_d[k] for k in WEIGHTS],
            *[out_m[k] for k in WEIGHTS], *[out_v[k] for k in WEIGHTS])
```

```python
import math

import jax
import jax.numpy as jnp
from jax import lax
from jax.experimental import pallas as pl
from jax.experimental.pallas import tpu as pltpu

F32 = jnp.float32
BF16 = jnp.bfloat16
EPS = 1e-6
N_DEV = 8
SB_HEAD_DIM = 64
SSM_STATE = 128
SSM_CONV = 4
ADAM_LR = 0.001
ADAM_B1 = 0.9
ADAM_B2 = 0.999
ADAM_EPS = 1e-08
ADAM_WD = 0.01
ADAM_STEP = 10
VMEM_LIMIT_BYTES = 56 * 1024 * 1024
MATMUL_VMEM_BUDGET = 40 * 1024 * 1024
LANES = 128
PACK_COLS = 1024

BIG = ("sb_w_qkv", "sb_w_o", "gm_w_in", "gm_w_out", "ssm_w_in", "ssm_w_out", "ffn_w_gu", "ffn_w_down")
COL_SHARDED = ("sb_w_qkv", "gm_w_in", "ssm_w_in", "ffn_w_gu")
SMALL = ("mix_norm", "ffn_norm", "sb_q_gain", "sb_k_gain", "gm_b_in", "gm_v_gain", "gm_w_s", "gm_b_s",
         "ssm_conv_w", "ssm_conv_b", "ssm_dt_bias", "ssm_a_log", "ssm_d", "ssm_norm_gain")
SMALL_SHARDED = ("ssm_conv_w", "ssm_conv_b", "ssm_norm_gain")
WEIGHTS = ("mix_norm", "ffn_norm", "sb_w_qkv", "sb_q_gain", "sb_k_gain", "sb_w_o", "gm_w_in", "gm_b_in",
           "gm_v_gain", "gm_w_s", "gm_b_s", "gm_w_out", "ssm_w_in", "ssm_conv_w", "ssm_conv_b", "ssm_dt_bias",
           "ssm_a_log", "ssm_d", "ssm_norm_gain", "ssm_w_out", "ffn_w_gu", "ffn_w_down")


def _params(semantics=None):
    return pltpu.CompilerParams(dimension_semantics=semantics, vmem_limit_bytes=VMEM_LIMIT_BYTES)


def _pick(n, prefs):
    for t in prefs:
        if t <= n and n % t == 0:
            return t
    return n


def _dot(a, b, ca=1, cb=0):
    return lax.dot_general(a, b, (((ca,), (cb,)), ((), ())), preferred_element_type=F32)


def _split3(v):
    h1 = v.astype(BF16)
    r1 = v - h1.astype(F32)
    h2 = r1.astype(BF16)
    h3 = (r1 - h2.astype(F32)).astype(BF16)
    return h1, h2, h3


def _dot_exact_left(mat01, v):
    h1, h2, h3 = _split3(v)
    return _dot(mat01, h1) + _dot(mat01, h2) + _dot(mat01, h3)


def _dot_split2_right(v, mat01):
    hi = v.astype(BF16)
    lo = (v - hi.astype(F32)).astype(BF16)
    return _dot(hi, mat01) + _dot(lo, mat01)


def _sigmoid(v):
    return 1.0 / (1.0 + jnp.exp(-v))


def _softplus(v):
    return jnp.maximum(v, 0.0) + jnp.log(1.0 + jnp.exp(-jnp.abs(v)))


def _erf(v):
    a = jnp.abs(v)
    t = 1.0 / (1.0 + 0.3275911 * a)
    poly = t * (0.254829592 + t * (-0.284496736 + t * (1.421413741 + t * (-1.453152027 + t * 1.061405429))))
    e = 1.0 - poly * jnp.exp(-a * a)
    return jnp.where(v < 0, -e, e)


def _gelu_and_grad(v):
    cdf = 0.5 * (1.0 + _erf(v * (1.0 / math.sqrt(2.0))))
    pdf = jnp.exp(-0.5 * v * v) * (1.0 / math.sqrt(2.0 * math.pi))
    return v * cdf, cdf + v * pdf


def matmul(a, b, *, ta=False, tb=False, out_dtype=F32, residual=None, a_split=1, b_split=1, name):
    if a_split > 1:
        assert not ta and a.shape[0] == a_split
        M, K = a.shape[1], a_split * a.shape[2]
    elif ta:
        K, M = a.shape
    else:
        M, K = a.shape
    if b_split > 1:
        assert not tb and b.shape[0] == b_split
        Kb, N = b.shape[1], b_split * b.shape[2]
    elif tb:
        N, Kb = b.shape
    else:
        Kb, N = b.shape
    assert K == Kb, (a.shape, b.shape, ta, tb)
    has_res = residual is not None
    tm = _pick(M, (1024, 1408, 768, 512, 256, 128))
    tn = _pick(N // b_split, (1024, 1408, 1536, 768, 512, 256, 128))

    def vmem_bytes(tk):
        tiles = tm * tk * a.dtype.itemsize + tk * tn * b.dtype.itemsize
        outs = tm * tn * jnp.dtype(out_dtype).itemsize + (tm * tn * 4 if has_res else 0)
        return 2 * tiles + 2 * outs + (tm * tn * 4 if tk < K else 0)

    kp = K // a_split
    tk = next((t for t in (K, 2048, 1408, 1024, 512, 256) if t <= kp and kp % t == 0 and vmem_bytes(t) <= MATMUL_VMEM_BUDGET),
              _pick(kp, (128,)))
    nk = K // tk
    if a_split > 1:
        nkb = kp // tk
        a_spec = pl.BlockSpec((None, tm, tk), lambda i, j, k: (k // nkb, i, k % nkb))
    else:
        a_spec = pl.BlockSpec((tk, tm), lambda i, j, k: (k, i)) if ta else pl.BlockSpec((tm, tk), lambda i, j, k: (i, k))
    if b_split > 1:
        njb = N // b_split // tn
        b_spec = pl.BlockSpec((None, tk, tn), lambda i, j, k: (j // njb, k, j % njb))
    else:
        b_spec = pl.BlockSpec((tn, tk), lambda i, j, k: (j, k)) if tb else pl.BlockSpec((tk, tn), lambda i, j, k: (k, j))
    o_spec = pl.BlockSpec((tm, tn), lambda i, j, k: (i, j))
    ca, cb = (0 if ta else 1), (1 if tb else 0)

    def body(*refs):
        a_ref, b_ref = refs[:2]
        r_ref = refs[2] if has_res else None
        o_ref = refs[3] if has_res else refs[2]

        def finish(r):
            if has_res:
                r = r + r_ref[...]
            o_ref[...] = r.astype(out_dtype)

        def part():
            return _dot(a_ref[...].astype(BF16), b_ref[...].astype(BF16), ca, cb)

        if nk == 1:
            finish(part())
            return
        acc = refs[-1]
        k = pl.program_id(2)

        @pl.when(k == 0)
        def _():
            acc[...] = part()

        @pl.when(jnp.logical_and(k > 0, k < nk - 1))
        def _():
            acc[...] += part()

        @pl.when(k == nk - 1)
        def _():
            finish(acc[...] + part())

    in_specs = [a_spec, b_spec] + ([o_spec] if has_res else [])
    args = (a, b) + ((residual,) if has_res else ())
    return pl.pallas_call(
        body, name=name, grid=(M // tm, N // tn, nk), in_specs=in_specs, out_specs=o_spec,
        out_shape=jax.ShapeDtypeStruct((M, N), out_dtype),
        scratch_shapes=[pltpu.VMEM((tm, tn), F32)] if nk > 1 else [],
        compiler_params=_params(("parallel", "parallel", "arbitrary")),
    )(*args)


def rms_fwd(x, gain, *, name):
    S, D = x.shape
    tr = _pick(S, (512, 256, 128))

    def body(x_ref, g_ref, o_ref):
        xv = x_ref[...]
        r = lax.rsqrt(jnp.mean(xv * xv, axis=1, keepdims=True) + EPS)
        o_ref[...] = (xv * r * g_ref[...]).astype(BF16)

    return pl.pallas_call(
        body, name=name, grid=(S // tr,),
        in_specs=[pl.BlockSpec((tr, D), lambda i: (i, 0)), pl.BlockSpec((1, D), lambda i: (0, 0))],
        out_specs=pl.BlockSpec((tr, D), lambda i: (i, 0)), out_shape=jax.ShapeDtypeStruct((S, D), BF16),
        compiler_params=_params(("parallel",)),
    )(x, gain)


def rms_bwd(x, gain, dh, dres, *, name):
    S, D = x.shape
    tr = _pick(S, (512, 256, 128))

    def body(x_ref, g_ref, dh_ref, dr_ref, dx_ref, dg_ref):
        @pl.when(pl.program_id(0) == 0)
        def _():
            dg_ref[...] = jnp.zeros_like(dg_ref)

        xv = x_ref[...]
        dhv = dh_ref[...]
        r = lax.rsqrt(jnp.mean(xv * xv, axis=1, keepdims=True) + EPS)
        xhat = xv * r
        t = dhv * g_ref[...]
        dx_ref[...] = dr_ref[...] + r * (t - xhat * jnp.mean(xhat * t, axis=1, keepdims=True))
        dg_ref[...] += jnp.sum(dhv * xhat, axis=0, keepdims=True)

    row = pl.BlockSpec((tr, D), lambda i: (i, 0))
    vec = pl.BlockSpec((1, D), lambda i: (0, 0))
    return pl.pallas_call(
        body, name=name, grid=(S // tr,), in_specs=[row, vec, row, row], out_specs=[row, vec],
        out_shape=[jax.ShapeDtypeStruct((S, D), F32), jax.ShapeDtypeStruct((1, D), F32)],
        compiler_params=_params(("arbitrary",)),
    )(x, gain, dh, dres)


def ffn_up_fwd(h, w_gu, *, name):
    S, K = h.shape
    F = w_gu.shape[1] // 2
    tm = _pick(S, (512, 256, 128))
    tn = _pick(F, (1408, 1024, 768, 512, 256, 128))
    nj = F // tn

    def body(h_ref, wg_ref, wu_ref, gu_ref, a_ref):
        hv = h_ref[...]
        g = _dot(hv, wg_ref[...])
        u = _dot(hv, wu_ref[...])
        gu_ref[0] = g
        gu_ref[1] = u
        a_ref[...] = (g * _sigmoid(g) * u).astype(BF16)

    return pl.pallas_call(
        body, name=name, grid=(nj, S // tm),
        in_specs=[pl.BlockSpec((tm, K), lambda j, i: (i, 0)), pl.BlockSpec((K, tn), lambda j, i: (0, j)),
                  pl.BlockSpec((K, tn), lambda j, i: (0, nj + j))],
        out_specs=[pl.BlockSpec((2, tm, tn), lambda j, i: (0, i, j)), pl.BlockSpec((tm, tn), lambda j, i: (i, j))],
        out_shape=[jax.ShapeDtypeStruct((2, S, F), F32), jax.ShapeDtypeStruct((S, F), BF16)],
        compiler_params=_params(("parallel", "parallel")),
    )(h, w_gu, w_gu)


def ffn_up_bwd(dy, w_down, gu, *, name):
    S, D = dy.shape
    F = w_down.shape[0]
    tm = _pick(S, (512, 256, 128))
    tn = _pick(F, (1408, 1024, 768, 512, 256, 128))

    def body(dy_ref, wd_ref, gu_ref, o_ref):
        da = _dot(dy_ref[...].astype(BF16), wd_ref[...], 1, 1)
        g = gu_ref[0]
        u = gu_ref[1]
        s = _sigmoid(g)
        o_ref[0] = (da * u * (s * (1.0 + g * (1.0 - s)))).astype(BF16)
        o_ref[1] = (da * g * s).astype(BF16)

    pair = pl.BlockSpec((2, tm, tn), lambda j, i: (0, i, j))
    return pl.pallas_call(
        body, name=name, grid=(F // tn, S // tm),
        in_specs=[pl.BlockSpec((tm, D), lambda j, i: (i, 0)), pl.BlockSpec((tn, D), lambda j, i: (j, 0)), pair],
        out_specs=pair, out_shape=jax.ShapeDtypeStruct((2, S, F), BF16),
        compiler_params=_params(("parallel", "parallel")),
    )(dy, w_down, gu)


def loss_head(y, target, *, name):
    S, D = y.shape
    tr = _pick(S, (512, 256, 128))

    def body(y_ref, t_ref, dy_ref, l_ref):
        @pl.when(pl.program_id(0) == 0)
        def _():
            l_ref[...] = jnp.zeros_like(l_ref)

        err = y_ref[...] - t_ref[...]
        dy_ref[...] = err * (1.0 / D)
        l_ref[...] += jnp.sum(0.5 * jnp.mean(err * err, axis=1, keepdims=True), axis=0, keepdims=True)

    row = pl.BlockSpec((tr, D), lambda i: (i, 0))
    one = pl.BlockSpec((1, 1), lambda i: (0, 0))
    dy, l = pl.pallas_call(
        body, name=name, grid=(S // tr,), in_specs=[row, row], out_specs=[row, one],
        out_shape=[jax.ShapeDtypeStruct((S, D), F32), jax.ShapeDtypeStruct((1, 1), F32)],
        compiler_params=_params(("arbitrary",)),
    )(y, target)
    return dy, l


def adamw(w, g, m, v, *, name):
    R, C = w.shape[-2:]
    tr = _pick(R, (512, 256, 128, 64, 32, 16, 8))

    def body(w_ref, g_ref, m_ref, v_ref, d_ref, mo_ref, vo_ref):
        gv = g_ref[...]
        mn = ADAM_B1 * m_ref[...] + (1.0 - ADAM_B1) * gv
        vn = ADAM_B2 * v_ref[...] + (1.0 - ADAM_B2) * jnp.square(gv)
        m_hat = mn / (1.0 - ADAM_B1 ** ADAM_STEP)
        v_hat = vn / (1.0 - ADAM_B2 ** ADAM_STEP)
        d_ref[...] = -ADAM_LR * (m_hat / (jnp.sqrt(v_hat) + ADAM_EPS) + ADAM_WD * w_ref[...])
        mo_ref[...] = mn
        vo_ref[...] = vn

    if w.ndim == 3:
        grid = (w.shape[0], R // tr)
        blk = pl.BlockSpec((None, tr, C), lambda l, i: (l, i, 0))
    else:
        grid = (R // tr,)
        blk = pl.BlockSpec((tr, C), lambda i: (i, 0))
    sds = jax.ShapeDtypeStruct(w.shape, F32)
    return pl.pallas_call(
        body, name=name, grid=grid, in_specs=[blk] * 4, out_specs=[blk] * 3, out_shape=[sds] * 3,
        compiler_params=_params(("parallel",) * len(grid)),
    )(w, g, m, v)


def _tri(n, kind):
    r = lax.broadcasted_iota(jnp.int32, (n, n), 0)
    c = lax.broadcasted_iota(jnp.int32, (n, n), 1)
    if kind == "row_gt_col":
        return (r > c).astype(BF16)
    if kind == "row_ge_col":
        return (r >= c).astype(BF16)
    if kind == "row_le_col":
        return (r <= c).astype(BF16)
    raise ValueError(kind)


def _sb_tile(qs, kj, r_carry, u_strict, masked):
    z = _dot(qs, kj, 1, 1)
    lb = jnp.minimum(z, 0.0) - jnp.log(1.0 + jnp.exp(-jnp.abs(z)))
    l1m = lb - z
    keep = None
    if masked:
        tq, tk = z.shape
        keep = lax.broadcasted_iota(jnp.int32, (tq, tk), 1) < lax.broadcasted_iota(jnp.int32, (tq, tk), 0)
        l1m = jnp.where(keep, l1m, 0.0)
    w = jnp.exp(lb + _dot(l1m.astype(BF16), u_strict) + r_carry)
    if masked:
        w = jnp.where(keep, w, 0.0)
    return lb, l1m, w, keep


def _sb_prep(T, nb, hd, refs_in, gains, scratch):
    q_scale = 1.0 / math.sqrt(hd)
    assert math.log2(q_scale) == round(math.log2(q_scale))

    def prep(i, _):
        rows = pl.ds(pl.multiple_of(i * T, T), T)
        for hh in range(2):
            sl = slice(hd * hh, hd * hh + hd)
            for n, (src, dst) in enumerate(zip(refs_in, scratch)):
                v = src[rows, sl]
                if n < 2:
                    v = v * lax.rsqrt(jnp.mean(v * v, axis=1, keepdims=True) + EPS) * gains[n][...]
                if n == 0:
                    v = v * q_scale
                dst[hh, rows, :] = v.astype(BF16)
        return 0

    lax.fori_loop(0, nb, prep, 0)


def _sb_chains(m, T):
    rows = [pl.ds(pl.multiple_of((2 * m + qb) * T, T), T) for qb in range(2)]
    return rows, [(hh, qb) for qb in range(2) for hh in range(2)]


def sb_attn_fwd(qkv, q_gain, k_gain, *, name, gather=None):
    S, D3 = qkv.shape
    D = D3 // 3
    npairs = D // LANES
    hd = SB_HEAD_DIM
    T = min(256, S)
    nb = S // T
    assert nb % 2 == 0

    def body(*refs):
        if gather is None:
            q_ref, k_ref, v_ref, qg_ref, kg_ref, us_ref, o_ref, qn_s, kn_s, vb_s = refs
        else:
            q_ref, k_ref, v_ref, qg_ref, kg_ref, us_ref, ag_in, o_ref, ag_out, qn_s, kn_s, vb_s = refs[:12]
            comm = (ag_in, ag_out) + refs[12:]
            step = pl.program_id(0)
            pl.when(step == 0)(lambda: _ag_phase(comm, "start"))
            pl.when(step == (5 * npairs) // 8)(lambda: _ag_phase(comm, "forward"))
        us = us_ref[...]
        _sb_prep(T, nb, hd, (q_ref, k_ref, v_ref), (qg_ref, kg_ref), (qn_s, kn_s, vb_s))

        def superblock(m, _):
            rows_q, chains = _sb_chains(m, T)
            qs = {ch: qn_s[ch[0], rows_q[ch[1]], :] for ch in chains}

            def tile(j, carry, which):
                rows_j = pl.ds(pl.multiple_of(j * T, T), T)
                new = dict(carry)
                for ch, masked in which:
                    acc, rc = carry[ch]
                    _, l1m, w, _ = _sb_tile(qs[ch], kn_s[ch[0], rows_j, :], rc, us, masked)
                    new[ch] = (acc + _dot(w.astype(BF16), vb_s[ch[0], rows_j, :]),
                               rc + jnp.sum(l1m, axis=1, keepdims=True))
                return new

            carry = {ch: (jnp.zeros((T, hd), F32), jnp.zeros((T, 1), F32)) for ch in chains}
            carry = tile(2 * m + 1, carry, [(ch, True) for ch in chains if ch[1] == 1])
            carry = tile(2 * m, carry, [(ch, ch[1] == 0) for ch in chains])
            carry = lax.fori_loop(0, 2 * m, lambda jj, c: tile(2 * m - 1 - jj, c, [(ch, False) for ch in chains]),
                                  carry)
            for qb in range(2):
                o_ref[rows_q[qb], :] = jnp.concatenate([carry[(0, qb)][0], carry[(1, qb)][0]], axis=1)
            return 0

        lax.fori_loop(0, nb // 2, superblock, 0)
        if gather is not None:
            pl.when(step == npairs - 1)(lambda: _ag_phase(comm, "finish"))

    col = lambda off: pl.BlockSpec((S, LANES), lambda p, off=off: (0, off + p))
    gain = pl.BlockSpec((1, hd), lambda p: (0, 0))
    in_specs = [col(0), col(npairs), col(2 * npairs), gain, gain, pl.BlockSpec((T, T), lambda p: (0, 0))]
    out_specs = [pl.BlockSpec((S, LANES), lambda p: (0, p))]
    out_shape = [jax.ShapeDtypeStruct((S, D), F32)]
    scratch = [pltpu.VMEM((2, S, hd), BF16)] * 3
    args = [qkv, qkv, qkv, q_gain, k_gain, _tri(T, "row_gt_col")]
    if gather is not None:
        in_specs.append(HBM_SPEC)
        out_specs.append(HBM_SPEC)
        out_shape.append(jax.ShapeDtypeStruct((N_DEV,) + gather.shape, gather.dtype))
        scratch += COMM_SEMAPHORES
        args.append(gather)
    out = pl.pallas_call(
        body, name=name, grid=(npairs,), in_specs=in_specs, out_specs=out_specs, out_shape=out_shape,
        scratch_shapes=scratch, compiler_params=_params(("arbitrary",)),
    )(*args)
    return out[0] if gather is None else tuple(out)


def sb_attn_bwd(qkv, o, do, q_gain, k_gain, *, name, scatter=None):
    S, D3 = qkv.shape
    D = D3 // 3
    npairs = D // LANES
    hd = SB_HEAD_DIM
    T = min(256, S)
    nb = S // T
    scale = 1.0 / math.sqrt(hd)

    def body(*refs):
        if scatter is None:
            (q_ref, k_ref, v_ref, o_ref, do_ref, qg_ref, kg_ref, us_ref,
             dqkv_ref, dg_ref, qn_s, kn_s, vb_s, dob_s) = refs
        else:
            ns = len(scatter)
            q_ref, k_ref, v_ref, o_ref, do_ref, qg_ref, kg_ref, us_ref = refs[:8]
            rs_in = refs[8:8 + ns]
            dqkv_ref, dg_ref = refs[8 + ns:10 + ns]
            rs_out = refs[10 + ns:10 + 2 * ns]
            qn_s, kn_s, vb_s, dob_s = refs[10 + 2 * ns:14 + 2 * ns]
            rs_sems = refs[14 + 2 * ns:]
            pl.when(pl.program_id(0) == 0)(lambda: _rs_phase(rs_in, rs_out, rs_sems, "start"))
        dq_ref, dk_ref, dv_ref = dqkv_ref.at[0], dqkv_ref.at[1], dqkv_ref.at[2]

        @pl.when(pl.program_id(0) == 0)
        def _():
            dg_ref[...] = jnp.zeros_like(dg_ref)

        us = us_ref[...]
        u_prefix = (1.0 - us.astype(F32)).astype(BF16)
        _sb_prep(T, nb, hd, (q_ref, k_ref, v_ref, do_ref), (qg_ref, kg_ref), (qn_s, kn_s, vb_s, dob_s))
        dk_ref[...] = jnp.zeros_like(dk_ref)
        dv_ref[...] = jnp.zeros_like(dv_ref)

        def superblock(m, _):
            rows_q, chains = _sb_chains(m, T)
            qs = {ch: qn_s[ch[0], rows_q[ch[1]], :] for ch in chains}
            doi = {ch: dob_s[ch[0], rows_q[ch[1]], :] for ch in chains}
            dt_total = {ch: jnp.sum(doi[ch].astype(F32) * o_ref[rows_q[ch[1]], hd * ch[0]:hd * ch[0] + hd],
                                    axis=1, keepdims=True) for ch in chains}

            def tile(j, carry, which):
                rows_j = pl.ds(pl.multiple_of(j * T, T), T)
                new = dict(carry)
                dk_part, dv_part = {}, {}
                for ch, masked in which:
                    hh = ch[0]
                    dq_acc, rc, gc = carry[ch]
                    kj = kn_s[hh, rows_j, :]
                    lb, l1m, w, keep = _sb_tile(qs[ch], kj, rc, us, masked)
                    wb = w.astype(BF16)
                    g = _dot(doi[ch], vb_s[hh, rows_j, :], 1, 1) * wb.astype(F32)
                    g_row = jnp.sum(g, axis=1, keepdims=True)
                    g_upto = (dt_total[ch] - gc - g_row) + _dot(g.astype(BF16), u_prefix)
                    dz = g - g_upto * jnp.exp(lb)
                    if masked:
                        dz = jnp.where(keep, dz, 0.0)
                    dzb = dz.astype(BF16)
                    dv_part[hh] = dv_part.get(hh, 0.0) + _dot(wb, doi[ch], 0, 0)
                    dk_part[hh] = dk_part.get(hh, 0.0) + _dot(dzb, qs[ch], 0, 0)
                    new[ch] = (dq_acc + _dot(dzb, kj), rc + jnp.sum(l1m, axis=1, keepdims=True),
                               gc + g_row)
                dv_ref[rows_j, :] += jnp.concatenate([dv_part[0], dv_part[1]], axis=1)
                dk_ref[rows_j, :] += jnp.concatenate([dk_part[0], dk_part[1]], axis=1)
                return new

            zero1 = jnp.zeros((T, 1), F32)
            carry = {ch: (jnp.zeros((T, hd), F32), zero1, zero1) for ch in chains}
            carry = tile(2 * m + 1, carry, [(ch, True) for ch in chains if ch[1] == 1])
            carry = tile(2 * m, carry, [(ch, ch[1] == 0) for ch in chains])
            carry = lax.fori_loop(0, 2 * m, lambda jj, c: tile(2 * m - 1 - jj, c, [(ch, False) for ch in chains]),
                                  carry)
            for qb in range(2):
                dq_ref[rows_q[qb], :] = jnp.concatenate([carry[(0, qb)][0], carry[(1, qb)][0]], axis=1) * scale
            return 0

        lax.fori_loop(0, nb // 2, superblock, 0)

        def finish(i, carry):
            rows = pl.ds(pl.multiple_of(i * T, T), T)
            new = []
            for hh in range(2):
                sl = slice(hd * hh, hd * hh + hd)
                outs = []
                for raw_ref, gain_ref, dn in ((q_ref, qg_ref, dq_ref[rows, sl]), (k_ref, kg_ref, dk_ref[rows, sl])):
                    raw = raw_ref[rows, sl]
                    r = lax.rsqrt(jnp.mean(raw * raw, axis=1, keepdims=True) + EPS)
                    hat = raw * r
                    t = dn * gain_ref[...]
                    outs.append((r * (t - hat * jnp.mean(hat * t, axis=1, keepdims=True)),
                                 jnp.sum(dn * hat, axis=0, keepdims=True)))
                dq_ref[rows, sl] = outs[0][0]
                dk_ref[rows, sl] = outs[1][0]
                new.append((carry[hh][0] + outs[0][1], carry[hh][1] + outs[1][1]))
            return tuple(new)

        zg = (jnp.zeros((1, hd), F32), jnp.zeros((1, hd), F32))
        tot = lax.fori_loop(0, nb, finish, (zg, zg))
        dg_ref[0:1, 0:hd] += tot[0][0] + tot[1][0]
        dg_ref[1:2, 0:hd] += tot[0][1] + tot[1][1]
        if scatter is not None:
            pl.when(pl.program_id(0) == npairs - 1)(lambda: _rs_phase(rs_in, rs_out, rs_sems, "finish"))

    col = lambda off: pl.BlockSpec((S, LANES), lambda p, off=off: (0, off + p))
    gain = pl.BlockSpec((1, hd), lambda p: (0, 0))
    tri = pl.BlockSpec((T, T), lambda p: (0, 0))
    pair = pl.BlockSpec((S, LANES), lambda p: (0, p))
    in_specs = [col(0), col(npairs), col(2 * npairs), pair, pair, gain, gain, tri]
    out_specs = [pl.BlockSpec((3, S, LANES), lambda p: (0, 0, p)), pl.BlockSpec((8, LANES), lambda p: (0, 0))]
    out_shape = [jax.ShapeDtypeStruct((3, S, D), F32), jax.ShapeDtypeStruct((8, LANES), F32)]
    scratch = [pltpu.VMEM((2, S, hd), BF16)] * 4
    args = [qkv, qkv, qkv, o, do, q_gain, k_gain, _tri(T, "row_gt_col")]
    if scatter is not None:
        in_specs += [HBM_SPEC] * len(scatter)
        out_specs += [HBM_SPEC] * len(scatter)
        out_shape += [jax.ShapeDtypeStruct(g.shape, g.dtype) for g in scatter]
        scratch += _rs_semaphores(len(scatter))
        args += list(scatter)
    out = pl.pallas_call(
        body, name=name, grid=(npairs,), in_specs=in_specs, out_specs=out_specs, out_shape=out_shape,
        scratch_shapes=scratch, compiler_params=_params(("arbitrary",)),
    )(*args)
    res = (out[0], out[1][0:1, :hd], out[1][1:2, :hd])
    return res if scatter is None else res + (list(out[2:]),)


def gmlp_fwd(zzpre, b_in, v_gain, wc, bsf, *, name):
    S, H2 = zzpre.shape
    H = H2 // 2
    G, T, _ = wc.shape
    gd = H // G

    def body(z_ref, b_ref, vg_ref, wc_ref, bs_ref, p_ref):
        zz, _ = _gelu_and_grad(z_ref[...] + b_ref[...])
        u = zz[:, :H]
        v = zz[:, H:]
        vn = v * lax.rsqrt(jnp.mean(v * v, axis=1, keepdims=True) + EPS) * vg_ref[...]
        for g in range(G):
            gs = slice(g * gd, (g + 1) * gd)
            mixed = _dot(wc_ref[g], vn[:, gs].astype(BF16)) + bs_ref[g]
            p_ref[:, gs] = (u[:, gs] * mixed).astype(BF16)

    full3 = lambda shp: pl.BlockSpec(shp, lambda c: (0, 0, 0))
    return pl.pallas_call(
        body, name=name, grid=(S // T,),
        in_specs=[pl.BlockSpec((T, H2), lambda c: (c, 0)), pl.BlockSpec((1, H2), lambda c: (0, 0)),
                  pl.BlockSpec((1, H), lambda c: (0, 0)), full3((G, T, T)), full3((G, T, gd))],
        out_specs=pl.BlockSpec((T, H), lambda c: (c, 0)), out_shape=jax.ShapeDtypeStruct((S, H), BF16),
        compiler_params=_params(("parallel",)),
    )(zzpre, b_in, v_gain, wc, bsf)


def gmlp_bwd(zzpre, b_in, v_gain, wc, bsf, dp, *, name):
    S, H2 = zzpre.shape
    H = H2 // 2
    G, T, _ = wc.shape
    gd = H // G
    assert G <= LANES

    def body(z_ref, b_ref, vg_ref, wc_ref, bs_ref, dp_ref, dzz_ref, db_ref, dvg_ref, dws_ref, dbs_ref):
        @pl.when(pl.program_id(0) == 0)
        def _():
            db_ref[...] = jnp.zeros_like(db_ref)
            dvg_ref[...] = jnp.zeros_like(dvg_ref)
            dws_ref[...] = jnp.zeros_like(dws_ref)
            dbs_ref[...] = jnp.zeros_like(dbs_ref)

        zz, gp = _gelu_and_grad(z_ref[...] + b_ref[...])
        u = zz[:, :H]
        v = zz[:, H:]
        r = lax.rsqrt(jnp.mean(v * v, axis=1, keepdims=True) + EPS)
        vhat = v * r
        vg = vg_ref[...]
        vn = vhat * vg
        dpv = dp_ref[...]
        tril = lax.broadcasted_iota(jnp.int32, (T, T), 1) <= lax.broadcasted_iota(jnp.int32, (T, T), 0)
        lane = lax.broadcasted_iota(jnp.int32, (T, LANES), 1)
        dbs = jnp.zeros((T, LANES), F32)
        du_parts, dvn_parts = [], []
        for g in range(G):
            gs = slice(g * gd, (g + 1) * gd)
            vng = vn[:, gs].astype(BF16)
            wcg = wc_ref[g]
            mixed = _dot(wcg, vng) + bs_ref[g]
            dpg = dpv[:, gs]
            du_parts.append(dpg * mixed)
            dmx = dpg * u[:, gs]
            dmxb = dmx.astype(BF16)
            dvn_parts.append(_dot(wcg, dmxb, 0, 0))
            dws_ref[g] += jnp.where(tril, _dot(dmxb, vng, 1, 1), 0.0)
            dbs = dbs + jnp.where(lane == g, jnp.sum(dmx, axis=1, keepdims=True), 0.0)
        dbs_ref[...] += dbs
        du = jnp.concatenate(du_parts, axis=1)
        dvn = jnp.concatenate(dvn_parts, axis=1)
        dvg_ref[...] += jnp.sum(dvn * vhat, axis=0, keepdims=True)
        t = dvn * vg
        dv = r * (t - vhat * jnp.mean(vhat * t, axis=1, keepdims=True))
        dzu = du * gp[:, :H]
        dzv = dv * gp[:, H:]
        dzz_ref[:, :H] = dzu.astype(BF16)
        dzz_ref[:, H:] = dzv.astype(BF16)
        db_ref[:, :H] += jnp.sum(dzu, axis=0, keepdims=True)
        db_ref[:, H:] += jnp.sum(dzv, axis=0, keepdims=True)

    full3 = lambda shp: pl.BlockSpec(shp, lambda c: (0, 0, 0))
    vec = lambda n: pl.BlockSpec((1, n), lambda c: (0, 0))
    return pl.pallas_call(
        body, name=name, grid=(S // T,),
        in_specs=[pl.BlockSpec((T, H2), lambda c: (c, 0)), vec(H2), vec(H), full3((G, T, T)), full3((G, T, gd)),
                  pl.BlockSpec((T, H), lambda c: (c, 0))],
        out_specs=[pl.BlockSpec((T, H2), lambda c: (c, 0)), vec(H2), vec(H), full3((G, T, T)),
                   pl.BlockSpec((T, LANES), lambda c: (0, 0))],
        out_shape=[jax.ShapeDtypeStruct((S, H2), BF16), jax.ShapeDtypeStruct((1, H2), F32),
                   jax.ShapeDtypeStruct((1, H), F32), jax.ShapeDtypeStruct((G, T, T), F32),
                   jax.ShapeDtypeStruct((T, LANES), F32)],
        compiler_params=_params(("arbitrary",)),
    )(zzpre, b_in, v_gain, wc, bsf, dp)


def _shift_rows(v, k, n_rows):
    if k == 0:
        return v
    rolled = pltpu.roll(v, k % n_rows, 0)
    row = lax.broadcasted_iota(jnp.int32, v.shape, 0)
    keep = (row >= k) if k > 0 else (row < n_rows + k)
    return jnp.where(keep, rolled, 0.0)


def conv_fwd(zx, conv_w, conv_b, col0, *, name):
    S = zx.shape[0]
    C = conv_w.shape[1]
    tc = _pick(C, (256, 128))
    off = col0 // tc
    assert col0 % tc == 0

    def body(x_ref, w_ref, b_ref, o_ref):
        xv = x_ref[...]
        acc = b_ref[...] + w_ref[SSM_CONV - 1:SSM_CONV, :] * xv
        for k in range(SSM_CONV - 1):
            acc = acc + w_ref[k:k + 1, :] * _shift_rows(xv, SSM_CONV - 1 - k, S)
        o_ref[...] = acc * _sigmoid(acc)

    return pl.pallas_call(
        body, name=name, grid=(C // tc,),
        in_specs=[pl.BlockSpec((S, tc), lambda j: (0, off + j)), pl.BlockSpec((SSM_CONV, tc), lambda j: (0, j)),
                  pl.BlockSpec((1, tc), lambda j: (0, j))],
        out_specs=pl.BlockSpec((S, tc), lambda j: (0, j)), out_shape=jax.ShapeDtypeStruct((S, C), F32),
        compiler_params=_params(("parallel",)),
    )(zx, conv_w, conv_b)


def conv_bwd(zx, conv_w, conv_b, col0, dout, *, name):
    S = zx.shape[0]
    C = conv_w.shape[1]
    tc = _pick(C, (256, 128))
    off = col0 // tc

    def body(x_ref, w_ref, b_ref, do_ref, dx_ref, dw_ref, db_ref):
        xv = x_ref[...]
        shifted = [_shift_rows(xv, SSM_CONV - 1 - k, S) for k in range(SSM_CONV)]
        acc = b_ref[...]
        for k in range(SSM_CONV):
            acc = acc + w_ref[k:k + 1, :] * shifted[k]
        s = _sigmoid(acc)
        dacc = do_ref[...] * (s * (1.0 + acc * (1.0 - s)))
        db_ref[...] = jnp.sum(dacc, axis=0, keepdims=True)
        dx = jnp.zeros_like(xv)
        for k in range(SSM_CONV):
            dw_ref[k:k + 1, :] = jnp.sum(dacc * shifted[k], axis=0, keepdims=True)
            dx = dx + w_ref[k:k + 1, :] * _shift_rows(dacc, -(SSM_CONV - 1 - k), S)
        dx_ref[...] = dx

    slab = pl.BlockSpec((S, tc), lambda j: (0, j))
    return pl.pallas_call(
        body, name=name, grid=(C // tc,),
        in_specs=[pl.BlockSpec((S, tc), lambda j: (0, off + j)), pl.BlockSpec((SSM_CONV, tc), lambda j: (0, j)),
                  pl.BlockSpec((1, tc), lambda j: (0, j)), slab],
        out_specs=[slab, pl.BlockSpec((SSM_CONV, tc), lambda j: (0, j)), pl.BlockSpec((1, tc), lambda j: (0, j))],
        out_shape=[jax.ShapeDtypeStruct((S, C), F32), jax.ShapeDtypeStruct((SSM_CONV, C), F32),
                   jax.ShapeDtypeStruct((1, C), F32)],
        compiler_params=_params(("parallel",)),
    )(zx, conv_w, conv_b, dout)


def _ssd_chunk_terms(dtraw, bias, a_log, tl):
    dt = _softplus(dtraw + bias)
    a_neg = -jnp.exp(a_log)
    ac = _dot_exact_left(tl, dt * a_neg)
    ac_last = ac[ac.shape[0] - 1:, :]
    return dt, a_neg, ac, ac.T, jnp.exp(ac), jnp.exp(ac_last - ac), jnp.exp(ac_last)


def _ssd_specs(S, L, G, hpg, pd, inner):
    gw = hpg * pd
    n = SSM_STATE
    xb = inner // n

    def mk(cidx):
        return dict(
            x=pl.BlockSpec((L, gw), lambda g, c: (cidx(c), g)),
            b=pl.BlockSpec((L, n), lambda g, c: (cidx(c), xb + g)),
            c=pl.BlockSpec((L, n), lambda g, c: (cidx(c), xb + G + g)),
            z=pl.BlockSpec((L, gw), lambda g, c: (cidx(c), g)),
            dt=pl.BlockSpec((L, LANES), lambda g, c: (cidx(c), g)),
            gvec=pl.BlockSpec((1, 1, LANES), lambda g, c: (g, 0, 0)),
            chan=pl.BlockSpec((1, gw), lambda g, c: (0, g)),
            tri=pl.BlockSpec((L, L), lambda g, c: (0, 0)),
            hp=pl.BlockSpec((1, 1, gw, n), lambda g, c: (g, cidx(c), 0, 0)),
            bc=pl.BlockSpec((L, n), lambda g, c: (cidx(c), g)),
        )
    return mk


def ssd_fwd(xbc, zx, dtg, bias_g, alog_g, d_chan, ngain, L, G, *, name):
    S = xbc.shape[0]
    n = SSM_STATE
    inner = xbc.shape[1] - 2 * G * n
    gw = inner // G
    pd = SB_HEAD_DIM
    hpg = gw // pd
    nc = S // L
    sp = _ssd_specs(S, L, G, hpg, pd, inner)(lambda c: c)

    def body(x_ref, b_ref, c_ref, z_ref, dt_ref, bias_ref, alog_ref, d_ref, ng_ref, tl_ref,
             yn_ref, y_ref, hp_ref, state):
        @pl.when(pl.program_id(1) == 0)
        def _():
            state[...] = jnp.zeros_like(state)

        dt, _, ac, act, ea, dte, cd = _ssd_chunk_terms(dt_ref[...], bias_ref[0], alog_ref[0], tl_ref[...])
        xv = x_ref[...]
        bm = b_ref[...].astype(BF16)
        cm = c_ref[...].astype(BF16)
        cb = _dot(cm, bm, 1, 1)
        tril = lax.broadcasted_iota(jnp.int32, (L, L), 1) <= lax.broadcasted_iota(jnp.int32, (L, L), 0)
        hp_ref[0, 0] = state[...]
        for r in range(hpg):
            ps = slice(r * pd, (r + 1) * pd)
            xr = xv[:, ps]
            xdt = xr * dt[:, r:r + 1]
            lm = jnp.exp(jnp.where(tril, ac[:, r:r + 1] - act[r:r + 1, :], -jnp.inf))
            hprev = state[ps, :]
            y = _dot((cb * lm).astype(BF16), xdt.astype(BF16))
            y = y + _dot(cm, hprev.astype(BF16), 1, 1) * ea[:, r:r + 1]
            y_ref[:, ps] = y + xr * d_ref[:, ps]
            st = _dot((xdt * dte[:, r:r + 1]).astype(BF16), bm, 0, 0)
            state[ps, :] = hprev * cd[:, r:r + 1] + st
        yfull = y_ref[...]
        zg = z_ref[...]
        yg = yfull * (zg * _sigmoid(zg))
        yn_ref[...] = (yg * lax.rsqrt(jnp.mean(yg * yg, axis=1, keepdims=True) + EPS) * ng_ref[...]).astype(BF16)

    return pl.pallas_call(
        body, name=name, grid=(G, nc),
        in_specs=[sp["x"], sp["b"], sp["c"], sp["z"], sp["dt"], sp["gvec"], sp["gvec"], sp["chan"], sp["chan"], sp["tri"]],
        out_specs=[sp["x"], sp["x"], sp["hp"]],
        out_shape=[jax.ShapeDtypeStruct((S, inner), BF16), jax.ShapeDtypeStruct((S, inner), F32),
                   jax.ShapeDtypeStruct((G, nc, gw, n), F32)],
        scratch_shapes=[pltpu.VMEM((gw, n), F32)],
        compiler_params=_params(("arbitrary", "arbitrary")),
    )(xbc, xbc, xbc, zx, dtg, bias_g, alog_g, d_chan, ngain, _tri(L, "row_ge_col"))


def ssd_bwd(xbc, zx, dtg, bias_g, alog_g, d_chan, ngain, yfull, hp, dyn, L, G, *, name):
    S = xbc.shape[0]
    n = SSM_STATE
    inner = xbc.shape[1] - 2 * G * n
    gw = inner // G
    pd = SB_HEAD_DIM
    hpg = gw // pd
    nc = S // L
    sp = _ssd_specs(S, L, G, hpg, pd, inner)(lambda c: nc - 1 - c)

    def body(x_ref, b_ref, c_ref, z_ref, dt_ref, bias_ref, alog_ref, d_ref, ng_ref, tl_ref, tu_ref,
             yf_ref, hp_ref, dyn_ref,
             dz_ref, dx_ref, db_ref, dc_ref, ddt_ref, dbias_ref, dalog_ref, dd_ref, dng_ref, dstate):
        first = pl.program_id(1) == 0

        @pl.when(first)
        def _():
            dstate[...] = jnp.zeros_like(dstate)
            dbias_ref[...] = jnp.zeros_like(dbias_ref)
            dalog_ref[...] = jnp.zeros_like(dalog_ref)
            dd_ref[...] = jnp.zeros_like(dd_ref)
            dng_ref[...] = jnp.zeros_like(dng_ref)

        dtraw = dt_ref[...]
        dt, a_neg, ac, act, ea, dte, cd = _ssd_chunk_terms(dtraw, bias_ref[0], alog_ref[0], tl_ref[...])
        xv = x_ref[...]
        bm = b_ref[...].astype(BF16)
        cm = c_ref[...].astype(BF16)
        cb = _dot(cm, bm, 1, 1)
        tril = lax.broadcasted_iota(jnp.int32, (L, L), 1) <= lax.broadcasted_iota(jnp.int32, (L, L), 0)
        lane = lax.broadcasted_iota(jnp.int32, (L, LANES), 1)
        lane1 = lax.broadcasted_iota(jnp.int32, (1, LANES), 1)

        yfull = yf_ref[...]
        zg = z_ref[...]
        sg = _sigmoid(zg)
        gate = zg * sg
        yg = yfull * gate
        rr = lax.rsqrt(jnp.mean(yg * yg, axis=1, keepdims=True) + EPS)
        yhat = yg * rr
        dynv = dyn_ref[...]
        dng_ref[...] += jnp.sum(dynv * yhat, axis=0, keepdims=True)
        t = dynv * ng_ref[...]
        dyg = rr * (t - yhat * jnp.mean(yhat * t, axis=1, keepdims=True))
        dy = dyg * gate
        dz_ref[...] = dyg * yfull * (sg * (1.0 + zg * (1.0 - sg)))

        dcb = jnp.zeros((L, L), F32)
        dc_acc = jnp.zeros((L, n), F32)
        db_acc = jnp.zeros((L, n), F32)
        dac = jnp.zeros((L, LANES), F32)
        xdx = jnp.zeros((L, LANES), F32)
        tail = jnp.zeros((1, LANES), F32)
        dskip = jnp.zeros((1, LANES), F32)
        ones_l = jnp.ones((L, LANES), BF16)
        for r in range(hpg):
            ps = slice(r * pd, (r + 1) * pd)
            xr = xv[:, ps]
            dyr = dy[:, ps]
            dtr = dt[:, r:r + 1]
            dter = dte[:, r:r + 1]
            cdr = cd[:, r:r + 1]
            xdt = xr * dtr
            xdtb = xdt.astype(BF16)
            dyrb = dyr.astype(BF16)
            lm = jnp.exp(jnp.where(tril, ac[:, r:r + 1] - act[r:r + 1, :], -jnp.inf))
            m32 = cb * lm
            mb = m32.astype(BF16)
            hprev = hp_ref[0, 0, ps, :]
            hpb = hprev.astype(BF16)
            dhn = dstate[ps, :]
            dhnb = dhn.astype(BF16)
            ear = ea[:, r:r + 1]
            gy = (dyr * ear).astype(BF16)
            dc_acc = dc_acc + _dot(gy, hpb)
            dstate[ps, :] = _dot(gy, cm, 0, 0) + dhn * cdr
            bdh = _dot(bm, dhnb, 1, 1)
            db_acc = db_acc + _dot((xdt * dter).astype(BF16), dhnb)
            dm = _dot(dyrb, xdtb, 1, 1)
            dxdt = bdh * dter + _dot(mb, dyrb, 0, 0)
            dcb = dcb + dm * lm
            wmat = dm * m32
            whi = wmat.astype(BF16)
            wlo = (wmat - whi.astype(F32)).astype(BF16)
            col_w = _dot(whi, ones_l, 0, 0) + _dot(wlo, ones_l, 0, 0)
            sdte = jnp.sum(xdt * bdh * dter, axis=1, keepdims=True)
            e_r = jnp.sum(wmat, axis=1, keepdims=True) + jnp.sum(dyr * _dot(cm, hpb, 1, 1) * ear, axis=1, keepdims=True) - sdte
            c_r = cdr * jnp.sum(jnp.sum(dhn * hprev, axis=1, keepdims=True), axis=0, keepdims=True) \
                + jnp.sum(sdte, axis=0, keepdims=True)
            dac = dac + jnp.where(lane == r, e_r - col_w, 0.0)
            xdx = xdx + jnp.where(lane == r, jnp.sum(dxdt * xr, axis=1, keepdims=True), 0.0)
            tail = tail + jnp.where(lane1 == r, c_r, 0.0)
            dskip = dskip + jnp.where(lane1 == r, jnp.sum(jnp.sum(dyr * xr, axis=1, keepdims=True), axis=0, keepdims=True), 0.0)
            dx_ref[:, ps] = dxdt * dtr + dyr * d_ref[:, ps]
        dcbb = dcb.astype(BF16)
        dc_ref[...] = dc_acc + _dot(dcbb, bm)
        db_ref[...] = db_acc + _dot(dcbb, cm, 0, 0)
        da = _dot_exact_left(tu_ref[...], dac) + tail
        real = lane < hpg
        ddt = jnp.where(real, (da * a_neg + xdx) * _sigmoid(dtraw + bias_ref[0]), 0.0)
        ddt_ref[...] = ddt
        dd_ref[0] += dskip
        dbias_ref[0] += jnp.sum(ddt, axis=0, keepdims=True)
        dalog_ref[0] += jnp.where(lane1 < hpg, jnp.sum(da * dt, axis=0, keepdims=True) * a_neg, 0.0)

    return pl.pallas_call(
        body, name=name, grid=(G, nc),
        in_specs=[sp["x"], sp["b"], sp["c"], sp["z"], sp["dt"], sp["gvec"], sp["gvec"], sp["chan"], sp["chan"],
                  sp["tri"], sp["tri"], sp["x"], sp["hp"], sp["x"]],
        out_specs=[sp["x"], sp["x"], sp["bc"], sp["bc"], sp["dt"], sp["gvec"], sp["gvec"], sp["gvec"], sp["chan"]],
        out_shape=[jax.ShapeDtypeStruct((S, inner), F32), jax.ShapeDtypeStruct((S, inner), F32),
                   jax.ShapeDtypeStruct((S, G * n), F32), jax.ShapeDtypeStruct((S, G * n), F32),
                   jax.ShapeDtypeStruct((S, G * LANES), F32), jax.ShapeDtypeStruct((G, 1, LANES), F32),
                   jax.ShapeDtypeStruct((G, 1, LANES), F32), jax.ShapeDtypeStruct((G, 1, LANES), F32),
                   jax.ShapeDtypeStruct((1, inner), F32)],
        scratch_shapes=[pltpu.VMEM((gw, n), F32)],
        compiler_params=_params(("arbitrary", "arbitrary")),
    )(xbc, xbc, xbc, zx, dtg, bias_g, alog_g, d_chan, ngain, _tri(L, "row_ge_col"), _tri(L, "row_le_col"),
      yfull, hp, dyn)


def _spread_dt(w_dt, G, hpg):
    K = w_dt.shape[0]
    w = w_dt.reshape(K, G, hpg)
    return jnp.pad(w, ((0, 0), (0, 0), (0, LANES - hpg))).reshape(K, G * LANES)


def _group_vec(v, G, hpg):
    return jnp.pad(v.reshape(G, 1, hpg), ((0, 0), (0, 0), (0, LANES - hpg)))


def local_step(x, target, W, late=None):
    S, D = x.shape
    depth = W["mix_norm"].shape[0]
    gm_groups, gm_chunk = W["gm_w_s"].shape[1], W["gm_w_s"].shape[2]
    heads = W["ssm_dt_bias"].shape[1]
    inner = heads * SB_HEAD_DIM
    L = gm_chunk
    received = None

    saved = []
    for i in range(depth):
        kind, j = i % 3, i // 3
        s = dict(x=x)
        h = rms_fwd(x, W["mix_norm"][i:i + 1], name=f"rms_mix_fwd")
        s["h"] = h
        if kind == 0:
            qkv = matmul(h, W["sb_w_qkv"][j], name="mm_qkv")
            if late is not None and i == 0:
                o, gathered = sb_attn_fwd(qkv, W["sb_q_gain"][j:j + 1], W["sb_k_gain"][j:j + 1], name="sb_fwd_gather",
                                          gather=late.shard)
                late.fill(W, gathered)
            else:
                o = sb_attn_fwd(qkv, W["sb_q_gain"][j:j + 1], W["sb_k_gain"][j:j + 1], name="sb_fwd")
            x1 = matmul(o, W["sb_w_o"][j], residual=x, name="mm_sb_out")
            s.update(qkv=qkv, o=o)
        elif kind == 1:
            wc = jnp.where(jnp.tril(jnp.ones((gm_chunk, gm_chunk), bool)), W["gm_w_s"][j], 0.0).astype(BF16)
            bsf = jnp.broadcast_to(W["gm_b_s"][j][:, :, None], (gm_groups, gm_chunk, W["gm_v_gain"].shape[1] // gm_groups)).astype(F32)
            zzpre = matmul(h, W["gm_w_in"][j], name="mm_gm_in")
            p = gmlp_fwd(zzpre, W["gm_b_in"][j:j + 1], W["gm_v_gain"][j:j + 1], wc, bsf, name="gm_fwd")
            x1 = matmul(p, W["gm_w_out"][j], residual=x, name="mm_gm_out")
            s.update(zzpre=zzpre, p=p, wc=wc, bsf=bsf)
        else:
            conv_dim = W["ssm_conv_w"].shape[2]
            G = (conv_dim - inner) // (2 * SSM_STATE)
            hpg = heads // G
            w_in = W["ssm_w_in"][j]
            w_zx = w_in[:, :inner + conv_dim]
            w_dtg = _spread_dt(w_in[:, inner + conv_dim:], G, hpg)
            bias_g = _group_vec(W["ssm_dt_bias"][j], G, hpg)
            alog_g = _group_vec(W["ssm_a_log"][j], G, hpg)
            d_chan = jnp.repeat(W["ssm_d"][j], SB_HEAD_DIM)[None, :]
            ngain = W["ssm_norm_gain"][j:j + 1]
            zx = matmul(h, w_zx, name="mm_ssm_zx")
            dtg = matmul(h, w_dtg, name="mm_ssm_dt")
            xbc = conv_fwd(zx, W["ssm_conv_w"][j], W["ssm_conv_b"][j:j + 1], inner, name="conv_fwd")
            yn, yfull, hp = ssd_fwd(xbc, zx, dtg, bias_g, alog_g, d_chan, ngain, L, G, name="ssd_fwd")
            x1 = matmul(yn, W["ssm_w_out"][j], residual=x, name="mm_ssm_out")
            s.update(w_zx=w_zx, w_dtg=w_dtg, bias_g=bias_g, alog_g=alog_g, d_chan=d_chan, ngain=ngain,
                     zx=zx, dtg=dtg, xbc=xbc, yn=yn, yfull=yfull, hp=hp)
        h2 = rms_fwd(x1, W["ffn_norm"][i:i + 1], name="rms_ffn_fwd")
        gu, a = ffn_up_fwd(h2, W["ffn_w_gu"][i], name="ffn_up_fwd")
        x2 = matmul(a, W["ffn_w_down"][i], residual=x1, name="mm_ffn_down")
        s.update(x1=x1, h2=h2, gu=gu, a=a)
        saved.append(s)
        x = x2

    dx, loss = loss_head(x, target, name="loss_head")

    gw = {k: {} for k in WEIGHTS}
    for i in reversed(range(depth)):
        kind, j = i % 3, i // 3
        s = saved[i]
        gw["ffn_w_down"][i] = matmul(s["a"], dx, ta=True, out_dtype=BF16, name="mm_ffn_dwdown")
        dgu = ffn_up_bwd(dx, W["ffn_w_down"][i], s["gu"], name="ffn_up_bwd")
        dh2 = matmul(dgu, W["ffn_w_gu"][i], tb=True, a_split=2, name="mm_ffn_dh")
        gw["ffn_w_gu"][i] = matmul(s["h2"], dgu, ta=True, b_split=2, out_dtype=BF16, name="mm_ffn_dwgu")
        dx1, dgn = rms_bwd(s["x1"], W["ffn_norm"][i:i + 1], dh2, dx, name="rms_ffn_bwd")
        gw["ffn_norm"][i] = dgn[0]
        if kind == 0:
            do = matmul(dx1, W["sb_w_o"][j], tb=True, name="mm_sb_do")
            gw["sb_w_o"][j] = matmul(s["o"], dx1, ta=True, out_dtype=BF16, name="mm_sb_dwo")
            if late is not None and i == 0:
                dqkv, dqg, dkg, received = sb_attn_bwd(
                    s["qkv"], s["o"], do, W["sb_q_gain"][j:j + 1], W["sb_k_gain"][j:j + 1], name="sb_bwd_scatter",
                    scatter=late.contributions(gw))
            else:
                dqkv, dqg, dkg = sb_attn_bwd(s["qkv"], s["o"], do, W["sb_q_gain"][j:j + 1], W["sb_k_gain"][j:j + 1],
                                             name="sb_bwd")
            gw["sb_q_gain"][j] = dqg[0]
            gw["sb_k_gain"][j] = dkg[0]
            dh = matmul(dqkv, W["sb_w_qkv"][j], tb=True, a_split=3, name="mm_sb_dh")
            gw["sb_w_qkv"][j] = matmul(s["h"], dqkv, ta=True, b_split=3, out_dtype=BF16, name="mm_sb_dwqkv")
        elif kind == 1:
            dp = matmul(dx1, W["gm_w_out"][j], tb=True, name="mm_gm_dp")
            gw["gm_w_out"][j] = matmul(s["p"], dx1, ta=True, out_dtype=BF16, name="mm_gm_dwout")
            dzz, db_in, dvg, dws, dbs = gmlp_bwd(s["zzpre"], W["gm_b_in"][j:j + 1], W["gm_v_gain"][j:j + 1],
                                                s["wc"], s["bsf"], dp, name="gm_bwd")
            gw["gm_b_in"][j] = db_in[0]
            gw["gm_v_gain"][j] = dvg[0]
            gw["gm_w_s"][j] = dws
            gw["gm_b_s"][j] = dbs[:, :gm_groups].T
            dh = matmul(dzz, W["gm_w_in"][j], tb=True, name="mm_gm_dh")
            gw["gm_w_in"][j] = matmul(s["h"], dzz, ta=True, out_dtype=BF16, name="mm_gm_dwin")
        else:
            conv_dim = W["ssm_conv_w"].shape[2]
            G = (conv_dim - inner) // (2 * SSM_STATE)
            hpg = heads // G
            dyn = matmul(dx1, W["ssm_w_out"][j], tb=True, name="mm_ssm_dyn")
            gw["ssm_w_out"][j] = matmul(s["yn"], dx1, ta=True, out_dtype=BF16, name="mm_ssm_dwout")
            dz, dxs, dbm, dcm, ddt, dbias, dalog, dd, dng = ssd_bwd(
                s["xbc"], s["zx"], s["dtg"], s["bias_g"], s["alog_g"], s["d_chan"], s["ngain"], s["yfull"], s["hp"],
                dyn, L, G, name="ssd_bwd")
            dxbc = jnp.concatenate([dxs, dbm, dcm], axis=1)
            dpre, dcw, dcb = conv_bwd(s["zx"], W["ssm_conv_w"][j], W["ssm_conv_b"][j:j + 1], inner, dxbc,
                                      name="conv_bwd")
            dzx = jnp.concatenate([dz, dpre], axis=1)
            dh = matmul(ddt, s["w_dtg"], tb=True, name="mm_ssm_dh_dt")
            dh = matmul(dzx, s["w_zx"], tb=True, residual=dh, name="mm_ssm_dh")
            dw_zx = matmul(s["h"], dzx, ta=True, out_dtype=BF16, name="mm_ssm_dwzx")
            dw_dtg = matmul(s["h"], ddt, ta=True, out_dtype=BF16, name="mm_ssm_dwdt")
            dw_dt = dw_dtg.reshape(D, G, LANES)[:, :, :hpg].reshape(D, heads)
            gw["ssm_w_in"][j] = jnp.concatenate([dw_zx, dw_dt], axis=1)
            gw["ssm_conv_w"][j] = dcw
            gw["ssm_conv_b"][j] = dcb[0]
            gw["ssm_dt_bias"][j] = dbias[:, 0, :hpg].reshape(heads)
            gw["ssm_a_log"][j] = dalog[:, 0, :hpg].reshape(heads)
            gw["ssm_d"][j] = dd[:, 0, :hpg].reshape(heads)
            gw["ssm_norm_gain"][j] = dng[0]
        dx, dgn = rms_bwd(s["x"], W["mix_norm"][i:i + 1], dh, dx1, name="rms_mix_bwd")
        gw["mix_norm"][i] = dgn[0]

    return loss, dx, gw, received


MESH = pl.DeviceIdType.MESH
HBM_SPEC = pl.BlockSpec(memory_space=pltpu.HBM)
VMEM_SPEC = pl.BlockSpec(memory_space=pltpu.VMEM)


def _my_position():
    return lax.axis_index("x"), lax.axis_index("y"), lax.axis_index("c")


def _flip(v, bit):
    return 1 - v if bit else v


def all_gather_packed(shard, *, name):
    R, C = shard.shape

    def body(x_ref, out_ref, send_sems, recv_sems, local_sem):
        refs = (x_ref, out_ref, send_sems, recv_sems, local_sem)
        _ag_phase(refs, "start")
        _ag_phase(refs, "forward")
        _ag_phase(refs, "finish")

    return pl.pallas_call(
        body, name=name, out_shape=jax.ShapeDtypeStruct((N_DEV, R, C), shard.dtype),
        in_specs=[HBM_SPEC], out_specs=HBM_SPEC, scratch_shapes=COMM_SEMAPHORES,
    )(shard)


COMM_SEMAPHORES = [pltpu.SemaphoreType.DMA((7,)), pltpu.SemaphoreType.DMA((7,)), pltpu.SemaphoreType.DMA]


def _ag_phase(refs, phase):
    x_ref, out_ref, send_sems, recv_sems, local_sem = refs
    x, y, c = _my_position()
    me, sibling = (x, y, c), (x, y, 1 - c)
    chips = [(1 - x, y), (x, 1 - y), (1 - x, 1 - y)]

    def slot(px, py, pc):
        return out_ref.at[4 * px + 2 * py + pc]

    def copy(k, block, to, src=None):
        return pltpu.make_async_remote_copy(
            src_ref=slot(*block) if src is None else src, dst_ref=slot(*block),
            send_sem=send_sems.at[k], recv_sem=recv_sems.at[k], device_id=to, device_id_type=MESH)

    mine = pltpu.make_async_copy(x_ref, slot(*me), local_sem)
    first = [copy(0, me, sibling, src=x_ref)]
    first += [copy(1 + j, me, (*chip, c), src=x_ref) for j, chip in enumerate(chips)]
    passed = [copy(4 + j, (*chip, c), sibling) for j, chip in enumerate(chips)]
    if phase == "start":
        mine.start()
        for cp in first:
            cp.start()
    elif phase == "forward":
        for j, chip in enumerate(chips):
            copy(1 + j, (*chip, c), me).wait_recv()
            passed[j].start()
    else:
        copy(0, sibling, me).wait_recv()
        for j, chip in enumerate(chips):
            copy(4 + j, (*chip, 1 - c), me).wait_recv()
        for cp in first + passed:
            cp.wait_send()
        mine.wait()


def _rs_semaphores(n):
    return [pltpu.SemaphoreType.DMA((7 * n,)), pltpu.SemaphoreType.DMA((7 * n,)), pltpu.SemaphoreType.DMA((n,))]


def _rs_phase(g_refs, out_refs, sems, phase):
    send_sems, recv_sems, local_sems = sems
    x, y, c = _my_position()
    me = 4 * x + 2 * y + c
    copies = []
    for p, (g_ref, out_ref) in enumerate(zip(g_refs, out_refs)):
        copies.append(pltpu.make_async_copy(g_ref.at[me], out_ref.at[me], local_sems.at[p]))
        for k in range(1, N_DEV):
            px, py, pc = _flip(x, k & 4), _flip(y, k & 2), _flip(c, k & 1)
            copies.append(pltpu.make_async_remote_copy(
                src_ref=g_ref.at[4 * px + 2 * py + pc], dst_ref=out_ref.at[me],
                send_sem=send_sems.at[7 * p + k - 1], recv_sem=recv_sems.at[7 * p + k - 1],
                device_id=(px, py, pc), device_id_type=MESH))
    for cp in copies:
        if phase == "start":
            cp.start()
        else:
            cp.wait()


def exchange_for_reduce_scatter(gs, *, name):
    n = len(gs)

    def body(*refs):
        for phase in ("start", "finish"):
            _rs_phase(refs[:n], refs[n:2 * n], refs[2 * n:], phase)

    return pl.pallas_call(
        body, name=name, out_shape=[jax.ShapeDtypeStruct(g.shape, g.dtype) for g in gs],
        in_specs=[HBM_SPEC] * n, out_specs=[HBM_SPEC] * n, scratch_shapes=_rs_semaphores(n),
    )(*gs)


def sum_slots(recv, *, name):
    n, R, C = recv.shape
    tr = _pick(R, (512, 256, 128))

    def body(r_ref, o_ref):
        acc = r_ref[0].astype(F32)
        for s in range(1, n):
            acc = acc + r_ref[s].astype(F32)
        o_ref[...] = acc

    return pl.pallas_call(
        body, name=name, grid=(R // tr,), in_specs=[pl.BlockSpec((n, tr, C), lambda i: (0, i, 0))],
        out_specs=pl.BlockSpec((tr, C), lambda i: (i, 0)), out_shape=jax.ShapeDtypeStruct((R, C), F32),
        compiler_params=_params(("parallel",)),
    )(recv)


def all_reduce_small(v, *, name):
    R, C = v.shape

    def body(v_ref, o_ref, buf, send_sems, recv_sems):
        x, y, c = _my_position()
        me = 4 * x + 2 * y + c
        buf[me] = v_ref[...]
        copies = []
        for k in range(1, N_DEV):
            px, py, pc = _flip(x, k & 4), _flip(y, k & 2), _flip(c, k & 1)
            copies.append(pltpu.make_async_remote_copy(
                src_ref=v_ref, dst_ref=buf.at[me], send_sem=send_sems.at[k - 1], recv_sem=recv_sems.at[k - 1],
                device_id=(px, py, pc), device_id_type=MESH))
        for cp in copies:
            cp.start()
        for cp in copies:
            cp.wait()
        acc = buf[0]
        for s in range(1, N_DEV):
            acc = acc + buf[s]
        o_ref[...] = acc

    return pl.pallas_call(
        body, name=name, out_shape=jax.ShapeDtypeStruct((R, C), F32), in_specs=[VMEM_SPEC], out_specs=VMEM_SPEC,
        scratch_shapes=[pltpu.VMEM((N_DEV, R, C), F32), pltpu.SemaphoreType.DMA((7,)), pltpu.SemaphoreType.DMA((7,))],
        compiler_params=pltpu.CompilerParams(vmem_limit_bytes=VMEM_LIMIT_BYTES),
    )(v)


def _pad_rows(a, mult):
    pad = (-a.shape[0]) % mult
    return jnp.pad(a, ((0, pad), (0, 0))) if pad else a


def _pack_small(arrays):
    flat = []
    for a in arrays:
        f = a.reshape(-1).astype(F32)
        flat.append(jnp.pad(f, (0, (-f.shape[0]) % LANES)))
    return _pad_rows(jnp.concatenate(flat).reshape(-1, LANES), 8)


def _unpack_small(packed, shapes):
    flat = packed.reshape(-1)
    out, r = [], 0
    for shp in shapes:
        n = math.prod(shp)
        out.append(flat[r:r + n].reshape(shp))
        r += n + (-n) % LANES
    return out


ARG_NAMES = ("x",) + WEIGHTS + ("loss_target",) + tuple("m_" + w for w in WEIGHTS) + tuple("v_" + w for w in WEIGHTS)


def kernel(x, mix_norm, ffn_norm, sb_w_qkv, sb_q_gain, sb_k_gain, sb_w_o, gm_w_in, gm_b_in, gm_v_gain, gm_w_s, gm_b_s, gm_w_out, ssm_w_in, ssm_conv_w, ssm_conv_b, ssm_dt_bias, ssm_a_log, ssm_d, ssm_norm_gain, ssm_w_out, ffn_w_gu, ffn_w_down, loss_target, m_mix_norm, m_ffn_norm, m_sb_w_qkv, m_sb_q_gain, m_sb_k_gain, m_sb_w_o, m_gm_w_in, m_gm_b_in, m_gm_v_gain, m_gm_w_s, m_gm_b_s, m_gm_w_out, m_ssm_w_in, m_ssm_conv_w, m_ssm_conv_b, m_ssm_dt_bias, m_ssm_a_log, m_ssm_d, m_ssm_norm_gain, m_ssm_w_out, m_ffn_w_gu, m_ffn_w_down, v_mix_norm, v_ffn_norm, v_sb_w_qkv, v_sb_q_gain, v_sb_k_gain, v_sb_w_o, v_gm_w_in, v_gm_b_in, v_gm_v_gain, v_gm_w_s, v_gm_b_s, v_gm_w_out, v_ssm_w_in, v_ssm_conv_w, v_ssm_conv_b, v_ssm_dt_bias, v_ssm_a_log, v_ssm_d, v_ssm_norm_gain, v_ssm_w_out, v_ffn_w_gu, v_ffn_w_down):
    given = dict(zip(ARG_NAMES, (x, mix_norm, ffn_norm, sb_w_qkv, sb_q_gain, sb_k_gain, sb_w_o, gm_w_in, gm_b_in, gm_v_gain, gm_w_s, gm_b_s, gm_w_out, ssm_w_in, ssm_conv_w, ssm_conv_b, ssm_dt_bias, ssm_a_log, ssm_d, ssm_norm_gain, ssm_w_out, ffn_w_gu, ffn_w_down, loss_target, m_mix_norm, m_ffn_norm, m_sb_w_qkv, m_sb_q_gain, m_sb_k_gain, m_sb_w_o, m_gm_w_in, m_gm_b_in, m_gm_v_gain, m_gm_w_s, m_gm_b_s, m_gm_w_out, m_ssm_w_in, m_ssm_conv_w, m_ssm_conv_b, m_ssm_dt_bias, m_ssm_a_log, m_ssm_d, m_ssm_norm_gain, m_ssm_w_out, m_ffn_w_gu, m_ffn_w_down, v_mix_norm, v_ffn_norm, v_sb_w_qkv, v_sb_q_gain, v_sb_k_gain, v_sb_w_o, v_gm_w_in, v_gm_b_in, v_gm_v_gain, v_gm_w_s, v_gm_b_s, v_gm_w_out, v_ssm_w_in, v_ssm_conv_w, v_ssm_conv_b, v_ssm_dt_bias, v_ssm_a_log, v_ssm_d, v_ssm_norm_gain, v_ssm_w_out, v_ffn_w_gu, v_ffn_w_down)))
    mx, my, mc = _my_position()
    me = 4 * mx + 2 * my + mc

    pieces = [(k, l) for k in BIG for l in range(given[k].shape[0])]
    early = [("sb_w_qkv", 0)]
    late_pieces = [p for p in pieces if p not in early]
    last = [("sb_w_qkv", 0)]
    main = [p for p in pieces if p not in last]
    row_mult = 256

    def rows_of(p):
        return math.prod(given[p[0]].shape[1:]) // PACK_COLS

    def pack_shards(ps, prefix="", dtype=F32, extra=()):
        parts = [given[prefix + k][l].astype(dtype).reshape(-1, PACK_COLS) for k, l in ps] + list(extra)
        return _pad_rows(jnp.concatenate(parts, axis=0), row_mult)

    def split_rows(packed, ps):
        out, r = [], 0
        for p in ps:
            out.append(packed[..., r:r + rows_of(p), :])
            r += rows_of(p)
        return out

    def piece_to_full(g, k):
        shp = given[k].shape[1:]
        g = g.reshape((N_DEV,) + shp)
        if k in COL_SHARDED:
            return jnp.moveaxis(g, 0, 1).reshape(shp[0], N_DEV * shp[1])
        return g.reshape(N_DEV * shp[0], shp[1])

    def full_to_piece(full, k):
        shp = given[k].shape[1:]
        g = jnp.moveaxis(full.reshape(shp[0], N_DEV, shp[1]), 1, 0) if k in COL_SHARDED else full
        return g.reshape(N_DEV, -1, PACK_COLS)

    def contributions(gw, ps):
        return [full_to_piece(gw[k][l], k) for k, l in ps]

    sharded_small = [lax.bitcast_convert_type(given[k], BF16) for k in SMALL_SHARDED]
    tail = jnp.concatenate([a.reshape(-1) for a in sharded_small])
    tail = jnp.pad(tail, (0, (-tail.size) % PACK_COLS)).reshape(-1, PACK_COLS)

    W = {k: given[k] for k in SMALL if k not in SMALL_SHARDED}
    W.update({k: [None] * given[k].shape[0] for k in BIG})
    gathered_early = all_gather_packed(pack_shards(early, dtype=BF16), name="all_gather_early")
    for (k, l), g in zip(early, split_rows(gathered_early, early)):
        W[k][l] = piece_to_full(g, k)

    class Late:
        shard = pack_shards(late_pieces, dtype=BF16, extra=[tail])

        @staticmethod
        def fill(weights, gathered):
            for (k, l), g in zip(late_pieces, split_rows(gathered, late_pieces)):
                weights[k][l] = piece_to_full(g, k)
            r0 = sum(rows_of(p) for p in late_pieces)
            tail_g = gathered[:, r0:r0 + tail.shape[0], :].reshape(N_DEV, -1)
            off = 0
            for k, a in zip(SMALL_SHARDED, sharded_small):
                g = lax.bitcast_convert_type(tail_g[:, off:off + a.size].reshape((N_DEV,) + a.shape), F32)
                weights[k] = jnp.moveaxis(g, 0, -2).reshape(g.shape[1:-1] + (N_DEV * g.shape[-1],))
                off += a.size

        @staticmethod
        def contributions(gw):
            return contributions(gw, main)

    loss, gx, gw, received_main = local_step(given["x"][0], given["loss_target"][0], W, late=Late)
    received_last = exchange_for_reduce_scatter(contributions(gw, last), name="reduce_scatter_last")

    grads_small = {k: jnp.stack([gw[k][l] for l in sorted(gw[k])], axis=0) for k in SMALL}
    small_shapes = [grads_small[k].shape for k in SMALL] + [(1, 1)]
    reduced = all_reduce_small(_pack_small([grads_small[k] for k in SMALL] + [loss]), name="all_reduce_small")
    small_full = dict(zip(SMALL + ("loss",), _unpack_small(reduced, small_shapes)))

    g_piece = {}
    for grp, received in ((main, received_main), (last, received_last)):
        for p, r in zip(grp, received):
            g_piece[p] = sum_slots(r, name="reduce_scatter_sum").reshape(given[p[0]].shape[1:])
    out_g, out_d, out_m, out_v = {}, {}, {}, {}
    for k in BIG:
        g = jnp.stack([g_piece[(k, l)] for l in range(given[k].shape[0])], axis=0)
        out_g[k] = g
        out_d[k], out_m[k], out_v[k] = adamw(given[k], g, given["m_" + k], given["v_" + k], name="adamw_" + k)

    gsmall = {}
    for k in SMALL:
        g = small_full[k]
        if k in SMALL_SHARDED:
            n = given[k].shape[-1]
            g = lax.dynamic_slice_in_dim(g, me * n, n, axis=g.ndim - 1)
        gsmall[k] = g
    local_shapes = [given[k].shape for k in SMALL]
    dsm, nmsm, nvsm = adamw(_pack_small([given[k] for k in SMALL]), _pack_small([gsmall[k] for k in SMALL]),
                            _pack_small([given["m_" + k] for k in SMALL]), _pack_small([given["v_" + k] for k in SMALL]),
                            name="adamw_small")
    out_g.update(gsmall)
    for dst, src in ((out_d, dsm), (out_m, nmsm), (out_v, nvsm)):
        dst.update(zip(SMALL, _unpack_small(src, local_shapes)))

    return (small_full["loss"].reshape(()), gx[None],
            *[out_g[k] for k in WEIGHTS], *[out_d[k] for k in WEIGHTS],
            *[out_m[k] for k in WEIGHTS], *[out_v[k] for k in WEIGHTS])
```

```python
import math

import jax
import jax.numpy as jnp
from jax import lax
from jax.experimental import pallas as pl
from jax.experimental.pallas import tpu as pltpu

F32 = jnp.float32
BF16 = jnp.bfloat16
EPS = 1e-6
N_DEV = 8
SB_HEAD_DIM = 64
SSM_STATE = 128
SSM_CONV = 4
ADAM_LR = 0.001
ADAM_B1 = 0.9
ADAM_B2 = 0.999
ADAM_EPS = 1e-08
ADAM_WD = 0.01
ADAM_STEP = 10
VMEM_LIMIT_BYTES = 56 * 1024 * 1024
MATMUL_VMEM_BUDGET = 40 * 1024 * 1024
LANES = 128
PACK_COLS = 1024

BIG = ("sb_w_qkv", "sb_w_o", "gm_w_in", "gm_w_out", "ssm_w_in", "ssm_w_out", "ffn_w_gu", "ffn_w_down")
COL_SHARDED = ("sb_w_qkv", "gm_w_in", "ssm_w_in", "ffn_w_gu")
SMALL = ("mix_norm", "ffn_norm", "sb_q_gain", "sb_k_gain", "gm_b_in", "gm_v_gain", "gm_w_s", "gm_b_s",
         "ssm_conv_w", "ssm_conv_b", "ssm_dt_bias", "ssm_a_log", "ssm_d", "ssm_norm_gain")
SMALL_SHARDED = ("ssm_conv_w", "ssm_conv_b", "ssm_norm_gain")
WEIGHTS = ("mix_norm", "ffn_norm", "sb_w_qkv", "sb_q_gain", "sb_k_gain", "sb_w_o", "gm_w_in", "gm_b_in",
           "gm_v_gain", "gm_w_s", "gm_b_s", "gm_w_out", "ssm_w_in", "ssm_conv_w", "ssm_conv_b", "ssm_dt_bias",
           "ssm_a_log", "ssm_d", "ssm_norm_gain", "ssm_w_out", "ffn_w_gu", "ffn_w_down")


def _params(semantics=None):
    return pltpu.CompilerParams(dimension_semantics=semantics, vmem_limit_bytes=VMEM_LIMIT_BYTES)


def _pick(n, prefs):
    for t in prefs:
        if t <= n and n % t == 0:
            return t
    return n


def _dot(a, b, ca=1, cb=0):
    return lax.dot_general(a, b, (((ca,), (cb,)), ((), ())), preferred_element_type=F32)


def _split3(v):
    h1 = v.astype(BF16)
    r1 = v - h1.astype(F32)
    h2 = r1.astype(BF16)
    h3 = (r1 - h2.astype(F32)).astype(BF16)
    return h1, h2, h3


def _dot_exact_left(mat01, v):
    h1, h2, h3 = _split3(v)
    return _dot(mat01, h1) + _dot(mat01, h2) + _dot(mat01, h3)


def _dot_split2_right(v, mat01):
    hi = v.astype(BF16)
    lo = (v - hi.astype(F32)).astype(BF16)
    return _dot(hi, mat01) + _dot(lo, mat01)


def _sigmoid(v):
    return 1.0 / (1.0 + jnp.exp(-v))


def _softplus(v):
    return jnp.maximum(v, 0.0) + jnp.log(1.0 + jnp.exp(-jnp.abs(v)))


def _erf(v):
    a = jnp.abs(v)
    t = 1.0 / (1.0 + 0.3275911 * a)
    poly = t * (0.254829592 + t * (-0.284496736 + t * (1.421413741 + t * (-1.453152027 + t * 1.061405429))))
    e = 1.0 - poly * jnp.exp(-a * a)
    return jnp.where(v < 0, -e, e)


def _gelu_and_grad(v):
    cdf = 0.5 * (1.0 + _erf(v * (1.0 / math.sqrt(2.0))))
    pdf = jnp.exp(-0.5 * v * v) * (1.0 / math.sqrt(2.0 * math.pi))
    return v * cdf, cdf + v * pdf


def matmul(a, b, *, ta=False, tb=False, out_dtype=F32, residual=None, a_split=1, b_split=1, name):
    if a_split > 1 and ta:
        assert a.shape[0] == a_split
        K, M = a.shape[1], a_split * a.shape[2]
    elif a_split > 1:
        assert a.shape[0] == a_split
        M, K = a.shape[1], a_split * a.shape[2]
    elif ta:
        K, M = a.shape
    else:
        M, K = a.shape
    if b_split > 1:
        assert not tb and b.shape[0] == b_split
        Kb, N = b.shape[1], b_split * b.shape[2]
    elif tb:
        N, Kb = b.shape
    else:
        Kb, N = b.shape
    assert K == Kb, (a.shape, b.shape, ta, tb)
    has_res = residual is not None
    tm = _pick(M // a_split if ta else M, (1024, 1408, 768, 512, 256, 128))
    tn = _pick(N // b_split, (1024, 1408, 1536, 768, 512, 256, 128))

    def vmem_bytes(tk):
        tiles = tm * tk * a.dtype.itemsize + tk * tn * b.dtype.itemsize
        outs = tm * tn * jnp.dtype(out_dtype).itemsize + (tm * tn * 4 if has_res else 0)
        return 2 * tiles + 2 * outs + (tm * tn * 4 if tk < K else 0)

    kp = K if ta else K // a_split
    tk = next((t for t in (K, 2048, 1408, 1024, 512, 256) if t <= kp and kp % t == 0 and vmem_bytes(t) <= MATMUL_VMEM_BUDGET),
              _pick(kp, (128,)))
    nk = K // tk
    if a_split > 1 and ta:
        nib = M // a_split // tm
        a_spec = pl.BlockSpec((None, tk, tm), lambda i, j, k: (i // nib, k, i % nib))
    elif a_split > 1:
        nkb = kp // tk
        a_spec = pl.BlockSpec((None, tm, tk), lambda i, j, k: (k // nkb, i, k % nkb))
    else:
        a_spec = pl.BlockSpec((tk, tm), lambda i, j, k: (k, i)) if ta else pl.BlockSpec((tm, tk), lambda i, j, k: (i, k))
    if b_split > 1:
        njb = N // b_split // tn
        b_spec = pl.BlockSpec((None, tk, tn), lambda i, j, k: (j // njb, k, j % njb))
    else:
        b_spec = pl.BlockSpec((tn, tk), lambda i, j, k: (j, k)) if tb else pl.BlockSpec((tk, tn), lambda i, j, k: (k, j))
    o_spec = pl.BlockSpec((tm, tn), lambda i, j, k: (i, j))
    ca, cb = (0 if ta else 1), (1 if tb else 0)

    def body(*refs):
        a_ref, b_ref = refs[:2]
        r_ref = refs[2] if has_res else None
        o_ref = refs[3] if has_res else refs[2]

        def finish(r):
            if has_res:
                r = r + r_ref[...]
            o_ref[...] = r.astype(out_dtype)

        def part():
            return _dot(a_ref[...].astype(BF16), b_ref[...].astype(BF16), ca, cb)

        if nk == 1:
            finish(part())
            return
        acc = refs[-1]
        k = pl.program_id(2)

        @pl.when(k == 0)
        def _():
            acc[...] = part()

        @pl.when(jnp.logical_and(k > 0, k < nk - 1))
        def _():
            acc[...] += part()

        @pl.when(k == nk - 1)
        def _():
            finish(acc[...] + part())

    in_specs = [a_spec, b_spec] + ([o_spec] if has_res else [])
    args = (a, b) + ((residual,) if has_res else ())
    return pl.pallas_call(
        body, name=name, grid=(M // tm, N // tn, nk), in_specs=in_specs, out_specs=o_spec,
        out_shape=jax.ShapeDtypeStruct((M, N), out_dtype),
        scratch_shapes=[pltpu.VMEM((tm, tn), F32)] if nk > 1 else [],
        compiler_params=_params(("parallel", "parallel", "arbitrary")),
    )(*args)


def rms_fwd(x, gain, *, name):
    S, D = x.shape
    tr = _pick(S, (512, 256, 128))

    def body(x_ref, g_ref, o_ref):
        xv = x_ref[...]
        r = lax.rsqrt(jnp.mean(xv * xv, axis=1, keepdims=True) + EPS)
        o_ref[...] = (xv * r * g_ref[...]).astype(BF16)

    return pl.pallas_call(
        body, name=name, grid=(S // tr,),
        in_specs=[pl.BlockSpec((tr, D), lambda i: (i, 0)), pl.BlockSpec((1, D), lambda i: (0, 0))],
        out_specs=pl.BlockSpec((tr, D), lambda i: (i, 0)), out_shape=jax.ShapeDtypeStruct((S, D), BF16),
        compiler_params=_params(("parallel",)),
    )(x, gain)


def rms_bwd(x, gain, dh, dres, *, name):
    S, D = x.shape
    tr = _pick(S, (512, 256, 128))

    def body(x_ref, g_ref, dh_ref, dr_ref, dx_ref, dg_ref):
        @pl.when(pl.program_id(0) == 0)
        def _():
            dg_ref[...] = jnp.zeros_like(dg_ref)

        xv = x_ref[...]
        dhv = dh_ref[...]
        r = lax.rsqrt(jnp.mean(xv * xv, axis=1, keepdims=True) + EPS)
        xhat = xv * r
        t = dhv * g_ref[...]
        dx_ref[...] = dr_ref[...] + r * (t - xhat * jnp.mean(xhat * t, axis=1, keepdims=True))
        dg_ref[...] += jnp.sum(dhv * xhat, axis=0, keepdims=True)

    row = pl.BlockSpec((tr, D), lambda i: (i, 0))
    vec = pl.BlockSpec((1, D), lambda i: (0, 0))
    return pl.pallas_call(
        body, name=name, grid=(S // tr,), in_specs=[row, vec, row, row], out_specs=[row, vec],
        out_shape=[jax.ShapeDtypeStruct((S, D), F32), jax.ShapeDtypeStruct((1, D), F32)],
        compiler_params=_params(("arbitrary",)),
    )(x, gain, dh, dres)


def ffn_up_fwd(h, w_gu_t, *, name):
    S, K = h.shape
    F = w_gu_t.shape[0] // 2
    tm = _pick(S, (512, 256, 128))
    tn = _pick(F, (1408, 1024, 768, 512, 256, 128))
    nj = F // tn

    def body(h_ref, wg_ref, wu_ref, gu_ref, a_ref):
        hv = h_ref[...]
        g = _dot(hv, wg_ref[...], 1, 1)
        u = _dot(hv, wu_ref[...], 1, 1)
        gu_ref[0] = g
        gu_ref[1] = u
        a_ref[...] = (g * _sigmoid(g) * u).astype(BF16)

    return pl.pallas_call(
        body, name=name, grid=(nj, S // tm),
        in_specs=[pl.BlockSpec((tm, K), lambda j, i: (i, 0)), pl.BlockSpec((tn, K), lambda j, i: (j, 0)),
                  pl.BlockSpec((tn, K), lambda j, i: (nj + j, 0))],
        out_specs=[pl.BlockSpec((2, tm, tn), lambda j, i: (0, i, j)), pl.BlockSpec((tm, tn), lambda j, i: (i, j))],
        out_shape=[jax.ShapeDtypeStruct((2, S, F), F32), jax.ShapeDtypeStruct((S, F), BF16)],
        compiler_params=_params(("parallel", "parallel")),
    )(h, w_gu_t, w_gu_t)


def ffn_up_bwd(dy, w_down, gu, *, name):
    S, D = dy.shape
    F = w_down.shape[0]
    tm = _pick(S, (512, 256, 128))
    tn = _pick(F, (1408, 1024, 768, 512, 256, 128))

    def body(dy_ref, wd_ref, gu_ref, o_ref):
        da = _dot(dy_ref[...].astype(BF16), wd_ref[...], 1, 1)
        g = gu_ref[0]
        u = gu_ref[1]
        s = _sigmoid(g)
        o_ref[0] = (da * u * (s * (1.0 + g * (1.0 - s)))).astype(BF16)
        o_ref[1] = (da * g * s).astype(BF16)

    pair = pl.BlockSpec((2, tm, tn), lambda j, i: (0, i, j))
    return pl.pallas_call(
        body, name=name, grid=(F // tn, S // tm),
        in_specs=[pl.BlockSpec((tm, D), lambda j, i: (i, 0)), pl.BlockSpec((tn, D), lambda j, i: (j, 0)), pair],
        out_specs=pair, out_shape=jax.ShapeDtypeStruct((2, S, F), BF16),
        compiler_params=_params(("parallel", "parallel")),
    )(dy, w_down, gu)


def loss_head(y, target, *, name):
    S, D = y.shape
    tr = _pick(S, (512, 256, 128))

    def body(y_ref, t_ref, dy_ref, l_ref):
        @pl.when(pl.program_id(0) == 0)
        def _():
            l_ref[...] = jnp.zeros_like(l_ref)

        err = y_ref[...] - t_ref[...]
        dy_ref[...] = err * (1.0 / D)
        l_ref[...] += jnp.sum(0.5 * jnp.mean(err * err, axis=1, keepdims=True), axis=0, keepdims=True)

    row = pl.BlockSpec((tr, D), lambda i: (i, 0))
    one = pl.BlockSpec((1, 1), lambda i: (0, 0))
    dy, l = pl.pallas_call(
        body, name=name, grid=(S // tr,), in_specs=[row, row], out_specs=[row, one],
        out_shape=[jax.ShapeDtypeStruct((S, D), F32), jax.ShapeDtypeStruct((1, 1), F32)],
        compiler_params=_params(("arbitrary",)),
    )(y, target)
    return dy, l


def adamw(w, g, m, v, *, name):
    R, C = w.shape[-2:]
    tr = _pick(R, (512, 256, 128, 64, 32, 16, 8))

    def body(w_ref, g_ref, m_ref, v_ref, d_ref, mo_ref, vo_ref):
        gv = g_ref[...]
        mn = ADAM_B1 * m_ref[...] + (1.0 - ADAM_B1) * gv
        vn = ADAM_B2 * v_ref[...] + (1.0 - ADAM_B2) * jnp.square(gv)
        m_hat = mn / (1.0 - ADAM_B1 ** ADAM_STEP)
        v_hat = vn / (1.0 - ADAM_B2 ** ADAM_STEP)
        d_ref[...] = -ADAM_LR * (m_hat / (jnp.sqrt(v_hat) + ADAM_EPS) + ADAM_WD * w_ref[...])
        mo_ref[...] = mn
        vo_ref[...] = vn

    if w.ndim == 3:
        grid = (w.shape[0], R // tr)
        blk = pl.BlockSpec((None, tr, C), lambda l, i: (l, i, 0))
    else:
        grid = (R // tr,)
        blk = pl.BlockSpec((tr, C), lambda i: (i, 0))
    sds = jax.ShapeDtypeStruct(w.shape, F32)
    return pl.pallas_call(
        body, name=name, grid=grid, in_specs=[blk] * 4, out_specs=[blk] * 3, out_shape=[sds] * 3,
        compiler_params=_params(("parallel",) * len(grid)),
    )(w, g, m, v)


def _tri(n, kind):
    r = lax.broadcasted_iota(jnp.int32, (n, n), 0)
    c = lax.broadcasted_iota(jnp.int32, (n, n), 1)
    if kind == "row_gt_col":
        return (r > c).astype(BF16)
    if kind == "row_ge_col":
        return (r >= c).astype(BF16)
    if kind == "row_le_col":
        return (r <= c).astype(BF16)
    raise ValueError(kind)


def _sb_tile(qs, kj, r_carry, u_strict, masked):
    z = _dot(qs, kj, 1, 1)
    lb = jnp.minimum(z, 0.0) - jnp.log(1.0 + jnp.exp(-jnp.abs(z)))
    l1m = lb - z
    keep = None
    if masked:
        tq, tk = z.shape
        keep = lax.broadcasted_iota(jnp.int32, (tq, tk), 1) < lax.broadcasted_iota(jnp.int32, (tq, tk), 0)
        l1m = jnp.where(keep, l1m, 0.0)
    w = jnp.exp(lb + _dot(l1m.astype(BF16), u_strict) + r_carry)
    if masked:
        w = jnp.where(keep, w, 0.0)
    return lb, l1m, w, keep


def _sb_prep(T, nb, hd, refs_in, gains, scratch):
    q_scale = 1.0 / math.sqrt(hd)
    assert math.log2(q_scale) == round(math.log2(q_scale))

    def prep(i, _):
        rows = pl.ds(pl.multiple_of(i * T, T), T)
        for hh in range(2):
            sl = slice(hd * hh, hd * hh + hd)
            for n, (src, dst) in enumerate(zip(refs_in, scratch)):
                v = src[rows, sl]
                if n < 2:
                    v = v * lax.rsqrt(jnp.mean(v * v, axis=1, keepdims=True) + EPS) * gains[n][...]
                if n == 0:
                    v = v * q_scale
                dst[hh, rows, :] = v.astype(BF16)
        return 0

    lax.fori_loop(0, nb, prep, 0)


def _sb_chains(m, T):
    rows = [pl.ds(pl.multiple_of((2 * m + qb) * T, T), T) for qb in range(2)]
    return rows, [(hh, qb) for qb in range(2) for hh in range(2)]


def sb_attn_fwd(qkv, q_gain, k_gain, *, name, gather=None):
    S, D3 = qkv.shape
    D = D3 // 3
    npairs = D // LANES
    hd = SB_HEAD_DIM
    T = min(256, S)
    nb = S // T
    assert nb % 2 == 0

    def body(*refs):
        if gather is None:
            q_ref, k_ref, v_ref, qg_ref, kg_ref, us_ref, o_ref, qn_s, kn_s, vb_s = refs
        else:
            q_ref, k_ref, v_ref, qg_ref, kg_ref, us_ref, ag_in, o_ref, ag_out, qn_s, kn_s, vb_s = refs[:12]
            comm = (ag_in, ag_out) + refs[12:]
            step = pl.program_id(0)
            pl.when(step == 0)(lambda: _ag_phase(comm, "start"))
            pl.when(step == npairs - 1)(lambda: _ag_phase(comm, "forward"))
        us = us_ref[...]
        _sb_prep(T, nb, hd, (q_ref, k_ref, v_ref), (qg_ref, kg_ref), (qn_s, kn_s, vb_s))

        def superblock(m, _):
            rows_q, chains = _sb_chains(m, T)
            qs = {ch: qn_s[ch[0], rows_q[ch[1]], :] for ch in chains}

            def tile(j, carry, which):
                rows_j = pl.ds(pl.multiple_of(j * T, T), T)
                new = dict(carry)
                for ch, masked in which:
                    acc, rc = carry[ch]
                    _, l1m, w, _ = _sb_tile(qs[ch], kn_s[ch[0], rows_j, :], rc, us, masked)
                    new[ch] = (acc + _dot(w.astype(BF16), vb_s[ch[0], rows_j, :]),
                               rc + jnp.sum(l1m, axis=1, keepdims=True))
                return new

            carry = {ch: (jnp.zeros((T, hd), F32), jnp.zeros((T, 1), F32)) for ch in chains}
            carry = tile(2 * m + 1, carry, [(ch, True) for ch in chains if ch[1] == 1])
            carry = tile(2 * m, carry, [(ch, ch[1] == 0) for ch in chains])
            carry = lax.fori_loop(0, 2 * m, lambda jj, c: tile(2 * m - 1 - jj, c, [(ch, False) for ch in chains]),
                                  carry)
            for qb in range(2):
                o_ref[rows_q[qb], :] = jnp.concatenate([carry[(0, qb)][0], carry[(1, qb)][0]], axis=1)
            return 0

        lax.fori_loop(0, nb // 2, superblock, 0)
        if gather is not None:
            pl.when(step == npairs - 1)(lambda: _ag_phase(comm, "finish"))

    col = lambda off: pl.BlockSpec((S, LANES), lambda p, off=off: (0, off + p))
    gain = pl.BlockSpec((1, hd), lambda p: (0, 0))
    in_specs = [col(0), col(npairs), col(2 * npairs), gain, gain, pl.BlockSpec((T, T), lambda p: (0, 0))]
    out_specs = [pl.BlockSpec((S, LANES), lambda p: (0, p))]
    out_shape = [jax.ShapeDtypeStruct((S, D), F32)]
    scratch = [pltpu.VMEM((2, S, hd), BF16)] * 3
    args = [qkv, qkv, qkv, q_gain, k_gain, _tri(T, "row_gt_col")]
    if gather is not None:
        in_specs.append(HBM_SPEC)
        out_specs.append(HBM_SPEC)
        out_shape.append(jax.ShapeDtypeStruct((N_DEV,) + gather.shape, gather.dtype))
        scratch += COMM_SEMAPHORES
        args.append(gather)
    out = pl.pallas_call(
        body, name=name, grid=(npairs,), in_specs=in_specs, out_specs=out_specs, out_shape=out_shape,
        scratch_shapes=scratch, compiler_params=_params(("arbitrary",)),
    )(*args)
    return out[0] if gather is None else tuple(out)


def sb_attn_bwd(qkv, o, do, q_gain, k_gain, *, name, scatter=None):
    S, D3 = qkv.shape
    D = D3 // 3
    npairs = D // LANES
    hd = SB_HEAD_DIM
    T = min(256, S)
    nb = S // T
    scale = 1.0 / math.sqrt(hd)

    def body(*refs):
        if scatter is None:
            (q_ref, k_ref, v_ref, o_ref, do_ref, qg_ref, kg_ref, us_ref,
             dqkv_ref, dg_ref, qn_s, kn_s, vb_s, dob_s) = refs
        else:
            ns = len(scatter)
            q_ref, k_ref, v_ref, o_ref, do_ref, qg_ref, kg_ref, us_ref = refs[:8]
            rs_in = refs[8:8 + ns]
            dqkv_ref, dg_ref = refs[8 + ns:10 + ns]
            rs_out = refs[10 + ns:10 + 2 * ns]
            qn_s, kn_s, vb_s, dob_s = refs[10 + 2 * ns:14 + 2 * ns]
            rs_sems = refs[14 + 2 * ns:]
            pl.when(pl.program_id(0) == 0)(lambda: _rs_phase(rs_in, rs_out, rs_sems, "start"))
        dq_ref, dk_ref, dv_ref = dqkv_ref.at[0], dqkv_ref.at[1], dqkv_ref.at[2]

        @pl.when(pl.program_id(0) == 0)
        def _():
            dg_ref[...] = jnp.zeros_like(dg_ref)

        us = us_ref[...]
        u_prefix = (1.0 - us.astype(F32)).astype(BF16)
        _sb_prep(T, nb, hd, (q_ref, k_ref, v_ref, do_ref), (qg_ref, kg_ref), (qn_s, kn_s, vb_s, dob_s))
        dk_ref[...] = jnp.zeros_like(dk_ref)
        dv_ref[...] = jnp.zeros_like(dv_ref)

        def superblock(m, _):
            rows_q, chains = _sb_chains(m, T)
            qs = {ch: qn_s[ch[0], rows_q[ch[1]], :] for ch in chains}
            doi = {ch: dob_s[ch[0], rows_q[ch[1]], :] for ch in chains}
            dt_total = {ch: jnp.sum(doi[ch].astype(F32) * o_ref[rows_q[ch[1]], hd * ch[0]:hd * ch[0] + hd],
                                    axis=1, keepdims=True) for ch in chains}

            def tile(j, carry, which):
                rows_j = pl.ds(pl.multiple_of(j * T, T), T)
                new = dict(carry)
                dk_part, dv_part = {}, {}
                for ch, masked in which:
                    hh = ch[0]
                    dq_acc, rc, gc = carry[ch]
                    kj = kn_s[hh, rows_j, :]
                    lb, l1m, w, keep = _sb_tile(qs[ch], kj, rc, us, masked)
                    wb = w.astype(BF16)
                    g = _dot(doi[ch], vb_s[hh, rows_j, :], 1, 1) * wb.astype(F32)
                    g_row = jnp.sum(g, axis=1, keepdims=True)
                    g_upto = (dt_total[ch] - gc - g_row) + _dot(g.astype(BF16), u_prefix)
                    dz = g - g_upto * jnp.exp(lb)
                    if masked:
                        dz = jnp.where(keep, dz, 0.0)
                    dzb = dz.astype(BF16)
                    dv_part[hh] = dv_part.get(hh, 0.0) + _dot(wb, doi[ch], 0, 0)
                    dk_part[hh] = dk_part.get(hh, 0.0) + _dot(dzb, qs[ch], 0, 0)
                    new[ch] = (dq_acc + _dot(dzb, kj), rc + jnp.sum(l1m, axis=1, keepdims=True),
                               gc + g_row)
                dv_ref[rows_j, :] += jnp.concatenate([dv_part[0], dv_part[1]], axis=1)
                dk_ref[rows_j, :] += jnp.concatenate([dk_part[0], dk_part[1]], axis=1)
                return new

            zero1 = jnp.zeros((T, 1), F32)
            carry = {ch: (jnp.zeros((T, hd), F32), zero1, zero1) for ch in chains}
            carry = tile(2 * m + 1, carry, [(ch, True) for ch in chains if ch[1] == 1])
            carry = tile(2 * m, carry, [(ch, ch[1] == 0) for ch in chains])
            carry = lax.fori_loop(0, 2 * m, lambda jj, c: tile(2 * m - 1 - jj, c, [(ch, False) for ch in chains]),
                                  carry)
            for qb in range(2):
                dq_ref[rows_q[qb], :] = jnp.concatenate([carry[(0, qb)][0], carry[(1, qb)][0]], axis=1) * scale
            return 0

        lax.fori_loop(0, nb // 2, superblock, 0)

        def finish(i, carry):
            rows = pl.ds(pl.multiple_of(i * T, T), T)
            new = []
            for hh in range(2):
                sl = slice(hd * hh, hd * hh + hd)
                outs = []
                for raw_ref, gain_ref, dn in ((q_ref, qg_ref, dq_ref[rows, sl]), (k_ref, kg_ref, dk_ref[rows, sl])):
                    raw = raw_ref[rows, sl]
                    r = lax.rsqrt(jnp.mean(raw * raw, axis=1, keepdims=True) + EPS)
                    hat = raw * r
                    t = dn * gain_ref[...]
                    outs.append((r * (t - hat * jnp.mean(hat * t, axis=1, keepdims=True)),
                                 jnp.sum(dn * hat, axis=0, keepdims=True)))
                dq_ref[rows, sl] = outs[0][0]
                dk_ref[rows, sl] = outs[1][0]
                new.append((carry[hh][0] + outs[0][1], carry[hh][1] + outs[1][1]))
            return tuple(new)

        zg = (jnp.zeros((1, hd), F32), jnp.zeros((1, hd), F32))
        tot = lax.fori_loop(0, nb, finish, (zg, zg))
        dg_ref[0:1, 0:hd] += tot[0][0] + tot[1][0]
        dg_ref[1:2, 0:hd] += tot[0][1] + tot[1][1]
        if scatter is not None:
            pl.when(pl.program_id(0) == npairs - 1)(lambda: _rs_phase(rs_in, rs_out, rs_sems, "finish"))

    col = lambda off: pl.BlockSpec((S, LANES), lambda p, off=off: (0, off + p))
    gain = pl.BlockSpec((1, hd), lambda p: (0, 0))
    tri = pl.BlockSpec((T, T), lambda p: (0, 0))
    pair = pl.BlockSpec((S, LANES), lambda p: (0, p))
    in_specs = [col(0), col(npairs), col(2 * npairs), pair, pair, gain, gain, tri]
    out_specs = [pl.BlockSpec((3, S, LANES), lambda p: (0, 0, p)), pl.BlockSpec((8, LANES), lambda p: (0, 0))]
    out_shape = [jax.ShapeDtypeStruct((3, S, D), F32), jax.ShapeDtypeStruct((8, LANES), F32)]
    scratch = [pltpu.VMEM((2, S, hd), BF16)] * 4
    args = [qkv, qkv, qkv, o, do, q_gain, k_gain, _tri(T, "row_gt_col")]
    if scatter is not None:
        in_specs += [HBM_SPEC] * len(scatter)
        out_specs += [HBM_SPEC] * len(scatter)
        out_shape += [jax.ShapeDtypeStruct(g.shape, g.dtype) for g in scatter]
        scratch += _rs_semaphores(len(scatter))
        args += list(scatter)
    out = pl.pallas_call(
        body, name=name, grid=(npairs,), in_specs=in_specs, out_specs=out_specs, out_shape=out_shape,
        scratch_shapes=scratch, compiler_params=_params(("arbitrary",)),
    )(*args)
    res = (out[0], out[1][0:1, :hd], out[1][1:2, :hd])
    return res if scatter is None else res + (list(out[2:]),)


def gmlp_fwd(zzpre, b_in, v_gain, wc, bsf, *, name):
    S, H2 = zzpre.shape
    H = H2 // 2
    G, T, _ = wc.shape
    gd = H // G

    def body(z_ref, b_ref, vg_ref, wc_ref, bs_ref, p_ref):
        zz, _ = _gelu_and_grad(z_ref[...] + b_ref[...])
        u = zz[:, :H]
        v = zz[:, H:]
        vn = v * lax.rsqrt(jnp.mean(v * v, axis=1, keepdims=True) + EPS) * vg_ref[...]
        for g in range(G):
            gs = slice(g * gd, (g + 1) * gd)
            mixed = _dot(wc_ref[g], vn[:, gs].astype(BF16)) + bs_ref[g]
            p_ref[:, gs] = (u[:, gs] * mixed).astype(BF16)

    full3 = lambda shp: pl.BlockSpec(shp, lambda c: (0, 0, 0))
    return pl.pallas_call(
        body, name=name, grid=(S // T,),
        in_specs=[pl.BlockSpec((T, H2), lambda c: (c, 0)), pl.BlockSpec((1, H2), lambda c: (0, 0)),
                  pl.BlockSpec((1, H), lambda c: (0, 0)), full3((G, T, T)), full3((G, T, gd))],
        out_specs=pl.BlockSpec((T, H), lambda c: (c, 0)), out_shape=jax.ShapeDtypeStruct((S, H), BF16),
        compiler_params=_params(("parallel",)),
    )(zzpre, b_in, v_gain, wc, bsf)


def gmlp_bwd(zzpre, b_in, v_gain, wc, bsf, dp, *, name):
    S, H2 = zzpre.shape
    H = H2 // 2
    G, T, _ = wc.shape
    gd = H // G
    assert G <= LANES

    def body(z_ref, b_ref, vg_ref, wc_ref, bs_ref, dp_ref, dzz_ref, db_ref, dvg_ref, dws_ref, dbs_ref):
        @pl.when(pl.program_id(0) == 0)
        def _():
            db_ref[...] = jnp.zeros_like(db_ref)
            dvg_ref[...] = jnp.zeros_like(dvg_ref)
            dws_ref[...] = jnp.zeros_like(dws_ref)
            dbs_ref[...] = jnp.zeros_like(dbs_ref)

        zz, gp = _gelu_and_grad(z_ref[...] + b_ref[...])
        u = zz[:, :H]
        v = zz[:, H:]
        r = lax.rsqrt(jnp.mean(v * v, axis=1, keepdims=True) + EPS)
        vhat = v * r
        vg = vg_ref[...]
        vn = vhat * vg
        dpv = dp_ref[...]
        tril = lax.broadcasted_iota(jnp.int32, (T, T), 1) <= lax.broadcasted_iota(jnp.int32, (T, T), 0)
        lane = lax.broadcasted_iota(jnp.int32, (T, LANES), 1)
        dbs = jnp.zeros((T, LANES), F32)
        du_parts, dvn_parts = [], []
        for g in range(G):
            gs = slice(g * gd, (g + 1) * gd)
            vng = vn[:, gs].astype(BF16)
            wcg = wc_ref[g]
            mixed = _dot(wcg, vng) + bs_ref[g]
            dpg = dpv[:, gs]
            du_parts.append(dpg * mixed)
            dmx = dpg * u[:, gs]
            dmxb = dmx.astype(BF16)
            dvn_parts.append(_dot(wcg, dmxb, 0, 0))
            dws_ref[g] += jnp.where(tril, _dot(dmxb, vng, 1, 1), 0.0)
            dbs = dbs + jnp.where(lane == g, jnp.sum(dmx, axis=1, keepdims=True), 0.0)
        dbs_ref[...] += dbs
        du = jnp.concatenate(du_parts, axis=1)
        dvn = jnp.concatenate(dvn_parts, axis=1)
        dvg_ref[...] += jnp.sum(dvn * vhat, axis=0, keepdims=True)
        t = dvn * vg
        dv = r * (t - vhat * jnp.mean(vhat * t, axis=1, keepdims=True))
        dzu = du * gp[:, :H]
        dzv = dv * gp[:, H:]
        dzz_ref[:, :H] = dzu.astype(BF16)
        dzz_ref[:, H:] = dzv.astype(BF16)
        db_ref[:, :H] += jnp.sum(dzu, axis=0, keepdims=True)
        db_ref[:, H:] += jnp.sum(dzv, axis=0, keepdims=True)

    full3 = lambda shp: pl.BlockSpec(shp, lambda c: (0, 0, 0))
    vec = lambda n: pl.BlockSpec((1, n), lambda c: (0, 0))
    return pl.pallas_call(
        body, name=name, grid=(S // T,),
        in_specs=[pl.BlockSpec((T, H2), lambda c: (c, 0)), vec(H2), vec(H), full3((G, T, T)), full3((G, T, gd)),
                  pl.BlockSpec((T, H), lambda c: (c, 0))],
        out_specs=[pl.BlockSpec((T, H2), lambda c: (c, 0)), vec(H2), vec(H), full3((G, T, T)),
                   pl.BlockSpec((T, LANES), lambda c: (0, 0))],
        out_shape=[jax.ShapeDtypeStruct((S, H2), BF16), jax.ShapeDtypeStruct((1, H2), F32),
                   jax.ShapeDtypeStruct((1, H), F32), jax.ShapeDtypeStruct((G, T, T), F32),
                   jax.ShapeDtypeStruct((T, LANES), F32)],
        compiler_params=_params(("arbitrary",)),
    )(zzpre, b_in, v_gain, wc, bsf, dp)


def _shift_rows(v, k, n_rows):
    if k == 0:
        return v
    rolled = pltpu.roll(v, k % n_rows, 0)
    row = lax.broadcasted_iota(jnp.int32, v.shape, 0)
    keep = (row >= k) if k > 0 else (row < n_rows + k)
    return jnp.where(keep, rolled, 0.0)


def conv_fwd(zx, conv_w, conv_b, col0, *, name):
    S = zx.shape[0]
    C = conv_w.shape[1]
    tc = _pick(C, (256, 128))
    off = col0 // tc
    assert col0 % tc == 0

    def body(x_ref, w_ref, b_ref, o_ref):
        xv = x_ref[...]
        acc = b_ref[...] + w_ref[SSM_CONV - 1:SSM_CONV, :] * xv
        for k in range(SSM_CONV - 1):
            acc = acc + w_ref[k:k + 1, :] * _shift_rows(xv, SSM_CONV - 1 - k, S)
        o_ref[...] = acc * _sigmoid(acc)

    return pl.pallas_call(
        body, name=name, grid=(C // tc,),
        in_specs=[pl.BlockSpec((S, tc), lambda j: (0, off + j)), pl.BlockSpec((SSM_CONV, tc), lambda j: (0, j)),
                  pl.BlockSpec((1, tc), lambda j: (0, j))],
        out_specs=pl.BlockSpec((S, tc), lambda j: (0, j)), out_shape=jax.ShapeDtypeStruct((S, C), F32),
        compiler_params=_params(("parallel",)),
    )(zx, conv_w, conv_b)


def conv_bwd(zx, conv_w, conv_b, col0, dout, *, name):
    S = zx.shape[0]
    C = conv_w.shape[1]
    tc = _pick(C, (256, 128))
    off = col0 // tc

    def body(x_ref, w_ref, b_ref, do_ref, dx_ref, dw_ref, db_ref):
        xv = x_ref[...]
        shifted = [_shift_rows(xv, SSM_CONV - 1 - k, S) for k in range(SSM_CONV)]
        acc = b_ref[...]
        for k in range(SSM_CONV):
            acc = acc + w_ref[k:k + 1, :] * shifted[k]
        s = _sigmoid(acc)
        dacc = do_ref[...] * (s * (1.0 + acc * (1.0 - s)))
        db_ref[...] = jnp.sum(dacc, axis=0, keepdims=True)
        dx = jnp.zeros_like(xv)
        for k in range(SSM_CONV):
            dw_ref[k:k + 1, :] = jnp.sum(dacc * shifted[k], axis=0, keepdims=True)
            dx = dx + w_ref[k:k + 1, :] * _shift_rows(dacc, -(SSM_CONV - 1 - k), S)
        dx_ref[...] = dx

    slab = pl.BlockSpec((S, tc), lambda j: (0, j))
    return pl.pallas_call(
        body, name=name, grid=(C // tc,),
        in_specs=[pl.BlockSpec((S, tc), lambda j: (0, off + j)), pl.BlockSpec((SSM_CONV, tc), lambda j: (0, j)),
                  pl.BlockSpec((1, tc), lambda j: (0, j)), slab],
        out_specs=[slab, pl.BlockSpec((SSM_CONV, tc), lambda j: (0, j)), pl.BlockSpec((1, tc), lambda j: (0, j))],
        out_shape=[jax.ShapeDtypeStruct((S, C), F32), jax.ShapeDtypeStruct((SSM_CONV, C), F32),
                   jax.ShapeDtypeStruct((1, C), F32)],
        compiler_params=_params(("parallel",)),
    )(zx, conv_w, conv_b, dout)


def _ssd_chunk_terms(dtraw, bias, a_log, tl):
    dt = _softplus(dtraw + bias)
    a_neg = -jnp.exp(a_log)
    ac = _dot_exact_left(tl, dt * a_neg)
    ac_last = ac[ac.shape[0] - 1:, :]
    return dt, a_neg, ac, ac.T, jnp.exp(ac), jnp.exp(ac_last - ac), jnp.exp(ac_last)


def _ssd_specs(S, L, G, hpg, pd, inner):
    gw = hpg * pd
    n = SSM_STATE
    xb = inner // n

    def mk(cidx):
        return dict(
            x=pl.BlockSpec((L, gw), lambda g, c: (cidx(c), g)),
            b=pl.BlockSpec((L, n), lambda g, c: (cidx(c), xb + g)),
            c=pl.BlockSpec((L, n), lambda g, c: (cidx(c), xb + G + g)),
            z=pl.BlockSpec((L, gw), lambda g, c: (cidx(c), g)),
            dt=pl.BlockSpec((L, LANES), lambda g, c: (cidx(c), g)),
            gvec=pl.BlockSpec((1, 1, LANES), lambda g, c: (g, 0, 0)),
            chan=pl.BlockSpec((1, gw), lambda g, c: (0, g)),
            tri=pl.BlockSpec((L, L), lambda g, c: (0, 0)),
            hp=pl.BlockSpec((1, 1, gw, n), lambda g, c: (g, cidx(c), 0, 0)),
            bc=pl.BlockSpec((L, n), lambda g, c: (cidx(c), g)),
        )
    return mk


def ssd_fwd(xbc, zx, dtg, bias_g, alog_g, d_chan, ngain, L, G, *, name):
    S = xbc.shape[0]
    n = SSM_STATE
    inner = xbc.shape[1] - 2 * G * n
    gw = inner // G
    pd = SB_HEAD_DIM
    hpg = gw // pd
    nc = S // L
    sp = _ssd_specs(S, L, G, hpg, pd, inner)(lambda c: c)

    def body(x_ref, b_ref, c_ref, z_ref, dt_ref, bias_ref, alog_ref, d_ref, ng_ref, tl_ref,
             yn_ref, y_ref, hp_ref, state):
        @pl.when(pl.program_id(1) == 0)
        def _():
            state[...] = jnp.zeros_like(state)

        dt, _, ac, act, ea, dte, cd = _ssd_chunk_terms(dt_ref[...], bias_ref[0], alog_ref[0], tl_ref[...])
        xv = x_ref[...]
        bm = b_ref[...].astype(BF16)
        cm = c_ref[...].astype(BF16)
        cb = _dot(cm, bm, 1, 1)
        tril = lax.broadcasted_iota(jnp.int32, (L, L), 1) <= lax.broadcasted_iota(jnp.int32, (L, L), 0)
        hp_ref[0, 0] = state[...]
        for r in range(hpg):
            ps = slice(r * pd, (r + 1) * pd)
            xr = xv[:, ps]
            xdt = xr * dt[:, r:r + 1]
            lm = jnp.exp(jnp.where(tril, ac[:, r:r + 1] - act[r:r + 1, :], -jnp.inf))
            hprev = state[ps, :]
            y = _dot((cb * lm).astype(BF16), xdt.astype(BF16))
            y = y + _dot(cm, hprev.astype(BF16), 1, 1) * ea[:, r:r + 1]
            y_ref[:, ps] = y + xr * d_ref[:, ps]
            st = _dot((xdt * dte[:, r:r + 1]).astype(BF16), bm, 0, 0)
            state[ps, :] = hprev * cd[:, r:r + 1] + st
        yfull = y_ref[...]
        zg = z_ref[...]
        yg = yfull * (zg * _sigmoid(zg))
        yn_ref[...] = (yg * lax.rsqrt(jnp.mean(yg * yg, axis=1, keepdims=True) + EPS) * ng_ref[...]).astype(BF16)

    return pl.pallas_call(
        body, name=name, grid=(G, nc),
        in_specs=[sp["x"], sp["b"], sp["c"], sp["z"], sp["dt"], sp["gvec"], sp["gvec"], sp["chan"], sp["chan"], sp["tri"]],
        out_specs=[sp["x"], sp["x"], sp["hp"]],
        out_shape=[jax.ShapeDtypeStruct((S, inner), BF16), jax.ShapeDtypeStruct((S, inner), F32),
                   jax.ShapeDtypeStruct((G, nc, gw, n), F32)],
        scratch_shapes=[pltpu.VMEM((gw, n), F32)],
        compiler_params=_params(("arbitrary", "arbitrary")),
    )(xbc, xbc, xbc, zx, dtg, bias_g, alog_g, d_chan, ngain, _tri(L, "row_ge_col"))


def ssd_bwd(xbc, zx, dtg, bias_g, alog_g, d_chan, ngain, yfull, hp, dyn, L, G, *, name):
    S = xbc.shape[0]
    n = SSM_STATE
    inner = xbc.shape[1] - 2 * G * n
    gw = inner // G
    pd = SB_HEAD_DIM
    hpg = gw // pd
    nc = S // L
    sp = _ssd_specs(S, L, G, hpg, pd, inner)(lambda c: nc - 1 - c)

    def body(x_ref, b_ref, c_ref, z_ref, dt_ref, bias_ref, alog_ref, d_ref, ng_ref, tl_ref, tu_ref,
             yf_ref, hp_ref, dyn_ref,
             dz_ref, dx_ref, db_ref, dc_ref, ddt_ref, dbias_ref, dalog_ref, dd_ref, dng_ref, dstate):
        first = pl.program_id(1) == 0

        @pl.when(first)
        def _():
            dstate[...] = jnp.zeros_like(dstate)
            dbias_ref[...] = jnp.zeros_like(dbias_ref)
            dalog_ref[...] = jnp.zeros_like(dalog_ref)
            dd_ref[...] = jnp.zeros_like(dd_ref)
            dng_ref[...] = jnp.zeros_like(dng_ref)

        dtraw = dt_ref[...]
        dt, a_neg, ac, act, ea, dte, cd = _ssd_chunk_terms(dtraw, bias_ref[0], alog_ref[0], tl_ref[...])
        xv = x_ref[...]
        bm = b_ref[...].astype(BF16)
        cm = c_ref[...].astype(BF16)
        cb = _dot(cm, bm, 1, 1)
        tril = lax.broadcasted_iota(jnp.int32, (L, L), 1) <= lax.broadcasted_iota(jnp.int32, (L, L), 0)
        lane = lax.broadcasted_iota(jnp.int32, (L, LANES), 1)
        lane1 = lax.broadcasted_iota(jnp.int32, (1, LANES), 1)

        yfull = yf_ref[...]
        zg = z_ref[...]
        sg = _sigmoid(zg)
        gate = zg * sg
        yg = yfull * gate
        rr = lax.rsqrt(jnp.mean(yg * yg, axis=1, keepdims=True) + EPS)
        yhat = yg * rr
        dynv = dyn_ref[...]
        dng_ref[...] += jnp.sum(dynv * yhat, axis=0, keepdims=True)
        t = dynv * ng_ref[...]
        dyg = rr * (t - yhat * jnp.mean(yhat * t, axis=1, keepdims=True))
        dy = dyg * gate
        dz_ref[...] = dyg * yfull * (sg * (1.0 + zg * (1.0 - sg)))

        dcb = jnp.zeros((L, L), F32)
        dc_acc = jnp.zeros((L, n), F32)
        db_acc = jnp.zeros((L, n), F32)
        dac = jnp.zeros((L, LANES), F32)
        xdx = jnp.zeros((L, LANES), F32)
        tail = jnp.zeros((1, LANES), F32)
        dskip = jnp.zeros((1, LANES), F32)
        ones_l = jnp.ones((L, LANES), BF16)
        for r in range(hpg):
            ps = slice(r * pd, (r + 1) * pd)
            xr = xv[:, ps]
            dyr = dy[:, ps]
            dtr = dt[:, r:r + 1]
            dter = dte[:, r:r + 1]
            cdr = cd[:, r:r + 1]
            xdt = xr * dtr
            xdtb = xdt.astype(BF16)
            dyrb = dyr.astype(BF16)
            lm = jnp.exp(jnp.where(tril, ac[:, r:r + 1] - act[r:r + 1, :], -jnp.inf))
            m32 = cb * lm
            mb = m32.astype(BF16)
            hprev = hp_ref[0, 0, ps, :]
            hpb = hprev.astype(BF16)
            dhn = dstate[ps, :]
            dhnb = dhn.astype(BF16)
            ear = ea[:, r:r + 1]
            gy = (dyr * ear).astype(BF16)
            dc_acc = dc_acc + _dot(gy, hpb)
            dstate[ps, :] = _dot(gy, cm, 0, 0) + dhn * cdr
            bdh = _dot(bm, dhnb, 1, 1)
            db_acc = db_acc + _dot((xdt * dter).astype(BF16), dhnb)
            dm = _dot(dyrb, xdtb, 1, 1)
            dxdt = bdh * dter + _dot(mb, dyrb, 0, 0)
            dcb = dcb + dm * lm
            wmat = dm * m32
            whi = wmat.astype(BF16)
            wlo = (wmat - whi.astype(F32)).astype(BF16)
            col_w = _dot(whi, ones_l, 0, 0) + _dot(wlo, ones_l, 0, 0)
            sdte = jnp.sum(xdt * bdh * dter, axis=1, keepdims=True)
            e_r = jnp.sum(wmat, axis=1, keepdims=True) + jnp.sum(dyr * _dot(cm, hpb, 1, 1) * ear, axis=1, keepdims=True) - sdte
            c_r = cdr * jnp.sum(jnp.sum(dhn * hprev, axis=1, keepdims=True), axis=0, keepdims=True) \
                + jnp.sum(sdte, axis=0, keepdims=True)
            dac = dac + jnp.where(lane == r, e_r - col_w, 0.0)
            xdx = xdx + jnp.where(lane == r, jnp.sum(dxdt * xr, axis=1, keepdims=True), 0.0)
            tail = tail + jnp.where(lane1 == r, c_r, 0.0)
            dskip = dskip + jnp.where(lane1 == r, jnp.sum(jnp.sum(dyr * xr, axis=1, keepdims=True), axis=0, keepdims=True), 0.0)
            dx_ref[:, ps] = dxdt * dtr + dyr * d_ref[:, ps]
        dcbb = dcb.astype(BF16)
        dc_ref[...] = dc_acc + _dot(dcbb, bm)
        db_ref[...] = db_acc + _dot(dcbb, cm, 0, 0)
        da = _dot_exact_left(tu_ref[...], dac) + tail
        real = lane < hpg
        ddt = jnp.where(real, (da * a_neg + xdx) * _sigmoid(dtraw + bias_ref[0]), 0.0)
        ddt_ref[...] = ddt
        dd_ref[0] += dskip
        dbias_ref[0] += jnp.sum(ddt, axis=0, keepdims=True)
        dalog_ref[0] += jnp.where(lane1 < hpg, jnp.sum(da * dt, axis=0, keepdims=True) * a_neg, 0.0)

    return pl.pallas_call(
        body, name=name, grid=(G, nc),
        in_specs=[sp["x"], sp["b"], sp["c"], sp["z"], sp["dt"], sp["gvec"], sp["gvec"], sp["chan"], sp["chan"],
                  sp["tri"], sp["tri"], sp["x"], sp["hp"], sp["x"]],
        out_specs=[sp["x"], sp["x"], sp["bc"], sp["bc"], sp["dt"], sp["gvec"], sp["gvec"], sp["gvec"], sp["chan"]],
        out_shape=[jax.ShapeDtypeStruct((S, inner), F32), jax.ShapeDtypeStruct((S, inner), F32),
                   jax.ShapeDtypeStruct((S, G * n), F32), jax.ShapeDtypeStruct((S, G * n), F32),
                   jax.ShapeDtypeStruct((S, G * LANES), F32), jax.ShapeDtypeStruct((G, 1, LANES), F32),
                   jax.ShapeDtypeStruct((G, 1, LANES), F32), jax.ShapeDtypeStruct((G, 1, LANES), F32),
                   jax.ShapeDtypeStruct((1, inner), F32)],
        scratch_shapes=[pltpu.VMEM((gw, n), F32)],
        compiler_params=_params(("arbitrary", "arbitrary")),
    )(xbc, xbc, xbc, zx, dtg, bias_g, alog_g, d_chan, ngain, _tri(L, "row_ge_col"), _tri(L, "row_le_col"),
      yfull, hp, dyn)


def _spread_dt(w_dt_t, G, hpg):
    K = w_dt_t.shape[1]
    w = w_dt_t.reshape(G, hpg, K)
    return jnp.pad(w, ((0, 0), (0, LANES - hpg), (0, 0))).reshape(G * LANES, K)


def _group_vec(v, G, hpg):
    return jnp.pad(v.reshape(G, 1, hpg), ((0, 0), (0, 0), (0, LANES - hpg)))


def local_step(x, target, W, late=None):
    S, D = x.shape
    depth = W["mix_norm"].shape[0]
    gm_groups, gm_chunk = W["gm_w_s"].shape[1], W["gm_w_s"].shape[2]
    heads = W["ssm_dt_bias"].shape[1]
    inner = heads * SB_HEAD_DIM
    L = gm_chunk
    received = None

    saved = []
    for i in range(depth):
        kind, j = i % 3, i // 3
        s = dict(x=x)
        h = rms_fwd(x, W["mix_norm"][i:i + 1], name=f"rms_mix_fwd")
        s["h"] = h
        if kind == 0:
            qkv = matmul(h, W["sb_w_qkv"][j], tb=True, name="mm_qkv")
            if late is not None and i == 0:
                o, gathered = sb_attn_fwd(qkv, W["sb_q_gain"][j:j + 1], W["sb_k_gain"][j:j + 1], name="sb_fwd_gather",
                                          gather=late.shard)
                late.fill(W, gathered)
            else:
                o = sb_attn_fwd(qkv, W["sb_q_gain"][j:j + 1], W["sb_k_gain"][j:j + 1], name="sb_fwd")
            x1 = matmul(o, W["sb_w_o"][j], residual=x, name="mm_sb_out")
            s.update(qkv=qkv, o=o)
        elif kind == 1:
            wc = jnp.where(jnp.tril(jnp.ones((gm_chunk, gm_chunk), bool)), W["gm_w_s"][j], 0.0).astype(BF16)
            bsf = jnp.broadcast_to(W["gm_b_s"][j][:, :, None], (gm_groups, gm_chunk, W["gm_v_gain"].shape[1] // gm_groups)).astype(F32)
            zzpre = matmul(h, W["gm_w_in"][j], tb=True, name="mm_gm_in")
            p = gmlp_fwd(zzpre, W["gm_b_in"][j:j + 1], W["gm_v_gain"][j:j + 1], wc, bsf, name="gm_fwd")
            x1 = matmul(p, W["gm_w_out"][j], residual=x, name="mm_gm_out")
            s.update(zzpre=zzpre, p=p, wc=wc, bsf=bsf)
        else:
            conv_dim = W["ssm_conv_w"].shape[2]
            G = (conv_dim - inner) // (2 * SSM_STATE)
            hpg = heads // G
            w_in = W["ssm_w_in"][j]
            w_zx = w_in[:inner + conv_dim]
            w_dtg = _spread_dt(w_in[inner + conv_dim:], G, hpg)
            bias_g = _group_vec(W["ssm_dt_bias"][j], G, hpg)
            alog_g = _group_vec(W["ssm_a_log"][j], G, hpg)
            d_chan = jnp.repeat(W["ssm_d"][j], SB_HEAD_DIM)[None, :]
            ngain = W["ssm_norm_gain"][j:j + 1]
            zx = matmul(h, w_zx, tb=True, name="mm_ssm_zx")
            dtg = matmul(h, w_dtg, tb=True, name="mm_ssm_dt")
            xbc = conv_fwd(zx, W["ssm_conv_w"][j], W["ssm_conv_b"][j:j + 1], inner, name="conv_fwd")
            yn, yfull, hp = ssd_fwd(xbc, zx, dtg, bias_g, alog_g, d_chan, ngain, L, G, name="ssd_fwd")
            x1 = matmul(yn, W["ssm_w_out"][j], residual=x, name="mm_ssm_out")
            s.update(w_zx=w_zx, w_dtg=w_dtg, bias_g=bias_g, alog_g=alog_g, d_chan=d_chan, ngain=ngain,
                     zx=zx, dtg=dtg, xbc=xbc, yn=yn, yfull=yfull, hp=hp)
        h2 = rms_fwd(x1, W["ffn_norm"][i:i + 1], name="rms_ffn_fwd")
        gu, a = ffn_up_fwd(h2, W["ffn_w_gu"][i], name="ffn_up_fwd")
        x2 = matmul(a, W["ffn_w_down"][i], residual=x1, name="mm_ffn_down")
        s.update(x1=x1, h2=h2, gu=gu, a=a)
        saved.append(s)
        x = x2

    dx, loss = loss_head(x, target, name="loss_head")

    gw = {k: {} for k in WEIGHTS}
    for i in reversed(range(depth)):
        kind, j = i % 3, i // 3
        s = saved[i]
        gw["ffn_w_down"][i] = matmul(s["a"], dx, ta=True, out_dtype=BF16, name="mm_ffn_dwdown")
        dgu = ffn_up_bwd(dx, W["ffn_w_down"][i], s["gu"], name="ffn_up_bwd")
        dh2 = matmul(dgu, W["ffn_w_gu"][i], a_split=2, name="mm_ffn_dh")
        gw["ffn_w_gu"][i] = matmul(dgu, s["h2"], ta=True, a_split=2, out_dtype=BF16, name="mm_ffn_dwgu")
        dx1, dgn = rms_bwd(s["x1"], W["ffn_norm"][i:i + 1], dh2, dx, name="rms_ffn_bwd")
        gw["ffn_norm"][i] = dgn[0]
        if kind == 0:
            do = matmul(dx1, W["sb_w_o"][j], tb=True, name="mm_sb_do")
            gw["sb_w_o"][j] = matmul(s["o"], dx1, ta=True, out_dtype=BF16, name="mm_sb_dwo")
            if late is not None and i == 0:
                dqkv, dqg, dkg, received = sb_attn_bwd(
                    s["qkv"], s["o"], do, W["sb_q_gain"][j:j + 1], W["sb_k_gain"][j:j + 1], name="sb_bwd_scatter",
                    scatter=late.contributions(gw))
            else:
                dqkv, dqg, dkg = sb_attn_bwd(s["qkv"], s["o"], do, W["sb_q_gain"][j:j + 1], W["sb_k_gain"][j:j + 1],
                                             name="sb_bwd")
            gw["sb_q_gain"][j] = dqg[0]
            gw["sb_k_gain"][j] = dkg[0]
            dh = matmul(dqkv, W["sb_w_qkv"][j], a_split=3, name="mm_sb_dh")
            gw["sb_w_qkv"][j] = matmul(dqkv, s["h"], ta=True, a_split=3, out_dtype=BF16, name="mm_sb_dwqkv")
        elif kind == 1:
            dp = matmul(dx1, W["gm_w_out"][j], tb=True, name="mm_gm_dp")
            gw["gm_w_out"][j] = matmul(s["p"], dx1, ta=True, out_dtype=BF16, name="mm_gm_dwout")
            dzz, db_in, dvg, dws, dbs = gmlp_bwd(s["zzpre"], W["gm_b_in"][j:j + 1], W["gm_v_gain"][j:j + 1],
                                                s["wc"], s["bsf"], dp, name="gm_bwd")
            gw["gm_b_in"][j] = db_in[0]
            gw["gm_v_gain"][j] = dvg[0]
            gw["gm_w_s"][j] = dws
            gw["gm_b_s"][j] = dbs[:, :gm_groups].T
            dh = matmul(dzz, W["gm_w_in"][j], name="mm_gm_dh")
            gw["gm_w_in"][j] = matmul(dzz, s["h"], ta=True, out_dtype=BF16, name="mm_gm_dwin")
        else:
            conv_dim = W["ssm_conv_w"].shape[2]
            G = (conv_dim - inner) // (2 * SSM_STATE)
            hpg = heads // G
            dyn = matmul(dx1, W["ssm_w_out"][j], tb=True, name="mm_ssm_dyn")
            gw["ssm_w_out"][j] = matmul(s["yn"], dx1, ta=True, out_dtype=BF16, name="mm_ssm_dwout")
            dz, dxs, dbm, dcm, ddt, dbias, dalog, dd, dng = ssd_bwd(
                s["xbc"], s["zx"], s["dtg"], s["bias_g"], s["alog_g"], s["d_chan"], s["ngain"], s["yfull"], s["hp"],
                dyn, L, G, name="ssd_bwd")
            dxbc = jnp.concatenate([dxs, dbm, dcm], axis=1)
            dpre, dcw, dcb = conv_bwd(s["zx"], W["ssm_conv_w"][j], W["ssm_conv_b"][j:j + 1], inner, dxbc,
                                      name="conv_bwd")
            dzx = jnp.concatenate([dz, dpre], axis=1)
            dh = matmul(ddt, s["w_dtg"], name="mm_ssm_dh_dt")
            dh = matmul(dzx, s["w_zx"], residual=dh, name="mm_ssm_dh")
            dw_zx = matmul(dzx, s["h"], ta=True, out_dtype=BF16, name="mm_ssm_dwzx")
            dw_dtg = matmul(ddt, s["h"], ta=True, out_dtype=BF16, name="mm_ssm_dwdt")
            dw_dt = dw_dtg.reshape(G, LANES, D)[:, :hpg, :].reshape(heads, D)
            gw["ssm_w_in"][j] = jnp.concatenate([dw_zx, dw_dt], axis=0)
            gw["ssm_conv_w"][j] = dcw
            gw["ssm_conv_b"][j] = dcb[0]
            gw["ssm_dt_bias"][j] = dbias[:, 0, :hpg].reshape(heads)
            gw["ssm_a_log"][j] = dalog[:, 0, :hpg].reshape(heads)
            gw["ssm_d"][j] = dd[:, 0, :hpg].reshape(heads)
            gw["ssm_norm_gain"][j] = dng[0]
        dx, dgn = rms_bwd(s["x"], W["mix_norm"][i:i + 1], dh, dx1, name="rms_mix_bwd")
        gw["mix_norm"][i] = dgn[0]

    return loss, dx, gw, received


MESH = pl.DeviceIdType.MESH
HBM_SPEC = pl.BlockSpec(memory_space=pltpu.HBM)
VMEM_SPEC = pl.BlockSpec(memory_space=pltpu.VMEM)


def _my_position():
    return lax.axis_index("x"), lax.axis_index("y"), lax.axis_index("c")


def _flip(v, bit):
    return 1 - v if bit else v


def all_gather_packed(shard, *, name):
    R, C = shard.shape

    def body(x_ref, out_ref, send_sems, recv_sems, local_sem):
        refs = (x_ref, out_ref, send_sems, recv_sems, local_sem)
        _ag_phase(refs, "start")
        _ag_phase(refs, "forward")
        _ag_phase(refs, "finish")

    return pl.pallas_call(
        body, name=name, out_shape=jax.ShapeDtypeStruct((N_DEV, R, C), shard.dtype),
        in_specs=[HBM_SPEC], out_specs=HBM_SPEC, scratch_shapes=COMM_SEMAPHORES,
    )(shard)


COMM_SEMAPHORES = [pltpu.SemaphoreType.DMA((7,)), pltpu.SemaphoreType.DMA((7,)), pltpu.SemaphoreType.DMA]


def _ag_phase(refs, phase):
    x_ref, out_ref, send_sems, recv_sems, local_sem = refs
    x, y, c = _my_position()
    me, sibling = (x, y, c), (x, y, 1 - c)
    chips = [(1 - x, y), (x, 1 - y), (1 - x, 1 - y)]

    def slot(px, py, pc):
        return out_ref.at[4 * px + 2 * py + pc]

    def copy(k, block, to, src=None):
        return pltpu.make_async_remote_copy(
            src_ref=slot(*block) if src is None else src, dst_ref=slot(*block),
            send_sem=send_sems.at[k], recv_sem=recv_sems.at[k], device_id=to, device_id_type=MESH)

    mine = pltpu.make_async_copy(x_ref, slot(*me), local_sem)
    first = [copy(0, me, sibling, src=x_ref)]
    first += [copy(1 + j, me, (*chip, c), src=x_ref) for j, chip in enumerate(chips)]
    passed = [copy(4 + j, (*chip, c), sibling) for j, chip in enumerate(chips)]
    if phase == "start":
        mine.start()
        for cp in first:
            cp.start()
    elif phase == "forward":
        for j, chip in enumerate(chips):
            copy(1 + j, (*chip, c), me).wait_recv()
            passed[j].start()
    else:
        copy(0, sibling, me).wait_recv()
        for j, chip in enumerate(chips):
            copy(4 + j, (*chip, 1 - c), me).wait_recv()
        for cp in first + passed:
            cp.wait_send()
        mine.wait()


def _rs_semaphores(n):
    return [pltpu.SemaphoreType.DMA((7 * n,)), pltpu.SemaphoreType.DMA((7 * n,)), pltpu.SemaphoreType.DMA((n,))]


def _rs_phase(g_refs, out_refs, sems, phase):
    send_sems, recv_sems, local_sems = sems
    x, y, c = _my_position()
    me = 4 * x + 2 * y + c
    copies = []
    for p, (g_ref, out_ref) in enumerate(zip(g_refs, out_refs)):
        copies.append(pltpu.make_async_copy(g_ref.at[me], out_ref.at[me], local_sems.at[p]))
        for k in range(1, N_DEV):
            px, py, pc = _flip(x, k & 4), _flip(y, k & 2), _flip(c, k & 1)
            copies.append(pltpu.make_async_remote_copy(
                src_ref=g_ref.at[4 * px + 2 * py + pc], dst_ref=out_ref.at[me],
                send_sem=send_sems.at[7 * p + k - 1], recv_sem=recv_sems.at[7 * p + k - 1],
                device_id=(px, py, pc), device_id_type=MESH))
    for cp in copies:
        if phase == "start":
            cp.start()
        else:
            cp.wait()


def exchange_for_reduce_scatter(gs, *, name):
    n = len(gs)

    def body(*refs):
        for phase in ("start", "finish"):
            _rs_phase(refs[:n], refs[n:2 * n], refs[2 * n:], phase)

    return pl.pallas_call(
        body, name=name, out_shape=[jax.ShapeDtypeStruct(g.shape, g.dtype) for g in gs],
        in_specs=[HBM_SPEC] * n, out_specs=[HBM_SPEC] * n, scratch_shapes=_rs_semaphores(n),
    )(*gs)


def sum_slots(recv, *, name):
    n, R, C = recv.shape
    tr = _pick(R, (512, 256, 128))

    def body(r_ref, o_ref):
        acc = r_ref[0].astype(F32)
        for s in range(1, n):
            acc = acc + r_ref[s].astype(F32)
        o_ref[...] = acc

    return pl.pallas_call(
        body, name=name, grid=(R // tr,), in_specs=[pl.BlockSpec((n, tr, C), lambda i: (0, i, 0))],
        out_specs=pl.BlockSpec((tr, C), lambda i: (i, 0)), out_shape=jax.ShapeDtypeStruct((R, C), F32),
        compiler_params=_params(("parallel",)),
    )(recv)


def all_reduce_small(v, *, name):
    R, C = v.shape

    def body(v_ref, o_ref, buf, send_sems, recv_sems):
        x, y, c = _my_position()
        me = 4 * x + 2 * y + c
        buf[me] = v_ref[...]
        copies = []
        for k in range(1, N_DEV):
            px, py, pc = _flip(x, k & 4), _flip(y, k & 2), _flip(c, k & 1)
            copies.append(pltpu.make_async_remote_copy(
                src_ref=v_ref, dst_ref=buf.at[me], send_sem=send_sems.at[k - 1], recv_sem=recv_sems.at[k - 1],
                device_id=(px, py, pc), device_id_type=MESH))
        for cp in copies:
            cp.start()
        for cp in copies:
            cp.wait()
        acc = buf[0]
        for s in range(1, N_DEV):
            acc = acc + buf[s]
        o_ref[...] = acc

    return pl.pallas_call(
        body, name=name, out_shape=jax.ShapeDtypeStruct((R, C), F32), in_specs=[VMEM_SPEC], out_specs=VMEM_SPEC,
        scratch_shapes=[pltpu.VMEM((N_DEV, R, C), F32), pltpu.SemaphoreType.DMA((7,)), pltpu.SemaphoreType.DMA((7,))],
        compiler_params=pltpu.CompilerParams(vmem_limit_bytes=VMEM_LIMIT_BYTES),
    )(v)


def _pad_rows(a, mult):
    pad = (-a.shape[0]) % mult
    return jnp.pad(a, ((0, pad), (0, 0))) if pad else a


def _pack_small(arrays):
    flat = []
    for a in arrays:
        f = a.reshape(-1).astype(F32)
        flat.append(jnp.pad(f, (0, (-f.shape[0]) % LANES)))
    return _pad_rows(jnp.concatenate(flat).reshape(-1, LANES), 8)


def _unpack_small(packed, shapes):
    flat = packed.reshape(-1)
    out, r = [], 0
    for shp in shapes:
        n = math.prod(shp)
        out.append(flat[r:r + n].reshape(shp))
        r += n + (-n) % LANES
    return out


ARG_NAMES = ("x",) + WEIGHTS + ("loss_target",) + tuple("m_" + w for w in WEIGHTS) + tuple("v_" + w for w in WEIGHTS)


def kernel(x, mix_norm, ffn_norm, sb_w_qkv, sb_q_gain, sb_k_gain, sb_w_o, gm_w_in, gm_b_in, gm_v_gain, gm_w_s, gm_b_s, gm_w_out, ssm_w_in, ssm_conv_w, ssm_conv_b, ssm_dt_bias, ssm_a_log, ssm_d, ssm_norm_gain, ssm_w_out, ffn_w_gu, ffn_w_down, loss_target, m_mix_norm, m_ffn_norm, m_sb_w_qkv, m_sb_q_gain, m_sb_k_gain, m_sb_w_o, m_gm_w_in, m_gm_b_in, m_gm_v_gain, m_gm_w_s, m_gm_b_s, m_gm_w_out, m_ssm_w_in, m_ssm_conv_w, m_ssm_conv_b, m_ssm_dt_bias, m_ssm_a_log, m_ssm_d, m_ssm_norm_gain, m_ssm_w_out, m_ffn_w_gu, m_ffn_w_down, v_mix_norm, v_ffn_norm, v_sb_w_qkv, v_sb_q_gain, v_sb_k_gain, v_sb_w_o, v_gm_w_in, v_gm_b_in, v_gm_v_gain, v_gm_w_s, v_gm_b_s, v_gm_w_out, v_ssm_w_in, v_ssm_conv_w, v_ssm_conv_b, v_ssm_dt_bias, v_ssm_a_log, v_ssm_d, v_ssm_norm_gain, v_ssm_w_out, v_ffn_w_gu, v_ffn_w_down):
    given = dict(zip(ARG_NAMES, (x, mix_norm, ffn_norm, sb_w_qkv, sb_q_gain, sb_k_gain, sb_w_o, gm_w_in, gm_b_in, gm_v_gain, gm_w_s, gm_b_s, gm_w_out, ssm_w_in, ssm_conv_w, ssm_conv_b, ssm_dt_bias, ssm_a_log, ssm_d, ssm_norm_gain, ssm_w_out, ffn_w_gu, ffn_w_down, loss_target, m_mix_norm, m_ffn_norm, m_sb_w_qkv, m_sb_q_gain, m_sb_k_gain, m_sb_w_o, m_gm_w_in, m_gm_b_in, m_gm_v_gain, m_gm_w_s, m_gm_b_s, m_gm_w_out, m_ssm_w_in, m_ssm_conv_w, m_ssm_conv_b, m_ssm_dt_bias, m_ssm_a_log, m_ssm_d, m_ssm_norm_gain, m_ssm_w_out, m_ffn_w_gu, m_ffn_w_down, v_mix_norm, v_ffn_norm, v_sb_w_qkv, v_sb_q_gain, v_sb_k_gain, v_sb_w_o, v_gm_w_in, v_gm_b_in, v_gm_v_gain, v_gm_w_s, v_gm_b_s, v_gm_w_out, v_ssm_w_in, v_ssm_conv_w, v_ssm_conv_b, v_ssm_dt_bias, v_ssm_a_log, v_ssm_d, v_ssm_norm_gain, v_ssm_w_out, v_ffn_w_gu, v_ffn_w_down)))
    mx, my, mc = _my_position()
    me = 4 * mx + 2 * my + mc

    pieces = [(k, l) for k in BIG for l in range(given[k].shape[0])]
    early = [("sb_w_qkv", 0)]
    late_pieces = [p for p in pieces if p not in early]
    last = [("sb_w_qkv", 0)]
    main = [p for p in pieces if p not in last]
    row_mult = 256

    def rows_of(p):
        return math.prod(given[p[0]].shape[1:]) // PACK_COLS

    def pack_shards(ps, extra=()):
        parts = [(given[k][l].T if k in COL_SHARDED else given[k][l]).astype(BF16).reshape(-1, PACK_COLS)
                 for k, l in ps] + list(extra)
        return _pad_rows(jnp.concatenate(parts, axis=0), row_mult)

    def split_rows(packed, ps):
        out, r = [], 0
        for p in ps:
            out.append(packed[..., r:r + rows_of(p), :])
            r += rows_of(p)
        return out

    def piece_to_full(g, k):
        rows, cols = given[k].shape[1:]
        return g.reshape(N_DEV * cols, rows) if k in COL_SHARDED else g.reshape(N_DEV * rows, cols)

    def full_to_piece(full, k):
        return full.reshape(N_DEV, -1, PACK_COLS)

    def summed_to_shard(g, k):
        rows, cols = given[k].shape[1:]
        return g.reshape(cols, rows).T if k in COL_SHARDED else g.reshape(rows, cols)

    def contributions(gw, ps):
        return [full_to_piece(gw[k][l], k) for k, l in ps]

    sharded_small = [lax.bitcast_convert_type(given[k], BF16) for k in SMALL_SHARDED]
    tail = jnp.concatenate([a.reshape(-1) for a in sharded_small])
    tail = jnp.pad(tail, (0, (-tail.size) % PACK_COLS)).reshape(-1, PACK_COLS)

    W = {k: given[k] for k in SMALL if k not in SMALL_SHARDED}
    W.update({k: [None] * given[k].shape[0] for k in BIG})
    gathered_early = all_gather_packed(pack_shards(early), name="all_gather_early")
    for (k, l), g in zip(early, split_rows(gathered_early, early)):
        W[k][l] = piece_to_full(g, k)

    class Late:
        shard = pack_shards(late_pieces, extra=[tail])

        @staticmethod
        def fill(weights, gathered):
            for (k, l), g in zip(late_pieces, split_rows(gathered, late_pieces)):
                weights[k][l] = piece_to_full(g, k)
            r0 = sum(rows_of(p) for p in late_pieces)
            tail_g = gathered[:, r0:r0 + tail.shape[0], :].reshape(N_DEV, -1)
            off = 0
            for k, a in zip(SMALL_SHARDED, sharded_small):
                g = lax.bitcast_convert_type(tail_g[:, off:off + a.size].reshape((N_DEV,) + a.shape), F32)
                weights[k] = jnp.moveaxis(g, 0, -2).reshape(g.shape[1:-1] + (N_DEV * g.shape[-1],))
                off += a.size

        @staticmethod
        def contributions(gw):
            return contributions(gw, main)

    loss, gx, gw, received_main = local_step(given["x"][0], given["loss_target"][0], W, late=Late)
    received_last = exchange_for_reduce_scatter(contributions(gw, last), name="reduce_scatter_last")

    grads_small = {k: jnp.stack([gw[k][l] for l in sorted(gw[k])], axis=0) for k in SMALL}
    small_shapes = [grads_small[k].shape for k in SMALL] + [(1, 1)]
    reduced = all_reduce_small(_pack_small([grads_small[k] for k in SMALL] + [loss]), name="all_reduce_small")
    small_full = dict(zip(SMALL + ("loss",), _unpack_small(reduced, small_shapes)))

    g_piece = {}
    for grp, received in ((main, received_main), (last, received_last)):
        for p, r in zip(grp, received):
            g_piece[p] = summed_to_shard(sum_slots(r, name="reduce_scatter_sum"), p[0])
    out_g, out_d, out_m, out_v = {}, {}, {}, {}
    for k in BIG:
        g = jnp.stack([g_piece[(k, l)] for l in range(given[k].shape[0])], axis=0)
        out_g[k] = g
        out_d[k], out_m[k], out_v[k] = adamw(given[k], g, given["m_" + k], given["v_" + k], name="adamw_" + k)

    gsmall = {}
    for k in SMALL:
        g = small_full[k]
        if k in SMALL_SHARDED:
            n = given[k].shape[-1]
            g = lax.dynamic_slice_in_dim(g, me * n, n, axis=g.ndim - 1)
        gsmall[k] = g
    local_shapes = [given[k].shape for k in SMALL]
    dsm, nmsm, nvsm = adamw(_pack_small([given[k] for k in SMALL]), _pack_small([gsmall[k] for k in SMALL]),
                            _pack_small([given["m_" + k] for k in SMALL]), _pack_small([given["v_" + k] for k in SMALL]),
                            name="adamw_small")
    out_g.update(gsmall)
    for dst, src in ((out_d, dsm), (out_m, nmsm), (out_v, nvsm)):
        dst.update(zip(SMALL, _unpack_small(src, local_shapes)))

    return (small_full["loss"].reshape(()), gx[None],
            *[out_g[k] for k in WEIGHTS], *[out_d[k] for k in WEIGHTS],
            *[out_m[k] for k in WEIGHTS], *[out_v[k] for k in WEIGHTS])
```

```python
import math

import jax
import jax.numpy as jnp
from jax import lax
from jax.experimental import pallas as pl
from jax.experimental.pallas import tpu as pltpu

F32 = jnp.float32
BF16 = jnp.bfloat16
EPS = 1e-6
N_DEV = 8
SB_HEAD_DIM = 64
SB_FWD_QUERY_BLOCKS = 4
SB_BWD_QUERY_BLOCKS = 2
SSM_STATE = 128
SSM_CONV = 4
ADAM_LR = 0.001
ADAM_B1 = 0.9
ADAM_B2 = 0.999
ADAM_EPS = 1e-08
ADAM_WD = 0.01
ADAM_STEP = 10
VMEM_LIMIT_BYTES = 56 * 1024 * 1024
MATMUL_VMEM_BUDGET = 40 * 1024 * 1024
LANES = 128
PACK_COLS = 1024

BIG = ("sb_w_qkv", "sb_w_o", "gm_w_in", "gm_w_out", "ssm_w_in", "ssm_w_out", "ffn_w_gu", "ffn_w_down")
COL_SHARDED = ("sb_w_qkv", "gm_w_in", "ssm_w_in", "ffn_w_gu")
SMALL = ("mix_norm", "ffn_norm", "sb_q_gain", "sb_k_gain", "gm_b_in", "gm_v_gain", "gm_w_s", "gm_b_s",
         "ssm_conv_w", "ssm_conv_b", "ssm_dt_bias", "ssm_a_log", "ssm_d", "ssm_norm_gain")
SMALL_SHARDED = ("ssm_conv_w", "ssm_conv_b", "ssm_norm_gain")
WEIGHTS = ("mix_norm", "ffn_norm", "sb_w_qkv", "sb_q_gain", "sb_k_gain", "sb_w_o", "gm_w_in", "gm_b_in",
           "gm_v_gain", "gm_w_s", "gm_b_s", "gm_w_out", "ssm_w_in", "ssm_conv_w", "ssm_conv_b", "ssm_dt_bias",
           "ssm_a_log", "ssm_d", "ssm_norm_gain", "ssm_w_out", "ffn_w_gu", "ffn_w_down")


def _params(semantics=None):
    return pltpu.CompilerParams(dimension_semantics=semantics, vmem_limit_bytes=VMEM_LIMIT_BYTES)


def _pick(n, prefs):
    for t in prefs:
        if t <= n and n % t == 0:
            return t
    return n


def _dot(a, b, ca=1, cb=0):
    return lax.dot_general(a, b, (((ca,), (cb,)), ((), ())), preferred_element_type=F32)


def _split3(v):
    h1 = v.astype(BF16)
    r1 = v - h1.astype(F32)
    h2 = r1.astype(BF16)
    h3 = (r1 - h2.astype(F32)).astype(BF16)
    return h1, h2, h3


def _dot_exact_left(mat01, v):
    h1, h2, h3 = _split3(v)
    return _dot(mat01, h1) + _dot(mat01, h2) + _dot(mat01, h3)


def _dot_split2_right(v, mat01):
    hi = v.astype(BF16)
    lo = (v - hi.astype(F32)).astype(BF16)
    return _dot(hi, mat01) + _dot(lo, mat01)


def _sum_all(v):
    return jnp.sum(jnp.sum(v, axis=0, keepdims=True), axis=1, keepdims=True)


def _sigmoid(v):
    return 1.0 / (1.0 + jnp.exp(-v))


def _softplus(v):
    return jnp.maximum(v, 0.0) + jnp.log(1.0 + jnp.exp(-jnp.abs(v)))


def _erf(v):
    a = jnp.abs(v)
    t = 1.0 / (1.0 + 0.3275911 * a)
    poly = t * (0.254829592 + t * (-0.284496736 + t * (1.421413741 + t * (-1.453152027 + t * 1.061405429))))
    e = 1.0 - poly * jnp.exp(-a * a)
    return jnp.where(v < 0, -e, e)


def _gelu_and_grad(v):
    cdf = 0.5 * (1.0 + _erf(v * (1.0 / math.sqrt(2.0))))
    pdf = jnp.exp(-0.5 * v * v) * (1.0 / math.sqrt(2.0 * math.pi))
    return v * cdf, cdf + v * pdf


def matmul(a, b, *, ta=False, tb=False, out_dtype=F32, residual=None, a_split=1, b_split=1, name):
    if a_split > 1 and ta:
        assert a.shape[0] == a_split
        K, M = a.shape[1], a_split * a.shape[2]
    elif a_split > 1:
        assert a.shape[0] == a_split
        M, K = a.shape[1], a_split * a.shape[2]
    elif ta:
        K, M = a.shape
    else:
        M, K = a.shape
    if b_split > 1:
        assert not tb and b.shape[0] == b_split
        Kb, N = b.shape[1], b_split * b.shape[2]
    elif tb:
        N, Kb = b.shape
    else:
        Kb, N = b.shape
    assert K == Kb, (a.shape, b.shape, ta, tb)
    has_res = residual is not None
    tm = _pick(M // a_split if ta else M, (1024, 1408, 768, 512, 256, 128))
    tn = _pick(N // b_split, (1024, 1408, 1536, 768, 512, 256, 128))

    def vmem_bytes(tk):
        tiles = tm * tk * a.dtype.itemsize + tk * tn * b.dtype.itemsize
        outs = tm * tn * jnp.dtype(out_dtype).itemsize + (tm * tn * 4 if has_res else 0)
        return 2 * tiles + 2 * outs + (tm * tn * 4 if tk < K else 0)

    kp = K if ta else K // a_split
    tk = next((t for t in (K, 2048, 1408, 1024, 512, 256) if t <= kp and kp % t == 0 and vmem_bytes(t) <= MATMUL_VMEM_BUDGET),
              _pick(kp, (128,)))
    nk = K // tk
    if a_split > 1 and ta:
        nib = M // a_split // tm
        a_spec = pl.BlockSpec((None, tk, tm), lambda i, j, k: (i // nib, k, i % nib))
    elif a_split > 1:
        nkb = kp // tk
        a_spec = pl.BlockSpec((None, tm, tk), lambda i, j, k: (k // nkb, i, k % nkb))
    else:
        a_spec = pl.BlockSpec((tk, tm), lambda i, j, k: (k, i)) if ta else pl.BlockSpec((tm, tk), lambda i, j, k: (i, k))
    if b_split > 1:
        njb = N // b_split // tn
        b_spec = pl.BlockSpec((None, tk, tn), lambda i, j, k: (j // njb, k, j % njb))
    else:
        b_spec = pl.BlockSpec((tn, tk), lambda i, j, k: (j, k)) if tb else pl.BlockSpec((tk, tn), lambda i, j, k: (k, j))
    o_spec = pl.BlockSpec((tm, tn), lambda i, j, k: (i, j))
    ca, cb = (0 if ta else 1), (1 if tb else 0)

    def body(*refs):
        a_ref, b_ref = refs[:2]
        r_ref = refs[2] if has_res else None
        o_ref = refs[3] if has_res else refs[2]

        def finish(r):
            if has_res:
                r = r + r_ref[...]
            o_ref[...] = r.astype(out_dtype)

        def part():
            return _dot(a_ref[...].astype(BF16), b_ref[...].astype(BF16), ca, cb)

        if nk == 1:
            finish(part())
            return
        acc = refs[-1]
        k = pl.program_id(2)

        @pl.when(k == 0)
        def _():
            acc[...] = part()

        @pl.when(jnp.logical_and(k > 0, k < nk - 1))
        def _():
            acc[...] += part()

        @pl.when(k == nk - 1)
        def _():
            finish(acc[...] + part())

    in_specs = [a_spec, b_spec] + ([o_spec] if has_res else [])
    args = (a, b) + ((residual,) if has_res else ())
    return pl.pallas_call(
        body, name=name, grid=(M // tm, N // tn, nk), in_specs=in_specs, out_specs=o_spec,
        out_shape=jax.ShapeDtypeStruct((M, N), out_dtype),
        scratch_shapes=[pltpu.VMEM((tm, tn), F32)] if nk > 1 else [],
        compiler_params=_params(("parallel", "parallel", "arbitrary")),
    )(*args)


def rms_fwd(x, gain, *, name):
    S, D = x.shape
    tr = _pick(S, (512, 256, 128))

    def body(x_ref, g_ref, o_ref):
        xv = x_ref[...]
        r = lax.rsqrt(jnp.mean(xv * xv, axis=1, keepdims=True) + EPS)
        o_ref[...] = (xv * r * g_ref[...]).astype(BF16)

    return pl.pallas_call(
        body, name=name, grid=(S // tr,),
        in_specs=[pl.BlockSpec((tr, D), lambda i: (i, 0)), pl.BlockSpec((1, D), lambda i: (0, 0))],
        out_specs=pl.BlockSpec((tr, D), lambda i: (i, 0)), out_shape=jax.ShapeDtypeStruct((S, D), BF16),
        compiler_params=_params(("parallel",)),
    )(x, gain)


def rms_bwd(x, gain, dh, dres, *, name):
    S, D = x.shape
    tr = _pick(S, (512, 256, 128))

    def body(x_ref, g_ref, dh_ref, dr_ref, dx_ref, dg_ref):
        @pl.when(pl.program_id(0) == 0)
        def _():
            dg_ref[...] = jnp.zeros_like(dg_ref)

        xv = x_ref[...]
        dhv = dh_ref[...]
        r = lax.rsqrt(jnp.mean(xv * xv, axis=1, keepdims=True) + EPS)
        xhat = xv * r
        t = dhv * g_ref[...]
        dx_ref[...] = dr_ref[...] + r * (t - xhat * jnp.mean(xhat * t, axis=1, keepdims=True))
        dg_ref[...] += jnp.sum(dhv * xhat, axis=0, keepdims=True)

    row = pl.BlockSpec((tr, D), lambda i: (i, 0))
    vec = pl.BlockSpec((1, D), lambda i: (0, 0))
    return pl.pallas_call(
        body, name=name, grid=(S // tr,), in_specs=[row, vec, row, row], out_specs=[row, vec],
        out_shape=[jax.ShapeDtypeStruct((S, D), F32), jax.ShapeDtypeStruct((1, D), F32)],
        compiler_params=_params(("arbitrary",)),
    )(x, gain, dh, dres)


def ffn_up_fwd(h, w_gu_t, *, name):
    S, K = h.shape
    F = w_gu_t.shape[0] // 2
    tm = _pick(S, (512, 256, 128))
    tn = _pick(F, (1408, 1024, 768, 512, 256, 128))
    nj = F // tn

    def body(h_ref, wg_ref, wu_ref, gu_ref, a_ref):
        hv = h_ref[...]
        g = _dot(hv, wg_ref[...], 1, 1)
        u = _dot(hv, wu_ref[...], 1, 1)
        gu_ref[0] = g
        gu_ref[1] = u
        a_ref[...] = (g * _sigmoid(g) * u).astype(BF16)

    return pl.pallas_call(
        body, name=name, grid=(nj, S // tm),
        in_specs=[pl.BlockSpec((tm, K), lambda j, i: (i, 0)), pl.BlockSpec((tn, K), lambda j, i: (j, 0)),
                  pl.BlockSpec((tn, K), lambda j, i: (nj + j, 0))],
        out_specs=[pl.BlockSpec((2, tm, tn), lambda j, i: (0, i, j)), pl.BlockSpec((tm, tn), lambda j, i: (i, j))],
        out_shape=[jax.ShapeDtypeStruct((2, S, F), F32), jax.ShapeDtypeStruct((S, F), BF16)],
        compiler_params=_params(("parallel", "parallel")),
    )(h, w_gu_t, w_gu_t)


def ffn_up_bwd(dy, w_down, gu, *, name):
    S, D = dy.shape
    F = w_down.shape[0]
    tm = _pick(S, (512, 256, 128))
    tn = _pick(F, (1408, 1024, 768, 512, 256, 128))

    def body(dy_ref, wd_ref, gu_ref, o_ref):
        da = _dot(dy_ref[...].astype(BF16), wd_ref[...], 1, 1)
        g = gu_ref[0]
        u = gu_ref[1]
        s = _sigmoid(g)
        o_ref[0] = (da * u * (s * (1.0 + g * (1.0 - s)))).astype(BF16)
        o_ref[1] = (da * g * s).astype(BF16)

    pair = pl.BlockSpec((2, tm, tn), lambda j, i: (0, i, j))
    return pl.pallas_call(
        body, name=name, grid=(F // tn, S // tm),
        in_specs=[pl.BlockSpec((tm, D), lambda j, i: (i, 0)), pl.BlockSpec((tn, D), lambda j, i: (j, 0)), pair],
        out_specs=pair, out_shape=jax.ShapeDtypeStruct((2, S, F), BF16),
        compiler_params=_params(("parallel", "parallel")),
    )(dy, w_down, gu)


def loss_head(y, target, *, name):
    S, D = y.shape
    tr = _pick(S, (512, 256, 128))

    def body(y_ref, t_ref, dy_ref, l_ref):
        @pl.when(pl.program_id(0) == 0)
        def _():
            l_ref[...] = jnp.zeros_like(l_ref)

        err = y_ref[...] - t_ref[...]
        dy_ref[...] = err * (1.0 / D)
        l_ref[...] += jnp.sum(0.5 * jnp.mean(err * err, axis=1, keepdims=True), axis=0, keepdims=True)

    row = pl.BlockSpec((tr, D), lambda i: (i, 0))
    one = pl.BlockSpec((1, 1), lambda i: (0, 0))
    dy, l = pl.pallas_call(
        body, name=name, grid=(S // tr,), in_specs=[row, row], out_specs=[row, one],
        out_shape=[jax.ShapeDtypeStruct((S, D), F32), jax.ShapeDtypeStruct((1, 1), F32)],
        compiler_params=_params(("arbitrary",)),
    )(y, target)
    return dy, l


def adamw(w, g, m, v, *, name):
    R, C = w.shape[-2:]
    tr = _pick(R, (512, 256, 128, 64, 32, 16, 8))

    def body(w_ref, g_ref, m_ref, v_ref, d_ref, mo_ref, vo_ref):
        gv = g_ref[...]
        mn = ADAM_B1 * m_ref[...] + (1.0 - ADAM_B1) * gv
        vn = ADAM_B2 * v_ref[...] + (1.0 - ADAM_B2) * jnp.square(gv)
        m_hat = mn / (1.0 - ADAM_B1 ** ADAM_STEP)
        v_hat = vn / (1.0 - ADAM_B2 ** ADAM_STEP)
        d_ref[...] = -ADAM_LR * (m_hat / (jnp.sqrt(v_hat) + ADAM_EPS) + ADAM_WD * w_ref[...])
        mo_ref[...] = mn
        vo_ref[...] = vn

    if w.ndim == 3:
        grid = (w.shape[0], R // tr)
        blk = pl.BlockSpec((None, tr, C), lambda l, i: (l, i, 0))
    else:
        grid = (R // tr,)
        blk = pl.BlockSpec((tr, C), lambda i: (i, 0))
    sds = jax.ShapeDtypeStruct(w.shape, F32)
    return pl.pallas_call(
        body, name=name, grid=grid, in_specs=[blk] * 4, out_specs=[blk] * 3, out_shape=[sds] * 3,
        compiler_params=_params(("parallel",) * len(grid)),
    )(w, g, m, v)


def _tri(n, kind):
    r = lax.broadcasted_iota(jnp.int32, (n, n), 0)
    c = lax.broadcasted_iota(jnp.int32, (n, n), 1)
    if kind == "row_gt_col":
        return (r > c).astype(BF16)
    if kind == "row_ge_col":
        return (r >= c).astype(BF16)
    if kind == "row_le_col":
        return (r <= c).astype(BF16)
    raise ValueError(kind)


def _sb_tile(qs, kj, r_carry, u_strict, masked):
    z = _dot(qs, kj, 1, 1)
    lb = jnp.minimum(z, 0.0) - jnp.log(1.0 + jnp.exp(-jnp.abs(z)))
    l1m = lb - z
    keep = None
    if masked:
        tq, tk = z.shape
        keep = lax.broadcasted_iota(jnp.int32, (tq, tk), 1) < lax.broadcasted_iota(jnp.int32, (tq, tk), 0)
        l1m = jnp.where(keep, l1m, 0.0)
    w = jnp.exp(lb + _dot(l1m.astype(BF16), u_strict) + r_carry)
    if masked:
        w = jnp.where(keep, w, 0.0)
    return lb, l1m, w, keep


def _sb_prep(T, nb, hd, refs_in, gains, scratch):
    q_scale = 1.0 / math.sqrt(hd)
    assert math.log2(q_scale) == round(math.log2(q_scale))

    def prep(i, _):
        rows = pl.ds(pl.multiple_of(i * T, T), T)
        for hh in range(2):
            sl = slice(hd * hh, hd * hh + hd)
            for n, (src, dst) in enumerate(zip(refs_in, scratch)):
                v = src[rows, sl]
                if n < 2:
                    v = v * lax.rsqrt(jnp.mean(v * v, axis=1, keepdims=True) + EPS) * gains[n][...]
                if n == 0:
                    v = v * q_scale
                dst[hh, rows, :] = v.astype(BF16)
        return 0

    lax.fori_loop(0, nb, prep, 0)


def _sb_chains(m, T, nq):
    rows = [pl.ds(pl.multiple_of((nq * m + qb) * T, T), T) for qb in range(nq)]
    return rows, [(hh, qb) for qb in range(nq) for hh in range(2)]


def _sb_sweep(tile, carry, chains, m, nq):
    for kk in reversed(range(nq)):
        carry = tile(nq * m + kk, carry, [(ch, ch[1] == kk) for ch in chains if ch[1] >= kk])
    return lax.fori_loop(0, nq * m, lambda jj, c: tile(nq * m - 1 - jj, c, [(ch, False) for ch in chains]), carry)


def sb_attn_fwd(qkv, q_gain, k_gain, *, name, gather=None):
    S, D3 = qkv.shape
    D = D3 // 3
    npairs = D // LANES
    hd = SB_HEAD_DIM
    T = min(256, S)
    nb = S // T
    nq = SB_FWD_QUERY_BLOCKS
    assert nb % nq == 0

    def body(*refs):
        if gather is None:
            q_ref, k_ref, v_ref, qg_ref, kg_ref, us_ref, o_ref, qn_s, kn_s, vb_s = refs
        else:
            q_ref, k_ref, v_ref, qg_ref, kg_ref, us_ref, ag_in, o_ref, ag_out, qn_s, kn_s, vb_s = refs[:12]
            comm = (ag_in, ag_out) + refs[12:]
            step = pl.program_id(0)
            pl.when(step == 0)(lambda: _ag_phase(comm, "start"))
            pl.when(step == npairs - 1)(lambda: _ag_phase(comm, "forward"))
        us = us_ref[...]
        _sb_prep(T, nb, hd, (q_ref, k_ref, v_ref), (qg_ref, kg_ref), (qn_s, kn_s, vb_s))

        def superblock(m, _):
            rows_q, chains = _sb_chains(m, T, nq)
            qs = {ch: qn_s[ch[0], rows_q[ch[1]], :] for ch in chains}

            def tile(j, carry, which):
                rows_j = pl.ds(pl.multiple_of(j * T, T), T)
                new = dict(carry)
                for ch, masked in which:
                    acc, rc = carry[ch]
                    _, l1m, w, _ = _sb_tile(qs[ch], kn_s[ch[0], rows_j, :], rc, us, masked)
                    new[ch] = (acc + _dot(w.astype(BF16), vb_s[ch[0], rows_j, :]),
                               rc + jnp.sum(l1m, axis=1, keepdims=True))
                return new

            carry = {ch: (jnp.zeros((T, hd), F32), jnp.zeros((T, 1), F32)) for ch in chains}
            carry = _sb_sweep(tile, carry, chains, m, nq)
            for qb in range(nq):
                o_ref[rows_q[qb], :] = jnp.concatenate([carry[(0, qb)][0], carry[(1, qb)][0]], axis=1)
            return 0

        lax.fori_loop(0, nb // nq, superblock, 0)
        if gather is not None:
            pl.when(step == npairs - 1)(lambda: _ag_phase(comm, "finish"))

    col = lambda off: pl.BlockSpec((S, LANES), lambda p, off=off: (0, off + p))
    gain = pl.BlockSpec((1, hd), lambda p: (0, 0))
    in_specs = [col(0), col(npairs), col(2 * npairs), gain, gain, pl.BlockSpec((T, T), lambda p: (0, 0))]
    out_specs = [pl.BlockSpec((S, LANES), lambda p: (0, p))]
    out_shape = [jax.ShapeDtypeStruct((S, D), F32)]
    scratch = [pltpu.VMEM((2, S, hd), BF16)] * 3
    args = [qkv, qkv, qkv, q_gain, k_gain, _tri(T, "row_gt_col")]
    if gather is not None:
        in_specs.append(HBM_SPEC)
        out_specs.append(HBM_SPEC)
        out_shape.append(jax.ShapeDtypeStruct((N_DEV,) + gather.shape, gather.dtype))
        scratch += COMM_SEMAPHORES
        args.append(gather)
    out = pl.pallas_call(
        body, name=name, grid=(npairs,), in_specs=in_specs, out_specs=out_specs, out_shape=out_shape,
        scratch_shapes=scratch, compiler_params=_params(("arbitrary",)),
    )(*args)
    return out[0] if gather is None else tuple(out)


def sb_attn_bwd(qkv, o, do, q_gain, k_gain, *, name, scatter=None):
    S, D3 = qkv.shape
    D = D3 // 3
    npairs = D // LANES
    hd = SB_HEAD_DIM
    T = min(256, S)
    nb = S // T
    nq = SB_BWD_QUERY_BLOCKS
    assert nb % nq == 0
    scale = 1.0 / math.sqrt(hd)

    def body(*refs):
        if scatter is None:
            (q_ref, k_ref, v_ref, o_ref, do_ref, qg_ref, kg_ref, us_ref,
             dqkv_ref, dg_ref, qn_s, kn_s, vb_s, dob_s) = refs
        else:
            ns = len(scatter)
            q_ref, k_ref, v_ref, o_ref, do_ref, qg_ref, kg_ref, us_ref = refs[:8]
            rs_in = refs[8:8 + ns]
            dqkv_ref, dg_ref = refs[8 + ns:10 + ns]
            rs_out = refs[10 + ns:10 + 2 * ns]
            qn_s, kn_s, vb_s, dob_s = refs[10 + 2 * ns:14 + 2 * ns]
            rs_sems = refs[14 + 2 * ns:]
            pl.when(pl.program_id(0) == 0)(lambda: _rs_phase(rs_in, rs_out, rs_sems, "start"))
        dq_ref, dk_ref, dv_ref = dqkv_ref.at[0], dqkv_ref.at[1], dqkv_ref.at[2]

        @pl.when(pl.program_id(0) == 0)
        def _():
            dg_ref[...] = jnp.zeros_like(dg_ref)

        us = us_ref[...]
        u_prefix = (1.0 - us.astype(F32)).astype(BF16)
        _sb_prep(T, nb, hd, (q_ref, k_ref, v_ref, do_ref), (qg_ref, kg_ref), (qn_s, kn_s, vb_s, dob_s))
        dk_ref[...] = jnp.zeros_like(dk_ref)
        dv_ref[...] = jnp.zeros_like(dv_ref)

        def superblock(m, _):
            rows_q, chains = _sb_chains(m, T, nq)
            qs = {ch: qn_s[ch[0], rows_q[ch[1]], :] for ch in chains}
            doi ={ch: dob_s[ch[0], rows_q[ch[1]], :] for ch in chains}
            dt_total = {ch: jnp.sum(doi[ch].astype(F32) * o_ref[rows_q[ch[1]], hd * ch[0]:hd * ch[0] + hd],
                                    axis=1, keepdims=True) for ch in chains}

            def tile(j, carry, which):
                rows_j = pl.ds(pl.multiple_of(j * T, T), T)
                new = dict(carry)
                dk_part, dv_part = {}, {}
                for ch, masked in which:
                    hh = ch[0]
                    dq_acc, rc, gc = carry[ch]
                    kj = kn_s[hh, rows_j, :]
                    lb, l1m, w, keep = _sb_tile(qs[ch], kj, rc, us, masked)
                    wb = w.astype(BF16)
                    g = _dot(doi[ch], vb_s[hh, rows_j, :], 1, 1) * wb.astype(F32)
                    g_row = jnp.sum(g, axis=1, keepdims=True)
                    g_upto = (dt_total[ch] - gc - g_row) + _dot(g.astype(BF16), u_prefix)
                    dz = g - g_upto * jnp.exp(lb)
                    if masked:
                        dz = jnp.where(keep, dz, 0.0)
                    dzb = dz.astype(BF16)
                    dv_part[hh] = dv_part.get(hh, 0.0) + _dot(wb, doi[ch], 0, 0)
                    dk_part[hh] = dk_part.get(hh, 0.0) + _dot(dzb, qs[ch], 0, 0)
                    new[ch] = (dq_acc + _dot(dzb, kj), rc + jnp.sum(l1m, axis=1, keepdims=True),
                               gc + g_row)
                dv_ref[rows_j, :] += jnp.concatenate([dv_part[0], dv_part[1]], axis=1)
                dk_ref[rows_j, :] += jnp.concatenate([dk_part[0], dk_part[1]], axis=1)
                return new

            zero1 = jnp.zeros((T, 1), F32)
            carry = {ch: (jnp.zeros((T, hd), F32), zero1, zero1) for ch in chains}
            carry = _sb_sweep(tile, carry, chains, m, nq)
            for qb in range(nq):
                dq_ref[rows_q[qb], :] = jnp.concatenate([carry[(0, qb)][0], carry[(1, qb)][0]], axis=1) * scale
            return 0

        lax.fori_loop(0, nb // nq, superblock, 0)

        def finish(i, carry):
            rows = pl.ds(pl.multiple_of(i * T, T), T)
            new = []
            for hh in range(2):
                sl = slice(hd * hh, hd * hh + hd)
                outs = []
                for raw_ref, gain_ref, dn in ((q_ref, qg_ref, dq_ref[rows, sl]), (k_ref, kg_ref, dk_ref[rows, sl])):
                    raw = raw_ref[rows, sl]
                    r = lax.rsqrt(jnp.mean(raw * raw, axis=1, keepdims=True) + EPS)
                    hat = raw * r
                    t = dn * gain_ref[...]
                    outs.append((r * (t - hat * jnp.mean(hat * t, axis=1, keepdims=True)),
                                 jnp.sum(dn * hat, axis=0, keepdims=True)))
                dq_ref[rows, sl] = outs[0][0]
                dk_ref[rows, sl] = outs[1][0]
                new.append((carry[hh][0] + outs[0][1], carry[hh][1] + outs[1][1]))
            return tuple(new)

        zg = (jnp.zeros((1, hd), F32), jnp.zeros((1, hd), F32))
        tot = lax.fori_loop(0, nb, finish, (zg, zg))
        dg_ref[0:1, 0:hd] += tot[0][0] + tot[1][0]
        dg_ref[1:2, 0:hd] += tot[0][1] + tot[1][1]
        if scatter is not None:
            pl.when(pl.program_id(0) == npairs - 1)(lambda: _rs_phase(rs_in, rs_out, rs_sems, "finish"))

    col = lambda off: pl.BlockSpec((S, LANES), lambda p, off=off: (0, off + p))
    gain = pl.BlockSpec((1, hd), lambda p: (0, 0))
    tri = pl.BlockSpec((T, T), lambda p: (0, 0))
    pair = pl.BlockSpec((S, LANES), lambda p: (0, p))
    in_specs = [col(0), col(npairs), col(2 * npairs), pair, pair, gain, gain, tri]
    out_specs = [pl.BlockSpec((3, S, LANES), lambda p: (0, 0, p)), pl.BlockSpec((8, LANES), lambda p: (0, 0))]
    out_shape = [jax.ShapeDtypeStruct((3, S, D), F32), jax.ShapeDtypeStruct((8, LANES), F32)]
    scratch = [pltpu.VMEM((2, S, hd), BF16)] * 4
    args = [qkv, qkv, qkv, o, do, q_gain, k_gain, _tri(T, "row_gt_col")]
    if scatter is not None:
        in_specs += [HBM_SPEC] * len(scatter)
        out_specs += [HBM_SPEC] * len(scatter)
        out_shape += [jax.ShapeDtypeStruct(g.shape, g.dtype) for g in scatter]
        scratch += _rs_semaphores(len(scatter))
        args += list(scatter)
    out = pl.pallas_call(
        body, name=name, grid=(npairs,), in_specs=in_specs, out_specs=out_specs, out_shape=out_shape,
        scratch_shapes=scratch, compiler_params=_params(("arbitrary",)),
    )(*args)
    res = (out[0], out[1][0:1, :hd], out[1][1:2, :hd])
    return res if scatter is None else res + (list(out[2:]),)


def gmlp_fwd(zzpre, b_in, v_gain, wc, bsf, *, name):
    S, H2 = zzpre.shape
    H = H2 // 2
    G, T, _ = wc.shape
    gd = H // G

    def body(z_ref, b_ref, vg_ref, wc_ref, bs_ref, p_ref):
        zz, _ = _gelu_and_grad(z_ref[...] + b_ref[...])
        u = zz[:, :H]
        v = zz[:, H:]
        vn = v * lax.rsqrt(jnp.mean(v * v, axis=1, keepdims=True) + EPS) * vg_ref[...]
        for g in range(G):
            gs = slice(g * gd, (g + 1) * gd)
            mixed = _dot(wc_ref[g], vn[:, gs].astype(BF16)) + bs_ref[g]
            p_ref[:, gs] = (u[:, gs] * mixed).astype(BF16)

    full3 = lambda shp: pl.BlockSpec(shp, lambda c: (0, 0, 0))
    return pl.pallas_call(
        body, name=name, grid=(S // T,),
        in_specs=[pl.BlockSpec((T, H2), lambda c: (c, 0)), pl.BlockSpec((1, H2), lambda c: (0, 0)),
                  pl.BlockSpec((1, H), lambda c: (0, 0)), full3((G, T, T)), full3((G, T, gd))],
        out_specs=pl.BlockSpec((T, H), lambda c: (c, 0)), out_shape=jax.ShapeDtypeStruct((S, H), BF16),
        compiler_params=_params(("parallel",)),
    )(zzpre, b_in, v_gain, wc, bsf)


def gmlp_bwd(zzpre, b_in, v_gain, wc, bsf, dp, *, name):
    S, H2 = zzpre.shape
    H = H2 // 2
    G, T, _ = wc.shape
    gd = H // G
    assert G <= LANES

    def body(z_ref, b_ref, vg_ref, wc_ref, bs_ref, dp_ref, dzz_ref, db_ref, dvg_ref, dws_ref, dbs_ref):
        @pl.when(pl.program_id(0) == 0)
        def _():
            db_ref[...] = jnp.zeros_like(db_ref)
            dvg_ref[...] = jnp.zeros_like(dvg_ref)
            dws_ref[...] = jnp.zeros_like(dws_ref)
            dbs_ref[...] = jnp.zeros_like(dbs_ref)

        zz, gp = _gelu_and_grad(z_ref[...] + b_ref[...])
        u = zz[:, :H]
        v = zz[:, H:]
        r = lax.rsqrt(jnp.mean(v * v, axis=1, keepdims=True) + EPS)
        vhat = v * r
        vg = vg_ref[...]
        vn = vhat * vg
        dpv = dp_ref[...]
        tril = lax.broadcasted_iota(jnp.int32, (T, T), 1) <= lax.broadcasted_iota(jnp.int32, (T, T), 0)
        lane = lax.broadcasted_iota(jnp.int32, (T, LANES), 1)
        dbs = jnp.zeros((T, LANES), F32)
        du_parts, dvn_parts = [], []
        for g in range(G):
            gs = slice(g * gd, (g + 1) * gd)
            vng = vn[:, gs].astype(BF16)
            wcg = wc_ref[g]
            mixed = _dot(wcg, vng) + bs_ref[g]
            dpg = dpv[:, gs]
            du_parts.append(dpg * mixed)
            dmx = dpg * u[:, gs]
            dmxb = dmx.astype(BF16)
            dvn_parts.append(_dot(wcg, dmxb, 0, 0))
            dws_ref[g] += jnp.where(tril, _dot(dmxb, vng, 1, 1), 0.0)
            dbs = dbs + jnp.where(lane == g, jnp.sum(dmx, axis=1, keepdims=True), 0.0)
        dbs_ref[...] += dbs
        du = jnp.concatenate(du_parts, axis=1)
        dvn = jnp.concatenate(dvn_parts, axis=1)
        dvg_ref[...] += jnp.sum(dvn * vhat, axis=0, keepdims=True)
        t = dvn * vg
        dv = r * (t - vhat * jnp.mean(vhat * t, axis=1, keepdims=True))
        dzu = du * gp[:, :H]
        dzv = dv * gp[:, H:]
        dzz_ref[:, :H] = dzu.astype(BF16)
        dzz_ref[:, H:] = dzv.astype(BF16)
        db_ref[:, :H] += jnp.sum(dzu, axis=0, keepdims=True)
        db_ref[:, H:] += jnp.sum(dzv, axis=0, keepdims=True)

    full3 = lambda shp: pl.BlockSpec(shp, lambda c: (0, 0, 0))
    vec = lambda n: pl.BlockSpec((1, n), lambda c: (0, 0))
    return pl.pallas_call(
        body, name=name, grid=(S // T,),
        in_specs=[pl.BlockSpec((T, H2), lambda c: (c, 0)), vec(H2), vec(H), full3((G, T, T)), full3((G, T, gd)),
                  pl.BlockSpec((T, H), lambda c: (c, 0))],
        out_specs=[pl.BlockSpec((T, H2), lambda c: (c, 0)), vec(H2), vec(H), full3((G, T, T)),
                   pl.BlockSpec((T, LANES), lambda c: (0, 0))],
        out_shape=[jax.ShapeDtypeStruct((S, H2), BF16), jax.ShapeDtypeStruct((1, H2), F32),
                   jax.ShapeDtypeStruct((1, H), F32), jax.ShapeDtypeStruct((G, T, T), F32),
                   jax.ShapeDtypeStruct((T, LANES), F32)],
        compiler_params=_params(("arbitrary",)),
    )(zzpre, b_in, v_gain, wc, bsf, dp)


def _shift_rows(v, k, n_rows):
    if k == 0:
        return v
    rolled = pltpu.roll(v, k % n_rows, 0)
    row = lax.broadcasted_iota(jnp.int32, v.shape, 0)
    keep = (row >= k) if k > 0 else (row < n_rows + k)
    return jnp.where(keep, rolled, 0.0)


def conv_fwd(zx, conv_w, conv_b, col0, *, name):
    S = zx.shape[0]
    C = conv_w.shape[1]
    tc = _pick(C, (256, 128))
    off = col0 // tc
    assert col0 % tc == 0

    def body(x_ref, w_ref, b_ref, o_ref):
        xv = x_ref[...]
        acc = b_ref[...] + w_ref[SSM_CONV - 1:SSM_CONV, :] * xv
        for k in range(SSM_CONV - 1):
            acc = acc + w_ref[k:k + 1, :] * _shift_rows(xv, SSM_CONV - 1 - k, S)
        o_ref[...] = acc * _sigmoid(acc)

    return pl.pallas_call(
        body, name=name, grid=(C // tc,),
        in_specs=[pl.BlockSpec((S, tc), lambda j: (0, off + j)), pl.BlockSpec((SSM_CONV, tc), lambda j: (0, j)),
                  pl.BlockSpec((1, tc), lambda j: (0, j))],
        out_specs=pl.BlockSpec((S, tc), lambda j: (0, j)), out_shape=jax.ShapeDtypeStruct((S, C), F32),
        compiler_params=_params(("parallel",)),
    )(zx, conv_w, conv_b)


def conv_bwd(zx, conv_w, conv_b, col0, dout, *, name):
    S = zx.shape[0]
    C = conv_w.shape[1]
    tc = _pick(C, (256, 128))
    off = col0 // tc

    def body(x_ref, w_ref, b_ref, do_ref, dx_ref, dw_ref, db_ref):
        xv = x_ref[...]
        shifted = [_shift_rows(xv, SSM_CONV - 1 - k, S) for k in range(SSM_CONV)]
        acc = b_ref[...]
        for k in range(SSM_CONV):
            acc = acc + w_ref[k:k + 1, :] * shifted[k]
        s = _sigmoid(acc)
        dacc = do_ref[...] * (s * (1.0 + acc * (1.0 - s)))
        db_ref[...] = jnp.sum(dacc, axis=0, keepdims=True)
        dx = jnp.zeros_like(xv)
        for k in range(SSM_CONV):
            dw_ref[k:k + 1, :] = jnp.sum(dacc * shifted[k], axis=0, keepdims=True)
            dx = dx + w_ref[k:k + 1, :] * _shift_rows(dacc, -(SSM_CONV - 1 - k), S)
        dx_ref[...] = dx

    slab = pl.BlockSpec((S, tc), lambda j: (0, j))
    return pl.pallas_call(
        body, name=name, grid=(C // tc,),
        in_specs=[pl.BlockSpec((S, tc), lambda j: (0, off + j)), pl.BlockSpec((SSM_CONV, tc), lambda j: (0, j)),
                  pl.BlockSpec((1, tc), lambda j: (0, j)), slab],
        out_specs=[slab, pl.BlockSpec((SSM_CONV, tc), lambda j: (0, j)), pl.BlockSpec((1, tc), lambda j: (0, j))],
        out_shape=[jax.ShapeDtypeStruct((S, C), F32), jax.ShapeDtypeStruct((SSM_CONV, C), F32),
                   jax.ShapeDtypeStruct((1, C), F32)],
        compiler_params=_params(("parallel",)),
    )(zx, conv_w, conv_b, dout)


def _ssd_chunk_terms(dtraw, bias, a_log, tl):
    dt = _softplus(dtraw + bias)
    a_neg = -jnp.exp(a_log)
    ac = _dot_exact_left(tl, dt * a_neg)
    ac_last = ac[ac.shape[0] - 1:, :]
    return dt, a_neg, ac, ac.T, jnp.exp(ac), jnp.exp(ac_last - ac), jnp.exp(ac_last)


def _ssd_specs(S, L, G, hpg, pd, inner):
    gw = hpg * pd
    n = SSM_STATE
    xb = inner // n

    def mk(cidx):
        return dict(
            x=pl.BlockSpec((L, gw), lambda g, c: (cidx(c), g)),
            b=pl.BlockSpec((L, n), lambda g, c: (cidx(c), xb + g)),
            c=pl.BlockSpec((L, n), lambda g, c: (cidx(c), xb + G + g)),
            z=pl.BlockSpec((L, gw), lambda g, c: (cidx(c), g)),
            dt=pl.BlockSpec((L, LANES), lambda g, c: (cidx(c), g)),
            gvec=pl.BlockSpec((1, 1, LANES), lambda g, c: (g, 0, 0)),
            chan=pl.BlockSpec((1, gw), lambda g, c: (0, g)),
            tri=pl.BlockSpec((L, L), lambda g, c: (0, 0)),
            hp=pl.BlockSpec((1, 1, gw, n), lambda g, c: (g, cidx(c), 0, 0)),
            bc=pl.BlockSpec((L, n), lambda g, c: (cidx(c), g)),
        )
    return mk


def ssd_fwd(xbc, zx, dtg, bias_g, alog_g, d_chan, ngain, L, G, *, name):
    S = xbc.shape[0]
    n = SSM_STATE
    inner = xbc.shape[1] - 2 * G * n
    gw = inner // G
    pd = SB_HEAD_DIM
    hpg = gw // pd
    nc = S // L
    sp = _ssd_specs(S, L, G, hpg, pd, inner)(lambda c: c)

    def body(x_ref, b_ref, c_ref, z_ref, dt_ref, bias_ref, alog_ref, d_ref, ng_ref, tl_ref,
             yn_ref, y_ref, hp_ref, state):
        @pl.when(pl.program_id(1) == 0)
        def _():
            state[...] = jnp.zeros_like(state)

        dt, _, ac, act, ea, dte, cd = _ssd_chunk_terms(dt_ref[...], bias_ref[0], alog_ref[0], tl_ref[...])
        xv = x_ref[...]
        bm = b_ref[...].astype(BF16)
        cm = c_ref[...].astype(BF16)
        cb = _dot(cm, bm, 1, 1)
        tril = lax.broadcasted_iota(jnp.int32, (L, L), 1) <= lax.broadcasted_iota(jnp.int32, (L, L), 0)
        hp_ref[0, 0] = state[...]
        for r in range(hpg):
            ps = slice(r * pd, (r + 1) * pd)
            xr = xv[:, ps]
            xdt = xr * dt[:, r:r + 1]
            lm = jnp.exp(jnp.where(tril, ac[:, r:r + 1] - act[r:r + 1, :], -jnp.inf))
            hprev = state[ps, :]
            y = _dot((cb * lm).astype(BF16), xdt.astype(BF16))
            y = y + _dot(cm, hprev.astype(BF16), 1, 1) * ea[:, r:r + 1]
            y_ref[:, ps] = y + xr * d_ref[:, ps]
            st = _dot((xdt * dte[:, r:r + 1]).astype(BF16), bm, 0, 0)
            state[ps, :] = hprev * cd[:, r:r + 1] + st
        yfull = y_ref[...]
        zg = z_ref[...]
        yg = yfull * (zg * _sigmoid(zg))
        yn_ref[...] = (yg * lax.rsqrt(jnp.mean(yg * yg, axis=1, keepdims=True) + EPS) * ng_ref[...]).astype(BF16)

    return pl.pallas_call(
        body, name=name, grid=(G, nc),
        in_specs=[sp["x"], sp["b"], sp["c"], sp["z"], sp["dt"], sp["gvec"], sp["gvec"], sp["chan"], sp["chan"], sp["tri"]],
        out_specs=[sp["x"], sp["x"], sp["hp"]],
        out_shape=[jax.ShapeDtypeStruct((S, inner), BF16), jax.ShapeDtypeStruct((S, inner), F32),
                   jax.ShapeDtypeStruct((G, nc, gw, n), F32)],
        scratch_shapes=[pltpu.VMEM((gw, n), F32)],
        compiler_params=_params(("arbitrary", "arbitrary")),
    )(xbc, xbc, xbc, zx, dtg, bias_g, alog_g, d_chan, ngain, _tri(L, "row_ge_col"))


def ssd_bwd(xbc, zx, dtg, bias_g, alog_g, d_chan, ngain, yfull, hp, dyn, L, G, *, name):
    S = xbc.shape[0]
    n = SSM_STATE
    inner = xbc.shape[1] - 2 * G * n
    gw = inner // G
    pd = SB_HEAD_DIM
    hpg = gw // pd
    nc = S // L
    sp = _ssd_specs(S, L, G, hpg, pd, inner)(lambda c: nc - 1 - c)

    def body(x_ref, b_ref, c_ref, z_ref, dt_ref, bias_ref, alog_ref, d_ref, ng_ref, tl_ref, tu_ref,
             yf_ref, hp_ref, dyn_ref,
             dz_ref, dx_ref, db_ref, dc_ref, ddt_ref, dbias_ref, dalog_ref, dd_ref, dng_ref, dstate):
        first = pl.program_id(1) == 0

        @pl.when(first)
        def _():
            dstate[...] = jnp.zeros_like(dstate)
            dbias_ref[...] = jnp.zeros_like(dbias_ref)
            dalog_ref[...] = jnp.zeros_like(dalog_ref)
            dd_ref[...] = jnp.zeros_like(dd_ref)
            dng_ref[...] = jnp.zeros_like(dng_ref)

        dtraw = dt_ref[...]
        dt, a_neg, ac, act, ea, dte, cd = _ssd_chunk_terms(dtraw, bias_ref[0], alog_ref[0], tl_ref[...])
        xv = x_ref[...]
        bm = b_ref[...].astype(BF16)
        cm = c_ref[...].astype(BF16)
        cb = _dot(cm, bm, 1, 1)
        tril = lax.broadcasted_iota(jnp.int32, (L, L), 1) <= lax.broadcasted_iota(jnp.int32, (L, L), 0)
        lane = lax.broadcasted_iota(jnp.int32, (L, LANES), 1)
        lane1 = lax.broadcasted_iota(jnp.int32, (1, LANES), 1)

        yfull = yf_ref[...]
        zg = z_ref[...]
        sg = _sigmoid(zg)
        gate = zg * sg
        yg = yfull * gate
        rr = lax.rsqrt(jnp.mean(yg * yg, axis=1, keepdims=True) + EPS)
        yhat = yg * rr
        dynv = dyn_ref[...]
        dng_ref[...] += jnp.sum(dynv * yhat, axis=0, keepdims=True)
        t = dynv * ng_ref[...]
        dyg = rr * (t - yhat * jnp.mean(yhat * t, axis=1, keepdims=True))
        dy = dyg * gate
        dz_ref[...] = dyg * yfull * (sg * (1.0 + zg * (1.0 - sg)))

        dcb = jnp.zeros((L, L), F32)
        dc_acc = jnp.zeros((L, n), F32)
        db_acc = jnp.zeros((L, n), F32)
        dac = jnp.zeros((L, LANES), F32)
        xdx = jnp.zeros((L, LANES), F32)
        tail = jnp.zeros((1, LANES), F32)
        dskip = jnp.zeros((1, LANES), F32)
        ones_l = jnp.ones((L, LANES), BF16)
        for r in range(hpg):
            ps = slice(r * pd, (r + 1) * pd)
            xr = xv[:, ps]
            dyr = dy[:, ps]
            dtr = dt[:, r:r + 1]
            dter = dte[:, r:r + 1]
            cdr = cd[:, r:r + 1]
            xdt = xr * dtr
            xdtb = xdt.astype(BF16)
            dyrb = dyr.astype(BF16)
            lm = jnp.exp(jnp.where(tril, ac[:, r:r + 1] - act[r:r + 1, :], -jnp.inf))
            m32 = cb * lm
            mb = m32.astype(BF16)
            hprev = hp_ref[0, 0, ps, :]
            hpb = hprev.astype(BF16)
            dhn = dstate[ps, :]
            dhnb = dhn.astype(BF16)
            ear = ea[:, r:r + 1]
            gy = (dyr * ear).astype(BF16)
            dc_acc = dc_acc + _dot(gy, hpb)
            dstate[ps, :] = _dot(gy, cm, 0, 0) + dhn * cdr
            bdh = _dot(bm, dhnb, 1, 1)
            db_acc = db_acc + _dot((xdt * dter).astype(BF16), dhnb)
            dm = _dot(dyrb, xdtb, 1, 1)
            dxdt = bdh * dter + _dot(mb, dyrb, 0, 0)
            dcb = dcb + dm * lm
            wmat = dm * m32
            whi = wmat.astype(BF16)
            wlo = (wmat - whi.astype(F32)).astype(BF16)
            col_w = _dot(whi, ones_l, 0, 0) + _dot(wlo, ones_l, 0, 0)
            t_end = xdt * bdh * dter
            e_r = jnp.sum(wmat, axis=1, keepdims=True) \
                + jnp.sum(dyr * _dot(cm, hpb, 1, 1) * ear - t_end, axis=1, keepdims=True)
            c_r = cdr * _sum_all(dhn * hprev) + _sum_all(t_end)
            dac = dac + jnp.where(lane == r, e_r - col_w, 0.0)
            xdx = xdx + jnp.where(lane == r, jnp.sum(dxdt * xr, axis=1, keepdims=True), 0.0)
            tail = tail + jnp.where(lane1 == r, c_r, 0.0)
            dskip = dskip + jnp.where(lane1 == r, _sum_all(dyr * xr), 0.0)
            dx_ref[:, ps] = dxdt * dtr + dyr * d_ref[:, ps]
        dcbb = dcb.astype(BF16)
        dc_ref[...] = dc_acc + _dot(dcbb, bm)
        db_ref[...] = db_acc + _dot(dcbb, cm, 0, 0)
        da = _dot_exact_left(tu_ref[...], dac) + tail
        real = lane < hpg
        ddt = jnp.where(real, (da * a_neg + xdx) * _sigmoid(dtraw + bias_ref[0]), 0.0)
        ddt_ref[...] = ddt
        dd_ref[0] += dskip
        dbias_ref[0] += jnp.sum(ddt, axis=0, keepdims=True)
        dalog_ref[0] += jnp.where(lane1 < hpg, jnp.sum(da * dt, axis=0, keepdims=True) * a_neg, 0.0)

    return pl.pallas_call(
        body, name=name, grid=(G, nc),
        in_specs=[sp["x"], sp["b"], sp["c"], sp["z"], sp["dt"], sp["gvec"], sp["gvec"], sp["chan"], sp["chan"],
                  sp["tri"], sp["tri"], sp["x"], sp["hp"], sp["x"]],
        out_specs=[sp["x"], sp["x"], sp["bc"], sp["bc"], sp["dt"], sp["gvec"], sp["gvec"], sp["gvec"], sp["chan"]],
        out_shape=[jax.ShapeDtypeStruct((S, inner), F32), jax.ShapeDtypeStruct((S, inner), F32),
                   jax.ShapeDtypeStruct((S, G * n), F32), jax.ShapeDtypeStruct((S, G * n), F32),
                   jax.ShapeDtypeStruct((S, G * LANES), F32), jax.ShapeDtypeStruct((G, 1, LANES), F32),
                   jax.ShapeDtypeStruct((G, 1, LANES), F32), jax.ShapeDtypeStruct((G, 1, LANES), F32),
                   jax.ShapeDtypeStruct((1, inner), F32)],
        scratch_shapes=[pltpu.VMEM((gw, n), F32)],
        compiler_params=_params(("arbitrary", "arbitrary")),
    )(xbc, xbc, xbc, zx, dtg, bias_g, alog_g, d_chan, ngain, _tri(L, "row_ge_col"), _tri(L, "row_le_col"),
      yfull, hp, dyn)


def _spread_dt(w_dt_t, G, hpg):
    K = w_dt_t.shape[1]
    w = w_dt_t.reshape(G, hpg, K)
    return jnp.pad(w, ((0, 0), (0, LANES - hpg), (0, 0))).reshape(G * LANES, K)


def _group_vec(v, G, hpg):
    return jnp.pad(v.reshape(G, 1, hpg), ((0, 0), (0, 0), (0, LANES - hpg)))


def local_step(x, target, W, late=None):
    S, D = x.shape
    depth = W["mix_norm"].shape[0]
    gm_groups, gm_chunk = W["gm_w_s"].shape[1], W["gm_w_s"].shape[2]
    heads = W["ssm_dt_bias"].shape[1]
    inner = heads * SB_HEAD_DIM
    L = gm_chunk
    received = None

    saved = []
    for i in range(depth):
        kind, j = i % 3, i // 3
        s = dict(x=x)
        h = rms_fwd(x, W["mix_norm"][i:i + 1], name=f"rms_mix_fwd")
        s["h"] = h
        if kind == 0:
            qkv = matmul(h, W["sb_w_qkv"][j], tb=True, name="mm_qkv")
            if late is not None and i == 0:
                o, gathered = sb_attn_fwd(qkv, W["sb_q_gain"][j:j + 1], W["sb_k_gain"][j:j + 1], name="sb_fwd_gather",
                                          gather=late.shard)
                late.fill(W, gathered)
            else:
                o = sb_attn_fwd(qkv, W["sb_q_gain"][j:j + 1], W["sb_k_gain"][j:j + 1], name="sb_fwd")
            x1 = matmul(o, W["sb_w_o"][j], residual=x, name="mm_sb_out")
            s.update(qkv=qkv, o=o)
        elif kind == 1:
            wc = jnp.where(jnp.tril(jnp.ones((gm_chunk, gm_chunk), bool)), W["gm_w_s"][j], 0.0).astype(BF16)
            bsf = jnp.broadcast_to(W["gm_b_s"][j][:, :, None], (gm_groups, gm_chunk, W["gm_v_gain"].shape[1] // gm_groups)).astype(F32)
            zzpre = matmul(h, W["gm_w_in"][j], tb=True, name="mm_gm_in")
            p = gmlp_fwd(zzpre, W["gm_b_in"][j:j + 1], W["gm_v_gain"][j:j + 1], wc, bsf, name="gm_fwd")
            x1 = matmul(p, W["gm_w_out"][j], residual=x, name="mm_gm_out")
            s.update(zzpre=zzpre, p=p, wc=wc, bsf=bsf)
        else:
            conv_dim = W["ssm_conv_w"].shape[2]
            G = (conv_dim - inner) // (2 * SSM_STATE)
            hpg = heads // G
            w_in = W["ssm_w_in"][j]
            w_zx = w_in[:inner + conv_dim]
            w_dtg = _spread_dt(w_in[inner + conv_dim:], G, hpg)
            bias_g = _group_vec(W["ssm_dt_bias"][j], G, hpg)
            alog_g = _group_vec(W["ssm_a_log"][j], G, hpg)
            d_chan = jnp.repeat(W["ssm_d"][j], SB_HEAD_DIM)[None, :]
            ngain = W["ssm_norm_gain"][j:j + 1]
            zx = matmul(h, w_zx, tb=True, name="mm_ssm_zx")
            dtg = matmul(h, w_dtg, tb=True, name="mm_ssm_dt")
            xbc = conv_fwd(zx, W["ssm_conv_w"][j], W["ssm_conv_b"][j:j + 1], inner, name="conv_fwd")
            yn, yfull, hp = ssd_fwd(xbc, zx, dtg, bias_g, alog_g, d_chan, ngain, L, G, name="ssd_fwd")
            x1 = matmul(yn, W["ssm_w_out"][j], residual=x, name="mm_ssm_out")
            s.update(w_zx=w_zx, w_dtg=w_dtg, bias_g=bias_g, alog_g=alog_g, d_chan=d_chan, ngain=ngain,
                     zx=zx, dtg=dtg, xbc=xbc, yn=yn, yfull=yfull, hp=hp)
        h2 = rms_fwd(x1, W["ffn_norm"][i:i + 1], name="rms_ffn_fwd")
        gu, a = ffn_up_fwd(h2, W["ffn_w_gu"][i], name="ffn_up_fwd")
        x2 = matmul(a, W["ffn_w_down"][i], residual=x1, name="mm_ffn_down")
        s.update(x1=x1, h2=h2, gu=gu, a=a)
        saved.append(s)
        x = x2

    dx, loss = loss_head(x, target, name="loss_head")

    gw = {k: {} for k in WEIGHTS}
    for i in reversed(range(depth)):
        kind, j = i % 3, i // 3
        s = saved[i]
        gw["ffn_w_down"][i] = matmul(s["a"], dx, ta=True, out_dtype=BF16, name="mm_ffn_dwdown")
        dgu = ffn_up_bwd(dx, W["ffn_w_down"][i], s["gu"], name="ffn_up_bwd")
        dh2 = matmul(dgu, W["ffn_w_gu"][i], a_split=2, name="mm_ffn_dh")
        gw["ffn_w_gu"][i] = matmul(dgu, s["h2"], ta=True, a_split=2, out_dtype=BF16, name="mm_ffn_dwgu")
        dx1, dgn = rms_bwd(s["x1"], W["ffn_norm"][i:i + 1], dh2, dx, name="rms_ffn_bwd")
        gw["ffn_norm"][i] = dgn[0]
        if kind == 0:
            do = matmul(dx1, W["sb_w_o"][j], tb=True, name="mm_sb_do")
            gw["sb_w_o"][j] = matmul(s["o"], dx1, ta=True, out_dtype=BF16, name="mm_sb_dwo")
            if late is not None and i == 0:
                dqkv, dqg, dkg, received = sb_attn_bwd(
                    s["qkv"], s["o"], do, W["sb_q_gain"][j:j + 1], W["sb_k_gain"][j:j + 1], name="sb_bwd_scatter",
                    scatter=late.contributions(gw))
            else:
                dqkv, dqg, dkg = sb_attn_bwd(s["qkv"], s["o"], do, W["sb_q_gain"][j:j + 1], W["sb_k_gain"][j:j + 1],
                                             name="sb_bwd")
            gw["sb_q_gain"][j] = dqg[0]
            gw["sb_k_gain"][j] = dkg[0]
            dh = matmul(dqkv, W["sb_w_qkv"][j], a_split=3, name="mm_sb_dh")
            gw["sb_w_qkv"][j] = matmul(dqkv, s["h"], ta=True, a_split=3, out_dtype=BF16, name="mm_sb_dwqkv")
        elif kind == 1:
            dp = matmul(dx1, W["gm_w_out"][j], tb=True, name="mm_gm_dp")
            gw["gm_w_out"][j] = matmul(s["p"], dx1, ta=True, out_dtype=BF16, name="mm_gm_dwout")
            dzz, db_in, dvg, dws, dbs = gmlp_bwd(s["zzpre"], W["gm_b_in"][j:j + 1], W["gm_v_gain"][j:j + 1],
                                                s["wc"], s["bsf"], dp, name="gm_bwd")
            gw["gm_b_in"][j] = db_in[0]
            gw["gm_v_gain"][j] = dvg[0]
            gw["gm_w_s"][j] = dws
            gw["gm_b_s"][j] = dbs[:, :gm_groups].T
            dh = matmul(dzz, W["gm_w_in"][j], name="mm_gm_dh")
            gw["gm_w_in"][j] = matmul(dzz, s["h"], ta=True, out_dtype=BF16, name="mm_gm_dwin")
        else:
            conv_dim = W["ssm_conv_w"].shape[2]
            G = (conv_dim - inner) // (2 * SSM_STATE)
            hpg = heads // G
            dyn = matmul(dx1, W["ssm_w_out"][j], tb=True, name="mm_ssm_dyn")
            gw["ssm_w_out"][j] = matmul(s["yn"], dx1, ta=True, out_dtype=BF16, name="mm_ssm_dwout")
            dz, dxs, dbm, dcm, ddt, dbias, dalog, dd, dng = ssd_bwd(
                s["xbc"], s["zx"], s["dtg"], s["bias_g"], s["alog_g"], s["d_chan"], s["ngain"], s["yfull"], s["hp"],
                dyn, L, G, name="ssd_bwd")
            dxbc = jnp.concatenate([dxs, dbm, dcm], axis=1)
            dpre, dcw, dcb = conv_bwd(s["zx"], W["ssm_conv_w"][j], W["ssm_conv_b"][j:j + 1], inner, dxbc,
                                      name="conv_bwd")
            dzx = jnp.concatenate([dz, dpre], axis=1)
            dh = matmul(ddt, s["w_dtg"], name="mm_ssm_dh_dt")
            dh = matmul(dzx, s["w_zx"], residual=dh, name="mm_ssm_dh")
            dw_zx = matmul(dzx, s["h"], ta=True, out_dtype=BF16, name="mm_ssm_dwzx")
            dw_dtg = matmul(ddt, s["h"], ta=True, out_dtype=BF16, name="mm_ssm_dwdt")
            dw_dt = dw_dtg.reshape(G, LANES, D)[:, :hpg, :].reshape(heads, D)
            gw["ssm_w_in"][j] = jnp.concatenate([dw_zx, dw_dt], axis=0)
            gw["ssm_conv_w"][j] = dcw
            gw["ssm_conv_b"][j] = dcb[0]
            gw["ssm_dt_bias"][j] = dbias[:, 0, :hpg].reshape(heads)
            gw["ssm_a_log"][j] = dalog[:, 0, :hpg].reshape(heads)
            gw["ssm_d"][j] = dd[:, 0, :hpg].reshape(heads)
            gw["ssm_norm_gain"][j] = dng[0]
        dx, dgn = rms_bwd(s["x"], W["mix_norm"][i:i + 1], dh, dx1, name="rms_mix_bwd")
        gw["mix_norm"][i] = dgn[0]

    return loss, dx, gw, received


MESH = pl.DeviceIdType.MESH
HBM_SPEC = pl.BlockSpec(memory_space=pltpu.HBM)
VMEM_SPEC = pl.BlockSpec(memory_space=pltpu.VMEM)


def _my_position():
    return lax.axis_index("x"), lax.axis_index("y"), lax.axis_index("c")


def _flip(v, bit):
    return 1 - v if bit else v


def all_gather_packed(shard, *, name):
    R, C = shard.shape

    def body(x_ref, out_ref, send_sems, recv_sems, local_sem):
        refs = (x_ref, out_ref, send_sems, recv_sems, local_sem)
        _ag_phase(refs, "start")
        _ag_phase(refs, "forward")
        _ag_phase(refs, "finish")

    return pl.pallas_call(
        body, name=name, out_shape=jax.ShapeDtypeStruct((N_DEV, R, C), shard.dtype),
        in_specs=[HBM_SPEC], out_specs=HBM_SPEC, scratch_shapes=COMM_SEMAPHORES,
    )(shard)


COMM_SEMAPHORES = [pltpu.SemaphoreType.DMA((7,)), pltpu.SemaphoreType.DMA((7,)), pltpu.SemaphoreType.DMA]


def _ag_phase(refs, phase):
    x_ref, out_ref, send_sems, recv_sems, local_sem = refs
    x, y, c = _my_position()
    me, sibling = (x, y, c), (x, y, 1 - c)
    chips = [(1 - x, y), (x, 1 - y), (1 - x, 1 - y)]

    def slot(px, py, pc):
        return out_ref.at[4 * px + 2 * py + pc]

    def copy(k, block, to, src=None):
        return pltpu.make_async_remote_copy(
            src_ref=slot(*block) if src is None else src, dst_ref=slot(*block),
            send_sem=send_sems.at[k], recv_sem=recv_sems.at[k], device_id=to, device_id_type=MESH)

    mine = pltpu.make_async_copy(x_ref, slot(*me), local_sem)
    first = [copy(0, me, sibling, src=x_ref)]
    first += [copy(1 + j, me, (*chip, c), src=x_ref) for j, chip in enumerate(chips)]
    passed = [copy(4 + j, (*chip, c), sibling) for j, chip in enumerate(chips)]
    if phase == "start":
        mine.start()
        for cp in first:
            cp.start()
    elif phase == "forward":
        for j, chip in enumerate(chips):
            copy(1 + j, (*chip, c), me).wait_recv()
            passed[j].start()
    else:
        copy(0, sibling, me).wait_recv()
        for j, chip in enumerate(chips):
            copy(4 + j, (*chip, 1 - c), me).wait_recv()
        for cp in first + passed:
            cp.wait_send()
        mine.wait()


def _rs_semaphores(n):
    return [pltpu.SemaphoreType.DMA((7 * n,)), pltpu.SemaphoreType.DMA((7 * n,)), pltpu.SemaphoreType.DMA((n,))]


def _rs_phase(g_refs, out_refs, sems, phase):
    send_sems, recv_sems, local_sems = sems
    x, y, c = _my_position()
    me = 4 * x + 2 * y + c
    copies = []
    for p, (g_ref, out_ref) in enumerate(zip(g_refs, out_refs)):
        copies.append(pltpu.make_async_copy(g_ref.at[me], out_ref.at[me], local_sems.at[p]))
        for k in range(1, N_DEV):
            px, py, pc = _flip(x, k & 4), _flip(y, k & 2), _flip(c, k & 1)
            copies.append(pltpu.make_async_remote_copy(
                src_ref=g_ref.at[4 * px + 2 * py + pc], dst_ref=out_ref.at[me],
                send_sem=send_sems.at[7 * p + k - 1], recv_sem=recv_sems.at[7 * p + k - 1],
                device_id=(px, py, pc), device_id_type=MESH))
    for cp in copies:
        if phase == "start":
            cp.start()
        else:
            cp.wait()


def exchange_for_reduce_scatter(gs, *, name):
    n = len(gs)

    def body(*refs):
        for phase in ("start", "finish"):
            _rs_phase(refs[:n], refs[n:2 * n], refs[2 * n:], phase)

    return pl.pallas_call(
        body, name=name, out_shape=[jax.ShapeDtypeStruct(g.shape, g.dtype) for g in gs],
        in_specs=[HBM_SPEC] * n, out_specs=[HBM_SPEC] * n, scratch_shapes=_rs_semaphores(n),
    )(*gs)


def sum_slots(recv, *, name):
    n, R, C = recv.shape
    tr = _pick(R, (512, 256, 128))

    def body(r_ref, o_ref):
        acc = r_ref[0].astype(F32)
        for s in range(1, n):
            acc = acc + r_ref[s].astype(F32)
        o_ref[...] = acc

    return pl.pallas_call(
        body, name=name, grid=(R // tr,), in_specs=[pl.BlockSpec((n, tr, C), lambda i: (0, i, 0))],
        out_specs=pl.BlockSpec((tr, C), lambda i: (i, 0)), out_shape=jax.ShapeDtypeStruct((R, C), F32),
        compiler_params=_params(("parallel",)),
    )(recv)


def all_reduce_small(v, *, name):
    R, C = v.shape

    def body(v_ref, o_ref, buf, send_sems, recv_sems):
        x, y, c = _my_position()
        me = 4 * x + 2 * y + c
        buf[me] = v_ref[...]
        copies = []
        for k in range(1, N_DEV):
            px, py, pc = _flip(x, k & 4), _flip(y, k & 2), _flip(c, k & 1)
            copies.append(pltpu.make_async_remote_copy(
                src_ref=v_ref, dst_ref=buf.at[me], send_sem=send_sems.at[k - 1], recv_sem=recv_sems.at[k - 1],
                device_id=(px, py, pc), device_id_type=MESH))
        for cp in copies:
            cp.start()
        for cp in copies:
            cp.wait()
        acc = buf[0]
        for s in range(1, N_DEV):
            acc = acc + buf[s]
        o_ref[...] = acc

    return pl.pallas_call(
        body, name=name, out_shape=jax.ShapeDtypeStruct((R, C), F32), in_specs=[VMEM_SPEC], out_specs=VMEM_SPEC,
        scratch_shapes=[pltpu.VMEM((N_DEV, R, C), F32), pltpu.SemaphoreType.DMA((7,)), pltpu.SemaphoreType.DMA((7,))],
        compiler_params=pltpu.CompilerParams(vmem_limit_bytes=VMEM_LIMIT_BYTES),
    )(v)


def _pad_rows(a, mult):
    pad = (-a.shape[0]) % mult
    return jnp.pad(a, ((0, pad), (0, 0))) if pad else a


def _pack_small(arrays):
    flat = []
    for a in arrays:
        f = a.reshape(-1).astype(F32)
        flat.append(jnp.pad(f, (0, (-f.shape[0]) % LANES)))
    return _pad_rows(jnp.concatenate(flat).reshape(-1, LANES), 8)


def _unpack_small(packed, shapes):
    flat = packed.reshape(-1)
    out, r = [], 0
    for shp in shapes:
        n = math.prod(shp)
        out.append(flat[r:r + n].reshape(shp))
        r += n + (-n) % LANES
    return out


ARG_NAMES = ("x",) + WEIGHTS + ("loss_target",) + tuple("m_" + w for w in WEIGHTS) + tuple("v_" + w for w in WEIGHTS)


def kernel(x, mix_norm, ffn_norm, sb_w_qkv, sb_q_gain, sb_k_gain, sb_w_o, gm_w_in, gm_b_in, gm_v_gain, gm_w_s, gm_b_s, gm_w_out, ssm_w_in, ssm_conv_w, ssm_conv_b, ssm_dt_bias, ssm_a_log, ssm_d, ssm_norm_gain, ssm_w_out, ffn_w_gu, ffn_w_down, loss_target, m_mix_norm, m_ffn_norm, m_sb_w_qkv, m_sb_q_gain, m_sb_k_gain, m_sb_w_o, m_gm_w_in, m_gm_b_in, m_gm_v_gain, m_gm_w_s, m_gm_b_s, m_gm_w_out, m_ssm_w_in, m_ssm_conv_w, m_ssm_conv_b, m_ssm_dt_bias, m_ssm_a_log, m_ssm_d, m_ssm_norm_gain, m_ssm_w_out, m_ffn_w_gu, m_ffn_w_down, v_mix_norm, v_ffn_norm, v_sb_w_qkv, v_sb_q_gain, v_sb_k_gain, v_sb_w_o, v_gm_w_in, v_gm_b_in, v_gm_v_gain, v_gm_w_s, v_gm_b_s, v_gm_w_out, v_ssm_w_in, v_ssm_conv_w, v_ssm_conv_b, v_ssm_dt_bias, v_ssm_a_log, v_ssm_d, v_ssm_norm_gain, v_ssm_w_out, v_ffn_w_gu, v_ffn_w_down):
    given = dict(zip(ARG_NAMES, (x, mix_norm, ffn_norm, sb_w_qkv, sb_q_gain, sb_k_gain, sb_w_o, gm_w_in, gm_b_in, gm_v_gain, gm_w_s, gm_b_s, gm_w_out, ssm_w_in, ssm_conv_w, ssm_conv_b, ssm_dt_bias, ssm_a_log, ssm_d, ssm_norm_gain, ssm_w_out, ffn_w_gu, ffn_w_down, loss_target, m_mix_norm, m_ffn_norm, m_sb_w_qkv, m_sb_q_gain, m_sb_k_gain, m_sb_w_o, m_gm_w_in, m_gm_b_in, m_gm_v_gain, m_gm_w_s, m_gm_b_s, m_gm_w_out, m_ssm_w_in, m_ssm_conv_w, m_ssm_conv_b, m_ssm_dt_bias, m_ssm_a_log, m_ssm_d, m_ssm_norm_gain, m_ssm_w_out, m_ffn_w_gu, m_ffn_w_down, v_mix_norm, v_ffn_norm, v_sb_w_qkv, v_sb_q_gain, v_sb_k_gain, v_sb_w_o, v_gm_w_in, v_gm_b_in, v_gm_v_gain, v_gm_w_s, v_gm_b_s, v_gm_w_out, v_ssm_w_in, v_ssm_conv_w, v_ssm_conv_b, v_ssm_dt_bias, v_ssm_a_log, v_ssm_d, v_ssm_norm_gain, v_ssm_w_out, v_ffn_w_gu, v_ffn_w_down)))
    mx, my, mc = _my_position()
    me = 4 * mx + 2 * my + mc

    pieces = [(k, l) for k in BIG for l in range(given[k].shape[0])]
    early = [("sb_w_qkv", 0)]
    late_pieces = [p for p in pieces if p not in early]
    last = [("sb_w_qkv", 0)]
    main = [p for p in pieces if p not in last]
    row_mult = 256

    def rows_of(p):
        return math.prod(given[p[0]].shape[1:]) // PACK_COLS

    def pack_shards(ps, extra=()):
        parts = [(given[k][l].T if k in COL_SHARDED else given[k][l]).astype(BF16).reshape(-1, PACK_COLS)
                 for k, l in ps] + list(extra)
        return _pad_rows(jnp.concatenate(parts, axis=0), row_mult)

    def split_rows(packed, ps):
        out, r = [], 0
        for p in ps:
            out.append(packed[..., r:r + rows_of(p), :])
            r += rows_of(p)
        return out

    def piece_to_full(g, k):
        rows, cols = given[k].shape[1:]
        return g.reshape(N_DEV * cols, rows) if k in COL_SHARDED else g.reshape(N_DEV * rows, cols)

    def full_to_piece(full, k):
        return full.reshape(N_DEV, -1, PACK_COLS)

    def summed_to_shard(g, k):
        rows, cols = given[k].shape[1:]
        return g.reshape(cols, rows).T if k in COL_SHARDED else g.reshape(rows, cols)

    def contributions(gw, ps):
        return [full_to_piece(gw[k][l], k) for k, l in ps]

    sharded_small = [lax.bitcast_convert_type(given[k], BF16) for k in SMALL_SHARDED]
    tail = jnp.concatenate([a.reshape(-1) for a in sharded_small])
    tail = jnp.pad(tail, (0, (-tail.size) % PACK_COLS)).reshape(-1, PACK_COLS)

    W = {k: given[k] for k in SMALL if k not in SMALL_SHARDED}
    W.update({k: [None] * given[k].shape[0] for k in BIG})
    gathered_early = all_gather_packed(pack_shards(early), name="all_gather_early")
    for (k, l), g in zip(early, split_rows(gathered_early, early)):
        W[k][l] = piece_to_full(g, k)

    class Late:
        shard = pack_shards(late_pieces, extra=[tail])

        @staticmethod
        def fill(weights, gathered):
            for (k, l), g in zip(late_pieces, split_rows(gathered, late_pieces)):
                weights[k][l] = piece_to_full(g, k)
            r0 = sum(rows_of(p) for p in late_pieces)
            tail_g = gathered[:, r0:r0 + tail.shape[0], :].reshape(N_DEV, -1)
            off = 0
            for k, a in zip(SMALL_SHARDED, sharded_small):
                g = lax.bitcast_convert_type(tail_g[:, off:off + a.size].reshape((N_DEV,) + a.shape), F32)
                weights[k] = jnp.moveaxis(g, 0, -2).reshape(g.shape[1:-1] + (N_DEV * g.shape[-1],))
                off += a.size

        @staticmethod
        def contributions(gw):
            return contributions(gw, main)

    loss, gx, gw, received_main = local_step(given["x"][0], given["loss_target"][0], W, late=Late)
    received_last = exchange_for_reduce_scatter(contributions(gw, last), name="reduce_scatter_last")

    grads_small = {k: jnp.stack([gw[k][l] for l in sorted(gw[k])], axis=0) for k in SMALL}
    small_shapes = [grads_small[k].shape for k in SMALL] + [(1, 1)]
    reduced = all_reduce_small(_pack_small([grads_small[k] for k in SMALL] + [loss]), name="all_reduce_small")
    small_full = dict(zip(SMALL + ("loss",), _unpack_small(reduced, small_shapes)))

    g_piece = {}
    for grp, received in ((main, received_main), (last, received_last)):
        for p, r in zip(grp, received):
            g_piece[p] = summed_to_shard(sum_slots(r, name="reduce_scatter_sum"), p[0])
    out_g, out_d, out_m, out_v = {}, {}, {}, {}
    for k in BIG:
        g = jnp.stack([g_piece[(k, l)] for l in range(given[k].shape[0])], axis=0)
        out_g[k] = g
        out_d[k], out_m[k], out_v[k] = adamw(given[k], g, given["m_" + k], given["v_" + k], name="adamw_" + k)

    gsmall = {}
    for k in SMALL:
        g = small_full[k]
        if k in SMALL_SHARDED:
            n = given[k].shape[-1]
            g = lax.dynamic_slice_in_dim(g, me * n, n, axis=g.ndim - 1)
        gsmall[k] = g
    local_shapes = [given[k].shape for k in SMALL]
    dsm, nmsm, nvsm = adamw(_pack_small([given[k] for k in SMALL]), _pack_small([gsmall[k] for k in SMALL]),
                            _pack_small([given["m_" + k] for k in SMALL]), _pack_small([given["v_" + k] for k in SMALL]),
                            name="adamw_small")
    out_g.update(gsmall)
    for dst, src in ((out_d, dsm), (out_m, nmsm), (out_v, nvsm)):
        dst.update(zip(SMALL, _unpack_small(src, local_shapes)))

    return (small_full["loss"].reshape(()), gx[None],
            *[out_g[k] for k in WEIGHTS], *[out_d[k] for k in WEIGHTS],
            *[out_m[k] for k in WEIGHTS], *[out_v[k] for k in WEIGHTS])
```

```python
import math

import jax
import jax.numpy as jnp
from jax import lax
from jax.experimental import pallas as pl
from jax.experimental.pallas import tpu as pltpu

F32 = jnp.float32
BF16 = jnp.bfloat16
EPS = 1e-6
N_DEV = 8
SB_HEAD_DIM = 64
SB_FWD_QUERY_BLOCKS = 4
SB_BWD_QUERY_BLOCKS = 2
SSM_STATE = 128
SSM_CONV = 4
ADAM_LR = 0.001
ADAM_B1 = 0.9
ADAM_B2 = 0.999
ADAM_EPS = 1e-08
ADAM_WD = 0.01
ADAM_STEP = 10
VMEM_LIMIT_BYTES = 56 * 1024 * 1024
MATMUL_VMEM_BUDGET = 40 * 1024 * 1024
LANES = 128
PACK_COLS = 1024

BIG = ("sb_w_qkv", "sb_w_o", "gm_w_in", "gm_w_out", "ssm_w_in", "ssm_w_out", "ffn_w_gu", "ffn_w_down")
COL_SHARDED = ("sb_w_qkv", "gm_w_in", "ssm_w_in", "ffn_w_gu")
SMALL = ("mix_norm", "ffn_norm", "sb_q_gain", "sb_k_gain", "gm_b_in", "gm_v_gain", "gm_w_s", "gm_b_s",
         "ssm_conv_w", "ssm_conv_b", "ssm_dt_bias", "ssm_a_log", "ssm_d", "ssm_norm_gain")
SMALL_SHARDED = ("ssm_conv_w", "ssm_conv_b", "ssm_norm_gain")
WEIGHTS = ("mix_norm", "ffn_norm", "sb_w_qkv", "sb_q_gain", "sb_k_gain", "sb_w_o", "gm_w_in", "gm_b_in",
           "gm_v_gain", "gm_w_s", "gm_b_s", "gm_w_out", "ssm_w_in", "ssm_conv_w", "ssm_conv_b", "ssm_dt_bias",
           "ssm_a_log", "ssm_d", "ssm_norm_gain", "ssm_w_out", "ffn_w_gu", "ffn_w_down")


def _params(semantics=None):
    return pltpu.CompilerParams(dimension_semantics=semantics, vmem_limit_bytes=VMEM_LIMIT_BYTES)


def _pick(n, prefs):
    for t in prefs:
        if t <= n and n % t == 0:
            return t
    return n


def _dot(a, b, ca=1, cb=0):
    return lax.dot_general(a, b, (((ca,), (cb,)), ((), ())), preferred_element_type=F32)


def _split3(v):
    h1 = v.astype(BF16)
    r1 = v - h1.astype(F32)
    h2 = r1.astype(BF16)
    h3 = (r1 - h2.astype(F32)).astype(BF16)
    return h1, h2, h3


def _dot_exact_left(mat01, v):
    h1, h2, h3 = _split3(v)
    return _dot(mat01, h1) + _dot(mat01, h2) + _dot(mat01, h3)


def _dot_split2_right(v, mat01):
    hi = v.astype(BF16)
    lo = (v - hi.astype(F32)).astype(BF16)
    return _dot(hi, mat01) + _dot(lo, mat01)


def _row_mean_filled(v):
    w = v.shape[1]
    return _dot_split2_right(v, jnp.ones((w, w), BF16)) * (1.0 / w)


def _sum_all(v):
    return jnp.sum(jnp.sum(v, axis=0, keepdims=True), axis=1, keepdims=True)


def _sigmoid(v):
    return 1.0 / (1.0 + jnp.exp(-v))


def _softplus(v):
    return jnp.maximum(v, 0.0) + jnp.log(1.0 + jnp.exp(-jnp.abs(v)))


def _erf(v):
    a = jnp.abs(v)
    t = 1.0 / (1.0 + 0.3275911 * a)
    poly = t * (0.254829592 + t * (-0.284496736 + t * (1.421413741 + t * (-1.453152027 + t * 1.061405429))))
    e = 1.0 - poly * jnp.exp(-a * a)
    return jnp.where(v < 0, -e, e)


def _gelu_and_grad(v):
    cdf = 0.5 * (1.0 + _erf(v * (1.0 / math.sqrt(2.0))))
    pdf = jnp.exp(-0.5 * v * v) * (1.0 / math.sqrt(2.0 * math.pi))
    return v * cdf, cdf + v * pdf


def matmul(a, b, *, ta=False, tb=False, out_dtype=F32, residual=None, a_split=1, b_split=1, name):
    if a_split > 1 and ta:
        assert a.shape[0] == a_split
        K, M = a.shape[1], a_split * a.shape[2]
    elif a_split > 1:
        assert a.shape[0] == a_split
        M, K = a.shape[1], a_split * a.shape[2]
    elif ta:
        K, M = a.shape
    else:
        M, K = a.shape
    if b_split > 1:
        assert not tb and b.shape[0] == b_split
        Kb, N = b.shape[1], b_split * b.shape[2]
    elif tb:
        N, Kb = b.shape
    else:
        Kb, N = b.shape
    assert K == Kb, (a.shape, b.shape, ta, tb)
    has_res = residual is not None
    tm = _pick(M // a_split if ta else M, (1024, 1408, 768, 512, 256, 128))
    tn = _pick(N // b_split, (1024, 1408, 1536, 768, 512, 256, 128))

    def vmem_bytes(tk):
        tiles = tm * tk * a.dtype.itemsize + tk * tn * b.dtype.itemsize
        outs = tm * tn * jnp.dtype(out_dtype).itemsize + (tm * tn * 4 if has_res else 0)
        return 2 * tiles + 2 * outs + (tm * tn * 4 if tk < K else 0)

    kp = K if ta else K // a_split
    tk = next((t for t in (K, 2048, 1408, 1024, 512, 256) if t <= kp and kp % t == 0 and vmem_bytes(t) <= MATMUL_VMEM_BUDGET),
              _pick(kp, (128,)))
    nk = K // tk
    if a_split > 1 and ta:
        nib = M // a_split // tm
        a_spec = pl.BlockSpec((None, tk, tm), lambda i, j, k: (i // nib, k, i % nib))
    elif a_split > 1:
        nkb = kp // tk
        a_spec = pl.BlockSpec((None, tm, tk), lambda i, j, k: (k // nkb, i, k % nkb))
    else:
        a_spec = pl.BlockSpec((tk, tm), lambda i, j, k: (k, i)) if ta else pl.BlockSpec((tm, tk), lambda i, j, k: (i, k))
    if b_split > 1:
        njb = N // b_split // tn
        b_spec = pl.BlockSpec((None, tk, tn), lambda i, j, k: (j // njb, k, j % njb))
    else:
        b_spec = pl.BlockSpec((tn, tk), lambda i, j, k: (j, k)) if tb else pl.BlockSpec((tk, tn), lambda i, j, k: (k, j))
    o_spec = pl.BlockSpec((tm, tn), lambda i, j, k: (i, j))
    ca, cb = (0 if ta else 1), (1 if tb else 0)

    def body(*refs):
        a_ref, b_ref = refs[:2]
        r_ref = refs[2] if has_res else None
        o_ref = refs[3] if has_res else refs[2]

        def finish(r):
            if has_res:
                r = r + r_ref[...]
            o_ref[...] = r.astype(out_dtype)

        def part():
            return _dot(a_ref[...].astype(BF16), b_ref[...].astype(BF16), ca, cb)

        if nk == 1:
            finish(part())
            return
        acc = refs[-1]
        k = pl.program_id(2)

        @pl.when(k == 0)
        def _():
            acc[...] = part()

        @pl.when(jnp.logical_and(k > 0, k < nk - 1))
        def _():
            acc[...] += part()

        @pl.when(k == nk - 1)
        def _():
            finish(acc[...] + part())

    in_specs = [a_spec, b_spec] + ([o_spec] if has_res else [])
    args = (a, b) + ((residual,) if has_res else ())
    return pl.pallas_call(
        body, name=name, grid=(M // tm, N // tn, nk), in_specs=in_specs, out_specs=o_spec,
        out_shape=jax.ShapeDtypeStruct((M, N), out_dtype),
        scratch_shapes=[pltpu.VMEM((tm, tn), F32)] if nk > 1 else [],
        compiler_params=_params(("parallel", "parallel", "arbitrary")),
    )(*args)


def rms_fwd(x, gain, *, name):
    S, D = x.shape
    tr = _pick(S, (512, 256, 128))

    def body(x_ref, g_ref, o_ref):
        xv = x_ref[...]
        r = lax.rsqrt(jnp.mean(xv * xv, axis=1, keepdims=True) + EPS)
        o_ref[...] = (xv * r * g_ref[...]).astype(BF16)

    return pl.pallas_call(
        body, name=name, grid=(S // tr,),
        in_specs=[pl.BlockSpec((tr, D), lambda i: (i, 0)), pl.BlockSpec((1, D), lambda i: (0, 0))],
        out_specs=pl.BlockSpec((tr, D), lambda i: (i, 0)), out_shape=jax.ShapeDtypeStruct((S, D), BF16),
        compiler_params=_params(("parallel",)),
    )(x, gain)


def rms_bwd(x, gain, dh, dres, *, name):
    S, D = x.shape
    tr = _pick(S, (512, 256, 128))

    def body(x_ref, g_ref, dh_ref, dr_ref, dx_ref, dg_ref):
        @pl.when(pl.program_id(0) == 0)
        def _():
            dg_ref[...] = jnp.zeros_like(dg_ref)

        xv = x_ref[...]
        dhv = dh_ref[...]
        r = lax.rsqrt(jnp.mean(xv * xv, axis=1, keepdims=True) + EPS)
        xhat = xv * r
        t = dhv * g_ref[...]
        dx_ref[...] = dr_ref[...] + r * (t - xhat * jnp.mean(xhat * t, axis=1, keepdims=True))
        dg_ref[...] += jnp.sum(dhv * xhat, axis=0, keepdims=True)

    row = pl.BlockSpec((tr, D), lambda i: (i, 0))
    vec = pl.BlockSpec((1, D), lambda i: (0, 0))
    return pl.pallas_call(
        body, name=name, grid=(S // tr,), in_specs=[row, vec, row, row], out_specs=[row, vec],
        out_shape=[jax.ShapeDtypeStruct((S, D), F32), jax.ShapeDtypeStruct((1, D), F32)],
        compiler_params=_params(("arbitrary",)),
    )(x, gain, dh, dres)


def ffn_up_fwd(h, w_gu_t, *, name):
    S, K = h.shape
    F = w_gu_t.shape[0] // 2
    tm = _pick(S, (512, 256, 128))
    tn = _pick(F, (1408, 1024, 768, 512, 256, 128))
    nj = F // tn

    def body(h_ref, wg_ref, wu_ref, gu_ref, a_ref):
        hv = h_ref[...]
        g = _dot(hv, wg_ref[...], 1, 1)
        u = _dot(hv, wu_ref[...], 1, 1)
        gu_ref[0] = g
        gu_ref[1] = u
        a_ref[...] = (g * _sigmoid(g) * u).astype(BF16)

    return pl.pallas_call(
        body, name=name, grid=(nj, S // tm),
        in_specs=[pl.BlockSpec((tm, K), lambda j, i: (i, 0)), pl.BlockSpec((tn, K), lambda j, i: (j, 0)),
                  pl.BlockSpec((tn, K), lambda j, i: (nj + j, 0))],
        out_specs=[pl.BlockSpec((2, tm, tn), lambda j, i: (0, i, j)), pl.BlockSpec((tm, tn), lambda j, i: (i, j))],
        out_shape=[jax.ShapeDtypeStruct((2, S, F), F32), jax.ShapeDtypeStruct((S, F), BF16)],
        compiler_params=_params(("parallel", "parallel")),
    )(h, w_gu_t, w_gu_t)


def ffn_up_bwd(dy, w_down, gu, *, name):
    S, D = dy.shape
    F = w_down.shape[0]
    tm = _pick(S, (512, 256, 128))
    tn = _pick(F, (1408, 1024, 768, 512, 256, 128))

    def body(dy_ref, wd_ref, gu_ref, o_ref):
        da = _dot(dy_ref[...].astype(BF16), wd_ref[...], 1, 1)
        g = gu_ref[0]
        u = gu_ref[1]
        s = _sigmoid(g)
        o_ref[0] = (da * u * (s * (1.0 + g * (1.0 - s)))).astype(BF16)
        o_ref[1] = (da * g * s).astype(BF16)

    pair = pl.BlockSpec((2, tm, tn), lambda j, i: (0, i, j))
    return pl.pallas_call(
        body, name=name, grid=(F // tn, S // tm),
        in_specs=[pl.BlockSpec((tm, D), lambda j, i: (i, 0)), pl.BlockSpec((tn, D), lambda j, i: (j, 0)), pair],
        out_specs=pair, out_shape=jax.ShapeDtypeStruct((2, S, F), BF16),
        compiler_params=_params(("parallel", "parallel")),
    )(dy, w_down, gu)


def loss_head(y, target, *, name):
    S, D = y.shape
    tr = _pick(S, (512, 256, 128))

    def body(y_ref, t_ref, dy_ref, l_ref):
        @pl.when(pl.program_id(0) == 0)
        def _():
            l_ref[...] = jnp.zeros_like(l_ref)

        err = y_ref[...] - t_ref[...]
        dy_ref[...] = err * (1.0 / D)
        l_ref[...] += jnp.sum(0.5 * jnp.mean(err * err, axis=1, keepdims=True), axis=0, keepdims=True)

    row = pl.BlockSpec((tr, D), lambda i: (i, 0))
    one = pl.BlockSpec((1, 1), lambda i: (0, 0))
    dy, l = pl.pallas_call(
        body, name=name, grid=(S // tr,), in_specs=[row, row], out_specs=[row, one],
        out_shape=[jax.ShapeDtypeStruct((S, D), F32), jax.ShapeDtypeStruct((1, 1), F32)],
        compiler_params=_params(("arbitrary",)),
    )(y, target)
    return dy, l


def adamw(w, g, m, v, *, name):
    R, C = w.shape[-2:]
    tr = _pick(R, (512, 256, 128, 64, 32, 16, 8))

    def body(w_ref, g_ref, m_ref, v_ref, d_ref, mo_ref, vo_ref):
        gv = g_ref[...]
        mn = ADAM_B1 * m_ref[...] + (1.0 - ADAM_B1) * gv
        vn = ADAM_B2 * v_ref[...] + (1.0 - ADAM_B2) * jnp.square(gv)
        m_hat = mn / (1.0 - ADAM_B1 ** ADAM_STEP)
        v_hat = vn / (1.0 - ADAM_B2 ** ADAM_STEP)
        d_ref[...] = -ADAM_LR * (m_hat / (jnp.sqrt(v_hat) + ADAM_EPS) + ADAM_WD * w_ref[...])
        mo_ref[...] = mn
        vo_ref[...] = vn

    if w.ndim == 3:
        grid = (w.shape[0], R // tr)
        blk = pl.BlockSpec((None, tr, C), lambda l, i: (l, i, 0))
    else:
        grid = (R // tr,)
        blk = pl.BlockSpec((tr, C), lambda i: (i, 0))
    sds = jax.ShapeDtypeStruct(w.shape, F32)
    return pl.pallas_call(
        body, name=name, grid=grid, in_specs=[blk] * 4, out_specs=[blk] * 3, out_shape=[sds] * 3,
        compiler_params=_params(("parallel",) * len(grid)),
    )(w, g, m, v)


def _tri(n, kind):
    r = lax.broadcasted_iota(jnp.int32, (n, n), 0)
    c = lax.broadcasted_iota(jnp.int32, (n, n), 1)
    if kind == "row_gt_col":
        return (r > c).astype(BF16)
    if kind == "row_ge_col":
        return (r >= c).astype(BF16)
    if kind == "row_le_col":
        return (r <= c).astype(BF16)
    raise ValueError(kind)


def _sb_tile(qs, kj, r_carry, u_strict, masked):
    z = _dot(qs, kj, 1, 1)
    lb = jnp.minimum(z, 0.0) - jnp.log(1.0 + jnp.exp(-jnp.abs(z)))
    l1m = lb - z
    keep = None
    if masked:
        tq, tk = z.shape
        keep = lax.broadcasted_iota(jnp.int32, (tq, tk), 1) < lax.broadcasted_iota(jnp.int32, (tq, tk), 0)
        l1m = jnp.where(keep, l1m, 0.0)
    w = jnp.exp(lb + _dot(l1m.astype(BF16), u_strict) + r_carry)
    if masked:
        w = jnp.where(keep, w, 0.0)
    return lb, l1m, w, keep


def _sb_prep(T, nb, hd, refs_in, gains, scratch):
    q_scale = 1.0 / math.sqrt(hd)
    assert math.log2(q_scale) == round(math.log2(q_scale))

    def prep(i, _):
        rows = pl.ds(pl.multiple_of(i * T, T), T)
        for hh in range(2):
            sl = slice(hd * hh, hd * hh + hd)
            for n, (src, dst) in enumerate(zip(refs_in, scratch)):
                v = src[rows, sl]
                if n < 2:
                    v = v * lax.rsqrt(jnp.mean(v * v, axis=1, keepdims=True) + EPS) * gains[n][...]
                if n == 0:
                    v = v * q_scale
                dst[hh, rows, :] = v.astype(BF16)
        return 0

    lax.fori_loop(0, nb, prep, 0)


def _sb_chains(m, T, nq):
    rows = [pl.ds(pl.multiple_of((nq * m + qb) * T, T), T) for qb in range(nq)]
    return rows, [(hh, qb) for qb in range(nq) for hh in range(2)]


def _sb_sweep(tile, carry, chains, m, nq):
    for kk in reversed(range(nq)):
        carry = tile(nq * m + kk, carry, [(ch, ch[1] == kk) for ch in chains if ch[1] >= kk])
    return lax.fori_loop(0, nq * m, lambda jj, c: tile(nq * m - 1 - jj, c, [(ch, False) for ch in chains]), carry)


def sb_attn_fwd(qkv, q_gain, k_gain, *, name, gather=None):
    S, D3 = qkv.shape
    D = D3 // 3
    npairs = D // LANES
    hd = SB_HEAD_DIM
    T = min(256, S)
    nb = S // T
    nq = SB_FWD_QUERY_BLOCKS
    assert nb % nq == 0

    def body(*refs):
        if gather is None:
            q_ref, k_ref, v_ref, qg_ref, kg_ref, us_ref, o_ref, qn_s, kn_s, vb_s = refs
        else:
            q_ref, k_ref, v_ref, qg_ref, kg_ref, us_ref, ag_in, o_ref, ag_out, qn_s, kn_s, vb_s = refs[:12]
            comm = (ag_in, ag_out) + refs[12:]
            step = pl.program_id(0)
            pl.when(step == 0)(lambda: _ag_phase(comm, "start"))
            pl.when(step == npairs - 1)(lambda: _ag_phase(comm, "forward"))
        us = us_ref[...]
        _sb_prep(T, nb, hd, (q_ref, k_ref, v_ref), (qg_ref, kg_ref), (qn_s, kn_s, vb_s))

        def superblock(m, _):
            rows_q, chains = _sb_chains(m, T, nq)
            qs = {ch: qn_s[ch[0], rows_q[ch[1]], :] for ch in chains}

            def tile(j, carry, which):
                rows_j = pl.ds(pl.multiple_of(j * T, T), T)
                new = dict(carry)
                for ch, masked in which:
                    acc, rc = carry[ch]
                    _, l1m, w, _ = _sb_tile(qs[ch], kn_s[ch[0], rows_j, :], rc, us, masked)
                    new[ch] = (acc + _dot(w.astype(BF16), vb_s[ch[0], rows_j, :]),
                               rc + jnp.sum(l1m, axis=1, keepdims=True))
                return new

            carry = {ch: (jnp.zeros((T, hd), F32), jnp.zeros((T, 1), F32)) for ch in chains}
            carry = _sb_sweep(tile, carry, chains, m, nq)
            for qb in range(nq):
                o_ref[rows_q[qb], :] = jnp.concatenate([carry[(0, qb)][0], carry[(1, qb)][0]], axis=1)
            return 0

        lax.fori_loop(0, nb // nq, superblock, 0)
        if gather is not None:
            pl.when(step == npairs - 1)(lambda: _ag_phase(comm, "finish"))

    col = lambda off: pl.BlockSpec((S, LANES), lambda p, off=off: (0, off + p))
    gain = pl.BlockSpec((1, hd), lambda p: (0, 0))
    in_specs = [col(0), col(npairs), col(2 * npairs), gain, gain, pl.BlockSpec((T, T), lambda p: (0, 0))]
    out_specs = [pl.BlockSpec((S, LANES), lambda p: (0, p))]
    out_shape = [jax.ShapeDtypeStruct((S, D), F32)]
    scratch = [pltpu.VMEM((2, S, hd), BF16)] * 3
    args = [qkv, qkv, qkv, q_gain, k_gain, _tri(T, "row_gt_col")]
    if gather is not None:
        in_specs.append(HBM_SPEC)
        out_specs.append(HBM_SPEC)
        out_shape.append(jax.ShapeDtypeStruct((N_DEV,) + gather.shape, gather.dtype))
        scratch += COMM_SEMAPHORES
        args.append(gather)
    out = pl.pallas_call(
        body, name=name, grid=(npairs,), in_specs=in_specs, out_specs=out_specs, out_shape=out_shape,
        scratch_shapes=scratch, compiler_params=_params(("arbitrary",)),
    )(*args)
    return out[0] if gather is None else tuple(out)


def sb_attn_bwd(qkv, o, do, q_gain, k_gain, *, name, scatter=None):
    S, D3 = qkv.shape
    D = D3 // 3
    npairs = D // LANES
    hd = SB_HEAD_DIM
    T = min(256, S)
    nb = S // T
    nq = SB_BWD_QUERY_BLOCKS
    assert nb % nq == 0
    scale = 1.0 / math.sqrt(hd)

    def body(*refs):
        if scatter is None:
            (q_ref, k_ref, v_ref, o_ref, do_ref, qg_ref, kg_ref, us_ref,
             dqkv_ref, dg_ref, qn_s, kn_s, vb_s, dob_s) = refs
        else:
            ns = len(scatter)
            q_ref, k_ref, v_ref, o_ref, do_ref, qg_ref, kg_ref, us_ref = refs[:8]
            rs_in = refs[8:8 + ns]
            dqkv_ref, dg_ref = refs[8 + ns:10 + ns]
            rs_out = refs[10 + ns:10 + 2 * ns]
            qn_s, kn_s, vb_s, dob_s = refs[10 + 2 * ns:14 + 2 * ns]
            rs_sems = refs[14 + 2 * ns:]
            pl.when(pl.program_id(0) == 0)(lambda: _rs_phase(rs_in, rs_out, rs_sems, "start"))
        dq_ref, dk_ref, dv_ref = dqkv_ref.at[0], dqkv_ref.at[1], dqkv_ref.at[2]

        @pl.when(pl.program_id(0) == 0)
        def _():
            dg_ref[...] = jnp.zeros_like(dg_ref)

        us = us_ref[...]
        u_prefix = (1.0 - us.astype(F32)).astype(BF16)
        _sb_prep(T, nb, hd, (q_ref, k_ref, v_ref, do_ref), (qg_ref, kg_ref), (qn_s, kn_s, vb_s, dob_s))
        dk_ref[...] = jnp.zeros_like(dk_ref)
        dv_ref[...] = jnp.zeros_like(dv_ref)

        def superblock(m, _):
            rows_q, chains = _sb_chains(m, T, nq)
            qs = {ch: qn_s[ch[0], rows_q[ch[1]], :] for ch in chains}
            doi ={ch: dob_s[ch[0], rows_q[ch[1]], :] for ch in chains}
            dt_total = {ch: jnp.sum(doi[ch].astype(F32) * o_ref[rows_q[ch[1]], hd * ch[0]:hd * ch[0] + hd],
                                    axis=1, keepdims=True) for ch in chains}

            def tile(j, carry, which):
                rows_j = pl.ds(pl.multiple_of(j * T, T), T)
                new = dict(carry)
                dk_part, dv_part = {}, {}
                for ch, masked in which:
                    hh = ch[0]
                    dq_acc, rc, gc = carry[ch]
                    kj = kn_s[hh, rows_j, :]
                    lb, l1m, w, keep = _sb_tile(qs[ch], kj, rc, us, masked)
                    wb = w.astype(BF16)
                    g = _dot(doi[ch], vb_s[hh, rows_j, :], 1, 1) * wb.astype(F32)
                    g_row = jnp.sum(g, axis=1, keepdims=True)
                    g_upto = (dt_total[ch] - gc - g_row) + _dot(g.astype(BF16), u_prefix)
                    dz = g - g_upto * jnp.exp(lb)
                    if masked:
                        dz = jnp.where(keep, dz, 0.0)
                    dzb = dz.astype(BF16)
                    dv_part[hh] = dv_part.get(hh, 0.0) + _dot(wb, doi[ch], 0, 0)
                    dk_part[hh] = dk_part.get(hh, 0.0) + _dot(dzb, qs[ch], 0, 0)
                    new[ch] = (dq_acc + _dot(dzb, kj), rc + jnp.sum(l1m, axis=1, keepdims=True),
                               gc + g_row)
                dv_ref[rows_j, :] += jnp.concatenate([dv_part[0], dv_part[1]], axis=1)
                dk_ref[rows_j, :] += jnp.concatenate([dk_part[0], dk_part[1]], axis=1)
                return new

            zero1 = jnp.zeros((T, 1), F32)
            carry = {ch: (jnp.zeros((T, hd), F32), zero1, zero1) for ch in chains}
            carry = _sb_sweep(tile, carry, chains, m, nq)
            for qb in range(nq):
                dq_ref[rows_q[qb], :] = jnp.concatenate([carry[(0, qb)][0], carry[(1, qb)][0]], axis=1) * scale
            return 0

        lax.fori_loop(0, nb // nq, superblock, 0)

        def finish(i, carry):
            rows = pl.ds(pl.multiple_of(i * T, T), T)
            new = []
            for hh in range(2):
                sl = slice(hd * hh, hd * hh + hd)
                outs = []
                for raw_ref, gain_ref, dn in ((q_ref, qg_ref, dq_ref[rows, sl]), (k_ref, kg_ref, dk_ref[rows, sl])):
                    raw = raw_ref[rows, sl]
                    r = lax.rsqrt(jnp.mean(raw * raw, axis=1, keepdims=True) + EPS)
                    hat = raw * r
                    t = dn * gain_ref[...]
                    outs.append((r * (t - hat * jnp.mean(hat * t, axis=1, keepdims=True)),
                                 jnp.sum(dn * hat, axis=0, keepdims=True)))
                dq_ref[rows, sl] = outs[0][0]
                dk_ref[rows, sl] = outs[1][0]
                new.append((carry[hh][0] + outs[0][1], carry[hh][1] + outs[1][1]))
            return tuple(new)

        zg = (jnp.zeros((1, hd), F32), jnp.zeros((1, hd), F32))
        tot = lax.fori_loop(0, nb, finish, (zg, zg))
        dg_ref[0:1, 0:hd] += tot[0][0] + tot[1][0]
        dg_ref[1:2, 0:hd] += tot[0][1] + tot[1][1]
        if scatter is not None:
            pl.when(pl.program_id(0) == npairs - 1)(lambda: _rs_phase(rs_in, rs_out, rs_sems, "finish"))

    col = lambda off: pl.BlockSpec((S, LANES), lambda p, off=off: (0, off + p))
    gain = pl.BlockSpec((1, hd), lambda p: (0, 0))
    tri = pl.BlockSpec((T, T), lambda p: (0, 0))
    pair = pl.BlockSpec((S, LANES), lambda p: (0, p))
    in_specs = [col(0), col(npairs), col(2 * npairs), pair, pair, gain, gain, tri]
    out_specs = [pl.BlockSpec((3, S, LANES), lambda p: (0, 0, p)), pl.BlockSpec((8, LANES), lambda p: (0, 0))]
    out_shape = [jax.ShapeDtypeStruct((3, S, D), F32), jax.ShapeDtypeStruct((8, LANES), F32)]
    scratch = [pltpu.VMEM((2, S, hd), BF16)] * 4
    args = [qkv, qkv, qkv, o, do, q_gain, k_gain, _tri(T, "row_gt_col")]
    if scatter is not None:
        in_specs += [HBM_SPEC] * len(scatter)
        out_specs += [HBM_SPEC] * len(scatter)
        out_shape += [jax.ShapeDtypeStruct(g.shape, g.dtype) for g in scatter]
        scratch += _rs_semaphores(len(scatter))
        args += list(scatter)
    out = pl.pallas_call(
        body, name=name, grid=(npairs,), in_specs=in_specs, out_specs=out_specs, out_shape=out_shape,
        scratch_shapes=scratch, compiler_params=_params(("arbitrary",)),
    )(*args)
    res = (out[0], out[1][0:1, :hd], out[1][1:2, :hd])
    return res if scatter is None else res + (list(out[2:]),)


def gmlp_fwd(zzpre, b_in, v_gain, wc, bsf, *, name):
    S, H2 = zzpre.shape
    H = H2 // 2
    G, T, _ = wc.shape
    gd = H // G

    def body(z_ref, b_ref, vg_ref, wc_ref, bs_ref, p_ref):
        zz, _ = _gelu_and_grad(z_ref[...] + b_ref[...])
        u = zz[:, :H]
        v = zz[:, H:]
        vn = v * lax.rsqrt(jnp.mean(v * v, axis=1, keepdims=True) + EPS) * vg_ref[...]
        for g in range(G):
            gs = slice(g * gd, (g + 1) * gd)
            mixed = _dot(wc_ref[g], vn[:, gs].astype(BF16)) + bs_ref[g]
            p_ref[:, gs] = (u[:, gs] * mixed).astype(BF16)

    full3 = lambda shp: pl.BlockSpec(shp, lambda c: (0, 0, 0))
    return pl.pallas_call(
        body, name=name, grid=(S // T,),
        in_specs=[pl.BlockSpec((T, H2), lambda c: (c, 0)), pl.BlockSpec((1, H2), lambda c: (0, 0)),
                  pl.BlockSpec((1, H), lambda c: (0, 0)), full3((G, T, T)), full3((G, T, gd))],
        out_specs=pl.BlockSpec((T, H), lambda c: (c, 0)), out_shape=jax.ShapeDtypeStruct((S, H), BF16),
        compiler_params=_params(("parallel",)),
    )(zzpre, b_in, v_gain, wc, bsf)


def gmlp_bwd(zzpre, b_in, v_gain, wc, bsf, dp, *, name):
    S, H2 = zzpre.shape
    H = H2 // 2
    G, T, _ = wc.shape
    gd = H // G
    assert G <= LANES

    def body(z_ref, b_ref, vg_ref, wc_ref, bs_ref, dp_ref, dzz_ref, db_ref, dvg_ref, dws_ref, dbs_ref):
        @pl.when(pl.program_id(0) == 0)
        def _():
            db_ref[...] = jnp.zeros_like(db_ref)
            dvg_ref[...] = jnp.zeros_like(dvg_ref)
            dws_ref[...] = jnp.zeros_like(dws_ref)
            dbs_ref[...] = jnp.zeros_like(dbs_ref)

        zz, gp = _gelu_and_grad(z_ref[...] + b_ref[...])
        u = zz[:, :H]
        v = zz[:, H:]
        r = lax.rsqrt(jnp.mean(v * v, axis=1, keepdims=True) + EPS)
        vhat = v * r
        vg = vg_ref[...]
        vn = vhat * vg
        dpv = dp_ref[...]
        tril = lax.broadcasted_iota(jnp.int32, (T, T), 1) <= lax.broadcasted_iota(jnp.int32, (T, T), 0)
        lane = lax.broadcasted_iota(jnp.int32, (T, LANES), 1)
        dbs = jnp.zeros((T, LANES), F32)
        du_parts, dvn_parts = [], []
        for g in range(G):
            gs = slice(g * gd, (g + 1) * gd)
            vng = vn[:, gs].astype(BF16)
            wcg = wc_ref[g]
            mixed = _dot(wcg, vng) + bs_ref[g]
            dpg = dpv[:, gs]
            du_parts.append(dpg * mixed)
            dmx = dpg * u[:, gs]
            dmxb = dmx.astype(BF16)
            dvn_parts.append(_dot(wcg, dmxb, 0, 0))
            dws_ref[g] += jnp.where(tril, _dot(dmxb, vng, 1, 1), 0.0)
            dbs = dbs + jnp.where(lane == g, jnp.sum(dmx, axis=1, keepdims=True), 0.0)
        dbs_ref[...] += dbs
        du = jnp.concatenate(du_parts, axis=1)
        dvn = jnp.concatenate(dvn_parts, axis=1)
        dvg_ref[...] += jnp.sum(dvn * vhat, axis=0, keepdims=True)
        t = dvn * vg
        dv = r * (t - vhat * jnp.mean(vhat * t, axis=1, keepdims=True))
        dzu = du * gp[:, :H]
        dzv = dv * gp[:, H:]
        dzz_ref[:, :H] = dzu.astype(BF16)
        dzz_ref[:, H:] = dzv.astype(BF16)
        db_ref[:, :H] += jnp.sum(dzu, axis=0, keepdims=True)
        db_ref[:, H:] += jnp.sum(dzv, axis=0, keepdims=True)

    full3 = lambda shp: pl.BlockSpec(shp, lambda c: (0, 0, 0))
    vec = lambda n: pl.BlockSpec((1, n), lambda c: (0, 0))
    return pl.pallas_call(
        body, name=name, grid=(S // T,),
        in_specs=[pl.BlockSpec((T, H2), lambda c: (c, 0)), vec(H2), vec(H), full3((G, T, T)), full3((G, T, gd)),
                  pl.BlockSpec((T, H), lambda c: (c, 0))],
        out_specs=[pl.BlockSpec((T, H2), lambda c: (c, 0)), vec(H2), vec(H), full3((G, T, T)),
                   pl.BlockSpec((T, LANES), lambda c: (0, 0))],
        out_shape=[jax.ShapeDtypeStruct((S, H2), BF16), jax.ShapeDtypeStruct((1, H2), F32),
                   jax.ShapeDtypeStruct((1, H), F32), jax.ShapeDtypeStruct((G, T, T), F32),
                   jax.ShapeDtypeStruct((T, LANES), F32)],
        compiler_params=_params(("arbitrary",)),
    )(zzpre, b_in, v_gain, wc, bsf, dp)


def _shift_rows(v, k, n_rows):
    if k == 0:
        return v
    rolled = pltpu.roll(v, k % n_rows, 0)
    row = lax.broadcasted_iota(jnp.int32, v.shape, 0)
    keep = (row >= k) if k > 0 else (row < n_rows + k)
    return jnp.where(keep, rolled, 0.0)


def conv_fwd(zx, conv_w, conv_b, col0, *, name):
    S = zx.shape[0]
    C = conv_w.shape[1]
    tc = _pick(C, (256, 128))
    off = col0 // tc
    assert col0 % tc == 0

    def body(x_ref, w_ref, b_ref, o_ref):
        xv = x_ref[...]
        acc = b_ref[...] + w_ref[SSM_CONV - 1:SSM_CONV, :] * xv
        for k in range(SSM_CONV - 1):
            acc = acc + w_ref[k:k + 1, :] * _shift_rows(xv, SSM_CONV - 1 - k, S)
        o_ref[...] = acc * _sigmoid(acc)

    return pl.pallas_call(
        body, name=name, grid=(C // tc,),
        in_specs=[pl.BlockSpec((S, tc), lambda j: (0, off + j)), pl.BlockSpec((SSM_CONV, tc), lambda j: (0, j)),
                  pl.BlockSpec((1, tc), lambda j: (0, j))],
        out_specs=pl.BlockSpec((S, tc), lambda j: (0, j)), out_shape=jax.ShapeDtypeStruct((S, C), F32),
        compiler_params=_params(("parallel",)),
    )(zx, conv_w, conv_b)


def conv_bwd(zx, conv_w, conv_b, col0, dout, *, name):
    S = zx.shape[0]
    C = conv_w.shape[1]
    tc = _pick(C, (256, 128))
    off = col0 // tc

    def body(x_ref, w_ref, b_ref, do_ref, dx_ref, dw_ref, db_ref):
        xv = x_ref[...]
        shifted = [_shift_rows(xv, SSM_CONV - 1 - k, S) for k in range(SSM_CONV)]
        acc = b_ref[...]
        for k in range(SSM_CONV):
            acc = acc + w_ref[k:k + 1, :] * shifted[k]
        s = _sigmoid(acc)
        dacc = do_ref[...] * (s * (1.0 + acc * (1.0 - s)))
        db_ref[...] = jnp.sum(dacc, axis=0, keepdims=True)
        dx = jnp.zeros_like(xv)
        for k in range(SSM_CONV):
            dw_ref[k:k + 1, :] = jnp.sum(dacc * shifted[k], axis=0, keepdims=True)
            dx = dx + w_ref[k:k + 1, :] * _shift_rows(dacc, -(SSM_CONV - 1 - k), S)
        dx_ref[...] = dx

    slab = pl.BlockSpec((S, tc), lambda j: (0, j))
    return pl.pallas_call(
        body, name=name, grid=(C // tc,),
        in_specs=[pl.BlockSpec((S, tc), lambda j: (0, off + j)), pl.BlockSpec((SSM_CONV, tc), lambda j: (0, j)),
                  pl.BlockSpec((1, tc), lambda j: (0, j)), slab],
        out_specs=[slab, pl.BlockSpec((SSM_CONV, tc), lambda j: (0, j)), pl.BlockSpec((1, tc), lambda j: (0, j))],
        out_shape=[jax.ShapeDtypeStruct((S, C), F32), jax.ShapeDtypeStruct((SSM_CONV, C), F32),
                   jax.ShapeDtypeStruct((1, C), F32)],
        compiler_params=_params(("parallel",)),
    )(zx, conv_w, conv_b, dout)


def _ssd_chunk_terms(dtraw, bias, a_log, tl):
    dt = _softplus(dtraw + bias)
    a_neg = -jnp.exp(a_log)
    ac = _dot_exact_left(tl, dt * a_neg)
    ac_last = ac[ac.shape[0] - 1:, :]
    return dt, a_neg, ac, ac.T, jnp.exp(ac), jnp.exp(ac_last - ac), jnp.exp(ac_last)


def _ssd_specs(S, L, G, hpg, pd, inner):
    gw = hpg * pd
    n = SSM_STATE
    xb = inner // n

    def mk(cidx):
        return dict(
            x=pl.BlockSpec((L, gw), lambda g, c: (cidx(c), g)),
            b=pl.BlockSpec((L, n), lambda g, c: (cidx(c), xb + g)),
            c=pl.BlockSpec((L, n), lambda g, c: (cidx(c), xb + G + g)),
            z=pl.BlockSpec((L, gw), lambda g, c: (cidx(c), g)),
            dt=pl.BlockSpec((L, LANES), lambda g, c: (cidx(c), g)),
            gvec=pl.BlockSpec((1, 1, LANES), lambda g, c: (g, 0, 0)),
            chan=pl.BlockSpec((1, gw), lambda g, c: (0, g)),
            tri=pl.BlockSpec((L, L), lambda g, c: (0, 0)),
            hp=pl.BlockSpec((1, 1, gw, n), lambda g, c: (g, cidx(c), 0, 0)),
            bc=pl.BlockSpec((L, n), lambda g, c: (cidx(c), g)),
        )
    return mk


def ssd_fwd(xbc, zx, dtg, bias_g, alog_g, d_chan, ngain, L, G, *, name):
    S = xbc.shape[0]
    n = SSM_STATE
    inner = xbc.shape[1] - 2 * G * n
    gw = inner // G
    pd = SB_HEAD_DIM
    hpg = gw // pd
    nc = S // L
    sp = _ssd_specs(S, L, G, hpg, pd, inner)(lambda c: c)

    def body(x_ref, b_ref, c_ref, z_ref, dt_ref, bias_ref, alog_ref, d_ref, ng_ref, tl_ref,
             yn_ref, y_ref, hp_ref, state):
        @pl.when(pl.program_id(1) == 0)
        def _():
            state[...] = jnp.zeros_like(state)

        dt, _, ac, act, ea, dte, cd = _ssd_chunk_terms(dt_ref[...], bias_ref[0], alog_ref[0], tl_ref[...])
        xv = x_ref[...]
        bm = b_ref[...].astype(BF16)
        cm = c_ref[...].astype(BF16)
        cb = _dot(cm, bm, 1, 1)
        tril = lax.broadcasted_iota(jnp.int32, (L, L), 1) <= lax.broadcasted_iota(jnp.int32, (L, L), 0)
        hp_ref[0, 0] = state[...]
        for r in range(hpg):
            ps = slice(r * pd, (r + 1) * pd)
            xr = xv[:, ps]
            xdt = xr * dt[:, r:r + 1]
            lm = jnp.exp(jnp.where(tril, ac[:, r:r + 1] - act[r:r + 1, :], -jnp.inf))
            hprev = state[ps, :]
            y = _dot((cb * lm).astype(BF16), xdt.astype(BF16))
            y = y + _dot(cm, hprev.astype(BF16), 1, 1) * ea[:, r:r + 1]
            y_ref[:, ps] = y + xr * d_ref[:, ps]
            st = _dot((xdt * dte[:, r:r + 1]).astype(BF16), bm, 0, 0)
            state[ps, :] = hprev * cd[:, r:r + 1] + st
        yfull = y_ref[...]
        zg = z_ref[...]
        yg = yfull * (zg * _sigmoid(zg))
        yn_ref[...] = (yg * lax.rsqrt(_row_mean_filled(yg * yg) + EPS) * ng_ref[...]).astype(BF16)

    return pl.pallas_call(
        body, name=name, grid=(G, nc),
        in_specs=[sp["x"], sp["b"], sp["c"], sp["z"], sp["dt"], sp["gvec"], sp["gvec"], sp["chan"], sp["chan"], sp["tri"]],
        out_specs=[sp["x"], sp["x"], sp["hp"]],
        out_shape=[jax.ShapeDtypeStruct((S, inner), BF16), jax.ShapeDtypeStruct((S, inner), F32),
                   jax.ShapeDtypeStruct((G, nc, gw, n), F32)],
        scratch_shapes=[pltpu.VMEM((gw, n), F32)],
        compiler_params=_params(("arbitrary", "arbitrary")),
    )(xbc, xbc, xbc, zx, dtg, bias_g, alog_g, d_chan, ngain, _tri(L, "row_ge_col"))


def ssd_bwd(xbc, zx, dtg, bias_g, alog_g, d_chan, ngain, yfull, hp, dyn, L, G, *, name):
    S = xbc.shape[0]
    n = SSM_STATE
    inner = xbc.shape[1] - 2 * G * n
    gw = inner // G
    pd = SB_HEAD_DIM
    hpg = gw // pd
    nc = S // L
    sp = _ssd_specs(S, L, G, hpg, pd, inner)(lambda c: nc - 1 - c)

    def body(x_ref, b_ref, c_ref, z_ref, dt_ref, bias_ref, alog_ref, d_ref, ng_ref, tl_ref, tu_ref,
             yf_ref, hp_ref, dyn_ref,
             dz_ref, dx_ref, db_ref, dc_ref, ddt_ref, dbias_ref, dalog_ref, dd_ref, dng_ref, dstate):
        first = pl.program_id(1) == 0

        @pl.when(first)
        def _():
            dstate[...] = jnp.zeros_like(dstate)
            dbias_ref[...] = jnp.zeros_like(dbias_ref)
            dalog_ref[...] = jnp.zeros_like(dalog_ref)
            dd_ref[...] = jnp.zeros_like(dd_ref)
            dng_ref[...] = jnp.zeros_like(dng_ref)

        dtraw = dt_ref[...]
        dt, a_neg, ac, act, ea, dte, cd = _ssd_chunk_terms(dtraw, bias_ref[0], alog_ref[0], tl_ref[...])
        xv = x_ref[...]
        bm = b_ref[...].astype(BF16)
        cm = c_ref[...].astype(BF16)
        cb = _dot(cm, bm, 1, 1)
        tril = lax.broadcasted_iota(jnp.int32, (L, L), 1) <= lax.broadcasted_iota(jnp.int32, (L, L), 0)
        lane = lax.broadcasted_iota(jnp.int32, (L, LANES), 1)
        lane1 = lax.broadcasted_iota(jnp.int32, (1, LANES), 1)

        yfull = yf_ref[...]
        zg = z_ref[...]
        sg = _sigmoid(zg)
        gate = zg * sg
        yg = yfull * gate
        rr = lax.rsqrt(_row_mean_filled(yg * yg) + EPS)
        yhat = yg * rr
        dynv = dyn_ref[...]
        dng_ref[...] += jnp.sum(dynv * yhat, axis=0, keepdims=True)
        t = dynv * ng_ref[...]
        dyg = rr * (t - yhat * _row_mean_filled(yhat * t))
        dy = dyg * gate
        dz_ref[...] = dyg * yfull * (sg * (1.0 + zg * (1.0 - sg)))

        dcb = jnp.zeros((L, L), F32)
        dc_acc = jnp.zeros((L, n), F32)
        db_acc = jnp.zeros((L, n), F32)
        dac = jnp.zeros((L, LANES), F32)
        xdx = jnp.zeros((L, LANES), F32)
        tail = jnp.zeros((1, LANES), F32)
        dskip = jnp.zeros((1, LANES), F32)
        ones_l = jnp.ones((L, LANES), BF16)
        for r in range(hpg):
            ps = slice(r * pd, (r + 1) * pd)
            xr = xv[:, ps]
            dyr = dy[:, ps]
            dtr = dt[:, r:r + 1]
            dter = dte[:, r:r + 1]
            cdr = cd[:, r:r + 1]
            xdt = xr * dtr
            xdtb = xdt.astype(BF16)
            dyrb = dyr.astype(BF16)
            lm = jnp.exp(jnp.where(tril, ac[:, r:r + 1] - act[r:r + 1, :], -jnp.inf))
            m32 = cb * lm
            mb = m32.astype(BF16)
            hprev = hp_ref[0, 0, ps, :]
            hpb = hprev.astype(BF16)
            dhn = dstate[ps, :]
            dhnb = dhn.astype(BF16)
            ear = ea[:, r:r + 1]
            gy = (dyr * ear).astype(BF16)
            dc_acc = dc_acc + _dot(gy, hpb)
            dstate[ps, :] = _dot(gy, cm, 0, 0) + dhn * cdr
            bdh = _dot(bm, dhnb, 1, 1)
            db_acc = db_acc + _dot((xdt * dter).astype(BF16), dhnb)
            dm = _dot(dyrb, xdtb, 1, 1)
            dxdt = bdh * dter + _dot(mb, dyrb, 0, 0)
            dcb = dcb + dm * lm
            wmat = dm * m32
            whi = wmat.astype(BF16)
            wlo = (wmat - whi.astype(F32)).astype(BF16)
            col_w = _dot(whi, ones_l, 0, 0) + _dot(wlo, ones_l, 0, 0)
            t_end = xdt * bdh * dter
            sel_l = (lax.broadcasted_iota(jnp.int32, (L, LANES), 1) == r).astype(BF16)
            sel_p = (lax.broadcasted_iota(jnp.int32, (pd, LANES), 1) == r).astype(BF16)
            e_r = _dot(whi, sel_l) + _dot(wlo, sel_l) + _dot_split2_right(dyr * _dot(cm, hpb, 1, 1) * ear - t_end, sel_p)
            c_r = cdr * _sum_all(dhn * hprev) + _sum_all(t_end)
            dac = dac + e_r - jnp.where(lane == r, col_w, 0.0)
            xdx = xdx + _dot_split2_right(dxdt * xr, sel_p)
            tail = tail + jnp.where(lane1 == r, c_r, 0.0)
            dskip = dskip + jnp.where(lane1 == r, _sum_all(dyr * xr), 0.0)
            dx_ref[:, ps] = dxdt * dtr + dyr * d_ref[:, ps]
        dcbb = dcb.astype(BF16)
        dc_ref[...] = dc_acc + _dot(dcbb, bm)
        db_ref[...] = db_acc + _dot(dcbb, cm, 0, 0)
        da = _dot_exact_left(tu_ref[...], dac) + tail
        real = lane < hpg
        ddt = jnp.where(real, (da * a_neg + xdx) * _sigmoid(dtraw + bias_ref[0]), 0.0)
        ddt_ref[...] = ddt
        dd_ref[0] += dskip
        dbias_ref[0] += jnp.sum(ddt, axis=0, keepdims=True)
        dalog_ref[0] += jnp.where(lane1 < hpg, jnp.sum(da * dt, axis=0, keepdims=True) * a_neg, 0.0)

    return pl.pallas_call(
        body, name=name, grid=(G, nc),
        in_specs=[sp["x"], sp["b"], sp["c"], sp["z"], sp["dt"], sp["gvec"], sp["gvec"], sp["chan"], sp["chan"],
                  sp["tri"], sp["tri"], sp["x"], sp["hp"], sp["x"]],
        out_specs=[sp["x"], sp["x"], sp["bc"], sp["bc"], sp["dt"], sp["gvec"], sp["gvec"], sp["gvec"], sp["chan"]],
        out_shape=[jax.ShapeDtypeStruct((S, inner), F32), jax.ShapeDtypeStruct((S, inner), F32),
                   jax.ShapeDtypeStruct((S, G * n), F32), jax.ShapeDtypeStruct((S, G * n), F32),
                   jax.ShapeDtypeStruct((S, G * LANES), F32), jax.ShapeDtypeStruct((G, 1, LANES), F32),
                   jax.ShapeDtypeStruct((G, 1, LANES), F32), jax.ShapeDtypeStruct((G, 1, LANES), F32),
                   jax.ShapeDtypeStruct((1, inner), F32)],
        scratch_shapes=[pltpu.VMEM((gw, n), F32)],
        compiler_params=_params(("arbitrary", "arbitrary")),
    )(xbc, xbc, xbc, zx, dtg, bias_g, alog_g, d_chan, ngain, _tri(L, "row_ge_col"), _tri(L, "row_le_col"),
      yfull, hp, dyn)


def _spread_dt(w_dt_t, G, hpg):
    K = w_dt_t.shape[1]
    w = w_dt_t.reshape(G, hpg, K)
    return jnp.pad(w, ((0, 0), (0, LANES - hpg), (0, 0))).reshape(G * LANES, K)


def _group_vec(v, G, hpg):
    return jnp.pad(v.reshape(G, 1, hpg), ((0, 0), (0, 0), (0, LANES - hpg)))


def local_step(x, target, W, late=None):
    S, D = x.shape
    depth = W["mix_norm"].shape[0]
    gm_groups, gm_chunk = W["gm_w_s"].shape[1], W["gm_w_s"].shape[2]
    heads = W["ssm_dt_bias"].shape[1]
    inner = heads * SB_HEAD_DIM
    L = gm_chunk
    received = None

    saved = []
    for i in range(depth):
        kind, j = i % 3, i // 3
        s = dict(x=x)
        h = rms_fwd(x, W["mix_norm"][i:i + 1], name=f"rms_mix_fwd")
        s["h"] = h
        if kind == 0:
            qkv = matmul(h, W["sb_w_qkv"][j], tb=True, name="mm_qkv")
            if late is not None and i == 0:
                o, gathered = sb_attn_fwd(qkv, W["sb_q_gain"][j:j + 1], W["sb_k_gain"][j:j + 1], name="sb_fwd_gather",
                                          gather=late.shard)
                late.fill(W, gathered)
            else:
                o = sb_attn_fwd(qkv, W["sb_q_gain"][j:j + 1], W["sb_k_gain"][j:j + 1], name="sb_fwd")
            x1 = matmul(o, W["sb_w_o"][j], residual=x, name="mm_sb_out")
            s.update(qkv=qkv, o=o)
        elif kind == 1:
            wc = jnp.where(jnp.tril(jnp.ones((gm_chunk, gm_chunk), bool)), W["gm_w_s"][j], 0.0).astype(BF16)
            bsf = jnp.broadcast_to(W["gm_b_s"][j][:, :, None], (gm_groups, gm_chunk, W["gm_v_gain"].shape[1] // gm_groups)).astype(F32)
            zzpre = matmul(h, W["gm_w_in"][j], tb=True, name="mm_gm_in")
            p = gmlp_fwd(zzpre, W["gm_b_in"][j:j + 1], W["gm_v_gain"][j:j + 1], wc, bsf, name="gm_fwd")
            x1 = matmul(p, W["gm_w_out"][j], residual=x, name="mm_gm_out")
            s.update(zzpre=zzpre, p=p, wc=wc, bsf=bsf)
        else:
            conv_dim = W["ssm_conv_w"].shape[2]
            G = (conv_dim - inner) // (2 * SSM_STATE)
            hpg = heads // G
            w_in = W["ssm_w_in"][j]
            w_zx = w_in[:inner + conv_dim]
            w_dtg = _spread_dt(w_in[inner + conv_dim:], G, hpg)
            bias_g = _group_vec(W["ssm_dt_bias"][j], G, hpg)
            alog_g = _group_vec(W["ssm_a_log"][j], G, hpg)
            d_chan = jnp.repeat(W["ssm_d"][j], SB_HEAD_DIM)[None, :]
            ngain = W["ssm_norm_gain"][j:j + 1]
            zx = matmul(h, w_zx, tb=True, name="mm_ssm_zx")
            dtg = matmul(h, w_dtg, tb=True, name="mm_ssm_dt")
            xbc = conv_fwd(zx, W["ssm_conv_w"][j], W["ssm_conv_b"][j:j + 1], inner, name="conv_fwd")
            yn, yfull, hp = ssd_fwd(xbc, zx, dtg, bias_g, alog_g, d_chan, ngain, L, G, name="ssd_fwd")
            x1 = matmul(yn, W["ssm_w_out"][j], residual=x, name="mm_ssm_out")
            s.update(w_zx=w_zx, w_dtg=w_dtg, bias_g=bias_g, alog_g=alog_g, d_chan=d_chan, ngain=ngain,
                     zx=zx, dtg=dtg, xbc=xbc, yn=yn, yfull=yfull, hp=hp)
        h2 = rms_fwd(x1, W["ffn_norm"][i:i + 1], name="rms_ffn_fwd")
        gu, a = ffn_up_fwd(h2, W["ffn_w_gu"][i], name="ffn_up_fwd")
        x2 = matmul(a, W["ffn_w_down"][i], residual=x1, name="mm_ffn_down")
        s.update(x1=x1, h2=h2, gu=gu, a=a)
        saved.append(s)
        x = x2

    dx, loss = loss_head(x, target, name="loss_head")

    gw = {k: {} for k in WEIGHTS}
    for i in reversed(range(depth)):
        kind, j = i % 3, i // 3
        s = saved[i]
        gw["ffn_w_down"][i] = matmul(s["a"], dx, ta=True, out_dtype=BF16, name="mm_ffn_dwdown")
        dgu = ffn_up_bwd(dx, W["ffn_w_down"][i], s["gu"], name="ffn_up_bwd")
        dh2 = matmul(dgu, W["ffn_w_gu"][i], a_split=2, name="mm_ffn_dh")
        gw["ffn_w_gu"][i] = matmul(dgu, s["h2"], ta=True, a_split=2, out_dtype=BF16, name="mm_ffn_dwgu")
        dx1, dgn = rms_bwd(s["x1"], W["ffn_norm"][i:i + 1], dh2, dx, name="rms_ffn_bwd")
        gw["ffn_norm"][i] = dgn[0]
        if kind == 0:
            do = matmul(dx1, W["sb_w_o"][j], tb=True, name="mm_sb_do")
            gw["sb_w_o"][j] = matmul(s["o"], dx1, ta=True, out_dtype=BF16, name="mm_sb_dwo")
            if late is not None and i == 0:
                dqkv, dqg, dkg, received = sb_attn_bwd(
                    s["qkv"], s["o"], do, W["sb_q_gain"][j:j + 1], W["sb_k_gain"][j:j + 1], name="sb_bwd_scatter",
                    scatter=late.contributions(gw))
            else:
                dqkv, dqg, dkg = sb_attn_bwd(s["qkv"], s["o"], do, W["sb_q_gain"][j:j + 1], W["sb_k_gain"][j:j + 1],
                                             name="sb_bwd")
            gw["sb_q_gain"][j] = dqg[0]
            gw["sb_k_gain"][j] = dkg[0]
            dh = matmul(dqkv, W["sb_w_qkv"][j], a_split=3, name="mm_sb_dh")
            gw["sb_w_qkv"][j] = matmul(dqkv, s["h"], ta=True, a_split=3, out_dtype=BF16, name="mm_sb_dwqkv")
        elif kind == 1:
            dp = matmul(dx1, W["gm_w_out"][j], tb=True, name="mm_gm_dp")
            gw["gm_w_out"][j] = matmul(s["p"], dx1, ta=True, out_dtype=BF16, name="mm_gm_dwout")
            dzz, db_in, dvg, dws, dbs = gmlp_bwd(s["zzpre"], W["gm_b_in"][j:j + 1], W["gm_v_gain"][j:j + 1],
                                                s["wc"], s["bsf"], dp, name="gm_bwd")
            gw["gm_b_in"][j] = db_in[0]
            gw["gm_v_gain"][j] = dvg[0]
            gw["gm_w_s"][j] = dws
            gw["gm_b_s"][j] = dbs[:, :gm_groups].T
            dh = matmul(dzz, W["gm_w_in"][j], name="mm_gm_dh")
            gw["gm_w_in"][j] = matmul(dzz, s["h"], ta=True, out_dtype=BF16, name="mm_gm_dwin")
        else:
            conv_dim = W["ssm_conv_w"].shape[2]
            G = (conv_dim - inner) // (2 * SSM_STATE)
            hpg = heads // G
            dyn = matmul(dx1, W["ssm_w_out"][j], tb=True, name="mm_ssm_dyn")
            gw["ssm_w_out"][j] = matmul(s["yn"], dx1, ta=True, out_dtype=BF16, name="mm_ssm_dwout")
            dz, dxs, dbm, dcm, ddt, dbias, dalog, dd, dng = ssd_bwd(
                s["xbc"], s["zx"], s["dtg"], s["bias_g"], s["alog_g"], s["d_chan"], s["ngain"], s["yfull"], s["hp"],
                dyn, L, G, name="ssd_bwd")
            dxbc = jnp.concatenate([dxs, dbm, dcm], axis=1)
            dpre, dcw, dcb = conv_bwd(s["zx"], W["ssm_conv_w"][j], W["ssm_conv_b"][j:j + 1], inner, dxbc,
                                      name="conv_bwd")
            dzx = jnp.concatenate([dz, dpre], axis=1)
            dh = matmul(ddt, s["w_dtg"], name="mm_ssm_dh_dt")
            dh = matmul(dzx, s["w_zx"], residual=dh, name="mm_ssm_dh")
            dw_zx = matmul(dzx, s["h"], ta=True, out_dtype=BF16, name="mm_ssm_dwzx")
            dw_dtg = matmul(ddt, s["h"], ta=True, out_dtype=BF16, name="mm_ssm_dwdt")
            dw_dt = dw_dtg.reshape(G, LANES, D)[:, :hpg, :].reshape(heads, D)
            gw["ssm_w_in"][j] = jnp.concatenate([dw_zx, dw_dt], axis=0)
            gw["ssm_conv_w"][j] = dcw
            gw["ssm_conv_b"][j] = dcb[0]
            gw["ssm_dt_bias"][j] = dbias[:, 0, :hpg].reshape(heads)
            gw["ssm_a_log"][j] = dalog[:, 0, :hpg].reshape(heads)
            gw["ssm_d"][j] = dd[:, 0, :hpg].reshape(heads)
            gw["ssm_norm_gain"][j] = dng[0]
        dx, dgn = rms_bwd(s["x"], W["mix_norm"][i:i + 1], dh, dx1, name="rms_mix_bwd")
        gw["mix_norm"][i] = dgn[0]

    return loss, dx, gw, received


MESH = pl.DeviceIdType.MESH
HBM_SPEC = pl.BlockSpec(memory_space=pltpu.HBM)
VMEM_SPEC = pl.BlockSpec(memory_space=pltpu.VMEM)


def _my_position():
    return lax.axis_index("x"), lax.axis_index("y"), lax.axis_index("c")


def _flip(v, bit):
    return 1 - v if bit else v


def all_gather_packed(shard, *, name):
    R, C = shard.shape

    def body(x_ref, out_ref, send_sems, recv_sems, local_sem):
        refs = (x_ref, out_ref, send_sems, recv_sems, local_sem)
        _ag_phase(refs, "start")
        _ag_phase(refs, "forward")
        _ag_phase(refs, "finish")

    return pl.pallas_call(
        body, name=name, out_shape=jax.ShapeDtypeStruct((N_DEV, R, C), shard.dtype),
        in_specs=[HBM_SPEC], out_specs=HBM_SPEC, scratch_shapes=COMM_SEMAPHORES,
    )(shard)


COMM_SEMAPHORES = [pltpu.SemaphoreType.DMA((7,)), pltpu.SemaphoreType.DMA((7,)), pltpu.SemaphoreType.DMA]


def _ag_phase(refs, phase):
    x_ref, out_ref, send_sems, recv_sems, local_sem = refs
    x, y, c = _my_position()
    me, sibling = (x, y, c), (x, y, 1 - c)
    chips = [(1 - x, y), (x, 1 - y), (1 - x, 1 - y)]

    def slot(px, py, pc):
        return out_ref.at[4 * px + 2 * py + pc]

    def copy(k, block, to, src=None):
        return pltpu.make_async_remote_copy(
            src_ref=slot(*block) if src is None else src, dst_ref=slot(*block),
            send_sem=send_sems.at[k], recv_sem=recv_sems.at[k], device_id=to, device_id_type=MESH)

    mine = pltpu.make_async_copy(x_ref, slot(*me), local_sem)
    first = [copy(0, me, sibling, src=x_ref)]
    first += [copy(1 + j, me, (*chip, c), src=x_ref) for j, chip in enumerate(chips)]
    passed = [copy(4 + j, (*chip, c), sibling) for j, chip in enumerate(chips)]
    if phase == "start":
        mine.start()
        for cp in first:
            cp.start()
    elif phase == "forward":
        for j, chip in enumerate(chips):
            copy(1 + j, (*chip, c), me).wait_recv()
            passed[j].start()
    else:
        copy(0, sibling, me).wait_recv()
        for j, chip in enumerate(chips):
            copy(4 + j, (*chip, 1 - c), me).wait_recv()
        for cp in first + passed:
            cp.wait_send()
        mine.wait()


def _rs_semaphores(n):
    return [pltpu.SemaphoreType.DMA((7 * n,)), pltpu.SemaphoreType.DMA((7 * n,)), pltpu.SemaphoreType.DMA((n,))]


def _rs_phase(g_refs, out_refs, sems, phase):
    send_sems, recv_sems, local_sems = sems
    x, y, c = _my_position()
    me = 4 * x + 2 * y + c
    copies = []
    for p, (g_ref, out_ref) in enumerate(zip(g_refs, out_refs)):
        copies.append(pltpu.make_async_copy(g_ref.at[me], out_ref.at[me], local_sems.at[p]))
        for k in range(1, N_DEV):
            px, py, pc = _flip(x, k & 4), _flip(y, k & 2), _flip(c, k & 1)
            copies.append(pltpu.make_async_remote_copy(
                src_ref=g_ref.at[4 * px + 2 * py + pc], dst_ref=out_ref.at[me],
                send_sem=send_sems.at[7 * p + k - 1], recv_sem=recv_sems.at[7 * p + k - 1],
                device_id=(px, py, pc), device_id_type=MESH))
    for cp in copies:
        if phase == "start":
            cp.start()
        else:
            cp.wait()


def exchange_for_reduce_scatter(gs, *, name):
    n = len(gs)

    def body(*refs):
        for phase in ("start", "finish"):
            _rs_phase(refs[:n], refs[n:2 * n], refs[2 * n:], phase)

    return pl.pallas_call(
        body, name=name, out_shape=[jax.ShapeDtypeStruct(g.shape, g.dtype) for g in gs],
        in_specs=[HBM_SPEC] * n, out_specs=[HBM_SPEC] * n, scratch_shapes=_rs_semaphores(n),
    )(*gs)


def sum_slots(recv, *, name):
    n, R, C = recv.shape
    tr = _pick(R, (512, 256, 128))

    def body(r_ref, o_ref):
        acc = r_ref[0].astype(F32)
        for s in range(1, n):
            acc = acc + r_ref[s].astype(F32)
        o_ref[...] = acc

    return pl.pallas_call(
        body, name=name, grid=(R // tr,), in_specs=[pl.BlockSpec((n, tr, C), lambda i: (0, i, 0))],
        out_specs=pl.BlockSpec((tr, C), lambda i: (i, 0)), out_shape=jax.ShapeDtypeStruct((R, C), F32),
        compiler_params=_params(("parallel",)),
    )(recv)


def all_reduce_small(v, *, name):
    R, C = v.shape

    def body(v_ref, o_ref, buf, send_sems, recv_sems):
        x, y, c = _my_position()
        me = 4 * x + 2 * y + c
        buf[me] = v_ref[...]
        copies = []
        for k in range(1, N_DEV):
            px, py, pc = _flip(x, k & 4), _flip(y, k & 2), _flip(c, k & 1)
            copies.append(pltpu.make_async_remote_copy(
                src_ref=v_ref, dst_ref=buf.at[me], send_sem=send_sems.at[k - 1], recv_sem=recv_sems.at[k - 1],
                device_id=(px, py, pc), device_id_type=MESH))
        for cp in copies:
            cp.start()
        for cp in copies:
            cp.wait()
        acc = buf[0]
        for s in range(1, N_DEV):
            acc = acc + buf[s]
        o_ref[...] = acc

    return pl.pallas_call(
        body, name=name, out_shape=jax.ShapeDtypeStruct((R, C), F32), in_specs=[VMEM_SPEC], out_specs=VMEM_SPEC,
        scratch_shapes=[pltpu.VMEM((N_DEV, R, C), F32), pltpu.SemaphoreType.DMA((7,)), pltpu.SemaphoreType.DMA((7,))],
        compiler_params=pltpu.CompilerParams(vmem_limit_bytes=VMEM_LIMIT_BYTES),
    )(v)


def _pad_rows(a, mult):
    pad = (-a.shape[0]) % mult
    return jnp.pad(a, ((0, pad), (0, 0))) if pad else a


def _pack_small(arrays):
    flat = []
    for a in arrays:
        f = a.reshape(-1).astype(F32)
        flat.append(jnp.pad(f, (0, (-f.shape[0]) % LANES)))
    return _pad_rows(jnp.concatenate(flat).reshape(-1, LANES), 8)


def _unpack_small(packed, shapes):
    flat = packed.reshape(-1)
    out, r = [], 0
    for shp in shapes:
        n = math.prod(shp)
        out.append(flat[r:r + n].reshape(shp))
        r += n + (-n) % LANES
    return out


ARG_NAMES = ("x",) + WEIGHTS + ("loss_target",) + tuple("m_" + w for w in WEIGHTS) + tuple("v_" + w for w in WEIGHTS)


def kernel(x, mix_norm, ffn_norm, sb_w_qkv, sb_q_gain, sb_k_gain, sb_w_o, gm_w_in, gm_b_in, gm_v_gain, gm_w_s, gm_b_s, gm_w_out, ssm_w_in, ssm_conv_w, ssm_conv_b, ssm_dt_bias, ssm_a_log, ssm_d, ssm_norm_gain, ssm_w_out, ffn_w_gu, ffn_w_down, loss_target, m_mix_norm, m_ffn_norm, m_sb_w_qkv, m_sb_q_gain, m_sb_k_gain, m_sb_w_o, m_gm_w_in, m_gm_b_in, m_gm_v_gain, m_gm_w_s, m_gm_b_s, m_gm_w_out, m_ssm_w_in, m_ssm_conv_w, m_ssm_conv_b, m_ssm_dt_bias, m_ssm_a_log, m_ssm_d, m_ssm_norm_gain, m_ssm_w_out, m_ffn_w_gu, m_ffn_w_down, v_mix_norm, v_ffn_norm, v_sb_w_qkv, v_sb_q_gain, v_sb_k_gain, v_sb_w_o, v_gm_w_in, v_gm_b_in, v_gm_v_gain, v_gm_w_s, v_gm_b_s, v_gm_w_out, v_ssm_w_in, v_ssm_conv_w, v_ssm_conv_b, v_ssm_dt_bias, v_ssm_a_log, v_ssm_d, v_ssm_norm_gain, v_ssm_w_out, v_ffn_w_gu, v_ffn_w_down):
    given = dict(zip(ARG_NAMES, (x, mix_norm, ffn_norm, sb_w_qkv, sb_q_gain, sb_k_gain, sb_w_o, gm_w_in, gm_b_in, gm_v_gain, gm_w_s, gm_b_s, gm_w_out, ssm_w_in, ssm_conv_w, ssm_conv_b, ssm_dt_bias, ssm_a_log, ssm_d, ssm_norm_gain, ssm_w_out, ffn_w_gu, ffn_w_down, loss_target, m_mix_norm, m_ffn_norm, m_sb_w_qkv, m_sb_q_gain, m_sb_k_gain, m_sb_w_o, m_gm_w_in, m_gm_b_in, m_gm_v_gain, m_gm_w_s, m_gm_b_s, m_gm_w_out, m_ssm_w_in, m_ssm_conv_w, m_ssm_conv_b, m_ssm_dt_bias, m_ssm_a_log, m_ssm_d, m_ssm_norm_gain, m_ssm_w_out, m_ffn_w_gu, m_ffn_w_down, v_mix_norm, v_ffn_norm, v_sb_w_qkv, v_sb_q_gain, v_sb_k_gain, v_sb_w_o, v_gm_w_in, v_gm_b_in, v_gm_v_gain, v_gm_w_s, v_gm_b_s, v_gm_w_out, v_ssm_w_in, v_ssm_conv_w, v_ssm_conv_b, v_ssm_dt_bias, v_ssm_a_log, v_ssm_d, v_ssm_norm_gain, v_ssm_w_out, v_ffn_w_gu, v_ffn_w_down)))
    mx, my, mc = _my_position()
    me = 4 * mx + 2 * my + mc

    pieces = [(k, l) for k in BIG for l in range(given[k].shape[0])]
    early = [("sb_w_qkv", 0)]
    late_pieces = [p for p in pieces if p not in early]
    last = [("sb_w_qkv", 0)]
    main = [p for p in pieces if p not in last]
    row_mult = 256

    def rows_of(p):
        return math.prod(given[p[0]].shape[1:]) // PACK_COLS

    def pack_shards(ps, extra=()):
        parts = [(given[k][l].T if k in COL_SHARDED else given[k][l]).astype(BF16).reshape(-1, PACK_COLS)
                 for k, l in ps] + list(extra)
        return _pad_rows(jnp.concatenate(parts, axis=0), row_mult)

    def split_rows(packed, ps):
        out, r = [], 0
        for p in ps:
            out.append(packed[..., r:r + rows_of(p), :])
            r += rows_of(p)
        return out

    def piece_to_full(g, k):
        rows, cols = given[k].shape[1:]
        return g.reshape(N_DEV * cols, rows) if k in COL_SHARDED else g.reshape(N_DEV * rows, cols)

    def full_to_piece(full, k):
        return full.reshape(N_DEV, -1, PACK_COLS)

    def summed_to_shard(g, k):
        rows, cols = given[k].shape[1:]
        return g.reshape(cols, rows).T if k in COL_SHARDED else g.reshape(rows, cols)

    def contributions(gw, ps):
        return [full_to_piece(gw[k][l], k) for k, l in ps]

    sharded_small = [lax.bitcast_convert_type(given[k], BF16) for k in SMALL_SHARDED]
    tail = jnp.concatenate([a.reshape(-1) for a in sharded_small])
    tail = jnp.pad(tail, (0, (-tail.size) % PACK_COLS)).reshape(-1, PACK_COLS)

    W = {k: given[k] for k in SMALL if k not in SMALL_SHARDED}
    W.update({k: [None] * given[k].shape[0] for k in BIG})
    gathered_early = all_gather_packed(pack_shards(early), name="all_gather_early")
    for (k, l), g in zip(early, split_rows(gathered_early, early)):
        W[k][l] = piece_to_full(g, k)

    class Late:
        shard = pack_shards(late_pieces, extra=[tail])

        @staticmethod
        def fill(weights, gathered):
            for (k, l), g in zip(late_pieces, split_rows(gathered, late_pieces)):
                weights[k][l] = piece_to_full(g, k)
            r0 = sum(rows_of(p) for p in late_pieces)
            tail_g = gathered[:, r0:r0 + tail.shape[0], :].reshape(N_DEV, -1)
            off = 0
            for k, a in zip(SMALL_SHARDED, sharded_small):
                g = lax.bitcast_convert_type(tail_g[:, off:off + a.size].reshape((N_DEV,) + a.shape), F32)
                weights[k] = jnp.moveaxis(g, 0, -2).reshape(g.shape[1:-1] + (N_DEV * g.shape[-1],))
                off += a.size

        @staticmethod
        def contributions(gw):
            return contributions(gw, main)

    loss, gx, gw, received_main = local_step(given["x"][0], given["loss_target"][0], W, late=Late)
    received_last = exchange_for_reduce_scatter(contributions(gw, last), name="reduce_scatter_last")

    grads_small = {k: jnp.stack([gw[k][l] for l in sorted(gw[k])], axis=0) for k in SMALL}
    small_shapes = [grads_small[k].shape for k in SMALL] + [(1, 1)]
    reduced = all_reduce_small(_pack_small([grads_small[k] for k in SMALL] + [loss]), name="all_reduce_small")
    small_full = dict(zip(SMALL + ("loss",), _unpack_small(reduced, small_shapes)))

    g_piece = {}
    for grp, received in ((main, received_main), (last, received_last)):
        for p, r in zip(grp, received):
            g_piece[p] = summed_to_shard(sum_slots(r, name="reduce_scatter_sum"), p[0])
    out_g, out_d, out_m, out_v = {}, {}, {}, {}
    for k in BIG:
        g = jnp.stack([g_piece[(k, l)] for l in range(given[k].shape[0])], axis=0)
        out_g[k] = g
        out_d[k], out_m[k], out_v[k] = adamw(given[k], g, given["m_" + k], given["v_" + k], name="adamw_" + k)

    gsmall = {}
    for k in SMALL:
        g = small_full[k]
        if k in SMALL_SHARDED:
            n = given[k].shape[-1]
            g = lax.dynamic_slice_in_dim(g, me * n, n, axis=g.ndim - 1)
        gsmall[k] = g
    local_shapes = [given[k].shape for k in SMALL]
    dsm, nmsm, nvsm = adamw(_pack_small([given[k] for k in SMALL]), _pack_small([gsmall[k] for k in SMALL]),
                            _pack_small([given["m_" + k] for k in SMALL]), _pack_small([given["v_" + k] for k in SMALL]),
                            name="adamw_small")
    out_g.update(gsmall)
    for dst, src in ((out_d, dsm), (out_m, nmsm), (out_v, nvsm)):
        dst.update(zip(SMALL, _unpack_small(src, local_shapes)))

    return (small_full["loss"].reshape(()), gx[None],
            *[out_g[k] for k in WEIGHTS], *[out_d[k] for k in WEIGHTS],
            *[out_m[k] for k in WEIGHTS], *[out_v[k] for k in WEIGHTS])
```

```python
import math

import jax
import jax.numpy as jnp
from jax import lax
from jax.experimental import pallas as pl
from jax.experimental.pallas import tpu as pltpu

F32 = jnp.float32
BF16 = jnp.bfloat16
EPS = 1e-6
N_DEV = 8
SB_HEAD_DIM = 64
SB_FWD_QUERY_BLOCKS = 4
SB_BWD_QUERY_BLOCKS = 2
SSM_STATE = 128
SSM_CONV = 4
ADAM_LR = 0.001
ADAM_B1 = 0.9
ADAM_B2 = 0.999
ADAM_EPS = 1e-08
ADAM_WD = 0.01
ADAM_STEP = 10
VMEM_LIMIT_BYTES = 56 * 1024 * 1024
MATMUL_VMEM_BUDGET = 40 * 1024 * 1024
LANES = 128
PACK_COLS = 1024

BIG = ("sb_w_qkv", "sb_w_o", "gm_w_in", "gm_w_out", "ssm_w_in", "ssm_w_out", "ffn_w_gu", "ffn_w_down")
COL_SHARDED = ("sb_w_qkv", "gm_w_in", "ssm_w_in", "ffn_w_gu")
SMALL = ("mix_norm", "ffn_norm", "sb_q_gain", "sb_k_gain", "gm_b_in", "gm_v_gain", "gm_w_s", "gm_b_s",
         "ssm_conv_w", "ssm_conv_b", "ssm_dt_bias", "ssm_a_log", "ssm_d", "ssm_norm_gain")
SMALL_SHARDED = ("ssm_conv_w", "ssm_conv_b", "ssm_norm_gain")
WEIGHTS = ("mix_norm", "ffn_norm", "sb_w_qkv", "sb_q_gain", "sb_k_gain", "sb_w_o", "gm_w_in", "gm_b_in",
           "gm_v_gain", "gm_w_s", "gm_b_s", "gm_w_out", "ssm_w_in", "ssm_conv_w", "ssm_conv_b", "ssm_dt_bias",
           "ssm_a_log", "ssm_d", "ssm_norm_gain", "ssm_w_out", "ffn_w_gu", "ffn_w_down")


def _params(semantics=None):
    return pltpu.CompilerParams(dimension_semantics=semantics, vmem_limit_bytes=VMEM_LIMIT_BYTES)


def _pick(n, prefs):
    for t in prefs:
        if t <= n and n % t == 0:
            return t
    return n


def _dot(a, b, ca=1, cb=0):
    return lax.dot_general(a, b, (((ca,), (cb,)), ((), ())), preferred_element_type=F32)


def _split3(v):
    h1 = v.astype(BF16)
    r1 = v - h1.astype(F32)
    h2 = r1.astype(BF16)
    h3 = (r1 - h2.astype(F32)).astype(BF16)
    return h1, h2, h3


def _dot_exact_left(mat01, v):
    h1, h2, h3 = _split3(v)
    return _dot(mat01, h1) + _dot(mat01, h2) + _dot(mat01, h3)


def _dot_split2_right(v, mat01):
    hi = v.astype(BF16)
    lo = (v - hi.astype(F32)).astype(BF16)
    return _dot(hi, mat01) + _dot(lo, mat01)


def _sum_all(v):
    return jnp.sum(jnp.sum(v, axis=0, keepdims=True), axis=1, keepdims=True)


def _sigmoid(v):
    return 1.0 / (1.0 + jnp.exp(-v))


def _softplus(v):
    return jnp.maximum(v, 0.0) + jnp.log(1.0 + jnp.exp(-jnp.abs(v)))


def _erf(v):
    a = jnp.abs(v)
    t = 1.0 / (1.0 + 0.3275911 * a)
    poly = t * (0.254829592 + t * (-0.284496736 + t * (1.421413741 + t * (-1.453152027 + t * 1.061405429))))
    e = 1.0 - poly * jnp.exp(-a * a)
    return jnp.where(v < 0, -e, e)


def _gelu_and_grad(v):
    cdf = 0.5 * (1.0 + _erf(v * (1.0 / math.sqrt(2.0))))
    pdf = jnp.exp(-0.5 * v * v) * (1.0 / math.sqrt(2.0 * math.pi))
    return v * cdf, cdf + v * pdf


def matmul(a, b, *, ta=False, tb=False, out_dtype=F32, residual=None, a_split=1, b_split=1, name):
    if a_split > 1 and ta:
        assert a.shape[0] == a_split
        K, M = a.shape[1], a_split * a.shape[2]
    elif a_split > 1:
        assert a.shape[0] == a_split
        M, K = a.shape[1], a_split * a.shape[2]
    elif ta:
        K, M = a.shape
    else:
        M, K = a.shape
    if b_split > 1:
        assert not tb and b.shape[0] == b_split
        Kb, N = b.shape[1], b_split * b.shape[2]
    elif tb:
        N, Kb = b.shape
    else:
        Kb, N = b.shape
    assert K == Kb, (a.shape, b.shape, ta, tb)
    has_res = residual is not None
    tm = _pick(M // a_split if ta else M, (1024, 1408, 768, 512, 256, 128))
    tn = _pick(N // b_split, (1024, 1408, 1536, 768, 512, 256, 128))

    def vmem_bytes(tk):
        tiles = tm * tk * a.dtype.itemsize + tk * tn * b.dtype.itemsize
        outs = tm * tn * jnp.dtype(out_dtype).itemsize + (tm * tn * 4 if has_res else 0)
        return 2 * tiles + 2 * outs + (tm * tn * 4 if tk < K else 0)

    kp = K if ta else K // a_split
    tk = next((t for t in (K, 2048, 1408, 1024, 512, 256) if t <= kp and kp % t == 0 and vmem_bytes(t) <= MATMUL_VMEM_BUDGET),
              _pick(kp, (128,)))
    nk = K // tk
    if a_split > 1 and ta:
        nib = M // a_split // tm
        a_spec = pl.BlockSpec((None, tk, tm), lambda i, j, k: (i // nib, k, i % nib))
    elif a_split > 1:
        nkb = kp // tk
        a_spec = pl.BlockSpec((None, tm, tk), lambda i, j, k: (k // nkb, i, k % nkb))
    else:
        a_spec = pl.BlockSpec((tk, tm), lambda i, j, k: (k, i)) if ta else pl.BlockSpec((tm, tk), lambda i, j, k: (i, k))
    if b_split > 1:
        njb = N // b_split // tn
        b_spec = pl.BlockSpec((None, tk, tn), lambda i, j, k: (j // njb, k, j % njb))
    else:
        b_spec = pl.BlockSpec((tn, tk), lambda i, j, k: (j, k)) if tb else pl.BlockSpec((tk, tn), lambda i, j, k: (k, j))
    o_spec = pl.BlockSpec((tm, tn), lambda i, j, k: (i, j))
    ca, cb = (0 if ta else 1), (1 if tb else 0)

    def body(*refs):
        a_ref, b_ref = refs[:2]
        r_ref = refs[2] if has_res else None
        o_ref = refs[3] if has_res else refs[2]

        def finish(r):
            if has_res:
                r = r + r_ref[...]
            o_ref[...] = r.astype(out_dtype)

        def part():
            return _dot(a_ref[...].astype(BF16), b_ref[...].astype(BF16), ca, cb)

        if nk == 1:
            finish(part())
            return
        acc = refs[-1]
        k = pl.program_id(2)

        @pl.when(k == 0)
        def _():
            acc[...] = part()

        @pl.when(jnp.logical_and(k > 0, k < nk - 1))
        def _():
            acc[...] += part()

        @pl.when(k == nk - 1)
        def _():
            finish(acc[...] + part())

    in_specs = [a_spec, b_spec] + ([o_spec] if has_res else [])
    args = (a, b) + ((residual,) if has_res else ())
    return pl.pallas_call(
        body, name=name, grid=(M // tm, N // tn, nk), in_specs=in_specs, out_specs=o_spec,
        out_shape=jax.ShapeDtypeStruct((M, N), out_dtype),
        scratch_shapes=[pltpu.VMEM((tm, tn), F32)] if nk > 1 else [],
        compiler_params=_params(("parallel", "parallel", "arbitrary")),
    )(*args)


def rms_fwd(x, gain, *, name):
    S, D = x.shape
    tr = _pick(S, (512, 256, 128))

    def body(x_ref, g_ref, o_ref):
        xv = x_ref[...]
        r = lax.rsqrt(jnp.mean(xv * xv, axis=1, keepdims=True) + EPS)
        o_ref[...] = (xv * r * g_ref[...]).astype(BF16)

    return pl.pallas_call(
        body, name=name, grid=(S // tr,),
        in_specs=[pl.BlockSpec((tr, D), lambda i: (i, 0)), pl.BlockSpec((1, D), lambda i: (0, 0))],
        out_specs=pl.BlockSpec((tr, D), lambda i: (i, 0)), out_shape=jax.ShapeDtypeStruct((S, D), BF16),
        compiler_params=_params(("parallel",)),
    )(x, gain)


def rms_bwd(x, gain, dh, dres, *, name):
    S, D = x.shape
    tr = _pick(S, (512, 256, 128))

    def body(x_ref, g_ref, dh_ref, dr_ref, dx_ref, dg_ref):
        @pl.when(pl.program_id(0) == 0)
        def _():
            dg_ref[...] = jnp.zeros_like(dg_ref)

        xv = x_ref[...]
        dhv = dh_ref[...]
        r = lax.rsqrt(jnp.mean(xv * xv, axis=1, keepdims=True) + EPS)
        xhat = xv * r
        t = dhv * g_ref[...]
        dx_ref[...] = dr_ref[...] + r * (t - xhat * jnp.mean(xhat * t, axis=1, keepdims=True))
        dg_ref[...] += jnp.sum(dhv * xhat, axis=0, keepdims=True)

    row = pl.BlockSpec((tr, D), lambda i: (i, 0))
    vec = pl.BlockSpec((1, D), lambda i: (0, 0))
    return pl.pallas_call(
        body, name=name, grid=(S // tr,), in_specs=[row, vec, row, row], out_specs=[row, vec],
        out_shape=[jax.ShapeDtypeStruct((S, D), F32), jax.ShapeDtypeStruct((1, D), F32)],
        compiler_params=_params(("arbitrary",)),
    )(x, gain, dh, dres)


def ffn_up_fwd(h, w_gu_t, *, name):
    S, K = h.shape
    F = w_gu_t.shape[0] // 2
    tm = _pick(S, (512, 256, 128))
    tn = _pick(F, (1408, 1024, 768, 512, 256, 128))
    nj = F // tn

    def body(h_ref, wg_ref, wu_ref, gu_ref, a_ref):
        hv = h_ref[...]
        g = _dot(hv, wg_ref[...], 1, 1)
        u = _dot(hv, wu_ref[...], 1, 1)
        gu_ref[0] = g
        gu_ref[1] = u
        a_ref[...] = (g * _sigmoid(g) * u).astype(BF16)

    return pl.pallas_call(
        body, name=name, grid=(nj, S // tm),
        in_specs=[pl.BlockSpec((tm, K), lambda j, i: (i, 0)), pl.BlockSpec((tn, K), lambda j, i: (j, 0)),
                  pl.BlockSpec((tn, K), lambda j, i: (nj + j, 0))],
        out_specs=[pl.BlockSpec((2, tm, tn), lambda j, i: (0, i, j)), pl.BlockSpec((tm, tn), lambda j, i: (i, j))],
        out_shape=[jax.ShapeDtypeStruct((2, S, F), F32), jax.ShapeDtypeStruct((S, F), BF16)],
        compiler_params=_params(("parallel", "parallel")),
    )(h, w_gu_t, w_gu_t)


def ffn_up_bwd(dy, w_down, gu, *, name):
    S, D = dy.shape
    F = w_down.shape[0]
    tm = _pick(S, (512, 256, 128))
    tn = _pick(F, (1408, 1024, 768, 512, 256, 128))

    def body(dy_ref, wd_ref, gu_ref, o_ref):
        da = _dot(dy_ref[...].astype(BF16), wd_ref[...], 1, 1)
        g = gu_ref[0]
        u = gu_ref[1]
        s = _sigmoid(g)
        o_ref[0] = (da * u * (s * (1.0 + g * (1.0 - s)))).astype(BF16)
        o_ref[1] = (da * g * s).astype(BF16)

    pair = pl.BlockSpec((2, tm, tn), lambda j, i: (0, i, j))
    return pl.pallas_call(
        body, name=name, grid=(F // tn, S // tm),
        in_specs=[pl.BlockSpec((tm, D), lambda j, i: (i, 0)), pl.BlockSpec((tn, D), lambda j, i: (j, 0)), pair],
        out_specs=pair, out_shape=jax.ShapeDtypeStruct((2, S, F), BF16),
        compiler_params=_params(("parallel", "parallel")),
    )(dy, w_down, gu)


def loss_head(y, target, *, name):
    S, D = y.shape
    tr = _pick(S, (512, 256, 128))

    def body(y_ref, t_ref, dy_ref, l_ref):
        @pl.when(pl.program_id(0) == 0)
        def _():
            l_ref[...] = jnp.zeros_like(l_ref)

        err = y_ref[...] - t_ref[...]
        dy_ref[...] = err * (1.0 / D)
        l_ref[...] += jnp.sum(0.5 * jnp.mean(err * err, axis=1, keepdims=True), axis=0, keepdims=True)

    row = pl.BlockSpec((tr, D), lambda i: (i, 0))
    one = pl.BlockSpec((1, 1), lambda i: (0, 0))
    dy, l = pl.pallas_call(
        body, name=name, grid=(S // tr,), in_specs=[row, row], out_specs=[row, one],
        out_shape=[jax.ShapeDtypeStruct((S, D), F32), jax.ShapeDtypeStruct((1, 1), F32)],
        compiler_params=_params(("arbitrary",)),
    )(y, target)
    return dy, l


def adamw(w, g, m, v, *, name):
    R, C = w.shape[-2:]
    tr = _pick(R, (512, 256, 128, 64, 32, 16, 8))

    def body(w_ref, g_ref, m_ref, v_ref, d_ref, mo_ref, vo_ref):
        gv = g_ref[...]
        mn = ADAM_B1 * m_ref[...] + (1.0 - ADAM_B1) * gv
        vn = ADAM_B2 * v_ref[...] + (1.0 - ADAM_B2) * jnp.square(gv)
        m_hat = mn / (1.0 - ADAM_B1 ** ADAM_STEP)
        v_hat = vn / (1.0 - ADAM_B2 ** ADAM_STEP)
        d_ref[...] = -ADAM_LR * (m_hat / (jnp.sqrt(v_hat) + ADAM_EPS) + ADAM_WD * w_ref[...])
        mo_ref[...] = mn
        vo_ref[...] = vn

    if w.ndim == 3:
        grid = (w.shape[0], R // tr)
        blk = pl.BlockSpec((None, tr, C), lambda l, i: (l, i, 0))
    else:
        grid = (R // tr,)
        blk = pl.BlockSpec((tr, C), lambda i: (i, 0))
    sds = jax.ShapeDtypeStruct(w.shape, F32)
    return pl.pallas_call(
        body, name=name, grid=grid, in_specs=[blk] * 4, out_specs=[blk] * 3, out_shape=[sds] * 3,
        compiler_params=_params(("parallel",) * len(grid)),
    )(w, g, m, v)


def _tri(n, kind):
    r = lax.broadcasted_iota(jnp.int32, (n, n), 0)
    c = lax.broadcasted_iota(jnp.int32, (n, n), 1)
    if kind == "row_gt_col":
        return (r > c).astype(BF16)
    if kind == "row_ge_col":
        return (r >= c).astype(BF16)
    if kind == "row_le_col":
        return (r <= c).astype(BF16)
    raise ValueError(kind)


def _sb_tile(qs, kj, r_carry, u_strict, masked):
    z = _dot(qs, kj, 1, 1)
    lb = jnp.minimum(z, 0.0) - jnp.log(1.0 + jnp.exp(-jnp.abs(z)))
    l1m = lb - z
    keep = None
    if masked:
        tq, tk = z.shape
        keep = lax.broadcasted_iota(jnp.int32, (tq, tk), 1) < lax.broadcasted_iota(jnp.int32, (tq, tk), 0)
        l1m = jnp.where(keep, l1m, 0.0)
    w = jnp.exp(lb + _dot(l1m.astype(BF16), u_strict) + r_carry)
    if masked:
        w = jnp.where(keep, w, 0.0)
    return lb, l1m, w, keep


def _sb_prep(T, nb, hd, refs_in, gains, scratch):
    q_scale = 1.0 / math.sqrt(hd)
    assert math.log2(q_scale) == round(math.log2(q_scale))

    def prep(i, _):
        rows = pl.ds(pl.multiple_of(i * T, T), T)
        for hh in range(2):
            sl = slice(hd * hh, hd * hh + hd)
            for n, (src, dst) in enumerate(zip(refs_in, scratch)):
                v = src[rows, sl]
                if n < 2:
                    v = v * lax.rsqrt(jnp.mean(v * v, axis=1, keepdims=True) + EPS) * gains[n][...]
                if n == 0:
                    v = v * q_scale
                dst[hh, rows, :] = v.astype(BF16)
        return 0

    lax.fori_loop(0, nb, prep, 0)


def _sb_chains(m, T, nq):
    rows = [pl.ds(pl.multiple_of((nq * m + qb) * T, T), T) for qb in range(nq)]
    return rows, [(hh, qb) for qb in range(nq) for hh in range(2)]


def _sb_sweep(tile, carry, chains, m, nq):
    for kk in reversed(range(nq)):
        carry = tile(nq * m + kk, carry, [(ch, ch[1] == kk) for ch in chains if ch[1] >= kk])
    return lax.fori_loop(0, nq * m, lambda jj, c: tile(nq * m - 1 - jj, c, [(ch, False) for ch in chains]), carry)


def sb_attn_fwd(qkv, q_gain, k_gain, *, name, gather=None):
    S, D3 = qkv.shape
    D = D3 // 3
    npairs = D // LANES
    hd = SB_HEAD_DIM
    T = min(256, S)
    nb = S // T
    nq = SB_FWD_QUERY_BLOCKS
    assert nb % nq == 0

    def body(*refs):
        if gather is None:
            q_ref, k_ref, v_ref, qg_ref, kg_ref, us_ref, o_ref, qn_s, kn_s, vb_s = refs
        else:
            ng = len(gather)
            q_ref, k_ref, v_ref, qg_ref, kg_ref, us_ref = refs[:6]
            o_ref = refs[6 + ng]
            qn_s, kn_s, vb_s = refs[7 + 2 * ng:10 + 2 * ng]
            comm = (refs[6:6 + ng], refs[7 + ng:7 + 2 * ng], refs[10 + 2 * ng:])
            step = pl.program_id(0)
            pl.when(step == 0)(lambda: _ag_phase(*comm, "start"))
            pl.when(step == npairs - 1)(lambda: _ag_phase(*comm, "forward"))
        us = us_ref[...]
        _sb_prep(T, nb, hd, (q_ref, k_ref, v_ref), (qg_ref, kg_ref), (qn_s, kn_s, vb_s))

        def superblock(m, _):
            rows_q, chains = _sb_chains(m, T, nq)
            qs = {ch: qn_s[ch[0], rows_q[ch[1]], :] for ch in chains}

            def tile(j, carry, which):
                rows_j = pl.ds(pl.multiple_of(j * T, T), T)
                new = dict(carry)
                for ch, masked in which:
                    acc, rc = carry[ch]
                    _, l1m, w, _ = _sb_tile(qs[ch], kn_s[ch[0], rows_j, :], rc, us, masked)
                    new[ch] = (acc + _dot(w.astype(BF16), vb_s[ch[0], rows_j, :]),
                               rc + jnp.sum(l1m, axis=1, keepdims=True))
                return new

            carry = {ch: (jnp.zeros((T, hd), F32), jnp.zeros((T, 1), F32)) for ch in chains}
            carry = _sb_sweep(tile, carry, chains, m, nq)
            for qb in range(nq):
                o_ref[rows_q[qb], :] = jnp.concatenate([carry[(0, qb)][0], carry[(1, qb)][0]], axis=1)
            return 0

        lax.fori_loop(0, nb // nq, superblock, 0)
        if gather is not None:
            pl.when(step == npairs - 1)(lambda: _ag_phase(*comm, "finish"))

    col = lambda off: pl.BlockSpec((S, LANES), lambda p, off=off: (0, off + p))
    gain = pl.BlockSpec((1, hd), lambda p: (0, 0))
    in_specs = [col(0), col(npairs), col(2 * npairs), gain, gain, pl.BlockSpec((T, T), lambda p: (0, 0))]
    out_specs = [pl.BlockSpec((S, LANES), lambda p: (0, p))]
    out_shape = [jax.ShapeDtypeStruct((S, D), F32)]
    scratch = [pltpu.VMEM((2, S, hd), BF16)] * 3
    args = [qkv, qkv, qkv, q_gain, k_gain, _tri(T, "row_gt_col")]
    if gather is not None:
        in_specs += [HBM_SPEC] * len(gather)
        out_specs += [HBM_SPEC] * len(gather)
        out_shape += [jax.ShapeDtypeStruct((N_DEV,) + s.shape, s.dtype) for s in gather]
        scratch += _rs_semaphores(len(gather))
        args += list(gather)
    out = pl.pallas_call(
        body, name=name, grid=(npairs,), in_specs=in_specs, out_specs=out_specs, out_shape=out_shape,
        scratch_shapes=scratch, compiler_params=_params(("arbitrary",)),
    )(*args)
    return out[0] if gather is None else (out[0], list(out[1:]))


def sb_attn_bwd(qkv, o, do, q_gain, k_gain, *, name, scatter=None):
    S, D3 = qkv.shape
    D = D3 // 3
    npairs = D // LANES
    hd = SB_HEAD_DIM
    T = min(256, S)
    nb = S // T
    nq = SB_BWD_QUERY_BLOCKS
    assert nb % nq == 0
    scale = 1.0 / math.sqrt(hd)

    def body(*refs):
        if scatter is None:
            (q_ref, k_ref, v_ref, o_ref, do_ref, qg_ref, kg_ref, us_ref,
             dqkv_ref, dg_ref, qn_s, kn_s, vb_s, dob_s) = refs
        else:
            ns = len(scatter)
            q_ref, k_ref, v_ref, o_ref, do_ref, qg_ref, kg_ref, us_ref = refs[:8]
            rs_in = refs[8:8 + ns]
            dqkv_ref, dg_ref = refs[8 + ns:10 + ns]
            rs_out = refs[10 + ns:10 + 2 * ns]
            qn_s, kn_s, vb_s, dob_s = refs[10 + 2 * ns:14 + 2 * ns]
            rs_sems = refs[14 + 2 * ns:]
            pl.when(pl.program_id(0) == 0)(lambda: _rs_phase(rs_in, rs_out, rs_sems, "start"))
        dq_ref, dk_ref, dv_ref = dqkv_ref.at[0], dqkv_ref.at[1], dqkv_ref.at[2]

        @pl.when(pl.program_id(0) == 0)
        def _():
            dg_ref[...] = jnp.zeros_like(dg_ref)

        us = us_ref[...]
        u_prefix = (1.0 - us.astype(F32)).astype(BF16)
        _sb_prep(T, nb, hd, (q_ref, k_ref, v_ref, do_ref), (qg_ref, kg_ref), (qn_s, kn_s, vb_s, dob_s))
        dk_ref[...] = jnp.zeros_like(dk_ref)
        dv_ref[...] = jnp.zeros_like(dv_ref)

        def superblock(m, _):
            rows_q, chains = _sb_chains(m, T, nq)
            qs = {ch: qn_s[ch[0], rows_q[ch[1]], :] for ch in chains}
            doi ={ch: dob_s[ch[0], rows_q[ch[1]], :] for ch in chains}
            dt_total = {ch: jnp.sum(doi[ch].astype(F32) * o_ref[rows_q[ch[1]], hd * ch[0]:hd * ch[0] + hd],
                                    axis=1, keepdims=True) for ch in chains}

            def tile(j, carry, which):
                rows_j = pl.ds(pl.multiple_of(j * T, T), T)
                new = dict(carry)
                dk_part, dv_part = {}, {}
                for ch, masked in which:
                    hh = ch[0]
                    dq_acc, rc, gc = carry[ch]
                    kj = kn_s[hh, rows_j, :]
                    lb, l1m, w, keep = _sb_tile(qs[ch], kj, rc, us, masked)
                    wb = w.astype(BF16)
                    g = _dot(doi[ch], vb_s[hh, rows_j, :], 1, 1) * wb.astype(F32)
                    g_row = jnp.sum(g, axis=1, keepdims=True)
                    g_upto = (dt_total[ch] - gc - g_row) + _dot(g.astype(BF16), u_prefix)
                    dz = g - g_upto * jnp.exp(lb)
                    if masked:
                        dz = jnp.where(keep, dz, 0.0)
                    dzb = dz.astype(BF16)
                    dv_part[hh] = dv_part.get(hh, 0.0) + _dot(wb, doi[ch], 0, 0)
                    dk_part[hh] = dk_part.get(hh, 0.0) + _dot(dzb, qs[ch], 0, 0)
                    new[ch] = (dq_acc + _dot(dzb, kj), rc + jnp.sum(l1m, axis=1, keepdims=True),
                               gc + g_row)
                dv_ref[rows_j, :] += jnp.concatenate([dv_part[0], dv_part[1]], axis=1)
                dk_ref[rows_j, :] += jnp.concatenate([dk_part[0], dk_part[1]], axis=1)
                return new

            zero1 = jnp.zeros((T, 1), F32)
            carry = {ch: (jnp.zeros((T, hd), F32), zero1, zero1) for ch in chains}
            carry = _sb_sweep(tile, carry, chains, m, nq)
            for qb in range(nq):
                dq_ref[rows_q[qb], :] = jnp.concatenate([carry[(0, qb)][0], carry[(1, qb)][0]], axis=1) * scale
            return 0

        lax.fori_loop(0, nb // nq, superblock, 0)

        def finish(i, carry):
            rows = pl.ds(pl.multiple_of(i * T, T), T)
            new = []
            for hh in range(2):
                sl = slice(hd * hh, hd * hh + hd)
                outs = []
                for raw_ref, gain_ref, dn in ((q_ref, qg_ref, dq_ref[rows, sl]), (k_ref, kg_ref, dk_ref[rows, sl])):
                    raw = raw_ref[rows, sl]
                    r = lax.rsqrt(jnp.mean(raw * raw, axis=1, keepdims=True) + EPS)
                    hat = raw * r
                    t = dn * gain_ref[...]
                    outs.append((r * (t - hat * jnp.mean(hat * t, axis=1, keepdims=True)),
                                 jnp.sum(dn * hat, axis=0, keepdims=True)))
                dq_ref[rows, sl] = outs[0][0]
                dk_ref[rows, sl] = outs[1][0]
                new.append((carry[hh][0] + outs[0][1], carry[hh][1] + outs[1][1]))
            return tuple(new)

        zg = (jnp.zeros((1, hd), F32), jnp.zeros((1, hd), F32))
        tot = lax.fori_loop(0, nb, finish, (zg, zg))
        dg_ref[0:1, 0:hd] += tot[0][0] + tot[1][0]
        dg_ref[1:2, 0:hd] += tot[0][1] + tot[1][1]
        if scatter is not None:
            pl.when(pl.program_id(0) == npairs - 1)(lambda: _rs_phase(rs_in, rs_out, rs_sems, "finish"))

    col = lambda off: pl.BlockSpec((S, LANES), lambda p, off=off: (0, off + p))
    gain = pl.BlockSpec((1, hd), lambda p: (0, 0))
    tri = pl.BlockSpec((T, T), lambda p: (0, 0))
    pair = pl.BlockSpec((S, LANES), lambda p: (0, p))
    in_specs = [col(0), col(npairs), col(2 * npairs), pair, pair, gain, gain, tri]
    out_specs = [pl.BlockSpec((3, S, LANES), lambda p: (0, 0, p)), pl.BlockSpec((8, LANES), lambda p: (0, 0))]
    out_shape = [jax.ShapeDtypeStruct((3, S, D), F32), jax.ShapeDtypeStruct((8, LANES), F32)]
    scratch = [pltpu.VMEM((2, S, hd), BF16)] * 4
    args = [qkv, qkv, qkv, o, do, q_gain, k_gain, _tri(T, "row_gt_col")]
    if scatter is not None:
        in_specs += [HBM_SPEC] * len(scatter)
        out_specs += [HBM_SPEC] * len(scatter)
        out_shape += [jax.ShapeDtypeStruct(g.shape, g.dtype) for g in scatter]
        scratch += _rs_semaphores(len(scatter))
        args += list(scatter)
    out = pl.pallas_call(
        body, name=name, grid=(npairs,), in_specs=in_specs, out_specs=out_specs, out_shape=out_shape,
        scratch_shapes=scratch, compiler_params=_params(("arbitrary",)),
    )(*args)
    res = (out[0], out[1][0:1, :hd], out[1][1:2, :hd])
    return res if scatter is None else res + (list(out[2:]),)


def gmlp_fwd(zzpre, b_in, v_gain, wc, bsf, *, name):
    S, H2 = zzpre.shape
    H = H2 // 2
    G, T, _ = wc.shape
    gd = H // G

    def body(z_ref, b_ref, vg_ref, wc_ref, bs_ref, p_ref):
        zz, _ = _gelu_and_grad(z_ref[...] + b_ref[...])
        u = zz[:, :H]
        v = zz[:, H:]
        vn = v * lax.rsqrt(jnp.mean(v * v, axis=1, keepdims=True) + EPS) * vg_ref[...]
        for g in range(G):
            gs = slice(g * gd, (g + 1) * gd)
            mixed = _dot(wc_ref[g], vn[:, gs].astype(BF16)) + bs_ref[g]
            p_ref[:, gs] = (u[:, gs] * mixed).astype(BF16)

    full3 = lambda shp: pl.BlockSpec(shp, lambda c: (0, 0, 0))
    return pl.pallas_call(
        body, name=name, grid=(S // T,),
        in_specs=[pl.BlockSpec((T, H2), lambda c: (c, 0)), pl.BlockSpec((1, H2), lambda c: (0, 0)),
                  pl.BlockSpec((1, H), lambda c: (0, 0)), full3((G, T, T)), full3((G, T, gd))],
        out_specs=pl.BlockSpec((T, H), lambda c: (c, 0)), out_shape=jax.ShapeDtypeStruct((S, H), BF16),
        compiler_params=_params(("parallel",)),
    )(zzpre, b_in, v_gain, wc, bsf)


def gmlp_bwd(zzpre, b_in, v_gain, wc, bsf, dp, *, name):
    S, H2 = zzpre.shape
    H = H2 // 2
    G, T, _ = wc.shape
    gd = H // G
    assert G <= LANES

    def body(z_ref, b_ref, vg_ref, wc_ref, bs_ref, dp_ref, dzz_ref, db_ref, dvg_ref, dws_ref, dbs_ref):
        @pl.when(pl.program_id(0) == 0)
        def _():
            db_ref[...] = jnp.zeros_like(db_ref)
            dvg_ref[...] = jnp.zeros_like(dvg_ref)
            dws_ref[...] = jnp.zeros_like(dws_ref)
            dbs_ref[...] = jnp.zeros_like(dbs_ref)

        zz, gp = _gelu_and_grad(z_ref[...] + b_ref[...])
        u = zz[:, :H]
        v = zz[:, H:]
        r = lax.rsqrt(jnp.mean(v * v, axis=1, keepdims=True) + EPS)
        vhat = v * r
        vg = vg_ref[...]
        vn = vhat * vg
        dpv = dp_ref[...]
        tril = lax.broadcasted_iota(jnp.int32, (T, T), 1) <= lax.broadcasted_iota(jnp.int32, (T, T), 0)
        lane = lax.broadcasted_iota(jnp.int32, (T, LANES), 1)
        dbs = jnp.zeros((T, LANES), F32)
        du_parts, dvn_parts = [], []
        for g in range(G):
            gs = slice(g * gd, (g + 1) * gd)
            vng = vn[:, gs].astype(BF16)
            wcg = wc_ref[g]
            mixed = _dot(wcg, vng) + bs_ref[g]
            dpg = dpv[:, gs]
            du_parts.append(dpg * mixed)
            dmx = dpg * u[:, gs]
            dmxb = dmx.astype(BF16)
            dvn_parts.append(_dot(wcg, dmxb, 0, 0))
            dws_ref[g] += jnp.where(tril, _dot(dmxb, vng, 1, 1), 0.0)
            dbs = dbs + jnp.where(lane == g, jnp.sum(dmx, axis=1, keepdims=True), 0.0)
        dbs_ref[...] += dbs
        du = jnp.concatenate(du_parts, axis=1)
        dvn = jnp.concatenate(dvn_parts, axis=1)
        dvg_ref[...] += jnp.sum(dvn * vhat, axis=0, keepdims=True)
        t = dvn * vg
        dv = r * (t - vhat * jnp.mean(vhat * t, axis=1, keepdims=True))
        dzu = du * gp[:, :H]
        dzv = dv * gp[:, H:]
        dzz_ref[:, :H] = dzu.astype(BF16)
        dzz_ref[:, H:] = dzv.astype(BF16)
        db_ref[:, :H] += jnp.sum(dzu, axis=0, keepdims=True)
        db_ref[:, H:] += jnp.sum(dzv, axis=0, keepdims=True)

    full3 = lambda shp: pl.BlockSpec(shp, lambda c: (0, 0, 0))
    vec = lambda n: pl.BlockSpec((1, n), lambda c: (0, 0))
    return pl.pallas_call(
        body, name=name, grid=(S // T,),
        in_specs=[pl.BlockSpec((T, H2), lambda c: (c, 0)), vec(H2), vec(H), full3((G, T, T)), full3((G, T, gd)),
                  pl.BlockSpec((T, H), lambda c: (c, 0))],
        out_specs=[pl.BlockSpec((T, H2), lambda c: (c, 0)), vec(H2), vec(H), full3((G, T, T)),
                   pl.BlockSpec((T, LANES), lambda c: (0, 0))],
        out_shape=[jax.ShapeDtypeStruct((S, H2), BF16), jax.ShapeDtypeStruct((1, H2), F32),
                   jax.ShapeDtypeStruct((1, H), F32), jax.ShapeDtypeStruct((G, T, T), F32),
                   jax.ShapeDtypeStruct((T, LANES), F32)],
        compiler_params=_params(("arbitrary",)),
    )(zzpre, b_in, v_gain, wc, bsf, dp)


def _shift_rows(v, k, n_rows):
    if k == 0:
        return v
    rolled = pltpu.roll(v, k % n_rows, 0)
    row = lax.broadcasted_iota(jnp.int32, v.shape, 0)
    keep = (row >= k) if k > 0 else (row < n_rows + k)
    return jnp.where(keep, rolled, 0.0)


def conv_fwd(zx, conv_w, conv_b, col0, *, name):
    S = zx.shape[0]
    C = conv_w.shape[1]
    tc = _pick(C, (256, 128))
    off = col0 // tc
    assert col0 % tc == 0

    def body(x_ref, w_ref, b_ref, o_ref):
        xv = x_ref[...]
        acc = b_ref[...] + w_ref[SSM_CONV - 1:SSM_CONV, :] * xv
        for k in range(SSM_CONV - 1):
            acc = acc + w_ref[k:k + 1, :] * _shift_rows(xv, SSM_CONV - 1 - k, S)
        o_ref[...] = acc * _sigmoid(acc)

    return pl.pallas_call(
        body, name=name, grid=(C // tc,),
        in_specs=[pl.BlockSpec((S, tc), lambda j: (0, off + j)), pl.BlockSpec((SSM_CONV, tc), lambda j: (0, j)),
                  pl.BlockSpec((1, tc), lambda j: (0, j))],
        out_specs=pl.BlockSpec((S, tc), lambda j: (0, j)), out_shape=jax.ShapeDtypeStruct((S, C), F32),
        compiler_params=_params(("parallel",)),
    )(zx, conv_w, conv_b)


def conv_bwd(zx, conv_w, conv_b, col0, dout, *, name):
    S = zx.shape[0]
    C = conv_w.shape[1]
    tc = _pick(C, (256, 128))
    off = col0 // tc

    def body(x_ref, w_ref, b_ref, do_ref, dx_ref, dw_ref, db_ref):
        xv = x_ref[...]
        shifted = [_shift_rows(xv, SSM_CONV - 1 - k, S) for k in range(SSM_CONV)]
        acc = b_ref[...]
        for k in range(SSM_CONV):
            acc = acc + w_ref[k:k + 1, :] * shifted[k]
        s = _sigmoid(acc)
        dacc = do_ref[...] * (s * (1.0 + acc * (1.0 - s)))
        db_ref[...] = jnp.sum(dacc, axis=0, keepdims=True)
        dx = jnp.zeros_like(xv)
        for k in range(SSM_CONV):
            dw_ref[k:k + 1, :] = jnp.sum(dacc * shifted[k], axis=0, keepdims=True)
            dx = dx + w_ref[k:k + 1, :] * _shift_rows(dacc, -(SSM_CONV - 1 - k), S)
        dx_ref[...] = dx

    slab = pl.BlockSpec((S, tc), lambda j: (0, j))
    return pl.pallas_call(
        body, name=name, grid=(C // tc,),
        in_specs=[pl.BlockSpec((S, tc), lambda j: (0, off + j)), pl.BlockSpec((SSM_CONV, tc), lambda j: (0, j)),
                  pl.BlockSpec((1, tc), lambda j: (0, j)), slab],
        out_specs=[slab, pl.BlockSpec((SSM_CONV, tc), lambda j: (0, j)), pl.BlockSpec((1, tc), lambda j: (0, j))],
        out_shape=[jax.ShapeDtypeStruct((S, C), F32), jax.ShapeDtypeStruct((SSM_CONV, C), F32),
                   jax.ShapeDtypeStruct((1, C), F32)],
        compiler_params=_params(("parallel",)),
    )(zx, conv_w, conv_b, dout)


def _ssd_chunk_terms(dtraw, bias, a_log, tl):
    dt = _softplus(dtraw + bias)
    a_neg = -jnp.exp(a_log)
    ac = _dot_exact_left(tl, dt * a_neg)
    ac_last = ac[ac.shape[0] - 1:, :]
    return dt, a_neg, ac, ac.T, jnp.exp(ac), jnp.exp(ac_last - ac), jnp.exp(ac_last)


def _ssd_specs(S, L, G, hpg, pd, inner):
    gw = hpg * pd
    n = SSM_STATE
    xb = inner // n

    def mk(cidx):
        return dict(
            x=pl.BlockSpec((L, gw), lambda g, c: (cidx(c), g)),
            b=pl.BlockSpec((L, n), lambda g, c: (cidx(c), xb + g)),
            c=pl.BlockSpec((L, n), lambda g, c: (cidx(c), xb + G + g)),
            z=pl.BlockSpec((L, gw), lambda g, c: (cidx(c), g)),
            dt=pl.BlockSpec((L, LANES), lambda g, c: (cidx(c), g)),
            gvec=pl.BlockSpec((1, 1, LANES), lambda g, c: (g, 0, 0)),
            chan=pl.BlockSpec((1, gw), lambda g, c: (0, g)),
            tri=pl.BlockSpec((L, L), lambda g, c: (0, 0)),
            hp=pl.BlockSpec((1, 1, gw, n), lambda g, c: (g, cidx(c), 0, 0)),
            bc=pl.BlockSpec((L, n), lambda g, c: (cidx(c), g)),
        )
    return mk


def ssd_fwd(xbc, zx, dtg, bias_g, alog_g, d_chan, ngain, L, G, *, name):
    S = xbc.shape[0]
    n = SSM_STATE
    inner = xbc.shape[1] - 2 * G * n
    gw = inner // G
    pd = SB_HEAD_DIM
    hpg = gw // pd
    nc = S // L
    sp = _ssd_specs(S, L, G, hpg, pd, inner)(lambda c: c)

    def body(x_ref, b_ref, c_ref, z_ref, dt_ref, bias_ref, alog_ref, d_ref, ng_ref, tl_ref,
             yn_ref, y_ref, hp_ref, state):
        @pl.when(pl.program_id(1) == 0)
        def _():
            state[...] = jnp.zeros_like(state)

        dt, _, ac, act, ea, dte, cd = _ssd_chunk_terms(dt_ref[...], bias_ref[0], alog_ref[0], tl_ref[...])
        xv = x_ref[...]
        bm = b_ref[...].astype(BF16)
        cm = c_ref[...].astype(BF16)
        cb = _dot(cm, bm, 1, 1)
        tril = lax.broadcasted_iota(jnp.int32, (L, L), 1) <= lax.broadcasted_iota(jnp.int32, (L, L), 0)
        hp_ref[0, 0] = state[...]
        for r in range(hpg):
            ps = slice(r * pd, (r + 1) * pd)
            xr = xv[:, ps]
            xdt = xr * dt[:, r:r + 1]
            lm = jnp.exp(jnp.where(tril, ac[:, r:r + 1] - act[r:r + 1, :], -jnp.inf))
            hprev = state[ps, :]
            y = _dot((cb * lm).astype(BF16), xdt.astype(BF16))
            y = y + _dot(cm, hprev.astype(BF16), 1, 1) * ea[:, r:r + 1]
            y_ref[:, ps] = y + xr * d_ref[:, ps]
            st = _dot((xdt * dte[:, r:r + 1]).astype(BF16), bm, 0, 0)
            state[ps, :] = hprev * cd[:, r:r + 1] + st
        yfull = y_ref[...]
        zg = z_ref[...]
        yg = yfull * (zg * _sigmoid(zg))
        yn_ref[...] = (yg * lax.rsqrt(jnp.mean(yg * yg, axis=1, keepdims=True) + EPS) * ng_ref[...]).astype(BF16)

    return pl.pallas_call(
        body, name=name, grid=(G, nc),
        in_specs=[sp["x"], sp["b"], sp["c"], sp["z"], sp["dt"], sp["gvec"], sp["gvec"], sp["chan"], sp["chan"], sp["tri"]],
        out_specs=[sp["x"], sp["x"], sp["hp"]],
        out_shape=[jax.ShapeDtypeStruct((S, inner), BF16), jax.ShapeDtypeStruct((S, inner), F32),
                   jax.ShapeDtypeStruct((G, nc, gw, n), F32)],
        scratch_shapes=[pltpu.VMEM((gw, n), F32)],
        compiler_params=_params(("arbitrary", "arbitrary")),
    )(xbc, xbc, xbc, zx, dtg, bias_g, alog_g, d_chan, ngain, _tri(L, "row_ge_col"))


def ssd_bwd(xbc, zx, dtg, bias_g, alog_g, d_chan, ngain, yfull, hp, dyn, L, G, *, name):
    S = xbc.shape[0]
    n = SSM_STATE
    inner = xbc.shape[1] - 2 * G * n
    gw = inner // G
    pd = SB_HEAD_DIM
    hpg = gw // pd
    nc = S // L
    sp = _ssd_specs(S, L, G, hpg, pd, inner)(lambda c: nc - 1 - c)

    def body(x_ref, b_ref, c_ref, z_ref, dt_ref, bias_ref, alog_ref, d_ref, ng_ref, tl_ref, tu_ref,
             yf_ref, hp_ref, dyn_ref,
             dz_ref, dx_ref, db_ref, dc_ref, ddt_ref, dbias_ref, dalog_ref, dd_ref, dng_ref, dstate):
        first = pl.program_id(1) == 0

        @pl.when(first)
        def _():
            dstate[...] = jnp.zeros_like(dstate)
            dbias_ref[...] = jnp.zeros_like(dbias_ref)
            dalog_ref[...] = jnp.zeros_like(dalog_ref)
            dd_ref[...] = jnp.zeros_like(dd_ref)
            dng_ref[...] = jnp.zeros_like(dng_ref)

        dtraw = dt_ref[...]
        dt, a_neg, ac, act, ea, dte, cd = _ssd_chunk_terms(dtraw, bias_ref[0], alog_ref[0], tl_ref[...])
        xv = x_ref[...]
        bm = b_ref[...].astype(BF16)
        cm = c_ref[...].astype(BF16)
        cb = _dot(cm, bm, 1, 1)
        tril = lax.broadcasted_iota(jnp.int32, (L, L), 1) <= lax.broadcasted_iota(jnp.int32, (L, L), 0)
        lane = lax.broadcasted_iota(jnp.int32, (L, LANES), 1)
        lane1 = lax.broadcasted_iota(jnp.int32, (1, LANES), 1)

        yfull = yf_ref[...]
        zg = z_ref[...]
        sg = _sigmoid(zg)
        gate = zg * sg
        yg = yfull * gate
        rr = lax.rsqrt(jnp.mean(yg * yg, axis=1, keepdims=True) + EPS)
        yhat = yg * rr
        dynv = dyn_ref[...]
        dng_ref[...] += jnp.sum(dynv * yhat, axis=0, keepdims=True)
        t = dynv * ng_ref[...]
        dyg = rr * (t - yhat * jnp.mean(yhat * t, axis=1, keepdims=True))
        dy = dyg * gate
        dz_ref[...] = dyg * yfull * (sg * (1.0 + zg * (1.0 - sg)))

        dcb = jnp.zeros((L, L), F32)
        dc_acc = jnp.zeros((L, n), F32)
        db_acc = jnp.zeros((L, n), F32)
        dac = jnp.zeros((L, LANES), F32)
        xdx = jnp.zeros((L, LANES), F32)
        tail = jnp.zeros((1, LANES), F32)
        dskip = jnp.zeros((1, LANES), F32)
        ones_l = jnp.ones((L, LANES), BF16)
        for r in range(hpg):
            ps = slice(r * pd, (r + 1) * pd)
            xr = xv[:, ps]
            dyr = dy[:, ps]
            dtr = dt[:, r:r + 1]
            dter = dte[:, r:r + 1]
            cdr = cd[:, r:r + 1]
            xdt = xr * dtr
            xdtb = xdt.astype(BF16)
            dyrb = dyr.astype(BF16)
            lm = jnp.exp(jnp.where(tril, ac[:, r:r + 1] - act[r:r + 1, :], -jnp.inf))
            m32 = cb * lm
            mb = m32.astype(BF16)
            hprev = hp_ref[0, 0, ps, :]
            hpb = hprev.astype(BF16)
            dhn = dstate[ps, :]
            dhnb = dhn.astype(BF16)
            ear = ea[:, r:r + 1]
            gy = (dyr * ear).astype(BF16)
            dc_acc = dc_acc + _dot(gy, hpb)
            dstate[ps, :] = _dot(gy, cm, 0, 0) + dhn * cdr
            bdh = _dot(bm, dhnb, 1, 1)
            db_acc = db_acc + _dot((xdt * dter).astype(BF16), dhnb)
            dm = _dot(dyrb, xdtb, 1, 1)
            dxdt = bdh * dter + _dot(mb, dyrb, 0, 0)
            dcb = dcb + dm * lm
            wmat = dm * m32
            whi = wmat.astype(BF16)
            wlo = (wmat - whi.astype(F32)).astype(BF16)
            col_w = _dot(whi, ones_l, 0, 0) + _dot(wlo, ones_l, 0, 0)
            t_end = xdt * bdh * dter
            e_r = jnp.sum(wmat, axis=1, keepdims=True) \
                + jnp.sum(dyr * _dot(cm, hpb, 1, 1) * ear - t_end, axis=1, keepdims=True)
            c_r = cdr * _sum_all(dhn * hprev) + _sum_all(t_end)
            dac = dac + jnp.where(lane == r, e_r - col_w, 0.0)
            xdx = xdx + jnp.where(lane == r, jnp.sum(dxdt * xr, axis=1, keepdims=True), 0.0)
            tail = tail + jnp.where(lane1 == r, c_r, 0.0)
            dskip = dskip + jnp.where(lane1 == r, _sum_all(dyr * xr), 0.0)
            dx_ref[:, ps] = dxdt * dtr + dyr * d_ref[:, ps]
        dcbb = dcb.astype(BF16)
        dc_ref[...] = dc_acc + _dot(dcbb, bm)
        db_ref[...] = db_acc + _dot(dcbb, cm, 0, 0)
        da = _dot_exact_left(tu_ref[...], dac) + tail
        real = lane < hpg
        ddt = jnp.where(real, (da * a_neg + xdx) * _sigmoid(dtraw + bias_ref[0]), 0.0)
        ddt_ref[...] = ddt
        dd_ref[0] += dskip
        dbias_ref[0] += jnp.sum(ddt, axis=0, keepdims=True)
        dalog_ref[0] += jnp.where(lane1 < hpg, jnp.sum(da * dt, axis=0, keepdims=True) * a_neg, 0.0)

    return pl.pallas_call(
        body, name=name, grid=(G, nc),
        in_specs=[sp["x"], sp["b"], sp["c"], sp["z"], sp["dt"], sp["gvec"], sp["gvec"], sp["chan"], sp["chan"],
                  sp["tri"], sp["tri"], sp["x"], sp["hp"], sp["x"]],
        out_specs=[sp["x"], sp["x"], sp["bc"], sp["bc"], sp["dt"], sp["gvec"], sp["gvec"], sp["gvec"], sp["chan"]],
        out_shape=[jax.ShapeDtypeStruct((S, inner), F32), jax.ShapeDtypeStruct((S, inner), F32),
                   jax.ShapeDtypeStruct((S, G * n), F32), jax.ShapeDtypeStruct((S, G * n), F32),
                   jax.ShapeDtypeStruct((S, G * LANES), F32), jax.ShapeDtypeStruct((G, 1, LANES), F32),
                   jax.ShapeDtypeStruct((G, 1, LANES), F32), jax.ShapeDtypeStruct((G, 1, LANES), F32),
                   jax.ShapeDtypeStruct((1, inner), F32)],
        scratch_shapes=[pltpu.VMEM((gw, n), F32)],
        compiler_params=_params(("arbitrary", "arbitrary")),
    )(xbc, xbc, xbc, zx, dtg, bias_g, alog_g, d_chan, ngain, _tri(L, "row_ge_col"), _tri(L, "row_le_col"),
      yfull, hp, dyn)


def _spread_dt(w_dt_t, G, hpg):
    K = w_dt_t.shape[1]
    w = w_dt_t.reshape(G, hpg, K)
    return jnp.pad(w, ((0, 0), (0, LANES - hpg), (0, 0))).reshape(G * LANES, K)


def _group_vec(v, G, hpg):
    return jnp.pad(v.reshape(G, 1, hpg), ((0, 0), (0, 0), (0, LANES - hpg)))


def local_step(x, target, W, late=None):
    S, D = x.shape
    depth = W["mix_norm"].shape[0]
    gm_groups, gm_chunk = W["gm_w_s"].shape[1], W["gm_w_s"].shape[2]
    heads = W["ssm_dt_bias"].shape[1]
    inner = heads * SB_HEAD_DIM
    L = gm_chunk
    received = None

    saved = []
    for i in range(depth):
        kind, j = i % 3, i // 3
        s = dict(x=x)
        h = rms_fwd(x, W["mix_norm"][i:i + 1], name=f"rms_mix_fwd")
        s["h"] = h
        if kind == 0:
            qkv = matmul(h, W["sb_w_qkv"][j], tb=True, name="mm_qkv")
            if late is not None and i == 0:
                o, gathered = sb_attn_fwd(qkv, W["sb_q_gain"][j:j + 1], W["sb_k_gain"][j:j + 1], name="sb_fwd_gather",
                                          gather=late.shards)
                late.fill(W, gathered)
            else:
                o = sb_attn_fwd(qkv, W["sb_q_gain"][j:j + 1], W["sb_k_gain"][j:j + 1], name="sb_fwd")
            x1 = matmul(o, W["sb_w_o"][j], residual=x, name="mm_sb_out")
            s.update(qkv=qkv, o=o)
        elif kind == 1:
            wc = jnp.where(jnp.tril(jnp.ones((gm_chunk, gm_chunk), bool)), W["gm_w_s"][j], 0.0).astype(BF16)
            bsf = jnp.broadcast_to(W["gm_b_s"][j][:, :, None], (gm_groups, gm_chunk, W["gm_v_gain"].shape[1] // gm_groups)).astype(F32)
            zzpre = matmul(h, W["gm_w_in"][j], tb=True, name="mm_gm_in")
            p = gmlp_fwd(zzpre, W["gm_b_in"][j:j + 1], W["gm_v_gain"][j:j + 1], wc, bsf, name="gm_fwd")
            x1 = matmul(p, W["gm_w_out"][j], residual=x, name="mm_gm_out")
            s.update(zzpre=zzpre, p=p, wc=wc, bsf=bsf)
        else:
            conv_dim = W["ssm_conv_w"].shape[2]
            G = (conv_dim - inner) // (2 * SSM_STATE)
            hpg = heads // G
            w_in = W["ssm_w_in"][j]
            w_zx = w_in[:inner + conv_dim]
            w_dtg = _spread_dt(w_in[inner + conv_dim:], G, hpg)
            bias_g = _group_vec(W["ssm_dt_bias"][j], G, hpg)
            alog_g = _group_vec(W["ssm_a_log"][j], G, hpg)
            d_chan = jnp.repeat(W["ssm_d"][j], SB_HEAD_DIM)[None, :]
            ngain = W["ssm_norm_gain"][j:j + 1]
            zx = matmul(h, w_zx, tb=True, name="mm_ssm_zx")
            dtg = matmul(h, w_dtg, tb=True, name="mm_ssm_dt")
            xbc = conv_fwd(zx, W["ssm_conv_w"][j], W["ssm_conv_b"][j:j + 1], inner, name="conv_fwd")
            yn, yfull, hp = ssd_fwd(xbc, zx, dtg, bias_g, alog_g, d_chan, ngain, L, G, name="ssd_fwd")
            x1 = matmul(yn, W["ssm_w_out"][j], residual=x, name="mm_ssm_out")
            s.update(w_zx=w_zx, w_dtg=w_dtg, bias_g=bias_g, alog_g=alog_g, d_chan=d_chan, ngain=ngain,
                     zx=zx, dtg=dtg, xbc=xbc, yn=yn, yfull=yfull, hp=hp)
        h2 = rms_fwd(x1, W["ffn_norm"][i:i + 1], name="rms_ffn_fwd")
        gu, a = ffn_up_fwd(h2, W["ffn_w_gu"][i], name="ffn_up_fwd")
        x2 = matmul(a, W["ffn_w_down"][i], residual=x1, name="mm_ffn_down")
        s.update(x1=x1, h2=h2, gu=gu, a=a)
        saved.append(s)
        x = x2

    dx, loss = loss_head(x, target, name="loss_head")

    gw = {k: {} for k in WEIGHTS}
    for i in reversed(range(depth)):
        kind, j = i % 3, i // 3
        s = saved[i]
        gw["ffn_w_down"][i] = matmul(s["a"], dx, ta=True, out_dtype=BF16, name="mm_ffn_dwdown")
        dgu = ffn_up_bwd(dx, W["ffn_w_down"][i], s["gu"], name="ffn_up_bwd")
        dh2 = matmul(dgu, W["ffn_w_gu"][i], a_split=2, name="mm_ffn_dh")
        gw["ffn_w_gu"][i] = matmul(dgu, s["h2"], ta=True, a_split=2, out_dtype=BF16, name="mm_ffn_dwgu")
        dx1, dgn = rms_bwd(s["x1"], W["ffn_norm"][i:i + 1], dh2, dx, name="rms_ffn_bwd")
        gw["ffn_norm"][i] = dgn[0]
        if kind == 0:
            do = matmul(dx1, W["sb_w_o"][j], tb=True, name="mm_sb_do")
            gw["sb_w_o"][j] = matmul(s["o"], dx1, ta=True, out_dtype=BF16, name="mm_sb_dwo")
            if late is not None and i == 0:
                dqkv, dqg, dkg, received = sb_attn_bwd(
                    s["qkv"], s["o"], do, W["sb_q_gain"][j:j + 1], W["sb_k_gain"][j:j + 1], name="sb_bwd_scatter",
                    scatter=late.contributions(gw))
            else:
                dqkv, dqg, dkg = sb_attn_bwd(s["qkv"], s["o"], do, W["sb_q_gain"][j:j + 1], W["sb_k_gain"][j:j + 1],
                                             name="sb_bwd")
            gw["sb_q_gain"][j] = dqg[0]
            gw["sb_k_gain"][j] = dkg[0]
            dh = matmul(dqkv, W["sb_w_qkv"][j], a_split=3, name="mm_sb_dh")
            gw["sb_w_qkv"][j] = matmul(dqkv, s["h"], ta=True, a_split=3, out_dtype=BF16, name="mm_sb_dwqkv")
        elif kind == 1:
            dp = matmul(dx1, W["gm_w_out"][j], tb=True, name="mm_gm_dp")
            gw["gm_w_out"][j] = matmul(s["p"], dx1, ta=True, out_dtype=BF16, name="mm_gm_dwout")
            dzz, db_in, dvg, dws, dbs = gmlp_bwd(s["zzpre"], W["gm_b_in"][j:j + 1], W["gm_v_gain"][j:j + 1],
                                                s["wc"], s["bsf"], dp, name="gm_bwd")
            gw["gm_b_in"][j] = db_in[0]
            gw["gm_v_gain"][j] = dvg[0]
            gw["gm_w_s"][j] = dws
            gw["gm_b_s"][j] = dbs[:, :gm_groups].T
            dh = matmul(dzz, W["gm_w_in"][j], name="mm_gm_dh")
            gw["gm_w_in"][j] = matmul(dzz, s["h"], ta=True, out_dtype=BF16, name="mm_gm_dwin")
        else:
            conv_dim = W["ssm_conv_w"].shape[2]
            G = (conv_dim - inner) // (2 * SSM_STATE)
            hpg = heads // G
            dyn = matmul(dx1, W["ssm_w_out"][j], tb=True, name="mm_ssm_dyn")
            gw["ssm_w_out"][j] = matmul(s["yn"], dx1, ta=True, out_dtype=BF16, name="mm_ssm_dwout")
            dz, dxs, dbm, dcm, ddt, dbias, dalog, dd, dng = ssd_bwd(
                s["xbc"], s["zx"], s["dtg"], s["bias_g"], s["alog_g"], s["d_chan"], s["ngain"], s["yfull"], s["hp"],
                dyn, L, G, name="ssd_bwd")
            dxbc = jnp.concatenate([dxs, dbm, dcm], axis=1)
            dpre, dcw, dcb = conv_bwd(s["zx"], W["ssm_conv_w"][j], W["ssm_conv_b"][j:j + 1], inner, dxbc,
                                      name="conv_bwd")
            dzx = jnp.concatenate([dz, dpre], axis=1)
            dh = matmul(ddt, s["w_dtg"], name="mm_ssm_dh_dt")
            dh = matmul(dzx, s["w_zx"], residual=dh, name="mm_ssm_dh")
            dw_zx = matmul(dzx, s["h"], ta=True, out_dtype=BF16, name="mm_ssm_dwzx")
            dw_dtg = matmul(ddt, s["h"], ta=True, out_dtype=BF16, name="mm_ssm_dwdt")
            dw_dt = dw_dtg.reshape(G, LANES, D)[:, :hpg, :].reshape(heads, D)
            gw["ssm_w_in"][j] = jnp.concatenate([dw_zx, dw_dt], axis=0)
            gw["ssm_conv_w"][j] = dcw
            gw["ssm_conv_b"][j] = dcb[0]
            gw["ssm_dt_bias"][j] = dbias[:, 0, :hpg].reshape(heads)
            gw["ssm_a_log"][j] = dalog[:, 0, :hpg].reshape(heads)
            gw["ssm_d"][j] = dd[:, 0, :hpg].reshape(heads)
            gw["ssm_norm_gain"][j] = dng[0]
        dx, dgn = rms_bwd(s["x"], W["mix_norm"][i:i + 1], dh, dx1, name="rms_mix_bwd")
        gw["mix_norm"][i] = dgn[0]

    return loss, dx, gw, received


MESH = pl.DeviceIdType.MESH
HBM_SPEC = pl.BlockSpec(memory_space=pltpu.HBM)
VMEM_SPEC = pl.BlockSpec(memory_space=pltpu.VMEM)


def _my_position():
    return lax.axis_index("x"), lax.axis_index("y"), lax.axis_index("c")


def _flip(v, bit):
    return 1 - v if bit else v


def all_gather(shards, *, name):
    n = len(shards)

    def body(*refs):
        for phase in ("start", "forward", "finish"):
            _ag_phase(refs[:n], refs[n:2 * n], refs[2 * n:], phase)

    return pl.pallas_call(
        body, name=name, out_shape=[jax.ShapeDtypeStruct((N_DEV,) + s.shape, s.dtype) for s in shards],
        in_specs=[HBM_SPEC] * n, out_specs=[HBM_SPEC] * n, scratch_shapes=_rs_semaphores(n),
    )(*shards)


def _ag_phase(x_refs, out_refs, sems, phase):
    send_sems, recv_sems, local_sems = sems
    x, y, c = _my_position()
    me, sibling = (x, y, c), (x, y, 1 - c)
    chips = [(1 - x, y), (x, 1 - y), (1 - x, 1 - y)]
    for p, (x_ref, out_ref) in enumerate(zip(x_refs, out_refs)):
        def slot(px, py, pc):
            return out_ref.at[4 * px + 2 * py + pc]

        def copy(k, block, to, src=None):
            return pltpu.make_async_remote_copy(
                src_ref=slot(*block) if src is None else src, dst_ref=slot(*block),
                send_sem=send_sems.at[7 * p + k], recv_sem=recv_sems.at[7 * p + k], device_id=to, device_id_type=MESH)

        mine = pltpu.make_async_copy(x_ref, slot(*me), local_sems.at[p])
        first = [copy(0, me, sibling, src=x_ref)]
        first += [copy(1 + j, me, (*chip, c), src=x_ref) for j, chip in enumerate(chips)]
        passed = [copy(4 + j, (*chip, c), sibling) for j, chip in enumerate(chips)]
        if phase == "start":
            mine.start()
            for cp in first:
                cp.start()
        elif phase == "forward":
            for j, chip in enumerate(chips):
                copy(1 + j, (*chip, c), me).wait_recv()
                passed[j].start()
        else:
            copy(0, sibling, me).wait_recv()
            for j, chip in enumerate(chips):
                copy(4 + j, (*chip, 1 - c), me).wait_recv()
            for cp in first + passed:
                cp.wait_send()
            mine.wait()


def _rs_semaphores(n):
    return [pltpu.SemaphoreType.DMA((7 * n,)), pltpu.SemaphoreType.DMA((7 * n,)), pltpu.SemaphoreType.DMA((n,))]


def _rs_phase(g_refs, out_refs, sems, phase):
    send_sems, recv_sems, local_sems = sems
    x, y, c = _my_position()
    me = 4 * x + 2 * y + c
    copies = []
    for p, (g_ref, out_ref) in enumerate(zip(g_refs, out_refs)):
        copies.append(pltpu.make_async_copy(g_ref.at[me], out_ref.at[me], local_sems.at[p]))
        for k in range(1, N_DEV):
            px, py, pc = _flip(x, k & 4), _flip(y, k & 2), _flip(c, k & 1)
            copies.append(pltpu.make_async_remote_copy(
                src_ref=g_ref.at[4 * px + 2 * py + pc], dst_ref=out_ref.at[me],
                send_sem=send_sems.at[7 * p + k - 1], recv_sem=recv_sems.at[7 * p + k - 1],
                device_id=(px, py, pc), device_id_type=MESH))
    for cp in copies:
        if phase == "start":
            cp.start()
        else:
            cp.wait()


def exchange_for_reduce_scatter(gs, *, name):
    n = len(gs)

    def body(*refs):
        for phase in ("start", "finish"):
            _rs_phase(refs[:n], refs[n:2 * n], refs[2 * n:], phase)

    return pl.pallas_call(
        body, name=name, out_shape=[jax.ShapeDtypeStruct(g.shape, g.dtype) for g in gs],
        in_specs=[HBM_SPEC] * n, out_specs=[HBM_SPEC] * n, scratch_shapes=_rs_semaphores(n),
    )(*gs)


def sum_slots(recv, *, name):
    n, R, C = recv.shape
    tr = _pick(R, (512, 256, 128))

    def body(r_ref, o_ref):
        acc = r_ref[0].astype(F32)
        for s in range(1, n):
            acc = acc + r_ref[s].astype(F32)
        o_ref[...] = acc

    return pl.pallas_call(
        body, name=name, grid=(R // tr,), in_specs=[pl.BlockSpec((n, tr, C), lambda i: (0, i, 0))],
        out_specs=pl.BlockSpec((tr, C), lambda i: (i, 0)), out_shape=jax.ShapeDtypeStruct((R, C), F32),
        compiler_params=_params(("parallel",)),
    )(recv)


def all_reduce_small(vs, *, name):
    n = len(vs)

    def body(*refs):
        v_refs, o_refs, bufs = refs[:n], refs[n:2 * n], refs[2 * n:3 * n]
        send_sems, recv_sems = refs[3 * n:]
        x, y, c = _my_position()
        me = 4 * x + 2 * y + c
        copies = []
        for p, (v_ref, buf) in enumerate(zip(v_refs, bufs)):
            buf[me] = v_ref[...]
            for k in range(1, N_DEV):
                px, py, pc = _flip(x, k & 4), _flip(y, k & 2), _flip(c, k & 1)
                copies.append(pltpu.make_async_remote_copy(
                    src_ref=v_ref, dst_ref=buf.at[me], send_sem=send_sems.at[7 * p + k - 1],
                    recv_sem=recv_sems.at[7 * p + k - 1], device_id=(px, py, pc), device_id_type=MESH))
        for cp in copies:
            cp.start()
        for cp in copies:
            cp.wait()
        for o_ref, buf in zip(o_refs, bufs):
            acc = buf[0]
            for s in range(1, N_DEV):
                acc = acc + buf[s]
            o_ref[...] = acc

    return pl.pallas_call(
        body, name=name, out_shape=[jax.ShapeDtypeStruct(v.shape, F32) for v in vs],
        in_specs=[VMEM_SPEC] * n, out_specs=[VMEM_SPEC] * n,
        scratch_shapes=[pltpu.VMEM((N_DEV,) + v.shape, F32) for v in vs]
        + [pltpu.SemaphoreType.DMA((7 * n,)), pltpu.SemaphoreType.DMA((7 * n,))],
        compiler_params=pltpu.CompilerParams(vmem_limit_bytes=VMEM_LIMIT_BYTES),
    )(*vs)


def _pad_rows(a, mult):
    pad = (-a.shape[0]) % mult
    return jnp.pad(a, ((0, pad), (0, 0))) if pad else a


def _pack_small(arrays):
    flat = []
    for a in arrays:
        f = a.reshape(-1).astype(F32)
        flat.append(jnp.pad(f, (0, (-f.shape[0]) % LANES)))
    return _pad_rows(jnp.concatenate(flat).reshape(-1, LANES), 8)


def _unpack_small(packed, shapes):
    flat = packed.reshape(-1)
    out, r = [], 0
    for shp in shapes:
        n = math.prod(shp)
        out.append(flat[r:r + n].reshape(shp))
        r += n + (-n) % LANES
    return out


ARG_NAMES = ("x",) + WEIGHTS + ("loss_target",) + tuple("m_" + w for w in WEIGHTS) + tuple("v_" + w for w in WEIGHTS)


def kernel(x, mix_norm, ffn_norm, sb_w_qkv, sb_q_gain, sb_k_gain, sb_w_o, gm_w_in, gm_b_in, gm_v_gain, gm_w_s, gm_b_s, gm_w_out, ssm_w_in, ssm_conv_w, ssm_conv_b, ssm_dt_bias, ssm_a_log, ssm_d, ssm_norm_gain, ssm_w_out, ffn_w_gu, ffn_w_down, loss_target, m_mix_norm, m_ffn_norm, m_sb_w_qkv, m_sb_q_gain, m_sb_k_gain, m_sb_w_o, m_gm_w_in, m_gm_b_in, m_gm_v_gain, m_gm_w_s, m_gm_b_s, m_gm_w_out, m_ssm_w_in, m_ssm_conv_w, m_ssm_conv_b, m_ssm_dt_bias, m_ssm_a_log, m_ssm_d, m_ssm_norm_gain, m_ssm_w_out, m_ffn_w_gu, m_ffn_w_down, v_mix_norm, v_ffn_norm, v_sb_w_qkv, v_sb_q_gain, v_sb_k_gain, v_sb_w_o, v_gm_w_in, v_gm_b_in, v_gm_v_gain, v_gm_w_s, v_gm_b_s, v_gm_w_out, v_ssm_w_in, v_ssm_conv_w, v_ssm_conv_b, v_ssm_dt_bias, v_ssm_a_log, v_ssm_d, v_ssm_norm_gain, v_ssm_w_out, v_ffn_w_gu, v_ffn_w_down):
    given = dict(zip(ARG_NAMES, (x, mix_norm, ffn_norm, sb_w_qkv, sb_q_gain, sb_k_gain, sb_w_o, gm_w_in, gm_b_in, gm_v_gain, gm_w_s, gm_b_s, gm_w_out, ssm_w_in, ssm_conv_w, ssm_conv_b, ssm_dt_bias, ssm_a_log, ssm_d, ssm_norm_gain, ssm_w_out, ffn_w_gu, ffn_w_down, loss_target, m_mix_norm, m_ffn_norm, m_sb_w_qkv, m_sb_q_gain, m_sb_k_gain, m_sb_w_o, m_gm_w_in, m_gm_b_in, m_gm_v_gain, m_gm_w_s, m_gm_b_s, m_gm_w_out, m_ssm_w_in, m_ssm_conv_w, m_ssm_conv_b, m_ssm_dt_bias, m_ssm_a_log, m_ssm_d, m_ssm_norm_gain, m_ssm_w_out, m_ffn_w_gu, m_ffn_w_down, v_mix_norm, v_ffn_norm, v_sb_w_qkv, v_sb_q_gain, v_sb_k_gain, v_sb_w_o, v_gm_w_in, v_gm_b_in, v_gm_v_gain, v_gm_w_s, v_gm_b_s, v_gm_w_out, v_ssm_w_in, v_ssm_conv_w, v_ssm_conv_b, v_ssm_dt_bias, v_ssm_a_log, v_ssm_d, v_ssm_norm_gain, v_ssm_w_out, v_ffn_w_gu, v_ffn_w_down)))
    mx, my, mc = _my_position()
    me = 4 * mx + 2 * my + mc

    pieces = [(k, l) for k in BIG for l in range(given[k].shape[0])]
    early = [("sb_w_qkv", 0)]
    late_pieces = [p for p in pieces if p not in early]
    last = [("sb_w_qkv", 0)]
    main = [p for p in pieces if p not in last]

    def shards_of(ps):
        return [(given[k][l].T if k in COL_SHARDED else given[k][l]).astype(BF16) for k, l in ps]

    def piece_to_full(g, k):
        rows, cols = given[k].shape[1:]
        return g.reshape(N_DEV * cols, rows) if k in COL_SHARDED else g.reshape(N_DEV * rows, cols)

    def full_to_piece(full, k):
        return full.reshape(N_DEV, -1, PACK_COLS)

    def summed_to_shard(g, k):
        rows, cols = given[k].shape[1:]
        return g.reshape(cols, rows).T if k in COL_SHARDED else g.reshape(rows, cols)

    def contributions(gw, ps):
        return [full_to_piece(gw[k][l], k) for k, l in ps]

    sharded_small = [lax.bitcast_convert_type(given[k], BF16) for k in SMALL_SHARDED]
    tail = jnp.concatenate([a.reshape(-1) for a in sharded_small])
    tail = jnp.pad(tail, (0, (-tail.size) % PACK_COLS)).reshape(-1, PACK_COLS)

    W = {k: given[k] for k in SMALL if k not in SMALL_SHARDED}
    W.update({k: [None] * given[k].shape[0] for k in BIG})
    for (k, l), g in zip(early, all_gather(shards_of(early), name="all_gather_early")):
        W[k][l] = piece_to_full(g, k)

    class Late:
        shards = shards_of(late_pieces) + [tail]

        @staticmethod
        def fill(weights, gathered):
            for (k, l), g in zip(late_pieces, gathered):
                weights[k][l] = piece_to_full(g, k)
            tail_g = gathered[-1].reshape(N_DEV, -1)
            off = 0
            for k, a in zip(SMALL_SHARDED, sharded_small):
                g = lax.bitcast_convert_type(tail_g[:, off:off + a.size].reshape((N_DEV,) + a.shape), F32)
                weights[k] = jnp.moveaxis(g, 0, -2).reshape(g.shape[1:-1] + (N_DEV * g.shape[-1],))
                off += a.size

        @staticmethod
        def contributions(gw):
            return contributions(gw, main)

    loss, gx, gw, received_main = local_step(given["x"][0], given["loss_target"][0], W, late=Late)
    received_last = exchange_for_reduce_scatter(contributions(gw, last), name="reduce_scatter_last")

    grads_small = {k: jnp.stack([gw[k][l] for l in sorted(gw[k])], axis=0) for k in SMALL}
    packed_names = tuple(k for k in SMALL if k != "gm_w_s")
    small_shapes = [grads_small[k].shape for k in packed_names] + [(1, 1)]
    red_ws, red_rest = all_reduce_small(
        [grads_small["gm_w_s"].reshape(-1, LANES), _pack_small([grads_small[k] for k in packed_names] + [loss])],
        name="all_reduce_small")
    small_full = dict(zip(packed_names + ("loss",), _unpack_small(red_rest, small_shapes)))
    small_full["gm_w_s"] = red_ws.reshape(grads_small["gm_w_s"].shape)

    g_piece = {}
    for grp, received in ((main, received_main), (last, received_last)):
        for p, r in zip(grp, received):
            g_piece[p] = summed_to_shard(sum_slots(r, name="reduce_scatter_sum"), p[0])
    out_g, out_d, out_m, out_v = {}, {}, {}, {}
    for k in BIG:
        g = jnp.stack([g_piece[(k, l)] for l in range(given[k].shape[0])], axis=0)
        out_g[k] = g
        out_d[k], out_m[k], out_v[k] = adamw(given[k], g, given["m_" + k], given["v_" + k], name="adamw_" + k)

    gsmall = {}
    for k in SMALL:
        g = small_full[k]
        if k in SMALL_SHARDED:
            n = given[k].shape[-1]
            g = lax.dynamic_slice_in_dim(g, me * n, n, axis=g.ndim - 1)
        gsmall[k] = g
    local_shapes = [given[k].shape for k in packed_names]
    dsm, nmsm, nvsm = adamw(*[_pack_small([src[k] for k in packed_names]) for src in (
        given, gsmall, {k: given["m_" + k] for k in packed_names}, {k: given["v_" + k] for k in packed_names})],
        name="adamw_small")
    out_g.update(gsmall)
    for dst, src in ((out_d, dsm), (out_m, nmsm), (out_v, nvsm)):
        dst.update(zip(packed_names, _unpack_small(src, local_shapes)))
    ws_shape = given["gm_w_s"].shape
    out_d["gm_w_s"], out_m["gm_w_s"], out_v["gm_w_s"] = (a.reshape(ws_shape) for a in adamw(
        *[a.reshape((-1,) + ws_shape[-2:]) for a in (given["gm_w_s"], gsmall["gm_w_s"], given["m_gm_w_s"], given["v_gm_w_s"])],
        name="adamw_gm_w_s"))

    return (small_full["loss"].reshape(()), gx[None],
            *[out_g[k] for k in WEIGHTS], *[out_d[k] for k in WEIGHTS],
            *[out_m[k] for k in WEIGHTS], *[out_v[k] for k in WEIGHTS])
```

```python
import math

import jax
import jax.numpy as jnp
from jax import lax
from jax.experimental import pallas as pl
from jax.experimental.pallas import tpu as pltpu

F32 = jnp.float32
BF16 = jnp.bfloat16
EPS = 1e-6
N_DEV = 8
SB_HEAD_DIM = 64
SB_TILE = 256
SB_FWD_QUERY_BLOCKS = 4
SB_BWD_QUERY_BLOCKS = 2
SSM_STATE = 128
SSM_CONV = 4
ADAM_LR = 0.001
ADAM_B1 = 0.9
ADAM_B2 = 0.999
ADAM_EPS = 1e-08
ADAM_WD = 0.01
ADAM_STEP = 10
VMEM_LIMIT_BYTES = 56 * 1024 * 1024
MATMUL_VMEM_BUDGET = 40 * 1024 * 1024
LANES = 128
PACK_COLS = 1024

BIG = ("sb_w_qkv", "sb_w_o", "gm_w_in", "gm_w_out", "ssm_w_in", "ssm_w_out", "ffn_w_gu", "ffn_w_down")
COL_SHARDED = ("sb_w_qkv", "gm_w_in", "ssm_w_in", "ffn_w_gu")
SMALL = ("mix_norm", "ffn_norm", "sb_q_gain", "sb_k_gain", "gm_b_in", "gm_v_gain", "gm_w_s", "gm_b_s",
         "ssm_conv_w", "ssm_conv_b", "ssm_dt_bias", "ssm_a_log", "ssm_d", "ssm_norm_gain")
SMALL_SHARDED = ("ssm_conv_w", "ssm_conv_b", "ssm_norm_gain")
WEIGHTS = ("mix_norm", "ffn_norm", "sb_w_qkv", "sb_q_gain", "sb_k_gain", "sb_w_o", "gm_w_in", "gm_b_in",
           "gm_v_gain", "gm_w_s", "gm_b_s", "gm_w_out", "ssm_w_in", "ssm_conv_w", "ssm_conv_b", "ssm_dt_bias",
           "ssm_a_log", "ssm_d", "ssm_norm_gain", "ssm_w_out", "ffn_w_gu", "ffn_w_down")


def _params(semantics=None):
    return pltpu.CompilerParams(dimension_semantics=semantics, vmem_limit_bytes=VMEM_LIMIT_BYTES)


def _pick(n, prefs):
    for t in prefs:
        if t <= n and n % t == 0:
            return t
    return n


def _dot(a, b, ca=1, cb=0):
    return lax.dot_general(a, b, (((ca,), (cb,)), ((), ())), preferred_element_type=F32)


def _split3(v):
    h1 = v.astype(BF16)
    r1 = v - h1.astype(F32)
    h2 = r1.astype(BF16)
    h3 = (r1 - h2.astype(F32)).astype(BF16)
    return h1, h2, h3


def _dot_exact_left(mat01, v):
    h1, h2, h3 = _split3(v)
    return _dot(mat01, h1) + _dot(mat01, h2) + _dot(mat01, h3)


def _sum_all(v):
    return jnp.sum(jnp.sum(v, axis=0, keepdims=True), axis=1, keepdims=True)


def _sigmoid(v):
    return 1.0 / (1.0 + jnp.exp(-v))


def _softplus(v):
    return jnp.maximum(v, 0.0) + jnp.log(1.0 + jnp.exp(-jnp.abs(v)))


def _erf(v):
    a = jnp.abs(v)
    t = 1.0 / (1.0 + 0.3275911 * a)
    poly = t * (0.254829592 + t * (-0.284496736 + t * (1.421413741 + t * (-1.453152027 + t * 1.061405429))))
    e = 1.0 - poly * jnp.exp(-a * a)
    return jnp.where(v < 0, -e, e)


def _gelu_and_grad(v):
    cdf = 0.5 * (1.0 + _erf(v * (1.0 / math.sqrt(2.0))))
    pdf = jnp.exp(-0.5 * v * v) * (1.0 / math.sqrt(2.0 * math.pi))
    return v * cdf, cdf + v * pdf


def matmul(a, b, *, ta=False, tb=False, out_dtype=F32, residual=None, a_split=1, b_split=1, name):
    if a_split > 1 and ta:
        assert a.shape[0] == a_split
        K, M = a.shape[1], a_split * a.shape[2]
    elif a_split > 1:
        assert a.shape[0] == a_split
        M, K = a.shape[1], a_split * a.shape[2]
    elif ta:
        K, M = a.shape
    else:
        M, K = a.shape
    if b_split > 1:
        assert not tb and b.shape[0] == b_split
        Kb, N = b.shape[1], b_split * b.shape[2]
    elif tb:
        N, Kb = b.shape
    else:
        Kb, N = b.shape
    assert K == Kb, (a.shape, b.shape, ta, tb)
    has_res = residual is not None
    tm = _pick(M // a_split if ta else M, (1024, 1408, 768, 512, 256, 128))
    tn = _pick(N // b_split, (1024, 1408, 1536, 768, 512, 256, 128))

    def vmem_bytes(tk):
        tiles = tm * tk * a.dtype.itemsize + tk * tn * b.dtype.itemsize
        outs = tm * tn * jnp.dtype(out_dtype).itemsize + (tm * tn * 4 if has_res else 0)
        return 2 * tiles + 2 * outs + (tm * tn * 4 if tk < K else 0)

    kp = K if ta else K // a_split
    tk = next((t for t in (K, 2048, 1408, 1024, 512, 256) if t <= kp and kp % t == 0 and vmem_bytes(t) <= MATMUL_VMEM_BUDGET),
              _pick(kp, (128,)))
    nk = K // tk
    if a_split > 1 and ta:
        nib = M // a_split // tm
        a_spec = pl.BlockSpec((None, tk, tm), lambda i, j, k: (i // nib, k, i % nib))
    elif a_split > 1:
        nkb = kp // tk
        a_spec = pl.BlockSpec((None, tm, tk), lambda i, j, k: (k // nkb, i, k % nkb))
    else:
        a_spec = pl.BlockSpec((tk, tm), lambda i, j, k: (k, i)) if ta else pl.BlockSpec((tm, tk), lambda i, j, k: (i, k))
    if b_split > 1:
        njb = N // b_split // tn
        b_spec = pl.BlockSpec((None, tk, tn), lambda i, j, k: (j // njb, k, j % njb))
    else:
        b_spec = pl.BlockSpec((tn, tk), lambda i, j, k: (j, k)) if tb else pl.BlockSpec((tk, tn), lambda i, j, k: (k, j))
    o_spec = pl.BlockSpec((tm, tn), lambda i, j, k: (i, j))
    ca, cb = (0 if ta else 1), (1 if tb else 0)

    def body(*refs):
        a_ref, b_ref = refs[:2]
        r_ref = refs[2] if has_res else None
        o_ref = refs[3] if has_res else refs[2]

        def finish(r):
            if has_res:
                r = r + r_ref[...]
            o_ref[...] = r.astype(out_dtype)

        def part():
            return _dot(a_ref[...].astype(BF16), b_ref[...].astype(BF16), ca, cb)

        if nk == 1:
            finish(part())
            return
        acc = refs[-1]
        k = pl.program_id(2)

        @pl.when(k == 0)
        def _():
            acc[...] = part()

        @pl.when(jnp.logical_and(k > 0, k < nk - 1))
        def _():
            acc[...] += part()

        @pl.when(k == nk - 1)
        def _():
            finish(acc[...] + part())

    in_specs = [a_spec, b_spec] + ([o_spec] if has_res else [])
    args = (a, b) + ((residual,) if has_res else ())
    return pl.pallas_call(
        body, name=name, grid=(M // tm, N // tn, nk), in_specs=in_specs, out_specs=o_spec,
        out_shape=jax.ShapeDtypeStruct((M, N), out_dtype),
        scratch_shapes=[pltpu.VMEM((tm, tn), F32)] if nk > 1 else [],
        compiler_params=_params(("parallel", "parallel", "arbitrary")),
    )(*args)


def rms_fwd(x, gain, *, name):
    S, D = x.shape
    tr = _pick(S, (512, 256, 128))

    def body(x_ref, g_ref, o_ref):
        xv = x_ref[...]
        r = lax.rsqrt(jnp.mean(xv * xv, axis=1, keepdims=True) + EPS)
        o_ref[...] = (xv * r * g_ref[...]).astype(BF16)

    return pl.pallas_call(
        body, name=name, grid=(S // tr,),
        in_specs=[pl.BlockSpec((tr, D), lambda i: (i, 0)), pl.BlockSpec((1, D), lambda i: (0, 0))],
        out_specs=pl.BlockSpec((tr, D), lambda i: (i, 0)), out_shape=jax.ShapeDtypeStruct((S, D), BF16),
        compiler_params=_params(("parallel",)),
    )(x, gain)


def rms_bwd(x, gain, dh, dres, *, name):
    S, D = x.shape
    tr = _pick(S, (512, 256, 128))

    def body(x_ref, g_ref, dh_ref, dr_ref, dx_ref, dg_ref):
        @pl.when(pl.program_id(0) == 0)
        def _():
            dg_ref[...] = jnp.zeros_like(dg_ref)

        xv = x_ref[...]
        dhv = dh_ref[...]
        r = lax.rsqrt(jnp.mean(xv * xv, axis=1, keepdims=True) + EPS)
        xhat = xv * r
        t = dhv * g_ref[...]
        dx_ref[...] = dr_ref[...] + r * (t - xhat * jnp.mean(xhat * t, axis=1, keepdims=True))
        dg_ref[...] += jnp.sum(dhv * xhat, axis=0, keepdims=True)

    row = pl.BlockSpec((tr, D), lambda i: (i, 0))
    vec = pl.BlockSpec((1, D), lambda i: (0, 0))
    return pl.pallas_call(
        body, name=name, grid=(S // tr,), in_specs=[row, vec, row, row], out_specs=[row, vec],
        out_shape=[jax.ShapeDtypeStruct((S, D), F32), jax.ShapeDtypeStruct((1, D), F32)],
        compiler_params=_params(("arbitrary",)),
    )(x, gain, dh, dres)


def ffn_up_fwd(h, w_gu_t, *, name):
    S, K = h.shape
    F = w_gu_t.shape[0] // 2
    tm = _pick(S, (512, 256, 128))
    tn = _pick(F, (1408, 1024, 768, 512, 256, 128))
    nj = F // tn

    def body(h_ref, wg_ref, wu_ref, gu_ref, a_ref):
        hv = h_ref[...]
        g = _dot(hv, wg_ref[...], 1, 1)
        u = _dot(hv, wu_ref[...], 1, 1)
        gu_ref[0] = g
        gu_ref[1] = u
        a_ref[...] = (g * _sigmoid(g) * u).astype(BF16)

    return pl.pallas_call(
        body, name=name, grid=(nj, S // tm),
        in_specs=[pl.BlockSpec((tm, K), lambda j, i: (i, 0)), pl.BlockSpec((tn, K), lambda j, i: (j, 0)),
                  pl.BlockSpec((tn, K), lambda j, i: (nj + j, 0))],
        out_specs=[pl.BlockSpec((2, tm, tn), lambda j, i: (0, i, j)), pl.BlockSpec((tm, tn), lambda j, i: (i, j))],
        out_shape=[jax.ShapeDtypeStruct((2, S, F), F32), jax.ShapeDtypeStruct((S, F), BF16)],
        compiler_params=_params(("parallel", "parallel")),
    )(h, w_gu_t, w_gu_t)


def ffn_up_bwd(dy, w_down, gu, *, name):
    S, D = dy.shape
    F = w_down.shape[0]
    tm = _pick(S, (512, 256, 128))
    tn = _pick(F, (1408, 1024, 768, 512, 256, 128))

    def body(dy_ref, wd_ref, gu_ref, o_ref):
        da = _dot(dy_ref[...].astype(BF16), wd_ref[...], 1, 1)
        g = gu_ref[0]
        u = gu_ref[1]
        s = _sigmoid(g)
        o_ref[0] = (da * u * (s * (1.0 + g * (1.0 - s)))).astype(BF16)
        o_ref[1] = (da * g * s).astype(BF16)

    pair = pl.BlockSpec((2, tm, tn), lambda j, i: (0, i, j))
    return pl.pallas_call(
        body, name=name, grid=(F // tn, S // tm),
        in_specs=[pl.BlockSpec((tm, D), lambda j, i: (i, 0)), pl.BlockSpec((tn, D), lambda j, i: (j, 0)), pair],
        out_specs=pair, out_shape=jax.ShapeDtypeStruct((2, S, F), BF16),
        compiler_params=_params(("parallel", "parallel")),
    )(dy, w_down, gu)


def loss_head(y, target, *, name):
    S, D = y.shape
    tr = _pick(S, (512, 256, 128))

    def body(y_ref, t_ref, dy_ref, l_ref):
        @pl.when(pl.program_id(0) == 0)
        def _():
            l_ref[...] = jnp.zeros_like(l_ref)

        err = y_ref[...] - t_ref[...]
        dy_ref[...] = err * (1.0 / D)
        l_ref[...] += jnp.sum(0.5 * jnp.mean(err * err, axis=1, keepdims=True), axis=0, keepdims=True)

    row = pl.BlockSpec((tr, D), lambda i: (i, 0))
    one = pl.BlockSpec((1, 1), lambda i: (0, 0))
    dy, l = pl.pallas_call(
        body, name=name, grid=(S // tr,), in_specs=[row, row], out_specs=[row, one],
        out_shape=[jax.ShapeDtypeStruct((S, D), F32), jax.ShapeDtypeStruct((1, 1), F32)],
        compiler_params=_params(("arbitrary",)),
    )(y, target)
    return dy, l


def adamw(w, g, m, v, *, name):
    R, C = w.shape[-2:]
    tr = _pick(R, (512, 256, 128, 64, 32, 16, 8))

    def body(w_ref, g_ref, m_ref, v_ref, d_ref, mo_ref, vo_ref):
        gv = g_ref[...]
        mn = ADAM_B1 * m_ref[...] + (1.0 - ADAM_B1) * gv
        vn = ADAM_B2 * v_ref[...] + (1.0 - ADAM_B2) * jnp.square(gv)
        m_hat = mn / (1.0 - ADAM_B1 ** ADAM_STEP)
        v_hat = vn / (1.0 - ADAM_B2 ** ADAM_STEP)
        d_ref[...] = -ADAM_LR * (m_hat / (jnp.sqrt(v_hat) + ADAM_EPS) + ADAM_WD * w_ref[...])
        mo_ref[...] = mn
        vo_ref[...] = vn

    if w.ndim == 3:
        grid = (w.shape[0], R // tr)
        blk = pl.BlockSpec((None, tr, C), lambda l, i: (l, i, 0))
    else:
        grid = (R // tr,)
        blk = pl.BlockSpec((tr, C), lambda i: (i, 0))
    sds = jax.ShapeDtypeStruct(w.shape, F32)
    return pl.pallas_call(
        body, name=name, grid=grid, in_specs=[blk] * 4, out_specs=[blk] * 3, out_shape=[sds] * 3,
        compiler_params=_params(("parallel",) * len(grid)),
    )(w, g, m, v)


def _tri(n, kind):
    r = lax.broadcasted_iota(jnp.int32, (n, n), 0)
    c = lax.broadcasted_iota(jnp.int32, (n, n), 1)
    if kind == "row_gt_col":
        return (r > c).astype(BF16)
    if kind == "row_ge_col":
        return (r >= c).astype(BF16)
    if kind == "row_le_col":
        return (r <= c).astype(BF16)
    raise ValueError(kind)


def _sb_tile(qs, kj, r_carry, u_strict, masked):
    z = _dot(qs, kj, 1, 1)
    lb = jnp.minimum(z, 0.0) - jnp.log(1.0 + jnp.exp(-jnp.abs(z)))
    l1m = lb - z
    keep = None
    if masked:
        tq, tk = z.shape
        keep = lax.broadcasted_iota(jnp.int32, (tq, tk), 1) < lax.broadcasted_iota(jnp.int32, (tq, tk), 0)
        l1m = jnp.where(keep, l1m, 0.0)
    w = jnp.exp(lb + _dot(l1m.astype(BF16), u_strict) + r_carry)
    if masked:
        w = jnp.where(keep, w, 0.0)
    return lb, l1m, w, keep


def _sb_prep(T, nb, hd, refs_in, gains, scratch):
    q_scale = 1.0 / math.sqrt(hd)
    assert math.log2(q_scale) == round(math.log2(q_scale))

    def prep(i, _):
        rows = pl.ds(pl.multiple_of(i * T, T), T)
        for hh in range(2):
            sl = slice(hd * hh, hd * hh + hd)
            for n, (src, dst) in enumerate(zip(refs_in, scratch)):
                v = src[rows, sl]
                if n < 2:
                    v = v * lax.rsqrt(jnp.mean(v * v, axis=1, keepdims=True) + EPS) * gains[n][...]
                if n == 0:
                    v = v * q_scale
                dst[hh, rows, :] = v.astype(BF16)
        return 0

    lax.fori_loop(0, nb, prep, 0)


def _sb_chains(m, T, nq):
    rows = [pl.ds(pl.multiple_of((nq * m + qb) * T, T), T) for qb in range(nq)]
    return rows, [(hh, qb) for qb in range(nq) for hh in range(2)]


def _sb_sweep(tile, carry, chains, m, nq):
    for kk in reversed(range(nq)):
        carry = tile(nq * m + kk, carry, [(ch, ch[1] == kk) for ch in chains if ch[1] >= kk])
    return lax.fori_loop(0, nq * m, lambda jj, c: tile(nq * m - 1 - jj, c, [(ch, False) for ch in chains]), carry)


def sb_attn_fwd(qkv, q_gain, k_gain, *, name, gather=None):
    S, D3 = qkv.shape
    D = D3 // 3
    npairs = D // LANES
    hd = SB_HEAD_DIM
    T = min(SB_TILE, S)
    nb = S // T
    nq = SB_FWD_QUERY_BLOCKS
    assert nb % nq == 0

    def body(*refs):
        if gather is None:
            q_ref, k_ref, v_ref, qg_ref, kg_ref, us_ref, o_ref, qn_s, kn_s, vb_s = refs
        else:
            ng = len(gather)
            q_ref, k_ref, v_ref, qg_ref, kg_ref, us_ref = refs[:6]
            o_ref = refs[6 + ng]
            qn_s, kn_s, vb_s = refs[7 + 2 * ng:10 + 2 * ng]
            comm = (refs[6:6 + ng], refs[7 + ng:7 + 2 * ng], refs[10 + 2 * ng:])
            step = pl.program_id(0)
            pl.when(step == 0)(lambda: _ag_phase(*comm, "start"))
            pl.when(step == npairs - 1)(lambda: _ag_phase(*comm, "forward"))
        us = us_ref[...]
        _sb_prep(T, nb, hd, (q_ref, k_ref, v_ref), (qg_ref, kg_ref), (qn_s, kn_s, vb_s))

        def superblock(m, _):
            rows_q, chains = _sb_chains(m, T, nq)
            qs = {ch: qn_s[ch[0], rows_q[ch[1]], :] for ch in chains}

            def tile(j, carry, which):
                rows_j = pl.ds(pl.multiple_of(j * T, T), T)
                new = dict(carry)
                for ch, masked in which:
                    acc, rc = carry[ch]
                    _, l1m, w, _ = _sb_tile(qs[ch], kn_s[ch[0], rows_j, :], rc, us, masked)
                    new[ch] = (acc + _dot(w.astype(BF16), vb_s[ch[0], rows_j, :]),
                               rc + jnp.sum(l1m, axis=1, keepdims=True))
                return new

            carry = {ch: (jnp.zeros((T, hd), F32), jnp.zeros((T, 1), F32)) for ch in chains}
            carry = _sb_sweep(tile, carry, chains, m, nq)
            for qb in range(nq):
                o_ref[rows_q[qb], :] = jnp.concatenate([carry[(0, qb)][0], carry[(1, qb)][0]], axis=1)
            return 0

        lax.fori_loop(0, nb // nq, superblock, 0)
        if gather is not None:
            pl.when(step == npairs - 1)(lambda: _ag_phase(*comm, "finish"))

    col = lambda off: pl.BlockSpec((S, LANES), lambda p, off=off: (0, off + p))
    gain = pl.BlockSpec((1, hd), lambda p: (0, 0))
    in_specs = [col(0), col(npairs), col(2 * npairs), gain, gain, pl.BlockSpec((T, T), lambda p: (0, 0))]
    out_specs = [pl.BlockSpec((S, LANES), lambda p: (0, p))]
    out_shape = [jax.ShapeDtypeStruct((S, D), F32)]
    scratch = [pltpu.VMEM((2, S, hd), BF16)] * 3
    args = [qkv, qkv, qkv, q_gain, k_gain, _tri(T, "row_gt_col")]
    if gather is not None:
        in_specs += [HBM_SPEC] * len(gather)
        out_specs += [HBM_SPEC] * len(gather)
        out_shape += [jax.ShapeDtypeStruct((N_DEV,) + s.shape, s.dtype) for s in gather]
        scratch += _copy_semaphores(len(gather))
        args += list(gather)
    out = pl.pallas_call(
        body, name=name, grid=(npairs,), in_specs=in_specs, out_specs=out_specs, out_shape=out_shape,
        scratch_shapes=scratch, compiler_params=_params(("arbitrary",)),
    )(*args)
    return out[0] if gather is None else (out[0], list(out[1:]))


def sb_attn_bwd(qkv, o, do, q_gain, k_gain, *, name, scatter=None):
    S, D3 = qkv.shape
    D = D3 // 3
    npairs = D // LANES
    hd = SB_HEAD_DIM
    T = min(SB_TILE, S)
    nb = S // T
    nq = SB_BWD_QUERY_BLOCKS
    assert nb % nq == 0
    scale = 1.0 / math.sqrt(hd)

    def body(*refs):
        if scatter is None:
            (q_ref, k_ref, v_ref, o_ref, do_ref, qg_ref, kg_ref, us_ref,
             dqkv_ref, dg_ref, qn_s, kn_s, vb_s, dob_s) = refs
        else:
            ns = len(scatter)
            q_ref, k_ref, v_ref, o_ref, do_ref, qg_ref, kg_ref, us_ref = refs[:8]
            rs_in = refs[8:8 + ns]
            dqkv_ref, dg_ref = refs[8 + ns:10 + ns]
            rs_out = refs[10 + ns:10 + 2 * ns]
            qn_s, kn_s, vb_s, dob_s = refs[10 + 2 * ns:14 + 2 * ns]
            rs_sems = refs[14 + 2 * ns:]
            pl.when(pl.program_id(0) == 0)(lambda: _rs_phase(rs_in, rs_out, rs_sems, "start"))
        dq_ref, dk_ref, dv_ref = dqkv_ref.at[0], dqkv_ref.at[1], dqkv_ref.at[2]

        @pl.when(pl.program_id(0) == 0)
        def _():
            dg_ref[...] = jnp.zeros_like(dg_ref)

        us = us_ref[...]
        u_prefix = (1.0 - us.astype(F32)).astype(BF16)
        _sb_prep(T, nb, hd, (q_ref, k_ref, v_ref, do_ref), (qg_ref, kg_ref), (qn_s, kn_s, vb_s, dob_s))
        dk_ref[...] = jnp.zeros_like(dk_ref)
        dv_ref[...] = jnp.zeros_like(dv_ref)

        def superblock(m, _):
            rows_q, chains = _sb_chains(m, T, nq)
            qs = {ch: qn_s[ch[0], rows_q[ch[1]], :] for ch in chains}
            doi ={ch: dob_s[ch[0], rows_q[ch[1]], :] for ch in chains}
            dt_total = {ch: jnp.sum(doi[ch].astype(F32) * o_ref[rows_q[ch[1]], hd * ch[0]:hd * ch[0] + hd],
                                    axis=1, keepdims=True) for ch in chains}

            def tile(j, carry, which):
                rows_j = pl.ds(pl.multiple_of(j * T, T), T)
                new = dict(carry)
                dk_part, dv_part = {}, {}
                for ch, masked in which:
                    hh = ch[0]
                    dq_acc, rc, gc = carry[ch]
                    kj = kn_s[hh, rows_j, :]
                    lb, l1m, w, keep = _sb_tile(qs[ch], kj, rc, us, masked)
                    wb = w.astype(BF16)
                    g = _dot(doi[ch], vb_s[hh, rows_j, :], 1, 1) * wb.astype(F32)
                    g_row = jnp.sum(g, axis=1, keepdims=True)
                    g_upto = (dt_total[ch] - gc - g_row) + _dot(g.astype(BF16), u_prefix)
                    dz = g - g_upto * jnp.exp(lb)
                    if masked:
                        dz = jnp.where(keep, dz, 0.0)
                    dzb = dz.astype(BF16)
                    dv_part[hh] = dv_part.get(hh, 0.0) + _dot(wb, doi[ch], 0, 0)
                    dk_part[hh] = dk_part.get(hh, 0.0) + _dot(dzb, qs[ch], 0, 0)
                    new[ch] = (dq_acc + _dot(dzb, kj), rc + jnp.sum(l1m, axis=1, keepdims=True),
                               gc + g_row)
                dv_ref[rows_j, :] += jnp.concatenate([dv_part[0], dv_part[1]], axis=1)
                dk_ref[rows_j, :] += jnp.concatenate([dk_part[0], dk_part[1]], axis=1)
                return new

            zero1 = jnp.zeros((T, 1), F32)
            carry = {ch: (jnp.zeros((T, hd), F32), zero1, zero1) for ch in chains}
            carry = _sb_sweep(tile, carry, chains, m, nq)
            for qb in range(nq):
                dq_ref[rows_q[qb], :] = jnp.concatenate([carry[(0, qb)][0], carry[(1, qb)][0]], axis=1) * scale
            return 0

        lax.fori_loop(0, nb // nq, superblock, 0)

        def finish(i, carry):
            rows = pl.ds(pl.multiple_of(i * T, T), T)
            new = []
            for hh in range(2):
                sl = slice(hd * hh, hd * hh + hd)
                outs = []
                for raw_ref, gain_ref, dn in ((q_ref, qg_ref, dq_ref[rows, sl]), (k_ref, kg_ref, dk_ref[rows, sl])):
                    raw = raw_ref[rows, sl]
                    r = lax.rsqrt(jnp.mean(raw * raw, axis=1, keepdims=True) + EPS)
                    hat = raw * r
                    t = dn * gain_ref[...]
                    outs.append((r * (t - hat * jnp.mean(hat * t, axis=1, keepdims=True)),
                                 jnp.sum(dn * hat, axis=0, keepdims=True)))
                dq_ref[rows, sl] = outs[0][0]
                dk_ref[rows, sl] = outs[1][0]
                new.append((carry[hh][0] + outs[0][1], carry[hh][1] + outs[1][1]))
            return tuple(new)

        zg = (jnp.zeros((1, hd), F32), jnp.zeros((1, hd), F32))
        tot = lax.fori_loop(0, nb, finish, (zg, zg))
        dg_ref[0:1, 0:hd] += tot[0][0] + tot[1][0]
        dg_ref[1:2, 0:hd] += tot[0][1] + tot[1][1]
        if scatter is not None:
            pl.when(pl.program_id(0) == npairs - 1)(lambda: _rs_phase(rs_in, rs_out, rs_sems, "finish"))

    col = lambda off: pl.BlockSpec((S, LANES), lambda p, off=off: (0, off + p))
    gain = pl.BlockSpec((1, hd), lambda p: (0, 0))
    tri = pl.BlockSpec((T, T), lambda p: (0, 0))
    pair = pl.BlockSpec((S, LANES), lambda p: (0, p))
    in_specs = [col(0), col(npairs), col(2 * npairs), pair, pair, gain, gain, tri]
    out_specs = [pl.BlockSpec((3, S, LANES), lambda p: (0, 0, p)), pl.BlockSpec((8, LANES), lambda p: (0, 0))]
    out_shape = [jax.ShapeDtypeStruct((3, S, D), F32), jax.ShapeDtypeStruct((8, LANES), F32)]
    scratch = [pltpu.VMEM((2, S, hd), BF16)] * 4
    args = [qkv, qkv, qkv, o, do, q_gain, k_gain, _tri(T, "row_gt_col")]
    if scatter is not None:
        in_specs += [HBM_SPEC] * len(scatter)
        out_specs += [HBM_SPEC] * len(scatter)
        out_shape += [jax.ShapeDtypeStruct(g.shape, g.dtype) for g in scatter]
        scratch += _copy_semaphores(len(scatter))
        args += list(scatter)
    out = pl.pallas_call(
        body, name=name, grid=(npairs,), in_specs=in_specs, out_specs=out_specs, out_shape=out_shape,
        scratch_shapes=scratch, compiler_params=_params(("arbitrary",)),
    )(*args)
    res = (out[0], out[1][0:1, :hd], out[1][1:2, :hd])
    return res if scatter is None else res + (list(out[2:]),)


def gmlp_fwd(zzpre, b_in, v_gain, wc, bsf, *, name):
    S, H2 = zzpre.shape
    H = H2 // 2
    G, T, _ = wc.shape
    gd = H // G

    def body(z_ref, b_ref, vg_ref, wc_ref, bs_ref, p_ref):
        zz, _ = _gelu_and_grad(z_ref[...] + b_ref[...])
        u = zz[:, :H]
        v = zz[:, H:]
        vn = v * lax.rsqrt(jnp.mean(v * v, axis=1, keepdims=True) + EPS) * vg_ref[...]
        for g in range(G):
            gs = slice(g * gd, (g + 1) * gd)
            mixed = _dot(wc_ref[g], vn[:, gs].astype(BF16)) + bs_ref[g]
            p_ref[:, gs] = (u[:, gs] * mixed).astype(BF16)

    full3 = lambda shp: pl.BlockSpec(shp, lambda c: (0, 0, 0))
    return pl.pallas_call(
        body, name=name, grid=(S // T,),
        in_specs=[pl.BlockSpec((T, H2), lambda c: (c, 0)), pl.BlockSpec((1, H2), lambda c: (0, 0)),
                  pl.BlockSpec((1, H), lambda c: (0, 0)), full3((G, T, T)), full3((G, T, gd))],
        out_specs=pl.BlockSpec((T, H), lambda c: (c, 0)), out_shape=jax.ShapeDtypeStruct((S, H), BF16),
        compiler_params=_params(("parallel",)),
    )(zzpre, b_in, v_gain, wc, bsf)


def gmlp_bwd(zzpre, b_in, v_gain, wc, bsf, dp, *, name):
    S, H2 = zzpre.shape
    H = H2 // 2
    G, T, _ = wc.shape
    gd = H // G
    assert G <= LANES

    def body(z_ref, b_ref, vg_ref, wc_ref, bs_ref, dp_ref, dzz_ref, db_ref, dvg_ref, dws_ref, dbs_ref):
        @pl.when(pl.program_id(0) == 0)
        def _():
            db_ref[...] = jnp.zeros_like(db_ref)
            dvg_ref[...] = jnp.zeros_like(dvg_ref)
            dws_ref[...] = jnp.zeros_like(dws_ref)
            dbs_ref[...] = jnp.zeros_like(dbs_ref)

        zz, gp = _gelu_and_grad(z_ref[...] + b_ref[...])
        u = zz[:, :H]
        v = zz[:, H:]
        r = lax.rsqrt(jnp.mean(v * v, axis=1, keepdims=True) + EPS)
        vhat = v * r
        vg = vg_ref[...]
        vn = vhat * vg
        dpv = dp_ref[...]
        tril = lax.broadcasted_iota(jnp.int32, (T, T), 1) <= lax.broadcasted_iota(jnp.int32, (T, T), 0)
        lane = lax.broadcasted_iota(jnp.int32, (T, LANES), 1)
        dbs = jnp.zeros((T, LANES), F32)
        du_parts, dvn_parts = [], []
        for g in range(G):
            gs = slice(g * gd, (g + 1) * gd)
            vng = vn[:, gs].astype(BF16)
            wcg = wc_ref[g]
            mixed = _dot(wcg, vng) + bs_ref[g]
            dpg = dpv[:, gs]
            du_parts.append(dpg * mixed)
            dmx = dpg * u[:, gs]
            dmxb = dmx.astype(BF16)
            dvn_parts.append(_dot(wcg, dmxb, 0, 0))
            dws_ref[g] += jnp.where(tril, _dot(dmxb, vng, 1, 1), 0.0)
            dbs = dbs + jnp.where(lane == g, jnp.sum(dmx, axis=1, keepdims=True), 0.0)
        dbs_ref[...] += dbs
        du = jnp.concatenate(du_parts, axis=1)
        dvn = jnp.concatenate(dvn_parts, axis=1)
        dvg_ref[...] += jnp.sum(dvn * vhat, axis=0, keepdims=True)
        t = dvn * vg
        dv = r * (t - vhat * jnp.mean(vhat * t, axis=1, keepdims=True))
        dzu = du * gp[:, :H]
        dzv = dv * gp[:, H:]
        dzz_ref[:, :H] = dzu.astype(BF16)
        dzz_ref[:, H:] = dzv.astype(BF16)
        db_ref[:, :H] += jnp.sum(dzu, axis=0, keepdims=True)
        db_ref[:, H:] += jnp.sum(dzv, axis=0, keepdims=True)

    full3 = lambda shp: pl.BlockSpec(shp, lambda c: (0, 0, 0))
    vec = lambda n: pl.BlockSpec((1, n), lambda c: (0, 0))
    return pl.pallas_call(
        body, name=name, grid=(S // T,),
        in_specs=[pl.BlockSpec((T, H2), lambda c: (c, 0)), vec(H2), vec(H), full3((G, T, T)), full3((G, T, gd)),
                  pl.BlockSpec((T, H), lambda c: (c, 0))],
        out_specs=[pl.BlockSpec((T, H2), lambda c: (c, 0)), vec(H2), vec(H), full3((G, T, T)),
                   pl.BlockSpec((T, LANES), lambda c: (0, 0))],
        out_shape=[jax.ShapeDtypeStruct((S, H2), BF16), jax.ShapeDtypeStruct((1, H2), F32),
                   jax.ShapeDtypeStruct((1, H), F32), jax.ShapeDtypeStruct((G, T, T), F32),
                   jax.ShapeDtypeStruct((T, LANES), F32)],
        compiler_params=_params(("arbitrary",)),
    )(zzpre, b_in, v_gain, wc, bsf, dp)


def _shift_rows(v, k, n_rows):
    if k == 0:
        return v
    rolled = pltpu.roll(v, k % n_rows, 0)
    row = lax.broadcasted_iota(jnp.int32, v.shape, 0)
    keep = (row >= k) if k > 0 else (row < n_rows + k)
    return jnp.where(keep, rolled, 0.0)


def conv_fwd(zx, conv_w, conv_b, col0, *, name):
    S = zx.shape[0]
    C = conv_w.shape[1]
    tc = _pick(C, (256, 128))
    off = col0 // tc
    assert col0 % tc == 0

    def body(x_ref, w_ref, b_ref, o_ref):
        xv = x_ref[...]
        acc = b_ref[...] + w_ref[SSM_CONV - 1:SSM_CONV, :] * xv
        for k in range(SSM_CONV - 1):
            acc = acc + w_ref[k:k + 1, :] * _shift_rows(xv, SSM_CONV - 1 - k, S)
        o_ref[...] = acc * _sigmoid(acc)

    return pl.pallas_call(
        body, name=name, grid=(C // tc,),
        in_specs=[pl.BlockSpec((S, tc), lambda j: (0, off + j)), pl.BlockSpec((SSM_CONV, tc), lambda j: (0, j)),
                  pl.BlockSpec((1, tc), lambda j: (0, j))],
        out_specs=pl.BlockSpec((S, tc), lambda j: (0, j)), out_shape=jax.ShapeDtypeStruct((S, C), F32),
        compiler_params=_params(("parallel",)),
    )(zx, conv_w, conv_b)


def conv_bwd(zx, conv_w, conv_b, col0, douts, *, name):
    S = zx.shape[0]
    C = conv_w.shape[1]
    tc = LANES
    off = col0 // tc
    counts = [d.shape[1] // tc for d in douts]
    starts = [sum(counts[:p]) for p in range(len(douts))]
    assert sum(counts) * tc == C and all(d.shape[1] % tc == 0 for d in douts)

    def body(x_ref, w_ref, b_ref, *rest):
        do_refs, (dx_ref, dw_ref, db_ref) = rest[:len(douts)], rest[len(douts):]
        j = pl.program_id(0)
        dov = do_refs[-1][...]
        for p in reversed(range(len(douts) - 1)):
            dov = jnp.where(j < starts[p + 1], do_refs[p][...], dov)
        xv = x_ref[...]
        shifted = [_shift_rows(xv, SSM_CONV - 1 - k, S) for k in range(SSM_CONV)]
        acc = b_ref[...]
        for k in range(SSM_CONV):
            acc = acc + w_ref[k:k + 1, :] * shifted[k]
        s = _sigmoid(acc)
        dacc = dov * (s * (1.0 + acc * (1.0 - s)))
        db_ref[...] = jnp.sum(dacc, axis=0, keepdims=True)
        dx = jnp.zeros_like(xv)
        for k in range(SSM_CONV):
            dw_ref[k:k + 1, :] = jnp.sum(dacc * shifted[k], axis=0, keepdims=True)
            dx = dx + w_ref[k:k + 1, :] * _shift_rows(dacc, -(SSM_CONV - 1 - k), S)
        dx_ref[...] = dx

    slab = pl.BlockSpec((S, tc), lambda j: (0, j))
    piece_specs = [pl.BlockSpec((S, tc), lambda j, a=starts[p], n=counts[p]: (0, jnp.clip(j - a, 0, n - 1)))
                   for p in range(len(douts))]
    return pl.pallas_call(
        body, name=name, grid=(C // tc,),
        in_specs=[pl.BlockSpec((S, tc), lambda j: (0, off + j)), pl.BlockSpec((SSM_CONV, tc), lambda j: (0, j)),
                  pl.BlockSpec((1, tc), lambda j: (0, j))] + piece_specs,
        out_specs=[slab, pl.BlockSpec((SSM_CONV, tc), lambda j: (0, j)), pl.BlockSpec((1, tc), lambda j: (0, j))],
        out_shape=[jax.ShapeDtypeStruct((S, C), F32), jax.ShapeDtypeStruct((SSM_CONV, C), F32),
                   jax.ShapeDtypeStruct((1, C), F32)],
        compiler_params=_params(("arbitrary",)),
    )(zx, conv_w, conv_b, *douts)


def _ssd_chunk_terms(dtraw, bias, a_log, tl):
    dt = _softplus(dtraw + bias)
    a_neg = -jnp.exp(a_log)
    ac = _dot_exact_left(tl, dt * a_neg)
    ac_last = ac[ac.shape[0] - 1:, :]
    return dt, a_neg, ac, ac.T, jnp.exp(ac), jnp.exp(ac_last - ac), jnp.exp(ac_last)


def _ssd_specs(S, L, G, hpg, pd, inner):
    gw = hpg * pd
    n = SSM_STATE
    xb = inner // n

    def mk(cidx):
        return dict(
            x=pl.BlockSpec((L, gw), lambda g, c: (cidx(c), g)),
            b=pl.BlockSpec((L, n), lambda g, c: (cidx(c), xb + g)),
            c=pl.BlockSpec((L, n), lambda g, c: (cidx(c), xb + G + g)),
            z=pl.BlockSpec((L, gw), lambda g, c: (cidx(c), g)),
            dt=pl.BlockSpec((L, LANES), lambda g, c: (cidx(c), g)),
            gvec=pl.BlockSpec((1, 1, LANES), lambda g, c: (g, 0, 0)),
            chan=pl.BlockSpec((1, gw), lambda g, c: (0, g)),
            tri=pl.BlockSpec((L, L), lambda g, c: (0, 0)),
            hp=pl.BlockSpec((1, 1, gw, n), lambda g, c: (g, cidx(c), 0, 0)),
            bc=pl.BlockSpec((L, n), lambda g, c: (cidx(c), g)),
        )
    return mk


def ssd_fwd(xbc, zx, dtg, bias_g, alog_g, d_chan, ngain, L, G, *, name):
    S = xbc.shape[0]
    n = SSM_STATE
    inner = xbc.shape[1] - 2 * G * n
    gw = inner // G
    pd = SB_HEAD_DIM
    hpg = gw // pd
    nc = S // L
    sp = _ssd_specs(S, L, G, hpg, pd, inner)(lambda c: c)

    def body(x_ref, b_ref, c_ref, z_ref, dt_ref, bias_ref, alog_ref, d_ref, ng_ref, tl_ref,
             yn_ref, y_ref, hp_ref, state):
        @pl.when(pl.program_id(1) == 0)
        def _():
            state[...] = jnp.zeros_like(state)

        dt, _, ac, act, ea, dte, cd = _ssd_chunk_terms(dt_ref[...], bias_ref[0], alog_ref[0], tl_ref[...])
        xv = x_ref[...]
        bm = b_ref[...].astype(BF16)
        cm = c_ref[...].astype(BF16)
        cb = _dot(cm, bm, 1, 1)
        tril = lax.broadcasted_iota(jnp.int32, (L, L), 1) <= lax.broadcasted_iota(jnp.int32, (L, L), 0)
        hp_ref[0, 0] = state[...]
        for r in range(hpg):
            ps = slice(r * pd, (r + 1) * pd)
            xr = xv[:, ps]
            xdt = xr * dt[:, r:r + 1]
            lm = jnp.exp(jnp.where(tril, ac[:, r:r + 1] - act[r:r + 1, :], -jnp.inf))
            hprev = state[ps, :]
            y = _dot((cb * lm).astype(BF16), xdt.astype(BF16))
            y = y + _dot(cm, hprev.astype(BF16), 1, 1) * ea[:, r:r + 1]
            y_ref[:, ps] = y + xr * d_ref[:, ps]
            st = _dot((xdt * dte[:, r:r + 1]).astype(BF16), bm, 0, 0)
            state[ps, :] = hprev * cd[:, r:r + 1] + st
        yfull = y_ref[...]
        zg = z_ref[...]
        yg = yfull * (zg * _sigmoid(zg))
        yn_ref[...] = (yg * lax.rsqrt(jnp.mean(yg * yg, axis=1, keepdims=True) + EPS) * ng_ref[...]).astype(BF16)

    return pl.pallas_call(
        body, name=name, grid=(G, nc),
        in_specs=[sp["x"], sp["b"], sp["c"], sp["z"], sp["dt"], sp["gvec"], sp["gvec"], sp["chan"], sp["chan"], sp["tri"]],
        out_specs=[sp["x"], sp["x"], sp["hp"]],
        out_shape=[jax.ShapeDtypeStruct((S, inner), BF16), jax.ShapeDtypeStruct((S, inner), F32),
                   jax.ShapeDtypeStruct((G, nc, gw, n), F32)],
        scratch_shapes=[pltpu.VMEM((gw, n), F32)],
        compiler_params=_params(("arbitrary", "arbitrary")),
    )(xbc, xbc, xbc, zx, dtg, bias_g, alog_g, d_chan, ngain, _tri(L, "row_ge_col"))


def ssd_bwd(xbc, zx, dtg, bias_g, alog_g, d_chan, ngain, yfull, hp, dyn, L, G, *, name):
    S = xbc.shape[0]
    n = SSM_STATE
    inner = xbc.shape[1] - 2 * G * n
    gw = inner // G
    pd = SB_HEAD_DIM
    hpg = gw // pd
    nc = S // L
    sp = _ssd_specs(S, L, G, hpg, pd, inner)(lambda c: nc - 1 - c)

    def body(x_ref, b_ref, c_ref, z_ref, dt_ref, bias_ref, alog_ref, d_ref, ng_ref, tl_ref, tu_ref,
             yf_ref, hp_ref, dyn_ref,
             dz_ref, dx_ref, db_ref, dc_ref, ddt_ref, dbias_ref, dalog_ref, dd_ref, dng_ref, dstate):
        first = pl.program_id(1) == 0

        @pl.when(first)
        def _():
            dstate[...] = jnp.zeros_like(dstate)
            dbias_ref[...] = jnp.zeros_like(dbias_ref)
            dalog_ref[...] = jnp.zeros_like(dalog_ref)
            dd_ref[...] = jnp.zeros_like(dd_ref)
            dng_ref[...] = jnp.zeros_like(dng_ref)

        dtraw = dt_ref[...]
        dt, a_neg, ac, act, ea, dte, cd = _ssd_chunk_terms(dtraw, bias_ref[0], alog_ref[0], tl_ref[...])
        xv = x_ref[...]
        bm = b_ref[...].astype(BF16)
        cm = c_ref[...].astype(BF16)
        cb = _dot(cm, bm, 1, 1)
        tril = lax.broadcasted_iota(jnp.int32, (L, L), 1) <= lax.broadcasted_iota(jnp.int32, (L, L), 0)
        lane = lax.broadcasted_iota(jnp.int32, (L, LANES), 1)
        lane1 = lax.broadcasted_iota(jnp.int32, (1, LANES), 1)

        yfull = yf_ref[...]
        zg = z_ref[...]
        sg = _sigmoid(zg)
        gate = zg * sg
        yg = yfull * gate
        rr = lax.rsqrt(jnp.mean(yg * yg, axis=1, keepdims=True) + EPS)
        yhat = yg * rr
        dynv = dyn_ref[...]
        dng_ref[...] += jnp.sum(dynv * yhat, axis=0, keepdims=True)
        t = dynv * ng_ref[...]
        dyg = rr * (t - yhat * jnp.mean(yhat * t, axis=1, keepdims=True))
        dy = dyg * gate
        dz_ref[...] = dyg * yfull * (sg * (1.0 + zg * (1.0 - sg)))

        dcb = jnp.zeros((L, L), F32)
        dc_acc = jnp.zeros((L, n), F32)
        db_acc = jnp.zeros((L, n), F32)
        dac = jnp.zeros((L, LANES), F32)
        xdx = jnp.zeros((L, LANES), F32)
        tail = jnp.zeros((1, LANES), F32)
        dskip = jnp.zeros((1, LANES), F32)
        ones_l = jnp.ones((L, LANES), BF16)
        for r in range(hpg):
            ps = slice(r * pd, (r + 1) * pd)
            xr = xv[:, ps]
            dyr = dy[:, ps]
            dtr = dt[:, r:r + 1]
            dter = dte[:, r:r + 1]
            cdr = cd[:, r:r + 1]
            xdt = xr * dtr
            xdtb = xdt.astype(BF16)
            dyrb = dyr.astype(BF16)
            lm = jnp.exp(jnp.where(tril, ac[:, r:r + 1] - act[r:r + 1, :], -jnp.inf))
            m32 = cb * lm
            mb = m32.astype(BF16)
            hprev = hp_ref[0, 0, ps, :]
            hpb = hprev.astype(BF16)
            dhn = dstate[ps, :]
            dhnb = dhn.astype(BF16)
            ear = ea[:, r:r + 1]
            gy = (dyr * ear).astype(BF16)
            dc_acc = dc_acc + _dot(gy, hpb)
            dstate[ps, :] = _dot(gy, cm, 0, 0) + dhn * cdr
            bdh = _dot(bm, dhnb, 1, 1)
            db_acc = db_acc + _dot((xdt * dter).astype(BF16), dhnb)
            dm = _dot(dyrb, xdtb, 1, 1)
            dxdt = bdh * dter + _dot(mb, dyrb, 0, 0)
            dcb = dcb + dm * lm
            wmat = dm * m32
            whi = wmat.astype(BF16)
            wlo = (wmat - whi.astype(F32)).astype(BF16)
            col_w = _dot(whi, ones_l, 0, 0) + _dot(wlo, ones_l, 0, 0)
            t_end = xdt * bdh * dter
            e_r = jnp.sum(wmat, axis=1, keepdims=True) \
                + jnp.sum(dyr * _dot(cm, hpb, 1, 1) * ear - t_end, axis=1, keepdims=True)
            c_r = cdr * _sum_all(dhn * hprev) + _sum_all(t_end)
            dac = dac + jnp.where(lane == r, e_r - col_w, 0.0)
            xdx = xdx + jnp.where(lane == r, jnp.sum(dxdt * xr, axis=1, keepdims=True), 0.0)
            tail = tail + jnp.where(lane1 == r, c_r, 0.0)
            dskip = dskip + jnp.where(lane1 == r, _sum_all(dyr * xr), 0.0)
            dx_ref[:, ps] = dxdt * dtr + dyr * d_ref[:, ps]
        dcbb = dcb.astype(BF16)
        dc_ref[...] = dc_acc + _dot(dcbb, bm)
        db_ref[...] = db_acc + _dot(dcbb, cm, 0, 0)
        da = _dot_exact_left(tu_ref[...], dac) + tail
        real = lane < hpg
        ddt = jnp.where(real, (da * a_neg + xdx) * _sigmoid(dtraw + bias_ref[0]), 0.0)
        ddt_ref[...] = ddt
        dd_ref[0] += dskip
        dbias_ref[0] += jnp.sum(ddt, axis=0, keepdims=True)
        dalog_ref[0] += jnp.where(lane1 < hpg, jnp.sum(da * dt, axis=0, keepdims=True) * a_neg, 0.0)

    return pl.pallas_call(
        body, name=name, grid=(G, nc),
        in_specs=[sp["x"], sp["b"], sp["c"], sp["z"], sp["dt"], sp["gvec"], sp["gvec"], sp["chan"], sp["chan"],
                  sp["tri"], sp["tri"], sp["x"], sp["hp"], sp["x"]],
        out_specs=[sp["x"], sp["x"], sp["bc"], sp["bc"], sp["dt"], sp["gvec"], sp["gvec"], sp["gvec"], sp["chan"]],
        out_shape=[jax.ShapeDtypeStruct((S, inner), F32), jax.ShapeDtypeStruct((S, inner), F32),
                   jax.ShapeDtypeStruct((S, G * n), F32), jax.ShapeDtypeStruct((S, G * n), F32),
                   jax.ShapeDtypeStruct((S, G * LANES), F32), jax.ShapeDtypeStruct((G, 1, LANES), F32),
                   jax.ShapeDtypeStruct((G, 1, LANES), F32), jax.ShapeDtypeStruct((G, 1, LANES), F32),
                   jax.ShapeDtypeStruct((1, inner), F32)],
        scratch_shapes=[pltpu.VMEM((gw, n), F32)],
        compiler_params=_params(("arbitrary", "arbitrary")),
    )(xbc, xbc, xbc, zx, dtg, bias_g, alog_g, d_chan, ngain, _tri(L, "row_ge_col"), _tri(L, "row_le_col"),
      yfull, hp, dyn)


def _spread_dt(w_dt_t, G, hpg):
    K = w_dt_t.shape[1]
    w = w_dt_t.reshape(G, hpg, K)
    return jnp.pad(w, ((0, 0), (0, LANES - hpg), (0, 0))).reshape(G * LANES, K)


def _group_vec(v, G, hpg):
    return jnp.pad(v.reshape(G, 1, hpg), ((0, 0), (0, 0), (0, LANES - hpg)))


def local_step(x, target, W, late=None):
    S, D = x.shape
    depth = W["mix_norm"].shape[0]
    gm_groups, gm_chunk = W["gm_w_s"].shape[1], W["gm_w_s"].shape[2]
    heads = W["ssm_dt_bias"].shape[1]
    inner = heads * SB_HEAD_DIM
    L = gm_chunk
    received = None

    saved = []
    for i in range(depth):
        kind, j = i % 3, i // 3
        s = dict(x=x)
        h = rms_fwd(x, W["mix_norm"][i:i + 1], name="rms_mix_fwd")
        s["h"] = h
        if kind == 0:
            qkv = matmul(h, W["sb_w_qkv"][j], tb=True, name="mm_qkv")
            if late is not None and i == 0:
                o, gathered = sb_attn_fwd(qkv, W["sb_q_gain"][j:j + 1], W["sb_k_gain"][j:j + 1], name="sb_fwd_gather",
                                          gather=late.shards)
                late.fill(W, gathered)
            else:
                o = sb_attn_fwd(qkv, W["sb_q_gain"][j:j + 1], W["sb_k_gain"][j:j + 1], name="sb_fwd")
            x1 = matmul(o, W["sb_w_o"][j], residual=x, name="mm_sb_out")
            s.update(qkv=qkv, o=o)
        elif kind == 1:
            wc = jnp.where(jnp.tril(jnp.ones((gm_chunk, gm_chunk), bool)), W["gm_w_s"][j], 0.0).astype(BF16)
            bsf = jnp.broadcast_to(W["gm_b_s"][j][:, :, None], (gm_groups, gm_chunk, W["gm_v_gain"].shape[1] // gm_groups)).astype(F32)
            zzpre = matmul(h, W["gm_w_in"][j], tb=True, name="mm_gm_in")
            p = gmlp_fwd(zzpre, W["gm_b_in"][j:j + 1], W["gm_v_gain"][j:j + 1], wc, bsf, name="gm_fwd")
            x1 = matmul(p, W["gm_w_out"][j], residual=x, name="mm_gm_out")
            s.update(zzpre=zzpre, p=p, wc=wc, bsf=bsf)
        else:
            conv_dim = W["ssm_conv_w"].shape[2]
            G = (conv_dim - inner) // (2 * SSM_STATE)
            hpg = heads // G
            w_in = W["ssm_w_in"][j]
            w_zx = w_in[:inner + conv_dim]
            w_dtg = _spread_dt(w_in[inner + conv_dim:], G, hpg)
            bias_g = _group_vec(W["ssm_dt_bias"][j], G, hpg)
            alog_g = _group_vec(W["ssm_a_log"][j], G, hpg)
            d_chan = jnp.repeat(W["ssm_d"][j], SB_HEAD_DIM)[None, :]
            ngain = W["ssm_norm_gain"][j:j + 1]
            zx = matmul(h, w_zx, tb=True, name="mm_ssm_zx")
            dtg = matmul(h, w_dtg, tb=True, name="mm_ssm_dt")
            xbc = conv_fwd(zx, W["ssm_conv_w"][j], W["ssm_conv_b"][j:j + 1], inner, name="conv_fwd")
            yn, yfull, hp = ssd_fwd(xbc, zx, dtg, bias_g, alog_g, d_chan, ngain, L, G, name="ssd_fwd")
            x1 = matmul(yn, W["ssm_w_out"][j], residual=x, name="mm_ssm_out")
            s.update(w_zx=w_zx, w_dtg=w_dtg, bias_g=bias_g, alog_g=alog_g, d_chan=d_chan, ngain=ngain,
                     zx=zx, dtg=dtg, xbc=xbc, yn=yn, yfull=yfull, hp=hp)
        h2 = rms_fwd(x1, W["ffn_norm"][i:i + 1], name="rms_ffn_fwd")
        gu, a = ffn_up_fwd(h2, W["ffn_w_gu"][i], name="ffn_up_fwd")
        x2 = matmul(a, W["ffn_w_down"][i], residual=x1, name="mm_ffn_down")
        s.update(x1=x1, h2=h2, gu=gu, a=a)
        saved.append(s)
        x = x2

    dx, loss = loss_head(x, target, name="loss_head")

    gw = {k: {} for k in WEIGHTS}
    for i in reversed(range(depth)):
        kind, j = i % 3, i // 3
        s = saved[i]
        gw["ffn_w_down"][i] = matmul(s["a"], dx, ta=True, out_dtype=BF16, name="mm_ffn_dwdown")
        dgu = ffn_up_bwd(dx, W["ffn_w_down"][i], s["gu"], name="ffn_up_bwd")
        dh2 = matmul(dgu, W["ffn_w_gu"][i], a_split=2, name="mm_ffn_dh")
        gw["ffn_w_gu"][i] = matmul(dgu, s["h2"], ta=True, a_split=2, out_dtype=BF16, name="mm_ffn_dwgu")
        dx1, dgn = rms_bwd(s["x1"], W["ffn_norm"][i:i + 1], dh2, dx, name="rms_ffn_bwd")
        gw["ffn_norm"][i] = dgn[0]
        if kind == 0:
            do = matmul(dx1, W["sb_w_o"][j], tb=True, name="mm_sb_do")
            gw["sb_w_o"][j] = matmul(s["o"], dx1, ta=True, out_dtype=BF16, name="mm_sb_dwo")
            if late is not None and i == 0:
                dqkv, dqg, dkg, received = sb_attn_bwd(
                    s["qkv"], s["o"], do, W["sb_q_gain"][j:j + 1], W["sb_k_gain"][j:j + 1], name="sb_bwd_scatter",
                    scatter=late.contributions(gw))
            else:
                dqkv, dqg, dkg = sb_attn_bwd(s["qkv"], s["o"], do, W["sb_q_gain"][j:j + 1], W["sb_k_gain"][j:j + 1],
                                             name="sb_bwd")
            gw["sb_q_gain"][j] = dqg[0]
            gw["sb_k_gain"][j] = dkg[0]
            dh = matmul(dqkv, W["sb_w_qkv"][j], a_split=3, name="mm_sb_dh")
            gw["sb_w_qkv"][j] = matmul(dqkv, s["h"], ta=True, a_split=3, out_dtype=BF16, name="mm_sb_dwqkv")
        elif kind == 1:
            dp = matmul(dx1, W["gm_w_out"][j], tb=True, name="mm_gm_dp")
            gw["gm_w_out"][j] = matmul(s["p"], dx1, ta=True, out_dtype=BF16, name="mm_gm_dwout")
            dzz, db_in, dvg, dws, dbs = gmlp_bwd(s["zzpre"], W["gm_b_in"][j:j + 1], W["gm_v_gain"][j:j + 1],
                                                s["wc"], s["bsf"], dp, name="gm_bwd")
            gw["gm_b_in"][j] = db_in[0]
            gw["gm_v_gain"][j] = dvg[0]
            gw["gm_w_s"][j] = dws
            gw["gm_b_s"][j] = dbs[:, :gm_groups].T
            dh = matmul(dzz, W["gm_w_in"][j], name="mm_gm_dh")
            gw["gm_w_in"][j] = matmul(dzz, s["h"], ta=True, out_dtype=BF16, name="mm_gm_dwin")
        else:
            conv_dim = W["ssm_conv_w"].shape[2]
            G = (conv_dim - inner) // (2 * SSM_STATE)
            hpg = heads // G
            dyn = matmul(dx1, W["ssm_w_out"][j], tb=True, name="mm_ssm_dyn")
            gw["ssm_w_out"][j] = matmul(s["yn"], dx1, ta=True, out_dtype=BF16, name="mm_ssm_dwout")
            dz, dxs, dbm, dcm, ddt, dbias, dalog, dd, dng = ssd_bwd(
                s["xbc"], s["zx"], s["dtg"], s["bias_g"], s["alog_g"], s["d_chan"], s["ngain"], s["yfull"], s["hp"],
                dyn, L, G, name="ssd_bwd")
            dpre, dcw, dcb = conv_bwd(s["zx"], W["ssm_conv_w"][j], W["ssm_conv_b"][j:j + 1], inner, [dxs, dbm, dcm],
                                      name="conv_bwd")
            dzx = jnp.concatenate([dz, dpre], axis=1)
            dh = matmul(ddt, s["w_dtg"], name="mm_ssm_dh_dt")
            dh = matmul(dzx, s["w_zx"], residual=dh, name="mm_ssm_dh")
            dw_zx = matmul(dzx, s["h"], ta=True, out_dtype=BF16, name="mm_ssm_dwzx")
            dw_dtg = matmul(ddt, s["h"], ta=True, out_dtype=BF16, name="mm_ssm_dwdt")
            dw_dt = dw_dtg.reshape(G, LANES, D)[:, :hpg, :].reshape(heads, D)
            gw["ssm_w_in"][j] = jnp.concatenate([dw_zx, dw_dt], axis=0)
            gw["ssm_conv_w"][j] = dcw
            gw["ssm_conv_b"][j] = dcb[0]
            gw["ssm_dt_bias"][j] = dbias[:, 0, :hpg].reshape(heads)
            gw["ssm_a_log"][j] = dalog[:, 0, :hpg].reshape(heads)
            gw["ssm_d"][j] = dd[:, 0, :hpg].reshape(heads)
            gw["ssm_norm_gain"][j] = dng[0]
        dx, dgn = rms_bwd(s["x"], W["mix_norm"][i:i + 1], dh, dx1, name="rms_mix_bwd")
        gw["mix_norm"][i] = dgn[0]

    return loss, dx, gw, received


MESH = pl.DeviceIdType.MESH
HBM_SPEC = pl.BlockSpec(memory_space=pltpu.HBM)
VMEM_SPEC = pl.BlockSpec(memory_space=pltpu.VMEM)


def _my_position():
    return lax.axis_index("x"), lax.axis_index("y"), lax.axis_index("c")


def _flip(v, bit):
    return 1 - v if bit else v


def all_gather(shards, *, name):
    n = len(shards)

    def body(*refs):
        for phase in ("start", "forward", "finish"):
            _ag_phase(refs[:n], refs[n:2 * n], refs[2 * n:], phase)

    return pl.pallas_call(
        body, name=name, out_shape=[jax.ShapeDtypeStruct((N_DEV,) + s.shape, s.dtype) for s in shards],
        in_specs=[HBM_SPEC] * n, out_specs=[HBM_SPEC] * n, scratch_shapes=_copy_semaphores(n),
    )(*shards)


def _ag_phase(x_refs, out_refs, sems, phase):
    send_sems, recv_sems, local_sems = sems
    x, y, c = _my_position()
    me, sibling = (x, y, c), (x, y, 1 - c)
    chips = [(1 - x, y), (x, 1 - y), (1 - x, 1 - y)]
    for p, (x_ref, out_ref) in enumerate(zip(x_refs, out_refs)):
        def slot(px, py, pc):
            return out_ref.at[4 * px + 2 * py + pc]

        def copy(k, block, to, src=None):
            return pltpu.make_async_remote_copy(
                src_ref=slot(*block) if src is None else src, dst_ref=slot(*block),
                send_sem=send_sems.at[7 * p + k], recv_sem=recv_sems.at[7 * p + k], device_id=to, device_id_type=MESH)

        mine = pltpu.make_async_copy(x_ref, slot(*me), local_sems.at[p])
        first = [copy(0, me, sibling, src=x_ref)]
        first += [copy(1 + j, me, (*chip, c), src=x_ref) for j, chip in enumerate(chips)]
        passed = [copy(4 + j, (*chip, c), sibling) for j, chip in enumerate(chips)]
        if phase == "start":
            mine.start()
            for cp in first:
                cp.start()
        elif phase == "forward":
            for j, chip in enumerate(chips):
                copy(1 + j, (*chip, c), me).wait_recv()
                passed[j].start()
        else:
            copy(0, sibling, me).wait_recv()
            for j, chip in enumerate(chips):
                copy(4 + j, (*chip, 1 - c), me).wait_recv()
            for cp in first + passed:
                cp.wait_send()
            mine.wait()


def _copy_semaphores(n):
    return [pltpu.SemaphoreType.DMA((7 * n,)), pltpu.SemaphoreType.DMA((7 * n,)), pltpu.SemaphoreType.DMA((n,))]


def _rs_phase(g_refs, out_refs, sems, phase):
    send_sems, recv_sems, local_sems = sems
    x, y, c = _my_position()
    me = 4 * x + 2 * y + c
    copies = []
    for p, (g_ref, out_ref) in enumerate(zip(g_refs, out_refs)):
        copies.append(pltpu.make_async_copy(g_ref.at[me], out_ref.at[me], local_sems.at[p]))
        for k in range(1, N_DEV):
            px, py, pc = _flip(x, k & 4), _flip(y, k & 2), _flip(c, k & 1)
            copies.append(pltpu.make_async_remote_copy(
                src_ref=g_ref.at[4 * px + 2 * py + pc], dst_ref=out_ref.at[me],
                send_sem=send_sems.at[7 * p + k - 1], recv_sem=recv_sems.at[7 * p + k - 1],
                device_id=(px, py, pc), device_id_type=MESH))
    for cp in copies:
        if phase == "start":
            cp.start()
        else:
            cp.wait()


def exchange_for_reduce_scatter(gs, *, name):
    n = len(gs)

    def body(*refs):
        for phase in ("start", "finish"):
            _rs_phase(refs[:n], refs[n:2 * n], refs[2 * n:], phase)

    return pl.pallas_call(
        body, name=name, out_shape=[jax.ShapeDtypeStruct(g.shape, g.dtype) for g in gs],
        in_specs=[HBM_SPEC] * n, out_specs=[HBM_SPEC] * n, scratch_shapes=_copy_semaphores(n),
    )(*gs)


def sum_slots(recv, *, name):
    n, R, C = recv.shape
    tr = _pick(R, (512, 256, 128))

    def body(r_ref, o_ref):
        acc = r_ref[0].astype(F32)
        for s in range(1, n):
            acc = acc + r_ref[s].astype(F32)
        o_ref[...] = acc

    return pl.pallas_call(
        body, name=name, grid=(R // tr,), in_specs=[pl.BlockSpec((n, tr, C), lambda i: (0, i, 0))],
        out_specs=pl.BlockSpec((tr, C), lambda i: (i, 0)), out_shape=jax.ShapeDtypeStruct((R, C), F32),
        compiler_params=_params(("parallel",)),
    )(recv)


def all_reduce_small(vs, *, name):
    n = len(vs)

    def body(*refs):
        v_refs, o_refs, bufs = refs[:n], refs[n:2 * n], refs[2 * n:3 * n]
        send_sems, recv_sems = refs[3 * n:]
        x, y, c = _my_position()
        me = 4 * x + 2 * y + c
        copies = []
        for p, (v_ref, buf) in enumerate(zip(v_refs, bufs)):
            buf[me] = v_ref[...]
            for k in range(1, N_DEV):
                px, py, pc = _flip(x, k & 4), _flip(y, k & 2), _flip(c, k & 1)
                copies.append(pltpu.make_async_remote_copy(
                    src_ref=v_ref, dst_ref=buf.at[me], send_sem=send_sems.at[7 * p + k - 1],
                    recv_sem=recv_sems.at[7 * p + k - 1], device_id=(px, py, pc), device_id_type=MESH))
        for cp in copies:
            cp.start()
        for cp in copies:
            cp.wait()
        for o_ref, buf in zip(o_refs, bufs):
            acc = buf[0]
            for s in range(1, N_DEV):
                acc = acc + buf[s]
            o_ref[...] = acc

    return pl.pallas_call(
        body, name=name, out_shape=[jax.ShapeDtypeStruct(v.shape, F32) for v in vs],
        in_specs=[VMEM_SPEC] * n, out_specs=[VMEM_SPEC] * n,
        scratch_shapes=[pltpu.VMEM((N_DEV,) + v.shape, F32) for v in vs]
        + [pltpu.SemaphoreType.DMA((7 * n,)), pltpu.SemaphoreType.DMA((7 * n,))],
        compiler_params=pltpu.CompilerParams(vmem_limit_bytes=VMEM_LIMIT_BYTES),
    )(*vs)


def _pad_rows(a, mult):
    pad = (-a.shape[0]) % mult
    return jnp.pad(a, ((0, pad), (0, 0))) if pad else a


def _pack_small(arrays):
    flat = []
    for a in arrays:
        f = a.reshape(-1).astype(F32)
        flat.append(jnp.pad(f, (0, (-f.shape[0]) % LANES)))
    return _pad_rows(jnp.concatenate(flat).reshape(-1, LANES), 8)


def _unpack_small(packed, shapes):
    flat = packed.reshape(-1)
    out, r = [], 0
    for shp in shapes:
        n = math.prod(shp)
        out.append(flat[r:r + n].reshape(shp))
        r += n + (-n) % LANES
    return out


ARG_NAMES = ("x",) + WEIGHTS + ("loss_target",) + tuple("m_" + w for w in WEIGHTS) + tuple("v_" + w for w in WEIGHTS)


def kernel(x, mix_norm, ffn_norm, sb_w_qkv, sb_q_gain, sb_k_gain, sb_w_o, gm_w_in, gm_b_in, gm_v_gain, gm_w_s, gm_b_s, gm_w_out, ssm_w_in, ssm_conv_w, ssm_conv_b, ssm_dt_bias, ssm_a_log, ssm_d, ssm_norm_gain, ssm_w_out, ffn_w_gu, ffn_w_down, loss_target, m_mix_norm, m_ffn_norm, m_sb_w_qkv, m_sb_q_gain, m_sb_k_gain, m_sb_w_o, m_gm_w_in, m_gm_b_in, m_gm_v_gain, m_gm_w_s, m_gm_b_s, m_gm_w_out, m_ssm_w_in, m_ssm_conv_w, m_ssm_conv_b, m_ssm_dt_bias, m_ssm_a_log, m_ssm_d, m_ssm_norm_gain, m_ssm_w_out, m_ffn_w_gu, m_ffn_w_down, v_mix_norm, v_ffn_norm, v_sb_w_qkv, v_sb_q_gain, v_sb_k_gain, v_sb_w_o, v_gm_w_in, v_gm_b_in, v_gm_v_gain, v_gm_w_s, v_gm_b_s, v_gm_w_out, v_ssm_w_in, v_ssm_conv_w, v_ssm_conv_b, v_ssm_dt_bias, v_ssm_a_log, v_ssm_d, v_ssm_norm_gain, v_ssm_w_out, v_ffn_w_gu, v_ffn_w_down):
    given = dict(zip(ARG_NAMES, (x, mix_norm, ffn_norm, sb_w_qkv, sb_q_gain, sb_k_gain, sb_w_o, gm_w_in, gm_b_in, gm_v_gain, gm_w_s, gm_b_s, gm_w_out, ssm_w_in, ssm_conv_w, ssm_conv_b, ssm_dt_bias, ssm_a_log, ssm_d, ssm_norm_gain, ssm_w_out, ffn_w_gu, ffn_w_down, loss_target, m_mix_norm, m_ffn_norm, m_sb_w_qkv, m_sb_q_gain, m_sb_k_gain, m_sb_w_o, m_gm_w_in, m_gm_b_in, m_gm_v_gain, m_gm_w_s, m_gm_b_s, m_gm_w_out, m_ssm_w_in, m_ssm_conv_w, m_ssm_conv_b, m_ssm_dt_bias, m_ssm_a_log, m_ssm_d, m_ssm_norm_gain, m_ssm_w_out, m_ffn_w_gu, m_ffn_w_down, v_mix_norm, v_ffn_norm, v_sb_w_qkv, v_sb_q_gain, v_sb_k_gain, v_sb_w_o, v_gm_w_in, v_gm_b_in, v_gm_v_gain, v_gm_w_s, v_gm_b_s, v_gm_w_out, v_ssm_w_in, v_ssm_conv_w, v_ssm_conv_b, v_ssm_dt_bias, v_ssm_a_log, v_ssm_d, v_ssm_norm_gain, v_ssm_w_out, v_ffn_w_gu, v_ffn_w_down)))
    mx, my, mc = _my_position()
    me = 4 * mx + 2 * my + mc

    pieces = [(k, l) for k in BIG for l in range(given[k].shape[0])]
    early = [("sb_w_qkv", 0)]
    late_pieces = [p for p in pieces if p not in early]
    last = [("sb_w_qkv", 0)]
    main = [p for p in pieces if p not in last]

    def shards_of(ps):
        return [(given[k][l].T if k in COL_SHARDED else given[k][l]).astype(BF16) for k, l in ps]

    def piece_to_full(g, k):
        rows, cols = given[k].shape[1:]
        return g.reshape(N_DEV * cols, rows) if k in COL_SHARDED else g.reshape(N_DEV * rows, cols)

    def full_to_piece(full, k):
        return full.reshape(N_DEV, -1, PACK_COLS)

    def summed_to_shard(g, k):
        rows, cols = given[k].shape[1:]
        return g.reshape(cols, rows).T if k in COL_SHARDED else g.reshape(rows, cols)

    def contributions(gw, ps):
        return [full_to_piece(gw[k][l], k) for k, l in ps]

    sharded_small = [lax.bitcast_convert_type(given[k], BF16) for k in SMALL_SHARDED]
    tail = jnp.concatenate([a.reshape(-1) for a in sharded_small])
    tail = jnp.pad(tail, (0, (-tail.size) % PACK_COLS)).reshape(-1, PACK_COLS)

    W = {k: given[k] for k in SMALL if k not in SMALL_SHARDED}
    W.update({k: [None] * given[k].shape[0] for k in BIG})
    for (k, l), g in zip(early, all_gather(shards_of(early), name="all_gather_early")):
        W[k][l] = piece_to_full(g, k)

    class Late:
        shards = shards_of(late_pieces) + [tail]

        @staticmethod
        def fill(weights, gathered):
            for (k, l), g in zip(late_pieces, gathered):
                weights[k][l] = piece_to_full(g, k)
            tail_g = gathered[-1].reshape(N_DEV, -1)
            off = 0
            for k, a in zip(SMALL_SHARDED, sharded_small):
                g = lax.bitcast_convert_type(tail_g[:, off:off + a.size].reshape((N_DEV,) + a.shape), F32)
                weights[k] = jnp.moveaxis(g, 0, -2).reshape(g.shape[1:-1] + (N_DEV * g.shape[-1],))
                off += a.size

        @staticmethod
        def contributions(gw):
            return contributions(gw, main)

    loss, gx, gw, received_main = local_step(given["x"][0], given["loss_target"][0], W, late=Late)
    received_last = exchange_for_reduce_scatter(contributions(gw, last), name="reduce_scatter_last")

    grads_small = {k: jnp.stack([gw[k][l] for l in sorted(gw[k])], axis=0) for k in SMALL}
    packed_names = tuple(k for k in SMALL if k != "gm_w_s")
    small_shapes = [grads_small[k].shape for k in packed_names] + [(1, 1)]
    red_ws, red_rest = all_reduce_small(
        [grads_small["gm_w_s"].reshape(-1, LANES), _pack_small([grads_small[k] for k in packed_names] + [loss])],
        name="all_reduce_small")
    small_full = dict(zip(packed_names + ("loss",), _unpack_small(red_rest, small_shapes)))
    small_full["gm_w_s"] = red_ws.reshape(grads_small["gm_w_s"].shape)

    g_piece = {}
    for grp, received in ((main, received_main), (last, received_last)):
        for p, r in zip(grp, received):
            g_piece[p] = summed_to_shard(sum_slots(r, name="reduce_scatter_sum"), p[0])
    out_g, out_d, out_m, out_v = {}, {}, {}, {}
    for k in BIG:
        g = jnp.stack([g_piece[(k, l)] for l in range(given[k].shape[0])], axis=0)
        out_g[k] = g
        out_d[k], out_m[k], out_v[k] = adamw(given[k], g, given["m_" + k], given["v_" + k], name="adamw_" + k)

    gsmall = {}
    for k in SMALL:
        g = small_full[k]
        if k in SMALL_SHARDED:
            n = given[k].shape[-1]
            g = lax.dynamic_slice_in_dim(g, me * n, n, axis=g.ndim - 1)
        gsmall[k] = g
    local_shapes = [given[k].shape for k in packed_names]
    dsm, nmsm, nvsm = adamw(*[_pack_small([src[k] for k in packed_names]) for src in (
        given, gsmall, {k: given["m_" + k] for k in packed_names}, {k: given["v_" + k] for k in packed_names})],
        name="adamw_small")
    out_g.update(gsmall)
    for dst, src in ((out_d, dsm), (out_m, nmsm), (out_v, nvsm)):
        dst.update(zip(packed_names, _unpack_small(src, local_shapes)))
    ws_shape = given["gm_w_s"].shape
    out_d["gm_w_s"], out_m["gm_w_s"], out_v["gm_w_s"] = (a.reshape(ws_shape) for a in adamw(
        *[a.reshape((-1,) + ws_shape[-2:]) for a in (given["gm_w_s"], gsmall["gm_w_s"], given["m_gm_w_s"], given["v_gm_w_s"])],
        name="adamw_gm_w_s"))

    return (small_full["loss"].reshape(()), gx[None],
            *[out_g[k] for k in WEIGHTS], *[out_d[k] for k in WEIGHTS],
            *[out_m[k] for k in WEIGHTS], *[out_v[k] for k in WEIGHTS])
```

```python
import math

import jax
import jax.numpy as jnp
from jax import lax
from jax.experimental import pallas as pl
from jax.experimental.pallas import tpu as pltpu

F32 = jnp.float32
BF16 = jnp.bfloat16
EPS = 1e-6
N_DEV = 8
SB_HEAD_DIM = 64
SB_TILE = 256
SB_FWD_QUERY_BLOCKS = 4
SB_BWD_QUERY_BLOCKS = 2
SSM_STATE = 128
SSM_CONV = 4
ADAM_LR = 0.001
ADAM_B1 = 0.9
ADAM_B2 = 0.999
ADAM_EPS = 1e-08
ADAM_WD = 0.01
ADAM_STEP = 10
VMEM_LIMIT_BYTES = 56 * 1024 * 1024
MATMUL_VMEM_BUDGET = 40 * 1024 * 1024
LANES = 128
PACK_COLS = 1024

BIG = ("sb_w_qkv", "sb_w_o", "gm_w_in", "gm_w_out", "ssm_w_in", "ssm_w_out", "ffn_w_gu", "ffn_w_down")
COL_SHARDED = ("sb_w_qkv", "gm_w_in", "ssm_w_in", "ffn_w_gu")
SMALL = ("mix_norm", "ffn_norm", "sb_q_gain", "sb_k_gain", "gm_b_in", "gm_v_gain", "gm_w_s", "gm_b_s",
         "ssm_conv_w", "ssm_conv_b", "ssm_dt_bias", "ssm_a_log", "ssm_d", "ssm_norm_gain")
SMALL_SHARDED = ("ssm_conv_w", "ssm_conv_b", "ssm_norm_gain")
WEIGHTS = ("mix_norm", "ffn_norm", "sb_w_qkv", "sb_q_gain", "sb_k_gain", "sb_w_o", "gm_w_in", "gm_b_in",
           "gm_v_gain", "gm_w_s", "gm_b_s", "gm_w_out", "ssm_w_in", "ssm_conv_w", "ssm_conv_b", "ssm_dt_bias",
           "ssm_a_log", "ssm_d", "ssm_norm_gain", "ssm_w_out", "ffn_w_gu", "ffn_w_down")


def _params(semantics=None):
    return pltpu.CompilerParams(dimension_semantics=semantics, vmem_limit_bytes=VMEM_LIMIT_BYTES)


def _pick(n, prefs):
    for t in prefs:
        if t <= n and n % t == 0:
            return t
    return n


def _dot(a, b, ca=1, cb=0):
    return lax.dot_general(a, b, (((ca,), (cb,)), ((), ())), preferred_element_type=F32)


def _split3(v):
    h1 = v.astype(BF16)
    r1 = v - h1.astype(F32)
    h2 = r1.astype(BF16)
    h3 = (r1 - h2.astype(F32)).astype(BF16)
    return h1, h2, h3


def _dot_exact_left(mat01, v):
    h1, h2, h3 = _split3(v)
    return _dot(mat01, h1) + _dot(mat01, h2) + _dot(mat01, h3)


def _sum_all(v):
    return jnp.sum(jnp.sum(v, axis=0, keepdims=True), axis=1, keepdims=True)


def _sigmoid(v):
    return 1.0 / (1.0 + jnp.exp(-v))


def _softplus(v):
    return jnp.maximum(v, 0.0) + jnp.log(1.0 + jnp.exp(-jnp.abs(v)))


def _erf(v):
    a = jnp.abs(v)
    t = 1.0 / (1.0 + 0.3275911 * a)
    poly = t * (0.254829592 + t * (-0.284496736 + t * (1.421413741 + t * (-1.453152027 + t * 1.061405429))))
    e = 1.0 - poly * jnp.exp(-a * a)
    return jnp.where(v < 0, -e, e)


def _gelu_and_grad(v):
    cdf = 0.5 * (1.0 + _erf(v * (1.0 / math.sqrt(2.0))))
    pdf = jnp.exp(-0.5 * v * v) * (1.0 / math.sqrt(2.0 * math.pi))
    return v * cdf, cdf + v * pdf


def matmul(a, b, *, ta=False, tb=False, out_dtype=F32, residual=None, a_split=1, b_split=1, name):
    if a_split > 1 and ta:
        assert a.shape[0] == a_split
        K, M = a.shape[1], a_split * a.shape[2]
    elif a_split > 1:
        assert a.shape[0] == a_split
        M, K = a.shape[1], a_split * a.shape[2]
    elif ta:
        K, M = a.shape
    else:
        M, K = a.shape
    if b_split > 1:
        assert not tb and b.shape[0] == b_split
        Kb, N = b.shape[1], b_split * b.shape[2]
    elif tb:
        N, Kb = b.shape
    else:
        Kb, N = b.shape
    assert K == Kb, (a.shape, b.shape, ta, tb)
    has_res = residual is not None
    tm = _pick(M // a_split if ta else M, (1024, 1408, 768, 512, 256, 128))
    tn = _pick(N // b_split, (1024, 1408, 1536, 768, 512, 256, 128))

    def vmem_bytes(tk):
        tiles = tm * tk * a.dtype.itemsize + tk * tn * b.dtype.itemsize
        outs = tm * tn * jnp.dtype(out_dtype).itemsize + (tm * tn * 4 if has_res else 0)
        return 2 * tiles + 2 * outs + (tm * tn * 4 if tk < K else 0)

    kp = K if ta else K // a_split
    tk = next((t for t in (K, 2048, 1408, 1024, 512, 256) if t <= kp and kp % t == 0 and vmem_bytes(t) <= MATMUL_VMEM_BUDGET),
              _pick(kp, (128,)))
    nk = K // tk
    if a_split > 1 and ta:
        nib = M // a_split // tm
        a_spec = pl.BlockSpec((None, tk, tm), lambda i, j, k: (i // nib, k, i % nib))
    elif a_split > 1:
        nkb = kp // tk
        a_spec = pl.BlockSpec((None, tm, tk), lambda i, j, k: (k // nkb, i, k % nkb))
    else:
        a_spec = pl.BlockSpec((tk, tm), lambda i, j, k: (k, i)) if ta else pl.BlockSpec((tm, tk), lambda i, j, k: (i, k))
    if b_split > 1:
        njb = N // b_split // tn
        b_spec = pl.BlockSpec((None, tk, tn), lambda i, j, k: (j // njb, k, j % njb))
    else:
        b_spec = pl.BlockSpec((tn, tk), lambda i, j, k: (j, k)) if tb else pl.BlockSpec((tk, tn), lambda i, j, k: (k, j))
    o_spec = pl.BlockSpec((tm, tn), lambda i, j, k: (i, j))
    ca, cb = (0 if ta else 1), (1 if tb else 0)

    def body(*refs):
        a_ref, b_ref = refs[:2]
        r_ref = refs[2] if has_res else None
        o_ref = refs[3] if has_res else refs[2]

        def finish(r):
            if has_res:
                r = r + r_ref[...]
            o_ref[...] = r.astype(out_dtype)

        def part():
            return _dot(a_ref[...].astype(BF16), b_ref[...].astype(BF16), ca, cb)

        if nk == 1:
            finish(part())
            return
        acc = refs[-1]
        k = pl.program_id(2)

        @pl.when(k == 0)
        def _():
            acc[...] = part()

        @pl.when(jnp.logical_and(k > 0, k < nk - 1))
        def _():
            acc[...] += part()

        @pl.when(k == nk - 1)
        def _():
            finish(acc[...] + part())

    in_specs = [a_spec, b_spec] + ([o_spec] if has_res else [])
    args = (a, b) + ((residual,) if has_res else ())
    return pl.pallas_call(
        body, name=name, grid=(M // tm, N // tn, nk), in_specs=in_specs, out_specs=o_spec,
        out_shape=jax.ShapeDtypeStruct((M, N), out_dtype),
        scratch_shapes=[pltpu.VMEM((tm, tn), F32)] if nk > 1 else [],
        compiler_params=_params(("parallel", "parallel", "arbitrary")),
    )(*args)


def rms_fwd(x, gain, *, name):
    S, D = x.shape
    tr = _pick(S, (512, 256, 128))

    def body(x_ref, g_ref, o_ref):
        xv = x_ref[...]
        r = lax.rsqrt(jnp.mean(xv * xv, axis=1, keepdims=True) + EPS)
        o_ref[...] = (xv * r * g_ref[...]).astype(BF16)

    return pl.pallas_call(
        body, name=name, grid=(S // tr,),
        in_specs=[pl.BlockSpec((tr, D), lambda i: (i, 0)), pl.BlockSpec((1, D), lambda i: (0, 0))],
        out_specs=pl.BlockSpec((tr, D), lambda i: (i, 0)), out_shape=jax.ShapeDtypeStruct((S, D), BF16),
        compiler_params=_params(("parallel",)),
    )(x, gain)


def rms_bwd(x, gain, dh, dres, *, name):
    S, D = x.shape
    tr = _pick(S, (512, 256, 128))

    def body(x_ref, g_ref, dh_ref, dr_ref, dx_ref, dg_ref):
        @pl.when(pl.program_id(0) == 0)
        def _():
            dg_ref[...] = jnp.zeros_like(dg_ref)

        xv = x_ref[...]
        dhv = dh_ref[...]
        r = lax.rsqrt(jnp.mean(xv * xv, axis=1, keepdims=True) + EPS)
        xhat = xv * r
        t = dhv * g_ref[...]
        dx_ref[...] = dr_ref[...] + r * (t - xhat * jnp.mean(xhat * t, axis=1, keepdims=True))
        dg_ref[...] += jnp.sum(dhv * xhat, axis=0, keepdims=True)

    row = pl.BlockSpec((tr, D), lambda i: (i, 0))
    vec = pl.BlockSpec((1, D), lambda i: (0, 0))
    return pl.pallas_call(
        body, name=name, grid=(S // tr,), in_specs=[row, vec, row, row], out_specs=[row, vec],
        out_shape=[jax.ShapeDtypeStruct((S, D), F32), jax.ShapeDtypeStruct((1, D), F32)],
        compiler_params=_params(("arbitrary",)),
    )(x, gain, dh, dres)


def ffn_up_fwd(h, w_gu_t, *, name):
    S, K = h.shape
    F = w_gu_t.shape[0] // 2
    tm = _pick(S, (512, 256, 128))
    tn = _pick(F, (1408, 1024, 768, 512, 256, 128))
    nj = F // tn

    def body(h_ref, wg_ref, wu_ref, gu_ref, a_ref):
        hv = h_ref[...]
        g = _dot(hv, wg_ref[...], 1, 1)
        u = _dot(hv, wu_ref[...], 1, 1)
        gu_ref[0] = g
        gu_ref[1] = u
        a_ref[...] = (g * _sigmoid(g) * u).astype(BF16)

    return pl.pallas_call(
        body, name=name, grid=(nj, S // tm),
        in_specs=[pl.BlockSpec((tm, K), lambda j, i: (i, 0)), pl.BlockSpec((tn, K), lambda j, i: (j, 0)),
                  pl.BlockSpec((tn, K), lambda j, i: (nj + j, 0))],
        out_specs=[pl.BlockSpec((2, tm, tn), lambda j, i: (0, i, j)), pl.BlockSpec((tm, tn), lambda j, i: (i, j))],
        out_shape=[jax.ShapeDtypeStruct((2, S, F), F32), jax.ShapeDtypeStruct((S, F), BF16)],
        compiler_params=_params(("parallel", "parallel")),
    )(h, w_gu_t, w_gu_t)


def ffn_up_bwd(dy, w_down, gu, *, name):
    S, D = dy.shape
    F = w_down.shape[0]
    tm = _pick(S, (512, 256, 128))
    tn = _pick(F, (1408, 1024, 768, 512, 256, 128))

    def body(dy_ref, wd_ref, gu_ref, o_ref):
        da = _dot(dy_ref[...].astype(BF16), wd_ref[...], 1, 1)
        g = gu_ref[0]
        u = gu_ref[1]
        s = _sigmoid(g)
        o_ref[0] = (da * u * (s * (1.0 + g * (1.0 - s)))).astype(BF16)
        o_ref[1] = (da * g * s).astype(BF16)

    pair = pl.BlockSpec((2, tm, tn), lambda j, i: (0, i, j))
    return pl.pallas_call(
        body, name=name, grid=(F // tn, S // tm),
        in_specs=[pl.BlockSpec((tm, D), lambda j, i: (i, 0)), pl.BlockSpec((tn, D), lambda j, i: (j, 0)), pair],
        out_specs=pair, out_shape=jax.ShapeDtypeStruct((2, S, F), BF16),
        compiler_params=_params(("parallel", "parallel")),
    )(dy, w_down, gu)


def loss_head(y, target, *, name):
    S, D = y.shape
    tr = _pick(S, (512, 256, 128))

    def body(y_ref, t_ref, dy_ref, l_ref):
        @pl.when(pl.program_id(0) == 0)
        def _():
            l_ref[...] = jnp.zeros_like(l_ref)

        err = y_ref[...] - t_ref[...]
        dy_ref[...] = err * (1.0 / D)
        l_ref[...] += jnp.sum(0.5 * jnp.mean(err * err, axis=1, keepdims=True), axis=0, keepdims=True)

    row = pl.BlockSpec((tr, D), lambda i: (i, 0))
    one = pl.BlockSpec((1, 1), lambda i: (0, 0))
    dy, l = pl.pallas_call(
        body, name=name, grid=(S // tr,), in_specs=[row, row], out_specs=[row, one],
        out_shape=[jax.ShapeDtypeStruct((S, D), F32), jax.ShapeDtypeStruct((1, 1), F32)],
        compiler_params=_params(("arbitrary",)),
    )(y, target)
    return dy, l


def adamw(w, g, m, v, *, name):
    R, C = w.shape[-2:]
    tr = _pick(R, (512, 256, 128, 64, 32, 16, 8))

    def body(w_ref, g_ref, m_ref, v_ref, d_ref, mo_ref, vo_ref):
        gv = g_ref[...]
        mn = ADAM_B1 * m_ref[...] + (1.0 - ADAM_B1) * gv
        vn = ADAM_B2 * v_ref[...] + (1.0 - ADAM_B2) * jnp.square(gv)
        m_hat = mn / (1.0 - ADAM_B1 ** ADAM_STEP)
        v_hat = vn / (1.0 - ADAM_B2 ** ADAM_STEP)
        d_ref[...] = -ADAM_LR * (m_hat / (jnp.sqrt(v_hat) + ADAM_EPS) + ADAM_WD * w_ref[...])
        mo_ref[...] = mn
        vo_ref[...] = vn

    tc = C if tr < R or C % LANES else _pick(C, (256, 128))
    if w.ndim == 3:
        grid = (w.shape[0], R // tr, C // tc)
        blk = pl.BlockSpec((None, tr, tc), lambda l, i, j: (l, i, j))
    else:
        grid = (R // tr, C // tc)
        blk = pl.BlockSpec((tr, tc), lambda i, j: (i, j))
    sds = jax.ShapeDtypeStruct(w.shape, F32)
    return pl.pallas_call(
        body, name=name, grid=grid, in_specs=[blk] * 4, out_specs=[blk] * 3, out_shape=[sds] * 3,
        compiler_params=_params(("parallel",) * len(grid)),
    )(w, g, m, v)


def _tri(n, kind):
    r = lax.broadcasted_iota(jnp.int32, (n, n), 0)
    c = lax.broadcasted_iota(jnp.int32, (n, n), 1)
    if kind == "row_gt_col":
        return (r > c).astype(BF16)
    if kind == "row_ge_col":
        return (r >= c).astype(BF16)
    if kind == "row_le_col":
        return (r <= c).astype(BF16)
    raise ValueError(kind)


def _sb_tile(qs, kj, r_carry, u_strict, masked):
    z = _dot(qs, kj, 1, 1)
    lb = jnp.minimum(z, 0.0) - jnp.log(1.0 + jnp.exp(-jnp.abs(z)))
    l1m = lb - z
    keep = None
    if masked:
        tq, tk = z.shape
        keep = lax.broadcasted_iota(jnp.int32, (tq, tk), 1) < lax.broadcasted_iota(jnp.int32, (tq, tk), 0)
        l1m = jnp.where(keep, l1m, 0.0)
    w = jnp.exp(lb + _dot(l1m.astype(BF16), u_strict) + r_carry)
    if masked:
        w = jnp.where(keep, w, 0.0)
    return lb, l1m, w, keep


def _sb_prep(T, nb, hd, refs_in, gains, scratch):
    q_scale = 1.0 / math.sqrt(hd)
    assert math.log2(q_scale) == round(math.log2(q_scale))

    def prep(i, _):
        rows = pl.ds(pl.multiple_of(i * T, T), T)
        for hh in range(2):
            sl = slice(hd * hh, hd * hh + hd)
            for n, (src, dst) in enumerate(zip(refs_in, scratch)):
                v = src[rows, sl]
                if n < 2:
                    v = v * lax.rsqrt(jnp.mean(v * v, axis=1, keepdims=True) + EPS) * gains[n][...]
                if n == 0:
                    v = v * q_scale
                dst[hh, rows, :] = v.astype(BF16)
        return 0

    lax.fori_loop(0, nb, prep, 0)


def _sb_chains(m, T, nq):
    rows = [pl.ds(pl.multiple_of((nq * m + qb) * T, T), T) for qb in range(nq)]
    return rows, [(hh, qb) for qb in range(nq) for hh in range(2)]


def _sb_sweep(tile, carry, chains, m, nq):
    for kk in reversed(range(nq)):
        carry = tile(nq * m + kk, carry, [(ch, ch[1] == kk) for ch in chains if ch[1] >= kk])
    return lax.fori_loop(0, nq * m, lambda jj, c: tile(nq * m - 1 - jj, c, [(ch, False) for ch in chains]), carry)


def sb_attn_fwd(qkv, q_gain, k_gain, *, name, gather=None):
    S, D3 = qkv.shape
    D = D3 // 3
    npairs = D // LANES
    hd = SB_HEAD_DIM
    T = min(SB_TILE, S)
    nb = S // T
    nq = SB_FWD_QUERY_BLOCKS
    assert nb % nq == 0

    def body(*refs):
        if gather is None:
            q_ref, k_ref, v_ref, qg_ref, kg_ref, us_ref, o_ref, qn_s, kn_s, vb_s = refs
        else:
            ng = len(gather)
            q_ref, k_ref, v_ref, qg_ref, kg_ref, us_ref = refs[:6]
            o_ref = refs[6 + ng]
            qn_s, kn_s, vb_s = refs[7 + 2 * ng:10 + 2 * ng]
            comm = (refs[6:6 + ng], refs[7 + ng:7 + 2 * ng], refs[10 + 2 * ng:])
            step = pl.program_id(0)
            pl.when(step == 0)(lambda: _ag_phase(*comm, "start"))
            pl.when(step == npairs - 1)(lambda: _ag_phase(*comm, "forward"))
        us = us_ref[...]
        _sb_prep(T, nb, hd, (q_ref, k_ref, v_ref), (qg_ref, kg_ref), (qn_s, kn_s, vb_s))

        def superblock(m, _):
            rows_q, chains = _sb_chains(m, T, nq)
            qs = {ch: qn_s[ch[0], rows_q[ch[1]], :] for ch in chains}

            def tile(j, carry, which):
                rows_j = pl.ds(pl.multiple_of(j * T, T), T)
                new = dict(carry)
                for ch, masked in which:
                    acc, rc = carry[ch]
                    _, l1m, w, _ = _sb_tile(qs[ch], kn_s[ch[0], rows_j, :], rc, us, masked)
                    new[ch] = (acc + _dot(w.astype(BF16), vb_s[ch[0], rows_j, :]),
                               rc + jnp.sum(l1m, axis=1, keepdims=True))
                return new

            carry = {ch: (jnp.zeros((T, hd), F32), jnp.zeros((T, 1), F32)) for ch in chains}
            carry = _sb_sweep(tile, carry, chains, m, nq)
            for qb in range(nq):
                o_ref[rows_q[qb], :] = jnp.concatenate([carry[(0, qb)][0], carry[(1, qb)][0]], axis=1)
            return 0

        lax.fori_loop(0, nb // nq, superblock, 0)
        if gather is not None:
            pl.when(step == npairs - 1)(lambda: _ag_phase(*comm, "finish"))

    col = lambda off: pl.BlockSpec((S, LANES), lambda p, off=off: (0, off + p))
    gain = pl.BlockSpec((1, hd), lambda p: (0, 0))
    in_specs = [col(0), col(npairs), col(2 * npairs), gain, gain, pl.BlockSpec((T, T), lambda p: (0, 0))]
    out_specs = [pl.BlockSpec((S, LANES), lambda p: (0, p))]
    out_shape = [jax.ShapeDtypeStruct((S, D), F32)]
    scratch = [pltpu.VMEM((2, S, hd), BF16)] * 3
    args = [qkv, qkv, qkv, q_gain, k_gain, _tri(T, "row_gt_col")]
    if gather is not None:
        in_specs += [HBM_SPEC] * len(gather)
        out_specs += [HBM_SPEC] * len(gather)
        out_shape += [jax.ShapeDtypeStruct((N_DEV,) + s.shape, s.dtype) for s in gather]
        scratch += _copy_semaphores(len(gather))
        args += list(gather)
    out = pl.pallas_call(
        body, name=name, grid=(npairs,), in_specs=in_specs, out_specs=out_specs, out_shape=out_shape,
        scratch_shapes=scratch, compiler_params=_params(("arbitrary",)),
    )(*args)
    return out[0] if gather is None else (out[0], list(out[1:]))


def sb_attn_bwd(qkv, o, do, q_gain, k_gain, *, name, scatter=None):
    S, D3 = qkv.shape
    D = D3 // 3
    npairs = D // LANES
    hd = SB_HEAD_DIM
    T = min(SB_TILE, S)
    nb = S // T
    nq = SB_BWD_QUERY_BLOCKS
    assert nb % nq == 0
    scale = 1.0 / math.sqrt(hd)

    def body(*refs):
        if scatter is None:
            (q_ref, k_ref, v_ref, o_ref, do_ref, qg_ref, kg_ref, us_ref,
             dqkv_ref, dg_ref, qn_s, kn_s, vb_s, dob_s) = refs
        else:
            ns = len(scatter)
            q_ref, k_ref, v_ref, o_ref, do_ref, qg_ref, kg_ref, us_ref = refs[:8]
            rs_in = refs[8:8 + ns]
            dqkv_ref, dg_ref = refs[8 + ns:10 + ns]
            rs_out = refs[10 + ns:10 + 2 * ns]
            qn_s, kn_s, vb_s, dob_s = refs[10 + 2 * ns:14 + 2 * ns]
            rs_sems = refs[14 + 2 * ns:]
            pl.when(pl.program_id(0) == 0)(lambda: _rs_phase(rs_in, rs_out, rs_sems, "start"))
        dq_ref, dk_ref, dv_ref = dqkv_ref.at[0], dqkv_ref.at[1], dqkv_ref.at[2]

        @pl.when(pl.program_id(0) == 0)
        def _():
            dg_ref[...] = jnp.zeros_like(dg_ref)

        us = us_ref[...]
        u_prefix = (1.0 - us.astype(F32)).astype(BF16)
        _sb_prep(T, nb, hd, (q_ref, k_ref, v_ref, do_ref), (qg_ref, kg_ref), (qn_s, kn_s, vb_s, dob_s))
        dk_ref[...] = jnp.zeros_like(dk_ref)
        dv_ref[...] = jnp.zeros_like(dv_ref)

        def superblock(m, _):
            rows_q, chains = _sb_chains(m, T, nq)
            qs = {ch: qn_s[ch[0], rows_q[ch[1]], :] for ch in chains}
            doi ={ch: dob_s[ch[0], rows_q[ch[1]], :] for ch in chains}
            dt_total = {ch: jnp.sum(doi[ch].astype(F32) * o_ref[rows_q[ch[1]], hd * ch[0]:hd * ch[0] + hd],
                                    axis=1, keepdims=True) for ch in chains}

            def tile(j, carry, which):
                rows_j = pl.ds(pl.multiple_of(j * T, T), T)
                new = dict(carry)
                dk_part, dv_part = {}, {}
                for ch, masked in which:
                    hh = ch[0]
                    dq_acc, rc, gc = carry[ch]
                    kj = kn_s[hh, rows_j, :]
                    lb, l1m, w, keep = _sb_tile(qs[ch], kj, rc, us, masked)
                    wb = w.astype(BF16)
                    g = _dot(doi[ch], vb_s[hh, rows_j, :], 1, 1) * wb.astype(F32)
                    g_row = jnp.sum(g, axis=1, keepdims=True)
                    g_upto = (dt_total[ch] - gc - g_row) + _dot(g.astype(BF16), u_prefix)
                    dz = g - g_upto * jnp.exp(lb)
                    if masked:
                        dz = jnp.where(keep, dz, 0.0)
                    dzb = dz.astype(BF16)
                    dv_part[hh] = dv_part.get(hh, 0.0) + _dot(wb, doi[ch], 0, 0)
                    dk_part[hh] = dk_part.get(hh, 0.0) + _dot(dzb, qs[ch], 0, 0)
                    new[ch] = (dq_acc + _dot(dzb, kj), rc + jnp.sum(l1m, axis=1, keepdims=True),
                               gc + g_row)
                dv_ref[rows_j, :] += jnp.concatenate([dv_part[0], dv_part[1]], axis=1)
                dk_ref[rows_j, :] += jnp.concatenate([dk_part[0], dk_part[1]], axis=1)
                return new

            zero1 = jnp.zeros((T, 1), F32)
            carry = {ch: (jnp.zeros((T, hd), F32), zero1, zero1) for ch in chains}
            carry = _sb_sweep(tile, carry, chains, m, nq)
            for qb in range(nq):
                dq_ref[rows_q[qb], :] = jnp.concatenate([carry[(0, qb)][0], carry[(1, qb)][0]], axis=1) * scale
            return 0

        lax.fori_loop(0, nb // nq, superblock, 0)

        def finish(i, carry):
            rows = pl.ds(pl.multiple_of(i * T, T), T)
            new = []
            for hh in range(2):
                sl = slice(hd * hh, hd * hh + hd)
                outs = []
                for raw_ref, gain_ref, dn in ((q_ref, qg_ref, dq_ref[rows, sl]), (k_ref, kg_ref, dk_ref[rows, sl])):
                    raw = raw_ref[rows, sl]
                    r = lax.rsqrt(jnp.mean(raw * raw, axis=1, keepdims=True) + EPS)
                    hat = raw * r
                    t = dn * gain_ref[...]
                    outs.append((r * (t - hat * jnp.mean(hat * t, axis=1, keepdims=True)),
                                 jnp.sum(dn * hat, axis=0, keepdims=True)))
                dq_ref[rows, sl] = outs[0][0]
                dk_ref[rows, sl] = outs[1][0]
                new.append((carry[hh][0] + outs[0][1], carry[hh][1] + outs[1][1]))
            return tuple(new)

        zg = (jnp.zeros((1, hd), F32), jnp.zeros((1, hd), F32))
        tot = lax.fori_loop(0, nb, finish, (zg, zg))
        dg_ref[0:1, 0:hd] += tot[0][0] + tot[1][0]
        dg_ref[1:2, 0:hd] += tot[0][1] + tot[1][1]
        if scatter is not None:
            pl.when(pl.program_id(0) == npairs - 1)(lambda: _rs_phase(rs_in, rs_out, rs_sems, "finish"))

    col = lambda off: pl.BlockSpec((S, LANES), lambda p, off=off: (0, off + p))
    gain = pl.BlockSpec((1, hd), lambda p: (0, 0))
    tri = pl.BlockSpec((T, T), lambda p: (0, 0))
    pair = pl.BlockSpec((S, LANES), lambda p: (0, p))
    in_specs = [col(0), col(npairs), col(2 * npairs), pair, pair, gain, gain, tri]
    out_specs = [pl.BlockSpec((3, S, LANES), lambda p: (0, 0, p)), pl.BlockSpec((8, LANES), lambda p: (0, 0))]
    out_shape = [jax.ShapeDtypeStruct((3, S, D), F32), jax.ShapeDtypeStruct((8, LANES), F32)]
    scratch = [pltpu.VMEM((2, S, hd), BF16)] * 4
    args = [qkv, qkv, qkv, o, do, q_gain, k_gain, _tri(T, "row_gt_col")]
    if scatter is not None:
        in_specs += [HBM_SPEC] * len(scatter)
        out_specs += [HBM_SPEC] * len(scatter)
        out_shape += [jax.ShapeDtypeStruct(g.shape, g.dtype) for g in scatter]
        scratch += _copy_semaphores(len(scatter))
        args += list(scatter)
    out = pl.pallas_call(
        body, name=name, grid=(npairs,), in_specs=in_specs, out_specs=out_specs, out_shape=out_shape,
        scratch_shapes=scratch, compiler_params=_params(("arbitrary",)),
    )(*args)
    res = (out[0], out[1][0:1, :hd], out[1][1:2, :hd])
    return res if scatter is None else res + (list(out[2:]),)


def gmlp_fwd(zzpre, b_in, v_gain, wc, bsf, *, name):
    S, H2 = zzpre.shape
    H = H2 // 2
    G, T, _ = wc.shape
    gd = H // G

    def body(z_ref, b_ref, vg_ref, wc_ref, bs_ref, p_ref):
        zz, _ = _gelu_and_grad(z_ref[...] + b_ref[...])
        u = zz[:, :H]
        v = zz[:, H:]
        vn = v * lax.rsqrt(jnp.mean(v * v, axis=1, keepdims=True) + EPS) * vg_ref[...]
        for g in range(G):
            gs = slice(g * gd, (g + 1) * gd)
            mixed = _dot(wc_ref[g], vn[:, gs].astype(BF16)) + bs_ref[g]
            p_ref[:, gs] = (u[:, gs] * mixed).astype(BF16)

    full3 = lambda shp: pl.BlockSpec(shp, lambda c: (0, 0, 0))
    return pl.pallas_call(
        body, name=name, grid=(S // T,),
        in_specs=[pl.BlockSpec((T, H2), lambda c: (c, 0)), pl.BlockSpec((1, H2), lambda c: (0, 0)),
                  pl.BlockSpec((1, H), lambda c: (0, 0)), full3((G, T, T)), full3((G, T, gd))],
        out_specs=pl.BlockSpec((T, H), lambda c: (c, 0)), out_shape=jax.ShapeDtypeStruct((S, H), BF16),
        compiler_params=_params(("parallel",)),
    )(zzpre, b_in, v_gain, wc, bsf)


def gmlp_bwd(zzpre, b_in, v_gain, wc, bsf, dp, *, name):
    S, H2 = zzpre.shape
    H = H2 // 2
    G, T, _ = wc.shape
    gd = H // G
    assert G <= LANES

    def body(z_ref, b_ref, vg_ref, wc_ref, bs_ref, dp_ref, dzz_ref, db_ref, dvg_ref, dws_ref, dbs_ref):
        @pl.when(pl.program_id(0) == 0)
        def _():
            db_ref[...] = jnp.zeros_like(db_ref)
            dvg_ref[...] = jnp.zeros_like(dvg_ref)
            dws_ref[...] = jnp.zeros_like(dws_ref)
            dbs_ref[...] = jnp.zeros_like(dbs_ref)

        zz, gp = _gelu_and_grad(z_ref[...] + b_ref[...])
        u = zz[:, :H]
        v = zz[:, H:]
        r = lax.rsqrt(jnp.mean(v * v, axis=1, keepdims=True) + EPS)
        vhat = v * r
        vg = vg_ref[...]
        vn = vhat * vg
        dpv = dp_ref[...]
        tril = lax.broadcasted_iota(jnp.int32, (T, T), 1) <= lax.broadcasted_iota(jnp.int32, (T, T), 0)
        lane = lax.broadcasted_iota(jnp.int32, (T, LANES), 1)
        dbs = jnp.zeros((T, LANES), F32)
        du_parts, dvn_parts = [], []
        for g in range(G):
            gs = slice(g * gd, (g + 1) * gd)
            vng = vn[:, gs].astype(BF16)
            wcg = wc_ref[g]
            mixed = _dot(wcg, vng) + bs_ref[g]
            dpg = dpv[:, gs]
            du_parts.append(dpg * mixed)
            dmx = dpg * u[:, gs]
            dmxb = dmx.astype(BF16)
            dvn_parts.append(_dot(wcg, dmxb, 0, 0))
            dws_ref[g] += jnp.where(tril, _dot(dmxb, vng, 1, 1), 0.0)
            dbs = dbs + jnp.where(lane == g, jnp.sum(dmx, axis=1, keepdims=True), 0.0)
        dbs_ref[...] += dbs
        du = jnp.concatenate(du_parts, axis=1)
        dvn = jnp.concatenate(dvn_parts, axis=1)
        dvg_ref[...] += jnp.sum(dvn * vhat, axis=0, keepdims=True)
        t = dvn * vg
        dv = r * (t - vhat * jnp.mean(vhat * t, axis=1, keepdims=True))
        dzu = du * gp[:, :H]
        dzv = dv * gp[:, H:]
        dzz_ref[:, :H] = dzu.astype(BF16)
        dzz_ref[:, H:] = dzv.astype(BF16)
        db_ref[:, :H] += jnp.sum(dzu, axis=0, keepdims=True)
        db_ref[:, H:] += jnp.sum(dzv, axis=0, keepdims=True)

    full3 = lambda shp: pl.BlockSpec(shp, lambda c: (0, 0, 0))
    vec = lambda n: pl.BlockSpec((1, n), lambda c: (0, 0))
    return pl.pallas_call(
        body, name=name, grid=(S // T,),
        in_specs=[pl.BlockSpec((T, H2), lambda c: (c, 0)), vec(H2), vec(H), full3((G, T, T)), full3((G, T, gd)),
                  pl.BlockSpec((T, H), lambda c: (c, 0))],
        out_specs=[pl.BlockSpec((T, H2), lambda c: (c, 0)), vec(H2), vec(H), full3((G, T, T)),
                   pl.BlockSpec((T, LANES), lambda c: (0, 0))],
        out_shape=[jax.ShapeDtypeStruct((S, H2), BF16), jax.ShapeDtypeStruct((1, H2), F32),
                   jax.ShapeDtypeStruct((1, H), F32), jax.ShapeDtypeStruct((G, T, T), F32),
                   jax.ShapeDtypeStruct((T, LANES), F32)],
        compiler_params=_params(("arbitrary",)),
    )(zzpre, b_in, v_gain, wc, bsf, dp)


def _shift_rows(v, k, n_rows):
    if k == 0:
        return v
    rolled = pltpu.roll(v, k % n_rows, 0)
    row = lax.broadcasted_iota(jnp.int32, v.shape, 0)
    keep = (row >= k) if k > 0 else (row < n_rows + k)
    return jnp.where(keep, rolled, 0.0)


def conv_fwd(zx, conv_w, conv_b, col0, *, name):
    S = zx.shape[0]
    C = conv_w.shape[1]
    tc = _pick(C, (256, 128))
    off = col0 // tc
    assert col0 % tc == 0

    def body(x_ref, w_ref, b_ref, o_ref):
        xv = x_ref[...]
        acc = b_ref[...] + w_ref[SSM_CONV - 1:SSM_CONV, :] * xv
        for k in range(SSM_CONV - 1):
            acc = acc + w_ref[k:k + 1, :] * _shift_rows(xv, SSM_CONV - 1 - k, S)
        o_ref[...] = acc * _sigmoid(acc)

    return pl.pallas_call(
        body, name=name, grid=(C // tc,),
        in_specs=[pl.BlockSpec((S, tc), lambda j: (0, off + j)), pl.BlockSpec((SSM_CONV, tc), lambda j: (0, j)),
                  pl.BlockSpec((1, tc), lambda j: (0, j))],
        out_specs=pl.BlockSpec((S, tc), lambda j: (0, j)), out_shape=jax.ShapeDtypeStruct((S, C), F32),
        compiler_params=_params(("parallel",)),
    )(zx, conv_w, conv_b)


def conv_bwd(zx, conv_w, conv_b, col0, douts, *, name):
    S = zx.shape[0]
    C = conv_w.shape[1]
    tc = LANES
    off = col0 // tc
    counts = [d.shape[1] // tc for d in douts]
    starts = [sum(counts[:p]) for p in range(len(douts))]
    assert sum(counts) * tc == C and all(d.shape[1] % tc == 0 for d in douts)

    def body(x_ref, w_ref, b_ref, *rest):
        do_refs, (dx_ref, dw_ref, db_ref) = rest[:len(douts)], rest[len(douts):]
        j = pl.program_id(0)
        dov = do_refs[-1][...]
        for p in reversed(range(len(douts) - 1)):
            dov = jnp.where(j < starts[p + 1], do_refs[p][...], dov)
        xv = x_ref[...]
        shifted = [_shift_rows(xv, SSM_CONV - 1 - k, S) for k in range(SSM_CONV)]
        acc = b_ref[...]
        for k in range(SSM_CONV):
            acc = acc + w_ref[k:k + 1, :] * shifted[k]
        s = _sigmoid(acc)
        dacc = dov * (s * (1.0 + acc * (1.0 - s)))
        db_ref[...] = jnp.sum(dacc, axis=0, keepdims=True)
        dx = jnp.zeros_like(xv)
        for k in range(SSM_CONV):
            dw_ref[k:k + 1, :] = jnp.sum(dacc * shifted[k], axis=0, keepdims=True)
            dx = dx + w_ref[k:k + 1, :] * _shift_rows(dacc, -(SSM_CONV - 1 - k), S)
        dx_ref[...] = dx

    slab = pl.BlockSpec((S, tc), lambda j: (0, j))
    piece_specs = [pl.BlockSpec((S, tc), lambda j, a=starts[p], n=counts[p]: (0, jnp.clip(j - a, 0, n - 1)))
                   for p in range(len(douts))]
    return pl.pallas_call(
        body, name=name, grid=(C // tc,),
        in_specs=[pl.BlockSpec((S, tc), lambda j: (0, off + j)), pl.BlockSpec((SSM_CONV, tc), lambda j: (0, j)),
                  pl.BlockSpec((1, tc), lambda j: (0, j))] + piece_specs,
        out_specs=[slab, pl.BlockSpec((SSM_CONV, tc), lambda j: (0, j)), pl.BlockSpec((1, tc), lambda j: (0, j))],
        out_shape=[jax.ShapeDtypeStruct((S, C), F32), jax.ShapeDtypeStruct((SSM_CONV, C), F32),
                   jax.ShapeDtypeStruct((1, C), F32)],
        compiler_params=_params(("arbitrary",)),
    )(zx, conv_w, conv_b, *douts)


def _ssd_chunk_terms(dtraw, bias, a_log, tl):
    dt = _softplus(dtraw + bias)
    a_neg = -jnp.exp(a_log)
    ac = _dot_exact_left(tl, dt * a_neg)
    ac_last = ac[ac.shape[0] - 1:, :]
    return dt, a_neg, ac, ac.T, jnp.exp(ac), jnp.exp(ac_last - ac), jnp.exp(ac_last)


def _ssd_specs(S, L, G, hpg, pd, inner):
    gw = hpg * pd
    n = SSM_STATE
    xb = inner // n

    def mk(cidx):
        return dict(
            x=pl.BlockSpec((L, gw), lambda g, c: (cidx(c), g)),
            b=pl.BlockSpec((L, n), lambda g, c: (cidx(c), xb + g)),
            c=pl.BlockSpec((L, n), lambda g, c: (cidx(c), xb + G + g)),
            z=pl.BlockSpec((L, gw), lambda g, c: (cidx(c), g)),
            dt=pl.BlockSpec((L, LANES), lambda g, c: (cidx(c), g)),
            gvec=pl.BlockSpec((1, 1, LANES), lambda g, c: (g, 0, 0)),
            chan=pl.BlockSpec((1, gw), lambda g, c: (0, g)),
            tri=pl.BlockSpec((L, L), lambda g, c: (0, 0)),
            hp=pl.BlockSpec((1, 1, gw, n), lambda g, c: (g, cidx(c), 0, 0)),
            bc=pl.BlockSpec((L, n), lambda g, c: (cidx(c), g)),
        )
    return mk


def ssd_fwd(xbc, zx, dtg, bias_g, alog_g, d_chan, ngain, L, G, *, name):
    S = xbc.shape[0]
    n = SSM_STATE
    inner = xbc.shape[1] - 2 * G * n
    gw = inner // G
    pd = SB_HEAD_DIM
    hpg = gw // pd
    nc = S // L
    sp = _ssd_specs(S, L, G, hpg, pd, inner)(lambda c: c)

    def body(x_ref, b_ref, c_ref, z_ref, dt_ref, bias_ref, alog_ref, d_ref, ng_ref, tl_ref,
             yn_ref, y_ref, hp_ref, state):
        @pl.when(pl.program_id(1) == 0)
        def _():
            state[...] = jnp.zeros_like(state)

        dt, _, ac, act, ea, dte, cd = _ssd_chunk_terms(dt_ref[...], bias_ref[0], alog_ref[0], tl_ref[...])
        xv = x_ref[...]
        bm = b_ref[...].astype(BF16)
        cm = c_ref[...].astype(BF16)
        cb = _dot(cm, bm, 1, 1)
        tril = lax.broadcasted_iota(jnp.int32, (L, L), 1) <= lax.broadcasted_iota(jnp.int32, (L, L), 0)
        hp_ref[0, 0] = state[...]
        for r in range(hpg):
            ps = slice(r * pd, (r + 1) * pd)
            xr = xv[:, ps]
            xdt = xr * dt[:, r:r + 1]
            lm = jnp.exp(jnp.where(tril, ac[:, r:r + 1] - act[r:r + 1, :], -jnp.inf))
            hprev = state[ps, :]
            y = _dot((cb * lm).astype(BF16), xdt.astype(BF16))
            y = y + _dot(cm, hprev.astype(BF16), 1, 1) * ea[:, r:r + 1]
            y_ref[:, ps] = y + xr * d_ref[:, ps]
            st = _dot((xdt * dte[:, r:r + 1]).astype(BF16), bm, 0, 0)
            state[ps, :] = hprev * cd[:, r:r + 1] + st
        yfull = y_ref[...]
        zg = z_ref[...]
        yg = yfull * (zg * _sigmoid(zg))
        yn_ref[...] = (yg * lax.rsqrt(jnp.mean(yg * yg, axis=1, keepdims=True) + EPS) * ng_ref[...]).astype(BF16)

    return pl.pallas_call(
        body, name=name, grid=(G, nc),
        in_specs=[sp["x"], sp["b"], sp["c"], sp["z"], sp["dt"], sp["gvec"], sp["gvec"], sp["chan"], sp["chan"], sp["tri"]],
        out_specs=[sp["x"], sp["x"], sp["hp"]],
        out_shape=[jax.ShapeDtypeStruct((S, inner), BF16), jax.ShapeDtypeStruct((S, inner), F32),
                   jax.ShapeDtypeStruct((G, nc, gw, n), F32)],
        scratch_shapes=[pltpu.VMEM((gw, n), F32)],
        compiler_params=_params(("arbitrary", "arbitrary")),
    )(xbc, xbc, xbc, zx, dtg, bias_g, alog_g, d_chan, ngain, _tri(L, "row_ge_col"))


def ssd_bwd(xbc, zx, dtg, bias_g, alog_g, d_chan, ngain, yfull, hp, dyn, L, G, *, name):
    S = xbc.shape[0]
    n = SSM_STATE
    inner = xbc.shape[1] - 2 * G * n
    gw = inner // G
    pd = SB_HEAD_DIM
    hpg = gw // pd
    nc = S // L
    sp = _ssd_specs(S, L, G, hpg, pd, inner)(lambda c: nc - 1 - c)

    def body(x_ref, b_ref, c_ref, z_ref, dt_ref, bias_ref, alog_ref, d_ref, ng_ref, tl_ref, tu_ref,
             yf_ref, hp_ref, dyn_ref,
             dz_ref, dx_ref, db_ref, dc_ref, ddt_ref, dbias_ref, dalog_ref, dd_ref, dng_ref, dstate):
        first = pl.program_id(1) == 0

        @pl.when(first)
        def _():
            dstate[...] = jnp.zeros_like(dstate)
            dbias_ref[...] = jnp.zeros_like(dbias_ref)
            dalog_ref[...] = jnp.zeros_like(dalog_ref)
            dd_ref[...] = jnp.zeros_like(dd_ref)
            dng_ref[...] = jnp.zeros_like(dng_ref)

        dtraw = dt_ref[...]
        dt, a_neg, ac, act, ea, dte, cd = _ssd_chunk_terms(dtraw, bias_ref[0], alog_ref[0], tl_ref[...])
        xv = x_ref[...]
        bm = b_ref[...].astype(BF16)
        cm = c_ref[...].astype(BF16)
        cb = _dot(cm, bm, 1, 1)
        tril = lax.broadcasted_iota(jnp.int32, (L, L), 1) <= lax.broadcasted_iota(jnp.int32, (L, L), 0)
        lane = lax.broadcasted_iota(jnp.int32, (L, LANES), 1)
        lane1 = lax.broadcasted_iota(jnp.int32, (1, LANES), 1)

        yfull = yf_ref[...]
        zg = z_ref[...]
        sg = _sigmoid(zg)
        gate = zg * sg
        yg = yfull * gate
        rr = lax.rsqrt(jnp.mean(yg * yg, axis=1, keepdims=True) + EPS)
        yhat = yg * rr
        dynv = dyn_ref[...]
        dng_ref[...] += jnp.sum(dynv * yhat, axis=0, keepdims=True)
        t = dynv * ng_ref[...]
        dyg = rr * (t - yhat * jnp.mean(yhat * t, axis=1, keepdims=True))
        dy = dyg * gate
        dz_ref[...] = dyg * yfull * (sg * (1.0 + zg * (1.0 - sg)))

        dcb = jnp.zeros((L, L), F32)
        dc_acc = jnp.zeros((L, n), F32)
        db_acc = jnp.zeros((L, n), F32)
        dac = jnp.zeros((L, LANES), F32)
        xdx = jnp.zeros((L, LANES), F32)
        tail = jnp.zeros((1, LANES), F32)
        dskip = jnp.zeros((1, LANES), F32)
        ones_l = jnp.ones((L, LANES), BF16)
        for r in range(hpg):
            ps = slice(r * pd, (r + 1) * pd)
            xr = xv[:, ps]
            dyr = dy[:, ps]
            dtr = dt[:, r:r + 1]
            dter = dte[:, r:r + 1]
            cdr = cd[:, r:r + 1]
            xdt = xr * dtr
            xdtb = xdt.astype(BF16)
            dyrb = dyr.astype(BF16)
            lm = jnp.exp(jnp.where(tril, ac[:, r:r + 1] - act[r:r + 1, :], -jnp.inf))
            m32 = cb * lm
            mb = m32.astype(BF16)
            hprev = hp_ref[0, 0, ps, :]
            hpb = hprev.astype(BF16)
            dhn = dstate[ps, :]
            dhnb = dhn.astype(BF16)
            ear = ea[:, r:r + 1]
            gy = (dyr * ear).astype(BF16)
            dc_acc = dc_acc + _dot(gy, hpb)
            dstate[ps, :] = _dot(gy, cm, 0, 0) + dhn * cdr
            bdh = _dot(bm, dhnb, 1, 1)
            db_acc = db_acc + _dot((xdt * dter).astype(BF16), dhnb)
            dm = _dot(dyrb, xdtb, 1, 1)
            dxdt = bdh * dter + _dot(mb, dyrb, 0, 0)
            dcb = dcb + dm * lm
            wmat = dm * m32
            whi = wmat.astype(BF16)
            wlo = (wmat - whi.astype(F32)).astype(BF16)
            col_w = _dot(whi, ones_l, 0, 0) + _dot(wlo, ones_l, 0, 0)
            t_end = xdt * bdh * dter
            e_r = jnp.sum(wmat, axis=1, keepdims=True) \
                + jnp.sum(dyr * _dot(cm, hpb, 1, 1) * ear - t_end, axis=1, keepdims=True)
            c_r = cdr * _sum_all(dhn * hprev) + _sum_all(t_end)
            dac = dac + jnp.where(lane == r, e_r - col_w, 0.0)
            xdx = xdx + jnp.where(lane == r, jnp.sum(dxdt * xr, axis=1, keepdims=True), 0.0)
            tail = tail + jnp.where(lane1 == r, c_r, 0.0)
            dskip = dskip + jnp.where(lane1 == r, _sum_all(dyr * xr), 0.0)
            dx_ref[:, ps] = dxdt * dtr + dyr * d_ref[:, ps]
        dcbb = dcb.astype(BF16)
        dc_ref[...] = dc_acc + _dot(dcbb, bm)
        db_ref[...] = db_acc + _dot(dcbb, cm, 0, 0)
        da = _dot_exact_left(tu_ref[...], dac) + tail
        real = lane < hpg
        ddt = jnp.where(real, (da * a_neg + xdx) * _sigmoid(dtraw + bias_ref[0]), 0.0)
        ddt_ref[...] = ddt
        dd_ref[0] += dskip
        dbias_ref[0] += jnp.sum(ddt, axis=0, keepdims=True)
        dalog_ref[0] += jnp.where(lane1 < hpg, jnp.sum(da * dt, axis=0, keepdims=True) * a_neg, 0.0)

    return pl.pallas_call(
        body, name=name, grid=(G, nc),
        in_specs=[sp["x"], sp["b"], sp["c"], sp["z"], sp["dt"], sp["gvec"], sp["gvec"], sp["chan"], sp["chan"],
                  sp["tri"], sp["tri"], sp["x"], sp["hp"], sp["x"]],
        out_specs=[sp["x"], sp["x"], sp["bc"], sp["bc"], sp["dt"], sp["gvec"], sp["gvec"], sp["gvec"], sp["chan"]],
        out_shape=[jax.ShapeDtypeStruct((S, inner), F32), jax.ShapeDtypeStruct((S, inner), F32),
                   jax.ShapeDtypeStruct((S, G * n), F32), jax.ShapeDtypeStruct((S, G * n), F32),
                   jax.ShapeDtypeStruct((S, G * LANES), F32), jax.ShapeDtypeStruct((G, 1, LANES), F32),
                   jax.ShapeDtypeStruct((G, 1, LANES), F32), jax.ShapeDtypeStruct((G, 1, LANES), F32),
                   jax.ShapeDtypeStruct((1, inner), F32)],
        scratch_shapes=[pltpu.VMEM((gw, n), F32)],
        compiler_params=_params(("arbitrary", "arbitrary")),
    )(xbc, xbc, xbc, zx, dtg, bias_g, alog_g, d_chan, ngain, _tri(L, "row_ge_col"), _tri(L, "row_le_col"),
      yfull, hp, dyn)


def _spread_dt(w_dt_t, G, hpg):
    K = w_dt_t.shape[1]
    w = w_dt_t.reshape(G, hpg, K)
    return jnp.pad(w, ((0, 0), (0, LANES - hpg), (0, 0))).reshape(G * LANES, K)


def _group_vec(v, G, hpg):
    return jnp.pad(v.reshape(G, 1, hpg), ((0, 0), (0, 0), (0, LANES - hpg)))


def local_step(x, target, W, late=None):
    S, D = x.shape
    depth = W["mix_norm"].shape[0]
    gm_groups, gm_chunk = W["gm_w_s"].shape[1], W["gm_w_s"].shape[2]
    heads = W["ssm_dt_bias"].shape[1]
    inner = heads * SB_HEAD_DIM
    L = gm_chunk
    received = None

    saved = []
    for i in range(depth):
        kind, j = i % 3, i // 3
        s = dict(x=x)
        h = rms_fwd(x, W["mix_norm"][i:i + 1], name="rms_mix_fwd")
        s["h"] = h
        if kind == 0:
            qkv = matmul(h, W["sb_w_qkv"][j], tb=True, name="mm_qkv")
            if late is not None and i == 0:
                o, gathered = sb_attn_fwd(qkv, W["sb_q_gain"][j:j + 1], W["sb_k_gain"][j:j + 1], name="sb_fwd_gather",
                                          gather=late.shards)
                late.fill(W, gathered)
            else:
                o = sb_attn_fwd(qkv, W["sb_q_gain"][j:j + 1], W["sb_k_gain"][j:j + 1], name="sb_fwd")
            x1 = matmul(o, W["sb_w_o"][j], residual=x, name="mm_sb_out")
            s.update(qkv=qkv, o=o)
        elif kind == 1:
            wc = jnp.where(jnp.tril(jnp.ones((gm_chunk, gm_chunk), bool)), W["gm_w_s"][j], 0.0).astype(BF16)
            bsf = jnp.broadcast_to(W["gm_b_s"][j][:, :, None], (gm_groups, gm_chunk, W["gm_v_gain"].shape[1] // gm_groups)).astype(F32)
            zzpre = matmul(h, W["gm_w_in"][j], tb=True, name="mm_gm_in")
            p = gmlp_fwd(zzpre, W["gm_b_in"][j:j + 1], W["gm_v_gain"][j:j + 1], wc, bsf, name="gm_fwd")
            x1 = matmul(p, W["gm_w_out"][j], residual=x, name="mm_gm_out")
            s.update(zzpre=zzpre, p=p, wc=wc, bsf=bsf)
        else:
            conv_dim = W["ssm_conv_w"].shape[2]
            G = (conv_dim - inner) // (2 * SSM_STATE)
            hpg = heads // G
            w_in = W["ssm_w_in"][j]
            w_zx = w_in[:inner + conv_dim]
            w_dtg = _spread_dt(w_in[inner + conv_dim:], G, hpg)
            bias_g = _group_vec(W["ssm_dt_bias"][j], G, hpg)
            alog_g = _group_vec(W["ssm_a_log"][j], G, hpg)
            d_chan = jnp.repeat(W["ssm_d"][j], SB_HEAD_DIM)[None, :]
            ngain = W["ssm_norm_gain"][j:j + 1]
            zx = matmul(h, w_zx, tb=True, name="mm_ssm_zx")
            dtg = matmul(h, w_dtg, tb=True, name="mm_ssm_dt")
            xbc = conv_fwd(zx, W["ssm_conv_w"][j], W["ssm_conv_b"][j:j + 1], inner, name="conv_fwd")
            yn, yfull, hp = ssd_fwd(xbc, zx, dtg, bias_g, alog_g, d_chan, ngain, L, G, name="ssd_fwd")
            x1 = matmul(yn, W["ssm_w_out"][j], residual=x, name="mm_ssm_out")
            s.update(w_zx=w_zx, w_dtg=w_dtg, bias_g=bias_g, alog_g=alog_g, d_chan=d_chan, ngain=ngain,
                     zx=zx, dtg=dtg, xbc=xbc, yn=yn, yfull=yfull, hp=hp)
        h2 = rms_fwd(x1, W["ffn_norm"][i:i + 1], name="rms_ffn_fwd")
        gu, a = ffn_up_fwd(h2, W["ffn_w_gu"][i], name="ffn_up_fwd")
        x2 = matmul(a, W["ffn_w_down"][i], residual=x1, name="mm_ffn_down")
        s.update(x1=x1, h2=h2, gu=gu, a=a)
        saved.append(s)
        x = x2

    dx, loss = loss_head(x, target, name="loss_head")

    gw = {k: {} for k in WEIGHTS}
    for i in reversed(range(depth)):
        kind, j = i % 3, i // 3
        s = saved[i]
        gw["ffn_w_down"][i] = matmul(s["a"], dx, ta=True, out_dtype=BF16, name="mm_ffn_dwdown")
        dgu = ffn_up_bwd(dx, W["ffn_w_down"][i], s["gu"], name="ffn_up_bwd")
        dh2 = matmul(dgu, W["ffn_w_gu"][i], a_split=2, name="mm_ffn_dh")
        gw["ffn_w_gu"][i] = matmul(dgu, s["h2"], ta=True, a_split=2, out_dtype=BF16, name="mm_ffn_dwgu")
        dx1, dgn = rms_bwd(s["x1"], W["ffn_norm"][i:i + 1], dh2, dx, name="rms_ffn_bwd")
        gw["ffn_norm"][i] = dgn[0]
        if kind == 0:
            do = matmul(dx1, W["sb_w_o"][j], tb=True, name="mm_sb_do")
            gw["sb_w_o"][j] = matmul(s["o"], dx1, ta=True, out_dtype=BF16, name="mm_sb_dwo")
            if late is not None and i == 0:
                dqkv, dqg, dkg, received = sb_attn_bwd(
                    s["qkv"], s["o"], do, W["sb_q_gain"][j:j + 1], W["sb_k_gain"][j:j + 1], name="sb_bwd_scatter",
                    scatter=late.contributions(gw))
            else:
                dqkv, dqg, dkg = sb_attn_bwd(s["qkv"], s["o"], do, W["sb_q_gain"][j:j + 1], W["sb_k_gain"][j:j + 1],
                                             name="sb_bwd")
            gw["sb_q_gain"][j] = dqg[0]
            gw["sb_k_gain"][j] = dkg[0]
            dh = matmul(dqkv, W["sb_w_qkv"][j], a_split=3, name="mm_sb_dh")
            gw["sb_w_qkv"][j] = matmul(dqkv, s["h"], ta=True, a_split=3, out_dtype=BF16, name="mm_sb_dwqkv")
        elif kind == 1:
            dp = matmul(dx1, W["gm_w_out"][j], tb=True, name="mm_gm_dp")
            gw["gm_w_out"][j] = matmul(s["p"], dx1, ta=True, out_dtype=BF16, name="mm_gm_dwout")
            dzz, db_in, dvg, dws, dbs = gmlp_bwd(s["zzpre"], W["gm_b_in"][j:j + 1], W["gm_v_gain"][j:j + 1],
                                                s["wc"], s["bsf"], dp, name="gm_bwd")
            gw["gm_b_in"][j] = db_in[0]
            gw["gm_v_gain"][j] = dvg[0]
            gw["gm_w_s"][j] = dws
            gw["gm_b_s"][j] = dbs[:, :gm_groups].T
            dh = matmul(dzz, W["gm_w_in"][j], name="mm_gm_dh")
            gw["gm_w_in"][j] = matmul(dzz, s["h"], ta=True, out_dtype=BF16, name="mm_gm_dwin")
        else:
            conv_dim = W["ssm_conv_w"].shape[2]
            G = (conv_dim - inner) // (2 * SSM_STATE)
            hpg = heads // G
            dyn = matmul(dx1, W["ssm_w_out"][j], tb=True, name="mm_ssm_dyn")
            gw["ssm_w_out"][j] = matmul(s["yn"], dx1, ta=True, out_dtype=BF16, name="mm_ssm_dwout")
            dz, dxs, dbm, dcm, ddt, dbias, dalog, dd, dng = ssd_bwd(
                s["xbc"], s["zx"], s["dtg"], s["bias_g"], s["alog_g"], s["d_chan"], s["ngain"], s["yfull"], s["hp"],
                dyn, L, G, name="ssd_bwd")
            dpre, dcw, dcb = conv_bwd(s["zx"], W["ssm_conv_w"][j], W["ssm_conv_b"][j:j + 1], inner, [dxs, dbm, dcm],
                                      name="conv_bwd")
            dzx = jnp.concatenate([dz, dpre], axis=1)
            dh = matmul(ddt, s["w_dtg"], name="mm_ssm_dh_dt")
            dh = matmul(dzx, s["w_zx"], residual=dh, name="mm_ssm_dh")
            dw_zx = matmul(dzx, s["h"], ta=True, out_dtype=BF16, name="mm_ssm_dwzx")
            dw_dtg = matmul(ddt, s["h"], ta=True, out_dtype=BF16, name="mm_ssm_dwdt")
            dw_dt = dw_dtg.reshape(G, LANES, D)[:, :hpg, :].reshape(heads, D)
            gw["ssm_w_in"][j] = jnp.concatenate([dw_zx, dw_dt], axis=0)
            gw["ssm_conv_w"][j] = dcw
            gw["ssm_conv_b"][j] = dcb[0]
            gw["ssm_dt_bias"][j] = dbias[:, 0, :hpg].reshape(heads)
            gw["ssm_a_log"][j] = dalog[:, 0, :hpg].reshape(heads)
            gw["ssm_d"][j] = dd[:, 0, :hpg].reshape(heads)
            gw["ssm_norm_gain"][j] = dng[0]
        dx, dgn = rms_bwd(s["x"], W["mix_norm"][i:i + 1], dh, dx1, name="rms_mix_bwd")
        gw["mix_norm"][i] = dgn[0]

    return loss, dx, gw, received


MESH = pl.DeviceIdType.MESH
HBM_SPEC = pl.BlockSpec(memory_space=pltpu.HBM)
VMEM_SPEC = pl.BlockSpec(memory_space=pltpu.VMEM)


def _my_position():
    return lax.axis_index("x"), lax.axis_index("y"), lax.axis_index("c")


def _flip(v, bit):
    return 1 - v if bit else v


def all_gather(shards, *, name):
    n = len(shards)

    def body(*refs):
        for phase in ("start", "forward", "finish"):
            _ag_phase(refs[:n], refs[n:2 * n], refs[2 * n:], phase)

    return pl.pallas_call(
        body, name=name, out_shape=[jax.ShapeDtypeStruct((N_DEV,) + s.shape, s.dtype) for s in shards],
        in_specs=[HBM_SPEC] * n, out_specs=[HBM_SPEC] * n, scratch_shapes=_copy_semaphores(n),
    )(*shards)


def _ag_phase(x_refs, out_refs, sems, phase):
    send_sems, recv_sems, local_sems = sems
    x, y, c = _my_position()
    me, sibling = (x, y, c), (x, y, 1 - c)
    chips = [(1 - x, y), (x, 1 - y), (1 - x, 1 - y)]
    for p, (x_ref, out_ref) in enumerate(zip(x_refs, out_refs)):
        def slot(px, py, pc):
            return out_ref.at[4 * px + 2 * py + pc]

        def copy(k, block, to, src=None):
            return pltpu.make_async_remote_copy(
                src_ref=slot(*block) if src is None else src, dst_ref=slot(*block),
                send_sem=send_sems.at[7 * p + k], recv_sem=recv_sems.at[7 * p + k], device_id=to, device_id_type=MESH)

        mine = pltpu.make_async_copy(x_ref, slot(*me), local_sems.at[p])
        first = [copy(0, me, sibling, src=x_ref)]
        first += [copy(1 + j, me, (*chip, c), src=x_ref) for j, chip in enumerate(chips)]
        passed = [copy(4 + j, (*chip, c), sibling) for j, chip in enumerate(chips)]
        if phase == "start":
            mine.start()
            for cp in first:
                cp.start()
        elif phase == "forward":
            for j, chip in enumerate(chips):
                copy(1 + j, (*chip, c), me).wait_recv()
                passed[j].start()
        else:
            copy(0, sibling, me).wait_recv()
            for j, chip in enumerate(chips):
                copy(4 + j, (*chip, 1 - c), me).wait_recv()
            for cp in first + passed:
                cp.wait_send()
            mine.wait()


def _copy_semaphores(n):
    return [pltpu.SemaphoreType.DMA((7 * n,)), pltpu.SemaphoreType.DMA((7 * n,)), pltpu.SemaphoreType.DMA((n,))]


def _rs_phase(g_refs, out_refs, sems, phase):
    send_sems, recv_sems, local_sems = sems
    x, y, c = _my_position()
    me = 4 * x + 2 * y + c
    copies = []
    for p, (g_ref, out_ref) in enumerate(zip(g_refs, out_refs)):
        copies.append(pltpu.make_async_copy(g_ref.at[me], out_ref.at[me], local_sems.at[p]))
        for k in range(1, N_DEV):
            px, py, pc = _flip(x, k & 4), _flip(y, k & 2), _flip(c, k & 1)
            copies.append(pltpu.make_async_remote_copy(
                src_ref=g_ref.at[4 * px + 2 * py + pc], dst_ref=out_ref.at[me],
                send_sem=send_sems.at[7 * p + k - 1], recv_sem=recv_sems.at[7 * p + k - 1],
                device_id=(px, py, pc), device_id_type=MESH))
    for cp in copies:
        if phase == "start":
            cp.start()
        else:
            cp.wait()


def exchange_for_reduce_scatter(gs, *, name):
    n = len(gs)

    def body(*refs):
        for phase in ("start", "finish"):
            _rs_phase(refs[:n], refs[n:2 * n], refs[2 * n:], phase)

    return pl.pallas_call(
        body, name=name, out_shape=[jax.ShapeDtypeStruct(g.shape, g.dtype) for g in gs],
        in_specs=[HBM_SPEC] * n, out_specs=[HBM_SPEC] * n, scratch_shapes=_copy_semaphores(n),
    )(*gs)


def sum_slots(recv, *, name):
    n, R, C = recv.shape
    tr = _pick(R, (512, 256, 128))

    def body(r_ref, o_ref):
        acc = r_ref[0].astype(F32)
        for s in range(1, n):
            acc = acc + r_ref[s].astype(F32)
        o_ref[...] = acc

    return pl.pallas_call(
        body, name=name, grid=(R // tr,), in_specs=[pl.BlockSpec((n, tr, C), lambda i: (0, i, 0))],
        out_specs=pl.BlockSpec((tr, C), lambda i: (i, 0)), out_shape=jax.ShapeDtypeStruct((R, C), F32),
        compiler_params=_params(("parallel",)),
    )(recv)


def all_reduce_small(vs, *, name):
    n = len(vs)

    def body(*refs):
        v_refs, o_refs, bufs = refs[:n], refs[n:2 * n], refs[2 * n:3 * n]
        send_sems, recv_sems = refs[3 * n:]
        x, y, c = _my_position()
        me = 4 * x + 2 * y + c
        copies = []
        for p, (v_ref, buf) in enumerate(zip(v_refs, bufs)):
            buf[me] = v_ref[...]
            for k in range(1, N_DEV):
                px, py, pc = _flip(x, k & 4), _flip(y, k & 2), _flip(c, k & 1)
                copies.append(pltpu.make_async_remote_copy(
                    src_ref=v_ref, dst_ref=buf.at[me], send_sem=send_sems.at[7 * p + k - 1],
                    recv_sem=recv_sems.at[7 * p + k - 1], device_id=(px, py, pc), device_id_type=MESH))
        for cp in copies:
            cp.start()
        for cp in copies:
            cp.wait()
        for o_ref, buf in zip(o_refs, bufs):
            acc = buf[0]
            for s in range(1, N_DEV):
                acc = acc + buf[s]
            o_ref[...] = acc

    return pl.pallas_call(
        body, name=name, out_shape=[jax.ShapeDtypeStruct(v.shape, F32) for v in vs],
        in_specs=[VMEM_SPEC] * n, out_specs=[VMEM_SPEC] * n,
        scratch_shapes=[pltpu.VMEM((N_DEV,) + v.shape, F32) for v in vs]
        + [pltpu.SemaphoreType.DMA((7 * n,)), pltpu.SemaphoreType.DMA((7 * n,))],
        compiler_params=pltpu.CompilerParams(vmem_limit_bytes=VMEM_LIMIT_BYTES),
    )(*vs)


def _pad_rows(a, mult):
    pad = (-a.shape[0]) % mult
    return jnp.pad(a, ((0, pad), (0, 0))) if pad else a


def _pack_small(arrays):
    flat = []
    for a in arrays:
        f = a.reshape(-1).astype(F32)
        flat.append(jnp.pad(f, (0, (-f.shape[0]) % LANES)))
    return _pad_rows(jnp.concatenate(flat).reshape(-1, LANES), 8)


def _unpack_small(packed, shapes):
    flat = packed.reshape(-1)
    out, r = [], 0
    for shp in shapes:
        n = math.prod(shp)
        out.append(flat[r:r + n].reshape(shp))
        r += n + (-n) % LANES
    return out


ARG_NAMES = ("x",) + WEIGHTS + ("loss_target",) + tuple("m_" + w for w in WEIGHTS) + tuple("v_" + w for w in WEIGHTS)


def kernel(x, mix_norm, ffn_norm, sb_w_qkv, sb_q_gain, sb_k_gain, sb_w_o, gm_w_in, gm_b_in, gm_v_gain, gm_w_s, gm_b_s, gm_w_out, ssm_w_in, ssm_conv_w, ssm_conv_b, ssm_dt_bias, ssm_a_log, ssm_d, ssm_norm_gain, ssm_w_out, ffn_w_gu, ffn_w_down, loss_target, m_mix_norm, m_ffn_norm, m_sb_w_qkv, m_sb_q_gain, m_sb_k_gain, m_sb_w_o, m_gm_w_in, m_gm_b_in, m_gm_v_gain, m_gm_w_s, m_gm_b_s, m_gm_w_out, m_ssm_w_in, m_ssm_conv_w, m_ssm_conv_b, m_ssm_dt_bias, m_ssm_a_log, m_ssm_d, m_ssm_norm_gain, m_ssm_w_out, m_ffn_w_gu, m_ffn_w_down, v_mix_norm, v_ffn_norm, v_sb_w_qkv, v_sb_q_gain, v_sb_k_gain, v_sb_w_o, v_gm_w_in, v_gm_b_in, v_gm_v_gain, v_gm_w_s, v_gm_b_s, v_gm_w_out, v_ssm_w_in, v_ssm_conv_w, v_ssm_conv_b, v_ssm_dt_bias, v_ssm_a_log, v_ssm_d, v_ssm_norm_gain, v_ssm_w_out, v_ffn_w_gu, v_ffn_w_down):
    given = dict(zip(ARG_NAMES, (x, mix_norm, ffn_norm, sb_w_qkv, sb_q_gain, sb_k_gain, sb_w_o, gm_w_in, gm_b_in, gm_v_gain, gm_w_s, gm_b_s, gm_w_out, ssm_w_in, ssm_conv_w, ssm_conv_b, ssm_dt_bias, ssm_a_log, ssm_d, ssm_norm_gain, ssm_w_out, ffn_w_gu, ffn_w_down, loss_target, m_mix_norm, m_ffn_norm, m_sb_w_qkv, m_sb_q_gain, m_sb_k_gain, m_sb_w_o, m_gm_w_in, m_gm_b_in, m_gm_v_gain, m_gm_w_s, m_gm_b_s, m_gm_w_out, m_ssm_w_in, m_ssm_conv_w, m_ssm_conv_b, m_ssm_dt_bias, m_ssm_a_log, m_ssm_d, m_ssm_norm_gain, m_ssm_w_out, m_ffn_w_gu, m_ffn_w_down, v_mix_norm, v_ffn_norm, v_sb_w_qkv, v_sb_q_gain, v_sb_k_gain, v_sb_w_o, v_gm_w_in, v_gm_b_in, v_gm_v_gain, v_gm_w_s, v_gm_b_s, v_gm_w_out, v_ssm_w_in, v_ssm_conv_w, v_ssm_conv_b, v_ssm_dt_bias, v_ssm_a_log, v_ssm_d, v_ssm_norm_gain, v_ssm_w_out, v_ffn_w_gu, v_ffn_w_down)))
    mx, my, mc = _my_position()
    me = 4 * mx + 2 * my + mc

    pieces = [(k, l) for k in BIG for l in range(given[k].shape[0])]
    early = [("sb_w_qkv", 0)]
    late_pieces = [p for p in pieces if p not in early]
    last = [("sb_w_qkv", 0)]
    main = [p for p in pieces if p not in last]

    def shards_of(ps):
        return [(given[k][l].T if k in COL_SHARDED else given[k][l]).astype(BF16) for k, l in ps]

    def piece_to_full(g, k):
        rows, cols = given[k].shape[1:]
        return g.reshape(N_DEV * cols, rows) if k in COL_SHARDED else g.reshape(N_DEV * rows, cols)

    def full_to_piece(full, k):
        return full.reshape(N_DEV, -1, PACK_COLS)

    def summed_to_shard(g, k):
        rows, cols = given[k].shape[1:]
        return g.reshape(cols, rows) if k in COL_SHARDED else g.reshape(rows, cols)

    def contributions(gw, ps):
        return [full_to_piece(gw[k][l], k) for k, l in ps]

    sharded_small = [lax.bitcast_convert_type(given[k], BF16) for k in SMALL_SHARDED]
    tail = jnp.concatenate([a.reshape(-1) for a in sharded_small])
    tail = jnp.pad(tail, (0, (-tail.size) % PACK_COLS)).reshape(-1, PACK_COLS)

    W = {k: given[k] for k in SMALL if k not in SMALL_SHARDED}
    W.update({k: [None] * given[k].shape[0] for k in BIG})
    for (k, l), g in zip(early, all_gather(shards_of(early), name="all_gather_early")):
        W[k][l] = piece_to_full(g, k)

    class Late:
        shards = shards_of(late_pieces) + [tail]

        @staticmethod
        def fill(weights, gathered):
            for (k, l), g in zip(late_pieces, gathered):
                weights[k][l] = piece_to_full(g, k)
            tail_g = gathered[-1].reshape(N_DEV, -1)
            off = 0
            for k, a in zip(SMALL_SHARDED, sharded_small):
                g = lax.bitcast_convert_type(tail_g[:, off:off + a.size].reshape((N_DEV,) + a.shape), F32)
                weights[k] = jnp.moveaxis(g, 0, -2).reshape(g.shape[1:-1] + (N_DEV * g.shape[-1],))
                off += a.size

        @staticmethod
        def contributions(gw):
            return contributions(gw, main)

    loss, gx, gw, received_main = local_step(given["x"][0], given["loss_target"][0], W, late=Late)
    received_last = exchange_for_reduce_scatter(contributions(gw, last), name="reduce_scatter_last")

    grads_small = {k: jnp.stack([gw[k][l] for l in sorted(gw[k])], axis=0) for k in SMALL}
    packed_names = tuple(k for k in SMALL if k != "gm_w_s")
    small_shapes = [grads_small[k].shape for k in packed_names] + [(1, 1)]
    red_ws, red_rest = all_reduce_small(
        [grads_small["gm_w_s"].reshape(-1, LANES), _pack_small([grads_small[k] for k in packed_names] + [loss])],
        name="all_reduce_small")
    small_full = dict(zip(packed_names + ("loss",), _unpack_small(red_rest, small_shapes)))
    small_full["gm_w_s"] = red_ws.reshape(grads_small["gm_w_s"].shape)

    g_piece = {}
    for grp, received in ((main, received_main), (last, received_last)):
        for p, r in zip(grp, received):
            g_piece[p] = summed_to_shard(sum_slots(r, name="reduce_scatter_sum"), p[0])
    out_g, out_d, out_m, out_v = {}, {}, {}, {}
    for k in BIG:
        swap = lambda a: jnp.swapaxes(a, -1, -2)
        keep_t = k in COL_SHARDED and given[k].shape[-1] % LANES != 0
        flip = swap if keep_t else (lambda a: a)
        g = jnp.stack([g_piece[(k, l)] for l in range(given[k].shape[0])], axis=0)
        if k in COL_SHARDED and not keep_t:
            g = swap(g)
        res = adamw(flip(given[k]), g, flip(given["m_" + k]), flip(given["v_" + k]), name="adamw_" + k)
        out_g[k], out_d[k], out_m[k], out_v[k] = (flip(a) for a in (g,) + tuple(res))

    gsmall = {}
    for k in SMALL:
        g = small_full[k]
        if k in SMALL_SHARDED:
            n = given[k].shape[-1]
            g = lax.dynamic_slice_in_dim(g, me * n, n, axis=g.ndim - 1)
        gsmall[k] = g
    local_shapes = [given[k].shape for k in packed_names]
    dsm, nmsm, nvsm = adamw(*[_pack_small([src[k] for k in packed_names]) for src in (
        given, gsmall, {k: given["m_" + k] for k in packed_names}, {k: given["v_" + k] for k in packed_names})],
        name="adamw_small")
    out_g.update(gsmall)
    for dst, src in ((out_d, dsm), (out_m, nmsm), (out_v, nvsm)):
        dst.update(zip(packed_names, _unpack_small(src, local_shapes)))
    ws_shape = given["gm_w_s"].shape
    out_d["gm_w_s"], out_m["gm_w_s"], out_v["gm_w_s"] = (a.reshape(ws_shape) for a in adamw(
        *[a.reshape((-1,) + ws_shape[-2:]) for a in (given["gm_w_s"], gsmall["gm_w_s"], given["m_gm_w_s"], given["v_gm_w_s"])],
        name="adamw_gm_w_s"))

    return (small_full["loss"].reshape(()), gx[None],
            *[out_g[k] for k in WEIGHTS], *[out_d[k] for k in WEIGHTS],
            *[out_m[k] for k in WEIGHTS], *[out_v[k] for k in WEIGHTS])
```

```python
import math

import jax
import jax.numpy as jnp
from jax import lax
from jax.experimental import pallas as pl
from jax.experimental.pallas import tpu as pltpu

F32 = jnp.float32
BF16 = jnp.bfloat16
EPS = 1e-6
N_DEV = 8
SB_HEAD_DIM = 64
SB_TILE = 256
SB_FWD_QUERY_BLOCKS = 4
SB_BWD_QUERY_BLOCKS = 2
SSM_STATE = 128
SSM_CONV = 4
ADAM_LR = 0.001
ADAM_B1 = 0.9
ADAM_B2 = 0.999
ADAM_EPS = 1e-08
ADAM_WD = 0.01
ADAM_STEP = 10
VMEM_LIMIT_BYTES = 56 * 1024 * 1024
MATMUL_VMEM_BUDGET = 40 * 1024 * 1024
LANES = 128
PACK_COLS = 1024

BIG = ("sb_w_qkv", "sb_w_o", "gm_w_in", "gm_w_out", "ssm_w_in", "ssm_w_out", "ffn_w_gu", "ffn_w_down")
COL_SHARDED = ("sb_w_qkv", "gm_w_in", "ssm_w_in", "ffn_w_gu")
SMALL = ("mix_norm", "ffn_norm", "sb_q_gain", "sb_k_gain", "gm_b_in", "gm_v_gain", "gm_w_s", "gm_b_s",
         "ssm_conv_w", "ssm_conv_b", "ssm_dt_bias", "ssm_a_log", "ssm_d", "ssm_norm_gain")
SMALL_SHARDED = ("ssm_conv_w", "ssm_conv_b", "ssm_norm_gain")
WEIGHTS = ("mix_norm", "ffn_norm", "sb_w_qkv", "sb_q_gain", "sb_k_gain", "sb_w_o", "gm_w_in", "gm_b_in",
           "gm_v_gain", "gm_w_s", "gm_b_s", "gm_w_out", "ssm_w_in", "ssm_conv_w", "ssm_conv_b", "ssm_dt_bias",
           "ssm_a_log", "ssm_d", "ssm_norm_gain", "ssm_w_out", "ffn_w_gu", "ffn_w_down")


def _params(semantics=None):
    return pltpu.CompilerParams(dimension_semantics=semantics, vmem_limit_bytes=VMEM_LIMIT_BYTES)


def _pick(n, prefs):
    for t in prefs:
        if t <= n and n % t == 0:
            return t
    return n


def _dot(a, b, ca=1, cb=0):
    return lax.dot_general(a, b, (((ca,), (cb,)), ((), ())), preferred_element_type=F32)


def _split3(v):
    h1 = v.astype(BF16)
    r1 = v - h1.astype(F32)
    h2 = r1.astype(BF16)
    h3 = (r1 - h2.astype(F32)).astype(BF16)
    return h1, h2, h3


def _dot_exact_left(mat01, v):
    h1, h2, h3 = _split3(v)
    return _dot(mat01, h1) + _dot(mat01, h2) + _dot(mat01, h3)


def _sum_all(v):
    return jnp.sum(jnp.sum(v, axis=0, keepdims=True), axis=1, keepdims=True)


def _sigmoid(v):
    return 1.0 / (1.0 + jnp.exp(-v))


def _softplus(v):
    return jnp.maximum(v, 0.0) + jnp.log(1.0 + jnp.exp(-jnp.abs(v)))


def _erf(v):
    a = jnp.abs(v)
    t = 1.0 / (1.0 + 0.3275911 * a)
    poly = t * (0.254829592 + t * (-0.284496736 + t * (1.421413741 + t * (-1.453152027 + t * 1.061405429))))
    e = 1.0 - poly * jnp.exp(-a * a)
    return jnp.where(v < 0, -e, e)


def _gelu_and_grad(v):
    cdf = 0.5 * (1.0 + _erf(v * (1.0 / math.sqrt(2.0))))
    pdf = jnp.exp(-0.5 * v * v) * (1.0 / math.sqrt(2.0 * math.pi))
    return v * cdf, cdf + v * pdf


def matmul(a, b, *, ta=False, tb=False, out_dtype=F32, residual=None, a_split=1, b_split=1, name):
    if a_split > 1 and ta:
        assert a.shape[0] == a_split
        K, M = a.shape[1], a_split * a.shape[2]
    elif a_split > 1:
        assert a.shape[0] == a_split
        M, K = a.shape[1], a_split * a.shape[2]
    elif ta:
        K, M = a.shape
    else:
        M, K = a.shape
    if b_split > 1:
        assert not tb and b.shape[0] == b_split
        Kb, N = b.shape[1], b_split * b.shape[2]
    elif tb:
        N, Kb = b.shape
    else:
        Kb, N = b.shape
    assert K == Kb, (a.shape, b.shape, ta, tb)
    has_res = residual is not None
    tm = _pick(M // a_split if ta else M, (1024, 1408, 768, 512, 256, 128))
    tn = _pick(N // b_split, (1024, 1408, 1536, 768, 512, 256, 128))

    def vmem_bytes(tk):
        tiles = tm * tk * a.dtype.itemsize + tk * tn * b.dtype.itemsize
        outs = tm * tn * jnp.dtype(out_dtype).itemsize + (tm * tn * 4 if has_res else 0)
        return 2 * tiles + 2 * outs + (tm * tn * 4 if tk < K else 0)

    kp = K if ta else K // a_split
    tk = next((t for t in (K, 2048, 1408, 1024, 512, 256) if t <= kp and kp % t == 0 and vmem_bytes(t) <= MATMUL_VMEM_BUDGET),
              _pick(kp, (128,)))
    nk = K // tk
    if a_split > 1 and ta:
        nib = M // a_split // tm
        a_spec = pl.BlockSpec((None, tk, tm), lambda i, j, k: (i // nib, k, i % nib))
    elif a_split > 1:
        nkb = kp // tk
        a_spec = pl.BlockSpec((None, tm, tk), lambda i, j, k: (k // nkb, i, k % nkb))
    else:
        a_spec = pl.BlockSpec((tk, tm), lambda i, j, k: (k, i)) if ta else pl.BlockSpec((tm, tk), lambda i, j, k: (i, k))
    if b_split > 1:
        njb = N // b_split // tn
        b_spec = pl.BlockSpec((None, tk, tn), lambda i, j, k: (j // njb, k, j % njb))
    else:
        b_spec = pl.BlockSpec((tn, tk), lambda i, j, k: (j, k)) if tb else pl.BlockSpec((tk, tn), lambda i, j, k: (k, j))
    o_spec = pl.BlockSpec((tm, tn), lambda i, j, k: (i, j))
    ca, cb = (0 if ta else 1), (1 if tb else 0)

    def body(*refs):
        a_ref, b_ref = refs[:2]
        r_ref = refs[2] if has_res else None
        o_ref = refs[3] if has_res else refs[2]

        def finish(r):
            if has_res:
                r = r + r_ref[...]
            o_ref[...] = r.astype(out_dtype)

        def part():
            return _dot(a_ref[...].astype(BF16), b_ref[...].astype(BF16), ca, cb)

        if nk == 1:
            finish(part())
            return
        acc = refs[-1]
        k = pl.program_id(2)

        @pl.when(k == 0)
        def _():
            acc[...] = part()

        @pl.when(jnp.logical_and(k > 0, k < nk - 1))
        def _():
            acc[...] += part()

        @pl.when(k == nk - 1)
        def _():
            finish(acc[...] + part())

    in_specs = [a_spec, b_spec] + ([o_spec] if has_res else [])
    args = (a, b) + ((residual,) if has_res else ())
    return pl.pallas_call(
        body, name=name, grid=(M // tm, N // tn, nk), in_specs=in_specs, out_specs=o_spec,
        out_shape=jax.ShapeDtypeStruct((M, N), out_dtype),
        scratch_shapes=[pltpu.VMEM((tm, tn), F32)] if nk > 1 else [],
        compiler_params=_params(("parallel", "parallel", "arbitrary")),
    )(*args)


def rms_fwd(x, gain, *, name):
    S, D = x.shape
    tr = _pick(S, (512, 256, 128))

    def body(x_ref, g_ref, o_ref):
        xv = x_ref[...]
        r = lax.rsqrt(jnp.mean(xv * xv, axis=1, keepdims=True) + EPS)
        o_ref[...] = (xv * r * g_ref[...]).astype(BF16)

    return pl.pallas_call(
        body, name=name, grid=(S // tr,),
        in_specs=[pl.BlockSpec((tr, D), lambda i: (i, 0)), pl.BlockSpec((1, D), lambda i: (0, 0))],
        out_specs=pl.BlockSpec((tr, D), lambda i: (i, 0)), out_shape=jax.ShapeDtypeStruct((S, D), BF16),
        compiler_params=_params(("parallel",)),
    )(x, gain)


def rms_bwd(x, gain, dh, dres, *, name):
    S, D = x.shape
    tr = _pick(S, (512, 256, 128))

    def body(x_ref, g_ref, dh_ref, dr_ref, dx_ref, dg_ref):
        @pl.when(pl.program_id(0) == 0)
        def _():
            dg_ref[...] = jnp.zeros_like(dg_ref)

        xv = x_ref[...]
        dhv = dh_ref[...]
        r = lax.rsqrt(jnp.mean(xv * xv, axis=1, keepdims=True) + EPS)
        xhat = xv * r
        t = dhv * g_ref[...]
        dx_ref[...] = dr_ref[...] + r * (t - xhat * jnp.mean(xhat * t, axis=1, keepdims=True))
        dg_ref[...] += jnp.sum(dhv * xhat, axis=0, keepdims=True)

    row = pl.BlockSpec((tr, D), lambda i: (i, 0))
    vec = pl.BlockSpec((1, D), lambda i: (0, 0))
    return pl.pallas_call(
        body, name=name, grid=(S // tr,), in_specs=[row, vec, row, row], out_specs=[row, vec],
        out_shape=[jax.ShapeDtypeStruct((S, D), F32), jax.ShapeDtypeStruct((1, D), F32)],
        compiler_params=_params(("arbitrary",)),
    )(x, gain, dh, dres)


def ffn_up_fwd(h, w_gu_t, *, name):
    S, K = h.shape
    F = w_gu_t.shape[0] // 2
    tm = _pick(S, (512, 256, 128))
    tn = _pick(F, (1408, 1024, 768, 512, 256, 128))
    nj = F // tn

    def body(h_ref, wg_ref, wu_ref, gu_ref, a_ref):
        hv = h_ref[...]
        g = _dot(hv, wg_ref[...], 1, 1)
        u = _dot(hv, wu_ref[...], 1, 1)
        gu_ref[0] = g
        gu_ref[1] = u
        a_ref[...] = (g * _sigmoid(g) * u).astype(BF16)

    return pl.pallas_call(
        body, name=name, grid=(nj, S // tm),
        in_specs=[pl.BlockSpec((tm, K), lambda j, i: (i, 0)), pl.BlockSpec((tn, K), lambda j, i: (j, 0)),
                  pl.BlockSpec((tn, K), lambda j, i: (nj + j, 0))],
        out_specs=[pl.BlockSpec((2, tm, tn), lambda j, i: (0, i, j)), pl.BlockSpec((tm, tn), lambda j, i: (i, j))],
        out_shape=[jax.ShapeDtypeStruct((2, S, F), F32), jax.ShapeDtypeStruct((S, F), BF16)],
        compiler_params=_params(("parallel", "parallel")),
    )(h, w_gu_t, w_gu_t)


def ffn_up_bwd(dy, w_down, gu, *, name):
    S, D = dy.shape
    F = w_down.shape[0]
    tm = _pick(S, (512, 256, 128))
    tn = _pick(F, (1408, 1024, 768, 512, 256, 128))

    def body(dy_ref, wd_ref, gu_ref, o_ref):
        da = _dot(dy_ref[...].astype(BF16), wd_ref[...], 1, 1)
        g = gu_ref[0]
        u = gu_ref[1]
        s = _sigmoid(g)
        o_ref[0] = (da * u * (s * (1.0 + g * (1.0 - s)))).astype(BF16)
        o_ref[1] = (da * g * s).astype(BF16)

    pair = pl.BlockSpec((2, tm, tn), lambda j, i: (0, i, j))
    return pl.pallas_call(
        body, name=name, grid=(F // tn, S // tm),
        in_specs=[pl.BlockSpec((tm, D), lambda j, i: (i, 0)), pl.BlockSpec((tn, D), lambda j, i: (j, 0)), pair],
        out_specs=pair, out_shape=jax.ShapeDtypeStruct((2, S, F), BF16),
        compiler_params=_params(("parallel", "parallel")),
    )(dy, w_down, gu)


def loss_head(y, target, *, name):
    S, D = y.shape
    tr = _pick(S, (512, 256, 128))

    def body(y_ref, t_ref, dy_ref, l_ref):
        @pl.when(pl.program_id(0) == 0)
        def _():
            l_ref[...] = jnp.zeros_like(l_ref)

        err = y_ref[...] - t_ref[...]
        dy_ref[...] = err * (1.0 / D)
        l_ref[...] += jnp.sum(0.5 * jnp.mean(err * err, axis=1, keepdims=True), axis=0, keepdims=True)

    row = pl.BlockSpec((tr, D), lambda i: (i, 0))
    one = pl.BlockSpec((1, 1), lambda i: (0, 0))
    dy, l = pl.pallas_call(
        body, name=name, grid=(S // tr,), in_specs=[row, row], out_specs=[row, one],
        out_shape=[jax.ShapeDtypeStruct((S, D), F32), jax.ShapeDtypeStruct((1, 1), F32)],
        compiler_params=_params(("arbitrary",)),
    )(y, target)
    return dy, l


def adamw(w, g, m, v, *, name):
    R, C = w.shape[-2:]
    tr = _pick(R, (512, 256, 128, 64, 32, 16, 8))

    def body(w_ref, g_ref, m_ref, v_ref, d_ref, mo_ref, vo_ref):
        gv = g_ref[...]
        mn = ADAM_B1 * m_ref[...] + (1.0 - ADAM_B1) * gv
        vn = ADAM_B2 * v_ref[...] + (1.0 - ADAM_B2) * jnp.square(gv)
        m_hat = mn / (1.0 - ADAM_B1 ** ADAM_STEP)
        v_hat = vn / (1.0 - ADAM_B2 ** ADAM_STEP)
        d_ref[...] = -ADAM_LR * (m_hat / (jnp.sqrt(v_hat) + ADAM_EPS) + ADAM_WD * w_ref[...])
        mo_ref[...] = mn
        vo_ref[...] = vn

    tc = C if tr < R or C % LANES else _pick(C, (256, 128))
    if w.ndim == 3:
        grid = (w.shape[0], R // tr, C // tc)
        blk = pl.BlockSpec((None, tr, tc), lambda l, i, j: (l, i, j))
    else:
        grid = (R // tr, C // tc)
        blk = pl.BlockSpec((tr, tc), lambda i, j: (i, j))
    sds = jax.ShapeDtypeStruct(w.shape, F32)
    return pl.pallas_call(
        body, name=name, grid=grid, in_specs=[blk] * 4, out_specs=[blk] * 3, out_shape=[sds] * 3,
        compiler_params=_params(("parallel",) * len(grid)),
    )(w, g, m, v)


def _tri(n, kind):
    r = lax.broadcasted_iota(jnp.int32, (n, n), 0)
    c = lax.broadcasted_iota(jnp.int32, (n, n), 1)
    if kind == "row_gt_col":
        return (r > c).astype(BF16)
    if kind == "row_ge_col":
        return (r >= c).astype(BF16)
    if kind == "row_le_col":
        return (r <= c).astype(BF16)
    raise ValueError(kind)


def _sb_tile(qs, kj, r_carry, u_strict, masked):
    z = _dot(qs, kj, 1, 1)
    lb = jnp.minimum(z, 0.0) - jnp.log(1.0 + jnp.exp(-jnp.abs(z)))
    l1m = lb - z
    keep = None
    if masked:
        tq, tk = z.shape
        keep = lax.broadcasted_iota(jnp.int32, (tq, tk), 1) < lax.broadcasted_iota(jnp.int32, (tq, tk), 0)
        l1m = jnp.where(keep, l1m, 0.0)
    w = jnp.exp(lb + _dot(l1m.astype(BF16), u_strict) + r_carry)
    if masked:
        w = jnp.where(keep, w, 0.0)
    return lb, l1m, w, keep


def _sb_prep(T, nb, hd, refs_in, gains, scratch):
    q_scale = 1.0 / math.sqrt(hd)
    assert math.log2(q_scale) == round(math.log2(q_scale))

    def prep(i, _):
        rows = pl.ds(pl.multiple_of(i * T, T), T)
        for hh in range(2):
            sl = slice(hd * hh, hd * hh + hd)
            for n, (src, dst) in enumerate(zip(refs_in, scratch)):
                v = src[rows, sl]
                if n < 2:
                    v = v * lax.rsqrt(jnp.mean(v * v, axis=1, keepdims=True) + EPS) * gains[n][...]
                if n == 0:
                    v = v * q_scale
                dst[hh, rows, :] = v.astype(BF16)
        return 0

    lax.fori_loop(0, nb, prep, 0)


def _sb_chains(m, T, nq):
    rows = [pl.ds(pl.multiple_of((nq * m + qb) * T, T), T) for qb in range(nq)]
    return rows, [(hh, qb) for qb in range(nq) for hh in range(2)]


def _sb_sweep(tile, carry, chains, m, nq):
    for kk in reversed(range(nq)):
        carry = tile(nq * m + kk, carry, [(ch, ch[1] == kk) for ch in chains if ch[1] >= kk])
    return lax.fori_loop(0, nq * m, lambda jj, c: tile(nq * m - 1 - jj, c, [(ch, False) for ch in chains]), carry)


def sb_attn_fwd(qkv, q_gain, k_gain, *, name, gather=None):
    S, D3 = qkv.shape
    D = D3 // 3
    npairs = D // LANES
    hd = SB_HEAD_DIM
    T = min(SB_TILE, S)
    nb = S // T
    nq = SB_FWD_QUERY_BLOCKS
    assert nb % nq == 0

    def body(*refs):
        if gather is None:
            q_ref, k_ref, v_ref, qg_ref, kg_ref, us_ref, o_ref, qn_s, kn_s, vb_s = refs
        else:
            ng = len(gather)
            q_ref, k_ref, v_ref, qg_ref, kg_ref, us_ref = refs[:6]
            o_ref = refs[6 + ng]
            qn_s, kn_s, vb_s = refs[7 + 2 * ng:10 + 2 * ng]
            comm = (refs[6:6 + ng], refs[7 + ng:7 + 2 * ng], refs[10 + 2 * ng:])
            step = pl.program_id(0)
            pl.when(step == 0)(lambda: _ag_phase(*comm, "start"))
            pl.when(step == npairs - 1)(lambda: _ag_phase(*comm, "forward"))
        us = us_ref[...]
        _sb_prep(T, nb, hd, (q_ref, k_ref, v_ref), (qg_ref, kg_ref), (qn_s, kn_s, vb_s))

        def superblock(m, _):
            rows_q, chains = _sb_chains(m, T, nq)
            qs = {ch: qn_s[ch[0], rows_q[ch[1]], :] for ch in chains}

            def tile(j, carry, which):
                rows_j = pl.ds(pl.multiple_of(j * T, T), T)
                new = dict(carry)
                for ch, masked in which:
                    acc, rc = carry[ch]
                    _, l1m, w, _ = _sb_tile(qs[ch], kn_s[ch[0], rows_j, :], rc, us, masked)
                    new[ch] = (acc + _dot(w.astype(BF16), vb_s[ch[0], rows_j, :]),
                               rc + jnp.sum(l1m, axis=1, keepdims=True))
                return new

            carry = {ch: (jnp.zeros((T, hd), F32), jnp.zeros((T, 1), F32)) for ch in chains}
            carry = _sb_sweep(tile, carry, chains, m, nq)
            for qb in range(nq):
                o_ref[rows_q[qb], :] = jnp.concatenate([carry[(0, qb)][0], carry[(1, qb)][0]], axis=1)
            return 0

        lax.fori_loop(0, nb // nq, superblock, 0)
        if gather is not None:
            pl.when(step == npairs - 1)(lambda: _ag_phase(*comm, "finish"))

    col = lambda off: pl.BlockSpec((S, LANES), lambda p, off=off: (0, off + p))
    gain = pl.BlockSpec((1, hd), lambda p: (0, 0))
    in_specs = [col(0), col(npairs), col(2 * npairs), gain, gain, pl.BlockSpec((T, T), lambda p: (0, 0))]
    out_specs = [pl.BlockSpec((S, LANES), lambda p: (0, p))]
    out_shape = [jax.ShapeDtypeStruct((S, D), F32)]
    scratch = [pltpu.VMEM((2, S, hd), BF16)] * 3
    args = [qkv, qkv, qkv, q_gain, k_gain, _tri(T, "row_gt_col")]
    if gather is not None:
        in_specs += [HBM_SPEC] * len(gather)
        out_specs += [HBM_SPEC] * len(gather)
        out_shape += [jax.ShapeDtypeStruct((N_DEV,) + s.shape, s.dtype) for s in gather]
        scratch += _copy_semaphores(len(gather))
        args += list(gather)
    out = pl.pallas_call(
        body, name=name, grid=(npairs,), in_specs=in_specs, out_specs=out_specs, out_shape=out_shape,
        scratch_shapes=scratch, compiler_params=_params(("arbitrary",)),
    )(*args)
    return out[0] if gather is None else (out[0], list(out[1:]))


def sb_attn_bwd(qkv, o, do, q_gain, k_gain, *, name, scatter=None):
    S, D3 = qkv.shape
    D = D3 // 3
    npairs = D // LANES
    hd = SB_HEAD_DIM
    T = min(SB_TILE, S)
    nb = S // T
    nq = SB_BWD_QUERY_BLOCKS
    assert nb % nq == 0
    scale = 1.0 / math.sqrt(hd)

    def body(*refs):
        if scatter is None:
            (q_ref, k_ref, v_ref, o_ref, do_ref, qg_ref, kg_ref, us_ref,
             dqkv_ref, dg_ref, qn_s, kn_s, vb_s, dob_s) = refs
        else:
            ns = len(scatter)
            q_ref, k_ref, v_ref, o_ref, do_ref, qg_ref, kg_ref, us_ref = refs[:8]
            rs_in = refs[8:8 + ns]
            dqkv_ref, dg_ref = refs[8 + ns:10 + ns]
            rs_out = refs[10 + ns:10 + 2 * ns]
            qn_s, kn_s, vb_s, dob_s = refs[10 + 2 * ns:14 + 2 * ns]
            rs_sems = refs[14 + 2 * ns:]
            pl.when(pl.program_id(0) == 0)(lambda: _rs_phase(rs_in, rs_out, rs_sems, "start"))
        dq_ref, dk_ref, dv_ref = dqkv_ref.at[0], dqkv_ref.at[1], dqkv_ref.at[2]

        @pl.when(pl.program_id(0) == 0)
        def _():
            dg_ref[...] = jnp.zeros_like(dg_ref)

        us = us_ref[...]
        u_prefix = (1.0 - us.astype(F32)).astype(BF16)
        _sb_prep(T, nb, hd, (q_ref, k_ref, v_ref, do_ref), (qg_ref, kg_ref), (qn_s, kn_s, vb_s, dob_s))
        dk_ref[...] = jnp.zeros_like(dk_ref)
        dv_ref[...] = jnp.zeros_like(dv_ref)

        def superblock(m, _):
            rows_q, chains = _sb_chains(m, T, nq)
            qs = {ch: qn_s[ch[0], rows_q[ch[1]], :] for ch in chains}
            doi ={ch: dob_s[ch[0], rows_q[ch[1]], :] for ch in chains}
            dt_total = {ch: jnp.sum(doi[ch].astype(F32) * o_ref[rows_q[ch[1]], hd * ch[0]:hd * ch[0] + hd],
                                    axis=1, keepdims=True) for ch in chains}

            def tile(j, carry, which):
                rows_j = pl.ds(pl.multiple_of(j * T, T), T)
                new = dict(carry)
                dk_part, dv_part = {}, {}
                for ch, masked in which:
                    hh = ch[0]
                    dq_acc, rc, gc = carry[ch]
                    kj = kn_s[hh, rows_j, :]
                    lb, l1m, w, keep = _sb_tile(qs[ch], kj, rc, us, masked)
                    wb = w.astype(BF16)
                    g = _dot(doi[ch], vb_s[hh, rows_j, :], 1, 1) * wb.astype(F32)
                    g_row = jnp.sum(g, axis=1, keepdims=True)
                    g_upto = (dt_total[ch] - gc - g_row) + _dot(g.astype(BF16), u_prefix)
                    dz = g - g_upto * jnp.exp(lb)
                    if masked:
                        dz = jnp.where(keep, dz, 0.0)
                    dzb = dz.astype(BF16)
                    dv_part[hh] = dv_part.get(hh, 0.0) + _dot(wb, doi[ch], 0, 0)
                    dk_part[hh] = dk_part.get(hh, 0.0) + _dot(dzb, qs[ch], 0, 0)
                    new[ch] = (dq_acc + _dot(dzb, kj), rc + jnp.sum(l1m, axis=1, keepdims=True),
                               gc + g_row)
                dv_ref[rows_j, :] += jnp.concatenate([dv_part[0], dv_part[1]], axis=1)
                dk_ref[rows_j, :] += jnp.concatenate([dk_part[0], dk_part[1]], axis=1)
                return new

            zero1 = jnp.zeros((T, 1), F32)
            carry = {ch: (jnp.zeros((T, hd), F32), zero1, zero1) for ch in chains}
            carry = _sb_sweep(tile, carry, chains, m, nq)
            for qb in range(nq):
                dq_ref[rows_q[qb], :] = jnp.concatenate([carry[(0, qb)][0], carry[(1, qb)][0]], axis=1) * scale
            return 0

        lax.fori_loop(0, nb // nq, superblock, 0)

        def finish(i, carry):
            rows = pl.ds(pl.multiple_of(i * T, T), T)
            new = []
            for hh in range(2):
                sl = slice(hd * hh, hd * hh + hd)
                outs = []
                for raw_ref, gain_ref, dn in ((q_ref, qg_ref, dq_ref[rows, sl]), (k_ref, kg_ref, dk_ref[rows, sl])):
                    raw = raw_ref[rows, sl]
                    r = lax.rsqrt(jnp.mean(raw * raw, axis=1, keepdims=True) + EPS)
                    hat = raw * r
                    t = dn * gain_ref[...]
                    outs.append((r * (t - hat * jnp.mean(hat * t, axis=1, keepdims=True)),
                                 jnp.sum(dn * hat, axis=0, keepdims=True)))
                dq_ref[rows, sl] = outs[0][0]
                dk_ref[rows, sl] = outs[1][0]
                new.append((carry[hh][0] + outs[0][1], carry[hh][1] + outs[1][1]))
            return tuple(new)

        zg = (jnp.zeros((1, hd), F32), jnp.zeros((1, hd), F32))
        tot = lax.fori_loop(0, nb, finish, (zg, zg))
        dg_ref[0:1, 0:hd] += tot[0][0] + tot[1][0]
        dg_ref[1:2, 0:hd] += tot[0][1] + tot[1][1]
        if scatter is not None:
            pl.when(pl.program_id(0) == npairs - 1)(lambda: _rs_phase(rs_in, rs_out, rs_sems, "finish"))

    col = lambda off: pl.BlockSpec((S, LANES), lambda p, off=off: (0, off + p))
    gain = pl.BlockSpec((1, hd), lambda p: (0, 0))
    tri = pl.BlockSpec((T, T), lambda p: (0, 0))
    pair = pl.BlockSpec((S, LANES), lambda p: (0, p))
    in_specs = [col(0), col(npairs), col(2 * npairs), pair, pair, gain, gain, tri]
    out_specs = [pl.BlockSpec((3, S, LANES), lambda p: (0, 0, p)), pl.BlockSpec((8, LANES), lambda p: (0, 0))]
    out_shape = [jax.ShapeDtypeStruct((3, S, D), F32), jax.ShapeDtypeStruct((8, LANES), F32)]
    scratch = [pltpu.VMEM((2, S, hd), BF16)] * 4
    args = [qkv, qkv, qkv, o, do, q_gain, k_gain, _tri(T, "row_gt_col")]
    if scatter is not None:
        in_specs += [HBM_SPEC] * len(scatter)
        out_specs += [HBM_SPEC] * len(scatter)
        out_shape += [jax.ShapeDtypeStruct(g.shape, g.dtype) for g in scatter]
        scratch += _copy_semaphores(len(scatter))
        args += list(scatter)
    out = pl.pallas_call(
        body, name=name, grid=(npairs,), in_specs=in_specs, out_specs=out_specs, out_shape=out_shape,
        scratch_shapes=scratch, compiler_params=_params(("arbitrary",)),
    )(*args)
    res = (out[0], out[1][0:1, :hd], out[1][1:2, :hd])
    return res if scatter is None else res + (list(out[2:]),)


def gmlp_fwd(zzpre, b_in, v_gain, wc, bsf, *, name):
    S, H2 = zzpre.shape
    H = H2 // 2
    G, T, _ = wc.shape
    gd = H // G

    def body(z_ref, b_ref, vg_ref, wc_ref, bs_ref, p_ref):
        zz, _ = _gelu_and_grad(z_ref[...] + b_ref[...])
        u = zz[:, :H]
        v = zz[:, H:]
        vn = v * lax.rsqrt(jnp.mean(v * v, axis=1, keepdims=True) + EPS) * vg_ref[...]
        for g in range(G):
            gs = slice(g * gd, (g + 1) * gd)
            mixed = _dot(wc_ref[g], vn[:, gs].astype(BF16)) + bs_ref[g]
            p_ref[:, gs] = (u[:, gs] * mixed).astype(BF16)

    full3 = lambda shp: pl.BlockSpec(shp, lambda c: (0, 0, 0))
    return pl.pallas_call(
        body, name=name, grid=(S // T,),
        in_specs=[pl.BlockSpec((T, H2), lambda c: (c, 0)), pl.BlockSpec((1, H2), lambda c: (0, 0)),
                  pl.BlockSpec((1, H), lambda c: (0, 0)), full3((G, T, T)), full3((G, T, gd))],
        out_specs=pl.BlockSpec((T, H), lambda c: (c, 0)), out_shape=jax.ShapeDtypeStruct((S, H), BF16),
        compiler_params=_params(("parallel",)),
    )(zzpre, b_in, v_gain, wc, bsf)


def gmlp_bwd(zzpre, b_in, v_gain, wc, bsf, dp, *, name):
    S, H2 = zzpre.shape
    H = H2 // 2
    G, T, _ = wc.shape
    gd = H // G
    assert G <= LANES

    def body(z_ref, b_ref, vg_ref, wc_ref, bs_ref, dp_ref, dzz_ref, db_ref, dvg_ref, dws_ref, dbs_ref):
        @pl.when(pl.program_id(0) == 0)
        def _():
            db_ref[...] = jnp.zeros_like(db_ref)
            dvg_ref[...] = jnp.zeros_like(dvg_ref)
            dws_ref[...] = jnp.zeros_like(dws_ref)
            dbs_ref[...] = jnp.zeros_like(dbs_ref)

        zz, gp = _gelu_and_grad(z_ref[...] + b_ref[...])
        u = zz[:, :H]
        v = zz[:, H:]
        r = lax.rsqrt(jnp.mean(v * v, axis=1, keepdims=True) + EPS)
        vhat = v * r
        vg = vg_ref[...]
        vn = vhat * vg
        dpv = dp_ref[...]
        tril = lax.broadcasted_iota(jnp.int32, (T, T), 1) <= lax.broadcasted_iota(jnp.int32, (T, T), 0)
        lane = lax.broadcasted_iota(jnp.int32, (T, LANES), 1)
        dbs = jnp.zeros((T, LANES), F32)
        du_parts, dvn_parts = [], []
        for g in range(G):
            gs = slice(g * gd, (g + 1) * gd)
            vng = vn[:, gs].astype(BF16)
            wcg = wc_ref[g]
            mixed = _dot(wcg, vng) + bs_ref[g]
            dpg = dpv[:, gs]
            du_parts.append(dpg * mixed)
            dmx = dpg * u[:, gs]
            dmxb = dmx.astype(BF16)
            dvn_parts.append(_dot(wcg, dmxb, 0, 0))
            dws_ref[g] += jnp.where(tril, _dot(dmxb, vng, 1, 1), 0.0)
            dbs = dbs + jnp.where(lane == g, jnp.sum(dmx, axis=1, keepdims=True), 0.0)
        dbs_ref[...] += dbs
        du = jnp.concatenate(du_parts, axis=1)
        dvn = jnp.concatenate(dvn_parts, axis=1)
        dvg_ref[...] += jnp.sum(dvn * vhat, axis=0, keepdims=True)
        t = dvn * vg
        dv = r * (t - vhat * jnp.mean(vhat * t, axis=1, keepdims=True))
        dzu = du * gp[:, :H]
        dzv = dv * gp[:, H:]
        dzz_ref[:, :H] = dzu.astype(BF16)
        dzz_ref[:, H:] = dzv.astype(BF16)
        db_ref[:, :H] += jnp.sum(dzu, axis=0, keepdims=True)
        db_ref[:, H:] += jnp.sum(dzv, axis=0, keepdims=True)

    full3 = lambda shp: pl.BlockSpec(shp, lambda c: (0, 0, 0))
    vec = lambda n: pl.BlockSpec((1, n), lambda c: (0, 0))
    return pl.pallas_call(
        body, name=name, grid=(S // T,),
        in_specs=[pl.BlockSpec((T, H2), lambda c: (c, 0)), vec(H2), vec(H), full3((G, T, T)), full3((G, T, gd)),
                  pl.BlockSpec((T, H), lambda c: (c, 0))],
        out_specs=[pl.BlockSpec((T, H2), lambda c: (c, 0)), vec(H2), vec(H), full3((G, T, T)),
                   pl.BlockSpec((T, LANES), lambda c: (0, 0))],
        out_shape=[jax.ShapeDtypeStruct((S, H2), BF16), jax.ShapeDtypeStruct((1, H2), F32),
                   jax.ShapeDtypeStruct((1, H), F32), jax.ShapeDtypeStruct((G, T, T), F32),
                   jax.ShapeDtypeStruct((T, LANES), F32)],
        compiler_params=_params(("arbitrary",)),
    )(zzpre, b_in, v_gain, wc, bsf, dp)


def _shift_rows(v, k, n_rows):
    if k == 0:
        return v
    rolled = pltpu.roll(v, k % n_rows, 0)
    row = lax.broadcasted_iota(jnp.int32, v.shape, 0)
    keep = (row >= k) if k > 0 else (row < n_rows + k)
    return jnp.where(keep, rolled, 0.0)


def conv_fwd(zx, conv_w, conv_b, col0, *, name):
    S = zx.shape[0]
    C = conv_w.shape[1]
    tc = _pick(C, (256, 128))
    off = col0 // tc
    assert col0 % tc == 0

    def body(x_ref, w_ref, b_ref, o_ref):
        xv = x_ref[...]
        acc = b_ref[...] + w_ref[SSM_CONV - 1:SSM_CONV, :] * xv
        for k in range(SSM_CONV - 1):
            acc = acc + w_ref[k:k + 1, :] * _shift_rows(xv, SSM_CONV - 1 - k, S)
        o_ref[...] = acc * _sigmoid(acc)

    return pl.pallas_call(
        body, name=name, grid=(C // tc,),
        in_specs=[pl.BlockSpec((S, tc), lambda j: (0, off + j)), pl.BlockSpec((SSM_CONV, tc), lambda j: (0, j)),
                  pl.BlockSpec((1, tc), lambda j: (0, j))],
        out_specs=pl.BlockSpec((S, tc), lambda j: (0, j)), out_shape=jax.ShapeDtypeStruct((S, C), F32),
        compiler_params=_params(("parallel",)),
    )(zx, conv_w, conv_b)


def conv_bwd(zx, conv_w, conv_b, col0, douts, *, name):
    S = zx.shape[0]
    C = conv_w.shape[1]
    tc = LANES
    off = col0 // tc
    counts = [d.shape[1] // tc for d in douts]
    starts = [sum(counts[:p]) for p in range(len(douts))]
    assert sum(counts) * tc == C and all(d.shape[1] % tc == 0 for d in douts)

    def body(x_ref, w_ref, b_ref, *rest):
        do_refs, (dx_ref, dw_ref, db_ref) = rest[:len(douts)], rest[len(douts):]
        j = pl.program_id(0)
        dov = do_refs[-1][...]
        for p in reversed(range(len(douts) - 1)):
            dov = jnp.where(j < starts[p + 1], do_refs[p][...], dov)
        xv = x_ref[...]
        shifted = [_shift_rows(xv, SSM_CONV - 1 - k, S) for k in range(SSM_CONV)]
        acc = b_ref[...]
        for k in range(SSM_CONV):
            acc = acc + w_ref[k:k + 1, :] * shifted[k]
        s = _sigmoid(acc)
        dacc = dov * (s * (1.0 + acc * (1.0 - s)))
        db_ref[...] = jnp.sum(dacc, axis=0, keepdims=True)
        dx = jnp.zeros_like(xv)
        for k in range(SSM_CONV):
            dw_ref[k:k + 1, :] = jnp.sum(dacc * shifted[k], axis=0, keepdims=True)
            dx = dx + w_ref[k:k + 1, :] * _shift_rows(dacc, -(SSM_CONV - 1 - k), S)
        dx_ref[...] = dx

    slab = pl.BlockSpec((S, tc), lambda j: (0, j))
    piece_specs = [pl.BlockSpec((S, tc), lambda j, a=starts[p], n=counts[p]: (0, jnp.clip(j - a, 0, n - 1)))
                   for p in range(len(douts))]
    return pl.pallas_call(
        body, name=name, grid=(C // tc,),
        in_specs=[pl.BlockSpec((S, tc), lambda j: (0, off + j)), pl.BlockSpec((SSM_CONV, tc), lambda j: (0, j)),
                  pl.BlockSpec((1, tc), lambda j: (0, j))] + piece_specs,
        out_specs=[slab, pl.BlockSpec((SSM_CONV, tc), lambda j: (0, j)), pl.BlockSpec((1, tc), lambda j: (0, j))],
        out_shape=[jax.ShapeDtypeStruct((S, C), F32), jax.ShapeDtypeStruct((SSM_CONV, C), F32),
                   jax.ShapeDtypeStruct((1, C), F32)],
        compiler_params=_params(("arbitrary",)),
    )(zx, conv_w, conv_b, *douts)


def _ssd_chunk_terms(dtraw, bias, a_log, tl):
    dt = _softplus(dtraw + bias)
    a_neg = -jnp.exp(a_log)
    ac = _dot_exact_left(tl, dt * a_neg)
    ac_last = ac[ac.shape[0] - 1:, :]
    return dt, a_neg, ac, ac.T, jnp.exp(ac), jnp.exp(ac_last - ac), jnp.exp(ac_last)


def _ssd_specs(S, L, G, hpg, pd, inner):
    gw = hpg * pd
    n = SSM_STATE
    xb = inner // n

    def mk(cidx):
        return dict(
            x=pl.BlockSpec((L, gw), lambda g, c: (cidx(c), g)),
            b=pl.BlockSpec((L, n), lambda g, c: (cidx(c), xb + g)),
            c=pl.BlockSpec((L, n), lambda g, c: (cidx(c), xb + G + g)),
            z=pl.BlockSpec((L, gw), lambda g, c: (cidx(c), g)),
            dt=pl.BlockSpec((L, LANES), lambda g, c: (cidx(c), g)),
            gvec=pl.BlockSpec((1, 1, LANES), lambda g, c: (g, 0, 0)),
            chan=pl.BlockSpec((1, gw), lambda g, c: (0, g)),
            tri=pl.BlockSpec((L, L), lambda g, c: (0, 0)),
            hp=pl.BlockSpec((1, 1, gw, n), lambda g, c: (g, cidx(c), 0, 0)),
            bc=pl.BlockSpec((L, n), lambda g, c: (cidx(c), g)),
        )
    return mk


def ssd_fwd(xbc, zx, dtg, bias_g, alog_g, d_chan, ngain, L, G, *, name):
    S = xbc.shape[0]
    n = SSM_STATE
    inner = xbc.shape[1] - 2 * G * n
    gw = inner // G
    pd = SB_HEAD_DIM
    hpg = gw // pd
    nc = S // L
    sp = _ssd_specs(S, L, G, hpg, pd, inner)(lambda c: c)

    def body(x_ref, b_ref, c_ref, z_ref, dt_ref, bias_ref, alog_ref, d_ref, ng_ref, tl_ref,
             yn_ref, y_ref, hp_ref, state):
        @pl.when(pl.program_id(1) == 0)
        def _():
            state[...] = jnp.zeros_like(state)

        dt, _, ac, act, ea, dte, cd = _ssd_chunk_terms(dt_ref[...], bias_ref[0], alog_ref[0], tl_ref[...])
        xv = x_ref[...]
        bm = b_ref[...].astype(BF16)
        cm = c_ref[...].astype(BF16)
        cb = _dot(cm, bm, 1, 1)
        tril = lax.broadcasted_iota(jnp.int32, (L, L), 1) <= lax.broadcasted_iota(jnp.int32, (L, L), 0)
        hp_ref[0, 0] = state[...]
        for r in range(hpg):
            ps = slice(r * pd, (r + 1) * pd)
            xr = xv[:, ps]
            xdt = xr * dt[:, r:r + 1]
            lm = jnp.exp(jnp.where(tril, ac[:, r:r + 1] - act[r:r + 1, :], -jnp.inf))
            hprev = state[ps, :]
            y = _dot((cb * lm).astype(BF16), xdt.astype(BF16))
            y = y + _dot(cm, hprev.astype(BF16), 1, 1) * ea[:, r:r + 1]
            y_ref[:, ps] = y + xr * d_ref[:, ps]
            st = _dot((xdt * dte[:, r:r + 1]).astype(BF16), bm, 0, 0)
            state[ps, :] = hprev * cd[:, r:r + 1] + st
        yfull = y_ref[...]
        zg = z_ref[...]
        yg = yfull * (zg * _sigmoid(zg))
        yn_ref[...] = (yg * lax.rsqrt(jnp.mean(yg * yg, axis=1, keepdims=True) + EPS) * ng_ref[...]).astype(BF16)

    return pl.pallas_call(
        body, name=name, grid=(G, nc),
        in_specs=[sp["x"], sp["b"], sp["c"], sp["z"], sp["dt"], sp["gvec"], sp["gvec"], sp["chan"], sp["chan"], sp["tri"]],
        out_specs=[sp["x"], sp["x"], sp["hp"]],
        out_shape=[jax.ShapeDtypeStruct((S, inner), BF16), jax.ShapeDtypeStruct((S, inner), F32),
                   jax.ShapeDtypeStruct((G, nc, gw, n), F32)],
        scratch_shapes=[pltpu.VMEM((gw, n), F32)],
        compiler_params=_params(("arbitrary", "arbitrary")),
    )(xbc, xbc, xbc, zx, dtg, bias_g, alog_g, d_chan, ngain, _tri(L, "row_ge_col"))


def ssd_bwd(xbc, zx, dtg, bias_g, alog_g, d_chan, ngain, yfull, hp, dyn, L, G, *, name):
    S = xbc.shape[0]
    n = SSM_STATE
    inner = xbc.shape[1] - 2 * G * n
    gw = inner // G
    pd = SB_HEAD_DIM
    hpg = gw // pd
    nc = S // L
    sp = _ssd_specs(S, L, G, hpg, pd, inner)(lambda c: nc - 1 - c)

    def body(x_ref, b_ref, c_ref, z_ref, dt_ref, bias_ref, alog_ref, d_ref, ng_ref, tl_ref, tu_ref,
             yf_ref, hp_ref, dyn_ref,
             dz_ref, dx_ref, db_ref, dc_ref, ddt_ref, dbias_ref, dalog_ref, dd_ref, dng_ref, dstate):
        first = pl.program_id(1) == 0

        @pl.when(first)
        def _():
            dstate[...] = jnp.zeros_like(dstate)
            dbias_ref[...] = jnp.zeros_like(dbias_ref)
            dalog_ref[...] = jnp.zeros_like(dalog_ref)
            dd_ref[...] = jnp.zeros_like(dd_ref)
            dng_ref[...] = jnp.zeros_like(dng_ref)

        dtraw = dt_ref[...]
        dt, a_neg, ac, act, ea, dte, cd = _ssd_chunk_terms(dtraw, bias_ref[0], alog_ref[0], tl_ref[...])
        xv = x_ref[...]
        bm = b_ref[...].astype(BF16)
        cm = c_ref[...].astype(BF16)
        cb = _dot(cm, bm, 1, 1)
        tril = lax.broadcasted_iota(jnp.int32, (L, L), 1) <= lax.broadcasted_iota(jnp.int32, (L, L), 0)
        lane = lax.broadcasted_iota(jnp.int32, (L, LANES), 1)
        lane1 = lax.broadcasted_iota(jnp.int32, (1, LANES), 1)

        yfull = yf_ref[...]
        zg = z_ref[...]
        sg = _sigmoid(zg)
        gate = zg * sg
        yg = yfull * gate
        rr = lax.rsqrt(jnp.mean(yg * yg, axis=1, keepdims=True) + EPS)
        yhat = yg * rr
        dynv = dyn_ref[...]
        dng_ref[...] += jnp.sum(dynv * yhat, axis=0, keepdims=True)
        t = dynv * ng_ref[...]
        dyg = rr * (t - yhat * jnp.mean(yhat * t, axis=1, keepdims=True))
        dy = dyg * gate
        dz_ref[...] = dyg * yfull * (sg * (1.0 + zg * (1.0 - sg)))

        dcb = jnp.zeros((L, L), F32)
        dc_acc = jnp.zeros((L, n), F32)
        db_acc = jnp.zeros((L, n), F32)
        dac = jnp.zeros((L, LANES), F32)
        xdx = jnp.zeros((L, LANES), F32)
        tail = jnp.zeros((1, LANES), F32)
        dskip = jnp.zeros((1, LANES), F32)
        ones_l = jnp.ones((L, LANES), BF16)
        for r in range(hpg):
            ps = slice(r * pd, (r + 1) * pd)
            xr = xv[:, ps]
            dyr = dy[:, ps]
            dtr = dt[:, r:r + 1]
            dter = dte[:, r:r + 1]
            cdr = cd[:, r:r + 1]
            xdt = xr * dtr
            xdtb = xdt.astype(BF16)
            dyrb = dyr.astype(BF16)
            lm = jnp.exp(jnp.where(tril, ac[:, r:r + 1] - act[r:r + 1, :], -jnp.inf))
            m32 = cb * lm
            mb = m32.astype(BF16)
            hprev = hp_ref[0, 0, ps, :]
            hpb = hprev.astype(BF16)
            dhn = dstate[ps, :]
            dhnb = dhn.astype(BF16)
            ear = ea[:, r:r + 1]
            gy = (dyr * ear).astype(BF16)
            dc_acc = dc_acc + _dot(gy, hpb)
            dstate[ps, :] = _dot(gy, cm, 0, 0) + dhn * cdr
            bdh = _dot(bm, dhnb, 1, 1)
            db_acc = db_acc + _dot((xdt * dter).astype(BF16), dhnb)
            dm = _dot(dyrb, xdtb, 1, 1)
            dxdt = bdh * dter + _dot(mb, dyrb, 0, 0)
            dcb = dcb + dm * lm
            wmat = dm * m32
            whi = wmat.astype(BF16)
            wlo = (wmat - whi.astype(F32)).astype(BF16)
            col_w = _dot(whi, ones_l, 0, 0) + _dot(wlo, ones_l, 0, 0)
            t_end = xdt * bdh * dter
            e_r = jnp.sum(wmat, axis=1, keepdims=True) \
                + jnp.sum(dyr * _dot(cm, hpb, 1, 1) * ear - t_end, axis=1, keepdims=True)
            c_r = cdr * _sum_all(dhn * hprev) + _sum_all(t_end)
            dac = dac + jnp.where(lane == r, e_r - col_w, 0.0)
            xdx = xdx + jnp.where(lane == r, jnp.sum(dxdt * xr, axis=1, keepdims=True), 0.0)
            tail = tail + jnp.where(lane1 == r, c_r, 0.0)
            dskip = dskip + jnp.where(lane1 == r, _sum_all(dyr * xr), 0.0)
            dx_ref[:, ps] = dxdt * dtr + dyr * d_ref[:, ps]
        dcbb = dcb.astype(BF16)
        dc_ref[...] = dc_acc + _dot(dcbb, bm)
        db_ref[...] = db_acc + _dot(dcbb, cm, 0, 0)
        da = _dot_exact_left(tu_ref[...], dac) + tail
        real = lane < hpg
        ddt = jnp.where(real, (da * a_neg + xdx) * _sigmoid(dtraw + bias_ref[0]), 0.0)
        ddt_ref[...] = ddt
        dd_ref[0] += dskip
        dbias_ref[0] += jnp.sum(ddt, axis=0, keepdims=True)
        dalog_ref[0] += jnp.where(lane1 < hpg, jnp.sum(da * dt, axis=0, keepdims=True) * a_neg, 0.0)

    return pl.pallas_call(
        body, name=name, grid=(G, nc),
        in_specs=[sp["x"], sp["b"], sp["c"], sp["z"], sp["dt"], sp["gvec"], sp["gvec"], sp["chan"], sp["chan"],
                  sp["tri"], sp["tri"], sp["x"], sp["hp"], sp["x"]],
        out_specs=[sp["x"], sp["x"], sp["bc"], sp["bc"], sp["dt"], sp["gvec"], sp["gvec"], sp["gvec"], sp["chan"]],
        out_shape=[jax.ShapeDtypeStruct((S, inner), F32), jax.ShapeDtypeStruct((S, inner), F32),
                   jax.ShapeDtypeStruct((S, G * n), F32), jax.ShapeDtypeStruct((S, G * n), F32),
                   jax.ShapeDtypeStruct((S, G * LANES), F32), jax.ShapeDtypeStruct((G, 1, LANES), F32),
                   jax.ShapeDtypeStruct((G, 1, LANES), F32), jax.ShapeDtypeStruct((G, 1, LANES), F32),
                   jax.ShapeDtypeStruct((1, inner), F32)],
        scratch_shapes=[pltpu.VMEM((gw, n), F32)],
        compiler_params=_params(("arbitrary", "arbitrary")),
    )(xbc, xbc, xbc, zx, dtg, bias_g, alog_g, d_chan, ngain, _tri(L, "row_ge_col"), _tri(L, "row_le_col"),
      yfull, hp, dyn)


def _spread_dt(w_dt_t, G, hpg):
    K = w_dt_t.shape[1]
    w = w_dt_t.reshape(G, hpg, K)
    return jnp.pad(w, ((0, 0), (0, LANES - hpg), (0, 0))).reshape(G * LANES, K)


def _group_vec(v, G, hpg):
    return jnp.pad(v.reshape(G, 1, hpg), ((0, 0), (0, 0), (0, LANES - hpg)))


def local_step(x, target, W, late=None):
    S, D = x.shape
    depth = W["mix_norm"].shape[0]
    gm_groups, gm_chunk = W["gm_w_s"].shape[1], W["gm_w_s"].shape[2]
    heads = W["ssm_dt_bias"].shape[1]
    inner = heads * SB_HEAD_DIM
    L = gm_chunk
    received = None

    saved = []
    for i in range(depth):
        kind, j = i % 3, i // 3
        s = dict(x=x)
        h = rms_fwd(x, W["mix_norm"][i:i + 1], name="rms_mix_fwd")
        s["h"] = h
        if kind == 0:
            qkv = matmul(h, W["sb_w_qkv"][j], tb=True, name="mm_qkv")
            if late is not None and i == 0:
                o, gathered = sb_attn_fwd(qkv, W["sb_q_gain"][j:j + 1], W["sb_k_gain"][j:j + 1], name="sb_fwd_gather",
                                          gather=late.shards)
                late.fill(W, gathered)
            else:
                o = sb_attn_fwd(qkv, W["sb_q_gain"][j:j + 1], W["sb_k_gain"][j:j + 1], name="sb_fwd")
            x1 = matmul(o, W["sb_w_o"][j], residual=x, name="mm_sb_out")
            s.update(qkv=qkv, o=o)
        elif kind == 1:
            wc = jnp.where(jnp.tril(jnp.ones((gm_chunk, gm_chunk), bool)), W["gm_w_s"][j], 0.0).astype(BF16)
            bsf = jnp.broadcast_to(W["gm_b_s"][j][:, :, None], (gm_groups, gm_chunk, W["gm_v_gain"].shape[1] // gm_groups)).astype(F32)
            zzpre = matmul(h, W["gm_w_in"][j], tb=True, name="mm_gm_in")
            p = gmlp_fwd(zzpre, W["gm_b_in"][j:j + 1], W["gm_v_gain"][j:j + 1], wc, bsf, name="gm_fwd")
            x1 = matmul(p, W["gm_w_out"][j], residual=x, name="mm_gm_out")
            s.update(zzpre=zzpre, p=p, wc=wc, bsf=bsf)
        else:
            conv_dim = W["ssm_conv_w"].shape[2]
            G = (conv_dim - inner) // (2 * SSM_STATE)
            hpg = heads // G
            w_in = W["ssm_w_in"][j]
            w_zx = w_in[:inner + conv_dim]
            w_dtg = _spread_dt(w_in[inner + conv_dim:], G, hpg)
            bias_g = _group_vec(W["ssm_dt_bias"][j], G, hpg)
            alog_g = _group_vec(W["ssm_a_log"][j], G, hpg)
            d_chan = jnp.repeat(W["ssm_d"][j], SB_HEAD_DIM)[None, :]
            ngain = W["ssm_norm_gain"][j:j + 1]
            zx = matmul(h, w_zx, tb=True, name="mm_ssm_zx")
            dtg = matmul(h, w_dtg, tb=True, name="mm_ssm_dt")
            xbc = conv_fwd(zx, W["ssm_conv_w"][j], W["ssm_conv_b"][j:j + 1], inner, name="conv_fwd")
            yn, yfull, hp = ssd_fwd(xbc, zx, dtg, bias_g, alog_g, d_chan, ngain, L, G, name="ssd_fwd")
            x1 = matmul(yn, W["ssm_w_out"][j], residual=x, name="mm_ssm_out")
            s.update(w_zx=w_zx, w_dtg=w_dtg, bias_g=bias_g, alog_g=alog_g, d_chan=d_chan, ngain=ngain,
                     zx=zx, dtg=dtg, xbc=xbc, yn=yn, yfull=yfull, hp=hp)
        h2 = rms_fwd(x1, W["ffn_norm"][i:i + 1], name="rms_ffn_fwd")
        gu, a = ffn_up_fwd(h2, W["ffn_w_gu"][i], name="ffn_up_fwd")
        x2 = matmul(a, W["ffn_w_down"][i], residual=x1, name="mm_ffn_down")
        s.update(x1=x1, h2=h2, gu=gu, a=a)
        saved.append(s)
        x = x2

    dx, loss = loss_head(x, target, name="loss_head")

    gw = {k: {} for k in WEIGHTS}
    for i in reversed(range(depth)):
        kind, j = i % 3, i // 3
        s = saved[i]
        gw["ffn_w_down"][i] = matmul(s["a"], dx, ta=True, out_dtype=BF16, name="mm_ffn_dwdown")
        dgu = ffn_up_bwd(dx, W["ffn_w_down"][i], s["gu"], name="ffn_up_bwd")
        dh2 = matmul(dgu, W["ffn_w_gu"][i], a_split=2, name="mm_ffn_dh")
        gw["ffn_w_gu"][i] = matmul(dgu, s["h2"], ta=True, a_split=2, out_dtype=BF16, name="mm_ffn_dwgu")
        dx1, dgn = rms_bwd(s["x1"], W["ffn_norm"][i:i + 1], dh2, dx, name="rms_ffn_bwd")
        gw["ffn_norm"][i] = dgn[0]
        if kind == 0:
            do = matmul(dx1, W["sb_w_o"][j], tb=True, name="mm_sb_do")
            gw["sb_w_o"][j] = matmul(s["o"], dx1, ta=True, out_dtype=BF16, name="mm_sb_dwo")
            if late is not None and i == 0:
                dqkv, dqg, dkg, received = sb_attn_bwd(
                    s["qkv"], s["o"], do, W["sb_q_gain"][j:j + 1], W["sb_k_gain"][j:j + 1], name="sb_bwd_scatter",
                    scatter=late.contributions(gw))
            else:
                dqkv, dqg, dkg = sb_attn_bwd(s["qkv"], s["o"], do, W["sb_q_gain"][j:j + 1], W["sb_k_gain"][j:j + 1],
                                             name="sb_bwd")
            gw["sb_q_gain"][j] = dqg[0]
            gw["sb_k_gain"][j] = dkg[0]
            dh = matmul(dqkv, W["sb_w_qkv"][j], a_split=3, name="mm_sb_dh")
            gw["sb_w_qkv"][j] = matmul(dqkv, s["h"], ta=True, a_split=3, out_dtype=BF16, name="mm_sb_dwqkv")
        elif kind == 1:
            dp = matmul(dx1, W["gm_w_out"][j], tb=True, name="mm_gm_dp")
            gw["gm_w_out"][j] = matmul(s["p"], dx1, ta=True, out_dtype=BF16, name="mm_gm_dwout")
            dzz, db_in, dvg, dws, dbs = gmlp_bwd(s["zzpre"], W["gm_b_in"][j:j + 1], W["gm_v_gain"][j:j + 1],
                                                s["wc"], s["bsf"], dp, name="gm_bwd")
            gw["gm_b_in"][j] = db_in[0]
            gw["gm_v_gain"][j] = dvg[0]
            gw["gm_w_s"][j] = dws
            gw["gm_b_s"][j] = dbs[:, :gm_groups].T
            dh = matmul(dzz, W["gm_w_in"][j], name="mm_gm_dh")
            gw["gm_w_in"][j] = matmul(dzz, s["h"], ta=True, out_dtype=BF16, name="mm_gm_dwin")
        else:
            conv_dim = W["ssm_conv_w"].shape[2]
            G = (conv_dim - inner) // (2 * SSM_STATE)
            hpg = heads // G
            dyn = matmul(dx1, W["ssm_w_out"][j], tb=True, name="mm_ssm_dyn")
            gw["ssm_w_out"][j] = matmul(s["yn"], dx1, ta=True, out_dtype=BF16, name="mm_ssm_dwout")
            dz, dxs, dbm, dcm, ddt, dbias, dalog, dd, dng = ssd_bwd(
                s["xbc"], s["zx"], s["dtg"], s["bias_g"], s["alog_g"], s["d_chan"], s["ngain"], s["yfull"], s["hp"],
                dyn, L, G, name="ssd_bwd")
            dpre, dcw, dcb = conv_bwd(s["zx"], W["ssm_conv_w"][j], W["ssm_conv_b"][j:j + 1], inner, [dxs, dbm, dcm],
                                      name="conv_bwd")
            dzx = jnp.concatenate([dz, dpre], axis=1)
            dh = matmul(ddt, s["w_dtg"], name="mm_ssm_dh_dt")
            dh = matmul(dzx, s["w_zx"], residual=dh, name="mm_ssm_dh")
            dw_zx = matmul(dzx, s["h"], ta=True, out_dtype=BF16, name="mm_ssm_dwzx")
            dw_dtg = matmul(ddt, s["h"], ta=True, out_dtype=BF16, name="mm_ssm_dwdt")
            dw_dt = dw_dtg.reshape(G, LANES, D)[:, :hpg, :].reshape(heads, D)
            gw["ssm_w_in"][j] = jnp.concatenate([dw_zx, dw_dt], axis=0)
            gw["ssm_conv_w"][j] = dcw
            gw["ssm_conv_b"][j] = dcb[0]
            gw["ssm_dt_bias"][j] = dbias[:, 0, :hpg].reshape(heads)
            gw["ssm_a_log"][j] = dalog[:, 0, :hpg].reshape(heads)
            gw["ssm_d"][j] = dd[:, 0, :hpg].reshape(heads)
            gw["ssm_norm_gain"][j] = dng[0]
        dx, dgn = rms_bwd(s["x"], W["mix_norm"][i:i + 1], dh, dx1, name="rms_mix_bwd")
        gw["mix_norm"][i] = dgn[0]

    return loss, dx, gw, received


MESH = pl.DeviceIdType.MESH
HBM_SPEC = pl.BlockSpec(memory_space=pltpu.HBM)
VMEM_SPEC = pl.BlockSpec(memory_space=pltpu.VMEM)


def _my_position():
    return lax.axis_index("x"), lax.axis_index("y"), lax.axis_index("c")


def _flip(v, bit):
    return 1 - v if bit else v


def all_gather(shards, *, name):
    n = len(shards)

    def body(*refs):
        for phase in ("start", "forward", "finish"):
            _ag_phase(refs[:n], refs[n:2 * n], refs[2 * n:], phase)

    return pl.pallas_call(
        body, name=name, out_shape=[jax.ShapeDtypeStruct((N_DEV,) + s.shape, s.dtype) for s in shards],
        in_specs=[HBM_SPEC] * n, out_specs=[HBM_SPEC] * n, scratch_shapes=_copy_semaphores(n),
    )(*shards)


def _ag_phase(x_refs, out_refs, sems, phase):
    send_sems, recv_sems, local_sems = sems
    x, y, c = _my_position()
    me, sibling = (x, y, c), (x, y, 1 - c)
    chips = [(1 - x, y), (x, 1 - y), (1 - x, 1 - y)]
    for p, (x_ref, out_ref) in enumerate(zip(x_refs, out_refs)):
        def slot(px, py, pc):
            return out_ref.at[4 * px + 2 * py + pc]

        def copy(k, block, to, src=None):
            return pltpu.make_async_remote_copy(
                src_ref=slot(*block) if src is None else src, dst_ref=slot(*block),
                send_sem=send_sems.at[7 * p + k], recv_sem=recv_sems.at[7 * p + k], device_id=to, device_id_type=MESH)

        mine = pltpu.make_async_copy(x_ref, slot(*me), local_sems.at[p])
        first = [copy(0, me, sibling, src=x_ref)]
        first += [copy(1 + j, me, (*chip, c), src=x_ref) for j, chip in enumerate(chips)]
        passed = [copy(4 + j, (*chip, c), sibling) for j, chip in enumerate(chips)]
        if phase == "start":
            mine.start()
            for cp in first:
                cp.start()
        elif phase == "forward":
            for j, chip in enumerate(chips):
                copy(1 + j, (*chip, c), me).wait_recv()
                passed[j].start()
        else:
            copy(0, sibling, me).wait_recv()
            for j, chip in enumerate(chips):
                copy(4 + j, (*chip, 1 - c), me).wait_recv()
            for cp in first + passed:
                cp.wait_send()
            mine.wait()


def _copy_semaphores(n):
    return [pltpu.SemaphoreType.DMA((7 * n,)), pltpu.SemaphoreType.DMA((7 * n,)), pltpu.SemaphoreType.DMA((n,))]


def _rs_phase(g_refs, out_refs, sems, phase):
    send_sems, recv_sems, local_sems = sems
    x, y, c = _my_position()
    me = 4 * x + 2 * y + c
    copies = []
    for p, (g_ref, out_ref) in enumerate(zip(g_refs, out_refs)):
        copies.append(pltpu.make_async_copy(g_ref.at[me], out_ref.at[me], local_sems.at[p]))
        for k in range(1, N_DEV):
            px, py, pc = _flip(x, k & 4), _flip(y, k & 2), _flip(c, k & 1)
            copies.append(pltpu.make_async_remote_copy(
                src_ref=g_ref.at[4 * px + 2 * py + pc], dst_ref=out_ref.at[me],
                send_sem=send_sems.at[7 * p + k - 1], recv_sem=recv_sems.at[7 * p + k - 1],
                device_id=(px, py, pc), device_id_type=MESH))
    for cp in copies:
        if phase == "start":
            cp.start()
        else:
            cp.wait()


def sum_slots(recv, *, name):
    n, R, C = recv.shape
    tr = _pick(R, (512, 256, 128))

    def body(r_ref, o_ref):
        acc = r_ref[0].astype(F32)
        for s in range(1, n):
            acc = acc + r_ref[s].astype(F32)
        o_ref[...] = acc

    return pl.pallas_call(
        body, name=name, grid=(R // tr,), in_specs=[pl.BlockSpec((n, tr, C), lambda i: (0, i, 0))],
        out_specs=pl.BlockSpec((tr, C), lambda i: (i, 0)), out_shape=jax.ShapeDtypeStruct((R, C), F32),
        compiler_params=_params(("parallel",)),
    )(recv)


def all_reduce_small(vs, scatter, *, name):
    n, ns = len(vs), len(scatter)

    def body(*refs):
        v_refs, g_refs = refs[:n], refs[n:n + ns]
        o_refs, r_refs = refs[n + ns:2 * n + ns], refs[2 * n + ns:2 * n + 2 * ns]
        bufs = refs[2 * n + 2 * ns:3 * n + 2 * ns]
        send_sems, recv_sems = refs[3 * n + 2 * ns:3 * n + 2 * ns + 2]
        rs_sems = refs[3 * n + 2 * ns + 2:]
        x, y, c = _my_position()
        me = 4 * x + 2 * y + c
        _rs_phase(g_refs, r_refs, rs_sems, "start")
        copies = []
        for p, (v_ref, buf) in enumerate(zip(v_refs, bufs)):
            buf[me] = v_ref[...]
            for k in range(1, N_DEV):
                px, py, pc = _flip(x, k & 4), _flip(y, k & 2), _flip(c, k & 1)
                copies.append(pltpu.make_async_remote_copy(
                    src_ref=v_ref, dst_ref=buf.at[me], send_sem=send_sems.at[7 * p + k - 1],
                    recv_sem=recv_sems.at[7 * p + k - 1], device_id=(px, py, pc), device_id_type=MESH))
        for cp in copies:
            cp.start()
        for cp in copies:
            cp.wait()
        for o_ref, buf in zip(o_refs, bufs):
            acc = buf[0]
            for s in range(1, N_DEV):
                acc = acc + buf[s]
            o_ref[...] = acc
        _rs_phase(g_refs, r_refs, rs_sems, "finish")

    out = pl.pallas_call(
        body, name=name,
        out_shape=[jax.ShapeDtypeStruct(v.shape, F32) for v in vs] + [jax.ShapeDtypeStruct(g.shape, g.dtype) for g in scatter],
        in_specs=[VMEM_SPEC] * n + [HBM_SPEC] * ns, out_specs=[VMEM_SPEC] * n + [HBM_SPEC] * ns,
        scratch_shapes=[pltpu.VMEM((N_DEV,) + v.shape, F32) for v in vs]
        + [pltpu.SemaphoreType.DMA((7 * n,)), pltpu.SemaphoreType.DMA((7 * n,))] + _copy_semaphores(ns),
        compiler_params=pltpu.CompilerParams(vmem_limit_bytes=VMEM_LIMIT_BYTES),
    )(*vs, *scatter)
    return list(out[:n]), list(out[n:])


def _pad_rows(a, mult):
    pad = (-a.shape[0]) % mult
    return jnp.pad(a, ((0, pad), (0, 0))) if pad else a


def _pack_small(arrays):
    flat = []
    for a in arrays:
        f = a.reshape(-1).astype(F32)
        flat.append(jnp.pad(f, (0, (-f.shape[0]) % LANES)))
    return _pad_rows(jnp.concatenate(flat).reshape(-1, LANES), 8)


def _unpack_small(packed, shapes):
    flat = packed.reshape(-1)
    out, r = [], 0
    for shp in shapes:
        n = math.prod(shp)
        out.append(flat[r:r + n].reshape(shp))
        r += n + (-n) % LANES
    return out


ARG_NAMES = ("x",) + WEIGHTS + ("loss_target",) + tuple("m_" + w for w in WEIGHTS) + tuple("v_" + w for w in WEIGHTS)


def kernel(x, mix_norm, ffn_norm, sb_w_qkv, sb_q_gain, sb_k_gain, sb_w_o, gm_w_in, gm_b_in, gm_v_gain, gm_w_s, gm_b_s, gm_w_out, ssm_w_in, ssm_conv_w, ssm_conv_b, ssm_dt_bias, ssm_a_log, ssm_d, ssm_norm_gain, ssm_w_out, ffn_w_gu, ffn_w_down, loss_target, m_mix_norm, m_ffn_norm, m_sb_w_qkv, m_sb_q_gain, m_sb_k_gain, m_sb_w_o, m_gm_w_in, m_gm_b_in, m_gm_v_gain, m_gm_w_s, m_gm_b_s, m_gm_w_out, m_ssm_w_in, m_ssm_conv_w, m_ssm_conv_b, m_ssm_dt_bias, m_ssm_a_log, m_ssm_d, m_ssm_norm_gain, m_ssm_w_out, m_ffn_w_gu, m_ffn_w_down, v_mix_norm, v_ffn_norm, v_sb_w_qkv, v_sb_q_gain, v_sb_k_gain, v_sb_w_o, v_gm_w_in, v_gm_b_in, v_gm_v_gain, v_gm_w_s, v_gm_b_s, v_gm_w_out, v_ssm_w_in, v_ssm_conv_w, v_ssm_conv_b, v_ssm_dt_bias, v_ssm_a_log, v_ssm_d, v_ssm_norm_gain, v_ssm_w_out, v_ffn_w_gu, v_ffn_w_down):
    given = dict(zip(ARG_NAMES, (x, mix_norm, ffn_norm, sb_w_qkv, sb_q_gain, sb_k_gain, sb_w_o, gm_w_in, gm_b_in, gm_v_gain, gm_w_s, gm_b_s, gm_w_out, ssm_w_in, ssm_conv_w, ssm_conv_b, ssm_dt_bias, ssm_a_log, ssm_d, ssm_norm_gain, ssm_w_out, ffn_w_gu, ffn_w_down, loss_target, m_mix_norm, m_ffn_norm, m_sb_w_qkv, m_sb_q_gain, m_sb_k_gain, m_sb_w_o, m_gm_w_in, m_gm_b_in, m_gm_v_gain, m_gm_w_s, m_gm_b_s, m_gm_w_out, m_ssm_w_in, m_ssm_conv_w, m_ssm_conv_b, m_ssm_dt_bias, m_ssm_a_log, m_ssm_d, m_ssm_norm_gain, m_ssm_w_out, m_ffn_w_gu, m_ffn_w_down, v_mix_norm, v_ffn_norm, v_sb_w_qkv, v_sb_q_gain, v_sb_k_gain, v_sb_w_o, v_gm_w_in, v_gm_b_in, v_gm_v_gain, v_gm_w_s, v_gm_b_s, v_gm_w_out, v_ssm_w_in, v_ssm_conv_w, v_ssm_conv_b, v_ssm_dt_bias, v_ssm_a_log, v_ssm_d, v_ssm_norm_gain, v_ssm_w_out, v_ffn_w_gu, v_ffn_w_down)))
    mx, my, mc = _my_position()
    me = 4 * mx + 2 * my + mc

    pieces = [(k, l) for k in BIG for l in range(given[k].shape[0])]
    early = [("sb_w_qkv", 0)]
    late_pieces = [p for p in pieces if p not in early]
    last = [("sb_w_qkv", 0)]
    main = [p for p in pieces if p not in last]

    def shards_of(ps):
        return [(given[k][l].T if k in COL_SHARDED else given[k][l]).astype(BF16) for k, l in ps]

    def piece_to_full(g, k):
        rows, cols = given[k].shape[1:]
        return g.reshape(N_DEV * cols, rows) if k in COL_SHARDED else g.reshape(N_DEV * rows, cols)

    def full_to_piece(full, k):
        return full.reshape(N_DEV, -1, PACK_COLS)

    def summed_to_shard(g, k):
        rows, cols = given[k].shape[1:]
        return g.reshape(cols, rows) if k in COL_SHARDED else g.reshape(rows, cols)

    def contributions(gw, ps):
        return [full_to_piece(gw[k][l], k) for k, l in ps]

    sharded_small = [lax.bitcast_convert_type(given[k], BF16) for k in SMALL_SHARDED]
    tail = jnp.concatenate([a.reshape(-1) for a in sharded_small])
    tail = jnp.pad(tail, (0, (-tail.size) % PACK_COLS)).reshape(-1, PACK_COLS)

    W = {k: given[k] for k in SMALL if k not in SMALL_SHARDED}
    W.update({k: [None] * given[k].shape[0] for k in BIG})
    for (k, l), g in zip(early, all_gather(shards_of(early), name="all_gather_early")):
        W[k][l] = piece_to_full(g, k)

    class Late:
        shards = shards_of(late_pieces) + [tail]

        @staticmethod
        def fill(weights, gathered):
            for (k, l), g in zip(late_pieces, gathered):
                weights[k][l] = piece_to_full(g, k)
            tail_g = gathered[-1].reshape(N_DEV, -1)
            off = 0
            for k, a in zip(SMALL_SHARDED, sharded_small):
                g = lax.bitcast_convert_type(tail_g[:, off:off + a.size].reshape((N_DEV,) + a.shape), F32)
                weights[k] = jnp.moveaxis(g, 0, -2).reshape(g.shape[1:-1] + (N_DEV * g.shape[-1],))
                off += a.size

        @staticmethod
        def contributions(gw):
            return contributions(gw, main)

    loss, gx, gw, received_main = local_step(given["x"][0], given["loss_target"][0], W, late=Late)

    grads_small = {k: jnp.stack([gw[k][l] for l in sorted(gw[k])], axis=0) for k in SMALL}
    packed_names = tuple(k for k in SMALL if k != "gm_w_s")
    small_shapes = [grads_small[k].shape for k in packed_names] + [(1, 1)]
    (red_ws, red_rest), received_last = all_reduce_small(
        [grads_small["gm_w_s"].reshape(-1, LANES), _pack_small([grads_small[k] for k in packed_names] + [loss])],
        contributions(gw, last), name="all_reduce_small_and_last_exchange")
    small_full = dict(zip(packed_names + ("loss",), _unpack_small(red_rest, small_shapes)))
    small_full["gm_w_s"] = red_ws.reshape(grads_small["gm_w_s"].shape)

    g_piece = {}
    for grp, received in ((main, received_main), (last, received_last)):
        for p, r in zip(grp, received):
            g_piece[p] = summed_to_shard(sum_slots(r, name="reduce_scatter_sum"), p[0])
    out_g, out_d, out_m, out_v = {}, {}, {}, {}
    for k in BIG:
        swap = lambda a: jnp.swapaxes(a, -1, -2)
        keep_t = k in COL_SHARDED and given[k].shape[-1] % LANES != 0
        flip = swap if keep_t else (lambda a: a)
        g = jnp.stack([g_piece[(k, l)] for l in range(given[k].shape[0])], axis=0)
        if k in COL_SHARDED and not keep_t:
            g = swap(g)
        res = adamw(flip(given[k]), g, flip(given["m_" + k]), flip(given["v_" + k]), name="adamw_" + k)
        out_g[k], out_d[k], out_m[k], out_v[k] = (flip(a) for a in (g,) + tuple(res))

    gsmall = {}
    for k in SMALL:
        g = small_full[k]
        if k in SMALL_SHARDED:
            n = given[k].shape[-1]
            g = lax.dynamic_slice_in_dim(g, me * n, n, axis=g.ndim - 1)
        gsmall[k] = g
    local_shapes = [given[k].shape for k in packed_names]
    dsm, nmsm, nvsm = adamw(*[_pack_small([src[k] for k in packed_names]) for src in (
        given, gsmall, {k: given["m_" + k] for k in packed_names}, {k: given["v_" + k] for k in packed_names})],
        name="adamw_small")
    out_g.update(gsmall)
    for dst, src in ((out_d, dsm), (out_m, nmsm), (out_v, nvsm)):
        dst.update(zip(packed_names, _unpack_small(src, local_shapes)))
    ws_shape = given["gm_w_s"].shape
    out_d["gm_w_s"], out_m["gm_w_s"], out_v["gm_w_s"] = (a.reshape(ws_shape) for a in adamw(
        *[a.reshape((-1,) + ws_shape[-2:]) for a in (given["gm_w_s"], gsmall["gm_w_s"], given["m_gm_w_s"], given["v_gm_w_s"])],
        name="adamw_gm_w_s"))

    return (small_full["loss"].reshape(()), gx[None],
            *[out_g[k] for k in WEIGHTS], *[out_d[k] for k in WEIGHTS],
            *[out_m[k] for k in WEIGHTS], *[out_v[k] for k in WEIGHTS])
```

```python
import math

import jax
import jax.numpy as jnp
from jax import lax
from jax.experimental import pallas as pl
from jax.experimental.pallas import tpu as pltpu

F32 = jnp.float32
BF16 = jnp.bfloat16
EPS = 1e-6
N_DEV = 8
SB_HEAD_DIM = 64
SB_TILE = 256
SB_FWD_QUERY_BLOCKS = 4
SB_BWD_QUERY_BLOCKS = 2
SSM_STATE = 128
SSD_GROUPS_PER_STEP = 2
SSM_CONV = 4
ADAM_LR = 0.001
ADAM_B1 = 0.9
ADAM_B2 = 0.999
ADAM_EPS = 1e-08
ADAM_WD = 0.01
ADAM_STEP = 10
VMEM_LIMIT_BYTES = 56 * 1024 * 1024
MATMUL_VMEM_BUDGET = 40 * 1024 * 1024
LANES = 128
PACK_COLS = 1024

BIG = ("sb_w_qkv", "sb_w_o", "gm_w_in", "gm_w_out", "ssm_w_in", "ssm_w_out", "ffn_w_gu", "ffn_w_down")
COL_SHARDED = ("sb_w_qkv", "gm_w_in", "ssm_w_in", "ffn_w_gu")
SMALL = ("mix_norm", "ffn_norm", "sb_q_gain", "sb_k_gain", "gm_b_in", "gm_v_gain", "gm_w_s", "gm_b_s",
         "ssm_conv_w", "ssm_conv_b", "ssm_dt_bias", "ssm_a_log", "ssm_d", "ssm_norm_gain")
SMALL_SHARDED = ("ssm_conv_w", "ssm_conv_b", "ssm_norm_gain")
WEIGHTS = ("mix_norm", "ffn_norm", "sb_w_qkv", "sb_q_gain", "sb_k_gain", "sb_w_o", "gm_w_in", "gm_b_in",
           "gm_v_gain", "gm_w_s", "gm_b_s", "gm_w_out", "ssm_w_in", "ssm_conv_w", "ssm_conv_b", "ssm_dt_bias",
           "ssm_a_log", "ssm_d", "ssm_norm_gain", "ssm_w_out", "ffn_w_gu", "ffn_w_down")


def _params(semantics=None):
    return pltpu.CompilerParams(dimension_semantics=semantics, vmem_limit_bytes=VMEM_LIMIT_BYTES)


def _pick(n, prefs):
    for t in prefs:
        if t <= n and n % t == 0:
            return t
    return n


def _dot(a, b, ca=1, cb=0):
    return lax.dot_general(a, b, (((ca,), (cb,)), ((), ())), preferred_element_type=F32)


def _split3(v):
    h1 = v.astype(BF16)
    r1 = v - h1.astype(F32)
    h2 = r1.astype(BF16)
    h3 = (r1 - h2.astype(F32)).astype(BF16)
    return h1, h2, h3


def _dot_exact_left(mat01, v):
    h1, h2, h3 = _split3(v)
    return _dot(mat01, h1) + _dot(mat01, h2) + _dot(mat01, h3)


def _sum_all(v):
    return jnp.sum(jnp.sum(v, axis=0, keepdims=True), axis=1, keepdims=True)


def _sigmoid(v):
    return 1.0 / (1.0 + jnp.exp(-v))


def _softplus(v):
    return jnp.maximum(v, 0.0) + jnp.log(1.0 + jnp.exp(-jnp.abs(v)))


def _erf(v):
    a = jnp.abs(v)
    t = 1.0 / (1.0 + 0.3275911 * a)
    poly = t * (0.254829592 + t * (-0.284496736 + t * (1.421413741 + t * (-1.453152027 + t * 1.061405429))))
    e = 1.0 - poly * jnp.exp(-a * a)
    return jnp.where(v < 0, -e, e)


def _gelu_and_grad(v):
    cdf = 0.5 * (1.0 + _erf(v * (1.0 / math.sqrt(2.0))))
    pdf = jnp.exp(-0.5 * v * v) * (1.0 / math.sqrt(2.0 * math.pi))
    return v * cdf, cdf + v * pdf


def matmul(a, b, *, ta=False, tb=False, out_dtype=F32, residual=None, a_split=1, b_split=1, name):
    if a_split > 1 and ta:
        assert a.shape[0] == a_split
        K, M = a.shape[1], a_split * a.shape[2]
    elif a_split > 1:
        assert a.shape[0] == a_split
        M, K = a.shape[1], a_split * a.shape[2]
    elif ta:
        K, M = a.shape
    else:
        M, K = a.shape
    if b_split > 1:
        assert not tb and b.shape[0] == b_split
        Kb, N = b.shape[1], b_split * b.shape[2]
    elif tb:
        N, Kb = b.shape
    else:
        Kb, N = b.shape
    assert K == Kb, (a.shape, b.shape, ta, tb)
    has_res = residual is not None
    tm = _pick(M // a_split if ta else M, (1024, 1408, 768, 512, 256, 128))
    tn = _pick(N // b_split, (1024, 1408, 1536, 768, 512, 256, 128))

    def vmem_bytes(tk):
        tiles = tm * tk * a.dtype.itemsize + tk * tn * b.dtype.itemsize
        outs = tm * tn * jnp.dtype(out_dtype).itemsize + (tm * tn * 4 if has_res else 0)
        return 2 * tiles + 2 * outs + (tm * tn * 4 if tk < K else 0)

    kp = K if ta else K // a_split
    tk = next((t for t in (K, 2048, 1408, 1024, 512, 256) if t <= kp and kp % t == 0 and vmem_bytes(t) <= MATMUL_VMEM_BUDGET),
              _pick(kp, (128,)))
    nk = K // tk
    if a_split > 1 and ta:
        nib = M // a_split // tm
        a_spec = pl.BlockSpec((None, tk, tm), lambda i, j, k: (i // nib, k, i % nib))
    elif a_split > 1:
        nkb = kp // tk
        a_spec = pl.BlockSpec((None, tm, tk), lambda i, j, k: (k // nkb, i, k % nkb))
    else:
        a_spec = pl.BlockSpec((tk, tm), lambda i, j, k: (k, i)) if ta else pl.BlockSpec((tm, tk), lambda i, j, k: (i, k))
    if b_split > 1:
        njb = N // b_split // tn
        b_spec = pl.BlockSpec((None, tk, tn), lambda i, j, k: (j // njb, k, j % njb))
    else:
        b_spec = pl.BlockSpec((tn, tk), lambda i, j, k: (j, k)) if tb else pl.BlockSpec((tk, tn), lambda i, j, k: (k, j))
    o_spec = pl.BlockSpec((tm, tn), lambda i, j, k: (i, j))
    ca, cb = (0 if ta else 1), (1 if tb else 0)

    def body(*refs):
        a_ref, b_ref = refs[:2]
        r_ref = refs[2] if has_res else None
        o_ref = refs[3] if has_res else refs[2]

        def finish(r):
            if has_res:
                r = r + r_ref[...]
            o_ref[...] = r.astype(out_dtype)

        def part():
            return _dot(a_ref[...].astype(BF16), b_ref[...].astype(BF16), ca, cb)

        if nk == 1:
            finish(part())
            return
        acc = refs[-1]
        k = pl.program_id(2)

        @pl.when(k == 0)
        def _():
            acc[...] = part()

        @pl.when(jnp.logical_and(k > 0, k < nk - 1))
        def _():
            acc[...] += part()

        @pl.when(k == nk - 1)
        def _():
            finish(acc[...] + part())

    in_specs = [a_spec, b_spec] + ([o_spec] if has_res else [])
    args = (a, b) + ((residual,) if has_res else ())
    return pl.pallas_call(
        body, name=name, grid=(M // tm, N // tn, nk), in_specs=in_specs, out_specs=o_spec,
        out_shape=jax.ShapeDtypeStruct((M, N), out_dtype),
        scratch_shapes=[pltpu.VMEM((tm, tn), F32)] if nk > 1 else [],
        compiler_params=_params(("parallel", "parallel", "arbitrary")),
    )(*args)


def rms_fwd(x, gain, *, name):
    S, D = x.shape
    tr = _pick(S, (512, 256, 128))

    def body(x_ref, g_ref, o_ref):
        xv = x_ref[...]
        r = lax.rsqrt(jnp.mean(xv * xv, axis=1, keepdims=True) + EPS)
        o_ref[...] = (xv * r * g_ref[...]).astype(BF16)

    return pl.pallas_call(
        body, name=name, grid=(S // tr,),
        in_specs=[pl.BlockSpec((tr, D), lambda i: (i, 0)), pl.BlockSpec((1, D), lambda i: (0, 0))],
        out_specs=pl.BlockSpec((tr, D), lambda i: (i, 0)), out_shape=jax.ShapeDtypeStruct((S, D), BF16),
        compiler_params=_params(("parallel",)),
    )(x, gain)


def rms_bwd(x, gain, dh, dres, *, name):
    S, D = x.shape
    tr = _pick(S, (512, 256, 128))

    def body(x_ref, g_ref, dh_ref, dr_ref, dx_ref, dg_ref):
        @pl.when(pl.program_id(0) == 0)
        def _():
            dg_ref[...] = jnp.zeros_like(dg_ref)

        xv = x_ref[...]
        dhv = dh_ref[...]
        r = lax.rsqrt(jnp.mean(xv * xv, axis=1, keepdims=True) + EPS)
        xhat = xv * r
        t = dhv * g_ref[...]
        dx_ref[...] = dr_ref[...] + r * (t - xhat * jnp.mean(xhat * t, axis=1, keepdims=True))
        dg_ref[...] += jnp.sum(dhv * xhat, axis=0, keepdims=True)

    row = pl.BlockSpec((tr, D), lambda i: (i, 0))
    vec = pl.BlockSpec((1, D), lambda i: (0, 0))
    return pl.pallas_call(
        body, name=name, grid=(S // tr,), in_specs=[row, vec, row, row], out_specs=[row, vec],
        out_shape=[jax.ShapeDtypeStruct((S, D), F32), jax.ShapeDtypeStruct((1, D), F32)],
        compiler_params=_params(("arbitrary",)),
    )(x, gain, dh, dres)


def ffn_up_fwd(h, w_gu_t, *, name):
    S, K = h.shape
    F = w_gu_t.shape[0] // 2
    tm = _pick(S, (512, 256, 128))
    tn = _pick(F, (1408, 1024, 768, 512, 256, 128))
    nj = F // tn

    def body(h_ref, wg_ref, wu_ref, gu_ref, a_ref):
        hv = h_ref[...]
        g = _dot(hv, wg_ref[...], 1, 1)
        u = _dot(hv, wu_ref[...], 1, 1)
        gu_ref[0] = g
        gu_ref[1] = u
        a_ref[...] = (g * _sigmoid(g) * u).astype(BF16)

    return pl.pallas_call(
        body, name=name, grid=(nj, S // tm),
        in_specs=[pl.BlockSpec((tm, K), lambda j, i: (i, 0)), pl.BlockSpec((tn, K), lambda j, i: (j, 0)),
                  pl.BlockSpec((tn, K), lambda j, i: (nj + j, 0))],
        out_specs=[pl.BlockSpec((2, tm, tn), lambda j, i: (0, i, j)), pl.BlockSpec((tm, tn), lambda j, i: (i, j))],
        out_shape=[jax.ShapeDtypeStruct((2, S, F), F32), jax.ShapeDtypeStruct((S, F), BF16)],
        compiler_params=_params(("parallel", "parallel")),
    )(h, w_gu_t, w_gu_t)


def ffn_up_bwd(dy, w_down, gu, *, name):
    S, D = dy.shape
    F = w_down.shape[0]
    tm = _pick(S, (512, 256, 128))
    tn = _pick(F, (1408, 1024, 768, 512, 256, 128))

    def body(dy_ref, wd_ref, gu_ref, o_ref):
        da = _dot(dy_ref[...].astype(BF16), wd_ref[...], 1, 1)
        g = gu_ref[0]
        u = gu_ref[1]
        s = _sigmoid(g)
        o_ref[0] = (da * u * (s * (1.0 + g * (1.0 - s)))).astype(BF16)
        o_ref[1] = (da * g * s).astype(BF16)

    pair = pl.BlockSpec((2, tm, tn), lambda j, i: (0, i, j))
    return pl.pallas_call(
        body, name=name, grid=(F // tn, S // tm),
        in_specs=[pl.BlockSpec((tm, D), lambda j, i: (i, 0)), pl.BlockSpec((tn, D), lambda j, i: (j, 0)), pair],
        out_specs=pair, out_shape=jax.ShapeDtypeStruct((2, S, F), BF16),
        compiler_params=_params(("parallel", "parallel")),
    )(dy, w_down, gu)


def loss_head(y, target, *, name):
    S, D = y.shape
    tr = _pick(S, (512, 256, 128))

    def body(y_ref, t_ref, dy_ref, l_ref):
        @pl.when(pl.program_id(0) == 0)
        def _():
            l_ref[...] = jnp.zeros_like(l_ref)

        err = y_ref[...] - t_ref[...]
        dy_ref[...] = err * (1.0 / D)
        l_ref[...] += jnp.sum(0.5 * jnp.mean(err * err, axis=1, keepdims=True), axis=0, keepdims=True)

    row = pl.BlockSpec((tr, D), lambda i: (i, 0))
    one = pl.BlockSpec((1, 1), lambda i: (0, 0))
    dy, l = pl.pallas_call(
        body, name=name, grid=(S // tr,), in_specs=[row, row], out_specs=[row, one],
        out_shape=[jax.ShapeDtypeStruct((S, D), F32), jax.ShapeDtypeStruct((1, 1), F32)],
        compiler_params=_params(("arbitrary",)),
    )(y, target)
    return dy, l


def adamw(w, g, m, v, *, name):
    R, C = w.shape[-2:]
    tr = _pick(R, (512, 256, 128, 64, 32, 16, 8))

    def body(w_ref, g_ref, m_ref, v_ref, d_ref, mo_ref, vo_ref):
        gv = g_ref[...]
        mn = ADAM_B1 * m_ref[...] + (1.0 - ADAM_B1) * gv
        vn = ADAM_B2 * v_ref[...] + (1.0 - ADAM_B2) * jnp.square(gv)
        m_hat = mn / (1.0 - ADAM_B1 ** ADAM_STEP)
        v_hat = vn / (1.0 - ADAM_B2 ** ADAM_STEP)
        d_ref[...] = -ADAM_LR * (m_hat / (jnp.sqrt(v_hat) + ADAM_EPS) + ADAM_WD * w_ref[...])
        mo_ref[...] = mn
        vo_ref[...] = vn

    tc = C if tr < R or C % LANES else _pick(C, (256, 128))
    if w.ndim == 3:
        grid = (w.shape[0], R // tr, C // tc)
        blk = pl.BlockSpec((None, tr, tc), lambda l, i, j: (l, i, j))
    else:
        grid = (R // tr, C // tc)
        blk = pl.BlockSpec((tr, tc), lambda i, j: (i, j))
    sds = jax.ShapeDtypeStruct(w.shape, F32)
    return pl.pallas_call(
        body, name=name, grid=grid, in_specs=[blk] * 4, out_specs=[blk] * 3, out_shape=[sds] * 3,
        compiler_params=_params(("parallel",) * len(grid)),
    )(w, g, m, v)


def _tri(n, kind):
    r = lax.broadcasted_iota(jnp.int32, (n, n), 0)
    c = lax.broadcasted_iota(jnp.int32, (n, n), 1)
    if kind == "row_gt_col":
        return (r > c).astype(BF16)
    if kind == "row_ge_col":
        return (r >= c).astype(BF16)
    if kind == "row_le_col":
        return (r <= c).astype(BF16)
    raise ValueError(kind)


def _sb_tile(qs, kj, r_carry, u_strict, masked):
    z = _dot(qs, kj, 1, 1)
    lb = jnp.minimum(z, 0.0) - jnp.log(1.0 + jnp.exp(-jnp.abs(z)))
    l1m = lb - z
    keep = None
    if masked:
        tq, tk = z.shape
        keep = lax.broadcasted_iota(jnp.int32, (tq, tk), 1) < lax.broadcasted_iota(jnp.int32, (tq, tk), 0)
        l1m = jnp.where(keep, l1m, 0.0)
    w = jnp.exp(lb + _dot(l1m.astype(BF16), u_strict) + r_carry)
    if masked:
        w = jnp.where(keep, w, 0.0)
    return lb, l1m, w, keep


def _sb_prep(T, nb, hd, refs_in, gains, scratch):
    q_scale = 1.0 / math.sqrt(hd)
    assert math.log2(q_scale) == round(math.log2(q_scale))

    def prep(i, _):
        rows = pl.ds(pl.multiple_of(i * T, T), T)
        for hh in range(2):
            sl = slice(hd * hh, hd * hh + hd)
            for n, (src, dst) in enumerate(zip(refs_in, scratch)):
                v = src[rows, sl]
                if n < 2:
                    v = v * lax.rsqrt(jnp.mean(v * v, axis=1, keepdims=True) + EPS) * gains[n][...]
                if n == 0:
                    v = v * q_scale
                dst[hh, rows, :] = v.astype(BF16)
        return 0

    lax.fori_loop(0, nb, prep, 0)


def _sb_chains(m, T, nq):
    rows = [pl.ds(pl.multiple_of((nq * m + qb) * T, T), T) for qb in range(nq)]
    return rows, [(hh, qb) for qb in range(nq) for hh in range(2)]


def _sb_sweep(tile, carry, chains, m, nq):
    for kk in reversed(range(nq)):
        carry = tile(nq * m + kk, carry, [(ch, ch[1] == kk) for ch in chains if ch[1] >= kk])
    return lax.fori_loop(0, nq * m, lambda jj, c: tile(nq * m - 1 - jj, c, [(ch, False) for ch in chains]), carry)


def sb_attn_fwd(qkv, q_gain, k_gain, *, name, gather=None):
    S, D3 = qkv.shape
    D = D3 // 3
    npairs = D // LANES
    hd = SB_HEAD_DIM
    T = min(SB_TILE, S)
    nb = S // T
    nq = SB_FWD_QUERY_BLOCKS
    assert nb % nq == 0

    def body(*refs):
        if gather is None:
            q_ref, k_ref, v_ref, qg_ref, kg_ref, us_ref, o_ref, qn_s, kn_s, vb_s = refs
        else:
            ng = len(gather)
            q_ref, k_ref, v_ref, qg_ref, kg_ref, us_ref = refs[:6]
            o_ref = refs[6 + ng]
            qn_s, kn_s, vb_s = refs[7 + 2 * ng:10 + 2 * ng]
            comm = (refs[6:6 + ng], refs[7 + ng:7 + 2 * ng], refs[10 + 2 * ng:])
            step = pl.program_id(0)
            pl.when(step == 0)(lambda: _ag_phase(*comm, "start"))
            pl.when(step == npairs - 1)(lambda: _ag_phase(*comm, "forward"))
        us = us_ref[...]
        _sb_prep(T, nb, hd, (q_ref, k_ref, v_ref), (qg_ref, kg_ref), (qn_s, kn_s, vb_s))

        def superblock(m, _):
            rows_q, chains = _sb_chains(m, T, nq)
            qs = {ch: qn_s[ch[0], rows_q[ch[1]], :] for ch in chains}

            def tile(j, carry, which):
                rows_j = pl.ds(pl.multiple_of(j * T, T), T)
                new = dict(carry)
                for ch, masked in which:
                    acc, rc = carry[ch]
                    _, l1m, w, _ = _sb_tile(qs[ch], kn_s[ch[0], rows_j, :], rc, us, masked)
                    new[ch] = (acc + _dot(w.astype(BF16), vb_s[ch[0], rows_j, :]),
                               rc + jnp.sum(l1m, axis=1, keepdims=True))
                return new

            carry = {ch: (jnp.zeros((T, hd), F32), jnp.zeros((T, 1), F32)) for ch in chains}
            carry = _sb_sweep(tile, carry, chains, m, nq)
            for qb in range(nq):
                o_ref[rows_q[qb], :] = jnp.concatenate([carry[(0, qb)][0], carry[(1, qb)][0]], axis=1)
            return 0

        lax.fori_loop(0, nb // nq, superblock, 0)
        if gather is not None:
            pl.when(step == npairs - 1)(lambda: _ag_phase(*comm, "finish"))

    col = lambda off: pl.BlockSpec((S, LANES), lambda p, off=off: (0, off + p))
    gain = pl.BlockSpec((1, hd), lambda p: (0, 0))
    in_specs = [col(0), col(npairs), col(2 * npairs), gain, gain, pl.BlockSpec((T, T), lambda p: (0, 0))]
    out_specs = [pl.BlockSpec((S, LANES), lambda p: (0, p))]
    out_shape = [jax.ShapeDtypeStruct((S, D), F32)]
    scratch = [pltpu.VMEM((2, S, hd), BF16)] * 3
    args = [qkv, qkv, qkv, q_gain, k_gain, _tri(T, "row_gt_col")]
    if gather is not None:
        in_specs += [HBM_SPEC] * len(gather)
        out_specs += [HBM_SPEC] * len(gather)
        out_shape += [jax.ShapeDtypeStruct((N_DEV,) + s.shape, s.dtype) for s in gather]
        scratch += _copy_semaphores(len(gather))
        args += list(gather)
    out = pl.pallas_call(
        body, name=name, grid=(npairs,), in_specs=in_specs, out_specs=out_specs, out_shape=out_shape,
        scratch_shapes=scratch, compiler_params=_params(("arbitrary",)),
    )(*args)
    return out[0] if gather is None else (out[0], list(out[1:]))


def sb_attn_bwd(qkv, o, do, q_gain, k_gain, *, name, scatter=None):
    S, D3 = qkv.shape
    D = D3 // 3
    npairs = D // LANES
    hd = SB_HEAD_DIM
    T = min(SB_TILE, S)
    nb = S // T
    nq = SB_BWD_QUERY_BLOCKS
    assert nb % nq == 0
    scale = 1.0 / math.sqrt(hd)

    def body(*refs):
        if scatter is None:
            (q_ref, k_ref, v_ref, o_ref, do_ref, qg_ref, kg_ref, us_ref,
             dqkv_ref, dg_ref, qn_s, kn_s, vb_s, dob_s) = refs
        else:
            ns = len(scatter)
            q_ref, k_ref, v_ref, o_ref, do_ref, qg_ref, kg_ref, us_ref = refs[:8]
            rs_in = refs[8:8 + ns]
            dqkv_ref, dg_ref = refs[8 + ns:10 + ns]
            rs_out = refs[10 + ns:10 + 2 * ns]
            qn_s, kn_s, vb_s, dob_s = refs[10 + 2 * ns:14 + 2 * ns]
            rs_sems = refs[14 + 2 * ns:]
            pl.when(pl.program_id(0) == 0)(lambda: _rs_phase(rs_in, rs_out, rs_sems, "start"))
        dq_ref, dk_ref, dv_ref = dqkv_ref.at[0], dqkv_ref.at[1], dqkv_ref.at[2]

        @pl.when(pl.program_id(0) == 0)
        def _():
            dg_ref[...] = jnp.zeros_like(dg_ref)

        us = us_ref[...]
        u_prefix = (1.0 - us.astype(F32)).astype(BF16)
        _sb_prep(T, nb, hd, (q_ref, k_ref, v_ref, do_ref), (qg_ref, kg_ref), (qn_s, kn_s, vb_s, dob_s))
        dk_ref[...] = jnp.zeros_like(dk_ref)
        dv_ref[...] = jnp.zeros_like(dv_ref)

        def superblock(m, _):
            rows_q, chains = _sb_chains(m, T, nq)
            qs = {ch: qn_s[ch[0], rows_q[ch[1]], :] for ch in chains}
            doi ={ch: dob_s[ch[0], rows_q[ch[1]], :] for ch in chains}
            dt_total = {ch: jnp.sum(doi[ch].astype(F32) * o_ref[rows_q[ch[1]], hd * ch[0]:hd * ch[0] + hd],
                                    axis=1, keepdims=True) for ch in chains}

            def tile(j, carry, which):
                rows_j = pl.ds(pl.multiple_of(j * T, T), T)
                new = dict(carry)
                dk_part, dv_part = {}, {}
                for ch, masked in which:
                    hh = ch[0]
                    dq_acc, rc, gc = carry[ch]
                    kj = kn_s[hh, rows_j, :]
                    lb, l1m, w, keep = _sb_tile(qs[ch], kj, rc, us, masked)
                    wb = w.astype(BF16)
                    g = _dot(doi[ch], vb_s[hh, rows_j, :], 1, 1) * wb.astype(F32)
                    g_row = jnp.sum(g, axis=1, keepdims=True)
                    g_upto = (dt_total[ch] - gc - g_row) + _dot(g.astype(BF16), u_prefix)
                    dz = g - g_upto * jnp.exp(lb)
                    if masked:
                        dz = jnp.where(keep, dz, 0.0)
                    dzb = dz.astype(BF16)
                    dv_part[hh] = dv_part.get(hh, 0.0) + _dot(wb, doi[ch], 0, 0)
                    dk_part[hh] = dk_part.get(hh, 0.0) + _dot(dzb, qs[ch], 0, 0)
                    new[ch] = (dq_acc + _dot(dzb, kj), rc + jnp.sum(l1m, axis=1, keepdims=True),
                               gc + g_row)
                dv_ref[rows_j, :] += jnp.concatenate([dv_part[0], dv_part[1]], axis=1)
                dk_ref[rows_j, :] += jnp.concatenate([dk_part[0], dk_part[1]], axis=1)
                return new

            zero1 = jnp.zeros((T, 1), F32)
            carry = {ch: (jnp.zeros((T, hd), F32), zero1, zero1) for ch in chains}
            carry = _sb_sweep(tile, carry, chains, m, nq)
            for qb in range(nq):
                dq_ref[rows_q[qb], :] = jnp.concatenate([carry[(0, qb)][0], carry[(1, qb)][0]], axis=1) * scale
            return 0

        lax.fori_loop(0, nb // nq, superblock, 0)

        def finish(i, carry):
            rows = pl.ds(pl.multiple_of(i * T, T), T)
            new = []
            for hh in range(2):
                sl = slice(hd * hh, hd * hh + hd)
                outs = []
                for raw_ref, gain_ref, dn in ((q_ref, qg_ref, dq_ref[rows, sl]), (k_ref, kg_ref, dk_ref[rows, sl])):
                    raw = raw_ref[rows, sl]
                    r = lax.rsqrt(jnp.mean(raw * raw, axis=1, keepdims=True) + EPS)
                    hat = raw * r
                    t = dn * gain_ref[...]
                    outs.append((r * (t - hat * jnp.mean(hat * t, axis=1, keepdims=True)),
                                 jnp.sum(dn * hat, axis=0, keepdims=True)))
                dq_ref[rows, sl] = outs[0][0]
                dk_ref[rows, sl] = outs[1][0]
                new.append((carry[hh][0] + outs[0][1], carry[hh][1] + outs[1][1]))
            return tuple(new)

        zg = (jnp.zeros((1, hd), F32), jnp.zeros((1, hd), F32))
        tot = lax.fori_loop(0, nb, finish, (zg, zg))
        dg_ref[0:1, 0:hd] += tot[0][0] + tot[1][0]
        dg_ref[1:2, 0:hd] += tot[0][1] + tot[1][1]
        if scatter is not None:
            pl.when(pl.program_id(0) == npairs - 1)(lambda: _rs_phase(rs_in, rs_out, rs_sems, "finish"))

    col = lambda off: pl.BlockSpec((S, LANES), lambda p, off=off: (0, off + p))
    gain = pl.BlockSpec((1, hd), lambda p: (0, 0))
    tri = pl.BlockSpec((T, T), lambda p: (0, 0))
    pair = pl.BlockSpec((S, LANES), lambda p: (0, p))
    in_specs = [col(0), col(npairs), col(2 * npairs), pair, pair, gain, gain, tri]
    out_specs = [pl.BlockSpec((3, S, LANES), lambda p: (0, 0, p)), pl.BlockSpec((8, LANES), lambda p: (0, 0))]
    out_shape = [jax.ShapeDtypeStruct((3, S, D), F32), jax.ShapeDtypeStruct((8, LANES), F32)]
    scratch = [pltpu.VMEM((2, S, hd), BF16)] * 4
    args = [qkv, qkv, qkv, o, do, q_gain, k_gain, _tri(T, "row_gt_col")]
    if scatter is not None:
        in_specs += [HBM_SPEC] * len(scatter)
        out_specs += [HBM_SPEC] * len(scatter)
        out_shape += [jax.ShapeDtypeStruct(g.shape, g.dtype) for g in scatter]
        scratch += _copy_semaphores(len(scatter))
        args += list(scatter)
    out = pl.pallas_call(
        body, name=name, grid=(npairs,), in_specs=in_specs, out_specs=out_specs, out_shape=out_shape,
        scratch_shapes=scratch, compiler_params=_params(("arbitrary",)),
    )(*args)
    res = (out[0], out[1][0:1, :hd], out[1][1:2, :hd])
    return res if scatter is None else res + (list(out[2:]),)


def gmlp_fwd(zzpre, b_in, v_gain, wc, bsf, *, name):
    S, H2 = zzpre.shape
    H = H2 // 2
    G, T, _ = wc.shape
    gd = H // G

    def body(z_ref, b_ref, vg_ref, wc_ref, bs_ref, p_ref):
        zz, _ = _gelu_and_grad(z_ref[...] + b_ref[...])
        u = zz[:, :H]
        v = zz[:, H:]
        vn = v * lax.rsqrt(jnp.mean(v * v, axis=1, keepdims=True) + EPS) * vg_ref[...]
        for g in range(G):
            gs = slice(g * gd, (g + 1) * gd)
            mixed = _dot(wc_ref[g], vn[:, gs].astype(BF16)) + bs_ref[g]
            p_ref[:, gs] = (u[:, gs] * mixed).astype(BF16)

    full3 = lambda shp: pl.BlockSpec(shp, lambda c: (0, 0, 0))
    return pl.pallas_call(
        body, name=name, grid=(S // T,),
        in_specs=[pl.BlockSpec((T, H2), lambda c: (c, 0)), pl.BlockSpec((1, H2), lambda c: (0, 0)),
                  pl.BlockSpec((1, H), lambda c: (0, 0)), full3((G, T, T)), full3((G, T, gd))],
        out_specs=pl.BlockSpec((T, H), lambda c: (c, 0)), out_shape=jax.ShapeDtypeStruct((S, H), BF16),
        compiler_params=_params(("parallel",)),
    )(zzpre, b_in, v_gain, wc, bsf)


def gmlp_bwd(zzpre, b_in, v_gain, wc, bsf, dp, *, name):
    S, H2 = zzpre.shape
    H = H2 // 2
    G, T, _ = wc.shape
    gd = H // G
    assert G <= LANES

    def body(z_ref, b_ref, vg_ref, wc_ref, bs_ref, dp_ref, dzz_ref, db_ref, dvg_ref, dws_ref, dbs_ref):
        @pl.when(pl.program_id(0) == 0)
        def _():
            db_ref[...] = jnp.zeros_like(db_ref)
            dvg_ref[...] = jnp.zeros_like(dvg_ref)
            dws_ref[...] = jnp.zeros_like(dws_ref)
            dbs_ref[...] = jnp.zeros_like(dbs_ref)

        zz, gp = _gelu_and_grad(z_ref[...] + b_ref[...])
        u = zz[:, :H]
        v = zz[:, H:]
        r = lax.rsqrt(jnp.mean(v * v, axis=1, keepdims=True) + EPS)
        vhat = v * r
        vg = vg_ref[...]
        vn = vhat * vg
        dpv = dp_ref[...]
        tril = lax.broadcasted_iota(jnp.int32, (T, T), 1) <= lax.broadcasted_iota(jnp.int32, (T, T), 0)
        lane = lax.broadcasted_iota(jnp.int32, (T, LANES), 1)
        dbs = jnp.zeros((T, LANES), F32)
        du_parts, dvn_parts = [], []
        for g in range(G):
            gs = slice(g * gd, (g + 1) * gd)
            vng = vn[:, gs].astype(BF16)
            wcg = wc_ref[g]
            mixed = _dot(wcg, vng) + bs_ref[g]
            dpg = dpv[:, gs]
            du_parts.append(dpg * mixed)
            dmx = dpg * u[:, gs]
            dmxb = dmx.astype(BF16)
            dvn_parts.append(_dot(wcg, dmxb, 0, 0))
            dws_ref[g] += jnp.where(tril, _dot(dmxb, vng, 1, 1), 0.0)
            dbs = dbs + jnp.where(lane == g, jnp.sum(dmx, axis=1, keepdims=True), 0.0)
        dbs_ref[...] += dbs
        du = jnp.concatenate(du_parts, axis=1)
        dvn = jnp.concatenate(dvn_parts, axis=1)
        dvg_ref[...] += jnp.sum(dvn * vhat, axis=0, keepdims=True)
        t = dvn * vg
        dv = r * (t - vhat * jnp.mean(vhat * t, axis=1, keepdims=True))
        dzu = du * gp[:, :H]
        dzv = dv * gp[:, H:]
        dzz_ref[:, :H] = dzu.astype(BF16)
        dzz_ref[:, H:] = dzv.astype(BF16)
        db_ref[:, :H] += jnp.sum(dzu, axis=0, keepdims=True)
        db_ref[:, H:] += jnp.sum(dzv, axis=0, keepdims=True)

    full3 = lambda shp: pl.BlockSpec(shp, lambda c: (0, 0, 0))
    vec = lambda n: pl.BlockSpec((1, n), lambda c: (0, 0))
    return pl.pallas_call(
        body, name=name, grid=(S // T,),
        in_specs=[pl.BlockSpec((T, H2), lambda c: (c, 0)), vec(H2), vec(H), full3((G, T, T)), full3((G, T, gd)),
                  pl.BlockSpec((T, H), lambda c: (c, 0))],
        out_specs=[pl.BlockSpec((T, H2), lambda c: (c, 0)), vec(H2), vec(H), full3((G, T, T)),
                   pl.BlockSpec((T, LANES), lambda c: (0, 0))],
        out_shape=[jax.ShapeDtypeStruct((S, H2), BF16), jax.ShapeDtypeStruct((1, H2), F32),
                   jax.ShapeDtypeStruct((1, H), F32), jax.ShapeDtypeStruct((G, T, T), F32),
                   jax.ShapeDtypeStruct((T, LANES), F32)],
        compiler_params=_params(("arbitrary",)),
    )(zzpre, b_in, v_gain, wc, bsf, dp)


def _shift_rows(v, k, n_rows):
    if k == 0:
        return v
    rolled = pltpu.roll(v, k % n_rows, 0)
    row = lax.broadcasted_iota(jnp.int32, v.shape, 0)
    keep = (row >= k) if k > 0 else (row < n_rows + k)
    return jnp.where(keep, rolled, 0.0)


def conv_fwd(zx, conv_w, conv_b, col0, *, name):
    S = zx.shape[0]
    C = conv_w.shape[1]
    tc = _pick(C, (256, 128))
    off = col0 // tc
    assert col0 % tc == 0

    def body(x_ref, w_ref, b_ref, o_ref):
        xv = x_ref[...]
        acc = b_ref[...] + w_ref[SSM_CONV - 1:SSM_CONV, :] * xv
        for k in range(SSM_CONV - 1):
            acc = acc + w_ref[k:k + 1, :] * _shift_rows(xv, SSM_CONV - 1 - k, S)
        o_ref[...] = acc * _sigmoid(acc)

    return pl.pallas_call(
        body, name=name, grid=(C // tc,),
        in_specs=[pl.BlockSpec((S, tc), lambda j: (0, off + j)), pl.BlockSpec((SSM_CONV, tc), lambda j: (0, j)),
                  pl.BlockSpec((1, tc), lambda j: (0, j))],
        out_specs=pl.BlockSpec((S, tc), lambda j: (0, j)), out_shape=jax.ShapeDtypeStruct((S, C), F32),
        compiler_params=_params(("parallel",)),
    )(zx, conv_w, conv_b)


def conv_bwd(zx, conv_w, conv_b, col0, douts, *, name):
    S = zx.shape[0]
    C = conv_w.shape[1]
    tc = LANES
    off = col0 // tc
    counts = [d.shape[1] // tc for d in douts]
    starts = [sum(counts[:p]) for p in range(len(douts))]
    assert sum(counts) * tc == C and all(d.shape[1] % tc == 0 for d in douts)

    def body(x_ref, w_ref, b_ref, *rest):
        do_refs, (dx_ref, dw_ref, db_ref) = rest[:len(douts)], rest[len(douts):]
        j = pl.program_id(0)
        dov = do_refs[-1][...]
        for p in reversed(range(len(douts) - 1)):
            dov = jnp.where(j < starts[p + 1], do_refs[p][...], dov)
        xv = x_ref[...]
        shifted = [_shift_rows(xv, SSM_CONV - 1 - k, S) for k in range(SSM_CONV)]
        acc = b_ref[...]
        for k in range(SSM_CONV):
            acc = acc + w_ref[k:k + 1, :] * shifted[k]
        s = _sigmoid(acc)
        dacc = dov * (s * (1.0 + acc * (1.0 - s)))
        db_ref[...] = jnp.sum(dacc, axis=0, keepdims=True)
        dx = jnp.zeros_like(xv)
        for k in range(SSM_CONV):
            dw_ref[k:k + 1, :] = jnp.sum(dacc * shifted[k], axis=0, keepdims=True)
            dx = dx + w_ref[k:k + 1, :] * _shift_rows(dacc, -(SSM_CONV - 1 - k), S)
        dx_ref[...] = dx

    slab = pl.BlockSpec((S, tc), lambda j: (0, j))
    piece_specs = [pl.BlockSpec((S, tc), lambda j, a=starts[p], n=counts[p]: (0, jnp.clip(j - a, 0, n - 1)))
                   for p in range(len(douts))]
    return pl.pallas_call(
        body, name=name, grid=(C // tc,),
        in_specs=[pl.BlockSpec((S, tc), lambda j: (0, off + j)), pl.BlockSpec((SSM_CONV, tc), lambda j: (0, j)),
                  pl.BlockSpec((1, tc), lambda j: (0, j))] + piece_specs,
        out_specs=[slab, pl.BlockSpec((SSM_CONV, tc), lambda j: (0, j)), pl.BlockSpec((1, tc), lambda j: (0, j))],
        out_shape=[jax.ShapeDtypeStruct((S, C), F32), jax.ShapeDtypeStruct((SSM_CONV, C), F32),
                   jax.ShapeDtypeStruct((1, C), F32)],
        compiler_params=_params(("arbitrary",)),
    )(zx, conv_w, conv_b, *douts)


def _ssd_chunk_terms(dtraw, bias, a_log, tl):
    dt = _softplus(dtraw + bias)
    a_neg = -jnp.exp(a_log)
    ac = _dot_exact_left(tl, dt * a_neg)
    ac_last = ac[ac.shape[0] - 1:, :]
    return dt, a_neg, ac, ac.T, jnp.exp(ac), jnp.exp(ac_last - ac), jnp.exp(ac_last)


def _ssd_specs(S, L, G, hpg, pd, inner):
    gp = SSD_GROUPS_PER_STEP
    gw = hpg * pd
    n = SSM_STATE
    xb = inner // n
    assert G % gp == 0 and xb % gp == 0 and (xb + G) % gp == 0

    def mk(cidx):
        return dict(
            x=pl.BlockSpec((L, gp * gw), lambda g, c: (cidx(c), g)),
            b=pl.BlockSpec((L, gp * n), lambda g, c: (cidx(c), xb // gp + g)),
            c=pl.BlockSpec((L, gp * n), lambda g, c: (cidx(c), (xb + G) // gp + g)),
            z=pl.BlockSpec((L, gp * gw), lambda g, c: (cidx(c), g)),
            dt=pl.BlockSpec((L, gp * LANES), lambda g, c: (cidx(c), g)),
            gvec=pl.BlockSpec((gp, 1, LANES), lambda g, c: (g, 0, 0)),
            chan=pl.BlockSpec((1, gp * gw), lambda g, c: (0, g)),
            tri=pl.BlockSpec((L, L), lambda g, c: (0, 0)),
            hp=pl.BlockSpec((gp, 1, gw, n), lambda g, c: (g, cidx(c), 0, 0)),
            bc=pl.BlockSpec((L, gp * n), lambda g, c: (cidx(c), g)),
        )
    return mk


def _ssd_group_views(refs, kinds, gg, gw):
    n = SSM_STATE
    width = dict(x=gw, z=gw, chan=gw, b=n, c=n, bc=n, dt=LANES)
    out = []
    for ref, kind in zip(refs, kinds):
        if kind in width:
            out.append(ref.at[:, gg * width[kind]:(gg + 1) * width[kind]])
        elif kind in ("gvec", "hp"):
            out.append(ref.at[gg:gg + 1])
        elif kind == "state":
            out.append(ref.at[gg * gw:(gg + 1) * gw])
        else:
            out.append(ref)
    return out


def ssd_fwd(xbc, zx, dtg, bias_g, alog_g, d_chan, ngain, L, G, *, name):
    S = xbc.shape[0]
    n = SSM_STATE
    inner = xbc.shape[1] - 2 * G * n
    gw = inner // G
    pd = SB_HEAD_DIM
    hpg = gw // pd
    nc = S // L
    sp = _ssd_specs(S, L, G, hpg, pd, inner)(lambda c: c)

    kinds = ("x", "b", "c", "z", "dt", "gvec", "gvec", "chan", "chan", "tri", "x", "x", "hp", "state")

    def body(*refs):
        @pl.when(pl.program_id(1) == 0)
        def _():
            refs[-1][...] = jnp.zeros_like(refs[-1])

        for gg in range(SSD_GROUPS_PER_STEP):
            group_body(*_ssd_group_views(refs, kinds, gg, gw))

    def group_body(x_ref, b_ref, c_ref, z_ref, dt_ref, bias_ref, alog_ref, d_ref, ng_ref, tl_ref,
                   yn_ref, y_ref, hp_ref, state):
        dt, _, ac, act, ea, dte, cd = _ssd_chunk_terms(dt_ref[...], bias_ref[0], alog_ref[0], tl_ref[...])
        xv = x_ref[...]
        bm = b_ref[...].astype(BF16)
        cm = c_ref[...].astype(BF16)
        cb = _dot(cm, bm, 1, 1)
        tril = lax.broadcasted_iota(jnp.int32, (L, L), 1) <= lax.broadcasted_iota(jnp.int32, (L, L), 0)
        hp_ref[0, 0] = state[...]
        for r in range(hpg):
            ps = slice(r * pd, (r + 1) * pd)
            xr = xv[:, ps]
            xdt = xr * dt[:, r:r + 1]
            lm = jnp.exp(jnp.where(tril, ac[:, r:r + 1] - act[r:r + 1, :], -jnp.inf))
            hprev = state[ps, :]
            y = _dot((cb * lm).astype(BF16), xdt.astype(BF16))
            y = y + _dot(cm, hprev.astype(BF16), 1, 1) * ea[:, r:r + 1]
            y_ref[:, ps] = y + xr * d_ref[:, ps]
            st = _dot((xdt * dte[:, r:r + 1]).astype(BF16), bm, 0, 0)
            state[ps, :] = hprev * cd[:, r:r + 1] + st
        yfull = y_ref[...]
        zg = z_ref[...]
        yg = yfull * (zg * _sigmoid(zg))
        yn_ref[...] = (yg * lax.rsqrt(jnp.mean(yg * yg, axis=1, keepdims=True) + EPS) * ng_ref[...]).astype(BF16)

    return pl.pallas_call(
        body, name=name, grid=(G // SSD_GROUPS_PER_STEP, nc),
        in_specs=[sp["x"], sp["b"], sp["c"], sp["z"], sp["dt"], sp["gvec"], sp["gvec"], sp["chan"], sp["chan"], sp["tri"]],
        out_specs=[sp["x"], sp["x"], sp["hp"]],
        out_shape=[jax.ShapeDtypeStruct((S, inner), BF16), jax.ShapeDtypeStruct((S, inner), F32),
                   jax.ShapeDtypeStruct((G, nc, gw, n), F32)],
        scratch_shapes=[pltpu.VMEM((SSD_GROUPS_PER_STEP * gw, n), F32)],
        compiler_params=_params(("arbitrary", "arbitrary")),
    )(xbc, xbc, xbc, zx, dtg, bias_g, alog_g, d_chan, ngain, _tri(L, "row_ge_col"))


def ssd_bwd(xbc, zx, dtg, bias_g, alog_g, d_chan, ngain, yfull, hp, dyn, L, G, *, name):
    S = xbc.shape[0]
    n = SSM_STATE
    inner = xbc.shape[1] - 2 * G * n
    gw = inner // G
    pd = SB_HEAD_DIM
    hpg = gw // pd
    nc = S // L
    sp = _ssd_specs(S, L, G, hpg, pd, inner)(lambda c: nc - 1 - c)

    kinds = ("x", "b", "c", "z", "dt", "gvec", "gvec", "chan", "chan", "tri", "tri", "x", "hp", "x",
             "x", "x", "bc", "bc", "dt", "gvec", "gvec", "gvec", "chan", "state")

    def body(*refs):
        @pl.when(pl.program_id(1) == 0)
        def _():
            for acc in refs[-5:]:
                acc[...] = jnp.zeros_like(acc)

        for gg in range(SSD_GROUPS_PER_STEP):
            group_body(*_ssd_group_views(refs, kinds, gg, gw))

    def group_body(x_ref, b_ref, c_ref, z_ref, dt_ref, bias_ref, alog_ref, d_ref, ng_ref, tl_ref, tu_ref,
                   yf_ref, hp_ref, dyn_ref,
                   dz_ref, dx_ref, db_ref, dc_ref, ddt_ref, dbias_ref, dalog_ref, dd_ref, dng_ref, dstate):

        dtraw = dt_ref[...]
        dt, a_neg, ac, act, ea, dte, cd = _ssd_chunk_terms(dtraw, bias_ref[0], alog_ref[0], tl_ref[...])
        xv = x_ref[...]
        bm = b_ref[...].astype(BF16)
        cm = c_ref[...].astype(BF16)
        cb = _dot(cm, bm, 1, 1)
        tril = lax.broadcasted_iota(jnp.int32, (L, L), 1) <= lax.broadcasted_iota(jnp.int32, (L, L), 0)
        lane = lax.broadcasted_iota(jnp.int32, (L, LANES), 1)
        lane1 = lax.broadcasted_iota(jnp.int32, (1, LANES), 1)

        yfull = yf_ref[...]
        zg = z_ref[...]
        sg = _sigmoid(zg)
        gate = zg * sg
        yg = yfull * gate
        rr = lax.rsqrt(jnp.mean(yg * yg, axis=1, keepdims=True) + EPS)
        yhat = yg * rr
        dynv = dyn_ref[...]
        dng_ref[...] += jnp.sum(dynv * yhat, axis=0, keepdims=True)
        t = dynv * ng_ref[...]
        dyg = rr * (t - yhat * jnp.mean(yhat * t, axis=1, keepdims=True))
        dy = dyg * gate
        dz_ref[...] = dyg * yfull * (sg * (1.0 + zg * (1.0 - sg)))

        dcb = jnp.zeros((L, L), F32)
        dc_acc = jnp.zeros((L, n), F32)
        db_acc = jnp.zeros((L, n), F32)
        dac = jnp.zeros((L, LANES), F32)
        xdx = jnp.zeros((L, LANES), F32)
        tail = jnp.zeros((1, LANES), F32)
        dskip = jnp.zeros((1, LANES), F32)
        ones_l = jnp.ones((L, LANES), BF16)
        for r in range(hpg):
            ps = slice(r * pd, (r + 1) * pd)
            xr = xv[:, ps]
            dyr = dy[:, ps]
            dtr = dt[:, r:r + 1]
            dter = dte[:, r:r + 1]
            cdr = cd[:, r:r + 1]
            xdt = xr * dtr
            xdtb = xdt.astype(BF16)
            dyrb = dyr.astype(BF16)
            lm = jnp.exp(jnp.where(tril, ac[:, r:r + 1] - act[r:r + 1, :], -jnp.inf))
            m32 = cb * lm
            mb = m32.astype(BF16)
            hprev = hp_ref[0, 0, ps, :]
            hpb = hprev.astype(BF16)
            dhn = dstate[ps, :]
            dhnb = dhn.astype(BF16)
            ear = ea[:, r:r + 1]
            gy = (dyr * ear).astype(BF16)
            dc_acc = dc_acc + _dot(gy, hpb)
            dstate[ps, :] = _dot(gy, cm, 0, 0) + dhn * cdr
            bdh = _dot(bm, dhnb, 1, 1)
            db_acc = db_acc + _dot((xdt * dter).astype(BF16), dhnb)
            dm = _dot(dyrb, xdtb, 1, 1)
            dxdt = bdh * dter + _dot(mb, dyrb, 0, 0)
            dcb = dcb + dm * lm
            wmat = dm * m32
            whi = wmat.astype(BF16)
            wlo = (wmat - whi.astype(F32)).astype(BF16)
            col_w = _dot(whi, ones_l, 0, 0) + _dot(wlo, ones_l, 0, 0)
            t_end = xdt * bdh * dter
            e_r = jnp.sum(wmat, axis=1, keepdims=True) \
                + jnp.sum(dyr * _dot(cm, hpb, 1, 1) * ear - t_end, axis=1, keepdims=True)
            c_r = cdr * _sum_all(dhn * hprev) + _sum_all(t_end)
            dac = dac + jnp.where(lane == r, e_r - col_w, 0.0)
            xdx = xdx + jnp.where(lane == r, jnp.sum(dxdt * xr, axis=1, keepdims=True), 0.0)
            tail = tail + jnp.where(lane1 == r, c_r, 0.0)
            dskip = dskip + jnp.where(lane1 == r, _sum_all(dyr * xr), 0.0)
            dx_ref[:, ps] = dxdt * dtr + dyr * d_ref[:, ps]
        dcbb = dcb.astype(BF16)
        dc_ref[...] = dc_acc + _dot(dcbb, bm)
        db_ref[...] = db_acc + _dot(dcbb, cm, 0, 0)
        da = _dot_exact_left(tu_ref[...], dac) + tail
        real = lane < hpg
        ddt = jnp.where(real, (da * a_neg + xdx) * _sigmoid(dtraw + bias_ref[0]), 0.0)
        ddt_ref[...] = ddt
        dd_ref[0] += dskip
        dbias_ref[0] += jnp.sum(ddt, axis=0, keepdims=True)
        dalog_ref[0] += jnp.where(lane1 < hpg, jnp.sum(da * dt, axis=0, keepdims=True) * a_neg, 0.0)

    return pl.pallas_call(
        body, name=name, grid=(G // SSD_GROUPS_PER_STEP, nc),
        in_specs=[sp["x"], sp["b"], sp["c"], sp["z"], sp["dt"], sp["gvec"], sp["gvec"], sp["chan"], sp["chan"],
                  sp["tri"], sp["tri"], sp["x"], sp["hp"], sp["x"]],
        out_specs=[sp["x"], sp["x"], sp["bc"], sp["bc"], sp["dt"], sp["gvec"], sp["gvec"], sp["gvec"], sp["chan"]],
        out_shape=[jax.ShapeDtypeStruct((S, inner), F32), jax.ShapeDtypeStruct((S, inner), F32),
                   jax.ShapeDtypeStruct((S, G * n), F32), jax.ShapeDtypeStruct((S, G * n), F32),
                   jax.ShapeDtypeStruct((S, G * LANES), F32), jax.ShapeDtypeStruct((G, 1, LANES), F32),
                   jax.ShapeDtypeStruct((G, 1, LANES), F32), jax.ShapeDtypeStruct((G, 1, LANES), F32),
                   jax.ShapeDtypeStruct((1, inner), F32)],
        scratch_shapes=[pltpu.VMEM((SSD_GROUPS_PER_STEP * gw, n), F32)],
        compiler_params=_params(("arbitrary", "arbitrary")),
    )(xbc, xbc, xbc, zx, dtg, bias_g, alog_g, d_chan, ngain, _tri(L, "row_ge_col"), _tri(L, "row_le_col"),
      yfull, hp, dyn)


def _spread_dt(w_dt_t, G, hpg):
    K = w_dt_t.shape[1]
    w = w_dt_t.reshape(G, hpg, K)
    return jnp.pad(w, ((0, 0), (0, LANES - hpg), (0, 0))).reshape(G * LANES, K)


def _group_vec(v, G, hpg):
    return jnp.pad(v.reshape(G, 1, hpg), ((0, 0), (0, 0), (0, LANES - hpg)))


def local_step(x, target, W, late=None):
    S, D = x.shape
    depth = W["mix_norm"].shape[0]
    gm_groups, gm_chunk = W["gm_w_s"].shape[1], W["gm_w_s"].shape[2]
    heads = W["ssm_dt_bias"].shape[1]
    inner = heads * SB_HEAD_DIM
    L = gm_chunk
    received = None

    saved = []
    for i in range(depth):
        kind, j = i % 3, i // 3
        s = dict(x=x)
        h = rms_fwd(x, W["mix_norm"][i:i + 1], name="rms_mix_fwd")
        s["h"] = h
        if kind == 0:
            qkv = matmul(h, W["sb_w_qkv"][j], tb=True, name="mm_qkv")
            if late is not None and i == 0:
                o, gathered = sb_attn_fwd(qkv, W["sb_q_gain"][j:j + 1], W["sb_k_gain"][j:j + 1], name="sb_fwd_gather",
                                          gather=late.shards)
                late.fill(W, gathered)
            else:
                o = sb_attn_fwd(qkv, W["sb_q_gain"][j:j + 1], W["sb_k_gain"][j:j + 1], name="sb_fwd")
            x1 = matmul(o, W["sb_w_o"][j], residual=x, name="mm_sb_out")
            s.update(qkv=qkv, o=o)
        elif kind == 1:
            wc = jnp.where(jnp.tril(jnp.ones((gm_chunk, gm_chunk), bool)), W["gm_w_s"][j], 0.0).astype(BF16)
            bsf = jnp.broadcast_to(W["gm_b_s"][j][:, :, None], (gm_groups, gm_chunk, W["gm_v_gain"].shape[1] // gm_groups)).astype(F32)
            zzpre = matmul(h, W["gm_w_in"][j], tb=True, name="mm_gm_in")
            p = gmlp_fwd(zzpre, W["gm_b_in"][j:j + 1], W["gm_v_gain"][j:j + 1], wc, bsf, name="gm_fwd")
            x1 = matmul(p, W["gm_w_out"][j], residual=x, name="mm_gm_out")
            s.update(zzpre=zzpre, p=p, wc=wc, bsf=bsf)
        else:
            conv_dim = W["ssm_conv_w"].shape[2]
            G = (conv_dim - inner) // (2 * SSM_STATE)
            hpg = heads // G
            w_in = W["ssm_w_in"][j]
            w_zx = w_in[:inner + conv_dim]
            w_dtg = _spread_dt(w_in[inner + conv_dim:], G, hpg)
            bias_g = _group_vec(W["ssm_dt_bias"][j], G, hpg)
            alog_g = _group_vec(W["ssm_a_log"][j], G, hpg)
            d_chan = jnp.repeat(W["ssm_d"][j], SB_HEAD_DIM)[None, :]
            ngain = W["ssm_norm_gain"][j:j + 1]
            zx = matmul(h, w_zx, tb=True, name="mm_ssm_zx")
            dtg = matmul(h, w_dtg, tb=True, name="mm_ssm_dt")
            xbc = conv_fwd(zx, W["ssm_conv_w"][j], W["ssm_conv_b"][j:j + 1], inner, name="conv_fwd")
            yn, yfull, hp = ssd_fwd(xbc, zx, dtg, bias_g, alog_g, d_chan, ngain, L, G, name="ssd_fwd")
            x1 = matmul(yn, W["ssm_w_out"][j], residual=x, name="mm_ssm_out")
            s.update(w_zx=w_zx, w_dtg=w_dtg, bias_g=bias_g, alog_g=alog_g, d_chan=d_chan, ngain=ngain,
                     zx=zx, dtg=dtg, xbc=xbc, yn=yn, yfull=yfull, hp=hp)
        h2 = rms_fwd(x1, W["ffn_norm"][i:i + 1], name="rms_ffn_fwd")
        gu, a = ffn_up_fwd(h2, W["ffn_w_gu"][i], name="ffn_up_fwd")
        x2 = matmul(a, W["ffn_w_down"][i], residual=x1, name="mm_ffn_down")
        s.update(x1=x1, h2=h2, gu=gu, a=a)
        saved.append(s)
        x = x2

    dx, loss = loss_head(x, target, name="loss_head")

    gw = {k: {} for k in WEIGHTS}
    for i in reversed(range(depth)):
        kind, j = i % 3, i // 3
        s = saved[i]
        gw["ffn_w_down"][i] = matmul(s["a"], dx, ta=True, out_dtype=BF16, name="mm_ffn_dwdown")
        dgu = ffn_up_bwd(dx, W["ffn_w_down"][i], s["gu"], name="ffn_up_bwd")
        dh2 = matmul(dgu, W["ffn_w_gu"][i], a_split=2, name="mm_ffn_dh")
        gw["ffn_w_gu"][i] = matmul(dgu, s["h2"], ta=True, a_split=2, out_dtype=BF16, name="mm_ffn_dwgu")
        dx1, dgn = rms_bwd(s["x1"], W["ffn_norm"][i:i + 1], dh2, dx, name="rms_ffn_bwd")
        gw["ffn_norm"][i] = dgn[0]
        if kind == 0:
            do = matmul(dx1, W["sb_w_o"][j], tb=True, name="mm_sb_do")
            gw["sb_w_o"][j] = matmul(s["o"], dx1, ta=True, out_dtype=BF16, name="mm_sb_dwo")
            if late is not None and i == 0:
                dqkv, dqg, dkg, received = sb_attn_bwd(
                    s["qkv"], s["o"], do, W["sb_q_gain"][j:j + 1], W["sb_k_gain"][j:j + 1], name="sb_bwd_scatter",
                    scatter=late.contributions(gw))
            else:
                dqkv, dqg, dkg = sb_attn_bwd(s["qkv"], s["o"], do, W["sb_q_gain"][j:j + 1], W["sb_k_gain"][j:j + 1],
                                             name="sb_bwd")
            gw["sb_q_gain"][j] = dqg[0]
            gw["sb_k_gain"][j] = dkg[0]
            dh = matmul(dqkv, W["sb_w_qkv"][j], a_split=3, name="mm_sb_dh")
            gw["sb_w_qkv"][j] = matmul(dqkv, s["h"], ta=True, a_split=3, out_dtype=BF16, name="mm_sb_dwqkv")
        elif kind == 1:
            dp = matmul(dx1, W["gm_w_out"][j], tb=True, name="mm_gm_dp")
            gw["gm_w_out"][j] = matmul(s["p"], dx1, ta=True, out_dtype=BF16, name="mm_gm_dwout")
            dzz, db_in, dvg, dws, dbs = gmlp_bwd(s["zzpre"], W["gm_b_in"][j:j + 1], W["gm_v_gain"][j:j + 1],
                                                s["wc"], s["bsf"], dp, name="gm_bwd")
            gw["gm_b_in"][j] = db_in[0]
            gw["gm_v_gain"][j] = dvg[0]
            gw["gm_w_s"][j] = dws
            gw["gm_b_s"][j] = dbs[:, :gm_groups].T
            dh = matmul(dzz, W["gm_w_in"][j], name="mm_gm_dh")
            gw["gm_w_in"][j] = matmul(dzz, s["h"], ta=True, out_dtype=BF16, name="mm_gm_dwin")
        else:
            conv_dim = W["ssm_conv_w"].shape[2]
            G = (conv_dim - inner) // (2 * SSM_STATE)
            hpg = heads // G
            dyn = matmul(dx1, W["ssm_w_out"][j], tb=True, name="mm_ssm_dyn")
            gw["ssm_w_out"][j] = matmul(s["yn"], dx1, ta=True, out_dtype=BF16, name="mm_ssm_dwout")
            dz, dxs, dbm, dcm, ddt, dbias, dalog, dd, dng = ssd_bwd(
                s["xbc"], s["zx"], s["dtg"], s["bias_g"], s["alog_g"], s["d_chan"], s["ngain"], s["yfull"], s["hp"],
                dyn, L, G, name="ssd_bwd")
            dpre, dcw, dcb = conv_bwd(s["zx"], W["ssm_conv_w"][j], W["ssm_conv_b"][j:j + 1], inner, [dxs, dbm, dcm],
                                      name="conv_bwd")
            dzx = jnp.concatenate([dz, dpre], axis=1)
            dh = matmul(ddt, s["w_dtg"], name="mm_ssm_dh_dt")
            dh = matmul(dzx, s["w_zx"], residual=dh, name="mm_ssm_dh")
            dw_zx = matmul(dzx, s["h"], ta=True, out_dtype=BF16, name="mm_ssm_dwzx")
            dw_dtg = matmul(ddt, s["h"], ta=True, out_dtype=BF16, name="mm_ssm_dwdt")
            dw_dt = dw_dtg.reshape(G, LANES, D)[:, :hpg, :].reshape(heads, D)
            gw["ssm_w_in"][j] = jnp.concatenate([dw_zx, dw_dt], axis=0)
            gw["ssm_conv_w"][j] = dcw
            gw["ssm_conv_b"][j] = dcb[0]
            gw["ssm_dt_bias"][j] = dbias[:, 0, :hpg].reshape(heads)
            gw["ssm_a_log"][j] = dalog[:, 0, :hpg].reshape(heads)
            gw["ssm_d"][j] = dd[:, 0, :hpg].reshape(heads)
            gw["ssm_norm_gain"][j] = dng[0]
        dx, dgn = rms_bwd(s["x"], W["mix_norm"][i:i + 1], dh, dx1, name="rms_mix_bwd")
        gw["mix_norm"][i] = dgn[0]

    return loss, dx, gw, received


MESH = pl.DeviceIdType.MESH
HBM_SPEC = pl.BlockSpec(memory_space=pltpu.HBM)
VMEM_SPEC = pl.BlockSpec(memory_space=pltpu.VMEM)


def _my_position():
    return lax.axis_index("x"), lax.axis_index("y"), lax.axis_index("c")


def _flip(v, bit):
    return 1 - v if bit else v


def all_gather(shards, *, name):
    n = len(shards)

    def body(*refs):
        for phase in ("start", "forward", "finish"):
            _ag_phase(refs[:n], refs[n:2 * n], refs[2 * n:], phase)

    return pl.pallas_call(
        body, name=name, out_shape=[jax.ShapeDtypeStruct((N_DEV,) + s.shape, s.dtype) for s in shards],
        in_specs=[HBM_SPEC] * n, out_specs=[HBM_SPEC] * n, scratch_shapes=_copy_semaphores(n),
    )(*shards)


def _ag_phase(x_refs, out_refs, sems, phase):
    send_sems, recv_sems, local_sems = sems
    x, y, c = _my_position()
    me, sibling = (x, y, c), (x, y, 1 - c)
    chips = [(1 - x, y), (x, 1 - y), (1 - x, 1 - y)]
    for p, (x_ref, out_ref) in enumerate(zip(x_refs, out_refs)):
        def slot(px, py, pc):
            return out_ref.at[4 * px + 2 * py + pc]

        def copy(k, block, to, src=None):
            return pltpu.make_async_remote_copy(
                src_ref=slot(*block) if src is None else src, dst_ref=slot(*block),
                send_sem=send_sems.at[7 * p + k], recv_sem=recv_sems.at[7 * p + k], device_id=to, device_id_type=MESH)

        mine = pltpu.make_async_copy(x_ref, slot(*me), local_sems.at[p])
        first = [copy(0, me, sibling, src=x_ref)]
        first += [copy(1 + j, me, (*chip, c), src=x_ref) for j, chip in enumerate(chips)]
        passed = [copy(4 + j, (*chip, c), sibling) for j, chip in enumerate(chips)]
        if phase == "start":
            mine.start()
            for cp in first:
                cp.start()
        elif phase == "forward":
            for j, chip in enumerate(chips):
                copy(1 + j, (*chip, c), me).wait_recv()
                passed[j].start()
        else:
            copy(0, sibling, me).wait_recv()
            for j, chip in enumerate(chips):
                copy(4 + j, (*chip, 1 - c), me).wait_recv()
            for cp in first + passed:
                cp.wait_send()
            mine.wait()


def _copy_semaphores(n):
    return [pltpu.SemaphoreType.DMA((7 * n,)), pltpu.SemaphoreType.DMA((7 * n,)), pltpu.SemaphoreType.DMA((n,))]


def _rs_phase(g_refs, out_refs, sems, phase):
    send_sems, recv_sems, local_sems = sems
    x, y, c = _my_position()
    me = 4 * x + 2 * y + c
    copies = []
    for p, (g_ref, out_ref) in enumerate(zip(g_refs, out_refs)):
        copies.append(pltpu.make_async_copy(g_ref.at[me], out_ref.at[me], local_sems.at[p]))
        for k in range(1, N_DEV):
            px, py, pc = _flip(x, k & 4), _flip(y, k & 2), _flip(c, k & 1)
            copies.append(pltpu.make_async_remote_copy(
                src_ref=g_ref.at[4 * px + 2 * py + pc], dst_ref=out_ref.at[me],
                send_sem=send_sems.at[7 * p + k - 1], recv_sem=recv_sems.at[7 * p + k - 1],
                device_id=(px, py, pc), device_id_type=MESH))
    for cp in copies:
        if phase == "start":
            cp.start()
        else:
            cp.wait()


def sum_slots(recv, *, name):
    n, R, C = recv.shape
    tr = _pick(R, (512, 256, 128))

    def body(r_ref, o_ref):
        acc = r_ref[0].astype(F32)
        for s in range(1, n):
            acc = acc + r_ref[s].astype(F32)
        o_ref[...] = acc

    return pl.pallas_call(
        body, name=name, grid=(R // tr,), in_specs=[pl.BlockSpec((n, tr, C), lambda i: (0, i, 0))],
        out_specs=pl.BlockSpec((tr, C), lambda i: (i, 0)), out_shape=jax.ShapeDtypeStruct((R, C), F32),
        compiler_params=_params(("parallel",)),
    )(recv)


def all_reduce_small(vs, scatter, *, name):
    n, ns = len(vs), len(scatter)

    def body(*refs):
        v_refs, g_refs = refs[:n], refs[n:n + ns]
        o_refs, r_refs = refs[n + ns:2 * n + ns], refs[2 * n + ns:2 * n + 2 * ns]
        bufs = refs[2 * n + 2 * ns:3 * n + 2 * ns]
        send_sems, recv_sems = refs[3 * n + 2 * ns:3 * n + 2 * ns + 2]
        rs_sems = refs[3 * n + 2 * ns + 2:]
        x, y, c = _my_position()
        me = 4 * x + 2 * y + c
        _rs_phase(g_refs, r_refs, rs_sems, "start")
        copies = []
        for p, (v_ref, buf) in enumerate(zip(v_refs, bufs)):
            buf[me] = v_ref[...]
            for k in range(1, N_DEV):
                px, py, pc = _flip(x, k & 4), _flip(y, k & 2), _flip(c, k & 1)
                copies.append(pltpu.make_async_remote_copy(
                    src_ref=v_ref, dst_ref=buf.at[me], send_sem=send_sems.at[7 * p + k - 1],
                    recv_sem=recv_sems.at[7 * p + k - 1], device_id=(px, py, pc), device_id_type=MESH))
        for cp in copies:
            cp.start()
        for cp in copies:
            cp.wait()
        for o_ref, buf in zip(o_refs, bufs):
            acc = buf[0]
            for s in range(1, N_DEV):
                acc = acc + buf[s]
            o_ref[...] = acc
        _rs_phase(g_refs, r_refs, rs_sems, "finish")

    out = pl.pallas_call(
        body, name=name,
        out_shape=[jax.ShapeDtypeStruct(v.shape, F32) for v in vs] + [jax.ShapeDtypeStruct(g.shape, g.dtype) for g in scatter],
        in_specs=[VMEM_SPEC] * n + [HBM_SPEC] * ns, out_specs=[VMEM_SPEC] * n + [HBM_SPEC] * ns,
        scratch_shapes=[pltpu.VMEM((N_DEV,) + v.shape, F32) for v in vs]
        + [pltpu.SemaphoreType.DMA((7 * n,)), pltpu.SemaphoreType.DMA((7 * n,))] + _copy_semaphores(ns),
        compiler_params=pltpu.CompilerParams(vmem_limit_bytes=VMEM_LIMIT_BYTES),
    )(*vs, *scatter)
    return list(out[:n]), list(out[n:])


def _pad_rows(a, mult):
    pad = (-a.shape[0]) % mult
    return jnp.pad(a, ((0, pad), (0, 0))) if pad else a


def _pack_small(arrays):
    flat = []
    for a in arrays:
        f = a.reshape(-1).astype(F32)
        flat.append(jnp.pad(f, (0, (-f.shape[0]) % LANES)))
    return _pad_rows(jnp.concatenate(flat).reshape(-1, LANES), 8)


def _unpack_small(packed, shapes):
    flat = packed.reshape(-1)
    out, r = [], 0
    for shp in shapes:
        n = math.prod(shp)
        out.append(flat[r:r + n].reshape(shp))
        r += n + (-n) % LANES
    return out


ARG_NAMES = ("x",) + WEIGHTS + ("loss_target",) + tuple("m_" + w for w in WEIGHTS) + tuple("v_" + w for w in WEIGHTS)


def kernel(x, mix_norm, ffn_norm, sb_w_qkv, sb_q_gain, sb_k_gain, sb_w_o, gm_w_in, gm_b_in, gm_v_gain, gm_w_s, gm_b_s, gm_w_out, ssm_w_in, ssm_conv_w, ssm_conv_b, ssm_dt_bias, ssm_a_log, ssm_d, ssm_norm_gain, ssm_w_out, ffn_w_gu, ffn_w_down, loss_target, m_mix_norm, m_ffn_norm, m_sb_w_qkv, m_sb_q_gain, m_sb_k_gain, m_sb_w_o, m_gm_w_in, m_gm_b_in, m_gm_v_gain, m_gm_w_s, m_gm_b_s, m_gm_w_out, m_ssm_w_in, m_ssm_conv_w, m_ssm_conv_b, m_ssm_dt_bias, m_ssm_a_log, m_ssm_d, m_ssm_norm_gain, m_ssm_w_out, m_ffn_w_gu, m_ffn_w_down, v_mix_norm, v_ffn_norm, v_sb_w_qkv, v_sb_q_gain, v_sb_k_gain, v_sb_w_o, v_gm_w_in, v_gm_b_in, v_gm_v_gain, v_gm_w_s, v_gm_b_s, v_gm_w_out, v_ssm_w_in, v_ssm_conv_w, v_ssm_conv_b, v_ssm_dt_bias, v_ssm_a_log, v_ssm_d, v_ssm_norm_gain, v_ssm_w_out, v_ffn_w_gu, v_ffn_w_down):
    given = dict(zip(ARG_NAMES, (x, mix_norm, ffn_norm, sb_w_qkv, sb_q_gain, sb_k_gain, sb_w_o, gm_w_in, gm_b_in, gm_v_gain, gm_w_s, gm_b_s, gm_w_out, ssm_w_in, ssm_conv_w, ssm_conv_b, ssm_dt_bias, ssm_a_log, ssm_d, ssm_norm_gain, ssm_w_out, ffn_w_gu, ffn_w_down, loss_target, m_mix_norm, m_ffn_norm, m_sb_w_qkv, m_sb_q_gain, m_sb_k_gain, m_sb_w_o, m_gm_w_in, m_gm_b_in, m_gm_v_gain, m_gm_w_s, m_gm_b_s, m_gm_w_out, m_ssm_w_in, m_ssm_conv_w, m_ssm_conv_b, m_ssm_dt_bias, m_ssm_a_log, m_ssm_d, m_ssm_norm_gain, m_ssm_w_out, m_ffn_w_gu, m_ffn_w_down, v_mix_norm, v_ffn_norm, v_sb_w_qkv, v_sb_q_gain, v_sb_k_gain, v_sb_w_o, v_gm_w_in, v_gm_b_in, v_gm_v_gain, v_gm_w_s, v_gm_b_s, v_gm_w_out, v_ssm_w_in, v_ssm_conv_w, v_ssm_conv_b, v_ssm_dt_bias, v_ssm_a_log, v_ssm_d, v_ssm_norm_gain, v_ssm_w_out, v_ffn_w_gu, v_ffn_w_down)))
    mx, my, mc = _my_position()
    me = 4 * mx + 2 * my + mc

    pieces = [(k, l) for k in BIG for l in range(given[k].shape[0])]
    early = [("sb_w_qkv", 0)]
    late_pieces = [p for p in pieces if p not in early]
    last = [("sb_w_qkv", 0)]
    main = [p for p in pieces if p not in last]

    def shards_of(ps):
        return [(given[k][l].T if k in COL_SHARDED else given[k][l]).astype(BF16) for k, l in ps]

    def piece_to_full(g, k):
        rows, cols = given[k].shape[1:]
        return g.reshape(N_DEV * cols, rows) if k in COL_SHARDED else g.reshape(N_DEV * rows, cols)

    def full_to_piece(full, k):
        return full.reshape(N_DEV, -1, PACK_COLS)

    def summed_to_shard(g, k):
        rows, cols = given[k].shape[1:]
        return g.reshape(cols, rows) if k in COL_SHARDED else g.reshape(rows, cols)

    def contributions(gw, ps):
        return [full_to_piece(gw[k][l], k) for k, l in ps]

    sharded_small = [lax.bitcast_convert_type(given[k], BF16) for k in SMALL_SHARDED]
    tail = jnp.concatenate([a.reshape(-1) for a in sharded_small])
    tail = jnp.pad(tail, (0, (-tail.size) % PACK_COLS)).reshape(-1, PACK_COLS)

    W = {k: given[k] for k in SMALL if k not in SMALL_SHARDED}
    W.update({k: [None] * given[k].shape[0] for k in BIG})
    for (k, l), g in zip(early, all_gather(shards_of(early), name="all_gather_early")):
        W[k][l] = piece_to_full(g, k)

    class Late:
        shards = shards_of(late_pieces) + [tail]

        @staticmethod
        def fill(weights, gathered):
            for (k, l), g in zip(late_pieces, gathered):
                weights[k][l] = piece_to_full(g, k)
            tail_g = gathered[-1].reshape(N_DEV, -1)
            off = 0
            for k, a in zip(SMALL_SHARDED, sharded_small):
                g = lax.bitcast_convert_type(tail_g[:, off:off + a.size].reshape((N_DEV,) + a.shape), F32)
                weights[k] = jnp.moveaxis(g, 0, -2).reshape(g.shape[1:-1] + (N_DEV * g.shape[-1],))
                off += a.size

        @staticmethod
        def contributions(gw):
            return contributions(gw, main)

    loss, gx, gw, received_main = local_step(given["x"][0], given["loss_target"][0], W, late=Late)

    grads_small = {k: jnp.stack([gw[k][l] for l in sorted(gw[k])], axis=0) for k in SMALL}
    packed_names = tuple(k for k in SMALL if k != "gm_w_s")
    small_shapes = [grads_small[k].shape for k in packed_names] + [(1, 1)]
    (red_ws, red_rest), received_last = all_reduce_small(
        [grads_small["gm_w_s"].reshape(-1, LANES), _pack_small([grads_small[k] for k in packed_names] + [loss])],
        contributions(gw, last), name="all_reduce_small_and_last_exchange")
    small_full = dict(zip(packed_names + ("loss",), _unpack_small(red_rest, small_shapes)))
    small_full["gm_w_s"] = red_ws.reshape(grads_small["gm_w_s"].shape)

    g_piece = {}
    for grp, received in ((main, received_main), (last, received_last)):
        for p, r in zip(grp, received):
            g_piece[p] = summed_to_shard(sum_slots(r, name="reduce_scatter_sum"), p[0])
    out_g, out_d, out_m, out_v = {}, {}, {}, {}
    for k in BIG:
        swap = lambda a: jnp.swapaxes(a, -1, -2)
        keep_t = k in COL_SHARDED and given[k].shape[-1] % LANES != 0
        flip = swap if keep_t else (lambda a: a)
        g = jnp.stack([g_piece[(k, l)] for l in range(given[k].shape[0])], axis=0)
        if k in COL_SHARDED and not keep_t:
            g = swap(g)
        res = adamw(flip(given[k]), g, flip(given["m_" + k]), flip(given["v_" + k]), name="adamw_" + k)
        out_g[k], out_d[k], out_m[k], out_v[k] = (flip(a) for a in (g,) + tuple(res))

    gsmall = {}
    for k in SMALL:
        g = small_full[k]
        if k in SMALL_SHARDED:
            n = given[k].shape[-1]
            g = lax.dynamic_slice_in_dim(g, me * n, n, axis=g.ndim - 1)
        gsmall[k] = g
    local_shapes = [given[k].shape for k in packed_names]
    dsm, nmsm, nvsm = adamw(*[_pack_small([src[k] for k in packed_names]) for src in (
        given, gsmall, {k: given["m_" + k] for k in packed_names}, {k: given["v_" + k] for k in packed_names})],
        name="adamw_small")
    out_g.update(gsmall)
    for dst, src in ((out_d, dsm), (out_m, nmsm), (out_v, nvsm)):
        dst.update(zip(packed_names, _unpack_small(src, local_shapes)))
    ws_shape = given["gm_w_s"].shape
    out_d["gm_w_s"], out_m["gm_w_s"], out_v["gm_w_s"] = (a.reshape(ws_shape) for a in adamw(
        *[a.reshape((-1,) + ws_shape[-2:]) for a in (given["gm_w_s"], gsmall["gm_w_s"], given["m_gm_w_s"], given["v_gm_w_s"])],
        name="adamw_gm_w_s"))

    return (small_full["loss"].reshape(()), gx[None],
            *[out_g[k] for k in WEIGHTS], *[out_d[k] for k in WEIGHTS],
            *[out_m[k] for k in WEIGHTS], *[out_v[k] for k in WEIGHTS])
```

```python
import math

import jax
import jax.numpy as jnp
from jax import lax
from jax.experimental import pallas as pl
from jax.experimental.pallas import tpu as pltpu

F32 = jnp.float32
BF16 = jnp.bfloat16
EPS = 1e-6
N_DEV = 8
SB_HEAD_DIM = 64
SB_TILE = 256
SB_FWD_QUERY_BLOCKS = 4
SB_BWD_QUERY_BLOCKS = 2
SSM_STATE = 128
SSD_FWD_GROUPS_PER_STEP = 2
SSD_BWD_GROUPS_PER_STEP = 1
SSM_CONV = 4
ADAM_LR = 0.001
ADAM_B1 = 0.9
ADAM_B2 = 0.999
ADAM_EPS = 1e-08
ADAM_WD = 0.01
ADAM_STEP = 10
VMEM_LIMIT_BYTES = 56 * 1024 * 1024
MATMUL_VMEM_BUDGET = 40 * 1024 * 1024
LANES = 128
PACK_COLS = 1024

BIG = ("sb_w_qkv", "sb_w_o", "gm_w_in", "gm_w_out", "ssm_w_in", "ssm_w_out", "ffn_w_gu", "ffn_w_down")
COL_SHARDED = ("sb_w_qkv", "gm_w_in", "ssm_w_in", "ffn_w_gu")
SMALL = ("mix_norm", "ffn_norm", "sb_q_gain", "sb_k_gain", "gm_b_in", "gm_v_gain", "gm_w_s", "gm_b_s",
         "ssm_conv_w", "ssm_conv_b", "ssm_dt_bias", "ssm_a_log", "ssm_d", "ssm_norm_gain")
SMALL_SHARDED = ("ssm_conv_w", "ssm_conv_b", "ssm_norm_gain")
WEIGHTS = ("mix_norm", "ffn_norm", "sb_w_qkv", "sb_q_gain", "sb_k_gain", "sb_w_o", "gm_w_in", "gm_b_in",
           "gm_v_gain", "gm_w_s", "gm_b_s", "gm_w_out", "ssm_w_in", "ssm_conv_w", "ssm_conv_b", "ssm_dt_bias",
           "ssm_a_log", "ssm_d", "ssm_norm_gain", "ssm_w_out", "ffn_w_gu", "ffn_w_down")


def _params(semantics=None):
    return pltpu.CompilerParams(dimension_semantics=semantics, vmem_limit_bytes=VMEM_LIMIT_BYTES)


def _pick(n, prefs):
    for t in prefs:
        if t <= n and n % t == 0:
            return t
    return n


def _dot(a, b, ca=1, cb=0):
    return lax.dot_general(a, b, (((ca,), (cb,)), ((), ())), preferred_element_type=F32)


def _split3(v):
    h1 = v.astype(BF16)
    r1 = v - h1.astype(F32)
    h2 = r1.astype(BF16)
    h3 = (r1 - h2.astype(F32)).astype(BF16)
    return h1, h2, h3


def _dot_exact_left(mat01, v):
    h1, h2, h3 = _split3(v)
    return _dot(mat01, h1) + _dot(mat01, h2) + _dot(mat01, h3)


def _sum_all(v):
    return jnp.sum(jnp.sum(v, axis=0, keepdims=True), axis=1, keepdims=True)


def _sigmoid(v):
    return 1.0 / (1.0 + jnp.exp(-v))


def _softplus(v):
    return jnp.maximum(v, 0.0) + jnp.log(1.0 + jnp.exp(-jnp.abs(v)))


def _erf(v):
    a = jnp.abs(v)
    t = 1.0 / (1.0 + 0.3275911 * a)
    poly = t * (0.254829592 + t * (-0.284496736 + t * (1.421413741 + t * (-1.453152027 + t * 1.061405429))))
    e = 1.0 - poly * jnp.exp(-a * a)
    return jnp.where(v < 0, -e, e)


def _gelu_and_grad(v):
    cdf = 0.5 * (1.0 + _erf(v * (1.0 / math.sqrt(2.0))))
    pdf = jnp.exp(-0.5 * v * v) * (1.0 / math.sqrt(2.0 * math.pi))
    return v * cdf, cdf + v * pdf


def matmul(a, b, *, ta=False, tb=False, out_dtype=F32, residual=None, a_split=1, b_split=1, name):
    if a_split > 1 and ta:
        assert a.shape[0] == a_split
        K, M = a.shape[1], a_split * a.shape[2]
    elif a_split > 1:
        assert a.shape[0] == a_split
        M, K = a.shape[1], a_split * a.shape[2]
    elif ta:
        K, M = a.shape
    else:
        M, K = a.shape
    if b_split > 1:
        assert not tb and b.shape[0] == b_split
        Kb, N = b.shape[1], b_split * b.shape[2]
    elif tb:
        N, Kb = b.shape
    else:
        Kb, N = b.shape
    assert K == Kb, (a.shape, b.shape, ta, tb)
    has_res = residual is not None
    tm = _pick(M // a_split if ta else M, (1024, 1408, 768, 512, 256, 128))
    tn = _pick(N // b_split, (1024, 1408, 1536, 768, 512, 256, 128))

    def vmem_bytes(tk):
        tiles = tm * tk * a.dtype.itemsize + tk * tn * b.dtype.itemsize
        outs = tm * tn * jnp.dtype(out_dtype).itemsize + (tm * tn * 4 if has_res else 0)
        return 2 * tiles + 2 * outs + (tm * tn * 4 if tk < K else 0)

    kp = K if ta else K // a_split
    tk = next((t for t in (K, 2048, 1408, 1024, 512, 256) if t <= kp and kp % t == 0 and vmem_bytes(t) <= MATMUL_VMEM_BUDGET),
              _pick(kp, (128,)))
    nk = K // tk
    if a_split > 1 and ta:
        nib = M // a_split // tm
        a_spec = pl.BlockSpec((None, tk, tm), lambda i, j, k: (i // nib, k, i % nib))
    elif a_split > 1:
        nkb = kp // tk
        a_spec = pl.BlockSpec((None, tm, tk), lambda i, j, k: (k // nkb, i, k % nkb))
    else:
        a_spec = pl.BlockSpec((tk, tm), lambda i, j, k: (k, i)) if ta else pl.BlockSpec((tm, tk), lambda i, j, k: (i, k))
    if b_split > 1:
        njb = N // b_split // tn
        b_spec = pl.BlockSpec((None, tk, tn), lambda i, j, k: (j // njb, k, j % njb))
    else:
        b_spec = pl.BlockSpec((tn, tk), lambda i, j, k: (j, k)) if tb else pl.BlockSpec((tk, tn), lambda i, j, k: (k, j))
    o_spec = pl.BlockSpec((tm, tn), lambda i, j, k: (i, j))
    ca, cb = (0 if ta else 1), (1 if tb else 0)

    def body(*refs):
        a_ref, b_ref = refs[:2]
        r_ref = refs[2] if has_res else None
        o_ref = refs[3] if has_res else refs[2]

        def finish(r):
            if has_res:
                r = r + r_ref[...]
            o_ref[...] = r.astype(out_dtype)

        def part():
            return _dot(a_ref[...].astype(BF16), b_ref[...].astype(BF16), ca, cb)

        if nk == 1:
            finish(part())
            return
        acc = refs[-1]
        k = pl.program_id(2)

        @pl.when(k == 0)
        def _():
            acc[...] = part()

        @pl.when(jnp.logical_and(k > 0, k < nk - 1))
        def _():
            acc[...] += part()

        @pl.when(k == nk - 1)
        def _():
            finish(acc[...] + part())

    in_specs = [a_spec, b_spec] + ([o_spec] if has_res else [])
    args = (a, b) + ((residual,) if has_res else ())
    return pl.pallas_call(
        body, name=name, grid=(M // tm, N // tn, nk), in_specs=in_specs, out_specs=o_spec,
        out_shape=jax.ShapeDtypeStruct((M, N), out_dtype),
        scratch_shapes=[pltpu.VMEM((tm, tn), F32)] if nk > 1 else [],
        compiler_params=_params(("parallel", "parallel", "arbitrary")),
    )(*args)


def rms_fwd(x, gain, *, name):
    S, D = x.shape
    tr = _pick(S, (512, 256, 128))

    def body(x_ref, g_ref, o_ref):
        xv = x_ref[...]
        r = lax.rsqrt(jnp.mean(xv * xv, axis=1, keepdims=True) + EPS)
        o_ref[...] = (xv * r * g_ref[...]).astype(BF16)

    return pl.pallas_call(
        body, name=name, grid=(S // tr,),
        in_specs=[pl.BlockSpec((tr, D), lambda i: (i, 0)), pl.BlockSpec((1, D), lambda i: (0, 0))],
        out_specs=pl.BlockSpec((tr, D), lambda i: (i, 0)), out_shape=jax.ShapeDtypeStruct((S, D), BF16),
        compiler_params=_params(("parallel",)),
    )(x, gain)


def rms_bwd(x, gain, dh, dres, *, name):
    S, D = x.shape
    tr = _pick(S, (512, 256, 128))

    def body(x_ref, g_ref, dh_ref, dr_ref, dx_ref, dg_ref):
        @pl.when(pl.program_id(0) == 0)
        def _():
            dg_ref[...] = jnp.zeros_like(dg_ref)

        xv = x_ref[...]
        dhv = dh_ref[...]
        r = lax.rsqrt(jnp.mean(xv * xv, axis=1, keepdims=True) + EPS)
        xhat = xv * r
        t = dhv * g_ref[...]
        dx_ref[...] = dr_ref[...] + r * (t - xhat * jnp.mean(xhat * t, axis=1, keepdims=True))
        dg_ref[...] += jnp.sum(dhv * xhat, axis=0, keepdims=True)

    row = pl.BlockSpec((tr, D), lambda i: (i, 0))
    vec = pl.BlockSpec((1, D), lambda i: (0, 0))
    return pl.pallas_call(
        body, name=name, grid=(S // tr,), in_specs=[row, vec, row, row], out_specs=[row, vec],
        out_shape=[jax.ShapeDtypeStruct((S, D), F32), jax.ShapeDtypeStruct((1, D), F32)],
        compiler_params=_params(("arbitrary",)),
    )(x, gain, dh, dres)


def ffn_up_fwd(h, w_gu_t, *, name):
    S, K = h.shape
    F = w_gu_t.shape[0] // 2
    tm = _pick(S, (512, 256, 128))
    tn = _pick(F, (1408, 1024, 768, 512, 256, 128))
    nj = F // tn

    def body(h_ref, wg_ref, wu_ref, gu_ref, a_ref):
        hv = h_ref[...]
        g = _dot(hv, wg_ref[...], 1, 1)
        u = _dot(hv, wu_ref[...], 1, 1)
        gu_ref[0] = g
        gu_ref[1] = u
        a_ref[...] = (g * _sigmoid(g) * u).astype(BF16)

    return pl.pallas_call(
        body, name=name, grid=(nj, S // tm),
        in_specs=[pl.BlockSpec((tm, K), lambda j, i: (i, 0)), pl.BlockSpec((tn, K), lambda j, i: (j, 0)),
                  pl.BlockSpec((tn, K), lambda j, i: (nj + j, 0))],
        out_specs=[pl.BlockSpec((2, tm, tn), lambda j, i: (0, i, j)), pl.BlockSpec((tm, tn), lambda j, i: (i, j))],
        out_shape=[jax.ShapeDtypeStruct((2, S, F), F32), jax.ShapeDtypeStruct((S, F), BF16)],
        compiler_params=_params(("parallel", "parallel")),
    )(h, w_gu_t, w_gu_t)


def ffn_up_bwd(dy, w_down, gu, *, name):
    S, D = dy.shape
    F = w_down.shape[0]
    tm = _pick(S, (512, 256, 128))
    tn = _pick(F, (1408, 1024, 768, 512, 256, 128))

    def body(dy_ref, wd_ref, gu_ref, o_ref):
        da = _dot(dy_ref[...].astype(BF16), wd_ref[...], 1, 1)
        g = gu_ref[0]
        u = gu_ref[1]
        s = _sigmoid(g)
        o_ref[0] = (da * u * (s * (1.0 + g * (1.0 - s)))).astype(BF16)
        o_ref[1] = (da * g * s).astype(BF16)

    pair = pl.BlockSpec((2, tm, tn), lambda j, i: (0, i, j))
    return pl.pallas_call(
        body, name=name, grid=(F // tn, S // tm),
        in_specs=[pl.BlockSpec((tm, D), lambda j, i: (i, 0)), pl.BlockSpec((tn, D), lambda j, i: (j, 0)), pair],
        out_specs=pair, out_shape=jax.ShapeDtypeStruct((2, S, F), BF16),
        compiler_params=_params(("parallel", "parallel")),
    )(dy, w_down, gu)


def loss_head(y, target, *, name):
    S, D = y.shape
    tr = _pick(S, (512, 256, 128))

    def body(y_ref, t_ref, dy_ref, l_ref):
        @pl.when(pl.program_id(0) == 0)
        def _():
            l_ref[...] = jnp.zeros_like(l_ref)

        err = y_ref[...] - t_ref[...]
        dy_ref[...] = err * (1.0 / D)
        l_ref[...] += jnp.sum(0.5 * jnp.mean(err * err, axis=1, keepdims=True), axis=0, keepdims=True)

    row = pl.BlockSpec((tr, D), lambda i: (i, 0))
    one = pl.BlockSpec((1, 1), lambda i: (0, 0))
    dy, l = pl.pallas_call(
        body, name=name, grid=(S // tr,), in_specs=[row, row], out_specs=[row, one],
        out_shape=[jax.ShapeDtypeStruct((S, D), F32), jax.ShapeDtypeStruct((1, 1), F32)],
        compiler_params=_params(("arbitrary",)),
    )(y, target)
    return dy, l


def adamw(w, g, m, v, *, name):
    R, C = w.shape[-2:]
    tr = _pick(R, (512, 256, 128, 64, 32, 16, 8))

    def body(w_ref, g_ref, m_ref, v_ref, d_ref, mo_ref, vo_ref):
        gv = g_ref[...]
        mn = ADAM_B1 * m_ref[...] + (1.0 - ADAM_B1) * gv
        vn = ADAM_B2 * v_ref[...] + (1.0 - ADAM_B2) * jnp.square(gv)
        m_hat = mn / (1.0 - ADAM_B1 ** ADAM_STEP)
        v_hat = vn / (1.0 - ADAM_B2 ** ADAM_STEP)
        d_ref[...] = -ADAM_LR * (m_hat / (jnp.sqrt(v_hat) + ADAM_EPS) + ADAM_WD * w_ref[...])
        mo_ref[...] = mn
        vo_ref[...] = vn

    tc = C if tr < R or C % LANES else _pick(C, (256, 128))
    if w.ndim == 3:
        grid = (w.shape[0], R // tr, C // tc)
        blk = pl.BlockSpec((None, tr, tc), lambda l, i, j: (l, i, j))
    else:
        grid = (R // tr, C // tc)
        blk = pl.BlockSpec((tr, tc), lambda i, j: (i, j))
    sds = jax.ShapeDtypeStruct(w.shape, F32)
    return pl.pallas_call(
        body, name=name, grid=grid, in_specs=[blk] * 4, out_specs=[blk] * 3, out_shape=[sds] * 3,
        compiler_params=_params(("parallel",) * len(grid)),
    )(w, g, m, v)


def _tri(n, kind):
    r = lax.broadcasted_iota(jnp.int32, (n, n), 0)
    c = lax.broadcasted_iota(jnp.int32, (n, n), 1)
    if kind == "row_gt_col":
        return (r > c).astype(BF16)
    if kind == "row_ge_col":
        return (r >= c).astype(BF16)
    if kind == "row_le_col":
        return (r <= c).astype(BF16)
    raise ValueError(kind)


def _sb_tile(qs, kj, r_carry, u_strict, masked):
    z = _dot(qs, kj, 1, 1)
    lb = jnp.minimum(z, 0.0) - jnp.log(1.0 + jnp.exp(-jnp.abs(z)))
    l1m = lb - z
    keep = None
    if masked:
        tq, tk = z.shape
        keep = lax.broadcasted_iota(jnp.int32, (tq, tk), 1) < lax.broadcasted_iota(jnp.int32, (tq, tk), 0)
        l1m = jnp.where(keep, l1m, 0.0)
    w = jnp.exp(lb + _dot(l1m.astype(BF16), u_strict) + r_carry)
    if masked:
        w = jnp.where(keep, w, 0.0)
    return lb, l1m, w, keep


def _sb_prep(T, nb, hd, refs_in, gains, scratch):
    q_scale = 1.0 / math.sqrt(hd)
    assert math.log2(q_scale) == round(math.log2(q_scale))

    def prep(i, _):
        rows = pl.ds(pl.multiple_of(i * T, T), T)
        for hh in range(2):
            sl = slice(hd * hh, hd * hh + hd)
            for n, (src, dst) in enumerate(zip(refs_in, scratch)):
                v = src[rows, sl]
                if n < 2:
                    v = v * lax.rsqrt(jnp.mean(v * v, axis=1, keepdims=True) + EPS) * gains[n][...]
                if n == 0:
                    v = v * q_scale
                dst[hh, rows, :] = v.astype(BF16)
        return 0

    lax.fori_loop(0, nb, prep, 0)


def _sb_chains(m, T, nq):
    rows = [pl.ds(pl.multiple_of((nq * m + qb) * T, T), T) for qb in range(nq)]
    return rows, [(hh, qb) for qb in range(nq) for hh in range(2)]


def _sb_sweep(tile, carry, chains, m, nq):
    for kk in reversed(range(nq)):
        carry = tile(nq * m + kk, carry, [(ch, ch[1] == kk) for ch in chains if ch[1] >= kk])
    return lax.fori_loop(0, nq * m, lambda jj, c: tile(nq * m - 1 - jj, c, [(ch, False) for ch in chains]), carry)


def sb_attn_fwd(qkv, q_gain, k_gain, *, name, gather=None):
    S, D3 = qkv.shape
    D = D3 // 3
    npairs = D // LANES
    hd = SB_HEAD_DIM
    T = min(SB_TILE, S)
    nb = S // T
    nq = SB_FWD_QUERY_BLOCKS
    assert nb % nq == 0

    def body(*refs):
        if gather is None:
            q_ref, k_ref, v_ref, qg_ref, kg_ref, us_ref, o_ref, qn_s, kn_s, vb_s = refs
        else:
            ng = len(gather)
            q_ref, k_ref, v_ref, qg_ref, kg_ref, us_ref = refs[:6]
            o_ref = refs[6 + ng]
            qn_s, kn_s, vb_s = refs[7 + 2 * ng:10 + 2 * ng]
            comm = (refs[6:6 + ng], refs[7 + ng:7 + 2 * ng], refs[10 + 2 * ng:])
            step = pl.program_id(0)
            pl.when(step == 0)(lambda: _ag_phase(*comm, "start"))
            pl.when(step == npairs - 1)(lambda: _ag_phase(*comm, "forward"))
        us = us_ref[...]
        _sb_prep(T, nb, hd, (q_ref, k_ref, v_ref), (qg_ref, kg_ref), (qn_s, kn_s, vb_s))

        def superblock(m, _):
            rows_q, chains = _sb_chains(m, T, nq)
            qs = {ch: qn_s[ch[0], rows_q[ch[1]], :] for ch in chains}

            def tile(j, carry, which):
                rows_j = pl.ds(pl.multiple_of(j * T, T), T)
                new = dict(carry)
                for ch, masked in which:
                    acc, rc = carry[ch]
                    _, l1m, w, _ = _sb_tile(qs[ch], kn_s[ch[0], rows_j, :], rc, us, masked)
                    new[ch] = (acc + _dot(w.astype(BF16), vb_s[ch[0], rows_j, :]),
                               rc + jnp.sum(l1m, axis=1, keepdims=True))
                return new

            carry = {ch: (jnp.zeros((T, hd), F32), jnp.zeros((T, 1), F32)) for ch in chains}
            carry = _sb_sweep(tile, carry, chains, m, nq)
            for qb in range(nq):
                o_ref[rows_q[qb], :] = jnp.concatenate([carry[(0, qb)][0], carry[(1, qb)][0]], axis=1)
            return 0

        lax.fori_loop(0, nb // nq, superblock, 0)
        if gather is not None:
            pl.when(step == npairs - 1)(lambda: _ag_phase(*comm, "finish"))

    col = lambda off: pl.BlockSpec((S, LANES), lambda p, off=off: (0, off + p))
    gain = pl.BlockSpec((1, hd), lambda p: (0, 0))
    in_specs = [col(0), col(npairs), col(2 * npairs), gain, gain, pl.BlockSpec((T, T), lambda p: (0, 0))]
    out_specs = [pl.BlockSpec((S, LANES), lambda p: (0, p))]
    out_shape = [jax.ShapeDtypeStruct((S, D), F32)]
    scratch = [pltpu.VMEM((2, S, hd), BF16)] * 3
    args = [qkv, qkv, qkv, q_gain, k_gain, _tri(T, "row_gt_col")]
    if gather is not None:
        in_specs += [HBM_SPEC] * len(gather)
        out_specs += [HBM_SPEC] * len(gather)
        out_shape += [jax.ShapeDtypeStruct((N_DEV,) + s.shape, s.dtype) for s in gather]
        scratch += _copy_semaphores(len(gather))
        args += list(gather)
    out = pl.pallas_call(
        body, name=name, grid=(npairs,), in_specs=in_specs, out_specs=out_specs, out_shape=out_shape,
        scratch_shapes=scratch, compiler_params=_params(("arbitrary",)),
    )(*args)
    return out[0] if gather is None else (out[0], list(out[1:]))


def sb_attn_bwd(qkv, o, do, q_gain, k_gain, *, name, scatter=None):
    S, D3 = qkv.shape
    D = D3 // 3
    npairs = D // LANES
    hd = SB_HEAD_DIM
    T = min(SB_TILE, S)
    nb = S // T
    nq = SB_BWD_QUERY_BLOCKS
    assert nb % nq == 0
    scale = 1.0 / math.sqrt(hd)

    def body(*refs):
        if scatter is None:
            (q_ref, k_ref, v_ref, o_ref, do_ref, qg_ref, kg_ref, us_ref,
             dqkv_ref, dg_ref, qn_s, kn_s, vb_s, dob_s) = refs
        else:
            ns = len(scatter)
            q_ref, k_ref, v_ref, o_ref, do_ref, qg_ref, kg_ref, us_ref = refs[:8]
            rs_in = refs[8:8 + ns]
            dqkv_ref, dg_ref = refs[8 + ns:10 + ns]
            rs_out = refs[10 + ns:10 + 2 * ns]
            qn_s, kn_s, vb_s, dob_s = refs[10 + 2 * ns:14 + 2 * ns]
            rs_sems = refs[14 + 2 * ns:]
            pl.when(pl.program_id(0) == 0)(lambda: _rs_phase(rs_in, rs_out, rs_sems, "start"))
        dq_ref, dk_ref, dv_ref = dqkv_ref.at[0], dqkv_ref.at[1], dqkv_ref.at[2]

        @pl.when(pl.program_id(0) == 0)
        def _():
            dg_ref[...] = jnp.zeros_like(dg_ref)

        us = us_ref[...]
        u_prefix = (1.0 - us.astype(F32)).astype(BF16)
        _sb_prep(T, nb, hd, (q_ref, k_ref, v_ref, do_ref), (qg_ref, kg_ref), (qn_s, kn_s, vb_s, dob_s))
        dk_ref[...] = jnp.zeros_like(dk_ref)
        dv_ref[...] = jnp.zeros_like(dv_ref)

        def superblock(m, _):
            rows_q, chains = _sb_chains(m, T, nq)
            qs = {ch: qn_s[ch[0], rows_q[ch[1]], :] for ch in chains}
            doi ={ch: dob_s[ch[0], rows_q[ch[1]], :] for ch in chains}
            dt_total = {ch: jnp.sum(doi[ch].astype(F32) * o_ref[rows_q[ch[1]], hd * ch[0]:hd * ch[0] + hd],
                                    axis=1, keepdims=True) for ch in chains}

            def tile(j, carry, which):
                rows_j = pl.ds(pl.multiple_of(j * T, T), T)
                new = dict(carry)
                dk_part, dv_part = {}, {}
                for ch, masked in which:
                    hh = ch[0]
                    dq_acc, rc, gc = carry[ch]
                    kj = kn_s[hh, rows_j, :]
                    lb, l1m, w, keep = _sb_tile(qs[ch], kj, rc, us, masked)
                    wb = w.astype(BF16)
                    g = _dot(doi[ch], vb_s[hh, rows_j, :], 1, 1) * wb.astype(F32)
                    g_row = jnp.sum(g, axis=1, keepdims=True)
                    g_upto = (dt_total[ch] - gc - g_row) + _dot(g.astype(BF16), u_prefix)
                    dz = g - g_upto * jnp.exp(lb)
                    if masked:
                        dz = jnp.where(keep, dz, 0.0)
                    dzb = dz.astype(BF16)
                    dv_part[hh] = dv_part.get(hh, 0.0) + _dot(wb, doi[ch], 0, 0)
                    dk_part[hh] = dk_part.get(hh, 0.0) + _dot(dzb, qs[ch], 0, 0)
                    new[ch] = (dq_acc + _dot(dzb, kj), rc + jnp.sum(l1m, axis=1, keepdims=True),
                               gc + g_row)
                dv_ref[rows_j, :] += jnp.concatenate([dv_part[0], dv_part[1]], axis=1)
                dk_ref[rows_j, :] += jnp.concatenate([dk_part[0], dk_part[1]], axis=1)
                return new

            zero1 = jnp.zeros((T, 1), F32)
            carry = {ch: (jnp.zeros((T, hd), F32), zero1, zero1) for ch in chains}
            carry = _sb_sweep(tile, carry, chains, m, nq)
            for qb in range(nq):
                dq_ref[rows_q[qb], :] = jnp.concatenate([carry[(0, qb)][0], carry[(1, qb)][0]], axis=1) * scale
            return 0

        lax.fori_loop(0, nb // nq, superblock, 0)

        def finish(i, carry):
            rows = pl.ds(pl.multiple_of(i * T, T), T)
            new = []
            for hh in range(2):
                sl = slice(hd * hh, hd * hh + hd)
                outs = []
                for raw_ref, gain_ref, dn in ((q_ref, qg_ref, dq_ref[rows, sl]), (k_ref, kg_ref, dk_ref[rows, sl])):
                    raw = raw_ref[rows, sl]
                    r = lax.rsqrt(jnp.mean(raw * raw, axis=1, keepdims=True) + EPS)
                    hat = raw * r
                    t = dn * gain_ref[...]
                    outs.append((r * (t - hat * jnp.mean(hat * t, axis=1, keepdims=True)),
                                 jnp.sum(dn * hat, axis=0, keepdims=True)))
                dq_ref[rows, sl] = outs[0][0]
                dk_ref[rows, sl] = outs[1][0]
                new.append((carry[hh][0] + outs[0][1], carry[hh][1] + outs[1][1]))
            return tuple(new)

        zg = (jnp.zeros((1, hd), F32), jnp.zeros((1, hd), F32))
        tot = lax.fori_loop(0, nb, finish, (zg, zg))
        dg_ref[0:1, 0:hd] += tot[0][0] + tot[1][0]
        dg_ref[1:2, 0:hd] += tot[0][1] + tot[1][1]
        if scatter is not None:
            pl.when(pl.program_id(0) == npairs - 1)(lambda: _rs_phase(rs_in, rs_out, rs_sems, "finish"))

    col = lambda off: pl.BlockSpec((S, LANES), lambda p, off=off: (0, off + p))
    gain = pl.BlockSpec((1, hd), lambda p: (0, 0))
    tri = pl.BlockSpec((T, T), lambda p: (0, 0))
    pair = pl.BlockSpec((S, LANES), lambda p: (0, p))
    in_specs = [col(0), col(npairs), col(2 * npairs), pair, pair, gain, gain, tri]
    out_specs = [pl.BlockSpec((3, S, LANES), lambda p: (0, 0, p)), pl.BlockSpec((8, LANES), lambda p: (0, 0))]
    out_shape = [jax.ShapeDtypeStruct((3, S, D), F32), jax.ShapeDtypeStruct((8, LANES), F32)]
    scratch = [pltpu.VMEM((2, S, hd), BF16)] * 4
    args = [qkv, qkv, qkv, o, do, q_gain, k_gain, _tri(T, "row_gt_col")]
    if scatter is not None:
        in_specs += [HBM_SPEC] * len(scatter)
        out_specs += [HBM_SPEC] * len(scatter)
        out_shape += [jax.ShapeDtypeStruct(g.shape, g.dtype) for g in scatter]
        scratch += _copy_semaphores(len(scatter))
        args += list(scatter)
    out = pl.pallas_call(
        body, name=name, grid=(npairs,), in_specs=in_specs, out_specs=out_specs, out_shape=out_shape,
        scratch_shapes=scratch, compiler_params=_params(("arbitrary",)),
    )(*args)
    res = (out[0], out[1][0:1, :hd], out[1][1:2, :hd])
    return res if scatter is None else res + (list(out[2:]),)


def gmlp_fwd(zzpre, b_in, v_gain, wc, bsf, *, name):
    S, H2 = zzpre.shape
    H = H2 // 2
    G, T, _ = wc.shape
    gd = H // G

    def body(z_ref, b_ref, vg_ref, wc_ref, bs_ref, p_ref):
        zz, _ = _gelu_and_grad(z_ref[...] + b_ref[...])
        u = zz[:, :H]
        v = zz[:, H:]
        vn = v * lax.rsqrt(jnp.mean(v * v, axis=1, keepdims=True) + EPS) * vg_ref[...]
        for g in range(G):
            gs = slice(g * gd, (g + 1) * gd)
            mixed = _dot(wc_ref[g], vn[:, gs].astype(BF16)) + bs_ref[g]
            p_ref[:, gs] = (u[:, gs] * mixed).astype(BF16)

    full3 = lambda shp: pl.BlockSpec(shp, lambda c: (0, 0, 0))
    return pl.pallas_call(
        body, name=name, grid=(S // T,),
        in_specs=[pl.BlockSpec((T, H2), lambda c: (c, 0)), pl.BlockSpec((1, H2), lambda c: (0, 0)),
                  pl.BlockSpec((1, H), lambda c: (0, 0)), full3((G, T, T)), full3((G, T, gd))],
        out_specs=pl.BlockSpec((T, H), lambda c: (c, 0)), out_shape=jax.ShapeDtypeStruct((S, H), BF16),
        compiler_params=_params(("parallel",)),
    )(zzpre, b_in, v_gain, wc, bsf)


def gmlp_bwd(zzpre, b_in, v_gain, wc, bsf, dp, *, name):
    S, H2 = zzpre.shape
    H = H2 // 2
    G, T, _ = wc.shape
    gd = H // G
    assert G <= LANES

    def body(z_ref, b_ref, vg_ref, wc_ref, bs_ref, dp_ref, dzz_ref, db_ref, dvg_ref, dws_ref, dbs_ref):
        @pl.when(pl.program_id(0) == 0)
        def _():
            db_ref[...] = jnp.zeros_like(db_ref)
            dvg_ref[...] = jnp.zeros_like(dvg_ref)
            dws_ref[...] = jnp.zeros_like(dws_ref)
            dbs_ref[...] = jnp.zeros_like(dbs_ref)

        zz, gp = _gelu_and_grad(z_ref[...] + b_ref[...])
        u = zz[:, :H]
        v = zz[:, H:]
        r = lax.rsqrt(jnp.mean(v * v, axis=1, keepdims=True) + EPS)
        vhat = v * r
        vg = vg_ref[...]
        vn = vhat * vg
        dpv = dp_ref[...]
        tril = lax.broadcasted_iota(jnp.int32, (T, T), 1) <= lax.broadcasted_iota(jnp.int32, (T, T), 0)
        lane = lax.broadcasted_iota(jnp.int32, (T, LANES), 1)
        dbs = jnp.zeros((T, LANES), F32)
        du_parts, dvn_parts = [], []
        for g in range(G):
            gs = slice(g * gd, (g + 1) * gd)
            vng = vn[:, gs].astype(BF16)
            wcg = wc_ref[g]
            mixed = _dot(wcg, vng) + bs_ref[g]
            dpg = dpv[:, gs]
            du_parts.append(dpg * mixed)
            dmx = dpg * u[:, gs]
            dmxb = dmx.astype(BF16)
            dvn_parts.append(_dot(wcg, dmxb, 0, 0))
            dws_ref[g] += jnp.where(tril, _dot(dmxb, vng, 1, 1), 0.0)
            dbs = dbs + jnp.where(lane == g, jnp.sum(dmx, axis=1, keepdims=True), 0.0)
        dbs_ref[...] += dbs
        du = jnp.concatenate(du_parts, axis=1)
        dvn = jnp.concatenate(dvn_parts, axis=1)
        dvg_ref[...] += jnp.sum(dvn * vhat, axis=0, keepdims=True)
        t = dvn * vg
        dv = r * (t - vhat * jnp.mean(vhat * t, axis=1, keepdims=True))
        dzu = du * gp[:, :H]
        dzv = dv * gp[:, H:]
        dzz_ref[:, :H] = dzu.astype(BF16)
        dzz_ref[:, H:] = dzv.astype(BF16)
        db_ref[:, :H] += jnp.sum(dzu, axis=0, keepdims=True)
        db_ref[:, H:] += jnp.sum(dzv, axis=0, keepdims=True)

    full3 = lambda shp: pl.BlockSpec(shp, lambda c: (0, 0, 0))
    vec = lambda n: pl.BlockSpec((1, n), lambda c: (0, 0))
    return pl.pallas_call(
        body, name=name, grid=(S // T,),
        in_specs=[pl.BlockSpec((T, H2), lambda c: (c, 0)), vec(H2), vec(H), full3((G, T, T)), full3((G, T, gd)),
                  pl.BlockSpec((T, H), lambda c: (c, 0))],
        out_specs=[pl.BlockSpec((T, H2), lambda c: (c, 0)), vec(H2), vec(H), full3((G, T, T)),
                   pl.BlockSpec((T, LANES), lambda c: (0, 0))],
        out_shape=[jax.ShapeDtypeStruct((S, H2), BF16), jax.ShapeDtypeStruct((1, H2), F32),
                   jax.ShapeDtypeStruct((1, H), F32), jax.ShapeDtypeStruct((G, T, T), F32),
                   jax.ShapeDtypeStruct((T, LANES), F32)],
        compiler_params=_params(("arbitrary",)),
    )(zzpre, b_in, v_gain, wc, bsf, dp)


def _shift_rows(v, k, n_rows):
    if k == 0:
        return v
    rolled = pltpu.roll(v, k % n_rows, 0)
    row = lax.broadcasted_iota(jnp.int32, v.shape, 0)
    keep = (row >= k) if k > 0 else (row < n_rows + k)
    return jnp.where(keep, rolled, 0.0)


def conv_fwd(zx, conv_w, conv_b, col0, *, name):
    S = zx.shape[0]
    C = conv_w.shape[1]
    tc = _pick(C, (256, 128))
    off = col0 // tc
    assert col0 % tc == 0

    def body(x_ref, w_ref, b_ref, o_ref):
        xv = x_ref[...]
        acc = b_ref[...] + w_ref[SSM_CONV - 1:SSM_CONV, :] * xv
        for k in range(SSM_CONV - 1):
            acc = acc + w_ref[k:k + 1, :] * _shift_rows(xv, SSM_CONV - 1 - k, S)
        o_ref[...] = acc * _sigmoid(acc)

    return pl.pallas_call(
        body, name=name, grid=(C // tc,),
        in_specs=[pl.BlockSpec((S, tc), lambda j: (0, off + j)), pl.BlockSpec((SSM_CONV, tc), lambda j: (0, j)),
                  pl.BlockSpec((1, tc), lambda j: (0, j))],
        out_specs=pl.BlockSpec((S, tc), lambda j: (0, j)), out_shape=jax.ShapeDtypeStruct((S, C), F32),
        compiler_params=_params(("parallel",)),
    )(zx, conv_w, conv_b)


def conv_bwd(zx, conv_w, conv_b, col0, douts, *, name):
    S = zx.shape[0]
    C = conv_w.shape[1]
    tc = LANES
    off = col0 // tc
    counts = [d.shape[1] // tc for d in douts]
    starts = [sum(counts[:p]) for p in range(len(douts))]
    assert sum(counts) * tc == C and all(d.shape[1] % tc == 0 for d in douts)

    def body(x_ref, w_ref, b_ref, *rest):
        do_refs, (dx_ref, dw_ref, db_ref) = rest[:len(douts)], rest[len(douts):]
        j = pl.program_id(0)
        dov = do_refs[-1][...]
        for p in reversed(range(len(douts) - 1)):
            dov = jnp.where(j < starts[p + 1], do_refs[p][...], dov)
        xv = x_ref[...]
        shifted = [_shift_rows(xv, SSM_CONV - 1 - k, S) for k in range(SSM_CONV)]
        acc = b_ref[...]
        for k in range(SSM_CONV):
            acc = acc + w_ref[k:k + 1, :] * shifted[k]
        s = _sigmoid(acc)
        dacc = dov * (s * (1.0 + acc * (1.0 - s)))
        db_ref[...] = jnp.sum(dacc, axis=0, keepdims=True)
        dx = jnp.zeros_like(xv)
        for k in range(SSM_CONV):
            dw_ref[k:k + 1, :] = jnp.sum(dacc * shifted[k], axis=0, keepdims=True)
            dx = dx + w_ref[k:k + 1, :] * _shift_rows(dacc, -(SSM_CONV - 1 - k), S)
        dx_ref[...] = dx

    slab = pl.BlockSpec((S, tc), lambda j: (0, j))
    piece_specs = [pl.BlockSpec((S, tc), lambda j, a=starts[p], n=counts[p]: (0, jnp.clip(j - a, 0, n - 1)))
                   for p in range(len(douts))]
    return pl.pallas_call(
        body, name=name, grid=(C // tc,),
        in_specs=[pl.BlockSpec((S, tc), lambda j: (0, off + j)), pl.BlockSpec((SSM_CONV, tc), lambda j: (0, j)),
                  pl.BlockSpec((1, tc), lambda j: (0, j))] + piece_specs,
        out_specs=[slab, pl.BlockSpec((SSM_CONV, tc), lambda j: (0, j)), pl.BlockSpec((1, tc), lambda j: (0, j))],
        out_shape=[jax.ShapeDtypeStruct((S, C), F32), jax.ShapeDtypeStruct((SSM_CONV, C), F32),
                   jax.ShapeDtypeStruct((1, C), F32)],
        compiler_params=_params(("arbitrary",)),
    )(zx, conv_w, conv_b, *douts)


def _ssd_chunk_terms(dtraw, bias, a_log, tl):
    dt = _softplus(dtraw + bias)
    a_neg = -jnp.exp(a_log)
    ac = _dot_exact_left(tl, dt * a_neg)
    ac_last = ac[ac.shape[0] - 1:, :]
    return dt, a_neg, ac, ac.T, jnp.exp(ac), jnp.exp(ac_last - ac), jnp.exp(ac_last)


def _ssd_specs(S, L, G, hpg, pd, inner, gp):
    gw = hpg * pd
    n = SSM_STATE
    xb = inner // n
    assert G % gp == 0 and xb % gp == 0 and (xb + G) % gp == 0

    def mk(cidx):
        return dict(
            x=pl.BlockSpec((L, gp * gw), lambda g, c: (cidx(c), g)),
            b=pl.BlockSpec((L, gp * n), lambda g, c: (cidx(c), xb // gp + g)),
            c=pl.BlockSpec((L, gp * n), lambda g, c: (cidx(c), (xb + G) // gp + g)),
            z=pl.BlockSpec((L, gp * gw), lambda g, c: (cidx(c), g)),
            dt=pl.BlockSpec((L, gp * LANES), lambda g, c: (cidx(c), g)),
            gvec=pl.BlockSpec((gp, 1, LANES), lambda g, c: (g, 0, 0)),
            chan=pl.BlockSpec((1, gp * gw), lambda g, c: (0, g)),
            tri=pl.BlockSpec((L, L), lambda g, c: (0, 0)),
            hp=pl.BlockSpec((gp, 1, gw, n), lambda g, c: (g, cidx(c), 0, 0)),
            bc=pl.BlockSpec((L, gp * n), lambda g, c: (cidx(c), g)),
        )
    return mk


def _ssd_group_views(refs, kinds, gg, gw):
    n = SSM_STATE
    width = dict(x=gw, z=gw, chan=gw, b=n, c=n, bc=n, dt=LANES)
    out = []
    for ref, kind in zip(refs, kinds):
        if kind in width:
            out.append(ref.at[:, gg * width[kind]:(gg + 1) * width[kind]])
        elif kind in ("gvec", "hp"):
            out.append(ref.at[gg:gg + 1])
        elif kind == "state":
            out.append(ref.at[gg * gw:(gg + 1) * gw])
        else:
            out.append(ref)
    return out


def ssd_fwd(xbc, zx, dtg, bias_g, alog_g, d_chan, ngain, L, G, *, name):
    S = xbc.shape[0]
    n = SSM_STATE
    inner = xbc.shape[1] - 2 * G * n
    gw = inner // G
    pd = SB_HEAD_DIM
    hpg = gw // pd
    nc = S // L
    gp = SSD_FWD_GROUPS_PER_STEP
    sp = _ssd_specs(S, L, G, hpg, pd, inner, gp)(lambda c: c)

    kinds = ("x", "b", "c", "z", "dt", "gvec", "gvec", "chan", "chan", "tri", "x", "x", "hp", "state")

    def body(*refs):
        @pl.when(pl.program_id(1) == 0)
        def _():
            refs[-1][...] = jnp.zeros_like(refs[-1])

        for gg in range(gp):
            group_body(*_ssd_group_views(refs, kinds, gg, gw))

    def group_body(x_ref, b_ref, c_ref, z_ref, dt_ref, bias_ref, alog_ref, d_ref, ng_ref, tl_ref,
                   yn_ref, y_ref, hp_ref, state):
        dt, _, ac, act, ea, dte, cd = _ssd_chunk_terms(dt_ref[...], bias_ref[0], alog_ref[0], tl_ref[...])
        xv = x_ref[...]
        bm = b_ref[...].astype(BF16)
        cm = c_ref[...].astype(BF16)
        cb = _dot(cm, bm, 1, 1)
        tril = lax.broadcasted_iota(jnp.int32, (L, L), 1) <= lax.broadcasted_iota(jnp.int32, (L, L), 0)
        hp_ref[0, 0] = state[...]
        for r in range(hpg):
            ps = slice(r * pd, (r + 1) * pd)
            xr = xv[:, ps]
            xdt = xr * dt[:, r:r + 1]
            lm = jnp.exp(jnp.where(tril, ac[:, r:r + 1] - act[r:r + 1, :], -jnp.inf))
            hprev = state[ps, :]
            y = _dot((cb * lm).astype(BF16), xdt.astype(BF16))
            y = y + _dot(cm, hprev.astype(BF16), 1, 1) * ea[:, r:r + 1]
            y_ref[:, ps] = y + xr * d_ref[:, ps]
            st = _dot((xdt * dte[:, r:r + 1]).astype(BF16), bm, 0, 0)
            state[ps, :] = hprev * cd[:, r:r + 1] + st
        yfull = y_ref[...]
        zg = z_ref[...]
        yg = yfull * (zg * _sigmoid(zg))
        yn_ref[...] = (yg * lax.rsqrt(jnp.mean(yg * yg, axis=1, keepdims=True) + EPS) * ng_ref[...]).astype(BF16)

    return pl.pallas_call(
        body, name=name, grid=(G // gp, nc),
        in_specs=[sp["x"], sp["b"], sp["c"], sp["z"], sp["dt"], sp["gvec"], sp["gvec"], sp["chan"], sp["chan"], sp["tri"]],
        out_specs=[sp["x"], sp["x"], sp["hp"]],
        out_shape=[jax.ShapeDtypeStruct((S, inner), BF16), jax.ShapeDtypeStruct((S, inner), F32),
                   jax.ShapeDtypeStruct((G, nc, gw, n), F32)],
        scratch_shapes=[pltpu.VMEM((gp * gw, n), F32)],
        compiler_params=_params(("arbitrary", "arbitrary")),
    )(xbc, xbc, xbc, zx, dtg, bias_g, alog_g, d_chan, ngain, _tri(L, "row_ge_col"))


def ssd_bwd(xbc, zx, dtg, bias_g, alog_g, d_chan, ngain, yfull, hp, dyn, L, G, *, name):
    S = xbc.shape[0]
    n = SSM_STATE
    inner = xbc.shape[1] - 2 * G * n
    gw = inner // G
    pd = SB_HEAD_DIM
    hpg = gw // pd
    nc = S // L
    gp = SSD_BWD_GROUPS_PER_STEP
    sp = _ssd_specs(S, L, G, hpg, pd, inner, gp)(lambda c: nc - 1 - c)

    kinds = ("x", "b", "c", "z", "dt", "gvec", "gvec", "chan", "chan", "tri", "tri", "x", "hp", "x",
             "x", "x", "bc", "bc", "dt", "gvec", "gvec", "gvec", "chan", "state")

    def body(*refs):
        @pl.when(pl.program_id(1) == 0)
        def _():
            for acc in refs[-5:]:
                acc[...] = jnp.zeros_like(acc)

        for gg in range(gp):
            group_body(*_ssd_group_views(refs, kinds, gg, gw))

    def group_body(x_ref, b_ref, c_ref, z_ref, dt_ref, bias_ref, alog_ref, d_ref, ng_ref, tl_ref, tu_ref,
                   yf_ref, hp_ref, dyn_ref,
                   dz_ref, dx_ref, db_ref, dc_ref, ddt_ref, dbias_ref, dalog_ref, dd_ref, dng_ref, dstate):

        dtraw = dt_ref[...]
        dt, a_neg, ac, act, ea, dte, cd = _ssd_chunk_terms(dtraw, bias_ref[0], alog_ref[0], tl_ref[...])
        xv = x_ref[...]
        bm = b_ref[...].astype(BF16)
        cm = c_ref[...].astype(BF16)
        cb = _dot(cm, bm, 1, 1)
        tril = lax.broadcasted_iota(jnp.int32, (L, L), 1) <= lax.broadcasted_iota(jnp.int32, (L, L), 0)
        lane = lax.broadcasted_iota(jnp.int32, (L, LANES), 1)
        lane1 = lax.broadcasted_iota(jnp.int32, (1, LANES), 1)

        yfull = yf_ref[...]
        zg = z_ref[...]
        sg = _sigmoid(zg)
        gate = zg * sg
        yg = yfull * gate
        rr = lax.rsqrt(jnp.mean(yg * yg, axis=1, keepdims=True) + EPS)
        yhat = yg * rr
        dynv = dyn_ref[...]
        dng_ref[...] += jnp.sum(dynv * yhat, axis=0, keepdims=True)
        t = dynv * ng_ref[...]
        dyg = rr * (t - yhat * jnp.mean(yhat * t, axis=1, keepdims=True))
        dy = dyg * gate
        dz_ref[...] = dyg * yfull * (sg * (1.0 + zg * (1.0 - sg)))

        dcb = jnp.zeros((L, L), F32)
        dc_acc = jnp.zeros((L, n), F32)
        db_acc = jnp.zeros((L, n), F32)
        dac = jnp.zeros((L, LANES), F32)
        xdx = jnp.zeros((L, LANES), F32)
        tail = jnp.zeros((1, LANES), F32)
        dskip = jnp.zeros((1, LANES), F32)
        ones_l = jnp.ones((L, LANES), BF16)
        for r in range(hpg):
            ps = slice(r * pd, (r + 1) * pd)
            xr = xv[:, ps]
            dyr = dy[:, ps]
            dtr = dt[:, r:r + 1]
            dter = dte[:, r:r + 1]
            cdr = cd[:, r:r + 1]
            xdt = xr * dtr
            xdtb = xdt.astype(BF16)
            dyrb = dyr.astype(BF16)
            lm = jnp.exp(jnp.where(tril, ac[:, r:r + 1] - act[r:r + 1, :], -jnp.inf))
            m32 = cb * lm
            mb = m32.astype(BF16)
            hprev = hp_ref[0, 0, ps, :]
            hpb = hprev.astype(BF16)
            dhn = dstate[ps, :]
            dhnb = dhn.astype(BF16)
            ear = ea[:, r:r + 1]
            gy = (dyr * ear).astype(BF16)
            dc_acc = dc_acc + _dot(gy, hpb)
            dstate[ps, :] = _dot(gy, cm, 0, 0) + dhn * cdr
            bdh = _dot(bm, dhnb, 1, 1)
            db_acc = db_acc + _dot((xdt * dter).astype(BF16), dhnb)
            dm = _dot(dyrb, xdtb, 1, 1)
            dxdt = bdh * dter + _dot(mb, dyrb, 0, 0)
            dcb = dcb + dm * lm
            wmat = dm * m32
            whi = wmat.astype(BF16)
            wlo = (wmat - whi.astype(F32)).astype(BF16)
            col_w = _dot(whi, ones_l, 0, 0) + _dot(wlo, ones_l, 0, 0)
            t_end = xdt * bdh * dter
            e_r = jnp.sum(wmat, axis=1, keepdims=True) \
                + jnp.sum(dyr * _dot(cm, hpb, 1, 1) * ear - t_end, axis=1, keepdims=True)
            c_r = cdr * _sum_all(dhn * hprev) + _sum_all(t_end)
            dac = dac + jnp.where(lane == r, e_r - col_w, 0.0)
            xdx = xdx + jnp.where(lane == r, jnp.sum(dxdt * xr, axis=1, keepdims=True), 0.0)
            tail = tail + jnp.where(lane1 == r, c_r, 0.0)
            dskip = dskip + jnp.where(lane1 == r, _sum_all(dyr * xr), 0.0)
            dx_ref[:, ps] = dxdt * dtr + dyr * d_ref[:, ps]
        dcbb = dcb.astype(BF16)
        dc_ref[...] = dc_acc + _dot(dcbb, bm)
        db_ref[...] = db_acc + _dot(dcbb, cm, 0, 0)
        da = _dot_exact_left(tu_ref[...], dac) + tail
        real = lane < hpg
        ddt = jnp.where(real, (da * a_neg + xdx) * _sigmoid(dtraw + bias_ref[0]), 0.0)
        ddt_ref[...] = ddt
        dd_ref[0] += dskip
        dbias_ref[0] += jnp.sum(ddt, axis=0, keepdims=True)
        dalog_ref[0] += jnp.where(lane1 < hpg, jnp.sum(da * dt, axis=0, keepdims=True) * a_neg, 0.0)

    return pl.pallas_call(
        body, name=name, grid=(G // gp, nc),
        in_specs=[sp["x"], sp["b"], sp["c"], sp["z"], sp["dt"], sp["gvec"], sp["gvec"], sp["chan"], sp["chan"],
                  sp["tri"], sp["tri"], sp["x"], sp["hp"], sp["x"]],
        out_specs=[sp["x"], sp["x"], sp["bc"], sp["bc"], sp["dt"], sp["gvec"], sp["gvec"], sp["gvec"], sp["chan"]],
        out_shape=[jax.ShapeDtypeStruct((S, inner), F32), jax.ShapeDtypeStruct((S, inner), F32),
                   jax.ShapeDtypeStruct((S, G * n), F32), jax.ShapeDtypeStruct((S, G * n), F32),
                   jax.ShapeDtypeStruct((S, G * LANES), F32), jax.ShapeDtypeStruct((G, 1, LANES), F32),
                   jax.ShapeDtypeStruct((G, 1, LANES), F32), jax.ShapeDtypeStruct((G, 1, LANES), F32),
                   jax.ShapeDtypeStruct((1, inner), F32)],
        scratch_shapes=[pltpu.VMEM((gp * gw, n), F32)],
        compiler_params=_params(("arbitrary", "arbitrary")),
    )(xbc, xbc, xbc, zx, dtg, bias_g, alog_g, d_chan, ngain, _tri(L, "row_ge_col"), _tri(L, "row_le_col"),
      yfull, hp, dyn)


def _spread_dt(w_dt_t, G, hpg):
    K = w_dt_t.shape[1]
    w = w_dt_t.reshape(G, hpg, K)
    return jnp.pad(w, ((0, 0), (0, LANES - hpg), (0, 0))).reshape(G * LANES, K)


def _group_vec(v, G, hpg):
    return jnp.pad(v.reshape(G, 1, hpg), ((0, 0), (0, 0), (0, LANES - hpg)))


def local_step(x, target, W, late=None):
    S, D = x.shape
    depth = W["mix_norm"].shape[0]
    gm_groups, gm_chunk = W["gm_w_s"].shape[1], W["gm_w_s"].shape[2]
    heads = W["ssm_dt_bias"].shape[1]
    inner = heads * SB_HEAD_DIM
    L = gm_chunk
    received = None

    saved = []
    for i in range(depth):
        kind, j = i % 3, i // 3
        s = dict(x=x)
        h = rms_fwd(x, W["mix_norm"][i:i + 1], name="rms_mix_fwd")
        s["h"] = h
        if kind == 0:
            qkv = matmul(h, W["sb_w_qkv"][j], tb=True, name="mm_qkv")
            if late is not None and i == 0:
                o, gathered = sb_attn_fwd(qkv, W["sb_q_gain"][j:j + 1], W["sb_k_gain"][j:j + 1], name="sb_fwd_gather",
                                          gather=late.shards)
                late.fill(W, gathered)
            else:
                o = sb_attn_fwd(qkv, W["sb_q_gain"][j:j + 1], W["sb_k_gain"][j:j + 1], name="sb_fwd")
            x1 = matmul(o, W["sb_w_o"][j], residual=x, name="mm_sb_out")
            s.update(qkv=qkv, o=o)
        elif kind == 1:
            wc = jnp.where(jnp.tril(jnp.ones((gm_chunk, gm_chunk), bool)), W["gm_w_s"][j], 0.0).astype(BF16)
            bsf = jnp.broadcast_to(W["gm_b_s"][j][:, :, None], (gm_groups, gm_chunk, W["gm_v_gain"].shape[1] // gm_groups)).astype(F32)
            zzpre = matmul(h, W["gm_w_in"][j], tb=True, name="mm_gm_in")
            p = gmlp_fwd(zzpre, W["gm_b_in"][j:j + 1], W["gm_v_gain"][j:j + 1], wc, bsf, name="gm_fwd")
            x1 = matmul(p, W["gm_w_out"][j], residual=x, name="mm_gm_out")
            s.update(zzpre=zzpre, p=p, wc=wc, bsf=bsf)
        else:
            conv_dim = W["ssm_conv_w"].shape[2]
            G = (conv_dim - inner) // (2 * SSM_STATE)
            hpg = heads // G
            w_in = W["ssm_w_in"][j]
            w_zx = w_in[:inner + conv_dim]
            w_dtg = _spread_dt(w_in[inner + conv_dim:], G, hpg)
            bias_g = _group_vec(W["ssm_dt_bias"][j], G, hpg)
            alog_g = _group_vec(W["ssm_a_log"][j], G, hpg)
            d_chan = jnp.repeat(W["ssm_d"][j], SB_HEAD_DIM)[None, :]
            ngain = W["ssm_norm_gain"][j:j + 1]
            zx = matmul(h, w_zx, tb=True, name="mm_ssm_zx")
            dtg = matmul(h, w_dtg, tb=True, name="mm_ssm_dt")
            xbc = conv_fwd(zx, W["ssm_conv_w"][j], W["ssm_conv_b"][j:j + 1], inner, name="conv_fwd")
            yn, yfull, hp = ssd_fwd(xbc, zx, dtg, bias_g, alog_g, d_chan, ngain, L, G, name="ssd_fwd")
            x1 = matmul(yn, W["ssm_w_out"][j], residual=x, name="mm_ssm_out")
            s.update(w_zx=w_zx, w_dtg=w_dtg, bias_g=bias_g, alog_g=alog_g, d_chan=d_chan, ngain=ngain,
                     zx=zx, dtg=dtg, xbc=xbc, yn=yn, yfull=yfull, hp=hp)
        h2 = rms_fwd(x1, W["ffn_norm"][i:i + 1], name="rms_ffn_fwd")
        gu, a = ffn_up_fwd(h2, W["ffn_w_gu"][i], name="ffn_up_fwd")
        x2 = matmul(a, W["ffn_w_down"][i], residual=x1, name="mm_ffn_down")
        s.update(x1=x1, h2=h2, gu=gu, a=a)
        saved.append(s)
        x = x2

    dx, loss = loss_head(x, target, name="loss_head")

    gw = {k: {} for k in WEIGHTS}
    for i in reversed(range(depth)):
        kind, j = i % 3, i // 3
        s = saved[i]
        gw["ffn_w_down"][i] = matmul(s["a"], dx, ta=True, out_dtype=BF16, name="mm_ffn_dwdown")
        dgu = ffn_up_bwd(dx, W["ffn_w_down"][i], s["gu"], name="ffn_up_bwd")
        dh2 = matmul(dgu, W["ffn_w_gu"][i], a_split=2, name="mm_ffn_dh")
        gw["ffn_w_gu"][i] = matmul(dgu, s["h2"], ta=True, a_split=2, out_dtype=BF16, name="mm_ffn_dwgu")
        dx1, dgn = rms_bwd(s["x1"], W["ffn_norm"][i:i + 1], dh2, dx, name="rms_ffn_bwd")
        gw["ffn_norm"][i] = dgn[0]
        if kind == 0:
            do = matmul(dx1, W["sb_w_o"][j], tb=True, name="mm_sb_do")
            gw["sb_w_o"][j] = matmul(s["o"], dx1, ta=True, out_dtype=BF16, name="mm_sb_dwo")
            if late is not None and i == 0:
                dqkv, dqg, dkg, received = sb_attn_bwd(
                    s["qkv"], s["o"], do, W["sb_q_gain"][j:j + 1], W["sb_k_gain"][j:j + 1], name="sb_bwd_scatter",
                    scatter=late.contributions(gw))
            else:
                dqkv, dqg, dkg = sb_attn_bwd(s["qkv"], s["o"], do, W["sb_q_gain"][j:j + 1], W["sb_k_gain"][j:j + 1],
                                             name="sb_bwd")
            gw["sb_q_gain"][j] = dqg[0]
            gw["sb_k_gain"][j] = dkg[0]
            dh = matmul(dqkv, W["sb_w_qkv"][j], a_split=3, name="mm_sb_dh")
            gw["sb_w_qkv"][j] = matmul(dqkv, s["h"], ta=True, a_split=3, out_dtype=BF16, name="mm_sb_dwqkv")
        elif kind == 1:
            dp = matmul(dx1, W["gm_w_out"][j], tb=True, name="mm_gm_dp")
            gw["gm_w_out"][j] = matmul(s["p"], dx1, ta=True, out_dtype=BF16, name="mm_gm_dwout")
            dzz, db_in, dvg, dws, dbs = gmlp_bwd(s["zzpre"], W["gm_b_in"][j:j + 1], W["gm_v_gain"][j:j + 1],
                                                s["wc"], s["bsf"], dp, name="gm_bwd")
            gw["gm_b_in"][j] = db_in[0]
            gw["gm_v_gain"][j] = dvg[0]
            gw["gm_w_s"][j] = dws
            gw["gm_b_s"][j] = dbs[:, :gm_groups].T
            dh = matmul(dzz, W["gm_w_in"][j], name="mm_gm_dh")
            gw["gm_w_in"][j] = matmul(dzz, s["h"], ta=True, out_dtype=BF16, name="mm_gm_dwin")
        else:
            conv_dim = W["ssm_conv_w"].shape[2]
            G = (conv_dim - inner) // (2 * SSM_STATE)
            hpg = heads // G
            dyn = matmul(dx1, W["ssm_w_out"][j], tb=True, name="mm_ssm_dyn")
            gw["ssm_w_out"][j] = matmul(s["yn"], dx1, ta=True, out_dtype=BF16, name="mm_ssm_dwout")
            dz, dxs, dbm, dcm, ddt, dbias, dalog, dd, dng = ssd_bwd(
                s["xbc"], s["zx"], s["dtg"], s["bias_g"], s["alog_g"], s["d_chan"], s["ngain"], s["yfull"], s["hp"],
                dyn, L, G, name="ssd_bwd")
            dpre, dcw, dcb = conv_bwd(s["zx"], W["ssm_conv_w"][j], W["ssm_conv_b"][j:j + 1], inner, [dxs, dbm, dcm],
                                      name="conv_bwd")
            dzx = jnp.concatenate([dz, dpre], axis=1)
            dh = matmul(ddt, s["w_dtg"], name="mm_ssm_dh_dt")
            dh = matmul(dzx, s["w_zx"], residual=dh, name="mm_ssm_dh")
            dw_zx = matmul(dzx, s["h"], ta=True, out_dtype=BF16, name="mm_ssm_dwzx")
            dw_dtg = matmul(ddt, s["h"], ta=True, out_dtype=BF16, name="mm_ssm_dwdt")
            dw_dt = dw_dtg.reshape(G, LANES, D)[:, :hpg, :].reshape(heads, D)
            gw["ssm_w_in"][j] = jnp.concatenate([dw_zx, dw_dt], axis=0)
            gw["ssm_conv_w"][j] = dcw
            gw["ssm_conv_b"][j] = dcb[0]
            gw["ssm_dt_bias"][j] = dbias[:, 0, :hpg].reshape(heads)
            gw["ssm_a_log"][j] = dalog[:, 0, :hpg].reshape(heads)
            gw["ssm_d"][j] = dd[:, 0, :hpg].reshape(heads)
            gw["ssm_norm_gain"][j] = dng[0]
        dx, dgn = rms_bwd(s["x"], W["mix_norm"][i:i + 1], dh, dx1, name="rms_mix_bwd")
        gw["mix_norm"][i] = dgn[0]

    return loss, dx, gw, received


MESH = pl.DeviceIdType.MESH
HBM_SPEC = pl.BlockSpec(memory_space=pltpu.HBM)
VMEM_SPEC = pl.BlockSpec(memory_space=pltpu.VMEM)


def _my_position():
    return lax.axis_index("x"), lax.axis_index("y"), lax.axis_index("c")


def _flip(v, bit):
    return 1 - v if bit else v


def all_gather(shards, *, name):
    n = len(shards)

    def body(*refs):
        for phase in ("start", "forward", "finish"):
            _ag_phase(refs[:n], refs[n:2 * n], refs[2 * n:], phase)

    return pl.pallas_call(
        body, name=name, out_shape=[jax.ShapeDtypeStruct((N_DEV,) + s.shape, s.dtype) for s in shards],
        in_specs=[HBM_SPEC] * n, out_specs=[HBM_SPEC] * n, scratch_shapes=_copy_semaphores(n),
    )(*shards)


def _ag_phase(x_refs, out_refs, sems, phase):
    send_sems, recv_sems, local_sems = sems
    x, y, c = _my_position()
    me, sibling = (x, y, c), (x, y, 1 - c)
    chips = [(1 - x, y), (x, 1 - y), (1 - x, 1 - y)]
    for p, (x_ref, out_ref) in enumerate(zip(x_refs, out_refs)):
        def slot(px, py, pc):
            return out_ref.at[4 * px + 2 * py + pc]

        def copy(k, block, to, src=None):
            return pltpu.make_async_remote_copy(
                src_ref=slot(*block) if src is None else src, dst_ref=slot(*block),
                send_sem=send_sems.at[7 * p + k], recv_sem=recv_sems.at[7 * p + k], device_id=to, device_id_type=MESH)

        mine = pltpu.make_async_copy(x_ref, slot(*me), local_sems.at[p])
        first = [copy(0, me, sibling, src=x_ref)]
        first += [copy(1 + j, me, (*chip, c), src=x_ref) for j, chip in enumerate(chips)]
        passed = [copy(4 + j, (*chip, c), sibling) for j, chip in enumerate(chips)]
        if phase == "start":
            mine.start()
            for cp in first:
                cp.start()
        elif phase == "forward":
            for j, chip in enumerate(chips):
                copy(1 + j, (*chip, c), me).wait_recv()
                passed[j].start()
        else:
            copy(0, sibling, me).wait_recv()
            for j, chip in enumerate(chips):
                copy(4 + j, (*chip, 1 - c), me).wait_recv()
            for cp in first + passed:
                cp.wait_send()
            mine.wait()


def _copy_semaphores(n):
    return [pltpu.SemaphoreType.DMA((7 * n,)), pltpu.SemaphoreType.DMA((7 * n,)), pltpu.SemaphoreType.DMA((n,))]


def _rs_phase(g_refs, out_refs, sems, phase):
    send_sems, recv_sems, local_sems = sems
    x, y, c = _my_position()
    me = 4 * x + 2 * y + c
    copies = []
    for p, (g_ref, out_ref) in enumerate(zip(g_refs, out_refs)):
        copies.append(pltpu.make_async_copy(g_ref.at[me], out_ref.at[me], local_sems.at[p]))
        for k in range(1, N_DEV):
            px, py, pc = _flip(x, k & 4), _flip(y, k & 2), _flip(c, k & 1)
            copies.append(pltpu.make_async_remote_copy(
                src_ref=g_ref.at[4 * px + 2 * py + pc], dst_ref=out_ref.at[me],
                send_sem=send_sems.at[7 * p + k - 1], recv_sem=recv_sems.at[7 * p + k - 1],
                device_id=(px, py, pc), device_id_type=MESH))
    for cp in copies:
        if phase == "start":
            cp.start()
        else:
            cp.wait()


def sum_slots(recv, *, name):
    n, R, C = recv.shape
    tr = _pick(R, (512, 256, 128))

    def body(r_ref, o_ref):
        acc = r_ref[0].astype(F32)
        for s in range(1, n):
            acc = acc + r_ref[s].astype(F32)
        o_ref[...] = acc

    return pl.pallas_call(
        body, name=name, grid=(R // tr,), in_specs=[pl.BlockSpec((n, tr, C), lambda i: (0, i, 0))],
        out_specs=pl.BlockSpec((tr, C), lambda i: (i, 0)), out_shape=jax.ShapeDtypeStruct((R, C), F32),
        compiler_params=_params(("parallel",)),
    )(recv)


def all_reduce_small(vs, scatter, *, name):
    n, ns = len(vs), len(scatter)

    def body(*refs):
        v_refs, g_refs = refs[:n], refs[n:n + ns]
        o_refs, r_refs = refs[n + ns:2 * n + ns], refs[2 * n + ns:2 * n + 2 * ns]
        bufs = refs[2 * n + 2 * ns:3 * n + 2 * ns]
        send_sems, recv_sems = refs[3 * n + 2 * ns:3 * n + 2 * ns + 2]
        rs_sems = refs[3 * n + 2 * ns + 2:]
        x, y, c = _my_position()
        me = 4 * x + 2 * y + c
        _rs_phase(g_refs, r_refs, rs_sems, "start")
        copies = []
        for p, (v_ref, buf) in enumerate(zip(v_refs, bufs)):
            buf[me] = v_ref[...]
            for k in range(1, N_DEV):
                px, py, pc = _flip(x, k & 4), _flip(y, k & 2), _flip(c, k & 1)
                copies.append(pltpu.make_async_remote_copy(
                    src_ref=v_ref, dst_ref=buf.at[me], send_sem=send_sems.at[7 * p + k - 1],
                    recv_sem=recv_sems.at[7 * p + k - 1], device_id=(px, py, pc), device_id_type=MESH))
        for cp in copies:
            cp.start()
        for cp in copies:
            cp.wait()
        for o_ref, buf in zip(o_refs, bufs):
            acc = buf[0]
            for s in range(1, N_DEV):
                acc = acc + buf[s]
            o_ref[...] = acc
        _rs_phase(g_refs, r_refs, rs_sems, "finish")

    out = pl.pallas_call(
        body, name=name,
        out_shape=[jax.ShapeDtypeStruct(v.shape, F32) for v in vs] + [jax.ShapeDtypeStruct(g.shape, g.dtype) for g in scatter],
        in_specs=[VMEM_SPEC] * n + [HBM_SPEC] * ns, out_specs=[VMEM_SPEC] * n + [HBM_SPEC] * ns,
        scratch_shapes=[pltpu.VMEM((N_DEV,) + v.shape, F32) for v in vs]
        + [pltpu.SemaphoreType.DMA((7 * n,)), pltpu.SemaphoreType.DMA((7 * n,))] + _copy_semaphores(ns),
        compiler_params=pltpu.CompilerParams(vmem_limit_bytes=VMEM_LIMIT_BYTES),
    )(*vs, *scatter)
    return list(out[:n]), list(out[n:])


def _pad_rows(a, mult):
    pad = (-a.shape[0]) % mult
    return jnp.pad(a, ((0, pad), (0, 0))) if pad else a


def _pack_small(arrays):
    flat = []
    for a in arrays:
        f = a.reshape(-1).astype(F32)
        flat.append(jnp.pad(f, (0, (-f.shape[0]) % LANES)))
    return _pad_rows(jnp.concatenate(flat).reshape(-1, LANES), 8)


def _unpack_small(packed, shapes):
    flat = packed.reshape(-1)
    out, r = [], 0
    for shp in shapes:
        n = math.prod(shp)
        out.append(flat[r:r + n].reshape(shp))
        r += n + (-n) % LANES
    return out


ARG_NAMES = ("x",) + WEIGHTS + ("loss_target",) + tuple("m_" + w for w in WEIGHTS) + tuple("v_" + w for w in WEIGHTS)


def kernel(x, mix_norm, ffn_norm, sb_w_qkv, sb_q_gain, sb_k_gain, sb_w_o, gm_w_in, gm_b_in, gm_v_gain, gm_w_s, gm_b_s, gm_w_out, ssm_w_in, ssm_conv_w, ssm_conv_b, ssm_dt_bias, ssm_a_log, ssm_d, ssm_norm_gain, ssm_w_out, ffn_w_gu, ffn_w_down, loss_target, m_mix_norm, m_ffn_norm, m_sb_w_qkv, m_sb_q_gain, m_sb_k_gain, m_sb_w_o, m_gm_w_in, m_gm_b_in, m_gm_v_gain, m_gm_w_s, m_gm_b_s, m_gm_w_out, m_ssm_w_in, m_ssm_conv_w, m_ssm_conv_b, m_ssm_dt_bias, m_ssm_a_log, m_ssm_d, m_ssm_norm_gain, m_ssm_w_out, m_ffn_w_gu, m_ffn_w_down, v_mix_norm, v_ffn_norm, v_sb_w_qkv, v_sb_q_gain, v_sb_k_gain, v_sb_w_o, v_gm_w_in, v_gm_b_in, v_gm_v_gain, v_gm_w_s, v_gm_b_s, v_gm_w_out, v_ssm_w_in, v_ssm_conv_w, v_ssm_conv_b, v_ssm_dt_bias, v_ssm_a_log, v_ssm_d, v_ssm_norm_gain, v_ssm_w_out, v_ffn_w_gu, v_ffn_w_down):
    given = dict(zip(ARG_NAMES, (x, mix_norm, ffn_norm, sb_w_qkv, sb_q_gain, sb_k_gain, sb_w_o, gm_w_in, gm_b_in, gm_v_gain, gm_w_s, gm_b_s, gm_w_out, ssm_w_in, ssm_conv_w, ssm_conv_b, ssm_dt_bias, ssm_a_log, ssm_d, ssm_norm_gain, ssm_w_out, ffn_w_gu, ffn_w_down, loss_target, m_mix_norm, m_ffn_norm, m_sb_w_qkv, m_sb_q_gain, m_sb_k_gain, m_sb_w_o, m_gm_w_in, m_gm_b_in, m_gm_v_gain, m_gm_w_s, m_gm_b_s, m_gm_w_out, m_ssm_w_in, m_ssm_conv_w, m_ssm_conv_b, m_ssm_dt_bias, m_ssm_a_log, m_ssm_d, m_ssm_norm_gain, m_ssm_w_out, m_ffn_w_gu, m_ffn_w_down, v_mix_norm, v_ffn_norm, v_sb_w_qkv, v_sb_q_gain, v_sb_k_gain, v_sb_w_o, v_gm_w_in, v_gm_b_in, v_gm_v_gain, v_gm_w_s, v_gm_b_s, v_gm_w_out, v_ssm_w_in, v_ssm_conv_w, v_ssm_conv_b, v_ssm_dt_bias, v_ssm_a_log, v_ssm_d, v_ssm_norm_gain, v_ssm_w_out, v_ffn_w_gu, v_ffn_w_down)))
    mx, my, mc = _my_position()
    me = 4 * mx + 2 * my + mc

    pieces = [(k, l) for k in BIG for l in range(given[k].shape[0])]
    early = [("sb_w_qkv", 0)]
    late_pieces = [p for p in pieces if p not in early]
    last = [("sb_w_qkv", 0)]
    main = [p for p in pieces if p not in last]

    def shards_of(ps):
        return [(given[k][l].T if k in COL_SHARDED else given[k][l]).astype(BF16) for k, l in ps]

    def piece_to_full(g, k):
        rows, cols = given[k].shape[1:]
        return g.reshape(N_DEV * cols, rows) if k in COL_SHARDED else g.reshape(N_DEV * rows, cols)

    def full_to_piece(full, k):
        return full.reshape(N_DEV, -1, PACK_COLS)

    def summed_to_shard(g, k):
        rows, cols = given[k].shape[1:]
        return g.reshape(cols, rows) if k in COL_SHARDED else g.reshape(rows, cols)

    def contributions(gw, ps):
        return [full_to_piece(gw[k][l], k) for k, l in ps]

    sharded_small = [lax.bitcast_convert_type(given[k], BF16) for k in SMALL_SHARDED]
    tail = jnp.concatenate([a.reshape(-1) for a in sharded_small])
    tail = jnp.pad(tail, (0, (-tail.size) % PACK_COLS)).reshape(-1, PACK_COLS)

    W = {k: given[k] for k in SMALL if k not in SMALL_SHARDED}
    W.update({k: [None] * given[k].shape[0] for k in BIG})
    for (k, l), g in zip(early, all_gather(shards_of(early), name="all_gather_early")):
        W[k][l] = piece_to_full(g, k)

    class Late:
        shards = shards_of(late_pieces) + [tail]

        @staticmethod
        def fill(weights, gathered):
            for (k, l), g in zip(late_pieces, gathered):
                weights[k][l] = piece_to_full(g, k)
            tail_g = gathered[-1].reshape(N_DEV, -1)
            off = 0
            for k, a in zip(SMALL_SHARDED, sharded_small):
                g = lax.bitcast_convert_type(tail_g[:, off:off + a.size].reshape((N_DEV,) + a.shape), F32)
                weights[k] = jnp.moveaxis(g, 0, -2).reshape(g.shape[1:-1] + (N_DEV * g.shape[-1],))
                off += a.size

        @staticmethod
        def contributions(gw):
            return contributions(gw, main)

    loss, gx, gw, received_main = local_step(given["x"][0], given["loss_target"][0], W, late=Late)

    grads_small = {k: jnp.stack([gw[k][l] for l in sorted(gw[k])], axis=0) for k in SMALL}
    packed_names = tuple(k for k in SMALL if k != "gm_w_s")
    small_shapes = [grads_small[k].shape for k in packed_names] + [(1, 1)]
    (red_ws, red_rest), received_last = all_reduce_small(
        [grads_small["gm_w_s"].reshape(-1, LANES), _pack_small([grads_small[k] for k in packed_names] + [loss])],
        contributions(gw, last), name="all_reduce_small_and_last_exchange")
    small_full = dict(zip(packed_names + ("loss",), _unpack_small(red_rest, small_shapes)))
    small_full["gm_w_s"] = red_ws.reshape(grads_small["gm_w_s"].shape)

    g_piece = {}
    for grp, received in ((main, received_main), (last, received_last)):
        for p, r in zip(grp, received):
            g_piece[p] = summed_to_shard(sum_slots(r, name="reduce_scatter_sum"), p[0])
    out_g, out_d, out_m, out_v = {}, {}, {}, {}
    for k in BIG:
        swap = lambda a: jnp.swapaxes(a, -1, -2)
        keep_t = k in COL_SHARDED and given[k].shape[-1] % LANES != 0
        flip = swap if keep_t else (lambda a: a)
        g = jnp.stack([g_piece[(k, l)] for l in range(given[k].shape[0])], axis=0)
        if k in COL_SHARDED and not keep_t:
            g = swap(g)
        res = adamw(flip(given[k]), g, flip(given["m_" + k]), flip(given["v_" + k]), name="adamw_" + k)
        out_g[k], out_d[k], out_m[k], out_v[k] = (flip(a) for a in (g,) + tuple(res))

    gsmall = {}
    for k in SMALL:
        g = small_full[k]
        if k in SMALL_SHARDED:
            n = given[k].shape[-1]
            g = lax.dynamic_slice_in_dim(g, me * n, n, axis=g.ndim - 1)
        gsmall[k] = g
    local_shapes = [given[k].shape for k in packed_names]
    dsm, nmsm, nvsm = adamw(*[_pack_small([src[k] for k in packed_names]) for src in (
        given, gsmall, {k: given["m_" + k] for k in packed_names}, {k: given["v_" + k] for k in packed_names})],
        name="adamw_small")
    out_g.update(gsmall)
    for dst, src in ((out_d, dsm), (out_m, nmsm), (out_v, nvsm)):
        dst.update(zip(packed_names, _unpack_small(src, local_shapes)))
    ws_shape = given["gm_w_s"].shape
    out_d["gm_w_s"], out_m["gm_w_s"], out_v["gm_w_s"] = (a.reshape(ws_shape) for a in adamw(
        *[a.reshape((-1,) + ws_shape[-2:]) for a in (given["gm_w_s"], gsmall["gm_w_s"], given["m_gm_w_s"], given["v_gm_w_s"])],
        name="adamw_gm_w_s"))

    return (small_full["loss"].reshape(()), gx[None],
            *[out_g[k] for k in WEIGHTS], *[out_d[k] for k in WEIGHTS],
            *[out_m[k] for k in WEIGHTS], *[out_v[k] for k in WEIGHTS])
```

```python
import math

import jax
import jax.numpy as jnp
from jax import lax
from jax.experimental import pallas as pl
from jax.experimental.pallas import tpu as pltpu

F32 = jnp.float32
BF16 = jnp.bfloat16
EPS = 1e-6
N_DEV = 8
SB_HEAD_DIM = 64
SB_TILE = 256
SB_FWD_QUERY_BLOCKS = 4
SB_BWD_QUERY_BLOCKS = 2
SSM_STATE = 128
SSD_FWD_GROUPS_PER_STEP = 2
SSD_BWD_GROUPS_PER_STEP = 1
SSM_CONV = 4
ADAM_LR = 0.001
ADAM_B1 = 0.9
ADAM_B2 = 0.999
ADAM_EPS = 1e-08
ADAM_WD = 0.01
ADAM_STEP = 10
VMEM_LIMIT_BYTES = 56 * 1024 * 1024
MATMUL_VMEM_BUDGET = 40 * 1024 * 1024
LANES = 128
PACK_COLS = 1024

BIG = ("sb_w_qkv", "sb_w_o", "gm_w_in", "gm_w_out", "ssm_w_in", "ssm_w_out", "ffn_w_gu", "ffn_w_down")
COL_SHARDED = ("sb_w_qkv", "gm_w_in", "ssm_w_in", "ffn_w_gu")
SMALL = ("mix_norm", "ffn_norm", "sb_q_gain", "sb_k_gain", "gm_b_in", "gm_v_gain", "gm_w_s", "gm_b_s",
         "ssm_conv_w", "ssm_conv_b", "ssm_dt_bias", "ssm_a_log", "ssm_d", "ssm_norm_gain")
SMALL_SHARDED = ("ssm_conv_w", "ssm_conv_b", "ssm_norm_gain")
WEIGHTS = ("mix_norm", "ffn_norm", "sb_w_qkv", "sb_q_gain", "sb_k_gain", "sb_w_o", "gm_w_in", "gm_b_in",
           "gm_v_gain", "gm_w_s", "gm_b_s", "gm_w_out", "ssm_w_in", "ssm_conv_w", "ssm_conv_b", "ssm_dt_bias",
           "ssm_a_log", "ssm_d", "ssm_norm_gain", "ssm_w_out", "ffn_w_gu", "ffn_w_down")


def _params(semantics=None):
    return pltpu.CompilerParams(dimension_semantics=semantics, vmem_limit_bytes=VMEM_LIMIT_BYTES)


def _pick(n, prefs):
    for t in prefs:
        if t <= n and n % t == 0:
            return t
    return n


def _dot(a, b, ca=1, cb=0):
    return lax.dot_general(a, b, (((ca,), (cb,)), ((), ())), preferred_element_type=F32)


def _split3(v):
    h1 = v.astype(BF16)
    r1 = v - h1.astype(F32)
    h2 = r1.astype(BF16)
    h3 = (r1 - h2.astype(F32)).astype(BF16)
    return h1, h2, h3


def _dot_exact_left(mat01, v):
    h1, h2, h3 = _split3(v)
    return _dot(mat01, h1) + _dot(mat01, h2) + _dot(mat01, h3)


def _sum_all(v):
    return jnp.sum(jnp.sum(v, axis=0, keepdims=True), axis=1, keepdims=True)


def _sigmoid(v):
    return 1.0 / (1.0 + jnp.exp(-v))


def _softplus(v):
    return jnp.maximum(v, 0.0) + jnp.log(1.0 + jnp.exp(-jnp.abs(v)))


def _erf(v):
    a = jnp.abs(v)
    t = 1.0 / (1.0 + 0.3275911 * a)
    poly = t * (0.254829592 + t * (-0.284496736 + t * (1.421413741 + t * (-1.453152027 + t * 1.061405429))))
    e = 1.0 - poly * jnp.exp(-a * a)
    return jnp.where(v < 0, -e, e)


def _gelu_and_grad(v):
    cdf = 0.5 * (1.0 + _erf(v * (1.0 / math.sqrt(2.0))))
    pdf = jnp.exp(-0.5 * v * v) * (1.0 / math.sqrt(2.0 * math.pi))
    return v * cdf, cdf + v * pdf


def matmul(a, b, *, ta=False, tb=False, out_dtype=F32, residual=None, a_split=1, b_split=1, name):
    if a_split > 1 and ta:
        assert a.shape[0] == a_split
        K, M = a.shape[1], a_split * a.shape[2]
    elif a_split > 1:
        assert a.shape[0] == a_split
        M, K = a.shape[1], a_split * a.shape[2]
    elif ta:
        K, M = a.shape
    else:
        M, K = a.shape
    if b_split > 1:
        assert not tb and b.shape[0] == b_split
        Kb, N = b.shape[1], b_split * b.shape[2]
    elif tb:
        N, Kb = b.shape
    else:
        Kb, N = b.shape
    assert K == Kb, (a.shape, b.shape, ta, tb)
    has_res = residual is not None
    tm = _pick(M // a_split if ta else M, (1024, 1408, 768, 512, 256, 128))
    tn = _pick(N // b_split, (1024, 1408, 1536, 768, 512, 256, 128))

    def vmem_bytes(tk):
        tiles = tm * tk * a.dtype.itemsize + tk * tn * b.dtype.itemsize
        outs = tm * tn * jnp.dtype(out_dtype).itemsize + (tm * tn * 4 if has_res else 0)
        return 2 * tiles + 2 * outs + (tm * tn * 4 if tk < K else 0)

    kp = K if ta else K // a_split
    tk = next((t for t in (K, 2048, 1408, 1024, 512, 256) if t <= kp and kp % t == 0 and vmem_bytes(t) <= MATMUL_VMEM_BUDGET),
              _pick(kp, (128,)))
    nk = K // tk
    if a_split > 1 and ta:
        nib = M // a_split // tm
        a_spec = pl.BlockSpec((None, tk, tm), lambda i, j, k: (i // nib, k, i % nib))
    elif a_split > 1:
        nkb = kp // tk
        a_spec = pl.BlockSpec((None, tm, tk), lambda i, j, k: (k // nkb, i, k % nkb))
    else:
        a_spec = pl.BlockSpec((tk, tm), lambda i, j, k: (k, i)) if ta else pl.BlockSpec((tm, tk), lambda i, j, k: (i, k))
    if b_split > 1:
        njb = N // b_split // tn
        b_spec = pl.BlockSpec((None, tk, tn), lambda i, j, k: (j // njb, k, j % njb))
    else:
        b_spec = pl.BlockSpec((tn, tk), lambda i, j, k: (j, k)) if tb else pl.BlockSpec((tk, tn), lambda i, j, k: (k, j))
    o_spec = pl.BlockSpec((tm, tn), lambda i, j, k: (i, j))
    ca, cb = (0 if ta else 1), (1 if tb else 0)

    def body(*refs):
        a_ref, b_ref = refs[:2]
        r_ref = refs[2] if has_res else None
        o_ref = refs[3] if has_res else refs[2]

        def finish(r):
            if has_res:
                r = r + r_ref[...]
            o_ref[...] = r.astype(out_dtype)

        def part():
            return _dot(a_ref[...].astype(BF16), b_ref[...].astype(BF16), ca, cb)

        if nk == 1:
            finish(part())
            return
        acc = refs[-1]
        k = pl.program_id(2)

        @pl.when(k == 0)
        def _():
            acc[...] = part()

        @pl.when(jnp.logical_and(k > 0, k < nk - 1))
        def _():
            acc[...] += part()

        @pl.when(k == nk - 1)
        def _():
            finish(acc[...] + part())

    in_specs = [a_spec, b_spec] + ([o_spec] if has_res else [])
    args = (a, b) + ((residual,) if has_res else ())
    return pl.pallas_call(
        body, name=name, grid=(M // tm, N // tn, nk), in_specs=in_specs, out_specs=o_spec,
        out_shape=jax.ShapeDtypeStruct((M, N), out_dtype),
        scratch_shapes=[pltpu.VMEM((tm, tn), F32)] if nk > 1 else [],
        compiler_params=_params(("parallel", "parallel", "arbitrary")),
    )(*args)


def rms_fwd(x, gain, *, name):
    S, D = x.shape
    tr = _pick(S, (512, 256, 128))

    def body(x_ref, g_ref, o_ref):
        xv = x_ref[...]
        r = lax.rsqrt(jnp.mean(xv * xv, axis=1, keepdims=True) + EPS)
        o_ref[...] = (xv * r * g_ref[...]).astype(BF16)

    return pl.pallas_call(
        body, name=name, grid=(S // tr,),
        in_specs=[pl.BlockSpec((tr, D), lambda i: (i, 0)), pl.BlockSpec((1, D), lambda i: (0, 0))],
        out_specs=pl.BlockSpec((tr, D), lambda i: (i, 0)), out_shape=jax.ShapeDtypeStruct((S, D), BF16),
        compiler_params=_params(("parallel",)),
    )(x, gain)


def rms_bwd(x, gain, dh, dres, *, name):
    S, D = x.shape
    tr = _pick(S, (512, 256, 128))

    def body(x_ref, g_ref, dh_ref, dr_ref, dx_ref, dg_ref):
        @pl.when(pl.program_id(0) == 0)
        def _():
            dg_ref[...] = jnp.zeros_like(dg_ref)

        xv = x_ref[...]
        dhv = dh_ref[...]
        r = lax.rsqrt(jnp.mean(xv * xv, axis=1, keepdims=True) + EPS)
        xhat = xv * r
        t = dhv * g_ref[...]
        dx_ref[...] = dr_ref[...] + r * (t - xhat * jnp.mean(xhat * t, axis=1, keepdims=True))
        dg_ref[...] += jnp.sum(dhv * xhat, axis=0, keepdims=True)

    row = pl.BlockSpec((tr, D), lambda i: (i, 0))
    vec = pl.BlockSpec((1, D), lambda i: (0, 0))
    return pl.pallas_call(
        body, name=name, grid=(S // tr,), in_specs=[row, vec, row, row], out_specs=[row, vec],
        out_shape=[jax.ShapeDtypeStruct((S, D), F32), jax.ShapeDtypeStruct((1, D), F32)],
        compiler_params=_params(("arbitrary",)),
    )(x, gain, dh, dres)


def ffn_up_fwd(h, w_gu_t, *, name):
    S, K = h.shape
    F = w_gu_t.shape[0] // 2
    tm = _pick(S, (512, 256, 128))
    tn = _pick(F, (1408, 1024, 768, 512, 256, 128))
    nj = F // tn

    def body(h_ref, wg_ref, wu_ref, gu_ref, a_ref):
        hv = h_ref[...]
        g = _dot(hv, wg_ref[...], 1, 1)
        u = _dot(hv, wu_ref[...], 1, 1)
        gu_ref[0] = g
        gu_ref[1] = u
        a_ref[...] = (g * _sigmoid(g) * u).astype(BF16)

    return pl.pallas_call(
        body, name=name, grid=(nj, S // tm),
        in_specs=[pl.BlockSpec((tm, K), lambda j, i: (i, 0)), pl.BlockSpec((tn, K), lambda j, i: (j, 0)),
                  pl.BlockSpec((tn, K), lambda j, i: (nj + j, 0))],
        out_specs=[pl.BlockSpec((2, tm, tn), lambda j, i: (0, i, j)), pl.BlockSpec((tm, tn), lambda j, i: (i, j))],
        out_shape=[jax.ShapeDtypeStruct((2, S, F), F32), jax.ShapeDtypeStruct((S, F), BF16)],
        compiler_params=_params(("parallel", "parallel")),
    )(h, w_gu_t, w_gu_t)


def ffn_up_bwd(dy, w_down, gu, *, name):
    S, D = dy.shape
    F = w_down.shape[0]
    tm = _pick(S, (512, 256, 128))
    tn = _pick(F, (1408, 1024, 768, 512, 256, 128))

    def body(dy_ref, wd_ref, gu_ref, o_ref):
        da = _dot(dy_ref[...].astype(BF16), wd_ref[...], 1, 1)
        g = gu_ref[0]
        u = gu_ref[1]
        s = _sigmoid(g)
        o_ref[0] = (da * u * (s * (1.0 + g * (1.0 - s)))).astype(BF16)
        o_ref[1] = (da * g * s).astype(BF16)

    pair = pl.BlockSpec((2, tm, tn), lambda j, i: (0, i, j))
    return pl.pallas_call(
        body, name=name, grid=(F // tn, S // tm),
        in_specs=[pl.BlockSpec((tm, D), lambda j, i: (i, 0)), pl.BlockSpec((tn, D), lambda j, i: (j, 0)), pair],
        out_specs=pair, out_shape=jax.ShapeDtypeStruct((2, S, F), BF16),
        compiler_params=_params(("parallel", "parallel")),
    )(dy, w_down, gu)


def loss_head(y, target, *, name):
    S, D = y.shape
    tr = _pick(S, (512, 256, 128))

    def body(y_ref, t_ref, dy_ref, l_ref):
        @pl.when(pl.program_id(0) == 0)
        def _():
            l_ref[...] = jnp.zeros_like(l_ref)

        err = y_ref[...] - t_ref[...]
        dy_ref[...] = err * (1.0 / D)
        l_ref[...] += jnp.sum(0.5 * jnp.mean(err * err, axis=1, keepdims=True), axis=0, keepdims=True)

    row = pl.BlockSpec((tr, D), lambda i: (i, 0))
    one = pl.BlockSpec((1, 1), lambda i: (0, 0))
    dy, l = pl.pallas_call(
        body, name=name, grid=(S // tr,), in_specs=[row, row], out_specs=[row, one],
        out_shape=[jax.ShapeDtypeStruct((S, D), F32), jax.ShapeDtypeStruct((1, 1), F32)],
        compiler_params=_params(("arbitrary",)),
    )(y, target)
    return dy, l


def adamw(w, g, m, v, *, name):
    R, C = w.shape[-2:]
    tr = _pick(R, (512, 256, 128, 64, 32, 16, 8))

    def body(w_ref, g_ref, m_ref, v_ref, d_ref, mo_ref, vo_ref):
        gv = g_ref[...]
        mn = ADAM_B1 * m_ref[...] + (1.0 - ADAM_B1) * gv
        vn = ADAM_B2 * v_ref[...] + (1.0 - ADAM_B2) * jnp.square(gv)
        m_hat = mn / (1.0 - ADAM_B1 ** ADAM_STEP)
        v_hat = vn / (1.0 - ADAM_B2 ** ADAM_STEP)
        d_ref[...] = -ADAM_LR * (m_hat / (jnp.sqrt(v_hat) + ADAM_EPS) + ADAM_WD * w_ref[...])
        mo_ref[...] = mn
        vo_ref[...] = vn

    tc = C if tr < R or C % LANES else _pick(C, (256, 128))
    if w.ndim == 3:
        grid = (w.shape[0], R // tr, C // tc)
        blk = pl.BlockSpec((None, tr, tc), lambda l, i, j: (l, i, j))
    else:
        grid = (R // tr, C // tc)
        blk = pl.BlockSpec((tr, tc), lambda i, j: (i, j))
    sds = jax.ShapeDtypeStruct(w.shape, F32)
    return pl.pallas_call(
        body, name=name, grid=grid, in_specs=[blk] * 4, out_specs=[blk] * 3, out_shape=[sds] * 3,
        compiler_params=_params(("parallel",) * len(grid)),
    )(w, g, m, v)


def _tri(n, kind):
    r = lax.broadcasted_iota(jnp.int32, (n, n), 0)
    c = lax.broadcasted_iota(jnp.int32, (n, n), 1)
    if kind == "row_gt_col":
        return (r > c).astype(BF16)
    if kind == "row_ge_col":
        return (r >= c).astype(BF16)
    if kind == "row_le_col":
        return (r <= c).astype(BF16)
    raise ValueError(kind)


def _sb_tile(qs, kj, r_carry, u_strict, masked):
    z = _dot(qs, kj, 1, 1)
    lb = jnp.minimum(z, 0.0) - jnp.log(1.0 + jnp.exp(-jnp.abs(z)))
    l1m = lb - z
    keep = None
    if masked:
        tq, tk = z.shape
        keep = lax.broadcasted_iota(jnp.int32, (tq, tk), 1) < lax.broadcasted_iota(jnp.int32, (tq, tk), 0)
        l1m = jnp.where(keep, l1m, 0.0)
    w = jnp.exp(lb + _dot(l1m.astype(BF16), u_strict) + r_carry)
    if masked:
        w = jnp.where(keep, w, 0.0)
    return lb, l1m, w, keep


def _sb_prep(T, nb, hd, refs_in, gains, scratch):
    q_scale = 1.0 / math.sqrt(hd)
    assert math.log2(q_scale) == round(math.log2(q_scale))

    def prep(i, _):
        rows = pl.ds(pl.multiple_of(i * T, T), T)
        for hh in range(2):
            sl = slice(hd * hh, hd * hh + hd)
            for n, (src, dst) in enumerate(zip(refs_in, scratch)):
                v = src[rows, sl]
                if n < 2:
                    v = v * lax.rsqrt(jnp.mean(v * v, axis=1, keepdims=True) + EPS) * gains[n][...]
                if n == 0:
                    v = v * q_scale
                dst[hh, rows, :] = v.astype(BF16)
        return 0

    lax.fori_loop(0, nb, prep, 0)


def _sb_chains(m, T, nq):
    rows = [pl.ds(pl.multiple_of((nq * m + qb) * T, T), T) for qb in range(nq)]
    return rows, [(hh, qb) for qb in range(nq) for hh in range(2)]


def _sb_sweep(tile, carry, chains, m, nq):
    for kk in reversed(range(nq)):
        carry = tile(nq * m + kk, carry, [(ch, ch[1] == kk) for ch in chains if ch[1] >= kk])
    return lax.fori_loop(0, nq * m, lambda jj, c: tile(nq * m - 1 - jj, c, [(ch, False) for ch in chains]), carry)


def sb_attn_fwd(qkv, q_gain, k_gain, *, name, gather=None):
    S, D3 = qkv.shape
    D = D3 // 3
    npairs = D // LANES
    hd = SB_HEAD_DIM
    T = min(SB_TILE, S)
    nb = S // T
    nq = SB_FWD_QUERY_BLOCKS
    assert nb % nq == 0

    def body(*refs):
        if gather is None:
            q_ref, k_ref, v_ref, qg_ref, kg_ref, us_ref, o_ref, qn_s, kn_s, vb_s = refs
        else:
            ng = len(gather)
            q_ref, k_ref, v_ref, qg_ref, kg_ref, us_ref = refs[:6]
            o_ref = refs[6 + ng]
            qn_s, kn_s, vb_s = refs[7 + 2 * ng:10 + 2 * ng]
            comm = (refs[6:6 + ng], refs[7 + ng:7 + 2 * ng], refs[10 + 2 * ng:])
            step = pl.program_id(0)
            pl.when(step == 0)(lambda: _ag_phase(*comm, "start"))
            pl.when(step == npairs - 1)(lambda: _ag_phase(*comm, "forward"))
        us = us_ref[...]
        _sb_prep(T, nb, hd, (q_ref, k_ref, v_ref), (qg_ref, kg_ref), (qn_s, kn_s, vb_s))

        def superblock(m, _):
            rows_q, chains = _sb_chains(m, T, nq)
            qs = {ch: qn_s[ch[0], rows_q[ch[1]], :] for ch in chains}

            def tile(j, carry, which):
                rows_j = pl.ds(pl.multiple_of(j * T, T), T)
                new = dict(carry)
                for ch, masked in which:
                    acc, rc = carry[ch]
                    _, l1m, w, _ = _sb_tile(qs[ch], kn_s[ch[0], rows_j, :], rc, us, masked)
                    new[ch] = (acc + _dot(w.astype(BF16), vb_s[ch[0], rows_j, :]),
                               rc + jnp.sum(l1m, axis=1, keepdims=True))
                return new

            carry = {ch: (jnp.zeros((T, hd), F32), jnp.zeros((T, 1), F32)) for ch in chains}
            carry = _sb_sweep(tile, carry, chains, m, nq)
            for qb in range(nq):
                o_ref[rows_q[qb], :] = jnp.concatenate([carry[(0, qb)][0], carry[(1, qb)][0]], axis=1)
            return 0

        lax.fori_loop(0, nb // nq, superblock, 0)
        if gather is not None:
            pl.when(step == npairs - 1)(lambda: _ag_phase(*comm, "finish"))

    col = lambda off: pl.BlockSpec((S, LANES), lambda p, off=off: (0, off + p))
    gain = pl.BlockSpec((1, hd), lambda p: (0, 0))
    in_specs = [col(0), col(npairs), col(2 * npairs), gain, gain, pl.BlockSpec((T, T), lambda p: (0, 0))]
    out_specs = [pl.BlockSpec((S, LANES), lambda p: (0, p))]
    out_shape = [jax.ShapeDtypeStruct((S, D), F32)]
    scratch = [pltpu.VMEM((2, S, hd), BF16)] * 3
    args = [qkv, qkv, qkv, q_gain, k_gain, _tri(T, "row_gt_col")]
    if gather is not None:
        in_specs += [HBM_SPEC] * len(gather)
        out_specs += [HBM_SPEC] * len(gather)
        out_shape += [jax.ShapeDtypeStruct((N_DEV,) + s.shape, s.dtype) for s in gather]
        scratch += _copy_semaphores(len(gather))
        args += list(gather)
    out = pl.pallas_call(
        body, name=name, grid=(npairs,), in_specs=in_specs, out_specs=out_specs, out_shape=out_shape,
        scratch_shapes=scratch, compiler_params=_params(("arbitrary",)),
    )(*args)
    return out[0] if gather is None else (out[0], list(out[1:]))


def sb_attn_bwd(qkv, o, do, q_gain, k_gain, *, name, scatter=None):
    S, D3 = qkv.shape
    D = D3 // 3
    npairs = D // LANES
    hd = SB_HEAD_DIM
    T = min(SB_TILE, S)
    nb = S // T
    nq = SB_BWD_QUERY_BLOCKS
    assert nb % nq == 0
    scale = 1.0 / math.sqrt(hd)

    def body(*refs):
        if scatter is None:
            (q_ref, k_ref, v_ref, o_ref, do_ref, qg_ref, kg_ref, us_ref,
             dqkv_ref, dg_ref, qn_s, kn_s, vb_s, dob_s) = refs
        else:
            ns = len(scatter)
            q_ref, k_ref, v_ref, o_ref, do_ref, qg_ref, kg_ref, us_ref = refs[:8]
            rs_in = refs[8:8 + ns]
            dqkv_ref, dg_ref = refs[8 + ns:10 + ns]
            rs_out = refs[10 + ns:10 + 2 * ns]
            qn_s, kn_s, vb_s, dob_s = refs[10 + 2 * ns:14 + 2 * ns]
            rs_sems = refs[14 + 2 * ns:]
            pl.when(pl.program_id(0) == 0)(lambda: _rs_phase(rs_in, rs_out, rs_sems, "start"))
        dq_ref, dk_ref, dv_ref = dqkv_ref.at[0], dqkv_ref.at[1], dqkv_ref.at[2]

        @pl.when(pl.program_id(0) == 0)
        def _():
            dg_ref[...] = jnp.zeros_like(dg_ref)

        us = us_ref[...]
        u_prefix = (1.0 - us.astype(F32)).astype(BF16)
        _sb_prep(T, nb, hd, (q_ref, k_ref, v_ref, do_ref), (qg_ref, kg_ref), (qn_s, kn_s, vb_s, dob_s))
        dk_ref[...] = jnp.zeros_like(dk_ref)
        dv_ref[...] = jnp.zeros_like(dv_ref)

        def superblock(m, _):
            rows_q, chains = _sb_chains(m, T, nq)
            qs = {ch: qn_s[ch[0], rows_q[ch[1]], :] for ch in chains}
            doi ={ch: dob_s[ch[0], rows_q[ch[1]], :] for ch in chains}
            dt_total = {ch: jnp.sum(doi[ch].astype(F32) * o_ref[rows_q[ch[1]], hd * ch[0]:hd * ch[0] + hd],
                                    axis=1, keepdims=True) for ch in chains}

            def tile(j, carry, which):
                rows_j = pl.ds(pl.multiple_of(j * T, T), T)
                new = dict(carry)
                dk_part, dv_part = {}, {}
                for ch, masked in which:
                    hh = ch[0]
                    dq_acc, rc, gc = carry[ch]
                    kj = kn_s[hh, rows_j, :]
                    lb, l1m, w, keep = _sb_tile(qs[ch], kj, rc, us, masked)
                    wb = w.astype(BF16)
                    g = _dot(doi[ch], vb_s[hh, rows_j, :], 1, 1) * wb.astype(F32)
                    g_row = jnp.sum(g, axis=1, keepdims=True)
                    g_upto = (dt_total[ch] - gc - g_row) + _dot(g.astype(BF16), u_prefix)
                    dz = g - g_upto * jnp.exp(lb)
                    if masked:
                        dz = jnp.where(keep, dz, 0.0)
                    dzb = dz.astype(BF16)
                    dv_part[hh] = dv_part.get(hh, 0.0) + _dot(wb, doi[ch], 0, 0)
                    dk_part[hh] = dk_part.get(hh, 0.0) + _dot(dzb, qs[ch], 0, 0)
                    new[ch] = (dq_acc + _dot(dzb, kj), rc + jnp.sum(l1m, axis=1, keepdims=True),
                               gc + g_row)
                dv_ref[rows_j, :] += jnp.concatenate([dv_part[0], dv_part[1]], axis=1)
                dk_ref[rows_j, :] += jnp.concatenate([dk_part[0], dk_part[1]], axis=1)
                return new

            zero1 = jnp.zeros((T, 1), F32)
            carry = {ch: (jnp.zeros((T, hd), F32), zero1, zero1) for ch in chains}
            carry = _sb_sweep(tile, carry, chains, m, nq)
            for qb in range(nq):
                dq_ref[rows_q[qb], :] = jnp.concatenate([carry[(0, qb)][0], carry[(1, qb)][0]], axis=1) * scale
            return 0

        lax.fori_loop(0, nb // nq, superblock, 0)

        def finish(i, carry):
            rows = pl.ds(pl.multiple_of(i * T, T), T)
            new = []
            for hh in range(2):
                sl = slice(hd * hh, hd * hh + hd)
                outs = []
                for raw_ref, gain_ref, dn in ((q_ref, qg_ref, dq_ref[rows, sl]), (k_ref, kg_ref, dk_ref[rows, sl])):
                    raw = raw_ref[rows, sl]
                    r = lax.rsqrt(jnp.mean(raw * raw, axis=1, keepdims=True) + EPS)
                    hat = raw * r
                    t = dn * gain_ref[...]
                    outs.append((r * (t - hat * jnp.mean(hat * t, axis=1, keepdims=True)),
                                 jnp.sum(dn * hat, axis=0, keepdims=True)))
                dq_ref[rows, sl] = outs[0][0]
                dk_ref[rows, sl] = outs[1][0]
                new.append((carry[hh][0] + outs[0][1], carry[hh][1] + outs[1][1]))
            return tuple(new)

        zg = (jnp.zeros((1, hd), F32), jnp.zeros((1, hd), F32))
        tot = lax.fori_loop(0, nb, finish, (zg, zg))
        dg_ref[0:1, 0:hd] += tot[0][0] + tot[1][0]
        dg_ref[1:2, 0:hd] += tot[0][1] + tot[1][1]
        if scatter is not None:
            pl.when(pl.program_id(0) == npairs - 1)(lambda: _rs_phase(rs_in, rs_out, rs_sems, "finish"))

    col = lambda off: pl.BlockSpec((S, LANES), lambda p, off=off: (0, off + p))
    gain = pl.BlockSpec((1, hd), lambda p: (0, 0))
    tri = pl.BlockSpec((T, T), lambda p: (0, 0))
    pair = pl.BlockSpec((S, LANES), lambda p: (0, p))
    in_specs = [col(0), col(npairs), col(2 * npairs), pair, pair, gain, gain, tri]
    out_specs = [pl.BlockSpec((3, S, LANES), lambda p: (0, 0, p)), pl.BlockSpec((8, LANES), lambda p: (0, 0))]
    out_shape = [jax.ShapeDtypeStruct((3, S, D), F32), jax.ShapeDtypeStruct((8, LANES), F32)]
    scratch = [pltpu.VMEM((2, S, hd), BF16)] * 4
    args = [qkv, qkv, qkv, o, do, q_gain, k_gain, _tri(T, "row_gt_col")]
    if scatter is not None:
        in_specs += [HBM_SPEC] * len(scatter)
        out_specs += [HBM_SPEC] * len(scatter)
        out_shape += [jax.ShapeDtypeStruct(g.shape, g.dtype) for g in scatter]
        scratch += _copy_semaphores(len(scatter))
        args += list(scatter)
    out = pl.pallas_call(
        body, name=name, grid=(npairs,), in_specs=in_specs, out_specs=out_specs, out_shape=out_shape,
        scratch_shapes=scratch, compiler_params=_params(("arbitrary",)),
    )(*args)
    res = (out[0], out[1][0:1, :hd], out[1][1:2, :hd])
    return res if scatter is None else res + (list(out[2:]),)


def gmlp_fwd(zzpre, b_in, v_gain, wc, bsf, *, name):
    S, H2 = zzpre.shape
    H = H2 // 2
    G, T, _ = wc.shape
    gd = H // G

    def body(z_ref, b_ref, vg_ref, wc_ref, bs_ref, p_ref):
        zz, _ = _gelu_and_grad(z_ref[...] + b_ref[...])
        u = zz[:, :H]
        v = zz[:, H:]
        vn = v * lax.rsqrt(jnp.mean(v * v, axis=1, keepdims=True) + EPS) * vg_ref[...]
        for g in range(G):
            gs = slice(g * gd, (g + 1) * gd)
            mixed = _dot(wc_ref[g], vn[:, gs].astype(BF16)) + bs_ref[g]
            p_ref[:, gs] = (u[:, gs] * mixed).astype(BF16)

    full3 = lambda shp: pl.BlockSpec(shp, lambda c: (0, 0, 0))
    return pl.pallas_call(
        body, name=name, grid=(S // T,),
        in_specs=[pl.BlockSpec((T, H2), lambda c: (c, 0)), pl.BlockSpec((1, H2), lambda c: (0, 0)),
                  pl.BlockSpec((1, H), lambda c: (0, 0)), full3((G, T, T)), full3((G, T, gd))],
        out_specs=pl.BlockSpec((T, H), lambda c: (c, 0)), out_shape=jax.ShapeDtypeStruct((S, H), BF16),
        compiler_params=_params(("parallel",)),
    )(zzpre, b_in, v_gain, wc, bsf)


def gmlp_bwd(zzpre, b_in, v_gain, wc, bsf, dp, *, name):
    S, H2 = zzpre.shape
    H = H2 // 2
    G, T, _ = wc.shape
    gd = H // G
    assert G <= LANES

    def body(z_ref, b_ref, vg_ref, wc_ref, bs_ref, dp_ref, dzz_ref, db_ref, dvg_ref, dws_ref, dbs_ref):
        @pl.when(pl.program_id(0) == 0)
        def _():
            db_ref[...] = jnp.zeros_like(db_ref)
            dvg_ref[...] = jnp.zeros_like(dvg_ref)
            dws_ref[...] = jnp.zeros_like(dws_ref)
            dbs_ref[...] = jnp.zeros_like(dbs_ref)

        zz, gp = _gelu_and_grad(z_ref[...] + b_ref[...])
        u = zz[:, :H]
        v = zz[:, H:]
        r = lax.rsqrt(jnp.mean(v * v, axis=1, keepdims=True) + EPS)
        vhat = v * r
        vg = vg_ref[...]
        vn = vhat * vg
        dpv = dp_ref[...]
        tril = lax.broadcasted_iota(jnp.int32, (T, T), 1) <= lax.broadcasted_iota(jnp.int32, (T, T), 0)
        lane = lax.broadcasted_iota(jnp.int32, (T, LANES), 1)
        dbs = jnp.zeros((T, LANES), F32)
        du_parts, dvn_parts = [], []
        for g in range(G):
            gs = slice(g * gd, (g + 1) * gd)
            vng = vn[:, gs].astype(BF16)
            wcg = wc_ref[g]
            mixed = _dot(wcg, vng) + bs_ref[g]
            dpg = dpv[:, gs]
            du_parts.append(dpg * mixed)
            dmx = dpg * u[:, gs]
            dmxb = dmx.astype(BF16)
            dvn_parts.append(_dot(wcg, dmxb, 0, 0))
            dws_ref[g] += jnp.where(tril, _dot(dmxb, vng, 1, 1), 0.0)
            dbs = dbs + jnp.where(lane == g, jnp.sum(dmx, axis=1, keepdims=True), 0.0)
        dbs_ref[...] += dbs
        du = jnp.concatenate(du_parts, axis=1)
        dvn = jnp.concatenate(dvn_parts, axis=1)
        dvg_ref[...] += jnp.sum(dvn * vhat, axis=0, keepdims=True)
        t = dvn * vg
        dv = r * (t - vhat * jnp.mean(vhat * t, axis=1, keepdims=True))
        dzu = du * gp[:, :H]
        dzv = dv * gp[:, H:]
        dzz_ref[:, :H] = dzu.astype(BF16)
        dzz_ref[:, H:] = dzv.astype(BF16)
        db_ref[:, :H] += jnp.sum(dzu, axis=0, keepdims=True)
        db_ref[:, H:] += jnp.sum(dzv, axis=0, keepdims=True)

    full3 = lambda shp: pl.BlockSpec(shp, lambda c: (0, 0, 0))
    vec = lambda n: pl.BlockSpec((1, n), lambda c: (0, 0))
    return pl.pallas_call(
        body, name=name, grid=(S // T,),
        in_specs=[pl.BlockSpec((T, H2), lambda c: (c, 0)), vec(H2), vec(H), full3((G, T, T)), full3((G, T, gd)),
                  pl.BlockSpec((T, H), lambda c: (c, 0))],
        out_specs=[pl.BlockSpec((T, H2), lambda c: (c, 0)), vec(H2), vec(H), full3((G, T, T)),
                   pl.BlockSpec((T, LANES), lambda c: (0, 0))],
        out_shape=[jax.ShapeDtypeStruct((S, H2), BF16), jax.ShapeDtypeStruct((1, H2), F32),
                   jax.ShapeDtypeStruct((1, H), F32), jax.ShapeDtypeStruct((G, T, T), F32),
                   jax.ShapeDtypeStruct((T, LANES), F32)],
        compiler_params=_params(("arbitrary",)),
    )(zzpre, b_in, v_gain, wc, bsf, dp)


def _shift_rows(v, k, n_rows):
    if k == 0:
        return v
    rolled = pltpu.roll(v, k % n_rows, 0)
    row = lax.broadcasted_iota(jnp.int32, v.shape, 0)
    keep = (row >= k) if k > 0 else (row < n_rows + k)
    return jnp.where(keep, rolled, 0.0)


def conv_fwd(zx, conv_w, conv_b, col0, *, name):
    S = zx.shape[0]
    C = conv_w.shape[1]
    tc = _pick(C, (256, 128))
    off = col0 // tc
    assert col0 % tc == 0

    def body(x_ref, w_ref, b_ref, o_ref):
        xv = x_ref[...]
        acc = b_ref[...] + w_ref[SSM_CONV - 1:SSM_CONV, :] * xv
        for k in range(SSM_CONV - 1):
            acc = acc + w_ref[k:k + 1, :] * _shift_rows(xv, SSM_CONV - 1 - k, S)
        o_ref[...] = acc * _sigmoid(acc)

    return pl.pallas_call(
        body, name=name, grid=(C // tc,),
        in_specs=[pl.BlockSpec((S, tc), lambda j: (0, off + j)), pl.BlockSpec((SSM_CONV, tc), lambda j: (0, j)),
                  pl.BlockSpec((1, tc), lambda j: (0, j))],
        out_specs=pl.BlockSpec((S, tc), lambda j: (0, j)), out_shape=jax.ShapeDtypeStruct((S, C), F32),
        compiler_params=_params(("parallel",)),
    )(zx, conv_w, conv_b)


def conv_bwd(zx, conv_w, conv_b, col0, douts, *, name):
    S = zx.shape[0]
    C = conv_w.shape[1]
    tc = LANES
    off = col0 // tc
    counts = [d.shape[1] // tc for d in douts]
    starts = [sum(counts[:p]) for p in range(len(douts))]
    assert sum(counts) * tc == C and all(d.shape[1] % tc == 0 for d in douts)

    def body(x_ref, w_ref, b_ref, *rest):
        do_refs, (dx_ref, dw_ref, db_ref) = rest[:len(douts)], rest[len(douts):]
        j = pl.program_id(0)
        dov = do_refs[-1][...]
        for p in reversed(range(len(douts) - 1)):
            dov = jnp.where(j < starts[p + 1], do_refs[p][...], dov)
        xv = x_ref[...]
        shifted = [_shift_rows(xv, SSM_CONV - 1 - k, S) for k in range(SSM_CONV)]
        acc = b_ref[...]
        for k in range(SSM_CONV):
            acc = acc + w_ref[k:k + 1, :] * shifted[k]
        s = _sigmoid(acc)
        dacc = dov * (s * (1.0 + acc * (1.0 - s)))
        db_ref[...] = jnp.sum(dacc, axis=0, keepdims=True)
        dx = jnp.zeros_like(xv)
        for k in range(SSM_CONV):
            dw_ref[k:k + 1, :] = jnp.sum(dacc * shifted[k], axis=0, keepdims=True)
            dx = dx + w_ref[k:k + 1, :] * _shift_rows(dacc, -(SSM_CONV - 1 - k), S)
        dx_ref[...] = dx.astype(BF16)

    slab = pl.BlockSpec((S, tc), lambda j: (0, j))
    piece_specs = [pl.BlockSpec((S, tc), lambda j, a=starts[p], n=counts[p]: (0, jnp.clip(j - a, 0, n - 1)))
                   for p in range(len(douts))]
    return pl.pallas_call(
        body, name=name, grid=(C // tc,),
        in_specs=[pl.BlockSpec((S, tc), lambda j: (0, off + j)), pl.BlockSpec((SSM_CONV, tc), lambda j: (0, j)),
                  pl.BlockSpec((1, tc), lambda j: (0, j))] + piece_specs,
        out_specs=[slab, pl.BlockSpec((SSM_CONV, tc), lambda j: (0, j)), pl.BlockSpec((1, tc), lambda j: (0, j))],
        out_shape=[jax.ShapeDtypeStruct((S, C), BF16), jax.ShapeDtypeStruct((SSM_CONV, C), F32),
                   jax.ShapeDtypeStruct((1, C), F32)],
        compiler_params=_params(("arbitrary",)),
    )(zx, conv_w, conv_b, *douts)


def _ssd_chunk_terms(dtraw, bias, a_log, tl):
    dt = _softplus(dtraw + bias)
    a_neg = -jnp.exp(a_log)
    ac = _dot_exact_left(tl, dt * a_neg)
    ac_last = ac[ac.shape[0] - 1:, :]
    return dt, a_neg, ac, ac.T, jnp.exp(ac), jnp.exp(ac_last - ac), jnp.exp(ac_last)


def _ssd_specs(S, L, G, hpg, pd, inner, gp):
    gw = hpg * pd
    n = SSM_STATE
    xb = inner // n
    assert G % gp == 0 and xb % gp == 0 and (xb + G) % gp == 0

    def mk(cidx):
        return dict(
            x=pl.BlockSpec((L, gp * gw), lambda g, c: (cidx(c), g)),
            b=pl.BlockSpec((L, gp * n), lambda g, c: (cidx(c), xb // gp + g)),
            c=pl.BlockSpec((L, gp * n), lambda g, c: (cidx(c), (xb + G) // gp + g)),
            z=pl.BlockSpec((L, gp * gw), lambda g, c: (cidx(c), g)),
            dt=pl.BlockSpec((L, gp * LANES), lambda g, c: (cidx(c), g)),
            gvec=pl.BlockSpec((gp, 1, LANES), lambda g, c: (g, 0, 0)),
            chan=pl.BlockSpec((1, gp * gw), lambda g, c: (0, g)),
            tri=pl.BlockSpec((L, L), lambda g, c: (0, 0)),
            hp=pl.BlockSpec((gp, 1, gw, n), lambda g, c: (g, cidx(c), 0, 0)),
            bc=pl.BlockSpec((L, gp * n), lambda g, c: (cidx(c), g)),
        )
    return mk


def _ssd_group_views(refs, kinds, gg, gw):
    n = SSM_STATE
    width = dict(x=gw, z=gw, chan=gw, b=n, c=n, bc=n, dt=LANES)
    out = []
    for ref, kind in zip(refs, kinds):
        if kind in width:
            out.append(ref.at[:, gg * width[kind]:(gg + 1) * width[kind]])
        elif kind in ("gvec", "hp"):
            out.append(ref.at[gg:gg + 1])
        elif kind == "state":
            out.append(ref.at[gg * gw:(gg + 1) * gw])
        else:
            out.append(ref)
    return out


def ssd_fwd(xbc, zx, dtg, bias_g, alog_g, d_chan, ngain, L, G, *, name):
    S = xbc.shape[0]
    n = SSM_STATE
    inner = xbc.shape[1] - 2 * G * n
    gw = inner // G
    pd = SB_HEAD_DIM
    hpg = gw // pd
    nc = S // L
    gp = SSD_FWD_GROUPS_PER_STEP
    sp = _ssd_specs(S, L, G, hpg, pd, inner, gp)(lambda c: c)

    kinds = ("x", "b", "c", "z", "dt", "gvec", "gvec", "chan", "chan", "tri", "x", "x", "hp", "state")

    def body(*refs):
        @pl.when(pl.program_id(1) == 0)
        def _():
            refs[-1][...] = jnp.zeros_like(refs[-1])

        for gg in range(gp):
            group_body(*_ssd_group_views(refs, kinds, gg, gw))

    def group_body(x_ref, b_ref, c_ref, z_ref, dt_ref, bias_ref, alog_ref, d_ref, ng_ref, tl_ref,
                   yn_ref, y_ref, hp_ref, state):
        dt, _, ac, act, ea, dte, cd = _ssd_chunk_terms(dt_ref[...], bias_ref[0], alog_ref[0], tl_ref[...])
        xv = x_ref[...]
        bm = b_ref[...].astype(BF16)
        cm = c_ref[...].astype(BF16)
        cb = _dot(cm, bm, 1, 1)
        tril = lax.broadcasted_iota(jnp.int32, (L, L), 1) <= lax.broadcasted_iota(jnp.int32, (L, L), 0)
        hp_ref[0, 0] = state[...]
        for r in range(hpg):
            ps = slice(r * pd, (r + 1) * pd)
            xr = xv[:, ps]
            xdt = xr * dt[:, r:r + 1]
            lm = jnp.exp(jnp.where(tril, ac[:, r:r + 1] - act[r:r + 1, :], -jnp.inf))
            hprev = state[ps, :]
            y = _dot((cb * lm).astype(BF16), xdt.astype(BF16))
            y = y + _dot(cm, hprev.astype(BF16), 1, 1) * ea[:, r:r + 1]
            y_ref[:, ps] = y + xr * d_ref[:, ps]
            st = _dot((xdt * dte[:, r:r + 1]).astype(BF16), bm, 0, 0)
            state[ps, :] = hprev * cd[:, r:r + 1] + st
        yfull = y_ref[...]
        zg = z_ref[...]
        yg = yfull * (zg * _sigmoid(zg))
        yn_ref[...] = (yg * lax.rsqrt(jnp.mean(yg * yg, axis=1, keepdims=True) + EPS) * ng_ref[...]).astype(BF16)

    return pl.pallas_call(
        body, name=name, grid=(G // gp, nc),
        in_specs=[sp["x"], sp["b"], sp["c"], sp["z"], sp["dt"], sp["gvec"], sp["gvec"], sp["chan"], sp["chan"], sp["tri"]],
        out_specs=[sp["x"], sp["x"], sp["hp"]],
        out_shape=[jax.ShapeDtypeStruct((S, inner), BF16), jax.ShapeDtypeStruct((S, inner), F32),
                   jax.ShapeDtypeStruct((G, nc, gw, n), F32)],
        scratch_shapes=[pltpu.VMEM((gp * gw, n), F32)],
        compiler_params=_params(("arbitrary", "arbitrary")),
    )(xbc, xbc, xbc, zx, dtg, bias_g, alog_g, d_chan, ngain, _tri(L, "row_ge_col"))


def ssd_bwd(xbc, zx, dtg, bias_g, alog_g, d_chan, ngain, yfull, hp, dyn, L, G, *, name):
    S = xbc.shape[0]
    n = SSM_STATE
    inner = xbc.shape[1] - 2 * G * n
    gw = inner // G
    pd = SB_HEAD_DIM
    hpg = gw // pd
    nc = S // L
    gp = SSD_BWD_GROUPS_PER_STEP
    sp = _ssd_specs(S, L, G, hpg, pd, inner, gp)(lambda c: nc - 1 - c)

    kinds = ("x", "b", "c", "z", "dt", "gvec", "gvec", "chan", "chan", "tri", "tri", "x", "hp", "x",
             "x", "x", "bc", "bc", "dt", "gvec", "gvec", "gvec", "chan", "state")

    def body(*refs):
        @pl.when(pl.program_id(1) == 0)
        def _():
            for acc in refs[-5:]:
                acc[...] = jnp.zeros_like(acc)

        for gg in range(gp):
            group_body(*_ssd_group_views(refs, kinds, gg, gw))

    def group_body(x_ref, b_ref, c_ref, z_ref, dt_ref, bias_ref, alog_ref, d_ref, ng_ref, tl_ref, tu_ref,
                   yf_ref, hp_ref, dyn_ref,
                   dz_ref, dx_ref, db_ref, dc_ref, ddt_ref, dbias_ref, dalog_ref, dd_ref, dng_ref, dstate):

        dtraw = dt_ref[...]
        dt, a_neg, ac, act, ea, dte, cd = _ssd_chunk_terms(dtraw, bias_ref[0], alog_ref[0], tl_ref[...])
        xv = x_ref[...]
        bm = b_ref[...].astype(BF16)
        cm = c_ref[...].astype(BF16)
        cb = _dot(cm, bm, 1, 1)
        tril = lax.broadcasted_iota(jnp.int32, (L, L), 1) <= lax.broadcasted_iota(jnp.int32, (L, L), 0)
        lane = lax.broadcasted_iota(jnp.int32, (L, LANES), 1)
        lane1 = lax.broadcasted_iota(jnp.int32, (1, LANES), 1)

        yfull = yf_ref[...]
        zg = z_ref[...]
        sg = _sigmoid(zg)
        gate = zg * sg
        yg = yfull * gate
        rr = lax.rsqrt(jnp.mean(yg * yg, axis=1, keepdims=True) + EPS)
        yhat = yg * rr
        dynv = dyn_ref[...]
        dng_ref[...] += jnp.sum(dynv * yhat, axis=0, keepdims=True)
        t = dynv * ng_ref[...]
        dyg = rr * (t - yhat * jnp.mean(yhat * t, axis=1, keepdims=True))
        dy = dyg * gate
        dz_ref[...] = (dyg * yfull * (sg * (1.0 + zg * (1.0 - sg)))).astype(BF16)

        dcb = jnp.zeros((L, L), F32)
        dc_acc = jnp.zeros((L, n), F32)
        db_acc = jnp.zeros((L, n), F32)
        dac = jnp.zeros((L, LANES), F32)
        xdx = jnp.zeros((L, LANES), F32)
        tail = jnp.zeros((1, LANES), F32)
        dskip = jnp.zeros((1, LANES), F32)
        ones_l = jnp.ones((L, LANES), BF16)
        for r in range(hpg):
            ps = slice(r * pd, (r + 1) * pd)
            xr = xv[:, ps]
            dyr = dy[:, ps]
            dtr = dt[:, r:r + 1]
            dter = dte[:, r:r + 1]
            cdr = cd[:, r:r + 1]
            xdt = xr * dtr
            xdtb = xdt.astype(BF16)
            dyrb = dyr.astype(BF16)
            lm = jnp.exp(jnp.where(tril, ac[:, r:r + 1] - act[r:r + 1, :], -jnp.inf))
            m32 = cb * lm
            mb = m32.astype(BF16)
            hprev = hp_ref[0, 0, ps, :]
            hpb = hprev.astype(BF16)
            dhn = dstate[ps, :]
            dhnb = dhn.astype(BF16)
            ear = ea[:, r:r + 1]
            gy = (dyr * ear).astype(BF16)
            dc_acc = dc_acc + _dot(gy, hpb)
            dstate[ps, :] = _dot(gy, cm, 0, 0) + dhn * cdr
            bdh = _dot(bm, dhnb, 1, 1)
            db_acc = db_acc + _dot((xdt * dter).astype(BF16), dhnb)
            dm = _dot(dyrb, xdtb, 1, 1)
            dxdt = bdh * dter + _dot(mb, dyrb, 0, 0)
            dcb = dcb + dm * lm
            wmat = dm * m32
            whi = wmat.astype(BF16)
            wlo = (wmat - whi.astype(F32)).astype(BF16)
            col_w = _dot(whi, ones_l, 0, 0) + _dot(wlo, ones_l, 0, 0)
            t_end = xdt * bdh * dter
            e_r = jnp.sum(wmat, axis=1, keepdims=True) \
                + jnp.sum(dyr * _dot(cm, hpb, 1, 1) * ear - t_end, axis=1, keepdims=True)
            c_r = cdr * _sum_all(dhn * hprev) + _sum_all(t_end)
            dac = dac + jnp.where(lane == r, e_r - col_w, 0.0)
            xdx = xdx + jnp.where(lane == r, jnp.sum(dxdt * xr, axis=1, keepdims=True), 0.0)
            tail = tail + jnp.where(lane1 == r, c_r, 0.0)
            dskip = dskip + jnp.where(lane1 == r, _sum_all(dyr * xr), 0.0)
            dx_ref[:, ps] = dxdt * dtr + dyr * d_ref[:, ps]
        dcbb = dcb.astype(BF16)
        dc_ref[...] = dc_acc + _dot(dcbb, bm)
        db_ref[...] = db_acc + _dot(dcbb, cm, 0, 0)
        da = _dot_exact_left(tu_ref[...], dac) + tail
        real = lane < hpg
        ddt = jnp.where(real, (da * a_neg + xdx) * _sigmoid(dtraw + bias_ref[0]), 0.0)
        ddt_ref[...] = ddt
        dd_ref[0] += dskip
        dbias_ref[0] += jnp.sum(ddt, axis=0, keepdims=True)
        dalog_ref[0] += jnp.where(lane1 < hpg, jnp.sum(da * dt, axis=0, keepdims=True) * a_neg, 0.0)

    return pl.pallas_call(
        body, name=name, grid=(G // gp, nc),
        in_specs=[sp["x"], sp["b"], sp["c"], sp["z"], sp["dt"], sp["gvec"], sp["gvec"], sp["chan"], sp["chan"],
                  sp["tri"], sp["tri"], sp["x"], sp["hp"], sp["x"]],
        out_specs=[sp["x"], sp["x"], sp["bc"], sp["bc"], sp["dt"], sp["gvec"], sp["gvec"], sp["gvec"], sp["chan"]],
        out_shape=[jax.ShapeDtypeStruct((S, inner), BF16), jax.ShapeDtypeStruct((S, inner), F32),
                   jax.ShapeDtypeStruct((S, G * n), F32), jax.ShapeDtypeStruct((S, G * n), F32),
                   jax.ShapeDtypeStruct((S, G * LANES), F32), jax.ShapeDtypeStruct((G, 1, LANES), F32),
                   jax.ShapeDtypeStruct((G, 1, LANES), F32), jax.ShapeDtypeStruct((G, 1, LANES), F32),
                   jax.ShapeDtypeStruct((1, inner), F32)],
        scratch_shapes=[pltpu.VMEM((gp * gw, n), F32)],
        compiler_params=_params(("arbitrary", "arbitrary")),
    )(xbc, xbc, xbc, zx, dtg, bias_g, alog_g, d_chan, ngain, _tri(L, "row_ge_col"), _tri(L, "row_le_col"),
      yfull, hp, dyn)


def _spread_dt(w_dt_t, G, hpg):
    K = w_dt_t.shape[1]
    w = w_dt_t.reshape(G, hpg, K)
    return jnp.pad(w, ((0, 0), (0, LANES - hpg), (0, 0))).reshape(G * LANES, K)


def _group_vec(v, G, hpg):
    return jnp.pad(v.reshape(G, 1, hpg), ((0, 0), (0, 0), (0, LANES - hpg)))


def local_step(x, target, W, late=None):
    S, D = x.shape
    depth = W["mix_norm"].shape[0]
    gm_groups, gm_chunk = W["gm_w_s"].shape[1], W["gm_w_s"].shape[2]
    heads = W["ssm_dt_bias"].shape[1]
    inner = heads * SB_HEAD_DIM
    L = gm_chunk
    received = None

    saved = []
    for i in range(depth):
        kind, j = i % 3, i // 3
        s = dict(x=x)
        h = rms_fwd(x, W["mix_norm"][i:i + 1], name="rms_mix_fwd")
        s["h"] = h
        if kind == 0:
            qkv = matmul(h, W["sb_w_qkv"][j], tb=True, name="mm_qkv")
            if late is not None and i == 0:
                o, gathered = sb_attn_fwd(qkv, W["sb_q_gain"][j:j + 1], W["sb_k_gain"][j:j + 1], name="sb_fwd_gather",
                                          gather=late.shards)
                late.fill(W, gathered)
            else:
                o = sb_attn_fwd(qkv, W["sb_q_gain"][j:j + 1], W["sb_k_gain"][j:j + 1], name="sb_fwd")
            x1 = matmul(o, W["sb_w_o"][j], residual=x, name="mm_sb_out")
            s.update(qkv=qkv, o=o)
        elif kind == 1:
            wc = jnp.where(jnp.tril(jnp.ones((gm_chunk, gm_chunk), bool)), W["gm_w_s"][j], 0.0).astype(BF16)
            bsf = jnp.broadcast_to(W["gm_b_s"][j][:, :, None], (gm_groups, gm_chunk, W["gm_v_gain"].shape[1] // gm_groups)).astype(F32)
            zzpre = matmul(h, W["gm_w_in"][j], tb=True, name="mm_gm_in")
            p = gmlp_fwd(zzpre, W["gm_b_in"][j:j + 1], W["gm_v_gain"][j:j + 1], wc, bsf, name="gm_fwd")
            x1 = matmul(p, W["gm_w_out"][j], residual=x, name="mm_gm_out")
            s.update(zzpre=zzpre, p=p, wc=wc, bsf=bsf)
        else:
            conv_dim = W["ssm_conv_w"].shape[2]
            G = (conv_dim - inner) // (2 * SSM_STATE)
            hpg = heads // G
            w_in = W["ssm_w_in"][j]
            w_zx = w_in[:inner + conv_dim]
            w_dtg = _spread_dt(w_in[inner + conv_dim:], G, hpg)
            bias_g = _group_vec(W["ssm_dt_bias"][j], G, hpg)
            alog_g = _group_vec(W["ssm_a_log"][j], G, hpg)
            d_chan = jnp.repeat(W["ssm_d"][j], SB_HEAD_DIM)[None, :]
            ngain = W["ssm_norm_gain"][j:j + 1]
            zx = matmul(h, w_zx, tb=True, name="mm_ssm_zx")
            dtg = matmul(h, w_dtg, tb=True, name="mm_ssm_dt")
            xbc = conv_fwd(zx, W["ssm_conv_w"][j], W["ssm_conv_b"][j:j + 1], inner, name="conv_fwd")
            yn, yfull, hp = ssd_fwd(xbc, zx, dtg, bias_g, alog_g, d_chan, ngain, L, G, name="ssd_fwd")
            x1 = matmul(yn, W["ssm_w_out"][j], residual=x, name="mm_ssm_out")
            s.update(w_zx=w_zx, w_dtg=w_dtg, bias_g=bias_g, alog_g=alog_g, d_chan=d_chan, ngain=ngain,
                     zx=zx, dtg=dtg, xbc=xbc, yn=yn, yfull=yfull, hp=hp)
        h2 = rms_fwd(x1, W["ffn_norm"][i:i + 1], name="rms_ffn_fwd")
        gu, a = ffn_up_fwd(h2, W["ffn_w_gu"][i], name="ffn_up_fwd")
        x2 = matmul(a, W["ffn_w_down"][i], residual=x1, name="mm_ffn_down")
        s.update(x1=x1, h2=h2, gu=gu, a=a)
        saved.append(s)
        x = x2

    dx, loss = loss_head(x, target, name="loss_head")

    gw = {k: {} for k in WEIGHTS}
    for i in reversed(range(depth)):
        kind, j = i % 3, i // 3
        s = saved[i]
        gw["ffn_w_down"][i] = matmul(s["a"], dx, ta=True, out_dtype=BF16, name="mm_ffn_dwdown")
        dgu = ffn_up_bwd(dx, W["ffn_w_down"][i], s["gu"], name="ffn_up_bwd")
        dh2 = matmul(dgu, W["ffn_w_gu"][i], a_split=2, name="mm_ffn_dh")
        gw["ffn_w_gu"][i] = matmul(dgu, s["h2"], ta=True, a_split=2, out_dtype=BF16, name="mm_ffn_dwgu")
        dx1, dgn = rms_bwd(s["x1"], W["ffn_norm"][i:i + 1], dh2, dx, name="rms_ffn_bwd")
        gw["ffn_norm"][i] = dgn[0]
        if kind == 0:
            do = matmul(dx1, W["sb_w_o"][j], tb=True, out_dtype=BF16, name="mm_sb_do")
            gw["sb_w_o"][j] = matmul(s["o"], dx1, ta=True, out_dtype=BF16, name="mm_sb_dwo")
            if late is not None and i == 0:
                dqkv, dqg, dkg, received = sb_attn_bwd(
                    s["qkv"], s["o"], do, W["sb_q_gain"][j:j + 1], W["sb_k_gain"][j:j + 1], name="sb_bwd_scatter",
                    scatter=late.contributions(gw))
            else:
                dqkv, dqg, dkg = sb_attn_bwd(s["qkv"], s["o"], do, W["sb_q_gain"][j:j + 1], W["sb_k_gain"][j:j + 1],
                                             name="sb_bwd")
            gw["sb_q_gain"][j] = dqg[0]
            gw["sb_k_gain"][j] = dkg[0]
            dh = matmul(dqkv, W["sb_w_qkv"][j], a_split=3, name="mm_sb_dh")
            gw["sb_w_qkv"][j] = matmul(dqkv, s["h"], ta=True, a_split=3, out_dtype=BF16, name="mm_sb_dwqkv")
        elif kind == 1:
            dp = matmul(dx1, W["gm_w_out"][j], tb=True, name="mm_gm_dp")
            gw["gm_w_out"][j] = matmul(s["p"], dx1, ta=True, out_dtype=BF16, name="mm_gm_dwout")
            dzz, db_in, dvg, dws, dbs = gmlp_bwd(s["zzpre"], W["gm_b_in"][j:j + 1], W["gm_v_gain"][j:j + 1],
                                                s["wc"], s["bsf"], dp, name="gm_bwd")
            gw["gm_b_in"][j] = db_in[0]
            gw["gm_v_gain"][j] = dvg[0]
            gw["gm_w_s"][j] = dws
            gw["gm_b_s"][j] = dbs[:, :gm_groups].T
            dh = matmul(dzz, W["gm_w_in"][j], name="mm_gm_dh")
            gw["gm_w_in"][j] = matmul(dzz, s["h"], ta=True, out_dtype=BF16, name="mm_gm_dwin")
        else:
            conv_dim = W["ssm_conv_w"].shape[2]
            G = (conv_dim - inner) // (2 * SSM_STATE)
            hpg = heads // G
            dyn = matmul(dx1, W["ssm_w_out"][j], tb=True, name="mm_ssm_dyn")
            gw["ssm_w_out"][j] = matmul(s["yn"], dx1, ta=True, out_dtype=BF16, name="mm_ssm_dwout")
            dz, dxs, dbm, dcm, ddt, dbias, dalog, dd, dng = ssd_bwd(
                s["xbc"], s["zx"], s["dtg"], s["bias_g"], s["alog_g"], s["d_chan"], s["ngain"], s["yfull"], s["hp"],
                dyn, L, G, name="ssd_bwd")
            dpre, dcw, dcb = conv_bwd(s["zx"], W["ssm_conv_w"][j], W["ssm_conv_b"][j:j + 1], inner, [dxs, dbm, dcm],
                                      name="conv_bwd")
            dzx = jnp.concatenate([dz, dpre], axis=1)
            dh = matmul(ddt, s["w_dtg"], name="mm_ssm_dh_dt")
            dh = matmul(dzx, s["w_zx"], residual=dh, name="mm_ssm_dh")
            dw_zx = matmul(dzx, s["h"], ta=True, out_dtype=BF16, name="mm_ssm_dwzx")
            dw_dtg = matmul(ddt, s["h"], ta=True, out_dtype=BF16, name="mm_ssm_dwdt")
            dw_dt = dw_dtg.reshape(G, LANES, D)[:, :hpg, :].reshape(heads, D)
            gw["ssm_w_in"][j] = jnp.concatenate([dw_zx, dw_dt], axis=0)
            gw["ssm_conv_w"][j] = dcw
            gw["ssm_conv_b"][j] = dcb[0]
            gw["ssm_dt_bias"][j] = dbias[:, 0, :hpg].reshape(heads)
            gw["ssm_a_log"][j] = dalog[:, 0, :hpg].reshape(heads)
            gw["ssm_d"][j] = dd[:, 0, :hpg].reshape(heads)
            gw["ssm_norm_gain"][j] = dng[0]
        dx, dgn = rms_bwd(s["x"], W["mix_norm"][i:i + 1], dh, dx1, name="rms_mix_bwd")
        gw["mix_norm"][i] = dgn[0]

    return loss, dx, gw, received


MESH = pl.DeviceIdType.MESH
HBM_SPEC = pl.BlockSpec(memory_space=pltpu.HBM)
VMEM_SPEC = pl.BlockSpec(memory_space=pltpu.VMEM)


def _my_position():
    return lax.axis_index("x"), lax.axis_index("y"), lax.axis_index("c")


def _flip(v, bit):
    return 1 - v if bit else v


def all_gather(shards, *, name):
    n = len(shards)

    def body(*refs):
        for phase in ("start", "forward", "finish"):
            _ag_phase(refs[:n], refs[n:2 * n], refs[2 * n:], phase)

    return pl.pallas_call(
        body, name=name, out_shape=[jax.ShapeDtypeStruct((N_DEV,) + s.shape, s.dtype) for s in shards],
        in_specs=[HBM_SPEC] * n, out_specs=[HBM_SPEC] * n, scratch_shapes=_copy_semaphores(n),
    )(*shards)


def _ag_phase(x_refs, out_refs, sems, phase):
    send_sems, recv_sems, local_sems = sems
    x, y, c = _my_position()
    me, sibling = (x, y, c), (x, y, 1 - c)
    chips = [(1 - x, y), (x, 1 - y), (1 - x, 1 - y)]
    for p, (x_ref, out_ref) in enumerate(zip(x_refs, out_refs)):
        def slot(px, py, pc):
            return out_ref.at[4 * px + 2 * py + pc]

        def copy(k, block, to, src=None):
            return pltpu.make_async_remote_copy(
                src_ref=slot(*block) if src is None else src, dst_ref=slot(*block),
                send_sem=send_sems.at[7 * p + k], recv_sem=recv_sems.at[7 * p + k], device_id=to, device_id_type=MESH)

        mine = pltpu.make_async_copy(x_ref, slot(*me), local_sems.at[p])
        first = [copy(0, me, sibling, src=x_ref)]
        first += [copy(1 + j, me, (*chip, c), src=x_ref) for j, chip in enumerate(chips)]
        passed = [copy(4 + j, (*chip, c), sibling) for j, chip in enumerate(chips)]
        if phase == "start":
            mine.start()
            for cp in first:
                cp.start()
        elif phase == "forward":
            for j, chip in enumerate(chips):
                copy(1 + j, (*chip, c), me).wait_recv()
                passed[j].start()
        else:
            copy(0, sibling, me).wait_recv()
            for j, chip in enumerate(chips):
                copy(4 + j, (*chip, 1 - c), me).wait_recv()
            for cp in first + passed:
                cp.wait_send()
            mine.wait()


def _copy_semaphores(n):
    return [pltpu.SemaphoreType.DMA((7 * n,)), pltpu.SemaphoreType.DMA((7 * n,)), pltpu.SemaphoreType.DMA((n,))]


def _rs_phase(g_refs, out_refs, sems, phase):
    send_sems, recv_sems, local_sems = sems
    x, y, c = _my_position()
    me = 4 * x + 2 * y + c
    copies = []
    for p, (g_ref, out_ref) in enumerate(zip(g_refs, out_refs)):
        copies.append(pltpu.make_async_copy(g_ref.at[me], out_ref.at[me], local_sems.at[p]))
        for k in range(1, N_DEV):
            px, py, pc = _flip(x, k & 4), _flip(y, k & 2), _flip(c, k & 1)
            copies.append(pltpu.make_async_remote_copy(
                src_ref=g_ref.at[4 * px + 2 * py + pc], dst_ref=out_ref.at[me],
                send_sem=send_sems.at[7 * p + k - 1], recv_sem=recv_sems.at[7 * p + k - 1],
                device_id=(px, py, pc), device_id_type=MESH))
    for cp in copies:
        if phase == "start":
            cp.start()
        else:
            cp.wait()


def sum_slots(recv, *, name):
    n, R, C = recv.shape
    tr = _pick(R, (512, 256, 128))

    def body(r_ref, o_ref):
        acc = r_ref[0].astype(F32)
        for s in range(1, n):
            acc = acc + r_ref[s].astype(F32)
        o_ref[...] = acc

    return pl.pallas_call(
        body, name=name, grid=(R // tr,), in_specs=[pl.BlockSpec((n, tr, C), lambda i: (0, i, 0))],
        out_specs=pl.BlockSpec((tr, C), lambda i: (i, 0)), out_shape=jax.ShapeDtypeStruct((R, C), F32),
        compiler_params=_params(("parallel",)),
    )(recv)


def all_reduce_small(vs, scatter, *, name):
    n, ns = len(vs), len(scatter)

    def body(*refs):
        v_refs, g_refs = refs[:n], refs[n:n + ns]
        o_refs, r_refs = refs[n + ns:2 * n + ns], refs[2 * n + ns:2 * n + 2 * ns]
        bufs = refs[2 * n + 2 * ns:3 * n + 2 * ns]
        send_sems, recv_sems = refs[3 * n + 2 * ns:3 * n + 2 * ns + 2]
        rs_sems = refs[3 * n + 2 * ns + 2:]
        x, y, c = _my_position()
        me = 4 * x + 2 * y + c
        _rs_phase(g_refs, r_refs, rs_sems, "start")
        copies = []
        for p, (v_ref, buf) in enumerate(zip(v_refs, bufs)):
            buf[me] = v_ref[...]
            for k in range(1, N_DEV):
                px, py, pc = _flip(x, k & 4), _flip(y, k & 2), _flip(c, k & 1)
                copies.append(pltpu.make_async_remote_copy(
                    src_ref=v_ref, dst_ref=buf.at[me], send_sem=send_sems.at[7 * p + k - 1],
                    recv_sem=recv_sems.at[7 * p + k - 1], device_id=(px, py, pc), device_id_type=MESH))
        for cp in copies:
            cp.start()
        for cp in copies:
            cp.wait()
        for o_ref, buf in zip(o_refs, bufs):
            acc = buf[0]
            for s in range(1, N_DEV):
                acc = acc + buf[s]
            o_ref[...] = acc
        _rs_phase(g_refs, r_refs, rs_sems, "finish")

    out = pl.pallas_call(
        body, name=name,
        out_shape=[jax.ShapeDtypeStruct(v.shape, F32) for v in vs] + [jax.ShapeDtypeStruct(g.shape, g.dtype) for g in scatter],
        in_specs=[VMEM_SPEC] * n + [HBM_SPEC] * ns, out_specs=[VMEM_SPEC] * n + [HBM_SPEC] * ns,
        scratch_shapes=[pltpu.VMEM((N_DEV,) + v.shape, F32) for v in vs]
        + [pltpu.SemaphoreType.DMA((7 * n,)), pltpu.SemaphoreType.DMA((7 * n,))] + _copy_semaphores(ns),
        compiler_params=pltpu.CompilerParams(vmem_limit_bytes=VMEM_LIMIT_BYTES),
    )(*vs, *scatter)
    return list(out[:n]), list(out[n:])


def _pad_rows(a, mult):
    pad = (-a.shape[0]) % mult
    return jnp.pad(a, ((0, pad), (0, 0))) if pad else a


def _pack_small(arrays):
    flat = []
    for a in arrays:
        f = a.reshape(-1).astype(F32)
        flat.append(jnp.pad(f, (0, (-f.shape[0]) % LANES)))
    return _pad_rows(jnp.concatenate(flat).reshape(-1, LANES), 8)


def _unpack_small(packed, shapes):
    flat = packed.reshape(-1)
    out, r = [], 0
    for shp in shapes:
        n = math.prod(shp)
        out.append(flat[r:r + n].reshape(shp))
        r += n + (-n) % LANES
    return out


ARG_NAMES = ("x",) + WEIGHTS + ("loss_target",) + tuple("m_" + w for w in WEIGHTS) + tuple("v_" + w for w in WEIGHTS)


def kernel(x, mix_norm, ffn_norm, sb_w_qkv, sb_q_gain, sb_k_gain, sb_w_o, gm_w_in, gm_b_in, gm_v_gain, gm_w_s, gm_b_s, gm_w_out, ssm_w_in, ssm_conv_w, ssm_conv_b, ssm_dt_bias, ssm_a_log, ssm_d, ssm_norm_gain, ssm_w_out, ffn_w_gu, ffn_w_down, loss_target, m_mix_norm, m_ffn_norm, m_sb_w_qkv, m_sb_q_gain, m_sb_k_gain, m_sb_w_o, m_gm_w_in, m_gm_b_in, m_gm_v_gain, m_gm_w_s, m_gm_b_s, m_gm_w_out, m_ssm_w_in, m_ssm_conv_w, m_ssm_conv_b, m_ssm_dt_bias, m_ssm_a_log, m_ssm_d, m_ssm_norm_gain, m_ssm_w_out, m_ffn_w_gu, m_ffn_w_down, v_mix_norm, v_ffn_norm, v_sb_w_qkv, v_sb_q_gain, v_sb_k_gain, v_sb_w_o, v_gm_w_in, v_gm_b_in, v_gm_v_gain, v_gm_w_s, v_gm_b_s, v_gm_w_out, v_ssm_w_in, v_ssm_conv_w, v_ssm_conv_b, v_ssm_dt_bias, v_ssm_a_log, v_ssm_d, v_ssm_norm_gain, v_ssm_w_out, v_ffn_w_gu, v_ffn_w_down):
    given = dict(zip(ARG_NAMES, (x, mix_norm, ffn_norm, sb_w_qkv, sb_q_gain, sb_k_gain, sb_w_o, gm_w_in, gm_b_in, gm_v_gain, gm_w_s, gm_b_s, gm_w_out, ssm_w_in, ssm_conv_w, ssm_conv_b, ssm_dt_bias, ssm_a_log, ssm_d, ssm_norm_gain, ssm_w_out, ffn_w_gu, ffn_w_down, loss_target, m_mix_norm, m_ffn_norm, m_sb_w_qkv, m_sb_q_gain, m_sb_k_gain, m_sb_w_o, m_gm_w_in, m_gm_b_in, m_gm_v_gain, m_gm_w_s, m_gm_b_s, m_gm_w_out, m_ssm_w_in, m_ssm_conv_w, m_ssm_conv_b, m_ssm_dt_bias, m_ssm_a_log, m_ssm_d, m_ssm_norm_gain, m_ssm_w_out, m_ffn_w_gu, m_ffn_w_down, v_mix_norm, v_ffn_norm, v_sb_w_qkv, v_sb_q_gain, v_sb_k_gain, v_sb_w_o, v_gm_w_in, v_gm_b_in, v_gm_v_gain, v_gm_w_s, v_gm_b_s, v_gm_w_out, v_ssm_w_in, v_ssm_conv_w, v_ssm_conv_b, v_ssm_dt_bias, v_ssm_a_log, v_ssm_d, v_ssm_norm_gain, v_ssm_w_out, v_ffn_w_gu, v_ffn_w_down)))
    mx, my, mc = _my_position()
    me = 4 * mx + 2 * my + mc

    pieces = [(k, l) for k in BIG for l in range(given[k].shape[0])]
    early = [("sb_w_qkv", 0)]
    late_pieces = [p for p in pieces if p not in early]
    last = [("sb_w_qkv", 0)]
    main = [p for p in pieces if p not in last]

    def shards_of(ps):
        return [(given[k][l].T if k in COL_SHARDED else given[k][l]).astype(BF16) for k, l in ps]

    def piece_to_full(g, k):
        rows, cols = given[k].shape[1:]
        return g.reshape(N_DEV * cols, rows) if k in COL_SHARDED else g.reshape(N_DEV * rows, cols)

    def full_to_piece(full, k):
        return full.reshape(N_DEV, -1, PACK_COLS)

    def summed_to_shard(g, k):
        rows, cols = given[k].shape[1:]
        return g.reshape(cols, rows) if k in COL_SHARDED else g.reshape(rows, cols)

    def contributions(gw, ps):
        return [full_to_piece(gw[k][l], k) for k, l in ps]

    sharded_small = [lax.bitcast_convert_type(given[k], BF16) for k in SMALL_SHARDED]
    tail = jnp.concatenate([a.reshape(-1) for a in sharded_small])
    tail = jnp.pad(tail, (0, (-tail.size) % PACK_COLS)).reshape(-1, PACK_COLS)

    W = {k: given[k] for k in SMALL if k not in SMALL_SHARDED}
    W.update({k: [None] * given[k].shape[0] for k in BIG})
    for (k, l), g in zip(early, all_gather(shards_of(early), name="all_gather_early")):
        W[k][l] = piece_to_full(g, k)

    class Late:
        shards = shards_of(late_pieces) + [tail]

        @staticmethod
        def fill(weights, gathered):
            for (k, l), g in zip(late_pieces, gathered):
                weights[k][l] = piece_to_full(g, k)
            tail_g = gathered[-1].reshape(N_DEV, -1)
            off = 0
            for k, a in zip(SMALL_SHARDED, sharded_small):
                g = lax.bitcast_convert_type(tail_g[:, off:off + a.size].reshape((N_DEV,) + a.shape), F32)
                weights[k] = jnp.moveaxis(g, 0, -2).reshape(g.shape[1:-1] + (N_DEV * g.shape[-1],))
                off += a.size

        @staticmethod
        def contributions(gw):
            return contributions(gw, main)

    loss, gx, gw, received_main = local_step(given["x"][0], given["loss_target"][0], W, late=Late)

    grads_small = {k: jnp.stack([gw[k][l] for l in sorted(gw[k])], axis=0) for k in SMALL}
    packed_names = tuple(k for k in SMALL if k != "gm_w_s")
    small_shapes = [grads_small[k].shape for k in packed_names] + [(1, 1)]
    (red_ws, red_rest), received_last = all_reduce_small(
        [grads_small["gm_w_s"].reshape(-1, LANES), _pack_small([grads_small[k] for k in packed_names] + [loss])],
        contributions(gw, last), name="all_reduce_small_and_last_exchange")
    small_full = dict(zip(packed_names + ("loss",), _unpack_small(red_rest, small_shapes)))
    small_full["gm_w_s"] = red_ws.reshape(grads_small["gm_w_s"].shape)

    g_piece = {}
    for grp, received in ((main, received_main), (last, received_last)):
        for p, r in zip(grp, received):
            g_piece[p] = summed_to_shard(sum_slots(r, name="reduce_scatter_sum"), p[0])
    out_g, out_d, out_m, out_v = {}, {}, {}, {}
    for k in BIG:
        swap = lambda a: jnp.swapaxes(a, -1, -2)
        keep_t = k in COL_SHARDED and given[k].shape[-1] % LANES != 0
        flip = swap if keep_t else (lambda a: a)
        g = jnp.stack([g_piece[(k, l)] for l in range(given[k].shape[0])], axis=0)
        if k in COL_SHARDED and not keep_t:
            g = swap(g)
        res = adamw(flip(given[k]), g, flip(given["m_" + k]), flip(given["v_" + k]), name="adamw_" + k)
        out_g[k], out_d[k], out_m[k], out_v[k] = (flip(a) for a in (g,) + tuple(res))

    gsmall = {}
    for k in SMALL:
        g = small_full[k]
        if k in SMALL_SHARDED:
            n = given[k].shape[-1]
            g = lax.dynamic_slice_in_dim(g, me * n, n, axis=g.ndim - 1)
        gsmall[k] = g
    local_shapes = [given[k].shape for k in packed_names]
    dsm, nmsm, nvsm = adamw(*[_pack_small([src[k] for k in packed_names]) for src in (
        given, gsmall, {k: given["m_" + k] for k in packed_names}, {k: given["v_" + k] for k in packed_names})],
        name="adamw_small")
    out_g.update(gsmall)
    for dst, src in ((out_d, dsm), (out_m, nmsm), (out_v, nvsm)):
        dst.update(zip(packed_names, _unpack_small(src, local_shapes)))
    ws_shape = given["gm_w_s"].shape
    out_d["gm_w_s"], out_m["gm_w_s"], out_v["gm_w_s"] = (a.reshape(ws_shape) for a in adamw(
        *[a.reshape((-1,) + ws_shape[-2:]) for a in (given["gm_w_s"], gsmall["gm_w_s"], given["m_gm_w_s"], given["v_gm_w_s"])],
        name="adamw_gm_w_s"))

    return (small_full["loss"].reshape(()), gx[None],
            *[out_g[k] for k in WEIGHTS], *[out_d[k] for k in WEIGHTS],
            *[out_m[k] for k in WEIGHTS], *[out_v[k] for k in WEIGHTS])
```

```python
import math

import jax
import jax.numpy as jnp
from jax import lax
from jax.experimental import pallas as pl
from jax.experimental.pallas import tpu as pltpu

F32 = jnp.float32
BF16 = jnp.bfloat16
EPS = 1e-6
N_DEV = 8
SB_HEAD_DIM = 64
SB_TILE = 256
SB_FWD_QUERY_BLOCKS = 4
SB_BWD_QUERY_BLOCKS = 2
SSM_STATE = 128
SSD_FWD_GROUPS_PER_STEP = 2
SSD_BWD_GROUPS_PER_STEP = 1
SSM_CONV = 4
ADAM_LR = 0.001
ADAM_B1 = 0.9
ADAM_B2 = 0.999
ADAM_EPS = 1e-08
ADAM_WD = 0.01
ADAM_STEP = 10
VMEM_LIMIT_BYTES = 56 * 1024 * 1024
MATMUL_VMEM_BUDGET = 40 * 1024 * 1024
LANES = 128
PACK_COLS = 1024

BIG = ("sb_w_qkv", "sb_w_o", "gm_w_in", "gm_w_out", "ssm_w_in", "ssm_w_out", "ffn_w_gu", "ffn_w_down")
COL_SHARDED = ("sb_w_qkv", "gm_w_in", "ssm_w_in", "ffn_w_gu")
SMALL = ("mix_norm", "ffn_norm", "sb_q_gain", "sb_k_gain", "gm_b_in", "gm_v_gain", "gm_w_s", "gm_b_s",
         "ssm_conv_w", "ssm_conv_b", "ssm_dt_bias", "ssm_a_log", "ssm_d", "ssm_norm_gain")
SMALL_SHARDED = ("ssm_conv_w", "ssm_conv_b", "ssm_norm_gain")
WEIGHTS = ("mix_norm", "ffn_norm", "sb_w_qkv", "sb_q_gain", "sb_k_gain", "sb_w_o", "gm_w_in", "gm_b_in",
           "gm_v_gain", "gm_w_s", "gm_b_s", "gm_w_out", "ssm_w_in", "ssm_conv_w", "ssm_conv_b", "ssm_dt_bias",
           "ssm_a_log", "ssm_d", "ssm_norm_gain", "ssm_w_out", "ffn_w_gu", "ffn_w_down")


def _params(semantics=None):
    return pltpu.CompilerParams(dimension_semantics=semantics, vmem_limit_bytes=VMEM_LIMIT_BYTES)


def _pick(n, prefs):
    for t in prefs:
        if t <= n and n % t == 0:
            return t
    return n


def _dot(a, b, ca=1, cb=0):
    return lax.dot_general(a, b, (((ca,), (cb,)), ((), ())), preferred_element_type=F32)


def _split3(v):
    h1 = v.astype(BF16)
    r1 = v - h1.astype(F32)
    h2 = r1.astype(BF16)
    h3 = (r1 - h2.astype(F32)).astype(BF16)
    return h1, h2, h3


def _dot_exact_left(mat01, v):
    h1, h2, h3 = _split3(v)
    return _dot(mat01, h1) + _dot(mat01, h2) + _dot(mat01, h3)


def _sum_all(v):
    return jnp.sum(jnp.sum(v, axis=0, keepdims=True), axis=1, keepdims=True)


def _sigmoid(v):
    return 1.0 / (1.0 + jnp.exp(-v))


def _softplus(v):
    return jnp.maximum(v, 0.0) + jnp.log(1.0 + jnp.exp(-jnp.abs(v)))


def _erf(v):
    a = jnp.abs(v)
    t = 1.0 / (1.0 + 0.3275911 * a)
    poly = t * (0.254829592 + t * (-0.284496736 + t * (1.421413741 + t * (-1.453152027 + t * 1.061405429))))
    e = 1.0 - poly * jnp.exp(-a * a)
    return jnp.where(v < 0, -e, e)


def _gelu_and_grad(v):
    cdf = 0.5 * (1.0 + _erf(v * (1.0 / math.sqrt(2.0))))
    pdf = jnp.exp(-0.5 * v * v) * (1.0 / math.sqrt(2.0 * math.pi))
    return v * cdf, cdf + v * pdf


def matmul(a, b, *, ta=False, tb=False, out_dtype=F32, residual=None, a_split=1, b_split=1, name):
    if a_split > 1 and ta:
        assert a.shape[0] == a_split
        K, M = a.shape[1], a_split * a.shape[2]
    elif a_split > 1:
        assert a.shape[0] == a_split
        M, K = a.shape[1], a_split * a.shape[2]
    elif ta:
        K, M = a.shape
    else:
        M, K = a.shape
    if b_split > 1:
        assert not tb and b.shape[0] == b_split
        Kb, N = b.shape[1], b_split * b.shape[2]
    elif tb:
        N, Kb = b.shape
    else:
        Kb, N = b.shape
    assert K == Kb, (a.shape, b.shape, ta, tb)
    has_res = residual is not None
    tm = _pick(M // a_split if ta else M, (1024, 1408, 768, 512, 256, 128))
    tn = _pick(N // b_split, (1024, 1408, 1536, 768, 512, 256, 128))

    def vmem_bytes(tk):
        tiles = tm * tk * a.dtype.itemsize + tk * tn * b.dtype.itemsize
        outs = tm * tn * jnp.dtype(out_dtype).itemsize + (tm * tn * 4 if has_res else 0)
        return 2 * tiles + 2 * outs + (tm * tn * 4 if tk < K else 0)

    kp = K if ta else K // a_split
    tk = next((t for t in (K, 2048, 1408, 1024, 512, 256) if t <= kp and kp % t == 0 and vmem_bytes(t) <= MATMUL_VMEM_BUDGET),
              _pick(kp, (128,)))
    nk = K // tk
    if a_split > 1 and ta:
        nib = M // a_split // tm
        a_spec = pl.BlockSpec((None, tk, tm), lambda i, j, k: (i // nib, k, i % nib))
    elif a_split > 1:
        nkb = kp // tk
        a_spec = pl.BlockSpec((None, tm, tk), lambda i, j, k: (k // nkb, i, k % nkb))
    else:
        a_spec = pl.BlockSpec((tk, tm), lambda i, j, k: (k, i)) if ta else pl.BlockSpec((tm, tk), lambda i, j, k: (i, k))
    if b_split > 1:
        njb = N // b_split // tn
        b_spec = pl.BlockSpec((None, tk, tn), lambda i, j, k: (j // njb, k, j % njb))
    else:
        b_spec = pl.BlockSpec((tn, tk), lambda i, j, k: (j, k)) if tb else pl.BlockSpec((tk, tn), lambda i, j, k: (k, j))
    o_spec = pl.BlockSpec((tm, tn), lambda i, j, k: (i, j))
    ca, cb = (0 if ta else 1), (1 if tb else 0)

    def body(*refs):
        a_ref, b_ref = refs[:2]
        r_ref = refs[2] if has_res else None
        o_ref = refs[3] if has_res else refs[2]

        def finish(r):
            if has_res:
                r = r + r_ref[...]
            o_ref[...] = r.astype(out_dtype)

        def part():
            return _dot(a_ref[...].astype(BF16), b_ref[...].astype(BF16), ca, cb)

        if nk == 1:
            finish(part())
            return
        acc = refs[-1]
        k = pl.program_id(2)

        @pl.when(k == 0)
        def _():
            acc[...] = part()

        @pl.when(jnp.logical_and(k > 0, k < nk - 1))
        def _():
            acc[...] += part()

        @pl.when(k == nk - 1)
        def _():
            finish(acc[...] + part())

    in_specs = [a_spec, b_spec] + ([o_spec] if has_res else [])
    args = (a, b) + ((residual,) if has_res else ())
    return pl.pallas_call(
        body, name=name, grid=(M // tm, N // tn, nk), in_specs=in_specs, out_specs=o_spec,
        out_shape=jax.ShapeDtypeStruct((M, N), out_dtype),
        scratch_shapes=[pltpu.VMEM((tm, tn), F32)] if nk > 1 else [],
        compiler_params=_params(("parallel", "parallel", "arbitrary")),
    )(*args)


def rms_fwd(x, gain, *, name):
    S, D = x.shape
    tr = _pick(S, (512, 256, 128))

    def body(x_ref, g_ref, o_ref):
        xv = x_ref[...]
        r = lax.rsqrt(jnp.mean(xv * xv, axis=1, keepdims=True) + EPS)
        o_ref[...] = (xv * r * g_ref[...]).astype(BF16)

    return pl.pallas_call(
        body, name=name, grid=(S // tr,),
        in_specs=[pl.BlockSpec((tr, D), lambda i: (i, 0)), pl.BlockSpec((1, D), lambda i: (0, 0))],
        out_specs=pl.BlockSpec((tr, D), lambda i: (i, 0)), out_shape=jax.ShapeDtypeStruct((S, D), BF16),
        compiler_params=_params(("parallel",)),
    )(x, gain)


def rms_bwd(x, gain, dh, dres, *, name):
    S, D = x.shape
    tr = _pick(S, (512, 256, 128))

    def body(x_ref, g_ref, dh_ref, dr_ref, dx_ref, dg_ref):
        @pl.when(pl.program_id(0) == 0)
        def _():
            dg_ref[...] = jnp.zeros_like(dg_ref)

        xv = x_ref[...]
        dhv = dh_ref[...]
        r = lax.rsqrt(jnp.mean(xv * xv, axis=1, keepdims=True) + EPS)
        xhat = xv * r
        t = dhv * g_ref[...]
        dx_ref[...] = dr_ref[...] + r * (t - xhat * jnp.mean(xhat * t, axis=1, keepdims=True))
        dg_ref[...] += jnp.sum(dhv * xhat, axis=0, keepdims=True)

    row = pl.BlockSpec((tr, D), lambda i: (i, 0))
    vec = pl.BlockSpec((1, D), lambda i: (0, 0))
    return pl.pallas_call(
        body, name=name, grid=(S // tr,), in_specs=[row, vec, row, row], out_specs=[row, vec],
        out_shape=[jax.ShapeDtypeStruct((S, D), F32), jax.ShapeDtypeStruct((1, D), F32)],
        compiler_params=_params(("arbitrary",)),
    )(x, gain, dh, dres)


def ffn_up_fwd(h, w_gu_t, *, name):
    S, K = h.shape
    F = w_gu_t.shape[0] // 2
    tm = _pick(S, (512, 256, 128))
    tn = _pick(F, (1408, 1024, 768, 512, 256, 128))
    nj = F // tn

    def body(h_ref, wg_ref, wu_ref, gu_ref, a_ref):
        hv = h_ref[...]
        g = _dot(hv, wg_ref[...], 1, 1)
        u = _dot(hv, wu_ref[...], 1, 1)
        gu_ref[0] = g
        gu_ref[1] = u
        a_ref[...] = (g * _sigmoid(g) * u).astype(BF16)

    return pl.pallas_call(
        body, name=name, grid=(nj, S // tm),
        in_specs=[pl.BlockSpec((tm, K), lambda j, i: (i, 0)), pl.BlockSpec((tn, K), lambda j, i: (j, 0)),
                  pl.BlockSpec((tn, K), lambda j, i: (nj + j, 0))],
        out_specs=[pl.BlockSpec((2, tm, tn), lambda j, i: (0, i, j)), pl.BlockSpec((tm, tn), lambda j, i: (i, j))],
        out_shape=[jax.ShapeDtypeStruct((2, S, F), F32), jax.ShapeDtypeStruct((S, F), BF16)],
        compiler_params=_params(("parallel", "parallel")),
    )(h, w_gu_t, w_gu_t)


def ffn_up_bwd(dy, w_down, gu, *, name):
    S, D = dy.shape
    F = w_down.shape[0]
    tm = _pick(S, (512, 256, 128))
    tn = _pick(F, (1408, 1024, 768, 512, 256, 128))

    def body(dy_ref, wd_ref, gu_ref, o_ref):
        da = _dot(dy_ref[...].astype(BF16), wd_ref[...], 1, 1)
        g = gu_ref[0]
        u = gu_ref[1]
        s = _sigmoid(g)
        o_ref[0] = (da * u * (s * (1.0 + g * (1.0 - s)))).astype(BF16)
        o_ref[1] = (da * g * s).astype(BF16)

    pair = pl.BlockSpec((2, tm, tn), lambda j, i: (0, i, j))
    return pl.pallas_call(
        body, name=name, grid=(F // tn, S // tm),
        in_specs=[pl.BlockSpec((tm, D), lambda j, i: (i, 0)), pl.BlockSpec((tn, D), lambda j, i: (j, 0)), pair],
        out_specs=pair, out_shape=jax.ShapeDtypeStruct((2, S, F), BF16),
        compiler_params=_params(("parallel", "parallel")),
    )(dy, w_down, gu)


def loss_head(y, target, *, name):
    S, D = y.shape
    tr = _pick(S, (512, 256, 128))

    def body(y_ref, t_ref, dy_ref, l_ref):
        @pl.when(pl.program_id(0) == 0)
        def _():
            l_ref[...] = jnp.zeros_like(l_ref)

        err = y_ref[...] - t_ref[...]
        dy_ref[...] = err * (1.0 / D)
        l_ref[...] += jnp.sum(0.5 * jnp.mean(err * err, axis=1, keepdims=True), axis=0, keepdims=True)

    row = pl.BlockSpec((tr, D), lambda i: (i, 0))
    one = pl.BlockSpec((1, 1), lambda i: (0, 0))
    dy, l = pl.pallas_call(
        body, name=name, grid=(S // tr,), in_specs=[row, row], out_specs=[row, one],
        out_shape=[jax.ShapeDtypeStruct((S, D), F32), jax.ShapeDtypeStruct((1, 1), F32)],
        compiler_params=_params(("arbitrary",)),
    )(y, target)
    return dy, l


def adamw(w, g, m, v, *, name):
    R, C = w.shape[-2:]
    tr = _pick(R, (512, 256, 128, 64, 32, 16, 8))

    def body(w_ref, g_ref, m_ref, v_ref, d_ref, mo_ref, vo_ref):
        gv = g_ref[...]
        mn = ADAM_B1 * m_ref[...] + (1.0 - ADAM_B1) * gv
        vn = ADAM_B2 * v_ref[...] + (1.0 - ADAM_B2) * jnp.square(gv)
        m_hat = mn / (1.0 - ADAM_B1 ** ADAM_STEP)
        v_hat = vn / (1.0 - ADAM_B2 ** ADAM_STEP)
        d_ref[...] = -ADAM_LR * (m_hat / (jnp.sqrt(v_hat) + ADAM_EPS) + ADAM_WD * w_ref[...])
        mo_ref[...] = mn
        vo_ref[...] = vn

    tc = C if tr < R or C % LANES else _pick(C, (256, 128))
    if w.ndim == 3:
        grid = (w.shape[0], R // tr, C // tc)
        blk = pl.BlockSpec((None, tr, tc), lambda l, i, j: (l, i, j))
    else:
        grid = (R // tr, C // tc)
        blk = pl.BlockSpec((tr, tc), lambda i, j: (i, j))
    sds = jax.ShapeDtypeStruct(w.shape, F32)
    return pl.pallas_call(
        body, name=name, grid=grid, in_specs=[blk] * 4, out_specs=[blk] * 3, out_shape=[sds] * 3,
        compiler_params=_params(("parallel",) * len(grid)),
    )(w, g, m, v)


def _tri(n, kind):
    r = lax.broadcasted_iota(jnp.int32, (n, n), 0)
    c = lax.broadcasted_iota(jnp.int32, (n, n), 1)
    if kind == "row_gt_col":
        return (r > c).astype(BF16)
    if kind == "row_ge_col":
        return (r >= c).astype(BF16)
    if kind == "row_le_col":
        return (r <= c).astype(BF16)
    raise ValueError(kind)


def _sb_tile(qs, kj, r_carry, u_strict, masked):
    z = _dot(qs, kj, 1, 1)
    lb = jnp.minimum(z, 0.0) - jnp.log(1.0 + jnp.exp(-jnp.abs(z)))
    l1m = lb - z
    keep = None
    if masked:
        tq, tk = z.shape
        keep = lax.broadcasted_iota(jnp.int32, (tq, tk), 1) < lax.broadcasted_iota(jnp.int32, (tq, tk), 0)
        l1m = jnp.where(keep, l1m, 0.0)
    w = jnp.exp(lb + _dot(l1m.astype(BF16), u_strict) + r_carry)
    if masked:
        w = jnp.where(keep, w, 0.0)
    return lb, l1m, w, keep


def _sb_prep(T, nb, hd, refs_in, gains, scratch):
    q_scale = 1.0 / math.sqrt(hd)
    assert math.log2(q_scale) == round(math.log2(q_scale))

    def prep(i, _):
        rows = pl.ds(pl.multiple_of(i * T, T), T)
        for hh in range(2):
            sl = slice(hd * hh, hd * hh + hd)
            for n, (src, dst) in enumerate(zip(refs_in, scratch)):
                v = src[rows, sl]
                if n < 2:
                    v = v * lax.rsqrt(jnp.mean(v * v, axis=1, keepdims=True) + EPS) * gains[n][...]
                if n == 0:
                    v = v * q_scale
                dst[hh, rows, :] = v.astype(BF16)
        return 0

    lax.fori_loop(0, nb, prep, 0)


def _sb_chains(m, T, nq):
    rows = [pl.ds(pl.multiple_of((nq * m + qb) * T, T), T) for qb in range(nq)]
    return rows, [(hh, qb) for qb in range(nq) for hh in range(2)]


def _sb_sweep(tile, carry, chains, m, nq):
    for kk in reversed(range(nq)):
        carry = tile(nq * m + kk, carry, [(ch, ch[1] == kk) for ch in chains if ch[1] >= kk])
    return lax.fori_loop(0, nq * m, lambda jj, c: tile(nq * m - 1 - jj, c, [(ch, False) for ch in chains]), carry)


def sb_attn_fwd(qkv, q_gain, k_gain, *, name, gather=None):
    S, D3 = qkv.shape
    D = D3 // 3
    npairs = D // LANES
    hd = SB_HEAD_DIM
    T = min(SB_TILE, S)
    nb = S // T
    nq = SB_FWD_QUERY_BLOCKS
    assert nb % nq == 0

    def body(*refs):
        if gather is None:
            q_ref, k_ref, v_ref, qg_ref, kg_ref, us_ref, o_ref, qn_s, kn_s, vb_s = refs
        else:
            ng = len(gather)
            q_ref, k_ref, v_ref, qg_ref, kg_ref, us_ref = refs[:6]
            o_ref = refs[6 + ng]
            qn_s, kn_s, vb_s = refs[7 + 2 * ng:10 + 2 * ng]
            comm = (refs[6:6 + ng], refs[7 + ng:7 + 2 * ng], refs[10 + 2 * ng:])
            step = pl.program_id(0)
            pl.when(step == 0)(lambda: _ag_phase(*comm, "start"))
            pl.when(step == npairs - 1)(lambda: _ag_phase(*comm, "forward"))
        us = us_ref[...]
        _sb_prep(T, nb, hd, (q_ref, k_ref, v_ref), (qg_ref, kg_ref), (qn_s, kn_s, vb_s))

        def superblock(m, _):
            rows_q, chains = _sb_chains(m, T, nq)
            qs = {ch: qn_s[ch[0], rows_q[ch[1]], :] for ch in chains}

            def tile(j, carry, which):
                rows_j = pl.ds(pl.multiple_of(j * T, T), T)
                new = dict(carry)
                for ch, masked in which:
                    acc, rc = carry[ch]
                    _, l1m, w, _ = _sb_tile(qs[ch], kn_s[ch[0], rows_j, :], rc, us, masked)
                    new[ch] = (acc + _dot(w.astype(BF16), vb_s[ch[0], rows_j, :]),
                               rc + jnp.sum(l1m, axis=1, keepdims=True))
                return new

            carry = {ch: (jnp.zeros((T, hd), F32), jnp.zeros((T, 1), F32)) for ch in chains}
            carry = _sb_sweep(tile, carry, chains, m, nq)
            for qb in range(nq):
                o_ref[rows_q[qb], :] = jnp.concatenate([carry[(0, qb)][0], carry[(1, qb)][0]], axis=1)
            return 0

        lax.fori_loop(0, nb // nq, superblock, 0)
        if gather is not None:
            pl.when(step == npairs - 1)(lambda: _ag_phase(*comm, "finish"))

    col = lambda off: pl.BlockSpec((S, LANES), lambda p, off=off: (0, off + p))
    gain = pl.BlockSpec((1, hd), lambda p: (0, 0))
    in_specs = [col(0), col(npairs), col(2 * npairs), gain, gain, pl.BlockSpec((T, T), lambda p: (0, 0))]
    out_specs = [pl.BlockSpec((S, LANES), lambda p: (0, p))]
    out_shape = [jax.ShapeDtypeStruct((S, D), F32)]
    scratch = [pltpu.VMEM((2, S, hd), BF16)] * 3
    args = [qkv, qkv, qkv, q_gain, k_gain, _tri(T, "row_gt_col")]
    if gather is not None:
        in_specs += [HBM_SPEC] * len(gather)
        out_specs += [HBM_SPEC] * len(gather)
        out_shape += [jax.ShapeDtypeStruct((N_DEV,) + s.shape, s.dtype) for s in gather]
        scratch += _copy_semaphores(len(gather))
        args += list(gather)
    out = pl.pallas_call(
        body, name=name, grid=(npairs,), in_specs=in_specs, out_specs=out_specs, out_shape=out_shape,
        scratch_shapes=scratch, compiler_params=_params(("arbitrary",)),
    )(*args)
    return out[0] if gather is None else (out[0], list(out[1:]))


def sb_attn_bwd(qkv, o, do, q_gain, k_gain, *, name, scatter=None):
    S, D3 = qkv.shape
    D = D3 // 3
    npairs = D // LANES
    hd = SB_HEAD_DIM
    T = min(SB_TILE, S)
    nb = S // T
    nq = SB_BWD_QUERY_BLOCKS
    assert nb % nq == 0
    scale = 1.0 / math.sqrt(hd)

    def body(*refs):
        if scatter is None:
            (q_ref, k_ref, v_ref, o_ref, do_ref, qg_ref, kg_ref, us_ref,
             dqkv_ref, dg_ref, qn_s, kn_s, vb_s, dob_s, acc_s) = refs
        else:
            ns = len(scatter)
            q_ref, k_ref, v_ref, o_ref, do_ref, qg_ref, kg_ref, us_ref = refs[:8]
            rs_in = refs[8:8 + ns]
            dqkv_ref, dg_ref = refs[8 + ns:10 + ns]
            rs_out = refs[10 + ns:10 + 2 * ns]
            qn_s, kn_s, vb_s, dob_s, acc_s = refs[10 + 2 * ns:15 + 2 * ns]
            rs_sems = refs[15 + 2 * ns:]
            pl.when(pl.program_id(0) == 0)(lambda: _rs_phase(rs_in, rs_out, rs_sems, "start"))
        dq_ref, dk_ref, dv_ref = acc_s.at[0], acc_s.at[1], acc_s.at[2]

        @pl.when(pl.program_id(0) == 0)
        def _():
            dg_ref[...] = jnp.zeros_like(dg_ref)

        us = us_ref[...]
        u_prefix = (1.0 - us.astype(F32)).astype(BF16)
        _sb_prep(T, nb, hd, (q_ref, k_ref, v_ref, do_ref), (qg_ref, kg_ref), (qn_s, kn_s, vb_s, dob_s))
        dk_ref[...] = jnp.zeros_like(dk_ref)
        dv_ref[...] = jnp.zeros_like(dv_ref)

        def superblock(m, _):
            rows_q, chains = _sb_chains(m, T, nq)
            qs = {ch: qn_s[ch[0], rows_q[ch[1]], :] for ch in chains}
            doi ={ch: dob_s[ch[0], rows_q[ch[1]], :] for ch in chains}
            dt_total = {ch: jnp.sum(doi[ch].astype(F32) * o_ref[rows_q[ch[1]], hd * ch[0]:hd * ch[0] + hd],
                                    axis=1, keepdims=True) for ch in chains}

            def tile(j, carry, which):
                rows_j = pl.ds(pl.multiple_of(j * T, T), T)
                new = dict(carry)
                dk_part, dv_part = {}, {}
                for ch, masked in which:
                    hh = ch[0]
                    dq_acc, rc, gc = carry[ch]
                    kj = kn_s[hh, rows_j, :]
                    lb, l1m, w, keep = _sb_tile(qs[ch], kj, rc, us, masked)
                    wb = w.astype(BF16)
                    g = _dot(doi[ch], vb_s[hh, rows_j, :], 1, 1) * wb.astype(F32)
                    g_row = jnp.sum(g, axis=1, keepdims=True)
                    g_upto = (dt_total[ch] - gc - g_row) + _dot(g.astype(BF16), u_prefix)
                    dz = g - g_upto * jnp.exp(lb)
                    if masked:
                        dz = jnp.where(keep, dz, 0.0)
                    dzb = dz.astype(BF16)
                    dv_part[hh] = dv_part.get(hh, 0.0) + _dot(wb, doi[ch], 0, 0)
                    dk_part[hh] = dk_part.get(hh, 0.0) + _dot(dzb, qs[ch], 0, 0)
                    new[ch] = (dq_acc + _dot(dzb, kj), rc + jnp.sum(l1m, axis=1, keepdims=True),
                               gc + g_row)
                dv_ref[rows_j, :] += jnp.concatenate([dv_part[0], dv_part[1]], axis=1)
                dk_ref[rows_j, :] += jnp.concatenate([dk_part[0], dk_part[1]], axis=1)
                return new

            zero1 = jnp.zeros((T, 1), F32)
            carry = {ch: (jnp.zeros((T, hd), F32), zero1, zero1) for ch in chains}
            carry = _sb_sweep(tile, carry, chains, m, nq)
            for qb in range(nq):
                dq_ref[rows_q[qb], :] = jnp.concatenate([carry[(0, qb)][0], carry[(1, qb)][0]], axis=1) * scale
            return 0

        lax.fori_loop(0, nb // nq, superblock, 0)

        def finish(i, carry):
            rows = pl.ds(pl.multiple_of(i * T, T), T)
            new = []
            for hh in range(2):
                sl = slice(hd * hh, hd * hh + hd)
                outs = []
                for raw_ref, gain_ref, dn in ((q_ref, qg_ref, dq_ref[rows, sl]), (k_ref, kg_ref, dk_ref[rows, sl])):
                    raw = raw_ref[rows, sl]
                    r = lax.rsqrt(jnp.mean(raw * raw, axis=1, keepdims=True) + EPS)
                    hat = raw * r
                    t = dn * gain_ref[...]
                    outs.append((r * (t - hat * jnp.mean(hat * t, axis=1, keepdims=True)),
                                 jnp.sum(dn * hat, axis=0, keepdims=True)))
                dqkv_ref[0, rows, sl] = outs[0][0].astype(BF16)
                dqkv_ref[1, rows, sl] = outs[1][0].astype(BF16)
                new.append((carry[hh][0] + outs[0][1], carry[hh][1] + outs[1][1]))
            dqkv_ref[2, rows, :] = dv_ref[rows, :].astype(BF16)
            return tuple(new)

        zg = (jnp.zeros((1, hd), F32), jnp.zeros((1, hd), F32))
        tot = lax.fori_loop(0, nb, finish, (zg, zg))
        dg_ref[0:1, 0:hd] += tot[0][0] + tot[1][0]
        dg_ref[1:2, 0:hd] += tot[0][1] + tot[1][1]
        if scatter is not None:
            pl.when(pl.program_id(0) == npairs - 1)(lambda: _rs_phase(rs_in, rs_out, rs_sems, "finish"))

    col = lambda off: pl.BlockSpec((S, LANES), lambda p, off=off: (0, off + p))
    gain = pl.BlockSpec((1, hd), lambda p: (0, 0))
    tri = pl.BlockSpec((T, T), lambda p: (0, 0))
    pair = pl.BlockSpec((S, LANES), lambda p: (0, p))
    in_specs = [col(0), col(npairs), col(2 * npairs), pair, pair, gain, gain, tri]
    out_specs = [pl.BlockSpec((3, S, LANES), lambda p: (0, 0, p)), pl.BlockSpec((8, LANES), lambda p: (0, 0))]
    out_shape = [jax.ShapeDtypeStruct((3, S, D), BF16), jax.ShapeDtypeStruct((8, LANES), F32)]
    scratch = [pltpu.VMEM((2, S, hd), BF16)] * 4 + [pltpu.VMEM((3, S, LANES), F32)]
    args = [qkv, qkv, qkv, o, do, q_gain, k_gain, _tri(T, "row_gt_col")]
    if scatter is not None:
        in_specs += [HBM_SPEC] * len(scatter)
        out_specs += [HBM_SPEC] * len(scatter)
        out_shape += [jax.ShapeDtypeStruct(g.shape, g.dtype) for g in scatter]
        scratch += _copy_semaphores(len(scatter))
        args += list(scatter)
    out = pl.pallas_call(
        body, name=name, grid=(npairs,), in_specs=in_specs, out_specs=out_specs, out_shape=out_shape,
        scratch_shapes=scratch, compiler_params=_params(("arbitrary",)),
    )(*args)
    res = (out[0], out[1][0:1, :hd], out[1][1:2, :hd])
    return res if scatter is None else res + (list(out[2:]),)


def gmlp_fwd(zzpre, b_in, v_gain, wc, bsf, *, name):
    S, H2 = zzpre.shape
    H = H2 // 2
    G, T, _ = wc.shape
    gd = H // G

    def body(z_ref, b_ref, vg_ref, wc_ref, bs_ref, p_ref):
        zz, _ = _gelu_and_grad(z_ref[...] + b_ref[...])
        u = zz[:, :H]
        v = zz[:, H:]
        vn = v * lax.rsqrt(jnp.mean(v * v, axis=1, keepdims=True) + EPS) * vg_ref[...]
        for g in range(G):
            gs = slice(g * gd, (g + 1) * gd)
            mixed = _dot(wc_ref[g], vn[:, gs].astype(BF16)) + bs_ref[g]
            p_ref[:, gs] = (u[:, gs] * mixed).astype(BF16)

    full3 = lambda shp: pl.BlockSpec(shp, lambda c: (0, 0, 0))
    return pl.pallas_call(
        body, name=name, grid=(S // T,),
        in_specs=[pl.BlockSpec((T, H2), lambda c: (c, 0)), pl.BlockSpec((1, H2), lambda c: (0, 0)),
                  pl.BlockSpec((1, H), lambda c: (0, 0)), full3((G, T, T)), full3((G, T, gd))],
        out_specs=pl.BlockSpec((T, H), lambda c: (c, 0)), out_shape=jax.ShapeDtypeStruct((S, H), BF16),
        compiler_params=_params(("parallel",)),
    )(zzpre, b_in, v_gain, wc, bsf)


def gmlp_bwd(zzpre, b_in, v_gain, wc, bsf, dp, *, name):
    S, H2 = zzpre.shape
    H = H2 // 2
    G, T, _ = wc.shape
    gd = H // G
    assert G <= LANES

    def body(z_ref, b_ref, vg_ref, wc_ref, bs_ref, dp_ref, dzz_ref, db_ref, dvg_ref, dws_ref, dbs_ref):
        @pl.when(pl.program_id(0) == 0)
        def _():
            db_ref[...] = jnp.zeros_like(db_ref)
            dvg_ref[...] = jnp.zeros_like(dvg_ref)
            dws_ref[...] = jnp.zeros_like(dws_ref)
            dbs_ref[...] = jnp.zeros_like(dbs_ref)

        zz, gp = _gelu_and_grad(z_ref[...] + b_ref[...])
        u = zz[:, :H]
        v = zz[:, H:]
        r = lax.rsqrt(jnp.mean(v * v, axis=1, keepdims=True) + EPS)
        vhat = v * r
        vg = vg_ref[...]
        vn = vhat * vg
        dpv = dp_ref[...]
        tril = lax.broadcasted_iota(jnp.int32, (T, T), 1) <= lax.broadcasted_iota(jnp.int32, (T, T), 0)
        lane = lax.broadcasted_iota(jnp.int32, (T, LANES), 1)
        dbs = jnp.zeros((T, LANES), F32)
        du_parts, dvn_parts = [], []
        for g in range(G):
            gs = slice(g * gd, (g + 1) * gd)
            vng = vn[:, gs].astype(BF16)
            wcg = wc_ref[g]
            mixed = _dot(wcg, vng) + bs_ref[g]
            dpg = dpv[:, gs]
            du_parts.append(dpg * mixed)
            dmx = dpg * u[:, gs]
            dmxb = dmx.astype(BF16)
            dvn_parts.append(_dot(wcg, dmxb, 0, 0))
            dws_ref[g] += jnp.where(tril, _dot(dmxb, vng, 1, 1), 0.0)
            dbs = dbs + jnp.where(lane == g, jnp.sum(dmx, axis=1, keepdims=True), 0.0)
        dbs_ref[...] += dbs
        du = jnp.concatenate(du_parts, axis=1)
        dvn = jnp.concatenate(dvn_parts, axis=1)
        dvg_ref[...] += jnp.sum(dvn * vhat, axis=0, keepdims=True)
        t = dvn * vg
        dv = r * (t - vhat * jnp.mean(vhat * t, axis=1, keepdims=True))
        dzu = du * gp[:, :H]
        dzv = dv * gp[:, H:]
        dzz_ref[:, :H] = dzu.astype(BF16)
        dzz_ref[:, H:] = dzv.astype(BF16)
        db_ref[:, :H] += jnp.sum(dzu, axis=0, keepdims=True)
        db_ref[:, H:] += jnp.sum(dzv, axis=0, keepdims=True)

    full3 = lambda shp: pl.BlockSpec(shp, lambda c: (0, 0, 0))
    vec = lambda n: pl.BlockSpec((1, n), lambda c: (0, 0))
    return pl.pallas_call(
        body, name=name, grid=(S // T,),
        in_specs=[pl.BlockSpec((T, H2), lambda c: (c, 0)), vec(H2), vec(H), full3((G, T, T)), full3((G, T, gd)),
                  pl.BlockSpec((T, H), lambda c: (c, 0))],
        out_specs=[pl.BlockSpec((T, H2), lambda c: (c, 0)), vec(H2), vec(H), full3((G, T, T)),
                   pl.BlockSpec((T, LANES), lambda c: (0, 0))],
        out_shape=[jax.ShapeDtypeStruct((S, H2), BF16), jax.ShapeDtypeStruct((1, H2), F32),
                   jax.ShapeDtypeStruct((1, H), F32), jax.ShapeDtypeStruct((G, T, T), F32),
                   jax.ShapeDtypeStruct((T, LANES), F32)],
        compiler_params=_params(("arbitrary",)),
    )(zzpre, b_in, v_gain, wc, bsf, dp)


def _shift_rows(v, k, n_rows):
    if k == 0:
        return v
    rolled = pltpu.roll(v, k % n_rows, 0)
    row = lax.broadcasted_iota(jnp.int32, v.shape, 0)
    keep = (row >= k) if k > 0 else (row < n_rows + k)
    return jnp.where(keep, rolled, 0.0)


def conv_fwd(zx, conv_w, conv_b, col0, *, name):
    S = zx.shape[0]
    C = conv_w.shape[1]
    tc = _pick(C, (256, 128))
    off = col0 // tc
    assert col0 % tc == 0

    def body(x_ref, w_ref, b_ref, o_ref):
        xv = x_ref[...]
        acc = b_ref[...] + w_ref[SSM_CONV - 1:SSM_CONV, :] * xv
        for k in range(SSM_CONV - 1):
            acc = acc + w_ref[k:k + 1, :] * _shift_rows(xv, SSM_CONV - 1 - k, S)
        o_ref[...] = acc * _sigmoid(acc)

    return pl.pallas_call(
        body, name=name, grid=(C // tc,),
        in_specs=[pl.BlockSpec((S, tc), lambda j: (0, off + j)), pl.BlockSpec((SSM_CONV, tc), lambda j: (0, j)),
                  pl.BlockSpec((1, tc), lambda j: (0, j))],
        out_specs=pl.BlockSpec((S, tc), lambda j: (0, j)), out_shape=jax.ShapeDtypeStruct((S, C), F32),
        compiler_params=_params(("parallel",)),
    )(zx, conv_w, conv_b)


def conv_bwd(zx, conv_w, conv_b, col0, douts, *, name):
    S = zx.shape[0]
    C = conv_w.shape[1]
    tc = LANES
    off = col0 // tc
    counts = [d.shape[1] // tc for d in douts]
    starts = [sum(counts[:p]) for p in range(len(douts))]
    assert sum(counts) * tc == C and all(d.shape[1] % tc == 0 for d in douts)

    def body(x_ref, w_ref, b_ref, *rest):
        do_refs, (dx_ref, dw_ref, db_ref) = rest[:len(douts)], rest[len(douts):]
        j = pl.program_id(0)
        dov = do_refs[-1][...]
        for p in reversed(range(len(douts) - 1)):
            dov = jnp.where(j < starts[p + 1], do_refs[p][...], dov)
        xv = x_ref[...]
        shifted = [_shift_rows(xv, SSM_CONV - 1 - k, S) for k in range(SSM_CONV)]
        acc = b_ref[...]
        for k in range(SSM_CONV):
            acc = acc + w_ref[k:k + 1, :] * shifted[k]
        s = _sigmoid(acc)
        dacc = dov * (s * (1.0 + acc * (1.0 - s)))
        db_ref[...] = jnp.sum(dacc, axis=0, keepdims=True)
        dx = jnp.zeros_like(xv)
        for k in range(SSM_CONV):
            dw_ref[k:k + 1, :] = jnp.sum(dacc * shifted[k], axis=0, keepdims=True)
            dx = dx + w_ref[k:k + 1, :] * _shift_rows(dacc, -(SSM_CONV - 1 - k), S)
        dx_ref[...] = dx.astype(BF16)

    slab = pl.BlockSpec((S, tc), lambda j: (0, j))
    piece_specs = [pl.BlockSpec((S, tc), lambda j, a=starts[p], n=counts[p]: (0, jnp.clip(j - a, 0, n - 1)))
                   for p in range(len(douts))]
    return pl.pallas_call(
        body, name=name, grid=(C // tc,),
        in_specs=[pl.BlockSpec((S, tc), lambda j: (0, off + j)), pl.BlockSpec((SSM_CONV, tc), lambda j: (0, j)),
                  pl.BlockSpec((1, tc), lambda j: (0, j))] + piece_specs,
        out_specs=[slab, pl.BlockSpec((SSM_CONV, tc), lambda j: (0, j)), pl.BlockSpec((1, tc), lambda j: (0, j))],
        out_shape=[jax.ShapeDtypeStruct((S, C), BF16), jax.ShapeDtypeStruct((SSM_CONV, C), F32),
                   jax.ShapeDtypeStruct((1, C), F32)],
        compiler_params=_params(("arbitrary",)),
    )(zx, conv_w, conv_b, *douts)


def _ssd_chunk_terms(dtraw, bias, a_log, tl):
    dt = _softplus(dtraw + bias)
    a_neg = -jnp.exp(a_log)
    ac = _dot_exact_left(tl, dt * a_neg)
    ac_last = ac[ac.shape[0] - 1:, :]
    return dt, a_neg, ac, ac.T, jnp.exp(ac), jnp.exp(ac_last - ac), jnp.exp(ac_last)


def _ssd_specs(S, L, G, hpg, pd, inner, gp):
    gw = hpg * pd
    n = SSM_STATE
    xb = inner // n
    assert G % gp == 0 and xb % gp == 0 and (xb + G) % gp == 0

    def mk(cidx):
        return dict(
            x=pl.BlockSpec((L, gp * gw), lambda g, c: (cidx(c), g)),
            b=pl.BlockSpec((L, gp * n), lambda g, c: (cidx(c), xb // gp + g)),
            c=pl.BlockSpec((L, gp * n), lambda g, c: (cidx(c), (xb + G) // gp + g)),
            z=pl.BlockSpec((L, gp * gw), lambda g, c: (cidx(c), g)),
            dt=pl.BlockSpec((L, gp * LANES), lambda g, c: (cidx(c), g)),
            gvec=pl.BlockSpec((gp, 1, LANES), lambda g, c: (g, 0, 0)),
            chan=pl.BlockSpec((1, gp * gw), lambda g, c: (0, g)),
            tri=pl.BlockSpec((L, L), lambda g, c: (0, 0)),
            hp=pl.BlockSpec((gp, 1, gw, n), lambda g, c: (g, cidx(c), 0, 0)),
            bc=pl.BlockSpec((L, gp * n), lambda g, c: (cidx(c), g)),
        )
    return mk


def _ssd_group_views(refs, kinds, gg, gw):
    n = SSM_STATE
    width = dict(x=gw, z=gw, chan=gw, b=n, c=n, bc=n, dt=LANES)
    out = []
    for ref, kind in zip(refs, kinds):
        if kind in width:
            out.append(ref.at[:, gg * width[kind]:(gg + 1) * width[kind]])
        elif kind in ("gvec", "hp"):
            out.append(ref.at[gg:gg + 1])
        elif kind == "state":
            out.append(ref.at[gg * gw:(gg + 1) * gw])
        else:
            out.append(ref)
    return out


def ssd_fwd(xbc, zx, dtg, bias_g, alog_g, d_chan, ngain, L, G, *, name):
    S = xbc.shape[0]
    n = SSM_STATE
    inner = xbc.shape[1] - 2 * G * n
    gw = inner // G
    pd = SB_HEAD_DIM
    hpg = gw // pd
    nc = S // L
    gp = SSD_FWD_GROUPS_PER_STEP
    sp = _ssd_specs(S, L, G, hpg, pd, inner, gp)(lambda c: c)

    kinds = ("x", "b", "c", "z", "dt", "gvec", "gvec", "chan", "chan", "tri", "x", "x", "hp", "state")

    def body(*refs):
        @pl.when(pl.program_id(1) == 0)
        def _():
            refs[-1][...] = jnp.zeros_like(refs[-1])

        for gg in range(gp):
            group_body(*_ssd_group_views(refs, kinds, gg, gw))

    def group_body(x_ref, b_ref, c_ref, z_ref, dt_ref, bias_ref, alog_ref, d_ref, ng_ref, tl_ref,
                   yn_ref, y_ref, hp_ref, state):
        dt, _, ac, act, ea, dte, cd = _ssd_chunk_terms(dt_ref[...], bias_ref[0], alog_ref[0], tl_ref[...])
        xv = x_ref[...]
        bm = b_ref[...].astype(BF16)
        cm = c_ref[...].astype(BF16)
        cb = _dot(cm, bm, 1, 1)
        tril = lax.broadcasted_iota(jnp.int32, (L, L), 1) <= lax.broadcasted_iota(jnp.int32, (L, L), 0)
        hp_ref[0, 0] = state[...]
        for r in range(hpg):
            ps = slice(r * pd, (r + 1) * pd)
            xr = xv[:, ps]
            xdt = xr * dt[:, r:r + 1]
            lm = jnp.exp(jnp.where(tril, ac[:, r:r + 1] - act[r:r + 1, :], -jnp.inf))
            hprev = state[ps, :]
            y = _dot((cb * lm).astype(BF16), xdt.astype(BF16))
            y = y + _dot(cm, hprev.astype(BF16), 1, 1) * ea[:, r:r + 1]
            y_ref[:, ps] = y + xr * d_ref[:, ps]
            st = _dot((xdt * dte[:, r:r + 1]).astype(BF16), bm, 0, 0)
            state[ps, :] = hprev * cd[:, r:r + 1] + st
        yfull = y_ref[...]
        zg = z_ref[...]
        yg = yfull * (zg * _sigmoid(zg))
        yn_ref[...] = (yg * lax.rsqrt(jnp.mean(yg * yg, axis=1, keepdims=True) + EPS) * ng_ref[...]).astype(BF16)

    return pl.pallas_call(
        body, name=name, grid=(G // gp, nc),
        in_specs=[sp["x"], sp["b"], sp["c"], sp["z"], sp["dt"], sp["gvec"], sp["gvec"], sp["chan"], sp["chan"], sp["tri"]],
        out_specs=[sp["x"], sp["x"], sp["hp"]],
        out_shape=[jax.ShapeDtypeStruct((S, inner), BF16), jax.ShapeDtypeStruct((S, inner), F32),
                   jax.ShapeDtypeStruct((G, nc, gw, n), F32)],
        scratch_shapes=[pltpu.VMEM((gp * gw, n), F32)],
        compiler_params=_params(("arbitrary", "arbitrary")),
    )(xbc, xbc, xbc, zx, dtg, bias_g, alog_g, d_chan, ngain, _tri(L, "row_ge_col"))


def ssd_bwd(xbc, zx, dtg, bias_g, alog_g, d_chan, ngain, yfull, hp, dyn, L, G, *, name):
    S = xbc.shape[0]
    n = SSM_STATE
    inner = xbc.shape[1] - 2 * G * n
    gw = inner // G
    pd = SB_HEAD_DIM
    hpg = gw // pd
    nc = S // L
    gp = SSD_BWD_GROUPS_PER_STEP
    sp = _ssd_specs(S, L, G, hpg, pd, inner, gp)(lambda c: nc - 1 - c)

    kinds = ("x", "b", "c", "z", "dt", "gvec", "gvec", "chan", "chan", "tri", "tri", "x", "hp", "x",
             "x", "x", "bc", "bc", "dt", "gvec", "gvec", "gvec", "chan", "state")

    def body(*refs):
        @pl.when(pl.program_id(1) == 0)
        def _():
            for acc in refs[-5:]:
                acc[...] = jnp.zeros_like(acc)

        for gg in range(gp):
            group_body(*_ssd_group_views(refs, kinds, gg, gw))

    def group_body(x_ref, b_ref, c_ref, z_ref, dt_ref, bias_ref, alog_ref, d_ref, ng_ref, tl_ref, tu_ref,
                   yf_ref, hp_ref, dyn_ref,
                   dz_ref, dx_ref, db_ref, dc_ref, ddt_ref, dbias_ref, dalog_ref, dd_ref, dng_ref, dstate):

        dtraw = dt_ref[...]
        dt, a_neg, ac, act, ea, dte, cd = _ssd_chunk_terms(dtraw, bias_ref[0], alog_ref[0], tl_ref[...])
        xv = x_ref[...]
        bm = b_ref[...].astype(BF16)
        cm = c_ref[...].astype(BF16)
        cb = _dot(cm, bm, 1, 1)
        tril = lax.broadcasted_iota(jnp.int32, (L, L), 1) <= lax.broadcasted_iota(jnp.int32, (L, L), 0)
        lane = lax.broadcasted_iota(jnp.int32, (L, LANES), 1)
        lane1 = lax.broadcasted_iota(jnp.int32, (1, LANES), 1)

        yfull = yf_ref[...]
        zg = z_ref[...]
        sg = _sigmoid(zg)
        gate = zg * sg
        yg = yfull * gate
        rr = lax.rsqrt(jnp.mean(yg * yg, axis=1, keepdims=True) + EPS)
        yhat = yg * rr
        dynv = dyn_ref[...]
        dng_ref[...] += jnp.sum(dynv * yhat, axis=0, keepdims=True)
        t = dynv * ng_ref[...]
        dyg = rr * (t - yhat * jnp.mean(yhat * t, axis=1, keepdims=True))
        dy = dyg * gate
        dz_ref[...] = (dyg * yfull * (sg * (1.0 + zg * (1.0 - sg)))).astype(BF16)

        dcb = jnp.zeros((L, L), F32)
        dc_acc = jnp.zeros((L, n), F32)
        db_acc = jnp.zeros((L, n), F32)
        dac = jnp.zeros((L, LANES), F32)
        xdx = jnp.zeros((L, LANES), F32)
        tail = jnp.zeros((1, LANES), F32)
        dskip = jnp.zeros((1, LANES), F32)
        ones_l = jnp.ones((L, LANES), BF16)
        for r in range(hpg):
            ps = slice(r * pd, (r + 1) * pd)
            xr = xv[:, ps]
            dyr = dy[:, ps]
            dtr = dt[:, r:r + 1]
            dter = dte[:, r:r + 1]
            cdr = cd[:, r:r + 1]
            xdt = xr * dtr
            xdtb = xdt.astype(BF16)
            dyrb = dyr.astype(BF16)
            lm = jnp.exp(jnp.where(tril, ac[:, r:r + 1] - act[r:r + 1, :], -jnp.inf))
            m32 = cb * lm
            mb = m32.astype(BF16)
            hprev = hp_ref[0, 0, ps, :]
            hpb = hprev.astype(BF16)
            dhn = dstate[ps, :]
            dhnb = dhn.astype(BF16)
            ear = ea[:, r:r + 1]
            gy = (dyr * ear).astype(BF16)
            dc_acc = dc_acc + _dot(gy, hpb)
            dstate[ps, :] = _dot(gy, cm, 0, 0) + dhn * cdr
            bdh = _dot(bm, dhnb, 1, 1)
            db_acc = db_acc + _dot((xdt * dter).astype(BF16), dhnb)
            dm = _dot(dyrb, xdtb, 1, 1)
            dxdt = bdh * dter + _dot(mb, dyrb, 0, 0)
            dcb = dcb + dm * lm
            wmat = dm * m32
            whi = wmat.astype(BF16)
            wlo = (wmat - whi.astype(F32)).astype(BF16)
            col_w = _dot(whi, ones_l, 0, 0) + _dot(wlo, ones_l, 0, 0)
            t_end = xdt * bdh * dter
            e_r = jnp.sum(wmat, axis=1, keepdims=True) \
                + jnp.sum(dyr * _dot(cm, hpb, 1, 1) * ear - t_end, axis=1, keepdims=True)
            c_r = cdr * _sum_all(dhn * hprev) + _sum_all(t_end)
            dac = dac + jnp.where(lane == r, e_r - col_w, 0.0)
            xdx = xdx + jnp.where(lane == r, jnp.sum(dxdt * xr, axis=1, keepdims=True), 0.0)
            tail = tail + jnp.where(lane1 == r, c_r, 0.0)
            dskip = dskip + jnp.where(lane1 == r, _sum_all(dyr * xr), 0.0)
            dx_ref[:, ps] = dxdt * dtr + dyr * d_ref[:, ps]
        dcbb = dcb.astype(BF16)
        dc_ref[...] = dc_acc + _dot(dcbb, bm)
        db_ref[...] = db_acc + _dot(dcbb, cm, 0, 0)
        da = _dot_exact_left(tu_ref[...], dac) + tail
        real = lane < hpg
        ddt = jnp.where(real, (da * a_neg + xdx) * _sigmoid(dtraw + bias_ref[0]), 0.0)
        ddt_ref[...] = ddt
        dd_ref[0] += dskip
        dbias_ref[0] += jnp.sum(ddt, axis=0, keepdims=True)
        dalog_ref[0] += jnp.where(lane1 < hpg, jnp.sum(da * dt, axis=0, keepdims=True) * a_neg, 0.0)

    return pl.pallas_call(
        body, name=name, grid=(G // gp, nc),
        in_specs=[sp["x"], sp["b"], sp["c"], sp["z"], sp["dt"], sp["gvec"], sp["gvec"], sp["chan"], sp["chan"],
                  sp["tri"], sp["tri"], sp["x"], sp["hp"], sp["x"]],
        out_specs=[sp["x"], sp["x"], sp["bc"], sp["bc"], sp["dt"], sp["gvec"], sp["gvec"], sp["gvec"], sp["chan"]],
        out_shape=[jax.ShapeDtypeStruct((S, inner), BF16), jax.ShapeDtypeStruct((S, inner), F32),
                   jax.ShapeDtypeStruct((S, G * n), F32), jax.ShapeDtypeStruct((S, G * n), F32),
                   jax.ShapeDtypeStruct((S, G * LANES), F32), jax.ShapeDtypeStruct((G, 1, LANES), F32),
                   jax.ShapeDtypeStruct((G, 1, LANES), F32), jax.ShapeDtypeStruct((G, 1, LANES), F32),
                   jax.ShapeDtypeStruct((1, inner), F32)],
        scratch_shapes=[pltpu.VMEM((gp * gw, n), F32)],
        compiler_params=_params(("arbitrary", "arbitrary")),
    )(xbc, xbc, xbc, zx, dtg, bias_g, alog_g, d_chan, ngain, _tri(L, "row_ge_col"), _tri(L, "row_le_col"),
      yfull, hp, dyn)


def _spread_dt(w_dt_t, G, hpg):
    K = w_dt_t.shape[1]
    w = w_dt_t.reshape(G, hpg, K)
    return jnp.pad(w, ((0, 0), (0, LANES - hpg), (0, 0))).reshape(G * LANES, K)


def _group_vec(v, G, hpg):
    return jnp.pad(v.reshape(G, 1, hpg), ((0, 0), (0, 0), (0, LANES - hpg)))


def local_step(x, target, W, late=None):
    S, D = x.shape
    depth = W["mix_norm"].shape[0]
    gm_groups, gm_chunk = W["gm_w_s"].shape[1], W["gm_w_s"].shape[2]
    heads = W["ssm_dt_bias"].shape[1]
    inner = heads * SB_HEAD_DIM
    L = gm_chunk
    received = None

    saved = []
    for i in range(depth):
        kind, j = i % 3, i // 3
        s = dict(x=x)
        h = rms_fwd(x, W["mix_norm"][i:i + 1], name="rms_mix_fwd")
        s["h"] = h
        if kind == 0:
            qkv = matmul(h, W["sb_w_qkv"][j], tb=True, name="mm_qkv")
            if late is not None and i == 0:
                o, gathered = sb_attn_fwd(qkv, W["sb_q_gain"][j:j + 1], W["sb_k_gain"][j:j + 1], name="sb_fwd_gather",
                                          gather=late.shards)
                late.fill(W, gathered)
            else:
                o = sb_attn_fwd(qkv, W["sb_q_gain"][j:j + 1], W["sb_k_gain"][j:j + 1], name="sb_fwd")
            x1 = matmul(o, W["sb_w_o"][j], residual=x, name="mm_sb_out")
            s.update(qkv=qkv, o=o)
        elif kind == 1:
            wc = jnp.where(jnp.tril(jnp.ones((gm_chunk, gm_chunk), bool)), W["gm_w_s"][j], 0.0).astype(BF16)
            bsf = jnp.broadcast_to(W["gm_b_s"][j][:, :, None], (gm_groups, gm_chunk, W["gm_v_gain"].shape[1] // gm_groups)).astype(F32)
            zzpre = matmul(h, W["gm_w_in"][j], tb=True, name="mm_gm_in")
            p = gmlp_fwd(zzpre, W["gm_b_in"][j:j + 1], W["gm_v_gain"][j:j + 1], wc, bsf, name="gm_fwd")
            x1 = matmul(p, W["gm_w_out"][j], residual=x, name="mm_gm_out")
            s.update(zzpre=zzpre, p=p, wc=wc, bsf=bsf)
        else:
            conv_dim = W["ssm_conv_w"].shape[2]
            G = (conv_dim - inner) // (2 * SSM_STATE)
            hpg = heads // G
            w_in = W["ssm_w_in"][j]
            w_zx = w_in[:inner + conv_dim]
            w_dtg = _spread_dt(w_in[inner + conv_dim:], G, hpg)
            bias_g = _group_vec(W["ssm_dt_bias"][j], G, hpg)
            alog_g = _group_vec(W["ssm_a_log"][j], G, hpg)
            d_chan = jnp.repeat(W["ssm_d"][j], SB_HEAD_DIM)[None, :]
            ngain = W["ssm_norm_gain"][j:j + 1]
            zx = matmul(h, w_zx, tb=True, name="mm_ssm_zx")
            dtg = matmul(h, w_dtg, tb=True, name="mm_ssm_dt")
            xbc = conv_fwd(zx, W["ssm_conv_w"][j], W["ssm_conv_b"][j:j + 1], inner, name="conv_fwd")
            yn, yfull, hp = ssd_fwd(xbc, zx, dtg, bias_g, alog_g, d_chan, ngain, L, G, name="ssd_fwd")
            x1 = matmul(yn, W["ssm_w_out"][j], residual=x, name="mm_ssm_out")
            s.update(w_zx=w_zx, w_dtg=w_dtg, bias_g=bias_g, alog_g=alog_g, d_chan=d_chan, ngain=ngain,
                     zx=zx, dtg=dtg, xbc=xbc, yn=yn, yfull=yfull, hp=hp)
        h2 = rms_fwd(x1, W["ffn_norm"][i:i + 1], name="rms_ffn_fwd")
        gu, a = ffn_up_fwd(h2, W["ffn_w_gu"][i], name="ffn_up_fwd")
        x2 = matmul(a, W["ffn_w_down"][i], residual=x1, name="mm_ffn_down")
        s.update(x1=x1, h2=h2, gu=gu, a=a)
        saved.append(s)
        x = x2

    dx, loss = loss_head(x, target, name="loss_head")

    gw = {k: {} for k in WEIGHTS}
    for i in reversed(range(depth)):
        kind, j = i % 3, i // 3
        s = saved[i]
        gw["ffn_w_down"][i] = matmul(s["a"], dx, ta=True, out_dtype=BF16, name="mm_ffn_dwdown")
        dgu = ffn_up_bwd(dx, W["ffn_w_down"][i], s["gu"], name="ffn_up_bwd")
        dh2 = matmul(dgu, W["ffn_w_gu"][i], a_split=2, name="mm_ffn_dh")
        gw["ffn_w_gu"][i] = matmul(dgu, s["h2"], ta=True, a_split=2, out_dtype=BF16, name="mm_ffn_dwgu")
        dx1, dgn = rms_bwd(s["x1"], W["ffn_norm"][i:i + 1], dh2, dx, name="rms_ffn_bwd")
        gw["ffn_norm"][i] = dgn[0]
        if kind == 0:
            do = matmul(dx1, W["sb_w_o"][j], tb=True, out_dtype=BF16, name="mm_sb_do")
            gw["sb_w_o"][j] = matmul(s["o"], dx1, ta=True, out_dtype=BF16, name="mm_sb_dwo")
            if late is not None and i == 0:
                dqkv, dqg, dkg, received = sb_attn_bwd(
                    s["qkv"], s["o"], do, W["sb_q_gain"][j:j + 1], W["sb_k_gain"][j:j + 1], name="sb_bwd_scatter",
                    scatter=late.contributions(gw))
            else:
                dqkv, dqg, dkg = sb_attn_bwd(s["qkv"], s["o"], do, W["sb_q_gain"][j:j + 1], W["sb_k_gain"][j:j + 1],
                                             name="sb_bwd")
            gw["sb_q_gain"][j] = dqg[0]
            gw["sb_k_gain"][j] = dkg[0]
            dh = matmul(dqkv, W["sb_w_qkv"][j], a_split=3, name="mm_sb_dh")
            gw["sb_w_qkv"][j] = matmul(dqkv, s["h"], ta=True, a_split=3, out_dtype=BF16, name="mm_sb_dwqkv")
        elif kind == 1:
            dp = matmul(dx1, W["gm_w_out"][j], tb=True, name="mm_gm_dp")
            gw["gm_w_out"][j] = matmul(s["p"], dx1, ta=True, out_dtype=BF16, name="mm_gm_dwout")
            dzz, db_in, dvg, dws, dbs = gmlp_bwd(s["zzpre"], W["gm_b_in"][j:j + 1], W["gm_v_gain"][j:j + 1],
                                                s["wc"], s["bsf"], dp, name="gm_bwd")
            gw["gm_b_in"][j] = db_in[0]
            gw["gm_v_gain"][j] = dvg[0]
            gw["gm_w_s"][j] = dws
            gw["gm_b_s"][j] = dbs[:, :gm_groups].T
            dh = matmul(dzz, W["gm_w_in"][j], name="mm_gm_dh")
            gw["gm_w_in"][j] = matmul(dzz, s["h"], ta=True, out_dtype=BF16, name="mm_gm_dwin")
        else:
            conv_dim = W["ssm_conv_w"].shape[2]
            G = (conv_dim - inner) // (2 * SSM_STATE)
            hpg = heads // G
            dyn = matmul(dx1, W["ssm_w_out"][j], tb=True, name="mm_ssm_dyn")
            gw["ssm_w_out"][j] = matmul(s["yn"], dx1, ta=True, out_dtype=BF16, name="mm_ssm_dwout")
            dz, dxs, dbm, dcm, ddt, dbias, dalog, dd, dng = ssd_bwd(
                s["xbc"], s["zx"], s["dtg"], s["bias_g"], s["alog_g"], s["d_chan"], s["ngain"], s["yfull"], s["hp"],
                dyn, L, G, name="ssd_bwd")
            dpre, dcw, dcb = conv_bwd(s["zx"], W["ssm_conv_w"][j], W["ssm_conv_b"][j:j + 1], inner, [dxs, dbm, dcm],
                                      name="conv_bwd")
            dzx = jnp.concatenate([dz, dpre], axis=1)
            dh = matmul(ddt, s["w_dtg"], name="mm_ssm_dh_dt")
            dh = matmul(dzx, s["w_zx"], residual=dh, name="mm_ssm_dh")
            dw_zx = matmul(dzx, s["h"], ta=True, out_dtype=BF16, name="mm_ssm_dwzx")
            dw_dtg = matmul(ddt, s["h"], ta=True, out_dtype=BF16, name="mm_ssm_dwdt")
            dw_dt = dw_dtg.reshape(G, LANES, D)[:, :hpg, :].reshape(heads, D)
            gw["ssm_w_in"][j] = jnp.concatenate([dw_zx, dw_dt], axis=0)
            gw["ssm_conv_w"][j] = dcw
            gw["ssm_conv_b"][j] = dcb[0]
            gw["ssm_dt_bias"][j] = dbias[:, 0, :hpg].reshape(heads)
            gw["ssm_a_log"][j] = dalog[:, 0, :hpg].reshape(heads)
            gw["ssm_d"][j] = dd[:, 0, :hpg].reshape(heads)
            gw["ssm_norm_gain"][j] = dng[0]
        dx, dgn = rms_bwd(s["x"], W["mix_norm"][i:i + 1], dh, dx1, name="rms_mix_bwd")
        gw["mix_norm"][i] = dgn[0]

    return loss, dx, gw, received


MESH = pl.DeviceIdType.MESH
HBM_SPEC = pl.BlockSpec(memory_space=pltpu.HBM)
VMEM_SPEC = pl.BlockSpec(memory_space=pltpu.VMEM)


def _my_position():
    return lax.axis_index("x"), lax.axis_index("y"), lax.axis_index("c")


def _flip(v, bit):
    return 1 - v if bit else v


def all_gather(shards, *, name):
    n = len(shards)

    def body(*refs):
        for phase in ("start", "forward", "finish"):
            _ag_phase(refs[:n], refs[n:2 * n], refs[2 * n:], phase)

    return pl.pallas_call(
        body, name=name, out_shape=[jax.ShapeDtypeStruct((N_DEV,) + s.shape, s.dtype) for s in shards],
        in_specs=[HBM_SPEC] * n, out_specs=[HBM_SPEC] * n, scratch_shapes=_copy_semaphores(n),
    )(*shards)


def _ag_phase(x_refs, out_refs, sems, phase):
    send_sems, recv_sems, local_sems = sems
    x, y, c = _my_position()
    me, sibling = (x, y, c), (x, y, 1 - c)
    chips = [(1 - x, y), (x, 1 - y), (1 - x, 1 - y)]
    for p, (x_ref, out_ref) in enumerate(zip(x_refs, out_refs)):
        def slot(px, py, pc):
            return out_ref.at[4 * px + 2 * py + pc]

        def copy(k, block, to, src=None):
            return pltpu.make_async_remote_copy(
                src_ref=slot(*block) if src is None else src, dst_ref=slot(*block),
                send_sem=send_sems.at[7 * p + k], recv_sem=recv_sems.at[7 * p + k], device_id=to, device_id_type=MESH)

        mine = pltpu.make_async_copy(x_ref, slot(*me), local_sems.at[p])
        first = [copy(0, me, sibling, src=x_ref)]
        first += [copy(1 + j, me, (*chip, c), src=x_ref) for j, chip in enumerate(chips)]
        passed = [copy(4 + j, (*chip, c), sibling) for j, chip in enumerate(chips)]
        if phase == "start":
            mine.start()
            for cp in first:
                cp.start()
        elif phase == "forward":
            for j, chip in enumerate(chips):
                copy(1 + j, (*chip, c), me).wait_recv()
                passed[j].start()
        else:
            copy(0, sibling, me).wait_recv()
            for j, chip in enumerate(chips):
                copy(4 + j, (*chip, 1 - c), me).wait_recv()
            for cp in first + passed:
                cp.wait_send()
            mine.wait()


def _copy_semaphores(n):
    return [pltpu.SemaphoreType.DMA((7 * n,)), pltpu.SemaphoreType.DMA((7 * n,)), pltpu.SemaphoreType.DMA((n,))]


def _rs_phase(g_refs, out_refs, sems, phase):
    send_sems, recv_sems, local_sems = sems
    x, y, c = _my_position()
    me = 4 * x + 2 * y + c
    copies = []
    for p, (g_ref, out_ref) in enumerate(zip(g_refs, out_refs)):
        copies.append(pltpu.make_async_copy(g_ref.at[me], out_ref.at[me], local_sems.at[p]))
        for k in range(1, N_DEV):
            px, py, pc = _flip(x, k & 4), _flip(y, k & 2), _flip(c, k & 1)
            copies.append(pltpu.make_async_remote_copy(
                src_ref=g_ref.at[4 * px + 2 * py + pc], dst_ref=out_ref.at[me],
                send_sem=send_sems.at[7 * p + k - 1], recv_sem=recv_sems.at[7 * p + k - 1],
                device_id=(px, py, pc), device_id_type=MESH))
    for cp in copies:
        if phase == "start":
            cp.start()
        else:
            cp.wait()


def sum_slots(recv, *, name):
    n, R, C = recv.shape
    tr = _pick(R, (512, 256, 128))

    def body(r_ref, o_ref):
        acc = r_ref[0].astype(F32)
        for s in range(1, n):
            acc = acc + r_ref[s].astype(F32)
        o_ref[...] = acc

    return pl.pallas_call(
        body, name=name, grid=(R // tr,), in_specs=[pl.BlockSpec((n, tr, C), lambda i: (0, i, 0))],
        out_specs=pl.BlockSpec((tr, C), lambda i: (i, 0)), out_shape=jax.ShapeDtypeStruct((R, C), F32),
        compiler_params=_params(("parallel",)),
    )(recv)


def all_reduce_small(vs, scatter, *, name):
    n, ns = len(vs), len(scatter)

    def body(*refs):
        v_refs, g_refs = refs[:n], refs[n:n + ns]
        o_refs, r_refs = refs[n + ns:2 * n + ns], refs[2 * n + ns:2 * n + 2 * ns]
        bufs = refs[2 * n + 2 * ns:3 * n + 2 * ns]
        send_sems, recv_sems = refs[3 * n + 2 * ns:3 * n + 2 * ns + 2]
        rs_sems = refs[3 * n + 2 * ns + 2:]
        x, y, c = _my_position()
        me = 4 * x + 2 * y + c
        _rs_phase(g_refs, r_refs, rs_sems, "start")
        copies = []
        for p, (v_ref, buf) in enumerate(zip(v_refs, bufs)):
            buf[me] = v_ref[...]
            for k in range(1, N_DEV):
                px, py, pc = _flip(x, k & 4), _flip(y, k & 2), _flip(c, k & 1)
                copies.append(pltpu.make_async_remote_copy(
                    src_ref=v_ref, dst_ref=buf.at[me], send_sem=send_sems.at[7 * p + k - 1],
                    recv_sem=recv_sems.at[7 * p + k - 1], device_id=(px, py, pc), device_id_type=MESH))
        for cp in copies:
            cp.start()
        for cp in copies:
            cp.wait()
        for o_ref, buf in zip(o_refs, bufs):
            acc = buf[0]
            for s in range(1, N_DEV):
                acc = acc + buf[s]
            o_ref[...] = acc
        _rs_phase(g_refs, r_refs, rs_sems, "finish")

    out = pl.pallas_call(
        body, name=name,
        out_shape=[jax.ShapeDtypeStruct(v.shape, F32) for v in vs] + [jax.ShapeDtypeStruct(g.shape, g.dtype) for g in scatter],
        in_specs=[VMEM_SPEC] * n + [HBM_SPEC] * ns, out_specs=[VMEM_SPEC] * n + [HBM_SPEC] * ns,
        scratch_shapes=[pltpu.VMEM((N_DEV,) + v.shape, F32) for v in vs]
        + [pltpu.SemaphoreType.DMA((7 * n,)), pltpu.SemaphoreType.DMA((7 * n,))] + _copy_semaphores(ns),
        compiler_params=pltpu.CompilerParams(vmem_limit_bytes=VMEM_LIMIT_BYTES),
    )(*vs, *scatter)
    return list(out[:n]), list(out[n:])


def _pad_rows(a, mult):
    pad = (-a.shape[0]) % mult
    return jnp.pad(a, ((0, pad), (0, 0))) if pad else a


def _pack_small(arrays):
    flat = []
    for a in arrays:
        f = a.reshape(-1).astype(F32)
        flat.append(jnp.pad(f, (0, (-f.shape[0]) % LANES)))
    return _pad_rows(jnp.concatenate(flat).reshape(-1, LANES), 8)


def _unpack_small(packed, shapes):
    flat = packed.reshape(-1)
    out, r = [], 0
    for shp in shapes:
        n = math.prod(shp)
        out.append(flat[r:r + n].reshape(shp))
        r += n + (-n) % LANES
    return out


ARG_NAMES = ("x",) + WEIGHTS + ("loss_target",) + tuple("m_" + w for w in WEIGHTS) + tuple("v_" + w for w in WEIGHTS)


def kernel(x, mix_norm, ffn_norm, sb_w_qkv, sb_q_gain, sb_k_gain, sb_w_o, gm_w_in, gm_b_in, gm_v_gain, gm_w_s, gm_b_s, gm_w_out, ssm_w_in, ssm_conv_w, ssm_conv_b, ssm_dt_bias, ssm_a_log, ssm_d, ssm_norm_gain, ssm_w_out, ffn_w_gu, ffn_w_down, loss_target, m_mix_norm, m_ffn_norm, m_sb_w_qkv, m_sb_q_gain, m_sb_k_gain, m_sb_w_o, m_gm_w_in, m_gm_b_in, m_gm_v_gain, m_gm_w_s, m_gm_b_s, m_gm_w_out, m_ssm_w_in, m_ssm_conv_w, m_ssm_conv_b, m_ssm_dt_bias, m_ssm_a_log, m_ssm_d, m_ssm_norm_gain, m_ssm_w_out, m_ffn_w_gu, m_ffn_w_down, v_mix_norm, v_ffn_norm, v_sb_w_qkv, v_sb_q_gain, v_sb_k_gain, v_sb_w_o, v_gm_w_in, v_gm_b_in, v_gm_v_gain, v_gm_w_s, v_gm_b_s, v_gm_w_out, v_ssm_w_in, v_ssm_conv_w, v_ssm_conv_b, v_ssm_dt_bias, v_ssm_a_log, v_ssm_d, v_ssm_norm_gain, v_ssm_w_out, v_ffn_w_gu, v_ffn_w_down):
    given = dict(zip(ARG_NAMES, (x, mix_norm, ffn_norm, sb_w_qkv, sb_q_gain, sb_k_gain, sb_w_o, gm_w_in, gm_b_in, gm_v_gain, gm_w_s, gm_b_s, gm_w_out, ssm_w_in, ssm_conv_w, ssm_conv_b, ssm_dt_bias, ssm_a_log, ssm_d, ssm_norm_gain, ssm_w_out, ffn_w_gu, ffn_w_down, loss_target, m_mix_norm, m_ffn_norm, m_sb_w_qkv, m_sb_q_gain, m_sb_k_gain, m_sb_w_o, m_gm_w_in, m_gm_b_in, m_gm_v_gain, m_gm_w_s, m_gm_b_s, m_gm_w_out, m_ssm_w_in, m_ssm_conv_w, m_ssm_conv_b, m_ssm_dt_bias, m_ssm_a_log, m_ssm_d, m_ssm_norm_gain, m_ssm_w_out, m_ffn_w_gu, m_ffn_w_down, v_mix_norm, v_ffn_norm, v_sb_w_qkv, v_sb_q_gain, v_sb_k_gain, v_sb_w_o, v_gm_w_in, v_gm_b_in, v_gm_v_gain, v_gm_w_s, v_gm_b_s, v_gm_w_out, v_ssm_w_in, v_ssm_conv_w, v_ssm_conv_b, v_ssm_dt_bias, v_ssm_a_log, v_ssm_d, v_ssm_norm_gain, v_ssm_w_out, v_ffn_w_gu, v_ffn_w_down)))
    mx, my, mc = _my_position()
    me = 4 * mx + 2 * my + mc

    pieces = [(k, l) for k in BIG for l in range(given[k].shape[0])]
    early = [("sb_w_qkv", 0)]
    late_pieces = [p for p in pieces if p not in early]
    last = [("sb_w_qkv", 0)]
    main = [p for p in pieces if p not in last]

    def shards_of(ps):
        return [(given[k][l].T if k in COL_SHARDED else given[k][l]).astype(BF16) for k, l in ps]

    def piece_to_full(g, k):
        rows, cols = given[k].shape[1:]
        return g.reshape(N_DEV * cols, rows) if k in COL_SHARDED else g.reshape(N_DEV * rows, cols)

    def full_to_piece(full, k):
        return full.reshape(N_DEV, -1, PACK_COLS)

    def summed_to_shard(g, k):
        rows, cols = given[k].shape[1:]
        return g.reshape(cols, rows) if k in COL_SHARDED else g.reshape(rows, cols)

    def contributions(gw, ps):
        return [full_to_piece(gw[k][l], k) for k, l in ps]

    sharded_small = [lax.bitcast_convert_type(given[k], BF16) for k in SMALL_SHARDED]
    tail = jnp.concatenate([a.reshape(-1) for a in sharded_small])
    tail = jnp.pad(tail, (0, (-tail.size) % PACK_COLS)).reshape(-1, PACK_COLS)

    W = {k: given[k] for k in SMALL if k not in SMALL_SHARDED}
    W.update({k: [None] * given[k].shape[0] for k in BIG})
    for (k, l), g in zip(early, all_gather(shards_of(early), name="all_gather_early")):
        W[k][l] = piece_to_full(g, k)

    class Late:
        shards = shards_of(late_pieces) + [tail]

        @staticmethod
        def fill(weights, gathered):
            for (k, l), g in zip(late_pieces, gathered):
                weights[k][l] = piece_to_full(g, k)
            tail_g = gathered[-1].reshape(N_DEV, -1)
            off = 0
            for k, a in zip(SMALL_SHARDED, sharded_small):
                g = lax.bitcast_convert_type(tail_g[:, off:off + a.size].reshape((N_DEV,) + a.shape), F32)
                weights[k] = jnp.moveaxis(g, 0, -2).reshape(g.shape[1:-1] + (N_DEV * g.shape[-1],))
                off += a.size

        @staticmethod
        def contributions(gw):
            return contributions(gw, main)

    loss, gx, gw, received_main = local_step(given["x"][0], given["loss_target"][0], W, late=Late)

    grads_small = {k: jnp.stack([gw[k][l] for l in sorted(gw[k])], axis=0) for k in SMALL}
    packed_names = tuple(k for k in SMALL if k != "gm_w_s")
    small_shapes = [grads_small[k].shape for k in packed_names] + [(1, 1)]
    (red_ws, red_rest), received_last = all_reduce_small(
        [grads_small["gm_w_s"].reshape(-1, LANES), _pack_small([grads_small[k] for k in packed_names] + [loss])],
        contributions(gw, last), name="all_reduce_small_and_last_exchange")
    small_full = dict(zip(packed_names + ("loss",), _unpack_small(red_rest, small_shapes)))
    small_full["gm_w_s"] = red_ws.reshape(grads_small["gm_w_s"].shape)

    g_piece = {}
    for grp, received in ((main, received_main), (last, received_last)):
        for p, r in zip(grp, received):
            g_piece[p] = summed_to_shard(sum_slots(r, name="reduce_scatter_sum"), p[0])
    out_g, out_d, out_m, out_v = {}, {}, {}, {}
    for k in BIG:
        swap = lambda a: jnp.swapaxes(a, -1, -2)
        keep_t = k in COL_SHARDED and given[k].shape[-1] % LANES != 0
        flip = swap if keep_t else (lambda a: a)
        g = jnp.stack([g_piece[(k, l)] for l in range(given[k].shape[0])], axis=0)
        if k in COL_SHARDED and not keep_t:
            g = swap(g)
        res = adamw(flip(given[k]), g, flip(given["m_" + k]), flip(given["v_" + k]), name="adamw_" + k)
        out_g[k], out_d[k], out_m[k], out_v[k] = (flip(a) for a in (g,) + tuple(res))

    gsmall = {}
    for k in SMALL:
        g = small_full[k]
        if k in SMALL_SHARDED:
            n = given[k].shape[-1]
            g = lax.dynamic_slice_in_dim(g, me * n, n, axis=g.ndim - 1)
        gsmall[k] = g
    local_shapes = [given[k].shape for k in packed_names]
    dsm, nmsm, nvsm = adamw(*[_pack_small([src[k] for k in packed_names]) for src in (
        given, gsmall, {k: given["m_" + k] for k in packed_names}, {k: given["v_" + k] for k in packed_names})],
        name="adamw_small")
    out_g.update(gsmall)
    for dst, src in ((out_d, dsm), (out_m, nmsm), (out_v, nvsm)):
        dst.update(zip(packed_names, _unpack_small(src, local_shapes)))
    ws_shape = given["gm_w_s"].shape
    out_d["gm_w_s"], out_m["gm_w_s"], out_v["gm_w_s"] = (a.reshape(ws_shape) for a in adamw(
        *[a.reshape((-1,) + ws_shape[-2:]) for a in (given["gm_w_s"], gsmall["gm_w_s"], given["m_gm_w_s"], given["v_gm_w_s"])],
        name="adamw_gm_w_s"))

    return (small_full["loss"].reshape(()), gx[None],
            *[out_g[k] for k in WEIGHTS], *[out_d[k] for k in WEIGHTS],
            *[out_m[k] for k in WEIGHTS], *[out_v[k] for k in WEIGHTS])
```

```python
import math

import jax
import jax.numpy as jnp
from jax import lax
from jax.experimental import pallas as pl
from jax.experimental.pallas import tpu as pltpu

F32 = jnp.float32
BF16 = jnp.bfloat16
EPS = 1e-6
N_DEV = 8
SB_HEAD_DIM = 64
SB_TILE = 256
SB_FWD_QUERY_BLOCKS = 4
SB_BWD_QUERY_BLOCKS = 2
SSM_STATE = 128
SSD_FWD_GROUPS_PER_STEP = 2
SSD_BWD_GROUPS_PER_STEP = 1
SSM_CONV = 4
ADAM_LR = 0.001
ADAM_B1 = 0.9
ADAM_B2 = 0.999
ADAM_EPS = 1e-08
ADAM_WD = 0.01
ADAM_STEP = 10
VMEM_LIMIT_BYTES = 56 * 1024 * 1024
MATMUL_VMEM_BUDGET = 40 * 1024 * 1024
LANES = 128
PACK_COLS = 1024

BIG = ("sb_w_qkv", "sb_w_o", "gm_w_in", "gm_w_out", "ssm_w_in", "ssm_w_out", "ffn_w_gu", "ffn_w_down")
COL_SHARDED = ("sb_w_qkv", "gm_w_in", "ssm_w_in", "ffn_w_gu")
SMALL = ("mix_norm", "ffn_norm", "sb_q_gain", "sb_k_gain", "gm_b_in", "gm_v_gain", "gm_w_s", "gm_b_s",
         "ssm_conv_w", "ssm_conv_b", "ssm_dt_bias", "ssm_a_log", "ssm_d", "ssm_norm_gain")
SMALL_SHARDED = ("ssm_conv_w", "ssm_conv_b", "ssm_norm_gain")
WEIGHTS = ("mix_norm", "ffn_norm", "sb_w_qkv", "sb_q_gain", "sb_k_gain", "sb_w_o", "gm_w_in", "gm_b_in",
           "gm_v_gain", "gm_w_s", "gm_b_s", "gm_w_out", "ssm_w_in", "ssm_conv_w", "ssm_conv_b", "ssm_dt_bias",
           "ssm_a_log", "ssm_d", "ssm_norm_gain", "ssm_w_out", "ffn_w_gu", "ffn_w_down")


def _params(semantics=None):
    return pltpu.CompilerParams(dimension_semantics=semantics, vmem_limit_bytes=VMEM_LIMIT_BYTES)


def _pick(n, prefs):
    for t in prefs:
        if t <= n and n % t == 0:
            return t
    return n


def _dot(a, b, ca=1, cb=0):
    return lax.dot_general(a, b, (((ca,), (cb,)), ((), ())), preferred_element_type=F32)


def _split3(v):
    h1 = v.astype(BF16)
    r1 = v - h1.astype(F32)
    h2 = r1.astype(BF16)
    h3 = (r1 - h2.astype(F32)).astype(BF16)
    return h1, h2, h3


def _dot_exact_left(mat01, v):
    h1, h2, h3 = _split3(v)
    return _dot(mat01, h1) + _dot(mat01, h2) + _dot(mat01, h3)


def _sum_all(v):
    return jnp.sum(jnp.sum(v, axis=0, keepdims=True), axis=1, keepdims=True)


def _sigmoid(v):
    return 1.0 / (1.0 + jnp.exp(-v))


def _softplus(v):
    return jnp.maximum(v, 0.0) + jnp.log(1.0 + jnp.exp(-jnp.abs(v)))


def _erf(v):
    a = jnp.abs(v)
    t = 1.0 / (1.0 + 0.3275911 * a)
    poly = t * (0.254829592 + t * (-0.284496736 + t * (1.421413741 + t * (-1.453152027 + t * 1.061405429))))
    e = 1.0 - poly * jnp.exp(-a * a)
    return jnp.where(v < 0, -e, e)


def _gelu_and_grad(v):
    cdf = 0.5 * (1.0 + _erf(v * (1.0 / math.sqrt(2.0))))
    pdf = jnp.exp(-0.5 * v * v) * (1.0 / math.sqrt(2.0 * math.pi))
    return v * cdf, cdf + v * pdf


def matmul(a, b, *, ta=False, tb=False, out_dtype=F32, residual=None, a_split=1, b_split=1, norm_bwd=None, name):
    if a_split > 1 and ta:
        assert a.shape[0] == a_split
        K, M = a.shape[1], a_split * a.shape[2]
    elif a_split > 1:
        assert a.shape[0] == a_split
        M, K = a.shape[1], a_split * a.shape[2]
    elif ta:
        K, M = a.shape
    else:
        M, K = a.shape
    if b_split > 1:
        assert not tb and b.shape[0] == b_split
        Kb, N = b.shape[1], b_split * b.shape[2]
    elif tb:
        N, Kb = b.shape
    else:
        Kb, N = b.shape
    assert K == Kb, (a.shape, b.shape, ta, tb)
    has_res = residual is not None
    tm = _pick(M // a_split if ta else M, (1024, 1408, 768, 512, 256, 128))
    tn = _pick(N // b_split, (1024, 1408, 1536, 768, 512, 256, 128))

    has_norm = norm_bwd is not None
    assert not has_norm or (tn == N and out_dtype == F32)

    def vmem_bytes(tk):
        tiles = tm * tk * a.dtype.itemsize + tk * tn * b.dtype.itemsize
        outs = tm * tn * jnp.dtype(out_dtype).itemsize + (tm * tn * 4 if has_res else 0) + (2 * tm * tn * 4 if has_norm else 0)
        return 2 * tiles + 2 * outs + (tm * tn * 4 if tk < K else 0)

    kp = K if ta else K // a_split
    tk = next((t for t in (K, 2048, 1408, 1024, 512, 256) if t <= kp and kp % t == 0 and vmem_bytes(t) <= MATMUL_VMEM_BUDGET),
              _pick(kp, (128,)))
    nk = K // tk
    if a_split > 1 and ta:
        nib = M // a_split // tm
        a_spec = pl.BlockSpec((None, tk, tm), lambda i, j, k: (i // nib, k, i % nib))
    elif a_split > 1:
        nkb = kp // tk
        a_spec = pl.BlockSpec((None, tm, tk), lambda i, j, k: (k // nkb, i, k % nkb))
    else:
        a_spec = pl.BlockSpec((tk, tm), lambda i, j, k: (k, i)) if ta else pl.BlockSpec((tm, tk), lambda i, j, k: (i, k))
    if b_split > 1:
        njb = N // b_split // tn
        b_spec = pl.BlockSpec((None, tk, tn), lambda i, j, k: (j // njb, k, j % njb))
    else:
        b_spec = pl.BlockSpec((tn, tk), lambda i, j, k: (j, k)) if tb else pl.BlockSpec((tk, tn), lambda i, j, k: (k, j))
    o_spec = pl.BlockSpec((tm, tn), lambda i, j, k: (i, j))
    ca, cb = (0 if ta else 1), (1 if tb else 0)

    def body(*refs):
        a_ref, b_ref = refs[:2]
        n_in = 2 + has_res + 3 * has_norm
        r_ref = refs[2] if has_res else None
        o_ref = refs[n_in]
        if has_norm:
            x_ref, g_ref, dres_ref = refs[n_in - 3:n_in]
            dg_ref = refs[n_in + 1]

            @pl.when(jnp.logical_and(pl.program_id(0) == 0, pl.program_id(2) == 0))
            def _():
                dg_ref[...] = jnp.zeros_like(dg_ref)

        def finish(r):
            if has_res:
                r = r + r_ref[...]
            if has_norm:
                xv = x_ref[...]
                rs = lax.rsqrt(jnp.mean(xv * xv, axis=1, keepdims=True) + EPS)
                xhat = xv * rs
                t = r * g_ref[...]
                dg_ref[...] += jnp.sum(r * xhat, axis=0, keepdims=True)
                r = dres_ref[...] + rs * (t - xhat * jnp.mean(xhat * t, axis=1, keepdims=True))
            o_ref[...] = r.astype(out_dtype)

        def part():
            return _dot(a_ref[...].astype(BF16), b_ref[...].astype(BF16), ca, cb)

        if nk == 1:
            finish(part())
            return
        acc = refs[-1]
        k = pl.program_id(2)


        @pl.when(k == 0)
        def _():
            acc[...] = part()

        @pl.when(jnp.logical_and(k > 0, k < nk - 1))
        def _():
            acc[...] += part()

        @pl.when(k == nk - 1)
        def _():
            finish(acc[...] + part())

    in_specs = [a_spec, b_spec] + ([o_spec] if has_res else [])
    args = (a, b) + ((residual,) if has_res else ())
    out_specs, out_shape = o_spec, jax.ShapeDtypeStruct((M, N), out_dtype)
    if has_norm:
        vec = pl.BlockSpec((1, N), lambda i, j, k: (0, 0))
        in_specs += [o_spec, vec, o_spec]
        args += tuple(norm_bwd)
        out_specs, out_shape = [o_spec, vec], [out_shape, jax.ShapeDtypeStruct((1, N), F32)]
    return pl.pallas_call(
        body, name=name, grid=(M // tm, N // tn, nk), in_specs=in_specs, out_specs=out_specs, out_shape=out_shape,
        scratch_shapes=[pltpu.VMEM((tm, tn), F32)] if nk > 1 else [],
        compiler_params=_params(("arbitrary" if has_norm else "parallel", "parallel", "arbitrary")),
    )(*args)


def rms_fwd(x, gain, *, name):
    S, D = x.shape
    tr = _pick(S, (512, 256, 128))

    def body(x_ref, g_ref, o_ref):
        xv = x_ref[...]
        r = lax.rsqrt(jnp.mean(xv * xv, axis=1, keepdims=True) + EPS)
        o_ref[...] = (xv * r * g_ref[...]).astype(BF16)

    return pl.pallas_call(
        body, name=name, grid=(S // tr,),
        in_specs=[pl.BlockSpec((tr, D), lambda i: (i, 0)), pl.BlockSpec((1, D), lambda i: (0, 0))],
        out_specs=pl.BlockSpec((tr, D), lambda i: (i, 0)), out_shape=jax.ShapeDtypeStruct((S, D), BF16),
        compiler_params=_params(("parallel",)),
    )(x, gain)


def ffn_up_fwd(h, w_gu_t, *, name):
    S, K = h.shape
    F = w_gu_t.shape[0] // 2
    tm = _pick(S, (512, 256, 128))
    tn = _pick(F, (1408, 1024, 768, 512, 256, 128))
    nj = F // tn

    def body(h_ref, wg_ref, wu_ref, gu_ref, a_ref):
        hv = h_ref[...]
        g = _dot(hv, wg_ref[...], 1, 1)
        u = _dot(hv, wu_ref[...], 1, 1)
        gu_ref[0] = g
        gu_ref[1] = u
        a_ref[...] = (g * _sigmoid(g) * u).astype(BF16)

    return pl.pallas_call(
        body, name=name, grid=(nj, S // tm),
        in_specs=[pl.BlockSpec((tm, K), lambda j, i: (i, 0)), pl.BlockSpec((tn, K), lambda j, i: (j, 0)),
                  pl.BlockSpec((tn, K), lambda j, i: (nj + j, 0))],
        out_specs=[pl.BlockSpec((2, tm, tn), lambda j, i: (0, i, j)), pl.BlockSpec((tm, tn), lambda j, i: (i, j))],
        out_shape=[jax.ShapeDtypeStruct((2, S, F), F32), jax.ShapeDtypeStruct((S, F), BF16)],
        compiler_params=_params(("parallel", "parallel")),
    )(h, w_gu_t, w_gu_t)


def ffn_up_bwd(dy, w_down, gu, *, name):
    S, D = dy.shape
    F = w_down.shape[0]
    tm = _pick(S, (512, 256, 128))
    tn = _pick(F, (1408, 1024, 768, 512, 256, 128))

    def body(dy_ref, wd_ref, gu_ref, o_ref):
        da = _dot(dy_ref[...].astype(BF16), wd_ref[...], 1, 1)
        g = gu_ref[0]
        u = gu_ref[1]
        s = _sigmoid(g)
        o_ref[0] = (da * u * (s * (1.0 + g * (1.0 - s)))).astype(BF16)
        o_ref[1] = (da * g * s).astype(BF16)

    pair = pl.BlockSpec((2, tm, tn), lambda j, i: (0, i, j))
    return pl.pallas_call(
        body, name=name, grid=(F // tn, S // tm),
        in_specs=[pl.BlockSpec((tm, D), lambda j, i: (i, 0)), pl.BlockSpec((tn, D), lambda j, i: (j, 0)), pair],
        out_specs=pair, out_shape=jax.ShapeDtypeStruct((2, S, F), BF16),
        compiler_params=_params(("parallel", "parallel")),
    )(dy, w_down, gu)


def loss_head(y, target, *, name):
    S, D = y.shape
    tr = _pick(S, (512, 256, 128))

    def body(y_ref, t_ref, dy_ref, l_ref):
        @pl.when(pl.program_id(0) == 0)
        def _():
            l_ref[...] = jnp.zeros_like(l_ref)

        err = y_ref[...] - t_ref[...]
        dy_ref[...] = err * (1.0 / D)
        l_ref[...] += jnp.sum(0.5 * jnp.mean(err * err, axis=1, keepdims=True), axis=0, keepdims=True)

    row = pl.BlockSpec((tr, D), lambda i: (i, 0))
    one = pl.BlockSpec((1, 1), lambda i: (0, 0))
    dy, l = pl.pallas_call(
        body, name=name, grid=(S // tr,), in_specs=[row, row], out_specs=[row, one],
        out_shape=[jax.ShapeDtypeStruct((S, D), F32), jax.ShapeDtypeStruct((1, 1), F32)],
        compiler_params=_params(("arbitrary",)),
    )(y, target)
    return dy, l


def adamw(w, g, m, v, *, name):
    R, C = w.shape[-2:]
    tr = _pick(R, (512, 256, 128, 64, 32, 16, 8))

    def body(w_ref, g_ref, m_ref, v_ref, d_ref, mo_ref, vo_ref):
        gv = g_ref[...]
        mn = ADAM_B1 * m_ref[...] + (1.0 - ADAM_B1) * gv
        vn = ADAM_B2 * v_ref[...] + (1.0 - ADAM_B2) * jnp.square(gv)
        m_hat = mn / (1.0 - ADAM_B1 ** ADAM_STEP)
        v_hat = vn / (1.0 - ADAM_B2 ** ADAM_STEP)
        d_ref[...] = -ADAM_LR * (m_hat / (jnp.sqrt(v_hat) + ADAM_EPS) + ADAM_WD * w_ref[...])
        mo_ref[...] = mn
        vo_ref[...] = vn

    tc = C if tr < R or C % LANES else _pick(C, (256, 128))
    if w.ndim == 3:
        grid = (w.shape[0], R // tr, C // tc)
        blk = pl.BlockSpec((None, tr, tc), lambda l, i, j: (l, i, j))
    else:
        grid = (R // tr, C // tc)
        blk = pl.BlockSpec((tr, tc), lambda i, j: (i, j))
    sds = jax.ShapeDtypeStruct(w.shape, F32)
    return pl.pallas_call(
        body, name=name, grid=grid, in_specs=[blk] * 4, out_specs=[blk] * 3, out_shape=[sds] * 3,
        compiler_params=_params(("parallel",) * len(grid)),
    )(w, g, m, v)


def _tri(n, kind):
    r = lax.broadcasted_iota(jnp.int32, (n, n), 0)
    c = lax.broadcasted_iota(jnp.int32, (n, n), 1)
    if kind == "row_gt_col":
        return (r > c).astype(BF16)
    if kind == "row_ge_col":
        return (r >= c).astype(BF16)
    if kind == "row_le_col":
        return (r <= c).astype(BF16)
    raise ValueError(kind)


def _sb_tile(qs, kj, r_carry, u_strict, masked):
    z = _dot(qs, kj, 1, 1)
    lb = jnp.minimum(z, 0.0) - jnp.log(1.0 + jnp.exp(-jnp.abs(z)))
    l1m = lb - z
    keep = None
    if masked:
        tq, tk = z.shape
        keep = lax.broadcasted_iota(jnp.int32, (tq, tk), 1) < lax.broadcasted_iota(jnp.int32, (tq, tk), 0)
        l1m = jnp.where(keep, l1m, 0.0)
    w = jnp.exp(lb + _dot(l1m.astype(BF16), u_strict) + r_carry)
    if masked:
        w = jnp.where(keep, w, 0.0)
    return lb, l1m, w, keep


def _sb_prep(T, nb, hd, refs_in, gains, scratch):
    q_scale = 1.0 / math.sqrt(hd)
    assert math.log2(q_scale) == round(math.log2(q_scale))

    def prep(i, _):
        rows = pl.ds(pl.multiple_of(i * T, T), T)
        for hh in range(2):
            sl = slice(hd * hh, hd * hh + hd)
            for n, (src, dst) in enumerate(zip(refs_in, scratch)):
                v = src[rows, sl]
                if n < 2:
                    v = v * lax.rsqrt(jnp.mean(v * v, axis=1, keepdims=True) + EPS) * gains[n][...]
                if n == 0:
                    v = v * q_scale
                dst[hh, rows, :] = v.astype(BF16)
        return 0

    lax.fori_loop(0, nb, prep, 0)


def _sb_chains(m, T, nq):
    rows = [pl.ds(pl.multiple_of((nq * m + qb) * T, T), T) for qb in range(nq)]
    return rows, [(hh, qb) for qb in range(nq) for hh in range(2)]


def _sb_sweep(tile, carry, chains, m, nq):
    for kk in reversed(range(nq)):
        carry = tile(nq * m + kk, carry, [(ch, ch[1] == kk) for ch in chains if ch[1] >= kk])
    return lax.fori_loop(0, nq * m, lambda jj, c: tile(nq * m - 1 - jj, c, [(ch, False) for ch in chains]), carry)


def sb_attn_fwd(qkv, q_gain, k_gain, *, name, gather=None):
    S, D3 = qkv.shape
    D = D3 // 3
    npairs = D // LANES
    hd = SB_HEAD_DIM
    T = min(SB_TILE, S)
    nb = S // T
    nq = SB_FWD_QUERY_BLOCKS
    assert nb % nq == 0

    def body(*refs):
        if gather is None:
            q_ref, k_ref, v_ref, qg_ref, kg_ref, us_ref, o_ref, qn_s, kn_s, vb_s = refs
        else:
            ng = len(gather)
            q_ref, k_ref, v_ref, qg_ref, kg_ref, us_ref = refs[:6]
            o_ref = refs[6 + ng]
            qn_s, kn_s, vb_s = refs[7 + 2 * ng:10 + 2 * ng]
            comm = (refs[6:6 + ng], refs[7 + ng:7 + 2 * ng], refs[10 + 2 * ng:])
            step = pl.program_id(0)
            pl.when(step == 0)(lambda: _ag_phase(*comm, "start"))
            pl.when(step == npairs - 1)(lambda: _ag_phase(*comm, "forward"))
        us = us_ref[...]
        _sb_prep(T, nb, hd, (q_ref, k_ref, v_ref), (qg_ref, kg_ref), (qn_s, kn_s, vb_s))

        def superblock(m, _):
            rows_q, chains = _sb_chains(m, T, nq)
            qs = {ch: qn_s[ch[0], rows_q[ch[1]], :] for ch in chains}

            def tile(j, carry, which):
                rows_j = pl.ds(pl.multiple_of(j * T, T), T)
                new = dict(carry)
                for ch, masked in which:
                    acc, rc = carry[ch]
                    _, l1m, w, _ = _sb_tile(qs[ch], kn_s[ch[0], rows_j, :], rc, us, masked)
                    new[ch] = (acc + _dot(w.astype(BF16), vb_s[ch[0], rows_j, :]),
                               rc + jnp.sum(l1m, axis=1, keepdims=True))
                return new

            carry = {ch: (jnp.zeros((T, hd), F32), jnp.zeros((T, 1), F32)) for ch in chains}
            carry = _sb_sweep(tile, carry, chains, m, nq)
            for qb in range(nq):
                o_ref[rows_q[qb], :] = jnp.concatenate([carry[(0, qb)][0], carry[(1, qb)][0]], axis=1)
            return 0

        lax.fori_loop(0, nb // nq, superblock, 0)
        if gather is not None:
            pl.when(step == npairs - 1)(lambda: _ag_phase(*comm, "finish"))

    col = lambda off: pl.BlockSpec((S, LANES), lambda p, off=off: (0, off + p))
    gain = pl.BlockSpec((1, hd), lambda p: (0, 0))
    in_specs = [col(0), col(npairs), col(2 * npairs), gain, gain, pl.BlockSpec((T, T), lambda p: (0, 0))]
    out_specs = [pl.BlockSpec((S, LANES), lambda p: (0, p))]
    out_shape = [jax.ShapeDtypeStruct((S, D), F32)]
    scratch = [pltpu.VMEM((2, S, hd), BF16)] * 3
    args = [qkv, qkv, qkv, q_gain, k_gain, _tri(T, "row_gt_col")]
    if gather is not None:
        in_specs += [HBM_SPEC] * len(gather)
        out_specs += [HBM_SPEC] * len(gather)
        out_shape += [jax.ShapeDtypeStruct((N_DEV,) + s.shape, s.dtype) for s in gather]
        scratch += _copy_semaphores(len(gather))
        args += list(gather)
    out = pl.pallas_call(
        body, name=name, grid=(npairs,), in_specs=in_specs, out_specs=out_specs, out_shape=out_shape,
        scratch_shapes=scratch, compiler_params=_params(("arbitrary",)),
    )(*args)
    return out[0] if gather is None else (out[0], list(out[1:]))


def sb_attn_bwd(qkv, o, do, q_gain, k_gain, *, name, scatter=None):
    S, D3 = qkv.shape
    D = D3 // 3
    npairs = D // LANES
    hd = SB_HEAD_DIM
    T = min(SB_TILE, S)
    nb = S // T
    nq = SB_BWD_QUERY_BLOCKS
    assert nb % nq == 0
    scale = 1.0 / math.sqrt(hd)

    def body(*refs):
        if scatter is None:
            (q_ref, k_ref, v_ref, o_ref, do_ref, qg_ref, kg_ref, us_ref,
             dqkv_ref, dg_ref, qn_s, kn_s, vb_s, dob_s, acc_s) = refs
        else:
            ns = len(scatter)
            q_ref, k_ref, v_ref, o_ref, do_ref, qg_ref, kg_ref, us_ref = refs[:8]
            rs_in = refs[8:8 + ns]
            dqkv_ref, dg_ref = refs[8 + ns:10 + ns]
            rs_out = refs[10 + ns:10 + 2 * ns]
            qn_s, kn_s, vb_s, dob_s, acc_s = refs[10 + 2 * ns:15 + 2 * ns]
            rs_sems = refs[15 + 2 * ns:]
            pl.when(pl.program_id(0) == 0)(lambda: _rs_phase(rs_in, rs_out, rs_sems, "start"))
        dq_ref, dk_ref, dv_ref = acc_s.at[0], acc_s.at[1], acc_s.at[2]

        @pl.when(pl.program_id(0) == 0)
        def _():
            dg_ref[...] = jnp.zeros_like(dg_ref)

        us = us_ref[...]
        u_prefix = (1.0 - us.astype(F32)).astype(BF16)
        _sb_prep(T, nb, hd, (q_ref, k_ref, v_ref, do_ref), (qg_ref, kg_ref), (qn_s, kn_s, vb_s, dob_s))
        dk_ref[...] = jnp.zeros_like(dk_ref)
        dv_ref[...] = jnp.zeros_like(dv_ref)

        def superblock(m, _):
            rows_q, chains = _sb_chains(m, T, nq)
            qs = {ch: qn_s[ch[0], rows_q[ch[1]], :] for ch in chains}
            doi ={ch: dob_s[ch[0], rows_q[ch[1]], :] for ch in chains}
            dt_total = {ch: jnp.sum(doi[ch].astype(F32) * o_ref[rows_q[ch[1]], hd * ch[0]:hd * ch[0] + hd],
                                    axis=1, keepdims=True) for ch in chains}

            def tile(j, carry, which):
                rows_j = pl.ds(pl.multiple_of(j * T, T), T)
                new = dict(carry)
                dk_part, dv_part = {}, {}
                for ch, masked in which:
                    hh = ch[0]
                    dq_acc, rc, gc = carry[ch]
                    kj = kn_s[hh, rows_j, :]
                    lb, l1m, w, keep = _sb_tile(qs[ch], kj, rc, us, masked)
                    wb = w.astype(BF16)
                    g = _dot(doi[ch], vb_s[hh, rows_j, :], 1, 1) * wb.astype(F32)
                    g_row = jnp.sum(g, axis=1, keepdims=True)
                    g_upto = (dt_total[ch] - gc - g_row) + _dot(g.astype(BF16), u_prefix)
                    dz = g - g_upto * jnp.exp(lb)
                    if masked:
                        dz = jnp.where(keep, dz, 0.0)
                    dzb = dz.astype(BF16)
                    dv_part[hh] = dv_part.get(hh, 0.0) + _dot(wb, doi[ch], 0, 0)
                    dk_part[hh] = dk_part.get(hh, 0.0) + _dot(dzb, qs[ch], 0, 0)
                    new[ch] = (dq_acc + _dot(dzb, kj), rc + jnp.sum(l1m, axis=1, keepdims=True),
                               gc + g_row)
                dv_ref[rows_j, :] += jnp.concatenate([dv_part[0], dv_part[1]], axis=1)
                dk_ref[rows_j, :] += jnp.concatenate([dk_part[0], dk_part[1]], axis=1)
                return new

            zero1 = jnp.zeros((T, 1), F32)
            carry = {ch: (jnp.zeros((T, hd), F32), zero1, zero1) for ch in chains}
            carry = _sb_sweep(tile, carry, chains, m, nq)
            for qb in range(nq):
                dq_ref[rows_q[qb], :] = jnp.concatenate([carry[(0, qb)][0], carry[(1, qb)][0]], axis=1) * scale
            return 0

        lax.fori_loop(0, nb // nq, superblock, 0)

        def finish(i, carry):
            rows = pl.ds(pl.multiple_of(i * T, T), T)
            new = []
            for hh in range(2):
                sl = slice(hd * hh, hd * hh + hd)
                outs = []
                for raw_ref, gain_ref, dn in ((q_ref, qg_ref, dq_ref[rows, sl]), (k_ref, kg_ref, dk_ref[rows, sl])):
                    raw = raw_ref[rows, sl]
                    r = lax.rsqrt(jnp.mean(raw * raw, axis=1, keepdims=True) + EPS)
                    hat = raw * r
                    t = dn * gain_ref[...]
                    outs.append((r * (t - hat * jnp.mean(hat * t, axis=1, keepdims=True)),
                                 jnp.sum(dn * hat, axis=0, keepdims=True)))
                dqkv_ref[0, rows, sl] = outs[0][0].astype(BF16)
                dqkv_ref[1, rows, sl] = outs[1][0].astype(BF16)
                new.append((carry[hh][0] + outs[0][1], carry[hh][1] + outs[1][1]))
            dqkv_ref[2, rows, :] = dv_ref[rows, :].astype(BF16)
            return tuple(new)

        zg = (jnp.zeros((1, hd), F32), jnp.zeros((1, hd), F32))
        tot = lax.fori_loop(0, nb, finish, (zg, zg))
        dg_ref[0:1, 0:hd] += tot[0][0] + tot[1][0]
        dg_ref[1:2, 0:hd] += tot[0][1] + tot[1][1]
        if scatter is not None:
            pl.when(pl.program_id(0) == npairs - 1)(lambda: _rs_phase(rs_in, rs_out, rs_sems, "finish"))

    col = lambda off: pl.BlockSpec((S, LANES), lambda p, off=off: (0, off + p))
    gain = pl.BlockSpec((1, hd), lambda p: (0, 0))
    tri = pl.BlockSpec((T, T), lambda p: (0, 0))
    pair = pl.BlockSpec((S, LANES), lambda p: (0, p))
    in_specs = [col(0), col(npairs), col(2 * npairs), pair, pair, gain, gain, tri]
    out_specs = [pl.BlockSpec((3, S, LANES), lambda p: (0, 0, p)), pl.BlockSpec((8, LANES), lambda p: (0, 0))]
    out_shape = [jax.ShapeDtypeStruct((3, S, D), BF16), jax.ShapeDtypeStruct((8, LANES), F32)]
    scratch = [pltpu.VMEM((2, S, hd), BF16)] * 4 + [pltpu.VMEM((3, S, LANES), F32)]
    args = [qkv, qkv, qkv, o, do, q_gain, k_gain, _tri(T, "row_gt_col")]
    if scatter is not None:
        in_specs += [HBM_SPEC] * len(scatter)
        out_specs += [HBM_SPEC] * len(scatter)
        out_shape += [jax.ShapeDtypeStruct(g.shape, g.dtype) for g in scatter]
        scratch += _copy_semaphores(len(scatter))
        args += list(scatter)
    out = pl.pallas_call(
        body, name=name, grid=(npairs,), in_specs=in_specs, out_specs=out_specs, out_shape=out_shape,
        scratch_shapes=scratch, compiler_params=_params(("arbitrary",)),
    )(*args)
    res = (out[0], out[1][0:1, :hd], out[1][1:2, :hd])
    return res if scatter is None else res + (list(out[2:]),)


def gmlp_fwd(zzpre, b_in, v_gain, wc, bsf, *, name):
    S, H2 = zzpre.shape
    H = H2 // 2
    G, T, _ = wc.shape
    gd = H // G

    def body(z_ref, b_ref, vg_ref, wc_ref, bs_ref, p_ref):
        zz, _ = _gelu_and_grad(z_ref[...] + b_ref[...])
        u = zz[:, :H]
        v = zz[:, H:]
        vn = v * lax.rsqrt(jnp.mean(v * v, axis=1, keepdims=True) + EPS) * vg_ref[...]
        for g in range(G):
            gs = slice(g * gd, (g + 1) * gd)
            mixed = _dot(wc_ref[g], vn[:, gs].astype(BF16)) + bs_ref[g]
            p_ref[:, gs] = (u[:, gs] * mixed).astype(BF16)

    full3 = lambda shp: pl.BlockSpec(shp, lambda c: (0, 0, 0))
    return pl.pallas_call(
        body, name=name, grid=(S // T,),
        in_specs=[pl.BlockSpec((T, H2), lambda c: (c, 0)), pl.BlockSpec((1, H2), lambda c: (0, 0)),
                  pl.BlockSpec((1, H), lambda c: (0, 0)), full3((G, T, T)), full3((G, T, gd))],
        out_specs=pl.BlockSpec((T, H), lambda c: (c, 0)), out_shape=jax.ShapeDtypeStruct((S, H), BF16),
        compiler_params=_params(("parallel",)),
    )(zzpre, b_in, v_gain, wc, bsf)


def gmlp_bwd(zzpre, b_in, v_gain, wc, bsf, dp, *, name):
    S, H2 = zzpre.shape
    H = H2 // 2
    G, T, _ = wc.shape
    gd = H // G
    assert G <= LANES

    def body(z_ref, b_ref, vg_ref, wc_ref, bs_ref, dp_ref, dzz_ref, db_ref, dvg_ref, dws_ref, dbs_ref):
        @pl.when(pl.program_id(0) == 0)
        def _():
            db_ref[...] = jnp.zeros_like(db_ref)
            dvg_ref[...] = jnp.zeros_like(dvg_ref)
            dws_ref[...] = jnp.zeros_like(dws_ref)
            dbs_ref[...] = jnp.zeros_like(dbs_ref)

        zz, gp = _gelu_and_grad(z_ref[...] + b_ref[...])
        u = zz[:, :H]
        v = zz[:, H:]
        r = lax.rsqrt(jnp.mean(v * v, axis=1, keepdims=True) + EPS)
        vhat = v * r
        vg = vg_ref[...]
        vn = vhat * vg
        dpv = dp_ref[...]
        tril = lax.broadcasted_iota(jnp.int32, (T, T), 1) <= lax.broadcasted_iota(jnp.int32, (T, T), 0)
        lane = lax.broadcasted_iota(jnp.int32, (T, LANES), 1)
        dbs = jnp.zeros((T, LANES), F32)
        du_parts, dvn_parts = [], []
        for g in range(G):
            gs = slice(g * gd, (g + 1) * gd)
            vng = vn[:, gs].astype(BF16)
            wcg = wc_ref[g]
            mixed = _dot(wcg, vng) + bs_ref[g]
            dpg = dpv[:, gs]
            du_parts.append(dpg * mixed)
            dmx = dpg * u[:, gs]
            dmxb = dmx.astype(BF16)
            dvn_parts.append(_dot(wcg, dmxb, 0, 0))
            dws_ref[g] += jnp.where(tril, _dot(dmxb, vng, 1, 1), 0.0)
            dbs = dbs + jnp.where(lane == g, jnp.sum(dmx, axis=1, keepdims=True), 0.0)
        dbs_ref[...] += dbs
        du = jnp.concatenate(du_parts, axis=1)
        dvn = jnp.concatenate(dvn_parts, axis=1)
        dvg_ref[...] += jnp.sum(dvn * vhat, axis=0, keepdims=True)
        t = dvn * vg
        dv = r * (t - vhat * jnp.mean(vhat * t, axis=1, keepdims=True))
        dzu = du * gp[:, :H]
        dzv = dv * gp[:, H:]
        dzz_ref[:, :H] = dzu.astype(BF16)
        dzz_ref[:, H:] = dzv.astype(BF16)
        db_ref[:, :H] += jnp.sum(dzu, axis=0, keepdims=True)
        db_ref[:, H:] += jnp.sum(dzv, axis=0, keepdims=True)

    full3 = lambda shp: pl.BlockSpec(shp, lambda c: (0, 0, 0))
    vec = lambda n: pl.BlockSpec((1, n), lambda c: (0, 0))
    return pl.pallas_call(
        body, name=name, grid=(S // T,),
        in_specs=[pl.BlockSpec((T, H2), lambda c: (c, 0)), vec(H2), vec(H), full3((G, T, T)), full3((G, T, gd)),
                  pl.BlockSpec((T, H), lambda c: (c, 0))],
        out_specs=[pl.BlockSpec((T, H2), lambda c: (c, 0)), vec(H2), vec(H), full3((G, T, T)),
                   pl.BlockSpec((T, LANES), lambda c: (0, 0))],
        out_shape=[jax.ShapeDtypeStruct((S, H2), BF16), jax.ShapeDtypeStruct((1, H2), F32),
                   jax.ShapeDtypeStruct((1, H), F32), jax.ShapeDtypeStruct((G, T, T), F32),
                   jax.ShapeDtypeStruct((T, LANES), F32)],
        compiler_params=_params(("arbitrary",)),
    )(zzpre, b_in, v_gain, wc, bsf, dp)


def _shift_rows(v, k, n_rows):
    if k == 0:
        return v
    rolled = pltpu.roll(v, k % n_rows, 0)
    row = lax.broadcasted_iota(jnp.int32, v.shape, 0)
    keep = (row >= k) if k > 0 else (row < n_rows + k)
    return jnp.where(keep, rolled, 0.0)


def conv_fwd(zx, conv_w, conv_b, col0, *, name):
    S = zx.shape[0]
    C = conv_w.shape[1]
    tc = _pick(C, (256, 128))
    off = col0 // tc
    assert col0 % tc == 0

    def body(x_ref, w_ref, b_ref, o_ref):
        xv = x_ref[...]
        acc = b_ref[...] + w_ref[SSM_CONV - 1:SSM_CONV, :] * xv
        for k in range(SSM_CONV - 1):
            acc = acc + w_ref[k:k + 1, :] * _shift_rows(xv, SSM_CONV - 1 - k, S)
        o_ref[...] = acc * _sigmoid(acc)

    return pl.pallas_call(
        body, name=name, grid=(C // tc,),
        in_specs=[pl.BlockSpec((S, tc), lambda j: (0, off + j)), pl.BlockSpec((SSM_CONV, tc), lambda j: (0, j)),
                  pl.BlockSpec((1, tc), lambda j: (0, j))],
        out_specs=pl.BlockSpec((S, tc), lambda j: (0, j)), out_shape=jax.ShapeDtypeStruct((S, C), F32),
        compiler_params=_params(("parallel",)),
    )(zx, conv_w, conv_b)


def conv_bwd(zx, conv_w, conv_b, col0, douts, *, name):
    S = zx.shape[0]
    C = conv_w.shape[1]
    tc = LANES
    off = col0 // tc
    counts = [d.shape[1] // tc for d in douts]
    starts = [sum(counts[:p]) for p in range(len(douts))]
    assert sum(counts) * tc == C and all(d.shape[1] % tc == 0 for d in douts)

    def body(x_ref, w_ref, b_ref, *rest):
        do_refs, (dx_ref, dw_ref, db_ref) = rest[:len(douts)], rest[len(douts):]
        j = pl.program_id(0)
        dov = do_refs[-1][...]
        for p in reversed(range(len(douts) - 1)):
            dov = jnp.where(j < starts[p + 1], do_refs[p][...], dov)
        xv = x_ref[...]
        shifted = [_shift_rows(xv, SSM_CONV - 1 - k, S) for k in range(SSM_CONV)]
        acc = b_ref[...]
        for k in range(SSM_CONV):
            acc = acc + w_ref[k:k + 1, :] * shifted[k]
        s = _sigmoid(acc)
        dacc = dov * (s * (1.0 + acc * (1.0 - s)))
        db_ref[...] = jnp.sum(dacc, axis=0, keepdims=True)
        dx = jnp.zeros_like(xv)
        for k in range(SSM_CONV):
            dw_ref[k:k + 1, :] = jnp.sum(dacc * shifted[k], axis=0, keepdims=True)
            dx = dx + w_ref[k:k + 1, :] * _shift_rows(dacc, -(SSM_CONV - 1 - k), S)
        dx_ref[...] = dx.astype(BF16)

    slab = pl.BlockSpec((S, tc), lambda j: (0, j))
    piece_specs = [pl.BlockSpec((S, tc), lambda j, a=starts[p], n=counts[p]: (0, jnp.clip(j - a, 0, n - 1)))
                   for p in range(len(douts))]
    return pl.pallas_call(
        body, name=name, grid=(C // tc,),
        in_specs=[pl.BlockSpec((S, tc), lambda j: (0, off + j)), pl.BlockSpec((SSM_CONV, tc), lambda j: (0, j)),
                  pl.BlockSpec((1, tc), lambda j: (0, j))] + piece_specs,
        out_specs=[slab, pl.BlockSpec((SSM_CONV, tc), lambda j: (0, j)), pl.BlockSpec((1, tc), lambda j: (0, j))],
        out_shape=[jax.ShapeDtypeStruct((S, C), BF16), jax.ShapeDtypeStruct((SSM_CONV, C), F32),
                   jax.ShapeDtypeStruct((1, C), F32)],
        compiler_params=_params(("arbitrary",)),
    )(zx, conv_w, conv_b, *douts)


def _ssd_chunk_terms(dtraw, bias, a_log, tl):
    dt = _softplus(dtraw + bias)
    a_neg = -jnp.exp(a_log)
    ac = _dot_exact_left(tl, dt * a_neg)
    ac_last = ac[ac.shape[0] - 1:, :]
    return dt, a_neg, ac, ac.T, jnp.exp(ac), jnp.exp(ac_last - ac), jnp.exp(ac_last)


def _ssd_specs(S, L, G, hpg, pd, inner, gp):
    gw = hpg * pd
    n = SSM_STATE
    xb = inner // n
    assert G % gp == 0 and xb % gp == 0 and (xb + G) % gp == 0

    def mk(cidx):
        return dict(
            x=pl.BlockSpec((L, gp * gw), lambda g, c: (cidx(c), g)),
            b=pl.BlockSpec((L, gp * n), lambda g, c: (cidx(c), xb // gp + g)),
            c=pl.BlockSpec((L, gp * n), lambda g, c: (cidx(c), (xb + G) // gp + g)),
            z=pl.BlockSpec((L, gp * gw), lambda g, c: (cidx(c), g)),
            dt=pl.BlockSpec((L, gp * LANES), lambda g, c: (cidx(c), g)),
            gvec=pl.BlockSpec((gp, 1, LANES), lambda g, c: (g, 0, 0)),
            chan=pl.BlockSpec((1, gp * gw), lambda g, c: (0, g)),
            tri=pl.BlockSpec((L, L), lambda g, c: (0, 0)),
            hp=pl.BlockSpec((gp, 1, gw, n), lambda g, c: (g, cidx(c), 0, 0)),
            bc=pl.BlockSpec((L, gp * n), lambda g, c: (cidx(c), g)),
        )
    return mk


def _ssd_group_views(refs, kinds, gg, gw):
    n = SSM_STATE
    width = dict(x=gw, z=gw, chan=gw, b=n, c=n, bc=n, dt=LANES)
    out = []
    for ref, kind in zip(refs, kinds):
        if kind in width:
            out.append(ref.at[:, gg * width[kind]:(gg + 1) * width[kind]])
        elif kind in ("gvec", "hp"):
            out.append(ref.at[gg:gg + 1])
        elif kind == "state":
            out.append(ref.at[gg * gw:(gg + 1) * gw])
        else:
            out.append(ref)
    return out


def ssd_fwd(xbc, zx, dtg, bias_g, alog_g, d_chan, ngain, L, G, *, name):
    S = xbc.shape[0]
    n = SSM_STATE
    inner = xbc.shape[1] - 2 * G * n
    gw = inner // G
    pd = SB_HEAD_DIM
    hpg = gw // pd
    nc = S // L
    gp = SSD_FWD_GROUPS_PER_STEP
    sp = _ssd_specs(S, L, G, hpg, pd, inner, gp)(lambda c: c)

    kinds = ("x", "b", "c", "z", "dt", "gvec", "gvec", "chan", "chan", "tri", "x", "x", "hp", "state")

    def body(*refs):
        @pl.when(pl.program_id(1) == 0)
        def _():
            refs[-1][...] = jnp.zeros_like(refs[-1])

        for gg in range(gp):
            group_body(*_ssd_group_views(refs, kinds, gg, gw))

    def group_body(x_ref, b_ref, c_ref, z_ref, dt_ref, bias_ref, alog_ref, d_ref, ng_ref, tl_ref,
                   yn_ref, y_ref, hp_ref, state):
        dt, _, ac, act, ea, dte, cd = _ssd_chunk_terms(dt_ref[...], bias_ref[0], alog_ref[0], tl_ref[...])
        xv = x_ref[...]
        bm = b_ref[...].astype(BF16)
        cm = c_ref[...].astype(BF16)
        cb = _dot(cm, bm, 1, 1)
        tril = lax.broadcasted_iota(jnp.int32, (L, L), 1) <= lax.broadcasted_iota(jnp.int32, (L, L), 0)
        hp_ref[0, 0] = state[...]
        for r in range(hpg):
            ps = slice(r * pd, (r + 1) * pd)
            xr = xv[:, ps]
            xdt = xr * dt[:, r:r + 1]
            lm = jnp.exp(jnp.where(tril, ac[:, r:r + 1] - act[r:r + 1, :], -jnp.inf))
            hprev = state[ps, :]
            y = _dot((cb * lm).astype(BF16), xdt.astype(BF16))
            y = y + _dot(cm, hprev.astype(BF16), 1, 1) * ea[:, r:r + 1]
            y_ref[:, ps] = y + xr * d_ref[:, ps]
            st = _dot((xdt * dte[:, r:r + 1]).astype(BF16), bm, 0, 0)
            state[ps, :] = hprev * cd[:, r:r + 1] + st
        yfull = y_ref[...]
        zg = z_ref[...]
        yg = yfull * (zg * _sigmoid(zg))
        yn_ref[...] = (yg * lax.rsqrt(jnp.mean(yg * yg, axis=1, keepdims=True) + EPS) * ng_ref[...]).astype(BF16)

    return pl.pallas_call(
        body, name=name, grid=(G // gp, nc),
        in_specs=[sp["x"], sp["b"], sp["c"], sp["z"], sp["dt"], sp["gvec"], sp["gvec"], sp["chan"], sp["chan"], sp["tri"]],
        out_specs=[sp["x"], sp["x"], sp["hp"]],
        out_shape=[jax.ShapeDtypeStruct((S, inner), BF16), jax.ShapeDtypeStruct((S, inner), F32),
                   jax.ShapeDtypeStruct((G, nc, gw, n), F32)],
        scratch_shapes=[pltpu.VMEM((gp * gw, n), F32)],
        compiler_params=_params(("arbitrary", "arbitrary")),
    )(xbc, xbc, xbc, zx, dtg, bias_g, alog_g, d_chan, ngain, _tri(L, "row_ge_col"))


def ssd_bwd(xbc, zx, dtg, bias_g, alog_g, d_chan, ngain, yfull, hp, dyn, L, G, *, name):
    S = xbc.shape[0]
    n = SSM_STATE
    inner = xbc.shape[1] - 2 * G * n
    gw = inner // G
    pd = SB_HEAD_DIM
    hpg = gw // pd
    nc = S // L
    gp = SSD_BWD_GROUPS_PER_STEP
    sp = _ssd_specs(S, L, G, hpg, pd, inner, gp)(lambda c: nc - 1 - c)

    kinds = ("x", "b", "c", "z", "dt", "gvec", "gvec", "chan", "chan", "tri", "tri", "x", "hp", "x",
             "x", "x", "bc", "bc", "dt", "gvec", "gvec", "gvec", "chan", "state")

    def body(*refs):
        @pl.when(pl.program_id(1) == 0)
        def _():
            for acc in refs[-5:]:
                acc[...] = jnp.zeros_like(acc)

        for gg in range(gp):
            group_body(*_ssd_group_views(refs, kinds, gg, gw))

    def group_body(x_ref, b_ref, c_ref, z_ref, dt_ref, bias_ref, alog_ref, d_ref, ng_ref, tl_ref, tu_ref,
                   yf_ref, hp_ref, dyn_ref,
                   dz_ref, dx_ref, db_ref, dc_ref, ddt_ref, dbias_ref, dalog_ref, dd_ref, dng_ref, dstate):

        dtraw = dt_ref[...]
        dt, a_neg, ac, act, ea, dte, cd = _ssd_chunk_terms(dtraw, bias_ref[0], alog_ref[0], tl_ref[...])
        xv = x_ref[...]
        bm = b_ref[...].astype(BF16)
        cm = c_ref[...].astype(BF16)
        cb = _dot(cm, bm, 1, 1)
        tril = lax.broadcasted_iota(jnp.int32, (L, L), 1) <= lax.broadcasted_iota(jnp.int32, (L, L), 0)
        lane = lax.broadcasted_iota(jnp.int32, (L, LANES), 1)
        lane1 = lax.broadcasted_iota(jnp.int32, (1, LANES), 1)

        yfull = yf_ref[...]
        zg = z_ref[...]
        sg = _sigmoid(zg)
        gate = zg * sg
        yg = yfull * gate
        rr = lax.rsqrt(jnp.mean(yg * yg, axis=1, keepdims=True) + EPS)
        yhat = yg * rr
        dynv = dyn_ref[...]
        dng_ref[...] += jnp.sum(dynv * yhat, axis=0, keepdims=True)
        t = dynv * ng_ref[...]
        dyg = rr * (t - yhat * jnp.mean(yhat * t, axis=1, keepdims=True))
        dy = dyg * gate
        dz_ref[...] = (dyg * yfull * (sg * (1.0 + zg * (1.0 - sg)))).astype(BF16)

        dcb = jnp.zeros((L, L), F32)
        dc_acc = jnp.zeros((L, n), F32)
        db_acc = jnp.zeros((L, n), F32)
        dac = jnp.zeros((L, LANES), F32)
        xdx = jnp.zeros((L, LANES), F32)
        tail = jnp.zeros((1, LANES), F32)
        dskip = jnp.zeros((1, LANES), F32)
        ones_l = jnp.ones((L, LANES), BF16)
        for r in range(hpg):
            ps = slice(r * pd, (r + 1) * pd)
            xr = xv[:, ps]
            dyr = dy[:, ps]
            dtr = dt[:, r:r + 1]
            dter = dte[:, r:r + 1]
            cdr = cd[:, r:r + 1]
            xdt = xr * dtr
            xdtb = xdt.astype(BF16)
            dyrb = dyr.astype(BF16)
            lm = jnp.exp(jnp.where(tril, ac[:, r:r + 1] - act[r:r + 1, :], -jnp.inf))
            m32 = cb * lm
            mb = m32.astype(BF16)
            hprev = hp_ref[0, 0, ps, :]
            hpb = hprev.astype(BF16)
            dhn = dstate[ps, :]
            dhnb = dhn.astype(BF16)
            ear = ea[:, r:r + 1]
            gy = (dyr * ear).astype(BF16)
            dc_acc = dc_acc + _dot(gy, hpb)
            dstate[ps, :] = _dot(gy, cm, 0, 0) + dhn * cdr
            bdh = _dot(bm, dhnb, 1, 1)
            db_acc = db_acc + _dot((xdt * dter).astype(BF16), dhnb)
            dm = _dot(dyrb, xdtb, 1, 1)
            dxdt = bdh * dter + _dot(mb, dyrb, 0, 0)
            dcb = dcb + dm * lm
            wmat = dm * m32
            whi = wmat.astype(BF16)
            wlo = (wmat - whi.astype(F32)).astype(BF16)
            col_w = _dot(whi, ones_l, 0, 0) + _dot(wlo, ones_l, 0, 0)
            t_end = xdt * bdh * dter
            e_r = jnp.sum(wmat, axis=1, keepdims=True) \
                + jnp.sum(dyr * _dot(cm, hpb, 1, 1) * ear - t_end, axis=1, keepdims=True)
            c_r = cdr * _sum_all(dhn * hprev) + _sum_all(t_end)
            dac = dac + jnp.where(lane == r, e_r - col_w, 0.0)
            xdx = xdx + jnp.where(lane == r, jnp.sum(dxdt * xr, axis=1, keepdims=True), 0.0)
            tail = tail + jnp.where(lane1 == r, c_r, 0.0)
            dskip = dskip + jnp.where(lane1 == r, _sum_all(dyr * xr), 0.0)
            dx_ref[:, ps] = dxdt * dtr + dyr * d_ref[:, ps]
        dcbb = dcb.astype(BF16)
        dc_ref[...] = dc_acc + _dot(dcbb, bm)
        db_ref[...] = db_acc + _dot(dcbb, cm, 0, 0)
        da = _dot_exact_left(tu_ref[...], dac) + tail
        real = lane < hpg
        ddt = jnp.where(real, (da * a_neg + xdx) * _sigmoid(dtraw + bias_ref[0]), 0.0)
        ddt_ref[...] = ddt
        dd_ref[0] += dskip
        dbias_ref[0] += jnp.sum(ddt, axis=0, keepdims=True)
        dalog_ref[0] += jnp.where(lane1 < hpg, jnp.sum(da * dt, axis=0, keepdims=True) * a_neg, 0.0)

    return pl.pallas_call(
        body, name=name, grid=(G // gp, nc),
        in_specs=[sp["x"], sp["b"], sp["c"], sp["z"], sp["dt"], sp["gvec"], sp["gvec"], sp["chan"], sp["chan"],
                  sp["tri"], sp["tri"], sp["x"], sp["hp"], sp["x"]],
        out_specs=[sp["x"], sp["x"], sp["bc"], sp["bc"], sp["dt"], sp["gvec"], sp["gvec"], sp["gvec"], sp["chan"]],
        out_shape=[jax.ShapeDtypeStruct((S, inner), BF16), jax.ShapeDtypeStruct((S, inner), F32),
                   jax.ShapeDtypeStruct((S, G * n), F32), jax.ShapeDtypeStruct((S, G * n), F32),
                   jax.ShapeDtypeStruct((S, G * LANES), F32), jax.ShapeDtypeStruct((G, 1, LANES), F32),
                   jax.ShapeDtypeStruct((G, 1, LANES), F32), jax.ShapeDtypeStruct((G, 1, LANES), F32),
                   jax.ShapeDtypeStruct((1, inner), F32)],
        scratch_shapes=[pltpu.VMEM((gp * gw, n), F32)],
        compiler_params=_params(("arbitrary", "arbitrary")),
    )(xbc, xbc, xbc, zx, dtg, bias_g, alog_g, d_chan, ngain, _tri(L, "row_ge_col"), _tri(L, "row_le_col"),
      yfull, hp, dyn)


def _spread_dt(w_dt_t, G, hpg):
    K = w_dt_t.shape[1]
    w = w_dt_t.reshape(G, hpg, K)
    return jnp.pad(w, ((0, 0), (0, LANES - hpg), (0, 0))).reshape(G * LANES, K)


def _group_vec(v, G, hpg):
    return jnp.pad(v.reshape(G, 1, hpg), ((0, 0), (0, 0), (0, LANES - hpg)))


def local_step(x, target, W, late=None):
    S, D = x.shape
    depth = W["mix_norm"].shape[0]
    gm_groups, gm_chunk = W["gm_w_s"].shape[1], W["gm_w_s"].shape[2]
    heads = W["ssm_dt_bias"].shape[1]
    inner = heads * SB_HEAD_DIM
    L = gm_chunk
    received = None

    saved = []
    for i in range(depth):
        kind, j = i % 3, i // 3
        s = dict(x=x)
        h = rms_fwd(x, W["mix_norm"][i:i + 1], name="rms_mix_fwd")
        s["h"] = h
        if kind == 0:
            qkv = matmul(h, W["sb_w_qkv"][j], tb=True, name="mm_qkv")
            if late is not None and i == 0:
                o, gathered = sb_attn_fwd(qkv, W["sb_q_gain"][j:j + 1], W["sb_k_gain"][j:j + 1], name="sb_fwd_gather",
                                          gather=late.shards)
                late.fill(W, gathered)
            else:
                o = sb_attn_fwd(qkv, W["sb_q_gain"][j:j + 1], W["sb_k_gain"][j:j + 1], name="sb_fwd")
            x1 = matmul(o, W["sb_w_o"][j], residual=x, name="mm_sb_out")
            s.update(qkv=qkv, o=o)
        elif kind == 1:
            wc = jnp.where(jnp.tril(jnp.ones((gm_chunk, gm_chunk), bool)), W["gm_w_s"][j], 0.0).astype(BF16)
            bsf = jnp.broadcast_to(W["gm_b_s"][j][:, :, None], (gm_groups, gm_chunk, W["gm_v_gain"].shape[1] // gm_groups)).astype(F32)
            zzpre = matmul(h, W["gm_w_in"][j], tb=True, name="mm_gm_in")
            p = gmlp_fwd(zzpre, W["gm_b_in"][j:j + 1], W["gm_v_gain"][j:j + 1], wc, bsf, name="gm_fwd")
            x1 = matmul(p, W["gm_w_out"][j], residual=x, name="mm_gm_out")
            s.update(zzpre=zzpre, p=p, wc=wc, bsf=bsf)
        else:
            conv_dim = W["ssm_conv_w"].shape[2]
            G = (conv_dim - inner) // (2 * SSM_STATE)
            hpg = heads // G
            w_in = W["ssm_w_in"][j]
            w_zx = w_in[:inner + conv_dim]
            w_dtg = _spread_dt(w_in[inner + conv_dim:], G, hpg)
            bias_g = _group_vec(W["ssm_dt_bias"][j], G, hpg)
            alog_g = _group_vec(W["ssm_a_log"][j], G, hpg)
            d_chan = jnp.repeat(W["ssm_d"][j], SB_HEAD_DIM)[None, :]
            ngain = W["ssm_norm_gain"][j:j + 1]
            zx = matmul(h, w_zx, tb=True, name="mm_ssm_zx")
            dtg = matmul(h, w_dtg, tb=True, name="mm_ssm_dt")
            xbc = conv_fwd(zx, W["ssm_conv_w"][j], W["ssm_conv_b"][j:j + 1], inner, name="conv_fwd")
            yn, yfull, hp = ssd_fwd(xbc, zx, dtg, bias_g, alog_g, d_chan, ngain, L, G, name="ssd_fwd")
            x1 = matmul(yn, W["ssm_w_out"][j], residual=x, name="mm_ssm_out")
            s.update(w_zx=w_zx, w_dtg=w_dtg, bias_g=bias_g, alog_g=alog_g, d_chan=d_chan, ngain=ngain,
                     zx=zx, dtg=dtg, xbc=xbc, yn=yn, yfull=yfull, hp=hp)
        h2 = rms_fwd(x1, W["ffn_norm"][i:i + 1], name="rms_ffn_fwd")
        gu, a = ffn_up_fwd(h2, W["ffn_w_gu"][i], name="ffn_up_fwd")
        x2 = matmul(a, W["ffn_w_down"][i], residual=x1, name="mm_ffn_down")
        s.update(x1=x1, h2=h2, gu=gu, a=a)
        saved.append(s)
        x = x2

    dx, loss = loss_head(x, target, name="loss_head")

    gw = {k: {} for k in WEIGHTS}
    for i in reversed(range(depth)):
        kind, j = i % 3, i // 3
        s = saved[i]
        gw["ffn_w_down"][i] = matmul(s["a"], dx, ta=True, out_dtype=BF16, name="mm_ffn_dwdown")
        dgu = ffn_up_bwd(dx, W["ffn_w_down"][i], s["gu"], name="ffn_up_bwd")
        dx1, dgn = matmul(dgu, W["ffn_w_gu"][i], a_split=2, norm_bwd=(s["x1"], W["ffn_norm"][i:i + 1], dx),
                          name="mm_ffn_dh")
        gw["ffn_w_gu"][i] = matmul(dgu, s["h2"], ta=True, a_split=2, out_dtype=BF16, name="mm_ffn_dwgu")
        gw["ffn_norm"][i] = dgn[0]
        mix_norm_bwd = (s["x"], W["mix_norm"][i:i + 1], dx1)
        if kind == 0:
            do = matmul(dx1, W["sb_w_o"][j], tb=True, out_dtype=BF16, name="mm_sb_do")
            gw["sb_w_o"][j] = matmul(s["o"], dx1, ta=True, out_dtype=BF16, name="mm_sb_dwo")
            if late is not None and i == 0:
                dqkv, dqg, dkg, received = sb_attn_bwd(
                    s["qkv"], s["o"], do, W["sb_q_gain"][j:j + 1], W["sb_k_gain"][j:j + 1], name="sb_bwd_scatter",
                    scatter=late.contributions(gw))
            else:
                dqkv, dqg, dkg = sb_attn_bwd(s["qkv"], s["o"], do, W["sb_q_gain"][j:j + 1], W["sb_k_gain"][j:j + 1],
                                             name="sb_bwd")
            gw["sb_q_gain"][j] = dqg[0]
            gw["sb_k_gain"][j] = dkg[0]
            dx, dgn = matmul(dqkv, W["sb_w_qkv"][j], a_split=3, norm_bwd=mix_norm_bwd, name="mm_sb_dh")
            gw["sb_w_qkv"][j] = matmul(dqkv, s["h"], ta=True, a_split=3, out_dtype=BF16, name="mm_sb_dwqkv")
        elif kind == 1:
            dp = matmul(dx1, W["gm_w_out"][j], tb=True, name="mm_gm_dp")
            gw["gm_w_out"][j] = matmul(s["p"], dx1, ta=True, out_dtype=BF16, name="mm_gm_dwout")
            dzz, db_in, dvg, dws, dbs = gmlp_bwd(s["zzpre"], W["gm_b_in"][j:j + 1], W["gm_v_gain"][j:j + 1],
                                                s["wc"], s["bsf"], dp, name="gm_bwd")
            gw["gm_b_in"][j] = db_in[0]
            gw["gm_v_gain"][j] = dvg[0]
            gw["gm_w_s"][j] = dws
            gw["gm_b_s"][j] = dbs[:, :gm_groups].T
            dx, dgn = matmul(dzz, W["gm_w_in"][j], norm_bwd=mix_norm_bwd, name="mm_gm_dh")
            gw["gm_w_in"][j] = matmul(dzz, s["h"], ta=True, out_dtype=BF16, name="mm_gm_dwin")
        else:
            conv_dim = W["ssm_conv_w"].shape[2]
            G = (conv_dim - inner) // (2 * SSM_STATE)
            hpg = heads // G
            dyn = matmul(dx1, W["ssm_w_out"][j], tb=True, name="mm_ssm_dyn")
            gw["ssm_w_out"][j] = matmul(s["yn"], dx1, ta=True, out_dtype=BF16, name="mm_ssm_dwout")
            dz, dxs, dbm, dcm, ddt, dbias, dalog, dd, dng = ssd_bwd(
                s["xbc"], s["zx"], s["dtg"], s["bias_g"], s["alog_g"], s["d_chan"], s["ngain"], s["yfull"], s["hp"],
                dyn, L, G, name="ssd_bwd")
            dpre, dcw, dcb = conv_bwd(s["zx"], W["ssm_conv_w"][j], W["ssm_conv_b"][j:j + 1], inner, [dxs, dbm, dcm],
                                      name="conv_bwd")
            dzx = jnp.concatenate([dz, dpre], axis=1)
            dh = matmul(ddt, s["w_dtg"], name="mm_ssm_dh_dt")
            dx, dgn = matmul(dzx, s["w_zx"], residual=dh, norm_bwd=mix_norm_bwd, name="mm_ssm_dh")
            dw_zx = matmul(dzx, s["h"], ta=True, out_dtype=BF16, name="mm_ssm_dwzx")
            dw_dtg = matmul(ddt, s["h"], ta=True, out_dtype=BF16, name="mm_ssm_dwdt")
            dw_dt = dw_dtg.reshape(G, LANES, D)[:, :hpg, :].reshape(heads, D)
            gw["ssm_w_in"][j] = jnp.concatenate([dw_zx, dw_dt], axis=0)
            gw["ssm_conv_w"][j] = dcw
            gw["ssm_conv_b"][j] = dcb[0]
            gw["ssm_dt_bias"][j] = dbias[:, 0, :hpg].reshape(heads)
            gw["ssm_a_log"][j] = dalog[:, 0, :hpg].reshape(heads)
            gw["ssm_d"][j] = dd[:, 0, :hpg].reshape(heads)
            gw["ssm_norm_gain"][j] = dng[0]
        gw["mix_norm"][i] = dgn[0]

    return loss, dx, gw, received


MESH = pl.DeviceIdType.MESH
HBM_SPEC = pl.BlockSpec(memory_space=pltpu.HBM)
VMEM_SPEC = pl.BlockSpec(memory_space=pltpu.VMEM)


def _my_position():
    return lax.axis_index("x"), lax.axis_index("y"), lax.axis_index("c")


def _flip(v, bit):
    return 1 - v if bit else v


def all_gather(shards, *, name):
    n = len(shards)

    def body(*refs):
        for phase in ("start", "forward", "finish"):
            _ag_phase(refs[:n], refs[n:2 * n], refs[2 * n:], phase)

    return pl.pallas_call(
        body, name=name, out_shape=[jax.ShapeDtypeStruct((N_DEV,) + s.shape, s.dtype) for s in shards],
        in_specs=[HBM_SPEC] * n, out_specs=[HBM_SPEC] * n, scratch_shapes=_copy_semaphores(n),
    )(*shards)


def _ag_phase(x_refs, out_refs, sems, phase):
    send_sems, recv_sems, local_sems = sems
    x, y, c = _my_position()
    me, sibling = (x, y, c), (x, y, 1 - c)
    chips = [(1 - x, y), (x, 1 - y), (1 - x, 1 - y)]
    for p, (x_ref, out_ref) in enumerate(zip(x_refs, out_refs)):
        def slot(px, py, pc):
            return out_ref.at[4 * px + 2 * py + pc]

        def copy(k, block, to, src=None):
            return pltpu.make_async_remote_copy(
                src_ref=slot(*block) if src is None else src, dst_ref=slot(*block),
                send_sem=send_sems.at[7 * p + k], recv_sem=recv_sems.at[7 * p + k], device_id=to, device_id_type=MESH)

        mine = pltpu.make_async_copy(x_ref, slot(*me), local_sems.at[p])
        first = [copy(0, me, sibling, src=x_ref)]
        first += [copy(1 + j, me, (*chip, c), src=x_ref) for j, chip in enumerate(chips)]
        passed = [copy(4 + j, (*chip, c), sibling) for j, chip in enumerate(chips)]
        if phase == "start":
            mine.start()
            for cp in first:
                cp.start()
        elif phase == "forward":
            for j, chip in enumerate(chips):
                copy(1 + j, (*chip, c), me).wait_recv()
                passed[j].start()
        else:
            copy(0, sibling, me).wait_recv()
            for j, chip in enumerate(chips):
                copy(4 + j, (*chip, 1 - c), me).wait_recv()
            for cp in first + passed:
                cp.wait_send()
            mine.wait()


def _copy_semaphores(n):
    return [pltpu.SemaphoreType.DMA((7 * n,)), pltpu.SemaphoreType.DMA((7 * n,)), pltpu.SemaphoreType.DMA((n,))]


def _rs_phase(g_refs, out_refs, sems, phase):
    send_sems, recv_sems, local_sems = sems
    x, y, c = _my_position()
    me = 4 * x + 2 * y + c
    copies = []
    for p, (g_ref, out_ref) in enumerate(zip(g_refs, out_refs)):
        copies.append(pltpu.make_async_copy(g_ref.at[me], out_ref.at[me], local_sems.at[p]))
        for k in range(1, N_DEV):
            px, py, pc = _flip(x, k & 4), _flip(y, k & 2), _flip(c, k & 1)
            copies.append(pltpu.make_async_remote_copy(
                src_ref=g_ref.at[4 * px + 2 * py + pc], dst_ref=out_ref.at[me],
                send_sem=send_sems.at[7 * p + k - 1], recv_sem=recv_sems.at[7 * p + k - 1],
                device_id=(px, py, pc), device_id_type=MESH))
    for cp in copies:
        if phase == "start":
            cp.start()
        else:
            cp.wait()


def sum_slots(recv, *, name):
    n, R, C = recv.shape
    tr = _pick(R, (512, 256, 128))

    def body(r_ref, o_ref):
        acc = r_ref[0].astype(F32)
        for s in range(1, n):
            acc = acc + r_ref[s].astype(F32)
        o_ref[...] = acc

    return pl.pallas_call(
        body, name=name, grid=(R // tr,), in_specs=[pl.BlockSpec((n, tr, C), lambda i: (0, i, 0))],
        out_specs=pl.BlockSpec((tr, C), lambda i: (i, 0)), out_shape=jax.ShapeDtypeStruct((R, C), F32),
        compiler_params=_params(("parallel",)),
    )(recv)


def all_reduce_small(vs, scatter, *, name):
    n, ns = len(vs), len(scatter)

    def body(*refs):
        v_refs, g_refs = refs[:n], refs[n:n + ns]
        o_refs, r_refs = refs[n + ns:2 * n + ns], refs[2 * n + ns:2 * n + 2 * ns]
        bufs = refs[2 * n + 2 * ns:3 * n + 2 * ns]
        send_sems, recv_sems = refs[3 * n + 2 * ns:3 * n + 2 * ns + 2]
        rs_sems = refs[3 * n + 2 * ns + 2:]
        x, y, c = _my_position()
        me = 4 * x + 2 * y + c
        _rs_phase(g_refs, r_refs, rs_sems, "start")
        copies = []
        for p, (v_ref, buf) in enumerate(zip(v_refs, bufs)):
            buf[me] = v_ref[...]
            for k in range(1, N_DEV):
                px, py, pc = _flip(x, k & 4), _flip(y, k & 2), _flip(c, k & 1)
                copies.append(pltpu.make_async_remote_copy(
                    src_ref=v_ref, dst_ref=buf.at[me], send_sem=send_sems.at[7 * p + k - 1],
                    recv_sem=recv_sems.at[7 * p + k - 1], device_id=(px, py, pc), device_id_type=MESH))
        for cp in copies:
            cp.start()
        for cp in copies:
            cp.wait()
        for o_ref, buf in zip(o_refs, bufs):
            acc = buf[0]
            for s in range(1, N_DEV):
                acc = acc + buf[s]
            o_ref[...] = acc
        _rs_phase(g_refs, r_refs, rs_sems, "finish")

    out = pl.pallas_call(
        body, name=name,
        out_shape=[jax.ShapeDtypeStruct(v.shape, F32) for v in vs] + [jax.ShapeDtypeStruct(g.shape, g.dtype) for g in scatter],
        in_specs=[VMEM_SPEC] * n + [HBM_SPEC] * ns, out_specs=[VMEM_SPEC] * n + [HBM_SPEC] * ns,
        scratch_shapes=[pltpu.VMEM((N_DEV,) + v.shape, F32) for v in vs]
        + [pltpu.SemaphoreType.DMA((7 * n,)), pltpu.SemaphoreType.DMA((7 * n,))] + _copy_semaphores(ns),
        compiler_params=pltpu.CompilerParams(vmem_limit_bytes=VMEM_LIMIT_BYTES),
    )(*vs, *scatter)
    return list(out[:n]), list(out[n:])


def _pad_rows(a, mult):
    pad = (-a.shape[0]) % mult
    return jnp.pad(a, ((0, pad), (0, 0))) if pad else a


def _pack_small(arrays):
    flat = []
    for a in arrays:
        f = a.reshape(-1).astype(F32)
        flat.append(jnp.pad(f, (0, (-f.shape[0]) % LANES)))
    return _pad_rows(jnp.concatenate(flat).reshape(-1, LANES), 8)


def _unpack_small(packed, shapes):
    flat = packed.reshape(-1)
    out, r = [], 0
    for shp in shapes:
        n = math.prod(shp)
        out.append(flat[r:r + n].reshape(shp))
        r += n + (-n) % LANES
    return out


ARG_NAMES = ("x",) + WEIGHTS + ("loss_target",) + tuple("m_" + w for w in WEIGHTS) + tuple("v_" + w for w in WEIGHTS)


def kernel(x, mix_norm, ffn_norm, sb_w_qkv, sb_q_gain, sb_k_gain, sb_w_o, gm_w_in, gm_b_in, gm_v_gain, gm_w_s, gm_b_s, gm_w_out, ssm_w_in, ssm_conv_w, ssm_conv_b, ssm_dt_bias, ssm_a_log, ssm_d, ssm_norm_gain, ssm_w_out, ffn_w_gu, ffn_w_down, loss_target, m_mix_norm, m_ffn_norm, m_sb_w_qkv, m_sb_q_gain, m_sb_k_gain, m_sb_w_o, m_gm_w_in, m_gm_b_in, m_gm_v_gain, m_gm_w_s, m_gm_b_s, m_gm_w_out, m_ssm_w_in, m_ssm_conv_w, m_ssm_conv_b, m_ssm_dt_bias, m_ssm_a_log, m_ssm_d, m_ssm_norm_gain, m_ssm_w_out, m_ffn_w_gu, m_ffn_w_down, v_mix_norm, v_ffn_norm, v_sb_w_qkv, v_sb_q_gain, v_sb_k_gain, v_sb_w_o, v_gm_w_in, v_gm_b_in, v_gm_v_gain, v_gm_w_s, v_gm_b_s, v_gm_w_out, v_ssm_w_in, v_ssm_conv_w, v_ssm_conv_b, v_ssm_dt_bias, v_ssm_a_log, v_ssm_d, v_ssm_norm_gain, v_ssm_w_out, v_ffn_w_gu, v_ffn_w_down):
    given = dict(zip(ARG_NAMES, (x, mix_norm, ffn_norm, sb_w_qkv, sb_q_gain, sb_k_gain, sb_w_o, gm_w_in, gm_b_in, gm_v_gain, gm_w_s, gm_b_s, gm_w_out, ssm_w_in, ssm_conv_w, ssm_conv_b, ssm_dt_bias, ssm_a_log, ssm_d, ssm_norm_gain, ssm_w_out, ffn_w_gu, ffn_w_down, loss_target, m_mix_norm, m_ffn_norm, m_sb_w_qkv, m_sb_q_gain, m_sb_k_gain, m_sb_w_o, m_gm_w_in, m_gm_b_in, m_gm_v_gain, m_gm_w_s, m_gm_b_s, m_gm_w_out, m_ssm_w_in, m_ssm_conv_w, m_ssm_conv_b, m_ssm_dt_bias, m_ssm_a_log, m_ssm_d, m_ssm_norm_gain, m_ssm_w_out, m_ffn_w_gu, m_ffn_w_down, v_mix_norm, v_ffn_norm, v_sb_w_qkv, v_sb_q_gain, v_sb_k_gain, v_sb_w_o, v_gm_w_in, v_gm_b_in, v_gm_v_gain, v_gm_w_s, v_gm_b_s, v_gm_w_out, v_ssm_w_in, v_ssm_conv_w, v_ssm_conv_b, v_ssm_dt_bias, v_ssm_a_log, v_ssm_d, v_ssm_norm_gain, v_ssm_w_out, v_ffn_w_gu, v_ffn_w_down)))
    mx, my, mc = _my_position()
    me = 4 * mx + 2 * my + mc

    pieces = [(k, l) for k in BIG for l in range(given[k].shape[0])]
    early = [("sb_w_qkv", 0)]
    late_pieces = [p for p in pieces if p not in early]
    last = [("sb_w_qkv", 0)]
    main = [p for p in pieces if p not in last]

    def shards_of(ps):
        return [(given[k][l].T if k in COL_SHARDED else given[k][l]).astype(BF16) for k, l in ps]

    def piece_to_full(g, k):
        rows, cols = given[k].shape[1:]
        return g.reshape(N_DEV * cols, rows) if k in COL_SHARDED else g.reshape(N_DEV * rows, cols)

    def full_to_piece(full, k):
        return full.reshape(N_DEV, -1, PACK_COLS)

    def summed_to_shard(g, k):
        rows, cols = given[k].shape[1:]
        return g.reshape(cols, rows) if k in COL_SHARDED else g.reshape(rows, cols)

    def contributions(gw, ps):
        return [full_to_piece(gw[k][l], k) for k, l in ps]

    sharded_small = [lax.bitcast_convert_type(given[k], BF16) for k in SMALL_SHARDED]
    tail = jnp.concatenate([a.reshape(-1) for a in sharded_small])
    tail = jnp.pad(tail, (0, (-tail.size) % PACK_COLS)).reshape(-1, PACK_COLS)

    W = {k: given[k] for k in SMALL if k not in SMALL_SHARDED}
    W.update({k: [None] * given[k].shape[0] for k in BIG})
    for (k, l), g in zip(early, all_gather(shards_of(early), name="all_gather_early")):
        W[k][l] = piece_to_full(g, k)

    class Late:
        shards = shards_of(late_pieces) + [tail]

        @staticmethod
        def fill(weights, gathered):
            for (k, l), g in zip(late_pieces, gathered):
                weights[k][l] = piece_to_full(g, k)
            tail_g = gathered[-1].reshape(N_DEV, -1)
            off = 0
            for k, a in zip(SMALL_SHARDED, sharded_small):
                g = lax.bitcast_convert_type(tail_g[:, off:off + a.size].reshape((N_DEV,) + a.shape), F32)
                weights[k] = jnp.moveaxis(g, 0, -2).reshape(g.shape[1:-1] + (N_DEV * g.shape[-1],))
                off += a.size

        @staticmethod
        def contributions(gw):
            return contributions(gw, main)

    loss, gx, gw, received_main = local_step(given["x"][0], given["loss_target"][0], W, late=Late)

    grads_small = {k: jnp.stack([gw[k][l] for l in sorted(gw[k])], axis=0) for k in SMALL}
    packed_names = tuple(k for k in SMALL if k != "gm_w_s")
    small_shapes = [grads_small[k].shape for k in packed_names] + [(1, 1)]
    (red_ws, red_rest), received_last = all_reduce_small(
        [grads_small["gm_w_s"].reshape(-1, LANES), _pack_small([grads_small[k] for k in packed_names] + [loss])],
        contributions(gw, last), name="all_reduce_small_and_last_exchange")
    small_full = dict(zip(packed_names + ("loss",), _unpack_small(red_rest, small_shapes)))
    small_full["gm_w_s"] = red_ws.reshape(grads_small["gm_w_s"].shape)

    g_piece = {}
    for grp, received in ((main, received_main), (last, received_last)):
        for p, r in zip(grp, received):
            g_piece[p] = summed_to_shard(sum_slots(r, name="reduce_scatter_sum"), p[0])
    out_g, out_d, out_m, out_v = {}, {}, {}, {}
    for k in BIG:
        swap = lambda a: jnp.swapaxes(a, -1, -2)
        keep_t = k in COL_SHARDED and given[k].shape[-1] % LANES != 0
        flip = swap if keep_t else (lambda a: a)
        g = jnp.stack([g_piece[(k, l)] for l in range(given[k].shape[0])], axis=0)
        if k in COL_SHARDED and not keep_t:
            g = swap(g)
        res = adamw(flip(given[k]), g, flip(given["m_" + k]), flip(given["v_" + k]), name="adamw_" + k)
        out_g[k], out_d[k], out_m[k], out_v[k] = (flip(a) for a in (g,) + tuple(res))

    gsmall = {}
    for k in SMALL:
        g = small_full[k]
        if k in SMALL_SHARDED:
            n = given[k].shape[-1]
            g = lax.dynamic_slice_in_dim(g, me * n, n, axis=g.ndim - 1)
        gsmall[k] = g
    local_shapes = [given[k].shape for k in packed_names]
    dsm, nmsm, nvsm = adamw(*[_pack_small([src[k] for k in packed_names]) for src in (
        given, gsmall, {k: given["m_" + k] for k in packed_names}, {k: given["v_" + k] for k in packed_names})],
        name="adamw_small")
    out_g.update(gsmall)
    for dst, src in ((out_d, dsm), (out_m, nmsm), (out_v, nvsm)):
        dst.update(zip(packed_names, _unpack_small(src, local_shapes)))
    ws_shape = given["gm_w_s"].shape
    out_d["gm_w_s"], out_m["gm_w_s"], out_v["gm_w_s"] = (a.reshape(ws_shape) for a in adamw(
        *[a.reshape((-1,) + ws_shape[-2:]) for a in (given["gm_w_s"], gsmall["gm_w_s"], given["m_gm_w_s"], given["v_gm_w_s"])],
        name="adamw_gm_w_s"))

    return (small_full["loss"].reshape(()), gx[None],
            *[out_g[k] for k in WEIGHTS], *[out_d[k] for k in WEIGHTS],
            *[out_m[k] for k in WEIGHTS], *[out_v[k] for k in WEIGHTS])
```

```python
import math

import jax
import jax.numpy as jnp
from jax import lax
from jax.experimental import pallas as pl
from jax.experimental.pallas import tpu as pltpu

F32 = jnp.float32
BF16 = jnp.bfloat16
EPS = 1e-6
N_DEV = 8
SB_HEAD_DIM = 64
SB_TILE = 256
SB_FWD_QUERY_BLOCKS = 4
SB_BWD_QUERY_BLOCKS = 2
SSM_STATE = 128
SSD_FWD_GROUPS_PER_STEP = 2
SSD_BWD_GROUPS_PER_STEP = 1
SSM_CONV = 4
ADAM_LR = 0.001
ADAM_B1 = 0.9
ADAM_B2 = 0.999
ADAM_EPS = 1e-08
ADAM_WD = 0.01
ADAM_STEP = 10
VMEM_LIMIT_BYTES = 56 * 1024 * 1024
MATMUL_VMEM_BUDGET = 40 * 1024 * 1024
LANES = 128
PACK_COLS = 1024

BIG = ("sb_w_qkv", "sb_w_o", "gm_w_in", "gm_w_out", "ssm_w_in", "ssm_w_out", "ffn_w_gu", "ffn_w_down")
COL_SHARDED = ("sb_w_qkv", "gm_w_in", "ssm_w_in", "ffn_w_gu")
SMALL = ("mix_norm", "ffn_norm", "sb_q_gain", "sb_k_gain", "gm_b_in", "gm_v_gain", "gm_w_s", "gm_b_s",
         "ssm_conv_w", "ssm_conv_b", "ssm_dt_bias", "ssm_a_log", "ssm_d", "ssm_norm_gain")
SMALL_SHARDED = ("ssm_conv_w", "ssm_conv_b", "ssm_norm_gain")
WEIGHTS = ("mix_norm", "ffn_norm", "sb_w_qkv", "sb_q_gain", "sb_k_gain", "sb_w_o", "gm_w_in", "gm_b_in",
           "gm_v_gain", "gm_w_s", "gm_b_s", "gm_w_out", "ssm_w_in", "ssm_conv_w", "ssm_conv_b", "ssm_dt_bias",
           "ssm_a_log", "ssm_d", "ssm_norm_gain", "ssm_w_out", "ffn_w_gu", "ffn_w_down")


def _params(semantics=None):
    return pltpu.CompilerParams(dimension_semantics=semantics, vmem_limit_bytes=VMEM_LIMIT_BYTES)


def _pick(n, prefs):
    for t in prefs:
        if t <= n and n % t == 0:
            return t
    return n


def _dot(a, b, ca=1, cb=0):
    return lax.dot_general(a, b, (((ca,), (cb,)), ((), ())), preferred_element_type=F32)


def _split3(v):
    h1 = v.astype(BF16)
    r1 = v - h1.astype(F32)
    h2 = r1.astype(BF16)
    h3 = (r1 - h2.astype(F32)).astype(BF16)
    return h1, h2, h3


def _dot_exact_left(mat01, v):
    h1, h2, h3 = _split3(v)
    return _dot(mat01, h1) + _dot(mat01, h2) + _dot(mat01, h3)


def _sum_all(v):
    return jnp.sum(jnp.sum(v, axis=0, keepdims=True), axis=1, keepdims=True)


def _sigmoid(v):
    return 1.0 / (1.0 + jnp.exp(-v))


def _softplus(v):
    return jnp.maximum(v, 0.0) + jnp.log(1.0 + jnp.exp(-jnp.abs(v)))


def _erf(v):
    a = jnp.abs(v)
    t = 1.0 / (1.0 + 0.3275911 * a)
    poly = t * (0.254829592 + t * (-0.284496736 + t * (1.421413741 + t * (-1.453152027 + t * 1.061405429))))
    e = 1.0 - poly * jnp.exp(-a * a)
    return jnp.where(v < 0, -e, e)


def _gelu_and_grad(v):
    cdf = 0.5 * (1.0 + _erf(v * (1.0 / math.sqrt(2.0))))
    pdf = jnp.exp(-0.5 * v * v) * (1.0 / math.sqrt(2.0 * math.pi))
    return v * cdf, cdf + v * pdf


def matmul(a, b, *, ta=False, tb=False, out_dtype=F32, residual=None, a_split=1, b_split=1, norm_bwd=None,
           norm_fwd=None, name):
    if a_split > 1 and ta:
        assert a.shape[0] == a_split
        K, M = a.shape[1], a_split * a.shape[2]
    elif a_split > 1:
        assert a.shape[0] == a_split
        M, K = a.shape[1], a_split * a.shape[2]
    elif ta:
        K, M = a.shape
    else:
        M, K = a.shape
    if b_split > 1:
        assert not tb and b.shape[0] == b_split
        Kb, N = b.shape[1], b_split * b.shape[2]
    elif tb:
        N, Kb = b.shape
    else:
        Kb, N = b.shape
    assert K == Kb, (a.shape, b.shape, ta, tb)
    has_res = residual is not None
    tm = _pick(M // a_split if ta else M, (1024, 1408, 768, 512, 256, 128))
    tn = _pick(N // b_split, (1024, 1408, 1536, 768, 512, 256, 128))

    has_norm = norm_bwd is not None
    has_nf = norm_fwd is not None
    assert not (has_norm or has_nf) or (tn == N and out_dtype == F32)
    assert not (has_norm and has_nf)

    def vmem_bytes(tk):
        tiles = tm * tk * a.dtype.itemsize + tk * tn * b.dtype.itemsize
        outs = tm * tn * jnp.dtype(out_dtype).itemsize + (tm * tn * 4 if has_res else 0) + (2 * tm * tn * 4 if has_norm else 0)
        return 2 * tiles + 2 * outs + (tm * tn * 4 if tk < K else 0)

    kp = K if ta else K // a_split
    tk = next((t for t in (K, 2048, 1408, 1024, 512, 256) if t <= kp and kp % t == 0 and vmem_bytes(t) <= MATMUL_VMEM_BUDGET),
              _pick(kp, (128,)))
    nk = K // tk
    if a_split > 1 and ta:
        nib = M // a_split // tm
        a_spec = pl.BlockSpec((None, tk, tm), lambda i, j, k: (i // nib, k, i % nib))
    elif a_split > 1:
        nkb = kp // tk
        a_spec = pl.BlockSpec((None, tm, tk), lambda i, j, k: (k // nkb, i, k % nkb))
    else:
        a_spec = pl.BlockSpec((tk, tm), lambda i, j, k: (k, i)) if ta else pl.BlockSpec((tm, tk), lambda i, j, k: (i, k))
    if b_split > 1:
        njb = N // b_split // tn
        b_spec = pl.BlockSpec((None, tk, tn), lambda i, j, k: (j // njb, k, j % njb))
    else:
        b_spec = pl.BlockSpec((tn, tk), lambda i, j, k: (j, k)) if tb else pl.BlockSpec((tk, tn), lambda i, j, k: (k, j))
    o_spec = pl.BlockSpec((tm, tn), lambda i, j, k: (i, j))
    ca, cb = (0 if ta else 1), (1 if tb else 0)

    def body(*refs):
        a_ref, b_ref = refs[:2]
        n_in = 2 + has_res + 3 * has_norm + has_nf
        r_ref = refs[2] if has_res else None
        o_ref = refs[n_in]
        if has_norm:
            x_ref, g_ref, dres_ref = refs[n_in - 3:n_in]
            dg_ref = refs[n_in + 1]

            @pl.when(jnp.logical_and(pl.program_id(0) == 0, pl.program_id(2) == 0))
            def _():
                dg_ref[...] = jnp.zeros_like(dg_ref)

        def finish(r):
            if has_res:
                r = r + r_ref[...]
            if has_norm:
                xv = x_ref[...]
                rs = lax.rsqrt(jnp.mean(xv * xv, axis=1, keepdims=True) + EPS)
                xhat = xv * rs
                t = r * g_ref[...]
                dg_ref[...] += jnp.sum(r * xhat, axis=0, keepdims=True)
                r = dres_ref[...] + rs * (t - xhat * jnp.mean(xhat * t, axis=1, keepdims=True))
            if has_nf:
                rs = lax.rsqrt(jnp.mean(r * r, axis=1, keepdims=True) + EPS)
                refs[n_in + 1][...] = (r * rs * refs[n_in - 1][...]).astype(BF16)
            o_ref[...] = r.astype(out_dtype)

        def part():
            return _dot(a_ref[...].astype(BF16), b_ref[...].astype(BF16), ca, cb)

        if nk == 1:
            finish(part())
            return
        acc = refs[-1]
        k = pl.program_id(2)


        @pl.when(k == 0)
        def _():
            acc[...] = part()

        @pl.when(jnp.logical_and(k > 0, k < nk - 1))
        def _():
            acc[...] += part()

        @pl.when(k == nk - 1)
        def _():
            finish(acc[...] + part())

    in_specs = [a_spec, b_spec] + ([o_spec] if has_res else [])
    args = (a, b) + ((residual,) if has_res else ())
    out_specs, out_shape = o_spec, jax.ShapeDtypeStruct((M, N), out_dtype)
    if has_norm:
        vec = pl.BlockSpec((1, N), lambda i, j, k: (0, 0))
        in_specs += [o_spec, vec, o_spec]
        args += tuple(norm_bwd)
        out_specs, out_shape = [o_spec, vec], [out_shape, jax.ShapeDtypeStruct((1, N), F32)]
    if has_nf:
        in_specs += [pl.BlockSpec((1, N), lambda i, j, k: (0, 0))]
        args += (norm_fwd,)
        out_specs, out_shape = [o_spec, o_spec], [out_shape, jax.ShapeDtypeStruct((M, N), BF16)]
    return pl.pallas_call(
        body, name=name, grid=(M // tm, N // tn, nk), in_specs=in_specs, out_specs=out_specs, out_shape=out_shape,
        scratch_shapes=[pltpu.VMEM((tm, tn), F32)] if nk > 1 else [],
        compiler_params=_params(("arbitrary" if has_norm else "parallel", "parallel", "arbitrary")),
    )(*args)


def rms_fwd(x, gain, *, name):
    S, D = x.shape
    tr = _pick(S, (512, 256, 128))

    def body(x_ref, g_ref, o_ref):
        xv = x_ref[...]
        r = lax.rsqrt(jnp.mean(xv * xv, axis=1, keepdims=True) + EPS)
        o_ref[...] = (xv * r * g_ref[...]).astype(BF16)

    return pl.pallas_call(
        body, name=name, grid=(S // tr,),
        in_specs=[pl.BlockSpec((tr, D), lambda i: (i, 0)), pl.BlockSpec((1, D), lambda i: (0, 0))],
        out_specs=pl.BlockSpec((tr, D), lambda i: (i, 0)), out_shape=jax.ShapeDtypeStruct((S, D), BF16),
        compiler_params=_params(("parallel",)),
    )(x, gain)


def ffn_up_fwd(h, w_gu_t, *, name):
    S, K = h.shape
    F = w_gu_t.shape[0] // 2
    tm = _pick(S, (512, 256, 128))
    tn = _pick(F, (1408, 1024, 768, 512, 256, 128))
    nj = F // tn

    def body(h_ref, wg_ref, wu_ref, gu_ref, a_ref):
        hv = h_ref[...]
        g = _dot(hv, wg_ref[...], 1, 1)
        u = _dot(hv, wu_ref[...], 1, 1)
        gu_ref[0] = g
        gu_ref[1] = u
        a_ref[...] = (g * _sigmoid(g) * u).astype(BF16)

    return pl.pallas_call(
        body, name=name, grid=(nj, S // tm),
        in_specs=[pl.BlockSpec((tm, K), lambda j, i: (i, 0)), pl.BlockSpec((tn, K), lambda j, i: (j, 0)),
                  pl.BlockSpec((tn, K), lambda j, i: (nj + j, 0))],
        out_specs=[pl.BlockSpec((2, tm, tn), lambda j, i: (0, i, j)), pl.BlockSpec((tm, tn), lambda j, i: (i, j))],
        out_shape=[jax.ShapeDtypeStruct((2, S, F), F32), jax.ShapeDtypeStruct((S, F), BF16)],
        compiler_params=_params(("parallel", "parallel")),
    )(h, w_gu_t, w_gu_t)


def ffn_up_bwd(dy, w_down, gu, *, name):
    S, D = dy.shape
    F = w_down.shape[0]
    tm = _pick(S, (512, 256, 128))
    tn = _pick(F, (1408, 1024, 768, 512, 256, 128))

    def body(dy_ref, wd_ref, gu_ref, o_ref):
        da = _dot(dy_ref[...].astype(BF16), wd_ref[...], 1, 1)
        g = gu_ref[0]
        u = gu_ref[1]
        s = _sigmoid(g)
        o_ref[0] = (da * u * (s * (1.0 + g * (1.0 - s)))).astype(BF16)
        o_ref[1] = (da * g * s).astype(BF16)

    pair = pl.BlockSpec((2, tm, tn), lambda j, i: (0, i, j))
    return pl.pallas_call(
        body, name=name, grid=(F // tn, S // tm),
        in_specs=[pl.BlockSpec((tm, D), lambda j, i: (i, 0)), pl.BlockSpec((tn, D), lambda j, i: (j, 0)), pair],
        out_specs=pair, out_shape=jax.ShapeDtypeStruct((2, S, F), BF16),
        compiler_params=_params(("parallel", "parallel")),
    )(dy, w_down, gu)


def loss_head(y, target, *, name):
    S, D = y.shape
    tr = _pick(S, (512, 256, 128))

    def body(y_ref, t_ref, dy_ref, l_ref):
        @pl.when(pl.program_id(0) == 0)
        def _():
            l_ref[...] = jnp.zeros_like(l_ref)

        err = y_ref[...] - t_ref[...]
        dy_ref[...] = err * (1.0 / D)
        l_ref[...] += jnp.sum(0.5 * jnp.mean(err * err, axis=1, keepdims=True), axis=0, keepdims=True)

    row = pl.BlockSpec((tr, D), lambda i: (i, 0))
    one = pl.BlockSpec((1, 1), lambda i: (0, 0))
    dy, l = pl.pallas_call(
        body, name=name, grid=(S // tr,), in_specs=[row, row], out_specs=[row, one],
        out_shape=[jax.ShapeDtypeStruct((S, D), F32), jax.ShapeDtypeStruct((1, 1), F32)],
        compiler_params=_params(("arbitrary",)),
    )(y, target)
    return dy, l


def adamw(w, g, m, v, *, name):
    R, C = w.shape[-2:]
    tr = _pick(R, (512, 256, 128, 64, 32, 16, 8))

    def body(w_ref, g_ref, m_ref, v_ref, d_ref, mo_ref, vo_ref):
        gv = g_ref[...]
        mn = ADAM_B1 * m_ref[...] + (1.0 - ADAM_B1) * gv
        vn = ADAM_B2 * v_ref[...] + (1.0 - ADAM_B2) * jnp.square(gv)
        m_hat = mn / (1.0 - ADAM_B1 ** ADAM_STEP)
        v_hat = vn / (1.0 - ADAM_B2 ** ADAM_STEP)
        d_ref[...] = -ADAM_LR * (m_hat / (jnp.sqrt(v_hat) + ADAM_EPS) + ADAM_WD * w_ref[...])
        mo_ref[...] = mn
        vo_ref[...] = vn

    tc = C if tr < R or C % LANES else _pick(C, (256, 128))
    if w.ndim == 3:
        grid = (w.shape[0], R // tr, C // tc)
        blk = pl.BlockSpec((None, tr, tc), lambda l, i, j: (l, i, j))
    else:
        grid = (R // tr, C // tc)
        blk = pl.BlockSpec((tr, tc), lambda i, j: (i, j))
    sds = jax.ShapeDtypeStruct(w.shape, F32)
    return pl.pallas_call(
        body, name=name, grid=grid, in_specs=[blk] * 4, out_specs=[blk] * 3, out_shape=[sds] * 3,
        compiler_params=_params(("parallel",) * len(grid)),
    )(w, g, m, v)


def _tri(n, kind):
    r = lax.broadcasted_iota(jnp.int32, (n, n), 0)
    c = lax.broadcasted_iota(jnp.int32, (n, n), 1)
    if kind == "row_gt_col":
        return (r > c).astype(BF16)
    if kind == "row_ge_col":
        return (r >= c).astype(BF16)
    if kind == "row_le_col":
        return (r <= c).astype(BF16)
    raise ValueError(kind)


def _sb_tile(qs, kj, r_carry, u_strict, masked):
    z = _dot(qs, kj, 1, 1)
    lb = jnp.minimum(z, 0.0) - jnp.log(1.0 + jnp.exp(-jnp.abs(z)))
    l1m = lb - z
    keep = None
    if masked:
        tq, tk = z.shape
        keep = lax.broadcasted_iota(jnp.int32, (tq, tk), 1) < lax.broadcasted_iota(jnp.int32, (tq, tk), 0)
        l1m = jnp.where(keep, l1m, 0.0)
    w = jnp.exp(lb + _dot(l1m.astype(BF16), u_strict) + r_carry)
    if masked:
        w = jnp.where(keep, w, 0.0)
    return lb, l1m, w, keep


def _sb_prep(T, nb, hd, refs_in, gains, scratch):
    q_scale = 1.0 / math.sqrt(hd)
    assert math.log2(q_scale) == round(math.log2(q_scale))

    def prep(i, _):
        rows = pl.ds(pl.multiple_of(i * T, T), T)
        for hh in range(2):
            sl = slice(hd * hh, hd * hh + hd)
            for n, (src, dst) in enumerate(zip(refs_in, scratch)):
                v = src[rows, sl]
                if n < 2:
                    v = v * lax.rsqrt(jnp.mean(v * v, axis=1, keepdims=True) + EPS) * gains[n][...]
                if n == 0:
                    v = v * q_scale
                dst[hh, rows, :] = v.astype(BF16)
        return 0

    lax.fori_loop(0, nb, prep, 0)


def _sb_chains(m, T, nq):
    rows = [pl.ds(pl.multiple_of((nq * m + qb) * T, T), T) for qb in range(nq)]
    return rows, [(hh, qb) for qb in range(nq) for hh in range(2)]


def _sb_sweep(tile, carry, chains, m, nq):
    for kk in reversed(range(nq)):
        carry = tile(nq * m + kk, carry, [(ch, ch[1] == kk) for ch in chains if ch[1] >= kk])
    return lax.fori_loop(0, nq * m, lambda jj, c: tile(nq * m - 1 - jj, c, [(ch, False) for ch in chains]), carry)


def sb_attn_fwd(qkv, q_gain, k_gain, *, name, gather=None):
    S, D3 = qkv.shape
    D = D3 // 3
    npairs = D // LANES
    hd = SB_HEAD_DIM
    T = min(SB_TILE, S)
    nb = S // T
    nq = SB_FWD_QUERY_BLOCKS
    assert nb % nq == 0

    def body(*refs):
        if gather is None:
            q_ref, k_ref, v_ref, qg_ref, kg_ref, us_ref, o_ref, qn_s, kn_s, vb_s = refs
        else:
            ng = len(gather)
            q_ref, k_ref, v_ref, qg_ref, kg_ref, us_ref = refs[:6]
            o_ref = refs[6 + ng]
            qn_s, kn_s, vb_s = refs[7 + 2 * ng:10 + 2 * ng]
            comm = (refs[6:6 + ng], refs[7 + ng:7 + 2 * ng], refs[10 + 2 * ng:])
            step = pl.program_id(0)
            pl.when(step == 0)(lambda: _ag_phase(*comm, "start"))
            pl.when(step == npairs - 1)(lambda: _ag_phase(*comm, "forward"))
        us = us_ref[...]
        _sb_prep(T, nb, hd, (q_ref, k_ref, v_ref), (qg_ref, kg_ref), (qn_s, kn_s, vb_s))

        def superblock(m, _):
            rows_q, chains = _sb_chains(m, T, nq)
            qs = {ch: qn_s[ch[0], rows_q[ch[1]], :] for ch in chains}

            def tile(j, carry, which):
                rows_j = pl.ds(pl.multiple_of(j * T, T), T)
                new = dict(carry)
                for ch, masked in which:
                    acc, rc = carry[ch]
                    _, l1m, w, _ = _sb_tile(qs[ch], kn_s[ch[0], rows_j, :], rc, us, masked)
                    new[ch] = (acc + _dot(w.astype(BF16), vb_s[ch[0], rows_j, :]),
                               rc + jnp.sum(l1m, axis=1, keepdims=True))
                return new

            carry = {ch: (jnp.zeros((T, hd), F32), jnp.zeros((T, 1), F32)) for ch in chains}
            carry = _sb_sweep(tile, carry, chains, m, nq)
            for qb in range(nq):
                o_ref[rows_q[qb], :] = jnp.concatenate([carry[(0, qb)][0], carry[(1, qb)][0]], axis=1)
            return 0

        lax.fori_loop(0, nb // nq, superblock, 0)
        if gather is not None:
            pl.when(step == npairs - 1)(lambda: _ag_phase(*comm, "finish"))

    col = lambda off: pl.BlockSpec((S, LANES), lambda p, off=off: (0, off + p))
    gain = pl.BlockSpec((1, hd), lambda p: (0, 0))
    in_specs = [col(0), col(npairs), col(2 * npairs), gain, gain, pl.BlockSpec((T, T), lambda p: (0, 0))]
    out_specs = [pl.BlockSpec((S, LANES), lambda p: (0, p))]
    out_shape = [jax.ShapeDtypeStruct((S, D), F32)]
    scratch = [pltpu.VMEM((2, S, hd), BF16)] * 3
    args = [qkv, qkv, qkv, q_gain, k_gain, _tri(T, "row_gt_col")]
    if gather is not None:
        in_specs += [HBM_SPEC] * len(gather)
        out_specs += [HBM_SPEC] * len(gather)
        out_shape += [jax.ShapeDtypeStruct((N_DEV,) + s.shape, s.dtype) for s in gather]
        scratch += _copy_semaphores(len(gather))
        args += list(gather)
    out = pl.pallas_call(
        body, name=name, grid=(npairs,), in_specs=in_specs, out_specs=out_specs, out_shape=out_shape,
        scratch_shapes=scratch, compiler_params=_params(("arbitrary",)),
    )(*args)
    return out[0] if gather is None else (out[0], list(out[1:]))


def sb_attn_bwd(qkv, o, do, q_gain, k_gain, *, name, scatter=None):
    S, D3 = qkv.shape
    D = D3 // 3
    npairs = D // LANES
    hd = SB_HEAD_DIM
    T = min(SB_TILE, S)
    nb = S // T
    nq = SB_BWD_QUERY_BLOCKS
    assert nb % nq == 0
    scale = 1.0 / math.sqrt(hd)

    def body(*refs):
        if scatter is None:
            (q_ref, k_ref, v_ref, o_ref, do_ref, qg_ref, kg_ref, us_ref,
             dqkv_ref, dg_ref, qn_s, kn_s, vb_s, dob_s, acc_s) = refs
        else:
            ns = len(scatter)
            q_ref, k_ref, v_ref, o_ref, do_ref, qg_ref, kg_ref, us_ref = refs[:8]
            rs_in = refs[8:8 + ns]
            dqkv_ref, dg_ref = refs[8 + ns:10 + ns]
            rs_out = refs[10 + ns:10 + 2 * ns]
            qn_s, kn_s, vb_s, dob_s, acc_s = refs[10 + 2 * ns:15 + 2 * ns]
            rs_sems = refs[15 + 2 * ns:]
            pl.when(pl.program_id(0) == 0)(lambda: _rs_phase(rs_in, rs_out, rs_sems, "start"))
        dq_ref, dk_ref, dv_ref = acc_s.at[0], acc_s.at[1], acc_s.at[2]

        @pl.when(pl.program_id(0) == 0)
        def _():
            dg_ref[...] = jnp.zeros_like(dg_ref)

        us = us_ref[...]
        u_prefix = (1.0 - us.astype(F32)).astype(BF16)
        _sb_prep(T, nb, hd, (q_ref, k_ref, v_ref, do_ref), (qg_ref, kg_ref), (qn_s, kn_s, vb_s, dob_s))
        dk_ref[...] = jnp.zeros_like(dk_ref)
        dv_ref[...] = jnp.zeros_like(dv_ref)

        def superblock(m, _):
            rows_q, chains = _sb_chains(m, T, nq)
            qs = {ch: qn_s[ch[0], rows_q[ch[1]], :] for ch in chains}
            doi ={ch: dob_s[ch[0], rows_q[ch[1]], :] for ch in chains}
            dt_total = {ch: jnp.sum(doi[ch].astype(F32) * o_ref[rows_q[ch[1]], hd * ch[0]:hd * ch[0] + hd],
                                    axis=1, keepdims=True) for ch in chains}

            def tile(j, carry, which):
                rows_j = pl.ds(pl.multiple_of(j * T, T), T)
                new = dict(carry)
                dk_part, dv_part = {}, {}
                for ch, masked in which:
                    hh = ch[0]
                    dq_acc, rc, gc = carry[ch]
                    kj = kn_s[hh, rows_j, :]
                    lb, l1m, w, keep = _sb_tile(qs[ch], kj, rc, us, masked)
                    wb = w.astype(BF16)
                    g = _dot(doi[ch], vb_s[hh, rows_j, :], 1, 1) * wb.astype(F32)
                    g_row = jnp.sum(g, axis=1, keepdims=True)
                    g_upto = (dt_total[ch] - gc - g_row) + _dot(g.astype(BF16), u_prefix)
                    dz = g - g_upto * jnp.exp(lb)
                    if masked:
                        dz = jnp.where(keep, dz, 0.0)
                    dzb = dz.astype(BF16)
                    dv_part[hh] = dv_part.get(hh, 0.0) + _dot(wb, doi[ch], 0, 0)
                    dk_part[hh] = dk_part.get(hh, 0.0) + _dot(dzb, qs[ch], 0, 0)
                    new[ch] = (dq_acc + _dot(dzb, kj), rc + jnp.sum(l1m, axis=1, keepdims=True),
                               gc + g_row)
                dv_ref[rows_j, :] += jnp.concatenate([dv_part[0], dv_part[1]], axis=1)
                dk_ref[rows_j, :] += jnp.concatenate([dk_part[0], dk_part[1]], axis=1)
                return new

            zero1 = jnp.zeros((T, 1), F32)
            carry = {ch: (jnp.zeros((T, hd), F32), zero1, zero1) for ch in chains}
            carry = _sb_sweep(tile, carry, chains, m, nq)
            for qb in range(nq):
                dq_ref[rows_q[qb], :] = jnp.concatenate([carry[(0, qb)][0], carry[(1, qb)][0]], axis=1) * scale
            return 0

        lax.fori_loop(0, nb // nq, superblock, 0)

        def finish(i, carry):
            rows = pl.ds(pl.multiple_of(i * T, T), T)
            new = []
            for hh in range(2):
                sl = slice(hd * hh, hd * hh + hd)
                outs = []
                for raw_ref, gain_ref, dn in ((q_ref, qg_ref, dq_ref[rows, sl]), (k_ref, kg_ref, dk_ref[rows, sl])):
                    raw = raw_ref[rows, sl]
                    r = lax.rsqrt(jnp.mean(raw * raw, axis=1, keepdims=True) + EPS)
                    hat = raw * r
                    t = dn * gain_ref[...]
                    outs.append((r * (t - hat * jnp.mean(hat * t, axis=1, keepdims=True)),
                                 jnp.sum(dn * hat, axis=0, keepdims=True)))
                dqkv_ref[0, rows, sl] = outs[0][0].astype(BF16)
                dqkv_ref[1, rows, sl] = outs[1][0].astype(BF16)
                new.append((carry[hh][0] + outs[0][1], carry[hh][1] + outs[1][1]))
            dqkv_ref[2, rows, :] = dv_ref[rows, :].astype(BF16)
            return tuple(new)

        zg = (jnp.zeros((1, hd), F32), jnp.zeros((1, hd), F32))
        tot = lax.fori_loop(0, nb, finish, (zg, zg))
        dg_ref[0:1, 0:hd] += tot[0][0] + tot[1][0]
        dg_ref[1:2, 0:hd] += tot[0][1] + tot[1][1]
        if scatter is not None:
            pl.when(pl.program_id(0) == npairs - 1)(lambda: _rs_phase(rs_in, rs_out, rs_sems, "finish"))

    col = lambda off: pl.BlockSpec((S, LANES), lambda p, off=off: (0, off + p))
    gain = pl.BlockSpec((1, hd), lambda p: (0, 0))
    tri = pl.BlockSpec((T, T), lambda p: (0, 0))
    pair = pl.BlockSpec((S, LANES), lambda p: (0, p))
    in_specs = [col(0), col(npairs), col(2 * npairs), pair, pair, gain, gain, tri]
    out_specs = [pl.BlockSpec((3, S, LANES), lambda p: (0, 0, p)), pl.BlockSpec((8, LANES), lambda p: (0, 0))]
    out_shape = [jax.ShapeDtypeStruct((3, S, D), BF16), jax.ShapeDtypeStruct((8, LANES), F32)]
    scratch = [pltpu.VMEM((2, S, hd), BF16)] * 4 + [pltpu.VMEM((3, S, LANES), F32)]
    args = [qkv, qkv, qkv, o, do, q_gain, k_gain, _tri(T, "row_gt_col")]
    if scatter is not None:
        in_specs += [HBM_SPEC] * len(scatter)
        out_specs += [HBM_SPEC] * len(scatter)
        out_shape += [jax.ShapeDtypeStruct(g.shape, g.dtype) for g in scatter]
        scratch += _copy_semaphores(len(scatter))
        args += list(scatter)
    out = pl.pallas_call(
        body, name=name, grid=(npairs,), in_specs=in_specs, out_specs=out_specs, out_shape=out_shape,
        scratch_shapes=scratch, compiler_params=_params(("arbitrary",)),
    )(*args)
    res = (out[0], out[1][0:1, :hd], out[1][1:2, :hd])
    return res if scatter is None else res + (list(out[2:]),)


def gmlp_fwd(zzpre, b_in, v_gain, wc, bsf, *, name):
    S, H2 = zzpre.shape
    H = H2 // 2
    G, T, _ = wc.shape
    gd = H // G

    def body(z_ref, b_ref, vg_ref, wc_ref, bs_ref, p_ref):
        zz, _ = _gelu_and_grad(z_ref[...] + b_ref[...])
        u = zz[:, :H]
        v = zz[:, H:]
        vn = v * lax.rsqrt(jnp.mean(v * v, axis=1, keepdims=True) + EPS) * vg_ref[...]
        for g in range(G):
            gs = slice(g * gd, (g + 1) * gd)
            mixed = _dot(wc_ref[g], vn[:, gs].astype(BF16)) + bs_ref[g]
            p_ref[:, gs] = (u[:, gs] * mixed).astype(BF16)

    full3 = lambda shp: pl.BlockSpec(shp, lambda c: (0, 0, 0))
    return pl.pallas_call(
        body, name=name, grid=(S // T,),
        in_specs=[pl.BlockSpec((T, H2), lambda c: (c, 0)), pl.BlockSpec((1, H2), lambda c: (0, 0)),
                  pl.BlockSpec((1, H), lambda c: (0, 0)), full3((G, T, T)), full3((G, T, gd))],
        out_specs=pl.BlockSpec((T, H), lambda c: (c, 0)), out_shape=jax.ShapeDtypeStruct((S, H), BF16),
        compiler_params=_params(("parallel",)),
    )(zzpre, b_in, v_gain, wc, bsf)


def gmlp_bwd(zzpre, b_in, v_gain, wc, bsf, dp, *, name):
    S, H2 = zzpre.shape
    H = H2 // 2
    G, T, _ = wc.shape
    gd = H // G
    assert G <= LANES

    def body(z_ref, b_ref, vg_ref, wc_ref, bs_ref, dp_ref, dzz_ref, db_ref, dvg_ref, dws_ref, dbs_ref):
        @pl.when(pl.program_id(0) == 0)
        def _():
            db_ref[...] = jnp.zeros_like(db_ref)
            dvg_ref[...] = jnp.zeros_like(dvg_ref)
            dws_ref[...] = jnp.zeros_like(dws_ref)
            dbs_ref[...] = jnp.zeros_like(dbs_ref)

        zz, gp = _gelu_and_grad(z_ref[...] + b_ref[...])
        u = zz[:, :H]
        v = zz[:, H:]
        r = lax.rsqrt(jnp.mean(v * v, axis=1, keepdims=True) + EPS)
        vhat = v * r
        vg = vg_ref[...]
        vn = vhat * vg
        dpv = dp_ref[...]
        tril = lax.broadcasted_iota(jnp.int32, (T, T), 1) <= lax.broadcasted_iota(jnp.int32, (T, T), 0)
        lane = lax.broadcasted_iota(jnp.int32, (T, LANES), 1)
        dbs = jnp.zeros((T, LANES), F32)
        du_parts, dvn_parts = [], []
        for g in range(G):
            gs = slice(g * gd, (g + 1) * gd)
            vng = vn[:, gs].astype(BF16)
            wcg = wc_ref[g]
            mixed = _dot(wcg, vng) + bs_ref[g]
            dpg = dpv[:, gs]
            du_parts.append(dpg * mixed)
            dmx = dpg * u[:, gs]
            dmxb = dmx.astype(BF16)
            dvn_parts.append(_dot(wcg, dmxb, 0, 0))
            dws_ref[g] += jnp.where(tril, _dot(dmxb, vng, 1, 1), 0.0)
            dbs = dbs + jnp.where(lane == g, jnp.sum(dmx, axis=1, keepdims=True), 0.0)
        dbs_ref[...] += dbs
        du = jnp.concatenate(du_parts, axis=1)
        dvn = jnp.concatenate(dvn_parts, axis=1)
        dvg_ref[...] += jnp.sum(dvn * vhat, axis=0, keepdims=True)
        t = dvn * vg
        dv = r * (t - vhat * jnp.mean(vhat * t, axis=1, keepdims=True))
        dzu = du * gp[:, :H]
        dzv = dv * gp[:, H:]
        dzz_ref[:, :H] = dzu.astype(BF16)
        dzz_ref[:, H:] = dzv.astype(BF16)
        db_ref[:, :H] += jnp.sum(dzu, axis=0, keepdims=True)
        db_ref[:, H:] += jnp.sum(dzv, axis=0, keepdims=True)

    full3 = lambda shp: pl.BlockSpec(shp, lambda c: (0, 0, 0))
    vec = lambda n: pl.BlockSpec((1, n), lambda c: (0, 0))
    return pl.pallas_call(
        body, name=name, grid=(S // T,),
        in_specs=[pl.BlockSpec((T, H2), lambda c: (c, 0)), vec(H2), vec(H), full3((G, T, T)), full3((G, T, gd)),
                  pl.BlockSpec((T, H), lambda c: (c, 0))],
        out_specs=[pl.BlockSpec((T, H2), lambda c: (c, 0)), vec(H2), vec(H), full3((G, T, T)),
                   pl.BlockSpec((T, LANES), lambda c: (0, 0))],
        out_shape=[jax.ShapeDtypeStruct((S, H2), BF16), jax.ShapeDtypeStruct((1, H2), F32),
                   jax.ShapeDtypeStruct((1, H), F32), jax.ShapeDtypeStruct((G, T, T), F32),
                   jax.ShapeDtypeStruct((T, LANES), F32)],
        compiler_params=_params(("arbitrary",)),
    )(zzpre, b_in, v_gain, wc, bsf, dp)


def _shift_rows(v, k, n_rows):
    if k == 0:
        return v
    rolled = pltpu.roll(v, k % n_rows, 0)
    row = lax.broadcasted_iota(jnp.int32, v.shape, 0)
    keep = (row >= k) if k > 0 else (row < n_rows + k)
    return jnp.where(keep, rolled, 0.0)


def conv_fwd(zx, conv_w, conv_b, col0, *, name):
    S = zx.shape[0]
    C = conv_w.shape[1]
    tc = _pick(C, (256, 128))
    off = col0 // tc
    assert col0 % tc == 0

    def body(x_ref, w_ref, b_ref, o_ref):
        xv = x_ref[...]
        acc = b_ref[...] + w_ref[SSM_CONV - 1:SSM_CONV, :] * xv
        for k in range(SSM_CONV - 1):
            acc = acc + w_ref[k:k + 1, :] * _shift_rows(xv, SSM_CONV - 1 - k, S)
        o_ref[...] = acc * _sigmoid(acc)

    return pl.pallas_call(
        body, name=name, grid=(C // tc,),
        in_specs=[pl.BlockSpec((S, tc), lambda j: (0, off + j)), pl.BlockSpec((SSM_CONV, tc), lambda j: (0, j)),
                  pl.BlockSpec((1, tc), lambda j: (0, j))],
        out_specs=pl.BlockSpec((S, tc), lambda j: (0, j)), out_shape=jax.ShapeDtypeStruct((S, C), F32),
        compiler_params=_params(("parallel",)),
    )(zx, conv_w, conv_b)


def conv_bwd(zx, conv_w, conv_b, col0, douts, *, name):
    S = zx.shape[0]
    C = conv_w.shape[1]
    tc = LANES
    off = col0 // tc
    counts = [d.shape[1] // tc for d in douts]
    starts = [sum(counts[:p]) for p in range(len(douts))]
    assert sum(counts) * tc == C and all(d.shape[1] % tc == 0 for d in douts)

    def body(x_ref, w_ref, b_ref, *rest):
        do_refs, (dx_ref, dw_ref, db_ref) = rest[:len(douts)], rest[len(douts):]
        j = pl.program_id(0)
        dov = do_refs[-1][...]
        for p in reversed(range(len(douts) - 1)):
            dov = jnp.where(j < starts[p + 1], do_refs[p][...], dov)
        xv = x_ref[...]
        shifted = [_shift_rows(xv, SSM_CONV - 1 - k, S) for k in range(SSM_CONV)]
        acc = b_ref[...]
        for k in range(SSM_CONV):
            acc = acc + w_ref[k:k + 1, :] * shifted[k]
        s = _sigmoid(acc)
        dacc = dov * (s * (1.0 + acc * (1.0 - s)))
        db_ref[...] = jnp.sum(dacc, axis=0, keepdims=True)
        dx = jnp.zeros_like(xv)
        for k in range(SSM_CONV):
            dw_ref[k:k + 1, :] = jnp.sum(dacc * shifted[k], axis=0, keepdims=True)
            dx = dx + w_ref[k:k + 1, :] * _shift_rows(dacc, -(SSM_CONV - 1 - k), S)
        dx_ref[...] = dx.astype(BF16)

    slab = pl.BlockSpec((S, tc), lambda j: (0, j))
    piece_specs = [pl.BlockSpec((S, tc), lambda j, a=starts[p], n=counts[p]: (0, jnp.clip(j - a, 0, n - 1)))
                   for p in range(len(douts))]
    return pl.pallas_call(
        body, name=name, grid=(C // tc,),
        in_specs=[pl.BlockSpec((S, tc), lambda j: (0, off + j)), pl.BlockSpec((SSM_CONV, tc), lambda j: (0, j)),
                  pl.BlockSpec((1, tc), lambda j: (0, j))] + piece_specs,
        out_specs=[slab, pl.BlockSpec((SSM_CONV, tc), lambda j: (0, j)), pl.BlockSpec((1, tc), lambda j: (0, j))],
        out_shape=[jax.ShapeDtypeStruct((S, C), BF16), jax.ShapeDtypeStruct((SSM_CONV, C), F32),
                   jax.ShapeDtypeStruct((1, C), F32)],
        compiler_params=_params(("arbitrary",)),
    )(zx, conv_w, conv_b, *douts)


def _ssd_chunk_terms(dtraw, bias, a_log, tl):
    dt = _softplus(dtraw + bias)
    a_neg = -jnp.exp(a_log)
    ac = _dot_exact_left(tl, dt * a_neg)
    ac_last = ac[ac.shape[0] - 1:, :]
    return dt, a_neg, ac, ac.T, jnp.exp(ac), jnp.exp(ac_last - ac), jnp.exp(ac_last)


def _ssd_specs(S, L, G, hpg, pd, inner, gp):
    gw = hpg * pd
    n = SSM_STATE
    xb = inner // n
    assert G % gp == 0 and xb % gp == 0 and (xb + G) % gp == 0

    def mk(cidx):
        return dict(
            x=pl.BlockSpec((L, gp * gw), lambda g, c: (cidx(c), g)),
            b=pl.BlockSpec((L, gp * n), lambda g, c: (cidx(c), xb // gp + g)),
            c=pl.BlockSpec((L, gp * n), lambda g, c: (cidx(c), (xb + G) // gp + g)),
            z=pl.BlockSpec((L, gp * gw), lambda g, c: (cidx(c), g)),
            dt=pl.BlockSpec((L, gp * LANES), lambda g, c: (cidx(c), g)),
            gvec=pl.BlockSpec((gp, 1, LANES), lambda g, c: (g, 0, 0)),
            chan=pl.BlockSpec((1, gp * gw), lambda g, c: (0, g)),
            tri=pl.BlockSpec((L, L), lambda g, c: (0, 0)),
            hp=pl.BlockSpec((gp, 1, gw, n), lambda g, c: (g, cidx(c), 0, 0)),
            bc=pl.BlockSpec((L, gp * n), lambda g, c: (cidx(c), g)),
        )
    return mk


def _ssd_group_views(refs, kinds, gg, gw):
    n = SSM_STATE
    width = dict(x=gw, z=gw, chan=gw, b=n, c=n, bc=n, dt=LANES)
    out = []
    for ref, kind in zip(refs, kinds):
        if kind in width:
            out.append(ref.at[:, gg * width[kind]:(gg + 1) * width[kind]])
        elif kind in ("gvec", "hp"):
            out.append(ref.at[gg:gg + 1])
        elif kind == "state":
            out.append(ref.at[gg * gw:(gg + 1) * gw])
        else:
            out.append(ref)
    return out


def ssd_fwd(xbc, zx, dtg, bias_g, alog_g, d_chan, ngain, L, G, *, name):
    S = xbc.shape[0]
    n = SSM_STATE
    inner = xbc.shape[1] - 2 * G * n
    gw = inner // G
    pd = SB_HEAD_DIM
    hpg = gw // pd
    nc = S // L
    gp = SSD_FWD_GROUPS_PER_STEP
    sp = _ssd_specs(S, L, G, hpg, pd, inner, gp)(lambda c: c)

    kinds = ("x", "b", "c", "z", "dt", "gvec", "gvec", "chan", "chan", "tri", "x", "x", "hp", "state")

    def body(*refs):
        @pl.when(pl.program_id(1) == 0)
        def _():
            refs[-1][...] = jnp.zeros_like(refs[-1])

        for gg in range(gp):
            group_body(*_ssd_group_views(refs, kinds, gg, gw))

    def group_body(x_ref, b_ref, c_ref, z_ref, dt_ref, bias_ref, alog_ref, d_ref, ng_ref, tl_ref,
                   yn_ref, y_ref, hp_ref, state):
        dt, _, ac, act, ea, dte, cd = _ssd_chunk_terms(dt_ref[...], bias_ref[0], alog_ref[0], tl_ref[...])
        xv = x_ref[...]
        bm = b_ref[...].astype(BF16)
        cm = c_ref[...].astype(BF16)
        cb = _dot(cm, bm, 1, 1)
        tril = lax.broadcasted_iota(jnp.int32, (L, L), 1) <= lax.broadcasted_iota(jnp.int32, (L, L), 0)
        hp_ref[0, 0] = state[...]
        for r in range(hpg):
            ps = slice(r * pd, (r + 1) * pd)
            xr = xv[:, ps]
            xdt = xr * dt[:, r:r + 1]
            lm = jnp.exp(jnp.where(tril, ac[:, r:r + 1] - act[r:r + 1, :], -jnp.inf))
            hprev = state[ps, :]
            y = _dot((cb * lm).astype(BF16), xdt.astype(BF16))
            y = y + _dot(cm, hprev.astype(BF16), 1, 1) * ea[:, r:r + 1]
            y_ref[:, ps] = y + xr * d_ref[:, ps]
            st = _dot((xdt * dte[:, r:r + 1]).astype(BF16), bm, 0, 0)
            state[ps, :] = hprev * cd[:, r:r + 1] + st
        yfull = y_ref[...]
        zg = z_ref[...]
        yg = yfull * (zg * _sigmoid(zg))
        yn_ref[...] = (yg * lax.rsqrt(jnp.mean(yg * yg, axis=1, keepdims=True) + EPS) * ng_ref[...]).astype(BF16)

    return pl.pallas_call(
        body, name=name, grid=(G // gp, nc),
        in_specs=[sp["x"], sp["b"], sp["c"], sp["z"], sp["dt"], sp["gvec"], sp["gvec"], sp["chan"], sp["chan"], sp["tri"]],
        out_specs=[sp["x"], sp["x"], sp["hp"]],
        out_shape=[jax.ShapeDtypeStruct((S, inner), BF16), jax.ShapeDtypeStruct((S, inner), F32),
                   jax.ShapeDtypeStruct((G, nc, gw, n), F32)],
        scratch_shapes=[pltpu.VMEM((gp * gw, n), F32)],
        compiler_params=_params(("arbitrary", "arbitrary")),
    )(xbc, xbc, xbc, zx, dtg, bias_g, alog_g, d_chan, ngain, _tri(L, "row_ge_col"))


def ssd_bwd(xbc, zx, dtg, bias_g, alog_g, d_chan, ngain, yfull, hp, dyn, L, G, *, name):
    S = xbc.shape[0]
    n = SSM_STATE
    inner = xbc.shape[1] - 2 * G * n
    gw = inner // G
    pd = SB_HEAD_DIM
    hpg = gw // pd
    nc = S // L
    gp = SSD_BWD_GROUPS_PER_STEP
    sp = _ssd_specs(S, L, G, hpg, pd, inner, gp)(lambda c: nc - 1 - c)

    kinds = ("x", "b", "c", "z", "dt", "gvec", "gvec", "chan", "chan", "tri", "tri", "x", "hp", "x",
             "x", "x", "bc", "bc", "dt", "gvec", "gvec", "gvec", "chan", "state")

    def body(*refs):
        @pl.when(pl.program_id(1) == 0)
        def _():
            for acc in refs[-5:]:
                acc[...] = jnp.zeros_like(acc)

        for gg in range(gp):
            group_body(*_ssd_group_views(refs, kinds, gg, gw))

    def group_body(x_ref, b_ref, c_ref, z_ref, dt_ref, bias_ref, alog_ref, d_ref, ng_ref, tl_ref, tu_ref,
                   yf_ref, hp_ref, dyn_ref,
                   dz_ref, dx_ref, db_ref, dc_ref, ddt_ref, dbias_ref, dalog_ref, dd_ref, dng_ref, dstate):

        dtraw = dt_ref[...]
        dt, a_neg, ac, act, ea, dte, cd = _ssd_chunk_terms(dtraw, bias_ref[0], alog_ref[0], tl_ref[...])
        xv = x_ref[...]
        bm = b_ref[...].astype(BF16)
        cm = c_ref[...].astype(BF16)
        cb = _dot(cm, bm, 1, 1)
        tril = lax.broadcasted_iota(jnp.int32, (L, L), 1) <= lax.broadcasted_iota(jnp.int32, (L, L), 0)
        lane = lax.broadcasted_iota(jnp.int32, (L, LANES), 1)
        lane1 = lax.broadcasted_iota(jnp.int32, (1, LANES), 1)

        yfull = yf_ref[...]
        zg = z_ref[...]
        sg = _sigmoid(zg)
        gate = zg * sg
        yg = yfull * gate
        rr = lax.rsqrt(jnp.mean(yg * yg, axis=1, keepdims=True) + EPS)
        yhat = yg * rr
        dynv = dyn_ref[...]
        dng_ref[...] += jnp.sum(dynv * yhat, axis=0, keepdims=True)
        t = dynv * ng_ref[...]
        dyg = rr * (t - yhat * jnp.mean(yhat * t, axis=1, keepdims=True))
        dy = dyg * gate
        dz_ref[...] = (dyg * yfull * (sg * (1.0 + zg * (1.0 - sg)))).astype(BF16)

        dcb = jnp.zeros((L, L), F32)
        dc_acc = jnp.zeros((L, n), F32)
        db_acc = jnp.zeros((L, n), F32)
        dac = jnp.zeros((L, LANES), F32)
        xdx = jnp.zeros((L, LANES), F32)
        tail = jnp.zeros((1, LANES), F32)
        dskip = jnp.zeros((1, LANES), F32)
        ones_l = jnp.ones((L, LANES), BF16)
        for r in range(hpg):
            ps = slice(r * pd, (r + 1) * pd)
            xr = xv[:, ps]
            dyr = dy[:, ps]
            dtr = dt[:, r:r + 1]
            dter = dte[:, r:r + 1]
            cdr = cd[:, r:r + 1]
            xdt = xr * dtr
            xdtb = xdt.astype(BF16)
            dyrb = dyr.astype(BF16)
            lm = jnp.exp(jnp.where(tril, ac[:, r:r + 1] - act[r:r + 1, :], -jnp.inf))
            m32 = cb * lm
            mb = m32.astype(BF16)
            hprev = hp_ref[0, 0, ps, :]
            hpb = hprev.astype(BF16)
            dhn = dstate[ps, :]
            dhnb = dhn.astype(BF16)
            ear = ea[:, r:r + 1]
            gy = (dyr * ear).astype(BF16)
            dc_acc = dc_acc + _dot(gy, hpb)
            dstate[ps, :] = _dot(gy, cm, 0, 0) + dhn * cdr
            bdh = _dot(bm, dhnb, 1, 1)
            db_acc = db_acc + _dot((xdt * dter).astype(BF16), dhnb)
            dm = _dot(dyrb, xdtb, 1, 1)
            dxdt = bdh * dter + _dot(mb, dyrb, 0, 0)
            dcb = dcb + dm * lm
            wmat = dm * m32
            whi = wmat.astype(BF16)
            wlo = (wmat - whi.astype(F32)).astype(BF16)
            col_w = _dot(whi, ones_l, 0, 0) + _dot(wlo, ones_l, 0, 0)
            t_end = xdt * bdh * dter
            e_r = jnp.sum(wmat, axis=1, keepdims=True) \
                + jnp.sum(dyr * _dot(cm, hpb, 1, 1) * ear - t_end, axis=1, keepdims=True)
            c_r = cdr * _sum_all(dhn * hprev) + _sum_all(t_end)
            dac = dac + jnp.where(lane == r, e_r - col_w, 0.0)
            xdx = xdx + jnp.where(lane == r, jnp.sum(dxdt * xr, axis=1, keepdims=True), 0.0)
            tail = tail + jnp.where(lane1 == r, c_r, 0.0)
            dskip = dskip + jnp.where(lane1 == r, _sum_all(dyr * xr), 0.0)
            dx_ref[:, ps] = dxdt * dtr + dyr * d_ref[:, ps]
        dcbb = dcb.astype(BF16)
        dc_ref[...] = dc_acc + _dot(dcbb, bm)
        db_ref[...] = db_acc + _dot(dcbb, cm, 0, 0)
        da = _dot_exact_left(tu_ref[...], dac) + tail
        real = lane < hpg
        ddt = jnp.where(real, (da * a_neg + xdx) * _sigmoid(dtraw + bias_ref[0]), 0.0)
        ddt_ref[...] = ddt
        dd_ref[0] += dskip
        dbias_ref[0] += jnp.sum(ddt, axis=0, keepdims=True)
        dalog_ref[0] += jnp.where(lane1 < hpg, jnp.sum(da * dt, axis=0, keepdims=True) * a_neg, 0.0)

    return pl.pallas_call(
        body, name=name, grid=(G // gp, nc),
        in_specs=[sp["x"], sp["b"], sp["c"], sp["z"], sp["dt"], sp["gvec"], sp["gvec"], sp["chan"], sp["chan"],
                  sp["tri"], sp["tri"], sp["x"], sp["hp"], sp["x"]],
        out_specs=[sp["x"], sp["x"], sp["bc"], sp["bc"], sp["dt"], sp["gvec"], sp["gvec"], sp["gvec"], sp["chan"]],
        out_shape=[jax.ShapeDtypeStruct((S, inner), BF16), jax.ShapeDtypeStruct((S, inner), F32),
                   jax.ShapeDtypeStruct((S, G * n), F32), jax.ShapeDtypeStruct((S, G * n), F32),
                   jax.ShapeDtypeStruct((S, G * LANES), F32), jax.ShapeDtypeStruct((G, 1, LANES), F32),
                   jax.ShapeDtypeStruct((G, 1, LANES), F32), jax.ShapeDtypeStruct((G, 1, LANES), F32),
                   jax.ShapeDtypeStruct((1, inner), F32)],
        scratch_shapes=[pltpu.VMEM((gp * gw, n), F32)],
        compiler_params=_params(("arbitrary", "arbitrary")),
    )(xbc, xbc, xbc, zx, dtg, bias_g, alog_g, d_chan, ngain, _tri(L, "row_ge_col"), _tri(L, "row_le_col"),
      yfull, hp, dyn)


def _spread_dt(w_dt_t, G, hpg):
    K = w_dt_t.shape[1]
    w = w_dt_t.reshape(G, hpg, K)
    return jnp.pad(w, ((0, 0), (0, LANES - hpg), (0, 0))).reshape(G * LANES, K)


def _group_vec(v, G, hpg):
    return jnp.pad(v.reshape(G, 1, hpg), ((0, 0), (0, 0), (0, LANES - hpg)))


def local_step(x, target, W, late=None):
    S, D = x.shape
    depth = W["mix_norm"].shape[0]
    gm_groups, gm_chunk = W["gm_w_s"].shape[1], W["gm_w_s"].shape[2]
    heads = W["ssm_dt_bias"].shape[1]
    inner = heads * SB_HEAD_DIM
    L = gm_chunk
    received = None

    saved = []
    for i in range(depth):
        kind, j = i % 3, i // 3
        s = dict(x=x)
        if i == 0:
            h = rms_fwd(x, W["mix_norm"][i:i + 1], name="rms_mix_fwd")
        s["h"] = h
        ffn_gain = W["ffn_norm"][i:i + 1]
        if kind == 0:
            qkv = matmul(h, W["sb_w_qkv"][j], tb=True, name="mm_qkv")
            if late is not None and i == 0:
                o, gathered = sb_attn_fwd(qkv, W["sb_q_gain"][j:j + 1], W["sb_k_gain"][j:j + 1], name="sb_fwd_gather",
                                          gather=late.shards)
                late.fill(W, gathered)
            else:
                o = sb_attn_fwd(qkv, W["sb_q_gain"][j:j + 1], W["sb_k_gain"][j:j + 1], name="sb_fwd")
            x1, h2 = matmul(o, W["sb_w_o"][j], residual=x, norm_fwd=ffn_gain, name="mm_sb_out")
            s.update(qkv=qkv, o=o)
        elif kind == 1:
            wc = jnp.where(jnp.tril(jnp.ones((gm_chunk, gm_chunk), bool)), W["gm_w_s"][j], 0.0).astype(BF16)
            bsf = jnp.broadcast_to(W["gm_b_s"][j][:, :, None], (gm_groups, gm_chunk, W["gm_v_gain"].shape[1] // gm_groups)).astype(F32)
            zzpre = matmul(h, W["gm_w_in"][j], tb=True, name="mm_gm_in")
            p = gmlp_fwd(zzpre, W["gm_b_in"][j:j + 1], W["gm_v_gain"][j:j + 1], wc, bsf, name="gm_fwd")
            x1, h2 = matmul(p, W["gm_w_out"][j], residual=x, norm_fwd=ffn_gain, name="mm_gm_out")
            s.update(zzpre=zzpre, p=p, wc=wc, bsf=bsf)
        else:
            conv_dim = W["ssm_conv_w"].shape[2]
            G = (conv_dim - inner) // (2 * SSM_STATE)
            hpg = heads // G
            w_in = W["ssm_w_in"][j]
            w_zx = w_in[:inner + conv_dim]
            w_dtg = _spread_dt(w_in[inner + conv_dim:], G, hpg)
            bias_g = _group_vec(W["ssm_dt_bias"][j], G, hpg)
            alog_g = _group_vec(W["ssm_a_log"][j], G, hpg)
            d_chan = jnp.repeat(W["ssm_d"][j], SB_HEAD_DIM)[None, :]
            ngain = W["ssm_norm_gain"][j:j + 1]
            zx = matmul(h, w_zx, tb=True, name="mm_ssm_zx")
            dtg = matmul(h, w_dtg, tb=True, name="mm_ssm_dt")
            xbc = conv_fwd(zx, W["ssm_conv_w"][j], W["ssm_conv_b"][j:j + 1], inner, name="conv_fwd")
            yn, yfull, hp = ssd_fwd(xbc, zx, dtg, bias_g, alog_g, d_chan, ngain, L, G, name="ssd_fwd")
            x1, h2 = matmul(yn, W["ssm_w_out"][j], residual=x, norm_fwd=ffn_gain, name="mm_ssm_out")
            s.update(w_zx=w_zx, w_dtg=w_dtg, bias_g=bias_g, alog_g=alog_g, d_chan=d_chan, ngain=ngain,
                     zx=zx, dtg=dtg, xbc=xbc, yn=yn, yfull=yfull, hp=hp)
        gu, a = ffn_up_fwd(h2, W["ffn_w_gu"][i], name="ffn_up_fwd")
        if i + 1 < depth:
            x2, h = matmul(a, W["ffn_w_down"][i], residual=x1, norm_fwd=W["mix_norm"][i + 1:i + 2], name="mm_ffn_down")
        else:
            x2 = matmul(a, W["ffn_w_down"][i], residual=x1, name="mm_ffn_down_last")
        s.update(x1=x1, h2=h2, gu=gu, a=a)
        saved.append(s)
        x = x2

    dx, loss = loss_head(x, target, name="loss_head")

    gw = {k: {} for k in WEIGHTS}
    for i in reversed(range(depth)):
        kind, j = i % 3, i // 3
        s = saved[i]
        gw["ffn_w_down"][i] = matmul(s["a"], dx, ta=True, out_dtype=BF16, name="mm_ffn_dwdown")
        dgu = ffn_up_bwd(dx, W["ffn_w_down"][i], s["gu"], name="ffn_up_bwd")
        dx1, dgn = matmul(dgu, W["ffn_w_gu"][i], a_split=2, norm_bwd=(s["x1"], W["ffn_norm"][i:i + 1], dx),
                          name="mm_ffn_dh")
        gw["ffn_w_gu"][i] = matmul(dgu, s["h2"], ta=True, a_split=2, out_dtype=BF16, name="mm_ffn_dwgu")
        gw["ffn_norm"][i] = dgn[0]
        mix_norm_bwd = (s["x"], W["mix_norm"][i:i + 1], dx1)
        if kind == 0:
            do = matmul(dx1, W["sb_w_o"][j], tb=True, out_dtype=BF16, name="mm_sb_do")
            gw["sb_w_o"][j] = matmul(s["o"], dx1, ta=True, out_dtype=BF16, name="mm_sb_dwo")
            if late is not None and i == 0:
                dqkv, dqg, dkg, received = sb_attn_bwd(
                    s["qkv"], s["o"], do, W["sb_q_gain"][j:j + 1], W["sb_k_gain"][j:j + 1], name="sb_bwd_scatter",
                    scatter=late.contributions(gw))
            else:
                dqkv, dqg, dkg = sb_attn_bwd(s["qkv"], s["o"], do, W["sb_q_gain"][j:j + 1], W["sb_k_gain"][j:j + 1],
                                             name="sb_bwd")
            gw["sb_q_gain"][j] = dqg[0]
            gw["sb_k_gain"][j] = dkg[0]
            dx, dgn = matmul(dqkv, W["sb_w_qkv"][j], a_split=3, norm_bwd=mix_norm_bwd, name="mm_sb_dh")
            gw["sb_w_qkv"][j] = matmul(dqkv, s["h"], ta=True, a_split=3, out_dtype=BF16, name="mm_sb_dwqkv")
        elif kind == 1:
            dp = matmul(dx1, W["gm_w_out"][j], tb=True, name="mm_gm_dp")
            gw["gm_w_out"][j] = matmul(s["p"], dx1, ta=True, out_dtype=BF16, name="mm_gm_dwout")
            dzz, db_in, dvg, dws, dbs = gmlp_bwd(s["zzpre"], W["gm_b_in"][j:j + 1], W["gm_v_gain"][j:j + 1],
                                                s["wc"], s["bsf"], dp, name="gm_bwd")
            gw["gm_b_in"][j] = db_in[0]
            gw["gm_v_gain"][j] = dvg[0]
            gw["gm_w_s"][j] = dws
            gw["gm_b_s"][j] = dbs[:, :gm_groups].T
            dx, dgn = matmul(dzz, W["gm_w_in"][j], norm_bwd=mix_norm_bwd, name="mm_gm_dh")
            gw["gm_w_in"][j] = matmul(dzz, s["h"], ta=True, out_dtype=BF16, name="mm_gm_dwin")
        else:
            conv_dim = W["ssm_conv_w"].shape[2]
            G = (conv_dim - inner) // (2 * SSM_STATE)
            hpg = heads // G
            dyn = matmul(dx1, W["ssm_w_out"][j], tb=True, name="mm_ssm_dyn")
            gw["ssm_w_out"][j] = matmul(s["yn"], dx1, ta=True, out_dtype=BF16, name="mm_ssm_dwout")
            dz, dxs, dbm, dcm, ddt, dbias, dalog, dd, dng = ssd_bwd(
                s["xbc"], s["zx"], s["dtg"], s["bias_g"], s["alog_g"], s["d_chan"], s["ngain"], s["yfull"], s["hp"],
                dyn, L, G, name="ssd_bwd")
            dpre, dcw, dcb = conv_bwd(s["zx"], W["ssm_conv_w"][j], W["ssm_conv_b"][j:j + 1], inner, [dxs, dbm, dcm],
                                      name="conv_bwd")
            dzx = jnp.concatenate([dz, dpre], axis=1)
            dh = matmul(ddt, s["w_dtg"], name="mm_ssm_dh_dt")
            dx, dgn = matmul(dzx, s["w_zx"], residual=dh, norm_bwd=mix_norm_bwd, name="mm_ssm_dh")
            dw_zx = matmul(dzx, s["h"], ta=True, out_dtype=BF16, name="mm_ssm_dwzx")
            dw_dtg = matmul(ddt, s["h"], ta=True, out_dtype=BF16, name="mm_ssm_dwdt")
            dw_dt = dw_dtg.reshape(G, LANES, D)[:, :hpg, :].reshape(heads, D)
            gw["ssm_w_in"][j] = jnp.concatenate([dw_zx, dw_dt], axis=0)
            gw["ssm_conv_w"][j] = dcw
            gw["ssm_conv_b"][j] = dcb[0]
            gw["ssm_dt_bias"][j] = dbias[:, 0, :hpg].reshape(heads)
            gw["ssm_a_log"][j] = dalog[:, 0, :hpg].reshape(heads)
            gw["ssm_d"][j] = dd[:, 0, :hpg].reshape(heads)
            gw["ssm_norm_gain"][j] = dng[0]
        gw["mix_norm"][i] = dgn[0]

    return loss, dx, gw, received


MESH = pl.DeviceIdType.MESH
HBM_SPEC = pl.BlockSpec(memory_space=pltpu.HBM)
VMEM_SPEC = pl.BlockSpec(memory_space=pltpu.VMEM)


def _my_position():
    return lax.axis_index("x"), lax.axis_index("y"), lax.axis_index("c")


def _flip(v, bit):
    return 1 - v if bit else v


def all_gather(shards, *, name):
    n = len(shards)

    def body(*refs):
        for phase in ("start", "forward", "finish"):
            _ag_phase(refs[:n], refs[n:2 * n], refs[2 * n:], phase)

    return pl.pallas_call(
        body, name=name, out_shape=[jax.ShapeDtypeStruct((N_DEV,) + s.shape, s.dtype) for s in shards],
        in_specs=[HBM_SPEC] * n, out_specs=[HBM_SPEC] * n, scratch_shapes=_copy_semaphores(n),
    )(*shards)


def _ag_phase(x_refs, out_refs, sems, phase):
    send_sems, recv_sems, local_sems = sems
    x, y, c = _my_position()
    me, sibling = (x, y, c), (x, y, 1 - c)
    chips = [(1 - x, y), (x, 1 - y), (1 - x, 1 - y)]
    for p, (x_ref, out_ref) in enumerate(zip(x_refs, out_refs)):
        def slot(px, py, pc):
            return out_ref.at[4 * px + 2 * py + pc]

        def copy(k, block, to, src=None):
            return pltpu.make_async_remote_copy(
                src_ref=slot(*block) if src is None else src, dst_ref=slot(*block),
                send_sem=send_sems.at[7 * p + k], recv_sem=recv_sems.at[7 * p + k], device_id=to, device_id_type=MESH)

        mine = pltpu.make_async_copy(x_ref, slot(*me), local_sems.at[p])
        first = [copy(0, me, sibling, src=x_ref)]
        first += [copy(1 + j, me, (*chip, c), src=x_ref) for j, chip in enumerate(chips)]
        passed = [copy(4 + j, (*chip, c), sibling) for j, chip in enumerate(chips)]
        if phase == "start":
            mine.start()
            for cp in first:
                cp.start()
        elif phase == "forward":
            for j, chip in enumerate(chips):
                copy(1 + j, (*chip, c), me).wait_recv()
                passed[j].start()
        else:
            copy(0, sibling, me).wait_recv()
            for j, chip in enumerate(chips):
                copy(4 + j, (*chip, 1 - c), me).wait_recv()
            for cp in first + passed:
                cp.wait_send()
            mine.wait()


def _copy_semaphores(n):
    return [pltpu.SemaphoreType.DMA((7 * n,)), pltpu.SemaphoreType.DMA((7 * n,)), pltpu.SemaphoreType.DMA((n,))]


def _rs_phase(g_refs, out_refs, sems, phase):
    send_sems, recv_sems, local_sems = sems
    x, y, c = _my_position()
    me = 4 * x + 2 * y + c
    copies = []
    for p, (g_ref, out_ref) in enumerate(zip(g_refs, out_refs)):
        copies.append(pltpu.make_async_copy(g_ref.at[me], out_ref.at[me], local_sems.at[p]))
        for k in range(1, N_DEV):
            px, py, pc = _flip(x, k & 4), _flip(y, k & 2), _flip(c, k & 1)
            copies.append(pltpu.make_async_remote_copy(
                src_ref=g_ref.at[4 * px + 2 * py + pc], dst_ref=out_ref.at[me],
                send_sem=send_sems.at[7 * p + k - 1], recv_sem=recv_sems.at[7 * p + k - 1],
                device_id=(px, py, pc), device_id_type=MESH))
    for cp in copies:
        if phase == "start":
            cp.start()
        else:
            cp.wait()


def sum_slots(recv, *, name):
    n, R, C = recv.shape
    tr = _pick(R, (512, 256, 128))

    def body(r_ref, o_ref):
        acc = r_ref[0].astype(F32)
        for s in range(1, n):
            acc = acc + r_ref[s].astype(F32)
        o_ref[...] = acc

    return pl.pallas_call(
        body, name=name, grid=(R // tr,), in_specs=[pl.BlockSpec((n, tr, C), lambda i: (0, i, 0))],
        out_specs=pl.BlockSpec((tr, C), lambda i: (i, 0)), out_shape=jax.ShapeDtypeStruct((R, C), F32),
        compiler_params=_params(("parallel",)),
    )(recv)


def all_reduce_small(vs, scatter, *, name):
    n, ns = len(vs), len(scatter)

    def body(*refs):
        v_refs, g_refs = refs[:n], refs[n:n + ns]
        o_refs, r_refs = refs[n + ns:2 * n + ns], refs[2 * n + ns:2 * n + 2 * ns]
        bufs = refs[2 * n + 2 * ns:3 * n + 2 * ns]
        send_sems, recv_sems = refs[3 * n + 2 * ns:3 * n + 2 * ns + 2]
        rs_sems = refs[3 * n + 2 * ns + 2:]
        x, y, c = _my_position()
        me = 4 * x + 2 * y + c
        _rs_phase(g_refs, r_refs, rs_sems, "start")
        copies = []
        for p, (v_ref, buf) in enumerate(zip(v_refs, bufs)):
            buf[me] = v_ref[...]
            for k in range(1, N_DEV):
                px, py, pc = _flip(x, k & 4), _flip(y, k & 2), _flip(c, k & 1)
                copies.append(pltpu.make_async_remote_copy(
                    src_ref=v_ref, dst_ref=buf.at[me], send_sem=send_sems.at[7 * p + k - 1],
                    recv_sem=recv_sems.at[7 * p + k - 1], device_id=(px, py, pc), device_id_type=MESH))
        for cp in copies:
            cp.start()
        for cp in copies:
            cp.wait()
        for o_ref, buf in zip(o_refs, bufs):
            acc = buf[0]
            for s in range(1, N_DEV):
                acc = acc + buf[s]
            o_ref[...] = acc
        _rs_phase(g_refs, r_refs, rs_sems, "finish")

    out = pl.pallas_call(
        body, name=name,
        out_shape=[jax.ShapeDtypeStruct(v.shape, F32) for v in vs] + [jax.ShapeDtypeStruct(g.shape, g.dtype) for g in scatter],
        in_specs=[VMEM_SPEC] * n + [HBM_SPEC] * ns, out_specs=[VMEM_SPEC] * n + [HBM_SPEC] * ns,
        scratch_shapes=[pltpu.VMEM((N_DEV,) + v.shape, F32) for v in vs]
        + [pltpu.SemaphoreType.DMA((7 * n,)), pltpu.SemaphoreType.DMA((7 * n,))] + _copy_semaphores(ns),
        compiler_params=pltpu.CompilerParams(vmem_limit_bytes=VMEM_LIMIT_BYTES),
    )(*vs, *scatter)
    return list(out[:n]), list(out[n:])


def _pad_rows(a, mult):
    pad = (-a.shape[0]) % mult
    return jnp.pad(a, ((0, pad), (0, 0))) if pad else a


def _pack_small(arrays):
    flat = []
    for a in arrays:
        f = a.reshape(-1).astype(F32)
        flat.append(jnp.pad(f, (0, (-f.shape[0]) % LANES)))
    return _pad_rows(jnp.concatenate(flat).reshape(-1, LANES), 8)


def _unpack_small(packed, shapes):
    flat = packed.reshape(-1)
    out, r = [], 0
    for shp in shapes:
        n = math.prod(shp)
        out.append(flat[r:r + n].reshape(shp))
        r += n + (-n) % LANES
    return out


ARG_NAMES = ("x",) + WEIGHTS + ("loss_target",) + tuple("m_" + w for w in WEIGHTS) + tuple("v_" + w for w in WEIGHTS)


def kernel(x, mix_norm, ffn_norm, sb_w_qkv, sb_q_gain, sb_k_gain, sb_w_o, gm_w_in, gm_b_in, gm_v_gain, gm_w_s, gm_b_s, gm_w_out, ssm_w_in, ssm_conv_w, ssm_conv_b, ssm_dt_bias, ssm_a_log, ssm_d, ssm_norm_gain, ssm_w_out, ffn_w_gu, ffn_w_down, loss_target, m_mix_norm, m_ffn_norm, m_sb_w_qkv, m_sb_q_gain, m_sb_k_gain, m_sb_w_o, m_gm_w_in, m_gm_b_in, m_gm_v_gain, m_gm_w_s, m_gm_b_s, m_gm_w_out, m_ssm_w_in, m_ssm_conv_w, m_ssm_conv_b, m_ssm_dt_bias, m_ssm_a_log, m_ssm_d, m_ssm_norm_gain, m_ssm_w_out, m_ffn_w_gu, m_ffn_w_down, v_mix_norm, v_ffn_norm, v_sb_w_qkv, v_sb_q_gain, v_sb_k_gain, v_sb_w_o, v_gm_w_in, v_gm_b_in, v_gm_v_gain, v_gm_w_s, v_gm_b_s, v_gm_w_out, v_ssm_w_in, v_ssm_conv_w, v_ssm_conv_b, v_ssm_dt_bias, v_ssm_a_log, v_ssm_d, v_ssm_norm_gain, v_ssm_w_out, v_ffn_w_gu, v_ffn_w_down):
    given = dict(zip(ARG_NAMES, (x, mix_norm, ffn_norm, sb_w_qkv, sb_q_gain, sb_k_gain, sb_w_o, gm_w_in, gm_b_in, gm_v_gain, gm_w_s, gm_b_s, gm_w_out, ssm_w_in, ssm_conv_w, ssm_conv_b, ssm_dt_bias, ssm_a_log, ssm_d, ssm_norm_gain, ssm_w_out, ffn_w_gu, ffn_w_down, loss_target, m_mix_norm, m_ffn_norm, m_sb_w_qkv, m_sb_q_gain, m_sb_k_gain, m_sb_w_o, m_gm_w_in, m_gm_b_in, m_gm_v_gain, m_gm_w_s, m_gm_b_s, m_gm_w_out, m_ssm_w_in, m_ssm_conv_w, m_ssm_conv_b, m_ssm_dt_bias, m_ssm_a_log, m_ssm_d, m_ssm_norm_gain, m_ssm_w_out, m_ffn_w_gu, m_ffn_w_down, v_mix_norm, v_ffn_norm, v_sb_w_qkv, v_sb_q_gain, v_sb_k_gain, v_sb_w_o, v_gm_w_in, v_gm_b_in, v_gm_v_gain, v_gm_w_s, v_gm_b_s, v_gm_w_out, v_ssm_w_in, v_ssm_conv_w, v_ssm_conv_b, v_ssm_dt_bias, v_ssm_a_log, v_ssm_d, v_ssm_norm_gain, v_ssm_w_out, v_ffn_w_gu, v_ffn_w_down)))
    mx, my, mc = _my_position()
    me = 4 * mx + 2 * my + mc

    pieces = [(k, l) for k in BIG for l in range(given[k].shape[0])]
    early = [("sb_w_qkv", 0)]
    late_pieces = [p for p in pieces if p not in early]
    last = [("sb_w_qkv", 0)]
    main = [p for p in pieces if p not in last]

    def shards_of(ps):
        return [(given[k][l].T if k in COL_SHARDED else given[k][l]).astype(BF16) for k, l in ps]

    def piece_to_full(g, k):
        rows, cols = given[k].shape[1:]
        return g.reshape(N_DEV * cols, rows) if k in COL_SHARDED else g.reshape(N_DEV * rows, cols)

    def full_to_piece(full, k):
        return full.reshape(N_DEV, -1, PACK_COLS)

    def summed_to_shard(g, k):
        rows, cols = given[k].shape[1:]
        return g.reshape(cols, rows) if k in COL_SHARDED else g.reshape(rows, cols)

    def contributions(gw, ps):
        return [full_to_piece(gw[k][l], k) for k, l in ps]

    sharded_small = [lax.bitcast_convert_type(given[k], BF16) for k in SMALL_SHARDED]
    tail = jnp.concatenate([a.reshape(-1) for a in sharded_small])
    tail = jnp.pad(tail, (0, (-tail.size) % PACK_COLS)).reshape(-1, PACK_COLS)

    W = {k: given[k] for k in SMALL if k not in SMALL_SHARDED}
    W.update({k: [None] * given[k].shape[0] for k in BIG})
    for (k, l), g in zip(early, all_gather(shards_of(early), name="all_gather_early")):
        W[k][l] = piece_to_full(g, k)

    class Late:
        shards = shards_of(late_pieces) + [tail]

        @staticmethod
        def fill(weights, gathered):
            for (k, l), g in zip(late_pieces, gathered):
                weights[k][l] = piece_to_full(g, k)
            tail_g = gathered[-1].reshape(N_DEV, -1)
            off = 0
            for k, a in zip(SMALL_SHARDED, sharded_small):
                g = lax.bitcast_convert_type(tail_g[:, off:off + a.size].reshape((N_DEV,) + a.shape), F32)
                weights[k] = jnp.moveaxis(g, 0, -2).reshape(g.shape[1:-1] + (N_DEV * g.shape[-1],))
                off += a.size

        @staticmethod
        def contributions(gw):
            return contributions(gw, main)

    loss, gx, gw, received_main = local_step(given["x"][0], given["loss_target"][0], W, late=Late)

    grads_small = {k: jnp.stack([gw[k][l] for l in sorted(gw[k])], axis=0) for k in SMALL}
    packed_names = tuple(k for k in SMALL if k != "gm_w_s")
    small_shapes = [grads_small[k].shape for k in packed_names] + [(1, 1)]
    (red_ws, red_rest), received_last = all_reduce_small(
        [grads_small["gm_w_s"].reshape(-1, LANES), _pack_small([grads_small[k] for k in packed_names] + [loss])],
        contributions(gw, last), name="all_reduce_small_and_last_exchange")
    small_full = dict(zip(packed_names + ("loss",), _unpack_small(red_rest, small_shapes)))
    small_full["gm_w_s"] = red_ws.reshape(grads_small["gm_w_s"].shape)

    g_piece = {}
    for grp, received in ((main, received_main), (last, received_last)):
        for p, r in zip(grp, received):
            g_piece[p] = summed_to_shard(sum_slots(r, name="reduce_scatter_sum"), p[0])
    out_g, out_d, out_m, out_v = {}, {}, {}, {}
    for k in BIG:
        swap = lambda a: jnp.swapaxes(a, -1, -2)
        keep_t = k in COL_SHARDED and given[k].shape[-1] % LANES != 0
        flip = swap if keep_t else (lambda a: a)
        g = jnp.stack([g_piece[(k, l)] for l in range(given[k].shape[0])], axis=0)
        if k in COL_SHARDED and not keep_t:
            g = swap(g)
        res = adamw(flip(given[k]), g, flip(given["m_" + k]), flip(given["v_" + k]), name="adamw_" + k)
        out_g[k], out_d[k], out_m[k], out_v[k] = (flip(a) for a in (g,) + tuple(res))

    gsmall = {}
    for k in SMALL:
        g = small_full[k]
        if k in SMALL_SHARDED:
            n = given[k].shape[-1]
            g = lax.dynamic_slice_in_dim(g, me * n, n, axis=g.ndim - 1)
        gsmall[k] = g
    local_shapes = [given[k].shape for k in packed_names]
    dsm, nmsm, nvsm = adamw(*[_pack_small([src[k] for k in packed_names]) for src in (
        given, gsmall, {k: given["m_" + k] for k in packed_names}, {k: given["v_" + k] for k in packed_names})],
        name="adamw_small")
    out_g.update(gsmall)
    for dst, src in ((out_d, dsm), (out_m, nmsm), (out_v, nvsm)):
        dst.update(zip(packed_names, _unpack_small(src, local_shapes)))
    ws_shape = given["gm_w_s"].shape
    out_d["gm_w_s"], out_m["gm_w_s"], out_v["gm_w_s"] = (a.reshape(ws_shape) for a in adamw(
        *[a.reshape((-1,) + ws_shape[-2:]) for a in (given["gm_w_s"], gsmall["gm_w_s"], given["m_gm_w_s"], given["v_gm_w_s"])],
        name="adamw_gm_w_s"))

    return (small_full["loss"].reshape(()), gx[None],
            *[out_g[k] for k in WEIGHTS], *[out_d[k] for k in WEIGHTS],
            *[out_m[k] for k in WEIGHTS], *[out_v[k] for k in WEIGHTS])
```

```python
import math

import jax
import jax.numpy as jnp
from jax import lax
from jax.experimental import pallas as pl
from jax.experimental.pallas import tpu as pltpu

F32 = jnp.float32
BF16 = jnp.bfloat16
EPS = 1e-6
N_DEV = 8
SB_HEAD_DIM = 64
SB_TILE = 256
SB_FWD_QUERY_BLOCKS = 4
SB_BWD_QUERY_BLOCKS = 2
SSM_STATE = 128
SSD_FWD_GROUPS_PER_STEP = 2
SSD_BWD_GROUPS_PER_STEP = 1
SSM_CONV = 4
ADAM_LR = 0.001
ADAM_B1 = 0.9
ADAM_B2 = 0.999
ADAM_EPS = 1e-08
ADAM_WD = 0.01
ADAM_STEP = 10
VMEM_LIMIT_BYTES = 56 * 1024 * 1024
MATMUL_VMEM_BUDGET = 40 * 1024 * 1024
LANES = 128
PACK_COLS = 1024

BIG = ("sb_w_qkv", "sb_w_o", "gm_w_in", "gm_w_out", "ssm_w_in", "ssm_w_out", "ffn_w_gu", "ffn_w_down")
COL_SHARDED = ("sb_w_qkv", "gm_w_in", "ssm_w_in", "ffn_w_gu")
SMALL = ("mix_norm", "ffn_norm", "sb_q_gain", "sb_k_gain", "gm_b_in", "gm_v_gain", "gm_w_s", "gm_b_s",
         "ssm_conv_w", "ssm_conv_b", "ssm_dt_bias", "ssm_a_log", "ssm_d", "ssm_norm_gain")
SMALL_SHARDED = ("ssm_conv_w", "ssm_conv_b", "ssm_norm_gain")
WEIGHTS = ("mix_norm", "ffn_norm", "sb_w_qkv", "sb_q_gain", "sb_k_gain", "sb_w_o", "gm_w_in", "gm_b_in",
           "gm_v_gain", "gm_w_s", "gm_b_s", "gm_w_out", "ssm_w_in", "ssm_conv_w", "ssm_conv_b", "ssm_dt_bias",
           "ssm_a_log", "ssm_d", "ssm_norm_gain", "ssm_w_out", "ffn_w_gu", "ffn_w_down")


def _params(semantics=None):
    return pltpu.CompilerParams(dimension_semantics=semantics, vmem_limit_bytes=VMEM_LIMIT_BYTES)


def _pick(n, prefs):
    for t in prefs:
        if t <= n and n % t == 0:
            return t
    return n


def _dot(a, b, ca=1, cb=0):
    return lax.dot_general(a, b, (((ca,), (cb,)), ((), ())), preferred_element_type=F32)


def _split3(v):
    h1 = v.astype(BF16)
    r1 = v - h1.astype(F32)
    h2 = r1.astype(BF16)
    h3 = (r1 - h2.astype(F32)).astype(BF16)
    return h1, h2, h3


def _dot_exact_left(mat01, v):
    h1, h2, h3 = _split3(v)
    return _dot(mat01, h1) + _dot(mat01, h2) + _dot(mat01, h3)


def _sum_all(v):
    return jnp.sum(jnp.sum(v, axis=0, keepdims=True), axis=1, keepdims=True)


def _sigmoid(v):
    return 1.0 / (1.0 + jnp.exp(-v))


def _softplus(v):
    return jnp.maximum(v, 0.0) + jnp.log(1.0 + jnp.exp(-jnp.abs(v)))


def _erf(v):
    a = jnp.abs(v)
    t = 1.0 / (1.0 + 0.3275911 * a)
    poly = t * (0.254829592 + t * (-0.284496736 + t * (1.421413741 + t * (-1.453152027 + t * 1.061405429))))
    e = 1.0 - poly * jnp.exp(-a * a)
    return jnp.where(v < 0, -e, e)


def _gelu_and_grad(v):
    cdf = 0.5 * (1.0 + _erf(v * (1.0 / math.sqrt(2.0))))
    pdf = jnp.exp(-0.5 * v * v) * (1.0 / math.sqrt(2.0 * math.pi))
    return v * cdf, cdf + v * pdf


def matmul(a, b, *, ta=False, tb=False, out_dtype=F32, residual=None, a_split=1, b_split=1, norm_bwd=None,
           norm_fwd=None, name):
    if a_split > 1 and ta:
        assert a.shape[0] == a_split
        K, M = a.shape[1], a_split * a.shape[2]
    elif a_split > 1:
        assert a.shape[0] == a_split
        M, K = a.shape[1], a_split * a.shape[2]
    elif ta:
        K, M = a.shape
    else:
        M, K = a.shape
    if b_split > 1:
        assert not tb and b.shape[0] == b_split
        Kb, N = b.shape[1], b_split * b.shape[2]
    elif tb:
        N, Kb = b.shape
    else:
        Kb, N = b.shape
    assert K == Kb, (a.shape, b.shape, ta, tb)
    has_res = residual is not None
    tm = _pick(M // a_split if ta else M, (1024, 1408, 768, 512, 256, 128))
    tn = _pick(N // b_split, (1024, 1408, 1536, 768, 512, 256, 128))

    has_norm = norm_bwd is not None
    has_nf = norm_fwd is not None
    assert not (has_norm or has_nf) or (tn == N and out_dtype == F32)
    assert not (has_norm and has_nf)

    def vmem_bytes(tk):
        tiles = tm * tk * a.dtype.itemsize + tk * tn * b.dtype.itemsize
        outs = tm * tn * jnp.dtype(out_dtype).itemsize + (tm * tn * 4 if has_res else 0) + (2 * tm * tn * 4 if has_norm else 0)
        return 2 * tiles + 2 * outs + (tm * tn * 4 if tk < K else 0)

    kp = K if ta else K // a_split
    tk = next((t for t in (K, 2048, 1408, 1024, 512, 256) if t <= kp and kp % t == 0 and vmem_bytes(t) <= MATMUL_VMEM_BUDGET),
              _pick(kp, (128,)))
    nk = K // tk
    if a_split > 1 and ta:
        nib = M // a_split // tm
        a_spec = pl.BlockSpec((None, tk, tm), lambda i, j, k: (i // nib, k, i % nib))
    elif a_split > 1:
        nkb = kp // tk
        a_spec = pl.BlockSpec((None, tm, tk), lambda i, j, k: (k // nkb, i, k % nkb))
    else:
        a_spec = pl.BlockSpec((tk, tm), lambda i, j, k: (k, i)) if ta else pl.BlockSpec((tm, tk), lambda i, j, k: (i, k))
    if b_split > 1:
        njb = N // b_split // tn
        b_spec = pl.BlockSpec((None, tk, tn), lambda i, j, k: (j // njb, k, j % njb))
    else:
        b_spec = pl.BlockSpec((tn, tk), lambda i, j, k: (j, k)) if tb else pl.BlockSpec((tk, tn), lambda i, j, k: (k, j))
    o_spec = pl.BlockSpec((tm, tn), lambda i, j, k: (i, j))
    ca, cb = (0 if ta else 1), (1 if tb else 0)

    def body(*refs):
        a_ref, b_ref = refs[:2]
        n_in = 2 + has_res + 3 * has_norm + has_nf
        r_ref = refs[2] if has_res else None
        o_ref = refs[n_in]
        if has_norm:
            x_ref, g_ref, dres_ref = refs[n_in - 3:n_in]
            dg_ref = refs[n_in + 1]

            @pl.when(jnp.logical_and(pl.program_id(0) == 0, pl.program_id(2) == 0))
            def _():
                dg_ref[...] = jnp.zeros_like(dg_ref)

        def finish(r):
            if has_res:
                r = r + r_ref[...]
            if has_norm:
                xv = x_ref[...]
                rs = lax.rsqrt(jnp.mean(xv * xv, axis=1, keepdims=True) + EPS)
                xhat = xv * rs
                t = r * g_ref[...]
                dg_ref[...] += jnp.sum(r * xhat, axis=0, keepdims=True)
                r = dres_ref[...] + rs * (t - xhat * jnp.mean(xhat * t, axis=1, keepdims=True))
            if has_nf:
                rs = lax.rsqrt(jnp.mean(r * r, axis=1, keepdims=True) + EPS)
                refs[n_in + 1][...] = (r * rs * refs[n_in - 1][...]).astype(BF16)
            o_ref[...] = r.astype(out_dtype)

        def part():
            return _dot(a_ref[...].astype(BF16), b_ref[...].astype(BF16), ca, cb)

        if nk == 1:
            finish(part())
            return
        acc = refs[-1]
        k = pl.program_id(2)


        @pl.when(k == 0)
        def _():
            acc[...] = part()

        @pl.when(jnp.logical_and(k > 0, k < nk - 1))
        def _():
            acc[...] += part()

        @pl.when(k == nk - 1)
        def _():
            finish(acc[...] + part())

    in_specs = [a_spec, b_spec] + ([o_spec] if has_res else [])
    args = (a, b) + ((residual,) if has_res else ())
    out_specs, out_shape = o_spec, jax.ShapeDtypeStruct((M, N), out_dtype)
    if has_norm:
        vec = pl.BlockSpec((1, N), lambda i, j, k: (0, 0))
        in_specs += [o_spec, vec, o_spec]
        args += tuple(norm_bwd)
        out_specs, out_shape = [o_spec, vec], [out_shape, jax.ShapeDtypeStruct((1, N), F32)]
    if has_nf:
        in_specs += [pl.BlockSpec((1, N), lambda i, j, k: (0, 0))]
        args += (norm_fwd,)
        out_specs, out_shape = [o_spec, o_spec], [out_shape, jax.ShapeDtypeStruct((M, N), BF16)]
    return pl.pallas_call(
        body, name=name, grid=(M // tm, N // tn, nk), in_specs=in_specs, out_specs=out_specs, out_shape=out_shape,
        scratch_shapes=[pltpu.VMEM((tm, tn), F32)] if nk > 1 else [],
        compiler_params=_params(("arbitrary" if has_norm else "parallel", "parallel", "arbitrary")),
    )(*args)


def rms_fwd(x, gain, *, name):
    S, D = x.shape
    tr = _pick(S, (512, 256, 128))

    def body(x_ref, g_ref, o_ref):
        xv = x_ref[...]
        r = lax.rsqrt(jnp.mean(xv * xv, axis=1, keepdims=True) + EPS)
        o_ref[...] = (xv * r * g_ref[...]).astype(BF16)

    return pl.pallas_call(
        body, name=name, grid=(S // tr,),
        in_specs=[pl.BlockSpec((tr, D), lambda i: (i, 0)), pl.BlockSpec((1, D), lambda i: (0, 0))],
        out_specs=pl.BlockSpec((tr, D), lambda i: (i, 0)), out_shape=jax.ShapeDtypeStruct((S, D), BF16),
        compiler_params=_params(("parallel",)),
    )(x, gain)


def ffn_up_fwd(h, w_gu_t, *, name):
    S, K = h.shape
    F = w_gu_t.shape[0] // 2
    tm = _pick(S, (512, 256, 128))
    tn = _pick(F, (1408, 1024, 768, 512, 256, 128))
    nj = F // tn

    def body(h_ref, wg_ref, wu_ref, gu_ref, a_ref):
        hv = h_ref[...]
        g = _dot(hv, wg_ref[...], 1, 1)
        u = _dot(hv, wu_ref[...], 1, 1)
        gu_ref[0] = g
        gu_ref[1] = u
        a_ref[...] = (g * _sigmoid(g) * u).astype(BF16)

    return pl.pallas_call(
        body, name=name, grid=(nj, S // tm),
        in_specs=[pl.BlockSpec((tm, K), lambda j, i: (i, 0)), pl.BlockSpec((tn, K), lambda j, i: (j, 0)),
                  pl.BlockSpec((tn, K), lambda j, i: (nj + j, 0))],
        out_specs=[pl.BlockSpec((2, tm, tn), lambda j, i: (0, i, j)), pl.BlockSpec((tm, tn), lambda j, i: (i, j))],
        out_shape=[jax.ShapeDtypeStruct((2, S, F), F32), jax.ShapeDtypeStruct((S, F), BF16)],
        compiler_params=_params(("parallel", "parallel")),
    )(h, w_gu_t, w_gu_t)


def ffn_up_bwd(dy, w_down, gu, *, name):
    S, D = dy.shape
    F = w_down.shape[0]
    tm = _pick(S, (512, 256, 128))
    tn = _pick(F, (1408, 1024, 768, 512, 256, 128))

    def body(dy_ref, wd_ref, gu_ref, o_ref):
        da = _dot(dy_ref[...].astype(BF16), wd_ref[...], 1, 1)
        g = gu_ref[0]
        u = gu_ref[1]
        s = _sigmoid(g)
        o_ref[0] = (da * u * (s * (1.0 + g * (1.0 - s)))).astype(BF16)
        o_ref[1] = (da * g * s).astype(BF16)

    pair = pl.BlockSpec((2, tm, tn), lambda j, i: (0, i, j))
    return pl.pallas_call(
        body, name=name, grid=(F // tn, S // tm),
        in_specs=[pl.BlockSpec((tm, D), lambda j, i: (i, 0)), pl.BlockSpec((tn, D), lambda j, i: (j, 0)), pair],
        out_specs=pair, out_shape=jax.ShapeDtypeStruct((2, S, F), BF16),
        compiler_params=_params(("parallel", "parallel")),
    )(dy, w_down, gu)


def loss_head(y, target, *, name):
    S, D = y.shape
    tr = _pick(S, (512, 256, 128))

    def body(y_ref, t_ref, dy_ref, l_ref):
        @pl.when(pl.program_id(0) == 0)
        def _():
            l_ref[...] = jnp.zeros_like(l_ref)

        err = y_ref[...] - t_ref[...]
        dy_ref[...] = err * (1.0 / D)
        l_ref[...] += jnp.sum(0.5 * jnp.mean(err * err, axis=1, keepdims=True), axis=0, keepdims=True)

    row = pl.BlockSpec((tr, D), lambda i: (i, 0))
    one = pl.BlockSpec((1, 1), lambda i: (0, 0))
    dy, l = pl.pallas_call(
        body, name=name, grid=(S // tr,), in_specs=[row, row], out_specs=[row, one],
        out_shape=[jax.ShapeDtypeStruct((S, D), F32), jax.ShapeDtypeStruct((1, 1), F32)],
        compiler_params=_params(("arbitrary",)),
    )(y, target)
    return dy, l


def adamw(w, g, m, v, *, name):
    R, C = w.shape[-2:]
    tr = _pick(R, (512, 256, 128, 64, 32, 16, 8))

    def body(w_ref, g_ref, m_ref, v_ref, d_ref, mo_ref, vo_ref):
        gv = g_ref[...]
        mn = ADAM_B1 * m_ref[...] + (1.0 - ADAM_B1) * gv
        vn = ADAM_B2 * v_ref[...] + (1.0 - ADAM_B2) * jnp.square(gv)
        m_hat = mn / (1.0 - ADAM_B1 ** ADAM_STEP)
        v_hat = vn / (1.0 - ADAM_B2 ** ADAM_STEP)
        d_ref[...] = -ADAM_LR * (m_hat / (jnp.sqrt(v_hat) + ADAM_EPS) + ADAM_WD * w_ref[...])
        mo_ref[...] = mn
        vo_ref[...] = vn

    tc = C if tr < R or C % LANES else _pick(C, (256, 128))
    if w.ndim == 3:
        grid = (w.shape[0], R // tr, C // tc)
        blk = pl.BlockSpec((None, tr, tc), lambda l, i, j: (l, i, j))
    else:
        grid = (R // tr, C // tc)
        blk = pl.BlockSpec((tr, tc), lambda i, j: (i, j))
    sds = jax.ShapeDtypeStruct(w.shape, F32)
    return pl.pallas_call(
        body, name=name, grid=grid, in_specs=[blk] * 4, out_specs=[blk] * 3, out_shape=[sds] * 3,
        compiler_params=_params(("parallel",) * len(grid)),
    )(w, g, m, v)


def _tri(n, kind):
    r = lax.broadcasted_iota(jnp.int32, (n, n), 0)
    c = lax.broadcasted_iota(jnp.int32, (n, n), 1)
    if kind == "row_gt_col":
        return (r > c).astype(BF16)
    if kind == "row_ge_col":
        return (r >= c).astype(BF16)
    if kind == "row_le_col":
        return (r <= c).astype(BF16)
    raise ValueError(kind)


def _sb_tile(qs, kj, r_carry, u_strict, masked):
    z = _dot(qs, kj, 1, 1)
    lb = jnp.minimum(z, 0.0) - jnp.log(1.0 + jnp.exp(-jnp.abs(z)))
    l1m = lb - z
    keep = None
    if masked:
        tq, tk = z.shape
        keep = lax.broadcasted_iota(jnp.int32, (tq, tk), 1) < lax.broadcasted_iota(jnp.int32, (tq, tk), 0)
        l1m = jnp.where(keep, l1m, 0.0)
    w = jnp.exp(lb + _dot(l1m.astype(BF16), u_strict) + r_carry)
    if masked:
        w = jnp.where(keep, w, 0.0)
    return lb, l1m, w, keep


def _sb_prep(T, nb, hd, refs_in, gains, scratch):
    q_scale = 1.0 / math.sqrt(hd)
    assert math.log2(q_scale) == round(math.log2(q_scale))

    def prep(i, _):
        rows = pl.ds(pl.multiple_of(i * T, T), T)
        for hh in range(2):
            sl = slice(hd * hh, hd * hh + hd)
            for n, (src, dst) in enumerate(zip(refs_in, scratch)):
                v = src[rows, sl]
                if n < 2:
                    v = v * lax.rsqrt(jnp.mean(v * v, axis=1, keepdims=True) + EPS) * gains[n][...]
                if n == 0:
                    v = v * q_scale
                dst[hh, rows, :] = v.astype(BF16)
        return 0

    lax.fori_loop(0, nb, prep, 0)


def _sb_chains(m, T, nq):
    rows = [pl.ds(pl.multiple_of((nq * m + qb) * T, T), T) for qb in range(nq)]
    return rows, [(hh, qb) for qb in range(nq) for hh in range(2)]


def _sb_sweep(tile, carry, chains, m, nq):
    for kk in reversed(range(nq)):
        carry = tile(nq * m + kk, carry, [(ch, ch[1] == kk) for ch in chains if ch[1] >= kk])
    return lax.fori_loop(0, nq * m, lambda jj, c: tile(nq * m - 1 - jj, c, [(ch, False) for ch in chains]), carry)


def sb_attn_fwd(qkv, q_gain, k_gain, *, name, gather=None):
    S, D3 = qkv.shape
    D = D3 // 3
    npairs = D // LANES
    hd = SB_HEAD_DIM
    T = min(SB_TILE, S)
    nb = S // T
    nq = SB_FWD_QUERY_BLOCKS
    assert nb % nq == 0

    def body(*refs):
        if gather is None:
            q_ref, k_ref, v_ref, qg_ref, kg_ref, us_ref, o_ref, qn_s, kn_s, vb_s = refs
        else:
            ng = len(gather)
            q_ref, k_ref, v_ref, qg_ref, kg_ref, us_ref = refs[:6]
            o_ref = refs[6 + ng]
            qn_s, kn_s, vb_s = refs[7 + 2 * ng:10 + 2 * ng]
            comm = (refs[6:6 + ng], refs[7 + ng:7 + 2 * ng], refs[10 + 2 * ng:])
            step = pl.program_id(0)
            pl.when(step == 0)(lambda: _ag_phase(*comm, "start"))
            pl.when(step == npairs - 1)(lambda: _ag_phase(*comm, "forward"))
        us = us_ref[...]
        _sb_prep(T, nb, hd, (q_ref, k_ref, v_ref), (qg_ref, kg_ref), (qn_s, kn_s, vb_s))

        def superblock(m, _):
            rows_q, chains = _sb_chains(m, T, nq)
            qs = {ch: qn_s[ch[0], rows_q[ch[1]], :] for ch in chains}

            def tile(j, carry, which):
                rows_j = pl.ds(pl.multiple_of(j * T, T), T)
                new = dict(carry)
                for ch, masked in which:
                    acc, rc = carry[ch]
                    _, l1m, w, _ = _sb_tile(qs[ch], kn_s[ch[0], rows_j, :], rc, us, masked)
                    new[ch] = (acc + _dot(w.astype(BF16), vb_s[ch[0], rows_j, :]),
                               rc + jnp.sum(l1m, axis=1, keepdims=True))
                return new

            carry = {ch: (jnp.zeros((T, hd), F32), jnp.zeros((T, 1), F32)) for ch in chains}
            carry = _sb_sweep(tile, carry, chains, m, nq)
            for qb in range(nq):
                o_ref[rows_q[qb], :] = jnp.concatenate([carry[(0, qb)][0], carry[(1, qb)][0]], axis=1)
            return 0

        lax.fori_loop(0, nb // nq, superblock, 0)
        if gather is not None:
            pl.when(step == npairs - 1)(lambda: _ag_phase(*comm, "finish"))

    col = lambda off: pl.BlockSpec((S, LANES), lambda p, off=off: (0, off + p))
    gain = pl.BlockSpec((1, hd), lambda p: (0, 0))
    in_specs = [col(0), col(npairs), col(2 * npairs), gain, gain, pl.BlockSpec((T, T), lambda p: (0, 0))]
    out_specs = [pl.BlockSpec((S, LANES), lambda p: (0, p))]
    out_shape = [jax.ShapeDtypeStruct((S, D), F32)]
    scratch = [pltpu.VMEM((2, S, hd), BF16)] * 3
    args = [qkv, qkv, qkv, q_gain, k_gain, _tri(T, "row_gt_col")]
    if gather is not None:
        in_specs += [HBM_SPEC] * len(gather)
        out_specs += [HBM_SPEC] * len(gather)
        out_shape += [jax.ShapeDtypeStruct((N_DEV,) + s.shape, s.dtype) for s in gather]
        scratch += _copy_semaphores(len(gather))
        args += list(gather)
    out = pl.pallas_call(
        body, name=name, grid=(npairs,), in_specs=in_specs, out_specs=out_specs, out_shape=out_shape,
        scratch_shapes=scratch, compiler_params=_params(("arbitrary",)),
    )(*args)
    return out[0] if gather is None else (out[0], list(out[1:]))


def sb_attn_bwd(qkv, o, do, q_gain, k_gain, *, name, scatter=None):
    S, D3 = qkv.shape
    D = D3 // 3
    npairs = D // LANES
    hd = SB_HEAD_DIM
    T = min(SB_TILE, S)
    nb = S // T
    nq = SB_BWD_QUERY_BLOCKS
    assert nb % nq == 0
    scale = 1.0 / math.sqrt(hd)

    def body(*refs):
        if scatter is None:
            (q_ref, k_ref, v_ref, o_ref, do_ref, qg_ref, kg_ref, us_ref,
             dqkv_ref, dg_ref, qn_s, kn_s, vb_s, dob_s, acc_s) = refs
        else:
            ns = len(scatter)
            q_ref, k_ref, v_ref, o_ref, do_ref, qg_ref, kg_ref, us_ref = refs[:8]
            rs_in = refs[8:8 + ns]
            dqkv_ref, dg_ref = refs[8 + ns:10 + ns]
            rs_out = refs[10 + ns:10 + 2 * ns]
            qn_s, kn_s, vb_s, dob_s, acc_s = refs[10 + 2 * ns:15 + 2 * ns]
            rs_sems = refs[15 + 2 * ns:]
            pl.when(pl.program_id(0) == 0)(lambda: _rs_phase(rs_in, rs_out, rs_sems, "start"))
        dq_ref, dk_ref, dv_ref = acc_s.at[0], acc_s.at[1], acc_s.at[2]

        @pl.when(pl.program_id(0) == 0)
        def _():
            dg_ref[...] = jnp.zeros_like(dg_ref)

        us = us_ref[...]
        u_prefix = (1.0 - us.astype(F32)).astype(BF16)
        _sb_prep(T, nb, hd, (q_ref, k_ref, v_ref, do_ref), (qg_ref, kg_ref), (qn_s, kn_s, vb_s, dob_s))
        dk_ref[...] = jnp.zeros_like(dk_ref)
        dv_ref[...] = jnp.zeros_like(dv_ref)

        def superblock(m, _):
            rows_q, chains = _sb_chains(m, T, nq)
            qs = {ch: qn_s[ch[0], rows_q[ch[1]], :] for ch in chains}
            doi ={ch: dob_s[ch[0], rows_q[ch[1]], :] for ch in chains}
            dt_total = {ch: jnp.sum(doi[ch].astype(F32) * o_ref[rows_q[ch[1]], hd * ch[0]:hd * ch[0] + hd],
                                    axis=1, keepdims=True) for ch in chains}

            def tile(j, carry, which):
                rows_j = pl.ds(pl.multiple_of(j * T, T), T)
                new = dict(carry)
                dk_part, dv_part = {}, {}
                for ch, masked in which:
                    hh = ch[0]
                    dq_acc, rc, gc = carry[ch]
                    kj = kn_s[hh, rows_j, :]
                    lb, l1m, w, keep = _sb_tile(qs[ch], kj, rc, us, masked)
                    wb = w.astype(BF16)
                    g = _dot(doi[ch], vb_s[hh, rows_j, :], 1, 1) * wb.astype(F32)
                    g_row = jnp.sum(g, axis=1, keepdims=True)
                    g_upto = (dt_total[ch] - gc - g_row) + _dot(g.astype(BF16), u_prefix)
                    dz = g - g_upto * jnp.exp(lb)
                    if masked:
                        dz = jnp.where(keep, dz, 0.0)
                    dzb = dz.astype(BF16)
                    dv_part[hh] = dv_part.get(hh, 0.0) + _dot(wb, doi[ch], 0, 0)
                    dk_part[hh] = dk_part.get(hh, 0.0) + _dot(dzb, qs[ch], 0, 0)
                    new[ch] = (dq_acc + _dot(dzb, kj), rc + jnp.sum(l1m, axis=1, keepdims=True),
                               gc + g_row)
                dv_ref[rows_j, :] += jnp.concatenate([dv_part[0], dv_part[1]], axis=1)
                dk_ref[rows_j, :] += jnp.concatenate([dk_part[0], dk_part[1]], axis=1)
                return new

            zero1 = jnp.zeros((T, 1), F32)
            carry = {ch: (jnp.zeros((T, hd), F32), zero1, zero1) for ch in chains}
            carry = _sb_sweep(tile, carry, chains, m, nq)
            for qb in range(nq):
                dq_ref[rows_q[qb], :] = jnp.concatenate([carry[(0, qb)][0], carry[(1, qb)][0]], axis=1) * scale
            return 0

        lax.fori_loop(0, nb // nq, superblock, 0)

        def finish(i, carry):
            rows = pl.ds(pl.multiple_of(i * T, T), T)
            new = []
            for hh in range(2):
                sl = slice(hd * hh, hd * hh + hd)
                outs = []
                for raw_ref, gain_ref, dn in ((q_ref, qg_ref, dq_ref[rows, sl]), (k_ref, kg_ref, dk_ref[rows, sl])):
                    raw = raw_ref[rows, sl]
                    r = lax.rsqrt(jnp.mean(raw * raw, axis=1, keepdims=True) + EPS)
                    hat = raw * r
                    t = dn * gain_ref[...]
                    outs.append((r * (t - hat * jnp.mean(hat * t, axis=1, keepdims=True)),
                                 jnp.sum(dn * hat, axis=0, keepdims=True)))
                dqkv_ref[0, rows, sl] = outs[0][0].astype(BF16)
                dqkv_ref[1, rows, sl] = outs[1][0].astype(BF16)
                new.append((carry[hh][0] + outs[0][1], carry[hh][1] + outs[1][1]))
            dqkv_ref[2, rows, :] = dv_ref[rows, :].astype(BF16)
            return tuple(new)

        zg = (jnp.zeros((1, hd), F32), jnp.zeros((1, hd), F32))
        tot = lax.fori_loop(0, nb, finish, (zg, zg))
        dg_ref[0:1, 0:hd] += tot[0][0] + tot[1][0]
        dg_ref[1:2, 0:hd] += tot[0][1] + tot[1][1]
        if scatter is not None:
            pl.when(pl.program_id(0) == npairs - 1)(lambda: _rs_phase(rs_in, rs_out, rs_sems, "finish"))

    col = lambda off: pl.BlockSpec((S, LANES), lambda p, off=off: (0, off + p))
    gain = pl.BlockSpec((1, hd), lambda p: (0, 0))
    tri = pl.BlockSpec((T, T), lambda p: (0, 0))
    pair = pl.BlockSpec((S, LANES), lambda p: (0, p))
    in_specs = [col(0), col(npairs), col(2 * npairs), pair, pair, gain, gain, tri]
    out_specs = [pl.BlockSpec((3, S, LANES), lambda p: (0, 0, p)), pl.BlockSpec((8, LANES), lambda p: (0, 0))]
    out_shape = [jax.ShapeDtypeStruct((3, S, D), BF16), jax.ShapeDtypeStruct((8, LANES), F32)]
    scratch = [pltpu.VMEM((2, S, hd), BF16)] * 4 + [pltpu.VMEM((3, S, LANES), F32)]
    args = [qkv, qkv, qkv, o, do, q_gain, k_gain, _tri(T, "row_gt_col")]
    if scatter is not None:
        in_specs += [HBM_SPEC] * len(scatter)
        out_specs += [HBM_SPEC] * len(scatter)
        out_shape += [jax.ShapeDtypeStruct(g.shape, g.dtype) for g in scatter]
        scratch += _copy_semaphores(len(scatter))
        args += list(scatter)
    out = pl.pallas_call(
        body, name=name, grid=(npairs,), in_specs=in_specs, out_specs=out_specs, out_shape=out_shape,
        scratch_shapes=scratch, compiler_params=_params(("arbitrary",)),
    )(*args)
    res = (out[0], out[1][0:1, :hd], out[1][1:2, :hd])
    return res if scatter is None else res + (list(out[2:]),)


def gmlp_fwd(zzpre, b_in, v_gain, wc, bsf, *, name):
    S, H2 = zzpre.shape
    H = H2 // 2
    G, T, _ = wc.shape
    gd = H // G

    def body(z_ref, b_ref, vg_ref, wc_ref, bs_ref, p_ref):
        zz, _ = _gelu_and_grad(z_ref[...] + b_ref[...])
        u = zz[:, :H]
        v = zz[:, H:]
        vn = v * lax.rsqrt(jnp.mean(v * v, axis=1, keepdims=True) + EPS) * vg_ref[...]
        for g in range(G):
            gs = slice(g * gd, (g + 1) * gd)
            mixed = _dot(wc_ref[g], vn[:, gs].astype(BF16)) + bs_ref[g]
            p_ref[:, gs] = (u[:, gs] * mixed).astype(BF16)

    full3 = lambda shp: pl.BlockSpec(shp, lambda c: (0, 0, 0))
    return pl.pallas_call(
        body, name=name, grid=(S // T,),
        in_specs=[pl.BlockSpec((T, H2), lambda c: (c, 0)), pl.BlockSpec((1, H2), lambda c: (0, 0)),
                  pl.BlockSpec((1, H), lambda c: (0, 0)), full3((G, T, T)), full3((G, T, gd))],
        out_specs=pl.BlockSpec((T, H), lambda c: (c, 0)), out_shape=jax.ShapeDtypeStruct((S, H), BF16),
        compiler_params=_params(("parallel",)),
    )(zzpre, b_in, v_gain, wc, bsf)


def gmlp_bwd(zzpre, b_in, v_gain, wc, bsf, dp, *, name):
    S, H2 = zzpre.shape
    H = H2 // 2
    G, T, _ = wc.shape
    gd = H // G
    assert G <= LANES

    def body(z_ref, b_ref, vg_ref, wc_ref, bs_ref, dp_ref, dzz_ref, db_ref, dvg_ref, dws_ref, dbs_ref):
        @pl.when(pl.program_id(0) == 0)
        def _():
            db_ref[...] = jnp.zeros_like(db_ref)
            dvg_ref[...] = jnp.zeros_like(dvg_ref)
            dws_ref[...] = jnp.zeros_like(dws_ref)
            dbs_ref[...] = jnp.zeros_like(dbs_ref)

        zz, gp = _gelu_and_grad(z_ref[...] + b_ref[...])
        u = zz[:, :H]
        v = zz[:, H:]
        r = lax.rsqrt(jnp.mean(v * v, axis=1, keepdims=True) + EPS)
        vhat = v * r
        vg = vg_ref[...]
        vn = vhat * vg
        dpv = dp_ref[...]
        tril = lax.broadcasted_iota(jnp.int32, (T, T), 1) <= lax.broadcasted_iota(jnp.int32, (T, T), 0)
        lane = lax.broadcasted_iota(jnp.int32, (T, LANES), 1)
        dbs = jnp.zeros((T, LANES), F32)
        du_parts, dvn_parts = [], []
        for g in range(G):
            gs = slice(g * gd, (g + 1) * gd)
            vng = vn[:, gs].astype(BF16)
            wcg = wc_ref[g]
            mixed = _dot(wcg, vng) + bs_ref[g]
            dpg = dpv[:, gs]
            du_parts.append(dpg * mixed)
            dmx = dpg * u[:, gs]
            dmxb = dmx.astype(BF16)
            dvn_parts.append(_dot(wcg, dmxb, 0, 0))
            dws_ref[g] += jnp.where(tril, _dot(dmxb, vng, 1, 1), 0.0)
            dbs = dbs + jnp.where(lane == g, jnp.sum(dmx, axis=1, keepdims=True), 0.0)
        dbs_ref[...] += dbs
        du = jnp.concatenate(du_parts, axis=1)
        dvn = jnp.concatenate(dvn_parts, axis=1)
        dvg_ref[...] += jnp.sum(dvn * vhat, axis=0, keepdims=True)
        t = dvn * vg
        dv = r * (t - vhat * jnp.mean(vhat * t, axis=1, keepdims=True))
        dzu = du * gp[:, :H]
        dzv = dv * gp[:, H:]
        dzz_ref[:, :H] = dzu.astype(BF16)
        dzz_ref[:, H:] = dzv.astype(BF16)
        db_ref[:, :H] += jnp.sum(dzu, axis=0, keepdims=True)
        db_ref[:, H:] += jnp.sum(dzv, axis=0, keepdims=True)

    full3 = lambda shp: pl.BlockSpec(shp, lambda c: (0, 0, 0))
    vec = lambda n: pl.BlockSpec((1, n), lambda c: (0, 0))
    return pl.pallas_call(
        body, name=name, grid=(S // T,),
        in_specs=[pl.BlockSpec((T, H2), lambda c: (c, 0)), vec(H2), vec(H), full3((G, T, T)), full3((G, T, gd)),
                  pl.BlockSpec((T, H), lambda c: (c, 0))],
        out_specs=[pl.BlockSpec((T, H2), lambda c: (c, 0)), vec(H2), vec(H), full3((G, T, T)),
                   pl.BlockSpec((T, LANES), lambda c: (0, 0))],
        out_shape=[jax.ShapeDtypeStruct((S, H2), BF16), jax.ShapeDtypeStruct((1, H2), F32),
                   jax.ShapeDtypeStruct((1, H), F32), jax.ShapeDtypeStruct((G, T, T), F32),
                   jax.ShapeDtypeStruct((T, LANES), F32)],
        compiler_params=_params(("arbitrary",)),
    )(zzpre, b_in, v_gain, wc, bsf, dp)


def _shift_rows(v, k, n_rows):
    if k == 0:
        return v
    rolled = pltpu.roll(v, k % n_rows, 0)
    row = lax.broadcasted_iota(jnp.int32, v.shape, 0)
    keep = (row >= k) if k > 0 else (row < n_rows + k)
    return jnp.where(keep, rolled, 0.0)


def conv_fwd(zx, conv_w, conv_b, col0, *, name):
    S = zx.shape[0]
    C = conv_w.shape[1]
    tc = _pick(C, (256, 128))
    off = col0 // tc
    assert col0 % tc == 0

    def body(x_ref, w_ref, b_ref, o_ref):
        xv = x_ref[...]
        acc = b_ref[...] + w_ref[SSM_CONV - 1:SSM_CONV, :] * xv
        for k in range(SSM_CONV - 1):
            acc = acc + w_ref[k:k + 1, :] * _shift_rows(xv, SSM_CONV - 1 - k, S)
        o_ref[...] = acc * _sigmoid(acc)

    return pl.pallas_call(
        body, name=name, grid=(C // tc,),
        in_specs=[pl.BlockSpec((S, tc), lambda j: (0, off + j)), pl.BlockSpec((SSM_CONV, tc), lambda j: (0, j)),
                  pl.BlockSpec((1, tc), lambda j: (0, j))],
        out_specs=pl.BlockSpec((S, tc), lambda j: (0, j)), out_shape=jax.ShapeDtypeStruct((S, C), F32),
        compiler_params=_params(("parallel",)),
    )(zx, conv_w, conv_b)


def conv_bwd(zx, conv_w, conv_b, col0, douts, *, name):
    S = zx.shape[0]
    C = conv_w.shape[1]
    tc = LANES
    off = col0 // tc
    counts = [d.shape[1] // tc for d in douts]
    starts = [sum(counts[:p]) for p in range(len(douts))]
    assert sum(counts) * tc == C and all(d.shape[1] % tc == 0 for d in douts)

    def body(x_ref, w_ref, b_ref, *rest):
        do_refs, (dx_ref, dw_ref, db_ref) = rest[:len(douts)], rest[len(douts):]
        j = pl.program_id(0)
        dov = do_refs[-1][...]
        for p in reversed(range(len(douts) - 1)):
            dov = jnp.where(j < starts[p + 1], do_refs[p][...], dov)
        xv = x_ref[...]
        shifted = [_shift_rows(xv, SSM_CONV - 1 - k, S) for k in range(SSM_CONV)]
        acc = b_ref[...]
        for k in range(SSM_CONV):
            acc = acc + w_ref[k:k + 1, :] * shifted[k]
        s = _sigmoid(acc)
        dacc = dov * (s * (1.0 + acc * (1.0 - s)))
        db_ref[...] = jnp.sum(dacc, axis=0, keepdims=True)
        dx = jnp.zeros_like(xv)
        for k in range(SSM_CONV):
            dw_ref[k:k + 1, :] = jnp.sum(dacc * shifted[k], axis=0, keepdims=True)
            dx = dx + w_ref[k:k + 1, :] * _shift_rows(dacc, -(SSM_CONV - 1 - k), S)
        dx_ref[...] = dx.astype(BF16)

    slab = pl.BlockSpec((S, tc), lambda j: (0, j))
    piece_specs = [pl.BlockSpec((S, tc), lambda j, a=starts[p], n=counts[p]: (0, jnp.clip(j - a, 0, n - 1)))
                   for p in range(len(douts))]
    return pl.pallas_call(
        body, name=name, grid=(C // tc,),
        in_specs=[pl.BlockSpec((S, tc), lambda j: (0, off + j)), pl.BlockSpec((SSM_CONV, tc), lambda j: (0, j)),
                  pl.BlockSpec((1, tc), lambda j: (0, j))] + piece_specs,
        out_specs=[slab, pl.BlockSpec((SSM_CONV, tc), lambda j: (0, j)), pl.BlockSpec((1, tc), lambda j: (0, j))],
        out_shape=[jax.ShapeDtypeStruct((S, C), BF16), jax.ShapeDtypeStruct((SSM_CONV, C), F32),
                   jax.ShapeDtypeStruct((1, C), F32)],
        compiler_params=_params(("arbitrary",)),
    )(zx, conv_w, conv_b, *douts)


def _ssd_chunk_terms(dtraw, bias, a_log, tl):
    dt = _softplus(dtraw + bias)
    a_neg = -jnp.exp(a_log)
    ac = _dot_exact_left(tl, dt * a_neg)
    ac_last = ac[ac.shape[0] - 1:, :]
    return dt, a_neg, ac, ac.T, jnp.exp(ac), jnp.exp(ac_last - ac), jnp.exp(ac_last)


def _ssd_specs(S, L, G, hpg, pd, inner, gp):
    gw = hpg * pd
    n = SSM_STATE
    xb = inner // n
    assert G % gp == 0 and xb % gp == 0 and (xb + G) % gp == 0

    def mk(cidx):
        return dict(
            x=pl.BlockSpec((L, gp * gw), lambda g, c: (cidx(c), g)),
            b=pl.BlockSpec((L, gp * n), lambda g, c: (cidx(c), xb // gp + g)),
            c=pl.BlockSpec((L, gp * n), lambda g, c: (cidx(c), (xb + G) // gp + g)),
            z=pl.BlockSpec((L, gp * gw), lambda g, c: (cidx(c), g)),
            dt=pl.BlockSpec((L, gp * LANES), lambda g, c: (cidx(c), g)),
            gvec=pl.BlockSpec((gp, 1, LANES), lambda g, c: (g, 0, 0)),
            chan=pl.BlockSpec((1, gp * gw), lambda g, c: (0, g)),
            tri=pl.BlockSpec((L, L), lambda g, c: (0, 0)),
            hp=pl.BlockSpec((gp, 1, gw, n), lambda g, c: (g, cidx(c), 0, 0)),
            bc=pl.BlockSpec((L, gp * n), lambda g, c: (cidx(c), g)),
        )
    return mk


def _ssd_group_views(refs, kinds, gg, gw):
    n = SSM_STATE
    width = dict(x=gw, z=gw, chan=gw, b=n, c=n, bc=n, dt=LANES)
    out = []
    for ref, kind in zip(refs, kinds):
        if kind in width:
            out.append(ref.at[:, gg * width[kind]:(gg + 1) * width[kind]])
        elif kind in ("gvec", "hp"):
            out.append(ref.at[gg:gg + 1])
        elif kind == "state":
            out.append(ref.at[gg * gw:(gg + 1) * gw])
        else:
            out.append(ref)
    return out


def ssd_fwd(xbc, zx, dtg, bias_g, alog_g, d_chan, ngain, L, G, *, name):
    S = xbc.shape[0]
    n = SSM_STATE
    inner = xbc.shape[1] - 2 * G * n
    gw = inner // G
    pd = SB_HEAD_DIM
    hpg = gw // pd
    nc = S // L
    gp = SSD_FWD_GROUPS_PER_STEP
    sp = _ssd_specs(S, L, G, hpg, pd, inner, gp)(lambda c: c)

    kinds = ("x", "b", "c", "z", "dt", "gvec", "gvec", "chan", "chan", "tri", "x", "x", "hp", "state")

    def body(*refs):
        @pl.when(pl.program_id(1) == 0)
        def _():
            refs[-1][...] = jnp.zeros_like(refs[-1])

        for gg in range(gp):
            group_body(*_ssd_group_views(refs, kinds, gg, gw))

    def group_body(x_ref, b_ref, c_ref, z_ref, dt_ref, bias_ref, alog_ref, d_ref, ng_ref, tl_ref,
                   yn_ref, y_ref, hp_ref, state):
        dt, _, ac, act, ea, dte, cd = _ssd_chunk_terms(dt_ref[...], bias_ref[0], alog_ref[0], tl_ref[...])
        xv = x_ref[...]
        bm = b_ref[...].astype(BF16)
        cm = c_ref[...].astype(BF16)
        cb = _dot(cm, bm, 1, 1)
        tril = lax.broadcasted_iota(jnp.int32, (L, L), 1) <= lax.broadcasted_iota(jnp.int32, (L, L), 0)
        hp_ref[0, 0] = state[...]
        for r in range(hpg):
            ps = slice(r * pd, (r + 1) * pd)
            xr = xv[:, ps]
            xdt = xr * dt[:, r:r + 1]
            lm = jnp.exp(jnp.where(tril, ac[:, r:r + 1] - act[r:r + 1, :], -jnp.inf))
            hprev = state[ps, :]
            y = _dot((cb * lm).astype(BF16), xdt.astype(BF16))
            y = y + _dot(cm, hprev.astype(BF16), 1, 1) * ea[:, r:r + 1]
            y_ref[:, ps] = y + xr * d_ref[:, ps]
            st = _dot((xdt * dte[:, r:r + 1]).astype(BF16), bm, 0, 0)
            state[ps, :] = hprev * cd[:, r:r + 1] + st
        yfull = y_ref[...]
        zg = z_ref[...]
        yg = yfull * (zg * _sigmoid(zg))
        yn_ref[...] = (yg * lax.rsqrt(jnp.mean(yg * yg, axis=1, keepdims=True) + EPS) * ng_ref[...]).astype(BF16)

    return pl.pallas_call(
        body, name=name, grid=(G // gp, nc),
        in_specs=[sp["x"], sp["b"], sp["c"], sp["z"], sp["dt"], sp["gvec"], sp["gvec"], sp["chan"], sp["chan"], sp["tri"]],
        out_specs=[sp["x"], sp["x"], sp["hp"]],
        out_shape=[jax.ShapeDtypeStruct((S, inner), BF16), jax.ShapeDtypeStruct((S, inner), F32),
                   jax.ShapeDtypeStruct((G, nc, gw, n), F32)],
        scratch_shapes=[pltpu.VMEM((gp * gw, n), F32)],
        compiler_params=_params(("arbitrary", "arbitrary")),
    )(xbc, xbc, xbc, zx, dtg, bias_g, alog_g, d_chan, ngain, _tri(L, "row_ge_col"))


def ssd_bwd(xbc, zx, dtg, bias_g, alog_g, d_chan, ngain, yfull, hp, dyn, L, G, *, name):
    S = xbc.shape[0]
    n = SSM_STATE
    inner = xbc.shape[1] - 2 * G * n
    gw = inner // G
    pd = SB_HEAD_DIM
    hpg = gw // pd
    nc = S // L
    gp = SSD_BWD_GROUPS_PER_STEP
    sp = _ssd_specs(S, L, G, hpg, pd, inner, gp)(lambda c: nc - 1 - c)

    kinds = ("x", "b", "c", "z", "dt", "gvec", "gvec", "chan", "chan", "tri", "tri", "x", "hp", "x",
             "x", "x", "bc", "bc", "dt", "gvec", "gvec", "gvec", "chan", "state")

    def body(*refs):
        @pl.when(pl.program_id(1) == 0)
        def _():
            for acc in refs[-5:]:
                acc[...] = jnp.zeros_like(acc)

        for gg in range(gp):
            group_body(*_ssd_group_views(refs, kinds, gg, gw))

    def group_body(x_ref, b_ref, c_ref, z_ref, dt_ref, bias_ref, alog_ref, d_ref, ng_ref, tl_ref, tu_ref,
                   yf_ref, hp_ref, dyn_ref,
                   dz_ref, dx_ref, db_ref, dc_ref, ddt_ref, dbias_ref, dalog_ref, dd_ref, dng_ref, dstate):

        dtraw = dt_ref[...]
        dt, a_neg, ac, act, ea, dte, cd = _ssd_chunk_terms(dtraw, bias_ref[0], alog_ref[0], tl_ref[...])
        xv = x_ref[...]
        bm = b_ref[...].astype(BF16)
        cm = c_ref[...].astype(BF16)
        cb = _dot(cm, bm, 1, 1)
        tril = lax.broadcasted_iota(jnp.int32, (L, L), 1) <= lax.broadcasted_iota(jnp.int32, (L, L), 0)
        lane = lax.broadcasted_iota(jnp.int32, (L, LANES), 1)
        lane1 = lax.broadcasted_iota(jnp.int32, (1, LANES), 1)

        yfull = yf_ref[...]
        zg = z_ref[...]
        sg = _sigmoid(zg)
        gate = zg * sg
        yg = yfull * gate
        rr = lax.rsqrt(jnp.mean(yg * yg, axis=1, keepdims=True) + EPS)
        yhat = yg * rr
        dynv = dyn_ref[...]
        dng_ref[...] += jnp.sum(dynv * yhat, axis=0, keepdims=True)
        t = dynv * ng_ref[...]
        dyg = rr * (t - yhat * jnp.mean(yhat * t, axis=1, keepdims=True))
        dy = dyg * gate
        dz_ref[...] = (dyg * yfull * (sg * (1.0 + zg * (1.0 - sg)))).astype(BF16)

        dcb = jnp.zeros((L, L), F32)
        dc_acc = jnp.zeros((L, n), F32)
        db_acc = jnp.zeros((L, n), F32)
        dac = jnp.zeros((L, LANES), F32)
        xdx = jnp.zeros((L, LANES), F32)
        tail = jnp.zeros((1, LANES), F32)
        dskip = jnp.zeros((1, LANES), F32)
        ones_l = jnp.ones((L, LANES), BF16)
        for r in range(hpg):
            ps = slice(r * pd, (r + 1) * pd)
            xr = xv[:, ps]
            dyr = dy[:, ps]
            dtr = dt[:, r:r + 1]
            dter = dte[:, r:r + 1]
            cdr = cd[:, r:r + 1]
            xdt = xr * dtr
            xdtb = xdt.astype(BF16)
            dyrb = dyr.astype(BF16)
            lm = jnp.exp(jnp.where(tril, ac[:, r:r + 1] - act[r:r + 1, :], -jnp.inf))
            m32 = cb * lm
            mb = m32.astype(BF16)
            hprev = hp_ref[0, 0, ps, :]
            hpb = hprev.astype(BF16)
            dhn = dstate[ps, :]
            dhnb = dhn.astype(BF16)
            ear = ea[:, r:r + 1]
            gy = (dyr * ear).astype(BF16)
            dc_acc = dc_acc + _dot(gy, hpb)
            dstate[ps, :] = _dot(gy, cm, 0, 0) + dhn * cdr
            bdh = _dot(bm, dhnb, 1, 1)
            db_acc = db_acc + _dot((xdt * dter).astype(BF16), dhnb)
            dm = _dot(dyrb, xdtb, 1, 1)
            dxdt = bdh * dter + _dot(mb, dyrb, 0, 0)
            dcb = dcb + dm * lm
            wmat = dm * m32
            whi = wmat.astype(BF16)
            wlo = (wmat - whi.astype(F32)).astype(BF16)
            col_w = _dot(whi, ones_l, 0, 0) + _dot(wlo, ones_l, 0, 0)
            t_end = xdt * bdh * dter
            e_r = jnp.sum(wmat, axis=1, keepdims=True) \
                + jnp.sum(dyr * _dot(cm, hpb, 1, 1) * ear - t_end, axis=1, keepdims=True)
            c_r = cdr * _sum_all(dhn * hprev) + _sum_all(t_end)
            dac = dac + jnp.where(lane == r, e_r - col_w, 0.0)
            xdx = xdx + jnp.where(lane == r, jnp.sum(dxdt * xr, axis=1, keepdims=True), 0.0)
            tail = tail + jnp.where(lane1 == r, c_r, 0.0)
            dskip = dskip + jnp.where(lane1 == r, _sum_all(dyr * xr), 0.0)
            dx_ref[:, ps] = dxdt * dtr + dyr * d_ref[:, ps]
        dcbb = dcb.astype(BF16)
        dc_ref[...] = dc_acc + _dot(dcbb, bm)
        db_ref[...] = db_acc + _dot(dcbb, cm, 0, 0)
        da = _dot_exact_left(tu_ref[...], dac) + tail
        real = lane < hpg
        ddt = jnp.where(real, (da * a_neg + xdx) * _sigmoid(dtraw + bias_ref[0]), 0.0)
        ddt_ref[...] = ddt
        dd_ref[0] += dskip
        dbias_ref[0] += jnp.sum(ddt, axis=0, keepdims=True)
        dalog_ref[0] += jnp.where(lane1 < hpg, jnp.sum(da * dt, axis=0, keepdims=True) * a_neg, 0.0)

    return pl.pallas_call(
        body, name=name, grid=(G // gp, nc),
        in_specs=[sp["x"], sp["b"], sp["c"], sp["z"], sp["dt"], sp["gvec"], sp["gvec"], sp["chan"], sp["chan"],
                  sp["tri"], sp["tri"], sp["x"], sp["hp"], sp["x"]],
        out_specs=[sp["x"], sp["x"], sp["bc"], sp["bc"], sp["dt"], sp["gvec"], sp["gvec"], sp["gvec"], sp["chan"]],
        out_shape=[jax.ShapeDtypeStruct((S, inner), BF16), jax.ShapeDtypeStruct((S, inner), F32),
                   jax.ShapeDtypeStruct((S, G * n), F32), jax.ShapeDtypeStruct((S, G * n), F32),
                   jax.ShapeDtypeStruct((S, G * LANES), F32), jax.ShapeDtypeStruct((G, 1, LANES), F32),
                   jax.ShapeDtypeStruct((G, 1, LANES), F32), jax.ShapeDtypeStruct((G, 1, LANES), F32),
                   jax.ShapeDtypeStruct((1, inner), F32)],
        scratch_shapes=[pltpu.VMEM((gp * gw, n), F32)],
        compiler_params=_params(("arbitrary", "arbitrary")),
    )(xbc, xbc, xbc, zx, dtg, bias_g, alog_g, d_chan, ngain, _tri(L, "row_ge_col"), _tri(L, "row_le_col"),
      yfull, hp, dyn)


def _spread_dt(w_dt_t, G, hpg):
    K = w_dt_t.shape[1]
    w = w_dt_t.reshape(G, hpg, K)
    return jnp.pad(w, ((0, 0), (0, LANES - hpg), (0, 0))).reshape(G * LANES, K)


def _group_vec(v, G, hpg):
    return jnp.pad(v.reshape(G, 1, hpg), ((0, 0), (0, 0), (0, LANES - hpg)))


def local_step(x, target, W, late=None):
    S, D = x.shape
    depth = W["mix_norm"].shape[0]
    gm_groups, gm_chunk = W["gm_w_s"].shape[1], W["gm_w_s"].shape[2]
    heads = W["ssm_dt_bias"].shape[1]
    inner = heads * SB_HEAD_DIM
    L = gm_chunk
    received = None

    saved = []
    for i in range(depth):
        kind, j = i % 3, i // 3
        s = dict(x=x)
        if i == 0:
            h = rms_fwd(x, W["mix_norm"][i:i + 1], name="rms_mix_fwd")
        s["h"] = h
        ffn_gain = W["ffn_norm"][i:i + 1]
        if kind == 0:
            qkv = matmul(h, W["sb_w_qkv"][j], tb=True, name="mm_qkv")
            if late is not None and i == 0:
                o, gathered = sb_attn_fwd(qkv, W["sb_q_gain"][j:j + 1], W["sb_k_gain"][j:j + 1], name="sb_fwd_gather",
                                          gather=late.shards)
                late.fill(W, gathered)
            else:
                o = sb_attn_fwd(qkv, W["sb_q_gain"][j:j + 1], W["sb_k_gain"][j:j + 1], name="sb_fwd")
            x1, h2 = matmul(o, W["sb_w_o"][j], residual=x, norm_fwd=ffn_gain, name="mm_sb_out")
            s.update(qkv=qkv, o=o)
        elif kind == 1:
            wc = jnp.where(jnp.tril(jnp.ones((gm_chunk, gm_chunk), bool)), W["gm_w_s"][j], 0.0).astype(BF16)
            bsf = jnp.broadcast_to(W["gm_b_s"][j][:, :, None], (gm_groups, gm_chunk, W["gm_v_gain"].shape[1] // gm_groups)).astype(F32)
            zzpre = matmul(h, W["gm_w_in"][j], tb=True, name="mm_gm_in")
            p = gmlp_fwd(zzpre, W["gm_b_in"][j:j + 1], W["gm_v_gain"][j:j + 1], wc, bsf, name="gm_fwd")
            x1, h2 = matmul(p, W["gm_w_out"][j], residual=x, norm_fwd=ffn_gain, name="mm_gm_out")
            s.update(zzpre=zzpre, p=p, wc=wc, bsf=bsf)
        else:
            conv_dim = W["ssm_conv_w"].shape[2]
            G = (conv_dim - inner) // (2 * SSM_STATE)
            hpg = heads // G
            w_in = W["ssm_w_in"][j]
            w_zx = w_in[:inner + conv_dim]
            w_dtg = _spread_dt(w_in[inner + conv_dim:], G, hpg)
            bias_g = _group_vec(W["ssm_dt_bias"][j], G, hpg)
            alog_g = _group_vec(W["ssm_a_log"][j], G, hpg)
            d_chan = jnp.repeat(W["ssm_d"][j], SB_HEAD_DIM)[None, :]
            ngain = W["ssm_norm_gain"][j:j + 1]
            zx = matmul(h, w_zx, tb=True, name="mm_ssm_zx")
            dtg = matmul(h, w_dtg, tb=True, name="mm_ssm_dt")
            xbc = conv_fwd(zx, W["ssm_conv_w"][j], W["ssm_conv_b"][j:j + 1], inner, name="conv_fwd")
            yn, yfull, hp = ssd_fwd(xbc, zx, dtg, bias_g, alog_g, d_chan, ngain, L, G, name="ssd_fwd")
            x1, h2 = matmul(yn, W["ssm_w_out"][j], residual=x, norm_fwd=ffn_gain, name="mm_ssm_out")
            s.update(w_zx=w_zx, w_dtg=w_dtg, bias_g=bias_g, alog_g=alog_g, d_chan=d_chan, ngain=ngain,
                     zx=zx, dtg=dtg, xbc=xbc, yn=yn, yfull=yfull, hp=hp)
        gu, a = ffn_up_fwd(h2, W["ffn_w_gu"][i], name="ffn_up_fwd")
        if i + 1 < depth:
            x2, h = matmul(a, W["ffn_w_down"][i], residual=x1, norm_fwd=W["mix_norm"][i + 1:i + 2], name="mm_ffn_down")
        else:
            x2 = matmul(a, W["ffn_w_down"][i], residual=x1, name="mm_ffn_down_last")
        s.update(x1=x1, h2=h2, gu=gu, a=a)
        saved.append(s)
        x = x2

    dx, loss = loss_head(x, target, name="loss_head")

    gw = {k: {} for k in WEIGHTS}
    for i in reversed(range(depth)):
        kind, j = i % 3, i // 3
        s = saved[i]
        gw["ffn_w_down"][i] = matmul(s["a"], dx, ta=True, out_dtype=BF16, name="mm_ffn_dwdown")
        dgu = ffn_up_bwd(dx, W["ffn_w_down"][i], s["gu"], name="ffn_up_bwd")
        dx1, dgn = matmul(dgu, W["ffn_w_gu"][i], a_split=2, norm_bwd=(s["x1"], W["ffn_norm"][i:i + 1], dx),
                          name="mm_ffn_dh")
        gw["ffn_w_gu"][i] = matmul(dgu, s["h2"], ta=True, a_split=2, out_dtype=BF16, name="mm_ffn_dwgu")
        gw["ffn_norm"][i] = dgn[0]
        mix_norm_bwd = (s["x"], W["mix_norm"][i:i + 1], dx1)
        if kind == 0:
            do = matmul(dx1, W["sb_w_o"][j], tb=True, out_dtype=BF16, name="mm_sb_do")
            gw["sb_w_o"][j] = matmul(s["o"], dx1, ta=True, out_dtype=BF16, name="mm_sb_dwo")
            if late is not None and i == 0:
                dqkv, dqg, dkg, received = sb_attn_bwd(
                    s["qkv"], s["o"], do, W["sb_q_gain"][j:j + 1], W["sb_k_gain"][j:j + 1], name="sb_bwd_scatter",
                    scatter=late.contributions(gw))
            else:
                dqkv, dqg, dkg = sb_attn_bwd(s["qkv"], s["o"], do, W["sb_q_gain"][j:j + 1], W["sb_k_gain"][j:j + 1],
                                             name="sb_bwd")
            gw["sb_q_gain"][j] = dqg[0]
            gw["sb_k_gain"][j] = dkg[0]
            dx, dgn = matmul(dqkv, W["sb_w_qkv"][j], a_split=3, norm_bwd=mix_norm_bwd, name="mm_sb_dh")
            gw["sb_w_qkv"][j] = matmul(dqkv, s["h"], ta=True, a_split=3, out_dtype=BF16, name="mm_sb_dwqkv")
        elif kind == 1:
            dp = matmul(dx1, W["gm_w_out"][j], tb=True, name="mm_gm_dp")
            gw["gm_w_out"][j] = matmul(s["p"], dx1, ta=True, out_dtype=BF16, name="mm_gm_dwout")
            dzz, db_in, dvg, dws, dbs = gmlp_bwd(s["zzpre"], W["gm_b_in"][j:j + 1], W["gm_v_gain"][j:j + 1],
                                                s["wc"], s["bsf"], dp, name="gm_bwd")
            gw["gm_b_in"][j] = db_in[0]
            gw["gm_v_gain"][j] = dvg[0]
            gw["gm_w_s"][j] = dws
            gw["gm_b_s"][j] = dbs[:, :gm_groups].T
            dx, dgn = matmul(dzz, W["gm_w_in"][j], norm_bwd=mix_norm_bwd, name="mm_gm_dh")
            gw["gm_w_in"][j] = matmul(dzz, s["h"], ta=True, out_dtype=BF16, name="mm_gm_dwin")
        else:
            conv_dim = W["ssm_conv_w"].shape[2]
            G = (conv_dim - inner) // (2 * SSM_STATE)
            hpg = heads // G
            dyn = matmul(dx1, W["ssm_w_out"][j], tb=True, name="mm_ssm_dyn")
            gw["ssm_w_out"][j] = matmul(s["yn"], dx1, ta=True, out_dtype=BF16, name="mm_ssm_dwout")
            dz, dxs, dbm, dcm, ddt, dbias, dalog, dd, dng = ssd_bwd(
                s["xbc"], s["zx"], s["dtg"], s["bias_g"], s["alog_g"], s["d_chan"], s["ngain"], s["yfull"], s["hp"],
                dyn, L, G, name="ssd_bwd")
            dpre, dcw, dcb = conv_bwd(s["zx"], W["ssm_conv_w"][j], W["ssm_conv_b"][j:j + 1], inner, [dxs, dbm, dcm],
                                      name="conv_bwd")
            dzx = jnp.concatenate([dz, dpre], axis=1)
            dh = matmul(ddt, s["w_dtg"], name="mm_ssm_dh_dt")
            dx, dgn = matmul(dzx, s["w_zx"], residual=dh, norm_bwd=mix_norm_bwd, name="mm_ssm_dh")
            dw_zx = matmul(dzx, s["h"], ta=True, out_dtype=BF16, name="mm_ssm_dwzx")
            dw_dtg = matmul(ddt, s["h"], ta=True, out_dtype=BF16, name="mm_ssm_dwdt")
            dw_dt = dw_dtg.reshape(G, LANES, D)[:, :hpg, :].reshape(heads, D)
            gw["ssm_w_in"][j] = jnp.concatenate([dw_zx, dw_dt], axis=0)
            gw["ssm_conv_w"][j] = dcw
            gw["ssm_conv_b"][j] = dcb[0]
            gw["ssm_dt_bias"][j] = dbias[:, 0, :hpg].reshape(heads)
            gw["ssm_a_log"][j] = dalog[:, 0, :hpg].reshape(heads)
            gw["ssm_d"][j] = dd[:, 0, :hpg].reshape(heads)
            gw["ssm_norm_gain"][j] = dng[0]
        gw["mix_norm"][i] = dgn[0]

    return loss, dx, gw, received


MESH = pl.DeviceIdType.MESH
HBM_SPEC = pl.BlockSpec(memory_space=pltpu.HBM)
VMEM_SPEC = pl.BlockSpec(memory_space=pltpu.VMEM)


def _my_position():
    return lax.axis_index("x"), lax.axis_index("y"), lax.axis_index("c")


def _flip(v, bit):
    return 1 - v if bit else v


def all_gather(shards, *, name):
    n = len(shards)

    def body(*refs):
        for phase in ("start", "forward", "finish"):
            _ag_phase(refs[:n], refs[n:2 * n], refs[2 * n:], phase)

    return pl.pallas_call(
        body, name=name, out_shape=[jax.ShapeDtypeStruct((N_DEV,) + s.shape, s.dtype) for s in shards],
        in_specs=[HBM_SPEC] * n, out_specs=[HBM_SPEC] * n, scratch_shapes=_copy_semaphores(n),
    )(*shards)


def _ag_phase(x_refs, out_refs, sems, phase):
    send_sems, recv_sems, local_sems = sems
    x, y, c = _my_position()
    me, sibling = (x, y, c), (x, y, 1 - c)
    chips = [(1 - x, y), (x, 1 - y), (1 - x, 1 - y)]
    for p, (x_ref, out_ref) in enumerate(zip(x_refs, out_refs)):
        def slot(px, py, pc):
            return out_ref.at[4 * px + 2 * py + pc]

        def copy(k, block, to, src=None):
            return pltpu.make_async_remote_copy(
                src_ref=slot(*block) if src is None else src, dst_ref=slot(*block),
                send_sem=send_sems.at[7 * p + k], recv_sem=recv_sems.at[7 * p + k], device_id=to, device_id_type=MESH)

        mine = pltpu.make_async_copy(x_ref, slot(*me), local_sems.at[p])
        first = [copy(0, me, sibling, src=x_ref)]
        first += [copy(1 + j, me, (*chip, c), src=x_ref) for j, chip in enumerate(chips)]
        passed = [copy(4 + j, (*chip, c), sibling) for j, chip in enumerate(chips)]
        if phase == "start":
            mine.start()
            for cp in first:
                cp.start()
        elif phase == "forward":
            for j, chip in enumerate(chips):
                copy(1 + j, (*chip, c), me).wait_recv()
                passed[j].start()
        else:
            copy(0, sibling, me).wait_recv()
            for j, chip in enumerate(chips):
                copy(4 + j, (*chip, 1 - c), me).wait_recv()
            for cp in first + passed:
                cp.wait_send()
            mine.wait()


def _copy_semaphores(n):
    return [pltpu.SemaphoreType.DMA((7 * n,)), pltpu.SemaphoreType.DMA((7 * n,)), pltpu.SemaphoreType.DMA((n,))]


def _rs_phase(g_refs, out_refs, sems, phase):
    send_sems, recv_sems, local_sems = sems
    x, y, c = _my_position()
    me = 4 * x + 2 * y + c
    copies = []
    for p, (g_ref, out_ref) in enumerate(zip(g_refs, out_refs)):
        copies.append(pltpu.make_async_copy(g_ref.at[me], out_ref.at[me], local_sems.at[p]))
        for k in range(1, N_DEV):
            px, py, pc = _flip(x, k & 4), _flip(y, k & 2), _flip(c, k & 1)
            copies.append(pltpu.make_async_remote_copy(
                src_ref=g_ref.at[4 * px + 2 * py + pc], dst_ref=out_ref.at[me],
                send_sem=send_sems.at[7 * p + k - 1], recv_sem=recv_sems.at[7 * p + k - 1],
                device_id=(px, py, pc), device_id_type=MESH))
    for cp in copies:
        if phase == "start":
            cp.start()
        else:
            cp.wait()


def sum_slots(recvs, *, name):
    n, R, C = recvs[0].shape
    nl = len(recvs)
    tr = _pick(R, (512, 256, 128, 64, 32, 16))

    def body(*refs):
        o_ref = refs[nl]
        for l in range(nl):
            @pl.when(pl.program_id(0) == l)
            def _(r_ref=refs[l]):
                acc = r_ref[0].astype(F32)
                for s in range(1, n):
                    acc = acc + r_ref[s].astype(F32)
                o_ref[...] = acc

    return pl.pallas_call(
        body, name=name, grid=(nl, R // tr),
        in_specs=[pl.BlockSpec((n, tr, C), lambda l, i, m=m: (0, jnp.where(l == m, i, 0), 0)) for m in range(nl)],
        out_specs=pl.BlockSpec((None, tr, C), lambda l, i: (l, i, 0)), out_shape=jax.ShapeDtypeStruct((nl, R, C), F32),
        compiler_params=_params(("arbitrary", "arbitrary")),
    )(*recvs)


def all_reduce_small(vs, scatter, *, name):
    n, ns = len(vs), len(scatter)

    def body(*refs):
        v_refs, g_refs = refs[:n], refs[n:n + ns]
        o_refs, r_refs = refs[n + ns:2 * n + ns], refs[2 * n + ns:2 * n + 2 * ns]
        bufs = refs[2 * n + 2 * ns:3 * n + 2 * ns]
        send_sems, recv_sems = refs[3 * n + 2 * ns:3 * n + 2 * ns + 2]
        rs_sems = refs[3 * n + 2 * ns + 2:]
        x, y, c = _my_position()
        me = 4 * x + 2 * y + c
        _rs_phase(g_refs, r_refs, rs_sems, "start")
        copies = []
        for p, (v_ref, buf) in enumerate(zip(v_refs, bufs)):
            buf[me] = v_ref[...]
            for k in range(1, N_DEV):
                px, py, pc = _flip(x, k & 4), _flip(y, k & 2), _flip(c, k & 1)
                copies.append(pltpu.make_async_remote_copy(
                    src_ref=v_ref, dst_ref=buf.at[me], send_sem=send_sems.at[7 * p + k - 1],
                    recv_sem=recv_sems.at[7 * p + k - 1], device_id=(px, py, pc), device_id_type=MESH))
        for cp in copies:
            cp.start()
        for cp in copies:
            cp.wait()
        for o_ref, buf in zip(o_refs, bufs):
            acc = buf[0]
            for s in range(1, N_DEV):
                acc = acc + buf[s]
            o_ref[...] = acc
        _rs_phase(g_refs, r_refs, rs_sems, "finish")

    out = pl.pallas_call(
        body, name=name,
        out_shape=[jax.ShapeDtypeStruct(v.shape, F32) for v in vs] + [jax.ShapeDtypeStruct(g.shape, g.dtype) for g in scatter],
        in_specs=[VMEM_SPEC] * n + [HBM_SPEC] * ns, out_specs=[VMEM_SPEC] * n + [HBM_SPEC] * ns,
        scratch_shapes=[pltpu.VMEM((N_DEV,) + v.shape, F32) for v in vs]
        + [pltpu.SemaphoreType.DMA((7 * n,)), pltpu.SemaphoreType.DMA((7 * n,))] + _copy_semaphores(ns),
        compiler_params=pltpu.CompilerParams(vmem_limit_bytes=VMEM_LIMIT_BYTES),
    )(*vs, *scatter)
    return list(out[:n]), list(out[n:])


def _pad_rows(a, mult):
    pad = (-a.shape[0]) % mult
    return jnp.pad(a, ((0, pad), (0, 0))) if pad else a


def _pack_small(arrays):
    flat = []
    for a in arrays:
        f = a.reshape(-1).astype(F32)
        flat.append(jnp.pad(f, (0, (-f.shape[0]) % LANES)))
    return _pad_rows(jnp.concatenate(flat).reshape(-1, LANES), 8)


def _unpack_small(packed, shapes):
    flat = packed.reshape(-1)
    out, r = [], 0
    for shp in shapes:
        n = math.prod(shp)
        out.append(flat[r:r + n].reshape(shp))
        r += n + (-n) % LANES
    return out


ARG_NAMES = ("x",) + WEIGHTS + ("loss_target",) + tuple("m_" + w for w in WEIGHTS) + tuple("v_" + w for w in WEIGHTS)


def kernel(x, mix_norm, ffn_norm, sb_w_qkv, sb_q_gain, sb_k_gain, sb_w_o, gm_w_in, gm_b_in, gm_v_gain, gm_w_s, gm_b_s, gm_w_out, ssm_w_in, ssm_conv_w, ssm_conv_b, ssm_dt_bias, ssm_a_log, ssm_d, ssm_norm_gain, ssm_w_out, ffn_w_gu, ffn_w_down, loss_target, m_mix_norm, m_ffn_norm, m_sb_w_qkv, m_sb_q_gain, m_sb_k_gain, m_sb_w_o, m_gm_w_in, m_gm_b_in, m_gm_v_gain, m_gm_w_s, m_gm_b_s, m_gm_w_out, m_ssm_w_in, m_ssm_conv_w, m_ssm_conv_b, m_ssm_dt_bias, m_ssm_a_log, m_ssm_d, m_ssm_norm_gain, m_ssm_w_out, m_ffn_w_gu, m_ffn_w_down, v_mix_norm, v_ffn_norm, v_sb_w_qkv, v_sb_q_gain, v_sb_k_gain, v_sb_w_o, v_gm_w_in, v_gm_b_in, v_gm_v_gain, v_gm_w_s, v_gm_b_s, v_gm_w_out, v_ssm_w_in, v_ssm_conv_w, v_ssm_conv_b, v_ssm_dt_bias, v_ssm_a_log, v_ssm_d, v_ssm_norm_gain, v_ssm_w_out, v_ffn_w_gu, v_ffn_w_down):
    given = dict(zip(ARG_NAMES, (x, mix_norm, ffn_norm, sb_w_qkv, sb_q_gain, sb_k_gain, sb_w_o, gm_w_in, gm_b_in, gm_v_gain, gm_w_s, gm_b_s, gm_w_out, ssm_w_in, ssm_conv_w, ssm_conv_b, ssm_dt_bias, ssm_a_log, ssm_d, ssm_norm_gain, ssm_w_out, ffn_w_gu, ffn_w_down, loss_target, m_mix_norm, m_ffn_norm, m_sb_w_qkv, m_sb_q_gain, m_sb_k_gain, m_sb_w_o, m_gm_w_in, m_gm_b_in, m_gm_v_gain, m_gm_w_s, m_gm_b_s, m_gm_w_out, m_ssm_w_in, m_ssm_conv_w, m_ssm_conv_b, m_ssm_dt_bias, m_ssm_a_log, m_ssm_d, m_ssm_norm_gain, m_ssm_w_out, m_ffn_w_gu, m_ffn_w_down, v_mix_norm, v_ffn_norm, v_sb_w_qkv, v_sb_q_gain, v_sb_k_gain, v_sb_w_o, v_gm_w_in, v_gm_b_in, v_gm_v_gain, v_gm_w_s, v_gm_b_s, v_gm_w_out, v_ssm_w_in, v_ssm_conv_w, v_ssm_conv_b, v_ssm_dt_bias, v_ssm_a_log, v_ssm_d, v_ssm_norm_gain, v_ssm_w_out, v_ffn_w_gu, v_ffn_w_down)))
    mx, my, mc = _my_position()
    me = 4 * mx + 2 * my + mc

    pieces = [(k, l) for k in BIG for l in range(given[k].shape[0])]
    early = [("sb_w_qkv", 0)]
    late_pieces = [p for p in pieces if p not in early]
    last = [("sb_w_qkv", 0)]
    main = [p for p in pieces if p not in last]

    def shards_of(ps):
        return [(given[k][l].T if k in COL_SHARDED else given[k][l]).astype(BF16) for k, l in ps]

    def piece_to_full(g, k):
        rows, cols = given[k].shape[1:]
        return g.reshape(N_DEV * cols, rows) if k in COL_SHARDED else g.reshape(N_DEV * rows, cols)

    def full_to_piece(full, k):
        return full.reshape(N_DEV, -1, PACK_COLS)

    def summed_to_shard(g, k):
        layers, rows, cols = given[k].shape
        return g.reshape(layers, cols, rows) if k in COL_SHARDED else g.reshape(layers, rows, cols)

    def contributions(gw, ps):
        return [full_to_piece(gw[k][l], k) for k, l in ps]

    sharded_small = [lax.bitcast_convert_type(given[k], BF16) for k in SMALL_SHARDED]
    tail = jnp.concatenate([a.reshape(-1) for a in sharded_small])
    tail = jnp.pad(tail, (0, (-tail.size) % PACK_COLS)).reshape(-1, PACK_COLS)

    W = {k: given[k] for k in SMALL if k not in SMALL_SHARDED}
    W.update({k: [None] * given[k].shape[0] for k in BIG})
    for (k, l), g in zip(early, all_gather(shards_of(early), name="all_gather_early")):
        W[k][l] = piece_to_full(g, k)

    class Late:
        shards = shards_of(late_pieces) + [tail]

        @staticmethod
        def fill(weights, gathered):
            for (k, l), g in zip(late_pieces, gathered):
                weights[k][l] = piece_to_full(g, k)
            tail_g = gathered[-1].reshape(N_DEV, -1)
            off = 0
            for k, a in zip(SMALL_SHARDED, sharded_small):
                g = lax.bitcast_convert_type(tail_g[:, off:off + a.size].reshape((N_DEV,) + a.shape), F32)
                weights[k] = jnp.moveaxis(g, 0, -2).reshape(g.shape[1:-1] + (N_DEV * g.shape[-1],))
                off += a.size

        @staticmethod
        def contributions(gw):
            return contributions(gw, main)

    loss, gx, gw, received_main = local_step(given["x"][0], given["loss_target"][0], W, late=Late)

    grads_small = {k: jnp.stack([gw[k][l] for l in sorted(gw[k])], axis=0) for k in SMALL}
    packed_names = tuple(k for k in SMALL if k != "gm_w_s")
    small_shapes = [grads_small[k].shape for k in packed_names] + [(1, 1)]
    (red_ws, red_rest), received_last = all_reduce_small(
        [grads_small["gm_w_s"].reshape(-1, LANES), _pack_small([grads_small[k] for k in packed_names] + [loss])],
        contributions(gw, last), name="all_reduce_small_and_last_exchange")
    small_full = dict(zip(packed_names + ("loss",), _unpack_small(red_rest, small_shapes)))
    small_full["gm_w_s"] = red_ws.reshape(grads_small["gm_w_s"].shape)

    recv_piece = dict(zip(main + last, list(received_main) + list(received_last)))
    out_g, out_d, out_m, out_v = {}, {}, {}, {}
    for k in BIG:
        swap = lambda a: jnp.swapaxes(a, -1, -2)
        keep_t = k in COL_SHARDED and given[k].shape[-1] % LANES != 0
        flip = swap if keep_t else (lambda a: a)
        g = summed_to_shard(sum_slots([recv_piece[(k, l)] for l in range(given[k].shape[0])],
                                      name="reduce_scatter_sum_" + k), k)
        if k in COL_SHARDED and not keep_t:
            g = swap(g)
        res = adamw(flip(given[k]), g, flip(given["m_" + k]), flip(given["v_" + k]), name="adamw_" + k)
        out_g[k], out_d[k], out_m[k], out_v[k] = (flip(a) for a in (g,) + tuple(res))

    gsmall = {}
    for k in SMALL:
        g = small_full[k]
        if k in SMALL_SHARDED:
            n = given[k].shape[-1]
            g = lax.dynamic_slice_in_dim(g, me * n, n, axis=g.ndim - 1)
        gsmall[k] = g
    local_shapes = [given[k].shape for k in packed_names]
    dsm, nmsm, nvsm = adamw(*[_pack_small([src[k] for k in packed_names]) for src in (
        given, gsmall, {k: given["m_" + k] for k in packed_names}, {k: given["v_" + k] for k in packed_names})],
        name="adamw_small")
    out_g.update(gsmall)
    for dst, src in ((out_d, dsm), (out_m, nmsm), (out_v, nvsm)):
        dst.update(zip(packed_names, _unpack_small(src, local_shapes)))
    ws_shape = given["gm_w_s"].shape
    out_d["gm_w_s"], out_m["gm_w_s"], out_v["gm_w_s"] = (a.reshape(ws_shape) for a in adamw(
        *[a.reshape((-1,) + ws_shape[-2:]) for a in (given["gm_w_s"], gsmall["gm_w_s"], given["m_gm_w_s"], given["v_gm_w_s"])],
        name="adamw_gm_w_s"))

    return (small_full["loss"].reshape(()), gx[None],
            *[out_g[k] for k in WEIGHTS], *[out_d[k] for k in WEIGHTS],
            *[out_m[k] for k in WEIGHTS], *[out_v[k] for k in WEIGHTS])
```

```python
import math

import jax
import jax.numpy as jnp
from jax import lax
from jax.experimental import pallas as pl
from jax.experimental.pallas import tpu as pltpu

F32 = jnp.float32
BF16 = jnp.bfloat16
EPS = 1e-6
N_DEV = 8
SB_HEAD_DIM = 64
SB_TILE = 256
SB_FWD_QUERY_BLOCKS = 4
SB_BWD_QUERY_BLOCKS = 2
SSM_STATE = 128
SSD_FWD_GROUPS_PER_STEP = 2
SSD_BWD_GROUPS_PER_STEP = 1
SSM_CONV = 4
ADAM_LR = 0.001
ADAM_B1 = 0.9
ADAM_B2 = 0.999
ADAM_EPS = 1e-08
ADAM_WD = 0.01
ADAM_STEP = 10
VMEM_LIMIT_BYTES = 56 * 1024 * 1024
MATMUL_VMEM_BUDGET = 40 * 1024 * 1024
LANES = 128
PACK_COLS = 1024

BIG = ("sb_w_qkv", "sb_w_o", "gm_w_in", "gm_w_out", "ssm_w_in", "ssm_w_out", "ffn_w_gu", "ffn_w_down")
COL_SHARDED = ("sb_w_qkv", "gm_w_in", "ssm_w_in", "ffn_w_gu")
SMALL = ("mix_norm", "ffn_norm", "sb_q_gain", "sb_k_gain", "gm_b_in", "gm_v_gain", "gm_w_s", "gm_b_s",
         "ssm_conv_w", "ssm_conv_b", "ssm_dt_bias", "ssm_a_log", "ssm_d", "ssm_norm_gain")
SMALL_SHARDED = ("ssm_conv_w", "ssm_conv_b", "ssm_norm_gain")
WEIGHTS = ("mix_norm", "ffn_norm", "sb_w_qkv", "sb_q_gain", "sb_k_gain", "sb_w_o", "gm_w_in", "gm_b_in",
           "gm_v_gain", "gm_w_s", "gm_b_s", "gm_w_out", "ssm_w_in", "ssm_conv_w", "ssm_conv_b", "ssm_dt_bias",
           "ssm_a_log", "ssm_d", "ssm_norm_gain", "ssm_w_out", "ffn_w_gu", "ffn_w_down")


def _params(semantics=None):
    return pltpu.CompilerParams(dimension_semantics=semantics, vmem_limit_bytes=VMEM_LIMIT_BYTES)


def _pick(n, prefs):
    for t in prefs:
        if t <= n and n % t == 0:
            return t
    return n


def _dot(a, b, ca=1, cb=0):
    return lax.dot_general(a, b, (((ca,), (cb,)), ((), ())), preferred_element_type=F32)


def _split3(v):
    h1 = v.astype(BF16)
    r1 = v - h1.astype(F32)
    h2 = r1.astype(BF16)
    h3 = (r1 - h2.astype(F32)).astype(BF16)
    return h1, h2, h3


def _dot_exact_left(mat01, v):
    h1, h2, h3 = _split3(v)
    return _dot(mat01, h1) + _dot(mat01, h2) + _dot(mat01, h3)


def _sum_all(v):
    return jnp.sum(jnp.sum(v, axis=0, keepdims=True), axis=1, keepdims=True)


def _sigmoid(v):
    return 1.0 / (1.0 + jnp.exp(-v))


def _softplus(v):
    return jnp.maximum(v, 0.0) + jnp.log(1.0 + jnp.exp(-jnp.abs(v)))


def _erf(v):
    a = jnp.abs(v)
    t = 1.0 / (1.0 + 0.3275911 * a)
    poly = t * (0.254829592 + t * (-0.284496736 + t * (1.421413741 + t * (-1.453152027 + t * 1.061405429))))
    e = 1.0 - poly * jnp.exp(-a * a)
    return jnp.where(v < 0, -e, e)


def _gelu_and_grad(v):
    cdf = 0.5 * (1.0 + _erf(v * (1.0 / math.sqrt(2.0))))
    pdf = jnp.exp(-0.5 * v * v) * (1.0 / math.sqrt(2.0 * math.pi))
    return v * cdf, cdf + v * pdf


def matmul(a, b, *, ta=False, tb=False, out_dtype=F32, residual=None, a_split=1, b_split=1, norm_bwd=None,
           norm_fwd=None, loss_target=None, name):
    if a_split > 1 and ta:
        assert a.shape[0] == a_split
        K, M = a.shape[1], a_split * a.shape[2]
    elif a_split > 1:
        assert a.shape[0] == a_split
        M, K = a.shape[1], a_split * a.shape[2]
    elif ta:
        K, M = a.shape
    else:
        M, K = a.shape
    if b_split > 1:
        assert not tb and b.shape[0] == b_split
        Kb, N = b.shape[1], b_split * b.shape[2]
    elif tb:
        N, Kb = b.shape
    else:
        Kb, N = b.shape
    assert K == Kb, (a.shape, b.shape, ta, tb)
    has_res = residual is not None
    tm = _pick(M // a_split if ta else M, (1024, 1408, 768, 512, 256, 128))
    tn = _pick(N // b_split, (1024, 1408, 1536, 768, 512, 256, 128))

    has_norm = norm_bwd is not None
    has_nf = norm_fwd is not None
    has_loss = loss_target is not None
    assert not (has_norm or has_nf or has_loss) or (tn == N and out_dtype == F32)
    assert has_norm + has_nf + has_loss <= 1

    def vmem_bytes(tk):
        tiles = tm * tk * a.dtype.itemsize + tk * tn * b.dtype.itemsize
        outs = tm * tn * jnp.dtype(out_dtype).itemsize + (tm * tn * 4 if has_res else 0) + (2 * tm * tn * 4 if has_norm else 0)
        return 2 * tiles + 2 * outs + (tm * tn * 4 if tk < K else 0)

    kp = K if ta else K // a_split
    tk = next((t for t in (K, 2048, 1408, 1024, 512, 256) if t <= kp and kp % t == 0 and vmem_bytes(t) <= MATMUL_VMEM_BUDGET),
              _pick(kp, (128,)))
    nk = K // tk
    if a_split > 1 and ta:
        nib = M // a_split // tm
        a_spec = pl.BlockSpec((None, tk, tm), lambda i, j, k: (i // nib, k, i % nib))
    elif a_split > 1:
        nkb = kp // tk
        a_spec = pl.BlockSpec((None, tm, tk), lambda i, j, k: (k // nkb, i, k % nkb))
    else:
        a_spec = pl.BlockSpec((tk, tm), lambda i, j, k: (k, i)) if ta else pl.BlockSpec((tm, tk), lambda i, j, k: (i, k))
    if b_split > 1:
        njb = N // b_split // tn
        b_spec = pl.BlockSpec((None, tk, tn), lambda i, j, k: (j // njb, k, j % njb))
    else:
        b_spec = pl.BlockSpec((tn, tk), lambda i, j, k: (j, k)) if tb else pl.BlockSpec((tk, tn), lambda i, j, k: (k, j))
    o_spec = pl.BlockSpec((tm, tn), lambda i, j, k: (i, j))
    ca, cb = (0 if ta else 1), (1 if tb else 0)

    def body(*refs):
        a_ref, b_ref = refs[:2]
        n_in = 2 + has_res + 3 * has_norm + has_nf + has_loss
        r_ref = refs[2] if has_res else None
        o_ref = refs[n_in]
        if has_loss:
            @pl.when(jnp.logical_and(pl.program_id(0) == 0, pl.program_id(2) == 0))
            def _():
                refs[n_in + 1][...] = jnp.zeros_like(refs[n_in + 1])
        if has_norm:
            x_ref, g_ref, dres_ref = refs[n_in - 3:n_in]
            dg_ref = refs[n_in + 1]

            @pl.when(jnp.logical_and(pl.program_id(0) == 0, pl.program_id(2) == 0))
            def _():
                dg_ref[...] = jnp.zeros_like(dg_ref)

        def finish(r):
            if has_res:
                r = r + r_ref[...]
            if has_norm:
                xv = x_ref[...]
                rs = lax.rsqrt(jnp.mean(xv * xv, axis=1, keepdims=True) + EPS)
                xhat = xv * rs
                t = r * g_ref[...]
                dg_ref[...] += jnp.sum(r * xhat, axis=0, keepdims=True)
                r = dres_ref[...] + rs * (t - xhat * jnp.mean(xhat * t, axis=1, keepdims=True))
            if has_nf:
                rs = lax.rsqrt(jnp.mean(r * r, axis=1, keepdims=True) + EPS)
                refs[n_in + 1][...] = (r * rs * refs[n_in - 1][...]).astype(BF16)
            if has_loss:
                err = r - refs[n_in - 1][...]
                refs[n_in + 1][...] += jnp.sum(0.5 * jnp.mean(err * err, axis=1, keepdims=True), axis=0, keepdims=True)
                r = err * (1.0 / N)
            o_ref[...] = r.astype(out_dtype)

        def part():
            return _dot(a_ref[...].astype(BF16), b_ref[...].astype(BF16), ca, cb)

        if nk == 1:
            finish(part())
            return
        acc = refs[-1]
        k = pl.program_id(2)


        @pl.when(k == 0)
        def _():
            acc[...] = part()

        @pl.when(jnp.logical_and(k > 0, k < nk - 1))
        def _():
            acc[...] += part()

        @pl.when(k == nk - 1)
        def _():
            finish(acc[...] + part())

    in_specs = [a_spec, b_spec] + ([o_spec] if has_res else [])
    args = (a, b) + ((residual,) if has_res else ())
    out_specs, out_shape = o_spec, jax.ShapeDtypeStruct((M, N), out_dtype)
    if has_norm:
        vec = pl.BlockSpec((1, N), lambda i, j, k: (0, 0))
        in_specs += [o_spec, vec, o_spec]
        args += tuple(norm_bwd)
        out_specs, out_shape = [o_spec, vec], [out_shape, jax.ShapeDtypeStruct((1, N), F32)]
    if has_nf:
        in_specs += [pl.BlockSpec((1, N), lambda i, j, k: (0, 0))]
        args += (norm_fwd,)
        out_specs, out_shape = [o_spec, o_spec], [out_shape, jax.ShapeDtypeStruct((M, N), BF16)]
    if has_loss:
        in_specs += [o_spec]
        args += (loss_target,)
        out_specs = [o_spec, pl.BlockSpec((1, 1), lambda i, j, k: (0, 0))]
        out_shape = [out_shape, jax.ShapeDtypeStruct((1, 1), F32)]
    return pl.pallas_call(
        body, name=name, grid=(M // tm, N // tn, nk), in_specs=in_specs, out_specs=out_specs, out_shape=out_shape,
        scratch_shapes=[pltpu.VMEM((tm, tn), F32)] if nk > 1 else [],
        compiler_params=_params(("arbitrary" if has_norm or has_loss else "parallel", "parallel", "arbitrary")),
    )(*args)


def rms_fwd(x, gain, *, name):
    S, D = x.shape
    tr = _pick(S, (512, 256, 128))

    def body(x_ref, g_ref, o_ref):
        xv = x_ref[...]
        r = lax.rsqrt(jnp.mean(xv * xv, axis=1, keepdims=True) + EPS)
        o_ref[...] = (xv * r * g_ref[...]).astype(BF16)

    return pl.pallas_call(
        body, name=name, grid=(S // tr,),
        in_specs=[pl.BlockSpec((tr, D), lambda i: (i, 0)), pl.BlockSpec((1, D), lambda i: (0, 0))],
        out_specs=pl.BlockSpec((tr, D), lambda i: (i, 0)), out_shape=jax.ShapeDtypeStruct((S, D), BF16),
        compiler_params=_params(("parallel",)),
    )(x, gain)


def ffn_up_fwd(h, w_gu_t, *, name):
    S, K = h.shape
    F = w_gu_t.shape[0] // 2
    tm = _pick(S, (512, 256, 128))
    tn = _pick(F, (1408, 1024, 768, 512, 256, 128))
    nj = F // tn

    def body(h_ref, wg_ref, wu_ref, gu_ref, a_ref):
        hv = h_ref[...]
        g = _dot(hv, wg_ref[...], 1, 1)
        u = _dot(hv, wu_ref[...], 1, 1)
        gu_ref[0] = g
        gu_ref[1] = u
        a_ref[...] = (g * _sigmoid(g) * u).astype(BF16)

    return pl.pallas_call(
        body, name=name, grid=(nj, S // tm),
        in_specs=[pl.BlockSpec((tm, K), lambda j, i: (i, 0)), pl.BlockSpec((tn, K), lambda j, i: (j, 0)),
                  pl.BlockSpec((tn, K), lambda j, i: (nj + j, 0))],
        out_specs=[pl.BlockSpec((2, tm, tn), lambda j, i: (0, i, j)), pl.BlockSpec((tm, tn), lambda j, i: (i, j))],
        out_shape=[jax.ShapeDtypeStruct((2, S, F), F32), jax.ShapeDtypeStruct((S, F), BF16)],
        compiler_params=_params(("parallel", "parallel")),
    )(h, w_gu_t, w_gu_t)


def ffn_up_bwd(dy, w_down, gu, *, name):
    S, D = dy.shape
    F = w_down.shape[0]
    tm = _pick(S, (512, 256, 128))
    tn = _pick(F, (1408, 1024, 768, 512, 256, 128))

    def body(dy_ref, wd_ref, gu_ref, o_ref):
        da = _dot(dy_ref[...].astype(BF16), wd_ref[...], 1, 1)
        g = gu_ref[0]
        u = gu_ref[1]
        s = _sigmoid(g)
        o_ref[0] = (da * u * (s * (1.0 + g * (1.0 - s)))).astype(BF16)
        o_ref[1] = (da * g * s).astype(BF16)

    pair = pl.BlockSpec((2, tm, tn), lambda j, i: (0, i, j))
    return pl.pallas_call(
        body, name=name, grid=(F // tn, S // tm),
        in_specs=[pl.BlockSpec((tm, D), lambda j, i: (i, 0)), pl.BlockSpec((tn, D), lambda j, i: (j, 0)), pair],
        out_specs=pair, out_shape=jax.ShapeDtypeStruct((2, S, F), BF16),
        compiler_params=_params(("parallel", "parallel")),
    )(dy, w_down, gu)


def loss_head(y, target, *, name):
    S, D = y.shape
    tr = _pick(S, (512, 256, 128))

    def body(y_ref, t_ref, dy_ref, l_ref):
        @pl.when(pl.program_id(0) == 0)
        def _():
            l_ref[...] = jnp.zeros_like(l_ref)

        err = y_ref[...] - t_ref[...]
        dy_ref[...] = err * (1.0 / D)
        l_ref[...] += jnp.sum(0.5 * jnp.mean(err * err, axis=1, keepdims=True), axis=0, keepdims=True)

    row = pl.BlockSpec((tr, D), lambda i: (i, 0))
    one = pl.BlockSpec((1, 1), lambda i: (0, 0))
    dy, l = pl.pallas_call(
        body, name=name, grid=(S // tr,), in_specs=[row, row], out_specs=[row, one],
        out_shape=[jax.ShapeDtypeStruct((S, D), F32), jax.ShapeDtypeStruct((1, 1), F32)],
        compiler_params=_params(("arbitrary",)),
    )(y, target)
    return dy, l


def adamw(w, g, m, v, *, name):
    R, C = w.shape[-2:]
    tr = _pick(R, (512, 256, 128, 64, 32, 16, 8))

    def body(w_ref, g_ref, m_ref, v_ref, d_ref, mo_ref, vo_ref):
        gv = g_ref[...]
        mn = ADAM_B1 * m_ref[...] + (1.0 - ADAM_B1) * gv
        vn = ADAM_B2 * v_ref[...] + (1.0 - ADAM_B2) * jnp.square(gv)
        m_hat = mn / (1.0 - ADAM_B1 ** ADAM_STEP)
        v_hat = vn / (1.0 - ADAM_B2 ** ADAM_STEP)
        d_ref[...] = -ADAM_LR * (m_hat / (jnp.sqrt(v_hat) + ADAM_EPS) + ADAM_WD * w_ref[...])
        mo_ref[...] = mn
        vo_ref[...] = vn

    tc = C if tr < R or C % LANES else _pick(C, (256, 128))
    if w.ndim == 3:
        grid = (w.shape[0], R // tr, C // tc)
        blk = pl.BlockSpec((None, tr, tc), lambda l, i, j: (l, i, j))
    else:
        grid = (R // tr, C // tc)
        blk = pl.BlockSpec((tr, tc), lambda i, j: (i, j))
    sds = jax.ShapeDtypeStruct(w.shape, F32)
    return pl.pallas_call(
        body, name=name, grid=grid, in_specs=[blk] * 4, out_specs=[blk] * 3, out_shape=[sds] * 3,
        compiler_params=_params(("parallel",) * len(grid)),
    )(w, g, m, v)


def _tri(n, kind):
    r = lax.broadcasted_iota(jnp.int32, (n, n), 0)
    c = lax.broadcasted_iota(jnp.int32, (n, n), 1)
    if kind == "row_gt_col":
        return (r > c).astype(BF16)
    if kind == "row_ge_col":
        return (r >= c).astype(BF16)
    if kind == "row_le_col":
        return (r <= c).astype(BF16)
    raise ValueError(kind)


def _sb_tile(qs, kj, r_carry, u_strict, masked):
    z = _dot(qs, kj, 1, 1)
    lb = jnp.minimum(z, 0.0) - jnp.log(1.0 + jnp.exp(-jnp.abs(z)))
    l1m = lb - z
    keep = None
    if masked:
        tq, tk = z.shape
        keep = lax.broadcasted_iota(jnp.int32, (tq, tk), 1) < lax.broadcasted_iota(jnp.int32, (tq, tk), 0)
        l1m = jnp.where(keep, l1m, 0.0)
    w = jnp.exp(lb + _dot(l1m.astype(BF16), u_strict) + r_carry)
    if masked:
        w = jnp.where(keep, w, 0.0)
    return lb, l1m, w, keep


def _sb_prep(T, nb, hd, refs_in, gains, scratch):
    q_scale = 1.0 / math.sqrt(hd)
    assert math.log2(q_scale) == round(math.log2(q_scale))

    def prep(i, _):
        rows = pl.ds(pl.multiple_of(i * T, T), T)
        for hh in range(2):
            sl = slice(hd * hh, hd * hh + hd)
            for n, (src, dst) in enumerate(zip(refs_in, scratch)):
                v = src[rows, sl]
                if n < 2:
                    v = v * lax.rsqrt(jnp.mean(v * v, axis=1, keepdims=True) + EPS) * gains[n][...]
                if n == 0:
                    v = v * q_scale
                dst[hh, rows, :] = v.astype(BF16)
        return 0

    lax.fori_loop(0, nb, prep, 0)


def _sb_chains(m, T, nq):
    rows = [pl.ds(pl.multiple_of((nq * m + qb) * T, T), T) for qb in range(nq)]
    return rows, [(hh, qb) for qb in range(nq) for hh in range(2)]


def _sb_sweep(tile, carry, chains, m, nq):
    for kk in reversed(range(nq)):
        carry = tile(nq * m + kk, carry, [(ch, ch[1] == kk) for ch in chains if ch[1] >= kk])
    return lax.fori_loop(0, nq * m, lambda jj, c: tile(nq * m - 1 - jj, c, [(ch, False) for ch in chains]), carry)


def sb_attn_fwd(qkv, q_gain, k_gain, *, name, gather=None):
    S, D3 = qkv.shape
    D = D3 // 3
    npairs = D // LANES
    hd = SB_HEAD_DIM
    T = min(SB_TILE, S)
    nb = S // T
    nq = SB_FWD_QUERY_BLOCKS
    assert nb % nq == 0

    def body(*refs):
        if gather is None:
            q_ref, k_ref, v_ref, qg_ref, kg_ref, us_ref, o_ref, qn_s, kn_s, vb_s = refs
        else:
            ng = len(gather)
            q_ref, k_ref, v_ref, qg_ref, kg_ref, us_ref = refs[:6]
            o_ref = refs[6 + ng]
            qn_s, kn_s, vb_s = refs[7 + 2 * ng:10 + 2 * ng]
            comm = (refs[6:6 + ng], refs[7 + ng:7 + 2 * ng], refs[10 + 2 * ng:])
            step = pl.program_id(0)
            pl.when(step == 0)(lambda: _ag_phase(*comm, "start"))
            pl.when(step == npairs - 1)(lambda: _ag_phase(*comm, "forward"))
        us = us_ref[...]
        _sb_prep(T, nb, hd, (q_ref, k_ref, v_ref), (qg_ref, kg_ref), (qn_s, kn_s, vb_s))

        def superblock(m, _):
            rows_q, chains = _sb_chains(m, T, nq)
            qs = {ch: qn_s[ch[0], rows_q[ch[1]], :] for ch in chains}

            def tile(j, carry, which):
                rows_j = pl.ds(pl.multiple_of(j * T, T), T)
                new = dict(carry)
                for ch, masked in which:
                    acc, rc = carry[ch]
                    _, l1m, w, _ = _sb_tile(qs[ch], kn_s[ch[0], rows_j, :], rc, us, masked)
                    new[ch] = (acc + _dot(w.astype(BF16), vb_s[ch[0], rows_j, :]),
                               rc + jnp.sum(l1m, axis=1, keepdims=True))
                return new

            carry = {ch: (jnp.zeros((T, hd), F32), jnp.zeros((T, 1), F32)) for ch in chains}
            carry = _sb_sweep(tile, carry, chains, m, nq)
            for qb in range(nq):
                o_ref[rows_q[qb], :] = jnp.concatenate([carry[(0, qb)][0], carry[(1, qb)][0]], axis=1)
            return 0

        lax.fori_loop(0, nb // nq, superblock, 0)
        if gather is not None:
            pl.when(step == npairs - 1)(lambda: _ag_phase(*comm, "finish"))

    col = lambda off: pl.BlockSpec((S, LANES), lambda p, off=off: (0, off + p))
    gain = pl.BlockSpec((1, hd), lambda p: (0, 0))
    in_specs = [col(0), col(npairs), col(2 * npairs), gain, gain, pl.BlockSpec((T, T), lambda p: (0, 0))]
    out_specs = [pl.BlockSpec((S, LANES), lambda p: (0, p))]
    out_shape = [jax.ShapeDtypeStruct((S, D), F32)]
    scratch = [pltpu.VMEM((2, S, hd), BF16)] * 3
    args = [qkv, qkv, qkv, q_gain, k_gain, _tri(T, "row_gt_col")]
    if gather is not None:
        in_specs += [HBM_SPEC] * len(gather)
        out_specs += [HBM_SPEC] * len(gather)
        out_shape += [jax.ShapeDtypeStruct((N_DEV,) + s.shape, s.dtype) for s in gather]
        scratch += _copy_semaphores(len(gather))
        args += list(gather)
    out = pl.pallas_call(
        body, name=name, grid=(npairs,), in_specs=in_specs, out_specs=out_specs, out_shape=out_shape,
        scratch_shapes=scratch, compiler_params=_params(("arbitrary",)),
    )(*args)
    return out[0] if gather is None else (out[0], list(out[1:]))


def sb_attn_bwd(qkv, o, do, q_gain, k_gain, *, name, scatter=None):
    S, D3 = qkv.shape
    D = D3 // 3
    npairs = D // LANES
    hd = SB_HEAD_DIM
    T = min(SB_TILE, S)
    nb = S // T
    nq = SB_BWD_QUERY_BLOCKS
    assert nb % nq == 0
    scale = 1.0 / math.sqrt(hd)

    def body(*refs):
        if scatter is None:
            (q_ref, k_ref, v_ref, o_ref, do_ref, qg_ref, kg_ref, us_ref,
             dqkv_ref, dg_ref, qn_s, kn_s, vb_s, dob_s, acc_s) = refs
        else:
            ns = len(scatter)
            q_ref, k_ref, v_ref, o_ref, do_ref, qg_ref, kg_ref, us_ref = refs[:8]
            rs_in = refs[8:8 + ns]
            dqkv_ref, dg_ref = refs[8 + ns:10 + ns]
            rs_out = refs[10 + ns:10 + 2 * ns]
            qn_s, kn_s, vb_s, dob_s, acc_s = refs[10 + 2 * ns:15 + 2 * ns]
            rs_sems = refs[15 + 2 * ns:]
            pl.when(pl.program_id(0) == 0)(lambda: _rs_phase(rs_in, rs_out, rs_sems, "start"))
        dq_ref, dk_ref, dv_ref = acc_s.at[0], acc_s.at[1], acc_s.at[2]

        @pl.when(pl.program_id(0) == 0)
        def _():
            dg_ref[...] = jnp.zeros_like(dg_ref)

        us = us_ref[...]
        u_prefix = (1.0 - us.astype(F32)).astype(BF16)
        _sb_prep(T, nb, hd, (q_ref, k_ref, v_ref, do_ref), (qg_ref, kg_ref), (qn_s, kn_s, vb_s, dob_s))
        dk_ref[...] = jnp.zeros_like(dk_ref)
        dv_ref[...] = jnp.zeros_like(dv_ref)

        def superblock(m, _):
            rows_q, chains = _sb_chains(m, T, nq)
            qs = {ch: qn_s[ch[0], rows_q[ch[1]], :] for ch in chains}
            doi ={ch: dob_s[ch[0], rows_q[ch[1]], :] for ch in chains}
            dt_total = {ch: jnp.sum(doi[ch].astype(F32) * o_ref[rows_q[ch[1]], hd * ch[0]:hd * ch[0] + hd],
                                    axis=1, keepdims=True) for ch in chains}

            def tile(j, carry, which):
                rows_j = pl.ds(pl.multiple_of(j * T, T), T)
                new = dict(carry)
                dk_part, dv_part = {}, {}
                for ch, masked in which:
                    hh = ch[0]
                    dq_acc, rc, gc = carry[ch]
                    kj = kn_s[hh, rows_j, :]
                    lb, l1m, w, keep = _sb_tile(qs[ch], kj, rc, us, masked)
                    wb = w.astype(BF16)
                    g = _dot(doi[ch], vb_s[hh, rows_j, :], 1, 1) * wb.astype(F32)
                    g_row = jnp.sum(g, axis=1, keepdims=True)
                    g_upto = (dt_total[ch] - gc - g_row) + _dot(g.astype(BF16), u_prefix)
                    dz = g - g_upto * jnp.exp(lb)
                    if masked:
                        dz = jnp.where(keep, dz, 0.0)
                    dzb = dz.astype(BF16)
                    dv_part[hh] = dv_part.get(hh, 0.0) + _dot(wb, doi[ch], 0, 0)
                    dk_part[hh] = dk_part.get(hh, 0.0) + _dot(dzb, qs[ch], 0, 0)
                    new[ch] = (dq_acc + _dot(dzb, kj), rc + jnp.sum(l1m, axis=1, keepdims=True),
                               gc + g_row)
                dv_ref[rows_j, :] += jnp.concatenate([dv_part[0], dv_part[1]], axis=1)
                dk_ref[rows_j, :] += jnp.concatenate([dk_part[0], dk_part[1]], axis=1)
                return new

            zero1 = jnp.zeros((T, 1), F32)
            carry = {ch: (jnp.zeros((T, hd), F32), zero1, zero1) for ch in chains}
            carry = _sb_sweep(tile, carry, chains, m, nq)
            for qb in range(nq):
                dq_ref[rows_q[qb], :] = jnp.concatenate([carry[(0, qb)][0], carry[(1, qb)][0]], axis=1) * scale
            return 0

        lax.fori_loop(0, nb // nq, superblock, 0)

        def finish(i, carry):
            rows = pl.ds(pl.multiple_of(i * T, T), T)
            new = []
            for hh in range(2):
                sl = slice(hd * hh, hd * hh + hd)
                outs = []
                for raw_ref, gain_ref, dn in ((q_ref, qg_ref, dq_ref[rows, sl]), (k_ref, kg_ref, dk_ref[rows, sl])):
                    raw = raw_ref[rows, sl]
                    r = lax.rsqrt(jnp.mean(raw * raw, axis=1, keepdims=True) + EPS)
                    hat = raw * r
                    t = dn * gain_ref[...]
                    outs.append((r * (t - hat * jnp.mean(hat * t, axis=1, keepdims=True)),
                                 jnp.sum(dn * hat, axis=0, keepdims=True)))
                dqkv_ref[0, rows, sl] = outs[0][0].astype(BF16)
                dqkv_ref[1, rows, sl] = outs[1][0].astype(BF16)
                new.append((carry[hh][0] + outs[0][1], carry[hh][1] + outs[1][1]))
            dqkv_ref[2, rows, :] = dv_ref[rows, :].astype(BF16)
            return tuple(new)

        zg = (jnp.zeros((1, hd), F32), jnp.zeros((1, hd), F32))
        tot = lax.fori_loop(0, nb, finish, (zg, zg))
        dg_ref[0:1, 0:hd] += tot[0][0] + tot[1][0]
        dg_ref[1:2, 0:hd] += tot[0][1] + tot[1][1]
        if scatter is not None:
            pl.when(pl.program_id(0) == npairs - 1)(lambda: _rs_phase(rs_in, rs_out, rs_sems, "finish"))

    col = lambda off: pl.BlockSpec((S, LANES), lambda p, off=off: (0, off + p))
    gain = pl.BlockSpec((1, hd), lambda p: (0, 0))
    tri = pl.BlockSpec((T, T), lambda p: (0, 0))
    pair = pl.BlockSpec((S, LANES), lambda p: (0, p))
    in_specs = [col(0), col(npairs), col(2 * npairs), pair, pair, gain, gain, tri]
    out_specs = [pl.BlockSpec((3, S, LANES), lambda p: (0, 0, p)), pl.BlockSpec((8, LANES), lambda p: (0, 0))]
    out_shape = [jax.ShapeDtypeStruct((3, S, D), BF16), jax.ShapeDtypeStruct((8, LANES), F32)]
    scratch = [pltpu.VMEM((2, S, hd), BF16)] * 4 + [pltpu.VMEM((3, S, LANES), F32)]
    args = [qkv, qkv, qkv, o, do, q_gain, k_gain, _tri(T, "row_gt_col")]
    if scatter is not None:
        in_specs += [HBM_SPEC] * len(scatter)
        out_specs += [HBM_SPEC] * len(scatter)
        out_shape += [jax.ShapeDtypeStruct(g.shape, g.dtype) for g in scatter]
        scratch += _copy_semaphores(len(scatter))
        args += list(scatter)
    out = pl.pallas_call(
        body, name=name, grid=(npairs,), in_specs=in_specs, out_specs=out_specs, out_shape=out_shape,
        scratch_shapes=scratch, compiler_params=_params(("arbitrary",)),
    )(*args)
    res = (out[0], out[1][0:1, :hd], out[1][1:2, :hd])
    return res if scatter is None else res + (list(out[2:]),)


def gmlp_fwd(zzpre, b_in, v_gain, wc, bsf, *, name):
    S, H2 = zzpre.shape
    H = H2 // 2
    G, T, _ = wc.shape
    gd = H // G

    def body(z_ref, b_ref, vg_ref, wc_ref, bs_ref, p_ref):
        zz, _ = _gelu_and_grad(z_ref[...] + b_ref[...])
        u = zz[:, :H]
        v = zz[:, H:]
        vn = v * lax.rsqrt(jnp.mean(v * v, axis=1, keepdims=True) + EPS) * vg_ref[...]
        for g in range(G):
            gs = slice(g * gd, (g + 1) * gd)
            mixed = _dot(wc_ref[g], vn[:, gs].astype(BF16)) + bs_ref[g]
            p_ref[:, gs] = (u[:, gs] * mixed).astype(BF16)

    full3 = lambda shp: pl.BlockSpec(shp, lambda c: (0, 0, 0))
    return pl.pallas_call(
        body, name=name, grid=(S // T,),
        in_specs=[pl.BlockSpec((T, H2), lambda c: (c, 0)), pl.BlockSpec((1, H2), lambda c: (0, 0)),
                  pl.BlockSpec((1, H), lambda c: (0, 0)), full3((G, T, T)), full3((G, T, gd))],
        out_specs=pl.BlockSpec((T, H), lambda c: (c, 0)), out_shape=jax.ShapeDtypeStruct((S, H), BF16),
        compiler_params=_params(("parallel",)),
    )(zzpre, b_in, v_gain, wc, bsf)


def gmlp_bwd(zzpre, b_in, v_gain, wc, bsf, dp, *, name):
    S, H2 = zzpre.shape
    H = H2 // 2
    G, T, _ = wc.shape
    gd = H // G
    assert G <= LANES

    def body(z_ref, b_ref, vg_ref, wc_ref, bs_ref, dp_ref, dzz_ref, db_ref, dvg_ref, dws_ref, dbs_ref):
        @pl.when(pl.program_id(0) == 0)
        def _():
            db_ref[...] = jnp.zeros_like(db_ref)
            dvg_ref[...] = jnp.zeros_like(dvg_ref)
            dws_ref[...] = jnp.zeros_like(dws_ref)
            dbs_ref[...] = jnp.zeros_like(dbs_ref)

        zz, gp = _gelu_and_grad(z_ref[...] + b_ref[...])
        u = zz[:, :H]
        v = zz[:, H:]
        r = lax.rsqrt(jnp.mean(v * v, axis=1, keepdims=True) + EPS)
        vhat = v * r
        vg = vg_ref[...]
        vn = vhat * vg
        dpv = dp_ref[...]
        tril = lax.broadcasted_iota(jnp.int32, (T, T), 1) <= lax.broadcasted_iota(jnp.int32, (T, T), 0)
        lane = lax.broadcasted_iota(jnp.int32, (T, LANES), 1)
        dbs = jnp.zeros((T, LANES), F32)
        du_parts, dvn_parts = [], []
        for g in range(G):
            gs = slice(g * gd, (g + 1) * gd)
            vng = vn[:, gs].astype(BF16)
            wcg = wc_ref[g]
            mixed = _dot(wcg, vng) + bs_ref[g]
            dpg = dpv[:, gs]
            du_parts.append(dpg * mixed)
            dmx = dpg * u[:, gs]
            dmxb = dmx.astype(BF16)
            dvn_parts.append(_dot(wcg, dmxb, 0, 0))
            dws_ref[g] += jnp.where(tril, _dot(dmxb, vng, 1, 1), 0.0)
            dbs = dbs + jnp.where(lane == g, jnp.sum(dmx, axis=1, keepdims=True), 0.0)
        dbs_ref[...] += dbs
        du = jnp.concatenate(du_parts, axis=1)
        dvn = jnp.concatenate(dvn_parts, axis=1)
        dvg_ref[...] += jnp.sum(dvn * vhat, axis=0, keepdims=True)
        t = dvn * vg
        dv = r * (t - vhat * jnp.mean(vhat * t, axis=1, keepdims=True))
        dzu = du * gp[:, :H]
        dzv = dv * gp[:, H:]
        dzz_ref[:, :H] = dzu.astype(BF16)
        dzz_ref[:, H:] = dzv.astype(BF16)
        db_ref[:, :H] += jnp.sum(dzu, axis=0, keepdims=True)
        db_ref[:, H:] += jnp.sum(dzv, axis=0, keepdims=True)

    full3 = lambda shp: pl.BlockSpec(shp, lambda c: (0, 0, 0))
    vec = lambda n: pl.BlockSpec((1, n), lambda c: (0, 0))
    return pl.pallas_call(
        body, name=name, grid=(S // T,),
        in_specs=[pl.BlockSpec((T, H2), lambda c: (c, 0)), vec(H2), vec(H), full3((G, T, T)), full3((G, T, gd)),
                  pl.BlockSpec((T, H), lambda c: (c, 0))],
        out_specs=[pl.BlockSpec((T, H2), lambda c: (c, 0)), vec(H2), vec(H), full3((G, T, T)),
                   pl.BlockSpec((T, LANES), lambda c: (0, 0))],
        out_shape=[jax.ShapeDtypeStruct((S, H2), BF16), jax.ShapeDtypeStruct((1, H2), F32),
                   jax.ShapeDtypeStruct((1, H), F32), jax.ShapeDtypeStruct((G, T, T), F32),
                   jax.ShapeDtypeStruct((T, LANES), F32)],
        compiler_params=_params(("arbitrary",)),
    )(zzpre, b_in, v_gain, wc, bsf, dp)


def _shift_rows(v, k, n_rows):
    if k == 0:
        return v
    rolled = pltpu.roll(v, k % n_rows, 0)
    row = lax.broadcasted_iota(jnp.int32, v.shape, 0)
    keep = (row >= k) if k > 0 else (row < n_rows + k)
    return jnp.where(keep, rolled, 0.0)


def conv_fwd(zx, conv_w, conv_b, col0, *, name):
    S = zx.shape[0]
    C = conv_w.shape[1]
    tc = _pick(C, (256, 128))
    off = col0 // tc
    assert col0 % tc == 0

    def body(x_ref, w_ref, b_ref, o_ref):
        xv = x_ref[...]
        acc = b_ref[...] + w_ref[SSM_CONV - 1:SSM_CONV, :] * xv
        for k in range(SSM_CONV - 1):
            acc = acc + w_ref[k:k + 1, :] * _shift_rows(xv, SSM_CONV - 1 - k, S)
        o_ref[...] = acc * _sigmoid(acc)

    return pl.pallas_call(
        body, name=name, grid=(C // tc,),
        in_specs=[pl.BlockSpec((S, tc), lambda j: (0, off + j)), pl.BlockSpec((SSM_CONV, tc), lambda j: (0, j)),
                  pl.BlockSpec((1, tc), lambda j: (0, j))],
        out_specs=pl.BlockSpec((S, tc), lambda j: (0, j)), out_shape=jax.ShapeDtypeStruct((S, C), F32),
        compiler_params=_params(("parallel",)),
    )(zx, conv_w, conv_b)


def conv_bwd(zx, conv_w, conv_b, col0, douts, *, name):
    S = zx.shape[0]
    C = conv_w.shape[1]
    tc = LANES
    off = col0 // tc
    counts = [d.shape[1] // tc for d in douts]
    starts = [sum(counts[:p]) for p in range(len(douts))]
    assert sum(counts) * tc == C and all(d.shape[1] % tc == 0 for d in douts)

    def body(x_ref, w_ref, b_ref, *rest):
        do_refs, (dx_ref, dw_ref, db_ref) = rest[:len(douts)], rest[len(douts):]
        j = pl.program_id(0)
        dov = do_refs[-1][...]
        for p in reversed(range(len(douts) - 1)):
            dov = jnp.where(j < starts[p + 1], do_refs[p][...], dov)
        xv = x_ref[...]
        shifted = [_shift_rows(xv, SSM_CONV - 1 - k, S) for k in range(SSM_CONV)]
        acc = b_ref[...]
        for k in range(SSM_CONV):
            acc = acc + w_ref[k:k + 1, :] * shifted[k]
        s = _sigmoid(acc)
        dacc = dov * (s * (1.0 + acc * (1.0 - s)))
        db_ref[...] = jnp.sum(dacc, axis=0, keepdims=True)
        dx = jnp.zeros_like(xv)
        for k in range(SSM_CONV):
            dw_ref[k:k + 1, :] = jnp.sum(dacc * shifted[k], axis=0, keepdims=True)
            dx = dx + w_ref[k:k + 1, :] * _shift_rows(dacc, -(SSM_CONV - 1 - k), S)
        dx_ref[...] = dx.astype(BF16)

    slab = pl.BlockSpec((S, tc), lambda j: (0, j))
    piece_specs = [pl.BlockSpec((S, tc), lambda j, a=starts[p], n=counts[p]: (0, jnp.clip(j - a, 0, n - 1)))
                   for p in range(len(douts))]
    return pl.pallas_call(
        body, name=name, grid=(C // tc,),
        in_specs=[pl.BlockSpec((S, tc), lambda j: (0, off + j)), pl.BlockSpec((SSM_CONV, tc), lambda j: (0, j)),
                  pl.BlockSpec((1, tc), lambda j: (0, j))] + piece_specs,
        out_specs=[slab, pl.BlockSpec((SSM_CONV, tc), lambda j: (0, j)), pl.BlockSpec((1, tc), lambda j: (0, j))],
        out_shape=[jax.ShapeDtypeStruct((S, C), BF16), jax.ShapeDtypeStruct((SSM_CONV, C), F32),
                   jax.ShapeDtypeStruct((1, C), F32)],
        compiler_params=_params(("arbitrary",)),
    )(zx, conv_w, conv_b, *douts)


def _ssd_chunk_terms(dtraw, bias, a_log, tl):
    dt = _softplus(dtraw + bias)
    a_neg = -jnp.exp(a_log)
    ac = _dot_exact_left(tl, dt * a_neg)
    ac_last = ac[ac.shape[0] - 1:, :]
    return dt, a_neg, ac, ac.T, jnp.exp(ac), jnp.exp(ac_last - ac), jnp.exp(ac_last)


def _ssd_specs(S, L, G, hpg, pd, inner, gp):
    gw = hpg * pd
    n = SSM_STATE
    xb = inner // n
    assert G % gp == 0 and xb % gp == 0 and (xb + G) % gp == 0

    def mk(cidx):
        return dict(
            x=pl.BlockSpec((L, gp * gw), lambda g, c: (cidx(c), g)),
            b=pl.BlockSpec((L, gp * n), lambda g, c: (cidx(c), xb // gp + g)),
            c=pl.BlockSpec((L, gp * n), lambda g, c: (cidx(c), (xb + G) // gp + g)),
            z=pl.BlockSpec((L, gp * gw), lambda g, c: (cidx(c), g)),
            dt=pl.BlockSpec((L, gp * LANES), lambda g, c: (cidx(c), g)),
            gvec=pl.BlockSpec((gp, 1, LANES), lambda g, c: (g, 0, 0)),
            chan=pl.BlockSpec((1, gp * gw), lambda g, c: (0, g)),
            tri=pl.BlockSpec((L, L), lambda g, c: (0, 0)),
            hp=pl.BlockSpec((gp, 1, gw, n), lambda g, c: (g, cidx(c), 0, 0)),
            bc=pl.BlockSpec((L, gp * n), lambda g, c: (cidx(c), g)),
        )
    return mk


def _ssd_group_views(refs, kinds, gg, gw):
    n = SSM_STATE
    width = dict(x=gw, z=gw, chan=gw, b=n, c=n, bc=n, dt=LANES)
    out = []
    for ref, kind in zip(refs, kinds):
        if kind in width:
            out.append(ref.at[:, gg * width[kind]:(gg + 1) * width[kind]])
        elif kind in ("gvec", "hp"):
            out.append(ref.at[gg:gg + 1])
        elif kind == "state":
            out.append(ref.at[gg * gw:(gg + 1) * gw])
        else:
            out.append(ref)
    return out


def ssd_fwd(xbc, zx, dtg, bias_g, alog_g, d_chan, ngain, L, G, *, name):
    S = xbc.shape[0]
    n = SSM_STATE
    inner = xbc.shape[1] - 2 * G * n
    gw = inner // G
    pd = SB_HEAD_DIM
    hpg = gw // pd
    nc = S // L
    gp = SSD_FWD_GROUPS_PER_STEP
    sp = _ssd_specs(S, L, G, hpg, pd, inner, gp)(lambda c: c)

    kinds = ("x", "b", "c", "z", "dt", "gvec", "gvec", "chan", "chan", "tri", "x", "x", "hp", "state")

    def body(*refs):
        @pl.when(pl.program_id(1) == 0)
        def _():
            refs[-1][...] = jnp.zeros_like(refs[-1])

        for gg in range(gp):
            group_body(*_ssd_group_views(refs, kinds, gg, gw))

    def group_body(x_ref, b_ref, c_ref, z_ref, dt_ref, bias_ref, alog_ref, d_ref, ng_ref, tl_ref,
                   yn_ref, y_ref, hp_ref, state):
        dt, _, ac, act, ea, dte, cd = _ssd_chunk_terms(dt_ref[...], bias_ref[0], alog_ref[0], tl_ref[...])
        xv = x_ref[...]
        bm = b_ref[...].astype(BF16)
        cm = c_ref[...].astype(BF16)
        cb = _dot(cm, bm, 1, 1)
        tril = lax.broadcasted_iota(jnp.int32, (L, L), 1) <= lax.broadcasted_iota(jnp.int32, (L, L), 0)
        hp_ref[0, 0] = state[...]
        for r in range(hpg):
            ps = slice(r * pd, (r + 1) * pd)
            xr = xv[:, ps]
            xdt = xr * dt[:, r:r + 1]
            lm = jnp.exp(jnp.where(tril, ac[:, r:r + 1] - act[r:r + 1, :], -jnp.inf))
            hprev = state[ps, :]
            y = _dot((cb * lm).astype(BF16), xdt.astype(BF16))
            y = y + _dot(cm, hprev.astype(BF16), 1, 1) * ea[:, r:r + 1]
            y_ref[:, ps] = y + xr * d_ref[:, ps]
            st = _dot((xdt * dte[:, r:r + 1]).astype(BF16), bm, 0, 0)
            state[ps, :] = hprev * cd[:, r:r + 1] + st
        yfull = y_ref[...]
        zg = z_ref[...]
        yg = yfull * (zg * _sigmoid(zg))
        yn_ref[...] = (yg * lax.rsqrt(jnp.mean(yg * yg, axis=1, keepdims=True) + EPS) * ng_ref[...]).astype(BF16)

    return pl.pallas_call(
        body, name=name, grid=(G // gp, nc),
        in_specs=[sp["x"], sp["b"], sp["c"], sp["z"], sp["dt"], sp["gvec"], sp["gvec"], sp["chan"], sp["chan"], sp["tri"]],
        out_specs=[sp["x"], sp["x"], sp["hp"]],
        out_shape=[jax.ShapeDtypeStruct((S, inner), BF16), jax.ShapeDtypeStruct((S, inner), F32),
                   jax.ShapeDtypeStruct((G, nc, gw, n), F32)],
        scratch_shapes=[pltpu.VMEM((gp * gw, n), F32)],
        compiler_params=_params(("arbitrary", "arbitrary")),
    )(xbc, xbc, xbc, zx, dtg, bias_g, alog_g, d_chan, ngain, _tri(L, "row_ge_col"))


def ssd_bwd(xbc, zx, dtg, bias_g, alog_g, d_chan, ngain, yfull, hp, dyn, L, G, *, name):
    S = xbc.shape[0]
    n = SSM_STATE
    inner = xbc.shape[1] - 2 * G * n
    gw = inner // G
    pd = SB_HEAD_DIM
    hpg = gw // pd
    nc = S // L
    gp = SSD_BWD_GROUPS_PER_STEP
    sp = _ssd_specs(S, L, G, hpg, pd, inner, gp)(lambda c: nc - 1 - c)

    kinds = ("x", "b", "c", "z", "dt", "gvec", "gvec", "chan", "chan", "tri", "tri", "x", "hp", "x",
             "x", "x", "bc", "bc", "dt", "gvec", "gvec", "gvec", "chan", "state")

    def body(*refs):
        @pl.when(pl.program_id(1) == 0)
        def _():
            for acc in refs[-5:]:
                acc[...] = jnp.zeros_like(acc)

        for gg in range(gp):
            group_body(*_ssd_group_views(refs, kinds, gg, gw))

    def group_body(x_ref, b_ref, c_ref, z_ref, dt_ref, bias_ref, alog_ref, d_ref, ng_ref, tl_ref, tu_ref,
                   yf_ref, hp_ref, dyn_ref,
                   dz_ref, dx_ref, db_ref, dc_ref, ddt_ref, dbias_ref, dalog_ref, dd_ref, dng_ref, dstate):

        dtraw = dt_ref[...]
        dt, a_neg, ac, act, ea, dte, cd = _ssd_chunk_terms(dtraw, bias_ref[0], alog_ref[0], tl_ref[...])
        xv = x_ref[...]
        bm = b_ref[...].astype(BF16)
        cm = c_ref[...].astype(BF16)
        cb = _dot(cm, bm, 1, 1)
        tril = lax.broadcasted_iota(jnp.int32, (L, L), 1) <= lax.broadcasted_iota(jnp.int32, (L, L), 0)
        lane = lax.broadcasted_iota(jnp.int32, (L, LANES), 1)
        lane1 = lax.broadcasted_iota(jnp.int32, (1, LANES), 1)

        yfull = yf_ref[...]
        zg = z_ref[...]
        sg = _sigmoid(zg)
        gate = zg * sg
        yg = yfull * gate
        rr = lax.rsqrt(jnp.mean(yg * yg, axis=1, keepdims=True) + EPS)
        yhat = yg * rr
        dynv = dyn_ref[...]
        dng_ref[...] += jnp.sum(dynv * yhat, axis=0, keepdims=True)
        t = dynv * ng_ref[...]
        dyg = rr * (t - yhat * jnp.mean(yhat * t, axis=1, keepdims=True))
        dy = dyg * gate
        dz_ref[...] = (dyg * yfull * (sg * (1.0 + zg * (1.0 - sg)))).astype(BF16)

        dcb = jnp.zeros((L, L), F32)
        dc_acc = jnp.zeros((L, n), F32)
        db_acc = jnp.zeros((L, n), F32)
        dac = jnp.zeros((L, LANES), F32)
        xdx = jnp.zeros((L, LANES), F32)
        tail = jnp.zeros((1, LANES), F32)
        dskip = jnp.zeros((1, LANES), F32)
        ones_l = jnp.ones((L, LANES), BF16)
        for r in range(hpg):
            ps = slice(r * pd, (r + 1) * pd)
            xr = xv[:, ps]
            dyr = dy[:, ps]
            dtr = dt[:, r:r + 1]
            dter = dte[:, r:r + 1]
            cdr = cd[:, r:r + 1]
            xdt = xr * dtr
            xdtb = xdt.astype(BF16)
            dyrb = dyr.astype(BF16)
            lm = jnp.exp(jnp.where(tril, ac[:, r:r + 1] - act[r:r + 1, :], -jnp.inf))
            m32 = cb * lm
            mb = m32.astype(BF16)
            hprev = hp_ref[0, 0, ps, :]
            hpb = hprev.astype(BF16)
            dhn = dstate[ps, :]
            dhnb = dhn.astype(BF16)
            ear = ea[:, r:r + 1]
            gy = (dyr * ear).astype(BF16)
            dc_acc = dc_acc + _dot(gy, hpb)
            dstate[ps, :] = _dot(gy, cm, 0, 0) + dhn * cdr
            bdh = _dot(bm, dhnb, 1, 1)
            db_acc = db_acc + _dot((xdt * dter).astype(BF16), dhnb)
            dm = _dot(dyrb, xdtb, 1, 1)
            dxdt = bdh * dter + _dot(mb, dyrb, 0, 0)
            dcb = dcb + dm * lm
            wmat = dm * m32
            whi = wmat.astype(BF16)
            wlo = (wmat - whi.astype(F32)).astype(BF16)
            col_w = _dot(whi, ones_l, 0, 0) + _dot(wlo, ones_l, 0, 0)
            t_end = xdt * bdh * dter
            e_r = jnp.sum(wmat, axis=1, keepdims=True) \
                + jnp.sum(dyr * _dot(cm, hpb, 1, 1) * ear - t_end, axis=1, keepdims=True)
            c_r = cdr * _sum_all(dhn * hprev) + _sum_all(t_end)
            dac = dac + jnp.where(lane == r, e_r - col_w, 0.0)
            xdx = xdx + jnp.where(lane == r, jnp.sum(dxdt * xr, axis=1, keepdims=True), 0.0)
            tail = tail + jnp.where(lane1 == r, c_r, 0.0)
            dskip = dskip + jnp.where(lane1 == r, _sum_all(dyr * xr), 0.0)
            dx_ref[:, ps] = dxdt * dtr + dyr * d_ref[:, ps]
        dcbb = dcb.astype(BF16)
        dc_ref[...] = dc_acc + _dot(dcbb, bm)
        db_ref[...] = db_acc + _dot(dcbb, cm, 0, 0)
        da = _dot_exact_left(tu_ref[...], dac) + tail
        real = lane < hpg
        ddt = jnp.where(real, (da * a_neg + xdx) * _sigmoid(dtraw + bias_ref[0]), 0.0)
        ddt_ref[...] = ddt
        dd_ref[0] += dskip
        dbias_ref[0] += jnp.sum(ddt, axis=0, keepdims=True)
        dalog_ref[0] += jnp.where(lane1 < hpg, jnp.sum(da * dt, axis=0, keepdims=True) * a_neg, 0.0)

    return pl.pallas_call(
        body, name=name, grid=(G // gp, nc),
        in_specs=[sp["x"], sp["b"], sp["c"], sp["z"], sp["dt"], sp["gvec"], sp["gvec"], sp["chan"], sp["chan"],
                  sp["tri"], sp["tri"], sp["x"], sp["hp"], sp["x"]],
        out_specs=[sp["x"], sp["x"], sp["bc"], sp["bc"], sp["dt"], sp["gvec"], sp["gvec"], sp["gvec"], sp["chan"]],
        out_shape=[jax.ShapeDtypeStruct((S, inner), BF16), jax.ShapeDtypeStruct((S, inner), F32),
                   jax.ShapeDtypeStruct((S, G * n), F32), jax.ShapeDtypeStruct((S, G * n), F32),
                   jax.ShapeDtypeStruct((S, G * LANES), F32), jax.ShapeDtypeStruct((G, 1, LANES), F32),
                   jax.ShapeDtypeStruct((G, 1, LANES), F32), jax.ShapeDtypeStruct((G, 1, LANES), F32),
                   jax.ShapeDtypeStruct((1, inner), F32)],
        scratch_shapes=[pltpu.VMEM((gp * gw, n), F32)],
        compiler_params=_params(("arbitrary", "arbitrary")),
    )(xbc, xbc, xbc, zx, dtg, bias_g, alog_g, d_chan, ngain, _tri(L, "row_ge_col"), _tri(L, "row_le_col"),
      yfull, hp, dyn)


def _spread_dt(w_dt_t, G, hpg):
    K = w_dt_t.shape[1]
    w = w_dt_t.reshape(G, hpg, K)
    return jnp.pad(w, ((0, 0), (0, LANES - hpg), (0, 0))).reshape(G * LANES, K)


def _group_vec(v, G, hpg):
    return jnp.pad(v.reshape(G, 1, hpg), ((0, 0), (0, 0), (0, LANES - hpg)))


def local_step(x, target, W, late=None):
    S, D = x.shape
    depth = W["mix_norm"].shape[0]
    gm_groups, gm_chunk = W["gm_w_s"].shape[1], W["gm_w_s"].shape[2]
    heads = W["ssm_dt_bias"].shape[1]
    inner = heads * SB_HEAD_DIM
    L = gm_chunk
    received = None

    saved = []
    for i in range(depth):
        kind, j = i % 3, i // 3
        s = dict(x=x)
        if i == 0:
            h = rms_fwd(x, W["mix_norm"][i:i + 1], name="rms_mix_fwd")
        s["h"] = h
        ffn_gain = W["ffn_norm"][i:i + 1]
        if kind == 0:
            qkv = matmul(h, W["sb_w_qkv"][j], tb=True, name="mm_qkv")
            if late is not None and i == 0:
                o, gathered = sb_attn_fwd(qkv, W["sb_q_gain"][j:j + 1], W["sb_k_gain"][j:j + 1], name="sb_fwd_gather",
                                          gather=late.shards)
                late.fill(W, gathered)
            else:
                o = sb_attn_fwd(qkv, W["sb_q_gain"][j:j + 1], W["sb_k_gain"][j:j + 1], name="sb_fwd")
            x1, h2 = matmul(o, W["sb_w_o"][j], residual=x, norm_fwd=ffn_gain, name="mm_sb_out")
            s.update(qkv=qkv, o=o)
        elif kind == 1:
            wc = jnp.where(jnp.tril(jnp.ones((gm_chunk, gm_chunk), bool)), W["gm_w_s"][j], 0.0).astype(BF16)
            bsf = jnp.broadcast_to(W["gm_b_s"][j][:, :, None], (gm_groups, gm_chunk, W["gm_v_gain"].shape[1] // gm_groups)).astype(F32)
            zzpre = matmul(h, W["gm_w_in"][j], tb=True, name="mm_gm_in")
            p = gmlp_fwd(zzpre, W["gm_b_in"][j:j + 1], W["gm_v_gain"][j:j + 1], wc, bsf, name="gm_fwd")
            x1, h2 = matmul(p, W["gm_w_out"][j], residual=x, norm_fwd=ffn_gain, name="mm_gm_out")
            s.update(zzpre=zzpre, p=p, wc=wc, bsf=bsf)
        else:
            conv_dim = W["ssm_conv_w"].shape[2]
            G = (conv_dim - inner) // (2 * SSM_STATE)
            hpg = heads // G
            w_in = W["ssm_w_in"][j]
            w_zx = w_in[:inner + conv_dim]
            w_dtg = _spread_dt(w_in[inner + conv_dim:], G, hpg)
            bias_g = _group_vec(W["ssm_dt_bias"][j], G, hpg)
            alog_g = _group_vec(W["ssm_a_log"][j], G, hpg)
            d_chan = jnp.repeat(W["ssm_d"][j], SB_HEAD_DIM)[None, :]
            ngain = W["ssm_norm_gain"][j:j + 1]
            zx = matmul(h, w_zx, tb=True, name="mm_ssm_zx")
            dtg = matmul(h, w_dtg, tb=True, name="mm_ssm_dt")
            xbc = conv_fwd(zx, W["ssm_conv_w"][j], W["ssm_conv_b"][j:j + 1], inner, name="conv_fwd")
            yn, yfull, hp = ssd_fwd(xbc, zx, dtg, bias_g, alog_g, d_chan, ngain, L, G, name="ssd_fwd")
            x1, h2 = matmul(yn, W["ssm_w_out"][j], residual=x, norm_fwd=ffn_gain, name="mm_ssm_out")
            s.update(w_zx=w_zx, w_dtg=w_dtg, bias_g=bias_g, alog_g=alog_g, d_chan=d_chan, ngain=ngain,
                     zx=zx, dtg=dtg, xbc=xbc, yn=yn, yfull=yfull, hp=hp)
        gu, a = ffn_up_fwd(h2, W["ffn_w_gu"][i], name="ffn_up_fwd")
        if i + 1 < depth:
            x2, h = matmul(a, W["ffn_w_down"][i], residual=x1, norm_fwd=W["mix_norm"][i + 1:i + 2], name="mm_ffn_down")
        else:
            x2, loss = matmul(a, W["ffn_w_down"][i], residual=x1, loss_target=target, name="mm_ffn_down_loss")
        s.update(x1=x1, h2=h2, gu=gu, a=a)
        saved.append(s)
        x = x2

    dx = x

    gw = {k: {} for k in WEIGHTS}
    for i in reversed(range(depth)):
        kind, j = i % 3, i // 3
        s = saved[i]
        gw["ffn_w_down"][i] = matmul(s["a"], dx, ta=True, out_dtype=BF16, name="mm_ffn_dwdown")
        dgu = ffn_up_bwd(dx, W["ffn_w_down"][i], s["gu"], name="ffn_up_bwd")
        dx1, dgn = matmul(dgu, W["ffn_w_gu"][i], a_split=2, norm_bwd=(s["x1"], W["ffn_norm"][i:i + 1], dx),
                          name="mm_ffn_dh")
        gw["ffn_w_gu"][i] = matmul(dgu, s["h2"], ta=True, a_split=2, out_dtype=BF16, name="mm_ffn_dwgu")
        gw["ffn_norm"][i] = dgn[0]
        mix_norm_bwd = (s["x"], W["mix_norm"][i:i + 1], dx1)
        if kind == 0:
            do = matmul(dx1, W["sb_w_o"][j], tb=True, out_dtype=BF16, name="mm_sb_do")
            gw["sb_w_o"][j] = matmul(s["o"], dx1, ta=True, out_dtype=BF16, name="mm_sb_dwo")
            if late is not None and i == 0:
                dqkv, dqg, dkg, received = sb_attn_bwd(
                    s["qkv"], s["o"], do, W["sb_q_gain"][j:j + 1], W["sb_k_gain"][j:j + 1], name="sb_bwd_scatter",
                    scatter=late.contributions(gw))
            else:
                dqkv, dqg, dkg = sb_attn_bwd(s["qkv"], s["o"], do, W["sb_q_gain"][j:j + 1], W["sb_k_gain"][j:j + 1],
                                             name="sb_bwd")
            gw["sb_q_gain"][j] = dqg[0]
            gw["sb_k_gain"][j] = dkg[0]
            dx, dgn = matmul(dqkv, W["sb_w_qkv"][j], a_split=3, norm_bwd=mix_norm_bwd, name="mm_sb_dh")
            gw["sb_w_qkv"][j] = matmul(dqkv, s["h"], ta=True, a_split=3, out_dtype=BF16, name="mm_sb_dwqkv")
        elif kind == 1:
            dp = matmul(dx1, W["gm_w_out"][j], tb=True, name="mm_gm_dp")
            gw["gm_w_out"][j] = matmul(s["p"], dx1, ta=True, out_dtype=BF16, name="mm_gm_dwout")
            dzz, db_in, dvg, dws, dbs = gmlp_bwd(s["zzpre"], W["gm_b_in"][j:j + 1], W["gm_v_gain"][j:j + 1],
                                                s["wc"], s["bsf"], dp, name="gm_bwd")
            gw["gm_b_in"][j] = db_in[0]
            gw["gm_v_gain"][j] = dvg[0]
            gw["gm_w_s"][j] = dws
            gw["gm_b_s"][j] = dbs[:, :gm_groups].T
            dx, dgn = matmul(dzz, W["gm_w_in"][j], norm_bwd=mix_norm_bwd, name="mm_gm_dh")
            gw["gm_w_in"][j] = matmul(dzz, s["h"], ta=True, out_dtype=BF16, name="mm_gm_dwin")
        else:
            conv_dim = W["ssm_conv_w"].shape[2]
            G = (conv_dim - inner) // (2 * SSM_STATE)
            hpg = heads // G
            dyn = matmul(dx1, W["ssm_w_out"][j], tb=True, name="mm_ssm_dyn")
            gw["ssm_w_out"][j] = matmul(s["yn"], dx1, ta=True, out_dtype=BF16, name="mm_ssm_dwout")
            dz, dxs, dbm, dcm, ddt, dbias, dalog, dd, dng = ssd_bwd(
                s["xbc"], s["zx"], s["dtg"], s["bias_g"], s["alog_g"], s["d_chan"], s["ngain"], s["yfull"], s["hp"],
                dyn, L, G, name="ssd_bwd")
            dpre, dcw, dcb = conv_bwd(s["zx"], W["ssm_conv_w"][j], W["ssm_conv_b"][j:j + 1], inner, [dxs, dbm, dcm],
                                      name="conv_bwd")
            dzx = jnp.concatenate([dz, dpre], axis=1)
            dh = matmul(ddt, s["w_dtg"], name="mm_ssm_dh_dt")
            dx, dgn = matmul(dzx, s["w_zx"], residual=dh, norm_bwd=mix_norm_bwd, name="mm_ssm_dh")
            dw_zx = matmul(dzx, s["h"], ta=True, out_dtype=BF16, name="mm_ssm_dwzx")
            dw_dtg = matmul(ddt, s["h"], ta=True, out_dtype=BF16, name="mm_ssm_dwdt")
            dw_dt = dw_dtg.reshape(G, LANES, D)[:, :hpg, :].reshape(heads, D)
            gw["ssm_w_in"][j] = jnp.concatenate([dw_zx, dw_dt], axis=0)
            gw["ssm_conv_w"][j] = dcw
            gw["ssm_conv_b"][j] = dcb[0]
            gw["ssm_dt_bias"][j] = dbias[:, 0, :hpg].reshape(heads)
            gw["ssm_a_log"][j] = dalog[:, 0, :hpg].reshape(heads)
            gw["ssm_d"][j] = dd[:, 0, :hpg].reshape(heads)
            gw["ssm_norm_gain"][j] = dng[0]
        gw["mix_norm"][i] = dgn[0]

    return loss, dx, gw, received


MESH = pl.DeviceIdType.MESH
HBM_SPEC = pl.BlockSpec(memory_space=pltpu.HBM)
VMEM_SPEC = pl.BlockSpec(memory_space=pltpu.VMEM)


def _my_position():
    return lax.axis_index("x"), lax.axis_index("y"), lax.axis_index("c")


def _flip(v, bit):
    return 1 - v if bit else v


def all_gather(shards, *, name):
    n = len(shards)

    def body(*refs):
        for phase in ("start", "forward", "finish"):
            _ag_phase(refs[:n], refs[n:2 * n], refs[2 * n:], phase)

    return pl.pallas_call(
        body, name=name, out_shape=[jax.ShapeDtypeStruct((N_DEV,) + s.shape, s.dtype) for s in shards],
        in_specs=[HBM_SPEC] * n, out_specs=[HBM_SPEC] * n, scratch_shapes=_copy_semaphores(n),
    )(*shards)


def _ag_phase(x_refs, out_refs, sems, phase):
    send_sems, recv_sems, local_sems = sems
    x, y, c = _my_position()
    me, sibling = (x, y, c), (x, y, 1 - c)
    chips = [(1 - x, y), (x, 1 - y), (1 - x, 1 - y)]
    for p, (x_ref, out_ref) in enumerate(zip(x_refs, out_refs)):
        def slot(px, py, pc):
            return out_ref.at[4 * px + 2 * py + pc]

        def copy(k, block, to, src=None):
            return pltpu.make_async_remote_copy(
                src_ref=slot(*block) if src is None else src, dst_ref=slot(*block),
                send_sem=send_sems.at[7 * p + k], recv_sem=recv_sems.at[7 * p + k], device_id=to, device_id_type=MESH)

        mine = pltpu.make_async_copy(x_ref, slot(*me), local_sems.at[p])
        first = [copy(0, me, sibling, src=x_ref)]
        first += [copy(1 + j, me, (*chip, c), src=x_ref) for j, chip in enumerate(chips)]
        passed = [copy(4 + j, (*chip, c), sibling) for j, chip in enumerate(chips)]
        if phase == "start":
            mine.start()
            for cp in first:
                cp.start()
        elif phase == "forward":
            for j, chip in enumerate(chips):
                copy(1 + j, (*chip, c), me).wait_recv()
                passed[j].start()
        else:
            copy(0, sibling, me).wait_recv()
            for j, chip in enumerate(chips):
                copy(4 + j, (*chip, 1 - c), me).wait_recv()
            for cp in first + passed:
                cp.wait_send()
            mine.wait()


def _copy_semaphores(n):
    return [pltpu.SemaphoreType.DMA((7 * n,)), pltpu.SemaphoreType.DMA((7 * n,)), pltpu.SemaphoreType.DMA((n,))]


def _rs_phase(g_refs, out_refs, sems, phase):
    send_sems, recv_sems, local_sems = sems
    x, y, c = _my_position()
    me = 4 * x + 2 * y + c
    copies = []
    for p, (g_ref, out_ref) in enumerate(zip(g_refs, out_refs)):
        copies.append(pltpu.make_async_copy(g_ref.at[me], out_ref.at[me], local_sems.at[p]))
        for k in range(1, N_DEV):
            px, py, pc = _flip(x, k & 4), _flip(y, k & 2), _flip(c, k & 1)
            copies.append(pltpu.make_async_remote_copy(
                src_ref=g_ref.at[4 * px + 2 * py + pc], dst_ref=out_ref.at[me],
                send_sem=send_sems.at[7 * p + k - 1], recv_sem=recv_sems.at[7 * p + k - 1],
                device_id=(px, py, pc), device_id_type=MESH))
    for cp in copies:
        if phase == "start":
            cp.start()
        else:
            cp.wait()


def sum_slots(recvs, *, name):
    n, R, C = recvs[0].shape
    nl = len(recvs)
    tr = _pick(R, (512, 256, 128, 64, 32, 16))

    def body(*refs):
        o_ref = refs[nl]
        for l in range(nl):
            @pl.when(pl.program_id(0) == l)
            def _(r_ref=refs[l]):
                acc = r_ref[0].astype(F32)
                for s in range(1, n):
                    acc = acc + r_ref[s].astype(F32)
                o_ref[...] = acc

    return pl.pallas_call(
        body, name=name, grid=(nl, R // tr),
        in_specs=[pl.BlockSpec((n, tr, C), lambda l, i, m=m: (0, jnp.where(l == m, i, 0), 0)) for m in range(nl)],
        out_specs=pl.BlockSpec((None, tr, C), lambda l, i: (l, i, 0)), out_shape=jax.ShapeDtypeStruct((nl, R, C), F32),
        compiler_params=_params(("arbitrary", "arbitrary")),
    )(*recvs)


def all_reduce_small(vs, scatter, *, name):
    n, ns = len(vs), len(scatter)

    def body(*refs):
        v_refs, g_refs = refs[:n], refs[n:n + ns]
        o_refs, r_refs = refs[n + ns:2 * n + ns], refs[2 * n + ns:2 * n + 2 * ns]
        bufs = refs[2 * n + 2 * ns:3 * n + 2 * ns]
        send_sems, recv_sems = refs[3 * n + 2 * ns:3 * n + 2 * ns + 2]
        rs_sems = refs[3 * n + 2 * ns + 2:]
        x, y, c = _my_position()
        me = 4 * x + 2 * y + c
        _rs_phase(g_refs, r_refs, rs_sems, "start")
        copies = []
        for p, (v_ref, buf) in enumerate(zip(v_refs, bufs)):
            buf[me] = v_ref[...]
            for k in range(1, N_DEV):
                px, py, pc = _flip(x, k & 4), _flip(y, k & 2), _flip(c, k & 1)
                copies.append(pltpu.make_async_remote_copy(
                    src_ref=v_ref, dst_ref=buf.at[me], send_sem=send_sems.at[7 * p + k - 1],
                    recv_sem=recv_sems.at[7 * p + k - 1], device_id=(px, py, pc), device_id_type=MESH))
        for cp in copies:
            cp.start()
        for cp in copies:
            cp.wait()
        for o_ref, buf in zip(o_refs, bufs):
            acc = buf[0]
            for s in range(1, N_DEV):
                acc = acc + buf[s]
            o_ref[...] = acc
        _rs_phase(g_refs, r_refs, rs_sems, "finish")

    out = pl.pallas_call(
        body, name=name,
        out_shape=[jax.ShapeDtypeStruct(v.shape, F32) for v in vs] + [jax.ShapeDtypeStruct(g.shape, g.dtype) for g in scatter],
        in_specs=[VMEM_SPEC] * n + [HBM_SPEC] * ns, out_specs=[VMEM_SPEC] * n + [HBM_SPEC] * ns,
        scratch_shapes=[pltpu.VMEM((N_DEV,) + v.shape, F32) for v in vs]
        + [pltpu.SemaphoreType.DMA((7 * n,)), pltpu.SemaphoreType.DMA((7 * n,))] + _copy_semaphores(ns),
        compiler_params=pltpu.CompilerParams(vmem_limit_bytes=VMEM_LIMIT_BYTES),
    )(*vs, *scatter)
    return list(out[:n]), list(out[n:])


def _pad_rows(a, mult):
    pad = (-a.shape[0]) % mult
    return jnp.pad(a, ((0, pad), (0, 0))) if pad else a


def _pack_small(arrays):
    flat = []
    for a in arrays:
        f = a.reshape(-1).astype(F32)
        flat.append(jnp.pad(f, (0, (-f.shape[0]) % LANES)))
    return _pad_rows(jnp.concatenate(flat).reshape(-1, LANES), 8)


def _unpack_small(packed, shapes):
    flat = packed.reshape(-1)
    out, r = [], 0
    for shp in shapes:
        n = math.prod(shp)
        out.append(flat[r:r + n].reshape(shp))
        r += n + (-n) % LANES
    return out


ARG_NAMES = ("x",) + WEIGHTS + ("loss_target",) + tuple("m_" + w for w in WEIGHTS) + tuple("v_" + w for w in WEIGHTS)


def kernel(x, mix_norm, ffn_norm, sb_w_qkv, sb_q_gain, sb_k_gain, sb_w_o, gm_w_in, gm_b_in, gm_v_gain, gm_w_s, gm_b_s, gm_w_out, ssm_w_in, ssm_conv_w, ssm_conv_b, ssm_dt_bias, ssm_a_log, ssm_d, ssm_norm_gain, ssm_w_out, ffn_w_gu, ffn_w_down, loss_target, m_mix_norm, m_ffn_norm, m_sb_w_qkv, m_sb_q_gain, m_sb_k_gain, m_sb_w_o, m_gm_w_in, m_gm_b_in, m_gm_v_gain, m_gm_w_s, m_gm_b_s, m_gm_w_out, m_ssm_w_in, m_ssm_conv_w, m_ssm_conv_b, m_ssm_dt_bias, m_ssm_a_log, m_ssm_d, m_ssm_norm_gain, m_ssm_w_out, m_ffn_w_gu, m_ffn_w_down, v_mix_norm, v_ffn_norm, v_sb_w_qkv, v_sb_q_gain, v_sb_k_gain, v_sb_w_o, v_gm_w_in, v_gm_b_in, v_gm_v_gain, v_gm_w_s, v_gm_b_s, v_gm_w_out, v_ssm_w_in, v_ssm_conv_w, v_ssm_conv_b, v_ssm_dt_bias, v_ssm_a_log, v_ssm_d, v_ssm_norm_gain, v_ssm_w_out, v_ffn_w_gu, v_ffn_w_down):
    given = dict(zip(ARG_NAMES, (x, mix_norm, ffn_norm, sb_w_qkv, sb_q_gain, sb_k_gain, sb_w_o, gm_w_in, gm_b_in, gm_v_gain, gm_w_s, gm_b_s, gm_w_out, ssm_w_in, ssm_conv_w, ssm_conv_b, ssm_dt_bias, ssm_a_log, ssm_d, ssm_norm_gain, ssm_w_out, ffn_w_gu, ffn_w_down, loss_target, m_mix_norm, m_ffn_norm, m_sb_w_qkv, m_sb_q_gain, m_sb_k_gain, m_sb_w_o, m_gm_w_in, m_gm_b_in, m_gm_v_gain, m_gm_w_s, m_gm_b_s, m_gm_w_out, m_ssm_w_in, m_ssm_conv_w, m_ssm_conv_b, m_ssm_dt_bias, m_ssm_a_log, m_ssm_d, m_ssm_norm_gain, m_ssm_w_out, m_ffn_w_gu, m_ffn_w_down, v_mix_norm, v_ffn_norm, v_sb_w_qkv, v_sb_q_gain, v_sb_k_gain, v_sb_w_o, v_gm_w_in, v_gm_b_in, v_gm_v_gain, v_gm_w_s, v_gm_b_s, v_gm_w_out, v_ssm_w_in, v_ssm_conv_w, v_ssm_conv_b, v_ssm_dt_bias, v_ssm_a_log, v_ssm_d, v_ssm_norm_gain, v_ssm_w_out, v_ffn_w_gu, v_ffn_w_down)))
    mx, my, mc = _my_position()
    me = 4 * mx + 2 * my + mc

    pieces = [(k, l) for k in BIG for l in range(given[k].shape[0])]
    early = [("sb_w_qkv", 0)]
    late_pieces = [p for p in pieces if p not in early]
    last = [("sb_w_qkv", 0)]
    main = [p for p in pieces if p not in last]

    def shards_of(ps):
        return [(given[k][l].T if k in COL_SHARDED else given[k][l]).astype(BF16) for k, l in ps]

    def piece_to_full(g, k):
        rows, cols = given[k].shape[1:]
        return g.reshape(N_DEV * cols, rows) if k in COL_SHARDED else g.reshape(N_DEV * rows, cols)

    def full_to_piece(full, k):
        return full.reshape(N_DEV, -1, PACK_COLS)

    def summed_to_shard(g, k):
        layers, rows, cols = given[k].shape
        return g.reshape(layers, cols, rows) if k in COL_SHARDED else g.reshape(layers, rows, cols)

    def contributions(gw, ps):
        return [full_to_piece(gw[k][l], k) for k, l in ps]

    sharded_small = [lax.bitcast_convert_type(given[k], BF16) for k in SMALL_SHARDED]
    tail = jnp.concatenate([a.reshape(-1) for a in sharded_small])
    tail = jnp.pad(tail, (0, (-tail.size) % PACK_COLS)).reshape(-1, PACK_COLS)

    W = {k: given[k] for k in SMALL if k not in SMALL_SHARDED}
    W.update({k: [None] * given[k].shape[0] for k in BIG})
    for (k, l), g in zip(early, all_gather(shards_of(early), name="all_gather_early")):
        W[k][l] = piece_to_full(g, k)

    class Late:
        shards = shards_of(late_pieces) + [tail]

        @staticmethod
        def fill(weights, gathered):
            for (k, l), g in zip(late_pieces, gathered):
                weights[k][l] = piece_to_full(g, k)
            tail_g = gathered[-1].reshape(N_DEV, -1)
            off = 0
            for k, a in zip(SMALL_SHARDED, sharded_small):
                g = lax.bitcast_convert_type(tail_g[:, off:off + a.size].reshape((N_DEV,) + a.shape), F32)
                weights[k] = jnp.moveaxis(g, 0, -2).reshape(g.shape[1:-1] + (N_DEV * g.shape[-1],))
                off += a.size

        @staticmethod
        def contributions(gw):
            return contributions(gw, main)

    loss, gx, gw, received_main = local_step(given["x"][0], given["loss_target"][0], W, late=Late)

    grads_small = {k: jnp.stack([gw[k][l] for l in sorted(gw[k])], axis=0) for k in SMALL}
    packed_names = tuple(k for k in SMALL if k != "gm_w_s")
    small_shapes = [grads_small[k].shape for k in packed_names] + [(1, 1)]
    (red_ws, red_rest), received_last = all_reduce_small(
        [grads_small["gm_w_s"].reshape(-1, LANES), _pack_small([grads_small[k] for k in packed_names] + [loss])],
        contributions(gw, last), name="all_reduce_small_and_last_exchange")
    small_full = dict(zip(packed_names + ("loss",), _unpack_small(red_rest, small_shapes)))
    small_full["gm_w_s"] = red_ws.reshape(grads_small["gm_w_s"].shape)

    recv_piece = dict(zip(main + last, list(received_main) + list(received_last)))
    out_g, out_d, out_m, out_v = {}, {}, {}, {}
    for k in BIG:
        swap = lambda a: jnp.swapaxes(a, -1, -2)
        keep_t = k in COL_SHARDED and given[k].shape[-1] % LANES != 0
        flip = swap if keep_t else (lambda a: a)
        g = summed_to_shard(sum_slots([recv_piece[(k, l)] for l in range(given[k].shape[0])],
                                      name="reduce_scatter_sum_" + k), k)
        if k in COL_SHARDED and not keep_t:
            g = swap(g)
        res = adamw(flip(given[k]), g, flip(given["m_" + k]), flip(given["v_" + k]), name="adamw_" + k)
        out_g[k], out_d[k], out_m[k], out_v[k] = (flip(a) for a in (g,) + tuple(res))

    gsmall = {}
    for k in SMALL:
        g = small_full[k]
        if k in SMALL_SHARDED:
            n = given[k].shape[-1]
            g = lax.dynamic_slice_in_dim(g, me * n, n, axis=g.ndim - 1)
        gsmall[k] = g
    local_shapes = [given[k].shape for k in packed_names]
    dsm, nmsm, nvsm = adamw(*[_pack_small([src[k] for k in packed_names]) for src in (
        given, gsmall, {k: given["m_" + k] for k in packed_names}, {k: given["v_" + k] for k in packed_names})],
        name="adamw_small")
    out_g.update(gsmall)
    for dst, src in ((out_d, dsm), (out_m, nmsm), (out_v, nvsm)):
        dst.update(zip(packed_names, _unpack_small(src, local_shapes)))
    ws_shape = given["gm_w_s"].shape
    out_d["gm_w_s"], out_m["gm_w_s"], out_v["gm_w_s"] = (a.reshape(ws_shape) for a in adamw(
        *[a.reshape((-1,) + ws_shape[-2:]) for a in (given["gm_w_s"], gsmall["gm_w_s"], given["m_gm_w_s"], given["v_gm_w_s"])],
        name="adamw_gm_w_s"))

    return (small_full["loss"].reshape(()), gx[None],
            *[out_g[k] for k in WEIGHTS], *[out_d[k] for k in WEIGHTS],
            *[out_m[k] for k in WEIGHTS], *[out_v[k] for k in WEIGHTS])
```
